```python
import math
import jax, jax.numpy as jnp
from jax import lax
import numpy as np

D_MODEL = 1024
BATCH = 8
SEQ = 4096
DEPTH = 1

MEM_LEN = 256
EPS = 1e-6
CONV_WIDTH = CONV_HEADS = None
CONV_K = 3
CONV_GROUPS = 8
CONV_DIM = D_MODEL
GM_HEADS = 8
GM_HEAD_DIM = D_MODEL // GM_HEADS
GM_DIM = GM_HEADS * GM_HEAD_DIM
CHUNK = 128
MIX_DIM = CONV_DIM + GM_DIM
IN_DIM = 4 * CONV_DIM + 3 * GM_DIM
X_HEADS = 4
X_HEAD_DIM = D_MODEL // X_HEADS

kernel_name = "hybrid_shortconv_gmlp_memxattn_block"


def rms_norm(x, g):
    xf = x.astype(jnp.float32)
    y = xf * lax.rsqrt(jnp.mean(xf * xf, axis=-1, keepdims=True) + EPS)
    return (y * g.astype(jnp.float32)).astype(x.dtype)


def causal_depthwise_conv(h, w):
    c = h.shape[-1]
    return lax.conv_general_dilated(
        h, w[:, None, :].astype(h.dtype), window_strides=(1,),
        padding=[(CONV_K - 1, 0)], dimension_numbers=("NWC", "WIO", "NWC"),
        feature_group_count=c)


def chunked_spatial_gating(u, v, ln_g, ln_b, ws, bs):
    b, s, _ = v.shape
    n = s // CHUNK
    vh = v.reshape(b, n, CHUNK, GM_HEADS, GM_HEAD_DIM).astype(jnp.float32)
    mu = jnp.mean(vh, axis=-1, keepdims=True)
    var = jnp.mean(jnp.square(vh - mu), axis=-1, keepdims=True)
    vn = (vh - mu) * lax.rsqrt(var + EPS)
    vn = (vn * ln_g.reshape(GM_HEADS, GM_HEAD_DIM).astype(jnp.float32)
          + ln_b.reshape(GM_HEADS, GM_HEAD_DIM).astype(jnp.float32)).astype(v.dtype)
    mask = jnp.tril(jnp.ones((CHUNK, CHUNK), dtype=bool))
    w_c = jnp.where(mask[None], ws, jnp.zeros_like(ws))
    sp = jnp.einsum("hts,bnshc->bnthc", w_c, vn) + bs.T[:, :, None]
    return u * sp.reshape(b, s, GM_DIM)


def mixer_sublayer(h, w_in, conv_w, gm_ln_g, gm_ln_b, gm_ws, gm_bs, w_out):
    proj = h @ w_in
    gb, gc, xa, za, u, v, zb = jnp.split(
        proj, np.cumsum([CONV_DIM] * 4 + [GM_DIM] * 2).tolist(), axis=-1)
    a = gb * causal_depthwise_conv(gc * xa, conv_w)
    a = a * jax.nn.silu(za)
    bo = chunked_spatial_gating(jax.nn.gelu(u), jax.nn.gelu(v),
                                gm_ln_g, gm_ln_b, gm_ws, gm_bs)
    bo = bo * jax.nn.silu(zb)
    return jnp.concatenate([a, bo], axis=-1) @ w_out


def memory_cross_attention(h, m, w_q, w_kv, w_xo):
    b, s, _ = h.shape
    q = (h @ w_q).reshape(b, s, X_HEADS, X_HEAD_DIM)
    k, vv = jnp.split(m @ w_kv, 2, axis=-1)
    k = k.reshape(b, MEM_LEN, X_HEADS, X_HEAD_DIM)
    vv = vv.reshape(b, MEM_LEN, X_HEADS, X_HEAD_DIM)
    scores = jnp.einsum("bshd,bmhd->bhsm", q, k).astype(jnp.float32)
    p = jax.nn.softmax(scores * (1.0 / math.sqrt(X_HEAD_DIM)), axis=-1).astype(vv.dtype)
    o = jnp.einsum("bhsm,bmhd->bshd", p, vv).reshape(b, s, D_MODEL)
    return o @ w_xo


def _fwd_setup_inputs(seed: int = 0) -> dict:
    key = jax.random.key(seed)
    ks = jax.random.split(key, 20)
    f32 = jnp.float32
    L = DEPTH
    nrm = lambda k, shape, scale: jax.random.normal(k, shape, f32) * scale
    return {
        "x": nrm(ks[0], (BATCH, SEQ, D_MODEL), 1.0),
        "mem": nrm(ks[1], (BATCH, MEM_LEN, D_MODEL), 1.0),
        "norm_mix_g": 1.0 + nrm(ks[2], (L, D_MODEL), 0.02),
        "w_in": nrm(ks[3], (L, D_MODEL, IN_DIM), D_MODEL ** -0.5),
        "conv_w": nrm(ks[4], (L, CONV_K, CONV_DIM), CONV_K ** -0.5),
        "gm_ln_g": 1.0 + nrm(ks[5], (L, GM_DIM), 0.02),
        "gm_ln_b": nrm(ks[6], (L, GM_DIM), 0.02),
        "gm_ws": nrm(ks[7], (L, GM_HEADS, CHUNK, CHUNK), 0.5 * CHUNK ** -0.5),
        "gm_bs": 1.0 + nrm(ks[8], (L, GM_HEADS, CHUNK), 0.02),
        "w_out": nrm(ks[9], (L, MIX_DIM, D_MODEL), MIX_DIM ** -0.5),
        "norm_x_g": 1.0 + nrm(ks[10], (L, D_MODEL), 0.02),
        "norm_mem_g": 1.0 + nrm(ks[11], (L, D_MODEL), 0.02),
        "w_q": nrm(ks[12], (L, D_MODEL, D_MODEL), D_MODEL ** -0.5),
        "w_kv": nrm(ks[13], (L, D_MODEL, 2 * D_MODEL), D_MODEL ** -0.5),
        "w_xo": nrm(ks[14], (L, D_MODEL, D_MODEL), D_MODEL ** -0.5),
        "norm_final_g": 1.0 + nrm(ks[15], (D_MODEL,), 0.02),
    }


def _fwd_reference(x, mem, norm_mix_g, w_in, conv_w, gm_ln_g, gm_ln_b, gm_ws, gm_bs,
              w_out, norm_x_g, norm_mem_g, w_q, w_kv, w_xo, norm_final_g):
    for l in range(DEPTH):
        h = rms_norm(x, norm_mix_g[l])
        x = x + mixer_sublayer(h, w_in[l], conv_w[l], gm_ln_g[l], gm_ln_b[l],
                               gm_ws[l], gm_bs[l], w_out[l])
        h = rms_norm(x, norm_x_g[l])
        m = rms_norm(mem, norm_mem_g[l])
        x = x + memory_cross_attention(h, m, w_q[l], w_kv[l], w_xo[l])
    return rms_norm(x, norm_final_g)


import jax as _jax
import jax.numpy as _jnp

TWIN_FORMAT = 'train_step'
FWD_PARAMS = ['x', 'mem', 'norm_mix_g', 'w_in', 'conv_w', 'gm_ln_g', 'gm_ln_b', 'gm_ws', 'gm_bs', 'w_out', 'norm_x_g', 'norm_mem_g', 'w_q', 'w_kv', 'w_xo', 'norm_final_g']
TWIN_WEIGHTS = ['norm_mix_g', 'w_in', 'conv_w', 'gm_ln_g', 'gm_ln_b', 'gm_ws', 'gm_bs', 'w_out', 'norm_x_g', 'norm_mem_g', 'w_q', 'w_kv', 'w_xo', 'norm_final_g']
TWIN_DIFF_INPUT = 'x'
TWIN_INPUTS = ['x', 'mem', 'norm_mix_g', 'w_in', 'conv_w', 'gm_ln_g', 'gm_ln_b', 'gm_ws', 'gm_bs', 'w_out', 'norm_x_g', 'norm_mem_g', 'w_q', 'w_kv', 'w_xo', 'norm_final_g', 'loss_target', 'm_norm_mix_g', 'm_w_in', 'm_conv_w', 'm_gm_ln_g', 'm_gm_ln_b', 'm_gm_ws', 'm_gm_bs', 'm_w_out', 'm_norm_x_g', 'm_norm_mem_g', 'm_w_q', 'm_w_kv', 'm_w_xo', 'm_norm_final_g', 'v_norm_mix_g', 'v_w_in', 'v_conv_w', 'v_gm_ln_g', 'v_gm_ln_b', 'v_gm_ws', 'v_gm_bs', 'v_w_out', 'v_norm_x_g', 'v_norm_mem_g', 'v_w_q', 'v_w_kv', 'v_w_xo', 'v_norm_final_g']
TWIN_OUTPUTS = ['loss', 'grad_x', 'grad_norm_mix_g', 'grad_w_in', 'grad_conv_w', 'grad_gm_ln_g', 'grad_gm_ln_b', 'grad_gm_ws', 'grad_gm_bs', 'grad_w_out', 'grad_norm_x_g', 'grad_norm_mem_g', 'grad_w_q', 'grad_w_kv', 'grad_w_xo', 'grad_norm_final_g', 'delta_norm_mix_g', 'delta_w_in', 'delta_conv_w', 'delta_gm_ln_g', 'delta_gm_ln_b', 'delta_gm_ws', 'delta_gm_bs', 'delta_w_out', 'delta_norm_x_g', 'delta_norm_mem_g', 'delta_w_q', 'delta_w_kv', 'delta_w_xo', 'delta_norm_final_g', 'new_m_norm_mix_g', 'new_m_w_in', 'new_m_conv_w', 'new_m_gm_ln_g', 'new_m_gm_ln_b', 'new_m_gm_ws', 'new_m_gm_bs', 'new_m_w_out', 'new_m_norm_x_g', 'new_m_norm_mem_g', 'new_m_w_q', 'new_m_w_kv', 'new_m_w_xo', 'new_m_norm_final_g', 'new_v_norm_mix_g', 'new_v_w_in', 'new_v_conv_w', 'new_v_gm_ln_g', 'new_v_gm_ln_b', 'new_v_gm_ws', 'new_v_gm_bs', 'new_v_w_out', 'new_v_norm_x_g', 'new_v_norm_mem_g', 'new_v_w_q', 'new_v_w_kv', 'new_v_w_xo', 'new_v_norm_final_g']
TWIN_LEAF_KINDS = {'loss': 'loss', 'grad_x': 'grad_x', 'grad_norm_mix_g': 'grad_w', 'grad_w_in': 'grad_w', 'grad_conv_w': 'grad_w', 'grad_gm_ln_g': 'grad_w', 'grad_gm_ln_b': 'grad_w', 'grad_gm_ws': 'grad_w', 'grad_gm_bs': 'grad_w', 'grad_w_out': 'grad_w', 'grad_norm_x_g': 'grad_w', 'grad_norm_mem_g': 'grad_w', 'grad_w_q': 'grad_w', 'grad_w_kv': 'grad_w', 'grad_w_xo': 'grad_w', 'grad_norm_final_g': 'grad_w', 'delta_norm_mix_g': 'delta_w', 'delta_w_in': 'delta_w', 'delta_conv_w': 'delta_w', 'delta_gm_ln_g': 'delta_w', 'delta_gm_ln_b': 'delta_w', 'delta_gm_ws': 'delta_w', 'delta_gm_bs': 'delta_w', 'delta_w_out': 'delta_w', 'delta_norm_x_g': 'delta_w', 'delta_norm_mem_g': 'delta_w', 'delta_w_q': 'delta_w', 'delta_w_kv': 'delta_w', 'delta_w_xo': 'delta_w', 'delta_norm_final_g': 'delta_w', 'new_m_norm_mix_g': 'new_m', 'new_m_w_in': 'new_m', 'new_m_conv_w': 'new_m', 'new_m_gm_ln_g': 'new_m', 'new_m_gm_ln_b': 'new_m', 'new_m_gm_ws': 'new_m', 'new_m_gm_bs': 'new_m', 'new_m_w_out': 'new_m', 'new_m_norm_x_g': 'new_m', 'new_m_norm_mem_g': 'new_m', 'new_m_w_q': 'new_m', 'new_m_w_kv': 'new_m', 'new_m_w_xo': 'new_m', 'new_m_norm_final_g': 'new_m', 'new_v_norm_mix_g': 'new_v', 'new_v_w_in': 'new_v', 'new_v_conv_w': 'new_v', 'new_v_gm_ln_g': 'new_v', 'new_v_gm_ln_b': 'new_v', 'new_v_gm_ws': 'new_v', 'new_v_gm_bs': 'new_v', 'new_v_w_out': 'new_v', 'new_v_norm_x_g': 'new_v', 'new_v_norm_mem_g': 'new_v', 'new_v_w_q': 'new_v', 'new_v_w_kv': 'new_v', 'new_v_w_xo': 'new_v', 'new_v_norm_final_g': 'new_v'}


def _forward(args):
    return _fwd_reference(*[args[k] for k in FWD_PARAMS])


def _output_shape():
    def fwd():
        inp = _fwd_setup_inputs(0)
        return _fwd_reference(*[inp[k] for k in FWD_PARAMS])
    out = _jax.eval_shape(fwd)
    return out.shape, out.dtype

N_MICROBATCH = 1
ADAM_LR = 0.001
ADAM_B1 = 0.9
ADAM_B2 = 0.999
ADAM_EPS = 1e-08
ADAM_WD = 0.01
ADAM_STEP = 10
PER_EXAMPLE_BATCH_AXIS = {'x': 0, 'mem': 0, 'loss_target': 0}
SHARED_INPUTS = []
_WEIGHT_DTYPES = {'norm_mix_g': _jnp.float32, 'w_in': _jnp.float32, 'conv_w': _jnp.float32, 'gm_ln_g': _jnp.float32, 'gm_ln_b': _jnp.float32, 'gm_ws': _jnp.float32, 'gm_bs': _jnp.float32, 'w_out': _jnp.float32, 'norm_x_g': _jnp.float32, 'norm_mem_g': _jnp.float32, 'w_q': _jnp.float32, 'w_kv': _jnp.float32, 'w_xo': _jnp.float32, 'norm_final_g': _jnp.float32}
MOMENT_SCALE = {'norm_mix_g': 1.477042e-01, 'w_in': 5.714216e-02, 'conv_w': 6.553952e-02, 'gm_ln_g': 1.548761e-02, 'gm_ln_b': 1.642428e-02, 'gm_ws': 3.075285e-02, 'gm_bs': 4.269660e-02, 'w_out': 8.035368e-02, 'norm_x_g': 1.627111e-02, 'norm_mem_g': 2.519844e-02, 'w_q': 1.642874e-02, 'w_kv': 1.657381e-02, 'w_xo': 1.661707e-02, 'norm_final_g': 3.199440e+01}


def _to_microbatches(a, axis):
    t = _jnp.moveaxis(a, axis, 0)
    t = t.reshape((N_MICROBATCH, t.shape[0] // N_MICROBATCH) + t.shape[1:])
    return _jnp.moveaxis(t, 1, axis + 1)


def setup_inputs(seed: int = 0) -> dict:
    inp = _fwd_setup_inputs(seed)
    key = _jax.random.fold_in(_jax.random.key(seed), 7919)
    shape, _ = _output_shape()
    out = dict(inp)
    out["loss_target"] = _jax.random.normal(_jax.random.fold_in(key, 0), shape, _jnp.float32)
    for i, name in enumerate(TWIN_WEIGHTS):
        w = inp[name].astype(_jnp.float32)
        if MOMENT_SCALE is None:
            s = _jnp.sqrt(_jnp.mean(_jnp.square(w)) + 1e-30)
        else:
            s = MOMENT_SCALE[name]
        km, kv = _jax.random.split(_jax.random.fold_in(key, i + 1))
        out[name] = w
        out["m_" + name] = s * _jax.random.normal(km, w.shape, _jnp.float32)
        out["v_" + name] = (s * s) * _jax.random.uniform(kv, w.shape, _jnp.float32, 0.5, 1.5)
    if N_MICROBATCH > 1:
        for name, axis in PER_EXAMPLE_BATCH_AXIS.items():
            out[name] = _to_microbatches(out[name], axis)
    return {'x': out['x'], 'mem': out['mem'], 'norm_mix_g': out['norm_mix_g'], 'w_in': out['w_in'], 'conv_w': out['conv_w'], 'gm_ln_g': out['gm_ln_g'], 'gm_ln_b': out['gm_ln_b'], 'gm_ws': out['gm_ws'], 'gm_bs': out['gm_bs'], 'w_out': out['w_out'], 'norm_x_g': out['norm_x_g'], 'norm_mem_g': out['norm_mem_g'], 'w_q': out['w_q'], 'w_kv': out['w_kv'], 'w_xo': out['w_xo'], 'norm_final_g': out['norm_final_g'], 'loss_target': out['loss_target'], 'm_norm_mix_g': out['m_norm_mix_g'], 'm_w_in': out['m_w_in'], 'm_conv_w': out['m_conv_w'], 'm_gm_ln_g': out['m_gm_ln_g'], 'm_gm_ln_b': out['m_gm_ln_b'], 'm_gm_ws': out['m_gm_ws'], 'm_gm_bs': out['m_gm_bs'], 'm_w_out': out['m_w_out'], 'm_norm_x_g': out['m_norm_x_g'], 'm_norm_mem_g': out['m_norm_mem_g'], 'm_w_q': out['m_w_q'], 'm_w_kv': out['m_w_kv'], 'm_w_xo': out['m_w_xo'], 'm_norm_final_g': out['m_norm_final_g'], 'v_norm_mix_g': out['v_norm_mix_g'], 'v_w_in': out['v_w_in'], 'v_conv_w': out['v_conv_w'], 'v_gm_ln_g': out['v_gm_ln_g'], 'v_gm_ln_b': out['v_gm_ln_b'], 'v_gm_ws': out['v_gm_ws'], 'v_gm_bs': out['v_gm_bs'], 'v_w_out': out['v_w_out'], 'v_norm_x_g': out['v_norm_x_g'], 'v_norm_mem_g': out['v_norm_mem_g'], 'v_w_q': out['v_w_q'], 'v_w_kv': out['v_w_kv'], 'v_w_xo': out['v_w_xo'], 'v_norm_final_g': out['v_norm_final_g']}


def _loss(weights, diff, rest, loss_target):
    with _jax.named_scope("forward"):
        args = {**rest, TWIN_DIFF_INPUT: diff, **{k: w.astype(_WEIGHT_DTYPES[k]) for k, w in weights.items()}}
        y = _forward(args)
    with _jax.named_scope("loss_head"):
        err = _jnp.square(y.astype(_jnp.float32) - loss_target)
        return 0.5 * _jnp.sum(_jnp.mean(err, axis=-1)) if err.ndim else 0.5 * err


def _adamw(w, g, m, v):
    m = ADAM_B1 * m + (1.0 - ADAM_B1) * g
    v = ADAM_B2 * v + (1.0 - ADAM_B2) * _jnp.square(g)
    m_hat = m / (1.0 - ADAM_B1 ** ADAM_STEP)
    v_hat = v / (1.0 - ADAM_B2 ** ADAM_STEP)
    delta = -ADAM_LR * (m_hat / (_jnp.sqrt(v_hat) + ADAM_EPS) + ADAM_WD * w)
    return delta, m, v


def reference(x, mem, norm_mix_g, w_in, conv_w, gm_ln_g, gm_ln_b, gm_ws, gm_bs, w_out, norm_x_g, norm_mem_g, w_q, w_kv, w_xo, norm_final_g, loss_target, m_norm_mix_g, m_w_in, m_conv_w, m_gm_ln_g, m_gm_ln_b, m_gm_ws, m_gm_bs, m_w_out, m_norm_x_g, m_norm_mem_g, m_w_q, m_w_kv, m_w_xo, m_norm_final_g, v_norm_mix_g, v_w_in, v_conv_w, v_gm_ln_g, v_gm_ln_b, v_gm_ws, v_gm_bs, v_w_out, v_norm_x_g, v_norm_mem_g, v_w_q, v_w_kv, v_w_xo, v_norm_final_g):
    given = dict(x=x, mem=mem, norm_mix_g=norm_mix_g, w_in=w_in, conv_w=conv_w, gm_ln_g=gm_ln_g, gm_ln_b=gm_ln_b, gm_ws=gm_ws, gm_bs=gm_bs, w_out=w_out, norm_x_g=norm_x_g, norm_mem_g=norm_mem_g, w_q=w_q, w_kv=w_kv, w_xo=w_xo, norm_final_g=norm_final_g, loss_target=loss_target, m_norm_mix_g=m_norm_mix_g, m_w_in=m_w_in, m_conv_w=m_conv_w, m_gm_ln_g=m_gm_ln_g, m_gm_ln_b=m_gm_ln_b, m_gm_ws=m_gm_ws, m_gm_bs=m_gm_bs, m_w_out=m_w_out, m_norm_x_g=m_norm_x_g, m_norm_mem_g=m_norm_mem_g, m_w_q=m_w_q, m_w_kv=m_w_kv, m_w_xo=m_w_xo, m_norm_final_g=m_norm_final_g, v_norm_mix_g=v_norm_mix_g, v_w_in=v_w_in, v_conv_w=v_conv_w, v_gm_ln_g=v_gm_ln_g, v_gm_ln_b=v_gm_ln_b, v_gm_ws=v_gm_ws, v_gm_bs=v_gm_bs, v_w_out=v_w_out, v_norm_x_g=v_norm_x_g, v_norm_mem_g=v_norm_mem_g, v_w_q=v_w_q, v_w_kv=v_w_kv, v_w_xo=v_w_xo, v_norm_final_g=v_norm_final_g)
    weights = {n: given[n] for n in TWIN_WEIGHTS}
    shared = {n: given[n] for n in SHARED_INPUTS}
    per_example = {n: given[n] for n in ['x', 'mem']}
    grad_fn = _jax.value_and_grad(_loss, argnums=(0, 1))

    def one_microbatch(ex, loss_target):
        ex = dict(ex)
        diff = ex.pop(TWIN_DIFF_INPUT)
        return grad_fn(weights, diff, {**shared, **ex}, loss_target)

    if N_MICROBATCH == 1:
        loss, (grad_w, grad_x) = one_microbatch(per_example, given["loss_target"])
    else:
        def body(carry, xs):
            loss_sum, grad_sum = carry
            l_k, (gw_k, gx_k) = one_microbatch(xs[0], xs[1])
            with _jax.named_scope("update"):
                return (loss_sum + l_k, _jax.tree.map(_jnp.add, grad_sum, gw_k)), gx_k

        init = (_jnp.zeros((), _jnp.float32), _jax.tree.map(_jnp.zeros_like, weights))
        (loss, grad_w), grad_x = _jax.lax.scan(body, init, (per_example, given["loss_target"]))
    with _jax.named_scope("update"):
        delta_w, new_m, new_v = {}, {}, {}
        for n in TWIN_WEIGHTS:
            delta_w[n], new_m[n], new_v[n] = _adamw(weights[n], grad_w[n], given["m_" + n], given["v_" + n])
    return (loss, grad_x, *[grad_w[n] for n in TWIN_WEIGHTS], *[delta_w[n] for n in TWIN_WEIGHTS],
            *[new_m[n] for n in TWIN_WEIGHTS], *[new_v[n] for n in TWIN_WEIGHTS])
```

```python
import math

import jax
import jax.numpy as jnp
from jax import lax
from jax.experimental import pallas as pl
from jax.experimental.pallas import tpu as pltpu

F32 = jnp.float32
BF16 = jnp.bfloat16
MESH = pl.DeviceIdType.MESH

D_MODEL = 1024
N_DEV = 8
N_CHIP = 4
CHUNK = 128
GROUPS = 7
IN_DIM = GROUPS * D_MODEL
MIX_DIM = 2 * D_MODEL
X_HEADS = 4
X_HEAD_DIM = D_MODEL // X_HEADS
MEM_LEN = 256
EPS = 1e-6
GELU_K0 = math.sqrt(2.0 / math.pi)
GELU_K1 = 0.044715

ADAM_LR = 0.001
ADAM_B1 = 0.9
ADAM_B2 = 0.999
ADAM_EPS = 1e-08
ADAM_WD = 0.01
ADAM_STEP = 10

TM_INPROJ = 512
TM_ATTN = 256
TM_DH1 = 512
TK_DW = 1024
TR_ELEMWISE = 256


def _dot(a, b):
    return lax.dot_general(a, b, (((1,), (0,)), ((), ())), preferred_element_type=F32)


def _dot_nt(a, b):
    return lax.dot_general(a, b, (((1,), (1,)), ((), ())), preferred_element_type=F32)


def _rows8(v):
    return jnp.sum(v.reshape(v.shape[0] // 8, 8, v.shape[1]), axis=0)


def _gelu(x):
    x2 = x * x
    t = jnp.tanh(GELU_K0 * (x + GELU_K1 * (x * x2)))
    g = 0.5 * x * (1.0 + t)
    dg = 0.5 * (1.0 + t) + 0.5 * x * (1.0 - t * t) * (GELU_K0 * (1.0 + 3.0 * GELU_K1 * x2))
    return g, dg


def _silu(z):
    s = jax.nn.sigmoid(z)
    return z * s, s * (1.0 + z * (1.0 - s))


def _rms(v):
    r = lax.rsqrt(jnp.mean(v * v, axis=-1, keepdims=True) + EPS)
    return v * r, r


def _rms_bwd(dy_g, vh, r):
    return r * (dy_g - vh * jnp.mean(dy_g * vh, axis=-1, keepdims=True))


def _place():
    return lax.axis_index("x"), lax.axis_index("y"), lax.axis_index("c")


def _other_chips(x, y):
    return [(1 - x, y), (x, 1 - y), (1 - x, 1 - y)]


def _pow2_divisor(n):
    return n & (-n)


def _shard_view(ref, axis, size, d):
    start = d * size
    align = _pow2_divisor(size)
    if align > 1:
        start = pl.multiple_of(start, align)
    idx = [slice(None)] * len(ref.shape)
    idx[axis] = pl.ds(start, size)
    return ref.at[tuple(idx)]


def _all_gather(name, shards, specs):
    n = len(shards)
    n_copy = 7

    def body(*refs):
        ins, outs = refs[:n], refs[n : 2 * n]
        send_sems, recv_sems, local_sems = refs[2 * n :]
        x, y, c = _place()
        me, sibling = (x, y, c), (x, y, 1 - c)
        chips = _other_chips(x, y)

        def block(a, p):
            _, axis, size = specs[a]
            return _shard_view(outs[a], axis, size, 4 * p[0] + 2 * p[1] + p[2])

        def copy(a, k, p, to, src=None):
            return pltpu.make_async_remote_copy(
                src_ref=block(a, p) if src is None else src,
                dst_ref=block(a, p),
                send_sem=send_sems.at[a * n_copy + k],
                recv_sem=recv_sems.at[a * n_copy + k],
                device_id=to,
                device_id_type=MESH,
            )

        mine = [pltpu.make_async_copy(ins[a], block(a, me), local_sems.at[a]) for a in range(n)]
        for cp in mine:
            cp.start()
        first = []
        for a in range(n):
            first.append(copy(a, 0, me, sibling, src=ins[a]))
            first += [copy(a, 1 + j, me, (*chip, c), src=ins[a]) for j, chip in enumerate(chips)]
        for cp in first:
            cp.start()
        passed = []
        for j, chip in enumerate(chips):
            for a in range(n):
                copy(a, 1 + j, (*chip, c), me).wait_recv()
                fwd = copy(a, 4 + j, (*chip, c), sibling)
                fwd.start()
                passed.append(fwd)
        for a in range(n):
            copy(a, 0, sibling, me).wait_recv()
            for j, chip in enumerate(chips):
                copy(a, 4 + j, (*chip, 1 - c), me).wait_recv()
        for cp in first + passed:
            cp.wait_send()
        for cp in mine:
            cp.wait()

    any_spec = pl.BlockSpec(memory_space=pl.ANY)
    return pl.pallas_call(
        body,
        name=name,
        out_shape=[jax.ShapeDtypeStruct(full, s.dtype) for s, (full, _, _) in zip(shards, specs)],
        in_specs=[any_spec] * n,
        out_specs=[any_spec] * n,
        scratch_shapes=[
            pltpu.SemaphoreType.DMA((n * n_copy,)),
            pltpu.SemaphoreType.DMA((n * n_copy,)),
            pltpu.SemaphoreType.DMA((n,)),
        ],
    )(*shards)


def _sibling_exchange(grads, specs):
    n = len(grads)

    def shard_shape(a):
        axis, size = specs[a]
        shp = list(grads[a].shape)
        shp[axis] = size
        return tuple(shp)

    def body(*refs):
        ins, outs = refs[:n], refs[n : 2 * n]
        send_sems, recv_sems = refs[2 * n :]
        x, y, c = _place()
        copies = []
        for a in range(n):
            axis, size = specs[a]
            for q in range(N_CHIP):
                copies.append(
                    pltpu.make_async_remote_copy(
                        src_ref=_shard_view(ins[a], axis, size, 2 * q + (1 - c)),
                        dst_ref=outs[a].at[q],
                        send_sem=send_sems.at[a * N_CHIP + q],
                        recv_sem=recv_sems.at[a * N_CHIP + q],
                        device_id=(x, y, 1 - c),
                        device_id_type=MESH,
                    )
                )
        for cp in copies:
            cp.start()
        for cp in copies:
            cp.wait()

    any_spec = pl.BlockSpec(memory_space=pl.ANY)
    return pl.pallas_call(
        body,
        name="grad_sibling_exchange",
        out_shape=[jax.ShapeDtypeStruct((N_CHIP, *shard_shape(a)), grads[a].dtype) for a in range(n)],
        in_specs=[any_spec] * n,
        out_specs=[any_spec] * n,
        scratch_shapes=[pltpu.SemaphoreType.DMA((n * N_CHIP,)), pltpu.SemaphoreType.DMA((n * N_CHIP,))],
    )(*grads)


def _chip_exchange(parts):
    n = len(parts)

    def body(*refs):
        ins, outs = refs[:n], refs[n : 2 * n]
        send_sems, recv_sems, local_sems = refs[2 * n :]
        x, y, c = _place()
        my_chip = 2 * x + y
        chips = _other_chips(x, y)
        mine = [pltpu.make_async_copy(ins[a].at[my_chip], outs[a].at[my_chip], local_sems.at[a]) for a in range(n)]
        for cp in mine:
            cp.start()
        sends, recvs = [], []
        for a in range(n):
            for j, chip in enumerate(chips):
                their_chip = 2 * chip[0] + chip[1]
                sems = dict(send_sem=send_sems.at[a * 3 + j], recv_sem=recv_sems.at[a * 3 + j],
                            device_id=(*chip, c), device_id_type=MESH)
                sends.append(pltpu.make_async_remote_copy(src_ref=ins[a].at[their_chip], dst_ref=outs[a].at[my_chip], **sems))
                recvs.append(pltpu.make_async_remote_copy(src_ref=ins[a].at[their_chip], dst_ref=outs[a].at[their_chip], **sems))
        for cp in sends:
            cp.start()
        for cp in recvs:
            cp.wait_recv()
        for cp in sends:
            cp.wait_send()
        for cp in mine:
            cp.wait()

    any_spec = pl.BlockSpec(memory_space=pl.ANY)
    return pl.pallas_call(
        body,
        name="grad_chip_exchange",
        out_shape=[jax.ShapeDtypeStruct(p.shape, p.dtype) for p in parts],
        in_specs=[any_spec] * n,
        out_specs=[any_spec] * n,
        scratch_shapes=[
            pltpu.SemaphoreType.DMA((n * 3,)),
            pltpu.SemaphoreType.DMA((n * 3,)),
            pltpu.SemaphoreType.DMA((n,)),
        ],
    )(*parts)


def _inproj(x, g, w_in):
    s = x.shape[0]
    tm = min(TM_INPROJ, s)

    def body(x_ref, g_ref, w_ref, proj_ref, h1_ref, h1t_ref):
        @pl.when(pl.program_id(1) == 0)
        def _():
            xh, _ = _rms(x_ref[...])
            h = xh * g_ref[...]
            h1_ref[...] = h.astype(BF16)
            h1t_ref[...] = h.T.astype(BF16)

        proj_ref[...] = _dot(h1_ref[...], w_ref[...]).astype(BF16)

    return pl.pallas_call(
        body,
        name="inproj",
        grid=(s // tm, GROUPS),
        in_specs=[
            pl.BlockSpec((tm, D_MODEL), lambda i, j: (i, 0)),
            pl.BlockSpec((1, D_MODEL), lambda i, j: (0, 0)),
            pl.BlockSpec((D_MODEL, D_MODEL), lambda i, j: (0, j)),
        ],
        out_specs=[
            pl.BlockSpec((tm, D_MODEL), lambda i, j: (i, j)),
            pl.BlockSpec((tm, D_MODEL), lambda i, j: (i, 0)),
            pl.BlockSpec((D_MODEL, tm), lambda i, j: (0, i)),
        ],
        out_shape=[
            jax.ShapeDtypeStruct((s, IN_DIM), BF16),
            jax.ShapeDtypeStruct((s, D_MODEL), BF16),
            jax.ShapeDtypeStruct((D_MODEL, s), BF16),
        ],
        compiler_params=pltpu.CompilerParams(dimension_semantics=("arbitrary", "arbitrary")),
    )(x, g, w_in)


def _causal_mask():
    row = lax.broadcasted_iota(jnp.int32, (CHUNK, CHUNK), 0)
    col = lax.broadcasted_iota(jnp.int32, (CHUNK, CHUNK), 1)
    return row, row >= col


def _mixer_fwd(proj, conv_w, ln_g, ln_b, ws, bs_b):
    s = proj.shape[0]
    nt = s // CHUNK

    def body(proj_ref, cw_ref, lng_ref, lnb_ref, ws_ref, bsb_ref, cat_ref, pcar_ref):
        @pl.when(pl.program_id(0) == 0)
        def _():
            pcar_ref[...] = jnp.zeros_like(pcar_ref)

        row, tril = _causal_mask()
        for j in range(8):
            cs = slice(CHUNK * j, CHUNK * (j + 1))

            def grp(k):
                return proj_ref[:, k * D_MODEL + CHUNK * j : k * D_MODEL + CHUNK * (j + 1)].astype(F32)

            w = cw_ref[:, cs]
            p = grp(1) * grp(2)
            pc = pcar_ref[:, cs]
            p1 = jnp.where(row == 0, pc[7:8], pltpu.roll(p, 1, 0))
            p2 = jnp.where(row == 0, pc[6:7], jnp.where(row == 1, pc[7:8], pltpu.roll(p, 2, 0)))
            pcar_ref[:, cs] = p[CHUNK - 8 :]
            cv = w[0:1] * p2 + w[1:2] * p1 + w[2:3] * p
            sa, _ = _silu(grp(3))
            cat_ref[:, cs] = ((grp(0) * cv) * sa).astype(BF16)
            gu, _ = _gelu(grp(4))
            gv, _ = _gelu(grp(5))
            dv = gv - jnp.mean(gv, axis=-1, keepdims=True)
            vn = dv * lax.rsqrt(jnp.mean(dv * dv, axis=-1, keepdims=True) + EPS)
            vn = vn * lng_ref[:, cs] + lnb_ref[:, cs]
            wc = jnp.where(tril, ws_ref[j], 0.0).astype(BF16)
            sp = _dot(wc, vn.astype(BF16)) + bsb_ref[j]
            sb, _ = _silu(grp(6))
            cat_ref[:, D_MODEL + CHUNK * j : D_MODEL + CHUNK * (j + 1)] = ((gu * sp) * sb).astype(BF16)

    whole = lambda shape: pl.BlockSpec(shape, lambda i: (0,) * len(shape))
    return pl.pallas_call(
        body,
        name="mixer_fwd",
        grid=(nt,),
        in_specs=[
            pl.BlockSpec((CHUNK, IN_DIM), lambda i: (i, 0)),
            whole((8, D_MODEL)),
            whole((1, D_MODEL)),
            whole((1, D_MODEL)),
            whole((8, CHUNK, CHUNK)),
            whole((8, CHUNK, CHUNK)),
        ],
        out_specs=pl.BlockSpec((CHUNK, MIX_DIM), lambda i: (i, 0)),
        out_shape=jax.ShapeDtypeStruct((s, MIX_DIM), BF16),
        scratch_shapes=[pltpu.VMEM((8, D_MODEL), F32)],
        compiler_params=pltpu.CompilerParams(dimension_semantics=("arbitrary",)),
    )(proj, conv_w, ln_g, ln_b, ws, bs_b)


def _mixer_bwd(proj, dcat, conv_w, ln_g, ln_b, ws, bs_b):
    s = proj.shape[0]
    nt = s // CHUNK

    def body(proj_ref, halo_ref, dcat_ref, cw_ref, lng_ref, lnb_ref, ws_ref, bsb_ref,
             dproj_ref, dcw_ref, dlng_ref, dlnb_ref, dws_ref, dbs_ref,
             car_ref, acc_cw, acc_lng, acc_lnb, acc_bs):
        i = pl.program_id(0)
        tile = nt - 1 - i

        @pl.when(i == 0)
        def _():
            car_ref[...] = jnp.zeros_like(car_ref)
            acc_cw[...] = jnp.zeros_like(acc_cw)
            acc_lng[...] = jnp.zeros_like(acc_lng)
            acc_lnb[...] = jnp.zeros_like(acc_lnb)
            acc_bs[...] = jnp.zeros_like(acc_bs)
            dws_ref[...] = jnp.zeros_like(dws_ref)

        row, tril = _causal_mask()
        has_prev = jnp.where(tile > 0, 1.0, 0.0).astype(F32)
        for j in range(8):
            cs = slice(CHUNK * j, CHUNK * (j + 1))

            def col(k):
                return slice(k * D_MODEL + CHUNK * j, k * D_MODEL + CHUNK * (j + 1))

            def grp(k):
                return proj_ref[:, col(k)].astype(F32)

            w = cw_ref[:, cs]
            gb, gc, xa = grp(0), grp(1), grp(2)
            p = gc * xa
            pprev = halo_ref[:, col(1)].astype(F32) * halo_ref[:, col(2)].astype(F32) * has_prev
            p1 = jnp.where(row == 0, pprev[15:16], pltpu.roll(p, 1, 0))
            p2 = jnp.where(row == 0, pprev[14:15], jnp.where(row == 1, pprev[15:16], pltpu.roll(p, 2, 0)))
            cv = w[0:1] * p2 + w[1:2] * p1 + w[2:3] * p
            za = grp(3)
            sa, dsa = _silu(za)
            da = dcat_ref[:, cs].astype(F32)
            dproj_ref[:, col(0)] = (da * cv * sa).astype(BF16)
            dproj_ref[:, col(3)] = (da * gb * cv * dsa).astype(BF16)
            dcv = da * gb * sa
            nxt = car_ref[:, cs]
            d1 = jnp.where(row == CHUNK - 1, nxt[0:1], pltpu.roll(dcv, CHUNK - 1, 0))
            d2 = jnp.where(row == CHUNK - 2, nxt[0:1], jnp.where(row == CHUNK - 1, nxt[1:2], pltpu.roll(dcv, CHUNK - 2, 0)))
            car_ref[:, cs] = dcv[0:8]
            dp = w[2:3] * dcv + w[1:2] * d1 + w[0:1] * d2
            dproj_ref[:, col(1)] = (dp * xa).astype(BF16)
            dproj_ref[:, col(2)] = (dp * gc).astype(BF16)
            acc_cw[0, :, cs] += _rows8(dcv * p2)
            acc_cw[1, :, cs] += _rows8(dcv * p1)
            acc_cw[2, :, cs] += _rows8(dcv * p)
            gu, dgu = _gelu(grp(4))
            gv, dgv = _gelu(grp(5))
            dv = gv - jnp.mean(gv, axis=-1, keepdims=True)
            rstd = lax.rsqrt(jnp.mean(dv * dv, axis=-1, keepdims=True) + EPS)
            vnh = dv * rstd
            lng = lng_ref[:, cs]
            vnb = (vnh * lng + lnb_ref[:, cs]).astype(BF16)
            wcf = jnp.where(tril, ws_ref[j], 0.0)
            sp = _dot(wcf.astype(BF16), vnb) + bsb_ref[j]
            zb = grp(6)
            sb, dsb = _silu(zb)
            db = dcat_ref[:, D_MODEL + CHUNK * j : D_MODEL + CHUNK * (j + 1)].astype(F32)
            dproj_ref[:, col(4)] = (db * sp * sb * dgu).astype(BF16)
            dproj_ref[:, col(6)] = (db * gu * sp * dsb).astype(BF16)
            dsp = db * gu * sb
            acc_bs[j] += dsp
            dspb = dsp.astype(BF16)
            dws_ref[j] += _dot_nt(dspb, vnb)
            dvn = _dot(wcf.T.astype(BF16), dspb)
            acc_lnb[:, cs] += _rows8(dvn)
            acc_lng[:, cs] += _rows8(dvn * vnh)
            dvh = dvn * lng
            dgvv = rstd * (dvh - jnp.mean(dvh, axis=-1, keepdims=True) - vnh * jnp.mean(dvh * vnh, axis=-1, keepdims=True))
            dproj_ref[:, col(5)] = (dgvv * dgv).astype(BF16)

        @pl.when(i == nt - 1)
        def _():
            dlng_ref[...] = jnp.sum(acc_lng[...], axis=0, keepdims=True)
            dlnb_ref[...] = jnp.sum(acc_lnb[...], axis=0, keepdims=True)
            dcw_ref[...] = jnp.zeros_like(dcw_ref)
            for k in range(3):
                dcw_ref[k : k + 1, :] = jnp.sum(acc_cw[k], axis=0, keepdims=True)
            for j in range(8):
                dws_ref[j] = jnp.where(tril, dws_ref[j], 0.0)
                dbs_ref[:, CHUNK * j : CHUNK * (j + 1)] = jnp.sum(acc_bs[j].T, axis=0, keepdims=True)

    whole = lambda shape: pl.BlockSpec(shape, lambda i: (0,) * len(shape))
    halo_rows = 16
    per = CHUNK // halo_rows
    return pl.pallas_call(
        body,
        name="mixer_bwd",
        grid=(nt,),
        in_specs=[
            pl.BlockSpec((CHUNK, IN_DIM), lambda i: (nt - 1 - i, 0)),
            pl.BlockSpec((halo_rows, IN_DIM), lambda i: (jnp.maximum((nt - 1 - i) * per - 1, 0), 0)),
            pl.BlockSpec((CHUNK, MIX_DIM), lambda i: (nt - 1 - i, 0)),
            whole((8, D_MODEL)),
            whole((1, D_MODEL)),
            whole((1, D_MODEL)),
            whole((8, CHUNK, CHUNK)),
            whole((8, CHUNK, CHUNK)),
        ],
        out_specs=[
            pl.BlockSpec((CHUNK, IN_DIM), lambda i: (nt - 1 - i, 0)),
            whole((8, D_MODEL)),
            whole((1, D_MODEL)),
            whole((1, D_MODEL)),
            whole((8, CHUNK, CHUNK)),
            whole((1, D_MODEL)),
        ],
        out_shape=[
            jax.ShapeDtypeStruct((s, IN_DIM), BF16),
            jax.ShapeDtypeStruct((8, D_MODEL), F32),
            jax.ShapeDtypeStruct((1, D_MODEL), F32),
            jax.ShapeDtypeStruct((1, D_MODEL), F32),
            jax.ShapeDtypeStruct((8, CHUNK, CHUNK), F32),
            jax.ShapeDtypeStruct((1, D_MODEL), F32),
        ],
        scratch_shapes=[
            pltpu.VMEM((8, D_MODEL), F32),
            pltpu.VMEM((3, 8, D_MODEL), F32),
            pltpu.VMEM((8, D_MODEL), F32),
            pltpu.VMEM((8, D_MODEL), F32),
            pltpu.VMEM((8, CHUNK, CHUNK), F32),
        ],
        compiler_params=pltpu.CompilerParams(dimension_semantics=("arbitrary",)),
    )(proj, proj, dcat, conv_w, ln_g, ln_b, ws, bs_b)


def _kv_fwd(mem, g_mem, w_kv):
    def body(mem_ref, g_ref, w_ref, k_ref, vv_ref, mt_ref):
        mh, _ = _rms(mem_ref[...])
        m = mh * g_ref[...]
        kv = _dot(m.astype(BF16), w_ref[...])
        k_ref[...] = kv[:, :D_MODEL].astype(BF16)
        vv_ref[...] = kv[:, D_MODEL:].astype(BF16)
        mt_ref[...] = m.T.astype(BF16)

    return pl.pallas_call(
        body,
        name="kv_fwd",
        out_shape=[
            jax.ShapeDtypeStruct((MEM_LEN, D_MODEL), BF16),
            jax.ShapeDtypeStruct((MEM_LEN, D_MODEL), BF16),
            jax.ShapeDtypeStruct((D_MODEL, MEM_LEN), BF16),
        ],
    )(mem, g_mem, w_kv)


def _kv_bwd(mem, w_kv, mt, dk, dvv):
    def body(mem_ref, w_ref, mt_ref, dk_ref, dvv_ref, dw_ref, dg_ref):
        dkv = jnp.concatenate([dk_ref[...], dvv_ref[...]], axis=1).astype(BF16)
        dw_ref[...] = _dot(mt_ref[...], dkv)
        dm = _dot_nt(dkv, w_ref[...])
        mh, _ = _rms(mem_ref[...])
        dg_ref[...] = jnp.sum(dm * mh, axis=0, keepdims=True)

    return pl.pallas_call(
        body,
        name="kv_bwd",
        out_shape=[
            jax.ShapeDtypeStruct((D_MODEL, 2 * D_MODEL), F32),
            jax.ShapeDtypeStruct((1, D_MODEL), F32),
        ],
    )(mem, w_kv, mt, dk, dvv)


def _attn(x, cat, target, w_out, w_q, w_xo, k, vv, g_x, g_f):
    s = x.shape[0]
    tm = min(TM_ATTN, s)
    nt = s // tm
    scale = 1.0 / math.sqrt(X_HEAD_DIM)

    def body(x_ref, cat_ref, t_ref, wout_ref, wq_ref, wxo_ref, k_ref, vv_ref, gx_ref, gf_ref,
             dx1_ref, dcat_ref, catt_ref, h2t_ref, ot_ref, dx2b_ref, dqb_ref, dx1b_ref,
             dk_ref, dvv_ref, dgf_ref, dgx_ref, loss_ref,
             p_scr, q_scr, o_scr, dq_scr, acc_gf, acc_gx, acc_loss):
        i = pl.program_id(0)

        @pl.when(i == 0)
        def _():
            dk_ref[...] = jnp.zeros_like(dk_ref)
            dvv_ref[...] = jnp.zeros_like(dvv_ref)
            acc_gf[...] = jnp.zeros_like(acc_gf)
            acc_gx[...] = jnp.zeros_like(acc_gx)
            acc_loss[...] = jnp.zeros_like(acc_loss)

        cat = cat_ref[...]
        catt_ref[...] = cat.astype(F32).T.astype(BF16)
        x1 = x_ref[...] + _dot(cat, wout_ref[...])
        x1h, r2 = _rms(x1)
        h2 = x1h * gx_ref[...]
        h2b = h2.astype(BF16)
        h2t_ref[...] = h2.T.astype(BF16)
        q_scr[...] = _dot(h2b, wq_ref[...]).astype(BF16)
        for h in range(X_HEADS):
            hs = slice(X_HEAD_DIM * h, X_HEAD_DIM * (h + 1))
            sc = _dot_nt(q_scr[:, hs], k_ref[:, hs]) * scale
            e = jnp.exp(sc - jnp.max(sc, axis=-1, keepdims=True))
            p = e / jnp.sum(e, axis=-1, keepdims=True)
            p_scr[:, hs] = p
            o_scr[:, hs] = _dot(p.astype(BF16), vv_ref[:, hs])
        o = o_scr[...]
        ot_ref[...] = o.T.astype(BF16)
        x2 = x1 + _dot(o.astype(BF16), wxo_ref[...])
        x2h, r3 = _rms(x2)
        gf = gf_ref[...]
        err = x2h * gf - t_ref[...]
        acc_loss[...] += _rows8(err * err)
        dy = err * (1.0 / D_MODEL)
        acc_gf[...] += _rows8(dy * x2h)
        dx2 = _rms_bwd(dy * gf, x2h, r3)
        dx2b = dx2.astype(BF16)
        dx2b_ref[...] = dx2b
        dob = _dot_nt(dx2b, wxo_ref[...]).astype(BF16)
        for h in range(X_HEADS):
            hs = slice(X_HEAD_DIM * h, X_HEAD_DIM * (h + 1))
            p = p_scr[:, hs]
            dp = _dot_nt(dob[:, hs], vv_ref[:, hs])
            ds = p * (dp - jnp.sum(dp * p, axis=-1, keepdims=True)) * scale
            dsb = ds.astype(BF16)
            dq_scr[:, hs] = _dot(dsb, k_ref[:, hs]).astype(BF16)
            dvv_ref[:, hs] += _dot(p.T.astype(BF16), dob[:, hs])
            dk_ref[:, hs] += _dot(ds.T.astype(BF16), q_scr[:, hs])
        dqb = dq_scr[...]
        dqb_ref[...] = dqb
        dh2 = _dot_nt(dqb, wq_ref[...])
        acc_gx[...] += _rows8(dh2 * x1h)
        dx1 = dx2 + _rms_bwd(dh2 * gx_ref[...], x1h, r2)
        dx1_ref[...] = dx1
        dx1b = dx1.astype(BF16)
        dx1b_ref[...] = dx1b
        dcat_ref[...] = _dot_nt(dx1b, wout_ref[...]).astype(BF16)

        @pl.when(i == nt - 1)
        def _():
            dgf_ref[...] = jnp.sum(acc_gf[...], axis=0, keepdims=True)
            dgx_ref[...] = jnp.sum(acc_gx[...], axis=0, keepdims=True)
            loss_ref[...] = jnp.sum(acc_loss[...], axis=0, keepdims=True)

    whole = lambda shape: pl.BlockSpec(shape, lambda i: (0,) * len(shape))
    rows = lambda width: pl.BlockSpec((tm, width), lambda i: (i, 0))
    cols = lambda height: pl.BlockSpec((height, tm), lambda i: (0, i))
    vec = jax.ShapeDtypeStruct((1, D_MODEL), F32)
    return pl.pallas_call(
        body,
        name="attn",
        grid=(nt,),
        in_specs=[
            rows(D_MODEL), rows(MIX_DIM), rows(D_MODEL),
            whole((MIX_DIM, D_MODEL)), whole((D_MODEL, D_MODEL)), whole((D_MODEL, D_MODEL)),
            whole((MEM_LEN, D_MODEL)), whole((MEM_LEN, D_MODEL)),
            whole((1, D_MODEL)), whole((1, D_MODEL)),
        ],
        out_specs=[
            rows(D_MODEL), rows(MIX_DIM), cols(MIX_DIM), cols(D_MODEL), cols(D_MODEL),
            rows(D_MODEL), rows(D_MODEL), rows(D_MODEL),
            whole((MEM_LEN, D_MODEL)), whole((MEM_LEN, D_MODEL)),
            whole((1, D_MODEL)), whole((1, D_MODEL)), whole((1, D_MODEL)),
        ],
        out_shape=[
            jax.ShapeDtypeStruct((s, D_MODEL), F32),
            jax.ShapeDtypeStruct((s, MIX_DIM), BF16),
            jax.ShapeDtypeStruct((MIX_DIM, s), BF16),
            jax.ShapeDtypeStruct((D_MODEL, s), BF16),
            jax.ShapeDtypeStruct((D_MODEL, s), BF16),
            jax.ShapeDtypeStruct((s, D_MODEL), BF16),
            jax.ShapeDtypeStruct((s, D_MODEL), BF16),
            jax.ShapeDtypeStruct((s, D_MODEL), BF16),
            jax.ShapeDtypeStruct((MEM_LEN, D_MODEL), F32),
            jax.ShapeDtypeStruct((MEM_LEN, D_MODEL), F32),
            vec, vec, vec,
        ],
        scratch_shapes=[
            pltpu.VMEM((tm, D_MODEL), F32),
            pltpu.VMEM((tm, D_MODEL), BF16),
            pltpu.VMEM((tm, D_MODEL), F32),
            pltpu.VMEM((tm, D_MODEL), BF16),
            pltpu.VMEM((8, D_MODEL), F32),
            pltpu.VMEM((8, D_MODEL), F32),
            pltpu.VMEM((8, D_MODEL), F32),
        ],
        compiler_params=pltpu.CompilerParams(dimension_semantics=("arbitrary",)),
    )(x, cat, target, w_out, w_q, w_xo, k, vv, g_x, g_f)


def _matmul_f32(name, a, b):
    m, kk = a.shape
    n = b.shape[1]
    tm, tn, tk = min(m, D_MODEL), min(n, D_MODEL), min(kk, TK_DW)
    nk = kk // tk

    def body(a_ref, b_ref, o_ref):
        kid = pl.program_id(2)
        prod = _dot(a_ref[...], b_ref[...])

        @pl.when(kid == 0)
        def _():
            o_ref[...] = prod

        @pl.when(kid > 0)
        def _():
            o_ref[...] += prod

    return pl.pallas_call(
        body,
        name=name,
        grid=(m // tm, n // tn, nk),
        in_specs=[pl.BlockSpec((tm, tk), lambda i, j, k: (i, k)), pl.BlockSpec((tk, tn), lambda i, j, k: (k, j))],
        out_specs=pl.BlockSpec((tm, tn), lambda i, j, k: (i, j)),
        out_shape=jax.ShapeDtypeStruct((m, n), F32),
        compiler_params=pltpu.CompilerParams(dimension_semantics=("arbitrary", "arbitrary", "arbitrary")),
    )(a, b)


def _dh1(dproj, w_in, x, dx1, g):
    s = x.shape[0]
    tm = min(TM_DH1, s)
    nt = s // tm

    def body(dp_ref, w_ref, x_ref, dx1_ref, g_ref, gx_ref, dg_ref, acc_ref, acc_g):
        i, j = pl.program_id(0), pl.program_id(1)
        prod = _dot_nt(dp_ref[...], w_ref[...])

        @pl.when(j == 0)
        def _():
            acc_ref[...] = prod

        @pl.when(j > 0)
        def _():
            acc_ref[...] += prod

        @pl.when((i == 0) & (j == 0))
        def _():
            acc_g[...] = jnp.zeros_like(acc_g)

        @pl.when(j == GROUPS - 1)
        def _():
            dh1 = acc_ref[...]
            xh, r = _rms(x_ref[...])
            acc_g[...] += _rows8(dh1 * xh)
            gx_ref[...] = dx1_ref[...] + _rms_bwd(dh1 * g_ref[...], xh, r)

        @pl.when((i == nt - 1) & (j == GROUPS - 1))
        def _():
            dg_ref[...] = jnp.sum(acc_g[...], axis=0, keepdims=True)

    return pl.pallas_call(
        body,
        name="dh1",
        grid=(nt, GROUPS),
        in_specs=[
            pl.BlockSpec((tm, D_MODEL), lambda i, j: (i, j)),
            pl.BlockSpec((D_MODEL, D_MODEL), lambda i, j: (0, j)),
            pl.BlockSpec((tm, D_MODEL), lambda i, j: (i, 0)),
            pl.BlockSpec((tm, D_MODEL), lambda i, j: (i, 0)),
            pl.BlockSpec((1, D_MODEL), lambda i, j: (0, 0)),
        ],
        out_specs=[
            pl.BlockSpec((tm, D_MODEL), lambda i, j: (i, 0)),
            pl.BlockSpec((1, D_MODEL), lambda i, j: (0, 0)),
        ],
        out_shape=[jax.ShapeDtypeStruct((s, D_MODEL), F32), jax.ShapeDtypeStruct((1, D_MODEL), F32)],
        scratch_shapes=[pltpu.VMEM((tm, D_MODEL), F32), pltpu.VMEM((8, D_MODEL), F32)],
        compiler_params=pltpu.CompilerParams(dimension_semantics=("arbitrary", "arbitrary")),
    )(dproj, w_in, x, dx1, g)


def _pair_sum(name, grad, got, axis, size, core):
    shard = list(grad.shape)
    shard[axis] = size
    r, cdim = shard
    tr = min(r, TR_ELEMWISE)
    nr = r // tr

    def body(core_ref, g_ref, got_ref, o_ref):
        o_ref[0] = (g_ref[...] + got_ref[0]).astype(BF16)

    if axis == 1:
        g_map = lambda q, t, core_ref: (t, 2 * q + core_ref[0])
    else:
        g_map = lambda q, t, core_ref: ((2 * q + core_ref[0]) * nr + t, 0)
    return pl.pallas_call(
        body,
        name=name,
        grid_spec=pltpu.PrefetchScalarGridSpec(
            num_scalar_prefetch=1,
            grid=(N_CHIP, nr),
            in_specs=[
                pl.BlockSpec((tr, cdim), g_map),
                pl.BlockSpec((1, tr, cdim), lambda q, t, core_ref: (q, t, 0)),
            ],
            out_specs=pl.BlockSpec((1, tr, cdim), lambda q, t, core_ref: (q, t, 0)),
        ),
        out_shape=jax.ShapeDtypeStruct((N_CHIP, r, cdim), BF16),
        compiler_params=pltpu.CompilerParams(dimension_semantics=("arbitrary", "arbitrary")),
    )(core, grad, got)


def _adamw(name, w, m, v, parts):
    r, cdim = w.shape
    n_parts = parts.shape[0]
    tr = min(r, TR_ELEMWISE)
    c1 = 1.0 / (1.0 - ADAM_B1**ADAM_STEP)
    c2 = 1.0 / (1.0 - ADAM_B2**ADAM_STEP)

    def body(w_ref, m_ref, v_ref, p_ref, g_ref, d_ref, nm_ref, nv_ref):
        g = p_ref[0].astype(F32)
        for q in range(1, n_parts):
            g = g + p_ref[q].astype(F32)
        m_new = ADAM_B1 * m_ref[...] + (1.0 - ADAM_B1) * g
        v_new = ADAM_B2 * v_ref[...] + (1.0 - ADAM_B2) * (g * g)
        g_ref[...] = g
        nm_ref[...] = m_new
        nv_ref[...] = v_new
        d_ref[...] = -ADAM_LR * ((m_new * c1) / (jnp.sqrt(v_new * c2) + ADAM_EPS) + ADAM_WD * w_ref[...])

    blk = pl.BlockSpec((tr, cdim), lambda t: (t, 0))
    out = jax.ShapeDtypeStruct((r, cdim), F32)
    return pl.pallas_call(
        body,
        name=name,
        grid=(r // tr,),
        in_specs=[blk, blk, blk, pl.BlockSpec((n_parts, tr, cdim), lambda t: (0, t, 0))],
        out_specs=[blk, blk, blk, blk],
        out_shape=[out, out, out, out],
        compiler_params=pltpu.CompilerParams(dimension_semantics=("arbitrary",)),
    )(w, m, v, parts)


def kernel(x, mem, norm_mix_g, w_in, conv_w, gm_ln_g, gm_ln_b, gm_ws, gm_bs, w_out, norm_x_g, norm_mem_g, w_q, w_kv, w_xo, norm_final_g, loss_target, m_norm_mix_g, m_w_in, m_conv_w, m_gm_ln_g, m_gm_ln_b, m_gm_ws, m_gm_bs, m_w_out, m_norm_x_g, m_norm_mem_g, m_w_q, m_w_kv, m_w_xo, m_norm_final_g, v_norm_mix_g, v_w_in, v_conv_w, v_gm_ln_g, v_gm_ln_b, v_gm_ws, v_gm_bs, v_w_out, v_norm_x_g, v_norm_mem_g, v_w_q, v_w_kv, v_w_xo, v_norm_final_g):
    s = x.shape[1]
    dev = 4 * lax.axis_index("x") + 2 * lax.axis_index("y") + lax.axis_index("c")
    core = lax.axis_index("c").astype(jnp.int32).reshape(1)
    x2 = x[0]
    target = loss_target[0]
    pad_taps = lambda t: jnp.pad(t[0], ((0, 5), (0, 0)))

    big_names = ["w_in", "w_out", "w_q", "w_kv", "w_xo"]
    big_w = dict(w_in=w_in[0], w_out=w_out[0], w_q=w_q[0], w_kv=w_kv[0], w_xo=w_xo[0])
    big_m = dict(w_in=m_w_in[0], w_out=m_w_out[0], w_q=m_w_q[0], w_kv=m_w_kv[0], w_xo=m_w_xo[0])
    big_v = dict(w_in=v_w_in[0], w_out=v_w_out[0], w_q=v_w_q[0], w_kv=v_w_kv[0], w_xo=v_w_xo[0])
    shard_axis = dict(w_in=1, w_out=0, w_q=0, w_kv=1, w_xo=0)
    full_shape = dict(w_in=(D_MODEL, IN_DIM), w_out=(MIX_DIM, D_MODEL), w_q=(D_MODEL, D_MODEL),
                      w_kv=(D_MODEL, 2 * D_MODEL), w_xo=(D_MODEL, D_MODEL))
    shard_size = {k: big_w[k].shape[shard_axis[k]] for k in big_names}
    gathered = _all_gather(
        "weight_all_gather",
        [big_w[k].astype(BF16) for k in big_names] + [pad_taps(conv_w)[None]],
        [(full_shape[k], shard_axis[k], shard_size[k]) for k in big_names] + [((N_DEV, 8, CHUNK), 0, 1)],
    )
    wf = dict(zip(big_names, gathered[:5]))
    conv_full = gathered[5].transpose(1, 0, 2).reshape(8, D_MODEL)

    ws = gm_ws[0]
    bs_b = jnp.broadcast_to(gm_bs[0][:, :, None], (8, CHUNK, CHUNK))
    g_f = norm_final_g.reshape(1, D_MODEL)

    proj, h1, h1t = _inproj(x2, norm_mix_g, wf["w_in"])
    cat = _mixer_fwd(proj, conv_full, gm_ln_g, gm_ln_b, ws, bs_b)
    k, vv, mt = _kv_fwd(mem[0], norm_mem_g, wf["w_kv"])
    (dx1, dcat, catt, h2t, ot, dx2b, dqb, dx1b, dk, dvv, d_gf, d_gx, sq_err) = _attn(
        x2, cat, target, wf["w_out"], wf["w_q"], wf["w_xo"], k, vv, norm_x_g, g_f)
    loss = lax.psum(0.5 * jnp.sum(sq_err) / D_MODEL, ("x", "y", "c"))

    dproj, d_cw, d_lng, d_lnb, d_ws, d_bs = _mixer_bwd(proj, dcat, conv_full, gm_ln_g, gm_ln_b, ws, bs_b)
    grad_x, d_gmix = _dh1(dproj, wf["w_in"], x2, dx1, norm_mix_g)
    dw_kv, d_gmem = _kv_bwd(mem[0], wf["w_kv"], mt, dk, dvv)
    big_g = dict(
        w_in=_matmul_f32("dw_in", h1t, dproj),
        w_out=_matmul_f32("dw_out", catt, dx1b),
        w_q=_matmul_f32("dw_q", h2t, dqb),
        w_kv=dw_kv,
        w_xo=_matmul_f32("dw_xo", ot, dx2b),
    )

    got = _sibling_exchange([big_g[k] for k in big_names], [(shard_axis[k], shard_size[k]) for k in big_names])
    pair = [_pair_sum("pair_sum_" + k, big_g[k], got[i], shard_axis[k], shard_size[k], core) for i, k in enumerate(big_names)]
    parts = dict(zip(big_names, _chip_exchange(pair)))

    vec_names = ["norm_mix_g", "gm_ln_g", "gm_ln_b", "gm_bs", "norm_x_g", "norm_mem_g", "norm_final_g"]
    vec_g = [d_gmix, d_lng, d_lnb, d_bs, d_gx, d_gmem, d_gf]
    small = jnp.concatenate(vec_g + [jnp.zeros((1, D_MODEL), F32), d_cw], axis=0)
    small_all, ws_all = _all_gather(
        "small_grad_all_gather",
        [small[None], d_ws.reshape(1, 8 * CHUNK, CHUNK)],
        [((N_DEV, 16, D_MODEL), 0, 1), ((N_DEV, 8 * CHUNK, CHUNK), 0, 1)],
    )
    row = lambda t: t.reshape(1, D_MODEL)
    vec_w = [norm_mix_g, gm_ln_g, gm_ln_b, row(gm_bs), norm_x_g, norm_mem_g, row(norm_final_g)]
    vec_m = [m_norm_mix_g, m_gm_ln_g, m_gm_ln_b, row(m_gm_bs), m_norm_x_g, m_norm_mem_g, row(m_norm_final_g)]
    vec_v = [v_norm_mix_g, v_gm_ln_g, v_gm_ln_b, row(v_gm_bs), v_norm_x_g, v_norm_mem_g, row(v_norm_final_g)]
    stack8 = lambda ts: jnp.concatenate(ts + [jnp.ones((1, D_MODEL), F32)], axis=0)
    vec_out = _adamw("adamw_vectors", stack8(vec_w), stack8(vec_m), stack8(vec_v), small_all[:, :8, :])
    conv_parts = lax.dynamic_slice(small_all, (0, 8, dev * CHUNK), (N_DEV, 8, CHUNK))
    conv_out = _adamw("adamw_conv", pad_taps(conv_w), pad_taps(m_conv_w), jnp.pad(v_conv_w[0], ((0, 5), (0, 0)), constant_values=1.0), conv_parts)
    ws_out = _adamw("adamw_gm_ws", ws.reshape(8 * CHUNK, CHUNK), m_gm_ws.reshape(8 * CHUNK, CHUNK), v_gm_ws.reshape(8 * CHUNK, CHUNK), ws_all)
    big_out = {k: _adamw("adamw_" + k, big_w[k], big_m[k], big_v[k], parts[k]) for k in big_names}

    def result(name, which):
        if name in big_out:
            return big_out[name][which][None]
        if name == "conv_w":
            return conv_out[which][:3][None]
        if name == "gm_ws":
            return ws_out[which].reshape(1, 8, CHUNK, CHUNK)
        r = vec_out[which][vec_names.index(name)]
        if name == "gm_bs":
            return r.reshape(1, 8, CHUNK)
        if name == "norm_final_g":
            return r
        return r.reshape(1, D_MODEL)

    weights = ["norm_mix_g", "w_in", "conv_w", "gm_ln_g", "gm_ln_b", "gm_ws", "gm_bs", "w_out", "norm_x_g", "norm_mem_g", "w_q", "w_kv", "w_xo", "norm_final_g"]
    outs = [loss, grad_x[None]]
    for which in range(4):
        outs += [result(name, which) for name in weights]
    return tuple(outs)
```

```python
import math

import jax
import jax.numpy as jnp
from jax import lax
from jax.experimental import pallas as pl
from jax.experimental.pallas import tpu as pltpu

F32 = jnp.float32
BF16 = jnp.bfloat16
MESH = pl.DeviceIdType.MESH

D_MODEL = 1024
N_DEV = 8
N_CHIP = 4
CHUNK = 128
GROUPS = 7
IN_DIM = GROUPS * D_MODEL
MIX_DIM = 2 * D_MODEL
X_HEADS = 4
X_HEAD_DIM = D_MODEL // X_HEADS
MEM_LEN = 256
EPS = 1e-6
GELU_K0 = math.sqrt(2.0 / math.pi)
GELU_K1 = 0.044715

ADAM_LR = 0.001
ADAM_B1 = 0.9
ADAM_B2 = 0.999
ADAM_EPS = 1e-08
ADAM_WD = 0.01
ADAM_STEP = 10

TM_INPROJ = 512
TM_ATTN = 256
TM_DH1 = 512
TK_DW = 1024
TR_ELEMWISE = 256


def _dot(a, b):
    return lax.dot_general(a, b, (((1,), (0,)), ((), ())), preferred_element_type=F32)


def _dot_nt(a, b):
    return lax.dot_general(a, b, (((1,), (1,)), ((), ())), preferred_element_type=F32)


def _rows8(v):
    return jnp.sum(v.reshape(v.shape[0] // 8, 8, v.shape[1]), axis=0)


def _gelu(x):
    x2 = x * x
    t = jnp.tanh(GELU_K0 * (x + GELU_K1 * (x * x2)))
    g = 0.5 * x * (1.0 + t)
    dg = 0.5 * (1.0 + t) + 0.5 * x * (1.0 - t * t) * (GELU_K0 * (1.0 + 3.0 * GELU_K1 * x2))
    return g, dg


def _silu(z):
    s = jax.nn.sigmoid(z)
    return z * s, s * (1.0 + z * (1.0 - s))


def _rms(v):
    r = lax.rsqrt(jnp.mean(v * v, axis=-1, keepdims=True) + EPS)
    return v * r, r


def _rms_bwd(dy_g, vh, r):
    return r * (dy_g - vh * jnp.mean(dy_g * vh, axis=-1, keepdims=True))


def _place():
    return lax.axis_index("x"), lax.axis_index("y"), lax.axis_index("c")


def _other_chips(x, y):
    return [(1 - x, y), (x, 1 - y), (1 - x, 1 - y)]


def _pow2_divisor(n):
    return n & (-n)


def _shard_view(ref, axis, size, d):
    start = d * size
    align = _pow2_divisor(size)
    if align > 1:
        start = pl.multiple_of(start, align)
    idx = [slice(None)] * len(ref.shape)
    idx[axis] = pl.ds(start, size)
    return ref.at[tuple(idx)]


class _Plan:
    def __init__(self, ins, out_shape, sems, start, finish, mid=None, aliases=None):
        self.ins, self.out_shape, self.sems = list(ins), list(out_shape), list(sems)
        self.start, self.mid, self.finish = start, mid, finish
        self.aliases = dict(aliases or {})


def _gather_plan(shards, specs):
    n = len(shards)
    n_copy = 7

    def copies(ins, outs, sems):
        send_sems, recv_sems, local_sems = sems
        x, y, c = _place()
        me, sibling = (x, y, c), (x, y, 1 - c)
        chips = _other_chips(x, y)

        def block(a, p):
            _, axis, size = specs[a]
            return _shard_view(outs[a], axis, size, 4 * p[0] + 2 * p[1] + p[2])

        def copy(a, k, p, to, src=None):
            return pltpu.make_async_remote_copy(
                src_ref=block(a, p) if src is None else src,
                dst_ref=block(a, p),
                send_sem=send_sems.at[a * n_copy + k],
                recv_sem=recv_sems.at[a * n_copy + k],
                device_id=to,
                device_id_type=MESH,
            )

        def mine():
            return [pltpu.make_async_copy(ins[a], block(a, me), local_sems.at[a]) for a in range(n)]

        def first():
            return [cp for a in range(n) for cp in
                    [copy(a, 0, me, sibling, src=ins[a])] + [copy(a, 1 + j, me, (*chip, c), src=ins[a]) for j, chip in enumerate(chips)]]

        def landed():
            return [copy(a, 1 + j, (*chip, c), me) for j, chip in enumerate(chips) for a in range(n)]

        def passed():
            return [copy(a, 4 + j, (*chip, c), sibling) for j, chip in enumerate(chips) for a in range(n)]

        def last():
            return [cp for a in range(n) for cp in
                    [copy(a, 0, sibling, me)] + [copy(a, 4 + j, (*chip, 1 - c), me) for j, chip in enumerate(chips)]]

        return mine, first, landed, passed, last

    def start(ins, outs, sems):
        mine, first, _, _, _ = copies(ins, outs, sems)
        for cp in mine() + first():
            cp.start()

    def mid(ins, outs, sems):
        _, _, landed, passed, _ = copies(ins, outs, sems)
        for got, fwd in zip(landed(), passed()):
            got.wait_recv()
            fwd.start()

    def finish(ins, outs, sems):
        mine, first, _, passed, last = copies(ins, outs, sems)
        for cp in last():
            cp.wait_recv()
        for cp in first() + passed():
            cp.wait_send()
        for cp in mine():
            cp.wait()

    return _Plan(
        shards,
        [jax.ShapeDtypeStruct(full, s.dtype) for s, (full, _, _) in zip(shards, specs)],
        [pltpu.SemaphoreType.DMA((n * n_copy,)), pltpu.SemaphoreType.DMA((n * n_copy,)), pltpu.SemaphoreType.DMA((n,))],
        start, finish, mid,
    )


def _sibling_plan(grads, specs):
    n = len(grads)

    def shard_shape(a):
        axis, size = specs[a]
        shp = list(grads[a].shape)
        shp[axis] = size
        return tuple(shp)

    def copies(ins, outs, sems):
        send_sems, recv_sems = sems
        x, y, c = _place()
        return [
            pltpu.make_async_remote_copy(
                src_ref=_shard_view(ins[a], specs[a][0], specs[a][1], 2 * q + (1 - c)),
                dst_ref=outs[a].at[q],
                send_sem=send_sems.at[a * N_CHIP + q],
                recv_sem=recv_sems.at[a * N_CHIP + q],
                device_id=(x, y, 1 - c),
                device_id_type=MESH,
            )
            for a in range(n) for q in range(N_CHIP)
        ]

    def start(ins, outs, sems):
        for cp in copies(ins, outs, sems):
            cp.start()

    def finish(ins, outs, sems):
        for cp in copies(ins, outs, sems):
            cp.wait()

    return _Plan(
        grads,
        [jax.ShapeDtypeStruct((N_CHIP, *shard_shape(a)), grads[a].dtype) for a in range(n)],
        [pltpu.SemaphoreType.DMA((n * N_CHIP,)), pltpu.SemaphoreType.DMA((n * N_CHIP,))],
        start, finish,
    )


def _chip_plan(parts, full_rows=None, row0=None, into=None):
    n = len(parts)
    full_rows = [p.shape[1] for p in parts] if full_rows is None else full_rows
    row0 = [0] * n if row0 is None else row0

    def copies(ins, outs, sems):
        send_sems, recv_sems, local_sems = sems
        x, y, c = _place()
        my_chip = 2 * x + y
        chips = _other_chips(x, y)

        def land(a, q):
            return outs[a].at[q, pl.ds(row0[a], parts[a].shape[1]), :]

        def mine():
            return [pltpu.make_async_copy(ins[a].at[my_chip], land(a, my_chip), local_sems.at[a]) for a in range(n)]

        def remote(sending):
            return [
                pltpu.make_async_remote_copy(
                    src_ref=ins[a].at[2 * chip[0] + chip[1]],
                    dst_ref=land(a, my_chip if sending else 2 * chip[0] + chip[1]),
                    send_sem=send_sems.at[a * 3 + j], recv_sem=recv_sems.at[a * 3 + j],
                    device_id=(*chip, c), device_id_type=MESH)
                for a in range(n) for j, chip in enumerate(chips)
            ]

        return mine, remote

    def start(ins, outs, sems):
        mine, remote = copies(ins, outs, sems)
        for cp in mine() + remote(True):
            cp.start()

    def finish(ins, outs, sems):
        mine, remote = copies(ins, outs, sems)
        for cp in remote(False):
            cp.wait_recv()
        for cp in remote(True):
            cp.wait_send()
        for cp in mine():
            cp.wait()

    return _Plan(
        list(parts) + (list(into) if into is not None else []),
        [jax.ShapeDtypeStruct((N_CHIP, full_rows[a], parts[a].shape[2]), parts[a].dtype) for a in range(n)],
        [pltpu.SemaphoreType.DMA((n * 3,)), pltpu.SemaphoreType.DMA((n * 3,)), pltpu.SemaphoreType.DMA((n,))],
        start, finish,
        aliases={n + a: a for a in range(n)} if into is not None else None,
    )


def _call(body, *, name, operands, out_shape, grid=(), in_specs=None, out_specs=None, scratch_shapes=(),
          plans=(), aliases=None):
    operands, out_shape, scratch_shapes = list(operands), list(out_shape), list(scratch_shapes)
    n_in, n_out, n_scr = len(operands), len(out_shape), len(scratch_shapes)
    in_vmem = pl.BlockSpec(memory_space=pltpu.VMEM)
    in_hbm = pl.BlockSpec(memory_space=pl.ANY)
    in_specs = [in_vmem] * n_in if in_specs is None else list(in_specs)
    out_specs = [in_vmem] * n_out if out_specs is None else list(out_specs)
    io_alias = dict(aliases or {})
    pos_in, pos_out = n_in, n_out
    for p in plans:
        for i, o in p.aliases.items():
            io_alias[pos_in + i] = pos_out + o
        pos_in += len(p.ins)
        pos_out += len(p.out_shape)
    steps = math.prod(grid) if grid else 1
    mid_step = (steps * 3) // 5

    def wrapped(*refs):
        refs = list(refs)
        take = lambda k: [refs.pop(0) for _ in range(k)]
        ins, p_ins = take(n_in), [take(len(p.ins)) for p in plans]
        outs, p_outs = take(n_out), [take(len(p.out_shape)) for p in plans]
        scr, p_sems = take(n_scr), [take(len(p.sems)) for p in plans]
        step = 0
        for ax, g in enumerate(grid):
            step = step * g + pl.program_id(ax)

        def hook(kind, at):
            todo = [(getattr(p, kind), a, b, c) for p, a, b, c in zip(plans, p_ins, p_outs, p_sems) if getattr(p, kind)]

            def run():
                for fn, a, b, c in todo:
                    fn(a, b, c)

            if todo and grid:
                pl.when(step == at)(run)
            elif todo:
                run()

        hook("start", 0)
        if body is not None:
            body(*ins, *outs, *scr)
        hook("mid", mid_step)
        hook("finish", steps - 1)

    res = pl.pallas_call(
        wrapped,
        name=name,
        grid=grid,
        in_specs=in_specs + [in_hbm] * sum(len(p.ins) for p in plans),
        out_specs=out_specs + [in_hbm] * sum(len(p.out_shape) for p in plans),
        out_shape=out_shape + [o for p in plans for o in p.out_shape],
        scratch_shapes=scratch_shapes + [s for p in plans for s in p.sems],
        input_output_aliases=io_alias,
        compiler_params=pltpu.CompilerParams(dimension_semantics=("arbitrary",) * len(grid)) if grid else None,
    )(*operands, *[a for p in plans for a in p.ins])
    res = list(res)
    outs, plan_outs = res[:n_out], []
    pos = n_out
    for p in plans:
        plan_outs.append(res[pos : pos + len(p.out_shape)])
        pos += len(p.out_shape)
    return outs, plan_outs


def _inproj(x, g, w_in, plans=()):
    s = x.shape[0]
    tm = min(TM_INPROJ, s)

    def body(x_ref, g_ref, w_ref, proj_ref, h1_ref, h1t_ref):
        @pl.when(pl.program_id(1) == 0)
        def _():
            xh, _ = _rms(x_ref[...])
            h = xh * g_ref[...]
            h1_ref[...] = h.astype(BF16)
            h1t_ref[...] = h.T.astype(BF16)

        proj_ref[...] = _dot(h1_ref[...], w_ref[...]).astype(BF16)

    return _call(
        body,
        name="inproj",
        operands=[x, g, w_in],
        grid=(s // tm, GROUPS),
        in_specs=[
            pl.BlockSpec((tm, D_MODEL), lambda i, j: (i, 0)),
            pl.BlockSpec((1, D_MODEL), lambda i, j: (0, 0)),
            pl.BlockSpec((D_MODEL, D_MODEL), lambda i, j: (0, j)),
        ],
        out_specs=[
            pl.BlockSpec((tm, D_MODEL), lambda i, j: (i, j)),
            pl.BlockSpec((tm, D_MODEL), lambda i, j: (i, 0)),
            pl.BlockSpec((D_MODEL, tm), lambda i, j: (0, i)),
        ],
        out_shape=[
            jax.ShapeDtypeStruct((s, IN_DIM), BF16),
            jax.ShapeDtypeStruct((s, D_MODEL), BF16),
            jax.ShapeDtypeStruct((D_MODEL, s), BF16),
        ],
        plans=plans,
    )


def _causal_mask():
    row = lax.broadcasted_iota(jnp.int32, (CHUNK, CHUNK), 0)
    col = lax.broadcasted_iota(jnp.int32, (CHUNK, CHUNK), 1)
    return row, row >= col


def _mixer_fwd(proj, conv_w, ln_g, ln_b, ws, bs_b):
    s = proj.shape[0]
    nt = s // CHUNK

    def body(proj_ref, cw_ref, lng_ref, lnb_ref, ws_ref, bsb_ref, cat_ref, pcar_ref):
        @pl.when(pl.program_id(0) == 0)
        def _():
            pcar_ref[...] = jnp.zeros_like(pcar_ref)

        row, tril = _causal_mask()
        for j in range(8):
            cs = slice(CHUNK * j, CHUNK * (j + 1))

            def grp(k):
                return proj_ref[:, k * D_MODEL + CHUNK * j : k * D_MODEL + CHUNK * (j + 1)].astype(F32)

            w = cw_ref[:, cs]
            p = grp(1) * grp(2)
            pc = pcar_ref[:, cs]
            p1 = jnp.where(row == 0, pc[7:8], pltpu.roll(p, 1, 0))
            p2 = jnp.where(row == 0, pc[6:7], jnp.where(row == 1, pc[7:8], pltpu.roll(p, 2, 0)))
            pcar_ref[:, cs] = p[CHUNK - 8 :]
            cv = w[0:1] * p2 + w[1:2] * p1 + w[2:3] * p
            sa, _ = _silu(grp(3))
            cat_ref[:, cs] = ((grp(0) * cv) * sa).astype(BF16)
            gu, _ = _gelu(grp(4))
            gv, _ = _gelu(grp(5))
            dv = gv - jnp.mean(gv, axis=-1, keepdims=True)
            vn = dv * lax.rsqrt(jnp.mean(dv * dv, axis=-1, keepdims=True) + EPS)
            vn = vn * lng_ref[:, cs] + lnb_ref[:, cs]
            wc = jnp.where(tril, ws_ref[j], 0.0).astype(BF16)
            sp = _dot(wc, vn.astype(BF16)) + bsb_ref[j]
            sb, _ = _silu(grp(6))
            cat_ref[:, D_MODEL + CHUNK * j : D_MODEL + CHUNK * (j + 1)] = ((gu * sp) * sb).astype(BF16)

    whole = lambda shape: pl.BlockSpec(shape, lambda i: (0,) * len(shape))
    return pl.pallas_call(
        body,
        name="mixer_fwd",
        grid=(nt,),
        in_specs=[
            pl.BlockSpec((CHUNK, IN_DIM), lambda i: (i, 0)),
            whole((8, D_MODEL)),
            whole((1, D_MODEL)),
            whole((1, D_MODEL)),
            whole((8, CHUNK, CHUNK)),
            whole((8, CHUNK, CHUNK)),
        ],
        out_specs=pl.BlockSpec((CHUNK, MIX_DIM), lambda i: (i, 0)),
        out_shape=jax.ShapeDtypeStruct((s, MIX_DIM), BF16),
        scratch_shapes=[pltpu.VMEM((8, D_MODEL), F32)],
        compiler_params=pltpu.CompilerParams(dimension_semantics=("arbitrary",)),
    )(proj, conv_w, ln_g, ln_b, ws, bs_b)


def _mixer_bwd(proj, dcat, conv_w, ln_g, ln_b, ws, bs_b, plans=()):
    s = proj.shape[0]
    nt = s // CHUNK

    def body(proj_ref, halo_ref, dcat_ref, cw_ref, lng_ref, lnb_ref, ws_ref, bsb_ref,
             dproj_ref, dcw_ref, dlng_ref, dlnb_ref, dws_ref, dbs_ref,
             car_ref, acc_cw, acc_lng, acc_lnb, acc_bs):
        i = pl.program_id(0)
        tile = nt - 1 - i

        @pl.when(i == 0)
        def _():
            car_ref[...] = jnp.zeros_like(car_ref)
            acc_cw[...] = jnp.zeros_like(acc_cw)
            acc_lng[...] = jnp.zeros_like(acc_lng)
            acc_lnb[...] = jnp.zeros_like(acc_lnb)
            acc_bs[...] = jnp.zeros_like(acc_bs)
            dws_ref[...] = jnp.zeros_like(dws_ref)

        row, tril = _causal_mask()
        has_prev = jnp.where(tile > 0, 1.0, 0.0).astype(F32)
        for j in range(8):
            cs = slice(CHUNK * j, CHUNK * (j + 1))

            def col(k):
                return slice(k * D_MODEL + CHUNK * j, k * D_MODEL + CHUNK * (j + 1))

            def grp(k):
                return proj_ref[:, col(k)].astype(F32)

            w = cw_ref[:, cs]
            gb, gc, xa = grp(0), grp(1), grp(2)
            p = gc * xa
            pprev = halo_ref[:, col(1)].astype(F32) * halo_ref[:, col(2)].astype(F32) * has_prev
            p1 = jnp.where(row == 0, pprev[15:16], pltpu.roll(p, 1, 0))
            p2 = jnp.where(row == 0, pprev[14:15], jnp.where(row == 1, pprev[15:16], pltpu.roll(p, 2, 0)))
            cv = w[0:1] * p2 + w[1:2] * p1 + w[2:3] * p
            za = grp(3)
            sa, dsa = _silu(za)
            da = dcat_ref[:, cs].astype(F32)
            dproj_ref[:, col(0)] = (da * cv * sa).astype(BF16)
            dproj_ref[:, col(3)] = (da * gb * cv * dsa).astype(BF16)
            dcv = da * gb * sa
            nxt = car_ref[:, cs]
            d1 = jnp.where(row == CHUNK - 1, nxt[0:1], pltpu.roll(dcv, CHUNK - 1, 0))
            d2 = jnp.where(row == CHUNK - 2, nxt[0:1], jnp.where(row == CHUNK - 1, nxt[1:2], pltpu.roll(dcv, CHUNK - 2, 0)))
            car_ref[:, cs] = dcv[0:8]
            dp = w[2:3] * dcv + w[1:2] * d1 + w[0:1] * d2
            dproj_ref[:, col(1)] = (dp * xa).astype(BF16)
            dproj_ref[:, col(2)] = (dp * gc).astype(BF16)
            acc_cw[0, :, cs] += _rows8(dcv * p2)
            acc_cw[1, :, cs] += _rows8(dcv * p1)
            acc_cw[2, :, cs] += _rows8(dcv * p)
            gu, dgu = _gelu(grp(4))
            gv, dgv = _gelu(grp(5))
            dv = gv - jnp.mean(gv, axis=-1, keepdims=True)
            rstd = lax.rsqrt(jnp.mean(dv * dv, axis=-1, keepdims=True) + EPS)
            vnh = dv * rstd
            lng = lng_ref[:, cs]
            vnb = (vnh * lng + lnb_ref[:, cs]).astype(BF16)
            wcf = jnp.where(tril, ws_ref[j], 0.0)
            sp = _dot(wcf.astype(BF16), vnb) + bsb_ref[j]
            zb = grp(6)
            sb, dsb = _silu(zb)
            db = dcat_ref[:, D_MODEL + CHUNK * j : D_MODEL + CHUNK * (j + 1)].astype(F32)
            dproj_ref[:, col(4)] = (db * sp * sb * dgu).astype(BF16)
            dproj_ref[:, col(6)] = (db * gu * sp * dsb).astype(BF16)
            dsp = db * gu * sb
            acc_bs[j] += dsp
            dspb = dsp.astype(BF16)
            dws_ref[j] += _dot_nt(dspb, vnb)
            dvn = _dot(wcf.T.astype(BF16), dspb)
            acc_lnb[:, cs] += _rows8(dvn)
            acc_lng[:, cs] += _rows8(dvn * vnh)
            dvh = dvn * lng
            dgvv = rstd * (dvh - jnp.mean(dvh, axis=-1, keepdims=True) - vnh * jnp.mean(dvh * vnh, axis=-1, keepdims=True))
            dproj_ref[:, col(5)] = (dgvv * dgv).astype(BF16)

        @pl.when(i == nt - 1)
        def _():
            dlng_ref[...] = jnp.sum(acc_lng[...], axis=0, keepdims=True)
            dlnb_ref[...] = jnp.sum(acc_lnb[...], axis=0, keepdims=True)
            dcw_ref[...] = jnp.zeros_like(dcw_ref)
            for k in range(3):
                dcw_ref[k : k + 1, :] = jnp.sum(acc_cw[k], axis=0, keepdims=True)
            for j in range(8):
                dws_ref[j] = jnp.where(tril, dws_ref[j], 0.0)
                dbs_ref[:, CHUNK * j : CHUNK * (j + 1)] = jnp.sum(acc_bs[j].T, axis=0, keepdims=True)

    whole = lambda shape: pl.BlockSpec(shape, lambda i: (0,) * len(shape))
    halo_rows = 16
    per = CHUNK // halo_rows
    return _call(
        body,
        name="mixer_bwd",
        operands=[proj, proj, dcat, conv_w, ln_g, ln_b, ws, bs_b],
        plans=plans,
        grid=(nt,),
        in_specs=[
            pl.BlockSpec((CHUNK, IN_DIM), lambda i: (nt - 1 - i, 0)),
            pl.BlockSpec((halo_rows, IN_DIM), lambda i: (jnp.maximum((nt - 1 - i) * per - 1, 0), 0)),
            pl.BlockSpec((CHUNK, MIX_DIM), lambda i: (nt - 1 - i, 0)),
            whole((8, D_MODEL)),
            whole((1, D_MODEL)),
            whole((1, D_MODEL)),
            whole((8, CHUNK, CHUNK)),
            whole((8, CHUNK, CHUNK)),
        ],
        out_specs=[
            pl.BlockSpec((CHUNK, IN_DIM), lambda i: (nt - 1 - i, 0)),
            whole((8, D_MODEL)),
            whole((1, D_MODEL)),
            whole((1, D_MODEL)),
            whole((8, CHUNK, CHUNK)),
            whole((1, D_MODEL)),
        ],
        out_shape=[
            jax.ShapeDtypeStruct((s, IN_DIM), BF16),
            jax.ShapeDtypeStruct((8, D_MODEL), F32),
            jax.ShapeDtypeStruct((1, D_MODEL), F32),
            jax.ShapeDtypeStruct((1, D_MODEL), F32),
            jax.ShapeDtypeStruct((8, CHUNK, CHUNK), F32),
            jax.ShapeDtypeStruct((1, D_MODEL), F32),
        ],
        scratch_shapes=[
            pltpu.VMEM((8, D_MODEL), F32),
            pltpu.VMEM((3, 8, D_MODEL), F32),
            pltpu.VMEM((8, D_MODEL), F32),
            pltpu.VMEM((8, D_MODEL), F32),
            pltpu.VMEM((8, CHUNK, CHUNK), F32),
        ],
    )


def _kv_fwd(mem, g_mem, w_kv):
    def body(mem_ref, g_ref, w_ref, k_ref, vv_ref, mt_ref):
        mh, _ = _rms(mem_ref[...])
        m = mh * g_ref[...]
        kv = _dot(m.astype(BF16), w_ref[...])
        k_ref[...] = kv[:, :D_MODEL].astype(BF16)
        vv_ref[...] = kv[:, D_MODEL:].astype(BF16)
        mt_ref[...] = m.T.astype(BF16)

    return pl.pallas_call(
        body,
        name="kv_fwd",
        out_shape=[
            jax.ShapeDtypeStruct((MEM_LEN, D_MODEL), BF16),
            jax.ShapeDtypeStruct((MEM_LEN, D_MODEL), BF16),
            jax.ShapeDtypeStruct((D_MODEL, MEM_LEN), BF16),
        ],
    )(mem, g_mem, w_kv)


def _kv_bwd(mem, w_kv, mt, dk, dvv):
    def body(mem_ref, w_ref, mt_ref, dk_ref, dvv_ref, dw_ref, dg_ref):
        dkv = jnp.concatenate([dk_ref[...], dvv_ref[...]], axis=1).astype(BF16)
        dw_ref[...] = _dot(mt_ref[...], dkv)
        dm = _dot_nt(dkv, w_ref[...])
        mh, _ = _rms(mem_ref[...])
        dg_ref[...] = jnp.sum(dm * mh, axis=0, keepdims=True)

    return pl.pallas_call(
        body,
        name="kv_bwd",
        out_shape=[
            jax.ShapeDtypeStruct((D_MODEL, 2 * D_MODEL), F32),
            jax.ShapeDtypeStruct((1, D_MODEL), F32),
        ],
    )(mem, w_kv, mt, dk, dvv)


def _attn(x, cat, target, w_out, w_q, w_xo, k, vv, g_x, g_f):
    s = x.shape[0]
    tm = min(TM_ATTN, s)
    nt = s // tm
    scale = 1.0 / math.sqrt(X_HEAD_DIM)

    def body(x_ref, cat_ref, t_ref, wout_ref, wq_ref, wxo_ref, k_ref, vv_ref, gx_ref, gf_ref,
             dx1_ref, dcat_ref, catt_ref, h2t_ref, ot_ref, dx2b_ref, dqb_ref, dx1b_ref,
             dk_ref, dvv_ref, dgf_ref, dgx_ref, loss_ref,
             p_scr, q_scr, o_scr, dq_scr, acc_gf, acc_gx, acc_loss):
        i = pl.program_id(0)

        @pl.when(i == 0)
        def _():
            dk_ref[...] = jnp.zeros_like(dk_ref)
            dvv_ref[...] = jnp.zeros_like(dvv_ref)
            acc_gf[...] = jnp.zeros_like(acc_gf)
            acc_gx[...] = jnp.zeros_like(acc_gx)
            acc_loss[...] = jnp.zeros_like(acc_loss)

        cat = cat_ref[...]
        catt_ref[...] = cat.astype(F32).T.astype(BF16)
        x1 = x_ref[...] + _dot(cat, wout_ref[...])
        x1h, r2 = _rms(x1)
        h2 = x1h * gx_ref[...]
        h2b = h2.astype(BF16)
        h2t_ref[...] = h2.T.astype(BF16)
        q_scr[...] = _dot(h2b, wq_ref[...]).astype(BF16)
        for h in range(X_HEADS):
            hs = slice(X_HEAD_DIM * h, X_HEAD_DIM * (h + 1))
            sc = _dot_nt(q_scr[:, hs], k_ref[:, hs]) * scale
            e = jnp.exp(sc - jnp.max(sc, axis=-1, keepdims=True))
            p = e / jnp.sum(e, axis=-1, keepdims=True)
            p_scr[:, hs] = p
            o_scr[:, hs] = _dot(p.astype(BF16), vv_ref[:, hs])
        o = o_scr[...]
        ot_ref[...] = o.T.astype(BF16)
        x2 = x1 + _dot(o.astype(BF16), wxo_ref[...])
        x2h, r3 = _rms(x2)
        gf = gf_ref[...]
        err = x2h * gf - t_ref[...]
        acc_loss[...] += _rows8(err * err)
        dy = err * (1.0 / D_MODEL)
        acc_gf[...] += _rows8(dy * x2h)
        dx2 = _rms_bwd(dy * gf, x2h, r3)
        dx2b = dx2.astype(BF16)
        dx2b_ref[...] = dx2b
        dob = _dot_nt(dx2b, wxo_ref[...]).astype(BF16)
        for h in range(X_HEADS):
            hs = slice(X_HEAD_DIM * h, X_HEAD_DIM * (h + 1))
            p = p_scr[:, hs]
            dp = _dot_nt(dob[:, hs], vv_ref[:, hs])
            ds = p * (dp - jnp.sum(dp * p, axis=-1, keepdims=True)) * scale
            dsb = ds.astype(BF16)
            dq_scr[:, hs] = _dot(dsb, k_ref[:, hs]).astype(BF16)
            dvv_ref[:, hs] += _dot(p.T.astype(BF16), dob[:, hs])
            dk_ref[:, hs] += _dot(ds.T.astype(BF16), q_scr[:, hs])
        dqb = dq_scr[...]
        dqb_ref[...] = dqb
        dh2 = _dot_nt(dqb, wq_ref[...])
        acc_gx[...] += _rows8(dh2 * x1h)
        dx1 = dx2 + _rms_bwd(dh2 * gx_ref[...], x1h, r2)
        dx1_ref[...] = dx1
        dx1b = dx1.astype(BF16)
        dx1b_ref[...] = dx1b
        dcat_ref[...] = _dot_nt(dx1b, wout_ref[...]).astype(BF16)

        @pl.when(i == nt - 1)
        def _():
            dgf_ref[...] = jnp.sum(acc_gf[...], axis=0, keepdims=True)
            dgx_ref[...] = jnp.sum(acc_gx[...], axis=0, keepdims=True)
            loss_ref[...] = jnp.sum(acc_loss[...], axis=0, keepdims=True)

    whole = lambda shape: pl.BlockSpec(shape, lambda i: (0,) * len(shape))
    rows = lambda width: pl.BlockSpec((tm, width), lambda i: (i, 0))
    cols = lambda height: pl.BlockSpec((height, tm), lambda i: (0, i))
    vec = jax.ShapeDtypeStruct((1, D_MODEL), F32)
    return pl.pallas_call(
        body,
        name="attn",
        grid=(nt,),
        in_specs=[
            rows(D_MODEL), rows(MIX_DIM), rows(D_MODEL),
            whole((MIX_DIM, D_MODEL)), whole((D_MODEL, D_MODEL)), whole((D_MODEL, D_MODEL)),
            whole((MEM_LEN, D_MODEL)), whole((MEM_LEN, D_MODEL)),
            whole((1, D_MODEL)), whole((1, D_MODEL)),
        ],
        out_specs=[
            rows(D_MODEL), rows(MIX_DIM), cols(MIX_DIM), cols(D_MODEL), cols(D_MODEL),
            rows(D_MODEL), rows(D_MODEL), rows(D_MODEL),
            whole((MEM_LEN, D_MODEL)), whole((MEM_LEN, D_MODEL)),
            whole((1, D_MODEL)), whole((1, D_MODEL)), whole((1, D_MODEL)),
        ],
        out_shape=[
            jax.ShapeDtypeStruct((s, D_MODEL), F32),
            jax.ShapeDtypeStruct((s, MIX_DIM), BF16),
            jax.ShapeDtypeStruct((MIX_DIM, s), BF16),
            jax.ShapeDtypeStruct((D_MODEL, s), BF16),
            jax.ShapeDtypeStruct((D_MODEL, s), BF16),
            jax.ShapeDtypeStruct((s, D_MODEL), BF16),
            jax.ShapeDtypeStruct((s, D_MODEL), BF16),
            jax.ShapeDtypeStruct((s, D_MODEL), BF16),
            jax.ShapeDtypeStruct((MEM_LEN, D_MODEL), F32),
            jax.ShapeDtypeStruct((MEM_LEN, D_MODEL), F32),
            vec, vec, vec,
        ],
        scratch_shapes=[
            pltpu.VMEM((tm, D_MODEL), F32),
            pltpu.VMEM((tm, D_MODEL), BF16),
            pltpu.VMEM((tm, D_MODEL), F32),
            pltpu.VMEM((tm, D_MODEL), BF16),
            pltpu.VMEM((8, D_MODEL), F32),
            pltpu.VMEM((8, D_MODEL), F32),
            pltpu.VMEM((8, D_MODEL), F32),
        ],
        compiler_params=pltpu.CompilerParams(dimension_semantics=("arbitrary",)),
    )(x, cat, target, w_out, w_q, w_xo, k, vv, g_x, g_f)


def _matmul_f32(name, a, b, rows=None, plans=()):
    first, m = (0, a.shape[0]) if rows is None else rows
    kk = a.shape[1]
    n = b.shape[1]
    tm, tn, tk = min(m, D_MODEL), min(n, D_MODEL), min(kk, TK_DW)
    nk = kk // tk
    i0 = first // tm

    def body(a_ref, b_ref, o_ref):
        kid = pl.program_id(2)
        prod = _dot(a_ref[...], b_ref[...])

        @pl.when(kid == 0)
        def _():
            o_ref[...] = prod

        @pl.when(kid > 0)
        def _():
            o_ref[...] += prod

    (out,), plan_outs = _call(
        body,
        name=name,
        operands=[a, b],
        grid=(m // tm, n // tn, nk),
        in_specs=[pl.BlockSpec((tm, tk), lambda i, j, k: (i + i0, k)), pl.BlockSpec((tk, tn), lambda i, j, k: (k, j))],
        out_specs=[pl.BlockSpec((tm, tn), lambda i, j, k: (i, j))],
        out_shape=[jax.ShapeDtypeStruct((m, n), F32)],
        plans=plans,
    )
    return out, plan_outs


def _dh1(name, dproj, w_in, x, dx1, g, tiles, into=None, plans=()):
    s = x.shape[0]
    tm = min(TM_DH1, s)
    t0, nt = tiles[0], tiles[1] - tiles[0]

    def body(dp_ref, w_ref, x_ref, dx1_ref, g_ref, *rest):
        gx_ref, dg_ref, acc_ref, acc_g = rest[-4:]
        i, j = pl.program_id(0), pl.program_id(1)
        prod = _dot_nt(dp_ref[...], w_ref[...])

        @pl.when(j == 0)
        def _():
            acc_ref[...] = prod

        @pl.when(j > 0)
        def _():
            acc_ref[...] += prod

        @pl.when((i == 0) & (j == 0))
        def _():
            acc_g[...] = jnp.zeros_like(acc_g)

        @pl.when(j == GROUPS - 1)
        def _():
            dh1 = acc_ref[...]
            xh, r = _rms(x_ref[...])
            acc_g[...] += _rows8(dh1 * xh)
            gx_ref[...] = dx1_ref[...] + _rms_bwd(dh1 * g_ref[...], xh, r)

        @pl.when((i == nt - 1) & (j == GROUPS - 1))
        def _():
            dg_ref[...] = jnp.sum(acc_g[...], axis=0, keepdims=True)

    rows = pl.BlockSpec((tm, D_MODEL), lambda i, j: (i + t0, 0))
    return _call(
        body,
        name=name,
        operands=[dproj, w_in, x, dx1, g] + ([into] if into is not None else []),
        grid=(nt, GROUPS),
        in_specs=[
            pl.BlockSpec((tm, D_MODEL), lambda i, j: (i + t0, j)),
            pl.BlockSpec((D_MODEL, D_MODEL), lambda i, j: (0, j)),
            rows,
            rows,
            pl.BlockSpec((1, D_MODEL), lambda i, j: (0, 0)),
        ] + ([pl.BlockSpec(memory_space=pl.ANY)] if into is not None else []),
        out_specs=[rows, pl.BlockSpec((1, D_MODEL), lambda i, j: (0, 0))],
        out_shape=[jax.ShapeDtypeStruct((s, D_MODEL), F32), jax.ShapeDtypeStruct((1, D_MODEL), F32)],
        scratch_shapes=[pltpu.VMEM((tm, D_MODEL), F32), pltpu.VMEM((8, D_MODEL), F32)],
        plans=plans,
        aliases={5: 0} if into is not None else None,
    )


def _pair_sum(name, grad, got, axis, size, core):
    shard = list(grad.shape)
    shard[axis] = size
    r, cdim = shard
    tr = min(r, TR_ELEMWISE)
    nr = r // tr

    def body(core_ref, g_ref, got_ref, o_ref):
        o_ref[0] = (g_ref[...] + got_ref[0]).astype(BF16)

    if axis == 1:
        g_map = lambda q, t, core_ref: (t, 2 * q + core_ref[0])
    else:
        g_map = lambda q, t, core_ref: ((2 * q + core_ref[0]) * nr + t, 0)
    return pl.pallas_call(
        body,
        name=name,
        grid_spec=pltpu.PrefetchScalarGridSpec(
            num_scalar_prefetch=1,
            grid=(N_CHIP, nr),
            in_specs=[
                pl.BlockSpec((tr, cdim), g_map),
                pl.BlockSpec((1, tr, cdim), lambda q, t, core_ref: (q, t, 0)),
            ],
            out_specs=pl.BlockSpec((1, tr, cdim), lambda q, t, core_ref: (q, t, 0)),
        ),
        out_shape=jax.ShapeDtypeStruct((N_CHIP, r, cdim), BF16),
        compiler_params=pltpu.CompilerParams(dimension_semantics=("arbitrary", "arbitrary")),
    )(core, grad, got)


def _adamw(name, w, m, v, parts, plans=()):
    r, cdim = w.shape
    n_parts = parts.shape[0]
    tr = min(r, TR_ELEMWISE)
    c1 = 1.0 / (1.0 - ADAM_B1**ADAM_STEP)
    c2 = 1.0 / (1.0 - ADAM_B2**ADAM_STEP)

    def body(w_ref, m_ref, v_ref, p_ref, g_ref, d_ref, nm_ref, nv_ref):
        g = p_ref[0].astype(F32)
        for q in range(1, n_parts):
            g = g + p_ref[q].astype(F32)
        m_new = ADAM_B1 * m_ref[...] + (1.0 - ADAM_B1) * g
        v_new = ADAM_B2 * v_ref[...] + (1.0 - ADAM_B2) * (g * g)
        g_ref[...] = g
        nm_ref[...] = m_new
        nv_ref[...] = v_new
        d_ref[...] = -ADAM_LR * ((m_new * c1) / (jnp.sqrt(v_new * c2) + ADAM_EPS) + ADAM_WD * w_ref[...])

    blk = pl.BlockSpec((tr, cdim), lambda t: (t, 0))
    out = jax.ShapeDtypeStruct((r, cdim), F32)
    return _call(
        body,
        name=name,
        operands=[w, m, v, parts],
        grid=(r // tr,),
        in_specs=[blk, blk, blk, pl.BlockSpec((n_parts, tr, cdim), lambda t: (0, t, 0))],
        out_specs=[blk, blk, blk, blk],
        out_shape=[out, out, out, out],
        plans=plans,
    )


def kernel(x, mem, norm_mix_g, w_in, conv_w, gm_ln_g, gm_ln_b, gm_ws, gm_bs, w_out, norm_x_g, norm_mem_g, w_q, w_kv, w_xo, norm_final_g, loss_target, m_norm_mix_g, m_w_in, m_conv_w, m_gm_ln_g, m_gm_ln_b, m_gm_ws, m_gm_bs, m_w_out, m_norm_x_g, m_norm_mem_g, m_w_q, m_w_kv, m_w_xo, m_norm_final_g, v_norm_mix_g, v_w_in, v_conv_w, v_gm_ln_g, v_gm_ln_b, v_gm_ws, v_gm_bs, v_w_out, v_norm_x_g, v_norm_mem_g, v_w_q, v_w_kv, v_w_xo, v_norm_final_g):
    s = x.shape[1]
    dev = 4 * lax.axis_index("x") + 2 * lax.axis_index("y") + lax.axis_index("c")
    core = lax.axis_index("c").astype(jnp.int32).reshape(1)
    x2 = x[0]
    target = loss_target[0]
    pad_taps = lambda t: jnp.pad(t[0], ((0, 5), (0, 0)))

    big_names = ["w_in", "w_out", "w_q", "w_kv", "w_xo"]
    big_w = dict(w_in=w_in[0], w_out=w_out[0], w_q=w_q[0], w_kv=w_kv[0], w_xo=w_xo[0])
    big_m = dict(w_in=m_w_in[0], w_out=m_w_out[0], w_q=m_w_q[0], w_kv=m_w_kv[0], w_xo=m_w_xo[0])
    big_v = dict(w_in=v_w_in[0], w_out=v_w_out[0], w_q=v_w_q[0], w_kv=v_w_kv[0], w_xo=v_w_xo[0])
    shard_axis = dict(w_in=1, w_out=0, w_q=0, w_kv=1, w_xo=0)
    full_shape = dict(w_in=(D_MODEL, IN_DIM), w_out=(MIX_DIM, D_MODEL), w_q=(D_MODEL, D_MODEL),
                      w_kv=(D_MODEL, 2 * D_MODEL), w_xo=(D_MODEL, D_MODEL))
    shard_size = {k: big_w[k].shape[shard_axis[k]] for k in big_names}
    others = big_names[1:]
    gather_spec = lambda k: (full_shape[k], shard_axis[k], shard_size[k])
    reduce_spec = lambda k: (shard_axis[k], shard_size[k])
    _, ((w_in_full, conv_g),) = _call(
        None, name="weight_all_gather_in", operands=[], out_shape=[],
        plans=[_gather_plan([big_w["w_in"].astype(BF16), pad_taps(conv_w)[None]], [gather_spec("w_in"), ((N_DEV, 8, CHUNK), 0, 1)])])
    conv_full = conv_g.transpose(1, 0, 2).reshape(8, D_MODEL)

    ws = gm_ws[0]
    bs_b = jnp.broadcast_to(gm_bs[0][:, :, None], (8, CHUNK, CHUNK))
    g_f = norm_final_g.reshape(1, D_MODEL)

    (proj, h1, h1t), (gathered,) = _inproj(
        x2, norm_mix_g, w_in_full,
        plans=[_gather_plan([big_w[k].astype(BF16) for k in others], [gather_spec(k) for k in others])])
    wf = dict(zip(others, gathered))
    cat = _mixer_fwd(proj, conv_full, gm_ln_g, gm_ln_b, ws, bs_b)
    k, vv, mt = _kv_fwd(mem[0], norm_mem_g, wf["w_kv"])
    (dx1, dcat, catt, h2t, ot, dx2b, dqb, dx1b, dk, dvv, d_gf, d_gx, sq_err) = _attn(
        x2, cat, target, wf["w_out"], wf["w_q"], wf["w_xo"], k, vv, norm_x_g, g_f)
    loss = lax.psum(0.5 * jnp.sum(sq_err) / D_MODEL, ("x", "y", "c"))

    dw_kv, d_gmem = _kv_bwd(mem[0], wf["w_kv"], mt, dk, dvv)
    big_g = dict(w_out=_matmul_f32("dw_out", catt, dx1b)[0], w_q=_matmul_f32("dw_q", h2t, dqb)[0],
                 w_kv=dw_kv, w_xo=_matmul_f32("dw_xo", ot, dx2b)[0])
    (dproj, d_cw, d_lng, d_lnb, d_ws, d_bs), (got,) = _mixer_bwd(
        proj, dcat, conv_full, gm_ln_g, gm_ln_b, ws, bs_b,
        plans=[_sibling_plan([big_g[k] for k in others], [reduce_spec(k) for k in others])])
    pair = [_pair_sum("pair_sum_" + k, big_g[k], got[i], *reduce_spec(k), core) for i, k in enumerate(others)]
    half = D_MODEL // 2
    in_spec = reduce_spec("w_in")
    dw_top, (parts_others,) = _matmul_f32("dw_in_top", h1t, dproj, rows=(0, half), plans=[_chip_plan(pair)])
    parts = dict(zip(others, parts_others))
    dw_bot, (got_top,) = _matmul_f32("dw_in_bot", h1t, dproj, rows=(half, half), plans=[_sibling_plan([dw_top], [in_spec])])
    pair_top = _pair_sum("pair_sum_w_in_top", dw_top, got_top[0], *in_spec, core)
    n_tiles = s // min(TM_DH1, s)
    (gx_half, d_gmix_a), (parts_top, got_bot) = _dh1(
        "dh1_a", dproj, w_in_full, x2, dx1, norm_mix_g, (0, n_tiles // 2),
        plans=[_chip_plan([pair_top], full_rows=[D_MODEL], row0=[0]), _sibling_plan([dw_bot], [in_spec])])
    pair_bot = _pair_sum("pair_sum_w_in_bot", dw_bot, got_bot[0], *in_spec, core)
    (grad_x, d_gmix_b), (parts_in, (ws_all,)) = _dh1(
        "dh1_b", dproj, w_in_full, x2, dx1, norm_mix_g, (n_tiles // 2, n_tiles), into=gx_half,
        plans=[_chip_plan([pair_bot], full_rows=[D_MODEL], row0=[half], into=parts_top),
               _gather_plan([d_ws.reshape(1, 8 * CHUNK, CHUNK)], [((N_DEV, 8 * CHUNK, CHUNK), 0, 1)])])
    parts["w_in"] = parts_in[0]

    vec_names = ["norm_mix_g", "gm_ln_g", "gm_ln_b", "gm_bs", "norm_x_g", "norm_mem_g", "norm_final_g"]
    vec_g = [d_gmix_a + d_gmix_b, d_lng, d_lnb, d_bs, d_gx, d_gmem, d_gf]
    small = jnp.concatenate(vec_g + [jnp.zeros((1, D_MODEL), F32), d_cw], axis=0)
    big_out = {}
    big_out["w_in"], ((small_all,),) = _adamw(
        "adamw_w_in", big_w["w_in"], big_m["w_in"], big_v["w_in"], parts["w_in"],
        plans=[_gather_plan([small[None]], [((N_DEV, 16, D_MODEL), 0, 1)])])
    for k in others:
        big_out[k], _ = _adamw("adamw_" + k, big_w[k], big_m[k], big_v[k], parts[k])
    row = lambda t: t.reshape(1, D_MODEL)
    vec_w = [norm_mix_g, gm_ln_g, gm_ln_b, row(gm_bs), norm_x_g, norm_mem_g, row(norm_final_g)]
    vec_m = [m_norm_mix_g, m_gm_ln_g, m_gm_ln_b, row(m_gm_bs), m_norm_x_g, m_norm_mem_g, row(m_norm_final_g)]
    vec_v = [v_norm_mix_g, v_gm_ln_g, v_gm_ln_b, row(v_gm_bs), v_norm_x_g, v_norm_mem_g, row(v_norm_final_g)]
    stack8 = lambda ts: jnp.concatenate(ts + [jnp.ones((1, D_MODEL), F32)], axis=0)
    vec_out, _ = _adamw("adamw_vectors", stack8(vec_w), stack8(vec_m), stack8(vec_v), small_all[:, :8, :])
    conv_parts = lax.dynamic_slice(small_all, (0, 8, dev * CHUNK), (N_DEV, 8, CHUNK))
    conv_out, _ = _adamw("adamw_conv", pad_taps(conv_w), pad_taps(m_conv_w), jnp.pad(v_conv_w[0], ((0, 5), (0, 0)), constant_values=1.0), conv_parts)
    ws_out, _ = _adamw("adamw_gm_ws", ws.reshape(8 * CHUNK, CHUNK), m_gm_ws.reshape(8 * CHUNK, CHUNK), v_gm_ws.reshape(8 * CHUNK, CHUNK), ws_all)

    def result(name, which):
        if name in big_out:
            return big_out[name][which][None]
        if name == "conv_w":
            return conv_out[which][:3][None]
        if name == "gm_ws":
            return ws_out[which].reshape(1, 8, CHUNK, CHUNK)
        r = vec_out[which][vec_names.index(name)]
        if name == "gm_bs":
            return r.reshape(1, 8, CHUNK)
        if name == "norm_final_g":
            return r
        return r.reshape(1, D_MODEL)

    weights = ["norm_mix_g", "w_in", "conv_w", "gm_ln_g", "gm_ln_b", "gm_ws", "gm_bs", "w_out", "norm_x_g", "norm_mem_g", "w_q", "w_kv", "w_xo", "norm_final_g"]
    outs = [loss, grad_x[None]]
    for which in range(4):
        outs += [result(name, which) for name in weights]
    return tuple(outs)
```

```python
import math

import jax
import jax.numpy as jnp
from jax import lax
from jax.experimental import pallas as pl
from jax.experimental.pallas import tpu as pltpu

F32 = jnp.float32
BF16 = jnp.bfloat16
MESH = pl.DeviceIdType.MESH

D_MODEL = 1024
N_DEV = 8
N_CHIP = 4
CHUNK = 128
GROUPS = 7
IN_DIM = GROUPS * D_MODEL
MIX_DIM = 2 * D_MODEL
X_HEADS = 4
X_HEAD_DIM = D_MODEL // X_HEADS
MEM_LEN = 256
EPS = 1e-6
GELU_K0 = math.sqrt(2.0 / math.pi)
GELU_K1 = 0.044715

ADAM_LR = 0.001
ADAM_B1 = 0.9
ADAM_B2 = 0.999
ADAM_EPS = 1e-08
ADAM_WD = 0.01
ADAM_STEP = 10

TM_INPROJ = 1024
TM_ATTN = 256
TM_DH1 = 512
TM_DW = 512
TK_DW = 4096
TR_ELEMWISE = 256


def _dot(a, b):
    return lax.dot_general(a, b, (((1,), (0,)), ((), ())), preferred_element_type=F32)


def _dot_nt(a, b):
    return lax.dot_general(a, b, (((1,), (1,)), ((), ())), preferred_element_type=F32)


def _rows8(v):
    return jnp.sum(v.reshape(v.shape[0] // 8, 8, v.shape[1]), axis=0)


def _gelu(x):
    x2 = x * x
    t = jnp.tanh(GELU_K0 * (x + GELU_K1 * (x * x2)))
    g = 0.5 * x * (1.0 + t)
    dg = 0.5 * (1.0 + t) + 0.5 * x * (1.0 - t * t) * (GELU_K0 * (1.0 + 3.0 * GELU_K1 * x2))
    return g, dg


def _silu(z):
    s = jax.nn.sigmoid(z)
    return z * s, s * (1.0 + z * (1.0 - s))


def _rms(v):
    r = lax.rsqrt(jnp.mean(v * v, axis=-1, keepdims=True) + EPS)
    return v * r, r


def _rms_bwd(dy_g, vh, r):
    return r * (dy_g - vh * jnp.mean(dy_g * vh, axis=-1, keepdims=True))


def _place():
    return lax.axis_index("x"), lax.axis_index("y"), lax.axis_index("c")


def _other_chips(x, y):
    return [(1 - x, y), (x, 1 - y), (1 - x, 1 - y)]


def _pow2_divisor(n):
    return n & (-n)


def _shard_view(ref, axis, size, d):
    start = d * size
    align = _pow2_divisor(size)
    if align > 1:
        start = pl.multiple_of(start, align)
    idx = [slice(None)] * len(ref.shape)
    idx[axis] = pl.ds(start, size)
    return ref.at[tuple(idx)]


class _Plan:
    def __init__(self, ins, out_shape, sems, start, finish, mid=None, aliases=None):
        self.ins, self.out_shape, self.sems = list(ins), list(out_shape), list(sems)
        self.start, self.mid, self.finish = start, mid, finish
        self.aliases = dict(aliases or {})


def _gather_plan(shards, specs):
    n = len(shards)
    n_copy = 7

    def copies(ins, outs, sems):
        send_sems, recv_sems, local_sems = sems
        x, y, c = _place()
        me, sibling = (x, y, c), (x, y, 1 - c)
        chips = _other_chips(x, y)

        def block(a, p):
            _, axis, size = specs[a]
            return _shard_view(outs[a], axis, size, 4 * p[0] + 2 * p[1] + p[2])

        def copy(a, k, p, to, src=None):
            return pltpu.make_async_remote_copy(
                src_ref=block(a, p) if src is None else src,
                dst_ref=block(a, p),
                send_sem=send_sems.at[a * n_copy + k],
                recv_sem=recv_sems.at[a * n_copy + k],
                device_id=to,
                device_id_type=MESH,
            )

        def mine():
            return [pltpu.make_async_copy(ins[a], block(a, me), local_sems.at[a]) for a in range(n)]

        def first():
            return [cp for a in range(n) for cp in
                    [copy(a, 0, me, sibling, src=ins[a])] + [copy(a, 1 + j, me, (*chip, c), src=ins[a]) for j, chip in enumerate(chips)]]

        def landed():
            return [copy(a, 1 + j, (*chip, c), me) for j, chip in enumerate(chips) for a in range(n)]

        def passed():
            return [copy(a, 4 + j, (*chip, c), sibling) for j, chip in enumerate(chips) for a in range(n)]

        def last():
            return [cp for a in range(n) for cp in
                    [copy(a, 0, sibling, me)] + [copy(a, 4 + j, (*chip, 1 - c), me) for j, chip in enumerate(chips)]]

        return mine, first, landed, passed, last

    def start(ins, outs, sems):
        mine, first, _, _, _ = copies(ins, outs, sems)
        for cp in mine() + first():
            cp.start()

    def mid(ins, outs, sems):
        _, _, landed, passed, _ = copies(ins, outs, sems)
        for got, fwd in zip(landed(), passed()):
            got.wait_recv()
            fwd.start()

    def finish(ins, outs, sems):
        mine, first, _, passed, last = copies(ins, outs, sems)
        for cp in last():
            cp.wait_recv()
        for cp in first() + passed():
            cp.wait_send()
        for cp in mine():
            cp.wait()

    return _Plan(
        shards,
        [jax.ShapeDtypeStruct(full, s.dtype) for s, (full, _, _) in zip(shards, specs)],
        [pltpu.SemaphoreType.DMA((n * n_copy,)), pltpu.SemaphoreType.DMA((n * n_copy,)), pltpu.SemaphoreType.DMA((n,))],
        start, finish, mid,
    )


def _sibling_plan(grads, specs):
    n = len(grads)

    def shard_shape(a):
        axis, size = specs[a]
        shp = list(grads[a].shape)
        shp[axis] = size
        return tuple(shp)

    def copies(ins, outs, sems):
        send_sems, recv_sems = sems
        x, y, c = _place()
        return [
            pltpu.make_async_remote_copy(
                src_ref=_shard_view(ins[a], specs[a][0], specs[a][1], 2 * q + (1 - c)),
                dst_ref=outs[a].at[q],
                send_sem=send_sems.at[a * N_CHIP + q],
                recv_sem=recv_sems.at[a * N_CHIP + q],
                device_id=(x, y, 1 - c),
                device_id_type=MESH,
            )
            for a in range(n) for q in range(N_CHIP)
        ]

    def start(ins, outs, sems):
        for cp in copies(ins, outs, sems):
            cp.start()

    def finish(ins, outs, sems):
        for cp in copies(ins, outs, sems):
            cp.wait()

    return _Plan(
        grads,
        [jax.ShapeDtypeStruct((N_CHIP, *shard_shape(a)), grads[a].dtype) for a in range(n)],
        [pltpu.SemaphoreType.DMA((n * N_CHIP,)), pltpu.SemaphoreType.DMA((n * N_CHIP,))],
        start, finish,
    )


def _chip_plan(parts, full_rows=None, row0=None, into=None):
    n = len(parts)
    full_rows = [p.shape[1] for p in parts] if full_rows is None else full_rows
    row0 = [0] * n if row0 is None else row0

    def copies(ins, outs, sems):
        send_sems, recv_sems, local_sems = sems
        x, y, c = _place()
        my_chip = 2 * x + y
        chips = _other_chips(x, y)

        def land(a, q):
            return outs[a].at[q, pl.ds(row0[a], parts[a].shape[1]), :]

        def mine():
            return [pltpu.make_async_copy(ins[a].at[my_chip], land(a, my_chip), local_sems.at[a]) for a in range(n)]

        def remote(sending):
            return [
                pltpu.make_async_remote_copy(
                    src_ref=ins[a].at[2 * chip[0] + chip[1]],
                    dst_ref=land(a, my_chip if sending else 2 * chip[0] + chip[1]),
                    send_sem=send_sems.at[a * 3 + j], recv_sem=recv_sems.at[a * 3 + j],
                    device_id=(*chip, c), device_id_type=MESH)
                for a in range(n) for j, chip in enumerate(chips)
            ]

        return mine, remote

    def start(ins, outs, sems):
        mine, remote = copies(ins, outs, sems)
        for cp in mine() + remote(True):
            cp.start()

    def finish(ins, outs, sems):
        mine, remote = copies(ins, outs, sems)
        for cp in remote(False):
            cp.wait_recv()
        for cp in remote(True):
            cp.wait_send()
        for cp in mine():
            cp.wait()

    return _Plan(
        list(parts) + (list(into) if into is not None else []),
        [jax.ShapeDtypeStruct((N_CHIP, full_rows[a], parts[a].shape[2]), parts[a].dtype) for a in range(n)],
        [pltpu.SemaphoreType.DMA((n * 3,)), pltpu.SemaphoreType.DMA((n * 3,)), pltpu.SemaphoreType.DMA((n,))],
        start, finish,
        aliases={n + a: a for a in range(n)} if into is not None else None,
    )


def _call(body, *, name, operands, out_shape, grid=(), in_specs=None, out_specs=None, scratch_shapes=(),
          plans=(), aliases=None):
    operands, out_shape, scratch_shapes = list(operands), list(out_shape), list(scratch_shapes)
    n_in, n_out, n_scr = len(operands), len(out_shape), len(scratch_shapes)
    in_vmem = pl.BlockSpec(memory_space=pltpu.VMEM)
    in_hbm = pl.BlockSpec(memory_space=pl.ANY)
    in_specs = [in_vmem] * n_in if in_specs is None else list(in_specs)
    out_specs = [in_vmem] * n_out if out_specs is None else list(out_specs)
    io_alias = dict(aliases or {})
    pos_in, pos_out = n_in, n_out
    for p in plans:
        for i, o in p.aliases.items():
            io_alias[pos_in + i] = pos_out + o
        pos_in += len(p.ins)
        pos_out += len(p.out_shape)
    steps = math.prod(grid) if grid else 1
    mid_step = (steps * 3) // 5

    def wrapped(*refs):
        refs = list(refs)
        take = lambda k: [refs.pop(0) for _ in range(k)]
        ins, p_ins = take(n_in), [take(len(p.ins)) for p in plans]
        outs, p_outs = take(n_out), [take(len(p.out_shape)) for p in plans]
        scr, p_sems = take(n_scr), [take(len(p.sems)) for p in plans]
        step = 0
        for ax, g in enumerate(grid):
            step = step * g + pl.program_id(ax)

        def hook(kind, at):
            todo = [(getattr(p, kind), a, b, c) for p, a, b, c in zip(plans, p_ins, p_outs, p_sems) if getattr(p, kind)]

            def run():
                for fn, a, b, c in todo:
                    fn(a, b, c)

            if todo and grid:
                pl.when(step == at)(run)
            elif todo:
                run()

        hook("start", 0)
        if body is not None:
            body(*ins, *outs, *scr)
        hook("mid", mid_step)
        hook("finish", steps - 1)

    res = pl.pallas_call(
        wrapped,
        name=name,
        grid=grid,
        in_specs=in_specs + [in_hbm] * sum(len(p.ins) for p in plans),
        out_specs=out_specs + [in_hbm] * sum(len(p.out_shape) for p in plans),
        out_shape=out_shape + [o for p in plans for o in p.out_shape],
        scratch_shapes=scratch_shapes + [s for p in plans for s in p.sems],
        input_output_aliases=io_alias,
        compiler_params=pltpu.CompilerParams(dimension_semantics=("arbitrary",) * len(grid)) if grid else None,
    )(*operands, *[a for p in plans for a in p.ins])
    res = list(res)
    outs, plan_outs = res[:n_out], []
    pos = n_out
    for p in plans:
        plan_outs.append(res[pos : pos + len(p.out_shape)])
        pos += len(p.out_shape)
    return outs, plan_outs


def _inproj(x, g, w_in, plans=()):
    s = x.shape[0]
    tm = min(TM_INPROJ, s)

    def body(x_ref, g_ref, w_ref, proj_ref, h1_ref, h1t_ref):
        @pl.when(pl.program_id(1) == 0)
        def _():
            xh, _ = _rms(x_ref[...])
            h = xh * g_ref[...]
            h1_ref[...] = h.astype(BF16)
            h1t_ref[...] = h.T.astype(BF16)

        proj_ref[...] = _dot(h1_ref[...], w_ref[...]).astype(BF16)

    return _call(
        body,
        name="inproj",
        operands=[x, g, w_in],
        grid=(s // tm, GROUPS),
        in_specs=[
            pl.BlockSpec((tm, D_MODEL), lambda i, j: (i, 0)),
            pl.BlockSpec((1, D_MODEL), lambda i, j: (0, 0)),
            pl.BlockSpec((D_MODEL, D_MODEL), lambda i, j: (0, j)),
        ],
        out_specs=[
            pl.BlockSpec((tm, D_MODEL), lambda i, j: (i, j)),
            pl.BlockSpec((tm, D_MODEL), lambda i, j: (i, 0)),
            pl.BlockSpec((D_MODEL, tm), lambda i, j: (0, i)),
        ],
        out_shape=[
            jax.ShapeDtypeStruct((s, IN_DIM), BF16),
            jax.ShapeDtypeStruct((s, D_MODEL), BF16),
            jax.ShapeDtypeStruct((D_MODEL, s), BF16),
        ],
        plans=plans,
    )


def _causal_mask():
    row = lax.broadcasted_iota(jnp.int32, (CHUNK, CHUNK), 0)
    col = lax.broadcasted_iota(jnp.int32, (CHUNK, CHUNK), 1)
    return row, row >= col


def _mixer_fwd(proj, conv_w, ln_g, ln_b, ws, bs_b):
    s = proj.shape[0]
    nt = s // CHUNK

    def body(proj_ref, cw_ref, lng_ref, lnb_ref, ws_ref, bsb_ref, cat_ref, pcar_ref):
        @pl.when(pl.program_id(0) == 0)
        def _():
            pcar_ref[...] = jnp.zeros_like(pcar_ref)

        row, tril = _causal_mask()
        for j in range(8):
            cs = slice(CHUNK * j, CHUNK * (j + 1))

            def grp(k):
                return proj_ref[:, k * D_MODEL + CHUNK * j : k * D_MODEL + CHUNK * (j + 1)].astype(F32)

            w = cw_ref[:, cs]
            p = grp(1) * grp(2)
            pc = pcar_ref[:, cs]
            p1 = jnp.where(row == 0, pc[7:8], pltpu.roll(p, 1, 0))
            p2 = jnp.where(row == 0, pc[6:7], jnp.where(row == 1, pc[7:8], pltpu.roll(p, 2, 0)))
            pcar_ref[:, cs] = p[CHUNK - 8 :]
            cv = w[0:1] * p2 + w[1:2] * p1 + w[2:3] * p
            sa, _ = _silu(grp(3))
            cat_ref[:, cs] = ((grp(0) * cv) * sa).astype(BF16)
            gu, _ = _gelu(grp(4))
            gv, _ = _gelu(grp(5))
            dv = gv - jnp.mean(gv, axis=-1, keepdims=True)
            vn = dv * lax.rsqrt(jnp.mean(dv * dv, axis=-1, keepdims=True) + EPS)
            vn = vn * lng_ref[:, cs] + lnb_ref[:, cs]
            wc = jnp.where(tril, ws_ref[j], 0.0).astype(BF16)
            sp = _dot(wc, vn.astype(BF16)) + bsb_ref[j]
            sb, _ = _silu(grp(6))
            cat_ref[:, D_MODEL + CHUNK * j : D_MODEL + CHUNK * (j + 1)] = ((gu * sp) * sb).astype(BF16)

    whole = lambda shape: pl.BlockSpec(shape, lambda i: (0,) * len(shape))
    return pl.pallas_call(
        body,
        name="mixer_fwd",
        grid=(nt,),
        in_specs=[
            pl.BlockSpec((CHUNK, IN_DIM), lambda i: (i, 0)),
            whole((8, D_MODEL)),
            whole((1, D_MODEL)),
            whole((1, D_MODEL)),
            whole((8, CHUNK, CHUNK)),
            whole((8, CHUNK, CHUNK)),
        ],
        out_specs=pl.BlockSpec((CHUNK, MIX_DIM), lambda i: (i, 0)),
        out_shape=jax.ShapeDtypeStruct((s, MIX_DIM), BF16),
        scratch_shapes=[pltpu.VMEM((8, D_MODEL), F32)],
        compiler_params=pltpu.CompilerParams(dimension_semantics=("arbitrary",)),
    )(proj, conv_w, ln_g, ln_b, ws, bs_b)


def _mixer_bwd(proj, dcat, conv_w, ln_g, ln_b, ws, bs_b, plans=()):
    s = proj.shape[0]
    nt = s // CHUNK

    def body(proj_ref, halo_ref, dcat_ref, cw_ref, lng_ref, lnb_ref, ws_ref, bsb_ref,
             dproj_ref, dcw_ref, dlng_ref, dlnb_ref, dws_ref, dbs_ref,
             car_ref, acc_cw, acc_lng, acc_lnb, acc_bs):
        i = pl.program_id(0)
        tile = nt - 1 - i

        @pl.when(i == 0)
        def _():
            car_ref[...] = jnp.zeros_like(car_ref)
            acc_cw[...] = jnp.zeros_like(acc_cw)
            acc_lng[...] = jnp.zeros_like(acc_lng)
            acc_lnb[...] = jnp.zeros_like(acc_lnb)
            acc_bs[...] = jnp.zeros_like(acc_bs)
            dws_ref[...] = jnp.zeros_like(dws_ref)

        row, tril = _causal_mask()
        has_prev = jnp.where(tile > 0, 1.0, 0.0).astype(F32)
        for j in range(8):
            cs = slice(CHUNK * j, CHUNK * (j + 1))

            def col(k):
                return slice(k * D_MODEL + CHUNK * j, k * D_MODEL + CHUNK * (j + 1))

            def grp(k):
                return proj_ref[:, col(k)].astype(F32)

            w = cw_ref[:, cs]
            gb, gc, xa = grp(0), grp(1), grp(2)
            p = gc * xa
            pprev = halo_ref[:, col(1)].astype(F32) * halo_ref[:, col(2)].astype(F32) * has_prev
            p1 = jnp.where(row == 0, pprev[15:16], pltpu.roll(p, 1, 0))
            p2 = jnp.where(row == 0, pprev[14:15], jnp.where(row == 1, pprev[15:16], pltpu.roll(p, 2, 0)))
            cv = w[0:1] * p2 + w[1:2] * p1 + w[2:3] * p
            za = grp(3)
            sa, dsa = _silu(za)
            da = dcat_ref[:, cs].astype(F32)
            dproj_ref[:, col(0)] = (da * cv * sa).astype(BF16)
            dproj_ref[:, col(3)] = (da * gb * cv * dsa).astype(BF16)
            dcv = da * gb * sa
            nxt = car_ref[:, cs]
            d1 = jnp.where(row == CHUNK - 1, nxt[0:1], pltpu.roll(dcv, CHUNK - 1, 0))
            d2 = jnp.where(row == CHUNK - 2, nxt[0:1], jnp.where(row == CHUNK - 1, nxt[1:2], pltpu.roll(dcv, CHUNK - 2, 0)))
            car_ref[:, cs] = dcv[0:8]
            dp = w[2:3] * dcv + w[1:2] * d1 + w[0:1] * d2
            dproj_ref[:, col(1)] = (dp * xa).astype(BF16)
            dproj_ref[:, col(2)] = (dp * gc).astype(BF16)
            acc_cw[0, :, cs] += _rows8(dcv * p2)
            acc_cw[1, :, cs] += _rows8(dcv * p1)
            acc_cw[2, :, cs] += _rows8(dcv * p)
            gu, dgu = _gelu(grp(4))
            gv, dgv = _gelu(grp(5))
            dv = gv - jnp.mean(gv, axis=-1, keepdims=True)
            rstd = lax.rsqrt(jnp.mean(dv * dv, axis=-1, keepdims=True) + EPS)
            vnh = dv * rstd
            lng = lng_ref[:, cs]
            vnb = (vnh * lng + lnb_ref[:, cs]).astype(BF16)
            wcf = jnp.where(tril, ws_ref[j], 0.0)
            sp = _dot(wcf.astype(BF16), vnb) + bsb_ref[j]
            zb = grp(6)
            sb, dsb = _silu(zb)
            db = dcat_ref[:, D_MODEL + CHUNK * j : D_MODEL + CHUNK * (j + 1)].astype(F32)
            dproj_ref[:, col(4)] = (db * sp * sb * dgu).astype(BF16)
            dproj_ref[:, col(6)] = (db * gu * sp * dsb).astype(BF16)
            dsp = db * gu * sb
            acc_bs[j] += dsp
            dspb = dsp.astype(BF16)
            dws_ref[j] += _dot_nt(dspb, vnb)
            dvn = _dot(wcf.T.astype(BF16), dspb)
            acc_lnb[:, cs] += _rows8(dvn)
            acc_lng[:, cs] += _rows8(dvn * vnh)
            dvh = dvn * lng
            dgvv = rstd * (dvh - jnp.mean(dvh, axis=-1, keepdims=True) - vnh * jnp.mean(dvh * vnh, axis=-1, keepdims=True))
            dproj_ref[:, col(5)] = (dgvv * dgv).astype(BF16)

        @pl.when(i == nt - 1)
        def _():
            dlng_ref[...] = jnp.sum(acc_lng[...], axis=0, keepdims=True)
            dlnb_ref[...] = jnp.sum(acc_lnb[...], axis=0, keepdims=True)
            dcw_ref[...] = jnp.zeros_like(dcw_ref)
            for k in range(3):
                dcw_ref[k : k + 1, :] = jnp.sum(acc_cw[k], axis=0, keepdims=True)
            for j in range(8):
                dws_ref[j] = jnp.where(tril, dws_ref[j], 0.0)
                dbs_ref[:, CHUNK * j : CHUNK * (j + 1)] = jnp.sum(acc_bs[j].T, axis=0, keepdims=True)

    whole = lambda shape: pl.BlockSpec(shape, lambda i: (0,) * len(shape))
    halo_rows = 16
    per = CHUNK // halo_rows
    return _call(
        body,
        name="mixer_bwd",
        operands=[proj, proj, dcat, conv_w, ln_g, ln_b, ws, bs_b],
        plans=plans,
        grid=(nt,),
        in_specs=[
            pl.BlockSpec((CHUNK, IN_DIM), lambda i: (nt - 1 - i, 0)),
            pl.BlockSpec((halo_rows, IN_DIM), lambda i: (jnp.maximum((nt - 1 - i) * per - 1, 0), 0)),
            pl.BlockSpec((CHUNK, MIX_DIM), lambda i: (nt - 1 - i, 0)),
            whole((8, D_MODEL)),
            whole((1, D_MODEL)),
            whole((1, D_MODEL)),
            whole((8, CHUNK, CHUNK)),
            whole((8, CHUNK, CHUNK)),
        ],
        out_specs=[
            pl.BlockSpec((CHUNK, IN_DIM), lambda i: (nt - 1 - i, 0)),
            whole((8, D_MODEL)),
            whole((1, D_MODEL)),
            whole((1, D_MODEL)),
            whole((8, CHUNK, CHUNK)),
            whole((1, D_MODEL)),
        ],
        out_shape=[
            jax.ShapeDtypeStruct((s, IN_DIM), BF16),
            jax.ShapeDtypeStruct((8, D_MODEL), F32),
            jax.ShapeDtypeStruct((1, D_MODEL), F32),
            jax.ShapeDtypeStruct((1, D_MODEL), F32),
            jax.ShapeDtypeStruct((8, CHUNK, CHUNK), F32),
            jax.ShapeDtypeStruct((1, D_MODEL), F32),
        ],
        scratch_shapes=[
            pltpu.VMEM((8, D_MODEL), F32),
            pltpu.VMEM((3, 8, D_MODEL), F32),
            pltpu.VMEM((8, D_MODEL), F32),
            pltpu.VMEM((8, D_MODEL), F32),
            pltpu.VMEM((8, CHUNK, CHUNK), F32),
        ],
    )


def _kv_fwd(mem, g_mem, w_kv):
    def body(mem_ref, g_ref, w_ref, k_ref, vv_ref, mt_ref):
        mh, _ = _rms(mem_ref[...])
        m = mh * g_ref[...]
        kv = _dot(m.astype(BF16), w_ref[...])
        k_ref[...] = kv[:, :D_MODEL].astype(BF16)
        vv_ref[...] = kv[:, D_MODEL:].astype(BF16)
        mt_ref[...] = m.T.astype(BF16)

    return pl.pallas_call(
        body,
        name="kv_fwd",
        out_shape=[
            jax.ShapeDtypeStruct((MEM_LEN, D_MODEL), BF16),
            jax.ShapeDtypeStruct((MEM_LEN, D_MODEL), BF16),
            jax.ShapeDtypeStruct((D_MODEL, MEM_LEN), BF16),
        ],
    )(mem, g_mem, w_kv)


def _kv_bwd(mem, w_kv, mt, dk, dvv):
    def body(mem_ref, w_ref, mt_ref, dk_ref, dvv_ref, dw_ref, dg_ref):
        dkv = jnp.concatenate([dk_ref[...], dvv_ref[...]], axis=1).astype(BF16)
        dw_ref[...] = _dot(mt_ref[...], dkv)
        dm = _dot_nt(dkv, w_ref[...])
        mh, _ = _rms(mem_ref[...])
        dg_ref[...] = jnp.sum(dm * mh, axis=0, keepdims=True)

    return pl.pallas_call(
        body,
        name="kv_bwd",
        out_shape=[
            jax.ShapeDtypeStruct((D_MODEL, 2 * D_MODEL), F32),
            jax.ShapeDtypeStruct((1, D_MODEL), F32),
        ],
    )(mem, w_kv, mt, dk, dvv)


def _attn(x, cat, target, w_out, w_q, w_xo, k, vv, g_x, g_f):
    s = x.shape[0]
    tm = min(TM_ATTN, s)
    nt = s // tm
    scale = 1.0 / math.sqrt(X_HEAD_DIM)

    def body(x_ref, cat_ref, t_ref, wout_ref, wq_ref, wxo_ref, k_ref, vv_ref, gx_ref, gf_ref,
             dx1_ref, dcat_ref, catt_ref, h2t_ref, ot_ref, dx2b_ref, dqb_ref, dx1b_ref,
             dk_ref, dvv_ref, dgf_ref, dgx_ref, loss_ref,
             p_scr, q_scr, o_scr, dq_scr, acc_gf, acc_gx, acc_loss):
        i = pl.program_id(0)

        @pl.when(i == 0)
        def _():
            dk_ref[...] = jnp.zeros_like(dk_ref)
            dvv_ref[...] = jnp.zeros_like(dvv_ref)
            acc_gf[...] = jnp.zeros_like(acc_gf)
            acc_gx[...] = jnp.zeros_like(acc_gx)
            acc_loss[...] = jnp.zeros_like(acc_loss)

        cat = cat_ref[...]
        catt_ref[...] = cat.astype(F32).T.astype(BF16)
        x1 = x_ref[...] + _dot(cat, wout_ref[...])
        x1h, r2 = _rms(x1)
        h2 = x1h * gx_ref[...]
        h2b = h2.astype(BF16)
        h2t_ref[...] = h2.T.astype(BF16)
        q_scr[...] = _dot(h2b, wq_ref[...]).astype(BF16)
        for h in range(X_HEADS):
            hs = slice(X_HEAD_DIM * h, X_HEAD_DIM * (h + 1))
            sc = _dot_nt(q_scr[:, hs], k_ref[:, hs]) * scale
            e = jnp.exp(sc - jnp.max(sc, axis=-1, keepdims=True))
            p = e / jnp.sum(e, axis=-1, keepdims=True)
            p_scr[:, hs] = p
            o_scr[:, hs] = _dot(p.astype(BF16), vv_ref[:, hs])
        o = o_scr[...]
        ot_ref[...] = o.T.astype(BF16)
        x2 = x1 + _dot(o.astype(BF16), wxo_ref[...])
        x2h, r3 = _rms(x2)
        gf = gf_ref[...]
        err = x2h * gf - t_ref[...]
        acc_loss[...] += _rows8(err * err)
        dy = err * (1.0 / D_MODEL)
        acc_gf[...] += _rows8(dy * x2h)
        dx2 = _rms_bwd(dy * gf, x2h, r3)
        dx2b = dx2.astype(BF16)
        dx2b_ref[...] = dx2b
        dob = _dot_nt(dx2b, wxo_ref[...]).astype(BF16)
        for h in range(X_HEADS):
            hs = slice(X_HEAD_DIM * h, X_HEAD_DIM * (h + 1))
            p = p_scr[:, hs]
            dp = _dot_nt(dob[:, hs], vv_ref[:, hs])
            ds = p * (dp - jnp.sum(dp * p, axis=-1, keepdims=True)) * scale
            dsb = ds.astype(BF16)
            dq_scr[:, hs] = _dot(dsb, k_ref[:, hs]).astype(BF16)
            dvv_ref[:, hs] += _dot(p.T.astype(BF16), dob[:, hs])
            dk_ref[:, hs] += _dot(ds.T.astype(BF16), q_scr[:, hs])
        dqb = dq_scr[...]
        dqb_ref[...] = dqb
        dh2 = _dot_nt(dqb, wq_ref[...])
        acc_gx[...] += _rows8(dh2 * x1h)
        dx1 = dx2 + _rms_bwd(dh2 * gx_ref[...], x1h, r2)
        dx1_ref[...] = dx1
        dx1b = dx1.astype(BF16)
        dx1b_ref[...] = dx1b
        dcat_ref[...] = _dot_nt(dx1b, wout_ref[...]).astype(BF16)

        @pl.when(i == nt - 1)
        def _():
            dgf_ref[...] = jnp.sum(acc_gf[...], axis=0, keepdims=True)
            dgx_ref[...] = jnp.sum(acc_gx[...], axis=0, keepdims=True)
            loss_ref[...] = jnp.sum(acc_loss[...], axis=0, keepdims=True)

    whole = lambda shape: pl.BlockSpec(shape, lambda i: (0,) * len(shape))
    rows = lambda width: pl.BlockSpec((tm, width), lambda i: (i, 0))
    cols = lambda height: pl.BlockSpec((height, tm), lambda i: (0, i))
    vec = jax.ShapeDtypeStruct((1, D_MODEL), F32)
    return pl.pallas_call(
        body,
        name="attn",
        grid=(nt,),
        in_specs=[
            rows(D_MODEL), rows(MIX_DIM), rows(D_MODEL),
            whole((MIX_DIM, D_MODEL)), whole((D_MODEL, D_MODEL)), whole((D_MODEL, D_MODEL)),
            whole((MEM_LEN, D_MODEL)), whole((MEM_LEN, D_MODEL)),
            whole((1, D_MODEL)), whole((1, D_MODEL)),
        ],
        out_specs=[
            rows(D_MODEL), rows(MIX_DIM), cols(MIX_DIM), cols(D_MODEL), cols(D_MODEL),
            rows(D_MODEL), rows(D_MODEL), rows(D_MODEL),
            whole((MEM_LEN, D_MODEL)), whole((MEM_LEN, D_MODEL)),
            whole((1, D_MODEL)), whole((1, D_MODEL)), whole((1, D_MODEL)),
        ],
        out_shape=[
            jax.ShapeDtypeStruct((s, D_MODEL), F32),
            jax.ShapeDtypeStruct((s, MIX_DIM), BF16),
            jax.ShapeDtypeStruct((MIX_DIM, s), BF16),
            jax.ShapeDtypeStruct((D_MODEL, s), BF16),
            jax.ShapeDtypeStruct((D_MODEL, s), BF16),
            jax.ShapeDtypeStruct((s, D_MODEL), BF16),
            jax.ShapeDtypeStruct((s, D_MODEL), BF16),
            jax.ShapeDtypeStruct((s, D_MODEL), BF16),
            jax.ShapeDtypeStruct((MEM_LEN, D_MODEL), F32),
            jax.ShapeDtypeStruct((MEM_LEN, D_MODEL), F32),
            vec, vec, vec,
        ],
        scratch_shapes=[
            pltpu.VMEM((tm, D_MODEL), F32),
            pltpu.VMEM((tm, D_MODEL), BF16),
            pltpu.VMEM((tm, D_MODEL), F32),
            pltpu.VMEM((tm, D_MODEL), BF16),
            pltpu.VMEM((8, D_MODEL), F32),
            pltpu.VMEM((8, D_MODEL), F32),
            pltpu.VMEM((8, D_MODEL), F32),
        ],
        compiler_params=pltpu.CompilerParams(dimension_semantics=("arbitrary",)),
    )(x, cat, target, w_out, w_q, w_xo, k, vv, g_x, g_f)


def _matmul_f32(name, a, b, rows=None, plans=()):
    first, m = (0, a.shape[0]) if rows is None else rows
    kk = a.shape[1]
    n = b.shape[1]
    tm, tn, tk = min(m, TM_DW), min(n, D_MODEL), min(kk, TK_DW)
    nk = kk // tk
    i0 = first // tm

    def body(a_ref, b_ref, o_ref):
        prod = _dot(a_ref[...], b_ref[...])
        if nk == 1:
            o_ref[...] = prod
            return
        kid = pl.program_id(2)

        @pl.when(kid == 0)
        def _():
            o_ref[...] = prod

        @pl.when(kid > 0)
        def _():
            o_ref[...] += prod

    (out,), plan_outs = _call(
        body,
        name=name,
        operands=[a, b],
        grid=(m // tm, n // tn, nk),
        in_specs=[pl.BlockSpec((tm, tk), lambda i, j, k: (i + i0, k)), pl.BlockSpec((tk, tn), lambda i, j, k: (k, j))],
        out_specs=[pl.BlockSpec((tm, tn), lambda i, j, k: (i, j))],
        out_shape=[jax.ShapeDtypeStruct((m, n), F32)],
        plans=plans,
    )
    return out, plan_outs


def _dh1(name, dproj, w_in, x, dx1, g, tiles, into=None, plans=()):
    s = x.shape[0]
    tm = min(TM_DH1, s)
    t0, nt = tiles[0], tiles[1] - tiles[0]

    def body(dp_ref, w_ref, x_ref, dx1_ref, g_ref, *rest):
        gx_ref, dg_ref, acc_g = rest[-3:]
        i = pl.program_id(0)

        @pl.when(i == 0)
        def _():
            acc_g[...] = jnp.zeros_like(acc_g)

        dh1 = _dot_nt(dp_ref[...], w_ref[...])
        xh, r = _rms(x_ref[...])
        acc_g[...] += _rows8(dh1 * xh)
        gx_ref[...] = dx1_ref[...] + _rms_bwd(dh1 * g_ref[...], xh, r)

        @pl.when(i == nt - 1)
        def _():
            dg_ref[...] = jnp.sum(acc_g[...], axis=0, keepdims=True)

    rows = pl.BlockSpec((tm, D_MODEL), lambda i: (i + t0, 0))
    return _call(
        body,
        name=name,
        operands=[dproj, w_in, x, dx1, g] + ([into] if into is not None else []),
        grid=(nt,),
        in_specs=[
            pl.BlockSpec((tm, IN_DIM), lambda i: (i + t0, 0)),
            pl.BlockSpec((D_MODEL, IN_DIM), lambda i: (0, 0), pipeline_mode=pl.Buffered(1)),
            rows,
            rows,
            pl.BlockSpec((1, D_MODEL), lambda i: (0, 0)),
        ] + ([pl.BlockSpec(memory_space=pl.ANY)] if into is not None else []),
        out_specs=[rows, pl.BlockSpec((1, D_MODEL), lambda i: (0, 0))],
        out_shape=[jax.ShapeDtypeStruct((s, D_MODEL), F32), jax.ShapeDtypeStruct((1, D_MODEL), F32)],
        scratch_shapes=[pltpu.VMEM((8, D_MODEL), F32)],
        plans=plans,
        aliases={5: 0} if into is not None else None,
    )


def _pair_sum(name, grad, got, axis, size, core):
    shard = list(grad.shape)
    shard[axis] = size
    r, cdim = shard
    tr = min(r, TR_ELEMWISE)
    nr = r // tr

    def body(core_ref, g_ref, got_ref, o_ref):
        o_ref[0] = (g_ref[...] + got_ref[0]).astype(BF16)

    if axis == 1:
        g_map = lambda q, t, core_ref: (t, 2 * q + core_ref[0])
    else:
        g_map = lambda q, t, core_ref: ((2 * q + core_ref[0]) * nr + t, 0)
    return pl.pallas_call(
        body,
        name=name,
        grid_spec=pltpu.PrefetchScalarGridSpec(
            num_scalar_prefetch=1,
            grid=(N_CHIP, nr),
            in_specs=[
                pl.BlockSpec((tr, cdim), g_map),
                pl.BlockSpec((1, tr, cdim), lambda q, t, core_ref: (q, t, 0)),
            ],
            out_specs=pl.BlockSpec((1, tr, cdim), lambda q, t, core_ref: (q, t, 0)),
        ),
        out_shape=jax.ShapeDtypeStruct((N_CHIP, r, cdim), BF16),
        compiler_params=pltpu.CompilerParams(dimension_semantics=("arbitrary", "arbitrary")),
    )(core, grad, got)


def _adamw(name, w, m, v, parts, plans=()):
    r, cdim = w.shape
    n_parts = parts.shape[0]
    tr = min(r, TR_ELEMWISE)
    c1 = 1.0 / (1.0 - ADAM_B1**ADAM_STEP)
    c2 = 1.0 / (1.0 - ADAM_B2**ADAM_STEP)

    def body(w_ref, m_ref, v_ref, p_ref, g_ref, d_ref, nm_ref, nv_ref):
        g = p_ref[0].astype(F32)
        for q in range(1, n_parts):
            g = g + p_ref[q].astype(F32)
        m_new = ADAM_B1 * m_ref[...] + (1.0 - ADAM_B1) * g
        v_new = ADAM_B2 * v_ref[...] + (1.0 - ADAM_B2) * (g * g)
        g_ref[...] = g
        nm_ref[...] = m_new
        nv_ref[...] = v_new
        d_ref[...] = -ADAM_LR * ((m_new * c1) / (jnp.sqrt(v_new * c2) + ADAM_EPS) + ADAM_WD * w_ref[...])

    blk = pl.BlockSpec((tr, cdim), lambda t: (t, 0))
    out = jax.ShapeDtypeStruct((r, cdim), F32)
    return _call(
        body,
        name=name,
        operands=[w, m, v, parts],
        grid=(r // tr,),
        in_specs=[blk, blk, blk, pl.BlockSpec((n_parts, tr, cdim), lambda t: (0, t, 0))],
        out_specs=[blk, blk, blk, blk],
        out_shape=[out, out, out, out],
        plans=plans,
    )


def kernel(x, mem, norm_mix_g, w_in, conv_w, gm_ln_g, gm_ln_b, gm_ws, gm_bs, w_out, norm_x_g, norm_mem_g, w_q, w_kv, w_xo, norm_final_g, loss_target, m_norm_mix_g, m_w_in, m_conv_w, m_gm_ln_g, m_gm_ln_b, m_gm_ws, m_gm_bs, m_w_out, m_norm_x_g, m_norm_mem_g, m_w_q, m_w_kv, m_w_xo, m_norm_final_g, v_norm_mix_g, v_w_in, v_conv_w, v_gm_ln_g, v_gm_ln_b, v_gm_ws, v_gm_bs, v_w_out, v_norm_x_g, v_norm_mem_g, v_w_q, v_w_kv, v_w_xo, v_norm_final_g):
    s = x.shape[1]
    dev = 4 * lax.axis_index("x") + 2 * lax.axis_index("y") + lax.axis_index("c")
    core = lax.axis_index("c").astype(jnp.int32).reshape(1)
    x2 = x[0]
    target = loss_target[0]
    pad_taps = lambda t: jnp.pad(t[0], ((0, 5), (0, 0)))

    big_names = ["w_in", "w_out", "w_q", "w_kv", "w_xo"]
    big_w = dict(w_in=w_in[0], w_out=w_out[0], w_q=w_q[0], w_kv=w_kv[0], w_xo=w_xo[0])
    big_m = dict(w_in=m_w_in[0], w_out=m_w_out[0], w_q=m_w_q[0], w_kv=m_w_kv[0], w_xo=m_w_xo[0])
    big_v = dict(w_in=v_w_in[0], w_out=v_w_out[0], w_q=v_w_q[0], w_kv=v_w_kv[0], w_xo=v_w_xo[0])
    shard_axis = dict(w_in=1, w_out=0, w_q=0, w_kv=1, w_xo=0)
    full_shape = dict(w_in=(D_MODEL, IN_DIM), w_out=(MIX_DIM, D_MODEL), w_q=(D_MODEL, D_MODEL),
                      w_kv=(D_MODEL, 2 * D_MODEL), w_xo=(D_MODEL, D_MODEL))
    shard_size = {k: big_w[k].shape[shard_axis[k]] for k in big_names}
    others = big_names[1:]
    gather_spec = lambda k: (full_shape[k], shard_axis[k], shard_size[k])
    reduce_spec = lambda k: (shard_axis[k], shard_size[k])
    _, ((w_in_full, conv_g),) = _call(
        None, name="weight_all_gather_in", operands=[], out_shape=[],
        plans=[_gather_plan([big_w["w_in"].astype(BF16), pad_taps(conv_w)[None]], [gather_spec("w_in"), ((N_DEV, 8, CHUNK), 0, 1)])])
    conv_full = conv_g.transpose(1, 0, 2).reshape(8, D_MODEL)

    ws = gm_ws[0]
    bs_b = jnp.broadcast_to(gm_bs[0][:, :, None], (8, CHUNK, CHUNK))
    g_f = norm_final_g.reshape(1, D_MODEL)

    (proj, h1, h1t), (gathered,) = _inproj(
        x2, norm_mix_g, w_in_full,
        plans=[_gather_plan([big_w[k].astype(BF16) for k in others], [gather_spec(k) for k in others])])
    wf = dict(zip(others, gathered))
    cat = _mixer_fwd(proj, conv_full, gm_ln_g, gm_ln_b, ws, bs_b)
    k, vv, mt = _kv_fwd(mem[0], norm_mem_g, wf["w_kv"])
    (dx1, dcat, catt, h2t, ot, dx2b, dqb, dx1b, dk, dvv, d_gf, d_gx, sq_err) = _attn(
        x2, cat, target, wf["w_out"], wf["w_q"], wf["w_xo"], k, vv, norm_x_g, g_f)

    def pair_sums(names, grads, got):
        return [_pair_sum("pair_sum_" + k, grads[k], got[i], *reduce_spec(k), core) for i, k in enumerate(names)]

    early, late = ["w_q", "w_xo"], ["w_out", "w_kv"]
    big_g = dict(w_q=_matmul_f32("dw_q", h2t, dqb)[0], w_xo=_matmul_f32("dw_xo", ot, dx2b)[0])
    big_g["w_out"], (got_early,) = _matmul_f32(
        "dw_out", catt, dx1b, plans=[_sibling_plan([big_g[k] for k in early], [reduce_spec(k) for k in early])])
    big_g["w_kv"], d_gmem = _kv_bwd(mem[0], wf["w_kv"], mt, dk, dvv)
    (dproj, d_cw, d_lng, d_lnb, d_ws, d_bs), (parts_early, got_late) = _mixer_bwd(
        proj, dcat, conv_full, gm_ln_g, gm_ln_b, ws, bs_b,
        plans=[_chip_plan(pair_sums(early, big_g, got_early)),
               _sibling_plan([big_g[k] for k in late], [reduce_spec(k) for k in late])])
    half = D_MODEL // 2
    in_spec = reduce_spec("w_in")
    dw_top, (parts_late,) = _matmul_f32("dw_in_top", h1t, dproj, rows=(0, half), plans=[_chip_plan(pair_sums(late, big_g, got_late))])
    parts = dict(zip(early + late, parts_early + parts_late))
    dw_bot, (got_top, (ws_all,)) = _matmul_f32(
        "dw_in_bot", h1t, dproj, rows=(half, half),
        plans=[_sibling_plan([dw_top], [in_spec]),
               _gather_plan([d_ws.reshape(1, 8 * CHUNK, CHUNK)], [((N_DEV, 8 * CHUNK, CHUNK), 0, 1)])])
    pair_top = _pair_sum("pair_sum_w_in_top", dw_top, got_top[0], *in_spec, core)
    n_tiles = s // min(TM_DH1, s)
    cut = [0, (3 * n_tiles) // 8, (6 * n_tiles) // 8, n_tiles]
    (gx_a, d_gmix_a), (parts_top, got_bot) = _dh1(
        "dh1_a", dproj, w_in_full, x2, dx1, norm_mix_g, (cut[0], cut[1]),
        plans=[_chip_plan([pair_top], full_rows=[D_MODEL], row0=[0]), _sibling_plan([dw_bot], [in_spec])])
    pair_bot = _pair_sum("pair_sum_w_in_bot", dw_bot, got_bot[0], *in_spec, core)
    (gx_b, d_gmix_b), (parts_in,) = _dh1(
        "dh1_b", dproj, w_in_full, x2, dx1, norm_mix_g, (cut[1], cut[2]), into=gx_a,
        plans=[_chip_plan([pair_bot], full_rows=[D_MODEL], row0=[half], into=parts_top)])
    (grad_x, d_gmix_c), _ = _dh1("dh1_c", dproj, w_in_full, x2, dx1, norm_mix_g, (cut[2], cut[3]), into=gx_b)
    parts["w_in"] = parts_in[0]

    vec_names = ["norm_mix_g", "gm_ln_g", "gm_ln_b", "gm_bs", "norm_x_g", "norm_mem_g", "norm_final_g"]
    vec_g = [d_gmix_a + d_gmix_b + d_gmix_c, d_lng, d_lnb, d_bs, d_gx, d_gmem, d_gf]
    small = jnp.concatenate(vec_g + [sq_err, d_cw], axis=0)
    big_out = {}
    big_out["w_in"], ((small_all,),) = _adamw(
        "adamw_w_in", big_w["w_in"], big_m["w_in"], big_v["w_in"], parts["w_in"],
        plans=[_gather_plan([small[None]], [((N_DEV, 16, D_MODEL), 0, 1)])])
    loss = 0.5 * jnp.sum(small_all[:, 7, :]) / D_MODEL
    for k in others:
        big_out[k], _ = _adamw("adamw_" + k, big_w[k], big_m[k], big_v[k], parts[k])
    row = lambda t: t.reshape(1, D_MODEL)
    vec_w = [norm_mix_g, gm_ln_g, gm_ln_b, row(gm_bs), norm_x_g, norm_mem_g, row(norm_final_g)]
    vec_m = [m_norm_mix_g, m_gm_ln_g, m_gm_ln_b, row(m_gm_bs), m_norm_x_g, m_norm_mem_g, row(m_norm_final_g)]
    vec_v = [v_norm_mix_g, v_gm_ln_g, v_gm_ln_b, row(v_gm_bs), v_norm_x_g, v_norm_mem_g, row(v_norm_final_g)]
    stack8 = lambda ts: jnp.concatenate(ts + [jnp.ones((1, D_MODEL), F32)], axis=0)
    vec_out, _ = _adamw("adamw_vectors", stack8(vec_w), stack8(vec_m), stack8(vec_v), small_all[:, :8, :])
    conv_parts = lax.dynamic_slice(small_all, (0, 8, dev * CHUNK), (N_DEV, 8, CHUNK))
    conv_out, _ = _adamw("adamw_conv", pad_taps(conv_w), pad_taps(m_conv_w), jnp.pad(v_conv_w[0], ((0, 5), (0, 0)), constant_values=1.0), conv_parts)
    ws_out, _ = _adamw("adamw_gm_ws", ws.reshape(8 * CHUNK, CHUNK), m_gm_ws.reshape(8 * CHUNK, CHUNK), v_gm_ws.reshape(8 * CHUNK, CHUNK), ws_all)

    def result(name, which):
        if name in big_out:
            return big_out[name][which][None]
        if name == "conv_w":
            return conv_out[which][:3][None]
        if name == "gm_ws":
            return ws_out[which].reshape(1, 8, CHUNK, CHUNK)
        r = vec_out[which][vec_names.index(name)]
        if name == "gm_bs":
            return r.reshape(1, 8, CHUNK)
        if name == "norm_final_g":
            return r
        return r.reshape(1, D_MODEL)

    weights = ["norm_mix_g", "w_in", "conv_w", "gm_ln_g", "gm_ln_b", "gm_ws", "gm_bs", "w_out", "norm_x_g", "norm_mem_g", "w_q", "w_kv", "w_xo", "norm_final_g"]
    outs = [loss, grad_x[None]]
    for which in range(4):
        outs += [result(name, which) for name in weights]
    return tuple(outs)
```

```python
import math

import jax
import jax.numpy as jnp
from jax import lax
from jax.experimental import pallas as pl
from jax.experimental.pallas import tpu as pltpu

F32 = jnp.float32
BF16 = jnp.bfloat16
MESH = pl.DeviceIdType.MESH

D_MODEL = 1024
N_DEV = 8
N_CHIP = 4
CHUNK = 128
GROUPS = 7
IN_DIM = GROUPS * D_MODEL
SHARD_IN = IN_DIM // N_DEV
MIX_DIM = 2 * D_MODEL
X_HEADS = 4
X_HEAD_DIM = D_MODEL // X_HEADS
MEM_LEN = 256
EPS = 1e-6
GELU_K0 = math.sqrt(2.0 / math.pi)
GELU_K1 = 0.044715

ADAM_LR = 0.001
ADAM_B1 = 0.9
ADAM_B2 = 0.999
ADAM_EPS = 1e-08
ADAM_WD = 0.01
ADAM_STEP = 10

TM_INPROJ = 512
TM_ATTN = 256
TM_DH1 = 512
TM_DW = 512
TK_DW = 4096
TR_ELEMWISE = 256


def _dot(a, b):
    return lax.dot_general(a, b, (((1,), (0,)), ((), ())), preferred_element_type=F32)


def _dot_nt(a, b):
    return lax.dot_general(a, b, (((1,), (1,)), ((), ())), preferred_element_type=F32)


def _rows8(v):
    return jnp.sum(v.reshape(v.shape[0] // 8, 8, v.shape[1]), axis=0)


def _gelu(x):
    x2 = x * x
    t = jnp.tanh(GELU_K0 * (x + GELU_K1 * (x * x2)))
    g = 0.5 * x * (1.0 + t)
    dg = 0.5 * (1.0 + t) + 0.5 * x * (1.0 - t * t) * (GELU_K0 * (1.0 + 3.0 * GELU_K1 * x2))
    return g, dg


def _silu(z):
    s = jax.nn.sigmoid(z)
    return z * s, s * (1.0 + z * (1.0 - s))


def _rms(v):
    r = lax.rsqrt(jnp.mean(v * v, axis=-1, keepdims=True) + EPS)
    return v * r, r


def _rms_bwd(dy_g, vh, r):
    return r * (dy_g - vh * jnp.mean(dy_g * vh, axis=-1, keepdims=True))


def _place():
    return lax.axis_index("x"), lax.axis_index("y"), lax.axis_index("c")


def _other_chips(x, y):
    return [(1 - x, y), (x, 1 - y), (1 - x, 1 - y)]


def _pow2_divisor(n):
    return n & (-n)


def _shard_view(ref, axis, size, d):
    start = d * size
    align = _pow2_divisor(size)
    if align > 1:
        start = pl.multiple_of(start, align)
    idx = [slice(None)] * len(ref.shape)
    idx[axis] = pl.ds(start, size)
    return ref.at[tuple(idx)]


class _Plan:
    def __init__(self, ins, out_shape, sems, start, finish, mid=None, aliases=None):
        self.ins, self.out_shape, self.sems = list(ins), list(out_shape), list(sems)
        self.start, self.mid, self.finish = start, mid, finish
        self.aliases = dict(aliases or {})


def _gather_plan(shards, specs):
    n = len(shards)
    n_copy = 7

    def copies(ins, outs, sems):
        send_sems, recv_sems, local_sems = sems
        x, y, c = _place()
        me, sibling = (x, y, c), (x, y, 1 - c)
        chips = _other_chips(x, y)

        def block(a, p):
            _, axis, size = specs[a]
            return _shard_view(outs[a], axis, size, 4 * p[0] + 2 * p[1] + p[2])

        def copy(a, k, p, to, src=None):
            return pltpu.make_async_remote_copy(
                src_ref=block(a, p) if src is None else src,
                dst_ref=block(a, p),
                send_sem=send_sems.at[a * n_copy + k],
                recv_sem=recv_sems.at[a * n_copy + k],
                device_id=to,
                device_id_type=MESH,
            )

        def mine():
            return [pltpu.make_async_copy(ins[a], block(a, me), local_sems.at[a]) for a in range(n)]

        def first():
            return [cp for a in range(n) for cp in
                    [copy(a, 0, me, sibling, src=ins[a])] + [copy(a, 1 + j, me, (*chip, c), src=ins[a]) for j, chip in enumerate(chips)]]

        def landed():
            return [copy(a, 1 + j, (*chip, c), me) for j, chip in enumerate(chips) for a in range(n)]

        def passed():
            return [copy(a, 4 + j, (*chip, c), sibling) for j, chip in enumerate(chips) for a in range(n)]

        def last():
            return [cp for a in range(n) for cp in
                    [copy(a, 0, sibling, me)] + [copy(a, 4 + j, (*chip, 1 - c), me) for j, chip in enumerate(chips)]]

        return mine, first, landed, passed, last

    def start(ins, outs, sems):
        mine, first, _, _, _ = copies(ins, outs, sems)
        for cp in mine() + first():
            cp.start()

    def mid(ins, outs, sems):
        _, _, landed, passed, _ = copies(ins, outs, sems)
        for got, fwd in zip(landed(), passed()):
            got.wait_recv()
            fwd.start()

    def finish(ins, outs, sems):
        mine, first, _, passed, last = copies(ins, outs, sems)
        for cp in last():
            cp.wait_recv()
        for cp in first() + passed():
            cp.wait_send()
        for cp in mine():
            cp.wait()

    return _Plan(
        shards,
        [jax.ShapeDtypeStruct(full, s.dtype) for s, (full, _, _) in zip(shards, specs)],
        [pltpu.SemaphoreType.DMA((n * n_copy,)), pltpu.SemaphoreType.DMA((n * n_copy,)), pltpu.SemaphoreType.DMA((n,))],
        start, finish, mid,
    )


def _sibling_plan(grads, specs):
    n = len(grads)

    def shard_shape(a):
        axis, size = specs[a]
        shp = list(grads[a].shape)
        shp[axis] = size
        return tuple(shp)

    def copies(ins, outs, sems):
        send_sems, recv_sems = sems
        x, y, c = _place()
        return [
            pltpu.make_async_remote_copy(
                src_ref=_shard_view(ins[a], specs[a][0], specs[a][1], 2 * q + (1 - c)),
                dst_ref=outs[a].at[q],
                send_sem=send_sems.at[a * N_CHIP + q],
                recv_sem=recv_sems.at[a * N_CHIP + q],
                device_id=(x, y, 1 - c),
                device_id_type=MESH,
            )
            for a in range(n) for q in range(N_CHIP)
        ]

    def start(ins, outs, sems):
        for cp in copies(ins, outs, sems):
            cp.start()

    def finish(ins, outs, sems):
        for cp in copies(ins, outs, sems):
            cp.wait()

    return _Plan(
        grads,
        [jax.ShapeDtypeStruct((N_CHIP, *shard_shape(a)), grads[a].dtype) for a in range(n)],
        [pltpu.SemaphoreType.DMA((n * N_CHIP,)), pltpu.SemaphoreType.DMA((n * N_CHIP,))],
        start, finish,
    )


def _chip_plan(parts, full_rows=None, row0=None, into=None):
    n = len(parts)
    full_rows = [p.shape[1] for p in parts] if full_rows is None else full_rows
    row0 = [0] * n if row0 is None else row0

    def copies(ins, outs, sems):
        send_sems, recv_sems, local_sems = sems
        x, y, c = _place()
        my_chip = 2 * x + y
        chips = _other_chips(x, y)

        def land(a, q):
            return outs[a].at[q, pl.ds(row0[a], parts[a].shape[1]), :]

        def mine():
            return [pltpu.make_async_copy(ins[a].at[my_chip], land(a, my_chip), local_sems.at[a]) for a in range(n)]

        def remote(sending):
            return [
                pltpu.make_async_remote_copy(
                    src_ref=ins[a].at[2 * chip[0] + chip[1]],
                    dst_ref=land(a, my_chip if sending else 2 * chip[0] + chip[1]),
                    send_sem=send_sems.at[a * 3 + j], recv_sem=recv_sems.at[a * 3 + j],
                    device_id=(*chip, c), device_id_type=MESH)
                for a in range(n) for j, chip in enumerate(chips)
            ]

        return mine, remote

    def start(ins, outs, sems):
        mine, remote = copies(ins, outs, sems)
        for cp in mine() + remote(True):
            cp.start()

    def finish(ins, outs, sems):
        mine, remote = copies(ins, outs, sems)
        for cp in remote(False):
            cp.wait_recv()
        for cp in remote(True):
            cp.wait_send()
        for cp in mine():
            cp.wait()

    return _Plan(
        list(parts) + (list(into) if into is not None else []),
        [jax.ShapeDtypeStruct((N_CHIP, full_rows[a], parts[a].shape[2]), parts[a].dtype) for a in range(n)],
        [pltpu.SemaphoreType.DMA((n * 3,)), pltpu.SemaphoreType.DMA((n * 3,)), pltpu.SemaphoreType.DMA((n,))],
        start, finish,
        aliases={n + a: a for a in range(n)} if into is not None else None,
    )


def _call(body, *, name, operands, out_shape, grid=(), in_specs=None, out_specs=None, scratch_shapes=(),
          plans=(), aliases=None, prefetch=()):
    operands, out_shape, scratch_shapes = list(operands), list(out_shape), list(scratch_shapes)
    n_pf, n_in, n_out, n_scr = len(prefetch), len(operands), len(out_shape), len(scratch_shapes)
    in_vmem = pl.BlockSpec(memory_space=pltpu.VMEM)
    in_hbm = pl.BlockSpec(memory_space=pl.ANY)
    in_specs = [in_vmem] * n_in if in_specs is None else list(in_specs)
    out_specs = [in_vmem] * n_out if out_specs is None else list(out_specs)
    io_alias = {n_pf + i: o for i, o in (aliases or {}).items()}
    pos_in, pos_out = n_pf + n_in, n_out
    for p in plans:
        for i, o in p.aliases.items():
            io_alias[pos_in + i] = pos_out + o
        pos_in += len(p.ins)
        pos_out += len(p.out_shape)
    steps = math.prod(grid) if grid else 1
    mid_step = (steps * 3) // 5

    def wrapped(*refs):
        refs = list(refs)
        take = lambda k: [refs.pop(0) for _ in range(k)]
        pf = take(n_pf)
        ins, p_ins = take(n_in), [take(len(p.ins)) for p in plans]
        outs, p_outs = take(n_out), [take(len(p.out_shape)) for p in plans]
        scr, p_sems = take(n_scr), [take(len(p.sems)) for p in plans]
        step = 0
        for ax, g in enumerate(grid):
            step = step * g + pl.program_id(ax)

        def hook(kind, at):
            todo = [(getattr(p, kind), a, b, c) for p, a, b, c in zip(plans, p_ins, p_outs, p_sems) if getattr(p, kind)]

            def run():
                for fn, a, b, c in todo:
                    fn(a, b, c)

            if todo and grid:
                pl.when(step == at)(run)
            elif todo:
                run()

        hook("start", 0)
        if body is not None:
            body(*pf, *ins, *outs, *scr)
        hook("mid", mid_step)
        hook("finish", steps - 1)

    layout = dict(
        grid=grid,
        in_specs=in_specs + [in_hbm] * sum(len(p.ins) for p in plans),
        out_specs=out_specs + [in_hbm] * sum(len(p.out_shape) for p in plans),
        scratch_shapes=scratch_shapes + [s for p in plans for s in p.sems],
    )
    if n_pf:
        layout = dict(grid_spec=pltpu.PrefetchScalarGridSpec(num_scalar_prefetch=n_pf, **layout))
    res = pl.pallas_call(
        wrapped,
        name=name,
        out_shape=out_shape + [o for p in plans for o in p.out_shape],
        input_output_aliases=io_alias,
        compiler_params=pltpu.CompilerParams(dimension_semantics=("arbitrary",) * len(grid)) if grid else None,
        **layout,
    )(*prefetch, *operands, *[a for p in plans for a in p.ins])
    res = list(res)
    outs, plan_outs = res[:n_out], []
    pos = n_out
    for p in plans:
        plan_outs.append(res[pos : pos + len(p.out_shape)])
        pos += len(p.out_shape)
    return outs, plan_outs


def _inproj(x, g, w_shard, order, plans=()):
    s = x.shape[0]
    tm = min(TM_INPROJ, s)
    nt = s // tm
    wide = 2 * SHARD_IN
    fwd_at = nt // 2

    def body(order_ref, x_ref, g_ref, w_hbm, proj_ref, h1_ref, h1t_ref, wfull_hbm,
             w_scr, h1_scr, send_sems, recv_sems, local_sems):
        k, i = pl.program_id(0), pl.program_id(1)
        px, py, c = _place()
        me, sibling = (px, py, c), (px, py, 1 - c)
        chips = _other_chips(px, py)

        def half(slot, core):
            return w_scr.at[slot, :, pl.ds(pl.multiple_of(core * SHARD_IN, 128), SHARD_IN)]

        def copy(n, slot, core, to, src=None):
            return pltpu.make_async_remote_copy(
                src_ref=half(slot, core) if src is None else src, dst_ref=half(slot, core),
                send_sem=send_sems.at[n], recv_sem=recv_sems.at[n], device_id=to, device_id_type=MESH)

        def mine():
            return pltpu.make_async_copy(w_hbm, half(0, c), local_sems.at[0])

        def first():
            return [copy(0, 0, c, sibling, src=w_hbm)] + [copy(1 + j, 1 + j, c, (*chip, c), src=w_hbm) for j, chip in enumerate(chips)]

        def save(slot):
            cols = pl.ds(pl.multiple_of(order_ref[slot] * wide, 128), wide)
            return pltpu.make_async_copy(w_scr.at[slot], wfull_hbm.at[:, cols], local_sems.at[1 + slot])

        @pl.when((k == 0) & (i == 0))
        def _():
            mine().start()
            for cp in first():
                cp.start()
            mine().wait()
            copy(0, 0, 1 - c, me).wait_recv()

        for j, chip in enumerate(chips):
            @pl.when((k == j) & (i == fwd_at))
            def _():
                copy(1 + j, 1 + j, c, me).wait_recv()
                copy(4 + j, 1 + j, c, sibling).start()

            @pl.when((k == j + 1) & (i == 0))
            def _():
                copy(4 + j, 1 + j, 1 - c, me).wait_recv()

        @pl.when((k == 3) & (i == 0))
        def _():
            for slot in range(N_CHIP):
                save(slot).start()

        @pl.when(k == 0)
        def _():
            xh, _ = _rms(x_ref[...])
            h = xh * g_ref[...]
            h1_ref[...] = h.astype(BF16)
            h1t_ref[...] = h.T.astype(BF16)
            h1_scr[pl.ds(pl.multiple_of(i * tm, tm), tm), :] = h.astype(BF16)

        proj_ref[...] = _dot(h1_scr[pl.ds(pl.multiple_of(i * tm, tm), tm), :], w_scr[k]).astype(BF16)

        @pl.when((k == 3) & (i == nt - 1))
        def _():
            for cp in first() + [copy(4 + j, 1 + j, c, sibling) for j in range(3)]:
                cp.wait_send()
            for slot in range(N_CHIP):
                save(slot).wait()

    once = lambda k, i, order_ref: (jnp.where(k == 0, i, nt - 1), 0)
    return _call(
        body,
        name="inproj",
        prefetch=[order],
        operands=[x, g, w_shard],
        grid=(N_CHIP, nt),
        in_specs=[
            pl.BlockSpec((tm, D_MODEL), once),
            pl.BlockSpec((1, D_MODEL), lambda k, i, order_ref: (0, 0)),
            pl.BlockSpec(memory_space=pl.ANY),
        ],
        out_specs=[
            pl.BlockSpec((tm, wide), lambda k, i, order_ref: (i, order_ref[k])),
            pl.BlockSpec((tm, D_MODEL), once),
            pl.BlockSpec((D_MODEL, tm), lambda k, i, order_ref: (0, jnp.where(k == 0, i, nt - 1))),
            pl.BlockSpec(memory_space=pl.ANY),
        ],
        out_shape=[
            jax.ShapeDtypeStruct((s, IN_DIM), BF16),
            jax.ShapeDtypeStruct((s, D_MODEL), BF16),
            jax.ShapeDtypeStruct((D_MODEL, s), BF16),
            jax.ShapeDtypeStruct((D_MODEL, IN_DIM), BF16),
        ],
        scratch_shapes=[
            pltpu.VMEM((N_CHIP, D_MODEL, wide), BF16),
            pltpu.VMEM((s, D_MODEL), BF16),
            pltpu.SemaphoreType.DMA((7,)),
            pltpu.SemaphoreType.DMA((7,)),
            pltpu.SemaphoreType.DMA((1 + N_CHIP,)),
        ],
        plans=plans,
    )


def _causal_mask():
    row = lax.broadcasted_iota(jnp.int32, (CHUNK, CHUNK), 0)
    col = lax.broadcasted_iota(jnp.int32, (CHUNK, CHUNK), 1)
    return row, row >= col


def _mixer_fwd(proj, conv_w, ln_g, ln_b, ws, bs_b, plans=()):
    s = proj.shape[0]
    nt = s // CHUNK

    def body(proj_ref, cw_ref, lng_ref, lnb_ref, ws_ref, bsb_ref, cat_ref, pcar_ref):
        @pl.when(pl.program_id(0) == 0)
        def _():
            pcar_ref[...] = jnp.zeros_like(pcar_ref)

        row, tril = _causal_mask()
        for j in range(8):
            cs = slice(CHUNK * j, CHUNK * (j + 1))

            def grp(k):
                return proj_ref[:, k * D_MODEL + CHUNK * j : k * D_MODEL + CHUNK * (j + 1)].astype(F32)

            w = cw_ref[:, cs]
            p = grp(1) * grp(2)
            pc = pcar_ref[:, cs]
            p1 = jnp.where(row == 0, pc[7:8], pltpu.roll(p, 1, 0))
            p2 = jnp.where(row == 0, pc[6:7], jnp.where(row == 1, pc[7:8], pltpu.roll(p, 2, 0)))
            pcar_ref[:, cs] = p[CHUNK - 8 :]
            cv = w[0:1] * p2 + w[1:2] * p1 + w[2:3] * p
            sa, _ = _silu(grp(3))
            cat_ref[:, cs] = ((grp(0) * cv) * sa).astype(BF16)
            gu, _ = _gelu(grp(4))
            gv, _ = _gelu(grp(5))
            dv = gv - jnp.mean(gv, axis=-1, keepdims=True)
            vn = dv * lax.rsqrt(jnp.mean(dv * dv, axis=-1, keepdims=True) + EPS)
            vn = vn * lng_ref[:, cs] + lnb_ref[:, cs]
            wc = jnp.where(tril, ws_ref[j], 0.0).astype(BF16)
            sp = _dot(wc, vn.astype(BF16)) + bsb_ref[j]
            sb, _ = _silu(grp(6))
            cat_ref[:, D_MODEL + CHUNK * j : D_MODEL + CHUNK * (j + 1)] = ((gu * sp) * sb).astype(BF16)

    whole = lambda shape: pl.BlockSpec(shape, lambda i: (0,) * len(shape))
    (cat,), plan_outs = _call(
        body,
        name="mixer_fwd",
        operands=[proj, conv_w, ln_g, ln_b, ws, bs_b],
        grid=(nt,),
        in_specs=[
            pl.BlockSpec((CHUNK, IN_DIM), lambda i: (i, 0)),
            whole((8, D_MODEL)),
            whole((1, D_MODEL)),
            whole((1, D_MODEL)),
            whole((8, CHUNK, CHUNK)),
            whole((8, CHUNK, CHUNK)),
        ],
        out_specs=[pl.BlockSpec((CHUNK, MIX_DIM), lambda i: (i, 0))],
        out_shape=[jax.ShapeDtypeStruct((s, MIX_DIM), BF16)],
        scratch_shapes=[pltpu.VMEM((8, D_MODEL), F32)],
        plans=plans,
    )
    return cat, plan_outs


def _mixer_bwd(proj, dcat, conv_w, ln_g, ln_b, ws, bs_b, plans=()):
    s = proj.shape[0]
    nt = s // CHUNK

    def body(proj_ref, halo_ref, dcat_ref, cw_ref, lng_ref, lnb_ref, ws_ref, bsb_ref,
             dproj_ref, dcw_ref, dlng_ref, dlnb_ref, dws_ref, dbs_ref,
             car_ref, acc_cw, acc_lng, acc_lnb, acc_bs):
        i = pl.program_id(0)
        tile = nt - 1 - i

        @pl.when(i == 0)
        def _():
            car_ref[...] = jnp.zeros_like(car_ref)
            acc_cw[...] = jnp.zeros_like(acc_cw)
            acc_lng[...] = jnp.zeros_like(acc_lng)
            acc_lnb[...] = jnp.zeros_like(acc_lnb)
            acc_bs[...] = jnp.zeros_like(acc_bs)
            dws_ref[...] = jnp.zeros_like(dws_ref)

        row, tril = _causal_mask()
        has_prev = jnp.where(tile > 0, 1.0, 0.0).astype(F32)
        for j in range(8):
            cs = slice(CHUNK * j, CHUNK * (j + 1))

            def col(k):
                return slice(k * D_MODEL + CHUNK * j, k * D_MODEL + CHUNK * (j + 1))

            def grp(k):
                return proj_ref[:, col(k)].astype(F32)

            w = cw_ref[:, cs]
            gb, gc, xa = grp(0), grp(1), grp(2)
            p = gc * xa
            pprev = halo_ref[:, col(1)].astype(F32) * halo_ref[:, col(2)].astype(F32) * has_prev
            p1 = jnp.where(row == 0, pprev[15:16], pltpu.roll(p, 1, 0))
            p2 = jnp.where(row == 0, pprev[14:15], jnp.where(row == 1, pprev[15:16], pltpu.roll(p, 2, 0)))
            cv = w[0:1] * p2 + w[1:2] * p1 + w[2:3] * p
            za = grp(3)
            sa, dsa = _silu(za)
            da = dcat_ref[:, cs].astype(F32)
            dproj_ref[:, col(0)] = (da * cv * sa).astype(BF16)
            dproj_ref[:, col(3)] = (da * gb * cv * dsa).astype(BF16)
            dcv = da * gb * sa
            nxt = car_ref[:, cs]
            d1 = jnp.where(row == CHUNK - 1, nxt[0:1], pltpu.roll(dcv, CHUNK - 1, 0))
            d2 = jnp.where(row == CHUNK - 2, nxt[0:1], jnp.where(row == CHUNK - 1, nxt[1:2], pltpu.roll(dcv, CHUNK - 2, 0)))
            car_ref[:, cs] = dcv[0:8]
            dp = w[2:3] * dcv + w[1:2] * d1 + w[0:1] * d2
            dproj_ref[:, col(1)] = (dp * xa).astype(BF16)
            dproj_ref[:, col(2)] = (dp * gc).astype(BF16)
            acc_cw[0, :, cs] += _rows8(dcv * p2)
            acc_cw[1, :, cs] += _rows8(dcv * p1)
            acc_cw[2, :, cs] += _rows8(dcv * p)
            gu, dgu = _gelu(grp(4))
            gv, dgv = _gelu(grp(5))
            dv = gv - jnp.mean(gv, axis=-1, keepdims=True)
            rstd = lax.rsqrt(jnp.mean(dv * dv, axis=-1, keepdims=True) + EPS)
            vnh = dv * rstd
            lng = lng_ref[:, cs]
            vnb = (vnh * lng + lnb_ref[:, cs]).astype(BF16)
            wcf = jnp.where(tril, ws_ref[j], 0.0)
            sp = _dot(wcf.astype(BF16), vnb) + bsb_ref[j]
            zb = grp(6)
            sb, dsb = _silu(zb)
            db = dcat_ref[:, D_MODEL + CHUNK * j : D_MODEL + CHUNK * (j + 1)].astype(F32)
            dproj_ref[:, col(4)] = (db * sp * sb * dgu).astype(BF16)
            dproj_ref[:, col(6)] = (db * gu * sp * dsb).astype(BF16)
            dsp = db * gu * sb
            acc_bs[j] += dsp
            dspb = dsp.astype(BF16)
            dws_ref[j] += _dot_nt(dspb, vnb)
            dvn = _dot(wcf.T.astype(BF16), dspb)
            acc_lnb[:, cs] += _rows8(dvn)
            acc_lng[:, cs] += _rows8(dvn * vnh)
            dvh = dvn * lng
            dgvv = rstd * (dvh - jnp.mean(dvh, axis=-1, keepdims=True) - vnh * jnp.mean(dvh * vnh, axis=-1, keepdims=True))
            dproj_ref[:, col(5)] = (dgvv * dgv).astype(BF16)

        @pl.when(i == nt - 1)
        def _():
            dlng_ref[...] = jnp.sum(acc_lng[...], axis=0, keepdims=True)
            dlnb_ref[...] = jnp.sum(acc_lnb[...], axis=0, keepdims=True)
            dcw_ref[...] = jnp.zeros_like(dcw_ref)
            for k in range(3):
                dcw_ref[k : k + 1, :] = jnp.sum(acc_cw[k], axis=0, keepdims=True)
            for j in range(8):
                dws_ref[j] = jnp.where(tril, dws_ref[j], 0.0)
                dbs_ref[:, CHUNK * j : CHUNK * (j + 1)] = jnp.sum(acc_bs[j].T, axis=0, keepdims=True)

    whole = lambda shape: pl.BlockSpec(shape, lambda i: (0,) * len(shape))
    halo_rows = 16
    per = CHUNK // halo_rows
    return _call(
        body,
        name="mixer_bwd",
        operands=[proj, proj, dcat, conv_w, ln_g, ln_b, ws, bs_b],
        plans=plans,
        grid=(nt,),
        in_specs=[
            pl.BlockSpec((CHUNK, IN_DIM), lambda i: (nt - 1 - i, 0)),
            pl.BlockSpec((halo_rows, IN_DIM), lambda i: (jnp.maximum((nt - 1 - i) * per - 1, 0), 0)),
            pl.BlockSpec((CHUNK, MIX_DIM), lambda i: (nt - 1 - i, 0)),
            whole((8, D_MODEL)),
            whole((1, D_MODEL)),
            whole((1, D_MODEL)),
            whole((8, CHUNK, CHUNK)),
            whole((8, CHUNK, CHUNK)),
        ],
        out_specs=[
            pl.BlockSpec((CHUNK, IN_DIM), lambda i: (nt - 1 - i, 0)),
            whole((8, D_MODEL)),
            whole((1, D_MODEL)),
            whole((1, D_MODEL)),
            whole((8, CHUNK, CHUNK)),
            whole((1, D_MODEL)),
        ],
        out_shape=[
            jax.ShapeDtypeStruct((s, IN_DIM), BF16),
            jax.ShapeDtypeStruct((8, D_MODEL), F32),
            jax.ShapeDtypeStruct((1, D_MODEL), F32),
            jax.ShapeDtypeStruct((1, D_MODEL), F32),
            jax.ShapeDtypeStruct((8, CHUNK, CHUNK), F32),
            jax.ShapeDtypeStruct((1, D_MODEL), F32),
        ],
        scratch_shapes=[
            pltpu.VMEM((8, D_MODEL), F32),
            pltpu.VMEM((3, 8, D_MODEL), F32),
            pltpu.VMEM((8, D_MODEL), F32),
            pltpu.VMEM((8, D_MODEL), F32),
            pltpu.VMEM((8, CHUNK, CHUNK), F32),
        ],
    )


def _kv_fwd(mem, g_mem, w_kv):
    def body(mem_ref, g_ref, w_ref, k_ref, vv_ref, mt_ref):
        mh, _ = _rms(mem_ref[...])
        m = mh * g_ref[...]
        kv = _dot(m.astype(BF16), w_ref[...])
        k_ref[...] = kv[:, :D_MODEL].astype(BF16)
        vv_ref[...] = kv[:, D_MODEL:].astype(BF16)
        mt_ref[...] = m.T.astype(BF16)

    return pl.pallas_call(
        body,
        name="kv_fwd",
        out_shape=[
            jax.ShapeDtypeStruct((MEM_LEN, D_MODEL), BF16),
            jax.ShapeDtypeStruct((MEM_LEN, D_MODEL), BF16),
            jax.ShapeDtypeStruct((D_MODEL, MEM_LEN), BF16),
        ],
    )(mem, g_mem, w_kv)


def _kv_bwd(mem, w_kv, mt, dk, dvv):
    def body(mem_ref, w_ref, mt_ref, dk_ref, dvv_ref, dw_ref, dg_ref):
        dkv = jnp.concatenate([dk_ref[...], dvv_ref[...]], axis=1).astype(BF16)
        dw_ref[...] = _dot(mt_ref[...], dkv)
        dm = _dot_nt(dkv, w_ref[...])
        mh, _ = _rms(mem_ref[...])
        dg_ref[...] = jnp.sum(dm * mh, axis=0, keepdims=True)

    return pl.pallas_call(
        body,
        name="kv_bwd",
        out_shape=[
            jax.ShapeDtypeStruct((D_MODEL, 2 * D_MODEL), F32),
            jax.ShapeDtypeStruct((1, D_MODEL), F32),
        ],
    )(mem, w_kv, mt, dk, dvv)


def _attn(x, cat, target, w_out, w_q, w_xo, k, vv, g_x, g_f):
    s = x.shape[0]
    tm = min(TM_ATTN, s)
    nt = s // tm
    scale = 1.0 / math.sqrt(X_HEAD_DIM)

    def body(x_ref, cat_ref, t_ref, wout_ref, wq_ref, wxo_ref, k_ref, vv_ref, gx_ref, gf_ref,
             dx1_ref, dcat_ref, catt_ref, h2t_ref, ot_ref, dx2b_ref, dqb_ref, dx1b_ref,
             dk_ref, dvv_ref, dgf_ref, dgx_ref, loss_ref,
             p_scr, q_scr, o_scr, dq_scr, acc_gf, acc_gx, acc_loss):
        i = pl.program_id(0)

        @pl.when(i == 0)
        def _():
            dk_ref[...] = jnp.zeros_like(dk_ref)
            dvv_ref[...] = jnp.zeros_like(dvv_ref)
            acc_gf[...] = jnp.zeros_like(acc_gf)
            acc_gx[...] = jnp.zeros_like(acc_gx)
            acc_loss[...] = jnp.zeros_like(acc_loss)

        cat = cat_ref[...]
        catt_ref[...] = cat.astype(F32).T.astype(BF16)
        x1 = x_ref[...] + _dot(cat, wout_ref[...])
        x1h, r2 = _rms(x1)
        h2 = x1h * gx_ref[...]
        h2b = h2.astype(BF16)
        h2t_ref[...] = h2.T.astype(BF16)
        q_scr[...] = _dot(h2b, wq_ref[...]).astype(BF16)
        for h in range(X_HEADS):
            hs = slice(X_HEAD_DIM * h, X_HEAD_DIM * (h + 1))
            sc = _dot_nt(q_scr[:, hs], k_ref[:, hs]) * scale
            e = jnp.exp(sc - jnp.max(sc, axis=-1, keepdims=True))
            p = e / jnp.sum(e, axis=-1, keepdims=True)
            p_scr[:, hs] = p
            o_scr[:, hs] = _dot(p.astype(BF16), vv_ref[:, hs])
        o = o_scr[...]
        ot_ref[...] = o.T.astype(BF16)
        x2 = x1 + _dot(o.astype(BF16), wxo_ref[...])
        x2h, r3 = _rms(x2)
        gf = gf_ref[...]
        err = x2h * gf - t_ref[...]
        acc_loss[...] += _rows8(err * err)
        dy = err * (1.0 / D_MODEL)
        acc_gf[...] += _rows8(dy * x2h)
        dx2 = _rms_bwd(dy * gf, x2h, r3)
        dx2b = dx2.astype(BF16)
        dx2b_ref[...] = dx2b
        dob = _dot_nt(dx2b, wxo_ref[...]).astype(BF16)
        for h in range(X_HEADS):
            hs = slice(X_HEAD_DIM * h, X_HEAD_DIM * (h + 1))
            p = p_scr[:, hs]
            dp = _dot_nt(dob[:, hs], vv_ref[:, hs])
            ds = p * (dp - jnp.sum(dp * p, axis=-1, keepdims=True)) * scale
            dsb = ds.astype(BF16)
            dq_scr[:, hs] = _dot(dsb, k_ref[:, hs]).astype(BF16)
            dvv_ref[:, hs] += _dot(p.T.astype(BF16), dob[:, hs])
            dk_ref[:, hs] += _dot(ds.T.astype(BF16), q_scr[:, hs])
        dqb = dq_scr[...]
        dqb_ref[...] = dqb
        dh2 = _dot_nt(dqb, wq_ref[...])
        acc_gx[...] += _rows8(dh2 * x1h)
        dx1 = dx2 + _rms_bwd(dh2 * gx_ref[...], x1h, r2)
        dx1_ref[...] = dx1
        dx1b = dx1.astype(BF16)
        dx1b_ref[...] = dx1b
        dcat_ref[...] = _dot_nt(dx1b, wout_ref[...]).astype(BF16)

        @pl.when(i == nt - 1)
        def _():
            dgf_ref[...] = jnp.sum(acc_gf[...], axis=0, keepdims=True)
            dgx_ref[...] = jnp.sum(acc_gx[...], axis=0, keepdims=True)
            loss_ref[...] = jnp.sum(acc_loss[...], axis=0, keepdims=True)

    whole = lambda shape: pl.BlockSpec(shape, lambda i: (0,) * len(shape))
    rows = lambda width: pl.BlockSpec((tm, width), lambda i: (i, 0))
    cols = lambda height: pl.BlockSpec((height, tm), lambda i: (0, i))
    vec = jax.ShapeDtypeStruct((1, D_MODEL), F32)
    return pl.pallas_call(
        body,
        name="attn",
        grid=(nt,),
        in_specs=[
            rows(D_MODEL), rows(MIX_DIM), rows(D_MODEL),
            whole((MIX_DIM, D_MODEL)), whole((D_MODEL, D_MODEL)), whole((D_MODEL, D_MODEL)),
            whole((MEM_LEN, D_MODEL)), whole((MEM_LEN, D_MODEL)),
            whole((1, D_MODEL)), whole((1, D_MODEL)),
        ],
        out_specs=[
            rows(D_MODEL), rows(MIX_DIM), cols(MIX_DIM), cols(D_MODEL), cols(D_MODEL),
            rows(D_MODEL), rows(D_MODEL), rows(D_MODEL),
            whole((MEM_LEN, D_MODEL)), whole((MEM_LEN, D_MODEL)),
            whole((1, D_MODEL)), whole((1, D_MODEL)), whole((1, D_MODEL)),
        ],
        out_shape=[
            jax.ShapeDtypeStruct((s, D_MODEL), F32),
            jax.ShapeDtypeStruct((s, MIX_DIM), BF16),
            jax.ShapeDtypeStruct((MIX_DIM, s), BF16),
            jax.ShapeDtypeStruct((D_MODEL, s), BF16),
            jax.ShapeDtypeStruct((D_MODEL, s), BF16),
            jax.ShapeDtypeStruct((s, D_MODEL), BF16),
            jax.ShapeDtypeStruct((s, D_MODEL), BF16),
            jax.ShapeDtypeStruct((s, D_MODEL), BF16),
            jax.ShapeDtypeStruct((MEM_LEN, D_MODEL), F32),
            jax.ShapeDtypeStruct((MEM_LEN, D_MODEL), F32),
            vec, vec, vec,
        ],
        scratch_shapes=[
            pltpu.VMEM((tm, D_MODEL), F32),
            pltpu.VMEM((tm, D_MODEL), BF16),
            pltpu.VMEM((tm, D_MODEL), F32),
            pltpu.VMEM((tm, D_MODEL), BF16),
            pltpu.VMEM((8, D_MODEL), F32),
            pltpu.VMEM((8, D_MODEL), F32),
            pltpu.VMEM((8, D_MODEL), F32),
        ],
        compiler_params=pltpu.CompilerParams(dimension_semantics=("arbitrary",)),
    )(x, cat, target, w_out, w_q, w_xo, k, vv, g_x, g_f)


def _matmul_f32(name, a, b, rows=None, plans=()):
    first, m = (0, a.shape[0]) if rows is None else rows
    kk = a.shape[1]
    n = b.shape[1]
    tm, tn, tk = min(m, TM_DW), min(n, D_MODEL), min(kk, TK_DW)
    nk = kk // tk
    i0 = first // tm

    def body(a_ref, b_ref, o_ref):
        prod = _dot(a_ref[...], b_ref[...])
        if nk == 1:
            o_ref[...] = prod
            return
        kid = pl.program_id(2)

        @pl.when(kid == 0)
        def _():
            o_ref[...] = prod

        @pl.when(kid > 0)
        def _():
            o_ref[...] += prod

    (out,), plan_outs = _call(
        body,
        name=name,
        operands=[a, b],
        grid=(m // tm, n // tn, nk),
        in_specs=[pl.BlockSpec((tm, tk), lambda i, j, k: (i + i0, k)), pl.BlockSpec((tk, tn), lambda i, j, k: (k, j))],
        out_specs=[pl.BlockSpec((tm, tn), lambda i, j, k: (i, j))],
        out_shape=[jax.ShapeDtypeStruct((m, n), F32)],
        plans=plans,
    )
    return out, plan_outs


def _dh1(name, dproj, w_in, x, dx1, g, tiles, into=None, plans=()):
    s = x.shape[0]
    tm = min(TM_DH1, s)
    t0, nt = tiles[0], tiles[1] - tiles[0]

    def body(dp_ref, w_ref, x_ref, dx1_ref, g_ref, *rest):
        gx_ref, dg_ref, acc_g = rest[-3:]
        i = pl.program_id(0)

        @pl.when(i == 0)
        def _():
            acc_g[...] = jnp.zeros_like(acc_g)

        dh1 = _dot_nt(dp_ref[...], w_ref[...])
        xh, r = _rms(x_ref[...])
        acc_g[...] += _rows8(dh1 * xh)
        gx_ref[...] = dx1_ref[...] + _rms_bwd(dh1 * g_ref[...], xh, r)

        @pl.when(i == nt - 1)
        def _():
            dg_ref[...] = jnp.sum(acc_g[...], axis=0, keepdims=True)

    rows = pl.BlockSpec((tm, D_MODEL), lambda i: (i + t0, 0))
    return _call(
        body,
        name=name,
        operands=[dproj, w_in, x, dx1, g] + ([into] if into is not None else []),
        grid=(nt,),
        in_specs=[
            pl.BlockSpec((tm, IN_DIM), lambda i: (i + t0, 0)),
            pl.BlockSpec((D_MODEL, IN_DIM), lambda i: (0, 0), pipeline_mode=pl.Buffered(1)),
            rows,
            rows,
            pl.BlockSpec((1, D_MODEL), lambda i: (0, 0)),
        ] + ([pl.BlockSpec(memory_space=pl.ANY)] if into is not None else []),
        out_specs=[rows, pl.BlockSpec((1, D_MODEL), lambda i: (0, 0))],
        out_shape=[jax.ShapeDtypeStruct((s, D_MODEL), F32), jax.ShapeDtypeStruct((1, D_MODEL), F32)],
        scratch_shapes=[pltpu.VMEM((8, D_MODEL), F32)],
        plans=plans,
        aliases={5: 0} if into is not None else None,
    )


def _pair_sum(name, grad, got, axis, size, core):
    shard = list(grad.shape)
    shard[axis] = size
    r, cdim = shard
    tr = min(r, TR_ELEMWISE)
    nr = r // tr

    def body(core_ref, g_ref, got_ref, o_ref):
        o_ref[0] = (g_ref[...] + got_ref[0]).astype(BF16)

    if axis == 1:
        g_map = lambda q, t, core_ref: (t, 2 * q + core_ref[0])
    else:
        g_map = lambda q, t, core_ref: ((2 * q + core_ref[0]) * nr + t, 0)
    return pl.pallas_call(
        body,
        name=name,
        grid_spec=pltpu.PrefetchScalarGridSpec(
            num_scalar_prefetch=1,
            grid=(N_CHIP, nr),
            in_specs=[
                pl.BlockSpec((tr, cdim), g_map),
                pl.BlockSpec((1, tr, cdim), lambda q, t, core_ref: (q, t, 0)),
            ],
            out_specs=pl.BlockSpec((1, tr, cdim), lambda q, t, core_ref: (q, t, 0)),
        ),
        out_shape=jax.ShapeDtypeStruct((N_CHIP, r, cdim), BF16),
        compiler_params=pltpu.CompilerParams(dimension_semantics=("arbitrary", "arbitrary")),
    )(core, grad, got)


def _adamw(name, w, m, v, parts, plans=()):
    r, cdim = w.shape
    n_parts = parts.shape[0]
    tr = min(r, TR_ELEMWISE)
    c1 = 1.0 / (1.0 - ADAM_B1**ADAM_STEP)
    c2 = 1.0 / (1.0 - ADAM_B2**ADAM_STEP)

    def body(w_ref, m_ref, v_ref, p_ref, g_ref, d_ref, nm_ref, nv_ref):
        g = p_ref[0].astype(F32)
        for q in range(1, n_parts):
            g = g + p_ref[q].astype(F32)
        m_new = ADAM_B1 * m_ref[...] + (1.0 - ADAM_B1) * g
        v_new = ADAM_B2 * v_ref[...] + (1.0 - ADAM_B2) * (g * g)
        g_ref[...] = g
        nm_ref[...] = m_new
        nv_ref[...] = v_new
        d_ref[...] = -ADAM_LR * ((m_new * c1) / (jnp.sqrt(v_new * c2) + ADAM_EPS) + ADAM_WD * w_ref[...])

    blk = pl.BlockSpec((tr, cdim), lambda t: (t, 0))
    out = jax.ShapeDtypeStruct((r, cdim), F32)
    return _call(
        body,
        name=name,
        operands=[w, m, v, parts],
        grid=(r // tr,),
        in_specs=[blk, blk, blk, pl.BlockSpec((n_parts, tr, cdim), lambda t: (0, t, 0))],
        out_specs=[blk, blk, blk, blk],
        out_shape=[out, out, out, out],
        plans=plans,
    )


def kernel(x, mem, norm_mix_g, w_in, conv_w, gm_ln_g, gm_ln_b, gm_ws, gm_bs, w_out, norm_x_g, norm_mem_g, w_q, w_kv, w_xo, norm_final_g, loss_target, m_norm_mix_g, m_w_in, m_conv_w, m_gm_ln_g, m_gm_ln_b, m_gm_ws, m_gm_bs, m_w_out, m_norm_x_g, m_norm_mem_g, m_w_q, m_w_kv, m_w_xo, m_norm_final_g, v_norm_mix_g, v_w_in, v_conv_w, v_gm_ln_g, v_gm_ln_b, v_gm_ws, v_gm_bs, v_w_out, v_norm_x_g, v_norm_mem_g, v_w_q, v_w_kv, v_w_xo, v_norm_final_g):
    s = x.shape[1]
    dev = 4 * lax.axis_index("x") + 2 * lax.axis_index("y") + lax.axis_index("c")
    core = lax.axis_index("c").astype(jnp.int32).reshape(1)
    x2 = x[0]
    target = loss_target[0]
    pad_taps = lambda t: jnp.pad(t[0], ((0, 5), (0, 0)))

    big_names = ["w_in", "w_out", "w_q", "w_kv", "w_xo"]
    big_w = dict(w_in=w_in[0], w_out=w_out[0], w_q=w_q[0], w_kv=w_kv[0], w_xo=w_xo[0])
    big_m = dict(w_in=m_w_in[0], w_out=m_w_out[0], w_q=m_w_q[0], w_kv=m_w_kv[0], w_xo=m_w_xo[0])
    big_v = dict(w_in=v_w_in[0], w_out=v_w_out[0], w_q=v_w_q[0], w_kv=v_w_kv[0], w_xo=v_w_xo[0])
    shard_axis = dict(w_in=1, w_out=0, w_q=0, w_kv=1, w_xo=0)
    full_shape = dict(w_in=(D_MODEL, IN_DIM), w_out=(MIX_DIM, D_MODEL), w_q=(D_MODEL, D_MODEL),
                      w_kv=(D_MODEL, 2 * D_MODEL), w_xo=(D_MODEL, D_MODEL))
    shard_size = {k: big_w[k].shape[shard_axis[k]] for k in big_names}
    others = big_names[1:]
    gather_spec = lambda k: (full_shape[k], shard_axis[k], shard_size[k])
    reduce_spec = lambda k: (shard_axis[k], shard_size[k])
    ws = gm_ws[0]
    bs_b = jnp.broadcast_to(gm_bs[0][:, :, None], (8, CHUNK, CHUNK))
    g_f = norm_final_g.reshape(1, D_MODEL)

    px, py = lax.axis_index("x"), lax.axis_index("y")
    chip_order = jnp.stack([2 * px + py, 2 * (1 - px) + py, 2 * px + 1 - py, 2 * (1 - px) + 1 - py]).astype(jnp.int32)
    (proj, h1, h1t, w_in_full), ((w_out_full, conv_g),) = _inproj(
        x2, norm_mix_g, big_w["w_in"].astype(BF16), chip_order,
        plans=[_gather_plan([big_w["w_out"].astype(BF16), pad_taps(conv_w)[None]], [gather_spec("w_out"), ((N_DEV, 8, CHUNK), 0, 1)])])
    conv_full = conv_g.transpose(1, 0, 2).reshape(8, D_MODEL)
    attn_w = ["w_q", "w_kv", "w_xo"]
    cat, (gathered,) = _mixer_fwd(
        proj, conv_full, gm_ln_g, gm_ln_b, ws, bs_b,
        plans=[_gather_plan([big_w[k].astype(BF16) for k in attn_w], [gather_spec(k) for k in attn_w])])
    wf = dict(zip(attn_w, gathered), w_out=w_out_full)
    k, vv, mt = _kv_fwd(mem[0], norm_mem_g, wf["w_kv"])
    (dx1, dcat, catt, h2t, ot, dx2b, dqb, dx1b, dk, dvv, d_gf, d_gx, sq_err) = _attn(
        x2, cat, target, wf["w_out"], wf["w_q"], wf["w_xo"], k, vv, norm_x_g, g_f)

    def pair_sums(names, grads, got):
        return [_pair_sum("pair_sum_" + k, grads[k], got[i], *reduce_spec(k), core) for i, k in enumerate(names)]

    sibling_plan = lambda names, grads: _sibling_plan([grads[k] for k in names], [reduce_spec(k) for k in names])
    early, late = ["w_out", "w_kv", "w_q"], ["w_xo"]
    big_g = dict(w_out=_matmul_f32("dw_out", catt, dx1b)[0])
    big_g["w_kv"], d_gmem = _kv_bwd(mem[0], wf["w_kv"], mt, dk, dvv)
    big_g["w_q"], (got_a,) = _matmul_f32("dw_q", h2t, dqb, plans=[sibling_plan(early[:2], big_g)])
    big_g["w_xo"], (got_b,) = _matmul_f32("dw_xo", ot, dx2b, plans=[sibling_plan(early[2:], big_g)])
    got_early = got_a + got_b
    (dproj, d_cw, d_lng, d_lnb, d_ws, d_bs), (parts_early, got_late) = _mixer_bwd(
        proj, dcat, conv_full, gm_ln_g, gm_ln_b, ws, bs_b,
        plans=[_chip_plan(pair_sums(early, big_g, got_early)), sibling_plan(late, big_g)])
    half = D_MODEL // 2
    in_spec = reduce_spec("w_in")
    dw_top, (parts_late,) = _matmul_f32("dw_in_top", h1t, dproj, rows=(0, half), plans=[_chip_plan(pair_sums(late, big_g, got_late))])
    parts = dict(zip(early + late, parts_early + parts_late))
    dw_bot, (got_top, (ws_all,)) = _matmul_f32(
        "dw_in_bot", h1t, dproj, rows=(half, half),
        plans=[_sibling_plan([dw_top], [in_spec]),
               _gather_plan([d_ws.reshape(1, 8 * CHUNK, CHUNK)], [((N_DEV, 8 * CHUNK, CHUNK), 0, 1)])])
    pair_top = _pair_sum("pair_sum_w_in_top", dw_top, got_top[0], *in_spec, core)
    n_tiles = s // min(TM_DH1, s)
    cut = [0, (3 * n_tiles) // 8, (6 * n_tiles) // 8, n_tiles]
    (gx_a, d_gmix_a), (parts_top, got_bot) = _dh1(
        "dh1_a", dproj, w_in_full, x2, dx1, norm_mix_g, (cut[0], cut[1]),
        plans=[_chip_plan([pair_top], full_rows=[D_MODEL], row0=[0]), _sibling_plan([dw_bot], [in_spec])])
    pair_bot = _pair_sum("pair_sum_w_in_bot", dw_bot, got_bot[0], *in_spec, core)
    (gx_b, d_gmix_b), (parts_in,) = _dh1(
        "dh1_b", dproj, w_in_full, x2, dx1, norm_mix_g, (cut[1], cut[2]), into=gx_a,
        plans=[_chip_plan([pair_bot], full_rows=[D_MODEL], row0=[half], into=parts_top)])
    (grad_x, d_gmix_c), _ = _dh1("dh1_c", dproj, w_in_full, x2, dx1, norm_mix_g, (cut[2], cut[3]), into=gx_b)
    parts["w_in"] = parts_in[0]

    vec_names = ["norm_mix_g", "gm_ln_g", "gm_ln_b", "gm_bs", "norm_x_g", "norm_mem_g", "norm_final_g"]
    vec_g = [d_gmix_a + d_gmix_b + d_gmix_c, d_lng, d_lnb, d_bs, d_gx, d_gmem, d_gf]
    small = jnp.concatenate(vec_g + [sq_err, d_cw], axis=0)
    big_out = {}
    big_out["w_in"], ((small_all,),) = _adamw(
        "adamw_w_in", big_w["w_in"], big_m["w_in"], big_v["w_in"], parts["w_in"],
        plans=[_gather_plan([small[None]], [((N_DEV, 16, D_MODEL), 0, 1)])])
    loss = 0.5 * jnp.sum(small_all[:, 7, :]) / D_MODEL
    for k in others:
        big_out[k], _ = _adamw("adamw_" + k, big_w[k], big_m[k], big_v[k], parts[k])
    row = lambda t: t.reshape(1, D_MODEL)
    vec_w = [norm_mix_g, gm_ln_g, gm_ln_b, row(gm_bs), norm_x_g, norm_mem_g, row(norm_final_g)]
    vec_m = [m_norm_mix_g, m_gm_ln_g, m_gm_ln_b, row(m_gm_bs), m_norm_x_g, m_norm_mem_g, row(m_norm_final_g)]
    vec_v = [v_norm_mix_g, v_gm_ln_g, v_gm_ln_b, row(v_gm_bs), v_norm_x_g, v_norm_mem_g, row(v_norm_final_g)]
    stack8 = lambda ts: jnp.concatenate(ts + [jnp.ones((1, D_MODEL), F32)], axis=0)
    vec_out, _ = _adamw("adamw_vectors", stack8(vec_w), stack8(vec_m), stack8(vec_v), small_all[:, :8, :])
    conv_parts = lax.dynamic_slice(small_all, (0, 8, dev * CHUNK), (N_DEV, 8, CHUNK))
    conv_out, _ = _adamw("adamw_conv", pad_taps(conv_w), pad_taps(m_conv_w), jnp.pad(v_conv_w[0], ((0, 5), (0, 0)), constant_values=1.0), conv_parts)
    ws_out, _ = _adamw("adamw_gm_ws", ws.reshape(8 * CHUNK, CHUNK), m_gm_ws.reshape(8 * CHUNK, CHUNK), v_gm_ws.reshape(8 * CHUNK, CHUNK), ws_all)

    def result(name, which):
        if name in big_out:
            return big_out[name][which][None]
        if name == "conv_w":
            return conv_out[which][:3][None]
        if name == "gm_ws":
            return ws_out[which].reshape(1, 8, CHUNK, CHUNK)
        r = vec_out[which][vec_names.index(name)]
        if name == "gm_bs":
            return r.reshape(1, 8, CHUNK)
        if name == "norm_final_g":
            return r
        return r.reshape(1, D_MODEL)

    weights = ["norm_mix_g", "w_in", "conv_w", "gm_ln_g", "gm_ln_b", "gm_ws", "gm_bs", "w_out", "norm_x_g", "norm_mem_g", "w_q", "w_kv", "w_xo", "norm_final_g"]
    outs = [loss, grad_x[None]]
    for which in range(4):
        outs += [result(name, which) for name in weights]
    return tuple(outs)
```

```python
import math

import jax
import jax.numpy as jnp
from jax import lax
from jax.experimental import pallas as pl
from jax.experimental.pallas import tpu as pltpu

F32 = jnp.float32
BF16 = jnp.bfloat16
MESH = pl.DeviceIdType.MESH

D_MODEL = 1024
N_DEV = 8
N_CHIP = 4
CHUNK = 128
GROUPS = 7
IN_DIM = GROUPS * D_MODEL
SHARD_IN = IN_DIM // N_DEV
MIX_DIM = 2 * D_MODEL
X_HEADS = 4
X_HEAD_DIM = D_MODEL // X_HEADS
MEM_LEN = 256
EPS = 1e-6
GELU_K0 = math.sqrt(2.0 / math.pi)
GELU_K1 = 0.044715

ADAM_LR = 0.001
ADAM_B1 = 0.9
ADAM_B2 = 0.999
ADAM_EPS = 1e-08
ADAM_WD = 0.01
ADAM_STEP = 10

TM_INPROJ = 512
TM_ATTN = 256
TM_DH1 = 512
TM_DW = 512
TK_DW = 4096
TR_ELEMWISE = 256


def _dot(a, b):
    return lax.dot_general(a, b, (((1,), (0,)), ((), ())), preferred_element_type=F32)


def _dot_nt(a, b):
    return lax.dot_general(a, b, (((1,), (1,)), ((), ())), preferred_element_type=F32)


def _rows8(v):
    return jnp.sum(v.reshape(v.shape[0] // 8, 8, v.shape[1]), axis=0)


def _gelu(x):
    x2 = x * x
    t = jnp.tanh(GELU_K0 * (x + GELU_K1 * (x * x2)))
    g = 0.5 * x * (1.0 + t)
    dg = 0.5 * (1.0 + t) + 0.5 * x * (1.0 - t * t) * (GELU_K0 * (1.0 + 3.0 * GELU_K1 * x2))
    return g, dg


def _silu(z):
    s = jax.nn.sigmoid(z)
    return z * s, s * (1.0 + z * (1.0 - s))


def _rms(v):
    r = lax.rsqrt(jnp.mean(v * v, axis=-1, keepdims=True) + EPS)
    return v * r, r


def _rms_bwd(dy_g, vh, r):
    return r * (dy_g - vh * jnp.mean(dy_g * vh, axis=-1, keepdims=True))


def _place():
    return lax.axis_index("x"), lax.axis_index("y"), lax.axis_index("c")


def _other_chips(x, y):
    return [(1 - x, y), (x, 1 - y), (1 - x, 1 - y)]


def _pow2_divisor(n):
    return n & (-n)


def _shard_view(ref, axis, size, d):
    start = d * size
    align = _pow2_divisor(size)
    if align > 1:
        start = pl.multiple_of(start, align)
    idx = [slice(None)] * len(ref.shape)
    idx[axis] = pl.ds(start, size)
    return ref.at[tuple(idx)]


class _Plan:
    def __init__(self, ins, out_shape, sems, start, finish, mid=None, aliases=None):
        self.ins, self.out_shape, self.sems = list(ins), list(out_shape), list(sems)
        self.start, self.mid, self.finish = start, mid, finish
        self.aliases = dict(aliases or {})


def _gather_plan(shards, specs):
    n = len(shards)
    n_copy = 7

    def copies(ins, outs, sems):
        send_sems, recv_sems, local_sems = sems
        x, y, c = _place()
        me, sibling = (x, y, c), (x, y, 1 - c)
        chips = _other_chips(x, y)

        def block(a, p):
            _, axis, size = specs[a]
            return _shard_view(outs[a], axis, size, 4 * p[0] + 2 * p[1] + p[2])

        def copy(a, k, p, to, src=None):
            return pltpu.make_async_remote_copy(
                src_ref=block(a, p) if src is None else src,
                dst_ref=block(a, p),
                send_sem=send_sems.at[a * n_copy + k],
                recv_sem=recv_sems.at[a * n_copy + k],
                device_id=to,
                device_id_type=MESH,
            )

        def mine():
            return [pltpu.make_async_copy(ins[a], block(a, me), local_sems.at[a]) for a in range(n)]

        def first():
            return [cp for a in range(n) for cp in
                    [copy(a, 0, me, sibling, src=ins[a])] + [copy(a, 1 + j, me, (*chip, c), src=ins[a]) for j, chip in enumerate(chips)]]

        def landed():
            return [copy(a, 1 + j, (*chip, c), me) for j, chip in enumerate(chips) for a in range(n)]

        def passed():
            return [copy(a, 4 + j, (*chip, c), sibling) for j, chip in enumerate(chips) for a in range(n)]

        def last():
            return [cp for a in range(n) for cp in
                    [copy(a, 0, sibling, me)] + [copy(a, 4 + j, (*chip, 1 - c), me) for j, chip in enumerate(chips)]]

        return mine, first, landed, passed, last

    def start(ins, outs, sems):
        mine, first, _, _, _ = copies(ins, outs, sems)
        for cp in mine() + first():
            cp.start()

    def mid(ins, outs, sems):
        _, _, landed, passed, _ = copies(ins, outs, sems)
        for got, fwd in zip(landed(), passed()):
            got.wait_recv()
            fwd.start()

    def finish(ins, outs, sems):
        mine, first, _, passed, last = copies(ins, outs, sems)
        for cp in last():
            cp.wait_recv()
        for cp in first() + passed():
            cp.wait_send()
        for cp in mine():
            cp.wait()

    return _Plan(
        shards,
        [jax.ShapeDtypeStruct(full, s.dtype) for s, (full, _, _) in zip(shards, specs)],
        [pltpu.SemaphoreType.DMA((n * n_copy,)), pltpu.SemaphoreType.DMA((n * n_copy,)), pltpu.SemaphoreType.DMA((n,))],
        start, finish, mid,
    )


def _sibling_plan(grads, specs):
    n = len(grads)

    def shard_shape(a):
        axis, size = specs[a]
        shp = list(grads[a].shape)
        shp[axis] = size
        return tuple(shp)

    def copies(ins, outs, sems):
        send_sems, recv_sems = sems
        x, y, c = _place()
        return [
            pltpu.make_async_remote_copy(
                src_ref=_shard_view(ins[a], specs[a][0], specs[a][1], 2 * q + (1 - c)),
                dst_ref=outs[a].at[q],
                send_sem=send_sems.at[a * N_CHIP + q],
                recv_sem=recv_sems.at[a * N_CHIP + q],
                device_id=(x, y, 1 - c),
                device_id_type=MESH,
            )
            for a in range(n) for q in range(N_CHIP)
        ]

    def start(ins, outs, sems):
        for cp in copies(ins, outs, sems):
            cp.start()

    def finish(ins, outs, sems):
        for cp in copies(ins, outs, sems):
            cp.wait()

    return _Plan(
        grads,
        [jax.ShapeDtypeStruct((N_CHIP, *shard_shape(a)), grads[a].dtype) for a in range(n)],
        [pltpu.SemaphoreType.DMA((n * N_CHIP,)), pltpu.SemaphoreType.DMA((n * N_CHIP,))],
        start, finish,
    )


def _chip_plan(parts, full_rows=None, row0=None, into=None):
    n = len(parts)
    full_rows = [p.shape[1] for p in parts] if full_rows is None else full_rows
    row0 = [0] * n if row0 is None else row0

    def copies(ins, outs, sems):
        send_sems, recv_sems, local_sems = sems
        x, y, c = _place()
        my_chip = 2 * x + y
        chips = _other_chips(x, y)

        def land(a, q):
            return outs[a].at[q, pl.ds(row0[a], parts[a].shape[1]), :]

        def mine():
            return [pltpu.make_async_copy(ins[a].at[my_chip], land(a, my_chip), local_sems.at[a]) for a in range(n)]

        def remote(sending):
            return [
                pltpu.make_async_remote_copy(
                    src_ref=ins[a].at[2 * chip[0] + chip[1]],
                    dst_ref=land(a, my_chip if sending else 2 * chip[0] + chip[1]),
                    send_sem=send_sems.at[a * 3 + j], recv_sem=recv_sems.at[a * 3 + j],
                    device_id=(*chip, c), device_id_type=MESH)
                for a in range(n) for j, chip in enumerate(chips)
            ]

        return mine, remote

    def start(ins, outs, sems):
        mine, remote = copies(ins, outs, sems)
        for cp in mine() + remote(True):
            cp.start()

    def finish(ins, outs, sems):
        mine, remote = copies(ins, outs, sems)
        for cp in remote(False):
            cp.wait_recv()
        for cp in remote(True):
            cp.wait_send()
        for cp in mine():
            cp.wait()

    return _Plan(
        list(parts) + (list(into) if into is not None else []),
        [jax.ShapeDtypeStruct((N_CHIP, full_rows[a], parts[a].shape[2]), parts[a].dtype) for a in range(n)],
        [pltpu.SemaphoreType.DMA((n * 3,)), pltpu.SemaphoreType.DMA((n * 3,)), pltpu.SemaphoreType.DMA((n,))],
        start, finish,
        aliases={n + a: a for a in range(n)} if into is not None else None,
    )


def _call(body, *, name, operands, out_shape, grid=(), in_specs=None, out_specs=None, scratch_shapes=(),
          plans=(), aliases=None, prefetch=(), start_after_body=False, pass_on_at=0.6):
    operands, out_shape, scratch_shapes = list(operands), list(out_shape), list(scratch_shapes)
    n_pf, n_in, n_out, n_scr = len(prefetch), len(operands), len(out_shape), len(scratch_shapes)
    in_vmem = pl.BlockSpec(memory_space=pltpu.VMEM)
    in_hbm = pl.BlockSpec(memory_space=pl.ANY)
    in_specs = [in_vmem] * n_in if in_specs is None else list(in_specs)
    out_specs = [in_vmem] * n_out if out_specs is None else list(out_specs)
    io_alias = {n_pf + i: o for i, o in (aliases or {}).items()}
    pos_in, pos_out = n_pf + n_in, n_out
    for p in plans:
        for i, o in p.aliases.items():
            io_alias[pos_in + i] = pos_out + o
        pos_in += len(p.ins)
        pos_out += len(p.out_shape)
    steps = math.prod(grid) if grid else 1
    mid_step = min(steps - 1, int(steps * pass_on_at))

    def wrapped(*refs):
        refs = list(refs)
        take = lambda k: [refs.pop(0) for _ in range(k)]
        pf = take(n_pf)
        ins, p_ins = take(n_in), [take(len(p.ins)) for p in plans]
        outs, p_outs = take(n_out), [take(len(p.out_shape)) for p in plans]
        scr, p_sems = take(n_scr), [take(len(p.sems)) for p in plans]
        step = 0
        for ax, g in enumerate(grid):
            step = step * g + pl.program_id(ax)

        def hook(kind, at):
            todo = [(getattr(p, kind), a, b, c) for p, a, b, c in zip(plans, p_ins, p_outs, p_sems) if getattr(p, kind)]

            def run():
                for fn, a, b, c in todo:
                    fn(a, b, c)

            if todo and grid:
                pl.when(step == at)(run)
            elif todo:
                run()

        if not start_after_body:
            hook("start", 0)
        if body is not None:
            body(*pf, *ins, *outs, *scr)
        if start_after_body:
            hook("start", 0)
        hook("mid", mid_step)
        hook("finish", steps - 1)

    layout = dict(
        grid=grid,
        in_specs=in_specs + [in_hbm] * sum(len(p.ins) for p in plans),
        out_specs=out_specs + [in_hbm] * sum(len(p.out_shape) for p in plans),
        scratch_shapes=scratch_shapes + [s for p in plans for s in p.sems],
    )
    if n_pf:
        layout = dict(grid_spec=pltpu.PrefetchScalarGridSpec(num_scalar_prefetch=n_pf, **layout))
    res = pl.pallas_call(
        wrapped,
        name=name,
        out_shape=out_shape + [o for p in plans for o in p.out_shape],
        input_output_aliases=io_alias,
        compiler_params=pltpu.CompilerParams(dimension_semantics=("arbitrary",) * len(grid)) if grid else None,
        **layout,
    )(*prefetch, *operands, *[a for p in plans for a in p.ins])
    res = list(res)
    outs, plan_outs = res[:n_out], []
    pos = n_out
    for p in plans:
        plan_outs.append(res[pos : pos + len(p.out_shape)])
        pos += len(p.out_shape)
    return outs, plan_outs


def _inproj(x, g, w_shard, order, plans=()):
    s = x.shape[0]
    tm = min(TM_INPROJ, s)
    nt = s // tm
    wide = 2 * SHARD_IN
    fwd_at = nt // 2

    def body(order_ref, x_ref, g_ref, w_hbm, proj_ref, h1_ref, h1t_ref, wfull_hbm,
             w_scr, h1_scr, send_sems, recv_sems, local_sems):
        k, i = pl.program_id(0), pl.program_id(1)
        px, py, c = _place()
        me, sibling = (px, py, c), (px, py, 1 - c)
        chips = _other_chips(px, py)

        def half(slot, core):
            return w_scr.at[slot, :, pl.ds(pl.multiple_of(core * SHARD_IN, 128), SHARD_IN)]

        def copy(n, slot, core, to, src=None):
            return pltpu.make_async_remote_copy(
                src_ref=half(slot, core) if src is None else src, dst_ref=half(slot, core),
                send_sem=send_sems.at[n], recv_sem=recv_sems.at[n], device_id=to, device_id_type=MESH)

        def mine():
            return pltpu.make_async_copy(w_hbm, half(0, c), local_sems.at[0])

        def first():
            return [copy(0, 0, c, sibling, src=w_hbm)] + [copy(1 + j, 1 + j, c, (*chip, c), src=w_hbm) for j, chip in enumerate(chips)]

        def save(slot):
            cols = pl.ds(pl.multiple_of(order_ref[slot] * wide, 128), wide)
            return pltpu.make_async_copy(w_scr.at[slot], wfull_hbm.at[:, cols], local_sems.at[1 + slot])

        @pl.when((k == 0) & (i == 0))
        def _():
            mine().start()
            for cp in first():
                cp.start()
            mine().wait()
            copy(0, 0, 1 - c, me).wait_recv()

        for j, chip in enumerate(chips):
            @pl.when((k == j) & (i == fwd_at))
            def _():
                copy(1 + j, 1 + j, c, me).wait_recv()
                copy(4 + j, 1 + j, c, sibling).start()

            @pl.when((k == j + 1) & (i == 0))
            def _():
                copy(4 + j, 1 + j, 1 - c, me).wait_recv()

        @pl.when((k == 3) & (i == 0))
        def _():
            for slot in range(N_CHIP):
                save(slot).start()

        @pl.when(k == 0)
        def _():
            xh, _ = _rms(x_ref[...])
            h = xh * g_ref[...]
            h1_ref[...] = h.astype(BF16)
            h1t_ref[...] = h.T.astype(BF16)
            h1_scr[pl.ds(pl.multiple_of(i * tm, tm), tm), :] = h.astype(BF16)

        proj_ref[...] = _dot(h1_scr[pl.ds(pl.multiple_of(i * tm, tm), tm), :], w_scr[k]).astype(BF16)

        @pl.when((k == 3) & (i == nt - 1))
        def _():
            for cp in first() + [copy(4 + j, 1 + j, c, sibling) for j in range(3)]:
                cp.wait_send()
            for slot in range(N_CHIP):
                save(slot).wait()

    once = lambda k, i, order_ref: (jnp.where(k == 0, i, nt - 1), 0)
    return _call(
        body,
        name="inproj",
        prefetch=[order],
        operands=[x, g, w_shard],
        grid=(N_CHIP, nt),
        in_specs=[
            pl.BlockSpec((tm, D_MODEL), once),
            pl.BlockSpec((1, D_MODEL), lambda k, i, order_ref: (0, 0)),
            pl.BlockSpec(memory_space=pl.ANY),
        ],
        out_specs=[
            pl.BlockSpec((tm, wide), lambda k, i, order_ref: (i, order_ref[k])),
            pl.BlockSpec((tm, D_MODEL), once),
            pl.BlockSpec((D_MODEL, tm), lambda k, i, order_ref: (0, jnp.where(k == 0, i, nt - 1))),
            pl.BlockSpec(memory_space=pl.ANY),
        ],
        out_shape=[
            jax.ShapeDtypeStruct((s, IN_DIM), BF16),
            jax.ShapeDtypeStruct((s, D_MODEL), BF16),
            jax.ShapeDtypeStruct((D_MODEL, s), BF16),
            jax.ShapeDtypeStruct((D_MODEL, IN_DIM), BF16),
        ],
        scratch_shapes=[
            pltpu.VMEM((N_CHIP, D_MODEL, wide), BF16),
            pltpu.VMEM((s, D_MODEL), BF16),
            pltpu.SemaphoreType.DMA((7,)),
            pltpu.SemaphoreType.DMA((7,)),
            pltpu.SemaphoreType.DMA((1 + N_CHIP,)),
        ],
        plans=plans,
        start_after_body=True,
        pass_on_at=0.8,
    )


def _causal_mask():
    row = lax.broadcasted_iota(jnp.int32, (CHUNK, CHUNK), 0)
    col = lax.broadcasted_iota(jnp.int32, (CHUNK, CHUNK), 1)
    return row, row >= col


def _mixer_fwd(proj, conv_w, ln_g, ln_b, ws, bs_b, plans=()):
    s = proj.shape[0]
    nt = s // CHUNK

    def body(proj_ref, cw_ref, lng_ref, lnb_ref, ws_ref, bsb_ref, cat_ref, pcar_ref):
        @pl.when(pl.program_id(0) == 0)
        def _():
            pcar_ref[...] = jnp.zeros_like(pcar_ref)

        row, tril = _causal_mask()
        for j in range(8):
            cs = slice(CHUNK * j, CHUNK * (j + 1))

            def grp(k):
                return proj_ref[:, k * D_MODEL + CHUNK * j : k * D_MODEL + CHUNK * (j + 1)].astype(F32)

            w = cw_ref[:, cs]
            p = grp(1) * grp(2)
            pc = pcar_ref[:, cs]
            p1 = jnp.where(row == 0, pc[7:8], pltpu.roll(p, 1, 0))
            p2 = jnp.where(row == 0, pc[6:7], jnp.where(row == 1, pc[7:8], pltpu.roll(p, 2, 0)))
            pcar_ref[:, cs] = p[CHUNK - 8 :]
            cv = w[0:1] * p2 + w[1:2] * p1 + w[2:3] * p
            sa, _ = _silu(grp(3))
            cat_ref[:, cs] = ((grp(0) * cv) * sa).astype(BF16)
            gu, _ = _gelu(grp(4))
            gv, _ = _gelu(grp(5))
            dv = gv - jnp.mean(gv, axis=-1, keepdims=True)
            vn = dv * lax.rsqrt(jnp.mean(dv * dv, axis=-1, keepdims=True) + EPS)
            vn = vn * lng_ref[:, cs] + lnb_ref[:, cs]
            wc = jnp.where(tril, ws_ref[j], 0.0).astype(BF16)
            sp = _dot(wc, vn.astype(BF16)) + bsb_ref[j]
            sb, _ = _silu(grp(6))
            cat_ref[:, D_MODEL + CHUNK * j : D_MODEL + CHUNK * (j + 1)] = ((gu * sp) * sb).astype(BF16)

    whole = lambda shape: pl.BlockSpec(shape, lambda i: (0,) * len(shape))
    (cat,), plan_outs = _call(
        body,
        name="mixer_fwd",
        operands=[proj, conv_w, ln_g, ln_b, ws, bs_b],
        grid=(nt,),
        in_specs=[
            pl.BlockSpec((CHUNK, IN_DIM), lambda i: (i, 0)),
            whole((8, D_MODEL)),
            whole((1, D_MODEL)),
            whole((1, D_MODEL)),
            whole((8, CHUNK, CHUNK)),
            whole((8, CHUNK, CHUNK)),
        ],
        out_specs=[pl.BlockSpec((CHUNK, MIX_DIM), lambda i: (i, 0))],
        out_shape=[jax.ShapeDtypeStruct((s, MIX_DIM), BF16)],
        scratch_shapes=[pltpu.VMEM((8, D_MODEL), F32)],
        plans=plans,
        pass_on_at=0.85,
    )
    return cat, plan_outs


def _mixer_bwd(proj, dcat, conv_w, ln_g, ln_b, ws, bs_b, plans=()):
    s = proj.shape[0]
    nt = s // CHUNK

    def body(proj_ref, halo_ref, dcat_ref, cw_ref, lng_ref, lnb_ref, ws_ref, bsb_ref,
             dproj_ref, dcw_ref, dlng_ref, dlnb_ref, dws_ref, dbs_ref,
             car_ref, acc_cw, acc_lng, acc_lnb, acc_bs):
        i = pl.program_id(0)
        tile = nt - 1 - i

        @pl.when(i == 0)
        def _():
            car_ref[...] = jnp.zeros_like(car_ref)
            acc_cw[...] = jnp.zeros_like(acc_cw)
            acc_lng[...] = jnp.zeros_like(acc_lng)
            acc_lnb[...] = jnp.zeros_like(acc_lnb)
            acc_bs[...] = jnp.zeros_like(acc_bs)
            dws_ref[...] = jnp.zeros_like(dws_ref)

        row, tril = _causal_mask()
        has_prev = jnp.where(tile > 0, 1.0, 0.0).astype(F32)
        for j in range(8):
            cs = slice(CHUNK * j, CHUNK * (j + 1))

            def col(k):
                return slice(k * D_MODEL + CHUNK * j, k * D_MODEL + CHUNK * (j + 1))

            def grp(k):
                return proj_ref[:, col(k)].astype(F32)

            w = cw_ref[:, cs]
            gb, gc, xa = grp(0), grp(1), grp(2)
            p = gc * xa
            pprev = halo_ref[:, col(1)].astype(F32) * halo_ref[:, col(2)].astype(F32) * has_prev
            p1 = jnp.where(row == 0, pprev[15:16], pltpu.roll(p, 1, 0))
            p2 = jnp.where(row == 0, pprev[14:15], jnp.where(row == 1, pprev[15:16], pltpu.roll(p, 2, 0)))
            cv = w[0:1] * p2 + w[1:2] * p1 + w[2:3] * p
            za = grp(3)
            sa, dsa = _silu(za)
            da = dcat_ref[:, cs].astype(F32)
            dproj_ref[:, col(0)] = (da * cv * sa).astype(BF16)
            dproj_ref[:, col(3)] = (da * gb * cv * dsa).astype(BF16)
            dcv = da * gb * sa
            nxt = car_ref[:, cs]
            d1 = jnp.where(row == CHUNK - 1, nxt[0:1], pltpu.roll(dcv, CHUNK - 1, 0))
            d2 = jnp.where(row == CHUNK - 2, nxt[0:1], jnp.where(row == CHUNK - 1, nxt[1:2], pltpu.roll(dcv, CHUNK - 2, 0)))
            car_ref[:, cs] = dcv[0:8]
            dp = w[2:3] * dcv + w[1:2] * d1 + w[0:1] * d2
            dproj_ref[:, col(1)] = (dp * xa).astype(BF16)
            dproj_ref[:, col(2)] = (dp * gc).astype(BF16)
            acc_cw[0, :, cs] += _rows8(dcv * p2)
            acc_cw[1, :, cs] += _rows8(dcv * p1)
            acc_cw[2, :, cs] += _rows8(dcv * p)
            gu, dgu = _gelu(grp(4))
            gv, dgv = _gelu(grp(5))
            dv = gv - jnp.mean(gv, axis=-1, keepdims=True)
            rstd = lax.rsqrt(jnp.mean(dv * dv, axis=-1, keepdims=True) + EPS)
            vnh = dv * rstd
            lng = lng_ref[:, cs]
            vnb = (vnh * lng + lnb_ref[:, cs]).astype(BF16)
            wcf = jnp.where(tril, ws_ref[j], 0.0)
            sp = _dot(wcf.astype(BF16), vnb) + bsb_ref[j]
            zb = grp(6)
            sb, dsb = _silu(zb)
            db = dcat_ref[:, D_MODEL + CHUNK * j : D_MODEL + CHUNK * (j + 1)].astype(F32)
            dproj_ref[:, col(4)] = (db * sp * sb * dgu).astype(BF16)
            dproj_ref[:, col(6)] = (db * gu * sp * dsb).astype(BF16)
            dsp = db * gu * sb
            acc_bs[j] += dsp
            dspb = dsp.astype(BF16)
            dws_ref[j] += _dot_nt(dspb, vnb)
            dvn = _dot(wcf.T.astype(BF16), dspb)
            acc_lnb[:, cs] += _rows8(dvn)
            acc_lng[:, cs] += _rows8(dvn * vnh)
            dvh = dvn * lng
            dgvv = rstd * (dvh - jnp.mean(dvh, axis=-1, keepdims=True) - vnh * jnp.mean(dvh * vnh, axis=-1, keepdims=True))
            dproj_ref[:, col(5)] = (dgvv * dgv).astype(BF16)

        @pl.when(i == nt - 1)
        def _():
            dlng_ref[...] = jnp.sum(acc_lng[...], axis=0, keepdims=True)
            dlnb_ref[...] = jnp.sum(acc_lnb[...], axis=0, keepdims=True)
            dcw_ref[...] = jnp.zeros_like(dcw_ref)
            for k in range(3):
                dcw_ref[k : k + 1, :] = jnp.sum(acc_cw[k], axis=0, keepdims=True)
            for j in range(8):
                dws_ref[j] = jnp.where(tril, dws_ref[j], 0.0)
                dbs_ref[:, CHUNK * j : CHUNK * (j + 1)] = jnp.sum(acc_bs[j].T, axis=0, keepdims=True)

    whole = lambda shape: pl.BlockSpec(shape, lambda i: (0,) * len(shape))
    halo_rows = 16
    per = CHUNK // halo_rows
    return _call(
        body,
        name="mixer_bwd",
        operands=[proj, proj, dcat, conv_w, ln_g, ln_b, ws, bs_b],
        plans=plans,
        grid=(nt,),
        in_specs=[
            pl.BlockSpec((CHUNK, IN_DIM), lambda i: (nt - 1 - i, 0)),
            pl.BlockSpec((halo_rows, IN_DIM), lambda i: (jnp.maximum((nt - 1 - i) * per - 1, 0), 0)),
            pl.BlockSpec((CHUNK, MIX_DIM), lambda i: (nt - 1 - i, 0)),
            whole((8, D_MODEL)),
            whole((1, D_MODEL)),
            whole((1, D_MODEL)),
            whole((8, CHUNK, CHUNK)),
            whole((8, CHUNK, CHUNK)),
        ],
        out_specs=[
            pl.BlockSpec((CHUNK, IN_DIM), lambda i: (nt - 1 - i, 0)),
            whole((8, D_MODEL)),
            whole((1, D_MODEL)),
            whole((1, D_MODEL)),
            whole((8, CHUNK, CHUNK)),
            whole((1, D_MODEL)),
        ],
        out_shape=[
            jax.ShapeDtypeStruct((s, IN_DIM), BF16),
            jax.ShapeDtypeStruct((8, D_MODEL), F32),
            jax.ShapeDtypeStruct((1, D_MODEL), F32),
            jax.ShapeDtypeStruct((1, D_MODEL), F32),
            jax.ShapeDtypeStruct((8, CHUNK, CHUNK), F32),
            jax.ShapeDtypeStruct((1, D_MODEL), F32),
        ],
        scratch_shapes=[
            pltpu.VMEM((8, D_MODEL), F32),
            pltpu.VMEM((3, 8, D_MODEL), F32),
            pltpu.VMEM((8, D_MODEL), F32),
            pltpu.VMEM((8, D_MODEL), F32),
            pltpu.VMEM((8, CHUNK, CHUNK), F32),
        ],
    )


def _kv_fwd(mem, g_mem, w_kv):
    def body(mem_ref, g_ref, w_ref, k_ref, vv_ref, mt_ref):
        mh, _ = _rms(mem_ref[...])
        m = mh * g_ref[...]
        kv = _dot(m.astype(BF16), w_ref[...])
        k_ref[...] = kv[:, :D_MODEL].astype(BF16)
        vv_ref[...] = kv[:, D_MODEL:].astype(BF16)
        mt_ref[...] = m.T.astype(BF16)

    return pl.pallas_call(
        body,
        name="kv_fwd",
        out_shape=[
            jax.ShapeDtypeStruct((MEM_LEN, D_MODEL), BF16),
            jax.ShapeDtypeStruct((MEM_LEN, D_MODEL), BF16),
            jax.ShapeDtypeStruct((D_MODEL, MEM_LEN), BF16),
        ],
    )(mem, g_mem, w_kv)


def _kv_bwd(mem, w_kv, mt, dk, dvv):
    def body(mem_ref, w_ref, mt_ref, dk_ref, dvv_ref, dw_ref, dg_ref):
        dkv = jnp.concatenate([dk_ref[...], dvv_ref[...]], axis=1).astype(BF16)
        dw_ref[...] = _dot(mt_ref[...], dkv)
        dm = _dot_nt(dkv, w_ref[...])
        mh, _ = _rms(mem_ref[...])
        dg_ref[...] = jnp.sum(dm * mh, axis=0, keepdims=True)

    return pl.pallas_call(
        body,
        name="kv_bwd",
        out_shape=[
            jax.ShapeDtypeStruct((D_MODEL, 2 * D_MODEL), F32),
            jax.ShapeDtypeStruct((1, D_MODEL), F32),
        ],
    )(mem, w_kv, mt, dk, dvv)


def _attn(x, cat, target, w_out, w_q, w_xo, k, vv, g_x, g_f):
    s = x.shape[0]
    tm = min(TM_ATTN, s)
    nt = s // tm
    scale = 1.0 / math.sqrt(X_HEAD_DIM)

    def body(x_ref, cat_ref, t_ref, wout_ref, wq_ref, wxo_ref, k_ref, vv_ref, gx_ref, gf_ref,
             dx1_ref, dcat_ref, catt_ref, h2t_ref, ot_ref, dx2b_ref, dqb_ref, dx1b_ref,
             dk_ref, dvv_ref, dgf_ref, dgx_ref, loss_ref,
             p_scr, q_scr, o_scr, dq_scr, acc_gf, acc_gx, acc_loss):
        i = pl.program_id(0)

        @pl.when(i == 0)
        def _():
            dk_ref[...] = jnp.zeros_like(dk_ref)
            dvv_ref[...] = jnp.zeros_like(dvv_ref)
            acc_gf[...] = jnp.zeros_like(acc_gf)
            acc_gx[...] = jnp.zeros_like(acc_gx)
            acc_loss[...] = jnp.zeros_like(acc_loss)

        cat = cat_ref[...]
        catt_ref[...] = cat.astype(F32).T.astype(BF16)
        x1 = x_ref[...] + _dot(cat, wout_ref[...])
        x1h, r2 = _rms(x1)
        h2 = x1h * gx_ref[...]
        h2b = h2.astype(BF16)
        h2t_ref[...] = h2.T.astype(BF16)
        q_scr[...] = _dot(h2b, wq_ref[...]).astype(BF16)
        for h in range(X_HEADS):
            hs = slice(X_HEAD_DIM * h, X_HEAD_DIM * (h + 1))
            sc = _dot_nt(q_scr[:, hs], k_ref[:, hs]) * scale
            e = jnp.exp(sc - jnp.max(sc, axis=-1, keepdims=True))
            p = e / jnp.sum(e, axis=-1, keepdims=True)
            p_scr[:, hs] = p
            o_scr[:, hs] = _dot(p.astype(BF16), vv_ref[:, hs])
        o = o_scr[...]
        ot_ref[...] = o.T.astype(BF16)
        x2 = x1 + _dot(o.astype(BF16), wxo_ref[...])
        x2h, r3 = _rms(x2)
        gf = gf_ref[...]
        err = x2h * gf - t_ref[...]
        acc_loss[...] += _rows8(err * err)
        dy = err * (1.0 / D_MODEL)
        acc_gf[...] += _rows8(dy * x2h)
        dx2 = _rms_bwd(dy * gf, x2h, r3)
        dx2b = dx2.astype(BF16)
        dx2b_ref[...] = dx2b
        dob = _dot_nt(dx2b, wxo_ref[...]).astype(BF16)
        for h in range(X_HEADS):
            hs = slice(X_HEAD_DIM * h, X_HEAD_DIM * (h + 1))
            p = p_scr[:, hs]
            dp = _dot_nt(dob[:, hs], vv_ref[:, hs])
            ds = p * (dp - jnp.sum(dp * p, axis=-1, keepdims=True)) * scale
            dsb = ds.astype(BF16)
            dq_scr[:, hs] = _dot(dsb, k_ref[:, hs]).astype(BF16)
            dvv_ref[:, hs] += _dot(p.T.astype(BF16), dob[:, hs])
            dk_ref[:, hs] += _dot(ds.T.astype(BF16), q_scr[:, hs])
        dqb = dq_scr[...]
        dqb_ref[...] = dqb
        dh2 = _dot_nt(dqb, wq_ref[...])
        acc_gx[...] += _rows8(dh2 * x1h)
        dx1 = dx2 + _rms_bwd(dh2 * gx_ref[...], x1h, r2)
        dx1_ref[...] = dx1
        dx1b = dx1.astype(BF16)
        dx1b_ref[...] = dx1b
        dcat_ref[...] = _dot_nt(dx1b, wout_ref[...]).astype(BF16)

        @pl.when(i == nt - 1)
        def _():
            dgf_ref[...] = jnp.sum(acc_gf[...], axis=0, keepdims=True)
            dgx_ref[...] = jnp.sum(acc_gx[...], axis=0, keepdims=True)
            loss_ref[...] = jnp.sum(acc_loss[...], axis=0, keepdims=True)

    whole = lambda shape: pl.BlockSpec(shape, lambda i: (0,) * len(shape))
    rows = lambda width: pl.BlockSpec((tm, width), lambda i: (i, 0))
    cols = lambda height: pl.BlockSpec((height, tm), lambda i: (0, i))
    vec = jax.ShapeDtypeStruct((1, D_MODEL), F32)
    return pl.pallas_call(
        body,
        name="attn",
        grid=(nt,),
        in_specs=[
            rows(D_MODEL), rows(MIX_DIM), rows(D_MODEL),
            whole((MIX_DIM, D_MODEL)), whole((D_MODEL, D_MODEL)), whole((D_MODEL, D_MODEL)),
            whole((MEM_LEN, D_MODEL)), whole((MEM_LEN, D_MODEL)),
            whole((1, D_MODEL)), whole((1, D_MODEL)),
        ],
        out_specs=[
            rows(D_MODEL), rows(MIX_DIM), cols(MIX_DIM), cols(D_MODEL), cols(D_MODEL),
            rows(D_MODEL), rows(D_MODEL), rows(D_MODEL),
            whole((MEM_LEN, D_MODEL)), whole((MEM_LEN, D_MODEL)),
            whole((1, D_MODEL)), whole((1, D_MODEL)), whole((1, D_MODEL)),
        ],
        out_shape=[
            jax.ShapeDtypeStruct((s, D_MODEL), F32),
            jax.ShapeDtypeStruct((s, MIX_DIM), BF16),
            jax.ShapeDtypeStruct((MIX_DIM, s), BF16),
            jax.ShapeDtypeStruct((D_MODEL, s), BF16),
            jax.ShapeDtypeStruct((D_MODEL, s), BF16),
            jax.ShapeDtypeStruct((s, D_MODEL), BF16),
            jax.ShapeDtypeStruct((s, D_MODEL), BF16),
            jax.ShapeDtypeStruct((s, D_MODEL), BF16),
            jax.ShapeDtypeStruct((MEM_LEN, D_MODEL), F32),
            jax.ShapeDtypeStruct((MEM_LEN, D_MODEL), F32),
            vec, vec, vec,
        ],
        scratch_shapes=[
            pltpu.VMEM((tm, D_MODEL), F32),
            pltpu.VMEM((tm, D_MODEL), BF16),
            pltpu.VMEM((tm, D_MODEL), F32),
            pltpu.VMEM((tm, D_MODEL), BF16),
            pltpu.VMEM((8, D_MODEL), F32),
            pltpu.VMEM((8, D_MODEL), F32),
            pltpu.VMEM((8, D_MODEL), F32),
        ],
        compiler_params=pltpu.CompilerParams(dimension_semantics=("arbitrary",)),
    )(x, cat, target, w_out, w_q, w_xo, k, vv, g_x, g_f)


def _matmul_f32(name, a, b, rows=None, plans=()):
    first, m = (0, a.shape[0]) if rows is None else rows
    kk = a.shape[1]
    n = b.shape[1]
    tm, tn, tk = min(m, TM_DW), min(n, D_MODEL), min(kk, TK_DW)
    nk = kk // tk
    i0 = first // tm

    def body(a_ref, b_ref, o_ref):
        prod = _dot(a_ref[...], b_ref[...])
        if nk == 1:
            o_ref[...] = prod
            return
        kid = pl.program_id(2)

        @pl.when(kid == 0)
        def _():
            o_ref[...] = prod

        @pl.when(kid > 0)
        def _():
            o_ref[...] += prod

    (out,), plan_outs = _call(
        body,
        name=name,
        operands=[a, b],
        grid=(m // tm, n // tn, nk),
        in_specs=[pl.BlockSpec((tm, tk), lambda i, j, k: (i + i0, k)), pl.BlockSpec((tk, tn), lambda i, j, k: (k, j))],
        out_specs=[pl.BlockSpec((tm, tn), lambda i, j, k: (i, j))],
        out_shape=[jax.ShapeDtypeStruct((m, n), F32)],
        plans=plans,
    )
    return out, plan_outs


def _dh1(name, dproj, w_in, x, dx1, g, tiles, into=None, plans=()):
    s = x.shape[0]
    tm = min(TM_DH1, s)
    t0, nt = tiles[0], tiles[1] - tiles[0]

    def body(dp_ref, w_ref, x_ref, dx1_ref, g_ref, *rest):
        gx_ref, dg_ref, acc_g = rest[-3:]
        i = pl.program_id(0)

        @pl.when(i == 0)
        def _():
            acc_g[...] = jnp.zeros_like(acc_g)

        dh1 = _dot_nt(dp_ref[...], w_ref[...])
        xh, r = _rms(x_ref[...])
        acc_g[...] += _rows8(dh1 * xh)
        gx_ref[...] = dx1_ref[...] + _rms_bwd(dh1 * g_ref[...], xh, r)

        @pl.when(i == nt - 1)
        def _():
            dg_ref[...] = jnp.sum(acc_g[...], axis=0, keepdims=True)

    rows = pl.BlockSpec((tm, D_MODEL), lambda i: (i + t0, 0))
    return _call(
        body,
        name=name,
        operands=[dproj, w_in, x, dx1, g] + ([into] if into is not None else []),
        grid=(nt,),
        in_specs=[
            pl.BlockSpec((tm, IN_DIM), lambda i: (i + t0, 0)),
            pl.BlockSpec((D_MODEL, IN_DIM), lambda i: (0, 0), pipeline_mode=pl.Buffered(1)),
            rows,
            rows,
            pl.BlockSpec((1, D_MODEL), lambda i: (0, 0)),
        ] + ([pl.BlockSpec(memory_space=pl.ANY)] if into is not None else []),
        out_specs=[rows, pl.BlockSpec((1, D_MODEL), lambda i: (0, 0))],
        out_shape=[jax.ShapeDtypeStruct((s, D_MODEL), F32), jax.ShapeDtypeStruct((1, D_MODEL), F32)],
        scratch_shapes=[pltpu.VMEM((8, D_MODEL), F32)],
        plans=plans,
        aliases={5: 0} if into is not None else None,
    )


def _pair_sum(name, grad, got, axis, size, core, plans=()):
    shard = list(grad.shape)
    shard[axis] = size
    r, cdim = shard
    tr = min(r, TR_ELEMWISE)
    nr = r // tr

    def body(core_ref, g_ref, got_ref, o_ref):
        o_ref[0] = (g_ref[...] + got_ref[0]).astype(BF16)

    if axis == 1:
        g_map = lambda q, t, core_ref: (t, 2 * q + core_ref[0])
    else:
        g_map = lambda q, t, core_ref: ((2 * q + core_ref[0]) * nr + t, 0)
    (out,), plan_outs = _call(
        body,
        name=name,
        prefetch=[core],
        operands=[grad, got],
        grid=(N_CHIP, nr),
        in_specs=[
            pl.BlockSpec((tr, cdim), g_map),
            pl.BlockSpec((1, tr, cdim), lambda q, t, core_ref: (q, t, 0)),
        ],
        out_specs=[pl.BlockSpec((1, tr, cdim), lambda q, t, core_ref: (q, t, 0))],
        out_shape=[jax.ShapeDtypeStruct((N_CHIP, r, cdim), BF16)],
        plans=plans,
    )
    return (out, plan_outs) if plans else out


def _adamw(name, w, m, v, parts, plans=()):
    r, cdim = w.shape
    parts = list(parts) if isinstance(parts, (list, tuple)) else [parts]
    n_parts = parts[0].shape[0]
    tr = min([r, TR_ELEMWISE] + [p.shape[1] for p in parts])
    first_tile, tiles = [], 0
    for p in parts:
        first_tile.append(tiles)
        tiles += p.shape[1] // tr
    c1 = 1.0 / (1.0 - ADAM_B1**ADAM_STEP)
    c2 = 1.0 / (1.0 - ADAM_B2**ADAM_STEP)

    def body(w_ref, m_ref, v_ref, *refs):
        p_refs, (g_ref, d_ref, nm_ref, nv_ref) = refs[: len(parts)], refs[len(parts) :]
        t = pl.program_id(0)
        g = None
        for k, p_ref in enumerate(p_refs):
            gk = p_ref[0].astype(F32)
            for q in range(1, n_parts):
                gk = gk + p_ref[q].astype(F32)
            g = gk if g is None else jnp.where(t >= first_tile[k], gk, g)
        m_new = ADAM_B1 * m_ref[...] + (1.0 - ADAM_B1) * g
        v_new = ADAM_B2 * v_ref[...] + (1.0 - ADAM_B2) * (g * g)
        g_ref[...] = g
        nm_ref[...] = m_new
        nv_ref[...] = v_new
        d_ref[...] = -ADAM_LR * ((m_new * c1) / (jnp.sqrt(v_new * c2) + ADAM_EPS) + ADAM_WD * w_ref[...])

    blk = pl.BlockSpec((tr, cdim), lambda t: (t, 0))
    out = jax.ShapeDtypeStruct((r, cdim), F32)
    def part_spec(k):
        last = parts[k].shape[1] // tr - 1
        return pl.BlockSpec((n_parts, tr, cdim), lambda t: (0, jnp.clip(t - first_tile[k], 0, last), 0))

    return _call(
        body,
        name=name,
        operands=[w, m, v] + parts,
        grid=(r // tr,),
        in_specs=[blk, blk, blk] + [part_spec(k) for k in range(len(parts))],
        out_specs=[blk, blk, blk, blk],
        out_shape=[out, out, out, out],
        plans=plans,
    )


def kernel(x, mem, norm_mix_g, w_in, conv_w, gm_ln_g, gm_ln_b, gm_ws, gm_bs, w_out, norm_x_g, norm_mem_g, w_q, w_kv, w_xo, norm_final_g, loss_target, m_norm_mix_g, m_w_in, m_conv_w, m_gm_ln_g, m_gm_ln_b, m_gm_ws, m_gm_bs, m_w_out, m_norm_x_g, m_norm_mem_g, m_w_q, m_w_kv, m_w_xo, m_norm_final_g, v_norm_mix_g, v_w_in, v_conv_w, v_gm_ln_g, v_gm_ln_b, v_gm_ws, v_gm_bs, v_w_out, v_norm_x_g, v_norm_mem_g, v_w_q, v_w_kv, v_w_xo, v_norm_final_g):
    s = x.shape[1]
    dev = 4 * lax.axis_index("x") + 2 * lax.axis_index("y") + lax.axis_index("c")
    core = lax.axis_index("c").astype(jnp.int32).reshape(1)
    x2 = x[0]
    target = loss_target[0]
    pad_taps = lambda t: jnp.pad(t[0], ((0, 5), (0, 0)))

    big_names = ["w_in", "w_out", "w_q", "w_kv", "w_xo"]
    big_w = dict(w_in=w_in[0], w_out=w_out[0], w_q=w_q[0], w_kv=w_kv[0], w_xo=w_xo[0])
    big_m = dict(w_in=m_w_in[0], w_out=m_w_out[0], w_q=m_w_q[0], w_kv=m_w_kv[0], w_xo=m_w_xo[0])
    big_v = dict(w_in=v_w_in[0], w_out=v_w_out[0], w_q=v_w_q[0], w_kv=v_w_kv[0], w_xo=v_w_xo[0])
    shard_axis = dict(w_in=1, w_out=0, w_q=0, w_kv=1, w_xo=0)
    full_shape = dict(w_in=(D_MODEL, IN_DIM), w_out=(MIX_DIM, D_MODEL), w_q=(D_MODEL, D_MODEL),
                      w_kv=(D_MODEL, 2 * D_MODEL), w_xo=(D_MODEL, D_MODEL))
    shard_size = {k: big_w[k].shape[shard_axis[k]] for k in big_names}
    others = big_names[1:]
    gather_spec = lambda k: (full_shape[k], shard_axis[k], shard_size[k])
    reduce_spec = lambda k: (shard_axis[k], shard_size[k])
    ws = gm_ws[0]
    bs_b = jnp.broadcast_to(gm_bs[0][:, :, None], (8, CHUNK, CHUNK))
    g_f = norm_final_g.reshape(1, D_MODEL)

    px, py = lax.axis_index("x"), lax.axis_index("y")
    chip_order = jnp.stack([2 * px + py, 2 * (1 - px) + py, 2 * px + 1 - py, 2 * (1 - px) + 1 - py]).astype(jnp.int32)
    (proj, h1, h1t, w_in_full), ((w_out_full, conv_g),) = _inproj(
        x2, norm_mix_g, big_w["w_in"].astype(BF16), chip_order,
        plans=[_gather_plan([big_w["w_out"].astype(BF16), pad_taps(conv_w)[None]], [gather_spec("w_out"), ((N_DEV, 8, CHUNK), 0, 1)])])
    conv_full = conv_g.transpose(1, 0, 2).reshape(8, D_MODEL)
    attn_w = ["w_q", "w_kv", "w_xo"]
    cat, (gathered,) = _mixer_fwd(
        proj, conv_full, gm_ln_g, gm_ln_b, ws, bs_b,
        plans=[_gather_plan([big_w[k].astype(BF16) for k in attn_w], [gather_spec(k) for k in attn_w])])
    wf = dict(zip(attn_w, gathered), w_out=w_out_full)
    k, vv, mt = _kv_fwd(mem[0], norm_mem_g, wf["w_kv"])
    (dx1, dcat, catt, h2t, ot, dx2b, dqb, dx1b, dk, dvv, d_gf, d_gx, sq_err) = _attn(
        x2, cat, target, wf["w_out"], wf["w_q"], wf["w_xo"], k, vv, norm_x_g, g_f)

    def pair_sums(names, grads, got):
        return [_pair_sum("pair_sum_" + k, grads[k], got[i], *reduce_spec(k), core) for i, k in enumerate(names)]

    sibling_plan = lambda names, grads: _sibling_plan([grads[k] for k in names], [reduce_spec(k) for k in names])
    early, late = ["w_out", "w_kv", "w_q"], ["w_xo"]
    big_g = dict(w_out=_matmul_f32("dw_out", catt, dx1b)[0])
    big_g["w_kv"], d_gmem = _kv_bwd(mem[0], wf["w_kv"], mt, dk, dvv)
    big_g["w_q"], (got_a,) = _matmul_f32("dw_q", h2t, dqb, plans=[sibling_plan(early[:2], big_g)])
    big_g["w_xo"], (got_b,) = _matmul_f32("dw_xo", ot, dx2b, plans=[sibling_plan(early[2:], big_g)])
    got_early = got_a + got_b
    (dproj, d_cw, d_lng, d_lnb, d_ws, d_bs), (parts_early, got_late) = _mixer_bwd(
        proj, dcat, conv_full, gm_ln_g, gm_ln_b, ws, bs_b,
        plans=[_chip_plan(pair_sums(early, big_g, got_early)), sibling_plan(late, big_g)])
    half = D_MODEL // 2
    in_spec = reduce_spec("w_in")
    dw_top, (parts_late,) = _matmul_f32("dw_in_top", h1t, dproj, rows=(0, half), plans=[_chip_plan(pair_sums(late, big_g, got_late))])
    parts = dict(zip(early + late, parts_early + parts_late))
    dw_bot, (got_top, (ws_all,)) = _matmul_f32(
        "dw_in_bot", h1t, dproj, rows=(half, half),
        plans=[_sibling_plan([dw_top], [in_spec]),
               _gather_plan([d_ws.reshape(1, 8 * CHUNK, CHUNK)], [((N_DEV, 8 * CHUNK, CHUNK), 0, 1)])])
    pair_top, ((got_bot,),) = _pair_sum("pair_sum_w_in_top", dw_top, got_top[0], *in_spec, core,
                                        plans=[_sibling_plan([dw_bot], [in_spec])])
    pair_bot = _pair_sum("pair_sum_w_in_bot", dw_bot, got_bot, *in_spec, core)
    n_tiles = s // min(TM_DH1, s)
    (grad_x, d_gmix), (parts_in,) = _dh1(
        "dh1", dproj, w_in_full, x2, dx1, norm_mix_g, (0, n_tiles), plans=[_chip_plan([pair_top, pair_bot])])
    parts["w_in"] = parts_in

    vec_names = ["norm_mix_g", "gm_ln_g", "gm_ln_b", "gm_bs", "norm_x_g", "norm_mem_g", "norm_final_g"]
    vec_g = [d_gmix, d_lng, d_lnb, d_bs, d_gx, d_gmem, d_gf]
    small = jnp.concatenate(vec_g + [sq_err, d_cw], axis=0)
    big_out = {}
    big_out["w_in"], ((small_all,),) = _adamw(
        "adamw_w_in", big_w["w_in"], big_m["w_in"], big_v["w_in"], parts["w_in"],
        plans=[_gather_plan([small[None]], [((N_DEV, 16, D_MODEL), 0, 1)])])
    loss = 0.5 * jnp.sum(small_all[:, 7, :]) / D_MODEL
    for k in others:
        big_out[k], _ = _adamw("adamw_" + k, big_w[k], big_m[k], big_v[k], parts[k])
    row = lambda t: t.reshape(1, D_MODEL)
    vec_w = [norm_mix_g, gm_ln_g, gm_ln_b, row(gm_bs), norm_x_g, norm_mem_g, row(norm_final_g)]
    vec_m = [m_norm_mix_g, m_gm_ln_g, m_gm_ln_b, row(m_gm_bs), m_norm_x_g, m_norm_mem_g, row(m_norm_final_g)]
    vec_v = [v_norm_mix_g, v_gm_ln_g, v_gm_ln_b, row(v_gm_bs), v_norm_x_g, v_norm_mem_g, row(v_norm_final_g)]
    stack8 = lambda ts: jnp.concatenate(ts + [jnp.ones((1, D_MODEL), F32)], axis=0)
    vec_out, _ = _adamw("adamw_vectors", stack8(vec_w), stack8(vec_m), stack8(vec_v), small_all[:, :8, :])
    conv_parts = lax.dynamic_slice(small_all, (0, 8, dev * CHUNK), (N_DEV, 8, CHUNK))
    conv_out, _ = _adamw("adamw_conv", pad_taps(conv_w), pad_taps(m_conv_w), jnp.pad(v_conv_w[0], ((0, 5), (0, 0)), constant_values=1.0), conv_parts)
    ws_out, _ = _adamw("adamw_gm_ws", ws.reshape(8 * CHUNK, CHUNK), m_gm_ws.reshape(8 * CHUNK, CHUNK), v_gm_ws.reshape(8 * CHUNK, CHUNK), ws_all)

    def result(name, which):
        if name in big_out:
            return big_out[name][which][None]
        if name == "conv_w":
            return conv_out[which][:3][None]
        if name == "gm_ws":
            return ws_out[which].reshape(1, 8, CHUNK, CHUNK)
        r = vec_out[which][vec_names.index(name)]
        if name == "gm_bs":
            return r.reshape(1, 8, CHUNK)
        if name == "norm_final_g":
            return r
        return r.reshape(1, D_MODEL)

    weights = ["norm_mix_g", "w_in", "conv_w", "gm_ln_g", "gm_ln_b", "gm_ws", "gm_bs", "w_out", "norm_x_g", "norm_mem_g", "w_q", "w_kv", "w_xo", "norm_final_g"]
    outs = [loss, grad_x[None]]
    for which in range(4):
        outs += [result(name, which) for name in weights]
    return tuple(outs)
```

```python
import math

import jax
import jax.numpy as jnp
from jax import lax
from jax.experimental import pallas as pl
from jax.experimental.pallas import tpu as pltpu

F32 = jnp.float32
BF16 = jnp.bfloat16
MESH = pl.DeviceIdType.MESH

D_MODEL = 1024
N_DEV = 8
N_CHIP = 4
CHUNK = 128
GROUPS = 7
IN_DIM = GROUPS * D_MODEL
SHARD_IN = IN_DIM // N_DEV
MIX_DIM = 2 * D_MODEL
X_HEADS = 4
X_HEAD_DIM = D_MODEL // X_HEADS
MEM_LEN = 256
EPS = 1e-6
GELU_K0 = math.sqrt(2.0 / math.pi)
GELU_K1 = 0.044715

ADAM_LR = 0.001
ADAM_B1 = 0.9
ADAM_B2 = 0.999
ADAM_EPS = 1e-08
ADAM_WD = 0.01
ADAM_STEP = 10

TM_INPROJ = 512
TM_ATTN = 512
SUB_ATTN = 256
TM_DH1 = 512
TM_DW = 512
TK_DW = 4096
TR_ELEMWISE = 256
TR_PAIR_SUM = 1024


def _dot(a, b):
    return lax.dot_general(a, b, (((1,), (0,)), ((), ())), preferred_element_type=F32)


def _dot_nt(a, b):
    return lax.dot_general(a, b, (((1,), (1,)), ((), ())), preferred_element_type=F32)


def _rows8(v):
    return jnp.sum(v.reshape(v.shape[0] // 8, 8, v.shape[1]), axis=0)


def _gelu(x):
    x2 = x * x
    t = jnp.tanh(GELU_K0 * (x + GELU_K1 * (x * x2)))
    g = 0.5 * x * (1.0 + t)
    dg = 0.5 * (1.0 + t) + 0.5 * x * (1.0 - t * t) * (GELU_K0 * (1.0 + 3.0 * GELU_K1 * x2))
    return g, dg


def _silu(z):
    s = jax.nn.sigmoid(z)
    return z * s, s * (1.0 + z * (1.0 - s))


def _rms(v):
    r = lax.rsqrt(jnp.mean(v * v, axis=-1, keepdims=True) + EPS)
    return v * r, r


def _rms_bwd(dy_g, vh, r):
    return r * (dy_g - vh * jnp.mean(dy_g * vh, axis=-1, keepdims=True))


def _place():
    return lax.axis_index("x"), lax.axis_index("y"), lax.axis_index("c")


def _other_chips(x, y):
    return [(1 - x, y), (x, 1 - y), (1 - x, 1 - y)]


def _pow2_divisor(n):
    return n & (-n)


def _shard_view(ref, axis, size, d):
    start = d * size
    align = _pow2_divisor(size)
    if align > 1:
        start = pl.multiple_of(start, align)
    idx = [slice(None)] * len(ref.shape)
    idx[axis] = pl.ds(start, size)
    return ref.at[tuple(idx)]


class _Plan:
    def __init__(self, ins, out_shape, sems, start, finish, mid=None, aliases=None):
        self.ins, self.out_shape, self.sems = list(ins), list(out_shape), list(sems)
        self.start, self.mid, self.finish = start, mid, finish
        self.aliases = dict(aliases or {})


def _gather_plan(shards, specs):
    n = len(shards)
    n_copy = 7

    def copies(ins, outs, sems):
        send_sems, recv_sems, local_sems = sems
        x, y, c = _place()
        me, sibling = (x, y, c), (x, y, 1 - c)
        chips = _other_chips(x, y)

        def block(a, p):
            _, axis, size = specs[a]
            return _shard_view(outs[a], axis, size, 4 * p[0] + 2 * p[1] + p[2])

        def copy(a, k, p, to, src=None):
            return pltpu.make_async_remote_copy(
                src_ref=block(a, p) if src is None else src,
                dst_ref=block(a, p),
                send_sem=send_sems.at[a * n_copy + k],
                recv_sem=recv_sems.at[a * n_copy + k],
                device_id=to,
                device_id_type=MESH,
            )

        def mine():
            return [pltpu.make_async_copy(ins[a], block(a, me), local_sems.at[a]) for a in range(n)]

        def first():
            return [cp for a in range(n) for cp in
                    [copy(a, 0, me, sibling, src=ins[a])] + [copy(a, 1 + j, me, (*chip, c), src=ins[a]) for j, chip in enumerate(chips)]]

        def landed():
            return [copy(a, 1 + j, (*chip, c), me) for j, chip in enumerate(chips) for a in range(n)]

        def passed():
            return [copy(a, 4 + j, (*chip, c), sibling) for j, chip in enumerate(chips) for a in range(n)]

        def last():
            return [cp for a in range(n) for cp in
                    [copy(a, 0, sibling, me)] + [copy(a, 4 + j, (*chip, 1 - c), me) for j, chip in enumerate(chips)]]

        return mine, first, landed, passed, last

    def start(ins, outs, sems):
        mine, first, _, _, _ = copies(ins, outs, sems)
        for cp in mine() + first():
            cp.start()

    def mid(ins, outs, sems):
        _, _, landed, passed, _ = copies(ins, outs, sems)
        for got, fwd in zip(landed(), passed()):
            got.wait_recv()
            fwd.start()

    def finish(ins, outs, sems):
        mine, first, _, passed, last = copies(ins, outs, sems)
        for cp in last():
            cp.wait_recv()
        for cp in first() + passed():
            cp.wait_send()
        for cp in mine():
            cp.wait()

    return _Plan(
        shards,
        [jax.ShapeDtypeStruct(full, s.dtype) for s, (full, _, _) in zip(shards, specs)],
        [pltpu.SemaphoreType.DMA((n * n_copy,)), pltpu.SemaphoreType.DMA((n * n_copy,)), pltpu.SemaphoreType.DMA((n,))],
        start, finish, mid,
    )


def _sibling_plan(grads, specs):
    n = len(grads)

    def shard_shape(a):
        axis, size = specs[a]
        shp = list(grads[a].shape)
        shp[axis] = size
        return tuple(shp)

    def copies(ins, outs, sems):
        send_sems, recv_sems = sems
        x, y, c = _place()
        return [
            pltpu.make_async_remote_copy(
                src_ref=_shard_view(ins[a], specs[a][0], specs[a][1], 2 * q + (1 - c)),
                dst_ref=outs[a].at[q],
                send_sem=send_sems.at[a * N_CHIP + q],
                recv_sem=recv_sems.at[a * N_CHIP + q],
                device_id=(x, y, 1 - c),
                device_id_type=MESH,
            )
            for a in range(n) for q in range(N_CHIP)
        ]

    def start(ins, outs, sems):
        for cp in copies(ins, outs, sems):
            cp.start()

    def finish(ins, outs, sems):
        for cp in copies(ins, outs, sems):
            cp.wait()

    return _Plan(
        grads,
        [jax.ShapeDtypeStruct((N_CHIP, *shard_shape(a)), grads[a].dtype) for a in range(n)],
        [pltpu.SemaphoreType.DMA((n * N_CHIP,)), pltpu.SemaphoreType.DMA((n * N_CHIP,))],
        start, finish,
    )


def _chip_plan(parts, full_rows=None, row0=None, into=None):
    n = len(parts)
    full_rows = [p.shape[1] for p in parts] if full_rows is None else full_rows
    row0 = [0] * n if row0 is None else row0

    def copies(ins, outs, sems):
        send_sems, recv_sems, local_sems = sems
        x, y, c = _place()
        my_chip = 2 * x + y
        chips = _other_chips(x, y)

        def land(a, q):
            return outs[a].at[q, pl.ds(row0[a], parts[a].shape[1]), :]

        def mine():
            return [pltpu.make_async_copy(ins[a].at[my_chip], land(a, my_chip), local_sems.at[a]) for a in range(n)]

        def remote(sending):
            return [
                pltpu.make_async_remote_copy(
                    src_ref=ins[a].at[2 * chip[0] + chip[1]],
                    dst_ref=land(a, my_chip if sending else 2 * chip[0] + chip[1]),
                    send_sem=send_sems.at[a * 3 + j], recv_sem=recv_sems.at[a * 3 + j],
                    device_id=(*chip, c), device_id_type=MESH)
                for a in range(n) for j, chip in enumerate(chips)
            ]

        return mine, remote

    def start(ins, outs, sems):
        mine, remote = copies(ins, outs, sems)
        for cp in mine() + remote(True):
            cp.start()

    def finish(ins, outs, sems):
        mine, remote = copies(ins, outs, sems)
        for cp in remote(False):
            cp.wait_recv()
        for cp in remote(True):
            cp.wait_send()
        for cp in mine():
            cp.wait()

    return _Plan(
        list(parts) + (list(into) if into is not None else []),
        [jax.ShapeDtypeStruct((N_CHIP, full_rows[a], parts[a].shape[2]), parts[a].dtype) for a in range(n)],
        [pltpu.SemaphoreType.DMA((n * 3,)), pltpu.SemaphoreType.DMA((n * 3,)), pltpu.SemaphoreType.DMA((n,))],
        start, finish,
        aliases={n + a: a for a in range(n)} if into is not None else None,
    )


def _call(body, *, name, operands, out_shape, grid=(), in_specs=None, out_specs=None, scratch_shapes=(),
          plans=(), aliases=None, prefetch=(), start_after_body=False, pass_on_at=0.6):
    operands, out_shape, scratch_shapes = list(operands), list(out_shape), list(scratch_shapes)
    n_pf, n_in, n_out, n_scr = len(prefetch), len(operands), len(out_shape), len(scratch_shapes)
    in_vmem = pl.BlockSpec(memory_space=pltpu.VMEM)
    in_hbm = pl.BlockSpec(memory_space=pl.ANY)
    in_specs = [in_vmem] * n_in if in_specs is None else list(in_specs)
    out_specs = [in_vmem] * n_out if out_specs is None else list(out_specs)
    io_alias = {n_pf + i: o for i, o in (aliases or {}).items()}
    pos_in, pos_out = n_pf + n_in, n_out
    for p in plans:
        for i, o in p.aliases.items():
            io_alias[pos_in + i] = pos_out + o
        pos_in += len(p.ins)
        pos_out += len(p.out_shape)
    steps = math.prod(grid) if grid else 1
    mid_step = min(steps - 1, int(steps * pass_on_at))

    def wrapped(*refs):
        refs = list(refs)
        take = lambda k: [refs.pop(0) for _ in range(k)]
        pf = take(n_pf)
        ins, p_ins = take(n_in), [take(len(p.ins)) for p in plans]
        outs, p_outs = take(n_out), [take(len(p.out_shape)) for p in plans]
        scr, p_sems = take(n_scr), [take(len(p.sems)) for p in plans]
        step = 0
        for ax, g in enumerate(grid):
            step = step * g + pl.program_id(ax)

        def hook(kind, at):
            todo = [(getattr(p, kind), a, b, c) for p, a, b, c in zip(plans, p_ins, p_outs, p_sems) if getattr(p, kind)]

            def run():
                for fn, a, b, c in todo:
                    fn(a, b, c)

            if todo and grid:
                pl.when(step == at)(run)
            elif todo:
                run()

        if not start_after_body:
            hook("start", 0)
        if body is not None:
            body(*pf, *ins, *outs, *scr)
        if start_after_body:
            hook("start", 0)
        hook("mid", mid_step)
        hook("finish", steps - 1)

    layout = dict(
        grid=grid,
        in_specs=in_specs + [in_hbm] * sum(len(p.ins) for p in plans),
        out_specs=out_specs + [in_hbm] * sum(len(p.out_shape) for p in plans),
        scratch_shapes=scratch_shapes + [s for p in plans for s in p.sems],
    )
    if n_pf:
        layout = dict(grid_spec=pltpu.PrefetchScalarGridSpec(num_scalar_prefetch=n_pf, **layout))
    res = pl.pallas_call(
        wrapped,
        name=name,
        out_shape=out_shape + [o for p in plans for o in p.out_shape],
        input_output_aliases=io_alias,
        compiler_params=pltpu.CompilerParams(dimension_semantics=("arbitrary",) * len(grid)) if grid else None,
        **layout,
    )(*prefetch, *operands, *[a for p in plans for a in p.ins])
    res = list(res)
    outs, plan_outs = res[:n_out], []
    pos = n_out
    for p in plans:
        plan_outs.append(res[pos : pos + len(p.out_shape)])
        pos += len(p.out_shape)
    return outs, plan_outs


def _inproj(x, g, w_shard, order, plans=()):
    s = x.shape[0]
    tm = min(TM_INPROJ, s)
    nt = s // tm
    wide = 2 * SHARD_IN
    fwd_at = nt // 2

    def body(order_ref, x_ref, g_ref, w_hbm, proj_ref, h1_ref, h1t_ref, wfull_hbm,
             w_scr, h1_scr, send_sems, recv_sems, local_sems):
        k, i = pl.program_id(0), pl.program_id(1)
        px, py, c = _place()
        me, sibling = (px, py, c), (px, py, 1 - c)
        chips = _other_chips(px, py)

        def half(slot, core):
            return w_scr.at[slot, :, pl.ds(pl.multiple_of(core * SHARD_IN, 128), SHARD_IN)]

        def copy(n, slot, core, to, src=None):
            return pltpu.make_async_remote_copy(
                src_ref=half(slot, core) if src is None else src, dst_ref=half(slot, core),
                send_sem=send_sems.at[n], recv_sem=recv_sems.at[n], device_id=to, device_id_type=MESH)

        def mine():
            return pltpu.make_async_copy(w_hbm, half(0, c), local_sems.at[0])

        def first():
            return [copy(0, 0, c, sibling, src=w_hbm)] + [copy(1 + j, 1 + j, c, (*chip, c), src=w_hbm) for j, chip in enumerate(chips)]

        def save(slot):
            cols = pl.ds(pl.multiple_of(order_ref[slot] * wide, 128), wide)
            return pltpu.make_async_copy(w_scr.at[slot], wfull_hbm.at[:, cols], local_sems.at[1 + slot])

        @pl.when((k == 0) & (i == 0))
        def _():
            mine().start()
            for cp in first():
                cp.start()
            mine().wait()
            copy(0, 0, 1 - c, me).wait_recv()

        for j, chip in enumerate(chips):
            @pl.when((k == j) & (i == fwd_at))
            def _():
                copy(1 + j, 1 + j, c, me).wait_recv()
                copy(4 + j, 1 + j, c, sibling).start()

            @pl.when((k == j + 1) & (i == 0))
            def _():
                copy(4 + j, 1 + j, 1 - c, me).wait_recv()

        @pl.when((k == 3) & (i == 0))
        def _():
            for slot in range(N_CHIP):
                save(slot).start()

        @pl.when(k == 0)
        def _():
            xh, _ = _rms(x_ref[...])
            h = xh * g_ref[...]
            h1_ref[...] = h.astype(BF16)
            h1t_ref[...] = h.T.astype(BF16)
            h1_scr[pl.ds(pl.multiple_of(i * tm, tm), tm), :] = h.astype(BF16)

        proj_ref[...] = _dot(h1_scr[pl.ds(pl.multiple_of(i * tm, tm), tm), :], w_scr[k]).astype(BF16)

        @pl.when((k == 3) & (i == nt - 1))
        def _():
            for cp in first() + [copy(4 + j, 1 + j, c, sibling) for j in range(3)]:
                cp.wait_send()
            for slot in range(N_CHIP):
                save(slot).wait()

    once = lambda k, i, order_ref: (jnp.where(k == 0, i, nt - 1), 0)
    return _call(
        body,
        name="inproj",
        prefetch=[order],
        operands=[x, g, w_shard],
        grid=(N_CHIP, nt),
        in_specs=[
            pl.BlockSpec((tm, D_MODEL), once),
            pl.BlockSpec((1, D_MODEL), lambda k, i, order_ref: (0, 0)),
            pl.BlockSpec(memory_space=pl.ANY),
        ],
        out_specs=[
            pl.BlockSpec((tm, wide), lambda k, i, order_ref: (i, order_ref[k])),
            pl.BlockSpec((tm, D_MODEL), once),
            pl.BlockSpec((D_MODEL, tm), lambda k, i, order_ref: (0, jnp.where(k == 0, i, nt - 1))),
            pl.BlockSpec(memory_space=pl.ANY),
        ],
        out_shape=[
            jax.ShapeDtypeStruct((s, IN_DIM), BF16),
            jax.ShapeDtypeStruct((s, D_MODEL), BF16),
            jax.ShapeDtypeStruct((D_MODEL, s), BF16),
            jax.ShapeDtypeStruct((D_MODEL, IN_DIM), BF16),
        ],
        scratch_shapes=[
            pltpu.VMEM((N_CHIP, D_MODEL, wide), BF16),
            pltpu.VMEM((s, D_MODEL), BF16),
            pltpu.SemaphoreType.DMA((7,)),
            pltpu.SemaphoreType.DMA((7,)),
            pltpu.SemaphoreType.DMA((1 + N_CHIP,)),
        ],
        plans=plans,
        start_after_body=True,
        pass_on_at=0.8,
    )


def _causal_mask():
    row = lax.broadcasted_iota(jnp.int32, (CHUNK, CHUNK), 0)
    col = lax.broadcasted_iota(jnp.int32, (CHUNK, CHUNK), 1)
    return row, row >= col


def _mixer_fwd(proj, conv_w, ln_g, ln_b, ws, bs_b, plans=()):
    s = proj.shape[0]
    nt = s // CHUNK

    def body(proj_ref, cw_ref, lng_ref, lnb_ref, ws_ref, bsb_ref, cat_ref, pcar_ref):
        @pl.when(pl.program_id(0) == 0)
        def _():
            pcar_ref[...] = jnp.zeros_like(pcar_ref)

        row, tril = _causal_mask()
        for j in range(8):
            cs = slice(CHUNK * j, CHUNK * (j + 1))

            def grp(k):
                return proj_ref[:, k * D_MODEL + CHUNK * j : k * D_MODEL + CHUNK * (j + 1)].astype(F32)

            w = cw_ref[:, cs]
            p = grp(1) * grp(2)
            pc = pcar_ref[:, cs]
            p1 = jnp.where(row == 0, pc[7:8], pltpu.roll(p, 1, 0))
            p2 = jnp.where(row == 0, pc[6:7], jnp.where(row == 1, pc[7:8], pltpu.roll(p, 2, 0)))
            pcar_ref[:, cs] = p[CHUNK - 8 :]
            cv = w[0:1] * p2 + w[1:2] * p1 + w[2:3] * p
            sa, _ = _silu(grp(3))
            cat_ref[:, cs] = ((grp(0) * cv) * sa).astype(BF16)
            gu, _ = _gelu(grp(4))
            gv, _ = _gelu(grp(5))
            dv = gv - jnp.mean(gv, axis=-1, keepdims=True)
            vn = dv * lax.rsqrt(jnp.mean(dv * dv, axis=-1, keepdims=True) + EPS)
            vn = vn * lng_ref[:, cs] + lnb_ref[:, cs]
            wc = jnp.where(tril, ws_ref[j], 0.0).astype(BF16)
            sp = _dot(wc, vn.astype(BF16)) + bsb_ref[j]
            sb, _ = _silu(grp(6))
            cat_ref[:, D_MODEL + CHUNK * j : D_MODEL + CHUNK * (j + 1)] = ((gu * sp) * sb).astype(BF16)

    whole = lambda shape: pl.BlockSpec(shape, lambda i: (0,) * len(shape))
    (cat,), plan_outs = _call(
        body,
        name="mixer_fwd",
        operands=[proj, conv_w, ln_g, ln_b, ws, bs_b],
        grid=(nt,),
        in_specs=[
            pl.BlockSpec((CHUNK, IN_DIM), lambda i: (i, 0)),
            whole((8, D_MODEL)),
            whole((1, D_MODEL)),
            whole((1, D_MODEL)),
            whole((8, CHUNK, CHUNK)),
            whole((8, CHUNK, CHUNK)),
        ],
        out_specs=[pl.BlockSpec((CHUNK, MIX_DIM), lambda i: (i, 0))],
        out_shape=[jax.ShapeDtypeStruct((s, MIX_DIM), BF16)],
        scratch_shapes=[pltpu.VMEM((8, D_MODEL), F32)],
        plans=plans,
        pass_on_at=0.85,
    )
    return cat, plan_outs


def _mixer_bwd(proj, dcat, conv_w, ln_g, ln_b, ws, bs_b, plans=()):
    s = proj.shape[0]
    nt = s // CHUNK

    def body(proj_ref, halo_ref, dcat_ref, cw_ref, lng_ref, lnb_ref, ws_ref, bsb_ref,
             dproj_ref, dcw_ref, dlng_ref, dlnb_ref, dws_ref, dbs_ref,
             car_ref, acc_cw, acc_lng, acc_lnb, acc_bs):
        i = pl.program_id(0)
        tile = nt - 1 - i

        @pl.when(i == 0)
        def _():
            car_ref[...] = jnp.zeros_like(car_ref)
            acc_cw[...] = jnp.zeros_like(acc_cw)
            acc_lng[...] = jnp.zeros_like(acc_lng)
            acc_lnb[...] = jnp.zeros_like(acc_lnb)
            acc_bs[...] = jnp.zeros_like(acc_bs)
            dws_ref[...] = jnp.zeros_like(dws_ref)

        row, tril = _causal_mask()
        has_prev = jnp.where(tile > 0, 1.0, 0.0).astype(F32)
        for j in range(8):
            cs = slice(CHUNK * j, CHUNK * (j + 1))

            def col(k):
                return slice(k * D_MODEL + CHUNK * j, k * D_MODEL + CHUNK * (j + 1))

            def grp(k):
                return proj_ref[:, col(k)].astype(F32)

            w = cw_ref[:, cs]
            gb, gc, xa = grp(0), grp(1), grp(2)
            p = gc * xa
            pprev = halo_ref[:, col(1)].astype(F32) * halo_ref[:, col(2)].astype(F32) * has_prev
            p1 = jnp.where(row == 0, pprev[15:16], pltpu.roll(p, 1, 0))
            p2 = jnp.where(row == 0, pprev[14:15], jnp.where(row == 1, pprev[15:16], pltpu.roll(p, 2, 0)))
            cv = w[0:1] * p2 + w[1:2] * p1 + w[2:3] * p
            za = grp(3)
            sa, dsa = _silu(za)
            da = dcat_ref[:, cs].astype(F32)
            dproj_ref[:, col(0)] = (da * cv * sa).astype(BF16)
            dproj_ref[:, col(3)] = (da * gb * cv * dsa).astype(BF16)
            dcv = da * gb * sa
            nxt = car_ref[:, cs]
            d1 = jnp.where(row == CHUNK - 1, nxt[0:1], pltpu.roll(dcv, CHUNK - 1, 0))
            d2 = jnp.where(row == CHUNK - 2, nxt[0:1], jnp.where(row == CHUNK - 1, nxt[1:2], pltpu.roll(dcv, CHUNK - 2, 0)))
            car_ref[:, cs] = dcv[0:8]
            dp = w[2:3] * dcv + w[1:2] * d1 + w[0:1] * d2
            dproj_ref[:, col(1)] = (dp * xa).astype(BF16)
            dproj_ref[:, col(2)] = (dp * gc).astype(BF16)
            acc_cw[0, :, cs] += _rows8(dcv * p2)
            acc_cw[1, :, cs] += _rows8(dcv * p1)
            acc_cw[2, :, cs] += _rows8(dcv * p)
            gu, dgu = _gelu(grp(4))
            gv, dgv = _gelu(grp(5))
            dv = gv - jnp.mean(gv, axis=-1, keepdims=True)
            rstd = lax.rsqrt(jnp.mean(dv * dv, axis=-1, keepdims=True) + EPS)
            vnh = dv * rstd
            lng = lng_ref[:, cs]
            vnb = (vnh * lng + lnb_ref[:, cs]).astype(BF16)
            wcf = jnp.where(tril, ws_ref[j], 0.0)
            sp = _dot(wcf.astype(BF16), vnb) + bsb_ref[j]
            zb = grp(6)
            sb, dsb = _silu(zb)
            db = dcat_ref[:, D_MODEL + CHUNK * j : D_MODEL + CHUNK * (j + 1)].astype(F32)
            dproj_ref[:, col(4)] = (db * sp * sb * dgu).astype(BF16)
            dproj_ref[:, col(6)] = (db * gu * sp * dsb).astype(BF16)
            dsp = db * gu * sb
            acc_bs[j] += dsp
            dspb = dsp.astype(BF16)
            dws_ref[j] += _dot_nt(dspb, vnb)
            dvn = _dot(wcf.T.astype(BF16), dspb)
            acc_lnb[:, cs] += _rows8(dvn)
            acc_lng[:, cs] += _rows8(dvn * vnh)
            dvh = dvn * lng
            dgvv = rstd * (dvh - jnp.mean(dvh, axis=-1, keepdims=True) - vnh * jnp.mean(dvh * vnh, axis=-1, keepdims=True))
            dproj_ref[:, col(5)] = (dgvv * dgv).astype(BF16)

        @pl.when(i == nt - 1)
        def _():
            dlng_ref[...] = jnp.sum(acc_lng[...], axis=0, keepdims=True)
            dlnb_ref[...] = jnp.sum(acc_lnb[...], axis=0, keepdims=True)
            dcw_ref[...] = jnp.zeros_like(dcw_ref)
            for k in range(3):
                dcw_ref[k : k + 1, :] = jnp.sum(acc_cw[k], axis=0, keepdims=True)
            for j in range(8):
                dws_ref[j] = jnp.where(tril, dws_ref[j], 0.0)
                dbs_ref[:, CHUNK * j : CHUNK * (j + 1)] = jnp.sum(acc_bs[j].T, axis=0, keepdims=True)

    whole = lambda shape: pl.BlockSpec(shape, lambda i: (0,) * len(shape))
    halo_rows = 16
    per = CHUNK // halo_rows
    return _call(
        body,
        name="mixer_bwd",
        operands=[proj, proj, dcat, conv_w, ln_g, ln_b, ws, bs_b],
        plans=plans,
        grid=(nt,),
        in_specs=[
            pl.BlockSpec((CHUNK, IN_DIM), lambda i: (nt - 1 - i, 0)),
            pl.BlockSpec((halo_rows, IN_DIM), lambda i: (jnp.maximum((nt - 1 - i) * per - 1, 0), 0)),
            pl.BlockSpec((CHUNK, MIX_DIM), lambda i: (nt - 1 - i, 0)),
            whole((8, D_MODEL)),
            whole((1, D_MODEL)),
            whole((1, D_MODEL)),
            whole((8, CHUNK, CHUNK)),
            whole((8, CHUNK, CHUNK)),
        ],
        out_specs=[
            pl.BlockSpec((CHUNK, IN_DIM), lambda i: (nt - 1 - i, 0)),
            whole((8, D_MODEL)),
            whole((1, D_MODEL)),
            whole((1, D_MODEL)),
            whole((8, CHUNK, CHUNK)),
            whole((1, D_MODEL)),
        ],
        out_shape=[
            jax.ShapeDtypeStruct((s, IN_DIM), BF16),
            jax.ShapeDtypeStruct((8, D_MODEL), F32),
            jax.ShapeDtypeStruct((1, D_MODEL), F32),
            jax.ShapeDtypeStruct((1, D_MODEL), F32),
            jax.ShapeDtypeStruct((8, CHUNK, CHUNK), F32),
            jax.ShapeDtypeStruct((1, D_MODEL), F32),
        ],
        scratch_shapes=[
            pltpu.VMEM((8, D_MODEL), F32),
            pltpu.VMEM((3, 8, D_MODEL), F32),
            pltpu.VMEM((8, D_MODEL), F32),
            pltpu.VMEM((8, D_MODEL), F32),
            pltpu.VMEM((8, CHUNK, CHUNK), F32),
        ],
    )


def _kv_fwd(mem, g_mem, w_kv):
    def body(mem_ref, g_ref, w_ref, k_ref, vv_ref, mt_ref):
        mh, _ = _rms(mem_ref[...])
        m = mh * g_ref[...]
        kv = _dot(m.astype(BF16), w_ref[...])
        k_ref[...] = kv[:, :D_MODEL].astype(BF16)
        vv_ref[...] = kv[:, D_MODEL:].astype(BF16)
        mt_ref[...] = m.T.astype(BF16)

    return pl.pallas_call(
        body,
        name="kv_fwd",
        out_shape=[
            jax.ShapeDtypeStruct((MEM_LEN, D_MODEL), BF16),
            jax.ShapeDtypeStruct((MEM_LEN, D_MODEL), BF16),
            jax.ShapeDtypeStruct((D_MODEL, MEM_LEN), BF16),
        ],
    )(mem, g_mem, w_kv)


def _kv_bwd(mem, w_kv, mt, dk, dvv):
    def body(mem_ref, w_ref, mt_ref, dk_ref, dvv_ref, dw_ref, dg_ref):
        dkv = jnp.concatenate([dk_ref[...], dvv_ref[...]], axis=1).astype(BF16)
        dw_ref[...] = _dot(mt_ref[...], dkv)
        dm = _dot_nt(dkv, w_ref[...])
        mh, _ = _rms(mem_ref[...])
        dg_ref[...] = jnp.sum(dm * mh, axis=0, keepdims=True)

    return pl.pallas_call(
        body,
        name="kv_bwd",
        out_shape=[
            jax.ShapeDtypeStruct((D_MODEL, 2 * D_MODEL), F32),
            jax.ShapeDtypeStruct((1, D_MODEL), F32),
        ],
    )(mem, w_kv, mt, dk, dvv)


def _attn(x, cat, target, w_out, w_q, w_xo, k, vv, g_x, g_f):
    s = x.shape[0]
    tm = min(TM_ATTN, s)
    sub = min(SUB_ATTN, tm)
    nt = s // tm
    scale = 1.0 / math.sqrt(X_HEAD_DIM)

    def body(x_ref, cat_ref, t_ref, wout_ref, wq_ref, wxo_ref, k_ref, vv_ref, gx_ref, gf_ref,
             dcat_ref, catt_ref, h2t_ref, ot_ref, dx2b_ref, dqb_ref, dx1b_ref,
             dk_ref, dvv_ref, dgf_ref, dgx_ref, loss_ref,
             p_scr, q_scr, o_scr, dq_scr, acc_gf, acc_gx, acc_loss):
        i = pl.program_id(0)

        @pl.when(i == 0)
        def _():
            dk_ref[...] = jnp.zeros_like(dk_ref)
            dvv_ref[...] = jnp.zeros_like(dvv_ref)
            acc_gf[...] = jnp.zeros_like(acc_gf)
            acc_gx[...] = jnp.zeros_like(acc_gx)
            acc_loss[...] = jnp.zeros_like(acc_loss)

        subs = [slice(u * sub, (u + 1) * sub) for u in range(tm // sub)]
        heads = [slice(X_HEAD_DIM * h, X_HEAD_DIM * (h + 1)) for h in range(X_HEADS)]
        gx, gf = gx_ref[...], gf_ref[...]
        x1, x1h, r2 = [], [], []
        for rs in subs:
            cat = cat_ref[rs, :]
            catt_ref[:, rs] = cat.astype(F32).T.astype(BF16)
            v = x_ref[rs, :] + _dot(cat, wout_ref[...])
            vh, r = _rms(v)
            h2 = vh * gx
            h2t_ref[:, rs] = h2.T.astype(BF16)
            q_scr[rs, :] = _dot(h2.astype(BF16), wq_ref[...]).astype(BF16)
            x1.append(v), x1h.append(vh), r2.append(r)
        for rs in subs:
            for hs in heads:
                sc = _dot_nt(q_scr[rs, hs], k_ref[:, hs]) * scale
                e = jnp.exp(sc - jnp.max(sc, axis=-1, keepdims=True))
                p = e / jnp.sum(e, axis=-1, keepdims=True)
                p_scr[rs, hs] = p
                o_scr[rs, hs] = _dot(p.astype(BF16), vv_ref[:, hs])
        dx2, sq, d_gf = [], None, None
        for u, rs in enumerate(subs):
            o = o_scr[rs, :]
            ot_ref[:, rs] = o.T.astype(BF16)
            x2 = x1[u] + _dot(o.astype(BF16), wxo_ref[...])
            x2h, r3 = _rms(x2)
            err = x2h * gf - t_ref[rs, :]
            dy = err * (1.0 / D_MODEL)
            sq = _rows8(err * err) if sq is None else sq + _rows8(err * err)
            d_gf = _rows8(dy * x2h) if d_gf is None else d_gf + _rows8(dy * x2h)
            dx2.append(_rms_bwd(dy * gf, x2h, r3))
        acc_loss[...] += sq
        acc_gf[...] += d_gf
        d_vv, d_k = [None] * X_HEADS, [None] * X_HEADS
        for u, rs in enumerate(subs):
            dx2b = dx2[u].astype(BF16)
            dx2b_ref[rs, :] = dx2b
            dob = _dot_nt(dx2b, wxo_ref[...]).astype(BF16)
            for h, hs in enumerate(heads):
                p = p_scr[rs, hs]
                dp = _dot_nt(dob[:, hs], vv_ref[:, hs])
                ds = p * (dp - jnp.sum(dp * p, axis=-1, keepdims=True)) * scale
                dq_scr[rs, hs] = _dot(ds.astype(BF16), k_ref[:, hs]).astype(BF16)
                dvv_h = _dot(p.T.astype(BF16), dob[:, hs])
                dk_h = _dot(ds.T.astype(BF16), q_scr[rs, hs])
                d_vv[h] = dvv_h if d_vv[h] is None else d_vv[h] + dvv_h
                d_k[h] = dk_h if d_k[h] is None else d_k[h] + dk_h
        for h, hs in enumerate(heads):
            dvv_ref[:, hs] += d_vv[h]
            dk_ref[:, hs] += d_k[h]
        d_gx = None
        for u, rs in enumerate(subs):
            dqb = dq_scr[rs, :]
            dqb_ref[rs, :] = dqb
            dh2 = _dot_nt(dqb, wq_ref[...])
            d_gx = _rows8(dh2 * x1h[u]) if d_gx is None else d_gx + _rows8(dh2 * x1h[u])
            dx1 = dx2[u] + _rms_bwd(dh2 * gx, x1h[u], r2[u])
            dx1b = dx1.astype(BF16)
            dx1b_ref[rs, :] = dx1b
            dcat_ref[rs, :] = _dot_nt(dx1b, wout_ref[...]).astype(BF16)
        acc_gx[...] += d_gx

        @pl.when(i == nt - 1)
        def _():
            dgf_ref[...] = jnp.sum(acc_gf[...], axis=0, keepdims=True)
            dgx_ref[...] = jnp.sum(acc_gx[...], axis=0, keepdims=True)
            loss_ref[...] = jnp.sum(acc_loss[...], axis=0, keepdims=True)

    whole = lambda shape: pl.BlockSpec(shape, lambda i: (0,) * len(shape))
    once = lambda shape: pl.BlockSpec(shape, lambda i: (0,) * len(shape), pipeline_mode=pl.Buffered(1))
    rows = lambda width: pl.BlockSpec((tm, width), lambda i: (i, 0))
    cols = lambda height: pl.BlockSpec((height, tm), lambda i: (0, i))
    vec = jax.ShapeDtypeStruct((1, D_MODEL), F32)
    return pl.pallas_call(
        body,
        name="attn",
        grid=(nt,),
        in_specs=[
            rows(D_MODEL), rows(MIX_DIM), rows(D_MODEL),
            once((MIX_DIM, D_MODEL)), once((D_MODEL, D_MODEL)), once((D_MODEL, D_MODEL)),
            once((MEM_LEN, D_MODEL)), once((MEM_LEN, D_MODEL)),
            whole((1, D_MODEL)), whole((1, D_MODEL)),
        ],
        out_specs=[
            rows(MIX_DIM), cols(MIX_DIM), cols(D_MODEL), cols(D_MODEL),
            rows(D_MODEL), rows(D_MODEL), rows(D_MODEL),
            whole((MEM_LEN, D_MODEL)), whole((MEM_LEN, D_MODEL)),
            whole((1, D_MODEL)), whole((1, D_MODEL)), whole((1, D_MODEL)),
        ],
        out_shape=[
            jax.ShapeDtypeStruct((s, MIX_DIM), BF16),
            jax.ShapeDtypeStruct((MIX_DIM, s), BF16),
            jax.ShapeDtypeStruct((D_MODEL, s), BF16),
            jax.ShapeDtypeStruct((D_MODEL, s), BF16),
            jax.ShapeDtypeStruct((s, D_MODEL), BF16),
            jax.ShapeDtypeStruct((s, D_MODEL), BF16),
            jax.ShapeDtypeStruct((s, D_MODEL), BF16),
            jax.ShapeDtypeStruct((MEM_LEN, D_MODEL), F32),
            jax.ShapeDtypeStruct((MEM_LEN, D_MODEL), F32),
            vec, vec, vec,
        ],
        scratch_shapes=[
            pltpu.VMEM((tm, D_MODEL), F32),
            pltpu.VMEM((tm, D_MODEL), BF16),
            pltpu.VMEM((tm, D_MODEL), F32),
            pltpu.VMEM((tm, D_MODEL), BF16),
            pltpu.VMEM((8, D_MODEL), F32),
            pltpu.VMEM((8, D_MODEL), F32),
            pltpu.VMEM((8, D_MODEL), F32),
        ],
        compiler_params=pltpu.CompilerParams(dimension_semantics=("arbitrary",)),
    )(x, cat, target, w_out, w_q, w_xo, k, vv, g_x, g_f)


def _matmul_f32(name, a, b, rows=None, plans=()):
    first, m = (0, a.shape[0]) if rows is None else rows
    kk = a.shape[1]
    n = b.shape[1]
    tm, tn, tk = min(m, TM_DW), min(n, D_MODEL), min(kk, TK_DW)
    nk = kk // tk
    i0 = first // tm

    def body(a_ref, b_ref, o_ref):
        prod = _dot(a_ref[...], b_ref[...])
        if nk == 1:
            o_ref[...] = prod
            return
        kid = pl.program_id(2)

        @pl.when(kid == 0)
        def _():
            o_ref[...] = prod

        @pl.when(kid > 0)
        def _():
            o_ref[...] += prod

    (out,), plan_outs = _call(
        body,
        name=name,
        operands=[a, b],
        grid=(m // tm, n // tn, nk),
        in_specs=[pl.BlockSpec((tm, tk), lambda i, j, k: (i + i0, k)), pl.BlockSpec((tk, tn), lambda i, j, k: (k, j))],
        out_specs=[pl.BlockSpec((tm, tn), lambda i, j, k: (i, j))],
        out_shape=[jax.ShapeDtypeStruct((m, n), F32)],
        plans=plans,
    )
    return out, plan_outs


def _dh1(name, dproj, w_in, x, dx1, g, tiles, into=None, plans=()):
    s = x.shape[0]
    tm = min(TM_DH1, s)
    t0, nt = tiles[0], tiles[1] - tiles[0]

    def body(dp_ref, w_ref, x_ref, dx1_ref, g_ref, *rest):
        gx_ref, dg_ref, acc_g = rest[-3:]
        i = pl.program_id(0)

        @pl.when(i == 0)
        def _():
            acc_g[...] = jnp.zeros_like(acc_g)

        dh1 = _dot_nt(dp_ref[...], w_ref[...])
        xh, r = _rms(x_ref[...])
        acc_g[...] += _rows8(dh1 * xh)
        gx_ref[...] = dx1_ref[...].astype(F32) + _rms_bwd(dh1 * g_ref[...], xh, r)

        @pl.when(i == nt - 1)
        def _():
            dg_ref[...] = jnp.sum(acc_g[...], axis=0, keepdims=True)

    rows = pl.BlockSpec((tm, D_MODEL), lambda i: (i + t0, 0))
    return _call(
        body,
        name=name,
        operands=[dproj, w_in, x, dx1, g] + ([into] if into is not None else []),
        grid=(nt,),
        in_specs=[
            pl.BlockSpec((tm, IN_DIM), lambda i: (i + t0, 0)),
            pl.BlockSpec((D_MODEL, IN_DIM), lambda i: (0, 0), pipeline_mode=pl.Buffered(1)),
            rows,
            rows,
            pl.BlockSpec((1, D_MODEL), lambda i: (0, 0)),
        ] + ([pl.BlockSpec(memory_space=pl.ANY)] if into is not None else []),
        out_specs=[rows, pl.BlockSpec((1, D_MODEL), lambda i: (0, 0))],
        out_shape=[jax.ShapeDtypeStruct((s, D_MODEL), F32), jax.ShapeDtypeStruct((1, D_MODEL), F32)],
        scratch_shapes=[pltpu.VMEM((8, D_MODEL), F32)],
        plans=plans,
        aliases={5: 0} if into is not None else None,
    )


def _pair_sum(name, grad, got, axis, size, core, plans=()):
    shard = list(grad.shape)
    shard[axis] = size
    r, cdim = shard
    tr = min(r, TR_PAIR_SUM)
    nr = r // tr

    def body(core_ref, g_ref, got_ref, o_ref):
        o_ref[0] = (g_ref[...] + got_ref[0]).astype(BF16)

    if axis == 1:
        g_map = lambda q, t, core_ref: (t, 2 * q + core_ref[0])
    else:
        g_map = lambda q, t, core_ref: ((2 * q + core_ref[0]) * nr + t, 0)
    (out,), plan_outs = _call(
        body,
        name=name,
        prefetch=[core],
        operands=[grad, got],
        grid=(N_CHIP, nr),
        in_specs=[
            pl.BlockSpec((tr, cdim), g_map),
            pl.BlockSpec((1, tr, cdim), lambda q, t, core_ref: (q, t, 0)),
        ],
        out_specs=[pl.BlockSpec((1, tr, cdim), lambda q, t, core_ref: (q, t, 0))],
        out_shape=[jax.ShapeDtypeStruct((N_CHIP, r, cdim), BF16)],
        plans=plans,
    )
    return (out, plan_outs) if plans else out


def _adamw(name, w, m, v, parts, plans=()):
    r, cdim = w.shape
    parts = list(parts) if isinstance(parts, (list, tuple)) else [parts]
    n_parts = parts[0].shape[0]
    tr = min([r, TR_ELEMWISE] + [p.shape[1] for p in parts])
    first_tile, tiles = [], 0
    for p in parts:
        first_tile.append(tiles)
        tiles += p.shape[1] // tr
    c1 = 1.0 / (1.0 - ADAM_B1**ADAM_STEP)
    c2 = 1.0 / (1.0 - ADAM_B2**ADAM_STEP)

    def body(w_ref, m_ref, v_ref, *refs):
        p_refs, (g_ref, d_ref, nm_ref, nv_ref) = refs[: len(parts)], refs[len(parts) :]
        t = pl.program_id(0)
        g = None
        for k, p_ref in enumerate(p_refs):
            gk = p_ref[0].astype(F32)
            for q in range(1, n_parts):
                gk = gk + p_ref[q].astype(F32)
            g = gk if g is None else jnp.where(t >= first_tile[k], gk, g)
        m_new = ADAM_B1 * m_ref[...] + (1.0 - ADAM_B1) * g
        v_new = ADAM_B2 * v_ref[...] + (1.0 - ADAM_B2) * (g * g)
        g_ref[...] = g
        nm_ref[...] = m_new
        nv_ref[...] = v_new
        d_ref[...] = -ADAM_LR * ((m_new * c1) / (jnp.sqrt(v_new * c2) + ADAM_EPS) + ADAM_WD * w_ref[...])

    blk = pl.BlockSpec((tr, cdim), lambda t: (t, 0))
    out = jax.ShapeDtypeStruct((r, cdim), F32)
    def part_spec(k):
        last = parts[k].shape[1] // tr - 1
        return pl.BlockSpec((n_parts, tr, cdim), lambda t: (0, jnp.clip(t - first_tile[k], 0, last), 0))

    return _call(
        body,
        name=name,
        operands=[w, m, v] + parts,
        grid=(r // tr,),
        in_specs=[blk, blk, blk] + [part_spec(k) for k in range(len(parts))],
        out_specs=[blk, blk, blk, blk],
        out_shape=[out, out, out, out],
        plans=plans,
    )


def kernel(x, mem, norm_mix_g, w_in, conv_w, gm_ln_g, gm_ln_b, gm_ws, gm_bs, w_out, norm_x_g, norm_mem_g, w_q, w_kv, w_xo, norm_final_g, loss_target, m_norm_mix_g, m_w_in, m_conv_w, m_gm_ln_g, m_gm_ln_b, m_gm_ws, m_gm_bs, m_w_out, m_norm_x_g, m_norm_mem_g, m_w_q, m_w_kv, m_w_xo, m_norm_final_g, v_norm_mix_g, v_w_in, v_conv_w, v_gm_ln_g, v_gm_ln_b, v_gm_ws, v_gm_bs, v_w_out, v_norm_x_g, v_norm_mem_g, v_w_q, v_w_kv, v_w_xo, v_norm_final_g):
    s = x.shape[1]
    dev = 4 * lax.axis_index("x") + 2 * lax.axis_index("y") + lax.axis_index("c")
    core = lax.axis_index("c").astype(jnp.int32).reshape(1)
    x2 = x[0]
    target = loss_target[0]
    pad_taps = lambda t: jnp.pad(t[0], ((0, 5), (0, 0)))

    big_names = ["w_in", "w_out", "w_q", "w_kv", "w_xo"]
    big_w = dict(w_in=w_in[0], w_out=w_out[0], w_q=w_q[0], w_kv=w_kv[0], w_xo=w_xo[0])
    big_m = dict(w_in=m_w_in[0], w_out=m_w_out[0], w_q=m_w_q[0], w_kv=m_w_kv[0], w_xo=m_w_xo[0])
    big_v = dict(w_in=v_w_in[0], w_out=v_w_out[0], w_q=v_w_q[0], w_kv=v_w_kv[0], w_xo=v_w_xo[0])
    shard_axis = dict(w_in=1, w_out=0, w_q=0, w_kv=1, w_xo=0)
    full_shape = dict(w_in=(D_MODEL, IN_DIM), w_out=(MIX_DIM, D_MODEL), w_q=(D_MODEL, D_MODEL),
                      w_kv=(D_MODEL, 2 * D_MODEL), w_xo=(D_MODEL, D_MODEL))
    shard_size = {k: big_w[k].shape[shard_axis[k]] for k in big_names}
    others = big_names[1:]
    gather_spec = lambda k: (full_shape[k], shard_axis[k], shard_size[k])
    reduce_spec = lambda k: (shard_axis[k], shard_size[k])
    ws = gm_ws[0]
    bs_b = jnp.broadcast_to(gm_bs[0][:, :, None], (8, CHUNK, CHUNK))
    g_f = norm_final_g.reshape(1, D_MODEL)

    px, py = lax.axis_index("x"), lax.axis_index("y")
    chip_order = jnp.stack([2 * px + py, 2 * (1 - px) + py, 2 * px + 1 - py, 2 * (1 - px) + 1 - py]).astype(jnp.int32)
    (proj, h1, h1t, w_in_full), ((w_out_full, conv_g),) = _inproj(
        x2, norm_mix_g, big_w["w_in"].astype(BF16), chip_order,
        plans=[_gather_plan([big_w["w_out"].astype(BF16), pad_taps(conv_w)[None]], [gather_spec("w_out"), ((N_DEV, 8, CHUNK), 0, 1)])])
    conv_full = conv_g.transpose(1, 0, 2).reshape(8, D_MODEL)
    attn_w = ["w_q", "w_kv", "w_xo"]
    cat, (gathered,) = _mixer_fwd(
        proj, conv_full, gm_ln_g, gm_ln_b, ws, bs_b,
        plans=[_gather_plan([big_w[k].astype(BF16) for k in attn_w], [gather_spec(k) for k in attn_w])])
    wf = dict(zip(attn_w, gathered), w_out=w_out_full)
    k, vv, mt = _kv_fwd(mem[0], norm_mem_g, wf["w_kv"])
    (dcat, catt, h2t, ot, dx2b, dqb, dx1b, dk, dvv, d_gf, d_gx, sq_err) = _attn(
        x2, cat, target, wf["w_out"], wf["w_q"], wf["w_xo"], k, vv, norm_x_g, g_f)

    def pair_sums(names, grads, got):
        return [_pair_sum("pair_sum_" + k, grads[k], got[i], *reduce_spec(k), core) for i, k in enumerate(names)]

    sibling_plan = lambda names, grads: _sibling_plan([grads[k] for k in names], [reduce_spec(k) for k in names])
    early, late = ["w_out", "w_kv", "w_q"], ["w_xo"]
    big_g = dict(w_out=_matmul_f32("dw_out", catt, dx1b)[0])
    big_g["w_kv"], d_gmem = _kv_bwd(mem[0], wf["w_kv"], mt, dk, dvv)
    big_g["w_q"], (got_a,) = _matmul_f32("dw_q", h2t, dqb, plans=[sibling_plan(early[:2], big_g)])
    big_g["w_xo"], (got_b,) = _matmul_f32("dw_xo", ot, dx2b, plans=[sibling_plan(early[2:], big_g)])
    got_early = got_a + got_b
    (dproj, d_cw, d_lng, d_lnb, d_ws, d_bs), (parts_early, got_late) = _mixer_bwd(
        proj, dcat, conv_full, gm_ln_g, gm_ln_b, ws, bs_b,
        plans=[_chip_plan(pair_sums(early, big_g, got_early)), sibling_plan(late, big_g)])
    half = D_MODEL // 2
    in_spec = reduce_spec("w_in")
    dw_top, (parts_late,) = _matmul_f32("dw_in_top", h1t, dproj, rows=(0, half), plans=[_chip_plan(pair_sums(late, big_g, got_late))])
    parts = dict(zip(early + late, parts_early + parts_late))
    dw_bot, (got_top, (ws_all,)) = _matmul_f32(
        "dw_in_bot", h1t, dproj, rows=(half, half),
        plans=[_sibling_plan([dw_top], [in_spec]),
               _gather_plan([d_ws.reshape(1, 8 * CHUNK, CHUNK)], [((N_DEV, 8 * CHUNK, CHUNK), 0, 1)])])
    pair_top, ((got_bot,),) = _pair_sum("pair_sum_w_in_top", dw_top, got_top[0], *in_spec, core,
                                        plans=[_sibling_plan([dw_bot], [in_spec])])
    pair_bot = _pair_sum("pair_sum_w_in_bot", dw_bot, got_bot, *in_spec, core)
    n_tiles = s // min(TM_DH1, s)
    (grad_x, d_gmix), (parts_in,) = _dh1(
        "dh1", dproj, w_in_full, x2, dx1b, norm_mix_g, (0, n_tiles), plans=[_chip_plan([pair_top, pair_bot])])
    parts["w_in"] = parts_in

    vec_names = ["norm_mix_g", "gm_ln_g", "gm_ln_b", "gm_bs", "norm_x_g", "norm_mem_g", "norm_final_g"]
    vec_g = [d_gmix, d_lng, d_lnb, d_bs, d_gx, d_gmem, d_gf]
    small = jnp.concatenate(vec_g + [sq_err, d_cw], axis=0)
    big_out = {}
    big_out["w_in"], ((small_all,),) = _adamw(
        "adamw_w_in", big_w["w_in"], big_m["w_in"], big_v["w_in"], parts["w_in"],
        plans=[_gather_plan([small[None]], [((N_DEV, 16, D_MODEL), 0, 1)])])
    loss = 0.5 * jnp.sum(small_all[:, 7, :]) / D_MODEL
    for k in others:
        big_out[k], _ = _adamw("adamw_" + k, big_w[k], big_m[k], big_v[k], parts[k])
    row = lambda t: t.reshape(1, D_MODEL)
    vec_w = [norm_mix_g, gm_ln_g, gm_ln_b, row(gm_bs), norm_x_g, norm_mem_g, row(norm_final_g)]
    vec_m = [m_norm_mix_g, m_gm_ln_g, m_gm_ln_b, row(m_gm_bs), m_norm_x_g, m_norm_mem_g, row(m_norm_final_g)]
    vec_v = [v_norm_mix_g, v_gm_ln_g, v_gm_ln_b, row(v_gm_bs), v_norm_x_g, v_norm_mem_g, row(v_norm_final_g)]
    stack8 = lambda ts: jnp.concatenate(ts + [jnp.ones((1, D_MODEL), F32)], axis=0)
    vec_out, _ = _adamw("adamw_vectors", stack8(vec_w), stack8(vec_m), stack8(vec_v), small_all[:, :8, :])
    conv_parts = lax.dynamic_slice(small_all, (0, 8, dev * CHUNK), (N_DEV, 8, CHUNK))
    conv_out, _ = _adamw("adamw_conv", pad_taps(conv_w), pad_taps(m_conv_w), jnp.pad(v_conv_w[0], ((0, 5), (0, 0)), constant_values=1.0), conv_parts)
    ws_out, _ = _adamw("adamw_gm_ws", ws.reshape(8 * CHUNK, CHUNK), m_gm_ws.reshape(8 * CHUNK, CHUNK), v_gm_ws.reshape(8 * CHUNK, CHUNK), ws_all)

    def result(name, which):
        if name in big_out:
            return big_out[name][which][None]
        if name == "conv_w":
            return conv_out[which][:3][None]
        if name == "gm_ws":
            return ws_out[which].reshape(1, 8, CHUNK, CHUNK)
        r = vec_out[which][vec_names.index(name)]
        if name == "gm_bs":
            return r.reshape(1, 8, CHUNK)
        if name == "norm_final_g":
            return r
        return r.reshape(1, D_MODEL)

    weights = ["norm_mix_g", "w_in", "conv_w", "gm_ln_g", "gm_ln_b", "gm_ws", "gm_bs", "w_out", "norm_x_g", "norm_mem_g", "w_q", "w_kv", "w_xo", "norm_final_g"]
    outs = [loss, grad_x[None]]
    for which in range(4):
        outs += [result(name, which) for name in weights]
    return tuple(outs)
```

```python
import math

import jax
import jax.numpy as jnp
from jax import lax
from jax.experimental import pallas as pl
from jax.experimental.pallas import tpu as pltpu

F32 = jnp.float32
BF16 = jnp.bfloat16
MESH = pl.DeviceIdType.MESH

D_MODEL = 1024
N_DEV = 8
N_CHIP = 4
CHUNK = 128
GROUPS = 7
IN_DIM = GROUPS * D_MODEL
SHARD_IN = IN_DIM // N_DEV
MIX_DIM = 2 * D_MODEL
X_HEADS = 4
X_HEAD_DIM = D_MODEL // X_HEADS
MEM_LEN = 256
EPS = 1e-6
GELU_K0 = math.sqrt(2.0 / math.pi)
GELU_K1 = 0.044715

ADAM_LR = 0.001
ADAM_B1 = 0.9
ADAM_B2 = 0.999
ADAM_EPS = 1e-08
ADAM_WD = 0.01
ADAM_STEP = 10

TM_INPROJ = 512
TM_ATTN = 512
SUB_ATTN = 256
TM_DH1 = 512
TM_DW = 512
TK_DW = 4096
TR_ELEMWISE = 256
TR_PAIR_SUM = 1024


def _dot(a, b):
    return lax.dot_general(a, b, (((1,), (0,)), ((), ())), preferred_element_type=F32)


def _dot_nt(a, b):
    return lax.dot_general(a, b, (((1,), (1,)), ((), ())), preferred_element_type=F32)


def _rows8(v):
    return jnp.sum(v.reshape(v.shape[0] // 8, 8, v.shape[1]), axis=0)


def _gelu(x):
    x2 = x * x
    t = jnp.tanh(GELU_K0 * (x + GELU_K1 * (x * x2)))
    g = 0.5 * x * (1.0 + t)
    dg = 0.5 * (1.0 + t) + 0.5 * x * (1.0 - t * t) * (GELU_K0 * (1.0 + 3.0 * GELU_K1 * x2))
    return g, dg


def _silu(z):
    s = jax.nn.sigmoid(z)
    return z * s, s * (1.0 + z * (1.0 - s))


def _rms(v):
    r = lax.rsqrt(jnp.mean(v * v, axis=-1, keepdims=True) + EPS)
    return v * r, r


def _rms_bwd(dy_g, vh, r):
    return r * (dy_g - vh * jnp.mean(dy_g * vh, axis=-1, keepdims=True))


def _place():
    return lax.axis_index("x"), lax.axis_index("y"), lax.axis_index("c")


def _other_chips(x, y):
    return [(1 - x, y), (x, 1 - y), (1 - x, 1 - y)]


def _pow2_divisor(n):
    return n & (-n)


def _shard_view(ref, axis, size, d):
    start = d * size
    align = _pow2_divisor(size)
    if align > 1:
        start = pl.multiple_of(start, align)
    idx = [slice(None)] * len(ref.shape)
    idx[axis] = pl.ds(start, size)
    return ref.at[tuple(idx)]


class _Plan:
    def __init__(self, ins, out_shape, sems, start, finish, mid=None, aliases=None):
        self.ins, self.out_shape, self.sems = list(ins), list(out_shape), list(sems)
        self.start, self.mid, self.finish = start, mid, finish
        self.aliases = dict(aliases or {})


def _gather_plan(shards, specs):
    n = len(shards)
    n_copy = 7

    def copies(ins, outs, sems):
        send_sems, recv_sems, local_sems = sems
        x, y, c = _place()
        me, sibling = (x, y, c), (x, y, 1 - c)
        chips = _other_chips(x, y)

        def block(a, p):
            _, axis, size = specs[a]
            return _shard_view(outs[a], axis, size, 4 * p[0] + 2 * p[1] + p[2])

        def copy(a, k, p, to, src=None):
            return pltpu.make_async_remote_copy(
                src_ref=block(a, p) if src is None else src,
                dst_ref=block(a, p),
                send_sem=send_sems.at[a * n_copy + k],
                recv_sem=recv_sems.at[a * n_copy + k],
                device_id=to,
                device_id_type=MESH,
            )

        def mine():
            return [pltpu.make_async_copy(ins[a], block(a, me), local_sems.at[a]) for a in range(n)]

        def first():
            return [cp for a in range(n) for cp in
                    [copy(a, 0, me, sibling, src=ins[a])] + [copy(a, 1 + j, me, (*chip, c), src=ins[a]) for j, chip in enumerate(chips)]]

        def landed():
            return [copy(a, 1 + j, (*chip, c), me) for j, chip in enumerate(chips) for a in range(n)]

        def passed():
            return [copy(a, 4 + j, (*chip, c), sibling) for j, chip in enumerate(chips) for a in range(n)]

        def last():
            return [cp for a in range(n) for cp in
                    [copy(a, 0, sibling, me)] + [copy(a, 4 + j, (*chip, 1 - c), me) for j, chip in enumerate(chips)]]

        return mine, first, landed, passed, last

    def start(ins, outs, sems):
        mine, first, _, _, _ = copies(ins, outs, sems)
        for cp in mine() + first():
            cp.start()

    def mid(ins, outs, sems):
        _, _, landed, passed, _ = copies(ins, outs, sems)
        for got, fwd in zip(landed(), passed()):
            got.wait_recv()
            fwd.start()

    def finish(ins, outs, sems):
        mine, first, _, passed, last = copies(ins, outs, sems)
        for cp in last():
            cp.wait_recv()
        for cp in first() + passed():
            cp.wait_send()
        for cp in mine():
            cp.wait()

    return _Plan(
        shards,
        [jax.ShapeDtypeStruct(full, s.dtype) for s, (full, _, _) in zip(shards, specs)],
        [pltpu.SemaphoreType.DMA((n * n_copy,)), pltpu.SemaphoreType.DMA((n * n_copy,)), pltpu.SemaphoreType.DMA((n,))],
        start, finish, mid,
    )


def _sibling_plan(grads, specs):
    n = len(grads)

    def shard_shape(a):
        axis, size = specs[a]
        shp = list(grads[a].shape)
        shp[axis] = size
        return tuple(shp)

    def copies(ins, outs, sems):
        send_sems, recv_sems = sems
        x, y, c = _place()
        return [
            pltpu.make_async_remote_copy(
                src_ref=_shard_view(ins[a], specs[a][0], specs[a][1], 2 * q + (1 - c)),
                dst_ref=outs[a].at[q],
                send_sem=send_sems.at[a * N_CHIP + q],
                recv_sem=recv_sems.at[a * N_CHIP + q],
                device_id=(x, y, 1 - c),
                device_id_type=MESH,
            )
            for a in range(n) for q in range(N_CHIP)
        ]

    def start(ins, outs, sems):
        for cp in copies(ins, outs, sems):
            cp.start()

    def finish(ins, outs, sems):
        for cp in copies(ins, outs, sems):
            cp.wait()

    return _Plan(
        grads,
        [jax.ShapeDtypeStruct((N_CHIP, *shard_shape(a)), grads[a].dtype) for a in range(n)],
        [pltpu.SemaphoreType.DMA((n * N_CHIP,)), pltpu.SemaphoreType.DMA((n * N_CHIP,))],
        start, finish,
    )


def _chip_plan(parts, full_rows=None, row0=None, into=None):
    n = len(parts)
    full_rows = [p.shape[1] for p in parts] if full_rows is None else full_rows
    row0 = [0] * n if row0 is None else row0

    def copies(ins, outs, sems):
        send_sems, recv_sems, local_sems = sems
        x, y, c = _place()
        my_chip = 2 * x + y
        chips = _other_chips(x, y)

        def land(a, q):
            return outs[a].at[q, pl.ds(row0[a], parts[a].shape[1]), :]

        def mine():
            return [pltpu.make_async_copy(ins[a].at[my_chip], land(a, my_chip), local_sems.at[a]) for a in range(n)]

        def remote(sending):
            return [
                pltpu.make_async_remote_copy(
                    src_ref=ins[a].at[2 * chip[0] + chip[1]],
                    dst_ref=land(a, my_chip if sending else 2 * chip[0] + chip[1]),
                    send_sem=send_sems.at[a * 3 + j], recv_sem=recv_sems.at[a * 3 + j],
                    device_id=(*chip, c), device_id_type=MESH)
                for a in range(n) for j, chip in enumerate(chips)
            ]

        return mine, remote

    def start(ins, outs, sems):
        mine, remote = copies(ins, outs, sems)
        for cp in mine() + remote(True):
            cp.start()

    def finish(ins, outs, sems):
        mine, remote = copies(ins, outs, sems)
        for cp in remote(False):
            cp.wait_recv()
        for cp in remote(True):
            cp.wait_send()
        for cp in mine():
            cp.wait()

    return _Plan(
        list(parts) + (list(into) if into is not None else []),
        [jax.ShapeDtypeStruct((N_CHIP, full_rows[a], parts[a].shape[2]), parts[a].dtype) for a in range(n)],
        [pltpu.SemaphoreType.DMA((n * 3,)), pltpu.SemaphoreType.DMA((n * 3,)), pltpu.SemaphoreType.DMA((n,))],
        start, finish,
        aliases={n + a: a for a in range(n)} if into is not None else None,
    )


def _call(body, *, name, operands, out_shape, grid=(), in_specs=None, out_specs=None, scratch_shapes=(),
          plans=(), aliases=None, prefetch=(), start_after_body=False, pass_on_at=0.6):
    operands, out_shape, scratch_shapes = list(operands), list(out_shape), list(scratch_shapes)
    n_pf, n_in, n_out, n_scr = len(prefetch), len(operands), len(out_shape), len(scratch_shapes)
    in_vmem = pl.BlockSpec(memory_space=pltpu.VMEM)
    in_hbm = pl.BlockSpec(memory_space=pl.ANY)
    in_specs = [in_vmem] * n_in if in_specs is None else list(in_specs)
    out_specs = [in_vmem] * n_out if out_specs is None else list(out_specs)
    io_alias = {n_pf + i: o for i, o in (aliases or {}).items()}
    pos_in, pos_out = n_pf + n_in, n_out
    for p in plans:
        for i, o in p.aliases.items():
            io_alias[pos_in + i] = pos_out + o
        pos_in += len(p.ins)
        pos_out += len(p.out_shape)
    steps = math.prod(grid) if grid else 1
    mid_step = min(steps - 1, int(steps * pass_on_at))

    def wrapped(*refs):
        refs = list(refs)
        take = lambda k: [refs.pop(0) for _ in range(k)]
        pf = take(n_pf)
        ins, p_ins = take(n_in), [take(len(p.ins)) for p in plans]
        outs, p_outs = take(n_out), [take(len(p.out_shape)) for p in plans]
        scr, p_sems = take(n_scr), [take(len(p.sems)) for p in plans]
        step = 0
        for ax, g in enumerate(grid):
            step = step * g + pl.program_id(ax)

        def hook(kind, at):
            todo = [(getattr(p, kind), a, b, c) for p, a, b, c in zip(plans, p_ins, p_outs, p_sems) if getattr(p, kind)]

            def run():
                for fn, a, b, c in todo:
                    fn(a, b, c)

            if todo and grid:
                pl.when(step == at)(run)
            elif todo:
                run()

        if not start_after_body:
            hook("start", 0)
        if body is not None:
            body(*pf, *ins, *outs, *scr)
        if start_after_body:
            hook("start", 0)
        hook("mid", mid_step)
        hook("finish", steps - 1)

    layout = dict(
        grid=grid,
        in_specs=in_specs + [in_hbm] * sum(len(p.ins) for p in plans),
        out_specs=out_specs + [in_hbm] * sum(len(p.out_shape) for p in plans),
        scratch_shapes=scratch_shapes + [s for p in plans for s in p.sems],
    )
    if n_pf:
        layout = dict(grid_spec=pltpu.PrefetchScalarGridSpec(num_scalar_prefetch=n_pf, **layout))
    res = pl.pallas_call(
        wrapped,
        name=name,
        out_shape=out_shape + [o for p in plans for o in p.out_shape],
        input_output_aliases=io_alias,
        compiler_params=pltpu.CompilerParams(dimension_semantics=("arbitrary",) * len(grid)) if grid else None,
        **layout,
    )(*prefetch, *operands, *[a for p in plans for a in p.ins])
    res = list(res)
    outs, plan_outs = res[:n_out], []
    pos = n_out
    for p in plans:
        plan_outs.append(res[pos : pos + len(p.out_shape)])
        pos += len(p.out_shape)
    return outs, plan_outs


def _inproj(x, g, w_shard, order, plans=()):
    s = x.shape[0]
    tm = min(TM_INPROJ, s)
    nt = s // tm
    wide = 2 * SHARD_IN
    fwd_at = nt // 2

    def body(order_ref, x_ref, g_ref, w_hbm, proj_ref, h1_ref, h1t_ref, wfull_hbm,
             w_scr, h1_scr, send_sems, recv_sems, local_sems):
        k, i = pl.program_id(0), pl.program_id(1)
        px, py, c = _place()
        me, sibling = (px, py, c), (px, py, 1 - c)
        chips = _other_chips(px, py)

        def half(slot, core):
            return w_scr.at[slot, :, pl.ds(pl.multiple_of(core * SHARD_IN, 128), SHARD_IN)]

        def copy(n, slot, core, to, src=None):
            return pltpu.make_async_remote_copy(
                src_ref=half(slot, core) if src is None else src, dst_ref=half(slot, core),
                send_sem=send_sems.at[n], recv_sem=recv_sems.at[n], device_id=to, device_id_type=MESH)

        def mine():
            return pltpu.make_async_copy(w_hbm, half(0, c), local_sems.at[0])

        def first():
            return [copy(0, 0, c, sibling, src=w_hbm)] + [copy(1 + j, 1 + j, c, (*chip, c), src=w_hbm) for j, chip in enumerate(chips)]

        def save(slot):
            cols = pl.ds(pl.multiple_of(order_ref[slot] * wide, 128), wide)
            return pltpu.make_async_copy(w_scr.at[slot], wfull_hbm.at[:, cols], local_sems.at[1 + slot])

        @pl.when((k == 0) & (i == 0))
        def _():
            mine().start()
            for cp in first():
                cp.start()
            mine().wait()
            copy(0, 0, 1 - c, me).wait_recv()

        for j, chip in enumerate(chips):
            @pl.when((k == j) & (i == fwd_at))
            def _():
                copy(1 + j, 1 + j, c, me).wait_recv()
                copy(4 + j, 1 + j, c, sibling).start()

            @pl.when((k == j + 1) & (i == 0))
            def _():
                copy(4 + j, 1 + j, 1 - c, me).wait_recv()

        @pl.when((k == 3) & (i == 0))
        def _():
            for slot in range(N_CHIP):
                save(slot).start()

        @pl.when(k == 0)
        def _():
            xh, _ = _rms(x_ref[...])
            h = xh * g_ref[...]
            h1_ref[...] = h.astype(BF16)
            h1t_ref[...] = h.T.astype(BF16)
            h1_scr[pl.ds(pl.multiple_of(i * tm, tm), tm), :] = h.astype(BF16)

        proj_ref[...] = _dot(h1_scr[pl.ds(pl.multiple_of(i * tm, tm), tm), :], w_scr[k]).astype(BF16)

        @pl.when((k == 3) & (i == nt - 1))
        def _():
            for cp in first() + [copy(4 + j, 1 + j, c, sibling) for j in range(3)]:
                cp.wait_send()
            for slot in range(N_CHIP):
                save(slot).wait()

    once = lambda k, i, order_ref: (jnp.where(k == 0, i, nt - 1), 0)
    return _call(
        body,
        name="inproj",
        prefetch=[order],
        operands=[x, g, w_shard],
        grid=(N_CHIP, nt),
        in_specs=[
            pl.BlockSpec((tm, D_MODEL), once),
            pl.BlockSpec((1, D_MODEL), lambda k, i, order_ref: (0, 0)),
            pl.BlockSpec(memory_space=pl.ANY),
        ],
        out_specs=[
            pl.BlockSpec((tm, wide), lambda k, i, order_ref: (i, order_ref[k])),
            pl.BlockSpec((tm, D_MODEL), once),
            pl.BlockSpec((D_MODEL, tm), lambda k, i, order_ref: (0, jnp.where(k == 0, i, nt - 1))),
            pl.BlockSpec(memory_space=pl.ANY),
        ],
        out_shape=[
            jax.ShapeDtypeStruct((s, IN_DIM), BF16),
            jax.ShapeDtypeStruct((s, D_MODEL), BF16),
            jax.ShapeDtypeStruct((D_MODEL, s), BF16),
            jax.ShapeDtypeStruct((D_MODEL, IN_DIM), BF16),
        ],
        scratch_shapes=[
            pltpu.VMEM((N_CHIP, D_MODEL, wide), BF16),
            pltpu.VMEM((s, D_MODEL), BF16),
            pltpu.SemaphoreType.DMA((7,)),
            pltpu.SemaphoreType.DMA((7,)),
            pltpu.SemaphoreType.DMA((1 + N_CHIP,)),
        ],
        plans=plans,
        start_after_body=True,
        pass_on_at=0.8,
    )


def _causal_mask():
    row = lax.broadcasted_iota(jnp.int32, (CHUNK, CHUNK), 0)
    col = lax.broadcasted_iota(jnp.int32, (CHUNK, CHUNK), 1)
    return row, row >= col


def _mixer_fwd(proj, conv_w, ln_g, ln_b, ws, bs_b, plans=()):
    s = proj.shape[0]
    nt = s // CHUNK

    def body(proj_ref, cw_ref, lng_ref, lnb_ref, ws_ref, bsb_ref, cat_ref, pcar_ref):
        @pl.when(pl.program_id(0) == 0)
        def _():
            pcar_ref[...] = jnp.zeros_like(pcar_ref)

        row, tril = _causal_mask()
        for j in range(8):
            cs = slice(CHUNK * j, CHUNK * (j + 1))

            def grp(k):
                return proj_ref[:, k * D_MODEL + CHUNK * j : k * D_MODEL + CHUNK * (j + 1)].astype(F32)

            w = cw_ref[:, cs]
            p = grp(1) * grp(2)
            pc = pcar_ref[:, cs]
            p1 = jnp.where(row == 0, pc[7:8], pltpu.roll(p, 1, 0))
            p2 = jnp.where(row == 0, pc[6:7], jnp.where(row == 1, pc[7:8], pltpu.roll(p, 2, 0)))
            pcar_ref[:, cs] = p[CHUNK - 8 :]
            cv = w[0:1] * p2 + w[1:2] * p1 + w[2:3] * p
            sa, _ = _silu(grp(3))
            cat_ref[:, cs] = ((grp(0) * cv) * sa).astype(BF16)
            gu, _ = _gelu(grp(4))
            gv, _ = _gelu(grp(5))
            dv = gv - jnp.mean(gv, axis=-1, keepdims=True)
            vn = dv * lax.rsqrt(jnp.mean(dv * dv, axis=-1, keepdims=True) + EPS)
            vn = vn * lng_ref[:, cs] + lnb_ref[:, cs]
            wc = jnp.where(tril, ws_ref[j], 0.0).astype(BF16)
            sp = _dot(wc, vn.astype(BF16)) + bsb_ref[j]
            sb, _ = _silu(grp(6))
            cat_ref[:, D_MODEL + CHUNK * j : D_MODEL + CHUNK * (j + 1)] = ((gu * sp) * sb).astype(BF16)

    whole = lambda shape: pl.BlockSpec(shape, lambda i: (0,) * len(shape))
    (cat,), plan_outs = _call(
        body,
        name="mixer_fwd",
        operands=[proj, conv_w, ln_g, ln_b, ws, bs_b],
        grid=(nt,),
        in_specs=[
            pl.BlockSpec((CHUNK, IN_DIM), lambda i: (i, 0)),
            whole((8, D_MODEL)),
            whole((1, D_MODEL)),
            whole((1, D_MODEL)),
            whole((8, CHUNK, CHUNK)),
            whole((8, CHUNK, CHUNK)),
        ],
        out_specs=[pl.BlockSpec((CHUNK, MIX_DIM), lambda i: (i, 0))],
        out_shape=[jax.ShapeDtypeStruct((s, MIX_DIM), BF16)],
        scratch_shapes=[pltpu.VMEM((8, D_MODEL), F32)],
        plans=plans,
        pass_on_at=0.85,
    )
    return cat, plan_outs


def _mixer_bwd(proj, dcat, conv_w, ln_g, ln_b, ws, bs_b, plans=()):
    s = proj.shape[0]
    nt = s // CHUNK

    def body(proj_ref, halo_ref, dcat_ref, cw_ref, lng_ref, lnb_ref, ws_ref, bsb_ref,
             dproj_ref, dcw_ref, dlng_ref, dlnb_ref, dws_ref, dbs_ref,
             car_ref, acc_cw, acc_lng, acc_lnb, acc_bs):
        i = pl.program_id(0)
        tile = nt - 1 - i

        @pl.when(i == 0)
        def _():
            car_ref[...] = jnp.zeros_like(car_ref)
            acc_cw[...] = jnp.zeros_like(acc_cw)
            acc_lng[...] = jnp.zeros_like(acc_lng)
            acc_lnb[...] = jnp.zeros_like(acc_lnb)
            acc_bs[...] = jnp.zeros_like(acc_bs)
            dws_ref[...] = jnp.zeros_like(dws_ref)

        row, tril = _causal_mask()
        has_prev = jnp.where(tile > 0, 1.0, 0.0).astype(F32)
        for j in range(8):
            cs = slice(CHUNK * j, CHUNK * (j + 1))

            def col(k):
                return slice(k * D_MODEL + CHUNK * j, k * D_MODEL + CHUNK * (j + 1))

            def grp(k):
                return proj_ref[:, col(k)].astype(F32)

            w = cw_ref[:, cs]
            gb, gc, xa = grp(0), grp(1), grp(2)
            p = gc * xa
            pprev = halo_ref[:, col(1)].astype(F32) * halo_ref[:, col(2)].astype(F32) * has_prev
            p1 = jnp.where(row == 0, pprev[15:16], pltpu.roll(p, 1, 0))
            p2 = jnp.where(row == 0, pprev[14:15], jnp.where(row == 1, pprev[15:16], pltpu.roll(p, 2, 0)))
            cv = w[0:1] * p2 + w[1:2] * p1 + w[2:3] * p
            za = grp(3)
            sa, dsa = _silu(za)
            da = dcat_ref[:, cs].astype(F32)
            dproj_ref[:, col(0)] = (da * cv * sa).astype(BF16)
            dproj_ref[:, col(3)] = (da * gb * cv * dsa).astype(BF16)
            dcv = da * gb * sa
            nxt = car_ref[:, cs]
            d1 = jnp.where(row == CHUNK - 1, nxt[0:1], pltpu.roll(dcv, CHUNK - 1, 0))
            d2 = jnp.where(row == CHUNK - 2, nxt[0:1], jnp.where(row == CHUNK - 1, nxt[1:2], pltpu.roll(dcv, CHUNK - 2, 0)))
            car_ref[:, cs] = dcv[0:8]
            dp = w[2:3] * dcv + w[1:2] * d1 + w[0:1] * d2
            dproj_ref[:, col(1)] = (dp * xa).astype(BF16)
            dproj_ref[:, col(2)] = (dp * gc).astype(BF16)
            acc_cw[0, :, cs] += _rows8(dcv * p2)
            acc_cw[1, :, cs] += _rows8(dcv * p1)
            acc_cw[2, :, cs] += _rows8(dcv * p)
            gu, dgu = _gelu(grp(4))
            gv, dgv = _gelu(grp(5))
            dv = gv - jnp.mean(gv, axis=-1, keepdims=True)
            rstd = lax.rsqrt(jnp.mean(dv * dv, axis=-1, keepdims=True) + EPS)
            vnh = dv * rstd
            lng = lng_ref[:, cs]
            vnb = (vnh * lng + lnb_ref[:, cs]).astype(BF16)
            wcf = jnp.where(tril, ws_ref[j], 0.0)
            sp = _dot(wcf.astype(BF16), vnb) + bsb_ref[j]
            zb = grp(6)
            sb, dsb = _silu(zb)
            db = dcat_ref[:, D_MODEL + CHUNK * j : D_MODEL + CHUNK * (j + 1)].astype(F32)
            dproj_ref[:, col(4)] = (db * sp * sb * dgu).astype(BF16)
            dproj_ref[:, col(6)] = (db * gu * sp * dsb).astype(BF16)
            dsp = db * gu * sb
            acc_bs[j] += dsp
            dspb = dsp.astype(BF16)
            dws_ref[j] += _dot_nt(dspb, vnb)
            dvn = _dot(wcf.T.astype(BF16), dspb)
            acc_lnb[:, cs] += _rows8(dvn)
            acc_lng[:, cs] += _rows8(dvn * vnh)
            dvh = dvn * lng
            dgvv = rstd * (dvh - jnp.mean(dvh, axis=-1, keepdims=True) - vnh * jnp.mean(dvh * vnh, axis=-1, keepdims=True))
            dproj_ref[:, col(5)] = (dgvv * dgv).astype(BF16)

        @pl.when(i == nt - 1)
        def _():
            dlng_ref[...] = jnp.sum(acc_lng[...], axis=0, keepdims=True)
            dlnb_ref[...] = jnp.sum(acc_lnb[...], axis=0, keepdims=True)
            dcw_ref[...] = jnp.zeros_like(dcw_ref)
            for k in range(3):
                dcw_ref[k : k + 1, :] = jnp.sum(acc_cw[k], axis=0, keepdims=True)
            for j in range(8):
                dws_ref[j] = jnp.where(tril, dws_ref[j], 0.0)
                dbs_ref[:, CHUNK * j : CHUNK * (j + 1)] = jnp.sum(acc_bs[j].T, axis=0, keepdims=True)

    whole = lambda shape: pl.BlockSpec(shape, lambda i: (0,) * len(shape))
    halo_rows = 16
    per = CHUNK // halo_rows
    return _call(
        body,
        name="mixer_bwd",
        operands=[proj, proj, dcat, conv_w, ln_g, ln_b, ws, bs_b],
        plans=plans,
        grid=(nt,),
        in_specs=[
            pl.BlockSpec((CHUNK, IN_DIM), lambda i: (nt - 1 - i, 0)),
            pl.BlockSpec((halo_rows, IN_DIM), lambda i: (jnp.maximum((nt - 1 - i) * per - 1, 0), 0)),
            pl.BlockSpec((CHUNK, MIX_DIM), lambda i: (nt - 1 - i, 0)),
            whole((8, D_MODEL)),
            whole((1, D_MODEL)),
            whole((1, D_MODEL)),
            whole((8, CHUNK, CHUNK)),
            whole((8, CHUNK, CHUNK)),
        ],
        out_specs=[
            pl.BlockSpec((CHUNK, IN_DIM), lambda i: (nt - 1 - i, 0)),
            whole((8, D_MODEL)),
            whole((1, D_MODEL)),
            whole((1, D_MODEL)),
            whole((8, CHUNK, CHUNK)),
            whole((1, D_MODEL)),
        ],
        out_shape=[
            jax.ShapeDtypeStruct((s, IN_DIM), BF16),
            jax.ShapeDtypeStruct((8, D_MODEL), F32),
            jax.ShapeDtypeStruct((1, D_MODEL), F32),
            jax.ShapeDtypeStruct((1, D_MODEL), F32),
            jax.ShapeDtypeStruct((8, CHUNK, CHUNK), F32),
            jax.ShapeDtypeStruct((1, D_MODEL), F32),
        ],
        scratch_shapes=[
            pltpu.VMEM((8, D_MODEL), F32),
            pltpu.VMEM((3, 8, D_MODEL), F32),
            pltpu.VMEM((8, D_MODEL), F32),
            pltpu.VMEM((8, D_MODEL), F32),
            pltpu.VMEM((8, CHUNK, CHUNK), F32),
        ],
    )


def _kv_fwd(mem, g_mem, w_kv):
    def body(mem_ref, g_ref, w_ref, k_ref, vv_ref, mt_ref):
        mh, _ = _rms(mem_ref[...])
        m = mh * g_ref[...]
        kv = _dot(m.astype(BF16), w_ref[...])
        k_ref[...] = kv[:, :D_MODEL].astype(BF16)
        vv_ref[...] = kv[:, D_MODEL:].astype(BF16)
        mt_ref[...] = m.T.astype(BF16)

    return pl.pallas_call(
        body,
        name="kv_fwd",
        out_shape=[
            jax.ShapeDtypeStruct((MEM_LEN, D_MODEL), BF16),
            jax.ShapeDtypeStruct((MEM_LEN, D_MODEL), BF16),
            jax.ShapeDtypeStruct((D_MODEL, MEM_LEN), BF16),
        ],
    )(mem, g_mem, w_kv)


def _kv_bwd(mem, w_kv, mt, dk, dvv):
    def body(mem_ref, w_ref, mt_ref, dk_ref, dvv_ref, dw_ref, dg_ref):
        dkv = jnp.concatenate([dk_ref[...], dvv_ref[...]], axis=1).astype(BF16)
        dw_ref[...] = _dot(mt_ref[...], dkv)
        dm = _dot_nt(dkv, w_ref[...])
        mh, _ = _rms(mem_ref[...])
        dg_ref[...] = jnp.sum(dm * mh, axis=0, keepdims=True)

    return pl.pallas_call(
        body,
        name="kv_bwd",
        out_shape=[
            jax.ShapeDtypeStruct((D_MODEL, 2 * D_MODEL), F32),
            jax.ShapeDtypeStruct((1, D_MODEL), F32),
        ],
    )(mem, w_kv, mt, dk, dvv)


def _attn(x, cat, target, w_out, w_q, w_xo, k, vv, g_x, g_f):
    s = x.shape[0]
    tm = min(TM_ATTN, s)
    sub = min(SUB_ATTN, tm)
    nt = s // tm
    scale = 1.0 / math.sqrt(X_HEAD_DIM)

    def body(x_ref, cat_ref, t_ref, wout_ref, wq_ref, wxo_ref, k_ref, vv_ref, gx_ref, gf_ref,
             dcat_ref, catt_ref, h2t_ref, ot_ref, dx2b_ref, dqb_ref, dx1b_ref,
             dk_ref, dvv_ref, dgf_ref, dgx_ref, loss_ref,
             p_scr, q_scr, o_scr, dq_scr, acc_gf, acc_gx, acc_loss):
        i = pl.program_id(0)

        @pl.when(i == 0)
        def _():
            dk_ref[...] = jnp.zeros_like(dk_ref)
            dvv_ref[...] = jnp.zeros_like(dvv_ref)
            acc_gf[...] = jnp.zeros_like(acc_gf)
            acc_gx[...] = jnp.zeros_like(acc_gx)
            acc_loss[...] = jnp.zeros_like(acc_loss)

        subs = [slice(u * sub, (u + 1) * sub) for u in range(tm // sub)]
        heads = [slice(X_HEAD_DIM * h, X_HEAD_DIM * (h + 1)) for h in range(X_HEADS)]
        gx, gf = gx_ref[...], gf_ref[...]
        x1, x1h, r2 = [], [], []
        for rs in subs:
            cat = cat_ref[rs, :]
            catt_ref[:, rs] = cat.astype(F32).T.astype(BF16)
            v = x_ref[rs, :] + _dot(cat, wout_ref[...])
            vh, r = _rms(v)
            h2 = vh * gx
            h2t_ref[:, rs] = h2.T.astype(BF16)
            q_scr[rs, :] = _dot(h2.astype(BF16), wq_ref[...]).astype(BF16)
            x1.append(v), x1h.append(vh), r2.append(r)
        for rs in subs:
            for hs in heads:
                sc = _dot_nt(q_scr[rs, hs], k_ref[:, hs]) * scale
                e = jnp.exp(sc - jnp.max(sc, axis=-1, keepdims=True))
                p = e / jnp.sum(e, axis=-1, keepdims=True)
                p_scr[rs, hs] = p
                o_scr[rs, hs] = _dot(p.astype(BF16), vv_ref[:, hs])
        dx2, sq, d_gf = [], None, None
        for u, rs in enumerate(subs):
            o = o_scr[rs, :]
            ot_ref[:, rs] = o.T.astype(BF16)
            x2 = x1[u] + _dot(o.astype(BF16), wxo_ref[...])
            x2h, r3 = _rms(x2)
            err = x2h * gf - t_ref[rs, :]
            dy = err * (1.0 / D_MODEL)
            sq = _rows8(err * err) if sq is None else sq + _rows8(err * err)
            d_gf = _rows8(dy * x2h) if d_gf is None else d_gf + _rows8(dy * x2h)
            dx2.append(_rms_bwd(dy * gf, x2h, r3))
        acc_loss[...] += sq
        acc_gf[...] += d_gf
        d_vv, d_k = [None] * X_HEADS, [None] * X_HEADS
        for u, rs in enumerate(subs):
            dx2b = dx2[u].astype(BF16)
            dx2b_ref[rs, :] = dx2b
            dob = _dot_nt(dx2b, wxo_ref[...]).astype(BF16)
            for h, hs in enumerate(heads):
                p = p_scr[rs, hs]
                dp = _dot_nt(dob[:, hs], vv_ref[:, hs])
                ds = p * (dp - jnp.sum(dp * p, axis=-1, keepdims=True)) * scale
                dq_scr[rs, hs] = _dot(ds.astype(BF16), k_ref[:, hs]).astype(BF16)
                dvv_h = _dot(p.T.astype(BF16), dob[:, hs])
                dk_h = _dot(ds.T.astype(BF16), q_scr[rs, hs])
                d_vv[h] = dvv_h if d_vv[h] is None else d_vv[h] + dvv_h
                d_k[h] = dk_h if d_k[h] is None else d_k[h] + dk_h
        for h, hs in enumerate(heads):
            dvv_ref[:, hs] += d_vv[h]
            dk_ref[:, hs] += d_k[h]
        d_gx = None
        for u, rs in enumerate(subs):
            dqb = dq_scr[rs, :]
            dqb_ref[rs, :] = dqb
            dh2 = _dot_nt(dqb, wq_ref[...])
            d_gx = _rows8(dh2 * x1h[u]) if d_gx is None else d_gx + _rows8(dh2 * x1h[u])
            dx1 = dx2[u] + _rms_bwd(dh2 * gx, x1h[u], r2[u])
            dx1b = dx1.astype(BF16)
            dx1b_ref[rs, :] = dx1b
            dcat_ref[rs, :] = _dot_nt(dx1b, wout_ref[...]).astype(BF16)
        acc_gx[...] += d_gx

        @pl.when(i == nt - 1)
        def _():
            dgf_ref[...] = jnp.sum(acc_gf[...], axis=0, keepdims=True)
            dgx_ref[...] = jnp.sum(acc_gx[...], axis=0, keepdims=True)
            loss_ref[...] = jnp.sum(acc_loss[...], axis=0, keepdims=True)

    whole = lambda shape: pl.BlockSpec(shape, lambda i: (0,) * len(shape))
    once = lambda shape: pl.BlockSpec(shape, lambda i: (0,) * len(shape), pipeline_mode=pl.Buffered(1))
    rows = lambda width: pl.BlockSpec((tm, width), lambda i: (i, 0))
    cols = lambda height: pl.BlockSpec((height, tm), lambda i: (0, i))
    vec = jax.ShapeDtypeStruct((1, D_MODEL), F32)
    return pl.pallas_call(
        body,
        name="attn",
        grid=(nt,),
        in_specs=[
            rows(D_MODEL), rows(MIX_DIM), rows(D_MODEL),
            once((MIX_DIM, D_MODEL)), once((D_MODEL, D_MODEL)), once((D_MODEL, D_MODEL)),
            once((MEM_LEN, D_MODEL)), once((MEM_LEN, D_MODEL)),
            whole((1, D_MODEL)), whole((1, D_MODEL)),
        ],
        out_specs=[
            rows(MIX_DIM), cols(MIX_DIM), cols(D_MODEL), cols(D_MODEL),
            rows(D_MODEL), rows(D_MODEL), rows(D_MODEL),
            whole((MEM_LEN, D_MODEL)), whole((MEM_LEN, D_MODEL)),
            whole((1, D_MODEL)), whole((1, D_MODEL)), whole((1, D_MODEL)),
        ],
        out_shape=[
            jax.ShapeDtypeStruct((s, MIX_DIM), BF16),
            jax.ShapeDtypeStruct((MIX_DIM, s), BF16),
            jax.ShapeDtypeStruct((D_MODEL, s), BF16),
            jax.ShapeDtypeStruct((D_MODEL, s), BF16),
            jax.ShapeDtypeStruct((s, D_MODEL), BF16),
            jax.ShapeDtypeStruct((s, D_MODEL), BF16),
            jax.ShapeDtypeStruct((s, D_MODEL), BF16),
            jax.ShapeDtypeStruct((MEM_LEN, D_MODEL), F32),
            jax.ShapeDtypeStruct((MEM_LEN, D_MODEL), F32),
            vec, vec, vec,
        ],
        scratch_shapes=[
            pltpu.VMEM((tm, D_MODEL), F32),
            pltpu.VMEM((tm, D_MODEL), BF16),
            pltpu.VMEM((tm, D_MODEL), F32),
            pltpu.VMEM((tm, D_MODEL), BF16),
            pltpu.VMEM((8, D_MODEL), F32),
            pltpu.VMEM((8, D_MODEL), F32),
            pltpu.VMEM((8, D_MODEL), F32),
        ],
        compiler_params=pltpu.CompilerParams(dimension_semantics=("arbitrary",)),
    )(x, cat, target, w_out, w_q, w_xo, k, vv, g_x, g_f)


def _matmul_f32(name, a, b, rows=None, plans=()):
    first, m = (0, a.shape[0]) if rows is None else rows
    kk = a.shape[1]
    n = b.shape[1]
    tm, tn, tk = min(m, TM_DW), min(n, D_MODEL), min(kk, TK_DW)
    nk = kk // tk
    i0 = first // tm

    def body(a_ref, b_ref, o_ref):
        prod = _dot(a_ref[...], b_ref[...])
        if nk == 1:
            o_ref[...] = prod
            return
        kid = pl.program_id(2)

        @pl.when(kid == 0)
        def _():
            o_ref[...] = prod

        @pl.when(kid > 0)
        def _():
            o_ref[...] += prod

    (out,), plan_outs = _call(
        body,
        name=name,
        operands=[a, b],
        grid=(m // tm, n // tn, nk),
        in_specs=[pl.BlockSpec((tm, tk), lambda i, j, k: (i + i0, k)), pl.BlockSpec((tk, tn), lambda i, j, k: (k, j))],
        out_specs=[pl.BlockSpec((tm, tn), lambda i, j, k: (i, j))],
        out_shape=[jax.ShapeDtypeStruct((m, n), F32)],
        plans=plans,
    )
    return out, plan_outs


def _dh1(name, dproj, w_in, x, dx1, g, tiles, into=None, plans=()):
    s = x.shape[0]
    tm = min(TM_DH1, s)
    t0, nt = tiles[0], tiles[1] - tiles[0]

    def body(dp_ref, w_ref, x_ref, dx1_ref, g_ref, *rest):
        gx_ref, dg_ref, acc_g = rest[-3:]
        i = pl.program_id(0)

        @pl.when(i == 0)
        def _():
            acc_g[...] = jnp.zeros_like(acc_g)

        dh1 = _dot_nt(dp_ref[...], w_ref[...])
        xh, r = _rms(x_ref[...])
        acc_g[...] += _rows8(dh1 * xh)
        gx_ref[...] = dx1_ref[...].astype(F32) + _rms_bwd(dh1 * g_ref[...], xh, r)

        @pl.when(i == nt - 1)
        def _():
            dg_ref[...] = jnp.sum(acc_g[...], axis=0, keepdims=True)

    rows = pl.BlockSpec((tm, D_MODEL), lambda i: (i + t0, 0))
    return _call(
        body,
        name=name,
        operands=[dproj, w_in, x, dx1, g] + ([into] if into is not None else []),
        grid=(nt,),
        in_specs=[
            pl.BlockSpec((tm, IN_DIM), lambda i: (i + t0, 0)),
            pl.BlockSpec((D_MODEL, IN_DIM), lambda i: (0, 0), pipeline_mode=pl.Buffered(1)),
            rows,
            rows,
            pl.BlockSpec((1, D_MODEL), lambda i: (0, 0)),
        ] + ([pl.BlockSpec(memory_space=pl.ANY)] if into is not None else []),
        out_specs=[rows, pl.BlockSpec((1, D_MODEL), lambda i: (0, 0))],
        out_shape=[jax.ShapeDtypeStruct((s, D_MODEL), F32), jax.ShapeDtypeStruct((1, D_MODEL), F32)],
        scratch_shapes=[pltpu.VMEM((8, D_MODEL), F32)],
        plans=plans,
        aliases={5: 0} if into is not None else None,
    )


def _pair_sum(name, grad, got, axis, size, core, plans=()):
    shard = list(grad.shape)
    shard[axis] = size
    r, cdim = shard
    tr = min(r, TR_PAIR_SUM)
    nr = r // tr

    def body(core_ref, g_ref, got_ref, o_ref):
        o_ref[0] = (g_ref[...] + got_ref[0]).astype(BF16)

    if axis == 1:
        g_map = lambda q, t, core_ref: (t, 2 * q + core_ref[0])
    else:
        g_map = lambda q, t, core_ref: ((2 * q + core_ref[0]) * nr + t, 0)
    (out,), plan_outs = _call(
        body,
        name=name,
        prefetch=[core],
        operands=[grad, got],
        grid=(N_CHIP, nr),
        in_specs=[
            pl.BlockSpec((tr, cdim), g_map),
            pl.BlockSpec((1, tr, cdim), lambda q, t, core_ref: (q, t, 0)),
        ],
        out_specs=[pl.BlockSpec((1, tr, cdim), lambda q, t, core_ref: (q, t, 0))],
        out_shape=[jax.ShapeDtypeStruct((N_CHIP, r, cdim), BF16)],
        plans=plans,
    )
    return (out, plan_outs) if plans else out


def _adamw(name, w, m, v, parts, own=None, plans=()):
    r, cdim = w.shape
    parts = list(parts) if isinstance(parts, (list, tuple)) else [parts]
    own_slot, own_parts = (None, []) if own is None else (own[0], list(own[1]))
    n_parts = parts[0].shape[0]
    tr = min([r, TR_ELEMWISE] + [p.shape[1] for p in parts])
    first_tile, tiles = [], 0
    for p in parts:
        first_tile.append(tiles)
        tiles += p.shape[1] // tr
    c1 = 1.0 / (1.0 - ADAM_B1**ADAM_STEP)
    c2 = 1.0 / (1.0 - ADAM_B2**ADAM_STEP)

    def body(*refs):
        refs = list(refs)
        slot_ref = refs.pop(0) if own is not None else None
        w_ref, m_ref, v_ref = refs[:3]
        p_refs = refs[3 : 3 + len(parts)]
        o_refs = refs[3 + len(parts) : 3 + len(parts) + len(own_parts)]
        g_ref, d_ref, nm_ref, nv_ref = refs[-4:]
        t = pl.program_id(0)
        g = None
        for k, p_ref in enumerate(p_refs):
            gk = None
            for q in range(n_parts):
                part = p_ref[q].astype(F32)
                if own is not None:
                    part = jnp.where(slot_ref[0] == q, o_refs[k][0].astype(F32), part)
                gk = part if gk is None else gk + part
            g = gk if g is None else jnp.where(t >= first_tile[k], gk, g)
        m_new = ADAM_B1 * m_ref[...] + (1.0 - ADAM_B1) * g
        v_new = ADAM_B2 * v_ref[...] + (1.0 - ADAM_B2) * (g * g)
        g_ref[...] = g
        nm_ref[...] = m_new
        nv_ref[...] = v_new
        d_ref[...] = -ADAM_LR * ((m_new * c1) / (jnp.sqrt(v_new * c2) + ADAM_EPS) + ADAM_WD * w_ref[...])

    blk = pl.BlockSpec((tr, cdim), lambda t, *_: (t, 0))
    out = jax.ShapeDtypeStruct((r, cdim), F32)

    def tile_of(k, t):
        return jnp.clip(t - first_tile[k], 0, parts[k].shape[1] // tr - 1)

    part_specs = [pl.BlockSpec((n_parts, tr, cdim), lambda t, *_, k=k: (0, tile_of(k, t), 0)) for k in range(len(parts))]
    own_specs = [pl.BlockSpec((1, tr, cdim), lambda t, slot_ref, k=k: (slot_ref[0], tile_of(k, t), 0)) for k in range(len(own_parts))]
    return _call(
        body,
        name=name,
        prefetch=[own_slot] if own is not None else [],
        operands=[w, m, v] + parts + own_parts,
        grid=(r // tr,),
        in_specs=[blk, blk, blk] + part_specs + own_specs,
        out_specs=[blk, blk, blk, blk],
        out_shape=[out, out, out, out],
        plans=plans,
    )


def _chip_exchange_start(parts):
    n = len(parts)
    hbm = pl.BlockSpec(memory_space=pltpu.HBM)
    sem = pl.BlockSpec(memory_space=pltpu.SEMAPHORE)

    def body(*refs):
        srcs, lands = refs[:n], refs[n : 2 * n]
        send_sems, recv_sems = refs[2 * n], refs[2 * n + 1]
        token = refs[-1]
        x, y, c = _place()
        my_chip = 2 * x + y
        for a in range(n):
            for j, chip in enumerate(_other_chips(x, y)):
                pltpu.make_async_remote_copy(
                    src_ref=srcs[a].at[2 * chip[0] + chip[1]], dst_ref=lands[a].at[my_chip],
                    send_sem=send_sems.at[a * 3 + j], recv_sem=recv_sems.at[a * 3 + j],
                    device_id=(*chip, c), device_id_type=MESH).start()
        token[...] = jnp.zeros_like(token)

    in_hbm = [pltpu.with_memory_space_constraint(p, pltpu.HBM) for p in parts]
    in_hbm += [pltpu.with_memory_space_constraint(lax.empty(p.shape, p.dtype), pltpu.HBM) for p in parts]
    res = pl.pallas_call(
        body,
        name="grad_chip_exchange_start",
        out_shape=[pltpu.SemaphoreType.DMA((n * 3,)), pltpu.SemaphoreType.DMA((n * 3,))]
        + [pltpu.HBM(p.shape, p.dtype) for p in parts] * 2 + [jax.ShapeDtypeStruct((8, CHUNK), F32)],
        in_specs=[hbm] * (2 * n),
        out_specs=[sem, sem] + [hbm] * (2 * n) + [pl.BlockSpec(memory_space=pltpu.VMEM)],
        input_output_aliases={i: 2 + i for i in range(2 * n)},
        compiler_params=pltpu.CompilerParams(has_side_effects=pltpu.SideEffectType.DATAFLOW_SIDE_EFFECTING),
    )(*in_hbm)
    return res[:-1], res[-1]


def _chip_exchange_wait(in_flight, after):
    n = (len(in_flight) - 2) // 2
    hbm = pl.BlockSpec(memory_space=pltpu.HBM)
    sem = pl.BlockSpec(memory_space=pltpu.SEMAPHORE)
    send_sems, recv_sems, *bufs = in_flight

    def body(*refs):
        srcs, lands = refs[:n], refs[n : 2 * n]
        send_ref, recv_ref = refs[2 * n], refs[2 * n + 1]
        x, y, c = _place()
        for a in range(n):
            for j, chip in enumerate(_other_chips(x, y)):
                their_chip = 2 * chip[0] + chip[1]
                cp = pltpu.make_async_remote_copy(
                    src_ref=srcs[a].at[their_chip], dst_ref=lands[a].at[their_chip],
                    send_sem=send_ref.at[a * 3 + j], recv_sem=recv_ref.at[a * 3 + j],
                    device_id=(*chip, c), device_id_type=MESH)
                cp.wait_send()
                cp.wait_recv()

    res = pl.pallas_call(
        body,
        name="grad_chip_exchange_wait",
        out_shape=[pltpu.HBM(b.shape, b.dtype) for b in bufs],
        in_specs=[hbm] * (2 * n) + [sem, sem] + [pl.BlockSpec(memory_space=pl.ANY)] * len(after),
        out_specs=[hbm] * (2 * n),
        input_output_aliases={i: i for i in range(2 * n)},
        compiler_params=pltpu.CompilerParams(has_side_effects=pltpu.SideEffectType.DATAFLOW_SIDE_EFFECTING),
    )(*bufs, send_sems, recv_sems, *after)
    return res[n:]


def kernel(x, mem, norm_mix_g, w_in, conv_w, gm_ln_g, gm_ln_b, gm_ws, gm_bs, w_out, norm_x_g, norm_mem_g, w_q, w_kv, w_xo, norm_final_g, loss_target, m_norm_mix_g, m_w_in, m_conv_w, m_gm_ln_g, m_gm_ln_b, m_gm_ws, m_gm_bs, m_w_out, m_norm_x_g, m_norm_mem_g, m_w_q, m_w_kv, m_w_xo, m_norm_final_g, v_norm_mix_g, v_w_in, v_conv_w, v_gm_ln_g, v_gm_ln_b, v_gm_ws, v_gm_bs, v_w_out, v_norm_x_g, v_norm_mem_g, v_w_q, v_w_kv, v_w_xo, v_norm_final_g):
    s = x.shape[1]
    dev = 4 * lax.axis_index("x") + 2 * lax.axis_index("y") + lax.axis_index("c")
    core = lax.axis_index("c").astype(jnp.int32).reshape(1)
    x2 = x[0]
    target = loss_target[0]
    pad_taps = lambda t: jnp.pad(t[0], ((0, 5), (0, 0)))

    big_names = ["w_in", "w_out", "w_q", "w_kv", "w_xo"]
    big_w = dict(w_in=w_in[0], w_out=w_out[0], w_q=w_q[0], w_kv=w_kv[0], w_xo=w_xo[0])
    big_m = dict(w_in=m_w_in[0], w_out=m_w_out[0], w_q=m_w_q[0], w_kv=m_w_kv[0], w_xo=m_w_xo[0])
    big_v = dict(w_in=v_w_in[0], w_out=v_w_out[0], w_q=v_w_q[0], w_kv=v_w_kv[0], w_xo=v_w_xo[0])
    shard_axis = dict(w_in=1, w_out=0, w_q=0, w_kv=1, w_xo=0)
    full_shape = dict(w_in=(D_MODEL, IN_DIM), w_out=(MIX_DIM, D_MODEL), w_q=(D_MODEL, D_MODEL),
                      w_kv=(D_MODEL, 2 * D_MODEL), w_xo=(D_MODEL, D_MODEL))
    shard_size = {k: big_w[k].shape[shard_axis[k]] for k in big_names}
    others = big_names[1:]
    gather_spec = lambda k: (full_shape[k], shard_axis[k], shard_size[k])
    reduce_spec = lambda k: (shard_axis[k], shard_size[k])
    ws = gm_ws[0]
    bs_b = jnp.broadcast_to(gm_bs[0][:, :, None], (8, CHUNK, CHUNK))
    g_f = norm_final_g.reshape(1, D_MODEL)

    px, py = lax.axis_index("x"), lax.axis_index("y")
    chip_order = jnp.stack([2 * px + py, 2 * (1 - px) + py, 2 * px + 1 - py, 2 * (1 - px) + 1 - py]).astype(jnp.int32)
    (proj, h1, h1t, w_in_full), ((w_out_full, conv_g),) = _inproj(
        x2, norm_mix_g, big_w["w_in"].astype(BF16), chip_order,
        plans=[_gather_plan([big_w["w_out"].astype(BF16), pad_taps(conv_w)[None]], [gather_spec("w_out"), ((N_DEV, 8, CHUNK), 0, 1)])])
    conv_full = conv_g.transpose(1, 0, 2).reshape(8, D_MODEL)
    attn_w = ["w_q", "w_kv", "w_xo"]
    cat, (gathered,) = _mixer_fwd(
        proj, conv_full, gm_ln_g, gm_ln_b, ws, bs_b,
        plans=[_gather_plan([big_w[k].astype(BF16) for k in attn_w], [gather_spec(k) for k in attn_w])])
    wf = dict(zip(attn_w, gathered), w_out=w_out_full)
    k, vv, mt = _kv_fwd(mem[0], norm_mem_g, wf["w_kv"])
    (dcat, catt, h2t, ot, dx2b, dqb, dx1b, dk, dvv, d_gf, d_gx, sq_err) = _attn(
        x2, cat, target, wf["w_out"], wf["w_q"], wf["w_xo"], k, vv, norm_x_g, g_f)

    def pair_sums(names, grads, got):
        return [_pair_sum("pair_sum_" + k, grads[k], got[i], *reduce_spec(k), core) for i, k in enumerate(names)]

    sibling_plan = lambda names, grads: _sibling_plan([grads[k] for k in names], [reduce_spec(k) for k in names])
    early, late = ["w_out", "w_kv", "w_q"], ["w_xo"]
    big_g = dict(w_out=_matmul_f32("dw_out", catt, dx1b)[0])
    big_g["w_kv"], d_gmem = _kv_bwd(mem[0], wf["w_kv"], mt, dk, dvv)
    big_g["w_q"], (got_a,) = _matmul_f32("dw_q", h2t, dqb, plans=[sibling_plan(early[:2], big_g)])
    big_g["w_xo"], (got_b,) = _matmul_f32("dw_xo", ot, dx2b, plans=[sibling_plan(early[2:], big_g)])
    got_early = got_a + got_b
    (dproj, d_cw, d_lng, d_lnb, d_ws, d_bs), (parts_early, got_late) = _mixer_bwd(
        proj, dcat, conv_full, gm_ln_g, gm_ln_b, ws, bs_b,
        plans=[_chip_plan(pair_sums(early, big_g, got_early)), sibling_plan(late, big_g)])
    half = D_MODEL // 2
    in_spec = reduce_spec("w_in")
    dw_top, _ = _matmul_f32("dw_in_top", h1t, dproj, rows=(0, half))
    pair_xo = pair_sums(late, big_g, got_late)[0]
    dw_bot, (got_top, (ws_all,)) = _matmul_f32(
        "dw_in_bot", h1t, dproj, rows=(half, half),
        plans=[_sibling_plan([dw_top], [in_spec]),
               _gather_plan([d_ws.reshape(1, 8 * CHUNK, CHUNK)], [((N_DEV, 8 * CHUNK, CHUNK), 0, 1)])])
    pair_top, ((got_bot,),) = _pair_sum("pair_sum_w_in_top", dw_top, got_top[0], *in_spec, core,
                                        plans=[_sibling_plan([dw_bot], [in_spec])])
    pair_bot = _pair_sum("pair_sum_w_in_bot", dw_bot, got_bot, *in_spec, core)
    in_flight, token = _chip_exchange_start([pair_top, pair_bot, pair_xo])
    n_tiles = s // min(TM_DH1, s)
    (grad_x, d_gmix), _ = _dh1("dh1", dproj, w_in_full, x2, dx1b, norm_mix_g + token[0, 0], (0, n_tiles))
    big_out = {}
    for k, got_k in zip(early, parts_early):
        big_out[k], _ = _adamw("adamw_" + k, big_w[k], big_m[k], big_v[k], got_k)
    ws_out, _ = _adamw("adamw_gm_ws", ws.reshape(8 * CHUNK, CHUNK), m_gm_ws.reshape(8 * CHUNK, CHUNK), v_gm_ws.reshape(8 * CHUNK, CHUNK), ws_all)
    land_top, land_bot, land_xo = _chip_exchange_wait(in_flight, [d_gmix, ws_out[0]] + [big_out[k][0] for k in early])
    my_chip = (2 * px + py).astype(jnp.int32).reshape(1)

    vec_names = ["norm_mix_g", "gm_ln_g", "gm_ln_b", "gm_bs", "norm_x_g", "norm_mem_g", "norm_final_g"]
    vec_g = [d_gmix, d_lng, d_lnb, d_bs, d_gx, d_gmem, d_gf]
    small = jnp.concatenate(vec_g + [sq_err, d_cw], axis=0)
    big_out["w_in"], ((small_all,),) = _adamw(
        "adamw_w_in", big_w["w_in"], big_m["w_in"], big_v["w_in"], [land_top, land_bot],
        own=(my_chip, [pair_top, pair_bot]), plans=[_gather_plan([small[None]], [((N_DEV, 16, D_MODEL), 0, 1)])])
    loss = 0.5 * jnp.sum(small_all[:, 7, :]) / D_MODEL
    big_out["w_xo"], _ = _adamw("adamw_w_xo", big_w["w_xo"], big_m["w_xo"], big_v["w_xo"], land_xo, own=(my_chip, [pair_xo]))
    row = lambda t: t.reshape(1, D_MODEL)
    vec_w = [norm_mix_g, gm_ln_g, gm_ln_b, row(gm_bs), norm_x_g, norm_mem_g, row(norm_final_g)]
    vec_m = [m_norm_mix_g, m_gm_ln_g, m_gm_ln_b, row(m_gm_bs), m_norm_x_g, m_norm_mem_g, row(m_norm_final_g)]
    vec_v = [v_norm_mix_g, v_gm_ln_g, v_gm_ln_b, row(v_gm_bs), v_norm_x_g, v_norm_mem_g, row(v_norm_final_g)]
    stack8 = lambda ts: jnp.concatenate(ts + [jnp.ones((1, D_MODEL), F32)], axis=0)
    vec_out, _ = _adamw("adamw_vectors", stack8(vec_w), stack8(vec_m), stack8(vec_v), small_all[:, :8, :])
    conv_parts = lax.dynamic_slice(small_all, (0, 8, dev * CHUNK), (N_DEV, 8, CHUNK))
    conv_out, _ = _adamw("adamw_conv", pad_taps(conv_w), pad_taps(m_conv_w), jnp.pad(v_conv_w[0], ((0, 5), (0, 0)), constant_values=1.0), conv_parts)

    def result(name, which):
        if name in big_out:
            return big_out[name][which][None]
        if name == "conv_w":
            return conv_out[which][:3][None]
        if name == "gm_ws":
            return ws_out[which].reshape(1, 8, CHUNK, CHUNK)
        r = vec_out[which][vec_names.index(name)]
        if name == "gm_bs":
            return r.reshape(1, 8, CHUNK)
        if name == "norm_final_g":
            return r
        return r.reshape(1, D_MODEL)

    weights = ["norm_mix_g", "w_in", "conv_w", "gm_ln_g", "gm_ln_b", "gm_ws", "gm_bs", "w_out", "norm_x_g", "norm_mem_g", "w_q", "w_kv", "w_xo", "norm_final_g"]
    outs = [loss, grad_x[None]]
    for which in range(4):
        outs += [result(name, which) for name in weights]
    return tuple(outs)
```

```python
import math

import jax
import jax.numpy as jnp
from jax import lax
from jax.experimental import pallas as pl
from jax.experimental.pallas import tpu as pltpu

F32 = jnp.float32
BF16 = jnp.bfloat16
MESH = pl.DeviceIdType.MESH

D_MODEL = 1024
N_DEV = 8
N_CHIP = 4
CHUNK = 128
GROUPS = 7
IN_DIM = GROUPS * D_MODEL
SHARD_IN = IN_DIM // N_DEV
MIX_DIM = 2 * D_MODEL
X_HEADS = 4
X_HEAD_DIM = D_MODEL // X_HEADS
MEM_LEN = 256
EPS = 1e-6
GELU_K0 = math.sqrt(2.0 / math.pi)
GELU_K1 = 0.044715

ADAM_LR = 0.001
ADAM_B1 = 0.9
ADAM_B2 = 0.999
ADAM_EPS = 1e-08
ADAM_WD = 0.01
ADAM_STEP = 10

TM_INPROJ = 512
TM_ATTN = 512
SUB_ATTN = 256
TM_DH1 = 512
TM_DW = 512
TK_DW = 4096
TR_PAIR_SUM = 1024


def _dot(a, b):
    return lax.dot_general(a, b, (((1,), (0,)), ((), ())), preferred_element_type=F32)


def _dot_nt(a, b):
    return lax.dot_general(a, b, (((1,), (1,)), ((), ())), preferred_element_type=F32)


def _rows8(v):
    return jnp.sum(v.reshape(v.shape[0] // 8, 8, v.shape[1]), axis=0)


def _gelu(x):
    x2 = x * x
    t = jnp.tanh(GELU_K0 * (x + GELU_K1 * (x * x2)))
    g = 0.5 * x * (1.0 + t)
    dg = 0.5 * (1.0 + t) + 0.5 * x * (1.0 - t * t) * (GELU_K0 * (1.0 + 3.0 * GELU_K1 * x2))
    return g, dg


def _silu(z):
    s = jax.nn.sigmoid(z)
    return z * s, s * (1.0 + z * (1.0 - s))


def _rms(v):
    r = lax.rsqrt(jnp.mean(v * v, axis=-1, keepdims=True) + EPS)
    return v * r, r


def _rms_bwd(dy_g, vh, r):
    return r * (dy_g - vh * jnp.mean(dy_g * vh, axis=-1, keepdims=True))


def _place():
    return lax.axis_index("x"), lax.axis_index("y"), lax.axis_index("c")


def _other_chips(x, y):
    return [(1 - x, y), (x, 1 - y), (1 - x, 1 - y)]


def _pow2_divisor(n):
    return n & (-n)


def _shard_view(ref, axis, size, d):
    start = d * size
    align = _pow2_divisor(size)
    if align > 1:
        start = pl.multiple_of(start, align)
    idx = [slice(None)] * len(ref.shape)
    idx[axis] = pl.ds(start, size)
    return ref.at[tuple(idx)]


class _Plan:
    def __init__(self, ins, out_shape, sems, start, finish, mid=None, aliases=None):
        self.ins, self.out_shape, self.sems = list(ins), list(out_shape), list(sems)
        self.start, self.mid, self.finish = start, mid, finish
        self.aliases = dict(aliases or {})


def _gather_plan(shards, specs):
    n = len(shards)
    n_copy = 7

    def copies(ins, outs, sems):
        send_sems, recv_sems, local_sems = sems
        x, y, c = _place()
        me, sibling = (x, y, c), (x, y, 1 - c)
        chips = _other_chips(x, y)

        def block(a, p):
            _, axis, size = specs[a]
            return _shard_view(outs[a], axis, size, 4 * p[0] + 2 * p[1] + p[2])

        def copy(a, k, p, to, src=None):
            return pltpu.make_async_remote_copy(
                src_ref=block(a, p) if src is None else src,
                dst_ref=block(a, p),
                send_sem=send_sems.at[a * n_copy + k],
                recv_sem=recv_sems.at[a * n_copy + k],
                device_id=to,
                device_id_type=MESH,
            )

        def mine():
            return [pltpu.make_async_copy(ins[a], block(a, me), local_sems.at[a]) for a in range(n)]

        def first():
            return [cp for a in range(n) for cp in
                    [copy(a, 0, me, sibling, src=ins[a])] + [copy(a, 1 + j, me, (*chip, c), src=ins[a]) for j, chip in enumerate(chips)]]

        def landed():
            return [copy(a, 1 + j, (*chip, c), me) for j, chip in enumerate(chips) for a in range(n)]

        def passed():
            return [copy(a, 4 + j, (*chip, c), sibling) for j, chip in enumerate(chips) for a in range(n)]

        def last():
            return [cp for a in range(n) for cp in
                    [copy(a, 0, sibling, me)] + [copy(a, 4 + j, (*chip, 1 - c), me) for j, chip in enumerate(chips)]]

        return mine, first, landed, passed, last

    def start(ins, outs, sems):
        mine, first, _, _, _ = copies(ins, outs, sems)
        for cp in mine() + first():
            cp.start()

    def mid(ins, outs, sems):
        _, _, landed, passed, _ = copies(ins, outs, sems)
        for got, fwd in zip(landed(), passed()):
            got.wait_recv()
            fwd.start()

    def finish(ins, outs, sems):
        mine, first, _, passed, last = copies(ins, outs, sems)
        for cp in last():
            cp.wait_recv()
        for cp in first() + passed():
            cp.wait_send()
        for cp in mine():
            cp.wait()

    return _Plan(
        shards,
        [jax.ShapeDtypeStruct(full, s.dtype) for s, (full, _, _) in zip(shards, specs)],
        [pltpu.SemaphoreType.DMA((n * n_copy,)), pltpu.SemaphoreType.DMA((n * n_copy,)), pltpu.SemaphoreType.DMA((n,))],
        start, finish, mid,
    )


def _sibling_plan(grads, specs):
    n = len(grads)

    def shard_shape(a):
        axis, size = specs[a]
        shp = list(grads[a].shape)
        shp[axis] = size
        return tuple(shp)

    def copies(ins, outs, sems):
        send_sems, recv_sems = sems
        x, y, c = _place()
        return [
            pltpu.make_async_remote_copy(
                src_ref=_shard_view(ins[a], specs[a][0], specs[a][1], 2 * q + (1 - c)),
                dst_ref=outs[a].at[q],
                send_sem=send_sems.at[a * N_CHIP + q],
                recv_sem=recv_sems.at[a * N_CHIP + q],
                device_id=(x, y, 1 - c),
                device_id_type=MESH,
            )
            for a in range(n) for q in range(N_CHIP)
        ]

    def start(ins, outs, sems):
        for cp in copies(ins, outs, sems):
            cp.start()

    def finish(ins, outs, sems):
        for cp in copies(ins, outs, sems):
            cp.wait()

    return _Plan(
        grads,
        [jax.ShapeDtypeStruct((N_CHIP, *shard_shape(a)), grads[a].dtype) for a in range(n)],
        [pltpu.SemaphoreType.DMA((n * N_CHIP,)), pltpu.SemaphoreType.DMA((n * N_CHIP,))],
        start, finish,
    )


def _chip_plan(parts, full_rows=None, row0=None, into=None):
    n = len(parts)
    full_rows = [p.shape[1] for p in parts] if full_rows is None else full_rows
    row0 = [0] * n if row0 is None else row0

    def copies(ins, outs, sems):
        send_sems, recv_sems, local_sems = sems
        x, y, c = _place()
        my_chip = 2 * x + y
        chips = _other_chips(x, y)

        def land(a, q):
            return outs[a].at[q, pl.ds(row0[a], parts[a].shape[1]), :]

        def mine():
            return [pltpu.make_async_copy(ins[a].at[my_chip], land(a, my_chip), local_sems.at[a]) for a in range(n)]

        def remote(sending):
            return [
                pltpu.make_async_remote_copy(
                    src_ref=ins[a].at[2 * chip[0] + chip[1]],
                    dst_ref=land(a, my_chip if sending else 2 * chip[0] + chip[1]),
                    send_sem=send_sems.at[a * 3 + j], recv_sem=recv_sems.at[a * 3 + j],
                    device_id=(*chip, c), device_id_type=MESH)
                for a in range(n) for j, chip in enumerate(chips)
            ]

        return mine, remote

    def start(ins, outs, sems):
        mine, remote = copies(ins, outs, sems)
        for cp in mine() + remote(True):
            cp.start()

    def finish(ins, outs, sems):
        mine, remote = copies(ins, outs, sems)
        for cp in remote(False):
            cp.wait_recv()
        for cp in remote(True):
            cp.wait_send()
        for cp in mine():
            cp.wait()

    return _Plan(
        list(parts) + (list(into) if into is not None else []),
        [jax.ShapeDtypeStruct((N_CHIP, full_rows[a], parts[a].shape[2]), parts[a].dtype) for a in range(n)],
        [pltpu.SemaphoreType.DMA((n * 3,)), pltpu.SemaphoreType.DMA((n * 3,)), pltpu.SemaphoreType.DMA((n,))],
        start, finish,
        aliases={n + a: a for a in range(n)} if into is not None else None,
    )


def _call(body, *, name, operands, out_shape, grid=(), in_specs=None, out_specs=None, scratch_shapes=(),
          plans=(), aliases=None, prefetch=(), start_after_body=False, pass_on_at=0.6):
    operands, out_shape, scratch_shapes = list(operands), list(out_shape), list(scratch_shapes)
    n_pf, n_in, n_out, n_scr = len(prefetch), len(operands), len(out_shape), len(scratch_shapes)
    in_vmem = pl.BlockSpec(memory_space=pltpu.VMEM)
    in_hbm = pl.BlockSpec(memory_space=pl.ANY)
    in_specs = [in_vmem] * n_in if in_specs is None else list(in_specs)
    out_specs = [in_vmem] * n_out if out_specs is None else list(out_specs)
    io_alias = {n_pf + i: o for i, o in (aliases or {}).items()}
    pos_in, pos_out = n_pf + n_in, n_out
    for p in plans:
        for i, o in p.aliases.items():
            io_alias[pos_in + i] = pos_out + o
        pos_in += len(p.ins)
        pos_out += len(p.out_shape)
    steps = math.prod(grid) if grid else 1
    mid_step = min(steps - 1, int(steps * pass_on_at))

    def wrapped(*refs):
        refs = list(refs)
        take = lambda k: [refs.pop(0) for _ in range(k)]
        pf = take(n_pf)
        ins, p_ins = take(n_in), [take(len(p.ins)) for p in plans]
        outs, p_outs = take(n_out), [take(len(p.out_shape)) for p in plans]
        scr, p_sems = take(n_scr), [take(len(p.sems)) for p in plans]
        step = 0
        for ax, g in enumerate(grid):
            step = step * g + pl.program_id(ax)

        def hook(kind, at):
            todo = [(getattr(p, kind), a, b, c) for p, a, b, c in zip(plans, p_ins, p_outs, p_sems) if getattr(p, kind)]

            def run():
                for fn, a, b, c in todo:
                    fn(a, b, c)

            if todo and grid:
                pl.when(step == at)(run)
            elif todo:
                run()

        if not start_after_body:
            hook("start", 0)
        if body is not None:
            body(*pf, *ins, *outs, *scr)
        if start_after_body:
            hook("start", 0)
        hook("mid", mid_step)
        hook("finish", steps - 1)

    layout = dict(
        grid=grid,
        in_specs=in_specs + [in_hbm] * sum(len(p.ins) for p in plans),
        out_specs=out_specs + [in_hbm] * sum(len(p.out_shape) for p in plans),
        scratch_shapes=scratch_shapes + [s for p in plans for s in p.sems],
    )
    if n_pf:
        layout = dict(grid_spec=pltpu.PrefetchScalarGridSpec(num_scalar_prefetch=n_pf, **layout))
    res = pl.pallas_call(
        wrapped,
        name=name,
        out_shape=out_shape + [o for p in plans for o in p.out_shape],
        input_output_aliases=io_alias,
        compiler_params=pltpu.CompilerParams(dimension_semantics=("arbitrary",) * len(grid)) if grid else None,
        **layout,
    )(*prefetch, *operands, *[a for p in plans for a in p.ins])
    res = list(res)
    outs, plan_outs = res[:n_out], []
    pos = n_out
    for p in plans:
        plan_outs.append(res[pos : pos + len(p.out_shape)])
        pos += len(p.out_shape)
    return outs, plan_outs


def _inproj(x, g, w_shard, order, plans=()):
    s = x.shape[0]
    tm = min(TM_INPROJ, s)
    nt = s // tm
    wide = 2 * SHARD_IN
    fwd_at = nt // 2

    def body(order_ref, x_ref, g_ref, w_hbm, proj_ref, h1_ref, h1t_ref, wfull_hbm,
             w_scr, h1_scr, send_sems, recv_sems, local_sems):
        k, i = pl.program_id(0), pl.program_id(1)
        px, py, c = _place()
        me, sibling = (px, py, c), (px, py, 1 - c)
        chips = _other_chips(px, py)

        def half(slot, core):
            return w_scr.at[slot, :, pl.ds(pl.multiple_of(core * SHARD_IN, 128), SHARD_IN)]

        def copy(n, slot, core, to, src=None):
            return pltpu.make_async_remote_copy(
                src_ref=half(slot, core) if src is None else src, dst_ref=half(slot, core),
                send_sem=send_sems.at[n], recv_sem=recv_sems.at[n], device_id=to, device_id_type=MESH)

        def mine():
            return pltpu.make_async_copy(w_hbm, half(0, c), local_sems.at[0])

        def first():
            return [copy(0, 0, c, sibling, src=w_hbm)] + [copy(1 + j, 1 + j, c, (*chip, c), src=w_hbm) for j, chip in enumerate(chips)]

        def save(slot):
            cols = pl.ds(pl.multiple_of(order_ref[slot] * wide, 128), wide)
            return pltpu.make_async_copy(w_scr.at[slot], wfull_hbm.at[:, cols], local_sems.at[1 + slot])

        @pl.when((k == 0) & (i == 0))
        def _():
            mine().start()
            for cp in first():
                cp.start()
            mine().wait()
            copy(0, 0, 1 - c, me).wait_recv()

        for j, chip in enumerate(chips):
            @pl.when((k == j) & (i == fwd_at))
            def _():
                copy(1 + j, 1 + j, c, me).wait_recv()
                copy(4 + j, 1 + j, c, sibling).start()

            @pl.when((k == j + 1) & (i == 0))
            def _():
                copy(4 + j, 1 + j, 1 - c, me).wait_recv()

        @pl.when((k == 3) & (i == 0))
        def _():
            for slot in range(N_CHIP):
                save(slot).start()

        @pl.when(k == 0)
        def _():
            xh, _ = _rms(x_ref[...])
            h = xh * g_ref[...]
            h1_ref[...] = h.astype(BF16)
            h1t_ref[...] = h.T.astype(BF16)
            h1_scr[pl.ds(pl.multiple_of(i * tm, tm), tm), :] = h.astype(BF16)

        proj_ref[...] = _dot(h1_scr[pl.ds(pl.multiple_of(i * tm, tm), tm), :], w_scr[k]).astype(BF16)

        @pl.when((k == 3) & (i == nt - 1))
        def _():
            for cp in first() + [copy(4 + j, 1 + j, c, sibling) for j in range(3)]:
                cp.wait_send()
            for slot in range(N_CHIP):
                save(slot).wait()

    once = lambda k, i, order_ref: (jnp.where(k == 0, i, nt - 1), 0)
    return _call(
        body,
        name="inproj",
        prefetch=[order],
        operands=[x, g, w_shard],
        grid=(N_CHIP, nt),
        in_specs=[
            pl.BlockSpec((tm, D_MODEL), once),
            pl.BlockSpec((1, D_MODEL), lambda k, i, order_ref: (0, 0)),
            pl.BlockSpec(memory_space=pl.ANY),
        ],
        out_specs=[
            pl.BlockSpec((tm, wide), lambda k, i, order_ref: (i, order_ref[k])),
            pl.BlockSpec((tm, D_MODEL), once),
            pl.BlockSpec((D_MODEL, tm), lambda k, i, order_ref: (0, jnp.where(k == 0, i, nt - 1))),
            pl.BlockSpec(memory_space=pl.ANY),
        ],
        out_shape=[
            jax.ShapeDtypeStruct((s, IN_DIM), BF16),
            jax.ShapeDtypeStruct((s, D_MODEL), BF16),
            jax.ShapeDtypeStruct((D_MODEL, s), BF16),
            jax.ShapeDtypeStruct((D_MODEL, IN_DIM), BF16),
        ],
        scratch_shapes=[
            pltpu.VMEM((N_CHIP, D_MODEL, wide), BF16),
            pltpu.VMEM((s, D_MODEL), BF16),
            pltpu.SemaphoreType.DMA((7,)),
            pltpu.SemaphoreType.DMA((7,)),
            pltpu.SemaphoreType.DMA((1 + N_CHIP,)),
        ],
        plans=plans,
        start_after_body=True,
        pass_on_at=0.8,
    )


def _causal_mask():
    row = lax.broadcasted_iota(jnp.int32, (CHUNK, CHUNK), 0)
    col = lax.broadcasted_iota(jnp.int32, (CHUNK, CHUNK), 1)
    return row, row >= col


def _mixer_fwd(proj, conv_w, ln_g, ln_b, ws, bs_b, plans=()):
    s = proj.shape[0]
    nt = s // CHUNK

    def body(proj_ref, cw_ref, lng_ref, lnb_ref, ws_ref, bsb_ref, cat_ref, pcar_ref):
        @pl.when(pl.program_id(0) == 0)
        def _():
            pcar_ref[...] = jnp.zeros_like(pcar_ref)

        row, tril = _causal_mask()
        for j in range(8):
            cs = slice(CHUNK * j, CHUNK * (j + 1))

            def grp(k):
                return proj_ref[:, k * D_MODEL + CHUNK * j : k * D_MODEL + CHUNK * (j + 1)].astype(F32)

            w = cw_ref[:, cs]
            p = grp(1) * grp(2)
            pc = pcar_ref[:, cs]
            p1 = jnp.where(row == 0, pc[7:8], pltpu.roll(p, 1, 0))
            p2 = jnp.where(row == 0, pc[6:7], jnp.where(row == 1, pc[7:8], pltpu.roll(p, 2, 0)))
            pcar_ref[:, cs] = p[CHUNK - 8 :]
            cv = w[0:1] * p2 + w[1:2] * p1 + w[2:3] * p
            sa, _ = _silu(grp(3))
            cat_ref[:, cs] = ((grp(0) * cv) * sa).astype(BF16)
            gu, _ = _gelu(grp(4))
            gv, _ = _gelu(grp(5))
            dv = gv - jnp.mean(gv, axis=-1, keepdims=True)
            vn = dv * lax.rsqrt(jnp.mean(dv * dv, axis=-1, keepdims=True) + EPS)
            vn = vn * lng_ref[:, cs] + lnb_ref[:, cs]
            wc = jnp.where(tril, ws_ref[j], 0.0).astype(BF16)
            sp = _dot(wc, vn.astype(BF16)) + bsb_ref[j]
            sb, _ = _silu(grp(6))
            cat_ref[:, D_MODEL + CHUNK * j : D_MODEL + CHUNK * (j + 1)] = ((gu * sp) * sb).astype(BF16)

    whole = lambda shape: pl.BlockSpec(shape, lambda i: (0,) * len(shape))
    (cat,), plan_outs = _call(
        body,
        name="mixer_fwd",
        operands=[proj, conv_w, ln_g, ln_b, ws, bs_b],
        grid=(nt,),
        in_specs=[
            pl.BlockSpec((CHUNK, IN_DIM), lambda i: (i, 0)),
            whole((8, D_MODEL)),
            whole((1, D_MODEL)),
            whole((1, D_MODEL)),
            whole((8, CHUNK, CHUNK)),
            whole((8, CHUNK, CHUNK)),
        ],
        out_specs=[pl.BlockSpec((CHUNK, MIX_DIM), lambda i: (i, 0))],
        out_shape=[jax.ShapeDtypeStruct((s, MIX_DIM), BF16)],
        scratch_shapes=[pltpu.VMEM((8, D_MODEL), F32)],
        plans=plans,
        pass_on_at=0.85,
    )
    return cat, plan_outs


def _mixer_bwd(proj, dcat, conv_w, ln_g, ln_b, ws, bs_b, plans=()):
    s = proj.shape[0]
    nt = s // CHUNK

    def body(proj_ref, halo_ref, dcat_ref, cw_ref, lng_ref, lnb_ref, ws_ref, bsb_ref,
             dproj_ref, dcw_ref, dlng_ref, dlnb_ref, dws_ref, dbs_ref,
             car_ref, acc_cw, acc_lng, acc_lnb, acc_bs):
        i = pl.program_id(0)
        tile = nt - 1 - i

        @pl.when(i == 0)
        def _():
            car_ref[...] = jnp.zeros_like(car_ref)
            acc_cw[...] = jnp.zeros_like(acc_cw)
            acc_lng[...] = jnp.zeros_like(acc_lng)
            acc_lnb[...] = jnp.zeros_like(acc_lnb)
            acc_bs[...] = jnp.zeros_like(acc_bs)
            dws_ref[...] = jnp.zeros_like(dws_ref)

        row, tril = _causal_mask()
        has_prev = jnp.where(tile > 0, 1.0, 0.0).astype(F32)
        for j in range(8):
            cs = slice(CHUNK * j, CHUNK * (j + 1))

            def col(k):
                return slice(k * D_MODEL + CHUNK * j, k * D_MODEL + CHUNK * (j + 1))

            def grp(k):
                return proj_ref[:, col(k)].astype(F32)

            w = cw_ref[:, cs]
            gb, gc, xa = grp(0), grp(1), grp(2)
            p = gc * xa
            pprev = halo_ref[:, col(1)].astype(F32) * halo_ref[:, col(2)].astype(F32) * has_prev
            p1 = jnp.where(row == 0, pprev[15:16], pltpu.roll(p, 1, 0))
            p2 = jnp.where(row == 0, pprev[14:15], jnp.where(row == 1, pprev[15:16], pltpu.roll(p, 2, 0)))
            cv = w[0:1] * p2 + w[1:2] * p1 + w[2:3] * p
            za = grp(3)
            sa, dsa = _silu(za)
            da = dcat_ref[:, cs].astype(F32)
            dproj_ref[:, col(0)] = (da * cv * sa).astype(BF16)
            dproj_ref[:, col(3)] = (da * gb * cv * dsa).astype(BF16)
            dcv = da * gb * sa
            nxt = car_ref[:, cs]
            d1 = jnp.where(row == CHUNK - 1, nxt[0:1], pltpu.roll(dcv, CHUNK - 1, 0))
            d2 = jnp.where(row == CHUNK - 2, nxt[0:1], jnp.where(row == CHUNK - 1, nxt[1:2], pltpu.roll(dcv, CHUNK - 2, 0)))
            car_ref[:, cs] = dcv[0:8]
            dp = w[2:3] * dcv + w[1:2] * d1 + w[0:1] * d2
            dproj_ref[:, col(1)] = (dp * xa).astype(BF16)
            dproj_ref[:, col(2)] = (dp * gc).astype(BF16)
            acc_cw[0, :, cs] += _rows8(dcv * p2)
            acc_cw[1, :, cs] += _rows8(dcv * p1)
            acc_cw[2, :, cs] += _rows8(dcv * p)
            gu, dgu = _gelu(grp(4))
            gv, dgv = _gelu(grp(5))
            dv = gv - jnp.mean(gv, axis=-1, keepdims=True)
            rstd = lax.rsqrt(jnp.mean(dv * dv, axis=-1, keepdims=True) + EPS)
            vnh = dv * rstd
            lng = lng_ref[:, cs]
            vnb = (vnh * lng + lnb_ref[:, cs]).astype(BF16)
            wcf = jnp.where(tril, ws_ref[j], 0.0)
            sp = _dot(wcf.astype(BF16), vnb) + bsb_ref[j]
            zb = grp(6)
            sb, dsb = _silu(zb)
            db = dcat_ref[:, D_MODEL + CHUNK * j : D_MODEL + CHUNK * (j + 1)].astype(F32)
            dproj_ref[:, col(4)] = (db * sp * sb * dgu).astype(BF16)
            dproj_ref[:, col(6)] = (db * gu * sp * dsb).astype(BF16)
            dsp = db * gu * sb
            acc_bs[j] += dsp
            dspb = dsp.astype(BF16)
            dws_ref[j] += _dot_nt(dspb, vnb)
            dvn = _dot(wcf.T.astype(BF16), dspb)
            acc_lnb[:, cs] += _rows8(dvn)
            acc_lng[:, cs] += _rows8(dvn * vnh)
            dvh = dvn * lng
            dgvv = rstd * (dvh - jnp.mean(dvh, axis=-1, keepdims=True) - vnh * jnp.mean(dvh * vnh, axis=-1, keepdims=True))
            dproj_ref[:, col(5)] = (dgvv * dgv).astype(BF16)

        @pl.when(i == nt - 1)
        def _():
            dlng_ref[...] = jnp.sum(acc_lng[...], axis=0, keepdims=True)
            dlnb_ref[...] = jnp.sum(acc_lnb[...], axis=0, keepdims=True)
            dcw_ref[...] = jnp.zeros_like(dcw_ref)
            for k in range(3):
                dcw_ref[k : k + 1, :] = jnp.sum(acc_cw[k], axis=0, keepdims=True)
            for j in range(8):
                dws_ref[j] = jnp.where(tril, dws_ref[j], 0.0)
                dbs_ref[:, CHUNK * j : CHUNK * (j + 1)] = jnp.sum(acc_bs[j].T, axis=0, keepdims=True)

    whole = lambda shape: pl.BlockSpec(shape, lambda i: (0,) * len(shape))
    halo_rows = 16
    per = CHUNK // halo_rows
    return _call(
        body,
        name="mixer_bwd",
        operands=[proj, proj, dcat, conv_w, ln_g, ln_b, ws, bs_b],
        plans=plans,
        grid=(nt,),
        in_specs=[
            pl.BlockSpec((CHUNK, IN_DIM), lambda i: (nt - 1 - i, 0)),
            pl.BlockSpec((halo_rows, IN_DIM), lambda i: (jnp.maximum((nt - 1 - i) * per - 1, 0), 0)),
            pl.BlockSpec((CHUNK, MIX_DIM), lambda i: (nt - 1 - i, 0)),
            whole((8, D_MODEL)),
            whole((1, D_MODEL)),
            whole((1, D_MODEL)),
            whole((8, CHUNK, CHUNK)),
            whole((8, CHUNK, CHUNK)),
        ],
        out_specs=[
            pl.BlockSpec((CHUNK, IN_DIM), lambda i: (nt - 1 - i, 0)),
            whole((8, D_MODEL)),
            whole((1, D_MODEL)),
            whole((1, D_MODEL)),
            whole((8, CHUNK, CHUNK)),
            whole((1, D_MODEL)),
        ],
        out_shape=[
            jax.ShapeDtypeStruct((s, IN_DIM), BF16),
            jax.ShapeDtypeStruct((8, D_MODEL), F32),
            jax.ShapeDtypeStruct((1, D_MODEL), F32),
            jax.ShapeDtypeStruct((1, D_MODEL), F32),
            jax.ShapeDtypeStruct((8, CHUNK, CHUNK), F32),
            jax.ShapeDtypeStruct((1, D_MODEL), F32),
        ],
        scratch_shapes=[
            pltpu.VMEM((8, D_MODEL), F32),
            pltpu.VMEM((3, 8, D_MODEL), F32),
            pltpu.VMEM((8, D_MODEL), F32),
            pltpu.VMEM((8, D_MODEL), F32),
            pltpu.VMEM((8, CHUNK, CHUNK), F32),
        ],
    )


def _kv_fwd(mem, g_mem, w_kv):
    def body(mem_ref, g_ref, w_ref, k_ref, vv_ref, mt_ref):
        mh, _ = _rms(mem_ref[...])
        m = mh * g_ref[...]
        kv = _dot(m.astype(BF16), w_ref[...])
        k_ref[...] = kv[:, :D_MODEL].astype(BF16)
        vv_ref[...] = kv[:, D_MODEL:].astype(BF16)
        mt_ref[...] = m.T.astype(BF16)

    return pl.pallas_call(
        body,
        name="kv_fwd",
        out_shape=[
            jax.ShapeDtypeStruct((MEM_LEN, D_MODEL), BF16),
            jax.ShapeDtypeStruct((MEM_LEN, D_MODEL), BF16),
            jax.ShapeDtypeStruct((D_MODEL, MEM_LEN), BF16),
        ],
    )(mem, g_mem, w_kv)


def _kv_bwd(mem, w_kv, mt, dk, dvv):
    def body(mem_ref, w_ref, mt_ref, dk_ref, dvv_ref, dw_ref, dg_ref):
        dkv = jnp.concatenate([dk_ref[...], dvv_ref[...]], axis=1).astype(BF16)
        dw_ref[...] = _dot(mt_ref[...], dkv)
        dm = _dot_nt(dkv, w_ref[...])
        mh, _ = _rms(mem_ref[...])
        dg_ref[...] = jnp.sum(dm * mh, axis=0, keepdims=True)

    return pl.pallas_call(
        body,
        name="kv_bwd",
        out_shape=[
            jax.ShapeDtypeStruct((D_MODEL, 2 * D_MODEL), F32),
            jax.ShapeDtypeStruct((1, D_MODEL), F32),
        ],
    )(mem, w_kv, mt, dk, dvv)


def _attn(x, cat, target, w_out, w_q, w_xo, k, vv, g_x, g_f):
    s = x.shape[0]
    tm = min(TM_ATTN, s)
    sub = min(SUB_ATTN, tm)
    nt = s // tm
    scale = 1.0 / math.sqrt(X_HEAD_DIM)

    def body(x_ref, cat_ref, t_ref, wout_ref, wq_ref, wxo_ref, k_ref, vv_ref, gx_ref, gf_ref,
             dcat_ref, catt_ref, h2t_ref, ot_ref, dx2b_ref, dqb_ref, dx1b_ref,
             dk_ref, dvv_ref, dgf_ref, dgx_ref, loss_ref,
             p_scr, q_scr, o_scr, dq_scr, acc_gf, acc_gx, acc_loss):
        i = pl.program_id(0)

        @pl.when(i == 0)
        def _():
            dk_ref[...] = jnp.zeros_like(dk_ref)
            dvv_ref[...] = jnp.zeros_like(dvv_ref)
            acc_gf[...] = jnp.zeros_like(acc_gf)
            acc_gx[...] = jnp.zeros_like(acc_gx)
            acc_loss[...] = jnp.zeros_like(acc_loss)

        subs = [slice(u * sub, (u + 1) * sub) for u in range(tm // sub)]
        heads = [slice(X_HEAD_DIM * h, X_HEAD_DIM * (h + 1)) for h in range(X_HEADS)]
        gx, gf = gx_ref[...], gf_ref[...]
        x1, x1h, r2 = [], [], []
        for rs in subs:
            cat = cat_ref[rs, :]
            catt_ref[:, rs] = cat.astype(F32).T.astype(BF16)
            v = x_ref[rs, :] + _dot(cat, wout_ref[...])
            vh, r = _rms(v)
            h2 = vh * gx
            h2t_ref[:, rs] = h2.T.astype(BF16)
            q_scr[rs, :] = _dot(h2.astype(BF16), wq_ref[...]).astype(BF16)
            x1.append(v), x1h.append(vh), r2.append(r)
        for rs in subs:
            for hs in heads:
                sc = _dot_nt(q_scr[rs, hs], k_ref[:, hs]) * scale
                e = jnp.exp(sc - jnp.max(sc, axis=-1, keepdims=True))
                p = e / jnp.sum(e, axis=-1, keepdims=True)
                p_scr[rs, hs] = p
                o_scr[rs, hs] = _dot(p.astype(BF16), vv_ref[:, hs])
        dx2, sq, d_gf = [], None, None
        for u, rs in enumerate(subs):
            o = o_scr[rs, :]
            ot_ref[:, rs] = o.T.astype(BF16)
            x2 = x1[u] + _dot(o.astype(BF16), wxo_ref[...])
            x2h, r3 = _rms(x2)
            err = x2h * gf - t_ref[rs, :]
            dy = err * (1.0 / D_MODEL)
            sq = _rows8(err * err) if sq is None else sq + _rows8(err * err)
            d_gf = _rows8(dy * x2h) if d_gf is None else d_gf + _rows8(dy * x2h)
            dx2.append(_rms_bwd(dy * gf, x2h, r3))
        acc_loss[...] += sq
        acc_gf[...] += d_gf
        d_vv, d_k = [None] * X_HEADS, [None] * X_HEADS
        for u, rs in enumerate(subs):
            dx2b = dx2[u].astype(BF16)
            dx2b_ref[rs, :] = dx2b
            dob = _dot_nt(dx2b, wxo_ref[...]).astype(BF16)
            for h, hs in enumerate(heads):
                p = p_scr[rs, hs]
                dp = _dot_nt(dob[:, hs], vv_ref[:, hs])
                ds = p * (dp - jnp.sum(dp * p, axis=-1, keepdims=True)) * scale
                dq_scr[rs, hs] = _dot(ds.astype(BF16), k_ref[:, hs]).astype(BF16)
                dvv_h = _dot(p.T.astype(BF16), dob[:, hs])
                dk_h = _dot(ds.T.astype(BF16), q_scr[rs, hs])
                d_vv[h] = dvv_h if d_vv[h] is None else d_vv[h] + dvv_h
                d_k[h] = dk_h if d_k[h] is None else d_k[h] + dk_h
        for h, hs in enumerate(heads):
            dvv_ref[:, hs] += d_vv[h]
            dk_ref[:, hs] += d_k[h]
        d_gx = None
        for u, rs in enumerate(subs):
            dqb = dq_scr[rs, :]
            dqb_ref[rs, :] = dqb
            dh2 = _dot_nt(dqb, wq_ref[...])
            d_gx = _rows8(dh2 * x1h[u]) if d_gx is None else d_gx + _rows8(dh2 * x1h[u])
            dx1 = dx2[u] + _rms_bwd(dh2 * gx, x1h[u], r2[u])
            dx1b = dx1.astype(BF16)
            dx1b_ref[rs, :] = dx1b
            dcat_ref[rs, :] = _dot_nt(dx1b, wout_ref[...]).astype(BF16)
        acc_gx[...] += d_gx

        @pl.when(i == nt - 1)
        def _():
            dgf_ref[...] = jnp.sum(acc_gf[...], axis=0, keepdims=True)
            dgx_ref[...] = jnp.sum(acc_gx[...], axis=0, keepdims=True)
            loss_ref[...] = jnp.sum(acc_loss[...], axis=0, keepdims=True)

    whole = lambda shape: pl.BlockSpec(shape, lambda i: (0,) * len(shape))
    once = lambda shape: pl.BlockSpec(shape, lambda i: (0,) * len(shape), pipeline_mode=pl.Buffered(1))
    rows = lambda width: pl.BlockSpec((tm, width), lambda i: (i, 0))
    cols = lambda height: pl.BlockSpec((height, tm), lambda i: (0, i))
    vec = jax.ShapeDtypeStruct((1, D_MODEL), F32)
    return pl.pallas_call(
        body,
        name="attn",
        grid=(nt,),
        in_specs=[
            rows(D_MODEL), rows(MIX_DIM), rows(D_MODEL),
            once((MIX_DIM, D_MODEL)), once((D_MODEL, D_MODEL)), once((D_MODEL, D_MODEL)),
            once((MEM_LEN, D_MODEL)), once((MEM_LEN, D_MODEL)),
            whole((1, D_MODEL)), whole((1, D_MODEL)),
        ],
        out_specs=[
            rows(MIX_DIM), cols(MIX_DIM), cols(D_MODEL), cols(D_MODEL),
            rows(D_MODEL), rows(D_MODEL), rows(D_MODEL),
            whole((MEM_LEN, D_MODEL)), whole((MEM_LEN, D_MODEL)),
            whole((1, D_MODEL)), whole((1, D_MODEL)), whole((1, D_MODEL)),
        ],
        out_shape=[
            jax.ShapeDtypeStruct((s, MIX_DIM), BF16),
            jax.ShapeDtypeStruct((MIX_DIM, s), BF16),
            jax.ShapeDtypeStruct((D_MODEL, s), BF16),
            jax.ShapeDtypeStruct((D_MODEL, s), BF16),
            jax.ShapeDtypeStruct((s, D_MODEL), BF16),
            jax.ShapeDtypeStruct((s, D_MODEL), BF16),
            jax.ShapeDtypeStruct((s, D_MODEL), BF16),
            jax.ShapeDtypeStruct((MEM_LEN, D_MODEL), F32),
            jax.ShapeDtypeStruct((MEM_LEN, D_MODEL), F32),
            vec, vec, vec,
        ],
        scratch_shapes=[
            pltpu.VMEM((tm, D_MODEL), F32),
            pltpu.VMEM((tm, D_MODEL), BF16),
            pltpu.VMEM((tm, D_MODEL), F32),
            pltpu.VMEM((tm, D_MODEL), BF16),
            pltpu.VMEM((8, D_MODEL), F32),
            pltpu.VMEM((8, D_MODEL), F32),
            pltpu.VMEM((8, D_MODEL), F32),
        ],
        compiler_params=pltpu.CompilerParams(dimension_semantics=("arbitrary",)),
    )(x, cat, target, w_out, w_q, w_xo, k, vv, g_x, g_f)


def _matmul_f32(name, a, b, rows=None, plans=()):
    first, m = (0, a.shape[0]) if rows is None else rows
    kk = a.shape[1]
    n = b.shape[1]
    tm, tn, tk = min(m, TM_DW), min(n, D_MODEL), min(kk, TK_DW)
    nk = kk // tk
    i0 = first // tm

    def body(a_ref, b_ref, o_ref):
        prod = _dot(a_ref[...], b_ref[...])
        if nk == 1:
            o_ref[...] = prod
            return
        kid = pl.program_id(2)

        @pl.when(kid == 0)
        def _():
            o_ref[...] = prod

        @pl.when(kid > 0)
        def _():
            o_ref[...] += prod

    (out,), plan_outs = _call(
        body,
        name=name,
        operands=[a, b],
        grid=(m // tm, n // tn, nk),
        in_specs=[pl.BlockSpec((tm, tk), lambda i, j, k: (i + i0, k)), pl.BlockSpec((tk, tn), lambda i, j, k: (k, j))],
        out_specs=[pl.BlockSpec((tm, tn), lambda i, j, k: (i, j))],
        out_shape=[jax.ShapeDtypeStruct((m, n), F32)],
        plans=plans,
    )
    return out, plan_outs


def _dh1(name, dproj, w_in, x, dx1, g, tiles, into=None, plans=()):
    s = x.shape[0]
    tm = min(TM_DH1, s)
    t0, nt = tiles[0], tiles[1] - tiles[0]

    def body(dp_ref, w_ref, x_ref, dx1_ref, g_ref, *rest):
        gx_ref, dg_ref, acc_g = rest[-3:]
        i = pl.program_id(0)

        @pl.when(i == 0)
        def _():
            acc_g[...] = jnp.zeros_like(acc_g)

        dh1 = _dot_nt(dp_ref[...], w_ref[...])
        xh, r = _rms(x_ref[...])
        acc_g[...] += _rows8(dh1 * xh)
        gx_ref[...] = dx1_ref[...].astype(F32) + _rms_bwd(dh1 * g_ref[...], xh, r)

        @pl.when(i == nt - 1)
        def _():
            dg_ref[...] = jnp.sum(acc_g[...], axis=0, keepdims=True)

    rows = pl.BlockSpec((tm, D_MODEL), lambda i: (i + t0, 0))
    return _call(
        body,
        name=name,
        operands=[dproj, w_in, x, dx1, g] + ([into] if into is not None else []),
        grid=(nt,),
        in_specs=[
            pl.BlockSpec((tm, IN_DIM), lambda i: (i + t0, 0)),
            pl.BlockSpec((D_MODEL, IN_DIM), lambda i: (0, 0), pipeline_mode=pl.Buffered(1)),
            rows,
            rows,
            pl.BlockSpec((1, D_MODEL), lambda i: (0, 0)),
        ] + ([pl.BlockSpec(memory_space=pl.ANY)] if into is not None else []),
        out_specs=[rows, pl.BlockSpec((1, D_MODEL), lambda i: (0, 0))],
        out_shape=[jax.ShapeDtypeStruct((s, D_MODEL), F32), jax.ShapeDtypeStruct((1, D_MODEL), F32)],
        scratch_shapes=[pltpu.VMEM((8, D_MODEL), F32)],
        plans=plans,
        aliases={5: 0} if into is not None else None,
    )


def _pair_sum(name, grad, got, axis, size, core, plans=()):
    shard = list(grad.shape)
    shard[axis] = size
    r, cdim = shard
    tr = min(r, TR_PAIR_SUM)
    nr = r // tr

    def body(core_ref, g_ref, got_ref, o_ref):
        o_ref[0] = (g_ref[...] + got_ref[0]).astype(BF16)

    if axis == 1:
        g_map = lambda q, t, core_ref: (t, 2 * q + core_ref[0])
    else:
        g_map = lambda q, t, core_ref: ((2 * q + core_ref[0]) * nr + t, 0)
    (out,), plan_outs = _call(
        body,
        name=name,
        prefetch=[core],
        operands=[grad, got],
        grid=(N_CHIP, nr),
        in_specs=[
            pl.BlockSpec((tr, cdim), g_map),
            pl.BlockSpec((1, tr, cdim), lambda q, t, core_ref: (q, t, 0)),
        ],
        out_specs=[pl.BlockSpec((1, tr, cdim), lambda q, t, core_ref: (q, t, 0))],
        out_shape=[jax.ShapeDtypeStruct((N_CHIP, r, cdim), BF16)],
        plans=plans,
    )
    return (out, plan_outs) if plans else out


def _adamw(name, groups, steps, own_slot=None, plans=()):
    c1 = 1.0 / (1.0 - ADAM_B1**ADAM_STEP)
    c2 = 1.0 / (1.0 - ADAM_B2**ADAM_STEP)
    lists = lambda p: list(p) if isinstance(p, (list, tuple)) else [p]
    groups = [(g[0], g[1], g[2], lists(g[3]), lists(g[4]) if len(g) > 4 else []) for g in groups]
    counts = [3 + len(g[3]) + len(g[4]) for g in groups]
    use_slot = own_slot is not None

    def first_tiles(w, parts):
        firsts, first = [], 0
        for p in parts:
            firsts.append(first)
            first += p.shape[1] // (w.shape[0] // steps)
        return firsts

    def update(t, slot_ref, firsts, w_ref, m_ref, v_ref, p_refs, o_refs, g_ref, d_ref, nm_ref, nv_ref):
        g = None
        for k, p_ref in enumerate(p_refs):
            gk = None
            for q in range(p_ref.shape[0]):
                part = p_ref[q].astype(F32)
                if o_refs:
                    part = jnp.where(slot_ref[0] == q, o_refs[k][0].astype(F32), part)
                gk = part if gk is None else gk + part
            g = gk if g is None else jnp.where(t >= firsts[k], gk, g)
        m_new = ADAM_B1 * m_ref[...] + (1.0 - ADAM_B1) * g
        v_new = ADAM_B2 * v_ref[...] + (1.0 - ADAM_B2) * (g * g)
        g_ref[...] = g
        nm_ref[...] = m_new
        nv_ref[...] = v_new
        d_ref[...] = -ADAM_LR * ((m_new * c1) / (jnp.sqrt(v_new * c2) + ADAM_EPS) + ADAM_WD * w_ref[...])

    def body(*refs):
        refs = list(refs)
        slot_ref = refs.pop(0) if use_slot else None
        t = pl.program_id(0)
        ins, outs = refs[: sum(counts)], refs[sum(counts) :]
        for i, (w, _, _, parts, own) in enumerate(groups):
            mine = ins[sum(counts[:i]) : sum(counts[: i + 1])]
            update(t, slot_ref, first_tiles(w, parts), *mine[:3], mine[3 : 3 + len(parts)], mine[3 + len(parts) :],
                   *outs[4 * i : 4 * i + 4])

    operands, in_specs, out_specs, out_shape = [], [], [], []
    for w, m, v, parts, own in groups:
        r, cdim = w.shape
        tr = r // steps
        firsts = first_tiles(w, parts)

        def tile_of(k, t, firsts=firsts, parts=parts, tr=tr):
            return jnp.clip(t - firsts[k], 0, parts[k].shape[1] // tr - 1)

        blk = pl.BlockSpec((tr, cdim), lambda t, *_: (t, 0))
        operands += [w, m, v] + parts + own
        in_specs += [blk, blk, blk]
        in_specs += [pl.BlockSpec((p.shape[0], tr, cdim), lambda t, *_, k=k, tile_of=tile_of: (0, tile_of(k, t), 0)) for k, p in enumerate(parts)]
        in_specs += [pl.BlockSpec((1, tr, cdim), lambda t, slot_ref, k=k, tile_of=tile_of: (slot_ref[0], tile_of(k, t), 0)) for k in range(len(own))]
        out_specs += [blk] * 4
        out_shape += [jax.ShapeDtypeStruct((r, cdim), F32)] * 4
    outs, plan_outs = _call(
        body,
        name=name,
        prefetch=[own_slot] if use_slot else [],
        operands=operands,
        grid=(steps,),
        in_specs=in_specs,
        out_specs=out_specs,
        out_shape=out_shape,
        plans=plans,
    )
    return [outs[4 * i : 4 * i + 4] for i in range(len(groups))], plan_outs


def _chip_exchange_start(parts):
    n = len(parts)
    hbm = pl.BlockSpec(memory_space=pltpu.HBM)
    sem = pl.BlockSpec(memory_space=pltpu.SEMAPHORE)

    def body(*refs):
        srcs, lands = refs[:n], refs[n : 2 * n]
        send_sems, recv_sems = refs[2 * n], refs[2 * n + 1]
        token = refs[-1]
        x, y, c = _place()
        my_chip = 2 * x + y
        for a in range(n):
            for j, chip in enumerate(_other_chips(x, y)):
                pltpu.make_async_remote_copy(
                    src_ref=srcs[a].at[2 * chip[0] + chip[1]], dst_ref=lands[a].at[my_chip],
                    send_sem=send_sems.at[a * 3 + j], recv_sem=recv_sems.at[a * 3 + j],
                    device_id=(*chip, c), device_id_type=MESH).start()
        token[...] = jnp.zeros_like(token)

    in_hbm = [pltpu.with_memory_space_constraint(p, pltpu.HBM) for p in parts]
    in_hbm += [pltpu.with_memory_space_constraint(lax.empty(p.shape, p.dtype), pltpu.HBM) for p in parts]
    res = pl.pallas_call(
        body,
        name="grad_chip_exchange_start",
        out_shape=[pltpu.SemaphoreType.DMA((n * 3,)), pltpu.SemaphoreType.DMA((n * 3,))]
        + [pltpu.HBM(p.shape, p.dtype) for p in parts] * 2 + [jax.ShapeDtypeStruct((8, CHUNK), F32)],
        in_specs=[hbm] * (2 * n),
        out_specs=[sem, sem] + [hbm] * (2 * n) + [pl.BlockSpec(memory_space=pltpu.VMEM)],
        input_output_aliases={i: 2 + i for i in range(2 * n)},
        compiler_params=pltpu.CompilerParams(has_side_effects=pltpu.SideEffectType.DATAFLOW_SIDE_EFFECTING),
    )(*in_hbm)
    return res[:-1], res[-1]


def _chip_exchange_wait(in_flight, after):
    n = (len(in_flight) - 2) // 2
    hbm = pl.BlockSpec(memory_space=pltpu.HBM)
    sem = pl.BlockSpec(memory_space=pltpu.SEMAPHORE)
    send_sems, recv_sems, *bufs = in_flight

    def body(*refs):
        srcs, lands = refs[:n], refs[n : 2 * n]
        send_ref, recv_ref = refs[2 * n], refs[2 * n + 1]
        x, y, c = _place()
        for a in range(n):
            for j, chip in enumerate(_other_chips(x, y)):
                their_chip = 2 * chip[0] + chip[1]
                cp = pltpu.make_async_remote_copy(
                    src_ref=srcs[a].at[their_chip], dst_ref=lands[a].at[their_chip],
                    send_sem=send_ref.at[a * 3 + j], recv_sem=recv_ref.at[a * 3 + j],
                    device_id=(*chip, c), device_id_type=MESH)
                cp.wait_send()
                cp.wait_recv()

    res = pl.pallas_call(
        body,
        name="grad_chip_exchange_wait",
        out_shape=[pltpu.HBM(b.shape, b.dtype) for b in bufs],
        in_specs=[hbm] * (2 * n) + [sem, sem] + [pl.BlockSpec(memory_space=pl.ANY)] * len(after),
        out_specs=[hbm] * (2 * n),
        input_output_aliases={i: i for i in range(2 * n)},
        compiler_params=pltpu.CompilerParams(has_side_effects=pltpu.SideEffectType.DATAFLOW_SIDE_EFFECTING),
    )(*bufs, send_sems, recv_sems, *after)
    return res[:n], res[n:]


def kernel(x, mem, norm_mix_g, w_in, conv_w, gm_ln_g, gm_ln_b, gm_ws, gm_bs, w_out, norm_x_g, norm_mem_g, w_q, w_kv, w_xo, norm_final_g, loss_target, m_norm_mix_g, m_w_in, m_conv_w, m_gm_ln_g, m_gm_ln_b, m_gm_ws, m_gm_bs, m_w_out, m_norm_x_g, m_norm_mem_g, m_w_q, m_w_kv, m_w_xo, m_norm_final_g, v_norm_mix_g, v_w_in, v_conv_w, v_gm_ln_g, v_gm_ln_b, v_gm_ws, v_gm_bs, v_w_out, v_norm_x_g, v_norm_mem_g, v_w_q, v_w_kv, v_w_xo, v_norm_final_g):
    s = x.shape[1]
    dev = 4 * lax.axis_index("x") + 2 * lax.axis_index("y") + lax.axis_index("c")
    core = lax.axis_index("c").astype(jnp.int32).reshape(1)
    x2 = x[0]
    target = loss_target[0]
    pad_taps = lambda t: jnp.pad(t[0], ((0, 5), (0, 0)))

    big_names = ["w_in", "w_out", "w_q", "w_kv", "w_xo"]
    big_w = dict(w_in=w_in[0], w_out=w_out[0], w_q=w_q[0], w_kv=w_kv[0], w_xo=w_xo[0])
    big_m = dict(w_in=m_w_in[0], w_out=m_w_out[0], w_q=m_w_q[0], w_kv=m_w_kv[0], w_xo=m_w_xo[0])
    big_v = dict(w_in=v_w_in[0], w_out=v_w_out[0], w_q=v_w_q[0], w_kv=v_w_kv[0], w_xo=v_w_xo[0])
    shard_axis = dict(w_in=1, w_out=0, w_q=0, w_kv=1, w_xo=0)
    full_shape = dict(w_in=(D_MODEL, IN_DIM), w_out=(MIX_DIM, D_MODEL), w_q=(D_MODEL, D_MODEL),
                      w_kv=(D_MODEL, 2 * D_MODEL), w_xo=(D_MODEL, D_MODEL))
    shard_size = {k: big_w[k].shape[shard_axis[k]] for k in big_names}
    others = big_names[1:]
    gather_spec = lambda k: (full_shape[k], shard_axis[k], shard_size[k])
    reduce_spec = lambda k: (shard_axis[k], shard_size[k])
    ws = gm_ws[0]
    bs_b = jnp.broadcast_to(gm_bs[0][:, :, None], (8, CHUNK, CHUNK))
    g_f = norm_final_g.reshape(1, D_MODEL)

    px, py = lax.axis_index("x"), lax.axis_index("y")
    chip_order = jnp.stack([2 * px + py, 2 * (1 - px) + py, 2 * px + 1 - py, 2 * (1 - px) + 1 - py]).astype(jnp.int32)
    (proj, h1, h1t, w_in_full), ((w_out_full, conv_g),) = _inproj(
        x2, norm_mix_g, big_w["w_in"].astype(BF16), chip_order,
        plans=[_gather_plan([big_w["w_out"].astype(BF16), pad_taps(conv_w)[None]], [gather_spec("w_out"), ((N_DEV, 8, CHUNK), 0, 1)])])
    conv_full = conv_g.transpose(1, 0, 2).reshape(8, D_MODEL)
    attn_w = ["w_q", "w_kv", "w_xo"]
    cat, (gathered,) = _mixer_fwd(
        proj, conv_full, gm_ln_g, gm_ln_b, ws, bs_b,
        plans=[_gather_plan([big_w[k].astype(BF16) for k in attn_w], [gather_spec(k) for k in attn_w])])
    wf = dict(zip(attn_w, gathered), w_out=w_out_full)
    k, vv, mt = _kv_fwd(mem[0], norm_mem_g, wf["w_kv"])
    (dcat, catt, h2t, ot, dx2b, dqb, dx1b, dk, dvv, d_gf, d_gx, sq_err) = _attn(
        x2, cat, target, wf["w_out"], wf["w_q"], wf["w_xo"], k, vv, norm_x_g, g_f)

    def pair_sums(names, grads, got):
        return [_pair_sum("pair_sum_" + k, grads[k], got[i], *reduce_spec(k), core) for i, k in enumerate(names)]

    sibling_plan = lambda names, grads: _sibling_plan([grads[k] for k in names], [reduce_spec(k) for k in names])
    early, late = ["w_out", "w_kv", "w_q"], ["w_xo"]
    big_g = dict(w_out=_matmul_f32("dw_out", catt, dx1b)[0])
    big_g["w_kv"], d_gmem = _kv_bwd(mem[0], wf["w_kv"], mt, dk, dvv)
    big_g["w_q"], (got_a,) = _matmul_f32("dw_q", h2t, dqb, plans=[sibling_plan(early[:2], big_g)])
    big_g["w_xo"], (got_b,) = _matmul_f32("dw_xo", ot, dx2b, plans=[sibling_plan(early[2:], big_g)])
    got_early = got_a + got_b
    (dproj, d_cw, d_lng, d_lnb, d_ws, d_bs), (parts_early, got_late) = _mixer_bwd(
        proj, dcat, conv_full, gm_ln_g, gm_ln_b, ws, bs_b,
        plans=[_chip_plan(pair_sums(early, big_g, got_early)), sibling_plan(late, big_g)])
    half = D_MODEL // 2
    in_spec = reduce_spec("w_in")
    dw_top, _ = _matmul_f32("dw_in_top", h1t, dproj, rows=(0, half))
    pair_xo = pair_sums(late, big_g, got_late)[0]
    dw_bot, (got_top, (ws_all,)) = _matmul_f32(
        "dw_in_bot", h1t, dproj, rows=(half, half),
        plans=[_sibling_plan([dw_top], [in_spec]),
               _gather_plan([d_ws.reshape(1, 8 * CHUNK, CHUNK).astype(BF16)], [((N_DEV, 8 * CHUNK, CHUNK), 0, 1)])])
    pair_top, ((got_bot,),) = _pair_sum("pair_sum_w_in_top", dw_top, got_top[0], *in_spec, core,
                                        plans=[_sibling_plan([dw_bot], [in_spec])])
    pair_bot = _pair_sum("pair_sum_w_in_bot", dw_bot, got_bot, *in_spec, core)
    in_flight, token = _chip_exchange_start([pair_top, pair_bot, pair_xo])
    n_tiles = s // min(TM_DH1, s)
    (grad_x, d_gmix), _ = _dh1("dh1", dproj, w_in_full, x2, dx1b, norm_mix_g + token[0, 0], (0, n_tiles))
    flat_ws = lambda t: t.reshape(8 * CHUNK, CHUNK)
    updated, _ = _adamw(
        "adamw_early",
        [(big_w[k], big_m[k], big_v[k], got_k) for k, got_k in zip(early, parts_early)]
        + [(flat_ws(ws), flat_ws(m_gm_ws), flat_ws(v_gm_ws), ws_all)], steps=2)
    big_out = dict(zip(early, updated[:3]))
    ws_out = updated[3]
    (pair_top, pair_bot, pair_xo), (land_top, land_bot, land_xo) = _chip_exchange_wait(
        in_flight, [d_gmix] + [u[0] for u in updated])
    my_chip = (2 * px + py).astype(jnp.int32).reshape(1)

    vec_names = ["norm_mix_g", "gm_ln_g", "gm_ln_b", "gm_bs", "norm_x_g", "norm_mem_g", "norm_final_g"]
    vec_g = [d_gmix, d_lng, d_lnb, d_bs, d_gx, d_gmem, d_gf]
    small = jnp.concatenate(vec_g + [sq_err, d_cw], axis=0)
    (big_out["w_in"],), ((small_all,),) = _adamw(
        "adamw_w_in", [(big_w["w_in"], big_m["w_in"], big_v["w_in"], [land_top, land_bot], [pair_top, pair_bot])],
        steps=4, own_slot=my_chip, plans=[_gather_plan([small[None]], [((N_DEV, 16, D_MODEL), 0, 1)])])
    loss = 0.5 * jnp.sum(small_all[:, 7, :]) / D_MODEL
    row = lambda t: t.reshape(1, D_MODEL)
    vec_w = [norm_mix_g, gm_ln_g, gm_ln_b, row(gm_bs), norm_x_g, norm_mem_g, row(norm_final_g)]
    vec_m = [m_norm_mix_g, m_gm_ln_g, m_gm_ln_b, row(m_gm_bs), m_norm_x_g, m_norm_mem_g, row(m_norm_final_g)]
    vec_v = [v_norm_mix_g, v_gm_ln_g, v_gm_ln_b, row(v_gm_bs), v_norm_x_g, v_norm_mem_g, row(v_norm_final_g)]
    stack8 = lambda ts: jnp.concatenate(ts + [jnp.ones((1, D_MODEL), F32)], axis=0)
    conv_parts = lax.dynamic_slice(small_all, (0, 8, dev * CHUNK), (N_DEV, 8, CHUNK))
    (big_out["w_xo"], vec_out, conv_out), _ = _adamw(
        "adamw_late",
        [(big_w["w_xo"], big_m["w_xo"], big_v["w_xo"], land_xo, pair_xo),
         (stack8(vec_w), stack8(vec_m), stack8(vec_v), small_all[:, :8, :]),
         (pad_taps(conv_w), pad_taps(m_conv_w), jnp.pad(v_conv_w[0], ((0, 5), (0, 0)), constant_values=1.0), conv_parts)],
        steps=1, own_slot=my_chip)

    def result(name, which):
        if name in big_out:
            return big_out[name][which][None]
        if name == "conv_w":
            return conv_out[which][:3][None]
        if name == "gm_ws":
            return ws_out[which].reshape(1, 8, CHUNK, CHUNK)
        r = vec_out[which][vec_names.index(name)]
        if name == "gm_bs":
            return r.reshape(1, 8, CHUNK)
        if name == "norm_final_g":
            return r
        return r.reshape(1, D_MODEL)

    weights = ["norm_mix_g", "w_in", "conv_w", "gm_ln_g", "gm_ln_b", "gm_ws", "gm_bs", "w_out", "norm_x_g", "norm_mem_g", "w_q", "w_kv", "w_xo", "norm_final_g"]
    outs = [loss, grad_x[None]]
    for which in range(4):
        outs += [result(name, which) for name in weights]
    return tuple(outs)
```

```python
import math

import jax
import jax.numpy as jnp
from jax import lax
from jax.experimental import pallas as pl
from jax.experimental.pallas import tpu as pltpu

F32 = jnp.float32
BF16 = jnp.bfloat16
MESH = pl.DeviceIdType.MESH

D_MODEL = 1024
N_DEV = 8
N_CHIP = 4
CHUNK = 128
GROUPS = 7
IN_DIM = GROUPS * D_MODEL
SHARD_IN = IN_DIM // N_DEV
MIX_DIM = 2 * D_MODEL
X_HEADS = 4
X_HEAD_DIM = D_MODEL // X_HEADS
MEM_LEN = 256
EPS = 1e-6
GELU_K0 = math.sqrt(2.0 / math.pi)
GELU_K1 = 0.044715

ADAM_LR = 0.001
ADAM_B1 = 0.9
ADAM_B2 = 0.999
ADAM_EPS = 1e-08
ADAM_WD = 0.01
ADAM_STEP = 10

TM_INPROJ = 512
TM_ATTN = 512
SUB_ATTN = 256
TM_DH1 = 512
TM_DW = 512
TK_DW = 4096
TR_PAIR_SUM = 1024


def _dot(a, b):
    return lax.dot_general(a, b, (((1,), (0,)), ((), ())), preferred_element_type=F32)


def _dot_nt(a, b):
    return lax.dot_general(a, b, (((1,), (1,)), ((), ())), preferred_element_type=F32)


def _rows8(v):
    return jnp.sum(v.reshape(v.shape[0] // 8, 8, v.shape[1]), axis=0)


def _gelu(x):
    x2 = x * x
    t = jnp.tanh(GELU_K0 * (x + GELU_K1 * (x * x2)))
    g = 0.5 * x * (1.0 + t)
    dg = 0.5 * (1.0 + t) + 0.5 * x * (1.0 - t * t) * (GELU_K0 * (1.0 + 3.0 * GELU_K1 * x2))
    return g, dg


def _silu(z):
    s = jax.nn.sigmoid(z)
    return z * s, s * (1.0 + z * (1.0 - s))


def _rms(v):
    r = lax.rsqrt(jnp.mean(v * v, axis=-1, keepdims=True) + EPS)
    return v * r, r


def _rms_bwd(dy_g, vh, r):
    return r * (dy_g - vh * jnp.mean(dy_g * vh, axis=-1, keepdims=True))


def _place():
    return lax.axis_index("x"), lax.axis_index("y"), lax.axis_index("c")


def _other_chips(x, y):
    return [(1 - x, y), (x, 1 - y), (1 - x, 1 - y)]


def _pow2_divisor(n):
    return n & (-n)


def _shard_view(ref, axis, size, d):
    start = d * size
    align = _pow2_divisor(size)
    if align > 1:
        start = pl.multiple_of(start, align)
    idx = [slice(None)] * len(ref.shape)
    idx[axis] = pl.ds(start, size)
    return ref.at[tuple(idx)]


class _Plan:
    def __init__(self, ins, out_shape, sems, start, finish, mids=(), aliases=None):
        self.ins, self.out_shape, self.sems = list(ins), list(out_shape), list(sems)
        self.start, self.mids, self.finish = start, list(mids), finish
        self.aliases = dict(aliases or {})


def _gather_plan(shards, specs):
    n = len(shards)
    n_copy = 7

    def copies(ins, outs, sems):
        send_sems, recv_sems, local_sems = sems
        x, y, c = _place()
        me, sibling = (x, y, c), (x, y, 1 - c)
        chips = _other_chips(x, y)

        def block(a, p):
            _, axis, size = specs[a]
            return _shard_view(outs[a], axis, size, 4 * p[0] + 2 * p[1] + p[2])

        def copy(a, k, p, to, src=None):
            return pltpu.make_async_remote_copy(
                src_ref=block(a, p) if src is None else src,
                dst_ref=block(a, p),
                send_sem=send_sems.at[a * n_copy + k],
                recv_sem=recv_sems.at[a * n_copy + k],
                device_id=to,
                device_id_type=MESH,
            )

        def mine():
            return [pltpu.make_async_copy(ins[a], block(a, me), local_sems.at[a]) for a in range(n)]

        def first():
            return [cp for a in range(n) for cp in
                    [copy(a, 0, me, sibling, src=ins[a])] + [copy(a, 1 + j, me, (*chip, c), src=ins[a]) for j, chip in enumerate(chips)]]

        def landed(js):
            return [copy(a, 1 + j, (*chips[j], c), me) for j in js for a in range(n)]

        def passed(js=(0, 1, 2)):
            return [copy(a, 4 + j, (*chips[j], c), sibling) for j in js for a in range(n)]

        def last():
            return [cp for a in range(n) for cp in
                    [copy(a, 0, sibling, me)] + [copy(a, 4 + j, (*chip, 1 - c), me) for j, chip in enumerate(chips)]]

        return mine, first, landed, passed, last

    def start(ins, outs, sems):
        mine, first, _, _, _ = copies(ins, outs, sems)
        for cp in mine() + first():
            cp.start()

    def pass_on(js):
        def run(ins, outs, sems):
            _, _, landed, passed, _ = copies(ins, outs, sems)
            for got, fwd in zip(landed(js), passed(js)):
                got.wait_recv()
                fwd.start()

        return run

    def finish(ins, outs, sems):
        mine, first, _, passed, last = copies(ins, outs, sems)
        for cp in last():
            cp.wait_recv()
        for cp in first() + passed():
            cp.wait_send()
        for cp in mine():
            cp.wait()

    return _Plan(
        shards,
        [jax.ShapeDtypeStruct(full, s.dtype) for s, (full, _, _) in zip(shards, specs)],
        [pltpu.SemaphoreType.DMA((n * n_copy,)), pltpu.SemaphoreType.DMA((n * n_copy,)), pltpu.SemaphoreType.DMA((n,))],
        start, finish, [pass_on((0, 1)), pass_on((2,))],
    )


def _sibling_plan(grads, specs):
    n = len(grads)

    def shard_shape(a):
        axis, size = specs[a]
        shp = list(grads[a].shape)
        shp[axis] = size
        return tuple(shp)

    def copies(ins, outs, sems):
        send_sems, recv_sems = sems
        x, y, c = _place()
        return [
            pltpu.make_async_remote_copy(
                src_ref=_shard_view(ins[a], specs[a][0], specs[a][1], 2 * q + (1 - c)),
                dst_ref=outs[a].at[q],
                send_sem=send_sems.at[a * N_CHIP + q],
                recv_sem=recv_sems.at[a * N_CHIP + q],
                device_id=(x, y, 1 - c),
                device_id_type=MESH,
            )
            for a in range(n) for q in range(N_CHIP)
        ]

    def start(ins, outs, sems):
        for cp in copies(ins, outs, sems):
            cp.start()

    def finish(ins, outs, sems):
        for cp in copies(ins, outs, sems):
            cp.wait()

    return _Plan(
        grads,
        [jax.ShapeDtypeStruct((N_CHIP, *shard_shape(a)), grads[a].dtype) for a in range(n)],
        [pltpu.SemaphoreType.DMA((n * N_CHIP,)), pltpu.SemaphoreType.DMA((n * N_CHIP,))],
        start, finish,
    )


def _chip_plan(parts, full_rows=None, row0=None, into=None):
    n = len(parts)
    full_rows = [p.shape[1] for p in parts] if full_rows is None else full_rows
    row0 = [0] * n if row0 is None else row0

    def copies(ins, outs, sems):
        send_sems, recv_sems, local_sems = sems
        x, y, c = _place()
        my_chip = 2 * x + y
        chips = _other_chips(x, y)

        def land(a, q):
            return outs[a].at[q, pl.ds(row0[a], parts[a].shape[1]), :]

        def mine():
            return [pltpu.make_async_copy(ins[a].at[my_chip], land(a, my_chip), local_sems.at[a]) for a in range(n)]

        def remote(sending):
            return [
                pltpu.make_async_remote_copy(
                    src_ref=ins[a].at[2 * chip[0] + chip[1]],
                    dst_ref=land(a, my_chip if sending else 2 * chip[0] + chip[1]),
                    send_sem=send_sems.at[a * 3 + j], recv_sem=recv_sems.at[a * 3 + j],
                    device_id=(*chip, c), device_id_type=MESH)
                for a in range(n) for j, chip in enumerate(chips)
            ]

        return mine, remote

    def start(ins, outs, sems):
        mine, remote = copies(ins, outs, sems)
        for cp in mine() + remote(True):
            cp.start()

    def finish(ins, outs, sems):
        mine, remote = copies(ins, outs, sems)
        for cp in remote(False):
            cp.wait_recv()
        for cp in remote(True):
            cp.wait_send()
        for cp in mine():
            cp.wait()

    return _Plan(
        list(parts) + (list(into) if into is not None else []),
        [jax.ShapeDtypeStruct((N_CHIP, full_rows[a], parts[a].shape[2]), parts[a].dtype) for a in range(n)],
        [pltpu.SemaphoreType.DMA((n * 3,)), pltpu.SemaphoreType.DMA((n * 3,)), pltpu.SemaphoreType.DMA((n,))],
        start, finish,
        aliases={n + a: a for a in range(n)} if into is not None else None,
    )


def _call(body, *, name, operands, out_shape, grid=(), in_specs=None, out_specs=None, scratch_shapes=(),
          plans=(), aliases=None, prefetch=(), start_after_body=False, pass_on_at=(0.5, 0.8)):
    operands, out_shape, scratch_shapes = list(operands), list(out_shape), list(scratch_shapes)
    n_pf, n_in, n_out, n_scr = len(prefetch), len(operands), len(out_shape), len(scratch_shapes)
    in_vmem = pl.BlockSpec(memory_space=pltpu.VMEM)
    in_hbm = pl.BlockSpec(memory_space=pl.ANY)
    in_specs = [in_vmem] * n_in if in_specs is None else list(in_specs)
    out_specs = [in_vmem] * n_out if out_specs is None else list(out_specs)
    io_alias = {n_pf + i: o for i, o in (aliases or {}).items()}
    pos_in, pos_out = n_pf + n_in, n_out
    for p in plans:
        for i, o in p.aliases.items():
            io_alias[pos_in + i] = pos_out + o
        pos_in += len(p.ins)
        pos_out += len(p.out_shape)
    steps = math.prod(grid) if grid else 1
    mid_steps = [min(steps - 1, int(steps * f)) for f in pass_on_at]

    def wrapped(*refs):
        refs = list(refs)
        take = lambda k: [refs.pop(0) for _ in range(k)]
        pf = take(n_pf)
        ins, p_ins = take(n_in), [take(len(p.ins)) for p in plans]
        outs, p_outs = take(n_out), [take(len(p.out_shape)) for p in plans]
        scr, p_sems = take(n_scr), [take(len(p.sems)) for p in plans]
        step = 0
        for ax, g in enumerate(grid):
            step = step * g + pl.program_id(ax)

        def hook(kind, at):
            if isinstance(kind, int):
                fns = [p.mids[kind] if kind < len(p.mids) else None for p in plans]
            else:
                fns = [getattr(p, kind) for p in plans]
            todo = [(fn, a, b, c) for fn, a, b, c in zip(fns, p_ins, p_outs, p_sems) if fn]

            def run():
                for fn, a, b, c in todo:
                    fn(a, b, c)

            if todo and grid:
                pl.when(step == at)(run)
            elif todo:
                run()

        if not start_after_body:
            hook("start", 0)
        if body is not None:
            body(*pf, *ins, *outs, *scr)
        if start_after_body:
            hook("start", 0)
        for stage, at in enumerate(mid_steps):
            hook(stage, at)
        hook("finish", steps - 1)

    layout = dict(
        grid=grid,
        in_specs=in_specs + [in_hbm] * sum(len(p.ins) for p in plans),
        out_specs=out_specs + [in_hbm] * sum(len(p.out_shape) for p in plans),
        scratch_shapes=scratch_shapes + [s for p in plans for s in p.sems],
    )
    if n_pf:
        layout = dict(grid_spec=pltpu.PrefetchScalarGridSpec(num_scalar_prefetch=n_pf, **layout))
    res = pl.pallas_call(
        wrapped,
        name=name,
        out_shape=out_shape + [o for p in plans for o in p.out_shape],
        input_output_aliases=io_alias,
        compiler_params=pltpu.CompilerParams(dimension_semantics=("arbitrary",) * len(grid)) if grid else None,
        **layout,
    )(*prefetch, *operands, *[a for p in plans for a in p.ins])
    res = list(res)
    outs, plan_outs = res[:n_out], []
    pos = n_out
    for p in plans:
        plan_outs.append(res[pos : pos + len(p.out_shape)])
        pos += len(p.out_shape)
    return outs, plan_outs


def _inproj(x, g, w_shard, order, plans=()):
    s = x.shape[0]
    tm = min(TM_INPROJ, s)
    nt = s // tm
    wide = 2 * SHARD_IN
    fwd_at = nt // 2

    def body(order_ref, x_ref, g_ref, w_hbm, proj_ref, h1_ref, h1t_ref, wfull_hbm,
             w_scr, h1_scr, send_sems, recv_sems, local_sems):
        k, i = pl.program_id(0), pl.program_id(1)
        px, py, c = _place()
        me, sibling = (px, py, c), (px, py, 1 - c)
        chips = _other_chips(px, py)

        def half(slot, core):
            return w_scr.at[slot, :, pl.ds(pl.multiple_of(core * SHARD_IN, 128), SHARD_IN)]

        def copy(n, slot, core, to, src=None):
            return pltpu.make_async_remote_copy(
                src_ref=half(slot, core) if src is None else src, dst_ref=half(slot, core),
                send_sem=send_sems.at[n], recv_sem=recv_sems.at[n], device_id=to, device_id_type=MESH)

        def mine():
            return pltpu.make_async_copy(w_hbm, half(0, c), local_sems.at[0])

        def first():
            return [copy(0, 0, c, sibling, src=w_hbm)] + [copy(1 + j, 1 + j, c, (*chip, c), src=w_hbm) for j, chip in enumerate(chips)]

        def save(slot):
            cols = pl.ds(pl.multiple_of(order_ref[slot] * wide, 128), wide)
            return pltpu.make_async_copy(w_scr.at[slot], wfull_hbm.at[:, cols], local_sems.at[1 + slot])

        @pl.when((k == 0) & (i == 0))
        def _():
            mine().start()
            for cp in first():
                cp.start()
            mine().wait()
            copy(0, 0, 1 - c, me).wait_recv()

        for j, chip in enumerate(chips):
            @pl.when((k == j) & (i == fwd_at))
            def _():
                copy(1 + j, 1 + j, c, me).wait_recv()
                copy(4 + j, 1 + j, c, sibling).start()

            @pl.when((k == j + 1) & (i == 0))
            def _():
                copy(4 + j, 1 + j, 1 - c, me).wait_recv()

        @pl.when((k == 3) & (i == 0))
        def _():
            for slot in range(N_CHIP):
                save(slot).start()

        @pl.when(k == 0)
        def _():
            xh, _ = _rms(x_ref[...])
            h = xh * g_ref[...]
            h1_ref[...] = h.astype(BF16)
            h1t_ref[...] = h.T.astype(BF16)
            h1_scr[pl.ds(pl.multiple_of(i * tm, tm), tm), :] = h.astype(BF16)

        proj_ref[...] = _dot(h1_scr[pl.ds(pl.multiple_of(i * tm, tm), tm), :], w_scr[k]).astype(BF16)

        @pl.when((k == 3) & (i == nt - 1))
        def _():
            for cp in first() + [copy(4 + j, 1 + j, c, sibling) for j in range(3)]:
                cp.wait_send()
            for slot in range(N_CHIP):
                save(slot).wait()

    once = lambda k, i, order_ref: (jnp.where(k == 0, i, nt - 1), 0)
    return _call(
        body,
        name="inproj",
        prefetch=[order],
        operands=[x, g, w_shard],
        grid=(N_CHIP, nt),
        in_specs=[
            pl.BlockSpec((tm, D_MODEL), once),
            pl.BlockSpec((1, D_MODEL), lambda k, i, order_ref: (0, 0)),
            pl.BlockSpec(memory_space=pl.ANY),
        ],
        out_specs=[
            pl.BlockSpec((tm, wide), lambda k, i, order_ref: (i, order_ref[k])),
            pl.BlockSpec((tm, D_MODEL), once),
            pl.BlockSpec((D_MODEL, tm), lambda k, i, order_ref: (0, jnp.where(k == 0, i, nt - 1))),
            pl.BlockSpec(memory_space=pl.ANY),
        ],
        out_shape=[
            jax.ShapeDtypeStruct((s, IN_DIM), BF16),
            jax.ShapeDtypeStruct((s, D_MODEL), BF16),
            jax.ShapeDtypeStruct((D_MODEL, s), BF16),
            jax.ShapeDtypeStruct((D_MODEL, IN_DIM), BF16),
        ],
        scratch_shapes=[
            pltpu.VMEM((N_CHIP, D_MODEL, wide), BF16),
            pltpu.VMEM((s, D_MODEL), BF16),
            pltpu.SemaphoreType.DMA((7,)),
            pltpu.SemaphoreType.DMA((7,)),
            pltpu.SemaphoreType.DMA((1 + N_CHIP,)),
        ],
        plans=plans,
        start_after_body=True,
        pass_on_at=(0.5, 0.85),
    )


def _causal_mask():
    row = lax.broadcasted_iota(jnp.int32, (CHUNK, CHUNK), 0)
    col = lax.broadcasted_iota(jnp.int32, (CHUNK, CHUNK), 1)
    return row, row >= col


def _mixer_fwd(proj, conv_w, ln_g, ln_b, ws, bs_b, plans=()):
    s = proj.shape[0]
    nt = s // CHUNK

    def body(proj_ref, cw_ref, lng_ref, lnb_ref, ws_ref, bsb_ref, cat_ref, pcar_ref):
        @pl.when(pl.program_id(0) == 0)
        def _():
            pcar_ref[...] = jnp.zeros_like(pcar_ref)

        row, tril = _causal_mask()
        for j in range(8):
            cs = slice(CHUNK * j, CHUNK * (j + 1))

            def grp(k):
                return proj_ref[:, k * D_MODEL + CHUNK * j : k * D_MODEL + CHUNK * (j + 1)].astype(F32)

            w = cw_ref[:, cs]
            p = grp(1) * grp(2)
            pc = pcar_ref[:, cs]
            p1 = jnp.where(row == 0, pc[7:8], pltpu.roll(p, 1, 0))
            p2 = jnp.where(row == 0, pc[6:7], jnp.where(row == 1, pc[7:8], pltpu.roll(p, 2, 0)))
            pcar_ref[:, cs] = p[CHUNK - 8 :]
            cv = w[0:1] * p2 + w[1:2] * p1 + w[2:3] * p
            sa, _ = _silu(grp(3))
            cat_ref[:, cs] = ((grp(0) * cv) * sa).astype(BF16)
            gu, _ = _gelu(grp(4))
            gv, _ = _gelu(grp(5))
            dv = gv - jnp.mean(gv, axis=-1, keepdims=True)
            vn = dv * lax.rsqrt(jnp.mean(dv * dv, axis=-1, keepdims=True) + EPS)
            vn = vn * lng_ref[:, cs] + lnb_ref[:, cs]
            wc = jnp.where(tril, ws_ref[j], 0.0).astype(BF16)
            sp = _dot(wc, vn.astype(BF16)) + bsb_ref[j]
            sb, _ = _silu(grp(6))
            cat_ref[:, D_MODEL + CHUNK * j : D_MODEL + CHUNK * (j + 1)] = ((gu * sp) * sb).astype(BF16)

    whole = lambda shape: pl.BlockSpec(shape, lambda i: (0,) * len(shape))
    (cat,), plan_outs = _call(
        body,
        name="mixer_fwd",
        operands=[proj, conv_w, ln_g, ln_b, ws, bs_b],
        grid=(nt,),
        in_specs=[
            pl.BlockSpec((CHUNK, IN_DIM), lambda i: (i, 0)),
            whole((8, D_MODEL)),
            whole((1, D_MODEL)),
            whole((1, D_MODEL)),
            whole((8, CHUNK, CHUNK)),
            whole((8, CHUNK, CHUNK)),
        ],
        out_specs=[pl.BlockSpec((CHUNK, MIX_DIM), lambda i: (i, 0))],
        out_shape=[jax.ShapeDtypeStruct((s, MIX_DIM), BF16)],
        scratch_shapes=[pltpu.VMEM((8, D_MODEL), F32)],
        plans=plans,
        pass_on_at=(0.55, 0.93),
    )
    return cat, plan_outs


def _mixer_bwd(proj, dcat, conv_w, ln_g, ln_b, ws, bs_b, plans=()):
    s = proj.shape[0]
    nt = s // CHUNK

    def body(proj_ref, halo_ref, dcat_ref, cw_ref, lng_ref, lnb_ref, ws_ref, bsb_ref,
             dproj_ref, dcw_ref, dlng_ref, dlnb_ref, dws_ref, dbs_ref,
             car_ref, acc_cw, acc_lng, acc_lnb, acc_bs):
        i = pl.program_id(0)
        tile = nt - 1 - i

        @pl.when(i == 0)
        def _():
            car_ref[...] = jnp.zeros_like(car_ref)
            acc_cw[...] = jnp.zeros_like(acc_cw)
            acc_lng[...] = jnp.zeros_like(acc_lng)
            acc_lnb[...] = jnp.zeros_like(acc_lnb)
            acc_bs[...] = jnp.zeros_like(acc_bs)
            dws_ref[...] = jnp.zeros_like(dws_ref)

        row, tril = _causal_mask()
        has_prev = jnp.where(tile > 0, 1.0, 0.0).astype(F32)
        for j in range(8):
            cs = slice(CHUNK * j, CHUNK * (j + 1))

            def col(k):
                return slice(k * D_MODEL + CHUNK * j, k * D_MODEL + CHUNK * (j + 1))

            def grp(k):
                return proj_ref[:, col(k)].astype(F32)

            w = cw_ref[:, cs]
            gb, gc, xa = grp(0), grp(1), grp(2)
            p = gc * xa
            pprev = halo_ref[:, col(1)].astype(F32) * halo_ref[:, col(2)].astype(F32) * has_prev
            p1 = jnp.where(row == 0, pprev[15:16], pltpu.roll(p, 1, 0))
            p2 = jnp.where(row == 0, pprev[14:15], jnp.where(row == 1, pprev[15:16], pltpu.roll(p, 2, 0)))
            cv = w[0:1] * p2 + w[1:2] * p1 + w[2:3] * p
            za = grp(3)
            sa, dsa = _silu(za)
            da = dcat_ref[:, cs].astype(F32)
            dproj_ref[:, col(0)] = (da * cv * sa).astype(BF16)
            dproj_ref[:, col(3)] = (da * gb * cv * dsa).astype(BF16)
            dcv = da * gb * sa
            nxt = car_ref[:, cs]
            d1 = jnp.where(row == CHUNK - 1, nxt[0:1], pltpu.roll(dcv, CHUNK - 1, 0))
            d2 = jnp.where(row == CHUNK - 2, nxt[0:1], jnp.where(row == CHUNK - 1, nxt[1:2], pltpu.roll(dcv, CHUNK - 2, 0)))
            car_ref[:, cs] = dcv[0:8]
            dp = w[2:3] * dcv + w[1:2] * d1 + w[0:1] * d2
            dproj_ref[:, col(1)] = (dp * xa).astype(BF16)
            dproj_ref[:, col(2)] = (dp * gc).astype(BF16)
            acc_cw[0, :, cs] += _rows8(dcv * p2)
            acc_cw[1, :, cs] += _rows8(dcv * p1)
            acc_cw[2, :, cs] += _rows8(dcv * p)
            gu, dgu = _gelu(grp(4))
            gv, dgv = _gelu(grp(5))
            dv = gv - jnp.mean(gv, axis=-1, keepdims=True)
            rstd = lax.rsqrt(jnp.mean(dv * dv, axis=-1, keepdims=True) + EPS)
            vnh = dv * rstd
            lng = lng_ref[:, cs]
            vnb = (vnh * lng + lnb_ref[:, cs]).astype(BF16)
            wcf = jnp.where(tril, ws_ref[j], 0.0)
            sp = _dot(wcf.astype(BF16), vnb) + bsb_ref[j]
            zb = grp(6)
            sb, dsb = _silu(zb)
            db = dcat_ref[:, D_MODEL + CHUNK * j : D_MODEL + CHUNK * (j + 1)].astype(F32)
            dproj_ref[:, col(4)] = (db * sp * sb * dgu).astype(BF16)
            dproj_ref[:, col(6)] = (db * gu * sp * dsb).astype(BF16)
            dsp = db * gu * sb
            acc_bs[j] += dsp
            dspb = dsp.astype(BF16)
            dws_ref[j] += _dot_nt(dspb, vnb)
            dvn = _dot(wcf.T.astype(BF16), dspb)
            acc_lnb[:, cs] += _rows8(dvn)
            acc_lng[:, cs] += _rows8(dvn * vnh)
            dvh = dvn * lng
            dgvv = rstd * (dvh - jnp.mean(dvh, axis=-1, keepdims=True) - vnh * jnp.mean(dvh * vnh, axis=-1, keepdims=True))
            dproj_ref[:, col(5)] = (dgvv * dgv).astype(BF16)

        @pl.when(i == nt - 1)
        def _():
            dlng_ref[...] = jnp.sum(acc_lng[...], axis=0, keepdims=True)
            dlnb_ref[...] = jnp.sum(acc_lnb[...], axis=0, keepdims=True)
            dcw_ref[...] = jnp.zeros_like(dcw_ref)
            for k in range(3):
                dcw_ref[k : k + 1, :] = jnp.sum(acc_cw[k], axis=0, keepdims=True)
            for j in range(8):
                dws_ref[j] = jnp.where(tril, dws_ref[j], 0.0)
                dbs_ref[:, CHUNK * j : CHUNK * (j + 1)] = jnp.sum(acc_bs[j].T, axis=0, keepdims=True)

    whole = lambda shape: pl.BlockSpec(shape, lambda i: (0,) * len(shape))
    halo_rows = 16
    per = CHUNK // halo_rows
    return _call(
        body,
        name="mixer_bwd",
        operands=[proj, proj, dcat, conv_w, ln_g, ln_b, ws, bs_b],
        plans=plans,
        grid=(nt,),
        in_specs=[
            pl.BlockSpec((CHUNK, IN_DIM), lambda i: (nt - 1 - i, 0)),
            pl.BlockSpec((halo_rows, IN_DIM), lambda i: (jnp.maximum((nt - 1 - i) * per - 1, 0), 0)),
            pl.BlockSpec((CHUNK, MIX_DIM), lambda i: (nt - 1 - i, 0)),
            whole((8, D_MODEL)),
            whole((1, D_MODEL)),
            whole((1, D_MODEL)),
            whole((8, CHUNK, CHUNK)),
            whole((8, CHUNK, CHUNK)),
        ],
        out_specs=[
            pl.BlockSpec((CHUNK, IN_DIM), lambda i: (nt - 1 - i, 0)),
            whole((8, D_MODEL)),
            whole((1, D_MODEL)),
            whole((1, D_MODEL)),
            whole((8, CHUNK, CHUNK)),
            whole((1, D_MODEL)),
        ],
        out_shape=[
            jax.ShapeDtypeStruct((s, IN_DIM), BF16),
            jax.ShapeDtypeStruct((8, D_MODEL), F32),
            jax.ShapeDtypeStruct((1, D_MODEL), F32),
            jax.ShapeDtypeStruct((1, D_MODEL), F32),
            jax.ShapeDtypeStruct((8, CHUNK, CHUNK), F32),
            jax.ShapeDtypeStruct((1, D_MODEL), F32),
        ],
        scratch_shapes=[
            pltpu.VMEM((8, D_MODEL), F32),
            pltpu.VMEM((3, 8, D_MODEL), F32),
            pltpu.VMEM((8, D_MODEL), F32),
            pltpu.VMEM((8, D_MODEL), F32),
            pltpu.VMEM((8, CHUNK, CHUNK), F32),
        ],
    )


def _kv_fwd(mem, g_mem, w_kv):
    def body(mem_ref, g_ref, w_ref, k_ref, vv_ref, mt_ref):
        mh, _ = _rms(mem_ref[...])
        m = mh * g_ref[...]
        kv = _dot(m.astype(BF16), w_ref[...])
        k_ref[...] = kv[:, :D_MODEL].astype(BF16)
        vv_ref[...] = kv[:, D_MODEL:].astype(BF16)
        mt_ref[...] = m.T.astype(BF16)

    return pl.pallas_call(
        body,
        name="kv_fwd",
        out_shape=[
            jax.ShapeDtypeStruct((MEM_LEN, D_MODEL), BF16),
            jax.ShapeDtypeStruct((MEM_LEN, D_MODEL), BF16),
            jax.ShapeDtypeStruct((D_MODEL, MEM_LEN), BF16),
        ],
    )(mem, g_mem, w_kv)


def _kv_bwd(mem, w_kv, mt, dk, dvv):
    def body(mem_ref, w_ref, mt_ref, dk_ref, dvv_ref, dw_ref, dg_ref):
        dkv = jnp.concatenate([dk_ref[...], dvv_ref[...]], axis=1).astype(BF16)
        dw_ref[...] = _dot(mt_ref[...], dkv)
        dm = _dot_nt(dkv, w_ref[...])
        mh, _ = _rms(mem_ref[...])
        dg_ref[...] = jnp.sum(dm * mh, axis=0, keepdims=True)

    return pl.pallas_call(
        body,
        name="kv_bwd",
        out_shape=[
            jax.ShapeDtypeStruct((D_MODEL, 2 * D_MODEL), F32),
            jax.ShapeDtypeStruct((1, D_MODEL), F32),
        ],
    )(mem, w_kv, mt, dk, dvv)


def _attn(x, cat, target, w_out, w_q, w_xo, k, vv, g_x, g_f):
    s = x.shape[0]
    tm = min(TM_ATTN, s)
    sub = min(SUB_ATTN, tm)
    nt = s // tm
    scale = 1.0 / math.sqrt(X_HEAD_DIM)

    def body(x_ref, cat_ref, t_ref, wout_ref, wq_ref, wxo_ref, k_ref, vv_ref, gx_ref, gf_ref,
             dcat_ref, catt_ref, h2t_ref, ot_ref, dx2b_ref, dqb_ref, dx1b_ref,
             dk_ref, dvv_ref, dgf_ref, dgx_ref, loss_ref,
             p_scr, q_scr, o_scr, dq_scr, acc_gf, acc_gx, acc_loss):
        i = pl.program_id(0)

        @pl.when(i == 0)
        def _():
            dk_ref[...] = jnp.zeros_like(dk_ref)
            dvv_ref[...] = jnp.zeros_like(dvv_ref)
            acc_gf[...] = jnp.zeros_like(acc_gf)
            acc_gx[...] = jnp.zeros_like(acc_gx)
            acc_loss[...] = jnp.zeros_like(acc_loss)

        subs = [slice(u * sub, (u + 1) * sub) for u in range(tm // sub)]
        heads = [slice(X_HEAD_DIM * h, X_HEAD_DIM * (h + 1)) for h in range(X_HEADS)]
        gx, gf = gx_ref[...], gf_ref[...]
        x1, x1h, r2 = [], [], []
        for rs in subs:
            cat = cat_ref[rs, :]
            catt_ref[:, rs] = cat.astype(F32).T.astype(BF16)
            v = x_ref[rs, :] + _dot(cat, wout_ref[...])
            vh, r = _rms(v)
            h2 = vh * gx
            h2t_ref[:, rs] = h2.T.astype(BF16)
            q_scr[rs, :] = _dot(h2.astype(BF16), wq_ref[...]).astype(BF16)
            x1.append(v), x1h.append(vh), r2.append(r)
        for rs in subs:
            for hs in heads:
                sc = _dot_nt(q_scr[rs, hs], k_ref[:, hs]) * scale
                e = jnp.exp(sc - jnp.max(sc, axis=-1, keepdims=True))
                p = e / jnp.sum(e, axis=-1, keepdims=True)
                p_scr[rs, hs] = p
                o_scr[rs, hs] = _dot(p.astype(BF16), vv_ref[:, hs])
        dx2, sq, d_gf = [], None, None
        for u, rs in enumerate(subs):
            o = o_scr[rs, :]
            ot_ref[:, rs] = o.T.astype(BF16)
            x2 = x1[u] + _dot(o.astype(BF16), wxo_ref[...])
            x2h, r3 = _rms(x2)
            err = x2h * gf - t_ref[rs, :]
            dy = err * (1.0 / D_MODEL)
            sq = _rows8(err * err) if sq is None else sq + _rows8(err * err)
            d_gf = _rows8(dy * x2h) if d_gf is None else d_gf + _rows8(dy * x2h)
            dx2.append(_rms_bwd(dy * gf, x2h, r3))
        acc_loss[...] += sq
        acc_gf[...] += d_gf
        d_vv, d_k = [None] * X_HEADS, [None] * X_HEADS
        for u, rs in enumerate(subs):
            dx2b = dx2[u].astype(BF16)
            dx2b_ref[rs, :] = dx2b
            dob = _dot_nt(dx2b, wxo_ref[...]).astype(BF16)
            for h, hs in enumerate(heads):
                p = p_scr[rs, hs]
                dp = _dot_nt(dob[:, hs], vv_ref[:, hs])
                ds = p * (dp - jnp.sum(dp * p, axis=-1, keepdims=True)) * scale
                dq_scr[rs, hs] = _dot(ds.astype(BF16), k_ref[:, hs]).astype(BF16)
                dvv_h = _dot(p.T.astype(BF16), dob[:, hs])
                dk_h = _dot(ds.T.astype(BF16), q_scr[rs, hs])
                d_vv[h] = dvv_h if d_vv[h] is None else d_vv[h] + dvv_h
                d_k[h] = dk_h if d_k[h] is None else d_k[h] + dk_h
        for h, hs in enumerate(heads):
            dvv_ref[:, hs] += d_vv[h]
            dk_ref[:, hs] += d_k[h]
        d_gx = None
        for u, rs in enumerate(subs):
            dqb = dq_scr[rs, :]
            dqb_ref[rs, :] = dqb
            dh2 = _dot_nt(dqb, wq_ref[...])
            d_gx = _rows8(dh2 * x1h[u]) if d_gx is None else d_gx + _rows8(dh2 * x1h[u])
            dx1 = dx2[u] + _rms_bwd(dh2 * gx, x1h[u], r2[u])
            dx1b = dx1.astype(BF16)
            dx1b_ref[rs, :] = dx1b
            dcat_ref[rs, :] = _dot_nt(dx1b, wout_ref[...]).astype(BF16)
        acc_gx[...] += d_gx

        @pl.when(i == nt - 1)
        def _():
            dgf_ref[...] = jnp.sum(acc_gf[...], axis=0, keepdims=True)
            dgx_ref[...] = jnp.sum(acc_gx[...], axis=0, keepdims=True)
            loss_ref[...] = jnp.sum(acc_loss[...], axis=0, keepdims=True)

    whole = lambda shape: pl.BlockSpec(shape, lambda i: (0,) * len(shape))
    once = lambda shape: pl.BlockSpec(shape, lambda i: (0,) * len(shape), pipeline_mode=pl.Buffered(1))
    rows = lambda width: pl.BlockSpec((tm, width), lambda i: (i, 0))
    cols = lambda height: pl.BlockSpec((height, tm), lambda i: (0, i))
    vec = jax.ShapeDtypeStruct((1, D_MODEL), F32)
    return pl.pallas_call(
        body,
        name="attn",
        grid=(nt,),
        in_specs=[
            rows(D_MODEL), rows(MIX_DIM), rows(D_MODEL),
            once((MIX_DIM, D_MODEL)), once((D_MODEL, D_MODEL)), once((D_MODEL, D_MODEL)),
            once((MEM_LEN, D_MODEL)), once((MEM_LEN, D_MODEL)),
            whole((1, D_MODEL)), whole((1, D_MODEL)),
        ],
        out_specs=[
            rows(MIX_DIM), cols(MIX_DIM), cols(D_MODEL), cols(D_MODEL),
            rows(D_MODEL), rows(D_MODEL), rows(D_MODEL),
            whole((MEM_LEN, D_MODEL)), whole((MEM_LEN, D_MODEL)),
            whole((1, D_MODEL)), whole((1, D_MODEL)), whole((1, D_MODEL)),
        ],
        out_shape=[
            jax.ShapeDtypeStruct((s, MIX_DIM), BF16),
            jax.ShapeDtypeStruct((MIX_DIM, s), BF16),
            jax.ShapeDtypeStruct((D_MODEL, s), BF16),
            jax.ShapeDtypeStruct((D_MODEL, s), BF16),
            jax.ShapeDtypeStruct((s, D_MODEL), BF16),
            jax.ShapeDtypeStruct((s, D_MODEL), BF16),
            jax.ShapeDtypeStruct((s, D_MODEL), BF16),
            jax.ShapeDtypeStruct((MEM_LEN, D_MODEL), F32),
            jax.ShapeDtypeStruct((MEM_LEN, D_MODEL), F32),
            vec, vec, vec,
        ],
        scratch_shapes=[
            pltpu.VMEM((tm, D_MODEL), F32),
            pltpu.VMEM((tm, D_MODEL), BF16),
            pltpu.VMEM((tm, D_MODEL), F32),
            pltpu.VMEM((tm, D_MODEL), BF16),
            pltpu.VMEM((8, D_MODEL), F32),
            pltpu.VMEM((8, D_MODEL), F32),
            pltpu.VMEM((8, D_MODEL), F32),
        ],
        compiler_params=pltpu.CompilerParams(dimension_semantics=("arbitrary",)),
    )(x, cat, target, w_out, w_q, w_xo, k, vv, g_x, g_f)


def _matmul_f32(name, a, b, rows=None, plans=()):
    first, m = (0, a.shape[0]) if rows is None else rows
    kk = a.shape[1]
    n = b.shape[1]
    tm, tn, tk = min(m, TM_DW), min(n, D_MODEL), min(kk, TK_DW)
    nk = kk // tk
    i0 = first // tm

    def body(a_ref, b_ref, o_ref):
        prod = _dot(a_ref[...], b_ref[...])
        if nk == 1:
            o_ref[...] = prod
            return
        kid = pl.program_id(2)

        @pl.when(kid == 0)
        def _():
            o_ref[...] = prod

        @pl.when(kid > 0)
        def _():
            o_ref[...] += prod

    (out,), plan_outs = _call(
        body,
        name=name,
        operands=[a, b],
        grid=(m // tm, n // tn, nk),
        in_specs=[pl.BlockSpec((tm, tk), lambda i, j, k: (i + i0, k)), pl.BlockSpec((tk, tn), lambda i, j, k: (k, j))],
        out_specs=[pl.BlockSpec((tm, tn), lambda i, j, k: (i, j))],
        out_shape=[jax.ShapeDtypeStruct((m, n), F32)],
        plans=plans,
    )
    return out, plan_outs


def _dh1(name, dproj, w_in, x, dx1, g, tiles, into=None, plans=()):
    s = x.shape[0]
    tm = min(TM_DH1, s)
    t0, nt = tiles[0], tiles[1] - tiles[0]

    def body(dp_ref, w_ref, x_ref, dx1_ref, g_ref, *rest):
        gx_ref, dg_ref, acc_g = rest[-3:]
        i = pl.program_id(0)

        @pl.when(i == 0)
        def _():
            acc_g[...] = jnp.zeros_like(acc_g)

        dh1 = _dot_nt(dp_ref[...], w_ref[...])
        xh, r = _rms(x_ref[...])
        acc_g[...] += _rows8(dh1 * xh)
        gx_ref[...] = dx1_ref[...].astype(F32) + _rms_bwd(dh1 * g_ref[...], xh, r)

        @pl.when(i == nt - 1)
        def _():
            dg_ref[...] = jnp.sum(acc_g[...], axis=0, keepdims=True)

    rows = pl.BlockSpec((tm, D_MODEL), lambda i: (i + t0, 0))
    return _call(
        body,
        name=name,
        operands=[dproj, w_in, x, dx1, g] + ([into] if into is not None else []),
        grid=(nt,),
        in_specs=[
            pl.BlockSpec((tm, IN_DIM), lambda i: (i + t0, 0)),
            pl.BlockSpec((D_MODEL, IN_DIM), lambda i: (0, 0), pipeline_mode=pl.Buffered(1)),
            rows,
            rows,
            pl.BlockSpec((1, D_MODEL), lambda i: (0, 0)),
        ] + ([pl.BlockSpec(memory_space=pl.ANY)] if into is not None else []),
        out_specs=[rows, pl.BlockSpec((1, D_MODEL), lambda i: (0, 0))],
        out_shape=[jax.ShapeDtypeStruct((s, D_MODEL), F32), jax.ShapeDtypeStruct((1, D_MODEL), F32)],
        scratch_shapes=[pltpu.VMEM((8, D_MODEL), F32)],
        plans=plans,
        aliases={5: 0} if into is not None else None,
    )


def _pair_sum(name, grad, got, axis, size, core, plans=()):
    shard = list(grad.shape)
    shard[axis] = size
    r, cdim = shard
    tr = min(r, TR_PAIR_SUM)
    nr = r // tr

    def body(core_ref, g_ref, got_ref, o_ref):
        o_ref[0] = (g_ref[...] + got_ref[0]).astype(BF16)

    if axis == 1:
        g_map = lambda q, t, core_ref: (t, 2 * q + core_ref[0])
    else:
        g_map = lambda q, t, core_ref: ((2 * q + core_ref[0]) * nr + t, 0)
    (out,), plan_outs = _call(
        body,
        name=name,
        prefetch=[core],
        operands=[grad, got],
        grid=(N_CHIP, nr),
        in_specs=[
            pl.BlockSpec((tr, cdim), g_map),
            pl.BlockSpec((1, tr, cdim), lambda q, t, core_ref: (q, t, 0)),
        ],
        out_specs=[pl.BlockSpec((1, tr, cdim), lambda q, t, core_ref: (q, t, 0))],
        out_shape=[jax.ShapeDtypeStruct((N_CHIP, r, cdim), BF16)],
        plans=plans,
    )
    return (out, plan_outs) if plans else out


def _adamw(name, groups, steps, own_slot=None, plans=()):
    c1 = 1.0 / (1.0 - ADAM_B1**ADAM_STEP)
    c2 = 1.0 / (1.0 - ADAM_B2**ADAM_STEP)
    lists = lambda p: list(p) if isinstance(p, (list, tuple)) else [p]
    groups = [(g[0], g[1], g[2], lists(g[3]), lists(g[4]) if len(g) > 4 else []) for g in groups]
    counts = [3 + len(g[3]) + len(g[4]) for g in groups]
    use_slot = own_slot is not None

    def first_tiles(w, parts):
        firsts, first = [], 0
        for p in parts:
            firsts.append(first)
            first += p.shape[1] // (w.shape[0] // steps)
        return firsts

    def update(t, slot_ref, firsts, w_ref, m_ref, v_ref, p_refs, o_refs, g_ref, d_ref, nm_ref, nv_ref):
        g = None
        for k, p_ref in enumerate(p_refs):
            gk = None
            for q in range(p_ref.shape[0]):
                part = p_ref[q].astype(F32)
                if o_refs:
                    part = jnp.where(slot_ref[0] == q, o_refs[k][0].astype(F32), part)
                gk = part if gk is None else gk + part
            g = gk if g is None else jnp.where(t >= firsts[k], gk, g)
        m_new = ADAM_B1 * m_ref[...] + (1.0 - ADAM_B1) * g
        v_new = ADAM_B2 * v_ref[...] + (1.0 - ADAM_B2) * (g * g)
        g_ref[...] = g
        nm_ref[...] = m_new
        nv_ref[...] = v_new
        d_ref[...] = -ADAM_LR * ((m_new * c1) / (jnp.sqrt(v_new * c2) + ADAM_EPS) + ADAM_WD * w_ref[...])

    def body(*refs):
        refs = list(refs)
        slot_ref = refs.pop(0) if use_slot else None
        t = pl.program_id(0)
        ins, outs = refs[: sum(counts)], refs[sum(counts) :]
        for i, (w, _, _, parts, own) in enumerate(groups):
            mine = ins[sum(counts[:i]) : sum(counts[: i + 1])]
            update(t, slot_ref, first_tiles(w, parts), *mine[:3], mine[3 : 3 + len(parts)], mine[3 + len(parts) :],
                   *outs[4 * i : 4 * i + 4])

    operands, in_specs, out_specs, out_shape = [], [], [], []
    for w, m, v, parts, own in groups:
        r, cdim = w.shape
        tr = r // steps
        firsts = first_tiles(w, parts)

        def tile_of(k, t, firsts=firsts, parts=parts, tr=tr):
            return jnp.clip(t - firsts[k], 0, parts[k].shape[1] // tr - 1)

        blk = pl.BlockSpec((tr, cdim), lambda t, *_: (t, 0))
        operands += [w, m, v] + parts + own
        in_specs += [blk, blk, blk]
        in_specs += [pl.BlockSpec((p.shape[0], tr, cdim), lambda t, *_, k=k, tile_of=tile_of: (0, tile_of(k, t), 0)) for k, p in enumerate(parts)]
        in_specs += [pl.BlockSpec((1, tr, cdim), lambda t, slot_ref, k=k, tile_of=tile_of: (slot_ref[0], tile_of(k, t), 0)) for k in range(len(own))]
        out_specs += [blk] * 4
        out_shape += [jax.ShapeDtypeStruct((r, cdim), F32)] * 4
    outs, plan_outs = _call(
        body,
        name=name,
        prefetch=[own_slot] if use_slot else [],
        operands=operands,
        grid=(steps,),
        in_specs=in_specs,
        out_specs=out_specs,
        out_shape=out_shape,
        plans=plans,
    )
    return [outs[4 * i : 4 * i + 4] for i in range(len(groups))], plan_outs


def _chip_exchange_start(parts):
    n = len(parts)
    hbm = pl.BlockSpec(memory_space=pltpu.HBM)
    sem = pl.BlockSpec(memory_space=pltpu.SEMAPHORE)

    def body(*refs):
        srcs, lands = refs[:n], refs[n : 2 * n]
        send_sems, recv_sems = refs[2 * n], refs[2 * n + 1]
        token = refs[-1]
        x, y, c = _place()
        my_chip = 2 * x + y
        for a in range(n):
            for j, chip in enumerate(_other_chips(x, y)):
                pltpu.make_async_remote_copy(
                    src_ref=srcs[a].at[2 * chip[0] + chip[1]], dst_ref=lands[a].at[my_chip],
                    send_sem=send_sems.at[a * 3 + j], recv_sem=recv_sems.at[a * 3 + j],
                    device_id=(*chip, c), device_id_type=MESH).start()
        token[...] = jnp.zeros_like(token)

    in_hbm = [pltpu.with_memory_space_constraint(p, pltpu.HBM) for p in parts]
    in_hbm += [pltpu.with_memory_space_constraint(lax.empty(p.shape, p.dtype), pltpu.HBM) for p in parts]
    res = pl.pallas_call(
        body,
        name="grad_chip_exchange_start",
        out_shape=[pltpu.SemaphoreType.DMA((n * 3,)), pltpu.SemaphoreType.DMA((n * 3,))]
        + [pltpu.HBM(p.shape, p.dtype) for p in parts] * 2 + [jax.ShapeDtypeStruct((8, CHUNK), F32)],
        in_specs=[hbm] * (2 * n),
        out_specs=[sem, sem] + [hbm] * (2 * n) + [pl.BlockSpec(memory_space=pltpu.VMEM)],
        input_output_aliases={i: 2 + i for i in range(2 * n)},
        compiler_params=pltpu.CompilerParams(has_side_effects=pltpu.SideEffectType.DATAFLOW_SIDE_EFFECTING),
    )(*in_hbm)
    return res[:-1], res[-1]


def _chip_exchange_wait(in_flight, after):
    n = (len(in_flight) - 2) // 2
    hbm = pl.BlockSpec(memory_space=pltpu.HBM)
    sem = pl.BlockSpec(memory_space=pltpu.SEMAPHORE)
    send_sems, recv_sems, *bufs = in_flight

    def body(*refs):
        srcs, lands = refs[:n], refs[n : 2 * n]
        send_ref, recv_ref = refs[2 * n], refs[2 * n + 1]
        x, y, c = _place()
        for a in range(n):
            for j, chip in enumerate(_other_chips(x, y)):
                their_chip = 2 * chip[0] + chip[1]
                cp = pltpu.make_async_remote_copy(
                    src_ref=srcs[a].at[their_chip], dst_ref=lands[a].at[their_chip],
                    send_sem=send_ref.at[a * 3 + j], recv_sem=recv_ref.at[a * 3 + j],
                    device_id=(*chip, c), device_id_type=MESH)
                cp.wait_send()
                cp.wait_recv()

    res = pl.pallas_call(
        body,
        name="grad_chip_exchange_wait",
        out_shape=[pltpu.HBM(b.shape, b.dtype) for b in bufs],
        in_specs=[hbm] * (2 * n) + [sem, sem] + [pl.BlockSpec(memory_space=pl.ANY)] * len(after),
        out_specs=[hbm] * (2 * n),
        input_output_aliases={i: i for i in range(2 * n)},
        compiler_params=pltpu.CompilerParams(has_side_effects=pltpu.SideEffectType.DATAFLOW_SIDE_EFFECTING),
    )(*bufs, send_sems, recv_sems, *after)
    return res[:n], res[n:]


def kernel(x, mem, norm_mix_g, w_in, conv_w, gm_ln_g, gm_ln_b, gm_ws, gm_bs, w_out, norm_x_g, norm_mem_g, w_q, w_kv, w_xo, norm_final_g, loss_target, m_norm_mix_g, m_w_in, m_conv_w, m_gm_ln_g, m_gm_ln_b, m_gm_ws, m_gm_bs, m_w_out, m_norm_x_g, m_norm_mem_g, m_w_q, m_w_kv, m_w_xo, m_norm_final_g, v_norm_mix_g, v_w_in, v_conv_w, v_gm_ln_g, v_gm_ln_b, v_gm_ws, v_gm_bs, v_w_out, v_norm_x_g, v_norm_mem_g, v_w_q, v_w_kv, v_w_xo, v_norm_final_g):
    s = x.shape[1]
    dev = 4 * lax.axis_index("x") + 2 * lax.axis_index("y") + lax.axis_index("c")
    core = lax.axis_index("c").astype(jnp.int32).reshape(1)
    x2 = x[0]
    target = loss_target[0]
    pad_taps = lambda t: jnp.pad(t[0], ((0, 5), (0, 0)))

    big_names = ["w_in", "w_out", "w_q", "w_kv", "w_xo"]
    big_w = dict(w_in=w_in[0], w_out=w_out[0], w_q=w_q[0], w_kv=w_kv[0], w_xo=w_xo[0])
    big_m = dict(w_in=m_w_in[0], w_out=m_w_out[0], w_q=m_w_q[0], w_kv=m_w_kv[0], w_xo=m_w_xo[0])
    big_v = dict(w_in=v_w_in[0], w_out=v_w_out[0], w_q=v_w_q[0], w_kv=v_w_kv[0], w_xo=v_w_xo[0])
    shard_axis = dict(w_in=1, w_out=0, w_q=0, w_kv=1, w_xo=0)
    full_shape = dict(w_in=(D_MODEL, IN_DIM), w_out=(MIX_DIM, D_MODEL), w_q=(D_MODEL, D_MODEL),
                      w_kv=(D_MODEL, 2 * D_MODEL), w_xo=(D_MODEL, D_MODEL))
    shard_size = {k: big_w[k].shape[shard_axis[k]] for k in big_names}
    others = big_names[1:]
    gather_spec = lambda k: (full_shape[k], shard_axis[k], shard_size[k])
    reduce_spec = lambda k: (shard_axis[k], shard_size[k])
    ws = gm_ws[0]
    bs_b = jnp.broadcast_to(gm_bs[0][:, :, None], (8, CHUNK, CHUNK))
    g_f = norm_final_g.reshape(1, D_MODEL)

    px, py = lax.axis_index("x"), lax.axis_index("y")
    chip_order = jnp.stack([2 * px + py, 2 * (1 - px) + py, 2 * px + 1 - py, 2 * (1 - px) + 1 - py]).astype(jnp.int32)
    (proj, h1, h1t, w_in_full), ((w_out_full, conv_g),) = _inproj(
        x2, norm_mix_g, big_w["w_in"].astype(BF16), chip_order,
        plans=[_gather_plan([big_w["w_out"].astype(BF16), pad_taps(conv_w)[None]], [gather_spec("w_out"), ((N_DEV, 8, CHUNK), 0, 1)])])
    conv_full = conv_g.transpose(1, 0, 2).reshape(8, D_MODEL)
    attn_w = ["w_q", "w_kv", "w_xo"]
    cat, (gathered,) = _mixer_fwd(
        proj, conv_full, gm_ln_g, gm_ln_b, ws, bs_b,
        plans=[_gather_plan([big_w[k].astype(BF16) for k in attn_w], [gather_spec(k) for k in attn_w])])
    wf = dict(zip(attn_w, gathered), w_out=w_out_full)
    k, vv, mt = _kv_fwd(mem[0], norm_mem_g, wf["w_kv"])
    (dcat, catt, h2t, ot, dx2b, dqb, dx1b, dk, dvv, d_gf, d_gx, sq_err) = _attn(
        x2, cat, target, wf["w_out"], wf["w_q"], wf["w_xo"], k, vv, norm_x_g, g_f)

    def pair_sums(names, grads, got):
        return [_pair_sum("pair_sum_" + k, grads[k], got[i], *reduce_spec(k), core) for i, k in enumerate(names)]

    sibling_plan = lambda names, grads: _sibling_plan([grads[k] for k in names], [reduce_spec(k) for k in names])
    early, late = ["w_out", "w_kv", "w_q"], ["w_xo"]
    big_g = dict(w_out=_matmul_f32("dw_out", catt, dx1b)[0])
    big_g["w_kv"], d_gmem = _kv_bwd(mem[0], wf["w_kv"], mt, dk, dvv)
    big_g["w_q"], (got_a,) = _matmul_f32("dw_q", h2t, dqb, plans=[sibling_plan(early[:2], big_g)])
    big_g["w_xo"], (got_b,) = _matmul_f32("dw_xo", ot, dx2b, plans=[sibling_plan(early[2:], big_g)])
    got_early = got_a + got_b
    (dproj, d_cw, d_lng, d_lnb, d_ws, d_bs), (parts_early, got_late) = _mixer_bwd(
        proj, dcat, conv_full, gm_ln_g, gm_ln_b, ws, bs_b,
        plans=[_chip_plan(pair_sums(early, big_g, got_early)), sibling_plan(late, big_g)])
    half = D_MODEL // 2
    in_spec = reduce_spec("w_in")
    dw_top, _ = _matmul_f32("dw_in_top", h1t, dproj, rows=(0, half))
    pair_xo = pair_sums(late, big_g, got_late)[0]
    dw_bot, (got_top, (ws_all,)) = _matmul_f32(
        "dw_in_bot", h1t, dproj, rows=(half, half),
        plans=[_sibling_plan([dw_top], [in_spec]),
               _gather_plan([d_ws.reshape(1, 8 * CHUNK, CHUNK).astype(BF16)], [((N_DEV, 8 * CHUNK, CHUNK), 0, 1)])])
    pair_top, ((got_bot,),) = _pair_sum("pair_sum_w_in_top", dw_top, got_top[0], *in_spec, core,
                                        plans=[_sibling_plan([dw_bot], [in_spec])])
    pair_bot = _pair_sum("pair_sum_w_in_bot", dw_bot, got_bot, *in_spec, core)
    in_flight, token = _chip_exchange_start([pair_top, pair_bot, pair_xo])
    n_tiles = s // min(TM_DH1, s)
    (grad_x, d_gmix), _ = _dh1("dh1", dproj, w_in_full, x2, dx1b, norm_mix_g + token[0, 0], (0, n_tiles))
    flat_ws = lambda t: t.reshape(8 * CHUNK, CHUNK)
    updated, _ = _adamw(
        "adamw_early",
        [(big_w[k], big_m[k], big_v[k], got_k) for k, got_k in zip(early, parts_early)]
        + [(flat_ws(ws), flat_ws(m_gm_ws), flat_ws(v_gm_ws), ws_all)], steps=2)
    big_out = dict(zip(early, updated[:3]))
    ws_out = updated[3]
    (pair_top, pair_bot, pair_xo), (land_top, land_bot, land_xo) = _chip_exchange_wait(
        in_flight, [d_gmix] + [u[0] for u in updated])
    my_chip = (2 * px + py).astype(jnp.int32).reshape(1)

    vec_names = ["norm_mix_g", "gm_ln_g", "gm_ln_b", "gm_bs", "norm_x_g", "norm_mem_g", "norm_final_g"]
    vec_g = [d_gmix, d_lng, d_lnb, d_bs, d_gx, d_gmem, d_gf]
    small = jnp.concatenate(vec_g + [sq_err, d_cw], axis=0)
    (big_out["w_in"],), ((small_all,),) = _adamw(
        "adamw_w_in", [(big_w["w_in"], big_m["w_in"], big_v["w_in"], [land_top, land_bot], [pair_top, pair_bot])],
        steps=4, own_slot=my_chip, plans=[_gather_plan([small[None]], [((N_DEV, 16, D_MODEL), 0, 1)])])
    loss = 0.5 * jnp.sum(small_all[:, 7, :]) / D_MODEL
    row = lambda t: t.reshape(1, D_MODEL)
    vec_w = [norm_mix_g, gm_ln_g, gm_ln_b, row(gm_bs), norm_x_g, norm_mem_g, row(norm_final_g)]
    vec_m = [m_norm_mix_g, m_gm_ln_g, m_gm_ln_b, row(m_gm_bs), m_norm_x_g, m_norm_mem_g, row(m_norm_final_g)]
    vec_v = [v_norm_mix_g, v_gm_ln_g, v_gm_ln_b, row(v_gm_bs), v_norm_x_g, v_norm_mem_g, row(v_norm_final_g)]
    stack8 = lambda ts: jnp.concatenate(ts + [jnp.ones((1, D_MODEL), F32)], axis=0)
    conv_parts = lax.dynamic_slice(small_all, (0, 8, dev * CHUNK), (N_DEV, 8, CHUNK))
    (big_out["w_xo"], vec_out, conv_out), _ = _adamw(
        "adamw_late",
        [(big_w["w_xo"], big_m["w_xo"], big_v["w_xo"], land_xo, pair_xo),
         (stack8(vec_w), stack8(vec_m), stack8(vec_v), small_all[:, :8, :]),
         (pad_taps(conv_w), pad_taps(m_conv_w), jnp.pad(v_conv_w[0], ((0, 5), (0, 0)), constant_values=1.0), conv_parts)],
        steps=1, own_slot=my_chip)

    def result(name, which):
        if name in big_out:
            return big_out[name][which][None]
        if name == "conv_w":
            return conv_out[which][:3][None]
        if name == "gm_ws":
            return ws_out[which].reshape(1, 8, CHUNK, CHUNK)
        r = vec_out[which][vec_names.index(name)]
        if name == "gm_bs":
            return r.reshape(1, 8, CHUNK)
        if name == "norm_final_g":
            return r
        return r.reshape(1, D_MODEL)

    weights = ["norm_mix_g", "w_in", "conv_w", "gm_ln_g", "gm_ln_b", "gm_ws", "gm_bs", "w_out", "norm_x_g", "norm_mem_g", "w_q", "w_kv", "w_xo", "norm_final_g"]
    outs = [loss, grad_x[None]]
    for which in range(4):
        outs += [result(name, which) for name in weights]
    return tuple(outs)
```

```python
import math

import jax
import jax.numpy as jnp
from jax import lax
from jax.experimental import pallas as pl
from jax.experimental.pallas import tpu as pltpu

F32 = jnp.float32
BF16 = jnp.bfloat16
MESH = pl.DeviceIdType.MESH

D_MODEL = 1024
N_DEV = 8
N_CHIP = 4
CHUNK = 128
GROUPS = 7
IN_DIM = GROUPS * D_MODEL
SHARD_IN = IN_DIM // N_DEV
MIX_DIM = 2 * D_MODEL
X_HEADS = 4
X_HEAD_DIM = D_MODEL // X_HEADS
MEM_LEN = 256
EPS = 1e-6
GELU_K0 = math.sqrt(2.0 / math.pi)
GELU_K1 = 0.044715

ADAM_LR = 0.001
ADAM_B1 = 0.9
ADAM_B2 = 0.999
ADAM_EPS = 1e-08
ADAM_WD = 0.01
ADAM_STEP = 10

TM_INPROJ = 512
TM_ATTN = 512
SUB_ATTN = 256
TM_DH1 = 512
TM_DW = 512
TK_DW = 4096
TR_PAIR_SUM = 1024


def _dot(a, b):
    return lax.dot_general(a, b, (((1,), (0,)), ((), ())), preferred_element_type=F32)


def _dot_nt(a, b):
    return lax.dot_general(a, b, (((1,), (1,)), ((), ())), preferred_element_type=F32)


def _rows8(v):
    return jnp.sum(v.reshape(v.shape[0] // 8, 8, v.shape[1]), axis=0)


def _gelu(x):
    x2 = x * x
    t = jnp.tanh(GELU_K0 * (x + GELU_K1 * (x * x2)))
    g = 0.5 * x * (1.0 + t)
    dg = 0.5 * (1.0 + t) + 0.5 * x * (1.0 - t * t) * (GELU_K0 * (1.0 + 3.0 * GELU_K1 * x2))
    return g, dg


def _silu(z):
    s = jax.nn.sigmoid(z)
    return z * s, s * (1.0 + z * (1.0 - s))


def _rms(v):
    r = lax.rsqrt(jnp.mean(v * v, axis=-1, keepdims=True) + EPS)
    return v * r, r


def _rms_bwd(dy_g, vh, r):
    return r * (dy_g - vh * jnp.mean(dy_g * vh, axis=-1, keepdims=True))


def _place():
    return lax.axis_index("x"), lax.axis_index("y"), lax.axis_index("c")


def _other_chips(x, y):
    return [(1 - x, y), (x, 1 - y), (1 - x, 1 - y)]


def _pow2_divisor(n):
    return n & (-n)


def _shard_view(ref, axis, size, d):
    start = d * size
    align = _pow2_divisor(size)
    if align > 1:
        start = pl.multiple_of(start, align)
    idx = [slice(None)] * len(ref.shape)
    idx[axis] = pl.ds(start, size)
    return ref.at[tuple(idx)]


class _Plan:
    def __init__(self, ins, out_shape, sems, start, finish, mid=None, aliases=None):
        self.ins, self.out_shape, self.sems = list(ins), list(out_shape), list(sems)
        self.start, self.mid, self.finish = start, mid, finish
        self.aliases = dict(aliases or {})


def _gather_plan(shards, specs):
    n = len(shards)
    n_copy = 7

    def copies(ins, outs, sems):
        send_sems, recv_sems, local_sems = sems
        x, y, c = _place()
        me, sibling = (x, y, c), (x, y, 1 - c)
        chips = _other_chips(x, y)

        def block(a, p):
            _, axis, size = specs[a]
            return _shard_view(outs[a], axis, size, 4 * p[0] + 2 * p[1] + p[2])

        def copy(a, k, p, to, src=None):
            return pltpu.make_async_remote_copy(
                src_ref=block(a, p) if src is None else src,
                dst_ref=block(a, p),
                send_sem=send_sems.at[a * n_copy + k],
                recv_sem=recv_sems.at[a * n_copy + k],
                device_id=to,
                device_id_type=MESH,
            )

        def mine():
            return [pltpu.make_async_copy(ins[a], block(a, me), local_sems.at[a]) for a in range(n)]

        def first():
            return [cp for a in range(n) for cp in
                    [copy(a, 0, me, sibling, src=ins[a])] + [copy(a, 1 + j, me, (*chip, c), src=ins[a]) for j, chip in enumerate(chips)]]

        def landed():
            return [copy(a, 1 + j, (*chip, c), me) for j, chip in enumerate(chips) for a in range(n)]

        def passed():
            return [copy(a, 4 + j, (*chip, c), sibling) for j, chip in enumerate(chips) for a in range(n)]

        def last():
            return [cp for a in range(n) for cp in
                    [copy(a, 0, sibling, me)] + [copy(a, 4 + j, (*chip, 1 - c), me) for j, chip in enumerate(chips)]]

        return mine, first, landed, passed, last

    def start(ins, outs, sems):
        mine, first, _, _, _ = copies(ins, outs, sems)
        for cp in mine() + first():
            cp.start()

    def mid(ins, outs, sems):
        _, _, landed, passed, _ = copies(ins, outs, sems)
        for got, fwd in zip(landed(), passed()):
            got.wait_recv()
            fwd.start()

    def finish(ins, outs, sems):
        mine, first, _, passed, last = copies(ins, outs, sems)
        for cp in last():
            cp.wait_recv()
        for cp in first() + passed():
            cp.wait_send()
        for cp in mine():
            cp.wait()

    return _Plan(
        shards,
        [jax.ShapeDtypeStruct(full, s.dtype) for s, (full, _, _) in zip(shards, specs)],
        [pltpu.SemaphoreType.DMA((n * n_copy,)), pltpu.SemaphoreType.DMA((n * n_copy,)), pltpu.SemaphoreType.DMA((n,))],
        start, finish, mid,
    )


def _sibling_plan(grads, specs):
    n = len(grads)

    def shard_shape(a):
        axis, size = specs[a]
        shp = list(grads[a].shape)
        shp[axis] = size
        return tuple(shp)

    def copies(ins, outs, sems):
        send_sems, recv_sems = sems
        x, y, c = _place()
        return [
            pltpu.make_async_remote_copy(
                src_ref=_shard_view(ins[a], specs[a][0], specs[a][1], 2 * q + (1 - c)),
                dst_ref=outs[a].at[q],
                send_sem=send_sems.at[a * N_CHIP + q],
                recv_sem=recv_sems.at[a * N_CHIP + q],
                device_id=(x, y, 1 - c),
                device_id_type=MESH,
            )
            for a in range(n) for q in range(N_CHIP)
        ]

    def start(ins, outs, sems):
        for cp in copies(ins, outs, sems):
            cp.start()

    def finish(ins, outs, sems):
        for cp in copies(ins, outs, sems):
            cp.wait()

    return _Plan(
        grads,
        [jax.ShapeDtypeStruct((N_CHIP, *shard_shape(a)), grads[a].dtype) for a in range(n)],
        [pltpu.SemaphoreType.DMA((n * N_CHIP,)), pltpu.SemaphoreType.DMA((n * N_CHIP,))],
        start, finish,
    )


def _chip_plan(parts, full_rows=None, row0=None, into=None):
    n = len(parts)
    full_rows = [p.shape[1] for p in parts] if full_rows is None else full_rows
    row0 = [0] * n if row0 is None else row0

    def copies(ins, outs, sems):
        send_sems, recv_sems, local_sems = sems
        x, y, c = _place()
        my_chip = 2 * x + y
        chips = _other_chips(x, y)

        def land(a, q):
            return outs[a].at[q, pl.ds(row0[a], parts[a].shape[1]), :]

        def mine():
            return [pltpu.make_async_copy(ins[a].at[my_chip], land(a, my_chip), local_sems.at[a]) for a in range(n)]

        def remote(sending):
            return [
                pltpu.make_async_remote_copy(
                    src_ref=ins[a].at[2 * chip[0] + chip[1]],
                    dst_ref=land(a, my_chip if sending else 2 * chip[0] + chip[1]),
                    send_sem=send_sems.at[a * 3 + j], recv_sem=recv_sems.at[a * 3 + j],
                    device_id=(*chip, c), device_id_type=MESH)
                for a in range(n) for j, chip in enumerate(chips)
            ]

        return mine, remote

    def start(ins, outs, sems):
        mine, remote = copies(ins, outs, sems)
        for cp in mine() + remote(True):
            cp.start()

    def finish(ins, outs, sems):
        mine, remote = copies(ins, outs, sems)
        for cp in remote(False):
            cp.wait_recv()
        for cp in remote(True):
            cp.wait_send()
        for cp in mine():
            cp.wait()

    return _Plan(
        list(parts) + (list(into) if into is not None else []),
        [jax.ShapeDtypeStruct((N_CHIP, full_rows[a], parts[a].shape[2]), parts[a].dtype) for a in range(n)],
        [pltpu.SemaphoreType.DMA((n * 3,)), pltpu.SemaphoreType.DMA((n * 3,)), pltpu.SemaphoreType.DMA((n,))],
        start, finish,
        aliases={n + a: a for a in range(n)} if into is not None else None,
    )


def _call(body, *, name, operands, out_shape, grid=(), in_specs=None, out_specs=None, scratch_shapes=(),
          plans=(), aliases=None, prefetch=(), start_after_body=False, pass_on_at=0.6):
    operands, out_shape, scratch_shapes = list(operands), list(out_shape), list(scratch_shapes)
    n_pf, n_in, n_out, n_scr = len(prefetch), len(operands), len(out_shape), len(scratch_shapes)
    in_vmem = pl.BlockSpec(memory_space=pltpu.VMEM)
    in_hbm = pl.BlockSpec(memory_space=pl.ANY)
    in_specs = [in_vmem] * n_in if in_specs is None else list(in_specs)
    out_specs = [in_vmem] * n_out if out_specs is None else list(out_specs)
    io_alias = {n_pf + i: o for i, o in (aliases or {}).items()}
    pos_in, pos_out = n_pf + n_in, n_out
    for p in plans:
        for i, o in p.aliases.items():
            io_alias[pos_in + i] = pos_out + o
        pos_in += len(p.ins)
        pos_out += len(p.out_shape)
    steps = math.prod(grid) if grid else 1
    mid_step = min(steps - 1, int(steps * pass_on_at))

    def wrapped(*refs):
        refs = list(refs)
        take = lambda k: [refs.pop(0) for _ in range(k)]
        pf = take(n_pf)
        ins, p_ins = take(n_in), [take(len(p.ins)) for p in plans]
        outs, p_outs = take(n_out), [take(len(p.out_shape)) for p in plans]
        scr, p_sems = take(n_scr), [take(len(p.sems)) for p in plans]
        step = 0
        for ax, g in enumerate(grid):
            step = step * g + pl.program_id(ax)

        def hook(kind, at):
            todo = [(getattr(p, kind), a, b, c) for p, a, b, c in zip(plans, p_ins, p_outs, p_sems) if getattr(p, kind)]

            def run():
                for fn, a, b, c in todo:
                    fn(a, b, c)

            if todo and grid:
                pl.when(step == at)(run)
            elif todo:
                run()

        if not start_after_body:
            hook("start", 0)
        if body is not None:
            body(*pf, *ins, *outs, *scr)
        if start_after_body:
            hook("start", 0)
        hook("mid", mid_step)
        hook("finish", steps - 1)

    layout = dict(
        grid=grid,
        in_specs=in_specs + [in_hbm] * sum(len(p.ins) for p in plans),
        out_specs=out_specs + [in_hbm] * sum(len(p.out_shape) for p in plans),
        scratch_shapes=scratch_shapes + [s for p in plans for s in p.sems],
    )
    if n_pf:
        layout = dict(grid_spec=pltpu.PrefetchScalarGridSpec(num_scalar_prefetch=n_pf, **layout))
    res = pl.pallas_call(
        wrapped,
        name=name,
        out_shape=out_shape + [o for p in plans for o in p.out_shape],
        input_output_aliases=io_alias,
        compiler_params=pltpu.CompilerParams(dimension_semantics=("arbitrary",) * len(grid)) if grid else None,
        **layout,
    )(*prefetch, *operands, *[a for p in plans for a in p.ins])
    res = list(res)
    outs, plan_outs = res[:n_out], []
    pos = n_out
    for p in plans:
        plan_outs.append(res[pos : pos + len(p.out_shape)])
        pos += len(p.out_shape)
    return outs, plan_outs


def _inproj(x, g, w_shard, order, plans=()):
    s = x.shape[0]
    tm = min(TM_INPROJ, s)
    nt = s // tm
    wide = 2 * SHARD_IN
    fwd_at = nt // 2

    def body(order_ref, x_ref, g_ref, w_hbm, proj_ref, h1_ref, h1t_ref, wfull_hbm,
             w_scr, h1_scr, send_sems, recv_sems, local_sems):
        k, i = pl.program_id(0), pl.program_id(1)
        px, py, c = _place()
        me, sibling = (px, py, c), (px, py, 1 - c)
        chips = _other_chips(px, py)

        def half(slot, core):
            return w_scr.at[slot, :, pl.ds(pl.multiple_of(core * SHARD_IN, 128), SHARD_IN)]

        def copy(n, slot, core, to, src=None):
            return pltpu.make_async_remote_copy(
                src_ref=half(slot, core) if src is None else src, dst_ref=half(slot, core),
                send_sem=send_sems.at[n], recv_sem=recv_sems.at[n], device_id=to, device_id_type=MESH)

        def mine():
            return pltpu.make_async_copy(w_hbm, half(0, c), local_sems.at[0])

        def first():
            return [copy(0, 0, c, sibling, src=w_hbm)] + [copy(1 + j, 1 + j, c, (*chip, c), src=w_hbm) for j, chip in enumerate(chips)]

        def save(slot):
            cols = pl.ds(pl.multiple_of(order_ref[slot] * wide, 128), wide)
            return pltpu.make_async_copy(w_scr.at[slot], wfull_hbm.at[:, cols], local_sems.at[1 + slot])

        @pl.when((k == 0) & (i == 0))
        def _():
            mine().start()
            for cp in first():
                cp.start()
            mine().wait()
            copy(0, 0, 1 - c, me).wait_recv()

        for j, chip in enumerate(chips):
            @pl.when((k == j) & (i == fwd_at))
            def _():
                copy(1 + j, 1 + j, c, me).wait_recv()
                copy(4 + j, 1 + j, c, sibling).start()

            @pl.when((k == j + 1) & (i == 0))
            def _():
                copy(4 + j, 1 + j, 1 - c, me).wait_recv()

        @pl.when((k == 3) & (i == 0))
        def _():
            for slot in range(N_CHIP):
                save(slot).start()

        @pl.when(k == 0)
        def _():
            xh, _ = _rms(x_ref[...])
            h = xh * g_ref[...]
            h1_ref[...] = h.astype(BF16)
            h1t_ref[...] = h.T.astype(BF16)
            h1_scr[pl.ds(pl.multiple_of(i * tm, tm), tm), :] = h.astype(BF16)

        proj_ref[...] = _dot(h1_scr[pl.ds(pl.multiple_of(i * tm, tm), tm), :], w_scr[k]).astype(BF16)

        @pl.when((k == 3) & (i == nt - 1))
        def _():
            for cp in first() + [copy(4 + j, 1 + j, c, sibling) for j in range(3)]:
                cp.wait_send()
            for slot in range(N_CHIP):
                save(slot).wait()

    once = lambda k, i, order_ref: (jnp.where(k == 0, i, nt - 1), 0)
    return _call(
        body,
        name="inproj",
        prefetch=[order],
        operands=[x, g, w_shard],
        grid=(N_CHIP, nt),
        in_specs=[
            pl.BlockSpec((tm, D_MODEL), once),
            pl.BlockSpec((1, D_MODEL), lambda k, i, order_ref: (0, 0)),
            pl.BlockSpec(memory_space=pl.ANY),
        ],
        out_specs=[
            pl.BlockSpec((tm, wide), lambda k, i, order_ref: (i, order_ref[k])),
            pl.BlockSpec((tm, D_MODEL), once),
            pl.BlockSpec((D_MODEL, tm), lambda k, i, order_ref: (0, jnp.where(k == 0, i, nt - 1))),
            pl.BlockSpec(memory_space=pl.ANY),
        ],
        out_shape=[
            jax.ShapeDtypeStruct((s, IN_DIM), BF16),
            jax.ShapeDtypeStruct((s, D_MODEL), BF16),
            jax.ShapeDtypeStruct((D_MODEL, s), BF16),
            jax.ShapeDtypeStruct((D_MODEL, IN_DIM), BF16),
        ],
        scratch_shapes=[
            pltpu.VMEM((N_CHIP, D_MODEL, wide), BF16),
            pltpu.VMEM((s, D_MODEL), BF16),
            pltpu.SemaphoreType.DMA((7,)),
            pltpu.SemaphoreType.DMA((7,)),
            pltpu.SemaphoreType.DMA((1 + N_CHIP,)),
        ],
        plans=plans,
        start_after_body=True,
        pass_on_at=0.8,
    )


def _causal_mask():
    row = lax.broadcasted_iota(jnp.int32, (CHUNK, CHUNK), 0)
    col = lax.broadcasted_iota(jnp.int32, (CHUNK, CHUNK), 1)
    return row, row >= col


def _mixer_fwd(proj, conv_w, ln_g, ln_b, ws, bs_b, plans=()):
    s = proj.shape[0]
    nt = s // CHUNK

    def body(proj_ref, cw_ref, lng_ref, lnb_ref, ws_ref, bsb_ref, cat_ref, pcar_ref):
        @pl.when(pl.program_id(0) == 0)
        def _():
            pcar_ref[...] = jnp.zeros_like(pcar_ref)

        row, tril = _causal_mask()
        for j in range(8):
            cs = slice(CHUNK * j, CHUNK * (j + 1))

            def grp(k):
                return proj_ref[:, k * D_MODEL + CHUNK * j : k * D_MODEL + CHUNK * (j + 1)].astype(F32)

            w = cw_ref[:, cs]
            p = grp(1) * grp(2)
            pc = pcar_ref[:, cs]
            p1 = jnp.where(row == 0, pc[7:8], pltpu.roll(p, 1, 0))
            p2 = jnp.where(row == 0, pc[6:7], jnp.where(row == 1, pc[7:8], pltpu.roll(p, 2, 0)))
            pcar_ref[:, cs] = p[CHUNK - 8 :]
            cv = w[0:1] * p2 + w[1:2] * p1 + w[2:3] * p
            sa, _ = _silu(grp(3))
            cat_ref[:, cs] = ((grp(0) * cv) * sa).astype(BF16)
            gu, _ = _gelu(grp(4))
            gv, _ = _gelu(grp(5))
            dv = gv - jnp.mean(gv, axis=-1, keepdims=True)
            vn = dv * lax.rsqrt(jnp.mean(dv * dv, axis=-1, keepdims=True) + EPS)
            vn = vn * lng_ref[:, cs] + lnb_ref[:, cs]
            wc = jnp.where(tril, ws_ref[j], 0.0).astype(BF16)
            sp = _dot(wc, vn.astype(BF16)) + bsb_ref[j]
            sb, _ = _silu(grp(6))
            cat_ref[:, D_MODEL + CHUNK * j : D_MODEL + CHUNK * (j + 1)] = ((gu * sp) * sb).astype(BF16)

    whole = lambda shape: pl.BlockSpec(shape, lambda i: (0,) * len(shape))
    (cat,), plan_outs = _call(
        body,
        name="mixer_fwd",
        operands=[proj, conv_w, ln_g, ln_b, ws, bs_b],
        grid=(nt,),
        in_specs=[
            pl.BlockSpec((CHUNK, IN_DIM), lambda i: (i, 0)),
            whole((8, D_MODEL)),
            whole((1, D_MODEL)),
            whole((1, D_MODEL)),
            whole((8, CHUNK, CHUNK)),
            whole((8, CHUNK, CHUNK)),
        ],
        out_specs=[pl.BlockSpec((CHUNK, MIX_DIM), lambda i: (i, 0))],
        out_shape=[jax.ShapeDtypeStruct((s, MIX_DIM), BF16)],
        scratch_shapes=[pltpu.VMEM((8, D_MODEL), F32)],
        plans=plans,
        pass_on_at=0.85,
    )
    return cat, plan_outs


def _mixer_bwd(proj, dcat, conv_w, ln_g, ln_b, ws, bs_b, plans=()):
    s = proj.shape[0]
    nt = s // CHUNK

    def body(proj_ref, halo_ref, dcat_ref, cw_ref, lng_ref, lnb_ref, ws_ref, bsb_ref,
             dproj_ref, dcw_ref, dlng_ref, dlnb_ref, dws_ref, dbs_ref,
             car_ref, acc_cw, acc_lng, acc_lnb, acc_bs):
        i = pl.program_id(0)
        tile = nt - 1 - i

        @pl.when(i == 0)
        def _():
            car_ref[...] = jnp.zeros_like(car_ref)
            acc_cw[...] = jnp.zeros_like(acc_cw)
            acc_lng[...] = jnp.zeros_like(acc_lng)
            acc_lnb[...] = jnp.zeros_like(acc_lnb)
            acc_bs[...] = jnp.zeros_like(acc_bs)
            dws_ref[...] = jnp.zeros_like(dws_ref)

        row, tril = _causal_mask()
        has_prev = jnp.where(tile > 0, 1.0, 0.0).astype(F32)
        for j in range(8):
            cs = slice(CHUNK * j, CHUNK * (j + 1))

            def col(k):
                return slice(k * D_MODEL + CHUNK * j, k * D_MODEL + CHUNK * (j + 1))

            def grp(k):
                return proj_ref[:, col(k)].astype(F32)

            w = cw_ref[:, cs]
            gb, gc, xa = grp(0), grp(1), grp(2)
            p = gc * xa
            pprev = halo_ref[:, col(1)].astype(F32) * halo_ref[:, col(2)].astype(F32) * has_prev
            p1 = jnp.where(row == 0, pprev[15:16], pltpu.roll(p, 1, 0))
            p2 = jnp.where(row == 0, pprev[14:15], jnp.where(row == 1, pprev[15:16], pltpu.roll(p, 2, 0)))
            cv = w[0:1] * p2 + w[1:2] * p1 + w[2:3] * p
            za = grp(3)
            sa, dsa = _silu(za)
            da = dcat_ref[:, cs].astype(F32)
            dproj_ref[:, col(0)] = (da * cv * sa).astype(BF16)
            dproj_ref[:, col(3)] = (da * gb * cv * dsa).astype(BF16)
            dcv = da * gb * sa
            nxt = car_ref[:, cs]
            d1 = jnp.where(row == CHUNK - 1, nxt[0:1], pltpu.roll(dcv, CHUNK - 1, 0))
            d2 = jnp.where(row == CHUNK - 2, nxt[0:1], jnp.where(row == CHUNK - 1, nxt[1:2], pltpu.roll(dcv, CHUNK - 2, 0)))
            car_ref[:, cs] = dcv[0:8]
            dp = w[2:3] * dcv + w[1:2] * d1 + w[0:1] * d2
            dproj_ref[:, col(1)] = (dp * xa).astype(BF16)
            dproj_ref[:, col(2)] = (dp * gc).astype(BF16)
            acc_cw[0, :, cs] += _rows8(dcv * p2)
            acc_cw[1, :, cs] += _rows8(dcv * p1)
            acc_cw[2, :, cs] += _rows8(dcv * p)
            gu, dgu = _gelu(grp(4))
            gv, dgv = _gelu(grp(5))
            dv = gv - jnp.mean(gv, axis=-1, keepdims=True)
            rstd = lax.rsqrt(jnp.mean(dv * dv, axis=-1, keepdims=True) + EPS)
            vnh = dv * rstd
            lng = lng_ref[:, cs]
            vnb = (vnh * lng + lnb_ref[:, cs]).astype(BF16)
            wcf = jnp.where(tril, ws_ref[j], 0.0)
            sp = _dot(wcf.astype(BF16), vnb) + bsb_ref[j]
            zb = grp(6)
            sb, dsb = _silu(zb)
            db = dcat_ref[:, D_MODEL + CHUNK * j : D_MODEL + CHUNK * (j + 1)].astype(F32)
            dproj_ref[:, col(4)] = (db * sp * sb * dgu).astype(BF16)
            dproj_ref[:, col(6)] = (db * gu * sp * dsb).astype(BF16)
            dsp = db * gu * sb
            acc_bs[j] += dsp
            dspb = dsp.astype(BF16)
            dws_ref[j] += _dot_nt(dspb, vnb)
            dvn = _dot(wcf.T.astype(BF16), dspb)
            acc_lnb[:, cs] += _rows8(dvn)
            acc_lng[:, cs] += _rows8(dvn * vnh)
            dvh = dvn * lng
            dgvv = rstd * (dvh - jnp.mean(dvh, axis=-1, keepdims=True) - vnh * jnp.mean(dvh * vnh, axis=-1, keepdims=True))
            dproj_ref[:, col(5)] = (dgvv * dgv).astype(BF16)

        @pl.when(i == nt - 1)
        def _():
            dlng_ref[...] = jnp.sum(acc_lng[...], axis=0, keepdims=True)
            dlnb_ref[...] = jnp.sum(acc_lnb[...], axis=0, keepdims=True)
            dcw_ref[...] = jnp.zeros_like(dcw_ref)
            for k in range(3):
                dcw_ref[k : k + 1, :] = jnp.sum(acc_cw[k], axis=0, keepdims=True)
            for j in range(8):
                dws_ref[j] = jnp.where(tril, dws_ref[j], 0.0)
                dbs_ref[:, CHUNK * j : CHUNK * (j + 1)] = jnp.sum(acc_bs[j].T, axis=0, keepdims=True)

    whole = lambda shape: pl.BlockSpec(shape, lambda i: (0,) * len(shape))
    halo_rows = 16
    per = CHUNK // halo_rows
    return _call(
        body,
        name="mixer_bwd",
        operands=[proj, proj, dcat, conv_w, ln_g, ln_b, ws, bs_b],
        plans=plans,
        grid=(nt,),
        in_specs=[
            pl.BlockSpec((CHUNK, IN_DIM), lambda i: (nt - 1 - i, 0)),
            pl.BlockSpec((halo_rows, IN_DIM), lambda i: (jnp.maximum((nt - 1 - i) * per - 1, 0), 0)),
            pl.BlockSpec((CHUNK, MIX_DIM), lambda i: (nt - 1 - i, 0)),
            whole((8, D_MODEL)),
            whole((1, D_MODEL)),
            whole((1, D_MODEL)),
            whole((8, CHUNK, CHUNK)),
            whole((8, CHUNK, CHUNK)),
        ],
        out_specs=[
            pl.BlockSpec((CHUNK, IN_DIM), lambda i: (nt - 1 - i, 0)),
            whole((8, D_MODEL)),
            whole((1, D_MODEL)),
            whole((1, D_MODEL)),
            whole((8, CHUNK, CHUNK)),
            whole((1, D_MODEL)),
        ],
        out_shape=[
            jax.ShapeDtypeStruct((s, IN_DIM), BF16),
            jax.ShapeDtypeStruct((8, D_MODEL), F32),
            jax.ShapeDtypeStruct((1, D_MODEL), F32),
            jax.ShapeDtypeStruct((1, D_MODEL), F32),
            jax.ShapeDtypeStruct((8, CHUNK, CHUNK), F32),
            jax.ShapeDtypeStruct((1, D_MODEL), F32),
        ],
        scratch_shapes=[
            pltpu.VMEM((8, D_MODEL), F32),
            pltpu.VMEM((3, 8, D_MODEL), F32),
            pltpu.VMEM((8, D_MODEL), F32),
            pltpu.VMEM((8, D_MODEL), F32),
            pltpu.VMEM((8, CHUNK, CHUNK), F32),
        ],
    )


def _kv_fwd(mem, g_mem, w_kv):
    def body(mem_ref, g_ref, w_ref, k_ref, vv_ref, mt_ref):
        mh, _ = _rms(mem_ref[...])
        m = mh * g_ref[...]
        kv = _dot(m.astype(BF16), w_ref[...])
        k_ref[...] = kv[:, :D_MODEL].astype(BF16)
        vv_ref[...] = kv[:, D_MODEL:].astype(BF16)
        mt_ref[...] = m.T.astype(BF16)

    return pl.pallas_call(
        body,
        name="kv_fwd",
        out_shape=[
            jax.ShapeDtypeStruct((MEM_LEN, D_MODEL), BF16),
            jax.ShapeDtypeStruct((MEM_LEN, D_MODEL), BF16),
            jax.ShapeDtypeStruct((D_MODEL, MEM_LEN), BF16),
        ],
    )(mem, g_mem, w_kv)


def _kv_bwd(mem, w_kv, mt, dk, dvv):
    def body(mem_ref, w_ref, mt_ref, dk_ref, dvv_ref, dw_ref, dg_ref):
        dkv = jnp.concatenate([dk_ref[...], dvv_ref[...]], axis=1).astype(BF16)
        dw_ref[...] = _dot(mt_ref[...], dkv)
        dm = _dot_nt(dkv, w_ref[...])
        mh, _ = _rms(mem_ref[...])
        dg_ref[...] = jnp.sum(dm * mh, axis=0, keepdims=True)

    return pl.pallas_call(
        body,
        name="kv_bwd",
        out_shape=[
            jax.ShapeDtypeStruct((D_MODEL, 2 * D_MODEL), F32),
            jax.ShapeDtypeStruct((1, D_MODEL), F32),
        ],
    )(mem, w_kv, mt, dk, dvv)


def _attn(x, cat, target, w_out, w_q, w_xo, k, vv, g_x, g_f):
    s = x.shape[0]
    tm = min(TM_ATTN, s)
    sub = min(SUB_ATTN, tm)
    nt = s // tm
    scale = 1.0 / math.sqrt(X_HEAD_DIM)

    def body(x_ref, cat_ref, t_ref, wout_ref, wq_ref, wxo_ref, k_ref, vv_ref, gx_ref, gf_ref,
             dcat_ref, catt_ref, h2t_ref, ot_ref, dx2b_ref, dqb_ref, dx1b_ref,
             dk_ref, dvv_ref, dgf_ref, dgx_ref, loss_ref,
             p_scr, q_scr, o_scr, dq_scr, acc_gf, acc_gx, acc_loss):
        i = pl.program_id(0)

        @pl.when(i == 0)
        def _():
            dk_ref[...] = jnp.zeros_like(dk_ref)
            dvv_ref[...] = jnp.zeros_like(dvv_ref)
            acc_gf[...] = jnp.zeros_like(acc_gf)
            acc_gx[...] = jnp.zeros_like(acc_gx)
            acc_loss[...] = jnp.zeros_like(acc_loss)

        subs = [slice(u * sub, (u + 1) * sub) for u in range(tm // sub)]
        heads = [slice(X_HEAD_DIM * h, X_HEAD_DIM * (h + 1)) for h in range(X_HEADS)]
        gx, gf = gx_ref[...], gf_ref[...]
        x1, x1h, r2 = [], [], []
        for rs in subs:
            cat = cat_ref[rs, :]
            catt_ref[:, rs] = cat.astype(F32).T.astype(BF16)
            v = x_ref[rs, :] + _dot(cat, wout_ref[...])
            vh, r = _rms(v)
            h2 = vh * gx
            h2t_ref[:, rs] = h2.T.astype(BF16)
            q_scr[rs, :] = _dot(h2.astype(BF16), wq_ref[...]).astype(BF16)
            x1.append(v), x1h.append(vh), r2.append(r)
        for rs in subs:
            for hs in heads:
                sc = _dot_nt(q_scr[rs, hs], k_ref[:, hs]) * scale
                e = jnp.exp(sc - jnp.max(sc, axis=-1, keepdims=True))
                p = e / jnp.sum(e, axis=-1, keepdims=True)
                p_scr[rs, hs] = p
                o_scr[rs, hs] = _dot(p.astype(BF16), vv_ref[:, hs])
        dx2, sq, d_gf = [], None, None
        for u, rs in enumerate(subs):
            o = o_scr[rs, :]
            ot_ref[:, rs] = o.T.astype(BF16)
            x2 = x1[u] + _dot(o.astype(BF16), wxo_ref[...])
            x2h, r3 = _rms(x2)
            err = x2h * gf - t_ref[rs, :]
            dy = err * (1.0 / D_MODEL)
            sq = _rows8(err * err) if sq is None else sq + _rows8(err * err)
            d_gf = _rows8(dy * x2h) if d_gf is None else d_gf + _rows8(dy * x2h)
            dx2.append(_rms_bwd(dy * gf, x2h, r3))
        acc_loss[...] += sq
        acc_gf[...] += d_gf
        d_vv, d_k = [None] * X_HEADS, [None] * X_HEADS
        for u, rs in enumerate(subs):
            dx2b = dx2[u].astype(BF16)
            dx2b_ref[rs, :] = dx2b
            dob = _dot_nt(dx2b, wxo_ref[...]).astype(BF16)
            for h, hs in enumerate(heads):
                p = p_scr[rs, hs]
                dp = _dot_nt(dob[:, hs], vv_ref[:, hs])
                ds = p * (dp - jnp.sum(dp * p, axis=-1, keepdims=True)) * scale
                dq_scr[rs, hs] = _dot(ds.astype(BF16), k_ref[:, hs]).astype(BF16)
                dvv_h = _dot(p.T.astype(BF16), dob[:, hs])
                dk_h = _dot(ds.T.astype(BF16), q_scr[rs, hs])
                d_vv[h] = dvv_h if d_vv[h] is None else d_vv[h] + dvv_h
                d_k[h] = dk_h if d_k[h] is None else d_k[h] + dk_h
        for h, hs in enumerate(heads):
            dvv_ref[:, hs] += d_vv[h]
            dk_ref[:, hs] += d_k[h]
        d_gx = None
        for u, rs in enumerate(subs):
            dqb = dq_scr[rs, :]
            dqb_ref[rs, :] = dqb
            dh2 = _dot_nt(dqb, wq_ref[...])
            d_gx = _rows8(dh2 * x1h[u]) if d_gx is None else d_gx + _rows8(dh2 * x1h[u])
            dx1 = dx2[u] + _rms_bwd(dh2 * gx, x1h[u], r2[u])
            dx1b = dx1.astype(BF16)
            dx1b_ref[rs, :] = dx1b
            dcat_ref[rs, :] = _dot_nt(dx1b, wout_ref[...]).astype(BF16)
        acc_gx[...] += d_gx

        @pl.when(i == nt - 1)
        def _():
            dgf_ref[...] = jnp.sum(acc_gf[...], axis=0, keepdims=True)
            dgx_ref[...] = jnp.sum(acc_gx[...], axis=0, keepdims=True)
            loss_ref[...] = jnp.sum(acc_loss[...], axis=0, keepdims=True)

    whole = lambda shape: pl.BlockSpec(shape, lambda i: (0,) * len(shape))
    once = lambda shape: pl.BlockSpec(shape, lambda i: (0,) * len(shape), pipeline_mode=pl.Buffered(1))
    rows = lambda width: pl.BlockSpec((tm, width), lambda i: (i, 0))
    cols = lambda height: pl.BlockSpec((height, tm), lambda i: (0, i))
    vec = jax.ShapeDtypeStruct((1, D_MODEL), F32)
    return pl.pallas_call(
        body,
        name="attn",
        grid=(nt,),
        in_specs=[
            rows(D_MODEL), rows(MIX_DIM), rows(D_MODEL),
            once((MIX_DIM, D_MODEL)), once((D_MODEL, D_MODEL)), once((D_MODEL, D_MODEL)),
            once((MEM_LEN, D_MODEL)), once((MEM_LEN, D_MODEL)),
            whole((1, D_MODEL)), whole((1, D_MODEL)),
        ],
        out_specs=[
            rows(MIX_DIM), cols(MIX_DIM), cols(D_MODEL), cols(D_MODEL),
            rows(D_MODEL), rows(D_MODEL), rows(D_MODEL),
            whole((MEM_LEN, D_MODEL)), whole((MEM_LEN, D_MODEL)),
            whole((1, D_MODEL)), whole((1, D_MODEL)), whole((1, D_MODEL)),
        ],
        out_shape=[
            jax.ShapeDtypeStruct((s, MIX_DIM), BF16),
            jax.ShapeDtypeStruct((MIX_DIM, s), BF16),
            jax.ShapeDtypeStruct((D_MODEL, s), BF16),
            jax.ShapeDtypeStruct((D_MODEL, s), BF16),
            jax.ShapeDtypeStruct((s, D_MODEL), BF16),
            jax.ShapeDtypeStruct((s, D_MODEL), BF16),
            jax.ShapeDtypeStruct((s, D_MODEL), BF16),
            jax.ShapeDtypeStruct((MEM_LEN, D_MODEL), F32),
            jax.ShapeDtypeStruct((MEM_LEN, D_MODEL), F32),
            vec, vec, vec,
        ],
        scratch_shapes=[
            pltpu.VMEM((tm, D_MODEL), F32),
            pltpu.VMEM((tm, D_MODEL), BF16),
            pltpu.VMEM((tm, D_MODEL), F32),
            pltpu.VMEM((tm, D_MODEL), BF16),
            pltpu.VMEM((8, D_MODEL), F32),
            pltpu.VMEM((8, D_MODEL), F32),
            pltpu.VMEM((8, D_MODEL), F32),
        ],
        compiler_params=pltpu.CompilerParams(dimension_semantics=("arbitrary",)),
    )(x, cat, target, w_out, w_q, w_xo, k, vv, g_x, g_f)


def _matmul_f32(name, a, b, rows=None, plans=()):
    first, m = (0, a.shape[0]) if rows is None else rows
    kk = a.shape[1]
    n = b.shape[1]
    tm, tn, tk = min(m, TM_DW), min(n, D_MODEL), min(kk, TK_DW)
    nk = kk // tk
    i0 = first // tm

    def body(a_ref, b_ref, o_ref):
        prod = _dot(a_ref[...], b_ref[...])
        if nk == 1:
            o_ref[...] = prod
            return
        kid = pl.program_id(2)

        @pl.when(kid == 0)
        def _():
            o_ref[...] = prod

        @pl.when(kid > 0)
        def _():
            o_ref[...] += prod

    (out,), plan_outs = _call(
        body,
        name=name,
        operands=[a, b],
        grid=(m // tm, n // tn, nk),
        in_specs=[pl.BlockSpec((tm, tk), lambda i, j, k: (i + i0, k)), pl.BlockSpec((tk, tn), lambda i, j, k: (k, j))],
        out_specs=[pl.BlockSpec((tm, tn), lambda i, j, k: (i, j))],
        out_shape=[jax.ShapeDtypeStruct((m, n), F32)],
        plans=plans,
    )
    return out, plan_outs


def _dh1(name, dproj, w_in, x, dx1, g, tiles, into=None, after=(), plans=()):
    s = x.shape[0]
    tm = min(TM_DH1, s)
    t0, nt = tiles[0], tiles[1] - tiles[0]

    def body(dp_ref, w_ref, x_ref, dx1_ref, g_ref, *rest):
        gx_ref, dg_ref, acc_g = rest[-3:]
        i = pl.program_id(0)

        @pl.when(i == 0)
        def _():
            acc_g[...] = jnp.zeros_like(acc_g)

        dh1 = _dot_nt(dp_ref[...], w_ref[...])
        xh, r = _rms(x_ref[...])
        acc_g[...] += _rows8(dh1 * xh)
        gx_ref[...] = dx1_ref[...].astype(F32) + _rms_bwd(dh1 * g_ref[...], xh, r)

        @pl.when(i == nt - 1)
        def _():
            dg_ref[...] = jnp.sum(acc_g[...], axis=0, keepdims=True)

    rows = pl.BlockSpec((tm, D_MODEL), lambda i: (i + t0, 0))
    return _call(
        body,
        name=name,
        operands=[dproj, w_in, x, dx1, g] + ([into] if into is not None else []) + list(after),
        grid=(nt,),
        in_specs=[
            pl.BlockSpec((tm, IN_DIM), lambda i: (i + t0, 0)),
            pl.BlockSpec((D_MODEL, IN_DIM), lambda i: (0, 0), pipeline_mode=pl.Buffered(1)),
            rows,
            rows,
            pl.BlockSpec((1, D_MODEL), lambda i: (0, 0)),
        ] + [pl.BlockSpec(memory_space=pl.ANY)] * ((into is not None) + len(after)),
        out_specs=[rows, pl.BlockSpec((1, D_MODEL), lambda i: (0, 0))],
        out_shape=[jax.ShapeDtypeStruct((s, D_MODEL), F32), jax.ShapeDtypeStruct((1, D_MODEL), F32)],
        scratch_shapes=[pltpu.VMEM((8, D_MODEL), F32)],
        plans=plans,
        aliases={5: 0} if into is not None else None,
    )


def _pair_sum(name, grad, got, axis, size, core, plans=()):
    shard = list(grad.shape)
    shard[axis] = size
    r, cdim = shard
    tr = min(r, TR_PAIR_SUM)
    nr = r // tr

    def body(core_ref, g_ref, got_ref, o_ref):
        o_ref[0] = (g_ref[...] + got_ref[0]).astype(BF16)

    if axis == 1:
        g_map = lambda q, t, core_ref: (t, 2 * q + core_ref[0])
    else:
        g_map = lambda q, t, core_ref: ((2 * q + core_ref[0]) * nr + t, 0)
    (out,), plan_outs = _call(
        body,
        name=name,
        prefetch=[core],
        operands=[grad, got],
        grid=(N_CHIP, nr),
        in_specs=[
            pl.BlockSpec((tr, cdim), g_map),
            pl.BlockSpec((1, tr, cdim), lambda q, t, core_ref: (q, t, 0)),
        ],
        out_specs=[pl.BlockSpec((1, tr, cdim), lambda q, t, core_ref: (q, t, 0))],
        out_shape=[jax.ShapeDtypeStruct((N_CHIP, r, cdim), BF16)],
        plans=plans,
    )
    return (out, plan_outs) if plans else out


def _adam_step(w, m, v, g):
    c1 = 1.0 / (1.0 - ADAM_B1**ADAM_STEP)
    c2 = 1.0 / (1.0 - ADAM_B2**ADAM_STEP)
    m_new = ADAM_B1 * m + (1.0 - ADAM_B1) * g
    v_new = ADAM_B2 * v + (1.0 - ADAM_B2) * (g * g)
    return -ADAM_LR * ((m_new * c1) / (jnp.sqrt(v_new * c2) + ADAM_EPS) + ADAM_WD * w), m_new, v_new


def _update_small(dev, vec_w, vec_m, vec_v, conv_w, conv_m, conv_v, small, landed):
    n_vec = len(vec_w)

    def body(dev_ref, *refs):
        refs = list(refs)
        take = lambda k: [refs.pop(0) for _ in range(k)]
        w_refs, m_refs, v_refs = take(n_vec), take(n_vec), take(n_vec)
        cw_ref, cm_ref, cv_ref, small_ref, land_ref, small_conv_ref, land_conv_ref = take(7)
        vec_outs, conv_outs, (loss_ref,) = take(4 * n_vec), take(4), take(1)

        def total(own_ref, others_ref):
            acc = None
            for d in range(N_DEV):
                part = jnp.where(dev_ref[0] == d, own_ref[0], others_ref[d])
                acc = part if acc is None else acc + part
            return acc

        g_all = total(small_ref, land_ref)
        for i in range(n_vec):
            g = g_all[i : i + 1]
            delta, m_new, v_new = _adam_step(w_refs[i][...], m_refs[i][...], v_refs[i][...], g)
            for ref, val in zip(vec_outs[4 * i : 4 * i + 4], (g, delta, m_new, v_new)):
                ref[...] = val
        g = total(small_conv_ref, land_conv_ref)[0:3]
        delta, m_new, v_new = _adam_step(cw_ref[0], cm_ref[0], cv_ref[0], g)
        for ref, val in zip(conv_outs, (g, delta, m_new, v_new)):
            ref[0] = val
        loss_ref[...] = jnp.broadcast_to(0.5 * jnp.sum(g_all[7:8], axis=-1, keepdims=True) / D_MODEL, loss_ref.shape)

    vec_spec = pl.BlockSpec((1, D_MODEL), lambda i, dev_ref: (0, 0))
    conv_spec = pl.BlockSpec((1, 3, CHUNK), lambda i, dev_ref: (0, 0, 0))
    vec_shape = jax.ShapeDtypeStruct((1, D_MODEL), F32)
    conv_shape = jax.ShapeDtypeStruct((1, 3, CHUNK), F32)
    outs, _ = _call(
        body,
        name="update_small",
        prefetch=[dev],
        operands=list(vec_w) + list(vec_m) + list(vec_v) + [conv_w, conv_m, conv_v, small, landed, small, landed],
        grid=(1,),
        in_specs=[vec_spec] * (3 * n_vec) + [conv_spec] * 3 + [
            pl.BlockSpec((1, 16, D_MODEL), lambda i, dev_ref: (0, 0, 0)),
            pl.BlockSpec((N_DEV, 16, D_MODEL), lambda i, dev_ref: (0, 0, 0)),
            pl.BlockSpec((1, 8, CHUNK), lambda i, dev_ref: (0, 1, dev_ref[0])),
            pl.BlockSpec((N_DEV, 8, CHUNK), lambda i, dev_ref: (0, 1, dev_ref[0])),
        ],
        out_specs=[vec_spec] * (4 * n_vec) + [conv_spec] * 4 + [pl.BlockSpec((1, CHUNK), lambda i, dev_ref: (0, 0))],
        out_shape=[vec_shape] * (4 * n_vec) + [conv_shape] * 4 + [jax.ShapeDtypeStruct((1, CHUNK), F32)],
    )
    return [outs[4 * i : 4 * i + 4] for i in range(n_vec)], outs[4 * n_vec : 4 * n_vec + 4], outs[-1]


def _adamw(name, groups, steps, own_slot=None, plans=()):
    lists = lambda p: list(p) if isinstance(p, (list, tuple)) else [p]
    groups = [(g[0], g[1], g[2], lists(g[3]), lists(g[4]) if len(g) > 4 else []) for g in groups]
    counts = [3 + len(g[3]) + len(g[4]) for g in groups]
    use_slot = own_slot is not None

    def first_tiles(w, parts):
        firsts, first = [], 0
        for p in parts:
            firsts.append(first)
            first += p.shape[1] // (w.shape[0] // steps)
        return firsts

    def update(t, slot_ref, firsts, w_ref, m_ref, v_ref, p_refs, o_refs, g_ref, d_ref, nm_ref, nv_ref):
        g = None
        for k, p_ref in enumerate(p_refs):
            gk = None
            for q in range(p_ref.shape[0]):
                part = p_ref[q].astype(F32)
                if o_refs:
                    part = jnp.where(slot_ref[0] == q, o_refs[k][0].astype(F32), part)
                gk = part if gk is None else gk + part
            g = gk if g is None else jnp.where(t >= firsts[k], gk, g)
        g_ref[...] = g
        d_ref[...], nm_ref[...], nv_ref[...] = _adam_step(w_ref[...], m_ref[...], v_ref[...], g)

    def body(*refs):
        refs = list(refs)
        slot_ref = refs.pop(0) if use_slot else None
        t = pl.program_id(0)
        ins, outs = refs[: sum(counts)], refs[sum(counts) :]
        for i, (w, _, _, parts, own) in enumerate(groups):
            mine = ins[sum(counts[:i]) : sum(counts[: i + 1])]
            update(t, slot_ref, first_tiles(w, parts), *mine[:3], mine[3 : 3 + len(parts)], mine[3 + len(parts) :],
                   *outs[4 * i : 4 * i + 4])

    operands, in_specs, out_specs, out_shape = [], [], [], []
    for w, m, v, parts, own in groups:
        r, cdim = w.shape
        tr = r // steps
        firsts = first_tiles(w, parts)

        def tile_of(k, t, firsts=firsts, parts=parts, tr=tr):
            return jnp.clip(t - firsts[k], 0, parts[k].shape[1] // tr - 1)

        blk = pl.BlockSpec((tr, cdim), lambda t, *_: (t, 0))
        operands += [w, m, v] + parts + own
        in_specs += [blk, blk, blk]
        in_specs += [pl.BlockSpec((p.shape[0], tr, cdim), lambda t, *_, k=k, tile_of=tile_of: (0, tile_of(k, t), 0)) for k, p in enumerate(parts)]
        in_specs += [pl.BlockSpec((1, tr, cdim), lambda t, slot_ref, k=k, tile_of=tile_of: (slot_ref[0], tile_of(k, t), 0)) for k in range(len(own))]
        out_specs += [blk] * 4
        out_shape += [jax.ShapeDtypeStruct((r, cdim), F32)] * 4
    outs, plan_outs = _call(
        body,
        name=name,
        prefetch=[own_slot] if use_slot else [],
        operands=operands,
        grid=(steps,),
        in_specs=in_specs,
        out_specs=out_specs,
        out_shape=out_shape,
        plans=plans,
    )
    return [outs[4 * i : 4 * i + 4] for i in range(len(groups))], plan_outs


def _chip_routes(n):
    x, y, c = _place()
    my_chip = 2 * x + y
    return [(a, (*chip, c), 2 * chip[0] + chip[1], my_chip, 2 * chip[0] + chip[1])
            for a in range(n) for chip in _other_chips(x, y)]


def _gather_routes(n):
    x, y, c = _place()
    flip = lambda v, f: 1 - v if f else v
    peers = [(flip(x, fx), flip(y, fy), flip(c, fc)) for fx in (0, 1) for fy in (0, 1) for fc in (0, 1) if fx + fy + fc]
    return [(0, p, 0, 4 * x + 2 * y + c, 4 * p[0] + 2 * p[1] + p[2]) for p in peers]


_chip_routes.copies_per_array = N_CHIP - 1
_gather_routes.copies_per_array = N_DEV - 1


def _split_start(name, routes, srcs, land_shapes):
    n = len(srcs)
    hbm = pl.BlockSpec(memory_space=pltpu.HBM)
    sem = pl.BlockSpec(memory_space=pltpu.SEMAPHORE)

    def body(*refs):
        src_refs, land_refs = refs[:n], refs[n : 2 * n]
        send_sems, recv_sems = refs[2 * n], refs[2 * n + 1]
        token = refs[-1]
        for k, (a, peer, src_slot, there, _) in enumerate(routes(n)):
            pltpu.make_async_remote_copy(
                src_ref=src_refs[a].at[src_slot], dst_ref=land_refs[a].at[there],
                send_sem=send_sems.at[k], recv_sem=recv_sems.at[k], device_id=peer, device_id_type=MESH).start()
        token[...] = jnp.zeros_like(token)

    in_hbm = [pltpu.with_memory_space_constraint(p, pltpu.HBM) for p in srcs]
    in_hbm += [pltpu.with_memory_space_constraint(lax.empty(shape, p.dtype), pltpu.HBM) for p, shape in zip(srcs, land_shapes)]
    n_copies = routes.copies_per_array * n
    res = pl.pallas_call(
        body,
        name=name,
        out_shape=[pltpu.SemaphoreType.DMA((n_copies,)), pltpu.SemaphoreType.DMA((n_copies,))]
        + [pltpu.HBM(p.shape, p.dtype) for p in srcs] + [pltpu.HBM(shape, p.dtype) for p, shape in zip(srcs, land_shapes)]
        + [jax.ShapeDtypeStruct((8, CHUNK), F32)],
        in_specs=[hbm] * (2 * n),
        out_specs=[sem, sem] + [hbm] * (2 * n) + [pl.BlockSpec(memory_space=pltpu.VMEM)],
        input_output_aliases={i: 2 + i for i in range(2 * n)},
        compiler_params=pltpu.CompilerParams(has_side_effects=pltpu.SideEffectType.DATAFLOW_SIDE_EFFECTING),
    )(*in_hbm)
    return res[:-1], res[-1]


def _split_wait(name, routes, in_flight, after):
    n = (len(in_flight) - 2) // 2
    hbm = pl.BlockSpec(memory_space=pltpu.HBM)
    sem = pl.BlockSpec(memory_space=pltpu.SEMAPHORE)
    send_sems, recv_sems, *bufs = in_flight

    def body(*refs):
        src_refs, land_refs = refs[:n], refs[n : 2 * n]
        send_ref, recv_ref = refs[2 * n], refs[2 * n + 1]
        for k, (a, peer, src_slot, _, here) in enumerate(routes(n)):
            cp = pltpu.make_async_remote_copy(
                src_ref=src_refs[a].at[src_slot], dst_ref=land_refs[a].at[here],
                send_sem=send_ref.at[k], recv_sem=recv_ref.at[k], device_id=peer, device_id_type=MESH)
            cp.wait_send()
            cp.wait_recv()

    res = pl.pallas_call(
        body,
        name=name,
        out_shape=[pltpu.HBM(b.shape, b.dtype) for b in bufs],
        in_specs=[hbm] * (2 * n) + [sem, sem] + [pl.BlockSpec(memory_space=pl.ANY)] * len(after),
        out_specs=[hbm] * (2 * n),
        input_output_aliases={i: i for i in range(2 * n)},
        compiler_params=pltpu.CompilerParams(has_side_effects=pltpu.SideEffectType.DATAFLOW_SIDE_EFFECTING),
    )(*bufs, send_sems, recv_sems, *after)
    return res[:n], res[n:]


def kernel(x, mem, norm_mix_g, w_in, conv_w, gm_ln_g, gm_ln_b, gm_ws, gm_bs, w_out, norm_x_g, norm_mem_g, w_q, w_kv, w_xo, norm_final_g, loss_target, m_norm_mix_g, m_w_in, m_conv_w, m_gm_ln_g, m_gm_ln_b, m_gm_ws, m_gm_bs, m_w_out, m_norm_x_g, m_norm_mem_g, m_w_q, m_w_kv, m_w_xo, m_norm_final_g, v_norm_mix_g, v_w_in, v_conv_w, v_gm_ln_g, v_gm_ln_b, v_gm_ws, v_gm_bs, v_w_out, v_norm_x_g, v_norm_mem_g, v_w_q, v_w_kv, v_w_xo, v_norm_final_g):
    s = x.shape[1]
    dev = 4 * lax.axis_index("x") + 2 * lax.axis_index("y") + lax.axis_index("c")
    core = lax.axis_index("c").astype(jnp.int32).reshape(1)
    x2 = x[0]
    target = loss_target[0]
    pad_taps = lambda t: jnp.pad(t[0], ((0, 5), (0, 0)))

    big_names = ["w_in", "w_out", "w_q", "w_kv", "w_xo"]
    big_w = dict(w_in=w_in[0], w_out=w_out[0], w_q=w_q[0], w_kv=w_kv[0], w_xo=w_xo[0])
    big_m = dict(w_in=m_w_in[0], w_out=m_w_out[0], w_q=m_w_q[0], w_kv=m_w_kv[0], w_xo=m_w_xo[0])
    big_v = dict(w_in=v_w_in[0], w_out=v_w_out[0], w_q=v_w_q[0], w_kv=v_w_kv[0], w_xo=v_w_xo[0])
    shard_axis = dict(w_in=1, w_out=0, w_q=0, w_kv=1, w_xo=0)
    full_shape = dict(w_in=(D_MODEL, IN_DIM), w_out=(MIX_DIM, D_MODEL), w_q=(D_MODEL, D_MODEL),
                      w_kv=(D_MODEL, 2 * D_MODEL), w_xo=(D_MODEL, D_MODEL))
    shard_size = {k: big_w[k].shape[shard_axis[k]] for k in big_names}
    others = big_names[1:]
    gather_spec = lambda k: (full_shape[k], shard_axis[k], shard_size[k])
    reduce_spec = lambda k: (shard_axis[k], shard_size[k])
    ws = gm_ws[0]
    bs_b = jnp.broadcast_to(gm_bs[0][:, :, None], (8, CHUNK, CHUNK))
    g_f = norm_final_g.reshape(1, D_MODEL)

    px, py = lax.axis_index("x"), lax.axis_index("y")
    chip_order = jnp.stack([2 * px + py, 2 * (1 - px) + py, 2 * px + 1 - py, 2 * (1 - px) + 1 - py]).astype(jnp.int32)
    (proj, h1, h1t, w_in_full), ((w_out_full, conv_g),) = _inproj(
        x2, norm_mix_g, big_w["w_in"].astype(BF16), chip_order,
        plans=[_gather_plan([big_w["w_out"].astype(BF16), pad_taps(conv_w)[None]], [gather_spec("w_out"), ((N_DEV, 8, CHUNK), 0, 1)])])
    conv_full = conv_g.transpose(1, 0, 2).reshape(8, D_MODEL)
    attn_w = ["w_q", "w_kv", "w_xo"]
    cat, (gathered,) = _mixer_fwd(
        proj, conv_full, gm_ln_g, gm_ln_b, ws, bs_b,
        plans=[_gather_plan([big_w[k].astype(BF16) for k in attn_w], [gather_spec(k) for k in attn_w])])
    wf = dict(zip(attn_w, gathered), w_out=w_out_full)
    k, vv, mt = _kv_fwd(mem[0], norm_mem_g, wf["w_kv"])
    (dcat, catt, h2t, ot, dx2b, dqb, dx1b, dk, dvv, d_gf, d_gx, sq_err) = _attn(
        x2, cat, target, wf["w_out"], wf["w_q"], wf["w_xo"], k, vv, norm_x_g, g_f)

    def pair_sums(names, grads, got):
        return [_pair_sum("pair_sum_" + k, grads[k], got[i], *reduce_spec(k), core) for i, k in enumerate(names)]

    sibling_plan = lambda names, grads: _sibling_plan([grads[k] for k in names], [reduce_spec(k) for k in names])
    early, late = ["w_out", "w_kv", "w_q"], ["w_xo"]
    big_g = dict(w_out=_matmul_f32("dw_out", catt, dx1b)[0])
    big_g["w_kv"], d_gmem = _kv_bwd(mem[0], wf["w_kv"], mt, dk, dvv)
    big_g["w_q"], (got_a,) = _matmul_f32("dw_q", h2t, dqb, plans=[sibling_plan(early[:2], big_g)])
    big_g["w_xo"], (got_b,) = _matmul_f32("dw_xo", ot, dx2b, plans=[sibling_plan(early[2:], big_g)])
    got_early = got_a + got_b
    (dproj, d_cw, d_lng, d_lnb, d_ws, d_bs), (parts_early, got_late) = _mixer_bwd(
        proj, dcat, conv_full, gm_ln_g, gm_ln_b, ws, bs_b,
        plans=[_chip_plan(pair_sums(early, big_g, got_early)), sibling_plan(late, big_g)])
    half = D_MODEL // 2
    in_spec = reduce_spec("w_in")
    dw_top, _ = _matmul_f32("dw_in_top", h1t, dproj, rows=(0, half))
    pair_xo = pair_sums(late, big_g, got_late)[0]
    dw_bot, (got_top, (ws_all,)) = _matmul_f32(
        "dw_in_bot", h1t, dproj, rows=(half, half),
        plans=[_sibling_plan([dw_top], [in_spec]),
               _gather_plan([d_ws.reshape(1, 8 * CHUNK, CHUNK).astype(BF16)], [((N_DEV, 8 * CHUNK, CHUNK), 0, 1)])])
    pair_top, ((got_bot,),) = _pair_sum("pair_sum_w_in_top", dw_top, got_top[0], *in_spec, core,
                                        plans=[_sibling_plan([dw_bot], [in_spec])])
    pair_bot = _pair_sum("pair_sum_w_in_bot", dw_bot, got_bot, *in_spec, core)
    chip_shapes = [p.shape for p in (pair_top, pair_bot, pair_xo)]
    chips_in_flight, token = _split_start("grad_chip_exchange_start", _chip_routes, [pair_top, pair_bot, pair_xo], chip_shapes)
    n_tiles = s // min(TM_DH1, s)
    (grad_x, d_gmix), _ = _dh1("dh1", dproj, w_in_full, x2, dx1b, norm_mix_g, (0, n_tiles), after=[token])
    vec_names = ["norm_mix_g", "gm_ln_g", "gm_ln_b", "gm_bs", "norm_x_g", "norm_mem_g", "norm_final_g"]
    vec_g = [d_gmix, d_lng, d_lnb, d_bs, d_gx, d_gmem, d_gf]
    small = jnp.concatenate(vec_g + [sq_err, d_cw], axis=0)[None]
    small_in_flight, _ = _split_start("small_all_gather_start", _gather_routes, [small], [(N_DEV, 16, D_MODEL)])
    flat_ws = lambda t: t.reshape(8 * CHUNK, CHUNK)
    updated, _ = _adamw(
        "adamw_early",
        [(big_w[k], big_m[k], big_v[k], got_k) for k, got_k in zip(early, parts_early)]
        + [(flat_ws(ws), flat_ws(m_gm_ws), flat_ws(v_gm_ws), ws_all)], steps=2)
    big_out = dict(zip(early, updated[:3]))
    ws_out = updated[3]
    (pair_top, pair_bot, pair_xo), (land_top, land_bot, land_xo) = _split_wait(
        "grad_chip_exchange_wait", _chip_routes, chips_in_flight, [d_gmix] + [u[0] for u in updated])
    my_chip = (2 * px + py).astype(jnp.int32).reshape(1)
    (big_out["w_in"], big_out["w_xo"]), _ = _adamw(
        "adamw_late",
        [(big_w["w_in"], big_m["w_in"], big_v["w_in"], [land_top, land_bot], [pair_top, pair_bot]),
         (big_w["w_xo"], big_m["w_xo"], big_v["w_xo"], land_xo, pair_xo)],
        steps=4, own_slot=my_chip)
    (small,), (small_all,) = _split_wait("small_all_gather_wait", _gather_routes, small_in_flight, [big_out["w_xo"][0]])
    row = lambda t: t.reshape(1, D_MODEL)
    vec_out, conv_out, loss_row = _update_small(
        dev.astype(jnp.int32).reshape(1),
        [norm_mix_g, gm_ln_g, gm_ln_b, row(gm_bs), norm_x_g, norm_mem_g, row(norm_final_g)],
        [m_norm_mix_g, m_gm_ln_g, m_gm_ln_b, row(m_gm_bs), m_norm_x_g, m_norm_mem_g, row(m_norm_final_g)],
        [v_norm_mix_g, v_gm_ln_g, v_gm_ln_b, row(v_gm_bs), v_norm_x_g, v_norm_mem_g, row(v_norm_final_g)],
        conv_w, m_conv_w, v_conv_w, small, small_all)
    loss = loss_row[0, 0]

    def result(name, which):
        if name in big_out:
            return big_out[name][which][None]
        if name == "conv_w":
            return conv_out[which]
        if name == "gm_ws":
            return ws_out[which].reshape(1, 8, CHUNK, CHUNK)
        r = vec_out[vec_names.index(name)][which]
        if name == "gm_bs":
            return r.reshape(1, 8, CHUNK)
        if name == "norm_final_g":
            return r.reshape(D_MODEL)
        return r

    weights = ["norm_mix_g", "w_in", "conv_w", "gm_ln_g", "gm_ln_b", "gm_ws", "gm_bs", "w_out", "norm_x_g", "norm_mem_g", "w_q", "w_kv", "w_xo", "norm_final_g"]
    outs = [loss, grad_x[None]]
    for which in range(4):
        outs += [result(name, which) for name in weights]
    return tuple(outs)
```

```python
import math

import jax
import jax.numpy as jnp
from jax import lax
from jax.experimental import pallas as pl
from jax.experimental.pallas import tpu as pltpu

F32 = jnp.float32
BF16 = jnp.bfloat16
MESH = pl.DeviceIdType.MESH

D_MODEL = 1024
N_DEV = 8
N_CHIP = 4
CHUNK = 128
GROUPS = 7
IN_DIM = GROUPS * D_MODEL
SHARD_IN = IN_DIM // N_DEV
MIX_DIM = 2 * D_MODEL
X_HEADS = 4
X_HEAD_DIM = D_MODEL // X_HEADS
MEM_LEN = 256
EPS = 1e-6
GELU_K0 = math.sqrt(2.0 / math.pi)
GELU_K1 = 0.044715

ADAM_LR = 0.001
ADAM_B1 = 0.9
ADAM_B2 = 0.999
ADAM_EPS = 1e-08
ADAM_WD = 0.01
ADAM_STEP = 10

TM_INPROJ = 512
TM_ATTN = 512
SUB_ATTN = 256
TM_DH1 = 512
TM_DW = 512
TK_DW = 4096
TR_PAIR_SUM = 1024


def _dot(a, b):
    return lax.dot_general(a, b, (((1,), (0,)), ((), ())), preferred_element_type=F32)


def _dot_nt(a, b):
    return lax.dot_general(a, b, (((1,), (1,)), ((), ())), preferred_element_type=F32)


def _rows8(v):
    return jnp.sum(v.reshape(v.shape[0] // 8, 8, v.shape[1]), axis=0)


def _gelu(x):
    x2 = x * x
    t = jnp.tanh(GELU_K0 * (x + GELU_K1 * (x * x2)))
    g = 0.5 * x * (1.0 + t)
    dg = 0.5 * (1.0 + t) + 0.5 * x * (1.0 - t * t) * (GELU_K0 * (1.0 + 3.0 * GELU_K1 * x2))
    return g, dg


def _silu(z):
    s = jax.nn.sigmoid(z)
    return z * s, s * (1.0 + z * (1.0 - s))


def _rms(v):
    r = lax.rsqrt(jnp.mean(v * v, axis=-1, keepdims=True) + EPS)
    return v * r, r


def _rms_bwd(dy_g, vh, r):
    return r * (dy_g - vh * jnp.mean(dy_g * vh, axis=-1, keepdims=True))


def _place():
    return lax.axis_index("x"), lax.axis_index("y"), lax.axis_index("c")


def _other_chips(x, y):
    return [(1 - x, y), (x, 1 - y), (1 - x, 1 - y)]


def _pow2_divisor(n):
    return n & (-n)


def _shard_view(ref, axis, size, d):
    start = d * size
    align = _pow2_divisor(size)
    if align > 1:
        start = pl.multiple_of(start, align)
    idx = [slice(None)] * len(ref.shape)
    idx[axis] = pl.ds(start, size)
    return ref.at[tuple(idx)]


class _Plan:
    def __init__(self, ins, out_shape, sems, start, finish, mid=None, aliases=None):
        self.ins, self.out_shape, self.sems = list(ins), list(out_shape), list(sems)
        self.start, self.mid, self.finish = start, mid, finish
        self.aliases = dict(aliases or {})


def _gather_plan(shards, specs):
    n = len(shards)
    n_copy = 7

    def copies(ins, outs, sems):
        send_sems, recv_sems, local_sems = sems
        x, y, c = _place()
        me, sibling = (x, y, c), (x, y, 1 - c)
        chips = _other_chips(x, y)

        def block(a, p):
            _, axis, size = specs[a]
            return _shard_view(outs[a], axis, size, 4 * p[0] + 2 * p[1] + p[2])

        def copy(a, k, p, to, src=None):
            return pltpu.make_async_remote_copy(
                src_ref=block(a, p) if src is None else src,
                dst_ref=block(a, p),
                send_sem=send_sems.at[a * n_copy + k],
                recv_sem=recv_sems.at[a * n_copy + k],
                device_id=to,
                device_id_type=MESH,
            )

        def mine():
            return [pltpu.make_async_copy(ins[a], block(a, me), local_sems.at[a]) for a in range(n)]

        def first():
            return [cp for a in range(n) for cp in
                    [copy(a, 0, me, sibling, src=ins[a])] + [copy(a, 1 + j, me, (*chip, c), src=ins[a]) for j, chip in enumerate(chips)]]

        def landed():
            return [copy(a, 1 + j, (*chip, c), me) for j, chip in enumerate(chips) for a in range(n)]

        def passed():
            return [copy(a, 4 + j, (*chip, c), sibling) for j, chip in enumerate(chips) for a in range(n)]

        def last():
            return [cp for a in range(n) for cp in
                    [copy(a, 0, sibling, me)] + [copy(a, 4 + j, (*chip, 1 - c), me) for j, chip in enumerate(chips)]]

        return mine, first, landed, passed, last

    def start(ins, outs, sems):
        mine, first, _, _, _ = copies(ins, outs, sems)
        for cp in mine() + first():
            cp.start()

    def mid(ins, outs, sems):
        _, _, landed, passed, _ = copies(ins, outs, sems)
        for got, fwd in zip(landed(), passed()):
            got.wait_recv()
            fwd.start()

    def finish(ins, outs, sems):
        mine, first, _, passed, last = copies(ins, outs, sems)
        for cp in last():
            cp.wait_recv()
        for cp in first() + passed():
            cp.wait_send()
        for cp in mine():
            cp.wait()

    return _Plan(
        shards,
        [jax.ShapeDtypeStruct(full, s.dtype) for s, (full, _, _) in zip(shards, specs)],
        [pltpu.SemaphoreType.DMA((n * n_copy,)), pltpu.SemaphoreType.DMA((n * n_copy,)), pltpu.SemaphoreType.DMA((n,))],
        start, finish, mid,
    )


def _sibling_plan(grads, specs):
    n = len(grads)

    def shard_shape(a):
        axis, size = specs[a]
        shp = list(grads[a].shape)
        shp[axis] = size
        return tuple(shp)

    def copies(ins, outs, sems):
        send_sems, recv_sems = sems
        x, y, c = _place()
        return [
            pltpu.make_async_remote_copy(
                src_ref=_shard_view(ins[a], specs[a][0], specs[a][1], 2 * q + (1 - c)),
                dst_ref=outs[a].at[q],
                send_sem=send_sems.at[a * N_CHIP + q],
                recv_sem=recv_sems.at[a * N_CHIP + q],
                device_id=(x, y, 1 - c),
                device_id_type=MESH,
            )
            for a in range(n) for q in range(N_CHIP)
        ]

    def start(ins, outs, sems):
        for cp in copies(ins, outs, sems):
            cp.start()

    def finish(ins, outs, sems):
        for cp in copies(ins, outs, sems):
            cp.wait()

    return _Plan(
        grads,
        [jax.ShapeDtypeStruct((N_CHIP, *shard_shape(a)), grads[a].dtype) for a in range(n)],
        [pltpu.SemaphoreType.DMA((n * N_CHIP,)), pltpu.SemaphoreType.DMA((n * N_CHIP,))],
        start, finish,
    )


def _chip_plan(parts, full_rows=None, row0=None, into=None):
    n = len(parts)
    full_rows = [p.shape[1] for p in parts] if full_rows is None else full_rows
    row0 = [0] * n if row0 is None else row0

    def copies(ins, outs, sems):
        send_sems, recv_sems, local_sems = sems
        x, y, c = _place()
        my_chip = 2 * x + y
        chips = _other_chips(x, y)

        def land(a, q):
            return outs[a].at[q, pl.ds(row0[a], parts[a].shape[1]), :]

        def mine():
            return [pltpu.make_async_copy(ins[a].at[my_chip], land(a, my_chip), local_sems.at[a]) for a in range(n)]

        def remote(sending):
            return [
                pltpu.make_async_remote_copy(
                    src_ref=ins[a].at[2 * chip[0] + chip[1]],
                    dst_ref=land(a, my_chip if sending else 2 * chip[0] + chip[1]),
                    send_sem=send_sems.at[a * 3 + j], recv_sem=recv_sems.at[a * 3 + j],
                    device_id=(*chip, c), device_id_type=MESH)
                for a in range(n) for j, chip in enumerate(chips)
            ]

        return mine, remote

    def start(ins, outs, sems):
        mine, remote = copies(ins, outs, sems)
        for cp in mine() + remote(True):
            cp.start()

    def finish(ins, outs, sems):
        mine, remote = copies(ins, outs, sems)
        for cp in remote(False):
            cp.wait_recv()
        for cp in remote(True):
            cp.wait_send()
        for cp in mine():
            cp.wait()

    return _Plan(
        list(parts) + (list(into) if into is not None else []),
        [jax.ShapeDtypeStruct((N_CHIP, full_rows[a], parts[a].shape[2]), parts[a].dtype) for a in range(n)],
        [pltpu.SemaphoreType.DMA((n * 3,)), pltpu.SemaphoreType.DMA((n * 3,)), pltpu.SemaphoreType.DMA((n,))],
        start, finish,
        aliases={n + a: a for a in range(n)} if into is not None else None,
    )


def _call(body, *, name, operands, out_shape, grid=(), in_specs=None, out_specs=None, scratch_shapes=(),
          plans=(), aliases=None, prefetch=(), start_after_body=False, pass_on_at=0.6):
    operands, out_shape, scratch_shapes = list(operands), list(out_shape), list(scratch_shapes)
    n_pf, n_in, n_out, n_scr = len(prefetch), len(operands), len(out_shape), len(scratch_shapes)
    in_vmem = pl.BlockSpec(memory_space=pltpu.VMEM)
    in_hbm = pl.BlockSpec(memory_space=pl.ANY)
    in_specs = [in_vmem] * n_in if in_specs is None else list(in_specs)
    out_specs = [in_vmem] * n_out if out_specs is None else list(out_specs)
    io_alias = {n_pf + i: o for i, o in (aliases or {}).items()}
    pos_in, pos_out = n_pf + n_in, n_out
    for p in plans:
        for i, o in p.aliases.items():
            io_alias[pos_in + i] = pos_out + o
        pos_in += len(p.ins)
        pos_out += len(p.out_shape)
    steps = math.prod(grid) if grid else 1
    mid_step = min(steps - 1, int(steps * pass_on_at))

    def wrapped(*refs):
        refs = list(refs)
        take = lambda k: [refs.pop(0) for _ in range(k)]
        pf = take(n_pf)
        ins, p_ins = take(n_in), [take(len(p.ins)) for p in plans]
        outs, p_outs = take(n_out), [take(len(p.out_shape)) for p in plans]
        scr, p_sems = take(n_scr), [take(len(p.sems)) for p in plans]
        step = 0
        for ax, g in enumerate(grid):
            step = step * g + pl.program_id(ax)

        def hook(kind, at):
            todo = [(getattr(p, kind), a, b, c) for p, a, b, c in zip(plans, p_ins, p_outs, p_sems) if getattr(p, kind)]

            def run():
                for fn, a, b, c in todo:
                    fn(a, b, c)

            if todo and grid:
                pl.when(step == at)(run)
            elif todo:
                run()

        if not start_after_body:
            hook("start", 0)
        if body is not None:
            body(*pf, *ins, *outs, *scr)
        if start_after_body:
            hook("start", 0)
        hook("mid", mid_step)
        hook("finish", steps - 1)

    layout = dict(
        grid=grid,
        in_specs=in_specs + [in_hbm] * sum(len(p.ins) for p in plans),
        out_specs=out_specs + [in_hbm] * sum(len(p.out_shape) for p in plans),
        scratch_shapes=scratch_shapes + [s for p in plans for s in p.sems],
    )
    if n_pf:
        layout = dict(grid_spec=pltpu.PrefetchScalarGridSpec(num_scalar_prefetch=n_pf, **layout))
    res = pl.pallas_call(
        wrapped,
        name=name,
        out_shape=out_shape + [o for p in plans for o in p.out_shape],
        input_output_aliases=io_alias,
        compiler_params=pltpu.CompilerParams(dimension_semantics=("arbitrary",) * len(grid)) if grid else None,
        **layout,
    )(*prefetch, *operands, *[a for p in plans for a in p.ins])
    res = list(res)
    outs, plan_outs = res[:n_out], []
    pos = n_out
    for p in plans:
        plan_outs.append(res[pos : pos + len(p.out_shape)])
        pos += len(p.out_shape)
    return outs, plan_outs


def _inproj(x, g, w_shard, order, plans=()):
    s = x.shape[0]
    tm = min(TM_INPROJ, s)
    nt = s // tm
    wide = 2 * SHARD_IN
    fwd_at = nt // 2

    def body(order_ref, x_ref, g_ref, w_hbm, proj_ref, h1_ref, h1t_ref, wfull_hbm,
             w_scr, h1_scr, send_sems, recv_sems, local_sems):
        k, i = pl.program_id(0), pl.program_id(1)
        px, py, c = _place()
        me, sibling = (px, py, c), (px, py, 1 - c)
        chips = _other_chips(px, py)

        def half(slot, core):
            return w_scr.at[slot, :, pl.ds(pl.multiple_of(core * SHARD_IN, 128), SHARD_IN)]

        def copy(n, slot, core, to, src=None):
            return pltpu.make_async_remote_copy(
                src_ref=half(slot, core) if src is None else src, dst_ref=half(slot, core),
                send_sem=send_sems.at[n], recv_sem=recv_sems.at[n], device_id=to, device_id_type=MESH)

        def mine():
            return pltpu.make_async_copy(w_hbm, half(0, c), local_sems.at[0])

        def first():
            return [copy(0, 0, c, sibling, src=w_hbm)] + [copy(1 + j, 1 + j, c, (*chips[j], c), src=w_hbm) for j in range(2)]

        def relay():
            to = (c * px + (1 - c) * (1 - px), c * (1 - py) + (1 - c) * py, c)
            return pltpu.make_async_remote_copy(
                src_ref=half(2 - c, c), dst_ref=half(3, c), send_sem=send_sems.at[3], recv_sem=recv_sems.at[3],
                device_id=to, device_id_type=MESH)

        def save(slot):
            cols = pl.ds(pl.multiple_of(order_ref[slot] * wide, 128), wide)
            return pltpu.make_async_copy(w_scr.at[slot], wfull_hbm.at[:, cols], local_sems.at[1 + slot])

        @pl.when((k == 0) & (i == 0))
        def _():
            mine().start()
            for cp in first():
                cp.start()
            mine().wait()
            copy(0, 0, 1 - c, me).wait_recv()

        for j, chip in enumerate(chips):
            @pl.when((k == j) & (i == fwd_at))
            def _():
                copy(1 + j, 1 + j, c, me).wait_recv()
                copy(4 + j, 1 + j, c, sibling).start()
                if j < 2:
                    pl.when(c == 1 - j)(lambda: relay().start())

            @pl.when((k == j + 1) & (i == 0))
            def _():
                copy(4 + j, 1 + j, 1 - c, me).wait_recv()

        @pl.when((k == 3) & (i == 0))
        def _():
            for slot in range(N_CHIP):
                save(slot).start()

        @pl.when(k == 0)
        def _():
            xh, _ = _rms(x_ref[...])
            h = xh * g_ref[...]
            h1_ref[...] = h.astype(BF16)
            h1t_ref[...] = h.T.astype(BF16)
            h1_scr[pl.ds(pl.multiple_of(i * tm, tm), tm), :] = h.astype(BF16)

        proj_ref[...] = _dot(h1_scr[pl.ds(pl.multiple_of(i * tm, tm), tm), :], w_scr[k]).astype(BF16)

        @pl.when((k == 3) & (i == nt - 1))
        def _():
            for cp in first() + [relay()] + [copy(4 + j, 1 + j, c, sibling) for j in range(3)]:
                cp.wait_send()
            for slot in range(N_CHIP):
                save(slot).wait()

    once = lambda k, i, order_ref: (jnp.where(k == 0, i, nt - 1), 0)
    return _call(
        body,
        name="inproj",
        prefetch=[order],
        operands=[x, g, w_shard],
        grid=(N_CHIP, nt),
        in_specs=[
            pl.BlockSpec((tm, D_MODEL), once),
            pl.BlockSpec((1, D_MODEL), lambda k, i, order_ref: (0, 0)),
            pl.BlockSpec(memory_space=pl.ANY),
        ],
        out_specs=[
            pl.BlockSpec((tm, wide), lambda k, i, order_ref: (i, order_ref[k])),
            pl.BlockSpec((tm, D_MODEL), once),
            pl.BlockSpec((D_MODEL, tm), lambda k, i, order_ref: (0, jnp.where(k == 0, i, nt - 1))),
            pl.BlockSpec(memory_space=pl.ANY),
        ],
        out_shape=[
            jax.ShapeDtypeStruct((s, IN_DIM), BF16),
            jax.ShapeDtypeStruct((s, D_MODEL), BF16),
            jax.ShapeDtypeStruct((D_MODEL, s), BF16),
            jax.ShapeDtypeStruct((D_MODEL, IN_DIM), BF16),
        ],
        scratch_shapes=[
            pltpu.VMEM((N_CHIP, D_MODEL, wide), BF16),
            pltpu.VMEM((s, D_MODEL), BF16),
            pltpu.SemaphoreType.DMA((7,)),
            pltpu.SemaphoreType.DMA((7,)),
            pltpu.SemaphoreType.DMA((1 + N_CHIP,)),
        ],
        plans=plans,
        start_after_body=True,
        pass_on_at=0.8,
    )


def _causal_mask():
    row = lax.broadcasted_iota(jnp.int32, (CHUNK, CHUNK), 0)
    col = lax.broadcasted_iota(jnp.int32, (CHUNK, CHUNK), 1)
    return row, row >= col


def _mixer_fwd(proj, conv_w, ln_g, ln_b, ws, bs_b, plans=()):
    s = proj.shape[0]
    nt = s // CHUNK

    def body(proj_ref, cw_ref, lng_ref, lnb_ref, ws_ref, bsb_ref, cat_ref, pcar_ref):
        @pl.when(pl.program_id(0) == 0)
        def _():
            pcar_ref[...] = jnp.zeros_like(pcar_ref)

        row, tril = _causal_mask()
        for j in range(8):
            cs = slice(CHUNK * j, CHUNK * (j + 1))

            def grp(k):
                return proj_ref[:, k * D_MODEL + CHUNK * j : k * D_MODEL + CHUNK * (j + 1)].astype(F32)

            w = cw_ref[:, cs]
            p = grp(1) * grp(2)
            pc = pcar_ref[:, cs]
            p1 = jnp.where(row == 0, pc[7:8], pltpu.roll(p, 1, 0))
            p2 = jnp.where(row == 0, pc[6:7], jnp.where(row == 1, pc[7:8], pltpu.roll(p, 2, 0)))
            pcar_ref[:, cs] = p[CHUNK - 8 :]
            cv = w[0:1] * p2 + w[1:2] * p1 + w[2:3] * p
            sa, _ = _silu(grp(3))
            cat_ref[:, cs] = ((grp(0) * cv) * sa).astype(BF16)
            gu, _ = _gelu(grp(4))
            gv, _ = _gelu(grp(5))
            dv = gv - jnp.mean(gv, axis=-1, keepdims=True)
            vn = dv * lax.rsqrt(jnp.mean(dv * dv, axis=-1, keepdims=True) + EPS)
            vn = vn * lng_ref[:, cs] + lnb_ref[:, cs]
            wc = jnp.where(tril, ws_ref[j], 0.0).astype(BF16)
            sp = _dot(wc, vn.astype(BF16)) + bsb_ref[j]
            sb, _ = _silu(grp(6))
            cat_ref[:, D_MODEL + CHUNK * j : D_MODEL + CHUNK * (j + 1)] = ((gu * sp) * sb).astype(BF16)

    whole = lambda shape: pl.BlockSpec(shape, lambda i: (0,) * len(shape))
    (cat,), plan_outs = _call(
        body,
        name="mixer_fwd",
        operands=[proj, conv_w, ln_g, ln_b, ws, bs_b],
        grid=(nt,),
        in_specs=[
            pl.BlockSpec((CHUNK, IN_DIM), lambda i: (i, 0)),
            whole((8, D_MODEL)),
            whole((1, D_MODEL)),
            whole((1, D_MODEL)),
            whole((8, CHUNK, CHUNK)),
            whole((8, CHUNK, CHUNK)),
        ],
        out_specs=[pl.BlockSpec((CHUNK, MIX_DIM), lambda i: (i, 0))],
        out_shape=[jax.ShapeDtypeStruct((s, MIX_DIM), BF16)],
        scratch_shapes=[pltpu.VMEM((8, D_MODEL), F32)],
        plans=plans,
        pass_on_at=0.85,
    )
    return cat, plan_outs


def _mixer_bwd(proj, dcat, conv_w, ln_g, ln_b, ws, bs_b, plans=()):
    s = proj.shape[0]
    nt = s // CHUNK

    def body(proj_ref, halo_ref, dcat_ref, cw_ref, lng_ref, lnb_ref, ws_ref, bsb_ref,
             dproj_ref, dcw_ref, dlng_ref, dlnb_ref, dws_ref, dbs_ref,
             car_ref, acc_cw, acc_lng, acc_lnb, acc_bs):
        i = pl.program_id(0)
        tile = nt - 1 - i

        @pl.when(i == 0)
        def _():
            car_ref[...] = jnp.zeros_like(car_ref)
            acc_cw[...] = jnp.zeros_like(acc_cw)
            acc_lng[...] = jnp.zeros_like(acc_lng)
            acc_lnb[...] = jnp.zeros_like(acc_lnb)
            acc_bs[...] = jnp.zeros_like(acc_bs)
            dws_ref[...] = jnp.zeros_like(dws_ref)

        row, tril = _causal_mask()
        has_prev = jnp.where(tile > 0, 1.0, 0.0).astype(F32)
        for j in range(8):
            cs = slice(CHUNK * j, CHUNK * (j + 1))

            def col(k):
                return slice(k * D_MODEL + CHUNK * j, k * D_MODEL + CHUNK * (j + 1))

            def grp(k):
                return proj_ref[:, col(k)].astype(F32)

            w = cw_ref[:, cs]
            gb, gc, xa = grp(0), grp(1), grp(2)
            p = gc * xa
            pprev = halo_ref[:, col(1)].astype(F32) * halo_ref[:, col(2)].astype(F32) * has_prev
            p1 = jnp.where(row == 0, pprev[15:16], pltpu.roll(p, 1, 0))
            p2 = jnp.where(row == 0, pprev[14:15], jnp.where(row == 1, pprev[15:16], pltpu.roll(p, 2, 0)))
            cv = w[0:1] * p2 + w[1:2] * p1 + w[2:3] * p
            za = grp(3)
            sa, dsa = _silu(za)
            da = dcat_ref[:, cs].astype(F32)
            dproj_ref[:, col(0)] = (da * cv * sa).astype(BF16)
            dproj_ref[:, col(3)] = (da * gb * cv * dsa).astype(BF16)
            dcv = da * gb * sa
            nxt = car_ref[:, cs]
            d1 = jnp.where(row == CHUNK - 1, nxt[0:1], pltpu.roll(dcv, CHUNK - 1, 0))
            d2 = jnp.where(row == CHUNK - 2, nxt[0:1], jnp.where(row == CHUNK - 1, nxt[1:2], pltpu.roll(dcv, CHUNK - 2, 0)))
            car_ref[:, cs] = dcv[0:8]
            dp = w[2:3] * dcv + w[1:2] * d1 + w[0:1] * d2
            dproj_ref[:, col(1)] = (dp * xa).astype(BF16)
            dproj_ref[:, col(2)] = (dp * gc).astype(BF16)
            acc_cw[0, :, cs] += _rows8(dcv * p2)
            acc_cw[1, :, cs] += _rows8(dcv * p1)
            acc_cw[2, :, cs] += _rows8(dcv * p)
            gu, dgu = _gelu(grp(4))
            gv, dgv = _gelu(grp(5))
            dv = gv - jnp.mean(gv, axis=-1, keepdims=True)
            rstd = lax.rsqrt(jnp.mean(dv * dv, axis=-1, keepdims=True) + EPS)
            vnh = dv * rstd
            lng = lng_ref[:, cs]
            vnb = (vnh * lng + lnb_ref[:, cs]).astype(BF16)
            wcf = jnp.where(tril, ws_ref[j], 0.0)
            sp = _dot(wcf.astype(BF16), vnb) + bsb_ref[j]
            zb = grp(6)
            sb, dsb = _silu(zb)
            db = dcat_ref[:, D_MODEL + CHUNK * j : D_MODEL + CHUNK * (j + 1)].astype(F32)
            dproj_ref[:, col(4)] = (db * sp * sb * dgu).astype(BF16)
            dproj_ref[:, col(6)] = (db * gu * sp * dsb).astype(BF16)
            dsp = db * gu * sb
            acc_bs[j] += dsp
            dspb = dsp.astype(BF16)
            dws_ref[j] += _dot_nt(dspb, vnb)
            dvn = _dot(wcf.T.astype(BF16), dspb)
            acc_lnb[:, cs] += _rows8(dvn)
            acc_lng[:, cs] += _rows8(dvn * vnh)
            dvh = dvn * lng
            dgvv = rstd * (dvh - jnp.mean(dvh, axis=-1, keepdims=True) - vnh * jnp.mean(dvh * vnh, axis=-1, keepdims=True))
            dproj_ref[:, col(5)] = (dgvv * dgv).astype(BF16)

        @pl.when(i == nt - 1)
        def _():
            dlng_ref[...] = jnp.sum(acc_lng[...], axis=0, keepdims=True)
            dlnb_ref[...] = jnp.sum(acc_lnb[...], axis=0, keepdims=True)
            dcw_ref[...] = jnp.zeros_like(dcw_ref)
            for k in range(3):
                dcw_ref[k : k + 1, :] = jnp.sum(acc_cw[k], axis=0, keepdims=True)
            for j in range(8):
                dws_ref[j] = jnp.where(tril, dws_ref[j], 0.0)
                dbs_ref[:, CHUNK * j : CHUNK * (j + 1)] = jnp.sum(acc_bs[j].T, axis=0, keepdims=True)

    whole = lambda shape: pl.BlockSpec(shape, lambda i: (0,) * len(shape))
    halo_rows = 16
    per = CHUNK // halo_rows
    return _call(
        body,
        name="mixer_bwd",
        operands=[proj, proj, dcat, conv_w, ln_g, ln_b, ws, bs_b],
        plans=plans,
        grid=(nt,),
        in_specs=[
            pl.BlockSpec((CHUNK, IN_DIM), lambda i: (nt - 1 - i, 0)),
            pl.BlockSpec((halo_rows, IN_DIM), lambda i: (jnp.maximum((nt - 1 - i) * per - 1, 0), 0)),
            pl.BlockSpec((CHUNK, MIX_DIM), lambda i: (nt - 1 - i, 0)),
            whole((8, D_MODEL)),
            whole((1, D_MODEL)),
            whole((1, D_MODEL)),
            whole((8, CHUNK, CHUNK)),
            whole((8, CHUNK, CHUNK)),
        ],
        out_specs=[
            pl.BlockSpec((CHUNK, IN_DIM), lambda i: (nt - 1 - i, 0)),
            whole((8, D_MODEL)),
            whole((1, D_MODEL)),
            whole((1, D_MODEL)),
            whole((8, CHUNK, CHUNK)),
            whole((1, D_MODEL)),
        ],
        out_shape=[
            jax.ShapeDtypeStruct((s, IN_DIM), BF16),
            jax.ShapeDtypeStruct((8, D_MODEL), F32),
            jax.ShapeDtypeStruct((1, D_MODEL), F32),
            jax.ShapeDtypeStruct((1, D_MODEL), F32),
            jax.ShapeDtypeStruct((8, CHUNK, CHUNK), F32),
            jax.ShapeDtypeStruct((1, D_MODEL), F32),
        ],
        scratch_shapes=[
            pltpu.VMEM((8, D_MODEL), F32),
            pltpu.VMEM((3, 8, D_MODEL), F32),
            pltpu.VMEM((8, D_MODEL), F32),
            pltpu.VMEM((8, D_MODEL), F32),
            pltpu.VMEM((8, CHUNK, CHUNK), F32),
        ],
    )


def _kv_fwd(mem, g_mem, w_kv):
    def body(mem_ref, g_ref, w_ref, k_ref, vv_ref, mt_ref):
        mh, _ = _rms(mem_ref[...])
        m = mh * g_ref[...]
        kv = _dot(m.astype(BF16), w_ref[...])
        k_ref[...] = kv[:, :D_MODEL].astype(BF16)
        vv_ref[...] = kv[:, D_MODEL:].astype(BF16)
        mt_ref[...] = m.T.astype(BF16)

    return pl.pallas_call(
        body,
        name="kv_fwd",
        out_shape=[
            jax.ShapeDtypeStruct((MEM_LEN, D_MODEL), BF16),
            jax.ShapeDtypeStruct((MEM_LEN, D_MODEL), BF16),
            jax.ShapeDtypeStruct((D_MODEL, MEM_LEN), BF16),
        ],
    )(mem, g_mem, w_kv)


def _kv_bwd(mem, w_kv, mt, dk, dvv):
    def body(mem_ref, w_ref, mt_ref, dk_ref, dvv_ref, dw_ref, dg_ref):
        dkv = jnp.concatenate([dk_ref[...], dvv_ref[...]], axis=1).astype(BF16)
        dw_ref[...] = _dot(mt_ref[...], dkv)
        dm = _dot_nt(dkv, w_ref[...])
        mh, _ = _rms(mem_ref[...])
        dg_ref[...] = jnp.sum(dm * mh, axis=0, keepdims=True)

    return pl.pallas_call(
        body,
        name="kv_bwd",
        out_shape=[
            jax.ShapeDtypeStruct((D_MODEL, 2 * D_MODEL), F32),
            jax.ShapeDtypeStruct((1, D_MODEL), F32),
        ],
    )(mem, w_kv, mt, dk, dvv)


def _attn(x, cat, target, w_out, w_q, w_xo, k, vv, g_x, g_f):
    s = x.shape[0]
    tm = min(TM_ATTN, s)
    sub = min(SUB_ATTN, tm)
    nt = s // tm
    scale = 1.0 / math.sqrt(X_HEAD_DIM)

    def body(x_ref, cat_ref, t_ref, wout_ref, wq_ref, wxo_ref, k_ref, vv_ref, gx_ref, gf_ref,
             dcat_ref, catt_ref, h2t_ref, ot_ref, dx2b_ref, dqb_ref, dx1b_ref,
             dk_ref, dvv_ref, dgf_ref, dgx_ref, loss_ref,
             p_scr, q_scr, o_scr, dq_scr, acc_gf, acc_gx, acc_loss):
        i = pl.program_id(0)

        @pl.when(i == 0)
        def _():
            dk_ref[...] = jnp.zeros_like(dk_ref)
            dvv_ref[...] = jnp.zeros_like(dvv_ref)
            acc_gf[...] = jnp.zeros_like(acc_gf)
            acc_gx[...] = jnp.zeros_like(acc_gx)
            acc_loss[...] = jnp.zeros_like(acc_loss)

        subs = [slice(u * sub, (u + 1) * sub) for u in range(tm // sub)]
        heads = [slice(X_HEAD_DIM * h, X_HEAD_DIM * (h + 1)) for h in range(X_HEADS)]
        gx, gf = gx_ref[...], gf_ref[...]
        x1, x1h, r2 = [], [], []
        for rs in subs:
            cat = cat_ref[rs, :]
            catt_ref[:, rs] = cat.astype(F32).T.astype(BF16)
            v = x_ref[rs, :] + _dot(cat, wout_ref[...])
            vh, r = _rms(v)
            h2 = vh * gx
            h2t_ref[:, rs] = h2.T.astype(BF16)
            q_scr[rs, :] = _dot(h2.astype(BF16), wq_ref[...]).astype(BF16)
            x1.append(v), x1h.append(vh), r2.append(r)
        for rs in subs:
            for hs in heads:
                sc = _dot_nt(q_scr[rs, hs], k_ref[:, hs]) * scale
                e = jnp.exp(sc - jnp.max(sc, axis=-1, keepdims=True))
                p = e / jnp.sum(e, axis=-1, keepdims=True)
                p_scr[rs, hs] = p
                o_scr[rs, hs] = _dot(p.astype(BF16), vv_ref[:, hs])
        dx2, sq, d_gf = [], None, None
        for u, rs in enumerate(subs):
            o = o_scr[rs, :]
            ot_ref[:, rs] = o.T.astype(BF16)
            x2 = x1[u] + _dot(o.astype(BF16), wxo_ref[...])
            x2h, r3 = _rms(x2)
            err = x2h * gf - t_ref[rs, :]
            dy = err * (1.0 / D_MODEL)
            sq = _rows8(err * err) if sq is None else sq + _rows8(err * err)
            d_gf = _rows8(dy * x2h) if d_gf is None else d_gf + _rows8(dy * x2h)
            dx2.append(_rms_bwd(dy * gf, x2h, r3))
        acc_loss[...] += sq
        acc_gf[...] += d_gf
        d_vv, d_k = [None] * X_HEADS, [None] * X_HEADS
        for u, rs in enumerate(subs):
            dx2b = dx2[u].astype(BF16)
            dx2b_ref[rs, :] = dx2b
            dob = _dot_nt(dx2b, wxo_ref[...]).astype(BF16)
            for h, hs in enumerate(heads):
                p = p_scr[rs, hs]
                dp = _dot_nt(dob[:, hs], vv_ref[:, hs])
                ds = p * (dp - jnp.sum(dp * p, axis=-1, keepdims=True)) * scale
                dq_scr[rs, hs] = _dot(ds.astype(BF16), k_ref[:, hs]).astype(BF16)
                dvv_h = _dot(p.T.astype(BF16), dob[:, hs])
                dk_h = _dot(ds.T.astype(BF16), q_scr[rs, hs])
                d_vv[h] = dvv_h if d_vv[h] is None else d_vv[h] + dvv_h
                d_k[h] = dk_h if d_k[h] is None else d_k[h] + dk_h
        for h, hs in enumerate(heads):
            dvv_ref[:, hs] += d_vv[h]
            dk_ref[:, hs] += d_k[h]
        d_gx = None
        for u, rs in enumerate(subs):
            dqb = dq_scr[rs, :]
            dqb_ref[rs, :] = dqb
            dh2 = _dot_nt(dqb, wq_ref[...])
            d_gx = _rows8(dh2 * x1h[u]) if d_gx is None else d_gx + _rows8(dh2 * x1h[u])
            dx1 = dx2[u] + _rms_bwd(dh2 * gx, x1h[u], r2[u])
            dx1b = dx1.astype(BF16)
            dx1b_ref[rs, :] = dx1b
            dcat_ref[rs, :] = _dot_nt(dx1b, wout_ref[...]).astype(BF16)
        acc_gx[...] += d_gx

        @pl.when(i == nt - 1)
        def _():
            dgf_ref[...] = jnp.sum(acc_gf[...], axis=0, keepdims=True)
            dgx_ref[...] = jnp.sum(acc_gx[...], axis=0, keepdims=True)
            loss_ref[...] = jnp.sum(acc_loss[...], axis=0, keepdims=True)

    whole = lambda shape: pl.BlockSpec(shape, lambda i: (0,) * len(shape))
    once = lambda shape: pl.BlockSpec(shape, lambda i: (0,) * len(shape), pipeline_mode=pl.Buffered(1))
    rows = lambda width: pl.BlockSpec((tm, width), lambda i: (i, 0))
    cols = lambda height: pl.BlockSpec((height, tm), lambda i: (0, i))
    vec = jax.ShapeDtypeStruct((1, D_MODEL), F32)
    return pl.pallas_call(
        body,
        name="attn",
        grid=(nt,),
        in_specs=[
            rows(D_MODEL), rows(MIX_DIM), rows(D_MODEL),
            once((MIX_DIM, D_MODEL)), once((D_MODEL, D_MODEL)), once((D_MODEL, D_MODEL)),
            once((MEM_LEN, D_MODEL)), once((MEM_LEN, D_MODEL)),
            whole((1, D_MODEL)), whole((1, D_MODEL)),
        ],
        out_specs=[
            rows(MIX_DIM), cols(MIX_DIM), cols(D_MODEL), cols(D_MODEL),
            rows(D_MODEL), rows(D_MODEL), rows(D_MODEL),
            whole((MEM_LEN, D_MODEL)), whole((MEM_LEN, D_MODEL)),
            whole((1, D_MODEL)), whole((1, D_MODEL)), whole((1, D_MODEL)),
        ],
        out_shape=[
            jax.ShapeDtypeStruct((s, MIX_DIM), BF16),
            jax.ShapeDtypeStruct((MIX_DIM, s), BF16),
            jax.ShapeDtypeStruct((D_MODEL, s), BF16),
            jax.ShapeDtypeStruct((D_MODEL, s), BF16),
            jax.ShapeDtypeStruct((s, D_MODEL), BF16),
            jax.ShapeDtypeStruct((s, D_MODEL), BF16),
            jax.ShapeDtypeStruct((s, D_MODEL), BF16),
            jax.ShapeDtypeStruct((MEM_LEN, D_MODEL), F32),
            jax.ShapeDtypeStruct((MEM_LEN, D_MODEL), F32),
            vec, vec, vec,
        ],
        scratch_shapes=[
            pltpu.VMEM((tm, D_MODEL), F32),
            pltpu.VMEM((tm, D_MODEL), BF16),
            pltpu.VMEM((tm, D_MODEL), F32),
            pltpu.VMEM((tm, D_MODEL), BF16),
            pltpu.VMEM((8, D_MODEL), F32),
            pltpu.VMEM((8, D_MODEL), F32),
            pltpu.VMEM((8, D_MODEL), F32),
        ],
        compiler_params=pltpu.CompilerParams(dimension_semantics=("arbitrary",)),
    )(x, cat, target, w_out, w_q, w_xo, k, vv, g_x, g_f)


def _matmul_f32(name, a, b, rows=None, plans=()):
    first, m = (0, a.shape[0]) if rows is None else rows
    kk = a.shape[1]
    n = b.shape[1]
    tm, tn, tk = min(m, TM_DW), min(n, D_MODEL), min(kk, TK_DW)
    nk = kk // tk
    i0 = first // tm

    def body(a_ref, b_ref, o_ref):
        prod = _dot(a_ref[...], b_ref[...])
        if nk == 1:
            o_ref[...] = prod
            return
        kid = pl.program_id(2)

        @pl.when(kid == 0)
        def _():
            o_ref[...] = prod

        @pl.when(kid > 0)
        def _():
            o_ref[...] += prod

    (out,), plan_outs = _call(
        body,
        name=name,
        operands=[a, b],
        grid=(m // tm, n // tn, nk),
        in_specs=[pl.BlockSpec((tm, tk), lambda i, j, k: (i + i0, k)), pl.BlockSpec((tk, tn), lambda i, j, k: (k, j))],
        out_specs=[pl.BlockSpec((tm, tn), lambda i, j, k: (i, j))],
        out_shape=[jax.ShapeDtypeStruct((m, n), F32)],
        plans=plans,
    )
    return out, plan_outs


def _dh1(name, dproj, w_in, x, dx1, g, tiles, into=None, after=(), plans=()):
    s = x.shape[0]
    tm = min(TM_DH1, s)
    t0, nt = tiles[0], tiles[1] - tiles[0]

    def body(dp_ref, w_ref, x_ref, dx1_ref, g_ref, *rest):
        gx_ref, dg_ref, acc_g = rest[-3:]
        i = pl.program_id(0)

        @pl.when(i == 0)
        def _():
            acc_g[...] = jnp.zeros_like(acc_g)

        dh1 = _dot_nt(dp_ref[...], w_ref[...])
        xh, r = _rms(x_ref[...])
        acc_g[...] += _rows8(dh1 * xh)
        gx_ref[...] = dx1_ref[...].astype(F32) + _rms_bwd(dh1 * g_ref[...], xh, r)

        @pl.when(i == nt - 1)
        def _():
            dg_ref[...] = jnp.sum(acc_g[...], axis=0, keepdims=True)

    rows = pl.BlockSpec((tm, D_MODEL), lambda i: (i + t0, 0))
    return _call(
        body,
        name=name,
        operands=[dproj, w_in, x, dx1, g] + ([into] if into is not None else []) + list(after),
        grid=(nt,),
        in_specs=[
            pl.BlockSpec((tm, IN_DIM), lambda i: (i + t0, 0)),
            pl.BlockSpec((D_MODEL, IN_DIM), lambda i: (0, 0), pipeline_mode=pl.Buffered(1)),
            rows,
            rows,
            pl.BlockSpec((1, D_MODEL), lambda i: (0, 0)),
        ] + [pl.BlockSpec(memory_space=pl.ANY)] * ((into is not None) + len(after)),
        out_specs=[rows, pl.BlockSpec((1, D_MODEL), lambda i: (0, 0))],
        out_shape=[jax.ShapeDtypeStruct((s, D_MODEL), F32), jax.ShapeDtypeStruct((1, D_MODEL), F32)],
        scratch_shapes=[pltpu.VMEM((8, D_MODEL), F32)],
        plans=plans,
        aliases={5: 0} if into is not None else None,
    )


def _pair_sum(name, grad, got, axis, size, core, plans=()):
    shard = list(grad.shape)
    shard[axis] = size
    r, cdim = shard
    tr = min(r, TR_PAIR_SUM)
    nr = r // tr

    def body(core_ref, g_ref, got_ref, o_ref):
        o_ref[0] = (g_ref[...] + got_ref[0]).astype(BF16)

    if axis == 1:
        g_map = lambda q, t, core_ref: (t, 2 * q + core_ref[0])
    else:
        g_map = lambda q, t, core_ref: ((2 * q + core_ref[0]) * nr + t, 0)
    (out,), plan_outs = _call(
        body,
        name=name,
        prefetch=[core],
        operands=[grad, got],
        grid=(N_CHIP, nr),
        in_specs=[
            pl.BlockSpec((tr, cdim), g_map),
            pl.BlockSpec((1, tr, cdim), lambda q, t, core_ref: (q, t, 0)),
        ],
        out_specs=[pl.BlockSpec((1, tr, cdim), lambda q, t, core_ref: (q, t, 0))],
        out_shape=[jax.ShapeDtypeStruct((N_CHIP, r, cdim), BF16)],
        plans=plans,
    )
    return (out, plan_outs) if plans else out


def _adam_step(w, m, v, g):
    c1 = 1.0 / (1.0 - ADAM_B1**ADAM_STEP)
    c2 = 1.0 / (1.0 - ADAM_B2**ADAM_STEP)
    m_new = ADAM_B1 * m + (1.0 - ADAM_B1) * g
    v_new = ADAM_B2 * v + (1.0 - ADAM_B2) * (g * g)
    return -ADAM_LR * ((m_new * c1) / (jnp.sqrt(v_new * c2) + ADAM_EPS) + ADAM_WD * w), m_new, v_new


def _update_small(dev, vec_w, vec_m, vec_v, conv_w, conv_m, conv_v, small, landed):
    n_vec = len(vec_w)

    def body(dev_ref, *refs):
        refs = list(refs)
        take = lambda k: [refs.pop(0) for _ in range(k)]
        w_refs, m_refs, v_refs = take(n_vec), take(n_vec), take(n_vec)
        cw_ref, cm_ref, cv_ref, small_ref, land_ref, small_conv_ref, land_conv_ref = take(7)
        vec_outs, conv_outs, (loss_ref,) = take(4 * n_vec), take(4), take(1)

        def total(own_ref, others_ref):
            acc = None
            for d in range(N_DEV):
                part = jnp.where(dev_ref[0] == d, own_ref[0], others_ref[d])
                acc = part if acc is None else acc + part
            return acc

        g_all = total(small_ref, land_ref)
        for i in range(n_vec):
            g = g_all[i : i + 1]
            delta, m_new, v_new = _adam_step(w_refs[i][...], m_refs[i][...], v_refs[i][...], g)
            for ref, val in zip(vec_outs[4 * i : 4 * i + 4], (g, delta, m_new, v_new)):
                ref[...] = val
        g = total(small_conv_ref, land_conv_ref)[0:3]
        delta, m_new, v_new = _adam_step(cw_ref[0], cm_ref[0], cv_ref[0], g)
        for ref, val in zip(conv_outs, (g, delta, m_new, v_new)):
            ref[0] = val
        loss_ref[...] = jnp.broadcast_to(0.5 * jnp.sum(g_all[7:8], axis=-1, keepdims=True) / D_MODEL, loss_ref.shape)

    vec_spec = pl.BlockSpec((1, D_MODEL), lambda i, dev_ref: (0, 0))
    conv_spec = pl.BlockSpec((1, 3, CHUNK), lambda i, dev_ref: (0, 0, 0))
    vec_shape = jax.ShapeDtypeStruct((1, D_MODEL), F32)
    conv_shape = jax.ShapeDtypeStruct((1, 3, CHUNK), F32)
    outs, _ = _call(
        body,
        name="update_small",
        prefetch=[dev],
        operands=list(vec_w) + list(vec_m) + list(vec_v) + [conv_w, conv_m, conv_v, small, landed, small, landed],
        grid=(1,),
        in_specs=[vec_spec] * (3 * n_vec) + [conv_spec] * 3 + [
            pl.BlockSpec((1, 16, D_MODEL), lambda i, dev_ref: (0, 0, 0)),
            pl.BlockSpec((N_DEV, 16, D_MODEL), lambda i, dev_ref: (0, 0, 0)),
            pl.BlockSpec((1, 8, CHUNK), lambda i, dev_ref: (0, 1, dev_ref[0])),
            pl.BlockSpec((N_DEV, 8, CHUNK), lambda i, dev_ref: (0, 1, dev_ref[0])),
        ],
        out_specs=[vec_spec] * (4 * n_vec) + [conv_spec] * 4 + [pl.BlockSpec((1, CHUNK), lambda i, dev_ref: (0, 0))],
        out_shape=[vec_shape] * (4 * n_vec) + [conv_shape] * 4 + [jax.ShapeDtypeStruct((1, CHUNK), F32)],
    )
    return [outs[4 * i : 4 * i + 4] for i in range(n_vec)], outs[4 * n_vec : 4 * n_vec + 4], outs[-1]


def _adamw(name, groups, steps, own_slot=None, plans=()):
    lists = lambda p: list(p) if isinstance(p, (list, tuple)) else [p]
    groups = [(g[0], g[1], g[2], lists(g[3]), lists(g[4]) if len(g) > 4 else []) for g in groups]
    counts = [3 + len(g[3]) + len(g[4]) for g in groups]
    use_slot = own_slot is not None

    def first_tiles(w, parts):
        firsts, first = [], 0
        for p in parts:
            firsts.append(first)
            first += p.shape[1] // (w.shape[0] // steps)
        return firsts

    def update(t, slot_ref, firsts, w_ref, m_ref, v_ref, p_refs, o_refs, g_ref, d_ref, nm_ref, nv_ref):
        g = None
        for k, p_ref in enumerate(p_refs):
            gk = None
            for q in range(p_ref.shape[0]):
                part = p_ref[q].astype(F32)
                if o_refs:
                    part = jnp.where(slot_ref[0] == q, o_refs[k][0].astype(F32), part)
                gk = part if gk is None else gk + part
            g = gk if g is None else jnp.where(t >= firsts[k], gk, g)
        g_ref[...] = g
        d_ref[...], nm_ref[...], nv_ref[...] = _adam_step(w_ref[...], m_ref[...], v_ref[...], g)

    def body(*refs):
        refs = list(refs)
        slot_ref = refs.pop(0) if use_slot else None
        t = pl.program_id(0)
        ins, outs = refs[: sum(counts)], refs[sum(counts) :]
        for i, (w, _, _, parts, own) in enumerate(groups):
            mine = ins[sum(counts[:i]) : sum(counts[: i + 1])]
            update(t, slot_ref, first_tiles(w, parts), *mine[:3], mine[3 : 3 + len(parts)], mine[3 + len(parts) :],
                   *outs[4 * i : 4 * i + 4])

    operands, in_specs, out_specs, out_shape = [], [], [], []
    for w, m, v, parts, own in groups:
        r, cdim = w.shape
        tr = r // steps
        firsts = first_tiles(w, parts)

        def tile_of(k, t, firsts=firsts, parts=parts, tr=tr):
            return jnp.clip(t - firsts[k], 0, parts[k].shape[1] // tr - 1)

        blk = pl.BlockSpec((tr, cdim), lambda t, *_: (t, 0))
        operands += [w, m, v] + parts + own
        in_specs += [blk, blk, blk]
        in_specs += [pl.BlockSpec((p.shape[0], tr, cdim), lambda t, *_, k=k, tile_of=tile_of: (0, tile_of(k, t), 0)) for k, p in enumerate(parts)]
        in_specs += [pl.BlockSpec((1, tr, cdim), lambda t, slot_ref, k=k, tile_of=tile_of: (slot_ref[0], tile_of(k, t), 0)) for k in range(len(own))]
        out_specs += [blk] * 4
        out_shape += [jax.ShapeDtypeStruct((r, cdim), F32)] * 4
    outs, plan_outs = _call(
        body,
        name=name,
        prefetch=[own_slot] if use_slot else [],
        operands=operands,
        grid=(steps,),
        in_specs=in_specs,
        out_specs=out_specs,
        out_shape=out_shape,
        plans=plans,
    )
    return [outs[4 * i : 4 * i + 4] for i in range(len(groups))], plan_outs


def _chip_routes(n):
    x, y, c = _place()
    my_chip = 2 * x + y
    return [(a, (*chip, c), 2 * chip[0] + chip[1], my_chip, 2 * chip[0] + chip[1])
            for a in range(n) for chip in _other_chips(x, y)]


def _gather_routes(n):
    x, y, c = _place()
    flip = lambda v, f: 1 - v if f else v
    peers = [(flip(x, fx), flip(y, fy), flip(c, fc)) for fx in (0, 1) for fy in (0, 1) for fc in (0, 1) if fx + fy + fc]
    return [(0, p, 0, 4 * x + 2 * y + c, 4 * p[0] + 2 * p[1] + p[2]) for p in peers]


_chip_routes.copies_per_array = N_CHIP - 1
_gather_routes.copies_per_array = N_DEV - 1


def _split_start(name, routes, srcs, land_shapes):
    n = len(srcs)
    hbm = pl.BlockSpec(memory_space=pltpu.HBM)
    sem = pl.BlockSpec(memory_space=pltpu.SEMAPHORE)

    def body(*refs):
        src_refs, land_refs = refs[:n], refs[n : 2 * n]
        send_sems, recv_sems = refs[2 * n], refs[2 * n + 1]
        token = refs[-1]
        for k, (a, peer, src_slot, there, _) in enumerate(routes(n)):
            pltpu.make_async_remote_copy(
                src_ref=src_refs[a].at[src_slot], dst_ref=land_refs[a].at[there],
                send_sem=send_sems.at[k], recv_sem=recv_sems.at[k], device_id=peer, device_id_type=MESH).start()
        token[...] = jnp.zeros_like(token)

    in_hbm = [pltpu.with_memory_space_constraint(p, pltpu.HBM) for p in srcs]
    in_hbm += [pltpu.with_memory_space_constraint(lax.empty(shape, p.dtype), pltpu.HBM) for p, shape in zip(srcs, land_shapes)]
    n_copies = routes.copies_per_array * n
    res = pl.pallas_call(
        body,
        name=name,
        out_shape=[pltpu.SemaphoreType.DMA((n_copies,)), pltpu.SemaphoreType.DMA((n_copies,))]
        + [pltpu.HBM(p.shape, p.dtype) for p in srcs] + [pltpu.HBM(shape, p.dtype) for p, shape in zip(srcs, land_shapes)]
        + [jax.ShapeDtypeStruct((8, CHUNK), F32)],
        in_specs=[hbm] * (2 * n),
        out_specs=[sem, sem] + [hbm] * (2 * n) + [pl.BlockSpec(memory_space=pltpu.VMEM)],
        input_output_aliases={i: 2 + i for i in range(2 * n)},
        compiler_params=pltpu.CompilerParams(has_side_effects=pltpu.SideEffectType.DATAFLOW_SIDE_EFFECTING),
    )(*in_hbm)
    return res[:-1], res[-1]


def _split_wait(name, routes, in_flight, after):
    n = (len(in_flight) - 2) // 2
    hbm = pl.BlockSpec(memory_space=pltpu.HBM)
    sem = pl.BlockSpec(memory_space=pltpu.SEMAPHORE)
    send_sems, recv_sems, *bufs = in_flight

    def body(*refs):
        src_refs, land_refs = refs[:n], refs[n : 2 * n]
        send_ref, recv_ref = refs[2 * n], refs[2 * n + 1]
        for k, (a, peer, src_slot, _, here) in enumerate(routes(n)):
            cp = pltpu.make_async_remote_copy(
                src_ref=src_refs[a].at[src_slot], dst_ref=land_refs[a].at[here],
                send_sem=send_ref.at[k], recv_sem=recv_ref.at[k], device_id=peer, device_id_type=MESH)
            cp.wait_send()
            cp.wait_recv()

    res = pl.pallas_call(
        body,
        name=name,
        out_shape=[pltpu.HBM(b.shape, b.dtype) for b in bufs],
        in_specs=[hbm] * (2 * n) + [sem, sem] + [pl.BlockSpec(memory_space=pl.ANY)] * len(after),
        out_specs=[hbm] * (2 * n),
        input_output_aliases={i: i for i in range(2 * n)},
        compiler_params=pltpu.CompilerParams(has_side_effects=pltpu.SideEffectType.DATAFLOW_SIDE_EFFECTING),
    )(*bufs, send_sems, recv_sems, *after)
    return res[:n], res[n:]


def kernel(x, mem, norm_mix_g, w_in, conv_w, gm_ln_g, gm_ln_b, gm_ws, gm_bs, w_out, norm_x_g, norm_mem_g, w_q, w_kv, w_xo, norm_final_g, loss_target, m_norm_mix_g, m_w_in, m_conv_w, m_gm_ln_g, m_gm_ln_b, m_gm_ws, m_gm_bs, m_w_out, m_norm_x_g, m_norm_mem_g, m_w_q, m_w_kv, m_w_xo, m_norm_final_g, v_norm_mix_g, v_w_in, v_conv_w, v_gm_ln_g, v_gm_ln_b, v_gm_ws, v_gm_bs, v_w_out, v_norm_x_g, v_norm_mem_g, v_w_q, v_w_kv, v_w_xo, v_norm_final_g):
    s = x.shape[1]
    dev = 4 * lax.axis_index("x") + 2 * lax.axis_index("y") + lax.axis_index("c")
    core = lax.axis_index("c").astype(jnp.int32).reshape(1)
    x2 = x[0]
    target = loss_target[0]
    pad_taps = lambda t: jnp.pad(t[0], ((0, 5), (0, 0)))

    big_names = ["w_in", "w_out", "w_q", "w_kv", "w_xo"]
    big_w = dict(w_in=w_in[0], w_out=w_out[0], w_q=w_q[0], w_kv=w_kv[0], w_xo=w_xo[0])
    big_m = dict(w_in=m_w_in[0], w_out=m_w_out[0], w_q=m_w_q[0], w_kv=m_w_kv[0], w_xo=m_w_xo[0])
    big_v = dict(w_in=v_w_in[0], w_out=v_w_out[0], w_q=v_w_q[0], w_kv=v_w_kv[0], w_xo=v_w_xo[0])
    shard_axis = dict(w_in=1, w_out=0, w_q=0, w_kv=1, w_xo=0)
    full_shape = dict(w_in=(D_MODEL, IN_DIM), w_out=(MIX_DIM, D_MODEL), w_q=(D_MODEL, D_MODEL),
                      w_kv=(D_MODEL, 2 * D_MODEL), w_xo=(D_MODEL, D_MODEL))
    shard_size = {k: big_w[k].shape[shard_axis[k]] for k in big_names}
    others = big_names[1:]
    gather_spec = lambda k: (full_shape[k], shard_axis[k], shard_size[k])
    reduce_spec = lambda k: (shard_axis[k], shard_size[k])
    ws = gm_ws[0]
    bs_b = jnp.broadcast_to(gm_bs[0][:, :, None], (8, CHUNK, CHUNK))
    g_f = norm_final_g.reshape(1, D_MODEL)

    px, py = lax.axis_index("x"), lax.axis_index("y")
    chip_order = jnp.stack([2 * px + py, 2 * (1 - px) + py, 2 * px + 1 - py, 2 * (1 - px) + 1 - py]).astype(jnp.int32)
    (proj, h1, h1t, w_in_full), ((w_out_full, conv_g),) = _inproj(
        x2, norm_mix_g, big_w["w_in"].astype(BF16), chip_order,
        plans=[_gather_plan([big_w["w_out"].astype(BF16), pad_taps(conv_w)[None]], [gather_spec("w_out"), ((N_DEV, 8, CHUNK), 0, 1)])])
    conv_full = conv_g.transpose(1, 0, 2).reshape(8, D_MODEL)
    attn_w = ["w_q", "w_kv", "w_xo"]
    cat, (gathered,) = _mixer_fwd(
        proj, conv_full, gm_ln_g, gm_ln_b, ws, bs_b,
        plans=[_gather_plan([big_w[k].astype(BF16) for k in attn_w], [gather_spec(k) for k in attn_w])])
    wf = dict(zip(attn_w, gathered), w_out=w_out_full)
    k, vv, mt = _kv_fwd(mem[0], norm_mem_g, wf["w_kv"])
    (dcat, catt, h2t, ot, dx2b, dqb, dx1b, dk, dvv, d_gf, d_gx, sq_err) = _attn(
        x2, cat, target, wf["w_out"], wf["w_q"], wf["w_xo"], k, vv, norm_x_g, g_f)

    def pair_sums(names, grads, got):
        return [_pair_sum("pair_sum_" + k, grads[k], got[i], *reduce_spec(k), core) for i, k in enumerate(names)]

    sibling_plan = lambda names, grads: _sibling_plan([grads[k] for k in names], [reduce_spec(k) for k in names])
    early, late = ["w_out", "w_kv", "w_q"], ["w_xo"]
    big_g = dict(w_out=_matmul_f32("dw_out", catt, dx1b)[0])
    big_g["w_kv"], d_gmem = _kv_bwd(mem[0], wf["w_kv"], mt, dk, dvv)
    big_g["w_q"], (got_a,) = _matmul_f32("dw_q", h2t, dqb, plans=[sibling_plan(early[:2], big_g)])
    big_g["w_xo"], (got_b,) = _matmul_f32("dw_xo", ot, dx2b, plans=[sibling_plan(early[2:], big_g)])
    got_early = got_a + got_b
    (dproj, d_cw, d_lng, d_lnb, d_ws, d_bs), (parts_early, got_late) = _mixer_bwd(
        proj, dcat, conv_full, gm_ln_g, gm_ln_b, ws, bs_b,
        plans=[_chip_plan(pair_sums(early, big_g, got_early)), sibling_plan(late, big_g)])
    half = D_MODEL // 2
    in_spec = reduce_spec("w_in")
    dw_top, _ = _matmul_f32("dw_in_top", h1t, dproj, rows=(0, half))
    pair_xo = pair_sums(late, big_g, got_late)[0]
    dw_bot, (got_top, (ws_all,)) = _matmul_f32(
        "dw_in_bot", h1t, dproj, rows=(half, half),
        plans=[_sibling_plan([dw_top], [in_spec]),
               _gather_plan([d_ws.reshape(1, 8 * CHUNK, CHUNK).astype(BF16)], [((N_DEV, 8 * CHUNK, CHUNK), 0, 1)])])
    pair_top, ((got_bot,),) = _pair_sum("pair_sum_w_in_top", dw_top, got_top[0], *in_spec, core,
                                        plans=[_sibling_plan([dw_bot], [in_spec])])
    pair_bot = _pair_sum("pair_sum_w_in_bot", dw_bot, got_bot, *in_spec, core)
    chip_shapes = [p.shape for p in (pair_top, pair_bot, pair_xo)]
    chips_in_flight, token = _split_start("grad_chip_exchange_start", _chip_routes, [pair_top, pair_bot, pair_xo], chip_shapes)
    n_tiles = s // min(TM_DH1, s)
    (grad_x, d_gmix), _ = _dh1("dh1", dproj, w_in_full, x2, dx1b, norm_mix_g, (0, n_tiles), after=[token])
    vec_names = ["norm_mix_g", "gm_ln_g", "gm_ln_b", "gm_bs", "norm_x_g", "norm_mem_g", "norm_final_g"]
    vec_g = [d_gmix, d_lng, d_lnb, d_bs, d_gx, d_gmem, d_gf]
    small = jnp.concatenate(vec_g + [sq_err, d_cw], axis=0)[None]
    small_in_flight, _ = _split_start("small_all_gather_start", _gather_routes, [small], [(N_DEV, 16, D_MODEL)])
    flat_ws = lambda t: t.reshape(8 * CHUNK, CHUNK)
    updated, _ = _adamw(
        "adamw_early",
        [(big_w[k], big_m[k], big_v[k], got_k) for k, got_k in zip(early, parts_early)]
        + [(flat_ws(ws), flat_ws(m_gm_ws), flat_ws(v_gm_ws), ws_all)], steps=2)
    big_out = dict(zip(early, updated[:3]))
    ws_out = updated[3]
    (pair_top, pair_bot, pair_xo), (land_top, land_bot, land_xo) = _split_wait(
        "grad_chip_exchange_wait", _chip_routes, chips_in_flight, [d_gmix] + [u[0] for u in updated])
    my_chip = (2 * px + py).astype(jnp.int32).reshape(1)
    (big_out["w_in"], big_out["w_xo"]), _ = _adamw(
        "adamw_late",
        [(big_w["w_in"], big_m["w_in"], big_v["w_in"], [land_top, land_bot], [pair_top, pair_bot]),
         (big_w["w_xo"], big_m["w_xo"], big_v["w_xo"], land_xo, pair_xo)],
        steps=4, own_slot=my_chip)
    (small,), (small_all,) = _split_wait("small_all_gather_wait", _gather_routes, small_in_flight, [big_out["w_xo"][0]])
    row = lambda t: t.reshape(1, D_MODEL)
    vec_out, conv_out, loss_row = _update_small(
        dev.astype(jnp.int32).reshape(1),
        [norm_mix_g, gm_ln_g, gm_ln_b, row(gm_bs), norm_x_g, norm_mem_g, row(norm_final_g)],
        [m_norm_mix_g, m_gm_ln_g, m_gm_ln_b, row(m_gm_bs), m_norm_x_g, m_norm_mem_g, row(m_norm_final_g)],
        [v_norm_mix_g, v_gm_ln_g, v_gm_ln_b, row(v_gm_bs), v_norm_x_g, v_norm_mem_g, row(v_norm_final_g)],
        conv_w, m_conv_w, v_conv_w, small, small_all)
    loss = loss_row[0, 0]

    def result(name, which):
        if name in big_out:
            return big_out[name][which][None]
        if name == "conv_w":
            return conv_out[which]
        if name == "gm_ws":
            return ws_out[which].reshape(1, 8, CHUNK, CHUNK)
        r = vec_out[vec_names.index(name)][which]
        if name == "gm_bs":
            return r.reshape(1, 8, CHUNK)
        if name == "norm_final_g":
            return r.reshape(D_MODEL)
        return r

    weights = ["norm_mix_g", "w_in", "conv_w", "gm_ln_g", "gm_ln_b", "gm_ws", "gm_bs", "w_out", "norm_x_g", "norm_mem_g", "w_q", "w_kv", "w_xo", "norm_final_g"]
    outs = [loss, grad_x[None]]
    for which in range(4):
        outs += [result(name, which) for name in weights]
    return tuple(outs)
```

```python
import math

import jax
import jax.numpy as jnp
from jax import lax
from jax.experimental import pallas as pl
from jax.experimental.pallas import tpu as pltpu

F32 = jnp.float32
BF16 = jnp.bfloat16
MESH = pl.DeviceIdType.MESH

D_MODEL = 1024
N_DEV = 8
N_CHIP = 4
CHUNK = 128
GROUPS = 7
IN_DIM = GROUPS * D_MODEL
SHARD_IN = IN_DIM // N_DEV
MIX_DIM = 2 * D_MODEL
X_HEADS = 4
X_HEAD_DIM = D_MODEL // X_HEADS
MEM_LEN = 256
EPS = 1e-6
GELU_K0 = math.sqrt(2.0 / math.pi)
GELU_K1 = 0.044715

ADAM_LR = 0.001
ADAM_B1 = 0.9
ADAM_B2 = 0.999
ADAM_EPS = 1e-08
ADAM_WD = 0.01
ADAM_STEP = 10

TM_INPROJ = 512
TM_ATTN = 512
SUB_ATTN = 256
TM_DH1 = 512
TM_DW = 512
TK_DW = 4096
TR_PAIR_SUM = 1024


def _dot(a, b):
    return lax.dot_general(a, b, (((1,), (0,)), ((), ())), preferred_element_type=F32)


def _dot_nt(a, b):
    return lax.dot_general(a, b, (((1,), (1,)), ((), ())), preferred_element_type=F32)


def _rows8(v):
    return jnp.sum(v.reshape(v.shape[0] // 8, 8, v.shape[1]), axis=0)


def _gelu(x):
    x2 = x * x
    t = jnp.tanh(GELU_K0 * (x + GELU_K1 * (x * x2)))
    g = 0.5 * x * (1.0 + t)
    dg = 0.5 * (1.0 + t) + 0.5 * x * (1.0 - t * t) * (GELU_K0 * (1.0 + 3.0 * GELU_K1 * x2))
    return g, dg


def _silu(z):
    s = jax.nn.sigmoid(z)
    return z * s, s * (1.0 + z * (1.0 - s))


def _rms(v):
    r = lax.rsqrt(jnp.mean(v * v, axis=-1, keepdims=True) + EPS)
    return v * r, r


def _rms_bwd(dy_g, vh, r):
    return r * (dy_g - vh * jnp.mean(dy_g * vh, axis=-1, keepdims=True))


def _place():
    return lax.axis_index("x"), lax.axis_index("y"), lax.axis_index("c")


def _other_chips(x, y):
    return [(1 - x, y), (x, 1 - y), (1 - x, 1 - y)]


def _pow2_divisor(n):
    return n & (-n)


def _shard_view(ref, axis, size, d):
    start = d * size
    align = _pow2_divisor(size)
    if align > 1:
        start = pl.multiple_of(start, align)
    idx = [slice(None)] * len(ref.shape)
    idx[axis] = pl.ds(start, size)
    return ref.at[tuple(idx)]


class _Plan:
    def __init__(self, ins, out_shape, sems, start, finish, mid=None, aliases=None):
        self.ins, self.out_shape, self.sems = list(ins), list(out_shape), list(sems)
        self.start, self.mid, self.finish = start, mid, finish
        self.aliases = dict(aliases or {})


def _gather_plan(shards, specs):
    n = len(shards)
    n_copy = 7

    def copies(ins, outs, sems):
        send_sems, recv_sems, local_sems = sems
        x, y, c = _place()
        me, sibling = (x, y, c), (x, y, 1 - c)
        chips = _other_chips(x, y)

        def block(a, p):
            _, axis, size = specs[a]
            return _shard_view(outs[a], axis, size, 4 * p[0] + 2 * p[1] + p[2])

        def copy(a, k, p, to, src=None):
            return pltpu.make_async_remote_copy(
                src_ref=block(a, p) if src is None else src,
                dst_ref=block(a, p),
                send_sem=send_sems.at[a * n_copy + k],
                recv_sem=recv_sems.at[a * n_copy + k],
                device_id=to,
                device_id_type=MESH,
            )

        def mine():
            return [pltpu.make_async_copy(ins[a], block(a, me), local_sems.at[a]) for a in range(n)]

        def first():
            return [cp for a in range(n) for cp in
                    [copy(a, 0, me, sibling, src=ins[a])] + [copy(a, 1 + j, me, (*chip, c), src=ins[a]) for j, chip in enumerate(chips)]]

        def landed():
            return [copy(a, 1 + j, (*chip, c), me) for j, chip in enumerate(chips) for a in range(n)]

        def passed():
            return [copy(a, 4 + j, (*chip, c), sibling) for j, chip in enumerate(chips) for a in range(n)]

        def last():
            return [cp for a in range(n) for cp in
                    [copy(a, 0, sibling, me)] + [copy(a, 4 + j, (*chip, 1 - c), me) for j, chip in enumerate(chips)]]

        return mine, first, landed, passed, last

    def start(ins, outs, sems):
        mine, first, _, _, _ = copies(ins, outs, sems)
        for cp in mine() + first():
            cp.start()

    def mid(ins, outs, sems):
        _, _, landed, passed, _ = copies(ins, outs, sems)
        for got, fwd in zip(landed(), passed()):
            got.wait_recv()
            fwd.start()

    def finish(ins, outs, sems):
        mine, first, _, passed, last = copies(ins, outs, sems)
        for cp in last():
            cp.wait_recv()
        for cp in first() + passed():
            cp.wait_send()
        for cp in mine():
            cp.wait()

    return _Plan(
        shards,
        [jax.ShapeDtypeStruct(full, s.dtype) for s, (full, _, _) in zip(shards, specs)],
        [pltpu.SemaphoreType.DMA((n * n_copy,)), pltpu.SemaphoreType.DMA((n * n_copy,)), pltpu.SemaphoreType.DMA((n,))],
        start, finish, mid,
    )


def _sibling_plan(grads, specs):
    n = len(grads)

    def shard_shape(a):
        axis, size = specs[a]
        shp = list(grads[a].shape)
        shp[axis] = size
        return tuple(shp)

    def copies(ins, outs, sems):
        send_sems, recv_sems = sems
        x, y, c = _place()
        return [
            pltpu.make_async_remote_copy(
                src_ref=_shard_view(ins[a], specs[a][0], specs[a][1], 2 * q + (1 - c)),
                dst_ref=outs[a].at[q],
                send_sem=send_sems.at[a * N_CHIP + q],
                recv_sem=recv_sems.at[a * N_CHIP + q],
                device_id=(x, y, 1 - c),
                device_id_type=MESH,
            )
            for a in range(n) for q in range(N_CHIP)
        ]

    def start(ins, outs, sems):
        for cp in copies(ins, outs, sems):
            cp.start()

    def finish(ins, outs, sems):
        for cp in copies(ins, outs, sems):
            cp.wait()

    return _Plan(
        grads,
        [jax.ShapeDtypeStruct((N_CHIP, *shard_shape(a)), grads[a].dtype) for a in range(n)],
        [pltpu.SemaphoreType.DMA((n * N_CHIP,)), pltpu.SemaphoreType.DMA((n * N_CHIP,))],
        start, finish,
    )


def _chip_plan(parts, full_rows=None, row0=None, into=None):
    n = len(parts)
    full_rows = [p.shape[1] for p in parts] if full_rows is None else full_rows
    row0 = [0] * n if row0 is None else row0

    def copies(ins, outs, sems):
        send_sems, recv_sems, local_sems = sems
        x, y, c = _place()
        my_chip = 2 * x + y
        chips = _other_chips(x, y)

        def land(a, q):
            return outs[a].at[q, pl.ds(row0[a], parts[a].shape[1]), :]

        def mine():
            return [pltpu.make_async_copy(ins[a].at[my_chip], land(a, my_chip), local_sems.at[a]) for a in range(n)]

        def remote(sending):
            return [
                pltpu.make_async_remote_copy(
                    src_ref=ins[a].at[2 * chip[0] + chip[1]],
                    dst_ref=land(a, my_chip if sending else 2 * chip[0] + chip[1]),
                    send_sem=send_sems.at[a * 3 + j], recv_sem=recv_sems.at[a * 3 + j],
                    device_id=(*chip, c), device_id_type=MESH)
                for a in range(n) for j, chip in enumerate(chips)
            ]

        return mine, remote

    def start(ins, outs, sems):
        mine, remote = copies(ins, outs, sems)
        for cp in mine() + remote(True):
            cp.start()

    def finish(ins, outs, sems):
        mine, remote = copies(ins, outs, sems)
        for cp in remote(False):
            cp.wait_recv()
        for cp in remote(True):
            cp.wait_send()
        for cp in mine():
            cp.wait()

    return _Plan(
        list(parts) + (list(into) if into is not None else []),
        [jax.ShapeDtypeStruct((N_CHIP, full_rows[a], parts[a].shape[2]), parts[a].dtype) for a in range(n)],
        [pltpu.SemaphoreType.DMA((n * 3,)), pltpu.SemaphoreType.DMA((n * 3,)), pltpu.SemaphoreType.DMA((n,))],
        start, finish,
        aliases={n + a: a for a in range(n)} if into is not None else None,
    )


def _call(body, *, name, operands, out_shape, grid=(), in_specs=None, out_specs=None, scratch_shapes=(),
          plans=(), aliases=None, prefetch=(), start_after_body=False, pass_on_at=0.6):
    operands, out_shape, scratch_shapes = list(operands), list(out_shape), list(scratch_shapes)
    n_pf, n_in, n_out, n_scr = len(prefetch), len(operands), len(out_shape), len(scratch_shapes)
    in_vmem = pl.BlockSpec(memory_space=pltpu.VMEM)
    in_hbm = pl.BlockSpec(memory_space=pl.ANY)
    in_specs = [in_vmem] * n_in if in_specs is None else list(in_specs)
    out_specs = [in_vmem] * n_out if out_specs is None else list(out_specs)
    io_alias = {n_pf + i: o for i, o in (aliases or {}).items()}
    pos_in, pos_out = n_pf + n_in, n_out
    for p in plans:
        for i, o in p.aliases.items():
            io_alias[pos_in + i] = pos_out + o
        pos_in += len(p.ins)
        pos_out += len(p.out_shape)
    steps = math.prod(grid) if grid else 1
    mid_step = min(steps - 1, int(steps * pass_on_at))

    def wrapped(*refs):
        refs = list(refs)
        take = lambda k: [refs.pop(0) for _ in range(k)]
        pf = take(n_pf)
        ins, p_ins = take(n_in), [take(len(p.ins)) for p in plans]
        outs, p_outs = take(n_out), [take(len(p.out_shape)) for p in plans]
        scr, p_sems = take(n_scr), [take(len(p.sems)) for p in plans]
        step = 0
        for ax, g in enumerate(grid):
            step = step * g + pl.program_id(ax)

        def hook(kind, at):
            todo = [(getattr(p, kind), a, b, c) for p, a, b, c in zip(plans, p_ins, p_outs, p_sems) if getattr(p, kind)]

            def run():
                for fn, a, b, c in todo:
                    fn(a, b, c)

            if todo and grid:
                pl.when(step == at)(run)
            elif todo:
                run()

        if not start_after_body:
            hook("start", 0)
        if body is not None:
            body(*pf, *ins, *outs, *scr)
        if start_after_body:
            hook("start", 0)
        hook("mid", mid_step)
        hook("finish", steps - 1)

    layout = dict(
        grid=grid,
        in_specs=in_specs + [in_hbm] * sum(len(p.ins) for p in plans),
        out_specs=out_specs + [in_hbm] * sum(len(p.out_shape) for p in plans),
        scratch_shapes=scratch_shapes + [s for p in plans for s in p.sems],
    )
    if n_pf:
        layout = dict(grid_spec=pltpu.PrefetchScalarGridSpec(num_scalar_prefetch=n_pf, **layout))
    res = pl.pallas_call(
        wrapped,
        name=name,
        out_shape=out_shape + [o for p in plans for o in p.out_shape],
        input_output_aliases=io_alias,
        compiler_params=pltpu.CompilerParams(dimension_semantics=("arbitrary",) * len(grid)) if grid else None,
        **layout,
    )(*prefetch, *operands, *[a for p in plans for a in p.ins])
    res = list(res)
    outs, plan_outs = res[:n_out], []
    pos = n_out
    for p in plans:
        plan_outs.append(res[pos : pos + len(p.out_shape)])
        pos += len(p.out_shape)
    return outs, plan_outs


def _inproj(x, g, w_shard, order, plans=()):
    s = x.shape[0]
    tm = min(TM_INPROJ, s)
    nt = s // tm
    wide = 2 * SHARD_IN
    fwd_at = nt // 2

    def body(order_ref, x_ref, g_ref, w_hbm, proj_ref, h1_ref, h1t_ref, wfull_hbm,
             w_scr, h1_scr, send_sems, recv_sems, local_sems):
        k, i = pl.program_id(0), pl.program_id(1)
        px, py, c = _place()
        me, sibling = (px, py, c), (px, py, 1 - c)
        chips = _other_chips(px, py)

        def half(slot, core):
            return w_scr.at[slot, :, pl.ds(pl.multiple_of(core * SHARD_IN, 128), SHARD_IN)]

        def copy(n, slot, core, to, src=None):
            return pltpu.make_async_remote_copy(
                src_ref=half(slot, core) if src is None else src, dst_ref=half(slot, core),
                send_sem=send_sems.at[n], recv_sem=recv_sems.at[n], device_id=to, device_id_type=MESH)

        def mine():
            return pltpu.make_async_copy(w_hbm, half(0, c), local_sems.at[0])

        def first():
            return [copy(0, 0, c, sibling, src=w_hbm)] + [copy(1 + j, 1 + j, c, (*chips[j], c), src=w_hbm) for j in range(2)]

        def relay():
            to = (c * px + (1 - c) * (1 - px), c * (1 - py) + (1 - c) * py, c)
            return pltpu.make_async_remote_copy(
                src_ref=half(2 - c, c), dst_ref=half(3, c), send_sem=send_sems.at[3], recv_sem=recv_sems.at[3],
                device_id=to, device_id_type=MESH)

        def save(slot):
            cols = pl.ds(pl.multiple_of(order_ref[slot] * wide, 128), wide)
            return pltpu.make_async_copy(w_scr.at[slot], wfull_hbm.at[:, cols], local_sems.at[1 + slot])

        @pl.when((k == 0) & (i == 0))
        def _():
            mine().start()
            for cp in first():
                cp.start()
            mine().wait()
            copy(0, 0, 1 - c, me).wait_recv()

        for j in range(3):
            @pl.when((k == j + 1) & (i == 0))
            def _():
                copy(4 + j, 1 + j, 1 - c, me).wait_recv()

        @pl.when((k == 3) & (i == 0))
        def _():
            for slot in range(N_CHIP):
                save(slot).start()

        @pl.when(k == 0)
        def _():
            xh, _ = _rms(x_ref[...])
            h = xh * g_ref[...]
            h1_ref[...] = h.astype(BF16)
            h1t_ref[...] = h.T.astype(BF16)
            h1_scr[pl.ds(pl.multiple_of(i * tm, tm), tm), :] = h.astype(BF16)

        proj_ref[...] = _dot(h1_scr[pl.ds(pl.multiple_of(i * tm, tm), tm), :], w_scr[k]).astype(BF16)

        for j in range(3):
            @pl.when((k == 0) & (i == nt - 1) if j < 2 else (k == 2) & (i == fwd_at))
            def _():
                copy(1 + j, 1 + j, c, me).wait_recv()
                copy(4 + j, 1 + j, c, sibling).start()
                if j < 2:
                    pl.when(c == 1 - j)(lambda: relay().start())

        @pl.when((k == 3) & (i == nt - 1))
        def _():
            for cp in first() + [relay()] + [copy(4 + j, 1 + j, c, sibling) for j in range(3)]:
                cp.wait_send()
            for slot in range(N_CHIP):
                save(slot).wait()

    once = lambda k, i, order_ref: (jnp.where(k == 0, i, nt - 1), 0)
    return _call(
        body,
        name="inproj",
        prefetch=[order],
        operands=[x, g, w_shard],
        grid=(N_CHIP, nt),
        in_specs=[
            pl.BlockSpec((tm, D_MODEL), once),
            pl.BlockSpec((1, D_MODEL), lambda k, i, order_ref: (0, 0)),
            pl.BlockSpec(memory_space=pl.ANY),
        ],
        out_specs=[
            pl.BlockSpec((tm, wide), lambda k, i, order_ref: (i, order_ref[k])),
            pl.BlockSpec((tm, D_MODEL), once),
            pl.BlockSpec((D_MODEL, tm), lambda k, i, order_ref: (0, jnp.where(k == 0, i, nt - 1))),
            pl.BlockSpec(memory_space=pl.ANY),
        ],
        out_shape=[
            jax.ShapeDtypeStruct((s, IN_DIM), BF16),
            jax.ShapeDtypeStruct((s, D_MODEL), BF16),
            jax.ShapeDtypeStruct((D_MODEL, s), BF16),
            jax.ShapeDtypeStruct((D_MODEL, IN_DIM), BF16),
        ],
        scratch_shapes=[
            pltpu.VMEM((N_CHIP, D_MODEL, wide), BF16),
            pltpu.VMEM((s, D_MODEL), BF16),
            pltpu.SemaphoreType.DMA((7,)),
            pltpu.SemaphoreType.DMA((7,)),
            pltpu.SemaphoreType.DMA((1 + N_CHIP,)),
        ],
        plans=plans,
        start_after_body=True,
        pass_on_at=0.8,
    )


def _causal_mask():
    row = lax.broadcasted_iota(jnp.int32, (CHUNK, CHUNK), 0)
    col = lax.broadcasted_iota(jnp.int32, (CHUNK, CHUNK), 1)
    return row, row >= col


def _mixer_fwd(proj, conv_w, ln_g, ln_b, ws, bs_b, plans=()):
    s = proj.shape[0]
    nt = s // CHUNK

    def body(proj_ref, cw_ref, lng_ref, lnb_ref, ws_ref, bsb_ref, cat_ref, pcar_ref):
        @pl.when(pl.program_id(0) == 0)
        def _():
            pcar_ref[...] = jnp.zeros_like(pcar_ref)

        row, tril = _causal_mask()
        for j in range(8):
            cs = slice(CHUNK * j, CHUNK * (j + 1))

            def grp(k):
                return proj_ref[:, k * D_MODEL + CHUNK * j : k * D_MODEL + CHUNK * (j + 1)].astype(F32)

            w = cw_ref[:, cs]
            p = grp(1) * grp(2)
            pc = pcar_ref[:, cs]
            p1 = jnp.where(row == 0, pc[7:8], pltpu.roll(p, 1, 0))
            p2 = jnp.where(row == 0, pc[6:7], jnp.where(row == 1, pc[7:8], pltpu.roll(p, 2, 0)))
            pcar_ref[:, cs] = p[CHUNK - 8 :]
            cv = w[0:1] * p2 + w[1:2] * p1 + w[2:3] * p
            sa, _ = _silu(grp(3))
            cat_ref[:, cs] = ((grp(0) * cv) * sa).astype(BF16)
            gu, _ = _gelu(grp(4))
            gv, _ = _gelu(grp(5))
            dv = gv - jnp.mean(gv, axis=-1, keepdims=True)
            vn = dv * lax.rsqrt(jnp.mean(dv * dv, axis=-1, keepdims=True) + EPS)
            vn = vn * lng_ref[:, cs] + lnb_ref[:, cs]
            wc = jnp.where(tril, ws_ref[j], 0.0).astype(BF16)
            sp = _dot(wc, vn.astype(BF16)) + bsb_ref[j]
            sb, _ = _silu(grp(6))
            cat_ref[:, D_MODEL + CHUNK * j : D_MODEL + CHUNK * (j + 1)] = ((gu * sp) * sb).astype(BF16)

    whole = lambda shape: pl.BlockSpec(shape, lambda i: (0,) * len(shape))
    (cat,), plan_outs = _call(
        body,
        name="mixer_fwd",
        operands=[proj, conv_w, ln_g, ln_b, ws, bs_b],
        grid=(nt,),
        in_specs=[
            pl.BlockSpec((CHUNK, IN_DIM), lambda i: (i, 0)),
            whole((8, D_MODEL)),
            whole((1, D_MODEL)),
            whole((1, D_MODEL)),
            whole((8, CHUNK, CHUNK)),
            whole((8, CHUNK, CHUNK)),
        ],
        out_specs=[pl.BlockSpec((CHUNK, MIX_DIM), lambda i: (i, 0))],
        out_shape=[jax.ShapeDtypeStruct((s, MIX_DIM), BF16)],
        scratch_shapes=[pltpu.VMEM((8, D_MODEL), F32)],
        plans=plans,
        pass_on_at=0.85,
    )
    return cat, plan_outs


def _mixer_bwd(proj, dcat, conv_w, ln_g, ln_b, ws, bs_b, plans=()):
    s = proj.shape[0]
    nt = s // CHUNK

    def body(proj_ref, halo_ref, dcat_ref, cw_ref, lng_ref, lnb_ref, ws_ref, bsb_ref,
             dproj_ref, dcw_ref, dlng_ref, dlnb_ref, dws_ref, dbs_ref,
             car_ref, acc_cw, acc_lng, acc_lnb, acc_bs):
        i = pl.program_id(0)
        tile = nt - 1 - i

        @pl.when(i == 0)
        def _():
            car_ref[...] = jnp.zeros_like(car_ref)
            acc_cw[...] = jnp.zeros_like(acc_cw)
            acc_lng[...] = jnp.zeros_like(acc_lng)
            acc_lnb[...] = jnp.zeros_like(acc_lnb)
            acc_bs[...] = jnp.zeros_like(acc_bs)
            dws_ref[...] = jnp.zeros_like(dws_ref)

        row, tril = _causal_mask()
        has_prev = jnp.where(tile > 0, 1.0, 0.0).astype(F32)
        for j in range(8):
            cs = slice(CHUNK * j, CHUNK * (j + 1))

            def col(k):
                return slice(k * D_MODEL + CHUNK * j, k * D_MODEL + CHUNK * (j + 1))

            def grp(k):
                return proj_ref[:, col(k)].astype(F32)

            w = cw_ref[:, cs]
            gb, gc, xa = grp(0), grp(1), grp(2)
            p = gc * xa
            pprev = halo_ref[:, col(1)].astype(F32) * halo_ref[:, col(2)].astype(F32) * has_prev
            p1 = jnp.where(row == 0, pprev[15:16], pltpu.roll(p, 1, 0))
            p2 = jnp.where(row == 0, pprev[14:15], jnp.where(row == 1, pprev[15:16], pltpu.roll(p, 2, 0)))
            cv = w[0:1] * p2 + w[1:2] * p1 + w[2:3] * p
            za = grp(3)
            sa, dsa = _silu(za)
            da = dcat_ref[:, cs].astype(F32)
            dproj_ref[:, col(0)] = (da * cv * sa).astype(BF16)
            dproj_ref[:, col(3)] = (da * gb * cv * dsa).astype(BF16)
            dcv = da * gb * sa
            nxt = car_ref[:, cs]
            d1 = jnp.where(row == CHUNK - 1, nxt[0:1], pltpu.roll(dcv, CHUNK - 1, 0))
            d2 = jnp.where(row == CHUNK - 2, nxt[0:1], jnp.where(row == CHUNK - 1, nxt[1:2], pltpu.roll(dcv, CHUNK - 2, 0)))
            car_ref[:, cs] = dcv[0:8]
            dp = w[2:3] * dcv + w[1:2] * d1 + w[0:1] * d2
            dproj_ref[:, col(1)] = (dp * xa).astype(BF16)
            dproj_ref[:, col(2)] = (dp * gc).astype(BF16)
            acc_cw[0, :, cs] += _rows8(dcv * p2)
            acc_cw[1, :, cs] += _rows8(dcv * p1)
            acc_cw[2, :, cs] += _rows8(dcv * p)
            gu, dgu = _gelu(grp(4))
            gv, dgv = _gelu(grp(5))
            dv = gv - jnp.mean(gv, axis=-1, keepdims=True)
            rstd = lax.rsqrt(jnp.mean(dv * dv, axis=-1, keepdims=True) + EPS)
            vnh = dv * rstd
            lng = lng_ref[:, cs]
            vnb = (vnh * lng + lnb_ref[:, cs]).astype(BF16)
            wcf = jnp.where(tril, ws_ref[j], 0.0)
            sp = _dot(wcf.astype(BF16), vnb) + bsb_ref[j]
            zb = grp(6)
            sb, dsb = _silu(zb)
            db = dcat_ref[:, D_MODEL + CHUNK * j : D_MODEL + CHUNK * (j + 1)].astype(F32)
            dproj_ref[:, col(4)] = (db * sp * sb * dgu).astype(BF16)
            dproj_ref[:, col(6)] = (db * gu * sp * dsb).astype(BF16)
            dsp = db * gu * sb
            acc_bs[j] += dsp
            dspb = dsp.astype(BF16)
            dws_ref[j] += _dot_nt(dspb, vnb)
            dvn = _dot(wcf.T.astype(BF16), dspb)
            acc_lnb[:, cs] += _rows8(dvn)
            acc_lng[:, cs] += _rows8(dvn * vnh)
            dvh = dvn * lng
            dgvv = rstd * (dvh - jnp.mean(dvh, axis=-1, keepdims=True) - vnh * jnp.mean(dvh * vnh, axis=-1, keepdims=True))
            dproj_ref[:, col(5)] = (dgvv * dgv).astype(BF16)

        @pl.when(i == nt - 1)
        def _():
            dlng_ref[...] = jnp.sum(acc_lng[...], axis=0, keepdims=True)
            dlnb_ref[...] = jnp.sum(acc_lnb[...], axis=0, keepdims=True)
            dcw_ref[...] = jnp.zeros_like(dcw_ref)
            for k in range(3):
                dcw_ref[k : k + 1, :] = jnp.sum(acc_cw[k], axis=0, keepdims=True)
            for j in range(8):
                dws_ref[j] = jnp.where(tril, dws_ref[j], 0.0)
                dbs_ref[:, CHUNK * j : CHUNK * (j + 1)] = jnp.sum(acc_bs[j].T, axis=0, keepdims=True)

    whole = lambda shape: pl.BlockSpec(shape, lambda i: (0,) * len(shape))
    halo_rows = 16
    per = CHUNK // halo_rows
    return _call(
        body,
        name="mixer_bwd",
        operands=[proj, proj, dcat, conv_w, ln_g, ln_b, ws, bs_b],
        plans=plans,
        grid=(nt,),
        in_specs=[
            pl.BlockSpec((CHUNK, IN_DIM), lambda i: (nt - 1 - i, 0)),
            pl.BlockSpec((halo_rows, IN_DIM), lambda i: (jnp.maximum((nt - 1 - i) * per - 1, 0), 0)),
            pl.BlockSpec((CHUNK, MIX_DIM), lambda i: (nt - 1 - i, 0)),
            whole((8, D_MODEL)),
            whole((1, D_MODEL)),
            whole((1, D_MODEL)),
            whole((8, CHUNK, CHUNK)),
            whole((8, CHUNK, CHUNK)),
        ],
        out_specs=[
            pl.BlockSpec((CHUNK, IN_DIM), lambda i: (nt - 1 - i, 0)),
            whole((8, D_MODEL)),
            whole((1, D_MODEL)),
            whole((1, D_MODEL)),
            whole((8, CHUNK, CHUNK)),
            whole((1, D_MODEL)),
        ],
        out_shape=[
            jax.ShapeDtypeStruct((s, IN_DIM), BF16),
            jax.ShapeDtypeStruct((8, D_MODEL), F32),
            jax.ShapeDtypeStruct((1, D_MODEL), F32),
            jax.ShapeDtypeStruct((1, D_MODEL), F32),
            jax.ShapeDtypeStruct((8, CHUNK, CHUNK), F32),
            jax.ShapeDtypeStruct((1, D_MODEL), F32),
        ],
        scratch_shapes=[
            pltpu.VMEM((8, D_MODEL), F32),
            pltpu.VMEM((3, 8, D_MODEL), F32),
            pltpu.VMEM((8, D_MODEL), F32),
            pltpu.VMEM((8, D_MODEL), F32),
            pltpu.VMEM((8, CHUNK, CHUNK), F32),
        ],
    )


def _kv_fwd(mem, g_mem, w_kv):
    def body(mem_ref, g_ref, w_ref, k_ref, vv_ref, mt_ref):
        mh, _ = _rms(mem_ref[...])
        m = mh * g_ref[...]
        kv = _dot(m.astype(BF16), w_ref[...])
        k_ref[...] = kv[:, :D_MODEL].astype(BF16)
        vv_ref[...] = kv[:, D_MODEL:].astype(BF16)
        mt_ref[...] = m.T.astype(BF16)

    return pl.pallas_call(
        body,
        name="kv_fwd",
        out_shape=[
            jax.ShapeDtypeStruct((MEM_LEN, D_MODEL), BF16),
            jax.ShapeDtypeStruct((MEM_LEN, D_MODEL), BF16),
            jax.ShapeDtypeStruct((D_MODEL, MEM_LEN), BF16),
        ],
    )(mem, g_mem, w_kv)


def _kv_bwd(mem, w_kv, mt, dk, dvv):
    def body(mem_ref, w_ref, mt_ref, dk_ref, dvv_ref, dw_ref, dg_ref):
        dkv = jnp.concatenate([dk_ref[...], dvv_ref[...]], axis=1).astype(BF16)
        dw_ref[...] = _dot(mt_ref[...], dkv)
        dm = _dot_nt(dkv, w_ref[...])
        mh, _ = _rms(mem_ref[...])
        dg_ref[...] = jnp.sum(dm * mh, axis=0, keepdims=True)

    return pl.pallas_call(
        body,
        name="kv_bwd",
        out_shape=[
            jax.ShapeDtypeStruct((D_MODEL, 2 * D_MODEL), F32),
            jax.ShapeDtypeStruct((1, D_MODEL), F32),
        ],
    )(mem, w_kv, mt, dk, dvv)


def _attn(x, cat, target, w_out, w_q, w_xo, k, vv, g_x, g_f):
    s = x.shape[0]
    tm = min(TM_ATTN, s)
    sub = min(SUB_ATTN, tm)
    nt = s // tm
    scale = 1.0 / math.sqrt(X_HEAD_DIM)

    def body(x_ref, cat_ref, t_ref, wout_ref, wq_ref, wxo_ref, k_ref, vv_ref, gx_ref, gf_ref,
             dcat_ref, catt_ref, h2t_ref, ot_ref, dx2b_ref, dqb_ref, dx1b_ref,
             dk_ref, dvv_ref, dgf_ref, dgx_ref, loss_ref,
             p_scr, q_scr, o_scr, dq_scr, acc_gf, acc_gx, acc_loss):
        i = pl.program_id(0)

        @pl.when(i == 0)
        def _():
            dk_ref[...] = jnp.zeros_like(dk_ref)
            dvv_ref[...] = jnp.zeros_like(dvv_ref)
            acc_gf[...] = jnp.zeros_like(acc_gf)
            acc_gx[...] = jnp.zeros_like(acc_gx)
            acc_loss[...] = jnp.zeros_like(acc_loss)

        subs = [slice(u * sub, (u + 1) * sub) for u in range(tm // sub)]
        heads = [slice(X_HEAD_DIM * h, X_HEAD_DIM * (h + 1)) for h in range(X_HEADS)]
        gx, gf = gx_ref[...], gf_ref[...]
        x1, x1h, r2 = [], [], []
        for rs in subs:
            cat = cat_ref[rs, :]
            catt_ref[:, rs] = cat.astype(F32).T.astype(BF16)
            v = x_ref[rs, :] + _dot(cat, wout_ref[...])
            vh, r = _rms(v)
            h2 = vh * gx
            h2t_ref[:, rs] = h2.T.astype(BF16)
            q_scr[rs, :] = _dot(h2.astype(BF16), wq_ref[...]).astype(BF16)
            x1.append(v), x1h.append(vh), r2.append(r)
        for rs in subs:
            for hs in heads:
                sc = _dot_nt(q_scr[rs, hs], k_ref[:, hs]) * scale
                e = jnp.exp(sc - jnp.max(sc, axis=-1, keepdims=True))
                p = e / jnp.sum(e, axis=-1, keepdims=True)
                p_scr[rs, hs] = p
                o_scr[rs, hs] = _dot(p.astype(BF16), vv_ref[:, hs])
        dx2, sq, d_gf = [], None, None
        for u, rs in enumerate(subs):
            o = o_scr[rs, :]
            ot_ref[:, rs] = o.T.astype(BF16)
            x2 = x1[u] + _dot(o.astype(BF16), wxo_ref[...])
            x2h, r3 = _rms(x2)
            err = x2h * gf - t_ref[rs, :]
            dy = err * (1.0 / D_MODEL)
            sq = _rows8(err * err) if sq is None else sq + _rows8(err * err)
            d_gf = _rows8(dy * x2h) if d_gf is None else d_gf + _rows8(dy * x2h)
            dx2.append(_rms_bwd(dy * gf, x2h, r3))
        acc_loss[...] += sq
        acc_gf[...] += d_gf
        d_vv, d_k = [None] * X_HEADS, [None] * X_HEADS
        for u, rs in enumerate(subs):
            dx2b = dx2[u].astype(BF16)
            dx2b_ref[rs, :] = dx2b
            dob = _dot_nt(dx2b, wxo_ref[...]).astype(BF16)
            for h, hs in enumerate(heads):
                p = p_scr[rs, hs]
                dp = _dot_nt(dob[:, hs], vv_ref[:, hs])
                ds = p * (dp - jnp.sum(dp * p, axis=-1, keepdims=True)) * scale
                dq_scr[rs, hs] = _dot(ds.astype(BF16), k_ref[:, hs]).astype(BF16)
                dvv_h = _dot(p.T.astype(BF16), dob[:, hs])
                dk_h = _dot(ds.T.astype(BF16), q_scr[rs, hs])
                d_vv[h] = dvv_h if d_vv[h] is None else d_vv[h] + dvv_h
                d_k[h] = dk_h if d_k[h] is None else d_k[h] + dk_h
        for h, hs in enumerate(heads):
            dvv_ref[:, hs] += d_vv[h]
            dk_ref[:, hs] += d_k[h]
        d_gx = None
        for u, rs in enumerate(subs):
            dqb = dq_scr[rs, :]
            dqb_ref[rs, :] = dqb
            dh2 = _dot_nt(dqb, wq_ref[...])
            d_gx = _rows8(dh2 * x1h[u]) if d_gx is None else d_gx + _rows8(dh2 * x1h[u])
            dx1 = dx2[u] + _rms_bwd(dh2 * gx, x1h[u], r2[u])
            dx1b = dx1.astype(BF16)
            dx1b_ref[rs, :] = dx1b
            dcat_ref[rs, :] = _dot_nt(dx1b, wout_ref[...]).astype(BF16)
        acc_gx[...] += d_gx

        @pl.when(i == nt - 1)
        def _():
            dgf_ref[...] = jnp.sum(acc_gf[...], axis=0, keepdims=True)
            dgx_ref[...] = jnp.sum(acc_gx[...], axis=0, keepdims=True)
            loss_ref[...] = jnp.sum(acc_loss[...], axis=0, keepdims=True)

    whole = lambda shape: pl.BlockSpec(shape, lambda i: (0,) * len(shape))
    once = lambda shape: pl.BlockSpec(shape, lambda i: (0,) * len(shape), pipeline_mode=pl.Buffered(1))
    rows = lambda width: pl.BlockSpec((tm, width), lambda i: (i, 0))
    cols = lambda height: pl.BlockSpec((height, tm), lambda i: (0, i))
    vec = jax.ShapeDtypeStruct((1, D_MODEL), F32)
    return pl.pallas_call(
        body,
        name="attn",
        grid=(nt,),
        in_specs=[
            rows(D_MODEL), rows(MIX_DIM), rows(D_MODEL),
            once((MIX_DIM, D_MODEL)), once((D_MODEL, D_MODEL)), once((D_MODEL, D_MODEL)),
            once((MEM_LEN, D_MODEL)), once((MEM_LEN, D_MODEL)),
            whole((1, D_MODEL)), whole((1, D_MODEL)),
        ],
        out_specs=[
            rows(MIX_DIM), cols(MIX_DIM), cols(D_MODEL), cols(D_MODEL),
            rows(D_MODEL), rows(D_MODEL), rows(D_MODEL),
            whole((MEM_LEN, D_MODEL)), whole((MEM_LEN, D_MODEL)),
            whole((1, D_MODEL)), whole((1, D_MODEL)), whole((1, D_MODEL)),
        ],
        out_shape=[
            jax.ShapeDtypeStruct((s, MIX_DIM), BF16),
            jax.ShapeDtypeStruct((MIX_DIM, s), BF16),
            jax.ShapeDtypeStruct((D_MODEL, s), BF16),
            jax.ShapeDtypeStruct((D_MODEL, s), BF16),
            jax.ShapeDtypeStruct((s, D_MODEL), BF16),
            jax.ShapeDtypeStruct((s, D_MODEL), BF16),
            jax.ShapeDtypeStruct((s, D_MODEL), BF16),
            jax.ShapeDtypeStruct((MEM_LEN, D_MODEL), F32),
            jax.ShapeDtypeStruct((MEM_LEN, D_MODEL), F32),
            vec, vec, vec,
        ],
        scratch_shapes=[
            pltpu.VMEM((tm, D_MODEL), F32),
            pltpu.VMEM((tm, D_MODEL), BF16),
            pltpu.VMEM((tm, D_MODEL), F32),
            pltpu.VMEM((tm, D_MODEL), BF16),
            pltpu.VMEM((8, D_MODEL), F32),
            pltpu.VMEM((8, D_MODEL), F32),
            pltpu.VMEM((8, D_MODEL), F32),
        ],
        compiler_params=pltpu.CompilerParams(dimension_semantics=("arbitrary",)),
    )(x, cat, target, w_out, w_q, w_xo, k, vv, g_x, g_f)


def _matmul_f32(name, a, b, rows=None, plans=()):
    first, m = (0, a.shape[0]) if rows is None else rows
    kk = a.shape[1]
    n = b.shape[1]
    tm, tn, tk = min(m, TM_DW), min(n, D_MODEL), min(kk, TK_DW)
    nk = kk // tk
    i0 = first // tm

    def body(a_ref, b_ref, o_ref):
        prod = _dot(a_ref[...], b_ref[...])
        if nk == 1:
            o_ref[...] = prod
            return
        kid = pl.program_id(2)

        @pl.when(kid == 0)
        def _():
            o_ref[...] = prod

        @pl.when(kid > 0)
        def _():
            o_ref[...] += prod

    (out,), plan_outs = _call(
        body,
        name=name,
        operands=[a, b],
        grid=(m // tm, n // tn, nk),
        in_specs=[pl.BlockSpec((tm, tk), lambda i, j, k: (i + i0, k)), pl.BlockSpec((tk, tn), lambda i, j, k: (k, j))],
        out_specs=[pl.BlockSpec((tm, tn), lambda i, j, k: (i, j))],
        out_shape=[jax.ShapeDtypeStruct((m, n), F32)],
        plans=plans,
    )
    return out, plan_outs


def _dh1(name, dproj, w_in, x, dx1, g, tiles, into=None, after=(), plans=()):
    s = x.shape[0]
    tm = min(TM_DH1, s)
    t0, nt = tiles[0], tiles[1] - tiles[0]

    def body(dp_ref, w_ref, x_ref, dx1_ref, g_ref, *rest):
        gx_ref, dg_ref, acc_g = rest[-3:]
        i = pl.program_id(0)

        @pl.when(i == 0)
        def _():
            acc_g[...] = jnp.zeros_like(acc_g)

        dh1 = _dot_nt(dp_ref[...], w_ref[...])
        xh, r = _rms(x_ref[...])
        acc_g[...] += _rows8(dh1 * xh)
        gx_ref[...] = dx1_ref[...].astype(F32) + _rms_bwd(dh1 * g_ref[...], xh, r)

        @pl.when(i == nt - 1)
        def _():
            dg_ref[...] = jnp.sum(acc_g[...], axis=0, keepdims=True)

    rows = pl.BlockSpec((tm, D_MODEL), lambda i: (i + t0, 0))
    return _call(
        body,
        name=name,
        operands=[dproj, w_in, x, dx1, g] + ([into] if into is not None else []) + list(after),
        grid=(nt,),
        in_specs=[
            pl.BlockSpec((tm, IN_DIM), lambda i: (i + t0, 0)),
            pl.BlockSpec((D_MODEL, IN_DIM), lambda i: (0, 0), pipeline_mode=pl.Buffered(1)),
            rows,
            rows,
            pl.BlockSpec((1, D_MODEL), lambda i: (0, 0)),
        ] + [pl.BlockSpec(memory_space=pl.ANY)] * ((into is not None) + len(after)),
        out_specs=[rows, pl.BlockSpec((1, D_MODEL), lambda i: (0, 0))],
        out_shape=[jax.ShapeDtypeStruct((s, D_MODEL), F32), jax.ShapeDtypeStruct((1, D_MODEL), F32)],
        scratch_shapes=[pltpu.VMEM((8, D_MODEL), F32)],
        plans=plans,
        aliases={5: 0} if into is not None else None,
    )


def _pair_sum(name, grad, got, axis, size, core, plans=()):
    shard = list(grad.shape)
    shard[axis] = size
    r, cdim = shard
    tr = min(r, TR_PAIR_SUM)
    nr = r // tr

    def body(core_ref, g_ref, got_ref, o_ref):
        o_ref[0] = (g_ref[...] + got_ref[0]).astype(BF16)

    if axis == 1:
        g_map = lambda q, t, core_ref: (t, 2 * q + core_ref[0])
    else:
        g_map = lambda q, t, core_ref: ((2 * q + core_ref[0]) * nr + t, 0)
    (out,), plan_outs = _call(
        body,
        name=name,
        prefetch=[core],
        operands=[grad, got],
        grid=(N_CHIP, nr),
        in_specs=[
            pl.BlockSpec((tr, cdim), g_map),
            pl.BlockSpec((1, tr, cdim), lambda q, t, core_ref: (q, t, 0)),
        ],
        out_specs=[pl.BlockSpec((1, tr, cdim), lambda q, t, core_ref: (q, t, 0))],
        out_shape=[jax.ShapeDtypeStruct((N_CHIP, r, cdim), BF16)],
        plans=plans,
    )
    return (out, plan_outs) if plans else out


def _adam_step(w, m, v, g):
    c1 = 1.0 / (1.0 - ADAM_B1**ADAM_STEP)
    c2 = 1.0 / (1.0 - ADAM_B2**ADAM_STEP)
    m_new = ADAM_B1 * m + (1.0 - ADAM_B1) * g
    v_new = ADAM_B2 * v + (1.0 - ADAM_B2) * (g * g)
    return -ADAM_LR * ((m_new * c1) / (jnp.sqrt(v_new * c2) + ADAM_EPS) + ADAM_WD * w), m_new, v_new


def _update_small(dev, vec_w, vec_m, vec_v, conv_w, conv_m, conv_v, small, landed):
    n_vec = len(vec_w)

    def body(dev_ref, *refs):
        refs = list(refs)
        take = lambda k: [refs.pop(0) for _ in range(k)]
        w_refs, m_refs, v_refs = take(n_vec), take(n_vec), take(n_vec)
        cw_ref, cm_ref, cv_ref, small_ref, land_ref, small_conv_ref, land_conv_ref = take(7)
        vec_outs, conv_outs, (loss_ref,) = take(4 * n_vec), take(4), take(1)

        def total(own_ref, others_ref):
            acc = None
            for d in range(N_DEV):
                part = jnp.where(dev_ref[0] == d, own_ref[0], others_ref[d])
                acc = part if acc is None else acc + part
            return acc

        g_all = total(small_ref, land_ref)
        for i in range(n_vec):
            g = g_all[i : i + 1]
            delta, m_new, v_new = _adam_step(w_refs[i][...], m_refs[i][...], v_refs[i][...], g)
            for ref, val in zip(vec_outs[4 * i : 4 * i + 4], (g, delta, m_new, v_new)):
                ref[...] = val
        g = total(small_conv_ref, land_conv_ref)[0:3]
        delta, m_new, v_new = _adam_step(cw_ref[0], cm_ref[0], cv_ref[0], g)
        for ref, val in zip(conv_outs, (g, delta, m_new, v_new)):
            ref[0] = val
        loss_ref[...] = jnp.broadcast_to(0.5 * jnp.sum(g_all[7:8], axis=-1, keepdims=True) / D_MODEL, loss_ref.shape)

    vec_spec = pl.BlockSpec((1, D_MODEL), lambda i, dev_ref: (0, 0))
    conv_spec = pl.BlockSpec((1, 3, CHUNK), lambda i, dev_ref: (0, 0, 0))
    vec_shape = jax.ShapeDtypeStruct((1, D_MODEL), F32)
    conv_shape = jax.ShapeDtypeStruct((1, 3, CHUNK), F32)
    outs, _ = _call(
        body,
        name="update_small",
        prefetch=[dev],
        operands=list(vec_w) + list(vec_m) + list(vec_v) + [conv_w, conv_m, conv_v, small, landed, small, landed],
        grid=(1,),
        in_specs=[vec_spec] * (3 * n_vec) + [conv_spec] * 3 + [
            pl.BlockSpec((1, 16, D_MODEL), lambda i, dev_ref: (0, 0, 0)),
            pl.BlockSpec((N_DEV, 16, D_MODEL), lambda i, dev_ref: (0, 0, 0)),
            pl.BlockSpec((1, 8, CHUNK), lambda i, dev_ref: (0, 1, dev_ref[0])),
            pl.BlockSpec((N_DEV, 8, CHUNK), lambda i, dev_ref: (0, 1, dev_ref[0])),
        ],
        out_specs=[vec_spec] * (4 * n_vec) + [conv_spec] * 4 + [pl.BlockSpec((1, CHUNK), lambda i, dev_ref: (0, 0))],
        out_shape=[vec_shape] * (4 * n_vec) + [conv_shape] * 4 + [jax.ShapeDtypeStruct((1, CHUNK), F32)],
    )
    return [outs[4 * i : 4 * i + 4] for i in range(n_vec)], outs[4 * n_vec : 4 * n_vec + 4], outs[-1]


def _adamw(name, groups, steps, own_slot=None, plans=()):
    lists = lambda p: list(p) if isinstance(p, (list, tuple)) else [p]
    groups = [(g[0], g[1], g[2], lists(g[3]), lists(g[4]) if len(g) > 4 else []) for g in groups]
    counts = [3 + len(g[3]) + len(g[4]) for g in groups]
    use_slot = own_slot is not None

    def first_tiles(w, parts):
        firsts, first = [], 0
        for p in parts:
            firsts.append(first)
            first += p.shape[1] // (w.shape[0] // steps)
        return firsts

    def update(t, slot_ref, firsts, w_ref, m_ref, v_ref, p_refs, o_refs, g_ref, d_ref, nm_ref, nv_ref):
        g = None
        for k, p_ref in enumerate(p_refs):
            gk = None
            for q in range(p_ref.shape[0]):
                part = p_ref[q].astype(F32)
                if o_refs:
                    part = jnp.where(slot_ref[0] == q, o_refs[k][0].astype(F32), part)
                gk = part if gk is None else gk + part
            g = gk if g is None else jnp.where(t >= firsts[k], gk, g)
        g_ref[...] = g
        d_ref[...], nm_ref[...], nv_ref[...] = _adam_step(w_ref[...], m_ref[...], v_ref[...], g)

    def body(*refs):
        refs = list(refs)
        slot_ref = refs.pop(0) if use_slot else None
        t = pl.program_id(0)
        ins, outs = refs[: sum(counts)], refs[sum(counts) :]
        for i, (w, _, _, parts, own) in enumerate(groups):
            mine = ins[sum(counts[:i]) : sum(counts[: i + 1])]
            update(t, slot_ref, first_tiles(w, parts), *mine[:3], mine[3 : 3 + len(parts)], mine[3 + len(parts) :],
                   *outs[4 * i : 4 * i + 4])

    operands, in_specs, out_specs, out_shape = [], [], [], []
    for w, m, v, parts, own in groups:
        r, cdim = w.shape
        tr = r // steps
        firsts = first_tiles(w, parts)

        def tile_of(k, t, firsts=firsts, parts=parts, tr=tr):
            return jnp.clip(t - firsts[k], 0, parts[k].shape[1] // tr - 1)

        blk = pl.BlockSpec((tr, cdim), lambda t, *_: (t, 0))
        operands += [w, m, v] + parts + own
        in_specs += [blk, blk, blk]
        in_specs += [pl.BlockSpec((p.shape[0], tr, cdim), lambda t, *_, k=k, tile_of=tile_of: (0, tile_of(k, t), 0)) for k, p in enumerate(parts)]
        in_specs += [pl.BlockSpec((1, tr, cdim), lambda t, slot_ref, k=k, tile_of=tile_of: (slot_ref[0], tile_of(k, t), 0)) for k in range(len(own))]
        out_specs += [blk] * 4
        out_shape += [jax.ShapeDtypeStruct((r, cdim), F32)] * 4
    outs, plan_outs = _call(
        body,
        name=name,
        prefetch=[own_slot] if use_slot else [],
        operands=operands,
        grid=(steps,),
        in_specs=in_specs,
        out_specs=out_specs,
        out_shape=out_shape,
        plans=plans,
    )
    return [outs[4 * i : 4 * i + 4] for i in range(len(groups))], plan_outs


def _chip_routes(n):
    x, y, c = _place()
    my_chip = 2 * x + y
    return [(a, (*chip, c), 2 * chip[0] + chip[1], my_chip, 2 * chip[0] + chip[1])
            for a in range(n) for chip in _other_chips(x, y)]


def _gather_routes(n):
    x, y, c = _place()
    flip = lambda v, f: 1 - v if f else v
    peers = [(flip(x, fx), flip(y, fy), flip(c, fc)) for fx in (0, 1) for fy in (0, 1) for fc in (0, 1) if fx + fy + fc]
    return [(0, p, 0, 4 * x + 2 * y + c, 4 * p[0] + 2 * p[1] + p[2]) for p in peers]


_chip_routes.copies_per_array = N_CHIP - 1
_gather_routes.copies_per_array = N_DEV - 1


def _split_start(name, routes, srcs, land_shapes):
    n = len(srcs)
    hbm = pl.BlockSpec(memory_space=pltpu.HBM)
    sem = pl.BlockSpec(memory_space=pltpu.SEMAPHORE)

    def body(*refs):
        src_refs, land_refs = refs[:n], refs[n : 2 * n]
        send_sems, recv_sems = refs[2 * n], refs[2 * n + 1]
        token = refs[-1]
        for k, (a, peer, src_slot, there, _) in enumerate(routes(n)):
            pltpu.make_async_remote_copy(
                src_ref=src_refs[a].at[src_slot], dst_ref=land_refs[a].at[there],
                send_sem=send_sems.at[k], recv_sem=recv_sems.at[k], device_id=peer, device_id_type=MESH).start()
        token[...] = jnp.zeros_like(token)

    in_hbm = [pltpu.with_memory_space_constraint(p, pltpu.HBM) for p in srcs]
    in_hbm += [pltpu.with_memory_space_constraint(lax.empty(shape, p.dtype), pltpu.HBM) for p, shape in zip(srcs, land_shapes)]
    n_copies = routes.copies_per_array * n
    res = pl.pallas_call(
        body,
        name=name,
        out_shape=[pltpu.SemaphoreType.DMA((n_copies,)), pltpu.SemaphoreType.DMA((n_copies,))]
        + [pltpu.HBM(p.shape, p.dtype) for p in srcs] + [pltpu.HBM(shape, p.dtype) for p, shape in zip(srcs, land_shapes)]
        + [jax.ShapeDtypeStruct((8, CHUNK), F32)],
        in_specs=[hbm] * (2 * n),
        out_specs=[sem, sem] + [hbm] * (2 * n) + [pl.BlockSpec(memory_space=pltpu.VMEM)],
        input_output_aliases={i: 2 + i for i in range(2 * n)},
        compiler_params=pltpu.CompilerParams(has_side_effects=pltpu.SideEffectType.DATAFLOW_SIDE_EFFECTING),
    )(*in_hbm)
    return res[:-1], res[-1]


def _split_wait(name, routes, in_flight, after):
    n = (len(in_flight) - 2) // 2
    hbm = pl.BlockSpec(memory_space=pltpu.HBM)
    sem = pl.BlockSpec(memory_space=pltpu.SEMAPHORE)
    send_sems, recv_sems, *bufs = in_flight

    def body(*refs):
        src_refs, land_refs = refs[:n], refs[n : 2 * n]
        send_ref, recv_ref = refs[2 * n], refs[2 * n + 1]
        for k, (a, peer, src_slot, _, here) in enumerate(routes(n)):
            cp = pltpu.make_async_remote_copy(
                src_ref=src_refs[a].at[src_slot], dst_ref=land_refs[a].at[here],
                send_sem=send_ref.at[k], recv_sem=recv_ref.at[k], device_id=peer, device_id_type=MESH)
            cp.wait_send()
            cp.wait_recv()

    res = pl.pallas_call(
        body,
        name=name,
        out_shape=[pltpu.HBM(b.shape, b.dtype) for b in bufs],
        in_specs=[hbm] * (2 * n) + [sem, sem] + [pl.BlockSpec(memory_space=pl.ANY)] * len(after),
        out_specs=[hbm] * (2 * n),
        input_output_aliases={i: i for i in range(2 * n)},
        compiler_params=pltpu.CompilerParams(has_side_effects=pltpu.SideEffectType.DATAFLOW_SIDE_EFFECTING),
    )(*bufs, send_sems, recv_sems, *after)
    return res[:n], res[n:]


def kernel(x, mem, norm_mix_g, w_in, conv_w, gm_ln_g, gm_ln_b, gm_ws, gm_bs, w_out, norm_x_g, norm_mem_g, w_q, w_kv, w_xo, norm_final_g, loss_target, m_norm_mix_g, m_w_in, m_conv_w, m_gm_ln_g, m_gm_ln_b, m_gm_ws, m_gm_bs, m_w_out, m_norm_x_g, m_norm_mem_g, m_w_q, m_w_kv, m_w_xo, m_norm_final_g, v_norm_mix_g, v_w_in, v_conv_w, v_gm_ln_g, v_gm_ln_b, v_gm_ws, v_gm_bs, v_w_out, v_norm_x_g, v_norm_mem_g, v_w_q, v_w_kv, v_w_xo, v_norm_final_g):
    s = x.shape[1]
    dev = 4 * lax.axis_index("x") + 2 * lax.axis_index("y") + lax.axis_index("c")
    core = lax.axis_index("c").astype(jnp.int32).reshape(1)
    x2 = x[0]
    target = loss_target[0]
    pad_taps = lambda t: jnp.pad(t[0], ((0, 5), (0, 0)))

    big_names = ["w_in", "w_out", "w_q", "w_kv", "w_xo"]
    big_w = dict(w_in=w_in[0], w_out=w_out[0], w_q=w_q[0], w_kv=w_kv[0], w_xo=w_xo[0])
    big_m = dict(w_in=m_w_in[0], w_out=m_w_out[0], w_q=m_w_q[0], w_kv=m_w_kv[0], w_xo=m_w_xo[0])
    big_v = dict(w_in=v_w_in[0], w_out=v_w_out[0], w_q=v_w_q[0], w_kv=v_w_kv[0], w_xo=v_w_xo[0])
    shard_axis = dict(w_in=1, w_out=0, w_q=0, w_kv=1, w_xo=0)
    full_shape = dict(w_in=(D_MODEL, IN_DIM), w_out=(MIX_DIM, D_MODEL), w_q=(D_MODEL, D_MODEL),
                      w_kv=(D_MODEL, 2 * D_MODEL), w_xo=(D_MODEL, D_MODEL))
    shard_size = {k: big_w[k].shape[shard_axis[k]] for k in big_names}
    others = big_names[1:]
    gather_spec = lambda k: (full_shape[k], shard_axis[k], shard_size[k])
    reduce_spec = lambda k: (shard_axis[k], shard_size[k])
    ws = gm_ws[0]
    bs_b = jnp.broadcast_to(gm_bs[0][:, :, None], (8, CHUNK, CHUNK))
    g_f = norm_final_g.reshape(1, D_MODEL)

    px, py = lax.axis_index("x"), lax.axis_index("y")
    chip_order = jnp.stack([2 * px + py, 2 * (1 - px) + py, 2 * px + 1 - py, 2 * (1 - px) + 1 - py]).astype(jnp.int32)
    (proj, h1, h1t, w_in_full), ((w_out_full, conv_g),) = _inproj(
        x2, norm_mix_g, big_w["w_in"].astype(BF16), chip_order,
        plans=[_gather_plan([big_w["w_out"].astype(BF16), pad_taps(conv_w)[None]], [gather_spec("w_out"), ((N_DEV, 8, CHUNK), 0, 1)])])
    conv_full = conv_g.transpose(1, 0, 2).reshape(8, D_MODEL)
    attn_w = ["w_q", "w_kv", "w_xo"]
    cat, (gathered,) = _mixer_fwd(
        proj, conv_full, gm_ln_g, gm_ln_b, ws, bs_b,
        plans=[_gather_plan([big_w[k].astype(BF16) for k in attn_w], [gather_spec(k) for k in attn_w])])
    wf = dict(zip(attn_w, gathered), w_out=w_out_full)
    k, vv, mt = _kv_fwd(mem[0], norm_mem_g, wf["w_kv"])
    (dcat, catt, h2t, ot, dx2b, dqb, dx1b, dk, dvv, d_gf, d_gx, sq_err) = _attn(
        x2, cat, target, wf["w_out"], wf["w_q"], wf["w_xo"], k, vv, norm_x_g, g_f)

    def pair_sums(names, grads, got):
        return [_pair_sum("pair_sum_" + k, grads[k], got[i], *reduce_spec(k), core) for i, k in enumerate(names)]

    sibling_plan = lambda names, grads: _sibling_plan([grads[k] for k in names], [reduce_spec(k) for k in names])
    early, late = ["w_out", "w_kv", "w_q"], ["w_xo"]
    big_g = dict(w_out=_matmul_f32("dw_out", catt, dx1b)[0])
    big_g["w_kv"], d_gmem = _kv_bwd(mem[0], wf["w_kv"], mt, dk, dvv)
    big_g["w_q"], (got_a,) = _matmul_f32("dw_q", h2t, dqb, plans=[sibling_plan(early[:2], big_g)])
    big_g["w_xo"], (got_b,) = _matmul_f32("dw_xo", ot, dx2b, plans=[sibling_plan(early[2:], big_g)])
    got_early = got_a + got_b
    (dproj, d_cw, d_lng, d_lnb, d_ws, d_bs), (parts_early, got_late) = _mixer_bwd(
        proj, dcat, conv_full, gm_ln_g, gm_ln_b, ws, bs_b,
        plans=[_chip_plan(pair_sums(early, big_g, got_early)), sibling_plan(late, big_g)])
    half = D_MODEL // 2
    in_spec = reduce_spec("w_in")
    dw_top, _ = _matmul_f32("dw_in_top", h1t, dproj, rows=(0, half))
    pair_xo = pair_sums(late, big_g, got_late)[0]
    dw_bot, (got_top, (ws_all,)) = _matmul_f32(
        "dw_in_bot", h1t, dproj, rows=(half, half),
        plans=[_sibling_plan([dw_top], [in_spec]),
               _gather_plan([d_ws.reshape(1, 8 * CHUNK, CHUNK).astype(BF16)], [((N_DEV, 8 * CHUNK, CHUNK), 0, 1)])])
    pair_top, ((got_bot,),) = _pair_sum("pair_sum_w_in_top", dw_top, got_top[0], *in_spec, core,
                                        plans=[_sibling_plan([dw_bot], [in_spec])])
    pair_bot = _pair_sum("pair_sum_w_in_bot", dw_bot, got_bot, *in_spec, core)
    chip_shapes = [p.shape for p in (pair_top, pair_bot, pair_xo)]
    chips_in_flight, token = _split_start("grad_chip_exchange_start", _chip_routes, [pair_top, pair_bot, pair_xo], chip_shapes)
    n_tiles = s // min(TM_DH1, s)
    (grad_x, d_gmix), _ = _dh1("dh1", dproj, w_in_full, x2, dx1b, norm_mix_g, (0, n_tiles), after=[token])
    vec_names = ["norm_mix_g", "gm_ln_g", "gm_ln_b", "gm_bs", "norm_x_g", "norm_mem_g", "norm_final_g"]
    vec_g = [d_gmix, d_lng, d_lnb, d_bs, d_gx, d_gmem, d_gf]
    small = jnp.concatenate(vec_g + [sq_err, d_cw], axis=0)[None]
    small_in_flight, _ = _split_start("small_all_gather_start", _gather_routes, [small], [(N_DEV, 16, D_MODEL)])
    flat_ws = lambda t: t.reshape(8 * CHUNK, CHUNK)
    updated, _ = _adamw(
        "adamw_early",
        [(big_w[k], big_m[k], big_v[k], got_k) for k, got_k in zip(early, parts_early)]
        + [(flat_ws(ws), flat_ws(m_gm_ws), flat_ws(v_gm_ws), ws_all)], steps=2)
    big_out = dict(zip(early, updated[:3]))
    ws_out = updated[3]
    (pair_top, pair_bot, pair_xo), (land_top, land_bot, land_xo) = _split_wait(
        "grad_chip_exchange_wait", _chip_routes, chips_in_flight, [d_gmix] + [u[0] for u in updated])
    my_chip = (2 * px + py).astype(jnp.int32).reshape(1)
    (big_out["w_in"], big_out["w_xo"]), _ = _adamw(
        "adamw_late",
        [(big_w["w_in"], big_m["w_in"], big_v["w_in"], [land_top, land_bot], [pair_top, pair_bot]),
         (big_w["w_xo"], big_m["w_xo"], big_v["w_xo"], land_xo, pair_xo)],
        steps=4, own_slot=my_chip)
    (small,), (small_all,) = _split_wait("small_all_gather_wait", _gather_routes, small_in_flight, [big_out["w_xo"][0]])
    row = lambda t: t.reshape(1, D_MODEL)
    vec_out, conv_out, loss_row = _update_small(
        dev.astype(jnp.int32).reshape(1),
        [norm_mix_g, gm_ln_g, gm_ln_b, row(gm_bs), norm_x_g, norm_mem_g, row(norm_final_g)],
        [m_norm_mix_g, m_gm_ln_g, m_gm_ln_b, row(m_gm_bs), m_norm_x_g, m_norm_mem_g, row(m_norm_final_g)],
        [v_norm_mix_g, v_gm_ln_g, v_gm_ln_b, row(v_gm_bs), v_norm_x_g, v_norm_mem_g, row(v_norm_final_g)],
        conv_w, m_conv_w, v_conv_w, small, small_all)
    loss = loss_row[0, 0]

    def result(name, which):
        if name in big_out:
            return big_out[name][which][None]
        if name == "conv_w":
            return conv_out[which]
        if name == "gm_ws":
            return ws_out[which].reshape(1, 8, CHUNK, CHUNK)
        r = vec_out[vec_names.index(name)][which]
        if name == "gm_bs":
            return r.reshape(1, 8, CHUNK)
        if name == "norm_final_g":
            return r.reshape(D_MODEL)
        return r

    weights = ["norm_mix_g", "w_in", "conv_w", "gm_ln_g", "gm_ln_b", "gm_ws", "gm_bs", "w_out", "norm_x_g", "norm_mem_g", "w_q", "w_kv", "w_xo", "norm_final_g"]
    outs = [loss, grad_x[None]]
    for which in range(4):
        outs += [result(name, which) for name in weights]
    return tuple(outs)
```

```python
import math

import jax
import jax.numpy as jnp
from jax import lax
from jax.experimental import pallas as pl
from jax.experimental.pallas import tpu as pltpu

F32 = jnp.float32
BF16 = jnp.bfloat16
MESH = pl.DeviceIdType.MESH

D_MODEL = 1024
N_DEV = 8
N_CHIP = 4
CHUNK = 128
GROUPS = 7
IN_DIM = GROUPS * D_MODEL
SHARD_IN = IN_DIM // N_DEV
MIX_DIM = 2 * D_MODEL
X_HEADS = 4
X_HEAD_DIM = D_MODEL // X_HEADS
MEM_LEN = 256
EPS = 1e-6
GELU_K0 = math.sqrt(2.0 / math.pi)
GELU_K1 = 0.044715

ADAM_LR = 0.001
ADAM_B1 = 0.9
ADAM_B2 = 0.999
ADAM_EPS = 1e-08
ADAM_WD = 0.01
ADAM_STEP = 10

TM_INPROJ = 512
TM_ATTN = 512
SUB_ATTN = 256
TM_DH1 = 512
TM_DW = 512
TK_DW = 4096
TK_DW_NARROW = 2048
TR_PAIR_SUM = 1024


def _dot(a, b):
    return lax.dot_general(a, b, (((1,), (0,)), ((), ())), preferred_element_type=F32)


def _dot_nt(a, b):
    return lax.dot_general(a, b, (((1,), (1,)), ((), ())), preferred_element_type=F32)


def _rows8(v):
    return jnp.sum(v.reshape(v.shape[0] // 8, 8, v.shape[1]), axis=0)


def _gelu(x):
    x2 = x * x
    t = jnp.tanh(GELU_K0 * (x + GELU_K1 * (x * x2)))
    g = 0.5 * x * (1.0 + t)
    dg = 0.5 * (1.0 + t) + 0.5 * x * (1.0 - t * t) * (GELU_K0 * (1.0 + 3.0 * GELU_K1 * x2))
    return g, dg


def _silu(z):
    s = jax.nn.sigmoid(z)
    return z * s, s * (1.0 + z * (1.0 - s))


def _rms(v):
    r = lax.rsqrt(jnp.mean(v * v, axis=-1, keepdims=True) + EPS)
    return v * r, r


def _rms_bwd(dy_g, vh, r):
    return r * (dy_g - vh * jnp.mean(dy_g * vh, axis=-1, keepdims=True))


def _place():
    return lax.axis_index("x"), lax.axis_index("y"), lax.axis_index("c")


def _other_chips(x, y):
    return [(1 - x, y), (x, 1 - y), (1 - x, 1 - y)]


def _pow2_divisor(n):
    return n & (-n)


def _shard_view(ref, axis, size, d):
    start = d * size
    align = _pow2_divisor(size)
    if align > 1:
        start = pl.multiple_of(start, align)
    idx = [slice(None)] * len(ref.shape)
    idx[axis] = pl.ds(start, size)
    return ref.at[tuple(idx)]


class _Plan:
    def __init__(self, ins, out_shape, sems, start, finish, mid=None, aliases=None):
        self.ins, self.out_shape, self.sems = list(ins), list(out_shape), list(sems)
        self.start, self.mid, self.finish = start, mid, finish
        self.aliases = dict(aliases or {})


def _gather_plan(shards, specs):
    n = len(shards)
    n_copy = 7

    def copies(ins, outs, sems):
        send_sems, recv_sems, local_sems = sems
        x, y, c = _place()
        me, sibling = (x, y, c), (x, y, 1 - c)
        chips = _other_chips(x, y)

        def block(a, p):
            _, axis, size = specs[a]
            return _shard_view(outs[a], axis, size, 4 * p[0] + 2 * p[1] + p[2])

        def copy(a, k, p, to, src=None):
            return pltpu.make_async_remote_copy(
                src_ref=block(a, p) if src is None else src,
                dst_ref=block(a, p),
                send_sem=send_sems.at[a * n_copy + k],
                recv_sem=recv_sems.at[a * n_copy + k],
                device_id=to,
                device_id_type=MESH,
            )

        def mine():
            return [pltpu.make_async_copy(ins[a], block(a, me), local_sems.at[a]) for a in range(n)]

        def first():
            return [cp for a in range(n) for cp in
                    [copy(a, 0, me, sibling, src=ins[a])] + [copy(a, 1 + j, me, (*chips[j], c), src=ins[a]) for j in range(2)]]

        def relay():
            pick = lambda north, south: (c * north[0] + (1 - c) * south[0], c * north[1] + (1 - c) * south[1], c)
            return [copy(a, 3, pick(chips[0], chips[1]), pick(chips[1], chips[0])) for a in range(n)]

        def landed(js):
            return [copy(a, 1 + j, (*chips[j], c), me) for j in js for a in range(n)]

        def passed(js):
            return [copy(a, 4 + j, (*chips[j], c), sibling) for j in js for a in range(n)]

        def last():
            return [cp for a in range(n) for cp in
                    [copy(a, 0, sibling, me)] + [copy(a, 4 + j, (*chip, 1 - c), me) for j, chip in enumerate(chips)]]

        return mine, first, relay, landed, passed, last

    def start(ins, outs, sems):
        mine, first, _, _, _, _ = copies(ins, outs, sems)
        for cp in mine() + first():
            cp.start()

    def mid(ins, outs, sems):
        _, _, relay, landed, passed, _ = copies(ins, outs, sems)
        for got, fwd in zip(landed((0, 1)), passed((0, 1))):
            got.wait_recv()
            fwd.start()
        for cp in relay():
            cp.start()

    def finish(ins, outs, sems):
        mine, first, relay, landed, passed, last = copies(ins, outs, sems)
        for got, fwd in zip(landed((2,)), passed((2,))):
            got.wait_recv()
            fwd.start()
        for cp in last():
            cp.wait_recv()
        for cp in first() + relay() + passed((0, 1, 2)):
            cp.wait_send()
        for cp in mine():
            cp.wait()

    return _Plan(
        shards,
        [jax.ShapeDtypeStruct(full, s.dtype) for s, (full, _, _) in zip(shards, specs)],
        [pltpu.SemaphoreType.DMA((n * n_copy,)), pltpu.SemaphoreType.DMA((n * n_copy,)), pltpu.SemaphoreType.DMA((n,))],
        start, finish, mid,
    )


def _sibling_plan(grads, specs):
    n = len(grads)

    def shard_shape(a):
        axis, size = specs[a]
        shp = list(grads[a].shape)
        shp[axis] = size
        return tuple(shp)

    def copies(ins, outs, sems):
        send_sems, recv_sems = sems
        x, y, c = _place()
        return [
            pltpu.make_async_remote_copy(
                src_ref=_shard_view(ins[a], specs[a][0], specs[a][1], 2 * q + (1 - c)),
                dst_ref=outs[a].at[q],
                send_sem=send_sems.at[a * N_CHIP + q],
                recv_sem=recv_sems.at[a * N_CHIP + q],
                device_id=(x, y, 1 - c),
                device_id_type=MESH,
            )
            for a in range(n) for q in range(N_CHIP)
        ]

    def start(ins, outs, sems):
        for cp in copies(ins, outs, sems):
            cp.start()

    def finish(ins, outs, sems):
        for cp in copies(ins, outs, sems):
            cp.wait()

    return _Plan(
        grads,
        [jax.ShapeDtypeStruct((N_CHIP, *shard_shape(a)), grads[a].dtype) for a in range(n)],
        [pltpu.SemaphoreType.DMA((n * N_CHIP,)), pltpu.SemaphoreType.DMA((n * N_CHIP,))],
        start, finish,
    )


def _chip_plan(parts, full_rows=None, row0=None, into=None):
    n = len(parts)
    full_rows = [p.shape[1] for p in parts] if full_rows is None else full_rows
    row0 = [0] * n if row0 is None else row0

    def copies(ins, outs, sems):
        send_sems, recv_sems, local_sems = sems
        x, y, c = _place()
        my_chip = 2 * x + y
        chips = _other_chips(x, y)

        def land(a, q):
            return outs[a].at[q, pl.ds(row0[a], parts[a].shape[1]), :]

        def mine():
            return [pltpu.make_async_copy(ins[a].at[my_chip], land(a, my_chip), local_sems.at[a]) for a in range(n)]

        def remote(sending):
            return [
                pltpu.make_async_remote_copy(
                    src_ref=ins[a].at[2 * chip[0] + chip[1]],
                    dst_ref=land(a, my_chip if sending else 2 * chip[0] + chip[1]),
                    send_sem=send_sems.at[a * 3 + j], recv_sem=recv_sems.at[a * 3 + j],
                    device_id=(*chip, c), device_id_type=MESH)
                for a in range(n) for j, chip in enumerate(chips)
            ]

        return mine, remote

    def start(ins, outs, sems):
        mine, remote = copies(ins, outs, sems)
        for cp in mine() + remote(True):
            cp.start()

    def finish(ins, outs, sems):
        mine, remote = copies(ins, outs, sems)
        for cp in remote(False):
            cp.wait_recv()
        for cp in remote(True):
            cp.wait_send()
        for cp in mine():
            cp.wait()

    return _Plan(
        list(parts) + (list(into) if into is not None else []),
        [jax.ShapeDtypeStruct((N_CHIP, full_rows[a], parts[a].shape[2]), parts[a].dtype) for a in range(n)],
        [pltpu.SemaphoreType.DMA((n * 3,)), pltpu.SemaphoreType.DMA((n * 3,)), pltpu.SemaphoreType.DMA((n,))],
        start, finish,
        aliases={n + a: a for a in range(n)} if into is not None else None,
    )


def _call(body, *, name, operands, out_shape, grid=(), in_specs=None, out_specs=None, scratch_shapes=(),
          plans=(), aliases=None, prefetch=(), start_after_body=False, pass_on_at=0.6):
    operands, out_shape, scratch_shapes = list(operands), list(out_shape), list(scratch_shapes)
    n_pf, n_in, n_out, n_scr = len(prefetch), len(operands), len(out_shape), len(scratch_shapes)
    in_vmem = pl.BlockSpec(memory_space=pltpu.VMEM)
    in_hbm = pl.BlockSpec(memory_space=pl.ANY)
    in_specs = [in_vmem] * n_in if in_specs is None else list(in_specs)
    out_specs = [in_vmem] * n_out if out_specs is None else list(out_specs)
    io_alias = {n_pf + i: o for i, o in (aliases or {}).items()}
    pos_in, pos_out = n_pf + n_in, n_out
    for p in plans:
        for i, o in p.aliases.items():
            io_alias[pos_in + i] = pos_out + o
        pos_in += len(p.ins)
        pos_out += len(p.out_shape)
    steps = math.prod(grid) if grid else 1
    mid_step = min(steps - 1, int(steps * pass_on_at))

    def wrapped(*refs):
        refs = list(refs)
        take = lambda k: [refs.pop(0) for _ in range(k)]
        pf = take(n_pf)
        ins, p_ins = take(n_in), [take(len(p.ins)) for p in plans]
        outs, p_outs = take(n_out), [take(len(p.out_shape)) for p in plans]
        scr, p_sems = take(n_scr), [take(len(p.sems)) for p in plans]
        step = 0
        for ax, g in enumerate(grid):
            step = step * g + pl.program_id(ax)

        def hook(kind, at):
            todo = [(getattr(p, kind), a, b, c) for p, a, b, c in zip(plans, p_ins, p_outs, p_sems) if getattr(p, kind)]

            def run():
                for fn, a, b, c in todo:
                    fn(a, b, c)

            if todo and grid:
                pl.when(step == at)(run)
            elif todo:
                run()

        if not start_after_body:
            hook("start", 0)
        if body is not None:
            body(*pf, *ins, *outs, *scr)
        if start_after_body:
            hook("start", 0)
        hook("mid", mid_step)
        hook("finish", steps - 1)

    layout = dict(
        grid=grid,
        in_specs=in_specs + [in_hbm] * sum(len(p.ins) for p in plans),
        out_specs=out_specs + [in_hbm] * sum(len(p.out_shape) for p in plans),
        scratch_shapes=scratch_shapes + [s for p in plans for s in p.sems],
    )
    if n_pf:
        layout = dict(grid_spec=pltpu.PrefetchScalarGridSpec(num_scalar_prefetch=n_pf, **layout))
    res = pl.pallas_call(
        wrapped,
        name=name,
        out_shape=out_shape + [o for p in plans for o in p.out_shape],
        input_output_aliases=io_alias,
        compiler_params=pltpu.CompilerParams(dimension_semantics=("arbitrary",) * len(grid)) if grid else None,
        **layout,
    )(*prefetch, *operands, *[a for p in plans for a in p.ins])
    res = list(res)
    outs, plan_outs = res[:n_out], []
    pos = n_out
    for p in plans:
        plan_outs.append(res[pos : pos + len(p.out_shape)])
        pos += len(p.out_shape)
    return outs, plan_outs


def _inproj(x, g, w_shard, order, plans=()):
    s = x.shape[0]
    tm = min(TM_INPROJ, s)
    nt = s // tm
    wide = 2 * SHARD_IN
    fwd_at = nt // 2

    def body(order_ref, x_ref, g_ref, w_hbm, proj_ref, h1_ref, h1t_ref, wfull_hbm,
             w_scr, h1_scr, send_sems, recv_sems, local_sems):
        k, i = pl.program_id(0), pl.program_id(1)
        px, py, c = _place()
        me, sibling = (px, py, c), (px, py, 1 - c)
        chips = _other_chips(px, py)

        def half(slot, core):
            return w_scr.at[slot, :, pl.ds(pl.multiple_of(core * SHARD_IN, 128), SHARD_IN)]

        def copy(n, slot, core, to, src=None):
            return pltpu.make_async_remote_copy(
                src_ref=half(slot, core) if src is None else src, dst_ref=half(slot, core),
                send_sem=send_sems.at[n], recv_sem=recv_sems.at[n], device_id=to, device_id_type=MESH)

        def mine():
            return pltpu.make_async_copy(w_hbm, half(0, c), local_sems.at[0])

        def first():
            return [copy(0, 0, c, sibling, src=w_hbm)] + [copy(1 + j, 1 + j, c, (*chips[j], c), src=w_hbm) for j in range(2)]

        def relay():
            to = (c * px + (1 - c) * (1 - px), c * (1 - py) + (1 - c) * py, c)
            return pltpu.make_async_remote_copy(
                src_ref=half(2 - c, c), dst_ref=half(3, c), send_sem=send_sems.at[3], recv_sem=recv_sems.at[3],
                device_id=to, device_id_type=MESH)

        def save(slot):
            cols = pl.ds(pl.multiple_of(order_ref[slot] * wide, 128), wide)
            return pltpu.make_async_copy(w_scr.at[slot], wfull_hbm.at[:, cols], local_sems.at[1 + slot])

        @pl.when((k == 0) & (i == 0))
        def _():
            mine().start()
            for cp in first():
                cp.start()
            mine().wait()
            copy(0, 0, 1 - c, me).wait_recv()

        for j in range(3):
            @pl.when((k == j + 1) & (i == 0))
            def _():
                copy(4 + j, 1 + j, 1 - c, me).wait_recv()

        @pl.when((k == 3) & (i == 0))
        def _():
            for slot in range(N_CHIP):
                save(slot).start()

        @pl.when(k == 0)
        def _():
            xh, _ = _rms(x_ref[...])
            h = xh * g_ref[...]
            h1_ref[...] = h.astype(BF16)
            h1t_ref[...] = h.T.astype(BF16)
            h1_scr[pl.ds(pl.multiple_of(i * tm, tm), tm), :] = h.astype(BF16)

        proj_ref[...] = _dot(h1_scr[pl.ds(pl.multiple_of(i * tm, tm), tm), :], w_scr[k]).astype(BF16)

        for j in range(3):
            @pl.when((k == 0) & (i == nt - 1) if j < 2 else (k == 2) & (i == fwd_at))
            def _():
                copy(1 + j, 1 + j, c, me).wait_recv()
                copy(4 + j, 1 + j, c, sibling).start()
                if j < 2:
                    pl.when(c == 1 - j)(lambda: relay().start())

        @pl.when((k == 3) & (i == nt - 1))
        def _():
            for cp in first() + [relay()] + [copy(4 + j, 1 + j, c, sibling) for j in range(3)]:
                cp.wait_send()
            for slot in range(N_CHIP):
                save(slot).wait()

    once = lambda k, i, order_ref: (jnp.where(k == 0, i, nt - 1), 0)
    return _call(
        body,
        name="inproj",
        prefetch=[order],
        operands=[x, g, w_shard],
        grid=(N_CHIP, nt),
        in_specs=[
            pl.BlockSpec((tm, D_MODEL), once),
            pl.BlockSpec((1, D_MODEL), lambda k, i, order_ref: (0, 0)),
            pl.BlockSpec(memory_space=pl.ANY),
        ],
        out_specs=[
            pl.BlockSpec((tm, wide), lambda k, i, order_ref: (i, order_ref[k])),
            pl.BlockSpec((tm, D_MODEL), once),
            pl.BlockSpec((D_MODEL, tm), lambda k, i, order_ref: (0, jnp.where(k == 0, i, nt - 1))),
            pl.BlockSpec(memory_space=pl.ANY),
        ],
        out_shape=[
            jax.ShapeDtypeStruct((s, IN_DIM), BF16),
            jax.ShapeDtypeStruct((s, D_MODEL), BF16),
            jax.ShapeDtypeStruct((D_MODEL, s), BF16),
            jax.ShapeDtypeStruct((D_MODEL, IN_DIM), BF16),
        ],
        scratch_shapes=[
            pltpu.VMEM((N_CHIP, D_MODEL, wide), BF16),
            pltpu.VMEM((s, D_MODEL), BF16),
            pltpu.SemaphoreType.DMA((7,)),
            pltpu.SemaphoreType.DMA((7,)),
            pltpu.SemaphoreType.DMA((1 + N_CHIP,)),
        ],
        plans=plans,
        start_after_body=True,
        pass_on_at=0.8,
    )


def _causal_mask():
    row = lax.broadcasted_iota(jnp.int32, (CHUNK, CHUNK), 0)
    col = lax.broadcasted_iota(jnp.int32, (CHUNK, CHUNK), 1)
    return row, row >= col


def _mixer_fwd(proj, conv_w, ln_g, ln_b, ws, bs_b, plans=()):
    s = proj.shape[0]
    nt = s // CHUNK

    def body(proj_ref, cw_ref, lng_ref, lnb_ref, ws_ref, bsb_ref, cat_ref, pcar_ref):
        @pl.when(pl.program_id(0) == 0)
        def _():
            pcar_ref[...] = jnp.zeros_like(pcar_ref)

        row, tril = _causal_mask()
        for j in range(8):
            cs = slice(CHUNK * j, CHUNK * (j + 1))

            def grp(k):
                return proj_ref[:, k * D_MODEL + CHUNK * j : k * D_MODEL + CHUNK * (j + 1)].astype(F32)

            w = cw_ref[:, cs]
            p = grp(1) * grp(2)
            pc = pcar_ref[:, cs]
            p1 = jnp.where(row == 0, pc[7:8], pltpu.roll(p, 1, 0))
            p2 = jnp.where(row == 0, pc[6:7], jnp.where(row == 1, pc[7:8], pltpu.roll(p, 2, 0)))
            pcar_ref[:, cs] = p[CHUNK - 8 :]
            cv = w[0:1] * p2 + w[1:2] * p1 + w[2:3] * p
            sa, _ = _silu(grp(3))
            cat_ref[:, cs] = ((grp(0) * cv) * sa).astype(BF16)
            gu, _ = _gelu(grp(4))
            gv, _ = _gelu(grp(5))
            dv = gv - jnp.mean(gv, axis=-1, keepdims=True)
            vn = dv * lax.rsqrt(jnp.mean(dv * dv, axis=-1, keepdims=True) + EPS)
            vn = vn * lng_ref[:, cs] + lnb_ref[:, cs]
            wc = jnp.where(tril, ws_ref[j], 0.0).astype(BF16)
            sp = _dot(wc, vn.astype(BF16)) + bsb_ref[j]
            sb, _ = _silu(grp(6))
            cat_ref[:, D_MODEL + CHUNK * j : D_MODEL + CHUNK * (j + 1)] = ((gu * sp) * sb).astype(BF16)

    whole = lambda shape: pl.BlockSpec(shape, lambda i: (0,) * len(shape))
    (cat,), plan_outs = _call(
        body,
        name="mixer_fwd",
        operands=[proj, conv_w, ln_g, ln_b, ws, bs_b],
        grid=(nt,),
        in_specs=[
            pl.BlockSpec((CHUNK, IN_DIM), lambda i: (i, 0)),
            whole((8, D_MODEL)),
            whole((1, D_MODEL)),
            whole((1, D_MODEL)),
            whole((8, CHUNK, CHUNK)),
            whole((8, CHUNK, CHUNK)),
        ],
        out_specs=[pl.BlockSpec((CHUNK, MIX_DIM), lambda i: (i, 0))],
        out_shape=[jax.ShapeDtypeStruct((s, MIX_DIM), BF16)],
        scratch_shapes=[pltpu.VMEM((8, D_MODEL), F32)],
        plans=plans,
        pass_on_at=0.85,
    )
    return cat, plan_outs


def _mixer_bwd(proj, dcat, conv_w, ln_g, ln_b, ws, bs_b, plans=()):
    s = proj.shape[0]
    nt = s // CHUNK

    def body(proj_ref, halo_ref, dcat_ref, cw_ref, lng_ref, lnb_ref, ws_ref, bsb_ref,
             dproj_ref, dcw_ref, dlng_ref, dlnb_ref, dws_ref, dbs_ref,
             car_ref, acc_cw, acc_lng, acc_lnb, acc_bs):
        i = pl.program_id(0)
        tile = nt - 1 - i

        @pl.when(i == 0)
        def _():
            car_ref[...] = jnp.zeros_like(car_ref)
            acc_cw[...] = jnp.zeros_like(acc_cw)
            acc_lng[...] = jnp.zeros_like(acc_lng)
            acc_lnb[...] = jnp.zeros_like(acc_lnb)
            acc_bs[...] = jnp.zeros_like(acc_bs)
            dws_ref[...] = jnp.zeros_like(dws_ref)

        row, tril = _causal_mask()
        has_prev = jnp.where(tile > 0, 1.0, 0.0).astype(F32)
        for j in range(8):
            cs = slice(CHUNK * j, CHUNK * (j + 1))

            def col(k):
                return slice(k * D_MODEL + CHUNK * j, k * D_MODEL + CHUNK * (j + 1))

            def grp(k):
                return proj_ref[:, col(k)].astype(F32)

            w = cw_ref[:, cs]
            gb, gc, xa = grp(0), grp(1), grp(2)
            p = gc * xa
            pprev = halo_ref[:, col(1)].astype(F32) * halo_ref[:, col(2)].astype(F32) * has_prev
            p1 = jnp.where(row == 0, pprev[15:16], pltpu.roll(p, 1, 0))
            p2 = jnp.where(row == 0, pprev[14:15], jnp.where(row == 1, pprev[15:16], pltpu.roll(p, 2, 0)))
            cv = w[0:1] * p2 + w[1:2] * p1 + w[2:3] * p
            za = grp(3)
            sa, dsa = _silu(za)
            da = dcat_ref[:, cs].astype(F32)
            dproj_ref[:, col(0)] = (da * cv * sa).astype(BF16)
            dproj_ref[:, col(3)] = (da * gb * cv * dsa).astype(BF16)
            dcv = da * gb * sa
            nxt = car_ref[:, cs]
            d1 = jnp.where(row == CHUNK - 1, nxt[0:1], pltpu.roll(dcv, CHUNK - 1, 0))
            d2 = jnp.where(row == CHUNK - 2, nxt[0:1], jnp.where(row == CHUNK - 1, nxt[1:2], pltpu.roll(dcv, CHUNK - 2, 0)))
            car_ref[:, cs] = dcv[0:8]
            dp = w[2:3] * dcv + w[1:2] * d1 + w[0:1] * d2
            dproj_ref[:, col(1)] = (dp * xa).astype(BF16)
            dproj_ref[:, col(2)] = (dp * gc).astype(BF16)
            acc_cw[0, :, cs] += _rows8(dcv * p2)
            acc_cw[1, :, cs] += _rows8(dcv * p1)
            acc_cw[2, :, cs] += _rows8(dcv * p)
            gu, dgu = _gelu(grp(4))
            gv, dgv = _gelu(grp(5))
            dv = gv - jnp.mean(gv, axis=-1, keepdims=True)
            rstd = lax.rsqrt(jnp.mean(dv * dv, axis=-1, keepdims=True) + EPS)
            vnh = dv * rstd
            lng = lng_ref[:, cs]
            vnb = (vnh * lng + lnb_ref[:, cs]).astype(BF16)
            wcf = jnp.where(tril, ws_ref[j], 0.0)
            sp = _dot(wcf.astype(BF16), vnb) + bsb_ref[j]
            zb = grp(6)
            sb, dsb = _silu(zb)
            db = dcat_ref[:, D_MODEL + CHUNK * j : D_MODEL + CHUNK * (j + 1)].astype(F32)
            dproj_ref[:, col(4)] = (db * sp * sb * dgu).astype(BF16)
            dproj_ref[:, col(6)] = (db * gu * sp * dsb).astype(BF16)
            dsp = db * gu * sb
            acc_bs[j] += dsp
            dspb = dsp.astype(BF16)
            dws_ref[j] += _dot_nt(dspb, vnb)
            dvn = _dot(wcf.T.astype(BF16), dspb)
            acc_lnb[:, cs] += _rows8(dvn)
            acc_lng[:, cs] += _rows8(dvn * vnh)
            dvh = dvn * lng
            dgvv = rstd * (dvh - jnp.mean(dvh, axis=-1, keepdims=True) - vnh * jnp.mean(dvh * vnh, axis=-1, keepdims=True))
            dproj_ref[:, col(5)] = (dgvv * dgv).astype(BF16)

        @pl.when(i == nt - 1)
        def _():
            dlng_ref[...] = jnp.sum(acc_lng[...], axis=0, keepdims=True)
            dlnb_ref[...] = jnp.sum(acc_lnb[...], axis=0, keepdims=True)
            dcw_ref[...] = jnp.zeros_like(dcw_ref)
            for k in range(3):
                dcw_ref[k : k + 1, :] = jnp.sum(acc_cw[k], axis=0, keepdims=True)
            for j in range(8):
                dws_ref[j] = jnp.where(tril, dws_ref[j], 0.0)
                dbs_ref[:, CHUNK * j : CHUNK * (j + 1)] = jnp.sum(acc_bs[j].T, axis=0, keepdims=True)

    whole = lambda shape: pl.BlockSpec(shape, lambda i: (0,) * len(shape))
    halo_rows = 16
    per = CHUNK // halo_rows
    return _call(
        body,
        name="mixer_bwd",
        operands=[proj, proj, dcat, conv_w, ln_g, ln_b, ws, bs_b],
        plans=plans,
        grid=(nt,),
        in_specs=[
            pl.BlockSpec((CHUNK, IN_DIM), lambda i: (nt - 1 - i, 0)),
            pl.BlockSpec((halo_rows, IN_DIM), lambda i: (jnp.maximum((nt - 1 - i) * per - 1, 0), 0)),
            pl.BlockSpec((CHUNK, MIX_DIM), lambda i: (nt - 1 - i, 0)),
            whole((8, D_MODEL)),
            whole((1, D_MODEL)),
            whole((1, D_MODEL)),
            whole((8, CHUNK, CHUNK)),
            whole((8, CHUNK, CHUNK)),
        ],
        out_specs=[
            pl.BlockSpec((CHUNK, IN_DIM), lambda i: (nt - 1 - i, 0)),
            whole((8, D_MODEL)),
            whole((1, D_MODEL)),
            whole((1, D_MODEL)),
            whole((8, CHUNK, CHUNK)),
            whole((1, D_MODEL)),
        ],
        out_shape=[
            jax.ShapeDtypeStruct((s, IN_DIM), BF16),
            jax.ShapeDtypeStruct((8, D_MODEL), F32),
            jax.ShapeDtypeStruct((1, D_MODEL), F32),
            jax.ShapeDtypeStruct((1, D_MODEL), F32),
            jax.ShapeDtypeStruct((8, CHUNK, CHUNK), F32),
            jax.ShapeDtypeStruct((1, D_MODEL), F32),
        ],
        scratch_shapes=[
            pltpu.VMEM((8, D_MODEL), F32),
            pltpu.VMEM((3, 8, D_MODEL), F32),
            pltpu.VMEM((8, D_MODEL), F32),
            pltpu.VMEM((8, D_MODEL), F32),
            pltpu.VMEM((8, CHUNK, CHUNK), F32),
        ],
    )


def _kv_fwd(mem, g_mem, w_kv):
    def body(mem_ref, g_ref, w_ref, k_ref, vv_ref, mt_ref):
        mh, _ = _rms(mem_ref[...])
        m = mh * g_ref[...]
        kv = _dot(m.astype(BF16), w_ref[...])
        k_ref[...] = kv[:, :D_MODEL].astype(BF16)
        vv_ref[...] = kv[:, D_MODEL:].astype(BF16)
        mt_ref[...] = m.T.astype(BF16)

    return pl.pallas_call(
        body,
        name="kv_fwd",
        out_shape=[
            jax.ShapeDtypeStruct((MEM_LEN, D_MODEL), BF16),
            jax.ShapeDtypeStruct((MEM_LEN, D_MODEL), BF16),
            jax.ShapeDtypeStruct((D_MODEL, MEM_LEN), BF16),
        ],
    )(mem, g_mem, w_kv)


def _kv_bwd(mem, w_kv, mt, dk, dvv):
    def body(mem_ref, w_ref, mt_ref, dk_ref, dvv_ref, dw_ref, dg_ref):
        dkv = jnp.concatenate([dk_ref[...], dvv_ref[...]], axis=1).astype(BF16)
        dw_ref[...] = _dot(mt_ref[...], dkv)
        dm = _dot_nt(dkv, w_ref[...])
        mh, _ = _rms(mem_ref[...])
        dg_ref[...] = jnp.sum(dm * mh, axis=0, keepdims=True)

    return pl.pallas_call(
        body,
        name="kv_bwd",
        out_shape=[
            jax.ShapeDtypeStruct((D_MODEL, 2 * D_MODEL), F32),
            jax.ShapeDtypeStruct((1, D_MODEL), F32),
        ],
    )(mem, w_kv, mt, dk, dvv)


def _attn(x, cat, target, w_out, w_q, w_xo, k, vv, g_x, g_f):
    s = x.shape[0]
    tm = min(TM_ATTN, s)
    sub = min(SUB_ATTN, tm)
    nt = s // tm
    scale = 1.0 / math.sqrt(X_HEAD_DIM)

    def body(x_ref, cat_ref, t_ref, wout_ref, wq_ref, wxo_ref, k_ref, vv_ref, gx_ref, gf_ref,
             dcat_ref, catt_ref, h2t_ref, ot_ref, dx2b_ref, dqb_ref, dx1b_ref,
             dk_ref, dvv_ref, dgf_ref, dgx_ref, loss_ref,
             p_scr, q_scr, o_scr, dq_scr, acc_gf, acc_gx, acc_loss):
        i = pl.program_id(0)

        @pl.when(i == 0)
        def _():
            dk_ref[...] = jnp.zeros_like(dk_ref)
            dvv_ref[...] = jnp.zeros_like(dvv_ref)
            acc_gf[...] = jnp.zeros_like(acc_gf)
            acc_gx[...] = jnp.zeros_like(acc_gx)
            acc_loss[...] = jnp.zeros_like(acc_loss)

        subs = [slice(u * sub, (u + 1) * sub) for u in range(tm // sub)]
        heads = [slice(X_HEAD_DIM * h, X_HEAD_DIM * (h + 1)) for h in range(X_HEADS)]
        gx, gf = gx_ref[...], gf_ref[...]
        x1, x1h, r2 = [], [], []
        for rs in subs:
            cat = cat_ref[rs, :]
            catt_ref[:, rs] = cat.astype(F32).T.astype(BF16)
            v = x_ref[rs, :] + _dot(cat, wout_ref[...])
            vh, r = _rms(v)
            h2 = vh * gx
            h2t_ref[:, rs] = h2.T.astype(BF16)
            q_scr[rs, :] = _dot(h2.astype(BF16), wq_ref[...]).astype(BF16)
            x1.append(v), x1h.append(vh), r2.append(r)
        for rs in subs:
            for hs in heads:
                sc = _dot_nt(q_scr[rs, hs], k_ref[:, hs]) * scale
                e = jnp.exp(sc - jnp.max(sc, axis=-1, keepdims=True))
                p = e / jnp.sum(e, axis=-1, keepdims=True)
                p_scr[rs, hs] = p
                o_scr[rs, hs] = _dot(p.astype(BF16), vv_ref[:, hs])
        dx2, sq, d_gf = [], None, None
        for u, rs in enumerate(subs):
            o = o_scr[rs, :]
            ot_ref[:, rs] = o.T.astype(BF16)
            x2 = x1[u] + _dot(o.astype(BF16), wxo_ref[...])
            x2h, r3 = _rms(x2)
            err = x2h * gf - t_ref[rs, :]
            dy = err * (1.0 / D_MODEL)
            sq = _rows8(err * err) if sq is None else sq + _rows8(err * err)
            d_gf = _rows8(dy * x2h) if d_gf is None else d_gf + _rows8(dy * x2h)
            dx2.append(_rms_bwd(dy * gf, x2h, r3))
        acc_loss[...] += sq
        acc_gf[...] += d_gf
        d_vv, d_k = [None] * X_HEADS, [None] * X_HEADS
        for u, rs in enumerate(subs):
            dx2b = dx2[u].astype(BF16)
            dx2b_ref[rs, :] = dx2b
            dob = _dot_nt(dx2b, wxo_ref[...]).astype(BF16)
            for h, hs in enumerate(heads):
                p = p_scr[rs, hs]
                dp = _dot_nt(dob[:, hs], vv_ref[:, hs])
                ds = p * (dp - jnp.sum(dp * p, axis=-1, keepdims=True)) * scale
                dq_scr[rs, hs] = _dot(ds.astype(BF16), k_ref[:, hs]).astype(BF16)
                dvv_h = _dot(p.T.astype(BF16), dob[:, hs])
                dk_h = _dot(ds.T.astype(BF16), q_scr[rs, hs])
                d_vv[h] = dvv_h if d_vv[h] is None else d_vv[h] + dvv_h
                d_k[h] = dk_h if d_k[h] is None else d_k[h] + dk_h
        for h, hs in enumerate(heads):
            dvv_ref[:, hs] += d_vv[h]
            dk_ref[:, hs] += d_k[h]
        d_gx = None
        for u, rs in enumerate(subs):
            dqb = dq_scr[rs, :]
            dqb_ref[rs, :] = dqb
            dh2 = _dot_nt(dqb, wq_ref[...])
            d_gx = _rows8(dh2 * x1h[u]) if d_gx is None else d_gx + _rows8(dh2 * x1h[u])
            dx1 = dx2[u] + _rms_bwd(dh2 * gx, x1h[u], r2[u])
            dx1b = dx1.astype(BF16)
            dx1b_ref[rs, :] = dx1b
            dcat_ref[rs, :] = _dot_nt(dx1b, wout_ref[...]).astype(BF16)
        acc_gx[...] += d_gx

        @pl.when(i == nt - 1)
        def _():
            dgf_ref[...] = jnp.sum(acc_gf[...], axis=0, keepdims=True)
            dgx_ref[...] = jnp.sum(acc_gx[...], axis=0, keepdims=True)
            loss_ref[...] = jnp.sum(acc_loss[...], axis=0, keepdims=True)

    whole = lambda shape: pl.BlockSpec(shape, lambda i: (0,) * len(shape))
    once = lambda shape: pl.BlockSpec(shape, lambda i: (0,) * len(shape), pipeline_mode=pl.Buffered(1))
    rows = lambda width: pl.BlockSpec((tm, width), lambda i: (i, 0))
    cols = lambda height: pl.BlockSpec((height, tm), lambda i: (0, i))
    vec = jax.ShapeDtypeStruct((1, D_MODEL), F32)
    return pl.pallas_call(
        body,
        name="attn",
        grid=(nt,),
        in_specs=[
            rows(D_MODEL), rows(MIX_DIM), rows(D_MODEL),
            once((MIX_DIM, D_MODEL)), once((D_MODEL, D_MODEL)), once((D_MODEL, D_MODEL)),
            once((MEM_LEN, D_MODEL)), once((MEM_LEN, D_MODEL)),
            whole((1, D_MODEL)), whole((1, D_MODEL)),
        ],
        out_specs=[
            rows(MIX_DIM), cols(MIX_DIM), cols(D_MODEL), cols(D_MODEL),
            rows(D_MODEL), rows(D_MODEL), rows(D_MODEL),
            whole((MEM_LEN, D_MODEL)), whole((MEM_LEN, D_MODEL)),
            whole((1, D_MODEL)), whole((1, D_MODEL)), whole((1, D_MODEL)),
        ],
        out_shape=[
            jax.ShapeDtypeStruct((s, MIX_DIM), BF16),
            jax.ShapeDtypeStruct((MIX_DIM, s), BF16),
            jax.ShapeDtypeStruct((D_MODEL, s), BF16),
            jax.ShapeDtypeStruct((D_MODEL, s), BF16),
            jax.ShapeDtypeStruct((s, D_MODEL), BF16),
            jax.ShapeDtypeStruct((s, D_MODEL), BF16),
            jax.ShapeDtypeStruct((s, D_MODEL), BF16),
            jax.ShapeDtypeStruct((MEM_LEN, D_MODEL), F32),
            jax.ShapeDtypeStruct((MEM_LEN, D_MODEL), F32),
            vec, vec, vec,
        ],
        scratch_shapes=[
            pltpu.VMEM((tm, D_MODEL), F32),
            pltpu.VMEM((tm, D_MODEL), BF16),
            pltpu.VMEM((tm, D_MODEL), F32),
            pltpu.VMEM((tm, D_MODEL), BF16),
            pltpu.VMEM((8, D_MODEL), F32),
            pltpu.VMEM((8, D_MODEL), F32),
            pltpu.VMEM((8, D_MODEL), F32),
        ],
        compiler_params=pltpu.CompilerParams(dimension_semantics=("arbitrary",)),
    )(x, cat, target, w_out, w_q, w_xo, k, vv, g_x, g_f)


def _matmul_f32(name, a, b, rows=None, plans=()):
    first, m = (0, a.shape[0]) if rows is None else rows
    kk = a.shape[1]
    n = b.shape[1]
    tm, tn, tk = min(m, TM_DW), min(n, D_MODEL), min(kk, TK_DW if n > D_MODEL else TK_DW_NARROW)
    nk = kk // tk
    i0 = first // tm

    def body(a_ref, b_ref, o_ref):
        prod = _dot(a_ref[...], b_ref[...])
        if nk == 1:
            o_ref[...] = prod
            return
        kid = pl.program_id(2)

        @pl.when(kid == 0)
        def _():
            o_ref[...] = prod

        @pl.when(kid > 0)
        def _():
            o_ref[...] += prod

    (out,), plan_outs = _call(
        body,
        name=name,
        operands=[a, b],
        grid=(m // tm, n // tn, nk),
        in_specs=[pl.BlockSpec((tm, tk), lambda i, j, k: (i + i0, k)), pl.BlockSpec((tk, tn), lambda i, j, k: (k, j))],
        out_specs=[pl.BlockSpec((tm, tn), lambda i, j, k: (i, j))],
        out_shape=[jax.ShapeDtypeStruct((m, n), F32)],
        plans=plans,
    )
    return out, plan_outs


def _dh1(name, dproj, w_in, x, dx1, g, tiles, into=None, after=(), plans=()):
    s = x.shape[0]
    tm = min(TM_DH1, s)
    t0, nt = tiles[0], tiles[1] - tiles[0]

    def body(dp_ref, w_ref, x_ref, dx1_ref, g_ref, *rest):
        gx_ref, dg_ref, acc_g = rest[-3:]
        i = pl.program_id(0)

        @pl.when(i == 0)
        def _():
            acc_g[...] = jnp.zeros_like(acc_g)

        dh1 = _dot_nt(dp_ref[...], w_ref[...])
        xh, r = _rms(x_ref[...])
        acc_g[...] += _rows8(dh1 * xh)
        gx_ref[...] = dx1_ref[...].astype(F32) + _rms_bwd(dh1 * g_ref[...], xh, r)

        @pl.when(i == nt - 1)
        def _():
            dg_ref[...] = jnp.sum(acc_g[...], axis=0, keepdims=True)

    rows = pl.BlockSpec((tm, D_MODEL), lambda i: (i + t0, 0))
    return _call(
        body,
        name=name,
        operands=[dproj, w_in, x, dx1, g] + ([into] if into is not None else []) + list(after),
        grid=(nt,),
        in_specs=[
            pl.BlockSpec((tm, IN_DIM), lambda i: (i + t0, 0)),
            pl.BlockSpec((D_MODEL, IN_DIM), lambda i: (0, 0), pipeline_mode=pl.Buffered(1)),
            rows,
            rows,
            pl.BlockSpec((1, D_MODEL), lambda i: (0, 0)),
        ] + [pl.BlockSpec(memory_space=pl.ANY)] * ((into is not None) + len(after)),
        out_specs=[rows, pl.BlockSpec((1, D_MODEL), lambda i: (0, 0))],
        out_shape=[jax.ShapeDtypeStruct((s, D_MODEL), F32), jax.ShapeDtypeStruct((1, D_MODEL), F32)],
        scratch_shapes=[pltpu.VMEM((8, D_MODEL), F32)],
        plans=plans,
        aliases={5: 0} if into is not None else None,
    )


def _pair_sum(name, grad, got, axis, size, core, plans=()):
    shard = list(grad.shape)
    shard[axis] = size
    r, cdim = shard
    tr = min(r, TR_PAIR_SUM)
    nr = r // tr

    def body(core_ref, g_ref, got_ref, o_ref):
        o_ref[0] = (g_ref[...] + got_ref[0]).astype(BF16)

    if axis == 1:
        g_map = lambda q, t, core_ref: (t, 2 * q + core_ref[0])
    else:
        g_map = lambda q, t, core_ref: ((2 * q + core_ref[0]) * nr + t, 0)
    (out,), plan_outs = _call(
        body,
        name=name,
        prefetch=[core],
        operands=[grad, got],
        grid=(N_CHIP, nr),
        in_specs=[
            pl.BlockSpec((tr, cdim), g_map),
            pl.BlockSpec((1, tr, cdim), lambda q, t, core_ref: (q, t, 0)),
        ],
        out_specs=[pl.BlockSpec((1, tr, cdim), lambda q, t, core_ref: (q, t, 0))],
        out_shape=[jax.ShapeDtypeStruct((N_CHIP, r, cdim), BF16)],
        plans=plans,
    )
    return (out, plan_outs) if plans else out


def _adam_step(w, m, v, g):
    c1 = 1.0 / (1.0 - ADAM_B1**ADAM_STEP)
    c2 = 1.0 / (1.0 - ADAM_B2**ADAM_STEP)
    m_new = ADAM_B1 * m + (1.0 - ADAM_B1) * g
    v_new = ADAM_B2 * v + (1.0 - ADAM_B2) * (g * g)
    return -ADAM_LR * ((m_new * c1) / (jnp.sqrt(v_new * c2) + ADAM_EPS) + ADAM_WD * w), m_new, v_new


def _update_small(dev, vec_w, vec_m, vec_v, conv_w, conv_m, conv_v, small, landed):
    n_vec = len(vec_w)

    def body(dev_ref, *refs):
        refs = list(refs)
        take = lambda k: [refs.pop(0) for _ in range(k)]
        w_refs, m_refs, v_refs = take(n_vec), take(n_vec), take(n_vec)
        cw_ref, cm_ref, cv_ref, small_ref, land_ref, small_conv_ref, land_conv_ref = take(7)
        vec_outs, conv_outs, (loss_ref,) = take(4 * n_vec), take(4), take(1)

        def total(own_ref, others_ref):
            acc = None
            for d in range(N_DEV):
                part = jnp.where(dev_ref[0] == d, own_ref[0], others_ref[d])
                acc = part if acc is None else acc + part
            return acc

        g_all = total(small_ref, land_ref)
        for i in range(n_vec):
            g = g_all[i : i + 1]
            delta, m_new, v_new = _adam_step(w_refs[i][...], m_refs[i][...], v_refs[i][...], g)
            for ref, val in zip(vec_outs[4 * i : 4 * i + 4], (g, delta, m_new, v_new)):
                ref[...] = val
        g = total(small_conv_ref, land_conv_ref)[0:3]
        delta, m_new, v_new = _adam_step(cw_ref[0], cm_ref[0], cv_ref[0], g)
        for ref, val in zip(conv_outs, (g, delta, m_new, v_new)):
            ref[0] = val
        loss_ref[...] = jnp.broadcast_to(0.5 * jnp.sum(g_all[7:8], axis=-1, keepdims=True) / D_MODEL, loss_ref.shape)

    vec_spec = pl.BlockSpec((1, D_MODEL), lambda i, dev_ref: (0, 0))
    conv_spec = pl.BlockSpec((1, 3, CHUNK), lambda i, dev_ref: (0, 0, 0))
    vec_shape = jax.ShapeDtypeStruct((1, D_MODEL), F32)
    conv_shape = jax.ShapeDtypeStruct((1, 3, CHUNK), F32)
    outs, _ = _call(
        body,
        name="update_small",
        prefetch=[dev],
        operands=list(vec_w) + list(vec_m) + list(vec_v) + [conv_w, conv_m, conv_v, small, landed, small, landed],
        grid=(1,),
        in_specs=[vec_spec] * (3 * n_vec) + [conv_spec] * 3 + [
            pl.BlockSpec((1, 16, D_MODEL), lambda i, dev_ref: (0, 0, 0)),
            pl.BlockSpec((N_DEV, 16, D_MODEL), lambda i, dev_ref: (0, 0, 0)),
            pl.BlockSpec((1, 8, CHUNK), lambda i, dev_ref: (0, 1, dev_ref[0])),
            pl.BlockSpec((N_DEV, 8, CHUNK), lambda i, dev_ref: (0, 1, dev_ref[0])),
        ],
        out_specs=[vec_spec] * (4 * n_vec) + [conv_spec] * 4 + [pl.BlockSpec((1, CHUNK), lambda i, dev_ref: (0, 0))],
        out_shape=[vec_shape] * (4 * n_vec) + [conv_shape] * 4 + [jax.ShapeDtypeStruct((1, CHUNK), F32)],
    )
    return [outs[4 * i : 4 * i + 4] for i in range(n_vec)], outs[4 * n_vec : 4 * n_vec + 4], outs[-1]


def _adamw(name, groups, steps, own_slot=None, plans=()):
    lists = lambda p: list(p) if isinstance(p, (list, tuple)) else [p]
    groups = [(g[0], g[1], g[2], lists(g[3]), lists(g[4]) if len(g) > 4 else []) for g in groups]
    counts = [3 + len(g[3]) + len(g[4]) for g in groups]
    use_slot = own_slot is not None

    def first_tiles(w, parts):
        firsts, first = [], 0
        for p in parts:
            firsts.append(first)
            first += p.shape[1] // (w.shape[0] // steps)
        return firsts

    def update(t, slot_ref, firsts, w_ref, m_ref, v_ref, p_refs, o_refs, g_ref, d_ref, nm_ref, nv_ref):
        g = None
        for k, p_ref in enumerate(p_refs):
            gk = None
            for q in range(p_ref.shape[0]):
                part = p_ref[q].astype(F32)
                if o_refs:
                    part = jnp.where(slot_ref[0] == q, o_refs[k][0].astype(F32), part)
                gk = part if gk is None else gk + part
            g = gk if g is None else jnp.where(t >= firsts[k], gk, g)
        g_ref[...] = g
        d_ref[...], nm_ref[...], nv_ref[...] = _adam_step(w_ref[...], m_ref[...], v_ref[...], g)

    def body(*refs):
        refs = list(refs)
        slot_ref = refs.pop(0) if use_slot else None
        t = pl.program_id(0)
        ins, outs = refs[: sum(counts)], refs[sum(counts) :]
        for i, (w, _, _, parts, own) in enumerate(groups):
            mine = ins[sum(counts[:i]) : sum(counts[: i + 1])]
            update(t, slot_ref, first_tiles(w, parts), *mine[:3], mine[3 : 3 + len(parts)], mine[3 + len(parts) :],
                   *outs[4 * i : 4 * i + 4])

    operands, in_specs, out_specs, out_shape = [], [], [], []
    for w, m, v, parts, own in groups:
        r, cdim = w.shape
        tr = r // steps
        firsts = first_tiles(w, parts)

        def tile_of(k, t, firsts=firsts, parts=parts, tr=tr):
            return jnp.clip(t - firsts[k], 0, parts[k].shape[1] // tr - 1)

        blk = pl.BlockSpec((tr, cdim), lambda t, *_: (t, 0))
        operands += [w, m, v] + parts + own
        in_specs += [blk, blk, blk]
        in_specs += [pl.BlockSpec((p.shape[0], tr, cdim), lambda t, *_, k=k, tile_of=tile_of: (0, tile_of(k, t), 0)) for k, p in enumerate(parts)]
        in_specs += [pl.BlockSpec((1, tr, cdim), lambda t, slot_ref, k=k, tile_of=tile_of: (slot_ref[0], tile_of(k, t), 0)) for k in range(len(own))]
        out_specs += [blk] * 4
        out_shape += [jax.ShapeDtypeStruct((r, cdim), F32)] * 4
    outs, plan_outs = _call(
        body,
        name=name,
        prefetch=[own_slot] if use_slot else [],
        operands=operands,
        grid=(steps,),
        in_specs=in_specs,
        out_specs=out_specs,
        out_shape=out_shape,
        plans=plans,
    )
    return [outs[4 * i : 4 * i + 4] for i in range(len(groups))], plan_outs


def _chip_routes(n):
    x, y, c = _place()
    my_chip = 2 * x + y
    return [(a, (*chip, c), 2 * chip[0] + chip[1], my_chip, 2 * chip[0] + chip[1])
            for a in range(n) for chip in _other_chips(x, y)]


def _gather_routes(n):
    x, y, c = _place()
    flip = lambda v, f: 1 - v if f else v
    peers = [(flip(x, fx), flip(y, fy), flip(c, fc)) for fx in (0, 1) for fy in (0, 1) for fc in (0, 1) if fx + fy + fc]
    return [(0, p, 0, 4 * x + 2 * y + c, 4 * p[0] + 2 * p[1] + p[2]) for p in peers]


_chip_routes.copies_per_array = N_CHIP - 1
_gather_routes.copies_per_array = N_DEV - 1


def _split_start(name, routes, srcs, land_shapes):
    n = len(srcs)
    hbm = pl.BlockSpec(memory_space=pltpu.HBM)
    sem = pl.BlockSpec(memory_space=pltpu.SEMAPHORE)

    def body(*refs):
        src_refs, land_refs = refs[:n], refs[n : 2 * n]
        send_sems, recv_sems = refs[2 * n], refs[2 * n + 1]
        token = refs[-1]
        for k, (a, peer, src_slot, there, _) in enumerate(routes(n)):
            pltpu.make_async_remote_copy(
                src_ref=src_refs[a].at[src_slot], dst_ref=land_refs[a].at[there],
                send_sem=send_sems.at[k], recv_sem=recv_sems.at[k], device_id=peer, device_id_type=MESH).start()
        token[...] = jnp.zeros_like(token)

    in_hbm = [pltpu.with_memory_space_constraint(p, pltpu.HBM) for p in srcs]
    in_hbm += [pltpu.with_memory_space_constraint(lax.empty(shape, p.dtype), pltpu.HBM) for p, shape in zip(srcs, land_shapes)]
    n_copies = routes.copies_per_array * n
    res = pl.pallas_call(
        body,
        name=name,
        out_shape=[pltpu.SemaphoreType.DMA((n_copies,)), pltpu.SemaphoreType.DMA((n_copies,))]
        + [pltpu.HBM(p.shape, p.dtype) for p in srcs] + [pltpu.HBM(shape, p.dtype) for p, shape in zip(srcs, land_shapes)]
        + [jax.ShapeDtypeStruct((8, CHUNK), F32)],
        in_specs=[hbm] * (2 * n),
        out_specs=[sem, sem] + [hbm] * (2 * n) + [pl.BlockSpec(memory_space=pltpu.VMEM)],
        input_output_aliases={i: 2 + i for i in range(2 * n)},
        compiler_params=pltpu.CompilerParams(has_side_effects=pltpu.SideEffectType.DATAFLOW_SIDE_EFFECTING),
    )(*in_hbm)
    return res[:-1], res[-1]


def _split_wait(name, routes, in_flight, after):
    n = (len(in_flight) - 2) // 2
    hbm = pl.BlockSpec(memory_space=pltpu.HBM)
    sem = pl.BlockSpec(memory_space=pltpu.SEMAPHORE)
    send_sems, recv_sems, *bufs = in_flight

    def body(*refs):
        src_refs, land_refs = refs[:n], refs[n : 2 * n]
        send_ref, recv_ref = refs[2 * n], refs[2 * n + 1]
        for k, (a, peer, src_slot, _, here) in enumerate(routes(n)):
            cp = pltpu.make_async_remote_copy(
                src_ref=src_refs[a].at[src_slot], dst_ref=land_refs[a].at[here],
                send_sem=send_ref.at[k], recv_sem=recv_ref.at[k], device_id=peer, device_id_type=MESH)
            cp.wait_send()
            cp.wait_recv()

    res = pl.pallas_call(
        body,
        name=name,
        out_shape=[pltpu.HBM(b.shape, b.dtype) for b in bufs],
        in_specs=[hbm] * (2 * n) + [sem, sem] + [pl.BlockSpec(memory_space=pl.ANY)] * len(after),
        out_specs=[hbm] * (2 * n),
        input_output_aliases={i: i for i in range(2 * n)},
        compiler_params=pltpu.CompilerParams(has_side_effects=pltpu.SideEffectType.DATAFLOW_SIDE_EFFECTING),
    )(*bufs, send_sems, recv_sems, *after)
    return res[:n], res[n:]


def kernel(x, mem, norm_mix_g, w_in, conv_w, gm_ln_g, gm_ln_b, gm_ws, gm_bs, w_out, norm_x_g, norm_mem_g, w_q, w_kv, w_xo, norm_final_g, loss_target, m_norm_mix_g, m_w_in, m_conv_w, m_gm_ln_g, m_gm_ln_b, m_gm_ws, m_gm_bs, m_w_out, m_norm_x_g, m_norm_mem_g, m_w_q, m_w_kv, m_w_xo, m_norm_final_g, v_norm_mix_g, v_w_in, v_conv_w, v_gm_ln_g, v_gm_ln_b, v_gm_ws, v_gm_bs, v_w_out, v_norm_x_g, v_norm_mem_g, v_w_q, v_w_kv, v_w_xo, v_norm_final_g):
    s = x.shape[1]
    dev = 4 * lax.axis_index("x") + 2 * lax.axis_index("y") + lax.axis_index("c")
    core = lax.axis_index("c").astype(jnp.int32).reshape(1)
    x2 = x[0]
    target = loss_target[0]
    pad_taps = lambda t: jnp.pad(t[0], ((0, 5), (0, 0)))

    big_names = ["w_in", "w_out", "w_q", "w_kv", "w_xo"]
    big_w = dict(w_in=w_in[0], w_out=w_out[0], w_q=w_q[0], w_kv=w_kv[0], w_xo=w_xo[0])
    big_m = dict(w_in=m_w_in[0], w_out=m_w_out[0], w_q=m_w_q[0], w_kv=m_w_kv[0], w_xo=m_w_xo[0])
    big_v = dict(w_in=v_w_in[0], w_out=v_w_out[0], w_q=v_w_q[0], w_kv=v_w_kv[0], w_xo=v_w_xo[0])
    shard_axis = dict(w_in=1, w_out=0, w_q=0, w_kv=1, w_xo=0)
    full_shape = dict(w_in=(D_MODEL, IN_DIM), w_out=(MIX_DIM, D_MODEL), w_q=(D_MODEL, D_MODEL),
                      w_kv=(D_MODEL, 2 * D_MODEL), w_xo=(D_MODEL, D_MODEL))
    shard_size = {k: big_w[k].shape[shard_axis[k]] for k in big_names}
    others = big_names[1:]
    gather_spec = lambda k: (full_shape[k], shard_axis[k], shard_size[k])
    reduce_spec = lambda k: (shard_axis[k], shard_size[k])
    ws = gm_ws[0]
    bs_b = jnp.broadcast_to(gm_bs[0][:, :, None], (8, CHUNK, CHUNK))
    g_f = norm_final_g.reshape(1, D_MODEL)

    px, py = lax.axis_index("x"), lax.axis_index("y")
    chip_order = jnp.stack([2 * px + py, 2 * (1 - px) + py, 2 * px + 1 - py, 2 * (1 - px) + 1 - py]).astype(jnp.int32)
    (proj, h1, h1t, w_in_full), ((w_out_full, conv_g),) = _inproj(
        x2, norm_mix_g, big_w["w_in"].astype(BF16), chip_order,
        plans=[_gather_plan([big_w["w_out"].astype(BF16), pad_taps(conv_w)[None]], [gather_spec("w_out"), ((N_DEV, 8, CHUNK), 0, 1)])])
    conv_full = conv_g.transpose(1, 0, 2).reshape(8, D_MODEL)
    attn_w = ["w_q", "w_kv", "w_xo"]
    cat, (gathered,) = _mixer_fwd(
        proj, conv_full, gm_ln_g, gm_ln_b, ws, bs_b,
        plans=[_gather_plan([big_w[k].astype(BF16) for k in attn_w], [gather_spec(k) for k in attn_w])])
    wf = dict(zip(attn_w, gathered), w_out=w_out_full)
    k, vv, mt = _kv_fwd(mem[0], norm_mem_g, wf["w_kv"])
    (dcat, catt, h2t, ot, dx2b, dqb, dx1b, dk, dvv, d_gf, d_gx, sq_err) = _attn(
        x2, cat, target, wf["w_out"], wf["w_q"], wf["w_xo"], k, vv, norm_x_g, g_f)

    def pair_sums(names, grads, got):
        return [_pair_sum("pair_sum_" + k, grads[k], got[i], *reduce_spec(k), core) for i, k in enumerate(names)]

    sibling_plan = lambda names, grads: _sibling_plan([grads[k] for k in names], [reduce_spec(k) for k in names])
    early, late = ["w_out", "w_kv", "w_q"], ["w_xo"]
    big_g = dict(w_out=_matmul_f32("dw_out", catt, dx1b)[0])
    big_g["w_kv"], d_gmem = _kv_bwd(mem[0], wf["w_kv"], mt, dk, dvv)
    big_g["w_q"], (got_a,) = _matmul_f32("dw_q", h2t, dqb, plans=[sibling_plan(early[:2], big_g)])
    big_g["w_xo"], (got_b,) = _matmul_f32("dw_xo", ot, dx2b, plans=[sibling_plan(early[2:], big_g)])
    got_early = got_a + got_b
    (dproj, d_cw, d_lng, d_lnb, d_ws, d_bs), (parts_early, got_late) = _mixer_bwd(
        proj, dcat, conv_full, gm_ln_g, gm_ln_b, ws, bs_b,
        plans=[_chip_plan(pair_sums(early, big_g, got_early)), sibling_plan(late, big_g)])
    half = D_MODEL // 2
    in_spec = reduce_spec("w_in")
    dw_top, _ = _matmul_f32("dw_in_top", h1t, dproj, rows=(0, half))
    pair_xo = pair_sums(late, big_g, got_late)[0]
    dw_bot, (got_top, (ws_all,)) = _matmul_f32(
        "dw_in_bot", h1t, dproj, rows=(half, half),
        plans=[_sibling_plan([dw_top], [in_spec]),
               _gather_plan([d_ws.reshape(1, 8 * CHUNK, CHUNK).astype(BF16)], [((N_DEV, 8 * CHUNK, CHUNK), 0, 1)])])
    pair_top, ((got_bot,),) = _pair_sum("pair_sum_w_in_top", dw_top, got_top[0], *in_spec, core,
                                        plans=[_sibling_plan([dw_bot], [in_spec])])
    pair_bot = _pair_sum("pair_sum_w_in_bot", dw_bot, got_bot, *in_spec, core)
    chip_shapes = [p.shape for p in (pair_top, pair_bot, pair_xo)]
    chips_in_flight, token = _split_start("grad_chip_exchange_start", _chip_routes, [pair_top, pair_bot, pair_xo], chip_shapes)
    n_tiles = s // min(TM_DH1, s)
    (grad_x, d_gmix), _ = _dh1("dh1", dproj, w_in_full, x2, dx1b, norm_mix_g, (0, n_tiles), after=[token])
    vec_names = ["norm_mix_g", "gm_ln_g", "gm_ln_b", "gm_bs", "norm_x_g", "norm_mem_g", "norm_final_g"]
    vec_g = [d_gmix, d_lng, d_lnb, d_bs, d_gx, d_gmem, d_gf]
    small = jnp.concatenate(vec_g + [sq_err, d_cw], axis=0)[None]
    small_in_flight, _ = _split_start("small_all_gather_start", _gather_routes, [small], [(N_DEV, 16, D_MODEL)])
    flat_ws = lambda t: t.reshape(8 * CHUNK, CHUNK)
    updated, _ = _adamw(
        "adamw_early",
        [(big_w[k], big_m[k], big_v[k], got_k) for k, got_k in zip(early, parts_early)]
        + [(flat_ws(ws), flat_ws(m_gm_ws), flat_ws(v_gm_ws), ws_all)], steps=2)
    big_out = dict(zip(early, updated[:3]))
    ws_out = updated[3]
    (pair_top, pair_bot, pair_xo), (land_top, land_bot, land_xo) = _split_wait(
        "grad_chip_exchange_wait", _chip_routes, chips_in_flight, [d_gmix] + [u[0] for u in updated])
    my_chip = (2 * px + py).astype(jnp.int32).reshape(1)
    (big_out["w_in"], big_out["w_xo"]), _ = _adamw(
        "adamw_late",
        [(big_w["w_in"], big_m["w_in"], big_v["w_in"], [land_top, land_bot], [pair_top, pair_bot]),
         (big_w["w_xo"], big_m["w_xo"], big_v["w_xo"], land_xo, pair_xo)],
        steps=4, own_slot=my_chip)
    (small,), (small_all,) = _split_wait("small_all_gather_wait", _gather_routes, small_in_flight, [big_out["w_xo"][0]])
    row = lambda t: t.reshape(1, D_MODEL)
    vec_out, conv_out, loss_row = _update_small(
        dev.astype(jnp.int32).reshape(1),
        [norm_mix_g, gm_ln_g, gm_ln_b, row(gm_bs), norm_x_g, norm_mem_g, row(norm_final_g)],
        [m_norm_mix_g, m_gm_ln_g, m_gm_ln_b, row(m_gm_bs), m_norm_x_g, m_norm_mem_g, row(m_norm_final_g)],
        [v_norm_mix_g, v_gm_ln_g, v_gm_ln_b, row(v_gm_bs), v_norm_x_g, v_norm_mem_g, row(v_norm_final_g)],
        conv_w, m_conv_w, v_conv_w, small, small_all)
    loss = loss_row[0, 0]

    def result(name, which):
        if name in big_out:
            return big_out[name][which][None]
        if name == "conv_w":
            return conv_out[which]
        if name == "gm_ws":
            return ws_out[which].reshape(1, 8, CHUNK, CHUNK)
        r = vec_out[vec_names.index(name)][which]
        if name == "gm_bs":
            return r.reshape(1, 8, CHUNK)
        if name == "norm_final_g":
            return r.reshape(D_MODEL)
        return r

    weights = ["norm_mix_g", "w_in", "conv_w", "gm_ln_g", "gm_ln_b", "gm_ws", "gm_bs", "w_out", "norm_x_g", "norm_mem_g", "w_q", "w_kv", "w_xo", "norm_final_g"]
    outs = [loss, grad_x[None]]
    for which in range(4):
        outs += [result(name, which) for name in weights]
    return tuple(outs)
```

```python
import math

import jax
import jax.numpy as jnp
from jax import lax
from jax.experimental import pallas as pl
from jax.experimental.pallas import tpu as pltpu

F32 = jnp.float32
BF16 = jnp.bfloat16
MESH = pl.DeviceIdType.MESH

D_MODEL = 1024
N_DEV = 8
N_CHIP = 4
CHUNK = 128
GROUPS = 7
IN_DIM = GROUPS * D_MODEL
SHARD_IN = IN_DIM // N_DEV
MIX_DIM = 2 * D_MODEL
X_HEADS = 4
X_HEAD_DIM = D_MODEL // X_HEADS
MEM_LEN = 256
EPS = 1e-6
GELU_K0 = math.sqrt(2.0 / math.pi)
GELU_K1 = 0.044715

ADAM_LR = 0.001
ADAM_B1 = 0.9
ADAM_B2 = 0.999
ADAM_EPS = 1e-08
ADAM_WD = 0.01
ADAM_STEP = 10

TM_INPROJ = 512
TM_ATTN = 512
SUB_ATTN = 256
TM_DH1 = 512
TM_DW = 512
TK_DW = 4096
TR_PAIR_SUM = 1024


def _dot(a, b):
    return lax.dot_general(a, b, (((1,), (0,)), ((), ())), preferred_element_type=F32)


def _dot_nt(a, b):
    return lax.dot_general(a, b, (((1,), (1,)), ((), ())), preferred_element_type=F32)


def _rows8(v):
    return jnp.sum(v.reshape(v.shape[0] // 8, 8, v.shape[1]), axis=0)


def _gelu(x, with_grad=True):
    x2 = x * x
    t = jnp.tanh(x * (GELU_K0 + (GELU_K0 * GELU_K1) * x2))
    half = 0.5 + 0.5 * t
    g = x * half
    if not with_grad:
        return g
    return g, half + ((0.5 * x) * (1.0 - t * t)) * (GELU_K0 + (3.0 * GELU_K0 * GELU_K1) * x2)


def _silu(z, with_grad=True):
    s = jax.nn.sigmoid(z)
    zs = z * s
    return (zs, s + zs * (1.0 - s)) if with_grad else zs


def _rms(v):
    r = lax.rsqrt(jnp.mean(v * v, axis=-1, keepdims=True) + EPS)
    return v * r, r


def _rms_bwd(dy_g, vh, r):
    return r * (dy_g - vh * jnp.mean(dy_g * vh, axis=-1, keepdims=True))


def _place():
    return lax.axis_index("x"), lax.axis_index("y"), lax.axis_index("c")


def _other_chips(x, y):
    return [(1 - x, y), (x, 1 - y), (1 - x, 1 - y)]


def _pow2_divisor(n):
    return n & (-n)


def _shard_view(ref, axis, size, d):
    start = d * size
    align = _pow2_divisor(size)
    if align > 1:
        start = pl.multiple_of(start, align)
    idx = [slice(None)] * len(ref.shape)
    idx[axis] = pl.ds(start, size)
    return ref.at[tuple(idx)]


class _Plan:
    def __init__(self, ins, out_shape, sems, start, finish, mid=None, aliases=None):
        self.ins, self.out_shape, self.sems = list(ins), list(out_shape), list(sems)
        self.start, self.mid, self.finish = start, mid, finish
        self.aliases = dict(aliases or {})


def _gather_plan(shards, specs):
    n = len(shards)
    n_copy = 7

    def copies(ins, outs, sems):
        send_sems, recv_sems, local_sems = sems
        x, y, c = _place()
        me, sibling = (x, y, c), (x, y, 1 - c)
        chips = _other_chips(x, y)

        def block(a, p):
            _, axis, size = specs[a]
            return _shard_view(outs[a], axis, size, 4 * p[0] + 2 * p[1] + p[2])

        def copy(a, k, p, to, src=None):
            return pltpu.make_async_remote_copy(
                src_ref=block(a, p) if src is None else src,
                dst_ref=block(a, p),
                send_sem=send_sems.at[a * n_copy + k],
                recv_sem=recv_sems.at[a * n_copy + k],
                device_id=to,
                device_id_type=MESH,
            )

        def mine():
            return [pltpu.make_async_copy(ins[a], block(a, me), local_sems.at[a]) for a in range(n)]

        def first():
            return [cp for a in range(n) for cp in
                    [copy(a, 0, me, sibling, src=ins[a])] + [copy(a, 1 + j, me, (*chip, c), src=ins[a]) for j, chip in enumerate(chips)]]

        def landed():
            return [copy(a, 1 + j, (*chip, c), me) for j, chip in enumerate(chips) for a in range(n)]

        def passed():
            return [copy(a, 4 + j, (*chip, c), sibling) for j, chip in enumerate(chips) for a in range(n)]

        def last():
            return [cp for a in range(n) for cp in
                    [copy(a, 0, sibling, me)] + [copy(a, 4 + j, (*chip, 1 - c), me) for j, chip in enumerate(chips)]]

        return mine, first, landed, passed, last

    def start(ins, outs, sems):
        mine, first, _, _, _ = copies(ins, outs, sems)
        for cp in mine() + first():
            cp.start()

    def mid(ins, outs, sems):
        _, _, landed, passed, _ = copies(ins, outs, sems)
        for got, fwd in zip(landed(), passed()):
            got.wait_recv()
            fwd.start()

    def finish(ins, outs, sems):
        mine, first, _, passed, last = copies(ins, outs, sems)
        for cp in last():
            cp.wait_recv()
        for cp in first() + passed():
            cp.wait_send()
        for cp in mine():
            cp.wait()

    return _Plan(
        shards,
        [jax.ShapeDtypeStruct(full, s.dtype) for s, (full, _, _) in zip(shards, specs)],
        [pltpu.SemaphoreType.DMA((n * n_copy,)), pltpu.SemaphoreType.DMA((n * n_copy,)), pltpu.SemaphoreType.DMA((n,))],
        start, finish, mid,
    )


def _sibling_plan(grads, specs):
    n = len(grads)

    def shard_shape(a):
        axis, size = specs[a]
        shp = list(grads[a].shape)
        shp[axis] = size
        return tuple(shp)

    def copies(ins, outs, sems):
        send_sems, recv_sems = sems
        x, y, c = _place()
        return [
            pltpu.make_async_remote_copy(
                src_ref=_shard_view(ins[a], specs[a][0], specs[a][1], 2 * q + (1 - c)),
                dst_ref=outs[a].at[q],
                send_sem=send_sems.at[a * N_CHIP + q],
                recv_sem=recv_sems.at[a * N_CHIP + q],
                device_id=(x, y, 1 - c),
                device_id_type=MESH,
            )
            for a in range(n) for q in range(N_CHIP)
        ]

    def start(ins, outs, sems):
        for cp in copies(ins, outs, sems):
            cp.start()

    def finish(ins, outs, sems):
        for cp in copies(ins, outs, sems):
            cp.wait()

    return _Plan(
        grads,
        [jax.ShapeDtypeStruct((N_CHIP, *shard_shape(a)), grads[a].dtype) for a in range(n)],
        [pltpu.SemaphoreType.DMA((n * N_CHIP,)), pltpu.SemaphoreType.DMA((n * N_CHIP,))],
        start, finish,
    )


def _chip_plan(parts, full_rows=None, row0=None, into=None):
    n = len(parts)
    full_rows = [p.shape[1] for p in parts] if full_rows is None else full_rows
    row0 = [0] * n if row0 is None else row0

    def copies(ins, outs, sems):
        send_sems, recv_sems, local_sems = sems
        x, y, c = _place()
        my_chip = 2 * x + y
        chips = _other_chips(x, y)

        def land(a, q):
            return outs[a].at[q, pl.ds(row0[a], parts[a].shape[1]), :]

        def mine():
            return [pltpu.make_async_copy(ins[a].at[my_chip], land(a, my_chip), local_sems.at[a]) for a in range(n)]

        def remote(sending):
            return [
                pltpu.make_async_remote_copy(
                    src_ref=ins[a].at[2 * chip[0] + chip[1]],
                    dst_ref=land(a, my_chip if sending else 2 * chip[0] + chip[1]),
                    send_sem=send_sems.at[a * 3 + j], recv_sem=recv_sems.at[a * 3 + j],
                    device_id=(*chip, c), device_id_type=MESH)
                for a in range(n) for j, chip in enumerate(chips)
            ]

        return mine, remote

    def start(ins, outs, sems):
        mine, remote = copies(ins, outs, sems)
        for cp in mine() + remote(True):
            cp.start()

    def finish(ins, outs, sems):
        mine, remote = copies(ins, outs, sems)
        for cp in remote(False):
            cp.wait_recv()
        for cp in remote(True):
            cp.wait_send()
        for cp in mine():
            cp.wait()

    return _Plan(
        list(parts) + (list(into) if into is not None else []),
        [jax.ShapeDtypeStruct((N_CHIP, full_rows[a], parts[a].shape[2]), parts[a].dtype) for a in range(n)],
        [pltpu.SemaphoreType.DMA((n * 3,)), pltpu.SemaphoreType.DMA((n * 3,)), pltpu.SemaphoreType.DMA((n,))],
        start, finish,
        aliases={n + a: a for a in range(n)} if into is not None else None,
    )


def _call(body, *, name, operands, out_shape, grid=(), in_specs=None, out_specs=None, scratch_shapes=(),
          plans=(), aliases=None, prefetch=(), start_after_body=False, pass_on_at=0.6):
    operands, out_shape, scratch_shapes = list(operands), list(out_shape), list(scratch_shapes)
    n_pf, n_in, n_out, n_scr = len(prefetch), len(operands), len(out_shape), len(scratch_shapes)
    in_vmem = pl.BlockSpec(memory_space=pltpu.VMEM)
    in_hbm = pl.BlockSpec(memory_space=pl.ANY)
    in_specs = [in_vmem] * n_in if in_specs is None else list(in_specs)
    out_specs = [in_vmem] * n_out if out_specs is None else list(out_specs)
    io_alias = {n_pf + i: o for i, o in (aliases or {}).items()}
    pos_in, pos_out = n_pf + n_in, n_out
    for p in plans:
        for i, o in p.aliases.items():
            io_alias[pos_in + i] = pos_out + o
        pos_in += len(p.ins)
        pos_out += len(p.out_shape)
    steps = math.prod(grid) if grid else 1
    mid_step = min(steps - 1, int(steps * pass_on_at))

    def wrapped(*refs):
        refs = list(refs)
        take = lambda k: [refs.pop(0) for _ in range(k)]
        pf = take(n_pf)
        ins, p_ins = take(n_in), [take(len(p.ins)) for p in plans]
        outs, p_outs = take(n_out), [take(len(p.out_shape)) for p in plans]
        scr, p_sems = take(n_scr), [take(len(p.sems)) for p in plans]
        step = 0
        for ax, g in enumerate(grid):
            step = step * g + pl.program_id(ax)

        def hook(kind, at):
            todo = [(getattr(p, kind), a, b, c) for p, a, b, c in zip(plans, p_ins, p_outs, p_sems) if getattr(p, kind)]

            def run():
                for fn, a, b, c in todo:
                    fn(a, b, c)

            if todo and grid:
                pl.when(step == at)(run)
            elif todo:
                run()

        if not start_after_body:
            hook("start", 0)
        if body is not None:
            body(*pf, *ins, *outs, *scr)
        if start_after_body:
            hook("start", 0)
        hook("mid", mid_step)
        hook("finish", steps - 1)

    layout = dict(
        grid=grid,
        in_specs=in_specs + [in_hbm] * sum(len(p.ins) for p in plans),
        out_specs=out_specs + [in_hbm] * sum(len(p.out_shape) for p in plans),
        scratch_shapes=scratch_shapes + [s for p in plans for s in p.sems],
    )
    if n_pf:
        layout = dict(grid_spec=pltpu.PrefetchScalarGridSpec(num_scalar_prefetch=n_pf, **layout))
    res = pl.pallas_call(
        wrapped,
        name=name,
        out_shape=out_shape + [o for p in plans for o in p.out_shape],
        input_output_aliases=io_alias,
        compiler_params=pltpu.CompilerParams(dimension_semantics=("arbitrary",) * len(grid)) if grid else None,
        **layout,
    )(*prefetch, *operands, *[a for p in plans for a in p.ins])
    res = list(res)
    outs, plan_outs = res[:n_out], []
    pos = n_out
    for p in plans:
        plan_outs.append(res[pos : pos + len(p.out_shape)])
        pos += len(p.out_shape)
    return outs, plan_outs


def _inproj(x, g, w_shard, order, plans=()):
    s = x.shape[0]
    tm = min(TM_INPROJ, s)
    nt = s // tm
    wide = 2 * SHARD_IN
    fwd_at = nt // 2

    def body(order_ref, x_ref, g_ref, w_hbm, proj_ref, h1_ref, h1t_ref, wfull_hbm,
             w_scr, h1_scr, send_sems, recv_sems, local_sems):
        k, i = pl.program_id(0), pl.program_id(1)
        px, py, c = _place()
        me, sibling = (px, py, c), (px, py, 1 - c)
        chips = _other_chips(px, py)

        def half(slot, core):
            return w_scr.at[slot, :, pl.ds(pl.multiple_of(core * SHARD_IN, 128), SHARD_IN)]

        def copy(n, slot, core, to, src=None):
            return pltpu.make_async_remote_copy(
                src_ref=half(slot, core) if src is None else src, dst_ref=half(slot, core),
                send_sem=send_sems.at[n], recv_sem=recv_sems.at[n], device_id=to, device_id_type=MESH)

        def mine():
            return pltpu.make_async_copy(w_hbm, half(0, c), local_sems.at[0])

        def first():
            return [copy(0, 0, c, sibling, src=w_hbm)] + [copy(1 + j, 1 + j, c, (*chips[j], c), src=w_hbm) for j in range(2)]

        def relay():
            to = (c * px + (1 - c) * (1 - px), c * (1 - py) + (1 - c) * py, c)
            return pltpu.make_async_remote_copy(
                src_ref=half(2 - c, c), dst_ref=half(3, c), send_sem=send_sems.at[3], recv_sem=recv_sems.at[3],
                device_id=to, device_id_type=MESH)

        def save(slot):
            cols = pl.ds(pl.multiple_of(order_ref[slot] * wide, 128), wide)
            return pltpu.make_async_copy(w_scr.at[slot], wfull_hbm.at[:, cols], local_sems.at[1 + slot])

        @pl.when((k == 0) & (i == 0))
        def _():
            mine().start()
            for cp in first():
                cp.start()
            mine().wait()
            copy(0, 0, 1 - c, me).wait_recv()

        for j in range(3):
            @pl.when((k == j + 1) & (i == 0))
            def _():
                copy(4 + j, 1 + j, 1 - c, me).wait_recv()

        @pl.when((k == 3) & (i == 0))
        def _():
            for slot in range(N_CHIP):
                save(slot).start()

        @pl.when(k == 0)
        def _():
            xh, _ = _rms(x_ref[...])
            h = xh * g_ref[...]
            h1_ref[...] = h.astype(BF16)
            h1t_ref[...] = h.T.astype(BF16)
            h1_scr[pl.ds(pl.multiple_of(i * tm, tm), tm), :] = h.astype(BF16)

        proj_ref[...] = _dot(h1_scr[pl.ds(pl.multiple_of(i * tm, tm), tm), :], w_scr[k]).astype(BF16)

        for j in range(3):
            @pl.when((k == 0) & (i == nt - 1) if j < 2 else (k == 2) & (i == fwd_at))
            def _():
                copy(1 + j, 1 + j, c, me).wait_recv()
                copy(4 + j, 1 + j, c, sibling).start()
                if j < 2:
                    pl.when(c == 1 - j)(lambda: relay().start())

        @pl.when((k == 3) & (i == nt - 1))
        def _():
            for cp in first() + [relay()] + [copy(4 + j, 1 + j, c, sibling) for j in range(3)]:
                cp.wait_send()
            for slot in range(N_CHIP):
                save(slot).wait()

    once = lambda k, i, order_ref: (jnp.where(k == 0, i, nt - 1), 0)
    return _call(
        body,
        name="inproj",
        prefetch=[order],
        operands=[x, g, w_shard],
        grid=(N_CHIP, nt),
        in_specs=[
            pl.BlockSpec((tm, D_MODEL), once),
            pl.BlockSpec((1, D_MODEL), lambda k, i, order_ref: (0, 0)),
            pl.BlockSpec(memory_space=pl.ANY),
        ],
        out_specs=[
            pl.BlockSpec((tm, wide), lambda k, i, order_ref: (i, order_ref[k])),
            pl.BlockSpec((tm, D_MODEL), once),
            pl.BlockSpec((D_MODEL, tm), lambda k, i, order_ref: (0, jnp.where(k == 0, i, nt - 1))),
            pl.BlockSpec(memory_space=pl.ANY),
        ],
        out_shape=[
            jax.ShapeDtypeStruct((s, IN_DIM), BF16),
            jax.ShapeDtypeStruct((s, D_MODEL), BF16),
            jax.ShapeDtypeStruct((D_MODEL, s), BF16),
            jax.ShapeDtypeStruct((D_MODEL, IN_DIM), BF16),
        ],
        scratch_shapes=[
            pltpu.VMEM((N_CHIP, D_MODEL, wide), BF16),
            pltpu.VMEM((s, D_MODEL), BF16),
            pltpu.SemaphoreType.DMA((7,)),
            pltpu.SemaphoreType.DMA((7,)),
            pltpu.SemaphoreType.DMA((1 + N_CHIP,)),
        ],
        plans=plans,
        start_after_body=True,
        pass_on_at=0.8,
    )


def _rows_from_above(v, above):
    row = lax.broadcasted_iota(jnp.int32, (8, v.shape[1]), 0)
    r1, r2 = pltpu.roll(v, 1, 0), pltpu.roll(v, 2, 0)
    top1 = jnp.where(row == 0, above[7:8], r1[0:8])
    top2 = jnp.where(row == 0, above[6:7], jnp.where(row == 1, above[7:8], r2[0:8]))
    return jnp.concatenate([top1, r1[8:]], axis=0), jnp.concatenate([top2, r2[8:]], axis=0)


def _rows_from_below(v, below):
    n = v.shape[0]
    row = lax.broadcasted_iota(jnp.int32, (8, v.shape[1]), 0)
    r1, r2 = pltpu.roll(v, n - 1, 0), pltpu.roll(v, n - 2, 0)
    end1 = jnp.where(row == 7, below[0:1], r1[n - 8 :])
    end2 = jnp.where(row == 6, below[0:1], jnp.where(row == 7, below[1:2], r2[n - 8 :]))
    return jnp.concatenate([r1[: n - 8], end1], axis=0), jnp.concatenate([r2[: n - 8], end2], axis=0)


def _causal_mask():
    row = lax.broadcasted_iota(jnp.int32, (CHUNK, CHUNK), 0)
    col = lax.broadcasted_iota(jnp.int32, (CHUNK, CHUNK), 1)
    return row, row >= col


def _mixer_fwd(proj, conv_w, ln_g, ln_b, ws, bs_b, plans=()):
    s = proj.shape[0]
    nt = s // CHUNK

    def body(proj_ref, cw_ref, lng_ref, lnb_ref, ws_ref, bsb_ref, cat_ref, pcar_ref):
        @pl.when(pl.program_id(0) == 0)
        def _():
            pcar_ref[...] = jnp.zeros_like(pcar_ref)

        row, tril = _causal_mask()
        for j in range(8):
            cs = slice(CHUNK * j, CHUNK * (j + 1))

            def grp(k):
                return proj_ref[:, k * D_MODEL + CHUNK * j : k * D_MODEL + CHUNK * (j + 1)].astype(F32)

            w = cw_ref[:, cs]
            p = grp(1) * grp(2)
            p1, p2 = _rows_from_above(p, pcar_ref[:, cs])
            pcar_ref[:, cs] = p[CHUNK - 8 :]
            cv = w[0:1] * p2 + w[1:2] * p1 + w[2:3] * p
            sa = _silu(grp(3), with_grad=False)
            cat_ref[:, cs] = ((grp(0) * cv) * sa).astype(BF16)
            gu = _gelu(grp(4), with_grad=False)
            gv = _gelu(grp(5), with_grad=False)
            dv = gv - jnp.mean(gv, axis=-1, keepdims=True)
            vn = dv * lax.rsqrt(jnp.mean(dv * dv, axis=-1, keepdims=True) + EPS)
            vn = vn * lng_ref[:, cs] + lnb_ref[:, cs]
            wc = jnp.where(tril, ws_ref[j], 0.0).astype(BF16)
            sp = _dot(wc, vn.astype(BF16)) + bsb_ref[j]
            sb = _silu(grp(6), with_grad=False)
            cat_ref[:, D_MODEL + CHUNK * j : D_MODEL + CHUNK * (j + 1)] = ((gu * sp) * sb).astype(BF16)

    whole = lambda shape: pl.BlockSpec(shape, lambda i: (0,) * len(shape))
    (cat,), plan_outs = _call(
        body,
        name="mixer_fwd",
        operands=[proj, conv_w, ln_g, ln_b, ws, bs_b],
        grid=(nt,),
        in_specs=[
            pl.BlockSpec((CHUNK, IN_DIM), lambda i: (i, 0)),
            whole((8, D_MODEL)),
            whole((1, D_MODEL)),
            whole((1, D_MODEL)),
            whole((8, CHUNK, CHUNK)),
            whole((8, CHUNK, CHUNK)),
        ],
        out_specs=[pl.BlockSpec((CHUNK, MIX_DIM), lambda i: (i, 0))],
        out_shape=[jax.ShapeDtypeStruct((s, MIX_DIM), BF16)],
        scratch_shapes=[pltpu.VMEM((8, D_MODEL), F32)],
        plans=plans,
        pass_on_at=0.85,
    )
    return cat, plan_outs


def _mixer_bwd(proj, dcat, conv_w, ln_g, ln_b, ws, bs_b, plans=()):
    s = proj.shape[0]
    nt = s // CHUNK

    def body(proj_ref, halo_ref, dcat_ref, cw_ref, lng_ref, lnb_ref, ws_ref, bsb_ref,
             dproj_ref, dcw_ref, dlng_ref, dlnb_ref, dws_ref, dbs_ref,
             car_ref, acc_cw, acc_lng, acc_lnb, acc_bs, wc_scr, wct_scr):
        i = pl.program_id(0)
        tile = nt - 1 - i
        row, tril = _causal_mask()

        @pl.when(i == 0)
        def _():
            for j in range(8):
                wc = jnp.where(tril, ws_ref[j], 0.0)
                wc_scr[j] = wc.astype(BF16)
                wct_scr[j] = wc.T.astype(BF16)
            car_ref[...] = jnp.zeros_like(car_ref)
            acc_cw[...] = jnp.zeros_like(acc_cw)
            acc_lng[...] = jnp.zeros_like(acc_lng)
            acc_lnb[...] = jnp.zeros_like(acc_lnb)
            acc_bs[...] = jnp.zeros_like(acc_bs)
            dws_ref[...] = jnp.zeros_like(dws_ref)

        has_prev = jnp.where(tile > 0, 1.0, 0.0).astype(F32)
        for j in range(8):
            cs = slice(CHUNK * j, CHUNK * (j + 1))

            def col(k):
                return slice(k * D_MODEL + CHUNK * j, k * D_MODEL + CHUNK * (j + 1))

            def grp(k):
                return proj_ref[:, col(k)].astype(F32)

            w = cw_ref[:, cs]
            gb, gc, xa = grp(0), grp(1), grp(2)
            p = gc * xa
            pprev = halo_ref[:, col(1)].astype(F32) * halo_ref[:, col(2)].astype(F32) * has_prev
            p1, p2 = _rows_from_above(p, pprev[8:16])
            cv = w[0:1] * p2 + w[1:2] * p1 + w[2:3] * p
            za = grp(3)
            sa, dsa = _silu(za)
            da = dcat_ref[:, cs].astype(F32)
            da_gb = da * gb
            dproj_ref[:, col(0)] = ((da * sa) * cv).astype(BF16)
            dproj_ref[:, col(3)] = ((da_gb * cv) * dsa).astype(BF16)
            dcv = da_gb * sa
            d1, d2 = _rows_from_below(dcv, car_ref[:, cs])
            car_ref[:, cs] = dcv[0:8]
            dp = w[2:3] * dcv + w[1:2] * d1 + w[0:1] * d2
            dproj_ref[:, col(1)] = (dp * xa).astype(BF16)
            dproj_ref[:, col(2)] = (dp * gc).astype(BF16)
            acc_cw[0, :, cs] += _rows8(dcv * p2)
            acc_cw[1, :, cs] += _rows8(dcv * p1)
            acc_cw[2, :, cs] += _rows8(dcv * p)
            gu, dgu = _gelu(grp(4))
            gv, dgv = _gelu(grp(5))
            dv = gv - jnp.mean(gv, axis=-1, keepdims=True)
            rstd = lax.rsqrt(jnp.mean(dv * dv, axis=-1, keepdims=True) + EPS)
            vnh = dv * rstd
            lng = lng_ref[:, cs]
            vnb = (vnh * lng + lnb_ref[:, cs]).astype(BF16)
            sp = _dot(wc_scr[j], vnb) + bsb_ref[j]
            zb = grp(6)
            sb, dsb = _silu(zb)
            db = dcat_ref[:, D_MODEL + CHUNK * j : D_MODEL + CHUNK * (j + 1)].astype(F32)
            db_sp, db_gu = db * sp, db * gu
            dproj_ref[:, col(4)] = ((db_sp * sb) * dgu).astype(BF16)
            dproj_ref[:, col(6)] = ((db_sp * gu) * dsb).astype(BF16)
            dsp = db_gu * sb
            acc_bs[j] += dsp
            dspb = dsp.astype(BF16)
            dws_ref[j] += _dot_nt(dspb, vnb)
            dvn = _dot(wct_scr[j], dspb)
            acc_lnb[:, cs] += _rows8(dvn)
            acc_lng[:, cs] += _rows8(dvn * vnh)
            dvh = dvn * lng
            dgvv = rstd * (dvh - jnp.mean(dvh, axis=-1, keepdims=True) - vnh * jnp.mean(dvh * vnh, axis=-1, keepdims=True))
            dproj_ref[:, col(5)] = (dgvv * dgv).astype(BF16)

        @pl.when(i == nt - 1)
        def _():
            dlng_ref[...] = jnp.sum(acc_lng[...], axis=0, keepdims=True)
            dlnb_ref[...] = jnp.sum(acc_lnb[...], axis=0, keepdims=True)
            dcw_ref[...] = jnp.zeros_like(dcw_ref)
            for k in range(3):
                dcw_ref[k : k + 1, :] = jnp.sum(acc_cw[k], axis=0, keepdims=True)
            for j in range(8):
                dws_ref[j] = jnp.where(tril, dws_ref[j], 0.0)
                dbs_ref[:, CHUNK * j : CHUNK * (j + 1)] = jnp.sum(acc_bs[j].T, axis=0, keepdims=True)

    whole = lambda shape: pl.BlockSpec(shape, lambda i: (0,) * len(shape))
    halo_rows = 16
    per = CHUNK // halo_rows
    return _call(
        body,
        name="mixer_bwd",
        operands=[proj, proj, dcat, conv_w, ln_g, ln_b, ws, bs_b],
        plans=plans,
        grid=(nt,),
        in_specs=[
            pl.BlockSpec((CHUNK, IN_DIM), lambda i: (nt - 1 - i, 0)),
            pl.BlockSpec((halo_rows, IN_DIM), lambda i: (jnp.maximum((nt - 1 - i) * per - 1, 0), 0)),
            pl.BlockSpec((CHUNK, MIX_DIM), lambda i: (nt - 1 - i, 0)),
            whole((8, D_MODEL)),
            whole((1, D_MODEL)),
            whole((1, D_MODEL)),
            whole((8, CHUNK, CHUNK)),
            whole((8, CHUNK, CHUNK)),
        ],
        out_specs=[
            pl.BlockSpec((CHUNK, IN_DIM), lambda i: (nt - 1 - i, 0)),
            whole((8, D_MODEL)),
            whole((1, D_MODEL)),
            whole((1, D_MODEL)),
            whole((8, CHUNK, CHUNK)),
            whole((1, D_MODEL)),
        ],
        out_shape=[
            jax.ShapeDtypeStruct((s, IN_DIM), BF16),
            jax.ShapeDtypeStruct((8, D_MODEL), F32),
            jax.ShapeDtypeStruct((1, D_MODEL), F32),
            jax.ShapeDtypeStruct((1, D_MODEL), F32),
            jax.ShapeDtypeStruct((8, CHUNK, CHUNK), F32),
            jax.ShapeDtypeStruct((1, D_MODEL), F32),
        ],
        scratch_shapes=[
            pltpu.VMEM((8, D_MODEL), F32),
            pltpu.VMEM((3, 8, D_MODEL), F32),
            pltpu.VMEM((8, D_MODEL), F32),
            pltpu.VMEM((8, D_MODEL), F32),
            pltpu.VMEM((8, CHUNK, CHUNK), F32),
            pltpu.VMEM((8, CHUNK, CHUNK), BF16),
            pltpu.VMEM((8, CHUNK, CHUNK), BF16),
        ],
    )


def _kv_fwd(mem, g_mem, w_kv):
    def body(mem_ref, g_ref, w_ref, k_ref, vv_ref, mt_ref):
        mh, _ = _rms(mem_ref[...])
        m = mh * g_ref[...]
        kv = _dot(m.astype(BF16), w_ref[...])
        k_ref[...] = kv[:, :D_MODEL].astype(BF16)
        vv_ref[...] = kv[:, D_MODEL:].astype(BF16)
        mt_ref[...] = m.T.astype(BF16)

    return pl.pallas_call(
        body,
        name="kv_fwd",
        out_shape=[
            jax.ShapeDtypeStruct((MEM_LEN, D_MODEL), BF16),
            jax.ShapeDtypeStruct((MEM_LEN, D_MODEL), BF16),
            jax.ShapeDtypeStruct((D_MODEL, MEM_LEN), BF16),
        ],
    )(mem, g_mem, w_kv)


def _kv_bwd(mem, w_kv, mt, dk, dvv):
    def body(mem_ref, w_ref, mt_ref, dk_ref, dvv_ref, dw_ref, dg_ref):
        dkv = jnp.concatenate([dk_ref[...], dvv_ref[...]], axis=1).astype(BF16)
        dw_ref[...] = _dot(mt_ref[...], dkv)
        dm = _dot_nt(dkv, w_ref[...])
        mh, _ = _rms(mem_ref[...])
        dg_ref[...] = jnp.sum(dm * mh, axis=0, keepdims=True)

    return pl.pallas_call(
        body,
        name="kv_bwd",
        out_shape=[
            jax.ShapeDtypeStruct((D_MODEL, 2 * D_MODEL), F32),
            jax.ShapeDtypeStruct((1, D_MODEL), F32),
        ],
    )(mem, w_kv, mt, dk, dvv)


def _attn(x, cat, target, w_out, w_q, w_xo, k, vv, g_x, g_f):
    s = x.shape[0]
    tm = min(TM_ATTN, s)
    sub = min(SUB_ATTN, tm)
    nt = s // tm
    scale = 1.0 / math.sqrt(X_HEAD_DIM)

    def body(x_ref, cat_ref, t_ref, wout_ref, wq_ref, wxo_ref, k_ref, vv_ref, gx_ref, gf_ref,
             dcat_ref, catt_ref, h2t_ref, ot_ref, dx2b_ref, dqb_ref, dx1b_ref,
             dk_ref, dvv_ref, dgf_ref, dgx_ref, loss_ref,
             p_scr, q_scr, o_scr, dq_scr, acc_gf, acc_gx, acc_loss):
        i = pl.program_id(0)

        @pl.when(i == 0)
        def _():
            dk_ref[...] = jnp.zeros_like(dk_ref)
            dvv_ref[...] = jnp.zeros_like(dvv_ref)
            acc_gf[...] = jnp.zeros_like(acc_gf)
            acc_gx[...] = jnp.zeros_like(acc_gx)
            acc_loss[...] = jnp.zeros_like(acc_loss)

        subs = [slice(u * sub, (u + 1) * sub) for u in range(tm // sub)]
        heads = [slice(X_HEAD_DIM * h, X_HEAD_DIM * (h + 1)) for h in range(X_HEADS)]
        gx, gf = gx_ref[...], gf_ref[...]
        x1, x1h, r2 = [], [], []
        for rs in subs:
            cat = cat_ref[rs, :]
            catt_ref[:, rs] = cat.astype(F32).T.astype(BF16)
            v = x_ref[rs, :] + _dot(cat, wout_ref[...])
            vh, r = _rms(v)
            h2 = vh * gx
            h2t_ref[:, rs] = h2.T.astype(BF16)
            q_scr[rs, :] = _dot(h2.astype(BF16), wq_ref[...]).astype(BF16)
            x1.append(v), x1h.append(vh), r2.append(r)
        for rs in subs:
            for hs in heads:
                sc = _dot_nt(q_scr[rs, hs], k_ref[:, hs]) * scale
                e = jnp.exp(sc - jnp.max(sc, axis=-1, keepdims=True))
                p = e / jnp.sum(e, axis=-1, keepdims=True)
                p_scr[rs, hs] = p
                o_scr[rs, hs] = _dot(p.astype(BF16), vv_ref[:, hs])
        dx2, sq, d_gf = [], None, None
        for u, rs in enumerate(subs):
            o = o_scr[rs, :]
            ot_ref[:, rs] = o.T.astype(BF16)
            x2 = x1[u] + _dot(o.astype(BF16), wxo_ref[...])
            x2h, r3 = _rms(x2)
            err = x2h * gf - t_ref[rs, :]
            dy = err * (1.0 / D_MODEL)
            sq = _rows8(err * err) if sq is None else sq + _rows8(err * err)
            d_gf = _rows8(dy * x2h) if d_gf is None else d_gf + _rows8(dy * x2h)
            dx2.append(_rms_bwd(dy * gf, x2h, r3))
        acc_loss[...] += sq
        acc_gf[...] += d_gf
        d_vv, d_k = [None] * X_HEADS, [None] * X_HEADS
        for u, rs in enumerate(subs):
            dx2b = dx2[u].astype(BF16)
            dx2b_ref[rs, :] = dx2b
            dob = _dot_nt(dx2b, wxo_ref[...]).astype(BF16)
            for h, hs in enumerate(heads):
                p = p_scr[rs, hs]
                dp = _dot_nt(dob[:, hs], vv_ref[:, hs])
                ds = p * (dp - jnp.sum(dp * p, axis=-1, keepdims=True)) * scale
                dq_scr[rs, hs] = _dot(ds.astype(BF16), k_ref[:, hs]).astype(BF16)
                dvv_h = _dot(p.T.astype(BF16), dob[:, hs])
                dk_h = _dot(ds.T.astype(BF16), q_scr[rs, hs])
                d_vv[h] = dvv_h if d_vv[h] is None else d_vv[h] + dvv_h
                d_k[h] = dk_h if d_k[h] is None else d_k[h] + dk_h
        for h, hs in enumerate(heads):
            dvv_ref[:, hs] += d_vv[h]
            dk_ref[:, hs] += d_k[h]
        d_gx = None
        for u, rs in enumerate(subs):
            dqb = dq_scr[rs, :]
            dqb_ref[rs, :] = dqb
            dh2 = _dot_nt(dqb, wq_ref[...])
            d_gx = _rows8(dh2 * x1h[u]) if d_gx is None else d_gx + _rows8(dh2 * x1h[u])
            dx1 = dx2[u] + _rms_bwd(dh2 * gx, x1h[u], r2[u])
            dx1b = dx1.astype(BF16)
            dx1b_ref[rs, :] = dx1b
            dcat_ref[rs, :] = _dot_nt(dx1b, wout_ref[...]).astype(BF16)
        acc_gx[...] += d_gx

        @pl.when(i == nt - 1)
        def _():
            dgf_ref[...] = jnp.sum(acc_gf[...], axis=0, keepdims=True)
            dgx_ref[...] = jnp.sum(acc_gx[...], axis=0, keepdims=True)
            loss_ref[...] = jnp.sum(acc_loss[...], axis=0, keepdims=True)

    whole = lambda shape: pl.BlockSpec(shape, lambda i: (0,) * len(shape))
    once = lambda shape: pl.BlockSpec(shape, lambda i: (0,) * len(shape), pipeline_mode=pl.Buffered(1))
    rows = lambda width: pl.BlockSpec((tm, width), lambda i: (i, 0))
    cols = lambda height: pl.BlockSpec((height, tm), lambda i: (0, i))
    vec = jax.ShapeDtypeStruct((1, D_MODEL), F32)
    return pl.pallas_call(
        body,
        name="attn",
        grid=(nt,),
        in_specs=[
            rows(D_MODEL), rows(MIX_DIM), rows(D_MODEL),
            once((MIX_DIM, D_MODEL)), once((D_MODEL, D_MODEL)), once((D_MODEL, D_MODEL)),
            once((MEM_LEN, D_MODEL)), once((MEM_LEN, D_MODEL)),
            whole((1, D_MODEL)), whole((1, D_MODEL)),
        ],
        out_specs=[
            rows(MIX_DIM), cols(MIX_DIM), cols(D_MODEL), cols(D_MODEL),
            rows(D_MODEL), rows(D_MODEL), rows(D_MODEL),
            whole((MEM_LEN, D_MODEL)), whole((MEM_LEN, D_MODEL)),
            whole((1, D_MODEL)), whole((1, D_MODEL)), whole((1, D_MODEL)),
        ],
        out_shape=[
            jax.ShapeDtypeStruct((s, MIX_DIM), BF16),
            jax.ShapeDtypeStruct((MIX_DIM, s), BF16),
            jax.ShapeDtypeStruct((D_MODEL, s), BF16),
            jax.ShapeDtypeStruct((D_MODEL, s), BF16),
            jax.ShapeDtypeStruct((s, D_MODEL), BF16),
            jax.ShapeDtypeStruct((s, D_MODEL), BF16),
            jax.ShapeDtypeStruct((s, D_MODEL), BF16),
            jax.ShapeDtypeStruct((MEM_LEN, D_MODEL), F32),
            jax.ShapeDtypeStruct((MEM_LEN, D_MODEL), F32),
            vec, vec, vec,
        ],
        scratch_shapes=[
            pltpu.VMEM((tm, D_MODEL), F32),
            pltpu.VMEM((tm, D_MODEL), BF16),
            pltpu.VMEM((tm, D_MODEL), F32),
            pltpu.VMEM((tm, D_MODEL), BF16),
            pltpu.VMEM((8, D_MODEL), F32),
            pltpu.VMEM((8, D_MODEL), F32),
            pltpu.VMEM((8, D_MODEL), F32),
        ],
        compiler_params=pltpu.CompilerParams(dimension_semantics=("arbitrary",)),
    )(x, cat, target, w_out, w_q, w_xo, k, vv, g_x, g_f)


def _matmul_f32(name, a, b, rows=None, plans=()):
    first, m = (0, a.shape[0]) if rows is None else rows
    kk = a.shape[1]
    n = b.shape[1]
    tm, tn, tk = min(m, TM_DW), min(n, D_MODEL), min(kk, TK_DW)
    nk = kk // tk
    i0 = first // tm

    def body(a_ref, b_ref, o_ref):
        prod = _dot(a_ref[...], b_ref[...])
        if nk == 1:
            o_ref[...] = prod
            return
        kid = pl.program_id(2)

        @pl.when(kid == 0)
        def _():
            o_ref[...] = prod

        @pl.when(kid > 0)
        def _():
            o_ref[...] += prod

    (out,), plan_outs = _call(
        body,
        name=name,
        operands=[a, b],
        grid=(m // tm, n // tn, nk),
        in_specs=[pl.BlockSpec((tm, tk), lambda i, j, k: (i + i0, k)), pl.BlockSpec((tk, tn), lambda i, j, k: (k, j))],
        out_specs=[pl.BlockSpec((tm, tn), lambda i, j, k: (i, j))],
        out_shape=[jax.ShapeDtypeStruct((m, n), F32)],
        plans=plans,
    )
    return out, plan_outs


def _dh1(name, dproj, w_in, x, dx1, g, tiles, into=None, after=(), plans=()):
    s = x.shape[0]
    tm = min(TM_DH1, s)
    t0, nt = tiles[0], tiles[1] - tiles[0]

    def body(dp_ref, w_ref, x_ref, dx1_ref, g_ref, *rest):
        gx_ref, dg_ref, acc_g = rest[-3:]
        i = pl.program_id(0)

        @pl.when(i == 0)
        def _():
            acc_g[...] = jnp.zeros_like(acc_g)

        dh1 = _dot_nt(dp_ref[...], w_ref[...])
        xh, r = _rms(x_ref[...])
        acc_g[...] += _rows8(dh1 * xh)
        gx_ref[...] = dx1_ref[...].astype(F32) + _rms_bwd(dh1 * g_ref[...], xh, r)

        @pl.when(i == nt - 1)
        def _():
            dg_ref[...] = jnp.sum(acc_g[...], axis=0, keepdims=True)

    rows = pl.BlockSpec((tm, D_MODEL), lambda i: (i + t0, 0))
    return _call(
        body,
        name=name,
        operands=[dproj, w_in, x, dx1, g] + ([into] if into is not None else []) + list(after),
        grid=(nt,),
        in_specs=[
            pl.BlockSpec((tm, IN_DIM), lambda i: (i + t0, 0)),
            pl.BlockSpec((D_MODEL, IN_DIM), lambda i: (0, 0), pipeline_mode=pl.Buffered(1)),
            rows,
            rows,
            pl.BlockSpec((1, D_MODEL), lambda i: (0, 0)),
        ] + [pl.BlockSpec(memory_space=pl.ANY)] * ((into is not None) + len(after)),
        out_specs=[rows, pl.BlockSpec((1, D_MODEL), lambda i: (0, 0))],
        out_shape=[jax.ShapeDtypeStruct((s, D_MODEL), F32), jax.ShapeDtypeStruct((1, D_MODEL), F32)],
        scratch_shapes=[pltpu.VMEM((8, D_MODEL), F32)],
        plans=plans,
        aliases={5: 0} if into is not None else None,
    )


def _pair_sum(name, grad, got, axis, size, core, plans=()):
    shard = list(grad.shape)
    shard[axis] = size
    r, cdim = shard
    tr = min(r, TR_PAIR_SUM)
    nr = r // tr

    def body(core_ref, g_ref, got_ref, o_ref):
        o_ref[0] = (g_ref[...] + got_ref[0]).astype(BF16)

    if axis == 1:
        g_map = lambda q, t, core_ref: (t, 2 * q + core_ref[0])
    else:
        g_map = lambda q, t, core_ref: ((2 * q + core_ref[0]) * nr + t, 0)
    (out,), plan_outs = _call(
        body,
        name=name,
        prefetch=[core],
        operands=[grad, got],
        grid=(N_CHIP, nr),
        in_specs=[
            pl.BlockSpec((tr, cdim), g_map),
            pl.BlockSpec((1, tr, cdim), lambda q, t, core_ref: (q, t, 0)),
        ],
        out_specs=[pl.BlockSpec((1, tr, cdim), lambda q, t, core_ref: (q, t, 0))],
        out_shape=[jax.ShapeDtypeStruct((N_CHIP, r, cdim), BF16)],
        plans=plans,
    )
    return (out, plan_outs) if plans else out


def _adam_step(w, m, v, g):
    c1 = 1.0 / (1.0 - ADAM_B1**ADAM_STEP)
    c2 = 1.0 / (1.0 - ADAM_B2**ADAM_STEP)
    m_new = ADAM_B1 * m + (1.0 - ADAM_B1) * g
    v_new = ADAM_B2 * v + (1.0 - ADAM_B2) * (g * g)
    return -ADAM_LR * ((m_new * c1) / (jnp.sqrt(v_new * c2) + ADAM_EPS) + ADAM_WD * w), m_new, v_new


def _update_small(dev, vec_w, vec_m, vec_v, conv_w, conv_m, conv_v, small, landed):
    n_vec = len(vec_w)

    def body(dev_ref, *refs):
        refs = list(refs)
        take = lambda k: [refs.pop(0) for _ in range(k)]
        w_refs, m_refs, v_refs = take(n_vec), take(n_vec), take(n_vec)
        cw_ref, cm_ref, cv_ref, small_ref, land_ref, small_conv_ref, land_conv_ref = take(7)
        vec_outs, conv_outs, (loss_ref,) = take(4 * n_vec), take(4), take(1)

        def total(own_ref, others_ref):
            acc = None
            for d in range(N_DEV):
                part = jnp.where(dev_ref[0] == d, own_ref[0], others_ref[d])
                acc = part if acc is None else acc + part
            return acc

        g_all = total(small_ref, land_ref)
        for i in range(n_vec):
            g = g_all[i : i + 1]
            delta, m_new, v_new = _adam_step(w_refs[i][...], m_refs[i][...], v_refs[i][...], g)
            for ref, val in zip(vec_outs[4 * i : 4 * i + 4], (g, delta, m_new, v_new)):
                ref[...] = val
        g = total(small_conv_ref, land_conv_ref)[0:3]
        delta, m_new, v_new = _adam_step(cw_ref[0], cm_ref[0], cv_ref[0], g)
        for ref, val in zip(conv_outs, (g, delta, m_new, v_new)):
            ref[0] = val
        loss_ref[...] = jnp.broadcast_to(0.5 * jnp.sum(g_all[7:8], axis=-1, keepdims=True) / D_MODEL, loss_ref.shape)

    vec_spec = pl.BlockSpec((1, D_MODEL), lambda i, dev_ref: (0, 0))
    conv_spec = pl.BlockSpec((1, 3, CHUNK), lambda i, dev_ref: (0, 0, 0))
    vec_shape = jax.ShapeDtypeStruct((1, D_MODEL), F32)
    conv_shape = jax.ShapeDtypeStruct((1, 3, CHUNK), F32)
    outs, _ = _call(
        body,
        name="update_small",
        prefetch=[dev],
        operands=list(vec_w) + list(vec_m) + list(vec_v) + [conv_w, conv_m, conv_v, small, landed, small, landed],
        grid=(1,),
        in_specs=[vec_spec] * (3 * n_vec) + [conv_spec] * 3 + [
            pl.BlockSpec((1, 16, D_MODEL), lambda i, dev_ref: (0, 0, 0)),
            pl.BlockSpec((N_DEV, 16, D_MODEL), lambda i, dev_ref: (0, 0, 0)),
            pl.BlockSpec((1, 8, CHUNK), lambda i, dev_ref: (0, 1, dev_ref[0])),
            pl.BlockSpec((N_DEV, 8, CHUNK), lambda i, dev_ref: (0, 1, dev_ref[0])),
        ],
        out_specs=[vec_spec] * (4 * n_vec) + [conv_spec] * 4 + [pl.BlockSpec((1, CHUNK), lambda i, dev_ref: (0, 0))],
        out_shape=[vec_shape] * (4 * n_vec) + [conv_shape] * 4 + [jax.ShapeDtypeStruct((1, CHUNK), F32)],
    )
    return [outs[4 * i : 4 * i + 4] for i in range(n_vec)], outs[4 * n_vec : 4 * n_vec + 4], outs[-1]


def _adamw(name, groups, steps, own_slot=None, plans=()):
    lists = lambda p: list(p) if isinstance(p, (list, tuple)) else [p]
    groups = [(g[0], g[1], g[2], lists(g[3]), lists(g[4]) if len(g) > 4 else []) for g in groups]
    counts = [3 + len(g[3]) + len(g[4]) for g in groups]
    use_slot = own_slot is not None

    def first_tiles(w, parts):
        firsts, first = [], 0
        for p in parts:
            firsts.append(first)
            first += p.shape[1] // (w.shape[0] // steps)
        return firsts

    def update(t, slot_ref, firsts, w_ref, m_ref, v_ref, p_refs, o_refs, g_ref, d_ref, nm_ref, nv_ref):
        g = None
        for k, p_ref in enumerate(p_refs):
            gk = None
            for q in range(p_ref.shape[0]):
                part = p_ref[q].astype(F32)
                if o_refs:
                    part = jnp.where(slot_ref[0] == q, o_refs[k][0].astype(F32), part)
                gk = part if gk is None else gk + part
            g = gk if g is None else jnp.where(t >= firsts[k], gk, g)
        g_ref[...] = g
        d_ref[...], nm_ref[...], nv_ref[...] = _adam_step(w_ref[...], m_ref[...], v_ref[...], g)

    def body(*refs):
        refs = list(refs)
        slot_ref = refs.pop(0) if use_slot else None
        t = pl.program_id(0)
        ins, outs = refs[: sum(counts)], refs[sum(counts) :]
        for i, (w, _, _, parts, own) in enumerate(groups):
            mine = ins[sum(counts[:i]) : sum(counts[: i + 1])]
            update(t, slot_ref, first_tiles(w, parts), *mine[:3], mine[3 : 3 + len(parts)], mine[3 + len(parts) :],
                   *outs[4 * i : 4 * i + 4])

    operands, in_specs, out_specs, out_shape = [], [], [], []
    for w, m, v, parts, own in groups:
        r, cdim = w.shape
        tr = r // steps
        firsts = first_tiles(w, parts)

        def tile_of(k, t, firsts=firsts, parts=parts, tr=tr):
            return jnp.clip(t - firsts[k], 0, parts[k].shape[1] // tr - 1)

        blk = pl.BlockSpec((tr, cdim), lambda t, *_: (t, 0))
        operands += [w, m, v] + parts + own
        in_specs += [blk, blk, blk]
        in_specs += [pl.BlockSpec((p.shape[0], tr, cdim), lambda t, *_, k=k, tile_of=tile_of: (0, tile_of(k, t), 0)) for k, p in enumerate(parts)]
        in_specs += [pl.BlockSpec((1, tr, cdim), lambda t, slot_ref, k=k, tile_of=tile_of: (slot_ref[0], tile_of(k, t), 0)) for k in range(len(own))]
        out_specs += [blk] * 4
        out_shape += [jax.ShapeDtypeStruct((r, cdim), F32)] * 4
    outs, plan_outs = _call(
        body,
        name=name,
        prefetch=[own_slot] if use_slot else [],
        operands=operands,
        grid=(steps,),
        in_specs=in_specs,
        out_specs=out_specs,
        out_shape=out_shape,
        plans=plans,
    )
    return [outs[4 * i : 4 * i + 4] for i in range(len(groups))], plan_outs


def _chip_routes(n):
    x, y, c = _place()
    my_chip = 2 * x + y
    return [(a, (*chip, c), 2 * chip[0] + chip[1], my_chip, 2 * chip[0] + chip[1])
            for a in range(n) for chip in _other_chips(x, y)]


def _gather_routes(n):
    x, y, c = _place()
    flip = lambda v, f: 1 - v if f else v
    peers = [(flip(x, fx), flip(y, fy), flip(c, fc)) for fx in (0, 1) for fy in (0, 1) for fc in (0, 1) if fx + fy + fc]
    return [(0, p, 0, 4 * x + 2 * y + c, 4 * p[0] + 2 * p[1] + p[2]) for p in peers]


_chip_routes.copies_per_array = N_CHIP - 1
_gather_routes.copies_per_array = N_DEV - 1


def _split_start(name, routes, srcs, land_shapes):
    n = len(srcs)
    hbm = pl.BlockSpec(memory_space=pltpu.HBM)
    sem = pl.BlockSpec(memory_space=pltpu.SEMAPHORE)

    def body(*refs):
        src_refs, land_refs = refs[:n], refs[n : 2 * n]
        send_sems, recv_sems = refs[2 * n], refs[2 * n + 1]
        token = refs[-1]
        for k, (a, peer, src_slot, there, _) in enumerate(routes(n)):
            pltpu.make_async_remote_copy(
                src_ref=src_refs[a].at[src_slot], dst_ref=land_refs[a].at[there],
                send_sem=send_sems.at[k], recv_sem=recv_sems.at[k], device_id=peer, device_id_type=MESH).start()
        token[...] = jnp.zeros_like(token)

    in_hbm = [pltpu.with_memory_space_constraint(p, pltpu.HBM) for p in srcs]
    in_hbm += [pltpu.with_memory_space_constraint(lax.empty(shape, p.dtype), pltpu.HBM) for p, shape in zip(srcs, land_shapes)]
    n_copies = routes.copies_per_array * n
    res = pl.pallas_call(
        body,
        name=name,
        out_shape=[pltpu.SemaphoreType.DMA((n_copies,)), pltpu.SemaphoreType.DMA((n_copies,))]
        + [pltpu.HBM(p.shape, p.dtype) for p in srcs] + [pltpu.HBM(shape, p.dtype) for p, shape in zip(srcs, land_shapes)]
        + [jax.ShapeDtypeStruct((8, CHUNK), F32)],
        in_specs=[hbm] * (2 * n),
        out_specs=[sem, sem] + [hbm] * (2 * n) + [pl.BlockSpec(memory_space=pltpu.VMEM)],
        input_output_aliases={i: 2 + i for i in range(2 * n)},
        compiler_params=pltpu.CompilerParams(has_side_effects=pltpu.SideEffectType.DATAFLOW_SIDE_EFFECTING),
    )(*in_hbm)
    return res[:-1], res[-1]


def _split_wait(name, routes, in_flight, after):
    n = (len(in_flight) - 2) // 2
    hbm = pl.BlockSpec(memory_space=pltpu.HBM)
    sem = pl.BlockSpec(memory_space=pltpu.SEMAPHORE)
    send_sems, recv_sems, *bufs = in_flight

    def body(*refs):
        src_refs, land_refs = refs[:n], refs[n : 2 * n]
        send_ref, recv_ref = refs[2 * n], refs[2 * n + 1]
        for k, (a, peer, src_slot, _, here) in enumerate(routes(n)):
            cp = pltpu.make_async_remote_copy(
                src_ref=src_refs[a].at[src_slot], dst_ref=land_refs[a].at[here],
                send_sem=send_ref.at[k], recv_sem=recv_ref.at[k], device_id=peer, device_id_type=MESH)
            cp.wait_send()
            cp.wait_recv()

    res = pl.pallas_call(
        body,
        name=name,
        out_shape=[pltpu.HBM(b.shape, b.dtype) for b in bufs],
        in_specs=[hbm] * (2 * n) + [sem, sem] + [pl.BlockSpec(memory_space=pl.ANY)] * len(after),
        out_specs=[hbm] * (2 * n),
        input_output_aliases={i: i for i in range(2 * n)},
        compiler_params=pltpu.CompilerParams(has_side_effects=pltpu.SideEffectType.DATAFLOW_SIDE_EFFECTING),
    )(*bufs, send_sems, recv_sems, *after)
    return res[:n], res[n:]


def kernel(x, mem, norm_mix_g, w_in, conv_w, gm_ln_g, gm_ln_b, gm_ws, gm_bs, w_out, norm_x_g, norm_mem_g, w_q, w_kv, w_xo, norm_final_g, loss_target, m_norm_mix_g, m_w_in, m_conv_w, m_gm_ln_g, m_gm_ln_b, m_gm_ws, m_gm_bs, m_w_out, m_norm_x_g, m_norm_mem_g, m_w_q, m_w_kv, m_w_xo, m_norm_final_g, v_norm_mix_g, v_w_in, v_conv_w, v_gm_ln_g, v_gm_ln_b, v_gm_ws, v_gm_bs, v_w_out, v_norm_x_g, v_norm_mem_g, v_w_q, v_w_kv, v_w_xo, v_norm_final_g):
    s = x.shape[1]
    dev = 4 * lax.axis_index("x") + 2 * lax.axis_index("y") + lax.axis_index("c")
    core = lax.axis_index("c").astype(jnp.int32).reshape(1)
    x2 = x[0]
    target = loss_target[0]
    pad_taps = lambda t: jnp.pad(t[0], ((0, 5), (0, 0)))

    big_names = ["w_in", "w_out", "w_q", "w_kv", "w_xo"]
    big_w = dict(w_in=w_in[0], w_out=w_out[0], w_q=w_q[0], w_kv=w_kv[0], w_xo=w_xo[0])
    big_m = dict(w_in=m_w_in[0], w_out=m_w_out[0], w_q=m_w_q[0], w_kv=m_w_kv[0], w_xo=m_w_xo[0])
    big_v = dict(w_in=v_w_in[0], w_out=v_w_out[0], w_q=v_w_q[0], w_kv=v_w_kv[0], w_xo=v_w_xo[0])
    shard_axis = dict(w_in=1, w_out=0, w_q=0, w_kv=1, w_xo=0)
    full_shape = dict(w_in=(D_MODEL, IN_DIM), w_out=(MIX_DIM, D_MODEL), w_q=(D_MODEL, D_MODEL),
                      w_kv=(D_MODEL, 2 * D_MODEL), w_xo=(D_MODEL, D_MODEL))
    shard_size = {k: big_w[k].shape[shard_axis[k]] for k in big_names}
    others = big_names[1:]
    gather_spec = lambda k: (full_shape[k], shard_axis[k], shard_size[k])
    reduce_spec = lambda k: (shard_axis[k], shard_size[k])
    ws = gm_ws[0]
    bs_b = jnp.broadcast_to(gm_bs[0][:, :, None], (8, CHUNK, CHUNK))
    g_f = norm_final_g.reshape(1, D_MODEL)

    px, py = lax.axis_index("x"), lax.axis_index("y")
    chip_order = jnp.stack([2 * px + py, 2 * (1 - px) + py, 2 * px + 1 - py, 2 * (1 - px) + 1 - py]).astype(jnp.int32)
    (proj, h1, h1t, w_in_full), ((w_out_full, conv_g),) = _inproj(
        x2, norm_mix_g, big_w["w_in"].astype(BF16), chip_order,
        plans=[_gather_plan([big_w["w_out"].astype(BF16), pad_taps(conv_w)[None]], [gather_spec("w_out"), ((N_DEV, 8, CHUNK), 0, 1)])])
    conv_full = conv_g.transpose(1, 0, 2).reshape(8, D_MODEL)
    attn_w = ["w_q", "w_kv", "w_xo"]
    cat, (gathered,) = _mixer_fwd(
        proj, conv_full, gm_ln_g, gm_ln_b, ws, bs_b,
        plans=[_gather_plan([big_w[k].astype(BF16) for k in attn_w], [gather_spec(k) for k in attn_w])])
    wf = dict(zip(attn_w, gathered), w_out=w_out_full)
    k, vv, mt = _kv_fwd(mem[0], norm_mem_g, wf["w_kv"])
    (dcat, catt, h2t, ot, dx2b, dqb, dx1b, dk, dvv, d_gf, d_gx, sq_err) = _attn(
        x2, cat, target, wf["w_out"], wf["w_q"], wf["w_xo"], k, vv, norm_x_g, g_f)

    def pair_sums(names, grads, got):
        return [_pair_sum("pair_sum_" + k, grads[k], got[i], *reduce_spec(k), core) for i, k in enumerate(names)]

    sibling_plan = lambda names, grads: _sibling_plan([grads[k] for k in names], [reduce_spec(k) for k in names])
    early, late = ["w_out", "w_kv", "w_q"], ["w_xo"]
    big_g = dict(w_out=_matmul_f32("dw_out", catt, dx1b)[0])
    big_g["w_kv"], d_gmem = _kv_bwd(mem[0], wf["w_kv"], mt, dk, dvv)
    big_g["w_q"], (got_a,) = _matmul_f32("dw_q", h2t, dqb, plans=[sibling_plan(early[:2], big_g)])
    big_g["w_xo"], (got_b,) = _matmul_f32("dw_xo", ot, dx2b, plans=[sibling_plan(early[2:], big_g)])
    got_early = got_a + got_b
    (dproj, d_cw, d_lng, d_lnb, d_ws, d_bs), (parts_early, got_late) = _mixer_bwd(
        proj, dcat, conv_full, gm_ln_g, gm_ln_b, ws, bs_b,
        plans=[_chip_plan(pair_sums(early, big_g, got_early)), sibling_plan(late, big_g)])
    half = D_MODEL // 2
    in_spec = reduce_spec("w_in")
    dw_top, _ = _matmul_f32("dw_in_top", h1t, dproj, rows=(0, half))
    pair_xo = pair_sums(late, big_g, got_late)[0]
    dw_bot, (got_top, (ws_all,)) = _matmul_f32(
        "dw_in_bot", h1t, dproj, rows=(half, half),
        plans=[_sibling_plan([dw_top], [in_spec]),
               _gather_plan([d_ws.reshape(1, 8 * CHUNK, CHUNK).astype(BF16)], [((N_DEV, 8 * CHUNK, CHUNK), 0, 1)])])
    pair_top, ((got_bot,),) = _pair_sum("pair_sum_w_in_top", dw_top, got_top[0], *in_spec, core,
                                        plans=[_sibling_plan([dw_bot], [in_spec])])
    pair_bot = _pair_sum("pair_sum_w_in_bot", dw_bot, got_bot, *in_spec, core)
    chip_shapes = [p.shape for p in (pair_top, pair_bot, pair_xo)]
    chips_in_flight, token = _split_start("grad_chip_exchange_start", _chip_routes, [pair_top, pair_bot, pair_xo], chip_shapes)
    n_tiles = s // min(TM_DH1, s)
    (grad_x, d_gmix), _ = _dh1("dh1", dproj, w_in_full, x2, dx1b, norm_mix_g, (0, n_tiles), after=[token])
    vec_names = ["norm_mix_g", "gm_ln_g", "gm_ln_b", "gm_bs", "norm_x_g", "norm_mem_g", "norm_final_g"]
    vec_g = [d_gmix, d_lng, d_lnb, d_bs, d_gx, d_gmem, d_gf]
    small = jnp.concatenate(vec_g + [sq_err, d_cw], axis=0)[None]
    small_in_flight, _ = _split_start("small_all_gather_start", _gather_routes, [small], [(N_DEV, 16, D_MODEL)])
    flat_ws = lambda t: t.reshape(8 * CHUNK, CHUNK)
    updated, _ = _adamw(
        "adamw_early",
        [(big_w[k], big_m[k], big_v[k], got_k) for k, got_k in zip(early, parts_early)]
        + [(flat_ws(ws), flat_ws(m_gm_ws), flat_ws(v_gm_ws), ws_all)], steps=2)
    big_out = dict(zip(early, updated[:3]))
    ws_out = updated[3]
    (pair_top, pair_bot, pair_xo), (land_top, land_bot, land_xo) = _split_wait(
        "grad_chip_exchange_wait", _chip_routes, chips_in_flight, [d_gmix] + [u[0] for u in updated])
    my_chip = (2 * px + py).astype(jnp.int32).reshape(1)
    (big_out["w_in"], big_out["w_xo"]), _ = _adamw(
        "adamw_late",
        [(big_w["w_in"], big_m["w_in"], big_v["w_in"], [land_top, land_bot], [pair_top, pair_bot]),
         (big_w["w_xo"], big_m["w_xo"], big_v["w_xo"], land_xo, pair_xo)],
        steps=4, own_slot=my_chip)
    (small,), (small_all,) = _split_wait("small_all_gather_wait", _gather_routes, small_in_flight, [big_out["w_xo"][0]])
    row = lambda t: t.reshape(1, D_MODEL)
    vec_out, conv_out, loss_row = _update_small(
        dev.astype(jnp.int32).reshape(1),
        [norm_mix_g, gm_ln_g, gm_ln_b, row(gm_bs), norm_x_g, norm_mem_g, row(norm_final_g)],
        [m_norm_mix_g, m_gm_ln_g, m_gm_ln_b, row(m_gm_bs), m_norm_x_g, m_norm_mem_g, row(m_norm_final_g)],
        [v_norm_mix_g, v_gm_ln_g, v_gm_ln_b, row(v_gm_bs), v_norm_x_g, v_norm_mem_g, row(v_norm_final_g)],
        conv_w, m_conv_w, v_conv_w, small, small_all)
    loss = loss_row[0, 0]

    def result(name, which):
        if name in big_out:
            return big_out[name][which][None]
        if name == "conv_w":
            return conv_out[which]
        if name == "gm_ws":
            return ws_out[which].reshape(1, 8, CHUNK, CHUNK)
        r = vec_out[vec_names.index(name)][which]
        if name == "gm_bs":
            return r.reshape(1, 8, CHUNK)
        if name == "norm_final_g":
            return r.reshape(D_MODEL)
        return r

    weights = ["norm_mix_g", "w_in", "conv_w", "gm_ln_g", "gm_ln_b", "gm_ws", "gm_bs", "w_out", "norm_x_g", "norm_mem_g", "w_q", "w_kv", "w_xo", "norm_final_g"]
    outs = [loss, grad_x[None]]
    for which in range(4):
        outs += [result(name, which) for name in weights]
    return tuple(outs)
```

```python
import math

import jax
import jax.numpy as jnp
from jax import lax
from jax.experimental import pallas as pl
from jax.experimental.pallas import tpu as pltpu

F32 = jnp.float32
BF16 = jnp.bfloat16
MESH = pl.DeviceIdType.MESH

D_MODEL = 1024
N_DEV = 8
N_CHIP = 4
CHUNK = 128
GROUPS = 7
IN_DIM = GROUPS * D_MODEL
SHARD_IN = IN_DIM // N_DEV
MIX_DIM = 2 * D_MODEL
X_HEADS = 4
X_HEAD_DIM = D_MODEL // X_HEADS
MEM_LEN = 256
EPS = 1e-6
GELU_K0 = math.sqrt(2.0 / math.pi)
GELU_K1 = 0.044715

ADAM_LR = 0.001
ADAM_B1 = 0.9
ADAM_B2 = 0.999
ADAM_EPS = 1e-08
ADAM_WD = 0.01
ADAM_STEP = 10

TM_INPROJ = 512
TM_ATTN = 512
SUB_ATTN = 256
TM_DH1 = 512
TM_DW = 512
TK_DW = 4096
TR_PAIR_SUM = 1024
STEPS_ADAMW = 8


def _dot(a, b):
    return lax.dot_general(a, b, (((1,), (0,)), ((), ())), preferred_element_type=F32)


def _dot_nt(a, b):
    return lax.dot_general(a, b, (((1,), (1,)), ((), ())), preferred_element_type=F32)


def _rows8(v):
    return jnp.sum(v.reshape(v.shape[0] // 8, 8, v.shape[1]), axis=0)


def _gelu(x, with_grad=True):
    x2 = x * x
    t = jnp.tanh(x * (GELU_K0 + (GELU_K0 * GELU_K1) * x2))
    half = 0.5 + 0.5 * t
    g = x * half
    if not with_grad:
        return g
    return g, half + ((0.5 * x) * (1.0 - t * t)) * (GELU_K0 + (3.0 * GELU_K0 * GELU_K1) * x2)


def _silu(z, with_grad=True):
    s = jax.nn.sigmoid(z)
    zs = z * s
    return (zs, s + zs * (1.0 - s)) if with_grad else zs


def _rms(v):
    r = lax.rsqrt(jnp.mean(v * v, axis=-1, keepdims=True) + EPS)
    return v * r, r


def _rms_bwd(dy_g, vh, r):
    return r * (dy_g - vh * jnp.mean(dy_g * vh, axis=-1, keepdims=True))


def _place():
    return lax.axis_index("x"), lax.axis_index("y"), lax.axis_index("c")


def _other_chips(x, y):
    return [(1 - x, y), (x, 1 - y), (1 - x, 1 - y)]


def _pow2_divisor(n):
    return n & (-n)


def _shard_view(ref, axis, size, d):
    start = d * size
    align = _pow2_divisor(size)
    if align > 1:
        start = pl.multiple_of(start, align)
    idx = [slice(None)] * len(ref.shape)
    idx[axis] = pl.ds(start, size)
    return ref.at[tuple(idx)]


class _Plan:
    def __init__(self, ins, out_shape, sems, start, finish, mid=None, aliases=None):
        self.ins, self.out_shape, self.sems = list(ins), list(out_shape), list(sems)
        self.start, self.mid, self.finish = start, mid, finish
        self.aliases = dict(aliases or {})


def _gather_plan(shards, specs):
    n = len(shards)
    n_copy = 7

    def copies(ins, outs, sems):
        send_sems, recv_sems, local_sems = sems
        x, y, c = _place()
        me, sibling = (x, y, c), (x, y, 1 - c)
        chips = _other_chips(x, y)

        def block(a, p):
            _, axis, size = specs[a]
            return _shard_view(outs[a], axis, size, 4 * p[0] + 2 * p[1] + p[2])

        def copy(a, k, p, to, src=None):
            return pltpu.make_async_remote_copy(
                src_ref=block(a, p) if src is None else src,
                dst_ref=block(a, p),
                send_sem=send_sems.at[a * n_copy + k],
                recv_sem=recv_sems.at[a * n_copy + k],
                device_id=to,
                device_id_type=MESH,
            )

        def mine():
            return [pltpu.make_async_copy(ins[a], block(a, me), local_sems.at[a]) for a in range(n)]

        def first():
            return [cp for a in range(n) for cp in
                    [copy(a, 0, me, sibling, src=ins[a])] + [copy(a, 1 + j, me, (*chip, c), src=ins[a]) for j, chip in enumerate(chips)]]

        def landed():
            return [copy(a, 1 + j, (*chip, c), me) for j, chip in enumerate(chips) for a in range(n)]

        def passed():
            return [copy(a, 4 + j, (*chip, c), sibling) for j, chip in enumerate(chips) for a in range(n)]

        def last():
            return [cp for a in range(n) for cp in
                    [copy(a, 0, sibling, me)] + [copy(a, 4 + j, (*chip, 1 - c), me) for j, chip in enumerate(chips)]]

        return mine, first, landed, passed, last

    def start(ins, outs, sems):
        mine, first, _, _, _ = copies(ins, outs, sems)
        for cp in mine() + first():
            cp.start()

    def mid(ins, outs, sems):
        _, _, landed, passed, _ = copies(ins, outs, sems)
        for got, fwd in zip(landed(), passed()):
            got.wait_recv()
            fwd.start()

    def finish(ins, outs, sems):
        mine, first, _, passed, last = copies(ins, outs, sems)
        for cp in last():
            cp.wait_recv()
        for cp in first() + passed():
            cp.wait_send()
        for cp in mine():
            cp.wait()

    return _Plan(
        shards,
        [jax.ShapeDtypeStruct(full, s.dtype) for s, (full, _, _) in zip(shards, specs)],
        [pltpu.SemaphoreType.DMA((n * n_copy,)), pltpu.SemaphoreType.DMA((n * n_copy,)), pltpu.SemaphoreType.DMA((n,))],
        start, finish, mid,
    )


def _sibling_plan(grads, specs):
    n = len(grads)

    def shard_shape(a):
        axis, size = specs[a]
        shp = list(grads[a].shape)
        shp[axis] = size
        return tuple(shp)

    def copies(ins, outs, sems):
        send_sems, recv_sems = sems
        x, y, c = _place()
        return [
            pltpu.make_async_remote_copy(
                src_ref=_shard_view(ins[a], specs[a][0], specs[a][1], 2 * q + (1 - c)),
                dst_ref=outs[a].at[q],
                send_sem=send_sems.at[a * N_CHIP + q],
                recv_sem=recv_sems.at[a * N_CHIP + q],
                device_id=(x, y, 1 - c),
                device_id_type=MESH,
            )
            for a in range(n) for q in range(N_CHIP)
        ]

    def start(ins, outs, sems):
        for cp in copies(ins, outs, sems):
            cp.start()

    def finish(ins, outs, sems):
        for cp in copies(ins, outs, sems):
            cp.wait()

    return _Plan(
        grads,
        [jax.ShapeDtypeStruct((N_CHIP, *shard_shape(a)), grads[a].dtype) for a in range(n)],
        [pltpu.SemaphoreType.DMA((n * N_CHIP,)), pltpu.SemaphoreType.DMA((n * N_CHIP,))],
        start, finish,
    )


def _chip_plan(parts, full_rows=None, row0=None, into=None):
    n = len(parts)
    full_rows = [p.shape[1] for p in parts] if full_rows is None else full_rows
    row0 = [0] * n if row0 is None else row0

    def copies(ins, outs, sems):
        send_sems, recv_sems, local_sems = sems
        x, y, c = _place()
        my_chip = 2 * x + y
        chips = _other_chips(x, y)

        def land(a, q):
            return outs[a].at[q, pl.ds(row0[a], parts[a].shape[1]), :]

        def mine():
            return [pltpu.make_async_copy(ins[a].at[my_chip], land(a, my_chip), local_sems.at[a]) for a in range(n)]

        def remote(sending):
            return [
                pltpu.make_async_remote_copy(
                    src_ref=ins[a].at[2 * chip[0] + chip[1]],
                    dst_ref=land(a, my_chip if sending else 2 * chip[0] + chip[1]),
                    send_sem=send_sems.at[a * 3 + j], recv_sem=recv_sems.at[a * 3 + j],
                    device_id=(*chip, c), device_id_type=MESH)
                for a in range(n) for j, chip in enumerate(chips)
            ]

        return mine, remote

    def start(ins, outs, sems):
        mine, remote = copies(ins, outs, sems)
        for cp in mine() + remote(True):
            cp.start()

    def finish(ins, outs, sems):
        mine, remote = copies(ins, outs, sems)
        for cp in remote(False):
            cp.wait_recv()
        for cp in remote(True):
            cp.wait_send()
        for cp in mine():
            cp.wait()

    return _Plan(
        list(parts) + (list(into) if into is not None else []),
        [jax.ShapeDtypeStruct((N_CHIP, full_rows[a], parts[a].shape[2]), parts[a].dtype) for a in range(n)],
        [pltpu.SemaphoreType.DMA((n * 3,)), pltpu.SemaphoreType.DMA((n * 3,)), pltpu.SemaphoreType.DMA((n,))],
        start, finish,
        aliases={n + a: a for a in range(n)} if into is not None else None,
    )


def _call(body, *, name, operands, out_shape, grid=(), in_specs=None, out_specs=None, scratch_shapes=(),
          plans=(), aliases=None, prefetch=(), start_after_body=False, pass_on_at=0.6):
    operands, out_shape, scratch_shapes = list(operands), list(out_shape), list(scratch_shapes)
    n_pf, n_in, n_out, n_scr = len(prefetch), len(operands), len(out_shape), len(scratch_shapes)
    in_vmem = pl.BlockSpec(memory_space=pltpu.VMEM)
    in_hbm = pl.BlockSpec(memory_space=pl.ANY)
    in_specs = [in_vmem] * n_in if in_specs is None else list(in_specs)
    out_specs = [in_vmem] * n_out if out_specs is None else list(out_specs)
    io_alias = {n_pf + i: o for i, o in (aliases or {}).items()}
    pos_in, pos_out = n_pf + n_in, n_out
    for p in plans:
        for i, o in p.aliases.items():
            io_alias[pos_in + i] = pos_out + o
        pos_in += len(p.ins)
        pos_out += len(p.out_shape)
    steps = math.prod(grid) if grid else 1
    mid_step = min(steps - 1, int(steps * pass_on_at))

    def wrapped(*refs):
        refs = list(refs)
        take = lambda k: [refs.pop(0) for _ in range(k)]
        pf = take(n_pf)
        ins, p_ins = take(n_in), [take(len(p.ins)) for p in plans]
        outs, p_outs = take(n_out), [take(len(p.out_shape)) for p in plans]
        scr, p_sems = take(n_scr), [take(len(p.sems)) for p in plans]
        step = 0
        for ax, g in enumerate(grid):
            step = step * g + pl.program_id(ax)

        def hook(kind, at):
            todo = [(getattr(p, kind), a, b, c) for p, a, b, c in zip(plans, p_ins, p_outs, p_sems) if getattr(p, kind)]

            def run():
                for fn, a, b, c in todo:
                    fn(a, b, c)

            if todo and grid:
                pl.when(step == at)(run)
            elif todo:
                run()

        if not start_after_body:
            hook("start", 0)
        if body is not None:
            body(*pf, *ins, *outs, *scr)
        if start_after_body:
            hook("start", 0)
        hook("mid", mid_step)
        hook("finish", steps - 1)

    layout = dict(
        grid=grid,
        in_specs=in_specs + [in_hbm] * sum(len(p.ins) for p in plans),
        out_specs=out_specs + [in_hbm] * sum(len(p.out_shape) for p in plans),
        scratch_shapes=scratch_shapes + [s for p in plans for s in p.sems],
    )
    if n_pf:
        layout = dict(grid_spec=pltpu.PrefetchScalarGridSpec(num_scalar_prefetch=n_pf, **layout))
    res = pl.pallas_call(
        wrapped,
        name=name,
        out_shape=out_shape + [o for p in plans for o in p.out_shape],
        input_output_aliases=io_alias,
        compiler_params=pltpu.CompilerParams(dimension_semantics=("arbitrary",) * len(grid)) if grid else None,
        **layout,
    )(*prefetch, *operands, *[a for p in plans for a in p.ins])
    res = list(res)
    outs, plan_outs = res[:n_out], []
    pos = n_out
    for p in plans:
        plan_outs.append(res[pos : pos + len(p.out_shape)])
        pos += len(p.out_shape)
    return outs, plan_outs


def _inproj(x, g, w_shard, order, plans=()):
    s = x.shape[0]
    tm = min(TM_INPROJ, s)
    nt = s // tm
    wide = 2 * SHARD_IN
    fwd_at = nt // 2

    def body(order_ref, x_ref, g_ref, w_hbm, proj_ref, h1_ref, h1t_ref, wfull_hbm,
             w_scr, h1_scr, send_sems, recv_sems, local_sems):
        k, i = pl.program_id(0), pl.program_id(1)
        px, py, c = _place()
        me, sibling = (px, py, c), (px, py, 1 - c)
        chips = _other_chips(px, py)

        def half(slot, core):
            return w_scr.at[slot, :, pl.ds(pl.multiple_of(core * SHARD_IN, 128), SHARD_IN)]

        def copy(n, slot, core, to, src=None):
            return pltpu.make_async_remote_copy(
                src_ref=half(slot, core) if src is None else src, dst_ref=half(slot, core),
                send_sem=send_sems.at[n], recv_sem=recv_sems.at[n], device_id=to, device_id_type=MESH)

        def mine():
            return pltpu.make_async_copy(w_hbm, half(0, c), local_sems.at[0])

        def first():
            return [copy(0, 0, c, sibling, src=w_hbm)] + [copy(1 + j, 1 + j, c, (*chips[j], c), src=w_hbm) for j in range(2)]

        def relay():
            to = (c * px + (1 - c) * (1 - px), c * (1 - py) + (1 - c) * py, c)
            return pltpu.make_async_remote_copy(
                src_ref=half(2 - c, c), dst_ref=half(3, c), send_sem=send_sems.at[3], recv_sem=recv_sems.at[3],
                device_id=to, device_id_type=MESH)

        def save(slot):
            cols = pl.ds(pl.multiple_of(order_ref[slot] * wide, 128), wide)
            return pltpu.make_async_copy(w_scr.at[slot], wfull_hbm.at[:, cols], local_sems.at[1 + slot])

        @pl.when((k == 0) & (i == 0))
        def _():
            mine().start()
            for cp in first():
                cp.start()
            mine().wait()
            copy(0, 0, 1 - c, me).wait_recv()

        for j in range(3):
            @pl.when((k == j + 1) & (i == 0))
            def _():
                copy(4 + j, 1 + j, 1 - c, me).wait_recv()

        @pl.when((k == 3) & (i == 0))
        def _():
            for slot in range(N_CHIP):
                save(slot).start()

        @pl.when(k == 0)
        def _():
            xh, _ = _rms(x_ref[...])
            h = xh * g_ref[...]
            h1_ref[...] = h.astype(BF16)
            h1t_ref[...] = h.T.astype(BF16)
            h1_scr[pl.ds(pl.multiple_of(i * tm, tm), tm), :] = h.astype(BF16)

        proj_ref[...] = _dot(h1_scr[pl.ds(pl.multiple_of(i * tm, tm), tm), :], w_scr[k]).astype(BF16)

        for j in range(3):
            @pl.when((k == 0) & (i == nt - 1) if j < 2 else (k == 2) & (i == fwd_at))
            def _():
                copy(1 + j, 1 + j, c, me).wait_recv()
                copy(4 + j, 1 + j, c, sibling).start()
                if j < 2:
                    pl.when(c == 1 - j)(lambda: relay().start())

        @pl.when((k == 3) & (i == nt - 1))
        def _():
            for cp in first() + [relay()] + [copy(4 + j, 1 + j, c, sibling) for j in range(3)]:
                cp.wait_send()
            for slot in range(N_CHIP):
                save(slot).wait()

    once = lambda k, i, order_ref: (jnp.where(k == 0, i, nt - 1), 0)
    return _call(
        body,
        name="inproj",
        prefetch=[order],
        operands=[x, g, w_shard],
        grid=(N_CHIP, nt),
        in_specs=[
            pl.BlockSpec((tm, D_MODEL), once),
            pl.BlockSpec((1, D_MODEL), lambda k, i, order_ref: (0, 0)),
            pl.BlockSpec(memory_space=pl.ANY),
        ],
        out_specs=[
            pl.BlockSpec((tm, wide), lambda k, i, order_ref: (i, order_ref[k])),
            pl.BlockSpec((tm, D_MODEL), once),
            pl.BlockSpec((D_MODEL, tm), lambda k, i, order_ref: (0, jnp.where(k == 0, i, nt - 1))),
            pl.BlockSpec(memory_space=pl.ANY),
        ],
        out_shape=[
            jax.ShapeDtypeStruct((s, IN_DIM), BF16),
            jax.ShapeDtypeStruct((s, D_MODEL), BF16),
            jax.ShapeDtypeStruct((D_MODEL, s), BF16),
            jax.ShapeDtypeStruct((D_MODEL, IN_DIM), BF16),
        ],
        scratch_shapes=[
            pltpu.VMEM((N_CHIP, D_MODEL, wide), BF16),
            pltpu.VMEM((s, D_MODEL), BF16),
            pltpu.SemaphoreType.DMA((7,)),
            pltpu.SemaphoreType.DMA((7,)),
            pltpu.SemaphoreType.DMA((1 + N_CHIP,)),
        ],
        plans=plans,
        start_after_body=True,
        pass_on_at=0.8,
    )


def _rows_from_above(v, above):
    row = lax.broadcasted_iota(jnp.int32, (8, v.shape[1]), 0)
    r1, r2 = pltpu.roll(v, 1, 0), pltpu.roll(v, 2, 0)
    top1 = jnp.where(row == 0, above[7:8], r1[0:8])
    top2 = jnp.where(row == 0, above[6:7], jnp.where(row == 1, above[7:8], r2[0:8]))
    return jnp.concatenate([top1, r1[8:]], axis=0), jnp.concatenate([top2, r2[8:]], axis=0)


def _rows_from_below(v, below):
    n = v.shape[0]
    row = lax.broadcasted_iota(jnp.int32, (8, v.shape[1]), 0)
    r1, r2 = pltpu.roll(v, n - 1, 0), pltpu.roll(v, n - 2, 0)
    end1 = jnp.where(row == 7, below[0:1], r1[n - 8 :])
    end2 = jnp.where(row == 6, below[0:1], jnp.where(row == 7, below[1:2], r2[n - 8 :]))
    return jnp.concatenate([r1[: n - 8], end1], axis=0), jnp.concatenate([r2[: n - 8], end2], axis=0)


def _causal_mask():
    row = lax.broadcasted_iota(jnp.int32, (CHUNK, CHUNK), 0)
    col = lax.broadcasted_iota(jnp.int32, (CHUNK, CHUNK), 1)
    return row, row >= col


def _mixer_fwd(proj, conv_w, ln_g, ln_b, ws, bs_b, plans=()):
    s = proj.shape[0]
    nt = s // CHUNK

    def body(proj_ref, cw_ref, lng_ref, lnb_ref, ws_ref, bsb_ref, cat_ref, pcar_ref):
        @pl.when(pl.program_id(0) == 0)
        def _():
            pcar_ref[...] = jnp.zeros_like(pcar_ref)

        row, tril = _causal_mask()
        for j in range(8):
            cs = slice(CHUNK * j, CHUNK * (j + 1))

            def grp(k):
                return proj_ref[:, k * D_MODEL + CHUNK * j : k * D_MODEL + CHUNK * (j + 1)].astype(F32)

            w = cw_ref[:, cs]
            p = grp(1) * grp(2)
            p1, p2 = _rows_from_above(p, pcar_ref[:, cs])
            pcar_ref[:, cs] = p[CHUNK - 8 :]
            cv = w[0:1] * p2 + w[1:2] * p1 + w[2:3] * p
            sa = _silu(grp(3), with_grad=False)
            cat_ref[:, cs] = ((grp(0) * cv) * sa).astype(BF16)
            gu = _gelu(grp(4), with_grad=False)
            gv = _gelu(grp(5), with_grad=False)
            dv = gv - jnp.mean(gv, axis=-1, keepdims=True)
            vn = dv * lax.rsqrt(jnp.mean(dv * dv, axis=-1, keepdims=True) + EPS)
            vn = vn * lng_ref[:, cs] + lnb_ref[:, cs]
            wc = jnp.where(tril, ws_ref[j], 0.0).astype(BF16)
            sp = _dot(wc, vn.astype(BF16)) + bsb_ref[j]
            sb = _silu(grp(6), with_grad=False)
            cat_ref[:, D_MODEL + CHUNK * j : D_MODEL + CHUNK * (j + 1)] = ((gu * sp) * sb).astype(BF16)

    whole = lambda shape: pl.BlockSpec(shape, lambda i: (0,) * len(shape))
    (cat,), plan_outs = _call(
        body,
        name="mixer_fwd",
        operands=[proj, conv_w, ln_g, ln_b, ws, bs_b],
        grid=(nt,),
        in_specs=[
            pl.BlockSpec((CHUNK, IN_DIM), lambda i: (i, 0)),
            whole((8, D_MODEL)),
            whole((1, D_MODEL)),
            whole((1, D_MODEL)),
            whole((8, CHUNK, CHUNK)),
            whole((8, CHUNK, CHUNK)),
        ],
        out_specs=[pl.BlockSpec((CHUNK, MIX_DIM), lambda i: (i, 0))],
        out_shape=[jax.ShapeDtypeStruct((s, MIX_DIM), BF16)],
        scratch_shapes=[pltpu.VMEM((8, D_MODEL), F32)],
        plans=plans,
        pass_on_at=0.85,
    )
    return cat, plan_outs


def _mixer_bwd(proj, dcat, conv_w, ln_g, ln_b, ws, bs_b, plans=()):
    s = proj.shape[0]
    nt = s // CHUNK

    def body(proj_ref, halo_ref, dcat_ref, cw_ref, lng_ref, lnb_ref, ws_ref, bsb_ref,
             dproj_ref, dcw_ref, dlng_ref, dlnb_ref, dws_ref, dbs_ref,
             car_ref, acc_cw, acc_lng, acc_lnb, acc_bs, wc_scr, wct_scr):
        i = pl.program_id(0)
        tile = nt - 1 - i
        row, tril = _causal_mask()

        @pl.when(i == 0)
        def _():
            for j in range(8):
                wc = jnp.where(tril, ws_ref[j], 0.0)
                wc_scr[j] = wc.astype(BF16)
                wct_scr[j] = wc.T.astype(BF16)
            car_ref[...] = jnp.zeros_like(car_ref)
            acc_cw[...] = jnp.zeros_like(acc_cw)
            acc_lng[...] = jnp.zeros_like(acc_lng)
            acc_lnb[...] = jnp.zeros_like(acc_lnb)
            acc_bs[...] = jnp.zeros_like(acc_bs)
            dws_ref[...] = jnp.zeros_like(dws_ref)

        has_prev = jnp.where(tile > 0, 1.0, 0.0).astype(F32)
        for j in range(8):
            cs = slice(CHUNK * j, CHUNK * (j + 1))

            def col(k):
                return slice(k * D_MODEL + CHUNK * j, k * D_MODEL + CHUNK * (j + 1))

            def grp(k):
                return proj_ref[:, col(k)].astype(F32)

            w = cw_ref[:, cs]
            gb, gc, xa = grp(0), grp(1), grp(2)
            p = gc * xa
            pprev = halo_ref[:, col(1)].astype(F32) * halo_ref[:, col(2)].astype(F32) * has_prev
            p1, p2 = _rows_from_above(p, pprev[8:16])
            cv = w[0:1] * p2 + w[1:2] * p1 + w[2:3] * p
            za = grp(3)
            sa, dsa = _silu(za)
            da = dcat_ref[:, cs].astype(F32)
            da_gb = da * gb
            dproj_ref[:, col(0)] = ((da * sa) * cv).astype(BF16)
            dproj_ref[:, col(3)] = ((da_gb * cv) * dsa).astype(BF16)
            dcv = da_gb * sa
            d1, d2 = _rows_from_below(dcv, car_ref[:, cs])
            car_ref[:, cs] = dcv[0:8]
            dp = w[2:3] * dcv + w[1:2] * d1 + w[0:1] * d2
            dproj_ref[:, col(1)] = (dp * xa).astype(BF16)
            dproj_ref[:, col(2)] = (dp * gc).astype(BF16)
            acc_cw[0, :, cs] += _rows8(dcv * p2)
            acc_cw[1, :, cs] += _rows8(dcv * p1)
            acc_cw[2, :, cs] += _rows8(dcv * p)
            gu, dgu = _gelu(grp(4))
            gv, dgv = _gelu(grp(5))
            dv = gv - jnp.mean(gv, axis=-1, keepdims=True)
            rstd = lax.rsqrt(jnp.mean(dv * dv, axis=-1, keepdims=True) + EPS)
            vnh = dv * rstd
            lng = lng_ref[:, cs]
            vnb = (vnh * lng + lnb_ref[:, cs]).astype(BF16)
            sp = _dot(wc_scr[j], vnb) + bsb_ref[j]
            zb = grp(6)
            sb, dsb = _silu(zb)
            db = dcat_ref[:, D_MODEL + CHUNK * j : D_MODEL + CHUNK * (j + 1)].astype(F32)
            db_sp, db_gu = db * sp, db * gu
            dproj_ref[:, col(4)] = ((db_sp * sb) * dgu).astype(BF16)
            dproj_ref[:, col(6)] = ((db_sp * gu) * dsb).astype(BF16)
            dsp = db_gu * sb
            acc_bs[j] += dsp
            dspb = dsp.astype(BF16)
            dws_ref[j] += _dot_nt(dspb, vnb)
            dvn = _dot(wct_scr[j], dspb)
            acc_lnb[:, cs] += _rows8(dvn)
            acc_lng[:, cs] += _rows8(dvn * vnh)
            dvh = dvn * lng
            dgvv = rstd * (dvh - jnp.mean(dvh, axis=-1, keepdims=True) - vnh * jnp.mean(dvh * vnh, axis=-1, keepdims=True))
            dproj_ref[:, col(5)] = (dgvv * dgv).astype(BF16)

        @pl.when(i == nt - 1)
        def _():
            dlng_ref[...] = jnp.sum(acc_lng[...], axis=0, keepdims=True)
            dlnb_ref[...] = jnp.sum(acc_lnb[...], axis=0, keepdims=True)
            dcw_ref[...] = jnp.zeros_like(dcw_ref)
            for k in range(3):
                dcw_ref[k : k + 1, :] = jnp.sum(acc_cw[k], axis=0, keepdims=True)
            for j in range(8):
                dws_ref[j] = jnp.where(tril, dws_ref[j], 0.0)
                dbs_ref[:, CHUNK * j : CHUNK * (j + 1)] = jnp.sum(acc_bs[j].T, axis=0, keepdims=True)

    whole = lambda shape: pl.BlockSpec(shape, lambda i: (0,) * len(shape))
    halo_rows = 16
    per = CHUNK // halo_rows
    return _call(
        body,
        name="mixer_bwd",
        operands=[proj, proj, dcat, conv_w, ln_g, ln_b, ws, bs_b],
        plans=plans,
        grid=(nt,),
        in_specs=[
            pl.BlockSpec((CHUNK, IN_DIM), lambda i: (nt - 1 - i, 0)),
            pl.BlockSpec((halo_rows, IN_DIM), lambda i: (jnp.maximum((nt - 1 - i) * per - 1, 0), 0)),
            pl.BlockSpec((CHUNK, MIX_DIM), lambda i: (nt - 1 - i, 0)),
            whole((8, D_MODEL)),
            whole((1, D_MODEL)),
            whole((1, D_MODEL)),
            whole((8, CHUNK, CHUNK)),
            whole((8, CHUNK, CHUNK)),
        ],
        out_specs=[
            pl.BlockSpec((CHUNK, IN_DIM), lambda i: (nt - 1 - i, 0)),
            whole((8, D_MODEL)),
            whole((1, D_MODEL)),
            whole((1, D_MODEL)),
            whole((8, CHUNK, CHUNK)),
            whole((1, D_MODEL)),
        ],
        out_shape=[
            jax.ShapeDtypeStruct((s, IN_DIM), BF16),
            jax.ShapeDtypeStruct((8, D_MODEL), F32),
            jax.ShapeDtypeStruct((1, D_MODEL), F32),
            jax.ShapeDtypeStruct((1, D_MODEL), F32),
            jax.ShapeDtypeStruct((8, CHUNK, CHUNK), F32),
            jax.ShapeDtypeStruct((1, D_MODEL), F32),
        ],
        scratch_shapes=[
            pltpu.VMEM((8, D_MODEL), F32),
            pltpu.VMEM((3, 8, D_MODEL), F32),
            pltpu.VMEM((8, D_MODEL), F32),
            pltpu.VMEM((8, D_MODEL), F32),
            pltpu.VMEM((8, CHUNK, CHUNK), F32),
            pltpu.VMEM((8, CHUNK, CHUNK), BF16),
            pltpu.VMEM((8, CHUNK, CHUNK), BF16),
        ],
    )


def _kv_fwd(mem, g_mem, w_kv):
    def body(mem_ref, g_ref, w_ref, k_ref, vv_ref, mt_ref):
        mh, _ = _rms(mem_ref[...])
        m = mh * g_ref[...]
        kv = _dot(m.astype(BF16), w_ref[...])
        k_ref[...] = kv[:, :D_MODEL].astype(BF16)
        vv_ref[...] = kv[:, D_MODEL:].astype(BF16)
        mt_ref[...] = m.T.astype(BF16)

    return pl.pallas_call(
        body,
        name="kv_fwd",
        out_shape=[
            jax.ShapeDtypeStruct((MEM_LEN, D_MODEL), BF16),
            jax.ShapeDtypeStruct((MEM_LEN, D_MODEL), BF16),
            jax.ShapeDtypeStruct((D_MODEL, MEM_LEN), BF16),
        ],
    )(mem, g_mem, w_kv)


def _kv_bwd(mem, w_kv, mt, dk, dvv):
    def body(mem_ref, w_ref, mt_ref, dk_ref, dvv_ref, dw_ref, dg_ref):
        dkv = jnp.concatenate([dk_ref[...], dvv_ref[...]], axis=1).astype(BF16)
        dw_ref[...] = _dot(mt_ref[...], dkv)
        dm = _dot_nt(dkv, w_ref[...])
        mh, _ = _rms(mem_ref[...])
        dg_ref[...] = jnp.sum(dm * mh, axis=0, keepdims=True)

    return pl.pallas_call(
        body,
        name="kv_bwd",
        out_shape=[
            jax.ShapeDtypeStruct((D_MODEL, 2 * D_MODEL), F32),
            jax.ShapeDtypeStruct((1, D_MODEL), F32),
        ],
    )(mem, w_kv, mt, dk, dvv)


def _attn(x, cat, target, w_out, w_q, w_xo, k, vv, g_x, g_f):
    s = x.shape[0]
    tm = min(TM_ATTN, s)
    sub = min(SUB_ATTN, tm)
    nt = s // tm
    scale = 1.0 / math.sqrt(X_HEAD_DIM)

    def body(x_ref, cat_ref, t_ref, wout_ref, wq_ref, wxo_ref, k_ref, vv_ref, gx_ref, gf_ref,
             dcat_ref, catt_ref, h2t_ref, ot_ref, dx2b_ref, dqb_ref, dx1b_ref,
             dk_ref, dvv_ref, dgf_ref, dgx_ref, loss_ref,
             p_scr, q_scr, o_scr, dq_scr, acc_gf, acc_gx, acc_loss):
        i = pl.program_id(0)

        @pl.when(i == 0)
        def _():
            dk_ref[...] = jnp.zeros_like(dk_ref)
            dvv_ref[...] = jnp.zeros_like(dvv_ref)
            acc_gf[...] = jnp.zeros_like(acc_gf)
            acc_gx[...] = jnp.zeros_like(acc_gx)
            acc_loss[...] = jnp.zeros_like(acc_loss)

        subs = [slice(u * sub, (u + 1) * sub) for u in range(tm // sub)]
        heads = [slice(X_HEAD_DIM * h, X_HEAD_DIM * (h + 1)) for h in range(X_HEADS)]
        gx, gf = gx_ref[...], gf_ref[...]
        x1, x1h, r2 = [], [], []
        for rs in subs:
            cat = cat_ref[rs, :]
            catt_ref[:, rs] = cat.astype(F32).T.astype(BF16)
            v = x_ref[rs, :] + _dot(cat, wout_ref[...])
            vh, r = _rms(v)
            h2 = vh * gx
            h2t_ref[:, rs] = h2.T.astype(BF16)
            q_scr[rs, :] = _dot(h2.astype(BF16), wq_ref[...]).astype(BF16)
            x1.append(v), x1h.append(vh), r2.append(r)
        for rs in subs:
            for hs in heads:
                sc = _dot_nt(q_scr[rs, hs], k_ref[:, hs]) * scale
                e = jnp.exp(sc - jnp.max(sc, axis=-1, keepdims=True))
                p = e / jnp.sum(e, axis=-1, keepdims=True)
                p_scr[rs, hs] = p
                o_scr[rs, hs] = _dot(p.astype(BF16), vv_ref[:, hs])
        dx2, sq, d_gf = [], None, None
        for u, rs in enumerate(subs):
            o = o_scr[rs, :]
            ot_ref[:, rs] = o.T.astype(BF16)
            x2 = x1[u] + _dot(o.astype(BF16), wxo_ref[...])
            x2h, r3 = _rms(x2)
            err = x2h * gf - t_ref[rs, :]
            dy = err * (1.0 / D_MODEL)
            sq = _rows8(err * err) if sq is None else sq + _rows8(err * err)
            d_gf = _rows8(dy * x2h) if d_gf is None else d_gf + _rows8(dy * x2h)
            dx2.append(_rms_bwd(dy * gf, x2h, r3))
        acc_loss[...] += sq
        acc_gf[...] += d_gf
        d_vv, d_k = [None] * X_HEADS, [None] * X_HEADS
        for u, rs in enumerate(subs):
            dx2b = dx2[u].astype(BF16)
            dx2b_ref[rs, :] = dx2b
            dob = _dot_nt(dx2b, wxo_ref[...]).astype(BF16)
            for h, hs in enumerate(heads):
                p = p_scr[rs, hs]
                dp = _dot_nt(dob[:, hs], vv_ref[:, hs])
                ds = p * (dp - jnp.sum(dp * p, axis=-1, keepdims=True)) * scale
                dq_scr[rs, hs] = _dot(ds.astype(BF16), k_ref[:, hs]).astype(BF16)
                dvv_h = _dot(p.T.astype(BF16), dob[:, hs])
                dk_h = _dot(ds.T.astype(BF16), q_scr[rs, hs])
                d_vv[h] = dvv_h if d_vv[h] is None else d_vv[h] + dvv_h
                d_k[h] = dk_h if d_k[h] is None else d_k[h] + dk_h
        for h, hs in enumerate(heads):
            dvv_ref[:, hs] += d_vv[h]
            dk_ref[:, hs] += d_k[h]
        d_gx = None
        for u, rs in enumerate(subs):
            dqb = dq_scr[rs, :]
            dqb_ref[rs, :] = dqb
            dh2 = _dot_nt(dqb, wq_ref[...])
            d_gx = _rows8(dh2 * x1h[u]) if d_gx is None else d_gx + _rows8(dh2 * x1h[u])
            dx1 = dx2[u] + _rms_bwd(dh2 * gx, x1h[u], r2[u])
            dx1b = dx1.astype(BF16)
            dx1b_ref[rs, :] = dx1b
            dcat_ref[rs, :] = _dot_nt(dx1b, wout_ref[...]).astype(BF16)
        acc_gx[...] += d_gx

        @pl.when(i == nt - 1)
        def _():
            dgf_ref[...] = jnp.sum(acc_gf[...], axis=0, keepdims=True)
            dgx_ref[...] = jnp.sum(acc_gx[...], axis=0, keepdims=True)
            loss_ref[...] = jnp.sum(acc_loss[...], axis=0, keepdims=True)

    whole = lambda shape: pl.BlockSpec(shape, lambda i: (0,) * len(shape))
    once = lambda shape: pl.BlockSpec(shape, lambda i: (0,) * len(shape), pipeline_mode=pl.Buffered(1))
    rows = lambda width: pl.BlockSpec((tm, width), lambda i: (i, 0))
    cols = lambda height: pl.BlockSpec((height, tm), lambda i: (0, i))
    vec = jax.ShapeDtypeStruct((1, D_MODEL), F32)
    return pl.pallas_call(
        body,
        name="attn",
        grid=(nt,),
        in_specs=[
            rows(D_MODEL), rows(MIX_DIM), rows(D_MODEL),
            once((MIX_DIM, D_MODEL)), once((D_MODEL, D_MODEL)), once((D_MODEL, D_MODEL)),
            once((MEM_LEN, D_MODEL)), once((MEM_LEN, D_MODEL)),
            whole((1, D_MODEL)), whole((1, D_MODEL)),
        ],
        out_specs=[
            rows(MIX_DIM), cols(MIX_DIM), cols(D_MODEL), cols(D_MODEL),
            rows(D_MODEL), rows(D_MODEL), rows(D_MODEL),
            whole((MEM_LEN, D_MODEL)), whole((MEM_LEN, D_MODEL)),
            whole((1, D_MODEL)), whole((1, D_MODEL)), whole((1, D_MODEL)),
        ],
        out_shape=[
            jax.ShapeDtypeStruct((s, MIX_DIM), BF16),
            jax.ShapeDtypeStruct((MIX_DIM, s), BF16),
            jax.ShapeDtypeStruct((D_MODEL, s), BF16),
            jax.ShapeDtypeStruct((D_MODEL, s), BF16),
            jax.ShapeDtypeStruct((s, D_MODEL), BF16),
            jax.ShapeDtypeStruct((s, D_MODEL), BF16),
            jax.ShapeDtypeStruct((s, D_MODEL), BF16),
            jax.ShapeDtypeStruct((MEM_LEN, D_MODEL), F32),
            jax.ShapeDtypeStruct((MEM_LEN, D_MODEL), F32),
            vec, vec, vec,
        ],
        scratch_shapes=[
            pltpu.VMEM((tm, D_MODEL), F32),
            pltpu.VMEM((tm, D_MODEL), BF16),
            pltpu.VMEM((tm, D_MODEL), F32),
            pltpu.VMEM((tm, D_MODEL), BF16),
            pltpu.VMEM((8, D_MODEL), F32),
            pltpu.VMEM((8, D_MODEL), F32),
            pltpu.VMEM((8, D_MODEL), F32),
        ],
        compiler_params=pltpu.CompilerParams(dimension_semantics=("arbitrary",)),
    )(x, cat, target, w_out, w_q, w_xo, k, vv, g_x, g_f)


def _matmul_f32(name, a, b, rows=None, plans=()):
    first, m = (0, a.shape[0]) if rows is None else rows
    kk = a.shape[1]
    n = b.shape[1]
    tm, tn, tk = min(m, TM_DW), min(n, D_MODEL), min(kk, TK_DW)
    nk = kk // tk
    i0 = first // tm

    def body(a_ref, b_ref, o_ref):
        prod = _dot(a_ref[...], b_ref[...])
        if nk == 1:
            o_ref[...] = prod
            return
        kid = pl.program_id(2)

        @pl.when(kid == 0)
        def _():
            o_ref[...] = prod

        @pl.when(kid > 0)
        def _():
            o_ref[...] += prod

    (out,), plan_outs = _call(
        body,
        name=name,
        operands=[a, b],
        grid=(m // tm, n // tn, nk),
        in_specs=[pl.BlockSpec((tm, tk), lambda i, j, k: (i + i0, k)), pl.BlockSpec((tk, tn), lambda i, j, k: (k, j))],
        out_specs=[pl.BlockSpec((tm, tn), lambda i, j, k: (i, j))],
        out_shape=[jax.ShapeDtypeStruct((m, n), F32)],
        plans=plans,
    )
    return out, plan_outs


def _dh1(name, dproj, w_in, x, dx1, g, tiles, into=None, after=(), plans=()):
    s = x.shape[0]
    tm = min(TM_DH1, s)
    t0, nt = tiles[0], tiles[1] - tiles[0]

    def body(dp_ref, w_ref, x_ref, dx1_ref, g_ref, *rest):
        gx_ref, dg_ref, acc_g = rest[-3:]
        i = pl.program_id(0)

        @pl.when(i == 0)
        def _():
            acc_g[...] = jnp.zeros_like(acc_g)

        dh1 = _dot_nt(dp_ref[...], w_ref[...])
        xh, r = _rms(x_ref[...])
        acc_g[...] += _rows8(dh1 * xh)
        gx_ref[...] = dx1_ref[...].astype(F32) + _rms_bwd(dh1 * g_ref[...], xh, r)

        @pl.when(i == nt - 1)
        def _():
            dg_ref[...] = jnp.sum(acc_g[...], axis=0, keepdims=True)

    rows = pl.BlockSpec((tm, D_MODEL), lambda i: (i + t0, 0))
    return _call(
        body,
        name=name,
        operands=[dproj, w_in, x, dx1, g] + ([into] if into is not None else []) + list(after),
        grid=(nt,),
        in_specs=[
            pl.BlockSpec((tm, IN_DIM), lambda i: (i + t0, 0)),
            pl.BlockSpec((D_MODEL, IN_DIM), lambda i: (0, 0), pipeline_mode=pl.Buffered(1)),
            rows,
            rows,
            pl.BlockSpec((1, D_MODEL), lambda i: (0, 0)),
        ] + [pl.BlockSpec(memory_space=pl.ANY)] * ((into is not None) + len(after)),
        out_specs=[rows, pl.BlockSpec((1, D_MODEL), lambda i: (0, 0))],
        out_shape=[jax.ShapeDtypeStruct((s, D_MODEL), F32), jax.ShapeDtypeStruct((1, D_MODEL), F32)],
        scratch_shapes=[pltpu.VMEM((8, D_MODEL), F32)],
        plans=plans,
        aliases={5: 0} if into is not None else None,
    )


def _pair_sum(name, grad, got, axis, size, core, plans=()):
    shard = list(grad.shape)
    shard[axis] = size
    r, cdim = shard
    tr = min(r, TR_PAIR_SUM)
    nr = r // tr

    def body(core_ref, g_ref, got_ref, o_ref):
        o_ref[0] = (g_ref[...] + got_ref[0]).astype(BF16)

    if axis == 1:
        g_map = lambda q, t, core_ref: (t, 2 * q + core_ref[0])
    else:
        g_map = lambda q, t, core_ref: ((2 * q + core_ref[0]) * nr + t, 0)
    (out,), plan_outs = _call(
        body,
        name=name,
        prefetch=[core],
        operands=[grad, got],
        grid=(N_CHIP, nr),
        in_specs=[
            pl.BlockSpec((tr, cdim), g_map),
            pl.BlockSpec((1, tr, cdim), lambda q, t, core_ref: (q, t, 0)),
        ],
        out_specs=[pl.BlockSpec((1, tr, cdim), lambda q, t, core_ref: (q, t, 0))],
        out_shape=[jax.ShapeDtypeStruct((N_CHIP, r, cdim), BF16)],
        plans=plans,
    )
    return (out, plan_outs) if plans else out


def _adam_step(w, m, v, g):
    c1 = 1.0 / (1.0 - ADAM_B1**ADAM_STEP)
    c2 = 1.0 / (1.0 - ADAM_B2**ADAM_STEP)
    m_new = ADAM_B1 * m + (1.0 - ADAM_B1) * g
    v_new = ADAM_B2 * v + (1.0 - ADAM_B2) * (g * g)
    return -ADAM_LR * ((m_new * c1) / (jnp.sqrt(v_new * c2) + ADAM_EPS) + ADAM_WD * w), m_new, v_new


def _update_small(dev, vec_w, vec_m, vec_v, conv_w, conv_m, conv_v, small, landed):
    n_vec = len(vec_w)

    def body(dev_ref, *refs):
        refs = list(refs)
        take = lambda k: [refs.pop(0) for _ in range(k)]
        w_refs, m_refs, v_refs = take(n_vec), take(n_vec), take(n_vec)
        cw_ref, cm_ref, cv_ref, small_ref, land_ref, small_conv_ref, land_conv_ref = take(7)
        vec_outs, conv_outs, (loss_ref,) = take(4 * n_vec), take(4), take(1)

        def total(own_ref, others_ref):
            acc = None
            for d in range(N_DEV):
                part = jnp.where(dev_ref[0] == d, own_ref[0], others_ref[d])
                acc = part if acc is None else acc + part
            return acc

        g_all = total(small_ref, land_ref)
        for i in range(n_vec):
            g = g_all[i : i + 1]
            delta, m_new, v_new = _adam_step(w_refs[i][...], m_refs[i][...], v_refs[i][...], g)
            for ref, val in zip(vec_outs[4 * i : 4 * i + 4], (g, delta, m_new, v_new)):
                ref[...] = val
        g = total(small_conv_ref, land_conv_ref)[0:3]
        delta, m_new, v_new = _adam_step(cw_ref[0], cm_ref[0], cv_ref[0], g)
        for ref, val in zip(conv_outs, (g, delta, m_new, v_new)):
            ref[0] = val
        loss_ref[...] = jnp.broadcast_to(0.5 * jnp.sum(g_all[7:8], axis=-1, keepdims=True) / D_MODEL, loss_ref.shape)

    vec_spec = pl.BlockSpec((1, D_MODEL), lambda i, dev_ref: (0, 0))
    conv_spec = pl.BlockSpec((1, 3, CHUNK), lambda i, dev_ref: (0, 0, 0))
    vec_shape = jax.ShapeDtypeStruct((1, D_MODEL), F32)
    conv_shape = jax.ShapeDtypeStruct((1, 3, CHUNK), F32)
    outs, _ = _call(
        body,
        name="update_small",
        prefetch=[dev],
        operands=list(vec_w) + list(vec_m) + list(vec_v) + [conv_w, conv_m, conv_v, small, landed, small, landed],
        grid=(1,),
        in_specs=[vec_spec] * (3 * n_vec) + [conv_spec] * 3 + [
            pl.BlockSpec((1, 16, D_MODEL), lambda i, dev_ref: (0, 0, 0)),
            pl.BlockSpec((N_DEV, 16, D_MODEL), lambda i, dev_ref: (0, 0, 0)),
            pl.BlockSpec((1, 8, CHUNK), lambda i, dev_ref: (0, 1, dev_ref[0])),
            pl.BlockSpec((N_DEV, 8, CHUNK), lambda i, dev_ref: (0, 1, dev_ref[0])),
        ],
        out_specs=[vec_spec] * (4 * n_vec) + [conv_spec] * 4 + [pl.BlockSpec((1, CHUNK), lambda i, dev_ref: (0, 0))],
        out_shape=[vec_shape] * (4 * n_vec) + [conv_shape] * 4 + [jax.ShapeDtypeStruct((1, CHUNK), F32)],
    )
    return [outs[4 * i : 4 * i + 4] for i in range(n_vec)], outs[4 * n_vec : 4 * n_vec + 4], outs[-1]


def _adamw(name, groups, steps, own_slot=None, plans=()):
    lists = lambda p: list(p) if isinstance(p, (list, tuple)) else [p]
    groups = [(g[0], g[1], g[2], lists(g[3]), lists(g[4]) if len(g) > 4 else []) for g in groups]
    counts = [3 + len(g[3]) + len(g[4]) for g in groups]
    use_slot = own_slot is not None

    def first_tiles(w, parts):
        firsts, first = [], 0
        for p in parts:
            firsts.append(first)
            first += p.shape[1] // (w.shape[0] // steps)
        return firsts

    def update(t, slot_ref, firsts, w_ref, m_ref, v_ref, p_refs, o_refs, g_ref, d_ref, nm_ref, nv_ref):
        g = None
        for k, p_ref in enumerate(p_refs):
            gk = None
            for q in range(p_ref.shape[0]):
                part = p_ref[q].astype(F32)
                if o_refs:
                    part = jnp.where(slot_ref[0] == q, o_refs[k][0].astype(F32), part)
                gk = part if gk is None else gk + part
            g = gk if g is None else jnp.where(t >= firsts[k], gk, g)
        g_ref[...] = g
        d_ref[...], nm_ref[...], nv_ref[...] = _adam_step(w_ref[...], m_ref[...], v_ref[...], g)

    def body(*refs):
        refs = list(refs)
        slot_ref = refs.pop(0) if use_slot else None
        t = pl.program_id(0)
        ins, outs = refs[: sum(counts)], refs[sum(counts) :]
        for i, (w, _, _, parts, own) in enumerate(groups):
            mine = ins[sum(counts[:i]) : sum(counts[: i + 1])]
            update(t, slot_ref, first_tiles(w, parts), *mine[:3], mine[3 : 3 + len(parts)], mine[3 + len(parts) :],
                   *outs[4 * i : 4 * i + 4])

    operands, in_specs, out_specs, out_shape = [], [], [], []
    for w, m, v, parts, own in groups:
        r, cdim = w.shape
        tr = r // steps
        firsts = first_tiles(w, parts)

        def tile_of(k, t, firsts=firsts, parts=parts, tr=tr):
            return jnp.clip(t - firsts[k], 0, parts[k].shape[1] // tr - 1)

        blk = pl.BlockSpec((tr, cdim), lambda t, *_: (t, 0))
        operands += [w, m, v] + parts + own
        in_specs += [blk, blk, blk]
        in_specs += [pl.BlockSpec((p.shape[0], tr, cdim), lambda t, *_, k=k, tile_of=tile_of: (0, tile_of(k, t), 0)) for k, p in enumerate(parts)]
        in_specs += [pl.BlockSpec((1, tr, cdim), lambda t, slot_ref, k=k, tile_of=tile_of: (slot_ref[0], tile_of(k, t), 0)) for k in range(len(own))]
        out_specs += [blk] * 4
        out_shape += [jax.ShapeDtypeStruct((r, cdim), F32)] * 4
    outs, plan_outs = _call(
        body,
        name=name,
        prefetch=[own_slot] if use_slot else [],
        operands=operands,
        grid=(steps,),
        in_specs=in_specs,
        out_specs=out_specs,
        out_shape=out_shape,
        plans=plans,
    )
    return [outs[4 * i : 4 * i + 4] for i in range(len(groups))], plan_outs


def _chip_routes(n):
    x, y, c = _place()
    my_chip = 2 * x + y
    return [(a, (*chip, c), 2 * chip[0] + chip[1], my_chip, 2 * chip[0] + chip[1])
            for a in range(n) for chip in _other_chips(x, y)]


def _gather_routes(n):
    x, y, c = _place()
    flip = lambda v, f: 1 - v if f else v
    peers = [(flip(x, fx), flip(y, fy), flip(c, fc)) for fx in (0, 1) for fy in (0, 1) for fc in (0, 1) if fx + fy + fc]
    return [(0, p, 0, 4 * x + 2 * y + c, 4 * p[0] + 2 * p[1] + p[2]) for p in peers]


_chip_routes.copies_per_array = N_CHIP - 1
_gather_routes.copies_per_array = N_DEV - 1


def _split_start(name, routes, srcs, land_shapes):
    n = len(srcs)
    hbm = pl.BlockSpec(memory_space=pltpu.HBM)
    sem = pl.BlockSpec(memory_space=pltpu.SEMAPHORE)

    def body(*refs):
        src_refs, land_refs = refs[:n], refs[n : 2 * n]
        send_sems, recv_sems = refs[2 * n], refs[2 * n + 1]
        token = refs[-1]
        for k, (a, peer, src_slot, there, _) in enumerate(routes(n)):
            pltpu.make_async_remote_copy(
                src_ref=src_refs[a].at[src_slot], dst_ref=land_refs[a].at[there],
                send_sem=send_sems.at[k], recv_sem=recv_sems.at[k], device_id=peer, device_id_type=MESH).start()
        token[...] = jnp.zeros_like(token)

    in_hbm = [pltpu.with_memory_space_constraint(p, pltpu.HBM) for p in srcs]
    in_hbm += [pltpu.with_memory_space_constraint(lax.empty(shape, p.dtype), pltpu.HBM) for p, shape in zip(srcs, land_shapes)]
    n_copies = routes.copies_per_array * n
    res = pl.pallas_call(
        body,
        name=name,
        out_shape=[pltpu.SemaphoreType.DMA((n_copies,)), pltpu.SemaphoreType.DMA((n_copies,))]
        + [pltpu.HBM(p.shape, p.dtype) for p in srcs] + [pltpu.HBM(shape, p.dtype) for p, shape in zip(srcs, land_shapes)]
        + [jax.ShapeDtypeStruct((8, CHUNK), F32)],
        in_specs=[hbm] * (2 * n),
        out_specs=[sem, sem] + [hbm] * (2 * n) + [pl.BlockSpec(memory_space=pltpu.VMEM)],
        input_output_aliases={i: 2 + i for i in range(2 * n)},
        compiler_params=pltpu.CompilerParams(has_side_effects=pltpu.SideEffectType.DATAFLOW_SIDE_EFFECTING),
    )(*in_hbm)
    return res[:-1], res[-1]


def _split_wait(name, routes, in_flight, after):
    n = (len(in_flight) - 2) // 2
    hbm = pl.BlockSpec(memory_space=pltpu.HBM)
    sem = pl.BlockSpec(memory_space=pltpu.SEMAPHORE)
    send_sems, recv_sems, *bufs = in_flight

    def body(*refs):
        src_refs, land_refs = refs[:n], refs[n : 2 * n]
        send_ref, recv_ref = refs[2 * n], refs[2 * n + 1]
        for k, (a, peer, src_slot, _, here) in enumerate(routes(n)):
            cp = pltpu.make_async_remote_copy(
                src_ref=src_refs[a].at[src_slot], dst_ref=land_refs[a].at[here],
                send_sem=send_ref.at[k], recv_sem=recv_ref.at[k], device_id=peer, device_id_type=MESH)
            cp.wait_send()
            cp.wait_recv()

    res = pl.pallas_call(
        body,
        name=name,
        out_shape=[pltpu.HBM(b.shape, b.dtype) for b in bufs],
        in_specs=[hbm] * (2 * n) + [sem, sem] + [pl.BlockSpec(memory_space=pl.ANY)] * len(after),
        out_specs=[hbm] * (2 * n),
        input_output_aliases={i: i for i in range(2 * n)},
        compiler_params=pltpu.CompilerParams(has_side_effects=pltpu.SideEffectType.DATAFLOW_SIDE_EFFECTING),
    )(*bufs, send_sems, recv_sems, *after)
    return res[:n], res[n:]


def kernel(x, mem, norm_mix_g, w_in, conv_w, gm_ln_g, gm_ln_b, gm_ws, gm_bs, w_out, norm_x_g, norm_mem_g, w_q, w_kv, w_xo, norm_final_g, loss_target, m_norm_mix_g, m_w_in, m_conv_w, m_gm_ln_g, m_gm_ln_b, m_gm_ws, m_gm_bs, m_w_out, m_norm_x_g, m_norm_mem_g, m_w_q, m_w_kv, m_w_xo, m_norm_final_g, v_norm_mix_g, v_w_in, v_conv_w, v_gm_ln_g, v_gm_ln_b, v_gm_ws, v_gm_bs, v_w_out, v_norm_x_g, v_norm_mem_g, v_w_q, v_w_kv, v_w_xo, v_norm_final_g):
    s = x.shape[1]
    dev = 4 * lax.axis_index("x") + 2 * lax.axis_index("y") + lax.axis_index("c")
    core = lax.axis_index("c").astype(jnp.int32).reshape(1)
    x2 = x[0]
    target = loss_target[0]
    pad_taps = lambda t: jnp.pad(t[0], ((0, 5), (0, 0)))

    big_names = ["w_in", "w_out", "w_q", "w_kv", "w_xo"]
    big_w = dict(w_in=w_in[0], w_out=w_out[0], w_q=w_q[0], w_kv=w_kv[0], w_xo=w_xo[0])
    big_m = dict(w_in=m_w_in[0], w_out=m_w_out[0], w_q=m_w_q[0], w_kv=m_w_kv[0], w_xo=m_w_xo[0])
    big_v = dict(w_in=v_w_in[0], w_out=v_w_out[0], w_q=v_w_q[0], w_kv=v_w_kv[0], w_xo=v_w_xo[0])
    shard_axis = dict(w_in=1, w_out=0, w_q=0, w_kv=1, w_xo=0)
    full_shape = dict(w_in=(D_MODEL, IN_DIM), w_out=(MIX_DIM, D_MODEL), w_q=(D_MODEL, D_MODEL),
                      w_kv=(D_MODEL, 2 * D_MODEL), w_xo=(D_MODEL, D_MODEL))
    shard_size = {k: big_w[k].shape[shard_axis[k]] for k in big_names}
    others = big_names[1:]
    gather_spec = lambda k: (full_shape[k], shard_axis[k], shard_size[k])
    reduce_spec = lambda k: (shard_axis[k], shard_size[k])
    ws = gm_ws[0]
    bs_b = jnp.broadcast_to(gm_bs[0][:, :, None], (8, CHUNK, CHUNK))
    g_f = norm_final_g.reshape(1, D_MODEL)

    px, py = lax.axis_index("x"), lax.axis_index("y")
    chip_order = jnp.stack([2 * px + py, 2 * (1 - px) + py, 2 * px + 1 - py, 2 * (1 - px) + 1 - py]).astype(jnp.int32)
    (proj, h1, h1t, w_in_full), ((w_out_full, conv_g),) = _inproj(
        x2, norm_mix_g, big_w["w_in"].astype(BF16), chip_order,
        plans=[_gather_plan([big_w["w_out"].astype(BF16), pad_taps(conv_w)[None]], [gather_spec("w_out"), ((N_DEV, 8, CHUNK), 0, 1)])])
    conv_full = conv_g.transpose(1, 0, 2).reshape(8, D_MODEL)
    attn_w = ["w_q", "w_kv", "w_xo"]
    cat, (gathered,) = _mixer_fwd(
        proj, conv_full, gm_ln_g, gm_ln_b, ws, bs_b,
        plans=[_gather_plan([big_w[k].astype(BF16) for k in attn_w], [gather_spec(k) for k in attn_w])])
    wf = dict(zip(attn_w, gathered), w_out=w_out_full)
    k, vv, mt = _kv_fwd(mem[0], norm_mem_g, wf["w_kv"])
    (dcat, catt, h2t, ot, dx2b, dqb, dx1b, dk, dvv, d_gf, d_gx, sq_err) = _attn(
        x2, cat, target, wf["w_out"], wf["w_q"], wf["w_xo"], k, vv, norm_x_g, g_f)

    def pair_sums(names, grads, got):
        return [_pair_sum("pair_sum_" + k, grads[k], got[i], *reduce_spec(k), core) for i, k in enumerate(names)]

    sibling_plan = lambda names, grads: _sibling_plan([grads[k] for k in names], [reduce_spec(k) for k in names])
    early, late = ["w_out", "w_kv", "w_q"], ["w_xo"]
    big_g = dict(w_out=_matmul_f32("dw_out", catt, dx1b)[0])
    big_g["w_kv"], d_gmem = _kv_bwd(mem[0], wf["w_kv"], mt, dk, dvv)
    big_g["w_q"], (got_a,) = _matmul_f32("dw_q", h2t, dqb, plans=[sibling_plan(early[:2], big_g)])
    big_g["w_xo"], (got_b,) = _matmul_f32("dw_xo", ot, dx2b, plans=[sibling_plan(early[2:], big_g)])
    got_early = got_a + got_b
    (dproj, d_cw, d_lng, d_lnb, d_ws, d_bs), (parts_early, got_late) = _mixer_bwd(
        proj, dcat, conv_full, gm_ln_g, gm_ln_b, ws, bs_b,
        plans=[_chip_plan(pair_sums(early, big_g, got_early)), sibling_plan(late, big_g)])
    half = D_MODEL // 2
    in_spec = reduce_spec("w_in")
    dw_top, _ = _matmul_f32("dw_in_top", h1t, dproj, rows=(0, half))
    pair_xo = pair_sums(late, big_g, got_late)[0]
    dw_bot, (got_top, (ws_all,)) = _matmul_f32(
        "dw_in_bot", h1t, dproj, rows=(half, half),
        plans=[_sibling_plan([dw_top], [in_spec]),
               _gather_plan([d_ws.reshape(1, 8 * CHUNK, CHUNK).astype(BF16)], [((N_DEV, 8 * CHUNK, CHUNK), 0, 1)])])
    pair_top, ((got_bot,),) = _pair_sum("pair_sum_w_in_top", dw_top, got_top[0], *in_spec, core,
                                        plans=[_sibling_plan([dw_bot], [in_spec])])
    pair_bot = _pair_sum("pair_sum_w_in_bot", dw_bot, got_bot, *in_spec, core)
    chip_shapes = [p.shape for p in (pair_top, pair_bot, pair_xo)]
    chips_in_flight, token = _split_start("grad_chip_exchange_start", _chip_routes, [pair_top, pair_bot, pair_xo], chip_shapes)
    n_tiles = s // min(TM_DH1, s)
    (grad_x, d_gmix), _ = _dh1("dh1", dproj, w_in_full, x2, dx1b, norm_mix_g, (0, n_tiles), after=[token])
    vec_names = ["norm_mix_g", "gm_ln_g", "gm_ln_b", "gm_bs", "norm_x_g", "norm_mem_g", "norm_final_g"]
    vec_g = [d_gmix, d_lng, d_lnb, d_bs, d_gx, d_gmem, d_gf]
    small = jnp.concatenate(vec_g + [sq_err, d_cw], axis=0)[None]
    small_in_flight, _ = _split_start("small_all_gather_start", _gather_routes, [small], [(N_DEV, 16, D_MODEL)])
    flat_ws = lambda t: t.reshape(8 * CHUNK, CHUNK)
    updated, _ = _adamw(
        "adamw_early",
        [(big_w[k], big_m[k], big_v[k], got_k) for k, got_k in zip(early, parts_early)]
        + [(flat_ws(ws), flat_ws(m_gm_ws), flat_ws(v_gm_ws), ws_all)], steps=STEPS_ADAMW // 2)
    big_out = dict(zip(early, updated[:3]))
    ws_out = updated[3]
    (pair_top, pair_bot, pair_xo), (land_top, land_bot, land_xo) = _split_wait(
        "grad_chip_exchange_wait", _chip_routes, chips_in_flight, [d_gmix] + [u[0] for u in updated])
    my_chip = (2 * px + py).astype(jnp.int32).reshape(1)
    (big_out["w_in"], big_out["w_xo"]), _ = _adamw(
        "adamw_late",
        [(big_w["w_in"], big_m["w_in"], big_v["w_in"], [land_top, land_bot], [pair_top, pair_bot]),
         (big_w["w_xo"], big_m["w_xo"], big_v["w_xo"], land_xo, pair_xo)],
        steps=STEPS_ADAMW, own_slot=my_chip)
    (small,), (small_all,) = _split_wait("small_all_gather_wait", _gather_routes, small_in_flight, [big_out["w_xo"][0]])
    row = lambda t: t.reshape(1, D_MODEL)
    vec_out, conv_out, loss_row = _update_small(
        dev.astype(jnp.int32).reshape(1),
        [norm_mix_g, gm_ln_g, gm_ln_b, row(gm_bs), norm_x_g, norm_mem_g, row(norm_final_g)],
        [m_norm_mix_g, m_gm_ln_g, m_gm_ln_b, row(m_gm_bs), m_norm_x_g, m_norm_mem_g, row(m_norm_final_g)],
        [v_norm_mix_g, v_gm_ln_g, v_gm_ln_b, row(v_gm_bs), v_norm_x_g, v_norm_mem_g, row(v_norm_final_g)],
        conv_w, m_conv_w, v_conv_w, small, small_all)
    loss = loss_row[0, 0]

    def result(name, which):
        if name in big_out:
            return big_out[name][which][None]
        if name == "conv_w":
            return conv_out[which]
        if name == "gm_ws":
            return ws_out[which].reshape(1, 8, CHUNK, CHUNK)
        r = vec_out[vec_names.index(name)][which]
        if name == "gm_bs":
            return r.reshape(1, 8, CHUNK)
        if name == "norm_final_g":
            return r.reshape(D_MODEL)
        return r

    weights = ["norm_mix_g", "w_in", "conv_w", "gm_ln_g", "gm_ln_b", "gm_ws", "gm_bs", "w_out", "norm_x_g", "norm_mem_g", "w_q", "w_kv", "w_xo", "norm_final_g"]
    outs = [loss, grad_x[None]]
    for which in range(4):
        outs += [result(name, which) for name in weights]
    return tuple(outs)
```

```python
import math

import jax
import jax.numpy as jnp
from jax import lax
from jax.experimental import pallas as pl
from jax.experimental.pallas import tpu as pltpu

F32 = jnp.float32
BF16 = jnp.bfloat16
MESH = pl.DeviceIdType.MESH

D_MODEL = 1024
N_DEV = 8
N_CHIP = 4
CHUNK = 128
GROUPS = 7
IN_DIM = GROUPS * D_MODEL
SHARD_IN = IN_DIM // N_DEV
MIX_DIM = 2 * D_MODEL
X_HEADS = 4
X_HEAD_DIM = D_MODEL // X_HEADS
MEM_LEN = 256
EPS = 1e-6
GELU_K0 = math.sqrt(2.0 / math.pi)
GELU_K1 = 0.044715

ADAM_LR = 0.001
ADAM_B1 = 0.9
ADAM_B2 = 0.999
ADAM_EPS = 1e-08
ADAM_WD = 0.01
ADAM_STEP = 10

TM_INPROJ = 512
TM_ATTN = 512
SUB_ATTN = 256
TM_DH1 = 512
TM_DW = 512
TK_DW = 4096
TR_PAIR_SUM = 1024
STEPS_ADAMW = 4


def _dot(a, b):
    return lax.dot_general(a, b, (((1,), (0,)), ((), ())), preferred_element_type=F32)


def _dot_nt(a, b):
    return lax.dot_general(a, b, (((1,), (1,)), ((), ())), preferred_element_type=F32)


def _rows8(v):
    return jnp.sum(v.reshape(v.shape[0] // 8, 8, v.shape[1]), axis=0)


def _gelu(x, with_grad=True):
    x2 = x * x
    t = jnp.tanh(x * (GELU_K0 + (GELU_K0 * GELU_K1) * x2))
    half = 0.5 + 0.5 * t
    g = x * half
    if not with_grad:
        return g
    return g, half + ((0.5 * x) * (1.0 - t * t)) * (GELU_K0 + (3.0 * GELU_K0 * GELU_K1) * x2)


def _silu(z, with_grad=True):
    s = jax.nn.sigmoid(z)
    zs = z * s
    return (zs, s + zs * (1.0 - s)) if with_grad else zs


def _rms(v):
    r = lax.rsqrt(jnp.mean(v * v, axis=-1, keepdims=True) + EPS)
    return v * r, r


def _rms_bwd(dy_g, vh, r):
    return r * (dy_g - vh * jnp.mean(dy_g * vh, axis=-1, keepdims=True))


def _place():
    return lax.axis_index("x"), lax.axis_index("y"), lax.axis_index("c")


def _other_chips(x, y):
    return [(1 - x, y), (x, 1 - y), (1 - x, 1 - y)]


def _pow2_divisor(n):
    return n & (-n)


def _shard_view(ref, axis, size, d):
    start = d * size
    align = _pow2_divisor(size)
    if align > 1:
        start = pl.multiple_of(start, align)
    idx = [slice(None)] * len(ref.shape)
    idx[axis] = pl.ds(start, size)
    return ref.at[tuple(idx)]


class _Plan:
    def __init__(self, ins, out_shape, sems, start, finish, mid=None, aliases=None):
        self.ins, self.out_shape, self.sems = list(ins), list(out_shape), list(sems)
        self.start, self.mid, self.finish = start, mid, finish
        self.aliases = dict(aliases or {})


def _gather_plan(shards, specs):
    n = len(shards)
    n_copy = 7

    def copies(ins, outs, sems):
        send_sems, recv_sems, local_sems = sems
        x, y, c = _place()
        me, sibling = (x, y, c), (x, y, 1 - c)
        chips = _other_chips(x, y)

        def block(a, p):
            _, axis, size = specs[a]
            return _shard_view(outs[a], axis, size, 4 * p[0] + 2 * p[1] + p[2])

        def copy(a, k, p, to, src=None):
            return pltpu.make_async_remote_copy(
                src_ref=block(a, p) if src is None else src,
                dst_ref=block(a, p),
                send_sem=send_sems.at[a * n_copy + k],
                recv_sem=recv_sems.at[a * n_copy + k],
                device_id=to,
                device_id_type=MESH,
            )

        def mine():
            return [pltpu.make_async_copy(ins[a], block(a, me), local_sems.at[a]) for a in range(n)]

        def first():
            return [cp for a in range(n) for cp in
                    [copy(a, 0, me, sibling, src=ins[a])] + [copy(a, 1 + j, me, (*chip, c), src=ins[a]) for j, chip in enumerate(chips)]]

        def landed():
            return [copy(a, 1 + j, (*chip, c), me) for j, chip in enumerate(chips) for a in range(n)]

        def passed():
            return [copy(a, 4 + j, (*chip, c), sibling) for j, chip in enumerate(chips) for a in range(n)]

        def last():
            return [cp for a in range(n) for cp in
                    [copy(a, 0, sibling, me)] + [copy(a, 4 + j, (*chip, 1 - c), me) for j, chip in enumerate(chips)]]

        return mine, first, landed, passed, last

    def start(ins, outs, sems):
        mine, first, _, _, _ = copies(ins, outs, sems)
        for cp in mine() + first():
            cp.start()

    def mid(ins, outs, sems):
        _, _, landed, passed, _ = copies(ins, outs, sems)
        for got, fwd in zip(landed(), passed()):
            got.wait_recv()
            fwd.start()

    def finish(ins, outs, sems):
        mine, first, _, passed, last = copies(ins, outs, sems)
        for cp in last():
            cp.wait_recv()
        for cp in first() + passed():
            cp.wait_send()
        for cp in mine():
            cp.wait()

    return _Plan(
        shards,
        [jax.ShapeDtypeStruct(full, s.dtype) for s, (full, _, _) in zip(shards, specs)],
        [pltpu.SemaphoreType.DMA((n * n_copy,)), pltpu.SemaphoreType.DMA((n * n_copy,)), pltpu.SemaphoreType.DMA((n,))],
        start, finish, mid,
    )


def _sibling_plan(grads, specs):
    n = len(grads)

    def shard_shape(a):
        axis, size = specs[a]
        shp = list(grads[a].shape)
        shp[axis] = size
        return tuple(shp)

    def copies(ins, outs, sems):
        send_sems, recv_sems = sems
        x, y, c = _place()
        return [
            pltpu.make_async_remote_copy(
                src_ref=_shard_view(ins[a], specs[a][0], specs[a][1], 2 * q + (1 - c)),
                dst_ref=outs[a].at[q],
                send_sem=send_sems.at[a * N_CHIP + q],
                recv_sem=recv_sems.at[a * N_CHIP + q],
                device_id=(x, y, 1 - c),
                device_id_type=MESH,
            )
            for a in range(n) for q in range(N_CHIP)
        ]

    def start(ins, outs, sems):
        for cp in copies(ins, outs, sems):
            cp.start()

    def finish(ins, outs, sems):
        for cp in copies(ins, outs, sems):
            cp.wait()

    return _Plan(
        grads,
        [jax.ShapeDtypeStruct((N_CHIP, *shard_shape(a)), grads[a].dtype) for a in range(n)],
        [pltpu.SemaphoreType.DMA((n * N_CHIP,)), pltpu.SemaphoreType.DMA((n * N_CHIP,))],
        start, finish,
    )


def _chip_plan(parts, full_rows=None, row0=None, into=None):
    n = len(parts)
    full_rows = [p.shape[1] for p in parts] if full_rows is None else full_rows
    row0 = [0] * n if row0 is None else row0

    def copies(ins, outs, sems):
        send_sems, recv_sems, local_sems = sems
        x, y, c = _place()
        my_chip = 2 * x + y
        chips = _other_chips(x, y)

        def land(a, q):
            return outs[a].at[q, pl.ds(row0[a], parts[a].shape[1]), :]

        def mine():
            return [pltpu.make_async_copy(ins[a].at[my_chip], land(a, my_chip), local_sems.at[a]) for a in range(n)]

        def remote(sending):
            return [
                pltpu.make_async_remote_copy(
                    src_ref=ins[a].at[2 * chip[0] + chip[1]],
                    dst_ref=land(a, my_chip if sending else 2 * chip[0] + chip[1]),
                    send_sem=send_sems.at[a * 3 + j], recv_sem=recv_sems.at[a * 3 + j],
                    device_id=(*chip, c), device_id_type=MESH)
                for a in range(n) for j, chip in enumerate(chips)
            ]

        return mine, remote

    def start(ins, outs, sems):
        mine, remote = copies(ins, outs, sems)
        for cp in mine() + remote(True):
            cp.start()

    def finish(ins, outs, sems):
        mine, remote = copies(ins, outs, sems)
        for cp in remote(False):
            cp.wait_recv()
        for cp in remote(True):
            cp.wait_send()
        for cp in mine():
            cp.wait()

    return _Plan(
        list(parts) + (list(into) if into is not None else []),
        [jax.ShapeDtypeStruct((N_CHIP, full_rows[a], parts[a].shape[2]), parts[a].dtype) for a in range(n)],
        [pltpu.SemaphoreType.DMA((n * 3,)), pltpu.SemaphoreType.DMA((n * 3,)), pltpu.SemaphoreType.DMA((n,))],
        start, finish,
        aliases={n + a: a for a in range(n)} if into is not None else None,
    )


def _call(body, *, name, operands, out_shape, grid=(), in_specs=None, out_specs=None, scratch_shapes=(),
          plans=(), aliases=None, prefetch=(), start_after_body=False, pass_on_at=0.6):
    operands, out_shape, scratch_shapes = list(operands), list(out_shape), list(scratch_shapes)
    n_pf, n_in, n_out, n_scr = len(prefetch), len(operands), len(out_shape), len(scratch_shapes)
    in_vmem = pl.BlockSpec(memory_space=pltpu.VMEM)
    in_hbm = pl.BlockSpec(memory_space=pl.ANY)
    in_specs = [in_vmem] * n_in if in_specs is None else list(in_specs)
    out_specs = [in_vmem] * n_out if out_specs is None else list(out_specs)
    io_alias = {n_pf + i: o for i, o in (aliases or {}).items()}
    pos_in, pos_out = n_pf + n_in, n_out
    for p in plans:
        for i, o in p.aliases.items():
            io_alias[pos_in + i] = pos_out + o
        pos_in += len(p.ins)
        pos_out += len(p.out_shape)
    steps = math.prod(grid) if grid else 1
    mid_step = min(steps - 1, int(steps * pass_on_at))

    def wrapped(*refs):
        refs = list(refs)
        take = lambda k: [refs.pop(0) for _ in range(k)]
        pf = take(n_pf)
        ins, p_ins = take(n_in), [take(len(p.ins)) for p in plans]
        outs, p_outs = take(n_out), [take(len(p.out_shape)) for p in plans]
        scr, p_sems = take(n_scr), [take(len(p.sems)) for p in plans]
        step = 0
        for ax, g in enumerate(grid):
            step = step * g + pl.program_id(ax)

        def hook(kind, at):
            todo = [(getattr(p, kind), a, b, c) for p, a, b, c in zip(plans, p_ins, p_outs, p_sems) if getattr(p, kind)]

            def run():
                for fn, a, b, c in todo:
                    fn(a, b, c)

            if todo and grid:
                pl.when(step == at)(run)
            elif todo:
                run()

        if not start_after_body:
            hook("start", 0)
        if body is not None:
            body(*pf, *ins, *outs, *scr)
        if start_after_body:
            hook("start", 0)
        hook("mid", mid_step)
        hook("finish", steps - 1)

    layout = dict(
        grid=grid,
        in_specs=in_specs + [in_hbm] * sum(len(p.ins) for p in plans),
        out_specs=out_specs + [in_hbm] * sum(len(p.out_shape) for p in plans),
        scratch_shapes=scratch_shapes + [s for p in plans for s in p.sems],
    )
    if n_pf:
        layout = dict(grid_spec=pltpu.PrefetchScalarGridSpec(num_scalar_prefetch=n_pf, **layout))
    res = pl.pallas_call(
        wrapped,
        name=name,
        out_shape=out_shape + [o for p in plans for o in p.out_shape],
        input_output_aliases=io_alias,
        compiler_params=pltpu.CompilerParams(dimension_semantics=("arbitrary",) * len(grid)) if grid else None,
        **layout,
    )(*prefetch, *operands, *[a for p in plans for a in p.ins])
    res = list(res)
    outs, plan_outs = res[:n_out], []
    pos = n_out
    for p in plans:
        plan_outs.append(res[pos : pos + len(p.out_shape)])
        pos += len(p.out_shape)
    return outs, plan_outs


def _inproj(x, g, w_shard, order, plans=()):
    s = x.shape[0]
    tm = min(TM_INPROJ, s)
    nt = s // tm
    wide = 2 * SHARD_IN
    fwd_at = nt // 2

    def body(order_ref, x_ref, g_ref, w_hbm, proj_ref, h1_ref, h1t_ref, wfull_hbm,
             w_scr, h1_scr, send_sems, recv_sems, local_sems):
        k, i = pl.program_id(0), pl.program_id(1)
        px, py, c = _place()
        me, sibling = (px, py, c), (px, py, 1 - c)
        chips = _other_chips(px, py)

        def half(slot, core):
            return w_scr.at[slot, :, pl.ds(pl.multiple_of(core * SHARD_IN, 128), SHARD_IN)]

        def copy(n, slot, core, to, src=None):
            return pltpu.make_async_remote_copy(
                src_ref=half(slot, core) if src is None else src, dst_ref=half(slot, core),
                send_sem=send_sems.at[n], recv_sem=recv_sems.at[n], device_id=to, device_id_type=MESH)

        def mine():
            return pltpu.make_async_copy(w_hbm, half(0, c), local_sems.at[0])

        def first():
            return [copy(0, 0, c, sibling, src=w_hbm)] + [copy(1 + j, 1 + j, c, (*chips[j], c), src=w_hbm) for j in range(2)]

        def relay():
            to = (c * px + (1 - c) * (1 - px), c * (1 - py) + (1 - c) * py, c)
            return pltpu.make_async_remote_copy(
                src_ref=half(2 - c, c), dst_ref=half(3, c), send_sem=send_sems.at[3], recv_sem=recv_sems.at[3],
                device_id=to, device_id_type=MESH)

        def save(slot):
            cols = pl.ds(pl.multiple_of(order_ref[slot] * wide, 128), wide)
            return pltpu.make_async_copy(w_scr.at[slot], wfull_hbm.at[:, cols], local_sems.at[1 + slot])

        @pl.when((k == 0) & (i == 0))
        def _():
            mine().start()
            for cp in first():
                cp.start()
            mine().wait()
            copy(0, 0, 1 - c, me).wait_recv()

        for j in range(3):
            @pl.when((k == j + 1) & (i == 0))
            def _():
                copy(4 + j, 1 + j, 1 - c, me).wait_recv()

        @pl.when((k == 3) & (i == 0))
        def _():
            for slot in range(N_CHIP):
                save(slot).start()

        @pl.when(k == 0)
        def _():
            xh, _ = _rms(x_ref[...])
            h = xh * g_ref[...]
            h1_ref[...] = h.astype(BF16)
            h1t_ref[...] = h.T.astype(BF16)
            h1_scr[pl.ds(pl.multiple_of(i * tm, tm), tm), :] = h.astype(BF16)

        proj_ref[...] = _dot(h1_scr[pl.ds(pl.multiple_of(i * tm, tm), tm), :], w_scr[k]).astype(BF16)

        for j in range(3):
            @pl.when((k == 0) & (i == nt - 1) if j < 2 else (k == 2) & (i == fwd_at))
            def _():
                copy(1 + j, 1 + j, c, me).wait_recv()
                copy(4 + j, 1 + j, c, sibling).start()
                if j < 2:
                    pl.when(c == 1 - j)(lambda: relay().start())

        @pl.when((k == 3) & (i == nt - 1))
        def _():
            for cp in first() + [relay()] + [copy(4 + j, 1 + j, c, sibling) for j in range(3)]:
                cp.wait_send()
            for slot in range(N_CHIP):
                save(slot).wait()

    once = lambda k, i, order_ref: (jnp.where(k == 0, i, nt - 1), 0)
    return _call(
        body,
        name="inproj",
        prefetch=[order],
        operands=[x, g, w_shard],
        grid=(N_CHIP, nt),
        in_specs=[
            pl.BlockSpec((tm, D_MODEL), once),
            pl.BlockSpec((1, D_MODEL), lambda k, i, order_ref: (0, 0)),
            pl.BlockSpec(memory_space=pl.ANY),
        ],
        out_specs=[
            pl.BlockSpec((tm, wide), lambda k, i, order_ref: (i, order_ref[k])),
            pl.BlockSpec((tm, D_MODEL), once),
            pl.BlockSpec((D_MODEL, tm), lambda k, i, order_ref: (0, jnp.where(k == 0, i, nt - 1))),
            pl.BlockSpec(memory_space=pl.ANY),
        ],
        out_shape=[
            jax.ShapeDtypeStruct((s, IN_DIM), BF16),
            jax.ShapeDtypeStruct((s, D_MODEL), BF16),
            jax.ShapeDtypeStruct((D_MODEL, s), BF16),
            jax.ShapeDtypeStruct((D_MODEL, IN_DIM), BF16),
        ],
        scratch_shapes=[
            pltpu.VMEM((N_CHIP, D_MODEL, wide), BF16),
            pltpu.VMEM((s, D_MODEL), BF16),
            pltpu.SemaphoreType.DMA((7,)),
            pltpu.SemaphoreType.DMA((7,)),
            pltpu.SemaphoreType.DMA((1 + N_CHIP,)),
        ],
        plans=plans,
        start_after_body=True,
        pass_on_at=0.8,
    )


def _rows_from_above(v, above):
    row = lax.broadcasted_iota(jnp.int32, (8, v.shape[1]), 0)
    r1, r2 = pltpu.roll(v, 1, 0), pltpu.roll(v, 2, 0)
    top1 = jnp.where(row == 0, above[7:8], r1[0:8])
    top2 = jnp.where(row == 0, above[6:7], jnp.where(row == 1, above[7:8], r2[0:8]))
    return jnp.concatenate([top1, r1[8:]], axis=0), jnp.concatenate([top2, r2[8:]], axis=0)


def _rows_from_below(v, below):
    n = v.shape[0]
    row = lax.broadcasted_iota(jnp.int32, (8, v.shape[1]), 0)
    r1, r2 = pltpu.roll(v, n - 1, 0), pltpu.roll(v, n - 2, 0)
    end1 = jnp.where(row == 7, below[0:1], r1[n - 8 :])
    end2 = jnp.where(row == 6, below[0:1], jnp.where(row == 7, below[1:2], r2[n - 8 :]))
    return jnp.concatenate([r1[: n - 8], end1], axis=0), jnp.concatenate([r2[: n - 8], end2], axis=0)


def _causal_mask():
    row = lax.broadcasted_iota(jnp.int32, (CHUNK, CHUNK), 0)
    col = lax.broadcasted_iota(jnp.int32, (CHUNK, CHUNK), 1)
    return row, row >= col


def _mixer_fwd(proj, conv_w, ln_g, ln_b, ws, bs_b, plans=()):
    s = proj.shape[0]
    nt = s // CHUNK

    def body(proj_ref, cw_ref, lng_ref, lnb_ref, ws_ref, bsb_ref, cat_ref, pcar_ref):
        @pl.when(pl.program_id(0) == 0)
        def _():
            pcar_ref[...] = jnp.zeros_like(pcar_ref)

        row, tril = _causal_mask()
        for j in range(8):
            cs = slice(CHUNK * j, CHUNK * (j + 1))

            def grp(k):
                return proj_ref[:, k * D_MODEL + CHUNK * j : k * D_MODEL + CHUNK * (j + 1)].astype(F32)

            w = cw_ref[:, cs]
            p = grp(1) * grp(2)
            p1, p2 = _rows_from_above(p, pcar_ref[:, cs])
            pcar_ref[:, cs] = p[CHUNK - 8 :]
            cv = w[0:1] * p2 + w[1:2] * p1 + w[2:3] * p
            sa = _silu(grp(3), with_grad=False)
            cat_ref[:, cs] = ((grp(0) * cv) * sa).astype(BF16)
            gu = _gelu(grp(4), with_grad=False)
            gv = _gelu(grp(5), with_grad=False)
            dv = gv - jnp.mean(gv, axis=-1, keepdims=True)
            vn = dv * lax.rsqrt(jnp.mean(dv * dv, axis=-1, keepdims=True) + EPS)
            vn = vn * lng_ref[:, cs] + lnb_ref[:, cs]
            wc = jnp.where(tril, ws_ref[j], 0.0).astype(BF16)
            sp = _dot(wc, vn.astype(BF16)) + bsb_ref[j]
            sb = _silu(grp(6), with_grad=False)
            cat_ref[:, D_MODEL + CHUNK * j : D_MODEL + CHUNK * (j + 1)] = ((gu * sp) * sb).astype(BF16)

    whole = lambda shape: pl.BlockSpec(shape, lambda i: (0,) * len(shape))
    (cat,), plan_outs = _call(
        body,
        name="mixer_fwd",
        operands=[proj, conv_w, ln_g, ln_b, ws, bs_b],
        grid=(nt,),
        in_specs=[
            pl.BlockSpec((CHUNK, IN_DIM), lambda i: (i, 0)),
            whole((8, D_MODEL)),
            whole((1, D_MODEL)),
            whole((1, D_MODEL)),
            whole((8, CHUNK, CHUNK)),
            whole((8, CHUNK, CHUNK)),
        ],
        out_specs=[pl.BlockSpec((CHUNK, MIX_DIM), lambda i: (i, 0))],
        out_shape=[jax.ShapeDtypeStruct((s, MIX_DIM), BF16)],
        scratch_shapes=[pltpu.VMEM((8, D_MODEL), F32)],
        plans=plans,
        pass_on_at=0.85,
    )
    return cat, plan_outs


def _mixer_bwd(proj, dcat, conv_w, ln_g, ln_b, ws, bs_b, plans=()):
    s = proj.shape[0]
    nt = s // CHUNK

    def body(proj_ref, halo_ref, dcat_ref, cw_ref, lng_ref, lnb_ref, ws_ref, bsb_ref,
             dproj_ref, dcw_ref, dlng_ref, dlnb_ref, dws_ref, dbs_ref,
             car_ref, acc_cw, acc_lng, acc_lnb, acc_bs, wc_scr, wct_scr):
        i = pl.program_id(0)
        tile = nt - 1 - i
        row, tril = _causal_mask()

        @pl.when(i == 0)
        def _():
            for j in range(8):
                wc = jnp.where(tril, ws_ref[j], 0.0)
                wc_scr[j] = wc.astype(BF16)
                wct_scr[j] = wc.T.astype(BF16)
            car_ref[...] = jnp.zeros_like(car_ref)
            acc_cw[...] = jnp.zeros_like(acc_cw)
            acc_lng[...] = jnp.zeros_like(acc_lng)
            acc_lnb[...] = jnp.zeros_like(acc_lnb)
            acc_bs[...] = jnp.zeros_like(acc_bs)
            dws_ref[...] = jnp.zeros_like(dws_ref)

        has_prev = jnp.where(tile > 0, 1.0, 0.0).astype(F32)
        for j in range(8):
            cs = slice(CHUNK * j, CHUNK * (j + 1))

            def col(k):
                return slice(k * D_MODEL + CHUNK * j, k * D_MODEL + CHUNK * (j + 1))

            def grp(k):
                return proj_ref[:, col(k)].astype(F32)

            w = cw_ref[:, cs]
            gb, gc, xa = grp(0), grp(1), grp(2)
            p = gc * xa
            pprev = halo_ref[:, col(1)].astype(F32) * halo_ref[:, col(2)].astype(F32) * has_prev
            p1, p2 = _rows_from_above(p, pprev[8:16])
            cv = w[0:1] * p2 + w[1:2] * p1 + w[2:3] * p
            za = grp(3)
            sa, dsa = _silu(za)
            da = dcat_ref[:, cs].astype(F32)
            da_gb = da * gb
            dproj_ref[:, col(0)] = ((da * sa) * cv).astype(BF16)
            dproj_ref[:, col(3)] = ((da_gb * cv) * dsa).astype(BF16)
            dcv = da_gb * sa
            d1, d2 = _rows_from_below(dcv, car_ref[:, cs])
            car_ref[:, cs] = dcv[0:8]
            dp = w[2:3] * dcv + w[1:2] * d1 + w[0:1] * d2
            dproj_ref[:, col(1)] = (dp * xa).astype(BF16)
            dproj_ref[:, col(2)] = (dp * gc).astype(BF16)
            acc_cw[0, :, cs] += _rows8(dcv * p2)
            acc_cw[1, :, cs] += _rows8(dcv * p1)
            acc_cw[2, :, cs] += _rows8(dcv * p)
            gu, dgu = _gelu(grp(4))
            gv, dgv = _gelu(grp(5))
            dv = gv - jnp.mean(gv, axis=-1, keepdims=True)
            rstd = lax.rsqrt(jnp.mean(dv * dv, axis=-1, keepdims=True) + EPS)
            vnh = dv * rstd
            lng = lng_ref[:, cs]
            vnb = (vnh * lng + lnb_ref[:, cs]).astype(BF16)
            sp = _dot(wc_scr[j], vnb) + bsb_ref[j]
            zb = grp(6)
            sb, dsb = _silu(zb)
            db = dcat_ref[:, D_MODEL + CHUNK * j : D_MODEL + CHUNK * (j + 1)].astype(F32)
            db_sp, db_gu = db * sp, db * gu
            dproj_ref[:, col(4)] = ((db_sp * sb) * dgu).astype(BF16)
            dproj_ref[:, col(6)] = ((db_sp * gu) * dsb).astype(BF16)
            dsp = db_gu * sb
            acc_bs[j] += dsp
            dspb = dsp.astype(BF16)
            dws_ref[j] += _dot_nt(dspb, vnb)
            dvn = _dot(wct_scr[j], dspb)
            acc_lnb[:, cs] += _rows8(dvn)
            acc_lng[:, cs] += _rows8(dvn * vnh)
            dvh = dvn * lng
            dgvv = rstd * (dvh - jnp.mean(dvh, axis=-1, keepdims=True) - vnh * jnp.mean(dvh * vnh, axis=-1, keepdims=True))
            dproj_ref[:, col(5)] = (dgvv * dgv).astype(BF16)

        @pl.when(i == nt - 1)
        def _():
            dlng_ref[...] = jnp.sum(acc_lng[...], axis=0, keepdims=True)
            dlnb_ref[...] = jnp.sum(acc_lnb[...], axis=0, keepdims=True)
            dcw_ref[...] = jnp.zeros_like(dcw_ref)
            for k in range(3):
                dcw_ref[k : k + 1, :] = jnp.sum(acc_cw[k], axis=0, keepdims=True)
            for j in range(8):
                dws_ref[j] = jnp.where(tril, dws_ref[j], 0.0)
                dbs_ref[:, CHUNK * j : CHUNK * (j + 1)] = jnp.sum(acc_bs[j].T, axis=0, keepdims=True)

    whole = lambda shape: pl.BlockSpec(shape, lambda i: (0,) * len(shape))
    halo_rows = 16
    per = CHUNK // halo_rows
    return _call(
        body,
        name="mixer_bwd",
        operands=[proj, proj, dcat, conv_w, ln_g, ln_b, ws, bs_b],
        plans=plans,
        grid=(nt,),
        in_specs=[
            pl.BlockSpec((CHUNK, IN_DIM), lambda i: (nt - 1 - i, 0)),
            pl.BlockSpec((halo_rows, IN_DIM), lambda i: (jnp.maximum((nt - 1 - i) * per - 1, 0), 0)),
            pl.BlockSpec((CHUNK, MIX_DIM), lambda i: (nt - 1 - i, 0)),
            whole((8, D_MODEL)),
            whole((1, D_MODEL)),
            whole((1, D_MODEL)),
            whole((8, CHUNK, CHUNK)),
            whole((8, CHUNK, CHUNK)),
        ],
        out_specs=[
            pl.BlockSpec((CHUNK, IN_DIM), lambda i: (nt - 1 - i, 0)),
            whole((8, D_MODEL)),
            whole((1, D_MODEL)),
            whole((1, D_MODEL)),
            whole((8, CHUNK, CHUNK)),
            whole((1, D_MODEL)),
        ],
        out_shape=[
            jax.ShapeDtypeStruct((s, IN_DIM), BF16),
            jax.ShapeDtypeStruct((8, D_MODEL), F32),
            jax.ShapeDtypeStruct((1, D_MODEL), F32),
            jax.ShapeDtypeStruct((1, D_MODEL), F32),
            jax.ShapeDtypeStruct((8, CHUNK, CHUNK), F32),
            jax.ShapeDtypeStruct((1, D_MODEL), F32),
        ],
        scratch_shapes=[
            pltpu.VMEM((8, D_MODEL), F32),
            pltpu.VMEM((3, 8, D_MODEL), F32),
            pltpu.VMEM((8, D_MODEL), F32),
            pltpu.VMEM((8, D_MODEL), F32),
            pltpu.VMEM((8, CHUNK, CHUNK), F32),
            pltpu.VMEM((8, CHUNK, CHUNK), BF16),
            pltpu.VMEM((8, CHUNK, CHUNK), BF16),
        ],
    )


def _kv_fwd(mem, g_mem, w_kv):
    def body(mem_ref, g_ref, w_ref, k_ref, vv_ref, mt_ref):
        mh, _ = _rms(mem_ref[...])
        m = mh * g_ref[...]
        kv = _dot(m.astype(BF16), w_ref[...])
        k_ref[...] = kv[:, :D_MODEL].astype(BF16)
        vv_ref[...] = kv[:, D_MODEL:].astype(BF16)
        mt_ref[...] = m.T.astype(BF16)

    return pl.pallas_call(
        body,
        name="kv_fwd",
        out_shape=[
            jax.ShapeDtypeStruct((MEM_LEN, D_MODEL), BF16),
            jax.ShapeDtypeStruct((MEM_LEN, D_MODEL), BF16),
            jax.ShapeDtypeStruct((D_MODEL, MEM_LEN), BF16),
        ],
    )(mem, g_mem, w_kv)


def _kv_bwd(mem, w_kv, mt, dk, dvv):
    def body(mem_ref, w_ref, mt_ref, dk_ref, dvv_ref, dw_ref, dwb_ref, dg_ref):
        dkv = jnp.concatenate([dk_ref[...], dvv_ref[...]], axis=1).astype(BF16)
        dw = _dot(mt_ref[...], dkv)
        dw_ref[...] = dw
        dwb_ref[...] = dw.astype(BF16)
        dm = _dot_nt(dkv, w_ref[...])
        mh, _ = _rms(mem_ref[...])
        dg_ref[...] = jnp.sum(dm * mh, axis=0, keepdims=True)

    return pl.pallas_call(
        body,
        name="kv_bwd",
        out_shape=[
            jax.ShapeDtypeStruct((D_MODEL, 2 * D_MODEL), F32),
            jax.ShapeDtypeStruct((D_MODEL, 2 * D_MODEL), BF16),
            jax.ShapeDtypeStruct((1, D_MODEL), F32),
        ],
    )(mem, w_kv, mt, dk, dvv)


def _attn(x, cat, target, w_out, w_q, w_xo, k, vv, g_x, g_f):
    s = x.shape[0]
    tm = min(TM_ATTN, s)
    sub = min(SUB_ATTN, tm)
    nt = s // tm
    scale = 1.0 / math.sqrt(X_HEAD_DIM)

    def body(x_ref, cat_ref, t_ref, wout_ref, wq_ref, wxo_ref, k_ref, vv_ref, gx_ref, gf_ref,
             dcat_ref, catt_ref, h2t_ref, ot_ref, dx2b_ref, dqb_ref, dx1b_ref,
             dk_ref, dvv_ref, dgf_ref, dgx_ref, loss_ref,
             p_scr, q_scr, o_scr, dq_scr, acc_gf, acc_gx, acc_loss):
        i = pl.program_id(0)

        @pl.when(i == 0)
        def _():
            dk_ref[...] = jnp.zeros_like(dk_ref)
            dvv_ref[...] = jnp.zeros_like(dvv_ref)
            acc_gf[...] = jnp.zeros_like(acc_gf)
            acc_gx[...] = jnp.zeros_like(acc_gx)
            acc_loss[...] = jnp.zeros_like(acc_loss)

        subs = [slice(u * sub, (u + 1) * sub) for u in range(tm // sub)]
        heads = [slice(X_HEAD_DIM * h, X_HEAD_DIM * (h + 1)) for h in range(X_HEADS)]
        gx, gf = gx_ref[...], gf_ref[...]
        x1, x1h, r2 = [], [], []
        for rs in subs:
            cat = cat_ref[rs, :]
            catt_ref[:, rs] = cat.astype(F32).T.astype(BF16)
            v = x_ref[rs, :] + _dot(cat, wout_ref[...])
            vh, r = _rms(v)
            h2 = vh * gx
            h2t_ref[:, rs] = h2.T.astype(BF16)
            q_scr[rs, :] = _dot(h2.astype(BF16), wq_ref[...]).astype(BF16)
            x1.append(v), x1h.append(vh), r2.append(r)
        for rs in subs:
            for hs in heads:
                sc = _dot_nt(q_scr[rs, hs], k_ref[:, hs]) * scale
                e = jnp.exp(sc - jnp.max(sc, axis=-1, keepdims=True))
                p = e / jnp.sum(e, axis=-1, keepdims=True)
                p_scr[rs, hs] = p
                o_scr[rs, hs] = _dot(p.astype(BF16), vv_ref[:, hs])
        dx2, sq, d_gf = [], None, None
        for u, rs in enumerate(subs):
            o = o_scr[rs, :]
            ot_ref[:, rs] = o.T.astype(BF16)
            x2 = x1[u] + _dot(o.astype(BF16), wxo_ref[...])
            x2h, r3 = _rms(x2)
            err = x2h * gf - t_ref[rs, :]
            dy = err * (1.0 / D_MODEL)
            sq = _rows8(err * err) if sq is None else sq + _rows8(err * err)
            d_gf = _rows8(dy * x2h) if d_gf is None else d_gf + _rows8(dy * x2h)
            dx2.append(_rms_bwd(dy * gf, x2h, r3))
        acc_loss[...] += sq
        acc_gf[...] += d_gf
        d_vv, d_k = [None] * X_HEADS, [None] * X_HEADS
        for u, rs in enumerate(subs):
            dx2b = dx2[u].astype(BF16)
            dx2b_ref[rs, :] = dx2b
            dob = _dot_nt(dx2b, wxo_ref[...]).astype(BF16)
            for h, hs in enumerate(heads):
                p = p_scr[rs, hs]
                dp = _dot_nt(dob[:, hs], vv_ref[:, hs])
                ds = p * (dp - jnp.sum(dp * p, axis=-1, keepdims=True)) * scale
                dq_scr[rs, hs] = _dot(ds.astype(BF16), k_ref[:, hs]).astype(BF16)
                dvv_h = _dot(p.T.astype(BF16), dob[:, hs])
                dk_h = _dot(ds.T.astype(BF16), q_scr[rs, hs])
                d_vv[h] = dvv_h if d_vv[h] is None else d_vv[h] + dvv_h
                d_k[h] = dk_h if d_k[h] is None else d_k[h] + dk_h
        for h, hs in enumerate(heads):
            dvv_ref[:, hs] += d_vv[h]
            dk_ref[:, hs] += d_k[h]
        d_gx = None
        for u, rs in enumerate(subs):
            dqb = dq_scr[rs, :]
            dqb_ref[rs, :] = dqb
            dh2 = _dot_nt(dqb, wq_ref[...])
            d_gx = _rows8(dh2 * x1h[u]) if d_gx is None else d_gx + _rows8(dh2 * x1h[u])
            dx1 = dx2[u] + _rms_bwd(dh2 * gx, x1h[u], r2[u])
            dx1b = dx1.astype(BF16)
            dx1b_ref[rs, :] = dx1b
            dcat_ref[rs, :] = _dot_nt(dx1b, wout_ref[...]).astype(BF16)
        acc_gx[...] += d_gx

        @pl.when(i == nt - 1)
        def _():
            dgf_ref[...] = jnp.sum(acc_gf[...], axis=0, keepdims=True)
            dgx_ref[...] = jnp.sum(acc_gx[...], axis=0, keepdims=True)
            loss_ref[...] = jnp.sum(acc_loss[...], axis=0, keepdims=True)

    whole = lambda shape: pl.BlockSpec(shape, lambda i: (0,) * len(shape))
    once = lambda shape: pl.BlockSpec(shape, lambda i: (0,) * len(shape), pipeline_mode=pl.Buffered(1))
    rows = lambda width: pl.BlockSpec((tm, width), lambda i: (i, 0))
    cols = lambda height: pl.BlockSpec((height, tm), lambda i: (0, i))
    vec = jax.ShapeDtypeStruct((1, D_MODEL), F32)
    return pl.pallas_call(
        body,
        name="attn",
        grid=(nt,),
        in_specs=[
            rows(D_MODEL), rows(MIX_DIM), rows(D_MODEL),
            once((MIX_DIM, D_MODEL)), once((D_MODEL, D_MODEL)), once((D_MODEL, D_MODEL)),
            once((MEM_LEN, D_MODEL)), once((MEM_LEN, D_MODEL)),
            whole((1, D_MODEL)), whole((1, D_MODEL)),
        ],
        out_specs=[
            rows(MIX_DIM), cols(MIX_DIM), cols(D_MODEL), cols(D_MODEL),
            rows(D_MODEL), rows(D_MODEL), rows(D_MODEL),
            whole((MEM_LEN, D_MODEL)), whole((MEM_LEN, D_MODEL)),
            whole((1, D_MODEL)), whole((1, D_MODEL)), whole((1, D_MODEL)),
        ],
        out_shape=[
            jax.ShapeDtypeStruct((s, MIX_DIM), BF16),
            jax.ShapeDtypeStruct((MIX_DIM, s), BF16),
            jax.ShapeDtypeStruct((D_MODEL, s), BF16),
            jax.ShapeDtypeStruct((D_MODEL, s), BF16),
            jax.ShapeDtypeStruct((s, D_MODEL), BF16),
            jax.ShapeDtypeStruct((s, D_MODEL), BF16),
            jax.ShapeDtypeStruct((s, D_MODEL), BF16),
            jax.ShapeDtypeStruct((MEM_LEN, D_MODEL), F32),
            jax.ShapeDtypeStruct((MEM_LEN, D_MODEL), F32),
            vec, vec, vec,
        ],
        scratch_shapes=[
            pltpu.VMEM((tm, D_MODEL), F32),
            pltpu.VMEM((tm, D_MODEL), BF16),
            pltpu.VMEM((tm, D_MODEL), F32),
            pltpu.VMEM((tm, D_MODEL), BF16),
            pltpu.VMEM((8, D_MODEL), F32),
            pltpu.VMEM((8, D_MODEL), F32),
            pltpu.VMEM((8, D_MODEL), F32),
        ],
        compiler_params=pltpu.CompilerParams(dimension_semantics=("arbitrary",)),
    )(x, cat, target, w_out, w_q, w_xo, k, vv, g_x, g_f)


def _matmul_f32(name, a, b, rows=None, plans=()):
    first, m = (0, a.shape[0]) if rows is None else rows
    kk = a.shape[1]
    n = b.shape[1]
    tm, tn, tk = min(m, TM_DW), min(n, D_MODEL), min(kk, TK_DW)
    nk = kk // tk
    i0 = first // tm

    def body(a_ref, b_ref, o_ref, ob_ref):
        prod = _dot(a_ref[...], b_ref[...])
        if nk == 1:
            o_ref[...] = prod
            ob_ref[...] = prod.astype(BF16)
            return
        kid = pl.program_id(2)

        @pl.when(kid == 0)
        def _():
            o_ref[...] = prod

        @pl.when(kid > 0)
        def _():
            o_ref[...] += prod

        @pl.when(kid == nk - 1)
        def _():
            ob_ref[...] = o_ref[...].astype(BF16)

    out_spec = pl.BlockSpec((tm, tn), lambda i, j, k: (i, j))
    outs, plan_outs = _call(
        body,
        name=name,
        operands=[a, b],
        grid=(m // tm, n // tn, nk),
        in_specs=[pl.BlockSpec((tm, tk), lambda i, j, k: (i + i0, k)), pl.BlockSpec((tk, tn), lambda i, j, k: (k, j))],
        out_specs=[out_spec, out_spec],
        out_shape=[jax.ShapeDtypeStruct((m, n), F32), jax.ShapeDtypeStruct((m, n), BF16)],
        plans=plans,
    )
    return outs, plan_outs


def _dh1(name, dproj, w_in, x, dx1, g, tiles, into=None, after=(), plans=()):
    s = x.shape[0]
    tm = min(TM_DH1, s)
    t0, nt = tiles[0], tiles[1] - tiles[0]

    def body(dp_ref, w_ref, x_ref, dx1_ref, g_ref, *rest):
        gx_ref, dg_ref, acc_g = rest[-3:]
        i = pl.program_id(0)

        @pl.when(i == 0)
        def _():
            acc_g[...] = jnp.zeros_like(acc_g)

        dh1 = _dot_nt(dp_ref[...], w_ref[...])
        xh, r = _rms(x_ref[...])
        acc_g[...] += _rows8(dh1 * xh)
        gx_ref[...] = dx1_ref[...].astype(F32) + _rms_bwd(dh1 * g_ref[...], xh, r)

        @pl.when(i == nt - 1)
        def _():
            dg_ref[...] = jnp.sum(acc_g[...], axis=0, keepdims=True)

    rows = pl.BlockSpec((tm, D_MODEL), lambda i: (i + t0, 0))
    return _call(
        body,
        name=name,
        operands=[dproj, w_in, x, dx1, g] + ([into] if into is not None else []) + list(after),
        grid=(nt,),
        in_specs=[
            pl.BlockSpec((tm, IN_DIM), lambda i: (i + t0, 0)),
            pl.BlockSpec((D_MODEL, IN_DIM), lambda i: (0, 0), pipeline_mode=pl.Buffered(1)),
            rows,
            rows,
            pl.BlockSpec((1, D_MODEL), lambda i: (0, 0)),
        ] + [pl.BlockSpec(memory_space=pl.ANY)] * ((into is not None) + len(after)),
        out_specs=[rows, pl.BlockSpec((1, D_MODEL), lambda i: (0, 0))],
        out_shape=[jax.ShapeDtypeStruct((s, D_MODEL), F32), jax.ShapeDtypeStruct((1, D_MODEL), F32)],
        scratch_shapes=[pltpu.VMEM((8, D_MODEL), F32)],
        plans=plans,
        aliases={5: 0} if into is not None else None,
    )


def _pair_sum(name, grad, got, axis, size, core, plans=()):
    shard = list(grad.shape)
    shard[axis] = size
    r, cdim = shard
    tr = min(r, TR_PAIR_SUM)
    nr = r // tr

    def body(core_ref, g_ref, got_ref, o_ref):
        o_ref[0] = (g_ref[...] + got_ref[0].astype(F32)).astype(BF16)

    if axis == 1:
        g_map = lambda q, t, core_ref: (t, 2 * q + core_ref[0])
    else:
        g_map = lambda q, t, core_ref: ((2 * q + core_ref[0]) * nr + t, 0)
    (out,), plan_outs = _call(
        body,
        name=name,
        prefetch=[core],
        operands=[grad, got],
        grid=(N_CHIP, nr),
        in_specs=[
            pl.BlockSpec((tr, cdim), g_map),
            pl.BlockSpec((1, tr, cdim), lambda q, t, core_ref: (q, t, 0)),
        ],
        out_specs=[pl.BlockSpec((1, tr, cdim), lambda q, t, core_ref: (q, t, 0))],
        out_shape=[jax.ShapeDtypeStruct((N_CHIP, r, cdim), BF16)],
        plans=plans,
    )
    return (out, plan_outs) if plans else out


def _adam_step(w, m, v, g):
    c1 = 1.0 / (1.0 - ADAM_B1**ADAM_STEP)
    c2 = 1.0 / (1.0 - ADAM_B2**ADAM_STEP)
    m_new = ADAM_B1 * m + (1.0 - ADAM_B1) * g
    v_new = ADAM_B2 * v + (1.0 - ADAM_B2) * (g * g)
    return -ADAM_LR * ((m_new * c1) / (jnp.sqrt(v_new * c2) + ADAM_EPS) + ADAM_WD * w), m_new, v_new


def _update_small(dev, vec_w, vec_m, vec_v, conv_w, conv_m, conv_v, small, landed):
    n_vec = len(vec_w)

    def body(dev_ref, *refs):
        refs = list(refs)
        take = lambda k: [refs.pop(0) for _ in range(k)]
        w_refs, m_refs, v_refs = take(n_vec), take(n_vec), take(n_vec)
        cw_ref, cm_ref, cv_ref, small_ref, land_ref, small_conv_ref, land_conv_ref = take(7)
        vec_outs, conv_outs, (loss_ref,) = take(4 * n_vec), take(4), take(1)

        def total(own_ref, others_ref):
            acc = None
            for d in range(N_DEV):
                part = jnp.where(dev_ref[0] == d, own_ref[0], others_ref[d])
                acc = part if acc is None else acc + part
            return acc

        g_all = total(small_ref, land_ref)
        for i in range(n_vec):
            g = g_all[i : i + 1]
            delta, m_new, v_new = _adam_step(w_refs[i][...], m_refs[i][...], v_refs[i][...], g)
            for ref, val in zip(vec_outs[4 * i : 4 * i + 4], (g, delta, m_new, v_new)):
                ref[...] = val
        g = total(small_conv_ref, land_conv_ref)[0:3]
        delta, m_new, v_new = _adam_step(cw_ref[0], cm_ref[0], cv_ref[0], g)
        for ref, val in zip(conv_outs, (g, delta, m_new, v_new)):
            ref[0] = val
        loss_ref[...] = jnp.broadcast_to(0.5 * jnp.sum(g_all[7:8], axis=-1, keepdims=True) / D_MODEL, loss_ref.shape)

    vec_spec = pl.BlockSpec((1, D_MODEL), lambda i, dev_ref: (0, 0))
    conv_spec = pl.BlockSpec((1, 3, CHUNK), lambda i, dev_ref: (0, 0, 0))
    vec_shape = jax.ShapeDtypeStruct((1, D_MODEL), F32)
    conv_shape = jax.ShapeDtypeStruct((1, 3, CHUNK), F32)
    outs, _ = _call(
        body,
        name="update_small",
        prefetch=[dev],
        operands=list(vec_w) + list(vec_m) + list(vec_v) + [conv_w, conv_m, conv_v, small, landed, small, landed],
        grid=(1,),
        in_specs=[vec_spec] * (3 * n_vec) + [conv_spec] * 3 + [
            pl.BlockSpec((1, 16, D_MODEL), lambda i, dev_ref: (0, 0, 0)),
            pl.BlockSpec((N_DEV, 16, D_MODEL), lambda i, dev_ref: (0, 0, 0)),
            pl.BlockSpec((1, 8, CHUNK), lambda i, dev_ref: (0, 1, dev_ref[0])),
            pl.BlockSpec((N_DEV, 8, CHUNK), lambda i, dev_ref: (0, 1, dev_ref[0])),
        ],
        out_specs=[vec_spec] * (4 * n_vec) + [conv_spec] * 4 + [pl.BlockSpec((1, CHUNK), lambda i, dev_ref: (0, 0))],
        out_shape=[vec_shape] * (4 * n_vec) + [conv_shape] * 4 + [jax.ShapeDtypeStruct((1, CHUNK), F32)],
    )
    return [outs[4 * i : 4 * i + 4] for i in range(n_vec)], outs[4 * n_vec : 4 * n_vec + 4], outs[-1]


def _adamw(name, groups, steps, own_slot=None, plans=()):
    lists = lambda p: list(p) if isinstance(p, (list, tuple)) else [p]
    groups = [(g[0], g[1], g[2], lists(g[3]), lists(g[4]) if len(g) > 4 else []) for g in groups]
    counts = [3 + len(g[3]) + len(g[4]) for g in groups]
    use_slot = own_slot is not None

    def first_tiles(w, parts):
        firsts, first = [], 0
        for p in parts:
            firsts.append(first)
            first += p.shape[1] // (w.shape[0] // steps)
        return firsts

    def update(t, slot_ref, firsts, w_ref, m_ref, v_ref, p_refs, o_refs, g_ref, d_ref, nm_ref, nv_ref):
        g = None
        for k, p_ref in enumerate(p_refs):
            gk = None
            for q in range(p_ref.shape[0]):
                part = p_ref[q].astype(F32)
                if o_refs:
                    part = jnp.where(slot_ref[0] == q, o_refs[k][0].astype(F32), part)
                gk = part if gk is None else gk + part
            g = gk if g is None else jnp.where(t >= firsts[k], gk, g)
        g_ref[...] = g
        d_ref[...], nm_ref[...], nv_ref[...] = _adam_step(w_ref[...], m_ref[...], v_ref[...], g)

    def body(*refs):
        refs = list(refs)
        slot_ref = refs.pop(0) if use_slot else None
        t = pl.program_id(0)
        ins, outs = refs[: sum(counts)], refs[sum(counts) :]
        for i, (w, _, _, parts, own) in enumerate(groups):
            mine = ins[sum(counts[:i]) : sum(counts[: i + 1])]
            update(t, slot_ref, first_tiles(w, parts), *mine[:3], mine[3 : 3 + len(parts)], mine[3 + len(parts) :],
                   *outs[4 * i : 4 * i + 4])

    operands, in_specs, out_specs, out_shape = [], [], [], []
    for w, m, v, parts, own in groups:
        r, cdim = w.shape
        tr = r // steps
        firsts = first_tiles(w, parts)

        def tile_of(k, t, firsts=firsts, parts=parts, tr=tr):
            return jnp.clip(t - firsts[k], 0, parts[k].shape[1] // tr - 1)

        blk = pl.BlockSpec((tr, cdim), lambda t, *_: (t, 0))
        operands += [w, m, v] + parts + own
        in_specs += [blk, blk, blk]
        in_specs += [pl.BlockSpec((p.shape[0], tr, cdim), lambda t, *_, k=k, tile_of=tile_of: (0, tile_of(k, t), 0)) for k, p in enumerate(parts)]
        in_specs += [pl.BlockSpec((1, tr, cdim), lambda t, slot_ref, k=k, tile_of=tile_of: (slot_ref[0], tile_of(k, t), 0)) for k in range(len(own))]
        out_specs += [blk] * 4
        out_shape += [jax.ShapeDtypeStruct((r, cdim), F32)] * 4
    outs, plan_outs = _call(
        body,
        name=name,
        prefetch=[own_slot] if use_slot else [],
        operands=operands,
        grid=(steps,),
        in_specs=in_specs,
        out_specs=out_specs,
        out_shape=out_shape,
        plans=plans,
    )
    return [outs[4 * i : 4 * i + 4] for i in range(len(groups))], plan_outs


def _chip_routes(n):
    x, y, c = _place()
    my_chip = 2 * x + y
    return [(a, (*chip, c), 2 * chip[0] + chip[1], my_chip, 2 * chip[0] + chip[1])
            for a in range(n) for chip in _other_chips(x, y)]


def _gather_routes(n):
    x, y, c = _place()
    flip = lambda v, f: 1 - v if f else v
    peers = [(flip(x, fx), flip(y, fy), flip(c, fc)) for fx in (0, 1) for fy in (0, 1) for fc in (0, 1) if fx + fy + fc]
    return [(0, p, 0, 4 * x + 2 * y + c, 4 * p[0] + 2 * p[1] + p[2]) for p in peers]


_chip_routes.copies_per_array = N_CHIP - 1
_gather_routes.copies_per_array = N_DEV - 1


def _split_start(name, routes, srcs, land_shapes):
    n = len(srcs)
    hbm = pl.BlockSpec(memory_space=pltpu.HBM)
    sem = pl.BlockSpec(memory_space=pltpu.SEMAPHORE)

    def body(*refs):
        src_refs, land_refs = refs[:n], refs[n : 2 * n]
        send_sems, recv_sems = refs[2 * n], refs[2 * n + 1]
        token = refs[-1]
        for k, (a, peer, src_slot, there, _) in enumerate(routes(n)):
            pltpu.make_async_remote_copy(
                src_ref=src_refs[a].at[src_slot], dst_ref=land_refs[a].at[there],
                send_sem=send_sems.at[k], recv_sem=recv_sems.at[k], device_id=peer, device_id_type=MESH).start()
        token[...] = jnp.zeros_like(token)

    in_hbm = [pltpu.with_memory_space_constraint(p, pltpu.HBM) for p in srcs]
    in_hbm += [pltpu.with_memory_space_constraint(lax.empty(shape, p.dtype), pltpu.HBM) for p, shape in zip(srcs, land_shapes)]
    n_copies = routes.copies_per_array * n
    res = pl.pallas_call(
        body,
        name=name,
        out_shape=[pltpu.SemaphoreType.DMA((n_copies,)), pltpu.SemaphoreType.DMA((n_copies,))]
        + [pltpu.HBM(p.shape, p.dtype) for p in srcs] + [pltpu.HBM(shape, p.dtype) for p, shape in zip(srcs, land_shapes)]
        + [jax.ShapeDtypeStruct((8, CHUNK), F32)],
        in_specs=[hbm] * (2 * n),
        out_specs=[sem, sem] + [hbm] * (2 * n) + [pl.BlockSpec(memory_space=pltpu.VMEM)],
        input_output_aliases={i: 2 + i for i in range(2 * n)},
        compiler_params=pltpu.CompilerParams(has_side_effects=pltpu.SideEffectType.DATAFLOW_SIDE_EFFECTING),
    )(*in_hbm)
    return res[:-1], res[-1]


def _split_wait(name, routes, in_flight, after):
    n = (len(in_flight) - 2) // 2
    hbm = pl.BlockSpec(memory_space=pltpu.HBM)
    sem = pl.BlockSpec(memory_space=pltpu.SEMAPHORE)
    send_sems, recv_sems, *bufs = in_flight

    def body(*refs):
        src_refs, land_refs = refs[:n], refs[n : 2 * n]
        send_ref, recv_ref = refs[2 * n], refs[2 * n + 1]
        for k, (a, peer, src_slot, _, here) in enumerate(routes(n)):
            cp = pltpu.make_async_remote_copy(
                src_ref=src_refs[a].at[src_slot], dst_ref=land_refs[a].at[here],
                send_sem=send_ref.at[k], recv_sem=recv_ref.at[k], device_id=peer, device_id_type=MESH)
            cp.wait_send()
            cp.wait_recv()

    res = pl.pallas_call(
        body,
        name=name,
        out_shape=[pltpu.HBM(b.shape, b.dtype) for b in bufs],
        in_specs=[hbm] * (2 * n) + [sem, sem] + [pl.BlockSpec(memory_space=pl.ANY)] * len(after),
        out_specs=[hbm] * (2 * n),
        input_output_aliases={i: i for i in range(2 * n)},
        compiler_params=pltpu.CompilerParams(has_side_effects=pltpu.SideEffectType.DATAFLOW_SIDE_EFFECTING),
    )(*bufs, send_sems, recv_sems, *after)
    return res[:n], res[n:]


def kernel(x, mem, norm_mix_g, w_in, conv_w, gm_ln_g, gm_ln_b, gm_ws, gm_bs, w_out, norm_x_g, norm_mem_g, w_q, w_kv, w_xo, norm_final_g, loss_target, m_norm_mix_g, m_w_in, m_conv_w, m_gm_ln_g, m_gm_ln_b, m_gm_ws, m_gm_bs, m_w_out, m_norm_x_g, m_norm_mem_g, m_w_q, m_w_kv, m_w_xo, m_norm_final_g, v_norm_mix_g, v_w_in, v_conv_w, v_gm_ln_g, v_gm_ln_b, v_gm_ws, v_gm_bs, v_w_out, v_norm_x_g, v_norm_mem_g, v_w_q, v_w_kv, v_w_xo, v_norm_final_g):
    s = x.shape[1]
    dev = 4 * lax.axis_index("x") + 2 * lax.axis_index("y") + lax.axis_index("c")
    core = lax.axis_index("c").astype(jnp.int32).reshape(1)
    x2 = x[0]
    target = loss_target[0]
    pad_taps = lambda t: jnp.pad(t[0], ((0, 5), (0, 0)))

    big_names = ["w_in", "w_out", "w_q", "w_kv", "w_xo"]
    big_w = dict(w_in=w_in[0], w_out=w_out[0], w_q=w_q[0], w_kv=w_kv[0], w_xo=w_xo[0])
    big_m = dict(w_in=m_w_in[0], w_out=m_w_out[0], w_q=m_w_q[0], w_kv=m_w_kv[0], w_xo=m_w_xo[0])
    big_v = dict(w_in=v_w_in[0], w_out=v_w_out[0], w_q=v_w_q[0], w_kv=v_w_kv[0], w_xo=v_w_xo[0])
    shard_axis = dict(w_in=1, w_out=0, w_q=0, w_kv=1, w_xo=0)
    full_shape = dict(w_in=(D_MODEL, IN_DIM), w_out=(MIX_DIM, D_MODEL), w_q=(D_MODEL, D_MODEL),
                      w_kv=(D_MODEL, 2 * D_MODEL), w_xo=(D_MODEL, D_MODEL))
    shard_size = {k: big_w[k].shape[shard_axis[k]] for k in big_names}
    others = big_names[1:]
    gather_spec = lambda k: (full_shape[k], shard_axis[k], shard_size[k])
    reduce_spec = lambda k: (shard_axis[k], shard_size[k])
    ws = gm_ws[0]
    bs_b = jnp.broadcast_to(gm_bs[0][:, :, None], (8, CHUNK, CHUNK))
    g_f = norm_final_g.reshape(1, D_MODEL)

    px, py = lax.axis_index("x"), lax.axis_index("y")
    chip_order = jnp.stack([2 * px + py, 2 * (1 - px) + py, 2 * px + 1 - py, 2 * (1 - px) + 1 - py]).astype(jnp.int32)
    (proj, h1, h1t, w_in_full), ((w_out_full, conv_g),) = _inproj(
        x2, norm_mix_g, big_w["w_in"].astype(BF16), chip_order,
        plans=[_gather_plan([big_w["w_out"].astype(BF16), pad_taps(conv_w)[None]], [gather_spec("w_out"), ((N_DEV, 8, CHUNK), 0, 1)])])
    conv_full = conv_g.transpose(1, 0, 2).reshape(8, D_MODEL)
    attn_w = ["w_q", "w_kv", "w_xo"]
    cat, (gathered,) = _mixer_fwd(
        proj, conv_full, gm_ln_g, gm_ln_b, ws, bs_b,
        plans=[_gather_plan([big_w[k].astype(BF16) for k in attn_w], [gather_spec(k) for k in attn_w])])
    wf = dict(zip(attn_w, gathered), w_out=w_out_full)
    k, vv, mt = _kv_fwd(mem[0], norm_mem_g, wf["w_kv"])
    (dcat, catt, h2t, ot, dx2b, dqb, dx1b, dk, dvv, d_gf, d_gx, sq_err) = _attn(
        x2, cat, target, wf["w_out"], wf["w_q"], wf["w_xo"], k, vv, norm_x_g, g_f)

    def pair_sums(names, grads, got):
        return [_pair_sum("pair_sum_" + k, grads[k], got[i], *reduce_spec(k), core) for i, k in enumerate(names)]

    sibling_plan = lambda names, grads: _sibling_plan([grads[k] for k in names], [reduce_spec(k) for k in names])
    early, late = ["w_out", "w_kv", "w_q"], ["w_xo"]
    big_g, big_gb = {}, {}
    (big_g["w_out"], big_gb["w_out"]), _ = _matmul_f32("dw_out", catt, dx1b)
    big_g["w_kv"], big_gb["w_kv"], d_gmem = _kv_bwd(mem[0], wf["w_kv"], mt, dk, dvv)
    (big_g["w_q"], big_gb["w_q"]), (got_a,) = _matmul_f32("dw_q", h2t, dqb, plans=[sibling_plan(early[:2], big_gb)])
    (big_g["w_xo"], big_gb["w_xo"]), (got_b,) = _matmul_f32("dw_xo", ot, dx2b, plans=[sibling_plan(early[2:], big_gb)])
    got_early = got_a + got_b
    (dproj, d_cw, d_lng, d_lnb, d_ws, d_bs), (parts_early, got_late) = _mixer_bwd(
        proj, dcat, conv_full, gm_ln_g, gm_ln_b, ws, bs_b,
        plans=[_chip_plan(pair_sums(early, big_g, got_early)), sibling_plan(late, big_gb)])
    half = D_MODEL // 2
    in_spec = reduce_spec("w_in")
    (dw_top, dwb_top), _ = _matmul_f32("dw_in_top", h1t, dproj, rows=(0, half))
    pair_xo = pair_sums(late, big_g, got_late)[0]
    (dw_bot, dwb_bot), (got_top, (ws_all,)) = _matmul_f32(
        "dw_in_bot", h1t, dproj, rows=(half, half),
        plans=[_sibling_plan([dwb_top], [in_spec]),
               _gather_plan([d_ws.reshape(1, 8 * CHUNK, CHUNK).astype(BF16)], [((N_DEV, 8 * CHUNK, CHUNK), 0, 1)])])
    pair_top, ((got_bot,),) = _pair_sum("pair_sum_w_in_top", dw_top, got_top[0], *in_spec, core,
                                        plans=[_sibling_plan([dwb_bot], [in_spec])])
    pair_bot = _pair_sum("pair_sum_w_in_bot", dw_bot, got_bot, *in_spec, core)
    chip_shapes = [p.shape for p in (pair_top, pair_bot, pair_xo)]
    chips_in_flight, token = _split_start("grad_chip_exchange_start", _chip_routes, [pair_top, pair_bot, pair_xo], chip_shapes)
    n_tiles = s // min(TM_DH1, s)
    (grad_x, d_gmix), _ = _dh1("dh1", dproj, w_in_full, x2, dx1b, norm_mix_g, (0, n_tiles), after=[token])
    vec_names = ["norm_mix_g", "gm_ln_g", "gm_ln_b", "gm_bs", "norm_x_g", "norm_mem_g", "norm_final_g"]
    vec_g = [d_gmix, d_lng, d_lnb, d_bs, d_gx, d_gmem, d_gf]
    small = jnp.concatenate(vec_g + [sq_err, d_cw], axis=0)[None]
    small_in_flight, _ = _split_start("small_all_gather_start", _gather_routes, [small], [(N_DEV, 16, D_MODEL)])
    flat_ws = lambda t: t.reshape(8 * CHUNK, CHUNK)
    updated, _ = _adamw(
        "adamw_early",
        [(big_w[k], big_m[k], big_v[k], got_k) for k, got_k in zip(early, parts_early)]
        + [(flat_ws(ws), flat_ws(m_gm_ws), flat_ws(v_gm_ws), ws_all)], steps=STEPS_ADAMW // 2)
    big_out = dict(zip(early, updated[:3]))
    ws_out = updated[3]
    (pair_top, pair_bot, pair_xo), (land_top, land_bot, land_xo) = _split_wait(
        "grad_chip_exchange_wait", _chip_routes, chips_in_flight, [d_gmix] + [u[0] for u in updated])
    my_chip = (2 * px + py).astype(jnp.int32).reshape(1)
    (big_out["w_in"], big_out["w_xo"]), _ = _adamw(
        "adamw_late",
        [(big_w["w_in"], big_m["w_in"], big_v["w_in"], [land_top, land_bot], [pair_top, pair_bot]),
         (big_w["w_xo"], big_m["w_xo"], big_v["w_xo"], land_xo, pair_xo)],
        steps=STEPS_ADAMW, own_slot=my_chip)
    (small,), (small_all,) = _split_wait("small_all_gather_wait", _gather_routes, small_in_flight, [big_out["w_xo"][0]])
    row = lambda t: t.reshape(1, D_MODEL)
    vec_out, conv_out, loss_row = _update_small(
        dev.astype(jnp.int32).reshape(1),
        [norm_mix_g, gm_ln_g, gm_ln_b, row(gm_bs), norm_x_g, norm_mem_g, row(norm_final_g)],
        [m_norm_mix_g, m_gm_ln_g, m_gm_ln_b, row(m_gm_bs), m_norm_x_g, m_norm_mem_g, row(m_norm_final_g)],
        [v_norm_mix_g, v_gm_ln_g, v_gm_ln_b, row(v_gm_bs), v_norm_x_g, v_norm_mem_g, row(v_norm_final_g)],
        conv_w, m_conv_w, v_conv_w, small, small_all)
    loss = loss_row[0, 0]

    def result(name, which):
        if name in big_out:
            return big_out[name][which][None]
        if name == "conv_w":
            return conv_out[which]
        if name == "gm_ws":
            return ws_out[which].reshape(1, 8, CHUNK, CHUNK)
        r = vec_out[vec_names.index(name)][which]
        if name == "gm_bs":
            return r.reshape(1, 8, CHUNK)
        if name == "norm_final_g":
            return r.reshape(D_MODEL)
        return r

    weights = ["norm_mix_g", "w_in", "conv_w", "gm_ln_g", "gm_ln_b", "gm_ws", "gm_bs", "w_out", "norm_x_g", "norm_mem_g", "w_q", "w_kv", "w_xo", "norm_final_g"]
    outs = [loss, grad_x[None]]
    for which in range(4):
        outs += [result(name, which) for name in weights]
    return tuple(outs)
```

```python
import math

import jax
import jax.numpy as jnp
from jax import lax
from jax.experimental import pallas as pl
from jax.experimental.pallas import tpu as pltpu

F32 = jnp.float32
BF16 = jnp.bfloat16
MESH = pl.DeviceIdType.MESH

D_MODEL = 1024
N_DEV = 8
N_CHIP = 4
CHUNK = 128
GROUPS = 7
IN_DIM = GROUPS * D_MODEL
SHARD_IN = IN_DIM // N_DEV
MIX_DIM = 2 * D_MODEL
X_HEADS = 4
X_HEAD_DIM = D_MODEL // X_HEADS
MEM_LEN = 256
EPS = 1e-6
GELU_K0 = math.sqrt(2.0 / math.pi)
GELU_K1 = 0.044715

ADAM_LR = 0.001
ADAM_B1 = 0.9
ADAM_B2 = 0.999
ADAM_EPS = 1e-08
ADAM_WD = 0.01
ADAM_STEP = 10

TM_INPROJ = 512
TM_ATTN = 512
SUB_ATTN = 256
TM_DH1 = 512
TM_DW = 512
TK_DW = 4096
TR_PAIR_SUM = 1024
STEPS_ADAMW = 4


def _dot(a, b):
    return lax.dot_general(a, b, (((1,), (0,)), ((), ())), preferred_element_type=F32)


def _dot_nt(a, b):
    return lax.dot_general(a, b, (((1,), (1,)), ((), ())), preferred_element_type=F32)


def _rows8(v):
    return jnp.sum(v.reshape(v.shape[0] // 8, 8, v.shape[1]), axis=0)


def _gelu(x, with_grad=True):
    x2 = x * x
    t = jnp.tanh(x * (GELU_K0 + (GELU_K0 * GELU_K1) * x2))
    half = 0.5 + 0.5 * t
    g = x * half
    if not with_grad:
        return g
    return g, half + ((0.5 * x) * (1.0 - t * t)) * (GELU_K0 + (3.0 * GELU_K0 * GELU_K1) * x2)


def _silu(z, with_grad=True):
    s = jax.nn.sigmoid(z)
    zs = z * s
    return (zs, s + zs * (1.0 - s)) if with_grad else zs


def _rms(v):
    r = lax.rsqrt(jnp.mean(v * v, axis=-1, keepdims=True) + EPS)
    return v * r, r


def _rms_bwd(dy_g, vh, r):
    return r * (dy_g - vh * jnp.mean(dy_g * vh, axis=-1, keepdims=True))


def _place():
    return lax.axis_index("x"), lax.axis_index("y"), lax.axis_index("c")


def _other_chips(x, y):
    return [(1 - x, y), (x, 1 - y), (1 - x, 1 - y)]


def _pow2_divisor(n):
    return n & (-n)


def _shard_view(ref, axis, size, d):
    start = d * size
    align = _pow2_divisor(size)
    if align > 1:
        start = pl.multiple_of(start, align)
    idx = [slice(None)] * len(ref.shape)
    idx[axis] = pl.ds(start, size)
    return ref.at[tuple(idx)]


class _Plan:
    def __init__(self, ins, out_shape, sems, start, finish, mid=None, aliases=None):
        self.ins, self.out_shape, self.sems = list(ins), list(out_shape), list(sems)
        self.start, self.mid, self.finish = start, mid, finish
        self.aliases = dict(aliases or {})


def _gather_plan(shards, specs):
    n = len(shards)
    n_copy = 7

    def copies(ins, outs, sems):
        send_sems, recv_sems, local_sems = sems
        x, y, c = _place()
        me, sibling = (x, y, c), (x, y, 1 - c)
        chips = _other_chips(x, y)

        def block(a, p):
            _, axis, size = specs[a]
            return _shard_view(outs[a], axis, size, 4 * p[0] + 2 * p[1] + p[2])

        def copy(a, k, p, to, src=None):
            return pltpu.make_async_remote_copy(
                src_ref=block(a, p) if src is None else src,
                dst_ref=block(a, p),
                send_sem=send_sems.at[a * n_copy + k],
                recv_sem=recv_sems.at[a * n_copy + k],
                device_id=to,
                device_id_type=MESH,
            )

        def mine():
            return [pltpu.make_async_copy(ins[a], block(a, me), local_sems.at[a]) for a in range(n)]

        def first():
            return [cp for a in range(n) for cp in
                    [copy(a, 0, me, sibling, src=ins[a])] + [copy(a, 1 + j, me, (*chip, c), src=ins[a]) for j, chip in enumerate(chips)]]

        def landed():
            return [copy(a, 1 + j, (*chip, c), me) for j, chip in enumerate(chips) for a in range(n)]

        def passed():
            return [copy(a, 4 + j, (*chip, c), sibling) for j, chip in enumerate(chips) for a in range(n)]

        def last():
            return [cp for a in range(n) for cp in
                    [copy(a, 0, sibling, me)] + [copy(a, 4 + j, (*chip, 1 - c), me) for j, chip in enumerate(chips)]]

        return mine, first, landed, passed, last

    def start(ins, outs, sems):
        mine, first, _, _, _ = copies(ins, outs, sems)
        for cp in mine() + first():
            cp.start()

    def mid(ins, outs, sems):
        _, _, landed, passed, _ = copies(ins, outs, sems)
        for got, fwd in zip(landed(), passed()):
            got.wait_recv()
            fwd.start()

    def finish(ins, outs, sems):
        mine, first, _, passed, last = copies(ins, outs, sems)
        for cp in last():
            cp.wait_recv()
        for cp in first() + passed():
            cp.wait_send()
        for cp in mine():
            cp.wait()

    return _Plan(
        shards,
        [jax.ShapeDtypeStruct(full, s.dtype) for s, (full, _, _) in zip(shards, specs)],
        [pltpu.SemaphoreType.DMA((n * n_copy,)), pltpu.SemaphoreType.DMA((n * n_copy,)), pltpu.SemaphoreType.DMA((n,))],
        start, finish, mid,
    )


def _sibling_plan(grads, specs):
    n = len(grads)

    def shard_shape(a):
        axis, size = specs[a]
        shp = list(grads[a].shape)
        shp[axis] = size
        return tuple(shp)

    def copies(ins, outs, sems):
        send_sems, recv_sems = sems
        x, y, c = _place()
        return [
            pltpu.make_async_remote_copy(
                src_ref=_shard_view(ins[a], specs[a][0], specs[a][1], 2 * q + (1 - c)),
                dst_ref=outs[a].at[q],
                send_sem=send_sems.at[a * N_CHIP + q],
                recv_sem=recv_sems.at[a * N_CHIP + q],
                device_id=(x, y, 1 - c),
                device_id_type=MESH,
            )
            for a in range(n) for q in range(N_CHIP)
        ]

    def start(ins, outs, sems):
        for cp in copies(ins, outs, sems):
            cp.start()

    def finish(ins, outs, sems):
        for cp in copies(ins, outs, sems):
            cp.wait()

    return _Plan(
        grads,
        [jax.ShapeDtypeStruct((N_CHIP, *shard_shape(a)), grads[a].dtype) for a in range(n)],
        [pltpu.SemaphoreType.DMA((n * N_CHIP,)), pltpu.SemaphoreType.DMA((n * N_CHIP,))],
        start, finish,
    )


def _chip_plan(parts, full_rows=None, row0=None, into=None):
    n = len(parts)
    full_rows = [p.shape[1] for p in parts] if full_rows is None else full_rows
    row0 = [0] * n if row0 is None else row0

    def copies(ins, outs, sems):
        send_sems, recv_sems, local_sems = sems
        x, y, c = _place()
        my_chip = 2 * x + y
        chips = _other_chips(x, y)

        def land(a, q):
            return outs[a].at[q, pl.ds(row0[a], parts[a].shape[1]), :]

        def mine():
            return [pltpu.make_async_copy(ins[a].at[my_chip], land(a, my_chip), local_sems.at[a]) for a in range(n)]

        def remote(sending):
            return [
                pltpu.make_async_remote_copy(
                    src_ref=ins[a].at[2 * chip[0] + chip[1]],
                    dst_ref=land(a, my_chip if sending else 2 * chip[0] + chip[1]),
                    send_sem=send_sems.at[a * 3 + j], recv_sem=recv_sems.at[a * 3 + j],
                    device_id=(*chip, c), device_id_type=MESH)
                for a in range(n) for j, chip in enumerate(chips)
            ]

        return mine, remote

    def start(ins, outs, sems):
        mine, remote = copies(ins, outs, sems)
        for cp in mine() + remote(True):
            cp.start()

    def finish(ins, outs, sems):
        mine, remote = copies(ins, outs, sems)
        for cp in remote(False):
            cp.wait_recv()
        for cp in remote(True):
            cp.wait_send()
        for cp in mine():
            cp.wait()

    return _Plan(
        list(parts) + (list(into) if into is not None else []),
        [jax.ShapeDtypeStruct((N_CHIP, full_rows[a], parts[a].shape[2]), parts[a].dtype) for a in range(n)],
        [pltpu.SemaphoreType.DMA((n * 3,)), pltpu.SemaphoreType.DMA((n * 3,)), pltpu.SemaphoreType.DMA((n,))],
        start, finish,
        aliases={n + a: a for a in range(n)} if into is not None else None,
    )


def _call(body, *, name, operands, out_shape, grid=(), in_specs=None, out_specs=None, scratch_shapes=(),
          plans=(), aliases=None, prefetch=(), start_after_body=False, pass_on_at=0.6):
    operands, out_shape, scratch_shapes = list(operands), list(out_shape), list(scratch_shapes)
    n_pf, n_in, n_out, n_scr = len(prefetch), len(operands), len(out_shape), len(scratch_shapes)
    in_vmem = pl.BlockSpec(memory_space=pltpu.VMEM)
    in_hbm = pl.BlockSpec(memory_space=pl.ANY)
    in_specs = [in_vmem] * n_in if in_specs is None else list(in_specs)
    out_specs = [in_vmem] * n_out if out_specs is None else list(out_specs)
    io_alias = {n_pf + i: o for i, o in (aliases or {}).items()}
    pos_in, pos_out = n_pf + n_in, n_out
    for p in plans:
        for i, o in p.aliases.items():
            io_alias[pos_in + i] = pos_out + o
        pos_in += len(p.ins)
        pos_out += len(p.out_shape)
    steps = math.prod(grid) if grid else 1
    mid_step = min(steps - 1, int(steps * pass_on_at))

    def wrapped(*refs):
        refs = list(refs)
        take = lambda k: [refs.pop(0) for _ in range(k)]
        pf = take(n_pf)
        ins, p_ins = take(n_in), [take(len(p.ins)) for p in plans]
        outs, p_outs = take(n_out), [take(len(p.out_shape)) for p in plans]
        scr, p_sems = take(n_scr), [take(len(p.sems)) for p in plans]
        step = 0
        for ax, g in enumerate(grid):
            step = step * g + pl.program_id(ax)

        def hook(kind, at):
            todo = [(getattr(p, kind), a, b, c) for p, a, b, c in zip(plans, p_ins, p_outs, p_sems) if getattr(p, kind)]

            def run():
                for fn, a, b, c in todo:
                    fn(a, b, c)

            if todo and grid:
                pl.when(step == at)(run)
            elif todo:
                run()

        if not start_after_body:
            hook("start", 0)
        if body is not None:
            body(*pf, *ins, *outs, *scr)
        if start_after_body:
            hook("start", 0)
        hook("mid", mid_step)
        hook("finish", steps - 1)

    layout = dict(
        grid=grid,
        in_specs=in_specs + [in_hbm] * sum(len(p.ins) for p in plans),
        out_specs=out_specs + [in_hbm] * sum(len(p.out_shape) for p in plans),
        scratch_shapes=scratch_shapes + [s for p in plans for s in p.sems],
    )
    if n_pf:
        layout = dict(grid_spec=pltpu.PrefetchScalarGridSpec(num_scalar_prefetch=n_pf, **layout))
    res = pl.pallas_call(
        wrapped,
        name=name,
        out_shape=out_shape + [o for p in plans for o in p.out_shape],
        input_output_aliases=io_alias,
        compiler_params=pltpu.CompilerParams(dimension_semantics=("arbitrary",) * len(grid)) if grid else None,
        **layout,
    )(*prefetch, *operands, *[a for p in plans for a in p.ins])
    res = list(res)
    outs, plan_outs = res[:n_out], []
    pos = n_out
    for p in plans:
        plan_outs.append(res[pos : pos + len(p.out_shape)])
        pos += len(p.out_shape)
    return outs, plan_outs


def _inproj(x, g, w_shard, order, plans=()):
    s = x.shape[0]
    tm = min(TM_INPROJ, s)
    nt = s // tm
    wide = 2 * SHARD_IN
    fwd_at = nt // 2

    def body(order_ref, x_ref, g_ref, w_hbm, proj_ref, h1_ref, h1t_ref, wfull_hbm,
             w_scr, h1_scr, send_sems, recv_sems, local_sems):
        k, i = pl.program_id(0), pl.program_id(1)
        px, py, c = _place()
        me, sibling = (px, py, c), (px, py, 1 - c)
        chips = _other_chips(px, py)

        def half(slot, core):
            return w_scr.at[slot, :, pl.ds(pl.multiple_of(core * SHARD_IN, 128), SHARD_IN)]

        def copy(n, slot, core, to, src=None):
            return pltpu.make_async_remote_copy(
                src_ref=half(slot, core) if src is None else src, dst_ref=half(slot, core),
                send_sem=send_sems.at[n], recv_sem=recv_sems.at[n], device_id=to, device_id_type=MESH)

        def mine():
            return pltpu.make_async_copy(w_hbm, half(0, c), local_sems.at[0])

        def first():
            return [copy(0, 0, c, sibling, src=w_hbm)] + [copy(1 + j, 1 + j, c, (*chips[j], c), src=w_hbm) for j in range(2)]

        def relay():
            to = (c * px + (1 - c) * (1 - px), c * (1 - py) + (1 - c) * py, c)
            return pltpu.make_async_remote_copy(
                src_ref=half(2 - c, c), dst_ref=half(3, c), send_sem=send_sems.at[3], recv_sem=recv_sems.at[3],
                device_id=to, device_id_type=MESH)

        def save(slot):
            cols = pl.ds(pl.multiple_of(order_ref[slot] * wide, 128), wide)
            return pltpu.make_async_copy(w_scr.at[slot], wfull_hbm.at[:, cols], local_sems.at[1 + slot])

        @pl.when((k == 0) & (i == 0))
        def _():
            mine().start()
            for cp in first():
                cp.start()
            mine().wait()
            copy(0, 0, 1 - c, me).wait_recv()

        for j in range(3):
            @pl.when((k == j + 1) & (i == 0))
            def _():
                copy(4 + j, 1 + j, 1 - c, me).wait_recv()

        @pl.when((k == 3) & (i == 0))
        def _():
            for slot in range(N_CHIP):
                save(slot).start()

        @pl.when(k == 0)
        def _():
            xh, _ = _rms(x_ref[...])
            h = xh * g_ref[...]
            h1_ref[...] = h.astype(BF16)
            h1t_ref[...] = h.T.astype(BF16)
            h1_scr[pl.ds(pl.multiple_of(i * tm, tm), tm), :] = h.astype(BF16)

        proj_ref[...] = _dot(h1_scr[pl.ds(pl.multiple_of(i * tm, tm), tm), :], w_scr[k]).astype(BF16)

        for j in range(3):
            @pl.when((k == 0) & (i == nt - 1) if j < 2 else (k == 2) & (i == fwd_at))
            def _():
                copy(1 + j, 1 + j, c, me).wait_recv()
                copy(4 + j, 1 + j, c, sibling).start()
                if j < 2:
                    pl.when(c == 1 - j)(lambda: relay().start())

        @pl.when((k == 3) & (i == nt - 1))
        def _():
            for cp in first() + [relay()] + [copy(4 + j, 1 + j, c, sibling) for j in range(3)]:
                cp.wait_send()
            for slot in range(N_CHIP):
                save(slot).wait()

    once = lambda k, i, order_ref: (jnp.where(k == 0, i, nt - 1), 0)
    return _call(
        body,
        name="inproj",
        prefetch=[order],
        operands=[x, g, w_shard],
        grid=(N_CHIP, nt),
        in_specs=[
            pl.BlockSpec((tm, D_MODEL), once),
            pl.BlockSpec((1, D_MODEL), lambda k, i, order_ref: (0, 0)),
            pl.BlockSpec(memory_space=pl.ANY),
        ],
        out_specs=[
            pl.BlockSpec((tm, wide), lambda k, i, order_ref: (i, order_ref[k])),
            pl.BlockSpec((tm, D_MODEL), once),
            pl.BlockSpec((D_MODEL, tm), lambda k, i, order_ref: (0, jnp.where(k == 0, i, nt - 1))),
            pl.BlockSpec(memory_space=pl.ANY),
        ],
        out_shape=[
            jax.ShapeDtypeStruct((s, IN_DIM), BF16),
            jax.ShapeDtypeStruct((s, D_MODEL), BF16),
            jax.ShapeDtypeStruct((D_MODEL, s), BF16),
            jax.ShapeDtypeStruct((D_MODEL, IN_DIM), BF16),
        ],
        scratch_shapes=[
            pltpu.VMEM((N_CHIP, D_MODEL, wide), BF16),
            pltpu.VMEM((s, D_MODEL), BF16),
            pltpu.SemaphoreType.DMA((7,)),
            pltpu.SemaphoreType.DMA((7,)),
            pltpu.SemaphoreType.DMA((1 + N_CHIP,)),
        ],
        plans=plans,
        start_after_body=True,
        pass_on_at=0.8,
    )


def _rows_from_above(v, above):
    row = lax.broadcasted_iota(jnp.int32, (8, v.shape[1]), 0)
    r1, r2 = pltpu.roll(v, 1, 0), pltpu.roll(v, 2, 0)
    top1 = jnp.where(row == 0, above[7:8], r1[0:8])
    top2 = jnp.where(row == 0, above[6:7], jnp.where(row == 1, above[7:8], r2[0:8]))
    return jnp.concatenate([top1, r1[8:]], axis=0), jnp.concatenate([top2, r2[8:]], axis=0)


def _rows_from_below(v, below):
    n = v.shape[0]
    row = lax.broadcasted_iota(jnp.int32, (8, v.shape[1]), 0)
    r1, r2 = pltpu.roll(v, n - 1, 0), pltpu.roll(v, n - 2, 0)
    end1 = jnp.where(row == 7, below[0:1], r1[n - 8 :])
    end2 = jnp.where(row == 6, below[0:1], jnp.where(row == 7, below[1:2], r2[n - 8 :]))
    return jnp.concatenate([r1[: n - 8], end1], axis=0), jnp.concatenate([r2[: n - 8], end2], axis=0)


def _causal_mask():
    row = lax.broadcasted_iota(jnp.int32, (CHUNK, CHUNK), 0)
    col = lax.broadcasted_iota(jnp.int32, (CHUNK, CHUNK), 1)
    return row, row >= col


def _mixer_fwd(proj, conv_w, ln_g, ln_b, ws, bs_b, plans=()):
    s = proj.shape[0]
    nt = s // CHUNK

    def body(proj_ref, cw_ref, lng_ref, lnb_ref, ws_ref, bsb_ref, cat_ref, pcar_ref):
        @pl.when(pl.program_id(0) == 0)
        def _():
            pcar_ref[...] = jnp.zeros_like(pcar_ref)

        row, tril = _causal_mask()
        for j in range(8):
            cs = slice(CHUNK * j, CHUNK * (j + 1))

            def grp(k):
                return proj_ref[:, k * D_MODEL + CHUNK * j : k * D_MODEL + CHUNK * (j + 1)].astype(F32)

            w = cw_ref[:, cs]
            p = grp(1) * grp(2)
            p1, p2 = _rows_from_above(p, pcar_ref[:, cs])
            pcar_ref[:, cs] = p[CHUNK - 8 :]
            cv = w[0:1] * p2 + w[1:2] * p1 + w[2:3] * p
            sa = _silu(grp(3), with_grad=False)
            cat_ref[:, cs] = ((grp(0) * cv) * sa).astype(BF16)
            gu = _gelu(grp(4), with_grad=False)
            gv = _gelu(grp(5), with_grad=False)
            dv = gv - jnp.mean(gv, axis=-1, keepdims=True)
            vn = dv * lax.rsqrt(jnp.mean(dv * dv, axis=-1, keepdims=True) + EPS)
            vn = vn * lng_ref[:, cs] + lnb_ref[:, cs]
            wc = jnp.where(tril, ws_ref[j], 0.0).astype(BF16)
            sp = _dot(wc, vn.astype(BF16)) + bsb_ref[j]
            sb = _silu(grp(6), with_grad=False)
            cat_ref[:, D_MODEL + CHUNK * j : D_MODEL + CHUNK * (j + 1)] = ((gu * sp) * sb).astype(BF16)

    whole = lambda shape: pl.BlockSpec(shape, lambda i: (0,) * len(shape))
    (cat,), plan_outs = _call(
        body,
        name="mixer_fwd",
        operands=[proj, conv_w, ln_g, ln_b, ws, bs_b],
        grid=(nt,),
        in_specs=[
            pl.BlockSpec((CHUNK, IN_DIM), lambda i: (i, 0)),
            whole((8, D_MODEL)),
            whole((1, D_MODEL)),
            whole((1, D_MODEL)),
            whole((8, CHUNK, CHUNK)),
            whole((8, CHUNK, CHUNK)),
        ],
        out_specs=[pl.BlockSpec((CHUNK, MIX_DIM), lambda i: (i, 0))],
        out_shape=[jax.ShapeDtypeStruct((s, MIX_DIM), BF16)],
        scratch_shapes=[pltpu.VMEM((8, D_MODEL), F32)],
        plans=plans,
        pass_on_at=0.85,
    )
    return cat, plan_outs


def _mixer_bwd(proj, dcat, conv_w, ln_g, ln_b, ws, bs_b, plans=()):
    s = proj.shape[0]
    nt = s // CHUNK

    def body(proj_ref, halo_ref, dcat_ref, cw_ref, lng_ref, lnb_ref, ws_ref, bsb_ref,
             dproj_ref, dcw_ref, dlng_ref, dlnb_ref, dws_ref, dbs_ref,
             car_ref, acc_cw, acc_lng, acc_lnb, acc_bs, wc_scr, wct_scr):
        i = pl.program_id(0)
        tile = nt - 1 - i
        row, tril = _causal_mask()

        @pl.when(i == 0)
        def _():
            for j in range(8):
                wc = jnp.where(tril, ws_ref[j], 0.0)
                wc_scr[j] = wc.astype(BF16)
                wct_scr[j] = wc.T.astype(BF16)
            car_ref[...] = jnp.zeros_like(car_ref)
            acc_cw[...] = jnp.zeros_like(acc_cw)
            acc_lng[...] = jnp.zeros_like(acc_lng)
            acc_lnb[...] = jnp.zeros_like(acc_lnb)
            acc_bs[...] = jnp.zeros_like(acc_bs)
            dws_ref[...] = jnp.zeros_like(dws_ref)

        has_prev = jnp.where(tile > 0, 1.0, 0.0).astype(F32)
        for j in range(8):
            cs = slice(CHUNK * j, CHUNK * (j + 1))

            def col(k):
                return slice(k * D_MODEL + CHUNK * j, k * D_MODEL + CHUNK * (j + 1))

            def grp(k):
                return proj_ref[:, col(k)].astype(F32)

            w = cw_ref[:, cs]
            gb, gc, xa = grp(0), grp(1), grp(2)
            p = gc * xa
            pprev = halo_ref[:, col(1)].astype(F32) * halo_ref[:, col(2)].astype(F32) * has_prev
            p1, p2 = _rows_from_above(p, pprev[8:16])
            cv = w[0:1] * p2 + w[1:2] * p1 + w[2:3] * p
            za = grp(3)
            sa, dsa = _silu(za)
            da = dcat_ref[:, cs].astype(F32)
            da_gb = da * gb
            dproj_ref[:, col(0)] = ((da * sa) * cv).astype(BF16)
            dproj_ref[:, col(3)] = ((da_gb * cv) * dsa).astype(BF16)
            dcv = da_gb * sa
            d1, d2 = _rows_from_below(dcv, car_ref[:, cs])
            car_ref[:, cs] = dcv[0:8]
            dp = w[2:3] * dcv + w[1:2] * d1 + w[0:1] * d2
            dproj_ref[:, col(1)] = (dp * xa).astype(BF16)
            dproj_ref[:, col(2)] = (dp * gc).astype(BF16)
            acc_cw[0, :, cs] += _rows8(dcv * p2)
            acc_cw[1, :, cs] += _rows8(dcv * p1)
            acc_cw[2, :, cs] += _rows8(dcv * p)
            gu, dgu = _gelu(grp(4))
            gv, dgv = _gelu(grp(5))
            dv = gv - jnp.mean(gv, axis=-1, keepdims=True)
            rstd = lax.rsqrt(jnp.mean(dv * dv, axis=-1, keepdims=True) + EPS)
            vnh = dv * rstd
            lng = lng_ref[:, cs]
            vnb = (vnh * lng + lnb_ref[:, cs]).astype(BF16)
            sp = _dot(wc_scr[j], vnb) + bsb_ref[j]
            zb = grp(6)
            sb, dsb = _silu(zb)
            db = dcat_ref[:, D_MODEL + CHUNK * j : D_MODEL + CHUNK * (j + 1)].astype(F32)
            db_sp, db_gu = db * sp, db * gu
            dproj_ref[:, col(4)] = ((db_sp * sb) * dgu).astype(BF16)
            dproj_ref[:, col(6)] = ((db_sp * gu) * dsb).astype(BF16)
            dsp = db_gu * sb
            acc_bs[j] += dsp
            dspb = dsp.astype(BF16)
            dws_ref[j] += _dot_nt(dspb, vnb)
            dvn = _dot(wct_scr[j], dspb)
            acc_lnb[:, cs] += _rows8(dvn)
            acc_lng[:, cs] += _rows8(dvn * vnh)
            dvh = dvn * lng
            dgvv = rstd * (dvh - jnp.mean(dvh, axis=-1, keepdims=True) - vnh * jnp.mean(dvh * vnh, axis=-1, keepdims=True))
            dproj_ref[:, col(5)] = (dgvv * dgv).astype(BF16)

        @pl.when(i == nt - 1)
        def _():
            dlng_ref[...] = jnp.sum(acc_lng[...], axis=0, keepdims=True)
            dlnb_ref[...] = jnp.sum(acc_lnb[...], axis=0, keepdims=True)
            dcw_ref[...] = jnp.zeros_like(dcw_ref)
            for k in range(3):
                dcw_ref[k : k + 1, :] = jnp.sum(acc_cw[k], axis=0, keepdims=True)
            for j in range(8):
                dws_ref[j] = jnp.where(tril, dws_ref[j], 0.0)
                dbs_ref[:, CHUNK * j : CHUNK * (j + 1)] = jnp.sum(acc_bs[j].T, axis=0, keepdims=True)

    whole = lambda shape: pl.BlockSpec(shape, lambda i: (0,) * len(shape))
    halo_rows = 16
    per = CHUNK // halo_rows
    return _call(
        body,
        name="mixer_bwd",
        operands=[proj, proj, dcat, conv_w, ln_g, ln_b, ws, bs_b],
        plans=plans,
        grid=(nt,),
        in_specs=[
            pl.BlockSpec((CHUNK, IN_DIM), lambda i: (nt - 1 - i, 0)),
            pl.BlockSpec((halo_rows, IN_DIM), lambda i: (jnp.maximum((nt - 1 - i) * per - 1, 0), 0)),
            pl.BlockSpec((CHUNK, MIX_DIM), lambda i: (nt - 1 - i, 0)),
            whole((8, D_MODEL)),
            whole((1, D_MODEL)),
            whole((1, D_MODEL)),
            whole((8, CHUNK, CHUNK)),
            whole((8, CHUNK, CHUNK)),
        ],
        out_specs=[
            pl.BlockSpec((CHUNK, IN_DIM), lambda i: (nt - 1 - i, 0)),
            whole((8, D_MODEL)),
            whole((1, D_MODEL)),
            whole((1, D_MODEL)),
            whole((8, CHUNK, CHUNK)),
            whole((1, D_MODEL)),
        ],
        out_shape=[
            jax.ShapeDtypeStruct((s, IN_DIM), BF16),
            jax.ShapeDtypeStruct((8, D_MODEL), F32),
            jax.ShapeDtypeStruct((1, D_MODEL), F32),
            jax.ShapeDtypeStruct((1, D_MODEL), F32),
            jax.ShapeDtypeStruct((8, CHUNK, CHUNK), F32),
            jax.ShapeDtypeStruct((1, D_MODEL), F32),
        ],
        scratch_shapes=[
            pltpu.VMEM((8, D_MODEL), F32),
            pltpu.VMEM((3, 8, D_MODEL), F32),
            pltpu.VMEM((8, D_MODEL), F32),
            pltpu.VMEM((8, D_MODEL), F32),
            pltpu.VMEM((8, CHUNK, CHUNK), F32),
            pltpu.VMEM((8, CHUNK, CHUNK), BF16),
            pltpu.VMEM((8, CHUNK, CHUNK), BF16),
        ],
    )


def _kv_fwd(mem, g_mem, w_kv):
    def body(mem_ref, g_ref, w_ref, k_ref, vv_ref, mt_ref):
        mh, _ = _rms(mem_ref[...])
        m = mh * g_ref[...]
        kv = _dot(m.astype(BF16), w_ref[...])
        k_ref[...] = kv[:, :D_MODEL].astype(BF16)
        vv_ref[...] = kv[:, D_MODEL:].astype(BF16)
        mt_ref[...] = m.T.astype(BF16)

    return pl.pallas_call(
        body,
        name="kv_fwd",
        out_shape=[
            jax.ShapeDtypeStruct((MEM_LEN, D_MODEL), BF16),
            jax.ShapeDtypeStruct((MEM_LEN, D_MODEL), BF16),
            jax.ShapeDtypeStruct((D_MODEL, MEM_LEN), BF16),
        ],
    )(mem, g_mem, w_kv)


def _kv_bwd(mem, w_kv, mt, dk, dvv):
    def body(mem_ref, w_ref, mt_ref, dk_ref, dvv_ref, dw_ref, dwb_ref, dg_ref):
        dkv = jnp.concatenate([dk_ref[...], dvv_ref[...]], axis=1).astype(BF16)
        dw = _dot(mt_ref[...], dkv)
        dw_ref[...] = dw
        dwb_ref[...] = dw.astype(BF16)
        dm = _dot_nt(dkv, w_ref[...])
        mh, _ = _rms(mem_ref[...])
        dg_ref[...] = jnp.sum(dm * mh, axis=0, keepdims=True)

    return pl.pallas_call(
        body,
        name="kv_bwd",
        out_shape=[
            jax.ShapeDtypeStruct((D_MODEL, 2 * D_MODEL), F32),
            jax.ShapeDtypeStruct((D_MODEL, 2 * D_MODEL), BF16),
            jax.ShapeDtypeStruct((1, D_MODEL), F32),
        ],
    )(mem, w_kv, mt, dk, dvv)


def _attn(x, cat, target, w_out, w_q, w_xo, k, vv, g_x, g_f):
    s = x.shape[0]
    tm = min(TM_ATTN, s)
    sub = min(SUB_ATTN, tm)
    nt = s // tm
    scale = 1.0 / math.sqrt(X_HEAD_DIM)

    def body(x_ref, cat_ref, t_ref, wout_ref, wq_ref, wxo_ref, k_ref, vv_ref, gx_ref, gf_ref,
             dcat_ref, actt_ref, cot_ref,
             dk_ref, dvv_ref, dgf_ref, dgx_ref, loss_ref,
             p_scr, q_scr, o_scr, dq_scr, acc_gf, acc_gx, acc_loss):
        i = pl.program_id(0)

        @pl.when(i == 0)
        def _():
            dk_ref[...] = jnp.zeros_like(dk_ref)
            dvv_ref[...] = jnp.zeros_like(dvv_ref)
            acc_gf[...] = jnp.zeros_like(acc_gf)
            acc_gx[...] = jnp.zeros_like(acc_gx)
            acc_loss[...] = jnp.zeros_like(acc_loss)

        subs = [slice(u * sub, (u + 1) * sub) for u in range(tm // sub)]
        heads = [slice(X_HEAD_DIM * h, X_HEAD_DIM * (h + 1)) for h in range(X_HEADS)]
        gx, gf = gx_ref[...], gf_ref[...]
        x1, x1h, r2 = [], [], []
        for rs in subs:
            cat = cat_ref[rs, :]
            actt_ref[0:MIX_DIM, rs] = cat.astype(F32).T.astype(BF16)
            v = x_ref[rs, :] + _dot(cat, wout_ref[...])
            vh, r = _rms(v)
            h2 = vh * gx
            actt_ref[MIX_DIM : MIX_DIM + D_MODEL, rs] = h2.T.astype(BF16)
            q_scr[rs, :] = _dot(h2.astype(BF16), wq_ref[...]).astype(BF16)
            x1.append(v), x1h.append(vh), r2.append(r)
        for rs in subs:
            for hs in heads:
                sc = _dot_nt(q_scr[rs, hs], k_ref[:, hs]) * scale
                e = jnp.exp(sc - jnp.max(sc, axis=-1, keepdims=True))
                p = e / jnp.sum(e, axis=-1, keepdims=True)
                p_scr[rs, hs] = p
                o_scr[rs, hs] = _dot(p.astype(BF16), vv_ref[:, hs])
        dx2, sq, d_gf = [], None, None
        for u, rs in enumerate(subs):
            o = o_scr[rs, :]
            actt_ref[MIX_DIM + D_MODEL :, rs] = o.T.astype(BF16)
            x2 = x1[u] + _dot(o.astype(BF16), wxo_ref[...])
            x2h, r3 = _rms(x2)
            err = x2h * gf - t_ref[rs, :]
            dy = err * (1.0 / D_MODEL)
            sq = _rows8(err * err) if sq is None else sq + _rows8(err * err)
            d_gf = _rows8(dy * x2h) if d_gf is None else d_gf + _rows8(dy * x2h)
            dx2.append(_rms_bwd(dy * gf, x2h, r3))
        acc_loss[...] += sq
        acc_gf[...] += d_gf
        d_vv, d_k = [None] * X_HEADS, [None] * X_HEADS
        for u, rs in enumerate(subs):
            dx2b = dx2[u].astype(BF16)
            cot_ref[2, rs, :] = dx2b
            dob = _dot_nt(dx2b, wxo_ref[...]).astype(BF16)
            for h, hs in enumerate(heads):
                p = p_scr[rs, hs]
                dp = _dot_nt(dob[:, hs], vv_ref[:, hs])
                ds = p * (dp - jnp.sum(dp * p, axis=-1, keepdims=True)) * scale
                dq_scr[rs, hs] = _dot(ds.astype(BF16), k_ref[:, hs]).astype(BF16)
                dvv_h = _dot(p.T.astype(BF16), dob[:, hs])
                dk_h = _dot(ds.T.astype(BF16), q_scr[rs, hs])
                d_vv[h] = dvv_h if d_vv[h] is None else d_vv[h] + dvv_h
                d_k[h] = dk_h if d_k[h] is None else d_k[h] + dk_h
        for h, hs in enumerate(heads):
            dvv_ref[:, hs] += d_vv[h]
            dk_ref[:, hs] += d_k[h]
        d_gx = None
        for u, rs in enumerate(subs):
            dqb = dq_scr[rs, :]
            cot_ref[1, rs, :] = dqb
            dh2 = _dot_nt(dqb, wq_ref[...])
            d_gx = _rows8(dh2 * x1h[u]) if d_gx is None else d_gx + _rows8(dh2 * x1h[u])
            dx1 = dx2[u] + _rms_bwd(dh2 * gx, x1h[u], r2[u])
            dx1b = dx1.astype(BF16)
            cot_ref[0, rs, :] = dx1b
            dcat_ref[rs, :] = _dot_nt(dx1b, wout_ref[...]).astype(BF16)
        acc_gx[...] += d_gx

        @pl.when(i == nt - 1)
        def _():
            dgf_ref[...] = jnp.sum(acc_gf[...], axis=0, keepdims=True)
            dgx_ref[...] = jnp.sum(acc_gx[...], axis=0, keepdims=True)
            loss_ref[...] = jnp.sum(acc_loss[...], axis=0, keepdims=True)

    whole = lambda shape: pl.BlockSpec(shape, lambda i: (0,) * len(shape))
    once = lambda shape: pl.BlockSpec(shape, lambda i: (0,) * len(shape), pipeline_mode=pl.Buffered(1))
    rows = lambda width: pl.BlockSpec((tm, width), lambda i: (i, 0))
    cols = lambda height: pl.BlockSpec((height, tm), lambda i: (0, i))
    vec = jax.ShapeDtypeStruct((1, D_MODEL), F32)
    return pl.pallas_call(
        body,
        name="attn",
        grid=(nt,),
        in_specs=[
            rows(D_MODEL), rows(MIX_DIM), rows(D_MODEL),
            once((MIX_DIM, D_MODEL)), once((D_MODEL, D_MODEL)), once((D_MODEL, D_MODEL)),
            once((MEM_LEN, D_MODEL)), once((MEM_LEN, D_MODEL)),
            whole((1, D_MODEL)), whole((1, D_MODEL)),
        ],
        out_specs=[
            rows(MIX_DIM), cols(2 * MIX_DIM),
            pl.BlockSpec((3, tm, D_MODEL), lambda i: (0, i, 0)),
            whole((MEM_LEN, D_MODEL)), whole((MEM_LEN, D_MODEL)),
            whole((1, D_MODEL)), whole((1, D_MODEL)), whole((1, D_MODEL)),
        ],
        out_shape=[
            jax.ShapeDtypeStruct((s, MIX_DIM), BF16),
            jax.ShapeDtypeStruct((2 * MIX_DIM, s), BF16),
            jax.ShapeDtypeStruct((3, s, D_MODEL), BF16),
            jax.ShapeDtypeStruct((MEM_LEN, D_MODEL), F32),
            jax.ShapeDtypeStruct((MEM_LEN, D_MODEL), F32),
            vec, vec, vec,
        ],
        scratch_shapes=[
            pltpu.VMEM((tm, D_MODEL), F32),
            pltpu.VMEM((tm, D_MODEL), BF16),
            pltpu.VMEM((tm, D_MODEL), F32),
            pltpu.VMEM((tm, D_MODEL), BF16),
            pltpu.VMEM((8, D_MODEL), F32),
            pltpu.VMEM((8, D_MODEL), F32),
            pltpu.VMEM((8, D_MODEL), F32),
        ],
        compiler_params=pltpu.CompilerParams(dimension_semantics=("arbitrary",)),
    )(x, cat, target, w_out, w_q, w_xo, k, vv, g_x, g_f)


def _matmul_f32(name, a, b, rows=None, plans=()):
    first, m = (0, a.shape[0]) if rows is None else rows
    kk = a.shape[1]
    n = b.shape[1]
    tm, tn, tk = min(m, TM_DW), min(n, D_MODEL), min(kk, TK_DW)
    nk = kk // tk
    i0 = first // tm

    def body(a_ref, b_ref, o_ref, ob_ref):
        prod = _dot(a_ref[...], b_ref[...])
        if nk == 1:
            o_ref[...] = prod
            ob_ref[...] = prod.astype(BF16)
            return
        kid = pl.program_id(2)

        @pl.when(kid == 0)
        def _():
            o_ref[...] = prod

        @pl.when(kid > 0)
        def _():
            o_ref[...] += prod

        @pl.when(kid == nk - 1)
        def _():
            ob_ref[...] = o_ref[...].astype(BF16)

    out_spec = pl.BlockSpec((tm, tn), lambda i, j, k: (i, j))
    outs, plan_outs = _call(
        body,
        name=name,
        operands=[a, b],
        grid=(m // tm, n // tn, nk),
        in_specs=[pl.BlockSpec((tm, tk), lambda i, j, k: (i + i0, k)), pl.BlockSpec((tk, tn), lambda i, j, k: (k, j))],
        out_specs=[out_spec, out_spec],
        out_shape=[jax.ShapeDtypeStruct((m, n), F32), jax.ShapeDtypeStruct((m, n), BF16)],
        plans=plans,
    )
    return outs, plan_outs


def _dw_attn(actt, cot, plans=()):
    s = actt.shape[1]
    tm = TM_DW
    counts = [MIX_DIM // tm, D_MODEL // tm, D_MODEL // tm]
    starts = [0, counts[0], counts[0] + counts[1]]
    which = lambda i: jnp.where(i < starts[1], 0, jnp.where(i < starts[2], 1, 2))

    def body(a_ref, b_ref, *outs):
        i = pl.program_id(0)
        prod = _dot(a_ref[...], b_ref[0])
        for m in range(3):
            @pl.when(which(i) == m)
            def _():
                outs[2 * m][...] = prod
                outs[2 * m + 1][...] = prod.astype(BF16)

    out_specs, out_shape = [], []
    for m in range(3):
        spec = pl.BlockSpec((tm, D_MODEL), lambda i, m=m: (jnp.clip(i - starts[m], 0, counts[m] - 1), 0))
        out_specs += [spec, spec]
        out_shape += [jax.ShapeDtypeStruct((counts[m] * tm, D_MODEL), F32), jax.ShapeDtypeStruct((counts[m] * tm, D_MODEL), BF16)]
    outs, plan_outs = _call(
        body,
        name="dw_attn",
        operands=[actt, cot],
        grid=(sum(counts),),
        in_specs=[pl.BlockSpec((tm, s), lambda i: (i, 0)), pl.BlockSpec((1, s, D_MODEL), lambda i: (which(i), 0, 0))],
        out_specs=out_specs,
        out_shape=out_shape,
        plans=plans,
    )
    return [outs[2 * m : 2 * m + 2] for m in range(3)], plan_outs


def _dh1(name, dproj, w_in, x, dx1, g, tiles, into=None, after=(), plans=()):
    s = x.shape[0]
    tm = min(TM_DH1, s)
    t0, nt = tiles[0], tiles[1] - tiles[0]

    def body(dp_ref, w_ref, x_ref, dx1_ref, g_ref, *rest):
        gx_ref, dg_ref, acc_g = rest[-3:]
        i = pl.program_id(0)

        @pl.when(i == 0)
        def _():
            acc_g[...] = jnp.zeros_like(acc_g)

        dh1 = _dot_nt(dp_ref[...], w_ref[...])
        xh, r = _rms(x_ref[...])
        acc_g[...] += _rows8(dh1 * xh)
        gx_ref[...] = dx1_ref[0].astype(F32) + _rms_bwd(dh1 * g_ref[...], xh, r)

        @pl.when(i == nt - 1)
        def _():
            dg_ref[...] = jnp.sum(acc_g[...], axis=0, keepdims=True)

    rows = pl.BlockSpec((tm, D_MODEL), lambda i: (i + t0, 0))
    return _call(
        body,
        name=name,
        operands=[dproj, w_in, x, dx1, g] + ([into] if into is not None else []) + list(after),
        grid=(nt,),
        in_specs=[
            pl.BlockSpec((tm, IN_DIM), lambda i: (i + t0, 0)),
            pl.BlockSpec((D_MODEL, IN_DIM), lambda i: (0, 0), pipeline_mode=pl.Buffered(1)),
            rows,
            pl.BlockSpec((1, tm, D_MODEL), lambda i: (0, i + t0, 0)),
            pl.BlockSpec((1, D_MODEL), lambda i: (0, 0)),
        ] + [pl.BlockSpec(memory_space=pl.ANY)] * ((into is not None) + len(after)),
        out_specs=[rows, pl.BlockSpec((1, D_MODEL), lambda i: (0, 0))],
        out_shape=[jax.ShapeDtypeStruct((s, D_MODEL), F32), jax.ShapeDtypeStruct((1, D_MODEL), F32)],
        scratch_shapes=[pltpu.VMEM((8, D_MODEL), F32)],
        plans=plans,
        aliases={5: 0} if into is not None else None,
    )


def _pair_sum(name, grad, got, axis, size, core, plans=()):
    shard = list(grad.shape)
    shard[axis] = size
    r, cdim = shard
    tr = min(r, TR_PAIR_SUM)
    nr = r // tr

    def body(core_ref, g_ref, got_ref, o_ref):
        o_ref[0] = (g_ref[...] + got_ref[0].astype(F32)).astype(BF16)

    if axis == 1:
        g_map = lambda q, t, core_ref: (t, 2 * q + core_ref[0])
    else:
        g_map = lambda q, t, core_ref: ((2 * q + core_ref[0]) * nr + t, 0)
    (out,), plan_outs = _call(
        body,
        name=name,
        prefetch=[core],
        operands=[grad, got],
        grid=(N_CHIP, nr),
        in_specs=[
            pl.BlockSpec((tr, cdim), g_map),
            pl.BlockSpec((1, tr, cdim), lambda q, t, core_ref: (q, t, 0)),
        ],
        out_specs=[pl.BlockSpec((1, tr, cdim), lambda q, t, core_ref: (q, t, 0))],
        out_shape=[jax.ShapeDtypeStruct((N_CHIP, r, cdim), BF16)],
        plans=plans,
    )
    return (out, plan_outs) if plans else out


def _adam_step(w, m, v, g):
    c1 = 1.0 / (1.0 - ADAM_B1**ADAM_STEP)
    c2 = 1.0 / (1.0 - ADAM_B2**ADAM_STEP)
    m_new = ADAM_B1 * m + (1.0 - ADAM_B1) * g
    v_new = ADAM_B2 * v + (1.0 - ADAM_B2) * (g * g)
    return -ADAM_LR * ((m_new * c1) / (jnp.sqrt(v_new * c2) + ADAM_EPS) + ADAM_WD * w), m_new, v_new


def _update_small(dev, vec_w, vec_m, vec_v, conv_w, conv_m, conv_v, small, landed):
    n_vec = len(vec_w)

    def body(dev_ref, *refs):
        refs = list(refs)
        take = lambda k: [refs.pop(0) for _ in range(k)]
        w_refs, m_refs, v_refs = take(n_vec), take(n_vec), take(n_vec)
        cw_ref, cm_ref, cv_ref, small_ref, land_ref, small_conv_ref, land_conv_ref = take(7)
        vec_outs, conv_outs, (loss_ref,) = take(4 * n_vec), take(4), take(1)

        def total(own_ref, others_ref):
            acc = None
            for d in range(N_DEV):
                part = jnp.where(dev_ref[0] == d, own_ref[0], others_ref[d])
                acc = part if acc is None else acc + part
            return acc

        g_all = total(small_ref, land_ref)
        for i in range(n_vec):
            g = g_all[i : i + 1]
            delta, m_new, v_new = _adam_step(w_refs[i][...], m_refs[i][...], v_refs[i][...], g)
            for ref, val in zip(vec_outs[4 * i : 4 * i + 4], (g, delta, m_new, v_new)):
                ref[...] = val
        g = total(small_conv_ref, land_conv_ref)[0:3]
        delta, m_new, v_new = _adam_step(cw_ref[0], cm_ref[0], cv_ref[0], g)
        for ref, val in zip(conv_outs, (g, delta, m_new, v_new)):
            ref[0] = val
        loss_ref[...] = jnp.broadcast_to(0.5 * jnp.sum(g_all[7:8], axis=-1, keepdims=True) / D_MODEL, loss_ref.shape)

    vec_spec = pl.BlockSpec((1, D_MODEL), lambda i, dev_ref: (0, 0))
    conv_spec = pl.BlockSpec((1, 3, CHUNK), lambda i, dev_ref: (0, 0, 0))
    vec_shape = jax.ShapeDtypeStruct((1, D_MODEL), F32)
    conv_shape = jax.ShapeDtypeStruct((1, 3, CHUNK), F32)
    outs, _ = _call(
        body,
        name="update_small",
        prefetch=[dev],
        operands=list(vec_w) + list(vec_m) + list(vec_v) + [conv_w, conv_m, conv_v, small, landed, small, landed],
        grid=(1,),
        in_specs=[vec_spec] * (3 * n_vec) + [conv_spec] * 3 + [
            pl.BlockSpec((1, 16, D_MODEL), lambda i, dev_ref: (0, 0, 0)),
            pl.BlockSpec((N_DEV, 16, D_MODEL), lambda i, dev_ref: (0, 0, 0)),
            pl.BlockSpec((1, 8, CHUNK), lambda i, dev_ref: (0, 1, dev_ref[0])),
            pl.BlockSpec((N_DEV, 8, CHUNK), lambda i, dev_ref: (0, 1, dev_ref[0])),
        ],
        out_specs=[vec_spec] * (4 * n_vec) + [conv_spec] * 4 + [pl.BlockSpec((1, CHUNK), lambda i, dev_ref: (0, 0))],
        out_shape=[vec_shape] * (4 * n_vec) + [conv_shape] * 4 + [jax.ShapeDtypeStruct((1, CHUNK), F32)],
    )
    return [outs[4 * i : 4 * i + 4] for i in range(n_vec)], outs[4 * n_vec : 4 * n_vec + 4], outs[-1]


def _adamw(name, groups, steps, own_slot=None, plans=()):
    lists = lambda p: list(p) if isinstance(p, (list, tuple)) else [p]
    groups = [(g[0], g[1], g[2], lists(g[3]), lists(g[4]) if len(g) > 4 else []) for g in groups]
    counts = [3 + len(g[3]) + len(g[4]) for g in groups]
    use_slot = own_slot is not None

    def first_tiles(w, parts):
        firsts, first = [], 0
        for p in parts:
            firsts.append(first)
            first += p.shape[1] // (w.shape[0] // steps)
        return firsts

    def update(t, slot_ref, firsts, w_ref, m_ref, v_ref, p_refs, o_refs, g_ref, d_ref, nm_ref, nv_ref):
        g = None
        for k, p_ref in enumerate(p_refs):
            gk = None
            for q in range(p_ref.shape[0]):
                part = p_ref[q].astype(F32)
                if o_refs:
                    part = jnp.where(slot_ref[0] == q, o_refs[k][0].astype(F32), part)
                gk = part if gk is None else gk + part
            g = gk if g is None else jnp.where(t >= firsts[k], gk, g)
        g_ref[...] = g
        d_ref[...], nm_ref[...], nv_ref[...] = _adam_step(w_ref[...], m_ref[...], v_ref[...], g)

    def body(*refs):
        refs = list(refs)
        slot_ref = refs.pop(0) if use_slot else None
        t = pl.program_id(0)
        ins, outs = refs[: sum(counts)], refs[sum(counts) :]
        for i, (w, _, _, parts, own) in enumerate(groups):
            mine = ins[sum(counts[:i]) : sum(counts[: i + 1])]
            update(t, slot_ref, first_tiles(w, parts), *mine[:3], mine[3 : 3 + len(parts)], mine[3 + len(parts) :],
                   *outs[4 * i : 4 * i + 4])

    operands, in_specs, out_specs, out_shape = [], [], [], []
    for w, m, v, parts, own in groups:
        r, cdim = w.shape
        tr = r // steps
        firsts = first_tiles(w, parts)

        def tile_of(k, t, firsts=firsts, parts=parts, tr=tr):
            return jnp.clip(t - firsts[k], 0, parts[k].shape[1] // tr - 1)

        blk = pl.BlockSpec((tr, cdim), lambda t, *_: (t, 0))
        operands += [w, m, v] + parts + own
        in_specs += [blk, blk, blk]
        in_specs += [pl.BlockSpec((p.shape[0], tr, cdim), lambda t, *_, k=k, tile_of=tile_of: (0, tile_of(k, t), 0)) for k, p in enumerate(parts)]
        in_specs += [pl.BlockSpec((1, tr, cdim), lambda t, slot_ref, k=k, tile_of=tile_of: (slot_ref[0], tile_of(k, t), 0)) for k in range(len(own))]
        out_specs += [blk] * 4
        out_shape += [jax.ShapeDtypeStruct((r, cdim), F32)] * 4
    outs, plan_outs = _call(
        body,
        name=name,
        prefetch=[own_slot] if use_slot else [],
        operands=operands,
        grid=(steps,),
        in_specs=in_specs,
        out_specs=out_specs,
        out_shape=out_shape,
        plans=plans,
    )
    return [outs[4 * i : 4 * i + 4] for i in range(len(groups))], plan_outs


def _chip_routes(n):
    x, y, c = _place()
    my_chip = 2 * x + y
    return [(a, (*chip, c), 2 * chip[0] + chip[1], my_chip, 2 * chip[0] + chip[1])
            for a in range(n) for chip in _other_chips(x, y)]


def _gather_routes(n):
    x, y, c = _place()
    flip = lambda v, f: 1 - v if f else v
    peers = [(flip(x, fx), flip(y, fy), flip(c, fc)) for fx in (0, 1) for fy in (0, 1) for fc in (0, 1) if fx + fy + fc]
    return [(0, p, 0, 4 * x + 2 * y + c, 4 * p[0] + 2 * p[1] + p[2]) for p in peers]


_chip_routes.copies_per_array = N_CHIP - 1
_gather_routes.copies_per_array = N_DEV - 1


def _split_start(name, routes, srcs, land_shapes):
    n = len(srcs)
    hbm = pl.BlockSpec(memory_space=pltpu.HBM)
    sem = pl.BlockSpec(memory_space=pltpu.SEMAPHORE)

    def body(*refs):
        src_refs, land_refs = refs[:n], refs[n : 2 * n]
        send_sems, recv_sems = refs[2 * n], refs[2 * n + 1]
        token = refs[-1]
        for k, (a, peer, src_slot, there, _) in enumerate(routes(n)):
            pltpu.make_async_remote_copy(
                src_ref=src_refs[a].at[src_slot], dst_ref=land_refs[a].at[there],
                send_sem=send_sems.at[k], recv_sem=recv_sems.at[k], device_id=peer, device_id_type=MESH).start()
        token[...] = jnp.zeros_like(token)

    in_hbm = [pltpu.with_memory_space_constraint(p, pltpu.HBM) for p in srcs]
    in_hbm += [pltpu.with_memory_space_constraint(lax.empty(shape, p.dtype), pltpu.HBM) for p, shape in zip(srcs, land_shapes)]
    n_copies = routes.copies_per_array * n
    res = pl.pallas_call(
        body,
        name=name,
        out_shape=[pltpu.SemaphoreType.DMA((n_copies,)), pltpu.SemaphoreType.DMA((n_copies,))]
        + [pltpu.HBM(p.shape, p.dtype) for p in srcs] + [pltpu.HBM(shape, p.dtype) for p, shape in zip(srcs, land_shapes)]
        + [jax.ShapeDtypeStruct((8, CHUNK), F32)],
        in_specs=[hbm] * (2 * n),
        out_specs=[sem, sem] + [hbm] * (2 * n) + [pl.BlockSpec(memory_space=pltpu.VMEM)],
        input_output_aliases={i: 2 + i for i in range(2 * n)},
        compiler_params=pltpu.CompilerParams(has_side_effects=pltpu.SideEffectType.DATAFLOW_SIDE_EFFECTING),
    )(*in_hbm)
    return res[:-1], res[-1]


def _split_wait(name, routes, in_flight, after):
    n = (len(in_flight) - 2) // 2
    hbm = pl.BlockSpec(memory_space=pltpu.HBM)
    sem = pl.BlockSpec(memory_space=pltpu.SEMAPHORE)
    send_sems, recv_sems, *bufs = in_flight

    def body(*refs):
        src_refs, land_refs = refs[:n], refs[n : 2 * n]
        send_ref, recv_ref = refs[2 * n], refs[2 * n + 1]
        for k, (a, peer, src_slot, _, here) in enumerate(routes(n)):
            cp = pltpu.make_async_remote_copy(
                src_ref=src_refs[a].at[src_slot], dst_ref=land_refs[a].at[here],
                send_sem=send_ref.at[k], recv_sem=recv_ref.at[k], device_id=peer, device_id_type=MESH)
            cp.wait_send()
            cp.wait_recv()

    res = pl.pallas_call(
        body,
        name=name,
        out_shape=[pltpu.HBM(b.shape, b.dtype) for b in bufs],
        in_specs=[hbm] * (2 * n) + [sem, sem] + [pl.BlockSpec(memory_space=pl.ANY)] * len(after),
        out_specs=[hbm] * (2 * n),
        input_output_aliases={i: i for i in range(2 * n)},
        compiler_params=pltpu.CompilerParams(has_side_effects=pltpu.SideEffectType.DATAFLOW_SIDE_EFFECTING),
    )(*bufs, send_sems, recv_sems, *after)
    return res[:n], res[n:]


def kernel(x, mem, norm_mix_g, w_in, conv_w, gm_ln_g, gm_ln_b, gm_ws, gm_bs, w_out, norm_x_g, norm_mem_g, w_q, w_kv, w_xo, norm_final_g, loss_target, m_norm_mix_g, m_w_in, m_conv_w, m_gm_ln_g, m_gm_ln_b, m_gm_ws, m_gm_bs, m_w_out, m_norm_x_g, m_norm_mem_g, m_w_q, m_w_kv, m_w_xo, m_norm_final_g, v_norm_mix_g, v_w_in, v_conv_w, v_gm_ln_g, v_gm_ln_b, v_gm_ws, v_gm_bs, v_w_out, v_norm_x_g, v_norm_mem_g, v_w_q, v_w_kv, v_w_xo, v_norm_final_g):
    s = x.shape[1]
    dev = 4 * lax.axis_index("x") + 2 * lax.axis_index("y") + lax.axis_index("c")
    core = lax.axis_index("c").astype(jnp.int32).reshape(1)
    x2 = x[0]
    target = loss_target[0]
    pad_taps = lambda t: jnp.pad(t[0], ((0, 5), (0, 0)))

    big_names = ["w_in", "w_out", "w_q", "w_kv", "w_xo"]
    big_w = dict(w_in=w_in[0], w_out=w_out[0], w_q=w_q[0], w_kv=w_kv[0], w_xo=w_xo[0])
    big_m = dict(w_in=m_w_in[0], w_out=m_w_out[0], w_q=m_w_q[0], w_kv=m_w_kv[0], w_xo=m_w_xo[0])
    big_v = dict(w_in=v_w_in[0], w_out=v_w_out[0], w_q=v_w_q[0], w_kv=v_w_kv[0], w_xo=v_w_xo[0])
    shard_axis = dict(w_in=1, w_out=0, w_q=0, w_kv=1, w_xo=0)
    full_shape = dict(w_in=(D_MODEL, IN_DIM), w_out=(MIX_DIM, D_MODEL), w_q=(D_MODEL, D_MODEL),
                      w_kv=(D_MODEL, 2 * D_MODEL), w_xo=(D_MODEL, D_MODEL))
    shard_size = {k: big_w[k].shape[shard_axis[k]] for k in big_names}
    others = big_names[1:]
    gather_spec = lambda k: (full_shape[k], shard_axis[k], shard_size[k])
    reduce_spec = lambda k: (shard_axis[k], shard_size[k])
    ws = gm_ws[0]
    bs_b = jnp.broadcast_to(gm_bs[0][:, :, None], (8, CHUNK, CHUNK))
    g_f = norm_final_g.reshape(1, D_MODEL)

    px, py = lax.axis_index("x"), lax.axis_index("y")
    chip_order = jnp.stack([2 * px + py, 2 * (1 - px) + py, 2 * px + 1 - py, 2 * (1 - px) + 1 - py]).astype(jnp.int32)
    (proj, h1, h1t, w_in_full), ((w_out_full, conv_g),) = _inproj(
        x2, norm_mix_g, big_w["w_in"].astype(BF16), chip_order,
        plans=[_gather_plan([big_w["w_out"].astype(BF16), pad_taps(conv_w)[None]], [gather_spec("w_out"), ((N_DEV, 8, CHUNK), 0, 1)])])
    conv_full = conv_g.transpose(1, 0, 2).reshape(8, D_MODEL)
    attn_w = ["w_q", "w_kv", "w_xo"]
    cat, (gathered,) = _mixer_fwd(
        proj, conv_full, gm_ln_g, gm_ln_b, ws, bs_b,
        plans=[_gather_plan([big_w[k].astype(BF16) for k in attn_w], [gather_spec(k) for k in attn_w])])
    wf = dict(zip(attn_w, gathered), w_out=w_out_full)
    k, vv, mt = _kv_fwd(mem[0], norm_mem_g, wf["w_kv"])
    (dcat, actt, cot, dk, dvv, d_gf, d_gx, sq_err) = _attn(
        x2, cat, target, wf["w_out"], wf["w_q"], wf["w_xo"], k, vv, norm_x_g, g_f)

    def pair_sums(names, grads, got):
        return [_pair_sum("pair_sum_" + k, grads[k], got[i], *reduce_spec(k), core) for i, k in enumerate(names)]

    sibling_plan = lambda names, grads: _sibling_plan([grads[k] for k in names], [reduce_spec(k) for k in names])
    early = ["w_kv", "w_out", "w_q", "w_xo"]
    big_g, big_gb = {}, {}
    big_g["w_kv"], big_gb["w_kv"], d_gmem = _kv_bwd(mem[0], wf["w_kv"], mt, dk, dvv)
    dws, (got_kv,) = _dw_attn(actt, cot, plans=[sibling_plan(early[:1], big_gb)])
    for name, (f32, bf16) in zip(early[1:], dws):
        big_g[name], big_gb[name] = f32, bf16
    pair_kv, (got_rest,) = _pair_sum("pair_sum_w_kv", big_g["w_kv"], got_kv[0], *reduce_spec("w_kv"), core,
                                     plans=[sibling_plan(early[1:], big_gb)])
    (dproj, d_cw, d_lng, d_lnb, d_ws, d_bs), (parts_early,) = _mixer_bwd(
        proj, dcat, conv_full, gm_ln_g, gm_ln_b, ws, bs_b,
        plans=[_chip_plan([pair_kv] + pair_sums(early[1:], big_g, got_rest))])
    half = D_MODEL // 2
    in_spec = reduce_spec("w_in")
    (dw_top, dwb_top), _ = _matmul_f32("dw_in_top", h1t, dproj, rows=(0, half))
    (dw_bot, dwb_bot), (got_top, (ws_all,)) = _matmul_f32(
        "dw_in_bot", h1t, dproj, rows=(half, half),
        plans=[_sibling_plan([dwb_top], [in_spec]),
               _gather_plan([d_ws.reshape(1, 8 * CHUNK, CHUNK).astype(BF16)], [((N_DEV, 8 * CHUNK, CHUNK), 0, 1)])])
    pair_top, ((got_bot,),) = _pair_sum("pair_sum_w_in_top", dw_top, got_top[0], *in_spec, core,
                                        plans=[_sibling_plan([dwb_bot], [in_spec])])
    pair_bot = _pair_sum("pair_sum_w_in_bot", dw_bot, got_bot, *in_spec, core)
    chip_shapes = [p.shape for p in (pair_top, pair_bot)]
    chips_in_flight, token = _split_start("grad_chip_exchange_start", _chip_routes, [pair_top, pair_bot], chip_shapes)
    n_tiles = s // min(TM_DH1, s)
    (grad_x, d_gmix), _ = _dh1("dh1", dproj, w_in_full, x2, cot, norm_mix_g, (0, n_tiles), after=[token])
    vec_names = ["norm_mix_g", "gm_ln_g", "gm_ln_b", "gm_bs", "norm_x_g", "norm_mem_g", "norm_final_g"]
    vec_g = [d_gmix, d_lng, d_lnb, d_bs, d_gx, d_gmem, d_gf]
    small = jnp.concatenate(vec_g + [sq_err, d_cw], axis=0)[None]
    small_in_flight, _ = _split_start("small_all_gather_start", _gather_routes, [small], [(N_DEV, 16, D_MODEL)])
    flat_ws = lambda t: t.reshape(8 * CHUNK, CHUNK)
    updated, _ = _adamw(
        "adamw_early",
        [(big_w[k], big_m[k], big_v[k], got_k) for k, got_k in zip(early, parts_early)]
        + [(flat_ws(ws), flat_ws(m_gm_ws), flat_ws(v_gm_ws), ws_all)], steps=STEPS_ADAMW // 2)
    big_out = dict(zip(early, updated[:4]))
    ws_out = updated[4]
    (pair_top, pair_bot), (land_top, land_bot) = _split_wait(
        "grad_chip_exchange_wait", _chip_routes, chips_in_flight, [d_gmix] + [u[0] for u in updated])
    my_chip = (2 * px + py).astype(jnp.int32).reshape(1)
    (big_out["w_in"],), _ = _adamw(
        "adamw_w_in", [(big_w["w_in"], big_m["w_in"], big_v["w_in"], [land_top, land_bot], [pair_top, pair_bot])],
        steps=STEPS_ADAMW, own_slot=my_chip)
    (small,), (small_all,) = _split_wait("small_all_gather_wait", _gather_routes, small_in_flight, [big_out["w_in"][0]])
    row = lambda t: t.reshape(1, D_MODEL)
    vec_out, conv_out, loss_row = _update_small(
        dev.astype(jnp.int32).reshape(1),
        [norm_mix_g, gm_ln_g, gm_ln_b, row(gm_bs), norm_x_g, norm_mem_g, row(norm_final_g)],
        [m_norm_mix_g, m_gm_ln_g, m_gm_ln_b, row(m_gm_bs), m_norm_x_g, m_norm_mem_g, row(m_norm_final_g)],
        [v_norm_mix_g, v_gm_ln_g, v_gm_ln_b, row(v_gm_bs), v_norm_x_g, v_norm_mem_g, row(v_norm_final_g)],
        conv_w, m_conv_w, v_conv_w, small, small_all)
    loss = loss_row[0, 0]

    def result(name, which):
        if name in big_out:
            return big_out[name][which][None]
        if name == "conv_w":
            return conv_out[which]
        if name == "gm_ws":
            return ws_out[which].reshape(1, 8, CHUNK, CHUNK)
        r = vec_out[vec_names.index(name)][which]
        if name == "gm_bs":
            return r.reshape(1, 8, CHUNK)
        if name == "norm_final_g":
            return r.reshape(D_MODEL)
        return r

    weights = ["norm_mix_g", "w_in", "conv_w", "gm_ln_g", "gm_ln_b", "gm_ws", "gm_bs", "w_out", "norm_x_g", "norm_mem_g", "w_q", "w_kv", "w_xo", "norm_final_g"]
    outs = [loss, grad_x[None]]
    for which in range(4):
        outs += [result(name, which) for name in weights]
    return tuple(outs)
```

```python
import math

import jax
import jax.numpy as jnp
from jax import lax
from jax.experimental import pallas as pl
from jax.experimental.pallas import tpu as pltpu

F32 = jnp.float32
BF16 = jnp.bfloat16
MESH = pl.DeviceIdType.MESH

D_MODEL = 1024
N_DEV = 8
N_CHIP = 4
CHUNK = 128
GROUPS = 7
IN_DIM = GROUPS * D_MODEL
SHARD_IN = IN_DIM // N_DEV
MIX_DIM = 2 * D_MODEL
X_HEADS = 4
X_HEAD_DIM = D_MODEL // X_HEADS
MEM_LEN = 256
EPS = 1e-6
GELU_K0 = math.sqrt(2.0 / math.pi)
GELU_K1 = 0.044715

ADAM_LR = 0.001
ADAM_B1 = 0.9
ADAM_B2 = 0.999
ADAM_EPS = 1e-08
ADAM_WD = 0.01
ADAM_STEP = 10

TM_INPROJ = 512
TM_ATTN = 512
SUB_ATTN = 256
TM_DH1 = 512
TM_DW = 512
TK_DW = 4096
TR_PAIR_SUM = 1024
STEPS_ADAMW = 4


def _dot(a, b):
    return lax.dot_general(a, b, (((1,), (0,)), ((), ())), preferred_element_type=F32)


def _dot_nt(a, b):
    return lax.dot_general(a, b, (((1,), (1,)), ((), ())), preferred_element_type=F32)


def _rows8(v):
    return jnp.sum(v.reshape(v.shape[0] // 8, 8, v.shape[1]), axis=0)


def _gelu(x, with_grad=True):
    x2 = x * x
    t = jnp.tanh(x * (GELU_K0 + (GELU_K0 * GELU_K1) * x2))
    half = 0.5 + 0.5 * t
    g = x * half
    if not with_grad:
        return g
    return g, half + ((0.5 * x) * (1.0 - t * t)) * (GELU_K0 + (3.0 * GELU_K0 * GELU_K1) * x2)


def _silu(z, with_grad=True):
    s = jax.nn.sigmoid(z)
    zs = z * s
    return (zs, s + zs * (1.0 - s)) if with_grad else zs


def _rms(v):
    r = lax.rsqrt(jnp.mean(v * v, axis=-1, keepdims=True) + EPS)
    return v * r, r


def _rms_bwd(dy_g, vh, r):
    return r * (dy_g - vh * jnp.mean(dy_g * vh, axis=-1, keepdims=True))


def _place():
    return lax.axis_index("x"), lax.axis_index("y"), lax.axis_index("c")


def _other_chips(x, y):
    return [(1 - x, y), (x, 1 - y), (1 - x, 1 - y)]


def _pow2_divisor(n):
    return n & (-n)


def _shard_view(ref, axis, size, d):
    start = d * size
    align = _pow2_divisor(size)
    if align > 1:
        start = pl.multiple_of(start, align)
    idx = [slice(None)] * len(ref.shape)
    idx[axis] = pl.ds(start, size)
    return ref.at[tuple(idx)]


class _Plan:
    def __init__(self, ins, out_shape, sems, start, finish, mid=None, aliases=None):
        self.ins, self.out_shape, self.sems = list(ins), list(out_shape), list(sems)
        self.start, self.mid, self.finish = start, mid, finish
        self.aliases = dict(aliases or {})


def _gather_plan(shards, specs):
    n = len(shards)
    n_copy = 7

    def copies(ins, outs, sems):
        send_sems, recv_sems, local_sems = sems
        x, y, c = _place()
        me, sibling = (x, y, c), (x, y, 1 - c)
        chips = _other_chips(x, y)

        def block(a, p):
            _, axis, size = specs[a]
            return _shard_view(outs[a], axis, size, 4 * p[0] + 2 * p[1] + p[2])

        def copy(a, k, p, to, src=None):
            return pltpu.make_async_remote_copy(
                src_ref=block(a, p) if src is None else src,
                dst_ref=block(a, p),
                send_sem=send_sems.at[a * n_copy + k],
                recv_sem=recv_sems.at[a * n_copy + k],
                device_id=to,
                device_id_type=MESH,
            )

        def mine():
            return [pltpu.make_async_copy(ins[a], block(a, me), local_sems.at[a]) for a in range(n)]

        def first():
            return [cp for a in range(n) for cp in
                    [copy(a, 0, me, sibling, src=ins[a])] + [copy(a, 1 + j, me, (*chip, c), src=ins[a]) for j, chip in enumerate(chips)]]

        def landed():
            return [copy(a, 1 + j, (*chip, c), me) for j, chip in enumerate(chips) for a in range(n)]

        def passed():
            return [copy(a, 4 + j, (*chip, c), sibling) for j, chip in enumerate(chips) for a in range(n)]

        def last():
            return [cp for a in range(n) for cp in
                    [copy(a, 0, sibling, me)] + [copy(a, 4 + j, (*chip, 1 - c), me) for j, chip in enumerate(chips)]]

        return mine, first, landed, passed, last

    def start(ins, outs, sems):
        mine, first, _, _, _ = copies(ins, outs, sems)
        for cp in mine() + first():
            cp.start()

    def mid(ins, outs, sems):
        _, _, landed, passed, _ = copies(ins, outs, sems)
        for got, fwd in zip(landed(), passed()):
            got.wait_recv()
            fwd.start()

    def finish(ins, outs, sems):
        mine, first, _, passed, last = copies(ins, outs, sems)
        for cp in last():
            cp.wait_recv()
        for cp in first() + passed():
            cp.wait_send()
        for cp in mine():
            cp.wait()

    return _Plan(
        shards,
        [jax.ShapeDtypeStruct(full, s.dtype) for s, (full, _, _) in zip(shards, specs)],
        [pltpu.SemaphoreType.DMA((n * n_copy,)), pltpu.SemaphoreType.DMA((n * n_copy,)), pltpu.SemaphoreType.DMA((n,))],
        start, finish, mid,
    )


def _sibling_plan(grads, specs):
    n = len(grads)

    def shard_shape(a):
        axis, size = specs[a]
        shp = list(grads[a].shape)
        shp[axis] = size
        return tuple(shp)

    def copies(ins, outs, sems):
        send_sems, recv_sems = sems
        x, y, c = _place()
        return [
            pltpu.make_async_remote_copy(
                src_ref=_shard_view(ins[a], specs[a][0], specs[a][1], 2 * q + (1 - c)),
                dst_ref=outs[a].at[q],
                send_sem=send_sems.at[a * N_CHIP + q],
                recv_sem=recv_sems.at[a * N_CHIP + q],
                device_id=(x, y, 1 - c),
                device_id_type=MESH,
            )
            for a in range(n) for q in range(N_CHIP)
        ]

    def start(ins, outs, sems):
        for cp in copies(ins, outs, sems):
            cp.start()

    def finish(ins, outs, sems):
        for cp in copies(ins, outs, sems):
            cp.wait()

    return _Plan(
        grads,
        [jax.ShapeDtypeStruct((N_CHIP, *shard_shape(a)), grads[a].dtype) for a in range(n)],
        [pltpu.SemaphoreType.DMA((n * N_CHIP,)), pltpu.SemaphoreType.DMA((n * N_CHIP,))],
        start, finish,
    )


def _chip_plan(parts, full_rows=None, row0=None, into=None):
    n = len(parts)
    full_rows = [p.shape[1] for p in parts] if full_rows is None else full_rows
    row0 = [0] * n if row0 is None else row0

    def copies(ins, outs, sems):
        send_sems, recv_sems, local_sems = sems
        x, y, c = _place()
        my_chip = 2 * x + y
        chips = _other_chips(x, y)

        def land(a, q):
            return outs[a].at[q, pl.ds(row0[a], parts[a].shape[1]), :]

        def mine():
            return [pltpu.make_async_copy(ins[a].at[my_chip], land(a, my_chip), local_sems.at[a]) for a in range(n)]

        def remote(sending):
            return [
                pltpu.make_async_remote_copy(
                    src_ref=ins[a].at[2 * chip[0] + chip[1]],
                    dst_ref=land(a, my_chip if sending else 2 * chip[0] + chip[1]),
                    send_sem=send_sems.at[a * 3 + j], recv_sem=recv_sems.at[a * 3 + j],
                    device_id=(*chip, c), device_id_type=MESH)
                for a in range(n) for j, chip in enumerate(chips)
            ]

        return mine, remote

    def start(ins, outs, sems):
        mine, remote = copies(ins, outs, sems)
        for cp in mine() + remote(True):
            cp.start()

    def finish(ins, outs, sems):
        mine, remote = copies(ins, outs, sems)
        for cp in remote(False):
            cp.wait_recv()
        for cp in remote(True):
            cp.wait_send()
        for cp in mine():
            cp.wait()

    return _Plan(
        list(parts) + (list(into) if into is not None else []),
        [jax.ShapeDtypeStruct((N_CHIP, full_rows[a], parts[a].shape[2]), parts[a].dtype) for a in range(n)],
        [pltpu.SemaphoreType.DMA((n * 3,)), pltpu.SemaphoreType.DMA((n * 3,)), pltpu.SemaphoreType.DMA((n,))],
        start, finish,
        aliases={n + a: a for a in range(n)} if into is not None else None,
    )


def _call(body, *, name, operands, out_shape, grid=(), in_specs=None, out_specs=None, scratch_shapes=(),
          plans=(), aliases=None, prefetch=(), start_after_body=False, pass_on_at=0.6):
    operands, out_shape, scratch_shapes = list(operands), list(out_shape), list(scratch_shapes)
    n_pf, n_in, n_out, n_scr = len(prefetch), len(operands), len(out_shape), len(scratch_shapes)
    in_vmem = pl.BlockSpec(memory_space=pltpu.VMEM)
    in_hbm = pl.BlockSpec(memory_space=pl.ANY)
    in_specs = [in_vmem] * n_in if in_specs is None else list(in_specs)
    out_specs = [in_vmem] * n_out if out_specs is None else list(out_specs)
    io_alias = {n_pf + i: o for i, o in (aliases or {}).items()}
    pos_in, pos_out = n_pf + n_in, n_out
    for p in plans:
        for i, o in p.aliases.items():
            io_alias[pos_in + i] = pos_out + o
        pos_in += len(p.ins)
        pos_out += len(p.out_shape)
    steps = math.prod(grid) if grid else 1
    mid_step = min(steps - 1, int(steps * pass_on_at))

    def wrapped(*refs):
        refs = list(refs)
        take = lambda k: [refs.pop(0) for _ in range(k)]
        pf = take(n_pf)
        ins, p_ins = take(n_in), [take(len(p.ins)) for p in plans]
        outs, p_outs = take(n_out), [take(len(p.out_shape)) for p in plans]
        scr, p_sems = take(n_scr), [take(len(p.sems)) for p in plans]
        step = 0
        for ax, g in enumerate(grid):
            step = step * g + pl.program_id(ax)

        def hook(kind, at):
            todo = [(getattr(p, kind), a, b, c) for p, a, b, c in zip(plans, p_ins, p_outs, p_sems) if getattr(p, kind)]

            def run():
                for fn, a, b, c in todo:
                    fn(a, b, c)

            if todo and grid:
                pl.when(step == at)(run)
            elif todo:
                run()

        if not start_after_body:
            hook("start", 0)
        if body is not None:
            body(*pf, *ins, *outs, *scr)
        if start_after_body:
            hook("start", 0)
        hook("mid", mid_step)
        hook("finish", steps - 1)

    layout = dict(
        grid=grid,
        in_specs=in_specs + [in_hbm] * sum(len(p.ins) for p in plans),
        out_specs=out_specs + [in_hbm] * sum(len(p.out_shape) for p in plans),
        scratch_shapes=scratch_shapes + [s for p in plans for s in p.sems],
    )
    if n_pf:
        layout = dict(grid_spec=pltpu.PrefetchScalarGridSpec(num_scalar_prefetch=n_pf, **layout))
    res = pl.pallas_call(
        wrapped,
        name=name,
        out_shape=out_shape + [o for p in plans for o in p.out_shape],
        input_output_aliases=io_alias,
        compiler_params=pltpu.CompilerParams(dimension_semantics=("arbitrary",) * len(grid)) if grid else None,
        **layout,
    )(*prefetch, *operands, *[a for p in plans for a in p.ins])
    res = list(res)
    outs, plan_outs = res[:n_out], []
    pos = n_out
    for p in plans:
        plan_outs.append(res[pos : pos + len(p.out_shape)])
        pos += len(p.out_shape)
    return outs, plan_outs


def _inproj(x, g, w_shard, order, plans=()):
    s = x.shape[0]
    tm = min(TM_INPROJ, s)
    nt = s // tm
    wide = 2 * SHARD_IN
    fwd_at = nt // 2

    def body(order_ref, x_ref, g_ref, w_hbm, proj_ref, h1_ref, h1t_ref, wfull_hbm,
             w_scr, h1_scr, send_sems, recv_sems, local_sems):
        k, i = pl.program_id(0), pl.program_id(1)
        px, py, c = _place()
        me, sibling = (px, py, c), (px, py, 1 - c)
        chips = _other_chips(px, py)

        def half(slot, core):
            return w_scr.at[slot, :, pl.ds(pl.multiple_of(core * SHARD_IN, 128), SHARD_IN)]

        def copy(n, slot, core, to, src=None):
            return pltpu.make_async_remote_copy(
                src_ref=half(slot, core) if src is None else src, dst_ref=half(slot, core),
                send_sem=send_sems.at[n], recv_sem=recv_sems.at[n], device_id=to, device_id_type=MESH)

        def mine():
            return pltpu.make_async_copy(w_hbm, half(0, c), local_sems.at[0])

        def first():
            return [copy(0, 0, c, sibling, src=w_hbm)] + [copy(1 + j, 1 + j, c, (*chips[j], c), src=w_hbm) for j in range(2)]

        def relay():
            to = (c * px + (1 - c) * (1 - px), c * (1 - py) + (1 - c) * py, c)
            return pltpu.make_async_remote_copy(
                src_ref=half(2 - c, c), dst_ref=half(3, c), send_sem=send_sems.at[3], recv_sem=recv_sems.at[3],
                device_id=to, device_id_type=MESH)

        def save(slot):
            cols = pl.ds(pl.multiple_of(order_ref[slot] * wide, 128), wide)
            return pltpu.make_async_copy(w_scr.at[slot], wfull_hbm.at[:, cols], local_sems.at[1 + slot])

        @pl.when((k == 0) & (i == 0))
        def _():
            mine().start()
            for cp in first():
                cp.start()
            mine().wait()
            copy(0, 0, 1 - c, me).wait_recv()

        for j in range(3):
            @pl.when((k == j + 1) & (i == 0))
            def _():
                copy(4 + j, 1 + j, 1 - c, me).wait_recv()

        @pl.when((k == 3) & (i == 0))
        def _():
            for slot in range(N_CHIP):
                save(slot).start()

        @pl.when(k == 0)
        def _():
            xh, _ = _rms(x_ref[...])
            h = xh * g_ref[...]
            h1_ref[...] = h.astype(BF16)
            h1t_ref[...] = h.T.astype(BF16)
            h1_scr[pl.ds(pl.multiple_of(i * tm, tm), tm), :] = h.astype(BF16)

        proj_ref[...] = _dot(h1_scr[pl.ds(pl.multiple_of(i * tm, tm), tm), :], w_scr[k]).astype(BF16)

        for j in range(3):
            @pl.when((k == 0) & (i == nt - 1) if j < 2 else (k == 2) & (i == fwd_at))
            def _():
                copy(1 + j, 1 + j, c, me).wait_recv()
                copy(4 + j, 1 + j, c, sibling).start()
                if j < 2:
                    pl.when(c == 1 - j)(lambda: relay().start())

        @pl.when((k == 3) & (i == nt - 1))
        def _():
            for cp in first() + [relay()] + [copy(4 + j, 1 + j, c, sibling) for j in range(3)]:
                cp.wait_send()
            for slot in range(N_CHIP):
                save(slot).wait()

    once = lambda k, i, order_ref: (jnp.where(k == 0, i, nt - 1), 0)
    return _call(
        body,
        name="inproj",
        prefetch=[order],
        operands=[x, g, w_shard],
        grid=(N_CHIP, nt),
        in_specs=[
            pl.BlockSpec((tm, D_MODEL), once),
            pl.BlockSpec((1, D_MODEL), lambda k, i, order_ref: (0, 0)),
            pl.BlockSpec(memory_space=pl.ANY),
        ],
        out_specs=[
            pl.BlockSpec((tm, wide), lambda k, i, order_ref: (i, order_ref[k])),
            pl.BlockSpec((tm, D_MODEL), once),
            pl.BlockSpec((D_MODEL, tm), lambda k, i, order_ref: (0, jnp.where(k == 0, i, nt - 1))),
            pl.BlockSpec(memory_space=pl.ANY),
        ],
        out_shape=[
            jax.ShapeDtypeStruct((s, IN_DIM), BF16),
            jax.ShapeDtypeStruct((s, D_MODEL), BF16),
            jax.ShapeDtypeStruct((D_MODEL, s), BF16),
            jax.ShapeDtypeStruct((D_MODEL, IN_DIM), BF16),
        ],
        scratch_shapes=[
            pltpu.VMEM((N_CHIP, D_MODEL, wide), BF16),
            pltpu.VMEM((s, D_MODEL), BF16),
            pltpu.SemaphoreType.DMA((7,)),
            pltpu.SemaphoreType.DMA((7,)),
            pltpu.SemaphoreType.DMA((1 + N_CHIP,)),
        ],
        plans=plans,
        start_after_body=True,
        pass_on_at=0.8,
    )


def _rows_from_above(v, above):
    row = lax.broadcasted_iota(jnp.int32, (8, v.shape[1]), 0)
    r1, r2 = pltpu.roll(v, 1, 0), pltpu.roll(v, 2, 0)
    top1 = jnp.where(row == 0, above[7:8], r1[0:8])
    top2 = jnp.where(row == 0, above[6:7], jnp.where(row == 1, above[7:8], r2[0:8]))
    return jnp.concatenate([top1, r1[8:]], axis=0), jnp.concatenate([top2, r2[8:]], axis=0)


def _rows_from_below(v, below):
    n = v.shape[0]
    row = lax.broadcasted_iota(jnp.int32, (8, v.shape[1]), 0)
    r1, r2 = pltpu.roll(v, n - 1, 0), pltpu.roll(v, n - 2, 0)
    end1 = jnp.where(row == 7, below[0:1], r1[n - 8 :])
    end2 = jnp.where(row == 6, below[0:1], jnp.where(row == 7, below[1:2], r2[n - 8 :]))
    return jnp.concatenate([r1[: n - 8], end1], axis=0), jnp.concatenate([r2[: n - 8], end2], axis=0)


def _causal_mask():
    row = lax.broadcasted_iota(jnp.int32, (CHUNK, CHUNK), 0)
    col = lax.broadcasted_iota(jnp.int32, (CHUNK, CHUNK), 1)
    return row, row >= col


def _mixer_fwd(proj, conv_w, ln_g, ln_b, ws, bs_b, plans=()):
    s = proj.shape[0]
    nt = s // CHUNK

    def body(proj_ref, cw_ref, lng_ref, lnb_ref, ws_ref, bsb_ref, cat_ref, pcar_ref):
        @pl.when(pl.program_id(0) == 0)
        def _():
            pcar_ref[...] = jnp.zeros_like(pcar_ref)

        row, tril = _causal_mask()
        for j in range(8):
            cs = slice(CHUNK * j, CHUNK * (j + 1))

            def grp(k):
                return proj_ref[:, k * D_MODEL + CHUNK * j : k * D_MODEL + CHUNK * (j + 1)].astype(F32)

            w = cw_ref[:, cs]
            p = grp(1) * grp(2)
            p1, p2 = _rows_from_above(p, pcar_ref[:, cs])
            pcar_ref[:, cs] = p[CHUNK - 8 :]
            cv = w[0:1] * p2 + w[1:2] * p1 + w[2:3] * p
            sa = _silu(grp(3), with_grad=False)
            cat_ref[:, cs] = ((grp(0) * cv) * sa).astype(BF16)
            gu = _gelu(grp(4), with_grad=False)
            gv = _gelu(grp(5), with_grad=False)
            dv = gv - jnp.mean(gv, axis=-1, keepdims=True)
            vn = dv * lax.rsqrt(jnp.mean(dv * dv, axis=-1, keepdims=True) + EPS)
            vn = vn * lng_ref[:, cs] + lnb_ref[:, cs]
            wc = jnp.where(tril, ws_ref[j], 0.0).astype(BF16)
            sp = _dot(wc, vn.astype(BF16)) + bsb_ref[j]
            sb = _silu(grp(6), with_grad=False)
            cat_ref[:, D_MODEL + CHUNK * j : D_MODEL + CHUNK * (j + 1)] = ((gu * sp) * sb).astype(BF16)

    whole = lambda shape: pl.BlockSpec(shape, lambda i: (0,) * len(shape))
    (cat,), plan_outs = _call(
        body,
        name="mixer_fwd",
        operands=[proj, conv_w, ln_g, ln_b, ws, bs_b],
        grid=(nt,),
        in_specs=[
            pl.BlockSpec((CHUNK, IN_DIM), lambda i: (i, 0)),
            whole((8, D_MODEL)),
            whole((1, D_MODEL)),
            whole((1, D_MODEL)),
            whole((8, CHUNK, CHUNK)),
            whole((8, CHUNK, CHUNK)),
        ],
        out_specs=[pl.BlockSpec((CHUNK, MIX_DIM), lambda i: (i, 0))],
        out_shape=[jax.ShapeDtypeStruct((s, MIX_DIM), BF16)],
        scratch_shapes=[pltpu.VMEM((8, D_MODEL), F32)],
        plans=plans,
        pass_on_at=0.85,
    )
    return cat, plan_outs


def _mixer_bwd(proj, dcat, conv_w, ln_g, ln_b, ws, bs_b, plans=()):
    s = proj.shape[0]
    nt = s // CHUNK

    def body(proj_ref, halo_ref, dcat_ref, cw_ref, lng_ref, lnb_ref, ws_ref, bsb_ref,
             dproj_ref, dcw_ref, dlng_ref, dlnb_ref, dws_ref, dbs_ref,
             car_ref, acc_cw, acc_lng, acc_lnb, acc_bs, wc_scr, wct_scr):
        i = pl.program_id(0)
        tile = nt - 1 - i
        row, tril = _causal_mask()

        @pl.when(i == 0)
        def _():
            for j in range(8):
                wc = jnp.where(tril, ws_ref[j], 0.0)
                wc_scr[j] = wc.astype(BF16)
                wct_scr[j] = wc.T.astype(BF16)
            car_ref[...] = jnp.zeros_like(car_ref)
            acc_cw[...] = jnp.zeros_like(acc_cw)
            acc_lng[...] = jnp.zeros_like(acc_lng)
            acc_lnb[...] = jnp.zeros_like(acc_lnb)
            acc_bs[...] = jnp.zeros_like(acc_bs)
            dws_ref[...] = jnp.zeros_like(dws_ref)

        has_prev = jnp.where(tile > 0, 1.0, 0.0).astype(F32)
        for j in range(8):
            cs = slice(CHUNK * j, CHUNK * (j + 1))

            def col(k):
                return slice(k * D_MODEL + CHUNK * j, k * D_MODEL + CHUNK * (j + 1))

            def grp(k):
                return proj_ref[:, col(k)].astype(F32)

            w = cw_ref[:, cs]
            gb, gc, xa = grp(0), grp(1), grp(2)
            p = gc * xa
            pprev = halo_ref[:, col(1)].astype(F32) * halo_ref[:, col(2)].astype(F32) * has_prev
            p1, p2 = _rows_from_above(p, pprev[8:16])
            cv = w[0:1] * p2 + w[1:2] * p1 + w[2:3] * p
            za = grp(3)
            sa, dsa = _silu(za)
            da = dcat_ref[:, cs].astype(F32)
            da_gb = da * gb
            dproj_ref[:, col(0)] = ((da * sa) * cv).astype(BF16)
            dproj_ref[:, col(3)] = ((da_gb * cv) * dsa).astype(BF16)
            dcv = da_gb * sa
            d1, d2 = _rows_from_below(dcv, car_ref[:, cs])
            car_ref[:, cs] = dcv[0:8]
            dp = w[2:3] * dcv + w[1:2] * d1 + w[0:1] * d2
            dproj_ref[:, col(1)] = (dp * xa).astype(BF16)
            dproj_ref[:, col(2)] = (dp * gc).astype(BF16)
            acc_cw[0, :, cs] += _rows8(dcv * p2)
            acc_cw[1, :, cs] += _rows8(dcv * p1)
            acc_cw[2, :, cs] += _rows8(dcv * p)
            gu, dgu = _gelu(grp(4))
            gv, dgv = _gelu(grp(5))
            dv = gv - jnp.mean(gv, axis=-1, keepdims=True)
            rstd = lax.rsqrt(jnp.mean(dv * dv, axis=-1, keepdims=True) + EPS)
            vnh = dv * rstd
            lng = lng_ref[:, cs]
            vnb = (vnh * lng + lnb_ref[:, cs]).astype(BF16)
            sp = _dot(wc_scr[j], vnb) + bsb_ref[j]
            zb = grp(6)
            sb, dsb = _silu(zb)
            db = dcat_ref[:, D_MODEL + CHUNK * j : D_MODEL + CHUNK * (j + 1)].astype(F32)
            db_sp, db_gu = db * sp, db * gu
            dproj_ref[:, col(4)] = ((db_sp * sb) * dgu).astype(BF16)
            dproj_ref[:, col(6)] = ((db_sp * gu) * dsb).astype(BF16)
            dsp = db_gu * sb
            acc_bs[j] += dsp
            dspb = dsp.astype(BF16)
            dws_ref[j] += _dot_nt(dspb, vnb)
            dvn = _dot(wct_scr[j], dspb)
            acc_lnb[:, cs] += _rows8(dvn)
            acc_lng[:, cs] += _rows8(dvn * vnh)
            dvh = dvn * lng
            dgvv = rstd * (dvh - jnp.mean(dvh, axis=-1, keepdims=True) - vnh * jnp.mean(dvh * vnh, axis=-1, keepdims=True))
            dproj_ref[:, col(5)] = (dgvv * dgv).astype(BF16)

        @pl.when(i == nt - 1)
        def _():
            dlng_ref[...] = jnp.sum(acc_lng[...], axis=0, keepdims=True)
            dlnb_ref[...] = jnp.sum(acc_lnb[...], axis=0, keepdims=True)
            dcw_ref[...] = jnp.zeros_like(dcw_ref)
            for k in range(3):
                dcw_ref[k : k + 1, :] = jnp.sum(acc_cw[k], axis=0, keepdims=True)
            for j in range(8):
                dws_ref[j] = jnp.where(tril, dws_ref[j], 0.0)
                dbs_ref[:, CHUNK * j : CHUNK * (j + 1)] = jnp.sum(acc_bs[j].T, axis=0, keepdims=True)

    whole = lambda shape: pl.BlockSpec(shape, lambda i: (0,) * len(shape))
    halo_rows = 16
    per = CHUNK // halo_rows
    return _call(
        body,
        name="mixer_bwd",
        operands=[proj, proj, dcat, conv_w, ln_g, ln_b, ws, bs_b],
        plans=plans,
        grid=(nt,),
        in_specs=[
            pl.BlockSpec((CHUNK, IN_DIM), lambda i: (nt - 1 - i, 0)),
            pl.BlockSpec((halo_rows, IN_DIM), lambda i: (jnp.maximum((nt - 1 - i) * per - 1, 0), 0)),
            pl.BlockSpec((CHUNK, MIX_DIM), lambda i: (nt - 1 - i, 0)),
            whole((8, D_MODEL)),
            whole((1, D_MODEL)),
            whole((1, D_MODEL)),
            whole((8, CHUNK, CHUNK)),
            whole((8, CHUNK, CHUNK)),
        ],
        out_specs=[
            pl.BlockSpec((CHUNK, IN_DIM), lambda i: (nt - 1 - i, 0)),
            whole((8, D_MODEL)),
            whole((1, D_MODEL)),
            whole((1, D_MODEL)),
            whole((8, CHUNK, CHUNK)),
            whole((1, D_MODEL)),
        ],
        out_shape=[
            jax.ShapeDtypeStruct((s, IN_DIM), BF16),
            jax.ShapeDtypeStruct((8, D_MODEL), F32),
            jax.ShapeDtypeStruct((1, D_MODEL), F32),
            jax.ShapeDtypeStruct((1, D_MODEL), F32),
            jax.ShapeDtypeStruct((8, CHUNK, CHUNK), F32),
            jax.ShapeDtypeStruct((1, D_MODEL), F32),
        ],
        scratch_shapes=[
            pltpu.VMEM((8, D_MODEL), F32),
            pltpu.VMEM((3, 8, D_MODEL), F32),
            pltpu.VMEM((8, D_MODEL), F32),
            pltpu.VMEM((8, D_MODEL), F32),
            pltpu.VMEM((8, CHUNK, CHUNK), F32),
            pltpu.VMEM((8, CHUNK, CHUNK), BF16),
            pltpu.VMEM((8, CHUNK, CHUNK), BF16),
        ],
    )


def _kv_fwd(mem, g_mem, w_kv):
    def body(mem_ref, g_ref, w_ref, k_ref, vv_ref, mt_ref):
        mh, _ = _rms(mem_ref[...])
        m = mh * g_ref[...]
        kv = _dot(m.astype(BF16), w_ref[...])
        k_ref[...] = kv[:, :D_MODEL].astype(BF16)
        vv_ref[...] = kv[:, D_MODEL:].astype(BF16)
        mt_ref[...] = m.T.astype(BF16)

    return pl.pallas_call(
        body,
        name="kv_fwd",
        out_shape=[
            jax.ShapeDtypeStruct((MEM_LEN, D_MODEL), BF16),
            jax.ShapeDtypeStruct((MEM_LEN, D_MODEL), BF16),
            jax.ShapeDtypeStruct((D_MODEL, MEM_LEN), BF16),
        ],
    )(mem, g_mem, w_kv)


def _kv_bwd(mem, w_kv, mt, dk, dvv, plans=()):
    def body(mem_ref, w_ref, mt_ref, dk_ref, dvv_ref, dw_ref, dwb_ref, dg_ref):
        dkv = jnp.concatenate([dk_ref[...], dvv_ref[...]], axis=1).astype(BF16)
        dw = _dot(mt_ref[...], dkv)
        dw_ref[...] = dw
        dwb_ref[...] = dw.astype(BF16)
        dm = _dot_nt(dkv, w_ref[...])
        mh, _ = _rms(mem_ref[...])
        dg_ref[...] = jnp.sum(dm * mh, axis=0, keepdims=True)

    return _call(
        body,
        name="kv_bwd",
        operands=[mem, w_kv, mt, dk, dvv],
        out_shape=[
            jax.ShapeDtypeStruct((D_MODEL, 2 * D_MODEL), F32),
            jax.ShapeDtypeStruct((D_MODEL, 2 * D_MODEL), BF16),
            jax.ShapeDtypeStruct((1, D_MODEL), F32),
        ],
        plans=plans,
    )


def _attn(x, cat, target, w_out, w_q, w_xo, k, vv, g_x, g_f):
    s = x.shape[0]
    tm = min(TM_ATTN, s)
    sub = min(SUB_ATTN, tm)
    nt = s // tm
    scale = 1.0 / math.sqrt(X_HEAD_DIM)

    def body(x_ref, cat_ref, t_ref, wout_ref, wq_ref, wxo_ref, k_ref, vv_ref, gx_ref, gf_ref,
             dcat_ref, actt_ref, cot_ref,
             dk_ref, dvv_ref, dgf_ref, dgx_ref, loss_ref,
             p_scr, q_scr, o_scr, dq_scr, acc_gf, acc_gx, acc_loss):
        i = pl.program_id(0)

        @pl.when(i == 0)
        def _():
            dk_ref[...] = jnp.zeros_like(dk_ref)
            dvv_ref[...] = jnp.zeros_like(dvv_ref)
            acc_gf[...] = jnp.zeros_like(acc_gf)
            acc_gx[...] = jnp.zeros_like(acc_gx)
            acc_loss[...] = jnp.zeros_like(acc_loss)

        subs = [slice(u * sub, (u + 1) * sub) for u in range(tm // sub)]
        heads = [slice(X_HEAD_DIM * h, X_HEAD_DIM * (h + 1)) for h in range(X_HEADS)]
        gx, gf = gx_ref[...], gf_ref[...]
        x1, x1h, r2 = [], [], []
        for rs in subs:
            cat = cat_ref[rs, :]
            actt_ref[0:MIX_DIM, rs] = cat.astype(F32).T.astype(BF16)
            v = x_ref[rs, :] + _dot(cat, wout_ref[...])
            vh, r = _rms(v)
            h2 = vh * gx
            actt_ref[MIX_DIM : MIX_DIM + D_MODEL, rs] = h2.T.astype(BF16)
            q_scr[rs, :] = _dot(h2.astype(BF16), wq_ref[...]).astype(BF16)
            x1.append(v), x1h.append(vh), r2.append(r)
        for rs in subs:
            for hs in heads:
                sc = _dot_nt(q_scr[rs, hs], k_ref[:, hs]) * scale
                e = jnp.exp(sc - jnp.max(sc, axis=-1, keepdims=True))
                p = e / jnp.sum(e, axis=-1, keepdims=True)
                p_scr[rs, hs] = p
                o_scr[rs, hs] = _dot(p.astype(BF16), vv_ref[:, hs])
        dx2, sq, d_gf = [], None, None
        for u, rs in enumerate(subs):
            o = o_scr[rs, :]
            actt_ref[MIX_DIM + D_MODEL :, rs] = o.T.astype(BF16)
            x2 = x1[u] + _dot(o.astype(BF16), wxo_ref[...])
            x2h, r3 = _rms(x2)
            err = x2h * gf - t_ref[rs, :]
            dy = err * (1.0 / D_MODEL)
            sq = _rows8(err * err) if sq is None else sq + _rows8(err * err)
            d_gf = _rows8(dy * x2h) if d_gf is None else d_gf + _rows8(dy * x2h)
            dx2.append(_rms_bwd(dy * gf, x2h, r3))
        acc_loss[...] += sq
        acc_gf[...] += d_gf
        d_vv, d_k = [None] * X_HEADS, [None] * X_HEADS
        for u, rs in enumerate(subs):
            dx2b = dx2[u].astype(BF16)
            cot_ref[2, rs, :] = dx2b
            dob = _dot_nt(dx2b, wxo_ref[...]).astype(BF16)
            for h, hs in enumerate(heads):
                p = p_scr[rs, hs]
                dp = _dot_nt(dob[:, hs], vv_ref[:, hs])
                ds = p * (dp - jnp.sum(dp * p, axis=-1, keepdims=True)) * scale
                dq_scr[rs, hs] = _dot(ds.astype(BF16), k_ref[:, hs]).astype(BF16)
                dvv_h = _dot(p.T.astype(BF16), dob[:, hs])
                dk_h = _dot(ds.T.astype(BF16), q_scr[rs, hs])
                d_vv[h] = dvv_h if d_vv[h] is None else d_vv[h] + dvv_h
                d_k[h] = dk_h if d_k[h] is None else d_k[h] + dk_h
        for h, hs in enumerate(heads):
            dvv_ref[:, hs] += d_vv[h]
            dk_ref[:, hs] += d_k[h]
        d_gx = None
        for u, rs in enumerate(subs):
            dqb = dq_scr[rs, :]
            cot_ref[1, rs, :] = dqb
            dh2 = _dot_nt(dqb, wq_ref[...])
            d_gx = _rows8(dh2 * x1h[u]) if d_gx is None else d_gx + _rows8(dh2 * x1h[u])
            dx1 = dx2[u] + _rms_bwd(dh2 * gx, x1h[u], r2[u])
            dx1b = dx1.astype(BF16)
            cot_ref[0, rs, :] = dx1b
            dcat_ref[rs, :] = _dot_nt(dx1b, wout_ref[...]).astype(BF16)
        acc_gx[...] += d_gx

        @pl.when(i == nt - 1)
        def _():
            dgf_ref[...] = jnp.sum(acc_gf[...], axis=0, keepdims=True)
            dgx_ref[...] = jnp.sum(acc_gx[...], axis=0, keepdims=True)
            loss_ref[...] = jnp.sum(acc_loss[...], axis=0, keepdims=True)

    whole = lambda shape: pl.BlockSpec(shape, lambda i: (0,) * len(shape))
    once = lambda shape: pl.BlockSpec(shape, lambda i: (0,) * len(shape), pipeline_mode=pl.Buffered(1))
    rows = lambda width: pl.BlockSpec((tm, width), lambda i: (i, 0))
    cols = lambda height: pl.BlockSpec((height, tm), lambda i: (0, i))
    vec = jax.ShapeDtypeStruct((1, D_MODEL), F32)
    return pl.pallas_call(
        body,
        name="attn",
        grid=(nt,),
        in_specs=[
            rows(D_MODEL), rows(MIX_DIM), rows(D_MODEL),
            once((MIX_DIM, D_MODEL)), once((D_MODEL, D_MODEL)), once((D_MODEL, D_MODEL)),
            once((MEM_LEN, D_MODEL)), once((MEM_LEN, D_MODEL)),
            whole((1, D_MODEL)), whole((1, D_MODEL)),
        ],
        out_specs=[
            rows(MIX_DIM), cols(2 * MIX_DIM),
            pl.BlockSpec((3, tm, D_MODEL), lambda i: (0, i, 0)),
            whole((MEM_LEN, D_MODEL)), whole((MEM_LEN, D_MODEL)),
            whole((1, D_MODEL)), whole((1, D_MODEL)), whole((1, D_MODEL)),
        ],
        out_shape=[
            jax.ShapeDtypeStruct((s, MIX_DIM), BF16),
            jax.ShapeDtypeStruct((2 * MIX_DIM, s), BF16),
            jax.ShapeDtypeStruct((3, s, D_MODEL), BF16),
            jax.ShapeDtypeStruct((MEM_LEN, D_MODEL), F32),
            jax.ShapeDtypeStruct((MEM_LEN, D_MODEL), F32),
            vec, vec, vec,
        ],
        scratch_shapes=[
            pltpu.VMEM((tm, D_MODEL), F32),
            pltpu.VMEM((tm, D_MODEL), BF16),
            pltpu.VMEM((tm, D_MODEL), F32),
            pltpu.VMEM((tm, D_MODEL), BF16),
            pltpu.VMEM((8, D_MODEL), F32),
            pltpu.VMEM((8, D_MODEL), F32),
            pltpu.VMEM((8, D_MODEL), F32),
        ],
        compiler_params=pltpu.CompilerParams(dimension_semantics=("arbitrary",)),
    )(x, cat, target, w_out, w_q, w_xo, k, vv, g_x, g_f)


def _matmul_f32(name, a, b, rows=None, plans=()):
    first, m = (0, a.shape[0]) if rows is None else rows
    kk = a.shape[1]
    n = b.shape[1]
    tm, tn, tk = min(m, TM_DW), min(n, D_MODEL), min(kk, TK_DW)
    nk = kk // tk
    i0 = first // tm

    def body(a_ref, b_ref, o_ref, ob_ref):
        prod = _dot(a_ref[...], b_ref[...])
        if nk == 1:
            o_ref[...] = prod
            ob_ref[...] = prod.astype(BF16)
            return
        kid = pl.program_id(2)

        @pl.when(kid == 0)
        def _():
            o_ref[...] = prod

        @pl.when(kid > 0)
        def _():
            o_ref[...] += prod

        @pl.when(kid == nk - 1)
        def _():
            ob_ref[...] = o_ref[...].astype(BF16)

    out_spec = pl.BlockSpec((tm, tn), lambda i, j, k: (i, j))
    outs, plan_outs = _call(
        body,
        name=name,
        operands=[a, b],
        grid=(m // tm, n // tn, nk),
        in_specs=[pl.BlockSpec((tm, tk), lambda i, j, k: (i + i0, k)), pl.BlockSpec((tk, tn), lambda i, j, k: (k, j))],
        out_specs=[out_spec, out_spec],
        out_shape=[jax.ShapeDtypeStruct((m, n), F32), jax.ShapeDtypeStruct((m, n), BF16)],
        plans=plans,
    )
    return outs, plan_outs


def _dw_attn(actt, cot, plans=()):
    s = actt.shape[1]
    tm = TM_DW
    counts = [MIX_DIM // tm, D_MODEL // tm, D_MODEL // tm]
    starts = [0, counts[0], counts[0] + counts[1]]
    which = lambda i: jnp.where(i < starts[1], 0, jnp.where(i < starts[2], 1, 2))

    def body(a_ref, b_ref, *outs):
        i = pl.program_id(0)
        prod = _dot(a_ref[...], b_ref[0])
        for m in range(3):
            @pl.when(which(i) == m)
            def _():
                outs[2 * m][...] = prod
                outs[2 * m + 1][...] = prod.astype(BF16)

    out_specs, out_shape = [], []
    for m in range(3):
        spec = pl.BlockSpec((tm, D_MODEL), lambda i, m=m: (jnp.clip(i - starts[m], 0, counts[m] - 1), 0))
        out_specs += [spec, spec]
        out_shape += [jax.ShapeDtypeStruct((counts[m] * tm, D_MODEL), F32), jax.ShapeDtypeStruct((counts[m] * tm, D_MODEL), BF16)]
    outs, plan_outs = _call(
        body,
        name="dw_attn",
        operands=[actt, cot],
        grid=(sum(counts),),
        in_specs=[pl.BlockSpec((tm, s), lambda i: (i, 0)), pl.BlockSpec((1, s, D_MODEL), lambda i: (which(i), 0, 0))],
        out_specs=out_specs,
        out_shape=out_shape,
        plans=plans,
    )
    return [outs[2 * m : 2 * m + 2] for m in range(3)], plan_outs


def _dh1(name, dproj, w_in, x, dx1, g, tiles, into=None, after=(), plans=()):
    s = x.shape[0]
    tm = min(TM_DH1, s)
    t0, nt = tiles[0], tiles[1] - tiles[0]

    def body(dp_ref, w_ref, x_ref, dx1_ref, g_ref, *rest):
        gx_ref, dg_ref, acc_g = rest[-3:]
        i = pl.program_id(0)

        @pl.when(i == 0)
        def _():
            acc_g[...] = jnp.zeros_like(acc_g)

        dh1 = _dot_nt(dp_ref[...], w_ref[...])
        xh, r = _rms(x_ref[...])
        acc_g[...] += _rows8(dh1 * xh)
        gx_ref[...] = dx1_ref[0].astype(F32) + _rms_bwd(dh1 * g_ref[...], xh, r)

        @pl.when(i == nt - 1)
        def _():
            dg_ref[...] = jnp.sum(acc_g[...], axis=0, keepdims=True)

    rows = pl.BlockSpec((tm, D_MODEL), lambda i: (i + t0, 0))
    return _call(
        body,
        name=name,
        operands=[dproj, w_in, x, dx1, g] + ([into] if into is not None else []) + list(after),
        grid=(nt,),
        in_specs=[
            pl.BlockSpec((tm, IN_DIM), lambda i: (i + t0, 0)),
            pl.BlockSpec((D_MODEL, IN_DIM), lambda i: (0, 0), pipeline_mode=pl.Buffered(1)),
            rows,
            pl.BlockSpec((1, tm, D_MODEL), lambda i: (0, i + t0, 0)),
            pl.BlockSpec((1, D_MODEL), lambda i: (0, 0)),
        ] + [pl.BlockSpec(memory_space=pl.ANY)] * ((into is not None) + len(after)),
        out_specs=[rows, pl.BlockSpec((1, D_MODEL), lambda i: (0, 0))],
        out_shape=[jax.ShapeDtypeStruct((s, D_MODEL), F32), jax.ShapeDtypeStruct((1, D_MODEL), F32)],
        scratch_shapes=[pltpu.VMEM((8, D_MODEL), F32)],
        plans=plans,
        aliases={5: 0} if into is not None else None,
    )


def _pair_sum(name, grad, got, axis, size, core, plans=()):
    shard = list(grad.shape)
    shard[axis] = size
    r, cdim = shard
    tr = min(r, TR_PAIR_SUM)
    nr = r // tr

    def body(core_ref, g_ref, got_ref, o_ref):
        o_ref[0] = (g_ref[...] + got_ref[0].astype(F32)).astype(BF16)

    if axis == 1:
        g_map = lambda q, t, core_ref: (t, 2 * q + core_ref[0])
    else:
        g_map = lambda q, t, core_ref: ((2 * q + core_ref[0]) * nr + t, 0)
    (out,), plan_outs = _call(
        body,
        name=name,
        prefetch=[core],
        operands=[grad, got],
        grid=(N_CHIP, nr),
        in_specs=[
            pl.BlockSpec((tr, cdim), g_map),
            pl.BlockSpec((1, tr, cdim), lambda q, t, core_ref: (q, t, 0)),
        ],
        out_specs=[pl.BlockSpec((1, tr, cdim), lambda q, t, core_ref: (q, t, 0))],
        out_shape=[jax.ShapeDtypeStruct((N_CHIP, r, cdim), BF16)],
        plans=plans,
    )
    return (out, plan_outs) if plans else out


def _adam_step(w, m, v, g):
    c1 = 1.0 / (1.0 - ADAM_B1**ADAM_STEP)
    c2 = 1.0 / (1.0 - ADAM_B2**ADAM_STEP)
    m_new = ADAM_B1 * m + (1.0 - ADAM_B1) * g
    v_new = ADAM_B2 * v + (1.0 - ADAM_B2) * (g * g)
    return -ADAM_LR * ((m_new * c1) / (jnp.sqrt(v_new * c2) + ADAM_EPS) + ADAM_WD * w), m_new, v_new


def _update_small(dev, vec_w, vec_m, vec_v, conv_w, conv_m, conv_v, small, landed):
    n_vec = len(vec_w)

    def body(dev_ref, *refs):
        refs = list(refs)
        take = lambda k: [refs.pop(0) for _ in range(k)]
        w_refs, m_refs, v_refs = take(n_vec), take(n_vec), take(n_vec)
        cw_ref, cm_ref, cv_ref, small_ref, land_ref, small_conv_ref, land_conv_ref = take(7)
        vec_outs, conv_outs, (loss_ref,) = take(4 * n_vec), take(4), take(1)

        def total(own_ref, others_ref):
            acc = None
            for d in range(N_DEV):
                part = jnp.where(dev_ref[0] == d, own_ref[0], others_ref[d])
                acc = part if acc is None else acc + part
            return acc

        g_all = total(small_ref, land_ref)
        for i in range(n_vec):
            g = g_all[i : i + 1]
            delta, m_new, v_new = _adam_step(w_refs[i][...], m_refs[i][...], v_refs[i][...], g)
            for ref, val in zip(vec_outs[4 * i : 4 * i + 4], (g, delta, m_new, v_new)):
                ref[...] = val
        g = total(small_conv_ref, land_conv_ref)[0:3]
        delta, m_new, v_new = _adam_step(cw_ref[0], cm_ref[0], cv_ref[0], g)
        for ref, val in zip(conv_outs, (g, delta, m_new, v_new)):
            ref[0] = val
        loss_ref[...] = jnp.broadcast_to(0.5 * jnp.sum(g_all[7:8], axis=-1, keepdims=True) / D_MODEL, loss_ref.shape)

    vec_spec = pl.BlockSpec((1, D_MODEL), lambda i, dev_ref: (0, 0))
    conv_spec = pl.BlockSpec((1, 3, CHUNK), lambda i, dev_ref: (0, 0, 0))
    vec_shape = jax.ShapeDtypeStruct((1, D_MODEL), F32)
    conv_shape = jax.ShapeDtypeStruct((1, 3, CHUNK), F32)
    outs, _ = _call(
        body,
        name="update_small",
        prefetch=[dev],
        operands=list(vec_w) + list(vec_m) + list(vec_v) + [conv_w, conv_m, conv_v, small, landed, small, landed],
        grid=(1,),
        in_specs=[vec_spec] * (3 * n_vec) + [conv_spec] * 3 + [
            pl.BlockSpec((1, 16, D_MODEL), lambda i, dev_ref: (0, 0, 0)),
            pl.BlockSpec((N_DEV, 16, D_MODEL), lambda i, dev_ref: (0, 0, 0)),
            pl.BlockSpec((1, 8, CHUNK), lambda i, dev_ref: (0, 1, dev_ref[0])),
            pl.BlockSpec((N_DEV, 8, CHUNK), lambda i, dev_ref: (0, 1, dev_ref[0])),
        ],
        out_specs=[vec_spec] * (4 * n_vec) + [conv_spec] * 4 + [pl.BlockSpec((1, CHUNK), lambda i, dev_ref: (0, 0))],
        out_shape=[vec_shape] * (4 * n_vec) + [conv_shape] * 4 + [jax.ShapeDtypeStruct((1, CHUNK), F32)],
    )
    return [outs[4 * i : 4 * i + 4] for i in range(n_vec)], outs[4 * n_vec : 4 * n_vec + 4], outs[-1]


def _adamw(name, groups, steps, own_slot=None, plans=()):
    lists = lambda p: list(p) if isinstance(p, (list, tuple)) else [p]
    groups = [(g[0], g[1], g[2], lists(g[3]), lists(g[4]) if len(g) > 4 else []) for g in groups]
    counts = [3 + len(g[3]) + len(g[4]) for g in groups]
    use_slot = own_slot is not None

    def first_tiles(w, parts):
        firsts, first = [], 0
        for p in parts:
            firsts.append(first)
            first += p.shape[1] // (w.shape[0] // steps)
        return firsts

    def update(t, slot_ref, firsts, w_ref, m_ref, v_ref, p_refs, o_refs, g_ref, d_ref, nm_ref, nv_ref):
        g = None
        for k, p_ref in enumerate(p_refs):
            gk = None
            for q in range(p_ref.shape[0]):
                part = p_ref[q].astype(F32)
                if o_refs:
                    part = jnp.where(slot_ref[0] == q, o_refs[k][0].astype(F32), part)
                gk = part if gk is None else gk + part
            g = gk if g is None else jnp.where(t >= firsts[k], gk, g)
        g_ref[...] = g
        d_ref[...], nm_ref[...], nv_ref[...] = _adam_step(w_ref[...], m_ref[...], v_ref[...], g)

    def body(*refs):
        refs = list(refs)
        slot_ref = refs.pop(0) if use_slot else None
        t = pl.program_id(0)
        ins, outs = refs[: sum(counts)], refs[sum(counts) :]
        for i, (w, _, _, parts, own) in enumerate(groups):
            mine = ins[sum(counts[:i]) : sum(counts[: i + 1])]
            update(t, slot_ref, first_tiles(w, parts), *mine[:3], mine[3 : 3 + len(parts)], mine[3 + len(parts) :],
                   *outs[4 * i : 4 * i + 4])

    operands, in_specs, out_specs, out_shape = [], [], [], []
    for w, m, v, parts, own in groups:
        r, cdim = w.shape
        tr = r // steps
        firsts = first_tiles(w, parts)

        def tile_of(k, t, firsts=firsts, parts=parts, tr=tr):
            return jnp.clip(t - firsts[k], 0, parts[k].shape[1] // tr - 1)

        blk = pl.BlockSpec((tr, cdim), lambda t, *_: (t, 0))
        operands += [w, m, v] + parts + own
        in_specs += [blk, blk, blk]
        in_specs += [pl.BlockSpec((p.shape[0], tr, cdim), lambda t, *_, k=k, tile_of=tile_of: (0, tile_of(k, t), 0)) for k, p in enumerate(parts)]
        in_specs += [pl.BlockSpec((1, tr, cdim), lambda t, slot_ref, k=k, tile_of=tile_of: (slot_ref[0], tile_of(k, t), 0)) for k in range(len(own))]
        out_specs += [blk] * 4
        out_shape += [jax.ShapeDtypeStruct((r, cdim), F32)] * 4
    outs, plan_outs = _call(
        body,
        name=name,
        prefetch=[own_slot] if use_slot else [],
        operands=operands,
        grid=(steps,),
        in_specs=in_specs,
        out_specs=out_specs,
        out_shape=out_shape,
        plans=plans,
    )
    return [outs[4 * i : 4 * i + 4] for i in range(len(groups))], plan_outs


def _chip_routes(n):
    x, y, c = _place()
    my_chip = 2 * x + y
    return [(a, (*chip, c), 2 * chip[0] + chip[1], my_chip, 2 * chip[0] + chip[1])
            for a in range(n) for chip in _other_chips(x, y)]


def _gather_routes(n):
    x, y, c = _place()
    flip = lambda v, f: 1 - v if f else v
    peers = [(flip(x, fx), flip(y, fy), flip(c, fc)) for fx in (0, 1) for fy in (0, 1) for fc in (0, 1) if fx + fy + fc]
    return [(0, p, 0, 4 * x + 2 * y + c, 4 * p[0] + 2 * p[1] + p[2]) for p in peers]


_chip_routes.copies_per_array = N_CHIP - 1
_gather_routes.copies_per_array = N_DEV - 1


def _split_start(name, routes, srcs, land_shapes):
    n = len(srcs)
    hbm = pl.BlockSpec(memory_space=pltpu.HBM)
    sem = pl.BlockSpec(memory_space=pltpu.SEMAPHORE)

    def body(*refs):
        src_refs, land_refs = refs[:n], refs[n : 2 * n]
        send_sems, recv_sems = refs[2 * n], refs[2 * n + 1]
        token = refs[-1]
        for k, (a, peer, src_slot, there, _) in enumerate(routes(n)):
            pltpu.make_async_remote_copy(
                src_ref=src_refs[a].at[src_slot], dst_ref=land_refs[a].at[there],
                send_sem=send_sems.at[k], recv_sem=recv_sems.at[k], device_id=peer, device_id_type=MESH).start()
        token[...] = jnp.zeros_like(token)

    in_hbm = [pltpu.with_memory_space_constraint(p, pltpu.HBM) for p in srcs]
    in_hbm += [pltpu.with_memory_space_constraint(lax.empty(shape, p.dtype), pltpu.HBM) for p, shape in zip(srcs, land_shapes)]
    n_copies = routes.copies_per_array * n
    res = pl.pallas_call(
        body,
        name=name,
        out_shape=[pltpu.SemaphoreType.DMA((n_copies,)), pltpu.SemaphoreType.DMA((n_copies,))]
        + [pltpu.HBM(p.shape, p.dtype) for p in srcs] + [pltpu.HBM(shape, p.dtype) for p, shape in zip(srcs, land_shapes)]
        + [jax.ShapeDtypeStruct((8, CHUNK), F32)],
        in_specs=[hbm] * (2 * n),
        out_specs=[sem, sem] + [hbm] * (2 * n) + [pl.BlockSpec(memory_space=pltpu.VMEM)],
        input_output_aliases={i: 2 + i for i in range(2 * n)},
        compiler_params=pltpu.CompilerParams(has_side_effects=pltpu.SideEffectType.DATAFLOW_SIDE_EFFECTING),
    )(*in_hbm)
    return res[:-1], res[-1]


def _split_wait(name, routes, in_flight, after):
    n = (len(in_flight) - 2) // 2
    hbm = pl.BlockSpec(memory_space=pltpu.HBM)
    sem = pl.BlockSpec(memory_space=pltpu.SEMAPHORE)
    send_sems, recv_sems, *bufs = in_flight

    def body(*refs):
        src_refs, land_refs = refs[:n], refs[n : 2 * n]
        send_ref, recv_ref = refs[2 * n], refs[2 * n + 1]
        for k, (a, peer, src_slot, _, here) in enumerate(routes(n)):
            cp = pltpu.make_async_remote_copy(
                src_ref=src_refs[a].at[src_slot], dst_ref=land_refs[a].at[here],
                send_sem=send_ref.at[k], recv_sem=recv_ref.at[k], device_id=peer, device_id_type=MESH)
            cp.wait_send()
            cp.wait_recv()

    res = pl.pallas_call(
        body,
        name=name,
        out_shape=[pltpu.HBM(b.shape, b.dtype) for b in bufs],
        in_specs=[hbm] * (2 * n) + [sem, sem] + [pl.BlockSpec(memory_space=pl.ANY)] * len(after),
        out_specs=[hbm] * (2 * n),
        input_output_aliases={i: i for i in range(2 * n)},
        compiler_params=pltpu.CompilerParams(has_side_effects=pltpu.SideEffectType.DATAFLOW_SIDE_EFFECTING),
    )(*bufs, send_sems, recv_sems, *after)
    return res[:n], res[n:]


def kernel(x, mem, norm_mix_g, w_in, conv_w, gm_ln_g, gm_ln_b, gm_ws, gm_bs, w_out, norm_x_g, norm_mem_g, w_q, w_kv, w_xo, norm_final_g, loss_target, m_norm_mix_g, m_w_in, m_conv_w, m_gm_ln_g, m_gm_ln_b, m_gm_ws, m_gm_bs, m_w_out, m_norm_x_g, m_norm_mem_g, m_w_q, m_w_kv, m_w_xo, m_norm_final_g, v_norm_mix_g, v_w_in, v_conv_w, v_gm_ln_g, v_gm_ln_b, v_gm_ws, v_gm_bs, v_w_out, v_norm_x_g, v_norm_mem_g, v_w_q, v_w_kv, v_w_xo, v_norm_final_g):
    s = x.shape[1]
    dev = 4 * lax.axis_index("x") + 2 * lax.axis_index("y") + lax.axis_index("c")
    core = lax.axis_index("c").astype(jnp.int32).reshape(1)
    x2 = x[0]
    target = loss_target[0]
    pad_taps = lambda t: jnp.pad(t[0], ((0, 5), (0, 0)))

    big_names = ["w_in", "w_out", "w_q", "w_kv", "w_xo"]
    big_w = dict(w_in=w_in[0], w_out=w_out[0], w_q=w_q[0], w_kv=w_kv[0], w_xo=w_xo[0])
    big_m = dict(w_in=m_w_in[0], w_out=m_w_out[0], w_q=m_w_q[0], w_kv=m_w_kv[0], w_xo=m_w_xo[0])
    big_v = dict(w_in=v_w_in[0], w_out=v_w_out[0], w_q=v_w_q[0], w_kv=v_w_kv[0], w_xo=v_w_xo[0])
    shard_axis = dict(w_in=1, w_out=0, w_q=0, w_kv=1, w_xo=0)
    full_shape = dict(w_in=(D_MODEL, IN_DIM), w_out=(MIX_DIM, D_MODEL), w_q=(D_MODEL, D_MODEL),
                      w_kv=(D_MODEL, 2 * D_MODEL), w_xo=(D_MODEL, D_MODEL))
    shard_size = {k: big_w[k].shape[shard_axis[k]] for k in big_names}
    others = big_names[1:]
    gather_spec = lambda k: (full_shape[k], shard_axis[k], shard_size[k])
    reduce_spec = lambda k: (shard_axis[k], shard_size[k])
    ws = gm_ws[0]
    bs_b = jnp.broadcast_to(gm_bs[0][:, :, None], (8, CHUNK, CHUNK))
    g_f = norm_final_g.reshape(1, D_MODEL)

    px, py = lax.axis_index("x"), lax.axis_index("y")
    chip_order = jnp.stack([2 * px + py, 2 * (1 - px) + py, 2 * px + 1 - py, 2 * (1 - px) + 1 - py]).astype(jnp.int32)
    (proj, h1, h1t, w_in_full), ((w_out_full, conv_g),) = _inproj(
        x2, norm_mix_g, big_w["w_in"].astype(BF16), chip_order,
        plans=[_gather_plan([big_w["w_out"].astype(BF16), pad_taps(conv_w)[None]], [gather_spec("w_out"), ((N_DEV, 8, CHUNK), 0, 1)])])
    conv_full = conv_g.transpose(1, 0, 2).reshape(8, D_MODEL)
    attn_w = ["w_q", "w_kv", "w_xo"]
    cat, (gathered,) = _mixer_fwd(
        proj, conv_full, gm_ln_g, gm_ln_b, ws, bs_b,
        plans=[_gather_plan([big_w[k].astype(BF16) for k in attn_w], [gather_spec(k) for k in attn_w])])
    wf = dict(zip(attn_w, gathered), w_out=w_out_full)
    k, vv, mt = _kv_fwd(mem[0], norm_mem_g, wf["w_kv"])
    (dcat, actt, cot, dk, dvv, d_gf, d_gx, sq_err) = _attn(
        x2, cat, target, wf["w_out"], wf["w_q"], wf["w_xo"], k, vv, norm_x_g, g_f)

    def pair_sums(names, grads, got):
        return [_pair_sum("pair_sum_" + k, grads[k], got[i], *reduce_spec(k), core) for i, k in enumerate(names)]

    sibling_plan = lambda names, grads: _sibling_plan([grads[k] for k in names], [reduce_spec(k) for k in names])
    early = ["w_kv", "w_out", "w_q", "w_xo"]
    big_g, big_gb = {}, {}
    dws, _ = _dw_attn(actt, cot)
    for name, (f32, bf16) in zip(early[1:], dws):
        big_g[name], big_gb[name] = f32, bf16
    (big_g["w_kv"], big_gb["w_kv"], d_gmem), (got_rest,) = _kv_bwd(
        mem[0], wf["w_kv"], mt, dk, dvv, plans=[sibling_plan(early[1:], big_gb)])
    pair_out, (got_kv,) = _pair_sum("pair_sum_w_out", big_g["w_out"], got_rest[0], *reduce_spec("w_out"), core,
                                    plans=[sibling_plan(early[:1], big_gb)])
    pair_kv = _pair_sum("pair_sum_w_kv", big_g["w_kv"], got_kv[0], *reduce_spec("w_kv"), core)
    (dproj, d_cw, d_lng, d_lnb, d_ws, d_bs), (parts_early,) = _mixer_bwd(
        proj, dcat, conv_full, gm_ln_g, gm_ln_b, ws, bs_b,
        plans=[_chip_plan([pair_kv, pair_out] + pair_sums(early[2:], big_g, got_rest[1:]))])
    half = D_MODEL // 2
    in_spec = reduce_spec("w_in")
    (dw_top, dwb_top), _ = _matmul_f32("dw_in_top", h1t, dproj, rows=(0, half))
    (dw_bot, dwb_bot), (got_top, (ws_all,)) = _matmul_f32(
        "dw_in_bot", h1t, dproj, rows=(half, half),
        plans=[_sibling_plan([dwb_top], [in_spec]),
               _gather_plan([d_ws.reshape(1, 8 * CHUNK, CHUNK).astype(BF16)], [((N_DEV, 8 * CHUNK, CHUNK), 0, 1)])])
    pair_top, ((got_bot,),) = _pair_sum("pair_sum_w_in_top", dw_top, got_top[0], *in_spec, core,
                                        plans=[_sibling_plan([dwb_bot], [in_spec])])
    pair_bot = _pair_sum("pair_sum_w_in_bot", dw_bot, got_bot, *in_spec, core)
    chip_shapes = [p.shape for p in (pair_top, pair_bot)]
    chips_in_flight, token = _split_start("grad_chip_exchange_start", _chip_routes, [pair_top, pair_bot], chip_shapes)
    n_tiles = s // min(TM_DH1, s)
    (grad_x, d_gmix), _ = _dh1("dh1", dproj, w_in_full, x2, cot, norm_mix_g, (0, n_tiles), after=[token])
    vec_names = ["norm_mix_g", "gm_ln_g", "gm_ln_b", "gm_bs", "norm_x_g", "norm_mem_g", "norm_final_g"]
    vec_g = [d_gmix, d_lng, d_lnb, d_bs, d_gx, d_gmem, d_gf]
    small = jnp.concatenate(vec_g + [sq_err, d_cw], axis=0)[None]
    small_in_flight, _ = _split_start("small_all_gather_start", _gather_routes, [small], [(N_DEV, 16, D_MODEL)])
    flat_ws = lambda t: t.reshape(8 * CHUNK, CHUNK)
    updated, _ = _adamw(
        "adamw_early",
        [(big_w[k], big_m[k], big_v[k], got_k) for k, got_k in zip(early, parts_early)]
        + [(flat_ws(ws), flat_ws(m_gm_ws), flat_ws(v_gm_ws), ws_all)], steps=STEPS_ADAMW // 2)
    big_out = dict(zip(early, updated[:4]))
    ws_out = updated[4]
    (pair_top, pair_bot), (land_top, land_bot) = _split_wait(
        "grad_chip_exchange_wait", _chip_routes, chips_in_flight, [d_gmix] + [u[0] for u in updated])
    my_chip = (2 * px + py).astype(jnp.int32).reshape(1)
    (big_out["w_in"],), _ = _adamw(
        "adamw_w_in", [(big_w["w_in"], big_m["w_in"], big_v["w_in"], [land_top, land_bot], [pair_top, pair_bot])],
        steps=STEPS_ADAMW, own_slot=my_chip)
    (small,), (small_all,) = _split_wait("small_all_gather_wait", _gather_routes, small_in_flight, [big_out["w_in"][0]])
    row = lambda t: t.reshape(1, D_MODEL)
    vec_out, conv_out, loss_row = _update_small(
        dev.astype(jnp.int32).reshape(1),
        [norm_mix_g, gm_ln_g, gm_ln_b, row(gm_bs), norm_x_g, norm_mem_g, row(norm_final_g)],
        [m_norm_mix_g, m_gm_ln_g, m_gm_ln_b, row(m_gm_bs), m_norm_x_g, m_norm_mem_g, row(m_norm_final_g)],
        [v_norm_mix_g, v_gm_ln_g, v_gm_ln_b, row(v_gm_bs), v_norm_x_g, v_norm_mem_g, row(v_norm_final_g)],
        conv_w, m_conv_w, v_conv_w, small, small_all)
    loss = loss_row[0, 0]

    def result(name, which):
        if name in big_out:
            return big_out[name][which][None]
        if name == "conv_w":
            return conv_out[which]
        if name == "gm_ws":
            return ws_out[which].reshape(1, 8, CHUNK, CHUNK)
        r = vec_out[vec_names.index(name)][which]
        if name == "gm_bs":
            return r.reshape(1, 8, CHUNK)
        if name == "norm_final_g":
            return r.reshape(D_MODEL)
        return r

    weights = ["norm_mix_g", "w_in", "conv_w", "gm_ln_g", "gm_ln_b", "gm_ws", "gm_bs", "w_out", "norm_x_g", "norm_mem_g", "w_q", "w_kv", "w_xo", "norm_final_g"]
    outs = [loss, grad_x[None]]
    for which in range(4):
        outs += [result(name, which) for name in weights]
    return tuple(outs)
```

```python
import math

import jax
import jax.numpy as jnp
from jax import lax
from jax.experimental import pallas as pl
from jax.experimental.pallas import tpu as pltpu

F32 = jnp.float32
BF16 = jnp.bfloat16
MESH = pl.DeviceIdType.MESH

D_MODEL = 1024
N_DEV = 8
N_CHIP = 4
CHUNK = 128
GROUPS = 7
IN_DIM = GROUPS * D_MODEL
SHARD_IN = IN_DIM // N_DEV
MIX_DIM = 2 * D_MODEL
X_HEADS = 4
X_HEAD_DIM = D_MODEL // X_HEADS
MEM_LEN = 256
EPS = 1e-6
GELU_K0 = math.sqrt(2.0 / math.pi)
GELU_K1 = 0.044715

ADAM_LR = 0.001
ADAM_B1 = 0.9
ADAM_B2 = 0.999
ADAM_EPS = 1e-08
ADAM_WD = 0.01
ADAM_STEP = 10

TM_INPROJ = 512
TM_ATTN = 512
SUB_ATTN = 256
TM_DH1 = 512
TM_DW = 512
TK_DW = 4096
TR_PAIR_SUM = 1024
SIBLING_BARRIER_ID = 1
STEPS_ADAMW = 4


def _dot(a, b):
    return lax.dot_general(a, b, (((1,), (0,)), ((), ())), preferred_element_type=F32)


def _dot_nt(a, b):
    return lax.dot_general(a, b, (((1,), (1,)), ((), ())), preferred_element_type=F32)


def _rows8(v):
    return jnp.sum(v.reshape(v.shape[0] // 8, 8, v.shape[1]), axis=0)


def _gelu(x, with_grad=True):
    x2 = x * x
    t = jnp.tanh(x * (GELU_K0 + (GELU_K0 * GELU_K1) * x2))
    half = 0.5 + 0.5 * t
    g = x * half
    if not with_grad:
        return g
    return g, half + ((0.5 * x) * (1.0 - t * t)) * (GELU_K0 + (3.0 * GELU_K0 * GELU_K1) * x2)


def _silu(z, with_grad=True):
    s = jax.nn.sigmoid(z)
    zs = z * s
    return (zs, s + zs * (1.0 - s)) if with_grad else zs


def _rms(v):
    r = lax.rsqrt(jnp.mean(v * v, axis=-1, keepdims=True) + EPS)
    return v * r, r


def _rms_bwd(dy_g, vh, r):
    return r * (dy_g - vh * jnp.mean(dy_g * vh, axis=-1, keepdims=True))


def _place():
    return lax.axis_index("x"), lax.axis_index("y"), lax.axis_index("c")


def _other_chips(x, y):
    return [(1 - x, y), (x, 1 - y), (1 - x, 1 - y)]


def _pow2_divisor(n):
    return n & (-n)


def _shard_view(ref, axis, size, d):
    start = d * size
    align = _pow2_divisor(size)
    if align > 1:
        start = pl.multiple_of(start, align)
    idx = [slice(None)] * len(ref.shape)
    idx[axis] = pl.ds(start, size)
    return ref.at[tuple(idx)]


class _Plan:
    def __init__(self, ins, out_shape, sems, start, finish, mid=None, aliases=None, sibling_only=False):
        self.ins, self.out_shape, self.sems = list(ins), list(out_shape), list(sems)
        self.start, self.mid, self.finish = start, mid, finish
        self.aliases = dict(aliases or {})
        self.sibling_only = sibling_only


def _gather_plan(shards, specs):
    n = len(shards)
    n_copy = 7

    def copies(ins, outs, sems):
        send_sems, recv_sems, local_sems = sems
        x, y, c = _place()
        me, sibling = (x, y, c), (x, y, 1 - c)
        chips = _other_chips(x, y)

        def block(a, p):
            _, axis, size = specs[a]
            return _shard_view(outs[a], axis, size, 4 * p[0] + 2 * p[1] + p[2])

        def copy(a, k, p, to, src=None):
            return pltpu.make_async_remote_copy(
                src_ref=block(a, p) if src is None else src,
                dst_ref=block(a, p),
                send_sem=send_sems.at[a * n_copy + k],
                recv_sem=recv_sems.at[a * n_copy + k],
                device_id=to,
                device_id_type=MESH,
            )

        def mine():
            return [pltpu.make_async_copy(ins[a], block(a, me), local_sems.at[a]) for a in range(n)]

        def first():
            return [cp for a in range(n) for cp in
                    [copy(a, 0, me, sibling, src=ins[a])] + [copy(a, 1 + j, me, (*chip, c), src=ins[a]) for j, chip in enumerate(chips)]]

        def landed():
            return [copy(a, 1 + j, (*chip, c), me) for j, chip in enumerate(chips) for a in range(n)]

        def passed():
            return [copy(a, 4 + j, (*chip, c), sibling) for j, chip in enumerate(chips) for a in range(n)]

        def last():
            return [cp for a in range(n) for cp in
                    [copy(a, 0, sibling, me)] + [copy(a, 4 + j, (*chip, 1 - c), me) for j, chip in enumerate(chips)]]

        return mine, first, landed, passed, last

    def start(ins, outs, sems):
        mine, first, _, _, _ = copies(ins, outs, sems)
        for cp in mine() + first():
            cp.start()

    def mid(ins, outs, sems):
        _, _, landed, passed, _ = copies(ins, outs, sems)
        for got, fwd in zip(landed(), passed()):
            got.wait_recv()
            fwd.start()

    def finish(ins, outs, sems):
        mine, first, _, passed, last = copies(ins, outs, sems)
        for cp in last():
            cp.wait_recv()
        for cp in first() + passed():
            cp.wait_send()
        for cp in mine():
            cp.wait()

    return _Plan(
        shards,
        [jax.ShapeDtypeStruct(full, s.dtype) for s, (full, _, _) in zip(shards, specs)],
        [pltpu.SemaphoreType.DMA((n * n_copy,)), pltpu.SemaphoreType.DMA((n * n_copy,)), pltpu.SemaphoreType.DMA((n,))],
        start, finish, mid,
    )


def _sibling_plan(grads, specs):
    n = len(grads)

    def shard_shape(a):
        axis, size = specs[a]
        shp = list(grads[a].shape)
        shp[axis] = size
        return tuple(shp)

    def copies(ins, outs, sems):
        send_sems, recv_sems = sems
        x, y, c = _place()
        return [
            pltpu.make_async_remote_copy(
                src_ref=_shard_view(ins[a], specs[a][0], specs[a][1], 2 * q + (1 - c)),
                dst_ref=outs[a].at[q],
                send_sem=send_sems.at[a * N_CHIP + q],
                recv_sem=recv_sems.at[a * N_CHIP + q],
                device_id=(x, y, 1 - c),
                device_id_type=MESH,
            )
            for a in range(n) for q in range(N_CHIP)
        ]

    def start(ins, outs, sems):
        for cp in copies(ins, outs, sems):
            cp.start()

    def finish(ins, outs, sems):
        for cp in copies(ins, outs, sems):
            cp.wait()

    return _Plan(
        grads,
        [jax.ShapeDtypeStruct((N_CHIP, *shard_shape(a)), grads[a].dtype) for a in range(n)],
        [pltpu.SemaphoreType.DMA((n * N_CHIP,)), pltpu.SemaphoreType.DMA((n * N_CHIP,))],
        start, finish, sibling_only=True,
    )


def _chip_plan(parts, full_rows=None, row0=None, into=None):
    n = len(parts)
    full_rows = [p.shape[1] for p in parts] if full_rows is None else full_rows
    row0 = [0] * n if row0 is None else row0

    def copies(ins, outs, sems):
        send_sems, recv_sems, local_sems = sems
        x, y, c = _place()
        my_chip = 2 * x + y
        chips = _other_chips(x, y)

        def land(a, q):
            return outs[a].at[q, pl.ds(row0[a], parts[a].shape[1]), :]

        def mine():
            return [pltpu.make_async_copy(ins[a].at[my_chip], land(a, my_chip), local_sems.at[a]) for a in range(n)]

        def remote(sending):
            return [
                pltpu.make_async_remote_copy(
                    src_ref=ins[a].at[2 * chip[0] + chip[1]],
                    dst_ref=land(a, my_chip if sending else 2 * chip[0] + chip[1]),
                    send_sem=send_sems.at[a * 3 + j], recv_sem=recv_sems.at[a * 3 + j],
                    device_id=(*chip, c), device_id_type=MESH)
                for a in range(n) for j, chip in enumerate(chips)
            ]

        return mine, remote

    def start(ins, outs, sems):
        mine, remote = copies(ins, outs, sems)
        for cp in mine() + remote(True):
            cp.start()

    def finish(ins, outs, sems):
        mine, remote = copies(ins, outs, sems)
        for cp in remote(False):
            cp.wait_recv()
        for cp in remote(True):
            cp.wait_send()
        for cp in mine():
            cp.wait()

    return _Plan(
        list(parts) + (list(into) if into is not None else []),
        [jax.ShapeDtypeStruct((N_CHIP, full_rows[a], parts[a].shape[2]), parts[a].dtype) for a in range(n)],
        [pltpu.SemaphoreType.DMA((n * 3,)), pltpu.SemaphoreType.DMA((n * 3,)), pltpu.SemaphoreType.DMA((n,))],
        start, finish,
        aliases={n + a: a for a in range(n)} if into is not None else None,
    )


def _call(body, *, name, operands, out_shape, grid=(), in_specs=None, out_specs=None, scratch_shapes=(),
          plans=(), aliases=None, prefetch=(), start_after_body=False, pass_on_at=0.6):
    operands, out_shape, scratch_shapes = list(operands), list(out_shape), list(scratch_shapes)
    n_pf, n_in, n_out, n_scr = len(prefetch), len(operands), len(out_shape), len(scratch_shapes)
    in_vmem = pl.BlockSpec(memory_space=pltpu.VMEM)
    in_hbm = pl.BlockSpec(memory_space=pl.ANY)
    in_specs = [in_vmem] * n_in if in_specs is None else list(in_specs)
    out_specs = [in_vmem] * n_out if out_specs is None else list(out_specs)
    io_alias = {n_pf + i: o for i, o in (aliases or {}).items()}
    pos_in, pos_out = n_pf + n_in, n_out
    for p in plans:
        for i, o in p.aliases.items():
            io_alias[pos_in + i] = pos_out + o
        pos_in += len(p.ins)
        pos_out += len(p.out_shape)
    steps = math.prod(grid) if grid else 1
    mid_step = min(steps - 1, int(steps * pass_on_at))
    pair_barrier = bool(plans) and all(p.sibling_only for p in plans)

    def shake_hands():
        x, y, c = _place()
        barrier = pltpu.get_barrier_semaphore()
        pl.semaphore_signal(barrier, inc=1, device_id=(x, y, 1 - c), device_id_type=MESH)
        pl.semaphore_wait(barrier, 1)

    def wrapped(*refs):
        refs = list(refs)
        take = lambda k: [refs.pop(0) for _ in range(k)]
        pf = take(n_pf)
        ins, p_ins = take(n_in), [take(len(p.ins)) for p in plans]
        outs, p_outs = take(n_out), [take(len(p.out_shape)) for p in plans]
        scr, p_sems = take(n_scr), [take(len(p.sems)) for p in plans]
        step = 0
        for ax, g in enumerate(grid):
            step = step * g + pl.program_id(ax)

        def hook(kind, at):
            todo = [(getattr(p, kind), a, b, c) for p, a, b, c in zip(plans, p_ins, p_outs, p_sems) if getattr(p, kind)]

            def run():
                if kind == "start" and pair_barrier:
                    shake_hands()
                for fn, a, b, c in todo:
                    fn(a, b, c)

            if todo and grid:
                pl.when(step == at)(run)
            elif todo:
                run()

        if not start_after_body:
            hook("start", 0)
        if body is not None:
            body(*pf, *ins, *outs, *scr)
        if start_after_body:
            hook("start", 0)
        hook("mid", mid_step)
        hook("finish", steps - 1)

    layout = dict(
        grid=grid,
        in_specs=in_specs + [in_hbm] * sum(len(p.ins) for p in plans),
        out_specs=out_specs + [in_hbm] * sum(len(p.out_shape) for p in plans),
        scratch_shapes=scratch_shapes + [s for p in plans for s in p.sems],
    )
    if n_pf:
        layout = dict(grid_spec=pltpu.PrefetchScalarGridSpec(num_scalar_prefetch=n_pf, **layout))
    res = pl.pallas_call(
        wrapped,
        name=name,
        out_shape=out_shape + [o for p in plans for o in p.out_shape],
        input_output_aliases=io_alias,
        compiler_params=pltpu.CompilerParams(
            dimension_semantics=("arbitrary",) * len(grid) if grid else None,
            collective_id=SIBLING_BARRIER_ID if pair_barrier else None),
        **layout,
    )(*prefetch, *operands, *[a for p in plans for a in p.ins])
    res = list(res)
    outs, plan_outs = res[:n_out], []
    pos = n_out
    for p in plans:
        plan_outs.append(res[pos : pos + len(p.out_shape)])
        pos += len(p.out_shape)
    return outs, plan_outs


def _inproj(x, g, w_shard, order, plans=()):
    s = x.shape[0]
    tm = min(TM_INPROJ, s)
    nt = s // tm
    wide = 2 * SHARD_IN
    fwd_at = nt // 2

    def body(order_ref, x_ref, g_ref, w_hbm, proj_ref, h1_ref, h1t_ref, wfull_hbm,
             w_scr, h1_scr, send_sems, recv_sems, local_sems):
        k, i = pl.program_id(0), pl.program_id(1)
        px, py, c = _place()
        me, sibling = (px, py, c), (px, py, 1 - c)
        chips = _other_chips(px, py)

        def half(slot, core):
            return w_scr.at[slot, :, pl.ds(pl.multiple_of(core * SHARD_IN, 128), SHARD_IN)]

        def copy(n, slot, core, to, src=None):
            return pltpu.make_async_remote_copy(
                src_ref=half(slot, core) if src is None else src, dst_ref=half(slot, core),
                send_sem=send_sems.at[n], recv_sem=recv_sems.at[n], device_id=to, device_id_type=MESH)

        def mine():
            return pltpu.make_async_copy(w_hbm, half(0, c), local_sems.at[0])

        def first():
            return [copy(0, 0, c, sibling, src=w_hbm)] + [copy(1 + j, 1 + j, c, (*chips[j], c), src=w_hbm) for j in range(2)]

        def relay():
            to = (c * px + (1 - c) * (1 - px), c * (1 - py) + (1 - c) * py, c)
            return pltpu.make_async_remote_copy(
                src_ref=half(2 - c, c), dst_ref=half(3, c), send_sem=send_sems.at[3], recv_sem=recv_sems.at[3],
                device_id=to, device_id_type=MESH)

        def save(slot):
            cols = pl.ds(pl.multiple_of(order_ref[slot] * wide, 128), wide)
            return pltpu.make_async_copy(w_scr.at[slot], wfull_hbm.at[:, cols], local_sems.at[1 + slot])

        @pl.when((k == 0) & (i == 0))
        def _():
            mine().start()
            for cp in first():
                cp.start()
            mine().wait()
            copy(0, 0, 1 - c, me).wait_recv()

        for j in range(3):
            @pl.when((k == j + 1) & (i == 0))
            def _():
                copy(4 + j, 1 + j, 1 - c, me).wait_recv()

        @pl.when((k == 3) & (i == 0))
        def _():
            for slot in range(N_CHIP):
                save(slot).start()

        @pl.when(k == 0)
        def _():
            xh, _ = _rms(x_ref[...])
            h = xh * g_ref[...]
            h1_ref[...] = h.astype(BF16)
            h1t_ref[...] = h.T.astype(BF16)
            h1_scr[pl.ds(pl.multiple_of(i * tm, tm), tm), :] = h.astype(BF16)

        proj_ref[...] = _dot(h1_scr[pl.ds(pl.multiple_of(i * tm, tm), tm), :], w_scr[k]).astype(BF16)

        for j in range(3):
            @pl.when((k == 0) & (i == nt - 1) if j < 2 else (k == 2) & (i == fwd_at))
            def _():
                copy(1 + j, 1 + j, c, me).wait_recv()
                copy(4 + j, 1 + j, c, sibling).start()
                if j < 2:
                    pl.when(c == 1 - j)(lambda: relay().start())

        @pl.when((k == 3) & (i == nt - 1))
        def _():
            for cp in first() + [relay()] + [copy(4 + j, 1 + j, c, sibling) for j in range(3)]:
                cp.wait_send()
            for slot in range(N_CHIP):
                save(slot).wait()

    once = lambda k, i, order_ref: (jnp.where(k == 0, i, nt - 1), 0)
    return _call(
        body,
        name="inproj",
        prefetch=[order],
        operands=[x, g, w_shard],
        grid=(N_CHIP, nt),
        in_specs=[
            pl.BlockSpec((tm, D_MODEL), once),
            pl.BlockSpec((1, D_MODEL), lambda k, i, order_ref: (0, 0)),
            pl.BlockSpec(memory_space=pl.ANY),
        ],
        out_specs=[
            pl.BlockSpec((tm, wide), lambda k, i, order_ref: (i, order_ref[k])),
            pl.BlockSpec((tm, D_MODEL), once),
            pl.BlockSpec((D_MODEL, tm), lambda k, i, order_ref: (0, jnp.where(k == 0, i, nt - 1))),
            pl.BlockSpec(memory_space=pl.ANY),
        ],
        out_shape=[
            jax.ShapeDtypeStruct((s, IN_DIM), BF16),
            jax.ShapeDtypeStruct((s, D_MODEL), BF16),
            jax.ShapeDtypeStruct((D_MODEL, s), BF16),
            jax.ShapeDtypeStruct((D_MODEL, IN_DIM), BF16),
        ],
        scratch_shapes=[
            pltpu.VMEM((N_CHIP, D_MODEL, wide), BF16),
            pltpu.VMEM((s, D_MODEL), BF16),
            pltpu.SemaphoreType.DMA((7,)),
            pltpu.SemaphoreType.DMA((7,)),
            pltpu.SemaphoreType.DMA((1 + N_CHIP,)),
        ],
        plans=plans,
        start_after_body=True,
        pass_on_at=0.8,
    )


def _rows_from_above(v, above):
    row = lax.broadcasted_iota(jnp.int32, (8, v.shape[1]), 0)
    r1, r2 = pltpu.roll(v, 1, 0), pltpu.roll(v, 2, 0)
    top1 = jnp.where(row == 0, above[7:8], r1[0:8])
    top2 = jnp.where(row == 0, above[6:7], jnp.where(row == 1, above[7:8], r2[0:8]))
    return jnp.concatenate([top1, r1[8:]], axis=0), jnp.concatenate([top2, r2[8:]], axis=0)


def _rows_from_below(v, below):
    n = v.shape[0]
    row = lax.broadcasted_iota(jnp.int32, (8, v.shape[1]), 0)
    r1, r2 = pltpu.roll(v, n - 1, 0), pltpu.roll(v, n - 2, 0)
    end1 = jnp.where(row == 7, below[0:1], r1[n - 8 :])
    end2 = jnp.where(row == 6, below[0:1], jnp.where(row == 7, below[1:2], r2[n - 8 :]))
    return jnp.concatenate([r1[: n - 8], end1], axis=0), jnp.concatenate([r2[: n - 8], end2], axis=0)


def _causal_mask():
    row = lax.broadcasted_iota(jnp.int32, (CHUNK, CHUNK), 0)
    col = lax.broadcasted_iota(jnp.int32, (CHUNK, CHUNK), 1)
    return row, row >= col


def _mixer_fwd(proj, conv_w, ln_g, ln_b, ws, bs_b, plans=()):
    s = proj.shape[0]
    nt = s // CHUNK

    def body(proj_ref, cw_ref, lng_ref, lnb_ref, ws_ref, bsb_ref, cat_ref, pcar_ref):
        @pl.when(pl.program_id(0) == 0)
        def _():
            pcar_ref[...] = jnp.zeros_like(pcar_ref)

        row, tril = _causal_mask()
        for j in range(8):
            cs = slice(CHUNK * j, CHUNK * (j + 1))

            def grp(k):
                return proj_ref[:, k * D_MODEL + CHUNK * j : k * D_MODEL + CHUNK * (j + 1)].astype(F32)

            w = cw_ref[:, cs]
            p = grp(1) * grp(2)
            p1, p2 = _rows_from_above(p, pcar_ref[:, cs])
            pcar_ref[:, cs] = p[CHUNK - 8 :]
            cv = w[0:1] * p2 + w[1:2] * p1 + w[2:3] * p
            sa = _silu(grp(3), with_grad=False)
            cat_ref[:, cs] = ((grp(0) * cv) * sa).astype(BF16)
            gu = _gelu(grp(4), with_grad=False)
            gv = _gelu(grp(5), with_grad=False)
            dv = gv - jnp.mean(gv, axis=-1, keepdims=True)
            vn = dv * lax.rsqrt(jnp.mean(dv * dv, axis=-1, keepdims=True) + EPS)
            vn = vn * lng_ref[:, cs] + lnb_ref[:, cs]
            wc = jnp.where(tril, ws_ref[j], 0.0).astype(BF16)
            sp = _dot(wc, vn.astype(BF16)) + bsb_ref[j]
            sb = _silu(grp(6), with_grad=False)
            cat_ref[:, D_MODEL + CHUNK * j : D_MODEL + CHUNK * (j + 1)] = ((gu * sp) * sb).astype(BF16)

    whole = lambda shape: pl.BlockSpec(shape, lambda i: (0,) * len(shape))
    (cat,), plan_outs = _call(
        body,
        name="mixer_fwd",
        operands=[proj, conv_w, ln_g, ln_b, ws, bs_b],
        grid=(nt,),
        in_specs=[
            pl.BlockSpec((CHUNK, IN_DIM), lambda i: (i, 0)),
            whole((8, D_MODEL)),
            whole((1, D_MODEL)),
            whole((1, D_MODEL)),
            whole((8, CHUNK, CHUNK)),
            whole((8, CHUNK, CHUNK)),
        ],
        out_specs=[pl.BlockSpec((CHUNK, MIX_DIM), lambda i: (i, 0))],
        out_shape=[jax.ShapeDtypeStruct((s, MIX_DIM), BF16)],
        scratch_shapes=[pltpu.VMEM((8, D_MODEL), F32)],
        plans=plans,
        pass_on_at=0.85,
    )
    return cat, plan_outs


def _mixer_bwd(proj, dcat, conv_w, ln_g, ln_b, ws, bs_b, plans=()):
    s = proj.shape[0]
    nt = s // CHUNK

    def body(proj_ref, halo_ref, dcat_ref, cw_ref, lng_ref, lnb_ref, ws_ref, bsb_ref,
             dproj_ref, dcw_ref, dlng_ref, dlnb_ref, dws_ref, dbs_ref,
             car_ref, acc_cw, acc_lng, acc_lnb, acc_bs, wc_scr, wct_scr):
        i = pl.program_id(0)
        tile = nt - 1 - i
        row, tril = _causal_mask()

        @pl.when(i == 0)
        def _():
            for j in range(8):
                wc = jnp.where(tril, ws_ref[j], 0.0)
                wc_scr[j] = wc.astype(BF16)
                wct_scr[j] = wc.T.astype(BF16)
            car_ref[...] = jnp.zeros_like(car_ref)
            acc_cw[...] = jnp.zeros_like(acc_cw)
            acc_lng[...] = jnp.zeros_like(acc_lng)
            acc_lnb[...] = jnp.zeros_like(acc_lnb)
            acc_bs[...] = jnp.zeros_like(acc_bs)
            dws_ref[...] = jnp.zeros_like(dws_ref)

        has_prev = jnp.where(tile > 0, 1.0, 0.0).astype(F32)
        for j in range(8):
            cs = slice(CHUNK * j, CHUNK * (j + 1))

            def col(k):
                return slice(k * D_MODEL + CHUNK * j, k * D_MODEL + CHUNK * (j + 1))

            def grp(k):
                return proj_ref[:, col(k)].astype(F32)

            w = cw_ref[:, cs]
            gb, gc, xa = grp(0), grp(1), grp(2)
            p = gc * xa
            pprev = halo_ref[:, col(1)].astype(F32) * halo_ref[:, col(2)].astype(F32) * has_prev
            p1, p2 = _rows_from_above(p, pprev[8:16])
            cv = w[0:1] * p2 + w[1:2] * p1 + w[2:3] * p
            za = grp(3)
            sa, dsa = _silu(za)
            da = dcat_ref[:, cs].astype(F32)
            da_gb = da * gb
            dproj_ref[:, col(0)] = ((da * sa) * cv).astype(BF16)
            dproj_ref[:, col(3)] = ((da_gb * cv) * dsa).astype(BF16)
            dcv = da_gb * sa
            d1, d2 = _rows_from_below(dcv, car_ref[:, cs])
            car_ref[:, cs] = dcv[0:8]
            dp = w[2:3] * dcv + w[1:2] * d1 + w[0:1] * d2
            dproj_ref[:, col(1)] = (dp * xa).astype(BF16)
            dproj_ref[:, col(2)] = (dp * gc).astype(BF16)
            acc_cw[0, :, cs] += _rows8(dcv * p2)
            acc_cw[1, :, cs] += _rows8(dcv * p1)
            acc_cw[2, :, cs] += _rows8(dcv * p)
            gu, dgu = _gelu(grp(4))
            gv, dgv = _gelu(grp(5))
            dv = gv - jnp.mean(gv, axis=-1, keepdims=True)
            rstd = lax.rsqrt(jnp.mean(dv * dv, axis=-1, keepdims=True) + EPS)
            vnh = dv * rstd
            lng = lng_ref[:, cs]
            vnb = (vnh * lng + lnb_ref[:, cs]).astype(BF16)
            sp = _dot(wc_scr[j], vnb) + bsb_ref[j]
            zb = grp(6)
            sb, dsb = _silu(zb)
            db = dcat_ref[:, D_MODEL + CHUNK * j : D_MODEL + CHUNK * (j + 1)].astype(F32)
            db_sp, db_gu = db * sp, db * gu
            dproj_ref[:, col(4)] = ((db_sp * sb) * dgu).astype(BF16)
            dproj_ref[:, col(6)] = ((db_sp * gu) * dsb).astype(BF16)
            dsp = db_gu * sb
            acc_bs[j] += dsp
            dspb = dsp.astype(BF16)
            dws_ref[j] += _dot_nt(dspb, vnb)
            dvn = _dot(wct_scr[j], dspb)
            acc_lnb[:, cs] += _rows8(dvn)
            acc_lng[:, cs] += _rows8(dvn * vnh)
            dvh = dvn * lng
            dgvv = rstd * (dvh - jnp.mean(dvh, axis=-1, keepdims=True) - vnh * jnp.mean(dvh * vnh, axis=-1, keepdims=True))
            dproj_ref[:, col(5)] = (dgvv * dgv).astype(BF16)

        @pl.when(i == nt - 1)
        def _():
            dlng_ref[...] = jnp.sum(acc_lng[...], axis=0, keepdims=True)
            dlnb_ref[...] = jnp.sum(acc_lnb[...], axis=0, keepdims=True)
            dcw_ref[...] = jnp.zeros_like(dcw_ref)
            for k in range(3):
                dcw_ref[k : k + 1, :] = jnp.sum(acc_cw[k], axis=0, keepdims=True)
            for j in range(8):
                dws_ref[j] = jnp.where(tril, dws_ref[j], 0.0)
                dbs_ref[:, CHUNK * j : CHUNK * (j + 1)] = jnp.sum(acc_bs[j].T, axis=0, keepdims=True)

    whole = lambda shape: pl.BlockSpec(shape, lambda i: (0,) * len(shape))
    halo_rows = 16
    per = CHUNK // halo_rows
    return _call(
        body,
        name="mixer_bwd",
        operands=[proj, proj, dcat, conv_w, ln_g, ln_b, ws, bs_b],
        plans=plans,
        grid=(nt,),
        in_specs=[
            pl.BlockSpec((CHUNK, IN_DIM), lambda i: (nt - 1 - i, 0)),
            pl.BlockSpec((halo_rows, IN_DIM), lambda i: (jnp.maximum((nt - 1 - i) * per - 1, 0), 0)),
            pl.BlockSpec((CHUNK, MIX_DIM), lambda i: (nt - 1 - i, 0)),
            whole((8, D_MODEL)),
            whole((1, D_MODEL)),
            whole((1, D_MODEL)),
            whole((8, CHUNK, CHUNK)),
            whole((8, CHUNK, CHUNK)),
        ],
        out_specs=[
            pl.BlockSpec((CHUNK, IN_DIM), lambda i: (nt - 1 - i, 0)),
            whole((8, D_MODEL)),
            whole((1, D_MODEL)),
            whole((1, D_MODEL)),
            whole((8, CHUNK, CHUNK)),
            whole((1, D_MODEL)),
        ],
        out_shape=[
            jax.ShapeDtypeStruct((s, IN_DIM), BF16),
            jax.ShapeDtypeStruct((8, D_MODEL), F32),
            jax.ShapeDtypeStruct((1, D_MODEL), F32),
            jax.ShapeDtypeStruct((1, D_MODEL), F32),
            jax.ShapeDtypeStruct((8, CHUNK, CHUNK), F32),
            jax.ShapeDtypeStruct((1, D_MODEL), F32),
        ],
        scratch_shapes=[
            pltpu.VMEM((8, D_MODEL), F32),
            pltpu.VMEM((3, 8, D_MODEL), F32),
            pltpu.VMEM((8, D_MODEL), F32),
            pltpu.VMEM((8, D_MODEL), F32),
            pltpu.VMEM((8, CHUNK, CHUNK), F32),
            pltpu.VMEM((8, CHUNK, CHUNK), BF16),
            pltpu.VMEM((8, CHUNK, CHUNK), BF16),
        ],
    )


def _kv_fwd(mem, g_mem, w_kv):
    def body(mem_ref, g_ref, w_ref, k_ref, vv_ref, mt_ref):
        mh, _ = _rms(mem_ref[...])
        m = mh * g_ref[...]
        kv = _dot(m.astype(BF16), w_ref[...])
        k_ref[...] = kv[:, :D_MODEL].astype(BF16)
        vv_ref[...] = kv[:, D_MODEL:].astype(BF16)
        mt_ref[...] = m.T.astype(BF16)

    return pl.pallas_call(
        body,
        name="kv_fwd",
        out_shape=[
            jax.ShapeDtypeStruct((MEM_LEN, D_MODEL), BF16),
            jax.ShapeDtypeStruct((MEM_LEN, D_MODEL), BF16),
            jax.ShapeDtypeStruct((D_MODEL, MEM_LEN), BF16),
        ],
    )(mem, g_mem, w_kv)


def _kv_bwd(mem, w_kv, mt, dk, dvv):
    def body(mem_ref, w_ref, mt_ref, dk_ref, dvv_ref, dw_ref, dwb_ref, dg_ref):
        dkv = jnp.concatenate([dk_ref[...], dvv_ref[...]], axis=1).astype(BF16)
        dw = _dot(mt_ref[...], dkv)
        dw_ref[...] = dw
        dwb_ref[...] = dw.astype(BF16)
        dm = _dot_nt(dkv, w_ref[...])
        mh, _ = _rms(mem_ref[...])
        dg_ref[...] = jnp.sum(dm * mh, axis=0, keepdims=True)

    return pl.pallas_call(
        body,
        name="kv_bwd",
        out_shape=[
            jax.ShapeDtypeStruct((D_MODEL, 2 * D_MODEL), F32),
            jax.ShapeDtypeStruct((D_MODEL, 2 * D_MODEL), BF16),
            jax.ShapeDtypeStruct((1, D_MODEL), F32),
        ],
    )(mem, w_kv, mt, dk, dvv)


def _attn(x, cat, target, w_out, w_q, w_xo, k, vv, g_x, g_f):
    s = x.shape[0]
    tm = min(TM_ATTN, s)
    sub = min(SUB_ATTN, tm)
    nt = s // tm
    scale = 1.0 / math.sqrt(X_HEAD_DIM)

    def body(x_ref, cat_ref, t_ref, wout_ref, wq_ref, wxo_ref, k_ref, vv_ref, gx_ref, gf_ref,
             dcat_ref, actt_ref, cot_ref,
             dk_ref, dvv_ref, dgf_ref, dgx_ref, loss_ref,
             p_scr, q_scr, o_scr, dq_scr, acc_gf, acc_gx, acc_loss):
        i = pl.program_id(0)

        @pl.when(i == 0)
        def _():
            dk_ref[...] = jnp.zeros_like(dk_ref)
            dvv_ref[...] = jnp.zeros_like(dvv_ref)
            acc_gf[...] = jnp.zeros_like(acc_gf)
            acc_gx[...] = jnp.zeros_like(acc_gx)
            acc_loss[...] = jnp.zeros_like(acc_loss)

        subs = [slice(u * sub, (u + 1) * sub) for u in range(tm // sub)]
        heads = [slice(X_HEAD_DIM * h, X_HEAD_DIM * (h + 1)) for h in range(X_HEADS)]
        gx, gf = gx_ref[...], gf_ref[...]
        x1, x1h, r2 = [], [], []
        for rs in subs:
            cat = cat_ref[rs, :]
            actt_ref[0:MIX_DIM, rs] = cat.astype(F32).T.astype(BF16)
            v = x_ref[rs, :] + _dot(cat, wout_ref[...])
            vh, r = _rms(v)
            h2 = vh * gx
            actt_ref[MIX_DIM : MIX_DIM + D_MODEL, rs] = h2.T.astype(BF16)
            q_scr[rs, :] = _dot(h2.astype(BF16), wq_ref[...]).astype(BF16)
            x1.append(v), x1h.append(vh), r2.append(r)
        for rs in subs:
            for hs in heads:
                sc = _dot_nt(q_scr[rs, hs], k_ref[:, hs]) * scale
                e = jnp.exp(sc - jnp.max(sc, axis=-1, keepdims=True))
                p = e / jnp.sum(e, axis=-1, keepdims=True)
                p_scr[rs, hs] = p
                o_scr[rs, hs] = _dot(p.astype(BF16), vv_ref[:, hs])
        dx2, sq, d_gf = [], None, None
        for u, rs in enumerate(subs):
            o = o_scr[rs, :]
            actt_ref[MIX_DIM + D_MODEL :, rs] = o.T.astype(BF16)
            x2 = x1[u] + _dot(o.astype(BF16), wxo_ref[...])
            x2h, r3 = _rms(x2)
            err = x2h * gf - t_ref[rs, :]
            dy = err * (1.0 / D_MODEL)
            sq = _rows8(err * err) if sq is None else sq + _rows8(err * err)
            d_gf = _rows8(dy * x2h) if d_gf is None else d_gf + _rows8(dy * x2h)
            dx2.append(_rms_bwd(dy * gf, x2h, r3))
        acc_loss[...] += sq
        acc_gf[...] += d_gf
        d_vv, d_k = [None] * X_HEADS, [None] * X_HEADS
        for u, rs in enumerate(subs):
            dx2b = dx2[u].astype(BF16)
            cot_ref[2, rs, :] = dx2b
            dob = _dot_nt(dx2b, wxo_ref[...]).astype(BF16)
            for h, hs in enumerate(heads):
                p = p_scr[rs, hs]
                dp = _dot_nt(dob[:, hs], vv_ref[:, hs])
                ds = p * (dp - jnp.sum(dp * p, axis=-1, keepdims=True)) * scale
                dq_scr[rs, hs] = _dot(ds.astype(BF16), k_ref[:, hs]).astype(BF16)
                dvv_h = _dot(p.T.astype(BF16), dob[:, hs])
                dk_h = _dot(ds.T.astype(BF16), q_scr[rs, hs])
                d_vv[h] = dvv_h if d_vv[h] is None else d_vv[h] + dvv_h
                d_k[h] = dk_h if d_k[h] is None else d_k[h] + dk_h
        for h, hs in enumerate(heads):
            dvv_ref[:, hs] += d_vv[h]
            dk_ref[:, hs] += d_k[h]
        d_gx = None
        for u, rs in enumerate(subs):
            dqb = dq_scr[rs, :]
            cot_ref[1, rs, :] = dqb
            dh2 = _dot_nt(dqb, wq_ref[...])
            d_gx = _rows8(dh2 * x1h[u]) if d_gx is None else d_gx + _rows8(dh2 * x1h[u])
            dx1 = dx2[u] + _rms_bwd(dh2 * gx, x1h[u], r2[u])
            dx1b = dx1.astype(BF16)
            cot_ref[0, rs, :] = dx1b
            dcat_ref[rs, :] = _dot_nt(dx1b, wout_ref[...]).astype(BF16)
        acc_gx[...] += d_gx

        @pl.when(i == nt - 1)
        def _():
            dgf_ref[...] = jnp.sum(acc_gf[...], axis=0, keepdims=True)
            dgx_ref[...] = jnp.sum(acc_gx[...], axis=0, keepdims=True)
            loss_ref[...] = jnp.sum(acc_loss[...], axis=0, keepdims=True)

    whole = lambda shape: pl.BlockSpec(shape, lambda i: (0,) * len(shape))
    once = lambda shape: pl.BlockSpec(shape, lambda i: (0,) * len(shape), pipeline_mode=pl.Buffered(1))
    rows = lambda width: pl.BlockSpec((tm, width), lambda i: (i, 0))
    cols = lambda height: pl.BlockSpec((height, tm), lambda i: (0, i))
    vec = jax.ShapeDtypeStruct((1, D_MODEL), F32)
    return pl.pallas_call(
        body,
        name="attn",
        grid=(nt,),
        in_specs=[
            rows(D_MODEL), rows(MIX_DIM), rows(D_MODEL),
            once((MIX_DIM, D_MODEL)), once((D_MODEL, D_MODEL)), once((D_MODEL, D_MODEL)),
            once((MEM_LEN, D_MODEL)), once((MEM_LEN, D_MODEL)),
            whole((1, D_MODEL)), whole((1, D_MODEL)),
        ],
        out_specs=[
            rows(MIX_DIM), cols(2 * MIX_DIM),
            pl.BlockSpec((3, tm, D_MODEL), lambda i: (0, i, 0)),
            whole((MEM_LEN, D_MODEL)), whole((MEM_LEN, D_MODEL)),
            whole((1, D_MODEL)), whole((1, D_MODEL)), whole((1, D_MODEL)),
        ],
        out_shape=[
            jax.ShapeDtypeStruct((s, MIX_DIM), BF16),
            jax.ShapeDtypeStruct((2 * MIX_DIM, s), BF16),
            jax.ShapeDtypeStruct((3, s, D_MODEL), BF16),
            jax.ShapeDtypeStruct((MEM_LEN, D_MODEL), F32),
            jax.ShapeDtypeStruct((MEM_LEN, D_MODEL), F32),
            vec, vec, vec,
        ],
        scratch_shapes=[
            pltpu.VMEM((tm, D_MODEL), F32),
            pltpu.VMEM((tm, D_MODEL), BF16),
            pltpu.VMEM((tm, D_MODEL), F32),
            pltpu.VMEM((tm, D_MODEL), BF16),
            pltpu.VMEM((8, D_MODEL), F32),
            pltpu.VMEM((8, D_MODEL), F32),
            pltpu.VMEM((8, D_MODEL), F32),
        ],
        compiler_params=pltpu.CompilerParams(dimension_semantics=("arbitrary",)),
    )(x, cat, target, w_out, w_q, w_xo, k, vv, g_x, g_f)


def _matmul_f32(name, a, b, rows=None, plans=()):
    first, m = (0, a.shape[0]) if rows is None else rows
    kk = a.shape[1]
    n = b.shape[1]
    tm, tn, tk = min(m, TM_DW), min(n, D_MODEL), min(kk, TK_DW)
    nk = kk // tk
    i0 = first // tm

    def body(a_ref, b_ref, o_ref, ob_ref):
        prod = _dot(a_ref[...], b_ref[...])
        if nk == 1:
            o_ref[...] = prod
            ob_ref[...] = prod.astype(BF16)
            return
        kid = pl.program_id(2)

        @pl.when(kid == 0)
        def _():
            o_ref[...] = prod

        @pl.when(kid > 0)
        def _():
            o_ref[...] += prod

        @pl.when(kid == nk - 1)
        def _():
            ob_ref[...] = o_ref[...].astype(BF16)

    out_spec = pl.BlockSpec((tm, tn), lambda i, j, k: (i, j))
    outs, plan_outs = _call(
        body,
        name=name,
        operands=[a, b],
        grid=(m // tm, n // tn, nk),
        in_specs=[pl.BlockSpec((tm, tk), lambda i, j, k: (i + i0, k)), pl.BlockSpec((tk, tn), lambda i, j, k: (k, j))],
        out_specs=[out_spec, out_spec],
        out_shape=[jax.ShapeDtypeStruct((m, n), F32), jax.ShapeDtypeStruct((m, n), BF16)],
        plans=plans,
    )
    return outs, plan_outs


def _dw_attn(actt, cot, plans=()):
    s = actt.shape[1]
    tm = TM_DW
    counts = [MIX_DIM // tm, D_MODEL // tm, D_MODEL // tm]
    starts = [0, counts[0], counts[0] + counts[1]]
    which = lambda i: jnp.where(i < starts[1], 0, jnp.where(i < starts[2], 1, 2))

    def body(a_ref, b_ref, *outs):
        i = pl.program_id(0)
        prod = _dot(a_ref[...], b_ref[0])
        for m in range(3):
            @pl.when(which(i) == m)
            def _():
                outs[2 * m][...] = prod
                outs[2 * m + 1][...] = prod.astype(BF16)

    out_specs, out_shape = [], []
    for m in range(3):
        spec = pl.BlockSpec((tm, D_MODEL), lambda i, m=m: (jnp.clip(i - starts[m], 0, counts[m] - 1), 0))
        out_specs += [spec, spec]
        out_shape += [jax.ShapeDtypeStruct((counts[m] * tm, D_MODEL), F32), jax.ShapeDtypeStruct((counts[m] * tm, D_MODEL), BF16)]
    outs, plan_outs = _call(
        body,
        name="dw_attn",
        operands=[actt, cot],
        grid=(sum(counts),),
        in_specs=[pl.BlockSpec((tm, s), lambda i: (i, 0)), pl.BlockSpec((1, s, D_MODEL), lambda i: (which(i), 0, 0))],
        out_specs=out_specs,
        out_shape=out_shape,
        plans=plans,
    )
    return [outs[2 * m : 2 * m + 2] for m in range(3)], plan_outs


def _dh1(name, dproj, w_in, x, dx1, g, tiles, into=None, after=(), plans=()):
    s = x.shape[0]
    tm = min(TM_DH1, s)
    t0, nt = tiles[0], tiles[1] - tiles[0]

    def body(dp_ref, w_ref, x_ref, dx1_ref, g_ref, *rest):
        gx_ref, dg_ref, acc_g = rest[-3:]
        i = pl.program_id(0)

        @pl.when(i == 0)
        def _():
            acc_g[...] = jnp.zeros_like(acc_g)

        dh1 = _dot_nt(dp_ref[...], w_ref[...])
        xh, r = _rms(x_ref[...])
        acc_g[...] += _rows8(dh1 * xh)
        gx_ref[...] = dx1_ref[0].astype(F32) + _rms_bwd(dh1 * g_ref[...], xh, r)

        @pl.when(i == nt - 1)
        def _():
            dg_ref[...] = jnp.sum(acc_g[...], axis=0, keepdims=True)

    rows = pl.BlockSpec((tm, D_MODEL), lambda i: (i + t0, 0))
    return _call(
        body,
        name=name,
        operands=[dproj, w_in, x, dx1, g] + ([into] if into is not None else []) + list(after),
        grid=(nt,),
        in_specs=[
            pl.BlockSpec((tm, IN_DIM), lambda i: (i + t0, 0)),
            pl.BlockSpec((D_MODEL, IN_DIM), lambda i: (0, 0), pipeline_mode=pl.Buffered(1)),
            rows,
            pl.BlockSpec((1, tm, D_MODEL), lambda i: (0, i + t0, 0)),
            pl.BlockSpec((1, D_MODEL), lambda i: (0, 0)),
        ] + [pl.BlockSpec(memory_space=pl.ANY)] * ((into is not None) + len(after)),
        out_specs=[rows, pl.BlockSpec((1, D_MODEL), lambda i: (0, 0))],
        out_shape=[jax.ShapeDtypeStruct((s, D_MODEL), F32), jax.ShapeDtypeStruct((1, D_MODEL), F32)],
        scratch_shapes=[pltpu.VMEM((8, D_MODEL), F32)],
        plans=plans,
        aliases={5: 0} if into is not None else None,
    )


def _pair_sum(name, grad, got, axis, size, core, plans=()):
    shard = list(grad.shape)
    shard[axis] = size
    r, cdim = shard
    tr = min(r, TR_PAIR_SUM)
    nr = r // tr

    def body(core_ref, g_ref, got_ref, o_ref):
        o_ref[0] = (g_ref[...] + got_ref[0].astype(F32)).astype(BF16)

    if axis == 1:
        g_map = lambda q, t, core_ref: (t, 2 * q + core_ref[0])
    else:
        g_map = lambda q, t, core_ref: ((2 * q + core_ref[0]) * nr + t, 0)
    (out,), plan_outs = _call(
        body,
        name=name,
        prefetch=[core],
        operands=[grad, got],
        grid=(N_CHIP, nr),
        in_specs=[
            pl.BlockSpec((tr, cdim), g_map),
            pl.BlockSpec((1, tr, cdim), lambda q, t, core_ref: (q, t, 0)),
        ],
        out_specs=[pl.BlockSpec((1, tr, cdim), lambda q, t, core_ref: (q, t, 0))],
        out_shape=[jax.ShapeDtypeStruct((N_CHIP, r, cdim), BF16)],
        plans=plans,
    )
    return (out, plan_outs) if plans else out


def _adam_step(w, m, v, g):
    c1 = 1.0 / (1.0 - ADAM_B1**ADAM_STEP)
    c2 = 1.0 / (1.0 - ADAM_B2**ADAM_STEP)
    m_new = ADAM_B1 * m + (1.0 - ADAM_B1) * g
    v_new = ADAM_B2 * v + (1.0 - ADAM_B2) * (g * g)
    return -ADAM_LR * ((m_new * c1) / (jnp.sqrt(v_new * c2) + ADAM_EPS) + ADAM_WD * w), m_new, v_new


def _update_small(dev, vec_w, vec_m, vec_v, conv_w, conv_m, conv_v, small, landed):
    n_vec = len(vec_w)

    def body(dev_ref, *refs):
        refs = list(refs)
        take = lambda k: [refs.pop(0) for _ in range(k)]
        w_refs, m_refs, v_refs = take(n_vec), take(n_vec), take(n_vec)
        cw_ref, cm_ref, cv_ref, small_ref, land_ref, small_conv_ref, land_conv_ref = take(7)
        vec_outs, conv_outs, (loss_ref,) = take(4 * n_vec), take(4), take(1)

        def total(own_ref, others_ref):
            acc = None
            for d in range(N_DEV):
                part = jnp.where(dev_ref[0] == d, own_ref[0], others_ref[d])
                acc = part if acc is None else acc + part
            return acc

        g_all = total(small_ref, land_ref)
        for i in range(n_vec):
            g = g_all[i : i + 1]
            delta, m_new, v_new = _adam_step(w_refs[i][...], m_refs[i][...], v_refs[i][...], g)
            for ref, val in zip(vec_outs[4 * i : 4 * i + 4], (g, delta, m_new, v_new)):
                ref[...] = val
        g = total(small_conv_ref, land_conv_ref)[0:3]
        delta, m_new, v_new = _adam_step(cw_ref[0], cm_ref[0], cv_ref[0], g)
        for ref, val in zip(conv_outs, (g, delta, m_new, v_new)):
            ref[0] = val
        loss_ref[...] = jnp.broadcast_to(0.5 * jnp.sum(g_all[7:8], axis=-1, keepdims=True) / D_MODEL, loss_ref.shape)

    vec_spec = pl.BlockSpec((1, D_MODEL), lambda i, dev_ref: (0, 0))
    conv_spec = pl.BlockSpec((1, 3, CHUNK), lambda i, dev_ref: (0, 0, 0))
    vec_shape = jax.ShapeDtypeStruct((1, D_MODEL), F32)
    conv_shape = jax.ShapeDtypeStruct((1, 3, CHUNK), F32)
    outs, _ = _call(
        body,
        name="update_small",
        prefetch=[dev],
        operands=list(vec_w) + list(vec_m) + list(vec_v) + [conv_w, conv_m, conv_v, small, landed, small, landed],
        grid=(1,),
        in_specs=[vec_spec] * (3 * n_vec) + [conv_spec] * 3 + [
            pl.BlockSpec((1, 16, D_MODEL), lambda i, dev_ref: (0, 0, 0)),
            pl.BlockSpec((N_DEV, 16, D_MODEL), lambda i, dev_ref: (0, 0, 0)),
            pl.BlockSpec((1, 8, CHUNK), lambda i, dev_ref: (0, 1, dev_ref[0])),
            pl.BlockSpec((N_DEV, 8, CHUNK), lambda i, dev_ref: (0, 1, dev_ref[0])),
        ],
        out_specs=[vec_spec] * (4 * n_vec) + [conv_spec] * 4 + [pl.BlockSpec((1, CHUNK), lambda i, dev_ref: (0, 0))],
        out_shape=[vec_shape] * (4 * n_vec) + [conv_shape] * 4 + [jax.ShapeDtypeStruct((1, CHUNK), F32)],
    )
    return [outs[4 * i : 4 * i + 4] for i in range(n_vec)], outs[4 * n_vec : 4 * n_vec + 4], outs[-1]


def _adamw(name, groups, steps, own_slot=None, plans=()):
    lists = lambda p: list(p) if isinstance(p, (list, tuple)) else [p]
    groups = [(g[0], g[1], g[2], lists(g[3]), lists(g[4]) if len(g) > 4 else []) for g in groups]
    counts = [3 + len(g[3]) + len(g[4]) for g in groups]
    use_slot = own_slot is not None

    def first_tiles(w, parts):
        firsts, first = [], 0
        for p in parts:
            firsts.append(first)
            first += p.shape[1] // (w.shape[0] // steps)
        return firsts

    def update(t, slot_ref, firsts, w_ref, m_ref, v_ref, p_refs, o_refs, g_ref, d_ref, nm_ref, nv_ref):
        g = None
        for k, p_ref in enumerate(p_refs):
            gk = None
            for q in range(p_ref.shape[0]):
                part = p_ref[q].astype(F32)
                if o_refs:
                    part = jnp.where(slot_ref[0] == q, o_refs[k][0].astype(F32), part)
                gk = part if gk is None else gk + part
            g = gk if g is None else jnp.where(t >= firsts[k], gk, g)
        g_ref[...] = g
        d_ref[...], nm_ref[...], nv_ref[...] = _adam_step(w_ref[...], m_ref[...], v_ref[...], g)

    def body(*refs):
        refs = list(refs)
        slot_ref = refs.pop(0) if use_slot else None
        t = pl.program_id(0)
        ins, outs = refs[: sum(counts)], refs[sum(counts) :]
        for i, (w, _, _, parts, own) in enumerate(groups):
            mine = ins[sum(counts[:i]) : sum(counts[: i + 1])]
            update(t, slot_ref, first_tiles(w, parts), *mine[:3], mine[3 : 3 + len(parts)], mine[3 + len(parts) :],
                   *outs[4 * i : 4 * i + 4])

    operands, in_specs, out_specs, out_shape = [], [], [], []
    for w, m, v, parts, own in groups:
        r, cdim = w.shape
        tr = r // steps
        firsts = first_tiles(w, parts)

        def tile_of(k, t, firsts=firsts, parts=parts, tr=tr):
            return jnp.clip(t - firsts[k], 0, parts[k].shape[1] // tr - 1)

        blk = pl.BlockSpec((tr, cdim), lambda t, *_: (t, 0))
        operands += [w, m, v] + parts + own
        in_specs += [blk, blk, blk]
        in_specs += [pl.BlockSpec((p.shape[0], tr, cdim), lambda t, *_, k=k, tile_of=tile_of: (0, tile_of(k, t), 0)) for k, p in enumerate(parts)]
        in_specs += [pl.BlockSpec((1, tr, cdim), lambda t, slot_ref, k=k, tile_of=tile_of: (slot_ref[0], tile_of(k, t), 0)) for k in range(len(own))]
        out_specs += [blk] * 4
        out_shape += [jax.ShapeDtypeStruct((r, cdim), F32)] * 4
    outs, plan_outs = _call(
        body,
        name=name,
        prefetch=[own_slot] if use_slot else [],
        operands=operands,
        grid=(steps,),
        in_specs=in_specs,
        out_specs=out_specs,
        out_shape=out_shape,
        plans=plans,
    )
    return [outs[4 * i : 4 * i + 4] for i in range(len(groups))], plan_outs


def _chip_routes(n):
    x, y, c = _place()
    my_chip = 2 * x + y
    return [(a, (*chip, c), 2 * chip[0] + chip[1], my_chip, 2 * chip[0] + chip[1])
            for a in range(n) for chip in _other_chips(x, y)]


def _gather_routes(n):
    x, y, c = _place()
    flip = lambda v, f: 1 - v if f else v
    peers = [(flip(x, fx), flip(y, fy), flip(c, fc)) for fx in (0, 1) for fy in (0, 1) for fc in (0, 1) if fx + fy + fc]
    return [(0, p, 0, 4 * x + 2 * y + c, 4 * p[0] + 2 * p[1] + p[2]) for p in peers]


_chip_routes.copies_per_array = N_CHIP - 1
_gather_routes.copies_per_array = N_DEV - 1


def _split_start(name, routes, srcs, land_shapes):
    n = len(srcs)
    hbm = pl.BlockSpec(memory_space=pltpu.HBM)
    sem = pl.BlockSpec(memory_space=pltpu.SEMAPHORE)

    def body(*refs):
        src_refs, land_refs = refs[:n], refs[n : 2 * n]
        send_sems, recv_sems = refs[2 * n], refs[2 * n + 1]
        token = refs[-1]
        for k, (a, peer, src_slot, there, _) in enumerate(routes(n)):
            pltpu.make_async_remote_copy(
                src_ref=src_refs[a].at[src_slot], dst_ref=land_refs[a].at[there],
                send_sem=send_sems.at[k], recv_sem=recv_sems.at[k], device_id=peer, device_id_type=MESH).start()
        token[...] = jnp.zeros_like(token)

    in_hbm = [pltpu.with_memory_space_constraint(p, pltpu.HBM) for p in srcs]
    in_hbm += [pltpu.with_memory_space_constraint(lax.empty(shape, p.dtype), pltpu.HBM) for p, shape in zip(srcs, land_shapes)]
    n_copies = routes.copies_per_array * n
    res = pl.pallas_call(
        body,
        name=name,
        out_shape=[pltpu.SemaphoreType.DMA((n_copies,)), pltpu.SemaphoreType.DMA((n_copies,))]
        + [pltpu.HBM(p.shape, p.dtype) for p in srcs] + [pltpu.HBM(shape, p.dtype) for p, shape in zip(srcs, land_shapes)]
        + [jax.ShapeDtypeStruct((8, CHUNK), F32)],
        in_specs=[hbm] * (2 * n),
        out_specs=[sem, sem] + [hbm] * (2 * n) + [pl.BlockSpec(memory_space=pltpu.VMEM)],
        input_output_aliases={i: 2 + i for i in range(2 * n)},
        compiler_params=pltpu.CompilerParams(has_side_effects=pltpu.SideEffectType.DATAFLOW_SIDE_EFFECTING),
    )(*in_hbm)
    return res[:-1], res[-1]


def _split_wait(name, routes, in_flight, after):
    n = (len(in_flight) - 2) // 2
    hbm = pl.BlockSpec(memory_space=pltpu.HBM)
    sem = pl.BlockSpec(memory_space=pltpu.SEMAPHORE)
    send_sems, recv_sems, *bufs = in_flight

    def body(*refs):
        src_refs, land_refs = refs[:n], refs[n : 2 * n]
        send_ref, recv_ref = refs[2 * n], refs[2 * n + 1]
        for k, (a, peer, src_slot, _, here) in enumerate(routes(n)):
            cp = pltpu.make_async_remote_copy(
                src_ref=src_refs[a].at[src_slot], dst_ref=land_refs[a].at[here],
                send_sem=send_ref.at[k], recv_sem=recv_ref.at[k], device_id=peer, device_id_type=MESH)
            cp.wait_send()
            cp.wait_recv()

    res = pl.pallas_call(
        body,
        name=name,
        out_shape=[pltpu.HBM(b.shape, b.dtype) for b in bufs],
        in_specs=[hbm] * (2 * n) + [sem, sem] + [pl.BlockSpec(memory_space=pl.ANY)] * len(after),
        out_specs=[hbm] * (2 * n),
        input_output_aliases={i: i for i in range(2 * n)},
        compiler_params=pltpu.CompilerParams(has_side_effects=pltpu.SideEffectType.DATAFLOW_SIDE_EFFECTING),
    )(*bufs, send_sems, recv_sems, *after)
    return res[:n], res[n:]


def kernel(x, mem, norm_mix_g, w_in, conv_w, gm_ln_g, gm_ln_b, gm_ws, gm_bs, w_out, norm_x_g, norm_mem_g, w_q, w_kv, w_xo, norm_final_g, loss_target, m_norm_mix_g, m_w_in, m_conv_w, m_gm_ln_g, m_gm_ln_b, m_gm_ws, m_gm_bs, m_w_out, m_norm_x_g, m_norm_mem_g, m_w_q, m_w_kv, m_w_xo, m_norm_final_g, v_norm_mix_g, v_w_in, v_conv_w, v_gm_ln_g, v_gm_ln_b, v_gm_ws, v_gm_bs, v_w_out, v_norm_x_g, v_norm_mem_g, v_w_q, v_w_kv, v_w_xo, v_norm_final_g):
    s = x.shape[1]
    dev = 4 * lax.axis_index("x") + 2 * lax.axis_index("y") + lax.axis_index("c")
    core = lax.axis_index("c").astype(jnp.int32).reshape(1)
    x2 = x[0]
    target = loss_target[0]
    pad_taps = lambda t: jnp.pad(t[0], ((0, 5), (0, 0)))

    big_names = ["w_in", "w_out", "w_q", "w_kv", "w_xo"]
    big_w = dict(w_in=w_in[0], w_out=w_out[0], w_q=w_q[0], w_kv=w_kv[0], w_xo=w_xo[0])
    big_m = dict(w_in=m_w_in[0], w_out=m_w_out[0], w_q=m_w_q[0], w_kv=m_w_kv[0], w_xo=m_w_xo[0])
    big_v = dict(w_in=v_w_in[0], w_out=v_w_out[0], w_q=v_w_q[0], w_kv=v_w_kv[0], w_xo=v_w_xo[0])
    shard_axis = dict(w_in=1, w_out=0, w_q=0, w_kv=1, w_xo=0)
    full_shape = dict(w_in=(D_MODEL, IN_DIM), w_out=(MIX_DIM, D_MODEL), w_q=(D_MODEL, D_MODEL),
                      w_kv=(D_MODEL, 2 * D_MODEL), w_xo=(D_MODEL, D_MODEL))
    shard_size = {k: big_w[k].shape[shard_axis[k]] for k in big_names}
    others = big_names[1:]
    gather_spec = lambda k: (full_shape[k], shard_axis[k], shard_size[k])
    reduce_spec = lambda k: (shard_axis[k], shard_size[k])
    ws = gm_ws[0]
    bs_b = jnp.broadcast_to(gm_bs[0][:, :, None], (8, CHUNK, CHUNK))
    g_f = norm_final_g.reshape(1, D_MODEL)

    px, py = lax.axis_index("x"), lax.axis_index("y")
    chip_order = jnp.stack([2 * px + py, 2 * (1 - px) + py, 2 * px + 1 - py, 2 * (1 - px) + 1 - py]).astype(jnp.int32)
    (proj, h1, h1t, w_in_full), ((w_out_full, conv_g),) = _inproj(
        x2, norm_mix_g, big_w["w_in"].astype(BF16), chip_order,
        plans=[_gather_plan([big_w["w_out"].astype(BF16), pad_taps(conv_w)[None]], [gather_spec("w_out"), ((N_DEV, 8, CHUNK), 0, 1)])])
    conv_full = conv_g.transpose(1, 0, 2).reshape(8, D_MODEL)
    attn_w = ["w_q", "w_kv", "w_xo"]
    cat, (gathered,) = _mixer_fwd(
        proj, conv_full, gm_ln_g, gm_ln_b, ws, bs_b,
        plans=[_gather_plan([big_w[k].astype(BF16) for k in attn_w], [gather_spec(k) for k in attn_w])])
    wf = dict(zip(attn_w, gathered), w_out=w_out_full)
    k, vv, mt = _kv_fwd(mem[0], norm_mem_g, wf["w_kv"])
    (dcat, actt, cot, dk, dvv, d_gf, d_gx, sq_err) = _attn(
        x2, cat, target, wf["w_out"], wf["w_q"], wf["w_xo"], k, vv, norm_x_g, g_f)

    def pair_sums(names, grads, got):
        return [_pair_sum("pair_sum_" + k, grads[k], got[i], *reduce_spec(k), core) for i, k in enumerate(names)]

    sibling_plan = lambda names, grads: _sibling_plan([grads[k] for k in names], [reduce_spec(k) for k in names])
    early = ["w_kv", "w_out", "w_q", "w_xo"]
    big_g, big_gb = {}, {}
    big_g["w_kv"], big_gb["w_kv"], d_gmem = _kv_bwd(mem[0], wf["w_kv"], mt, dk, dvv)
    dws, (got_kv,) = _dw_attn(actt, cot, plans=[sibling_plan(early[:1], big_gb)])
    for name, (f32, bf16) in zip(early[1:], dws):
        big_g[name], big_gb[name] = f32, bf16
    pair_kv, (got_rest,) = _pair_sum("pair_sum_w_kv", big_g["w_kv"], got_kv[0], *reduce_spec("w_kv"), core,
                                     plans=[sibling_plan(early[1:], big_gb)])
    (dproj, d_cw, d_lng, d_lnb, d_ws, d_bs), (parts_early,) = _mixer_bwd(
        proj, dcat, conv_full, gm_ln_g, gm_ln_b, ws, bs_b,
        plans=[_chip_plan([pair_kv] + pair_sums(early[1:], big_g, got_rest))])
    half = D_MODEL // 2
    in_spec = reduce_spec("w_in")
    (dw_top, dwb_top), _ = _matmul_f32("dw_in_top", h1t, dproj, rows=(0, half))
    (dw_bot, dwb_bot), (got_top, (ws_all,)) = _matmul_f32(
        "dw_in_bot", h1t, dproj, rows=(half, half),
        plans=[_sibling_plan([dwb_top], [in_spec]),
               _gather_plan([d_ws.reshape(1, 8 * CHUNK, CHUNK).astype(BF16)], [((N_DEV, 8 * CHUNK, CHUNK), 0, 1)])])
    pair_top, ((got_bot,),) = _pair_sum("pair_sum_w_in_top", dw_top, got_top[0], *in_spec, core,
                                        plans=[_sibling_plan([dwb_bot], [in_spec])])
    pair_bot = _pair_sum("pair_sum_w_in_bot", dw_bot, got_bot, *in_spec, core)
    chip_shapes = [p.shape for p in (pair_top, pair_bot)]
    chips_in_flight, token = _split_start("grad_chip_exchange_start", _chip_routes, [pair_top, pair_bot], chip_shapes)
    n_tiles = s // min(TM_DH1, s)
    (grad_x, d_gmix), _ = _dh1("dh1", dproj, w_in_full, x2, cot, norm_mix_g, (0, n_tiles), after=[token])
    vec_names = ["norm_mix_g", "gm_ln_g", "gm_ln_b", "gm_bs", "norm_x_g", "norm_mem_g", "norm_final_g"]
    vec_g = [d_gmix, d_lng, d_lnb, d_bs, d_gx, d_gmem, d_gf]
    small = jnp.concatenate(vec_g + [sq_err, d_cw], axis=0)[None]
    small_in_flight, _ = _split_start("small_all_gather_start", _gather_routes, [small], [(N_DEV, 16, D_MODEL)])
    flat_ws = lambda t: t.reshape(8 * CHUNK, CHUNK)
    updated, _ = _adamw(
        "adamw_early",
        [(big_w[k], big_m[k], big_v[k], got_k) for k, got_k in zip(early, parts_early)]
        + [(flat_ws(ws), flat_ws(m_gm_ws), flat_ws(v_gm_ws), ws_all)], steps=STEPS_ADAMW // 2)
    big_out = dict(zip(early, updated[:4]))
    ws_out = updated[4]
    (pair_top, pair_bot), (land_top, land_bot) = _split_wait(
        "grad_chip_exchange_wait", _chip_routes, chips_in_flight, [d_gmix] + [u[0] for u in updated])
    my_chip = (2 * px + py).astype(jnp.int32).reshape(1)
    (big_out["w_in"],), _ = _adamw(
        "adamw_w_in", [(big_w["w_in"], big_m["w_in"], big_v["w_in"], [land_top, land_bot], [pair_top, pair_bot])],
        steps=STEPS_ADAMW, own_slot=my_chip)
    (small,), (small_all,) = _split_wait("small_all_gather_wait", _gather_routes, small_in_flight, [big_out["w_in"][0]])
    row = lambda t: t.reshape(1, D_MODEL)
    vec_out, conv_out, loss_row = _update_small(
        dev.astype(jnp.int32).reshape(1),
        [norm_mix_g, gm_ln_g, gm_ln_b, row(gm_bs), norm_x_g, norm_mem_g, row(norm_final_g)],
        [m_norm_mix_g, m_gm_ln_g, m_gm_ln_b, row(m_gm_bs), m_norm_x_g, m_norm_mem_g, row(m_norm_final_g)],
        [v_norm_mix_g, v_gm_ln_g, v_gm_ln_b, row(v_gm_bs), v_norm_x_g, v_norm_mem_g, row(v_norm_final_g)],
        conv_w, m_conv_w, v_conv_w, small, small_all)
    loss = loss_row[0, 0]

    def result(name, which):
        if name in big_out:
            return big_out[name][which][None]
        if name == "conv_w":
            return conv_out[which]
        if name == "gm_ws":
            return ws_out[which].reshape(1, 8, CHUNK, CHUNK)
        r = vec_out[vec_names.index(name)][which]
        if name == "gm_bs":
            return r.reshape(1, 8, CHUNK)
        if name == "norm_final_g":
            return r.reshape(D_MODEL)
        return r

    weights = ["norm_mix_g", "w_in", "conv_w", "gm_ln_g", "gm_ln_b", "gm_ws", "gm_bs", "w_out", "norm_x_g", "norm_mem_g", "w_q", "w_kv", "w_xo", "norm_final_g"]
    outs = [loss, grad_x[None]]
    for which in range(4):
        outs += [result(name, which) for name in weights]
    return tuple(outs)
```

```python
import math

import jax
import jax.numpy as jnp
from jax import lax
from jax.experimental import pallas as pl
from jax.experimental.pallas import tpu as pltpu

F32 = jnp.float32
BF16 = jnp.bfloat16
MESH = pl.DeviceIdType.MESH

D_MODEL = 1024
N_DEV = 8
N_CHIP = 4
CHUNK = 128
GROUPS = 7
IN_DIM = GROUPS * D_MODEL
SHARD_IN = IN_DIM // N_DEV
MIX_DIM = 2 * D_MODEL
X_HEADS = 4
X_HEAD_DIM = D_MODEL // X_HEADS
MEM_LEN = 256
EPS = 1e-6
GELU_K0 = math.sqrt(2.0 / math.pi)
GELU_K1 = 0.044715

ADAM_LR = 0.001
ADAM_B1 = 0.9
ADAM_B2 = 0.999
ADAM_EPS = 1e-08
ADAM_WD = 0.01
ADAM_STEP = 10

TM_INPROJ = 512
TM_ATTN = 512
SUB_ATTN = 256
TM_DH1 = 512
TM_DW = 512
TK_DW = 4096
TR_PAIR_SUM = 1024
BARRIER_ID = {"sibling": 1, "chips": 2, "all": 3}
STEPS_ADAMW = 4


def _dot(a, b):
    return lax.dot_general(a, b, (((1,), (0,)), ((), ())), preferred_element_type=F32)


def _dot_nt(a, b):
    return lax.dot_general(a, b, (((1,), (1,)), ((), ())), preferred_element_type=F32)


def _rows8(v):
    return jnp.sum(v.reshape(v.shape[0] // 8, 8, v.shape[1]), axis=0)


def _gelu(x, with_grad=True):
    x2 = x * x
    t = jnp.tanh(x * (GELU_K0 + (GELU_K0 * GELU_K1) * x2))
    half = 0.5 + 0.5 * t
    g = x * half
    if not with_grad:
        return g
    return g, half + ((0.5 * x) * (1.0 - t * t)) * (GELU_K0 + (3.0 * GELU_K0 * GELU_K1) * x2)


def _silu(z, with_grad=True):
    s = jax.nn.sigmoid(z)
    zs = z * s
    return (zs, s + zs * (1.0 - s)) if with_grad else zs


def _rms(v):
    r = lax.rsqrt(jnp.mean(v * v, axis=-1, keepdims=True) + EPS)
    return v * r, r


def _rms_bwd(dy_g, vh, r):
    return r * (dy_g - vh * jnp.mean(dy_g * vh, axis=-1, keepdims=True))


def _place():
    return lax.axis_index("x"), lax.axis_index("y"), lax.axis_index("c")


def _other_chips(x, y):
    return [(1 - x, y), (x, 1 - y), (1 - x, 1 - y)]


def _pow2_divisor(n):
    return n & (-n)


def _shard_view(ref, axis, size, d):
    start = d * size
    align = _pow2_divisor(size)
    if align > 1:
        start = pl.multiple_of(start, align)
    idx = [slice(None)] * len(ref.shape)
    idx[axis] = pl.ds(start, size)
    return ref.at[tuple(idx)]


class _Plan:
    def __init__(self, ins, out_shape, sems, start, finish, mid=None, aliases=None, peers="all"):
        self.ins, self.out_shape, self.sems = list(ins), list(out_shape), list(sems)
        self.start, self.mid, self.finish = start, mid, finish
        self.aliases = dict(aliases or {})
        self.peers = peers


def _gather_plan(shards, specs):
    n = len(shards)
    n_copy = 7

    def copies(ins, outs, sems):
        send_sems, recv_sems, local_sems = sems
        x, y, c = _place()
        me, sibling = (x, y, c), (x, y, 1 - c)
        chips = _other_chips(x, y)

        def block(a, p):
            _, axis, size = specs[a]
            return _shard_view(outs[a], axis, size, 4 * p[0] + 2 * p[1] + p[2])

        def copy(a, k, p, to, src=None):
            return pltpu.make_async_remote_copy(
                src_ref=block(a, p) if src is None else src,
                dst_ref=block(a, p),
                send_sem=send_sems.at[a * n_copy + k],
                recv_sem=recv_sems.at[a * n_copy + k],
                device_id=to,
                device_id_type=MESH,
            )

        def mine():
            return [pltpu.make_async_copy(ins[a], block(a, me), local_sems.at[a]) for a in range(n)]

        def first():
            return [cp for a in range(n) for cp in
                    [copy(a, 0, me, sibling, src=ins[a])] + [copy(a, 1 + j, me, (*chip, c), src=ins[a]) for j, chip in enumerate(chips)]]

        def landed():
            return [copy(a, 1 + j, (*chip, c), me) for j, chip in enumerate(chips) for a in range(n)]

        def passed():
            return [copy(a, 4 + j, (*chip, c), sibling) for j, chip in enumerate(chips) for a in range(n)]

        def last():
            return [cp for a in range(n) for cp in
                    [copy(a, 0, sibling, me)] + [copy(a, 4 + j, (*chip, 1 - c), me) for j, chip in enumerate(chips)]]

        return mine, first, landed, passed, last

    def start(ins, outs, sems):
        mine, first, _, _, _ = copies(ins, outs, sems)
        for cp in mine() + first():
            cp.start()

    def mid(ins, outs, sems):
        _, _, landed, passed, _ = copies(ins, outs, sems)
        for got, fwd in zip(landed(), passed()):
            got.wait_recv()
            fwd.start()

    def finish(ins, outs, sems):
        mine, first, _, passed, last = copies(ins, outs, sems)
        for cp in last():
            cp.wait_recv()
        for cp in first() + passed():
            cp.wait_send()
        for cp in mine():
            cp.wait()

    return _Plan(
        shards,
        [jax.ShapeDtypeStruct(full, s.dtype) for s, (full, _, _) in zip(shards, specs)],
        [pltpu.SemaphoreType.DMA((n * n_copy,)), pltpu.SemaphoreType.DMA((n * n_copy,)), pltpu.SemaphoreType.DMA((n,))],
        start, finish, mid,
    )


def _sibling_plan(grads, specs):
    n = len(grads)

    def shard_shape(a):
        axis, size = specs[a]
        shp = list(grads[a].shape)
        shp[axis] = size
        return tuple(shp)

    def copies(ins, outs, sems):
        send_sems, recv_sems = sems
        x, y, c = _place()
        return [
            pltpu.make_async_remote_copy(
                src_ref=_shard_view(ins[a], specs[a][0], specs[a][1], 2 * q + (1 - c)),
                dst_ref=outs[a].at[q],
                send_sem=send_sems.at[a * N_CHIP + q],
                recv_sem=recv_sems.at[a * N_CHIP + q],
                device_id=(x, y, 1 - c),
                device_id_type=MESH,
            )
            for a in range(n) for q in range(N_CHIP)
        ]

    def start(ins, outs, sems):
        for cp in copies(ins, outs, sems):
            cp.start()

    def finish(ins, outs, sems):
        for cp in copies(ins, outs, sems):
            cp.wait()

    return _Plan(
        grads,
        [jax.ShapeDtypeStruct((N_CHIP, *shard_shape(a)), grads[a].dtype) for a in range(n)],
        [pltpu.SemaphoreType.DMA((n * N_CHIP,)), pltpu.SemaphoreType.DMA((n * N_CHIP,))],
        start, finish, peers="sibling",
    )


def _chip_plan(parts, full_rows=None, row0=None, into=None):
    n = len(parts)
    full_rows = [p.shape[1] for p in parts] if full_rows is None else full_rows
    row0 = [0] * n if row0 is None else row0

    def copies(ins, outs, sems):
        send_sems, recv_sems, local_sems = sems
        x, y, c = _place()
        my_chip = 2 * x + y
        chips = _other_chips(x, y)

        def land(a, q):
            return outs[a].at[q, pl.ds(row0[a], parts[a].shape[1]), :]

        def mine():
            return [pltpu.make_async_copy(ins[a].at[my_chip], land(a, my_chip), local_sems.at[a]) for a in range(n)]

        def remote(sending):
            return [
                pltpu.make_async_remote_copy(
                    src_ref=ins[a].at[2 * chip[0] + chip[1]],
                    dst_ref=land(a, my_chip if sending else 2 * chip[0] + chip[1]),
                    send_sem=send_sems.at[a * 3 + j], recv_sem=recv_sems.at[a * 3 + j],
                    device_id=(*chip, c), device_id_type=MESH)
                for a in range(n) for j, chip in enumerate(chips)
            ]

        return mine, remote

    def start(ins, outs, sems):
        mine, remote = copies(ins, outs, sems)
        for cp in mine() + remote(True):
            cp.start()

    def finish(ins, outs, sems):
        mine, remote = copies(ins, outs, sems)
        for cp in remote(False):
            cp.wait_recv()
        for cp in remote(True):
            cp.wait_send()
        for cp in mine():
            cp.wait()

    return _Plan(
        list(parts) + (list(into) if into is not None else []),
        [jax.ShapeDtypeStruct((N_CHIP, full_rows[a], parts[a].shape[2]), parts[a].dtype) for a in range(n)],
        [pltpu.SemaphoreType.DMA((n * 3,)), pltpu.SemaphoreType.DMA((n * 3,)), pltpu.SemaphoreType.DMA((n,))],
        start, finish,
        aliases={n + a: a for a in range(n)} if into is not None else None, peers="chips",
    )


def _call(body, *, name, operands, out_shape, grid=(), in_specs=None, out_specs=None, scratch_shapes=(),
          plans=(), aliases=None, prefetch=(), start_after_body=False, pass_on_at=0.6, body_peers=None):
    operands, out_shape, scratch_shapes = list(operands), list(out_shape), list(scratch_shapes)
    n_pf, n_in, n_out, n_scr = len(prefetch), len(operands), len(out_shape), len(scratch_shapes)
    in_vmem = pl.BlockSpec(memory_space=pltpu.VMEM)
    in_hbm = pl.BlockSpec(memory_space=pl.ANY)
    in_specs = [in_vmem] * n_in if in_specs is None else list(in_specs)
    out_specs = [in_vmem] * n_out if out_specs is None else list(out_specs)
    io_alias = {n_pf + i: o for i, o in (aliases or {}).items()}
    pos_in, pos_out = n_pf + n_in, n_out
    for p in plans:
        for i, o in p.aliases.items():
            io_alias[pos_in + i] = pos_out + o
        pos_in += len(p.ins)
        pos_out += len(p.out_shape)
    steps = math.prod(grid) if grid else 1
    mid_step = min(steps - 1, int(steps * pass_on_at))
    patterns = {p.peers for p in plans} | ({body_peers} if body_peers else set())
    pattern = None if not patterns else patterns.pop() if len(patterns) == 1 else "all"

    def shake_hands():
        x, y, c = _place()
        peers = [(x, y, 1 - c)] if pattern in ("sibling", "all") else []
        peers += [(*chip, c) for chip in _other_chips(x, y)] if pattern in ("chips", "all") else []
        barrier = pltpu.get_barrier_semaphore()
        for peer in peers:
            pl.semaphore_signal(barrier, inc=1, device_id=peer, device_id_type=MESH)
        pl.semaphore_wait(barrier, len(peers))

    def wrapped(*refs):
        refs = list(refs)
        take = lambda k: [refs.pop(0) for _ in range(k)]
        pf = take(n_pf)
        ins, p_ins = take(n_in), [take(len(p.ins)) for p in plans]
        outs, p_outs = take(n_out), [take(len(p.out_shape)) for p in plans]
        scr, p_sems = take(n_scr), [take(len(p.sems)) for p in plans]
        step = 0
        for ax, g in enumerate(grid):
            step = step * g + pl.program_id(ax)

        def hook(kind, at):
            todo = [(getattr(p, kind), a, b, c) for p, a, b, c in zip(plans, p_ins, p_outs, p_sems) if getattr(p, kind)]

            def run():
                for fn, a, b, c in todo:
                    fn(a, b, c)

            if todo and grid:
                pl.when(step == at)(run)
            elif todo:
                run()

        if pattern and grid:
            pl.when(step == 0)(shake_hands)
        elif pattern:
            shake_hands()
        if not start_after_body:
            hook("start", 0)
        if body is not None:
            body(*pf, *ins, *outs, *scr)
        if start_after_body:
            hook("start", 0)
        hook("mid", mid_step)
        hook("finish", steps - 1)

    layout = dict(
        grid=grid,
        in_specs=in_specs + [in_hbm] * sum(len(p.ins) for p in plans),
        out_specs=out_specs + [in_hbm] * sum(len(p.out_shape) for p in plans),
        scratch_shapes=scratch_shapes + [s for p in plans for s in p.sems],
    )
    if n_pf:
        layout = dict(grid_spec=pltpu.PrefetchScalarGridSpec(num_scalar_prefetch=n_pf, **layout))
    res = pl.pallas_call(
        wrapped,
        name=name,
        out_shape=out_shape + [o for p in plans for o in p.out_shape],
        input_output_aliases=io_alias,
        compiler_params=pltpu.CompilerParams(
            dimension_semantics=("arbitrary",) * len(grid) if grid else None,
            collective_id=BARRIER_ID[pattern] if pattern else None),
        **layout,
    )(*prefetch, *operands, *[a for p in plans for a in p.ins])
    res = list(res)
    outs, plan_outs = res[:n_out], []
    pos = n_out
    for p in plans:
        plan_outs.append(res[pos : pos + len(p.out_shape)])
        pos += len(p.out_shape)
    return outs, plan_outs


def _inproj(x, g, w_shard, order, plans=()):
    s = x.shape[0]
    tm = min(TM_INPROJ, s)
    nt = s // tm
    wide = 2 * SHARD_IN
    fwd_at = nt // 2

    def body(order_ref, x_ref, g_ref, w_hbm, proj_ref, h1_ref, h1t_ref, wfull_hbm,
             w_scr, h1_scr, send_sems, recv_sems, local_sems):
        k, i = pl.program_id(0), pl.program_id(1)
        px, py, c = _place()
        me, sibling = (px, py, c), (px, py, 1 - c)
        chips = _other_chips(px, py)

        def half(slot, core):
            return w_scr.at[slot, :, pl.ds(pl.multiple_of(core * SHARD_IN, 128), SHARD_IN)]

        def copy(n, slot, core, to, src=None):
            return pltpu.make_async_remote_copy(
                src_ref=half(slot, core) if src is None else src, dst_ref=half(slot, core),
                send_sem=send_sems.at[n], recv_sem=recv_sems.at[n], device_id=to, device_id_type=MESH)

        def mine():
            return pltpu.make_async_copy(w_hbm, half(0, c), local_sems.at[0])

        def first():
            return [copy(0, 0, c, sibling, src=w_hbm)] + [copy(1 + j, 1 + j, c, (*chips[j], c), src=w_hbm) for j in range(2)]

        def relay():
            to = (c * px + (1 - c) * (1 - px), c * (1 - py) + (1 - c) * py, c)
            return pltpu.make_async_remote_copy(
                src_ref=half(2 - c, c), dst_ref=half(3, c), send_sem=send_sems.at[3], recv_sem=recv_sems.at[3],
                device_id=to, device_id_type=MESH)

        def save(slot):
            cols = pl.ds(pl.multiple_of(order_ref[slot] * wide, 128), wide)
            return pltpu.make_async_copy(w_scr.at[slot], wfull_hbm.at[:, cols], local_sems.at[1 + slot])

        @pl.when((k == 0) & (i == 0))
        def _():
            mine().start()
            for cp in first():
                cp.start()
            mine().wait()
            copy(0, 0, 1 - c, me).wait_recv()

        for j in range(3):
            @pl.when((k == j + 1) & (i == 0))
            def _():
                copy(4 + j, 1 + j, 1 - c, me).wait_recv()

        @pl.when((k == 3) & (i == 0))
        def _():
            for slot in range(N_CHIP):
                save(slot).start()

        @pl.when(k == 0)
        def _():
            xh, _ = _rms(x_ref[...])
            h = xh * g_ref[...]
            h1_ref[...] = h.astype(BF16)
            h1t_ref[...] = h.T.astype(BF16)
            h1_scr[pl.ds(pl.multiple_of(i * tm, tm), tm), :] = h.astype(BF16)

        proj_ref[...] = _dot(h1_scr[pl.ds(pl.multiple_of(i * tm, tm), tm), :], w_scr[k]).astype(BF16)

        for j in range(3):
            @pl.when((k == 0) & (i == nt - 1) if j < 2 else (k == 2) & (i == fwd_at))
            def _():
                copy(1 + j, 1 + j, c, me).wait_recv()
                copy(4 + j, 1 + j, c, sibling).start()
                if j < 2:
                    pl.when(c == 1 - j)(lambda: relay().start())

        @pl.when((k == 3) & (i == nt - 1))
        def _():
            for cp in first() + [relay()] + [copy(4 + j, 1 + j, c, sibling) for j in range(3)]:
                cp.wait_send()
            for slot in range(N_CHIP):
                save(slot).wait()

    once = lambda k, i, order_ref: (jnp.where(k == 0, i, nt - 1), 0)
    return _call(
        body,
        name="inproj",
        prefetch=[order],
        operands=[x, g, w_shard],
        grid=(N_CHIP, nt),
        in_specs=[
            pl.BlockSpec((tm, D_MODEL), once),
            pl.BlockSpec((1, D_MODEL), lambda k, i, order_ref: (0, 0)),
            pl.BlockSpec(memory_space=pl.ANY),
        ],
        out_specs=[
            pl.BlockSpec((tm, wide), lambda k, i, order_ref: (i, order_ref[k])),
            pl.BlockSpec((tm, D_MODEL), once),
            pl.BlockSpec((D_MODEL, tm), lambda k, i, order_ref: (0, jnp.where(k == 0, i, nt - 1))),
            pl.BlockSpec(memory_space=pl.ANY),
        ],
        out_shape=[
            jax.ShapeDtypeStruct((s, IN_DIM), BF16),
            jax.ShapeDtypeStruct((s, D_MODEL), BF16),
            jax.ShapeDtypeStruct((D_MODEL, s), BF16),
            jax.ShapeDtypeStruct((D_MODEL, IN_DIM), BF16),
        ],
        scratch_shapes=[
            pltpu.VMEM((N_CHIP, D_MODEL, wide), BF16),
            pltpu.VMEM((s, D_MODEL), BF16),
            pltpu.SemaphoreType.DMA((7,)),
            pltpu.SemaphoreType.DMA((7,)),
            pltpu.SemaphoreType.DMA((1 + N_CHIP,)),
        ],
        plans=plans,
        start_after_body=True,
        pass_on_at=0.8,
        body_peers="all",
    )


def _rows_from_above(v, above):
    row = lax.broadcasted_iota(jnp.int32, (8, v.shape[1]), 0)
    r1, r2 = pltpu.roll(v, 1, 0), pltpu.roll(v, 2, 0)
    top1 = jnp.where(row == 0, above[7:8], r1[0:8])
    top2 = jnp.where(row == 0, above[6:7], jnp.where(row == 1, above[7:8], r2[0:8]))
    return jnp.concatenate([top1, r1[8:]], axis=0), jnp.concatenate([top2, r2[8:]], axis=0)


def _rows_from_below(v, below):
    n = v.shape[0]
    row = lax.broadcasted_iota(jnp.int32, (8, v.shape[1]), 0)
    r1, r2 = pltpu.roll(v, n - 1, 0), pltpu.roll(v, n - 2, 0)
    end1 = jnp.where(row == 7, below[0:1], r1[n - 8 :])
    end2 = jnp.where(row == 6, below[0:1], jnp.where(row == 7, below[1:2], r2[n - 8 :]))
    return jnp.concatenate([r1[: n - 8], end1], axis=0), jnp.concatenate([r2[: n - 8], end2], axis=0)


def _causal_mask():
    row = lax.broadcasted_iota(jnp.int32, (CHUNK, CHUNK), 0)
    col = lax.broadcasted_iota(jnp.int32, (CHUNK, CHUNK), 1)
    return row, row >= col


def _mixer_fwd(proj, conv_w, ln_g, ln_b, ws, bs_b, plans=()):
    s = proj.shape[0]
    nt = s // CHUNK

    def body(proj_ref, cw_ref, lng_ref, lnb_ref, ws_ref, bsb_ref, cat_ref, pcar_ref):
        @pl.when(pl.program_id(0) == 0)
        def _():
            pcar_ref[...] = jnp.zeros_like(pcar_ref)

        row, tril = _causal_mask()
        for j in range(8):
            cs = slice(CHUNK * j, CHUNK * (j + 1))

            def grp(k):
                return proj_ref[:, k * D_MODEL + CHUNK * j : k * D_MODEL + CHUNK * (j + 1)].astype(F32)

            w = cw_ref[:, cs]
            p = grp(1) * grp(2)
            p1, p2 = _rows_from_above(p, pcar_ref[:, cs])
            pcar_ref[:, cs] = p[CHUNK - 8 :]
            cv = w[0:1] * p2 + w[1:2] * p1 + w[2:3] * p
            sa = _silu(grp(3), with_grad=False)
            cat_ref[:, cs] = ((grp(0) * cv) * sa).astype(BF16)
            gu = _gelu(grp(4), with_grad=False)
            gv = _gelu(grp(5), with_grad=False)
            dv = gv - jnp.mean(gv, axis=-1, keepdims=True)
            vn = dv * lax.rsqrt(jnp.mean(dv * dv, axis=-1, keepdims=True) + EPS)
            vn = vn * lng_ref[:, cs] + lnb_ref[:, cs]
            wc = jnp.where(tril, ws_ref[j], 0.0).astype(BF16)
            sp = _dot(wc, vn.astype(BF16)) + bsb_ref[j]
            sb = _silu(grp(6), with_grad=False)
            cat_ref[:, D_MODEL + CHUNK * j : D_MODEL + CHUNK * (j + 1)] = ((gu * sp) * sb).astype(BF16)

    whole = lambda shape: pl.BlockSpec(shape, lambda i: (0,) * len(shape))
    (cat,), plan_outs = _call(
        body,
        name="mixer_fwd",
        operands=[proj, conv_w, ln_g, ln_b, ws, bs_b],
        grid=(nt,),
        in_specs=[
            pl.BlockSpec((CHUNK, IN_DIM), lambda i: (i, 0)),
            whole((8, D_MODEL)),
            whole((1, D_MODEL)),
            whole((1, D_MODEL)),
            whole((8, CHUNK, CHUNK)),
            whole((8, CHUNK, CHUNK)),
        ],
        out_specs=[pl.BlockSpec((CHUNK, MIX_DIM), lambda i: (i, 0))],
        out_shape=[jax.ShapeDtypeStruct((s, MIX_DIM), BF16)],
        scratch_shapes=[pltpu.VMEM((8, D_MODEL), F32)],
        plans=plans,
        pass_on_at=0.85,
    )
    return cat, plan_outs


def _mixer_bwd(proj, dcat, conv_w, ln_g, ln_b, ws, bs_b, plans=()):
    s = proj.shape[0]
    nt = s // CHUNK

    def body(proj_ref, halo_ref, dcat_ref, cw_ref, lng_ref, lnb_ref, ws_ref, bsb_ref,
             dproj_ref, dcw_ref, dlng_ref, dlnb_ref, dws_ref, dbs_ref,
             car_ref, acc_cw, acc_lng, acc_lnb, acc_bs, wc_scr, wct_scr):
        i = pl.program_id(0)
        tile = nt - 1 - i
        row, tril = _causal_mask()

        @pl.when(i == 0)
        def _():
            for j in range(8):
                wc = jnp.where(tril, ws_ref[j], 0.0)
                wc_scr[j] = wc.astype(BF16)
                wct_scr[j] = wc.T.astype(BF16)
            car_ref[...] = jnp.zeros_like(car_ref)
            acc_cw[...] = jnp.zeros_like(acc_cw)
            acc_lng[...] = jnp.zeros_like(acc_lng)
            acc_lnb[...] = jnp.zeros_like(acc_lnb)
            acc_bs[...] = jnp.zeros_like(acc_bs)
            dws_ref[...] = jnp.zeros_like(dws_ref)

        has_prev = jnp.where(tile > 0, 1.0, 0.0).astype(F32)
        for j in range(8):
            cs = slice(CHUNK * j, CHUNK * (j + 1))

            def col(k):
                return slice(k * D_MODEL + CHUNK * j, k * D_MODEL + CHUNK * (j + 1))

            def grp(k):
                return proj_ref[:, col(k)].astype(F32)

            w = cw_ref[:, cs]
            gb, gc, xa = grp(0), grp(1), grp(2)
            p = gc * xa
            pprev = halo_ref[:, col(1)].astype(F32) * halo_ref[:, col(2)].astype(F32) * has_prev
            p1, p2 = _rows_from_above(p, pprev[8:16])
            cv = w[0:1] * p2 + w[1:2] * p1 + w[2:3] * p
            za = grp(3)
            sa, dsa = _silu(za)
            da = dcat_ref[:, cs].astype(F32)
            da_gb = da * gb
            dproj_ref[:, col(0)] = ((da * sa) * cv).astype(BF16)
            dproj_ref[:, col(3)] = ((da_gb * cv) * dsa).astype(BF16)
            dcv = da_gb * sa
            d1, d2 = _rows_from_below(dcv, car_ref[:, cs])
            car_ref[:, cs] = dcv[0:8]
            dp = w[2:3] * dcv + w[1:2] * d1 + w[0:1] * d2
            dproj_ref[:, col(1)] = (dp * xa).astype(BF16)
            dproj_ref[:, col(2)] = (dp * gc).astype(BF16)
            acc_cw[0, :, cs] += _rows8(dcv * p2)
            acc_cw[1, :, cs] += _rows8(dcv * p1)
            acc_cw[2, :, cs] += _rows8(dcv * p)
            gu, dgu = _gelu(grp(4))
            gv, dgv = _gelu(grp(5))
            dv = gv - jnp.mean(gv, axis=-1, keepdims=True)
            rstd = lax.rsqrt(jnp.mean(dv * dv, axis=-1, keepdims=True) + EPS)
            vnh = dv * rstd
            lng = lng_ref[:, cs]
            vnb = (vnh * lng + lnb_ref[:, cs]).astype(BF16)
            sp = _dot(wc_scr[j], vnb) + bsb_ref[j]
            zb = grp(6)
            sb, dsb = _silu(zb)
            db = dcat_ref[:, D_MODEL + CHUNK * j : D_MODEL + CHUNK * (j + 1)].astype(F32)
            db_sp, db_gu = db * sp, db * gu
            dproj_ref[:, col(4)] = ((db_sp * sb) * dgu).astype(BF16)
            dproj_ref[:, col(6)] = ((db_sp * gu) * dsb).astype(BF16)
            dsp = db_gu * sb
            acc_bs[j] += dsp
            dspb = dsp.astype(BF16)
            dws_ref[j] += _dot_nt(dspb, vnb)
            dvn = _dot(wct_scr[j], dspb)
            acc_lnb[:, cs] += _rows8(dvn)
            acc_lng[:, cs] += _rows8(dvn * vnh)
            dvh = dvn * lng
            dgvv = rstd * (dvh - jnp.mean(dvh, axis=-1, keepdims=True) - vnh * jnp.mean(dvh * vnh, axis=-1, keepdims=True))
            dproj_ref[:, col(5)] = (dgvv * dgv).astype(BF16)

        @pl.when(i == nt - 1)
        def _():
            dlng_ref[...] = jnp.sum(acc_lng[...], axis=0, keepdims=True)
            dlnb_ref[...] = jnp.sum(acc_lnb[...], axis=0, keepdims=True)
            dcw_ref[...] = jnp.zeros_like(dcw_ref)
            for k in range(3):
                dcw_ref[k : k + 1, :] = jnp.sum(acc_cw[k], axis=0, keepdims=True)
            for j in range(8):
                dws_ref[j] = jnp.where(tril, dws_ref[j], 0.0)
                dbs_ref[:, CHUNK * j : CHUNK * (j + 1)] = jnp.sum(acc_bs[j].T, axis=0, keepdims=True)

    whole = lambda shape: pl.BlockSpec(shape, lambda i: (0,) * len(shape))
    halo_rows = 16
    per = CHUNK // halo_rows
    return _call(
        body,
        name="mixer_bwd",
        operands=[proj, proj, dcat, conv_w, ln_g, ln_b, ws, bs_b],
        plans=plans,
        grid=(nt,),
        in_specs=[
            pl.BlockSpec((CHUNK, IN_DIM), lambda i: (nt - 1 - i, 0)),
            pl.BlockSpec((halo_rows, IN_DIM), lambda i: (jnp.maximum((nt - 1 - i) * per - 1, 0), 0)),
            pl.BlockSpec((CHUNK, MIX_DIM), lambda i: (nt - 1 - i, 0)),
            whole((8, D_MODEL)),
            whole((1, D_MODEL)),
            whole((1, D_MODEL)),
            whole((8, CHUNK, CHUNK)),
            whole((8, CHUNK, CHUNK)),
        ],
        out_specs=[
            pl.BlockSpec((CHUNK, IN_DIM), lambda i: (nt - 1 - i, 0)),
            whole((8, D_MODEL)),
            whole((1, D_MODEL)),
            whole((1, D_MODEL)),
            whole((8, CHUNK, CHUNK)),
            whole((1, D_MODEL)),
        ],
        out_shape=[
            jax.ShapeDtypeStruct((s, IN_DIM), BF16),
            jax.ShapeDtypeStruct((8, D_MODEL), F32),
            jax.ShapeDtypeStruct((1, D_MODEL), F32),
            jax.ShapeDtypeStruct((1, D_MODEL), F32),
            jax.ShapeDtypeStruct((8, CHUNK, CHUNK), F32),
            jax.ShapeDtypeStruct((1, D_MODEL), F32),
        ],
        scratch_shapes=[
            pltpu.VMEM((8, D_MODEL), F32),
            pltpu.VMEM((3, 8, D_MODEL), F32),
            pltpu.VMEM((8, D_MODEL), F32),
            pltpu.VMEM((8, D_MODEL), F32),
            pltpu.VMEM((8, CHUNK, CHUNK), F32),
            pltpu.VMEM((8, CHUNK, CHUNK), BF16),
            pltpu.VMEM((8, CHUNK, CHUNK), BF16),
        ],
    )


def _kv_fwd(mem, g_mem, w_kv):
    def body(mem_ref, g_ref, w_ref, k_ref, vv_ref, mt_ref):
        mh, _ = _rms(mem_ref[...])
        m = mh * g_ref[...]
        kv = _dot(m.astype(BF16), w_ref[...])
        k_ref[...] = kv[:, :D_MODEL].astype(BF16)
        vv_ref[...] = kv[:, D_MODEL:].astype(BF16)
        mt_ref[...] = m.T.astype(BF16)

    return pl.pallas_call(
        body,
        name="kv_fwd",
        out_shape=[
            jax.ShapeDtypeStruct((MEM_LEN, D_MODEL), BF16),
            jax.ShapeDtypeStruct((MEM_LEN, D_MODEL), BF16),
            jax.ShapeDtypeStruct((D_MODEL, MEM_LEN), BF16),
        ],
    )(mem, g_mem, w_kv)


def _kv_bwd(mem, w_kv, mt, dk, dvv):
    def body(mem_ref, w_ref, mt_ref, dk_ref, dvv_ref, dw_ref, dwb_ref, dg_ref):
        dkv = jnp.concatenate([dk_ref[...], dvv_ref[...]], axis=1).astype(BF16)
        dw = _dot(mt_ref[...], dkv)
        dw_ref[...] = dw
        dwb_ref[...] = dw.astype(BF16)
        dm = _dot_nt(dkv, w_ref[...])
        mh, _ = _rms(mem_ref[...])
        dg_ref[...] = jnp.sum(dm * mh, axis=0, keepdims=True)

    return pl.pallas_call(
        body,
        name="kv_bwd",
        out_shape=[
            jax.ShapeDtypeStruct((D_MODEL, 2 * D_MODEL), F32),
            jax.ShapeDtypeStruct((D_MODEL, 2 * D_MODEL), BF16),
            jax.ShapeDtypeStruct((1, D_MODEL), F32),
        ],
    )(mem, w_kv, mt, dk, dvv)


def _attn(x, cat, target, w_out, w_q, w_xo, k, vv, g_x, g_f):
    s = x.shape[0]
    tm = min(TM_ATTN, s)
    sub = min(SUB_ATTN, tm)
    nt = s // tm
    scale = 1.0 / math.sqrt(X_HEAD_DIM)

    def body(x_ref, cat_ref, t_ref, wout_ref, wq_ref, wxo_ref, k_ref, vv_ref, gx_ref, gf_ref,
             dcat_ref, actt_ref, cot_ref,
             dk_ref, dvv_ref, dgf_ref, dgx_ref, loss_ref,
             p_scr, q_scr, o_scr, dq_scr, acc_gf, acc_gx, acc_loss):
        i = pl.program_id(0)

        @pl.when(i == 0)
        def _():
            dk_ref[...] = jnp.zeros_like(dk_ref)
            dvv_ref[...] = jnp.zeros_like(dvv_ref)
            acc_gf[...] = jnp.zeros_like(acc_gf)
            acc_gx[...] = jnp.zeros_like(acc_gx)
            acc_loss[...] = jnp.zeros_like(acc_loss)

        subs = [slice(u * sub, (u + 1) * sub) for u in range(tm // sub)]
        heads = [slice(X_HEAD_DIM * h, X_HEAD_DIM * (h + 1)) for h in range(X_HEADS)]
        gx, gf = gx_ref[...], gf_ref[...]
        x1, x1h, r2 = [], [], []
        for rs in subs:
            cat = cat_ref[rs, :]
            actt_ref[0:MIX_DIM, rs] = cat.astype(F32).T.astype(BF16)
            v = x_ref[rs, :] + _dot(cat, wout_ref[...])
            vh, r = _rms(v)
            h2 = vh * gx
            actt_ref[MIX_DIM : MIX_DIM + D_MODEL, rs] = h2.T.astype(BF16)
            q_scr[rs, :] = _dot(h2.astype(BF16), wq_ref[...]).astype(BF16)
            x1.append(v), x1h.append(vh), r2.append(r)
        for rs in subs:
            for hs in heads:
                sc = _dot_nt(q_scr[rs, hs], k_ref[:, hs]) * scale
                e = jnp.exp(sc - jnp.max(sc, axis=-1, keepdims=True))
                p = e / jnp.sum(e, axis=-1, keepdims=True)
                p_scr[rs, hs] = p
                o_scr[rs, hs] = _dot(p.astype(BF16), vv_ref[:, hs])
        dx2, sq, d_gf = [], None, None
        for u, rs in enumerate(subs):
            o = o_scr[rs, :]
            actt_ref[MIX_DIM + D_MODEL :, rs] = o.T.astype(BF16)
            x2 = x1[u] + _dot(o.astype(BF16), wxo_ref[...])
            x2h, r3 = _rms(x2)
            err = x2h * gf - t_ref[rs, :]
            dy = err * (1.0 / D_MODEL)
            sq = _rows8(err * err) if sq is None else sq + _rows8(err * err)
            d_gf = _rows8(dy * x2h) if d_gf is None else d_gf + _rows8(dy * x2h)
            dx2.append(_rms_bwd(dy * gf, x2h, r3))
        acc_loss[...] += sq
        acc_gf[...] += d_gf
        d_vv, d_k = [None] * X_HEADS, [None] * X_HEADS
        for u, rs in enumerate(subs):
            dx2b = dx2[u].astype(BF16)
            cot_ref[2, rs, :] = dx2b
            dob = _dot_nt(dx2b, wxo_ref[...]).astype(BF16)
            for h, hs in enumerate(heads):
                p = p_scr[rs, hs]
                dp = _dot_nt(dob[:, hs], vv_ref[:, hs])
                ds = p * (dp - jnp.sum(dp * p, axis=-1, keepdims=True)) * scale
                dq_scr[rs, hs] = _dot(ds.astype(BF16), k_ref[:, hs]).astype(BF16)
                dvv_h = _dot(p.T.astype(BF16), dob[:, hs])
                dk_h = _dot(ds.T.astype(BF16), q_scr[rs, hs])
                d_vv[h] = dvv_h if d_vv[h] is None else d_vv[h] + dvv_h
                d_k[h] = dk_h if d_k[h] is None else d_k[h] + dk_h
        for h, hs in enumerate(heads):
            dvv_ref[:, hs] += d_vv[h]
            dk_ref[:, hs] += d_k[h]
        d_gx = None
        for u, rs in enumerate(subs):
            dqb = dq_scr[rs, :]
            cot_ref[1, rs, :] = dqb
            dh2 = _dot_nt(dqb, wq_ref[...])
            d_gx = _rows8(dh2 * x1h[u]) if d_gx is None else d_gx + _rows8(dh2 * x1h[u])
            dx1 = dx2[u] + _rms_bwd(dh2 * gx, x1h[u], r2[u])
            dx1b = dx1.astype(BF16)
            cot_ref[0, rs, :] = dx1b
            dcat_ref[rs, :] = _dot_nt(dx1b, wout_ref[...]).astype(BF16)
        acc_gx[...] += d_gx

        @pl.when(i == nt - 1)
        def _():
            dgf_ref[...] = jnp.sum(acc_gf[...], axis=0, keepdims=True)
            dgx_ref[...] = jnp.sum(acc_gx[...], axis=0, keepdims=True)
            loss_ref[...] = jnp.sum(acc_loss[...], axis=0, keepdims=True)

    whole = lambda shape: pl.BlockSpec(shape, lambda i: (0,) * len(shape))
    once = lambda shape: pl.BlockSpec(shape, lambda i: (0,) * len(shape), pipeline_mode=pl.Buffered(1))
    rows = lambda width: pl.BlockSpec((tm, width), lambda i: (i, 0))
    cols = lambda height: pl.BlockSpec((height, tm), lambda i: (0, i))
    vec = jax.ShapeDtypeStruct((1, D_MODEL), F32)
    return pl.pallas_call(
        body,
        name="attn",
        grid=(nt,),
        in_specs=[
            rows(D_MODEL), rows(MIX_DIM), rows(D_MODEL),
            once((MIX_DIM, D_MODEL)), once((D_MODEL, D_MODEL)), once((D_MODEL, D_MODEL)),
            once((MEM_LEN, D_MODEL)), once((MEM_LEN, D_MODEL)),
            whole((1, D_MODEL)), whole((1, D_MODEL)),
        ],
        out_specs=[
            rows(MIX_DIM), cols(2 * MIX_DIM),
            pl.BlockSpec((3, tm, D_MODEL), lambda i: (0, i, 0)),
            whole((MEM_LEN, D_MODEL)), whole((MEM_LEN, D_MODEL)),
            whole((1, D_MODEL)), whole((1, D_MODEL)), whole((1, D_MODEL)),
        ],
        out_shape=[
            jax.ShapeDtypeStruct((s, MIX_DIM), BF16),
            jax.ShapeDtypeStruct((2 * MIX_DIM, s), BF16),
            jax.ShapeDtypeStruct((3, s, D_MODEL), BF16),
            jax.ShapeDtypeStruct((MEM_LEN, D_MODEL), F32),
            jax.ShapeDtypeStruct((MEM_LEN, D_MODEL), F32),
            vec, vec, vec,
        ],
        scratch_shapes=[
            pltpu.VMEM((tm, D_MODEL), F32),
            pltpu.VMEM((tm, D_MODEL), BF16),
            pltpu.VMEM((tm, D_MODEL), F32),
            pltpu.VMEM((tm, D_MODEL), BF16),
            pltpu.VMEM((8, D_MODEL), F32),
            pltpu.VMEM((8, D_MODEL), F32),
            pltpu.VMEM((8, D_MODEL), F32),
        ],
        compiler_params=pltpu.CompilerParams(dimension_semantics=("arbitrary",)),
    )(x, cat, target, w_out, w_q, w_xo, k, vv, g_x, g_f)


def _matmul_f32(name, a, b, rows=None, plans=()):
    first, m = (0, a.shape[0]) if rows is None else rows
    kk = a.shape[1]
    n = b.shape[1]
    tm, tn, tk = min(m, TM_DW), min(n, D_MODEL), min(kk, TK_DW)
    nk = kk // tk
    i0 = first // tm

    def body(a_ref, b_ref, o_ref, ob_ref):
        prod = _dot(a_ref[...], b_ref[...])
        if nk == 1:
            o_ref[...] = prod
            ob_ref[...] = prod.astype(BF16)
            return
        kid = pl.program_id(2)

        @pl.when(kid == 0)
        def _():
            o_ref[...] = prod

        @pl.when(kid > 0)
        def _():
            o_ref[...] += prod

        @pl.when(kid == nk - 1)
        def _():
            ob_ref[...] = o_ref[...].astype(BF16)

    out_spec = pl.BlockSpec((tm, tn), lambda i, j, k: (i, j))
    outs, plan_outs = _call(
        body,
        name=name,
        operands=[a, b],
        grid=(m // tm, n // tn, nk),
        in_specs=[pl.BlockSpec((tm, tk), lambda i, j, k: (i + i0, k)), pl.BlockSpec((tk, tn), lambda i, j, k: (k, j))],
        out_specs=[out_spec, out_spec],
        out_shape=[jax.ShapeDtypeStruct((m, n), F32), jax.ShapeDtypeStruct((m, n), BF16)],
        plans=plans,
    )
    return outs, plan_outs


def _dw_attn(actt, cot, plans=()):
    s = actt.shape[1]
    tm = TM_DW
    counts = [MIX_DIM // tm, D_MODEL // tm, D_MODEL // tm]
    starts = [0, counts[0], counts[0] + counts[1]]
    which = lambda i: jnp.where(i < starts[1], 0, jnp.where(i < starts[2], 1, 2))

    def body(a_ref, b_ref, *outs):
        i = pl.program_id(0)
        prod = _dot(a_ref[...], b_ref[0])
        for m in range(3):
            @pl.when(which(i) == m)
            def _():
                outs[2 * m][...] = prod
                outs[2 * m + 1][...] = prod.astype(BF16)

    out_specs, out_shape = [], []
    for m in range(3):
        spec = pl.BlockSpec((tm, D_MODEL), lambda i, m=m: (jnp.clip(i - starts[m], 0, counts[m] - 1), 0))
        out_specs += [spec, spec]
        out_shape += [jax.ShapeDtypeStruct((counts[m] * tm, D_MODEL), F32), jax.ShapeDtypeStruct((counts[m] * tm, D_MODEL), BF16)]
    outs, plan_outs = _call(
        body,
        name="dw_attn",
        operands=[actt, cot],
        grid=(sum(counts),),
        in_specs=[pl.BlockSpec((tm, s), lambda i: (i, 0)), pl.BlockSpec((1, s, D_MODEL), lambda i: (which(i), 0, 0))],
        out_specs=out_specs,
        out_shape=out_shape,
        plans=plans,
    )
    return [outs[2 * m : 2 * m + 2] for m in range(3)], plan_outs


def _dh1(name, dproj, w_in, x, dx1, g, tiles, into=None, after=(), plans=()):
    s = x.shape[0]
    tm = min(TM_DH1, s)
    t0, nt = tiles[0], tiles[1] - tiles[0]

    def body(dp_ref, w_ref, x_ref, dx1_ref, g_ref, *rest):
        gx_ref, dg_ref, acc_g = rest[-3:]
        i = pl.program_id(0)

        @pl.when(i == 0)
        def _():
            acc_g[...] = jnp.zeros_like(acc_g)

        dh1 = _dot_nt(dp_ref[...], w_ref[...])
        xh, r = _rms(x_ref[...])
        acc_g[...] += _rows8(dh1 * xh)
        gx_ref[...] = dx1_ref[0].astype(F32) + _rms_bwd(dh1 * g_ref[...], xh, r)

        @pl.when(i == nt - 1)
        def _():
            dg_ref[...] = jnp.sum(acc_g[...], axis=0, keepdims=True)

    rows = pl.BlockSpec((tm, D_MODEL), lambda i: (i + t0, 0))
    return _call(
        body,
        name=name,
        operands=[dproj, w_in, x, dx1, g] + ([into] if into is not None else []) + list(after),
        grid=(nt,),
        in_specs=[
            pl.BlockSpec((tm, IN_DIM), lambda i: (i + t0, 0)),
            pl.BlockSpec((D_MODEL, IN_DIM), lambda i: (0, 0), pipeline_mode=pl.Buffered(1)),
            rows,
            pl.BlockSpec((1, tm, D_MODEL), lambda i: (0, i + t0, 0)),
            pl.BlockSpec((1, D_MODEL), lambda i: (0, 0)),
        ] + [pl.BlockSpec(memory_space=pl.ANY)] * ((into is not None) + len(after)),
        out_specs=[rows, pl.BlockSpec((1, D_MODEL), lambda i: (0, 0))],
        out_shape=[jax.ShapeDtypeStruct((s, D_MODEL), F32), jax.ShapeDtypeStruct((1, D_MODEL), F32)],
        scratch_shapes=[pltpu.VMEM((8, D_MODEL), F32)],
        plans=plans,
        aliases={5: 0} if into is not None else None,
    )


def _pair_sum(name, grad, got, axis, size, core, plans=()):
    shard = list(grad.shape)
    shard[axis] = size
    r, cdim = shard
    tr = min(r, TR_PAIR_SUM)
    nr = r // tr

    def body(core_ref, g_ref, got_ref, o_ref):
        o_ref[0] = (g_ref[...] + got_ref[0].astype(F32)).astype(BF16)

    if axis == 1:
        g_map = lambda q, t, core_ref: (t, 2 * q + core_ref[0])
    else:
        g_map = lambda q, t, core_ref: ((2 * q + core_ref[0]) * nr + t, 0)
    (out,), plan_outs = _call(
        body,
        name=name,
        prefetch=[core],
        operands=[grad, got],
        grid=(N_CHIP, nr),
        in_specs=[
            pl.BlockSpec((tr, cdim), g_map),
            pl.BlockSpec((1, tr, cdim), lambda q, t, core_ref: (q, t, 0)),
        ],
        out_specs=[pl.BlockSpec((1, tr, cdim), lambda q, t, core_ref: (q, t, 0))],
        out_shape=[jax.ShapeDtypeStruct((N_CHIP, r, cdim), BF16)],
        plans=plans,
    )
    return (out, plan_outs) if plans else out


def _adam_step(w, m, v, g):
    c1 = 1.0 / (1.0 - ADAM_B1**ADAM_STEP)
    c2 = 1.0 / (1.0 - ADAM_B2**ADAM_STEP)
    m_new = ADAM_B1 * m + (1.0 - ADAM_B1) * g
    v_new = ADAM_B2 * v + (1.0 - ADAM_B2) * (g * g)
    return -ADAM_LR * ((m_new * c1) / (jnp.sqrt(v_new * c2) + ADAM_EPS) + ADAM_WD * w), m_new, v_new


def _update_small(dev, vec_w, vec_m, vec_v, conv_w, conv_m, conv_v, small, landed):
    n_vec = len(vec_w)

    def body(dev_ref, *refs):
        refs = list(refs)
        take = lambda k: [refs.pop(0) for _ in range(k)]
        w_refs, m_refs, v_refs = take(n_vec), take(n_vec), take(n_vec)
        cw_ref, cm_ref, cv_ref, small_ref, land_ref, small_conv_ref, land_conv_ref = take(7)
        vec_outs, conv_outs, (loss_ref,) = take(4 * n_vec), take(4), take(1)

        def total(own_ref, others_ref):
            acc = None
            for d in range(N_DEV):
                part = jnp.where(dev_ref[0] == d, own_ref[0], others_ref[d])
                acc = part if acc is None else acc + part
            return acc

        g_all = total(small_ref, land_ref)
        for i in range(n_vec):
            g = g_all[i : i + 1]
            delta, m_new, v_new = _adam_step(w_refs[i][...], m_refs[i][...], v_refs[i][...], g)
            for ref, val in zip(vec_outs[4 * i : 4 * i + 4], (g, delta, m_new, v_new)):
                ref[...] = val
        g = total(small_conv_ref, land_conv_ref)[0:3]
        delta, m_new, v_new = _adam_step(cw_ref[0], cm_ref[0], cv_ref[0], g)
        for ref, val in zip(conv_outs, (g, delta, m_new, v_new)):
            ref[0] = val
        loss_ref[...] = jnp.broadcast_to(0.5 * jnp.sum(g_all[7:8], axis=-1, keepdims=True) / D_MODEL, loss_ref.shape)

    vec_spec = pl.BlockSpec((1, D_MODEL), lambda i, dev_ref: (0, 0))
    conv_spec = pl.BlockSpec((1, 3, CHUNK), lambda i, dev_ref: (0, 0, 0))
    vec_shape = jax.ShapeDtypeStruct((1, D_MODEL), F32)
    conv_shape = jax.ShapeDtypeStruct((1, 3, CHUNK), F32)
    outs, _ = _call(
        body,
        name="update_small",
        prefetch=[dev],
        operands=list(vec_w) + list(vec_m) + list(vec_v) + [conv_w, conv_m, conv_v, small, landed, small, landed],
        grid=(1,),
        in_specs=[vec_spec] * (3 * n_vec) + [conv_spec] * 3 + [
            pl.BlockSpec((1, 16, D_MODEL), lambda i, dev_ref: (0, 0, 0)),
            pl.BlockSpec((N_DEV, 16, D_MODEL), lambda i, dev_ref: (0, 0, 0)),
            pl.BlockSpec((1, 8, CHUNK), lambda i, dev_ref: (0, 1, dev_ref[0])),
            pl.BlockSpec((N_DEV, 8, CHUNK), lambda i, dev_ref: (0, 1, dev_ref[0])),
        ],
        out_specs=[vec_spec] * (4 * n_vec) + [conv_spec] * 4 + [pl.BlockSpec((1, CHUNK), lambda i, dev_ref: (0, 0))],
        out_shape=[vec_shape] * (4 * n_vec) + [conv_shape] * 4 + [jax.ShapeDtypeStruct((1, CHUNK), F32)],
    )
    return [outs[4 * i : 4 * i + 4] for i in range(n_vec)], outs[4 * n_vec : 4 * n_vec + 4], outs[-1]


def _adamw(name, groups, steps, own_slot=None, plans=()):
    lists = lambda p: list(p) if isinstance(p, (list, tuple)) else [p]
    groups = [(g[0], g[1], g[2], lists(g[3]), lists(g[4]) if len(g) > 4 else []) for g in groups]
    counts = [3 + len(g[3]) + len(g[4]) for g in groups]
    use_slot = own_slot is not None

    def first_tiles(w, parts):
        firsts, first = [], 0
        for p in parts:
            firsts.append(first)
            first += p.shape[1] // (w.shape[0] // steps)
        return firsts

    def update(t, slot_ref, firsts, w_ref, m_ref, v_ref, p_refs, o_refs, g_ref, d_ref, nm_ref, nv_ref):
        g = None
        for k, p_ref in enumerate(p_refs):
            gk = None
            for q in range(p_ref.shape[0]):
                part = p_ref[q].astype(F32)
                if o_refs:
                    part = jnp.where(slot_ref[0] == q, o_refs[k][0].astype(F32), part)
                gk = part if gk is None else gk + part
            g = gk if g is None else jnp.where(t >= firsts[k], gk, g)
        g_ref[...] = g
        d_ref[...], nm_ref[...], nv_ref[...] = _adam_step(w_ref[...], m_ref[...], v_ref[...], g)

    def body(*refs):
        refs = list(refs)
        slot_ref = refs.pop(0) if use_slot else None
        t = pl.program_id(0)
        ins, outs = refs[: sum(counts)], refs[sum(counts) :]
        for i, (w, _, _, parts, own) in enumerate(groups):
            mine = ins[sum(counts[:i]) : sum(counts[: i + 1])]
            update(t, slot_ref, first_tiles(w, parts), *mine[:3], mine[3 : 3 + len(parts)], mine[3 + len(parts) :],
                   *outs[4 * i : 4 * i + 4])

    operands, in_specs, out_specs, out_shape = [], [], [], []
    for w, m, v, parts, own in groups:
        r, cdim = w.shape
        tr = r // steps
        firsts = first_tiles(w, parts)

        def tile_of(k, t, firsts=firsts, parts=parts, tr=tr):
            return jnp.clip(t - firsts[k], 0, parts[k].shape[1] // tr - 1)

        blk = pl.BlockSpec((tr, cdim), lambda t, *_: (t, 0))
        operands += [w, m, v] + parts + own
        in_specs += [blk, blk, blk]
        in_specs += [pl.BlockSpec((p.shape[0], tr, cdim), lambda t, *_, k=k, tile_of=tile_of: (0, tile_of(k, t), 0)) for k, p in enumerate(parts)]
        in_specs += [pl.BlockSpec((1, tr, cdim), lambda t, slot_ref, k=k, tile_of=tile_of: (slot_ref[0], tile_of(k, t), 0)) for k in range(len(own))]
        out_specs += [blk] * 4
        out_shape += [jax.ShapeDtypeStruct((r, cdim), F32)] * 4
    outs, plan_outs = _call(
        body,
        name=name,
        prefetch=[own_slot] if use_slot else [],
        operands=operands,
        grid=(steps,),
        in_specs=in_specs,
        out_specs=out_specs,
        out_shape=out_shape,
        plans=plans,
    )
    return [outs[4 * i : 4 * i + 4] for i in range(len(groups))], plan_outs


def _chip_routes(n):
    x, y, c = _place()
    my_chip = 2 * x + y
    return [(a, (*chip, c), 2 * chip[0] + chip[1], my_chip, 2 * chip[0] + chip[1])
            for a in range(n) for chip in _other_chips(x, y)]


def _gather_routes(n):
    x, y, c = _place()
    flip = lambda v, f: 1 - v if f else v
    peers = [(flip(x, fx), flip(y, fy), flip(c, fc)) for fx in (0, 1) for fy in (0, 1) for fc in (0, 1) if fx + fy + fc]
    return [(0, p, 0, 4 * x + 2 * y + c, 4 * p[0] + 2 * p[1] + p[2]) for p in peers]


_chip_routes.copies_per_array = N_CHIP - 1
_gather_routes.copies_per_array = N_DEV - 1


def _split_start(name, routes, srcs, land_shapes):
    n = len(srcs)
    hbm = pl.BlockSpec(memory_space=pltpu.HBM)
    sem = pl.BlockSpec(memory_space=pltpu.SEMAPHORE)

    def body(*refs):
        src_refs, land_refs = refs[:n], refs[n : 2 * n]
        send_sems, recv_sems = refs[2 * n], refs[2 * n + 1]
        token = refs[-1]
        for k, (a, peer, src_slot, there, _) in enumerate(routes(n)):
            pltpu.make_async_remote_copy(
                src_ref=src_refs[a].at[src_slot], dst_ref=land_refs[a].at[there],
                send_sem=send_sems.at[k], recv_sem=recv_sems.at[k], device_id=peer, device_id_type=MESH).start()
        token[...] = jnp.zeros_like(token)

    in_hbm = [pltpu.with_memory_space_constraint(p, pltpu.HBM) for p in srcs]
    in_hbm += [pltpu.with_memory_space_constraint(lax.empty(shape, p.dtype), pltpu.HBM) for p, shape in zip(srcs, land_shapes)]
    n_copies = routes.copies_per_array * n
    res = pl.pallas_call(
        body,
        name=name,
        out_shape=[pltpu.SemaphoreType.DMA((n_copies,)), pltpu.SemaphoreType.DMA((n_copies,))]
        + [pltpu.HBM(p.shape, p.dtype) for p in srcs] + [pltpu.HBM(shape, p.dtype) for p, shape in zip(srcs, land_shapes)]
        + [jax.ShapeDtypeStruct((8, CHUNK), F32)],
        in_specs=[hbm] * (2 * n),
        out_specs=[sem, sem] + [hbm] * (2 * n) + [pl.BlockSpec(memory_space=pltpu.VMEM)],
        input_output_aliases={i: 2 + i for i in range(2 * n)},
        compiler_params=pltpu.CompilerParams(has_side_effects=pltpu.SideEffectType.DATAFLOW_SIDE_EFFECTING),
    )(*in_hbm)
    return res[:-1], res[-1]


def _split_wait(name, routes, in_flight, after):
    n = (len(in_flight) - 2) // 2
    hbm = pl.BlockSpec(memory_space=pltpu.HBM)
    sem = pl.BlockSpec(memory_space=pltpu.SEMAPHORE)
    send_sems, recv_sems, *bufs = in_flight

    def body(*refs):
        src_refs, land_refs = refs[:n], refs[n : 2 * n]
        send_ref, recv_ref = refs[2 * n], refs[2 * n + 1]
        for k, (a, peer, src_slot, _, here) in enumerate(routes(n)):
            cp = pltpu.make_async_remote_copy(
                src_ref=src_refs[a].at[src_slot], dst_ref=land_refs[a].at[here],
                send_sem=send_ref.at[k], recv_sem=recv_ref.at[k], device_id=peer, device_id_type=MESH)
            cp.wait_send()
            cp.wait_recv()

    res = pl.pallas_call(
        body,
        name=name,
        out_shape=[pltpu.HBM(b.shape, b.dtype) for b in bufs],
        in_specs=[hbm] * (2 * n) + [sem, sem] + [pl.BlockSpec(memory_space=pl.ANY)] * len(after),
        out_specs=[hbm] * (2 * n),
        input_output_aliases={i: i for i in range(2 * n)},
        compiler_params=pltpu.CompilerParams(has_side_effects=pltpu.SideEffectType.DATAFLOW_SIDE_EFFECTING),
    )(*bufs, send_sems, recv_sems, *after)
    return res[:n], res[n:]


def kernel(x, mem, norm_mix_g, w_in, conv_w, gm_ln_g, gm_ln_b, gm_ws, gm_bs, w_out, norm_x_g, norm_mem_g, w_q, w_kv, w_xo, norm_final_g, loss_target, m_norm_mix_g, m_w_in, m_conv_w, m_gm_ln_g, m_gm_ln_b, m_gm_ws, m_gm_bs, m_w_out, m_norm_x_g, m_norm_mem_g, m_w_q, m_w_kv, m_w_xo, m_norm_final_g, v_norm_mix_g, v_w_in, v_conv_w, v_gm_ln_g, v_gm_ln_b, v_gm_ws, v_gm_bs, v_w_out, v_norm_x_g, v_norm_mem_g, v_w_q, v_w_kv, v_w_xo, v_norm_final_g):
    s = x.shape[1]
    dev = 4 * lax.axis_index("x") + 2 * lax.axis_index("y") + lax.axis_index("c")
    core = lax.axis_index("c").astype(jnp.int32).reshape(1)
    x2 = x[0]
    target = loss_target[0]
    pad_taps = lambda t: jnp.pad(t[0], ((0, 5), (0, 0)))

    big_names = ["w_in", "w_out", "w_q", "w_kv", "w_xo"]
    big_w = dict(w_in=w_in[0], w_out=w_out[0], w_q=w_q[0], w_kv=w_kv[0], w_xo=w_xo[0])
    big_m = dict(w_in=m_w_in[0], w_out=m_w_out[0], w_q=m_w_q[0], w_kv=m_w_kv[0], w_xo=m_w_xo[0])
    big_v = dict(w_in=v_w_in[0], w_out=v_w_out[0], w_q=v_w_q[0], w_kv=v_w_kv[0], w_xo=v_w_xo[0])
    shard_axis = dict(w_in=1, w_out=0, w_q=0, w_kv=1, w_xo=0)
    full_shape = dict(w_in=(D_MODEL, IN_DIM), w_out=(MIX_DIM, D_MODEL), w_q=(D_MODEL, D_MODEL),
                      w_kv=(D_MODEL, 2 * D_MODEL), w_xo=(D_MODEL, D_MODEL))
    shard_size = {k: big_w[k].shape[shard_axis[k]] for k in big_names}
    others = big_names[1:]
    gather_spec = lambda k: (full_shape[k], shard_axis[k], shard_size[k])
    reduce_spec = lambda k: (shard_axis[k], shard_size[k])
    ws = gm_ws[0]
    bs_b = jnp.broadcast_to(gm_bs[0][:, :, None], (8, CHUNK, CHUNK))
    g_f = norm_final_g.reshape(1, D_MODEL)

    px, py = lax.axis_index("x"), lax.axis_index("y")
    chip_order = jnp.stack([2 * px + py, 2 * (1 - px) + py, 2 * px + 1 - py, 2 * (1 - px) + 1 - py]).astype(jnp.int32)
    (proj, h1, h1t, w_in_full), ((w_out_full, conv_g),) = _inproj(
        x2, norm_mix_g, big_w["w_in"].astype(BF16), chip_order,
        plans=[_gather_plan([big_w["w_out"].astype(BF16), pad_taps(conv_w)[None]], [gather_spec("w_out"), ((N_DEV, 8, CHUNK), 0, 1)])])
    conv_full = conv_g.transpose(1, 0, 2).reshape(8, D_MODEL)
    attn_w = ["w_q", "w_kv", "w_xo"]
    cat, (gathered,) = _mixer_fwd(
        proj, conv_full, gm_ln_g, gm_ln_b, ws, bs_b,
        plans=[_gather_plan([big_w[k].astype(BF16) for k in attn_w], [gather_spec(k) for k in attn_w])])
    wf = dict(zip(attn_w, gathered), w_out=w_out_full)
    k, vv, mt = _kv_fwd(mem[0], norm_mem_g, wf["w_kv"])
    (dcat, actt, cot, dk, dvv, d_gf, d_gx, sq_err) = _attn(
        x2, cat, target, wf["w_out"], wf["w_q"], wf["w_xo"], k, vv, norm_x_g, g_f)

    def pair_sums(names, grads, got):
        return [_pair_sum("pair_sum_" + k, grads[k], got[i], *reduce_spec(k), core) for i, k in enumerate(names)]

    sibling_plan = lambda names, grads: _sibling_plan([grads[k] for k in names], [reduce_spec(k) for k in names])
    early = ["w_kv", "w_out", "w_q", "w_xo"]
    big_g, big_gb = {}, {}
    big_g["w_kv"], big_gb["w_kv"], d_gmem = _kv_bwd(mem[0], wf["w_kv"], mt, dk, dvv)
    dws, (got_kv,) = _dw_attn(actt, cot, plans=[sibling_plan(early[:1], big_gb)])
    for name, (f32, bf16) in zip(early[1:], dws):
        big_g[name], big_gb[name] = f32, bf16
    pair_kv, (got_rest,) = _pair_sum("pair_sum_w_kv", big_g["w_kv"], got_kv[0], *reduce_spec("w_kv"), core,
                                     plans=[sibling_plan(early[1:], big_gb)])
    (dproj, d_cw, d_lng, d_lnb, d_ws, d_bs), (parts_early,) = _mixer_bwd(
        proj, dcat, conv_full, gm_ln_g, gm_ln_b, ws, bs_b,
        plans=[_chip_plan([pair_kv] + pair_sums(early[1:], big_g, got_rest))])
    half = D_MODEL // 2
    in_spec = reduce_spec("w_in")
    (dw_top, dwb_top), _ = _matmul_f32("dw_in_top", h1t, dproj, rows=(0, half))
    (dw_bot, dwb_bot), (got_top, (ws_all,)) = _matmul_f32(
        "dw_in_bot", h1t, dproj, rows=(half, half),
        plans=[_sibling_plan([dwb_top], [in_spec]),
               _gather_plan([d_ws.reshape(1, 8 * CHUNK, CHUNK).astype(BF16)], [((N_DEV, 8 * CHUNK, CHUNK), 0, 1)])])
    pair_top, ((got_bot,),) = _pair_sum("pair_sum_w_in_top", dw_top, got_top[0], *in_spec, core,
                                        plans=[_sibling_plan([dwb_bot], [in_spec])])
    pair_bot = _pair_sum("pair_sum_w_in_bot", dw_bot, got_bot, *in_spec, core)
    chip_shapes = [p.shape for p in (pair_top, pair_bot)]
    chips_in_flight, token = _split_start("grad_chip_exchange_start", _chip_routes, [pair_top, pair_bot], chip_shapes)
    n_tiles = s // min(TM_DH1, s)
    (grad_x, d_gmix), _ = _dh1("dh1", dproj, w_in_full, x2, cot, norm_mix_g, (0, n_tiles), after=[token])
    vec_names = ["norm_mix_g", "gm_ln_g", "gm_ln_b", "gm_bs", "norm_x_g", "norm_mem_g", "norm_final_g"]
    vec_g = [d_gmix, d_lng, d_lnb, d_bs, d_gx, d_gmem, d_gf]
    small = jnp.concatenate(vec_g + [sq_err, d_cw], axis=0)[None]
    small_in_flight, _ = _split_start("small_all_gather_start", _gather_routes, [small], [(N_DEV, 16, D_MODEL)])
    flat_ws = lambda t: t.reshape(8 * CHUNK, CHUNK)
    updated, _ = _adamw(
        "adamw_early",
        [(big_w[k], big_m[k], big_v[k], got_k) for k, got_k in zip(early, parts_early)]
        + [(flat_ws(ws), flat_ws(m_gm_ws), flat_ws(v_gm_ws), ws_all)], steps=STEPS_ADAMW // 2)
    big_out = dict(zip(early, updated[:4]))
    ws_out = updated[4]
    (pair_top, pair_bot), (land_top, land_bot) = _split_wait(
        "grad_chip_exchange_wait", _chip_routes, chips_in_flight, [d_gmix] + [u[0] for u in updated])
    my_chip = (2 * px + py).astype(jnp.int32).reshape(1)
    (big_out["w_in"],), _ = _adamw(
        "adamw_w_in", [(big_w["w_in"], big_m["w_in"], big_v["w_in"], [land_top, land_bot], [pair_top, pair_bot])],
        steps=STEPS_ADAMW, own_slot=my_chip)
    (small,), (small_all,) = _split_wait("small_all_gather_wait", _gather_routes, small_in_flight, [big_out["w_in"][0]])
    row = lambda t: t.reshape(1, D_MODEL)
    vec_out, conv_out, loss_row = _update_small(
        dev.astype(jnp.int32).reshape(1),
        [norm_mix_g, gm_ln_g, gm_ln_b, row(gm_bs), norm_x_g, norm_mem_g, row(norm_final_g)],
        [m_norm_mix_g, m_gm_ln_g, m_gm_ln_b, row(m_gm_bs), m_norm_x_g, m_norm_mem_g, row(m_norm_final_g)],
        [v_norm_mix_g, v_gm_ln_g, v_gm_ln_b, row(v_gm_bs), v_norm_x_g, v_norm_mem_g, row(v_norm_final_g)],
        conv_w, m_conv_w, v_conv_w, small, small_all)
    loss = loss_row[0, 0]

    def result(name, which):
        if name in big_out:
            return big_out[name][which][None]
        if name == "conv_w":
            return conv_out[which]
        if name == "gm_ws":
            return ws_out[which].reshape(1, 8, CHUNK, CHUNK)
        r = vec_out[vec_names.index(name)][which]
        if name == "gm_bs":
            return r.reshape(1, 8, CHUNK)
        if name == "norm_final_g":
            return r.reshape(D_MODEL)
        return r

    weights = ["norm_mix_g", "w_in", "conv_w", "gm_ln_g", "gm_ln_b", "gm_ws", "gm_bs", "w_out", "norm_x_g", "norm_mem_g", "w_q", "w_kv", "w_xo", "norm_final_g"]
    outs = [loss, grad_x[None]]
    for which in range(4):
        outs += [result(name, which) for name in weights]
    return tuple(outs)
```

```python
import math

import jax
import jax.numpy as jnp
from jax import lax
from jax.experimental import pallas as pl
from jax.experimental.pallas import tpu as pltpu

F32 = jnp.float32
BF16 = jnp.bfloat16
MESH = pl.DeviceIdType.MESH

D_MODEL = 1024
N_DEV = 8
N_CHIP = 4
CHUNK = 128
GROUPS = 7
IN_DIM = GROUPS * D_MODEL
SHARD_IN = IN_DIM // N_DEV
MIX_DIM = 2 * D_MODEL
X_HEADS = 4
X_HEAD_DIM = D_MODEL // X_HEADS
MEM_LEN = 256
EPS = 1e-6
GELU_K0 = math.sqrt(2.0 / math.pi)
GELU_K1 = 0.044715

ADAM_LR = 0.001
ADAM_B1 = 0.9
ADAM_B2 = 0.999
ADAM_EPS = 1e-08
ADAM_WD = 0.01
ADAM_STEP = 10

TM_INPROJ = 512
TM_ATTN = 512
SUB_ATTN = 256
TM_DH1 = 512
TM_DW = 512
TK_DW = 4096
TR_PAIR_SUM = 1024
BARRIER_ID = {"sibling": 1, "chips": 2, "all": 3}
STEPS_ADAMW = 4


def _dot(a, b):
    return lax.dot_general(a, b, (((1,), (0,)), ((), ())), preferred_element_type=F32)


def _dot_nt(a, b):
    return lax.dot_general(a, b, (((1,), (1,)), ((), ())), preferred_element_type=F32)


def _rows8(v):
    return jnp.sum(v.reshape(v.shape[0] // 8, 8, v.shape[1]), axis=0)


def _gelu(x, with_grad=True):
    x2 = x * x
    t = jnp.tanh(x * (GELU_K0 + (GELU_K0 * GELU_K1) * x2))
    half = 0.5 + 0.5 * t
    g = x * half
    if not with_grad:
        return g
    return g, half + ((0.5 * x) * (1.0 - t * t)) * (GELU_K0 + (3.0 * GELU_K0 * GELU_K1) * x2)


def _silu(z, with_grad=True):
    s = jax.nn.sigmoid(z)
    zs = z * s
    return (zs, s + zs * (1.0 - s)) if with_grad else zs


def _rms(v):
    r = lax.rsqrt(jnp.mean(v * v, axis=-1, keepdims=True) + EPS)
    return v * r, r


def _rms_bwd(dy_g, vh, r):
    return r * (dy_g - vh * jnp.mean(dy_g * vh, axis=-1, keepdims=True))


def _place():
    return lax.axis_index("x"), lax.axis_index("y"), lax.axis_index("c")


def _other_chips(x, y):
    return [(1 - x, y), (x, 1 - y), (1 - x, 1 - y)]


def _pow2_divisor(n):
    return n & (-n)


def _shard_view(ref, axis, size, d):
    start = d * size
    align = _pow2_divisor(size)
    if align > 1:
        start = pl.multiple_of(start, align)
    idx = [slice(None)] * len(ref.shape)
    idx[axis] = pl.ds(start, size)
    return ref.at[tuple(idx)]


class _Plan:
    def __init__(self, ins, out_shape, sems, start, finish, mid=None, aliases=None, peers="all"):
        self.ins, self.out_shape, self.sems = list(ins), list(out_shape), list(sems)
        self.start, self.mid, self.finish = start, mid, finish
        self.aliases = dict(aliases or {})
        self.peers = peers


def _gather_plan(shards, specs):
    n = len(shards)
    n_copy = 7

    def copies(ins, outs, sems):
        send_sems, recv_sems, local_sems = sems
        x, y, c = _place()
        me, sibling = (x, y, c), (x, y, 1 - c)
        chips = _other_chips(x, y)

        def block(a, p):
            _, axis, size = specs[a]
            return _shard_view(outs[a], axis, size, 4 * p[0] + 2 * p[1] + p[2])

        def copy(a, k, p, to, src=None):
            return pltpu.make_async_remote_copy(
                src_ref=block(a, p) if src is None else src,
                dst_ref=block(a, p),
                send_sem=send_sems.at[a * n_copy + k],
                recv_sem=recv_sems.at[a * n_copy + k],
                device_id=to,
                device_id_type=MESH,
            )

        def mine():
            return [pltpu.make_async_copy(ins[a], block(a, me), local_sems.at[a]) for a in range(n)]

        def first():
            return [cp for a in range(n) for cp in
                    [copy(a, 0, me, sibling, src=ins[a])] + [copy(a, 1 + j, me, (*chip, c), src=ins[a]) for j, chip in enumerate(chips)]]

        def landed():
            return [copy(a, 1 + j, (*chip, c), me) for j, chip in enumerate(chips) for a in range(n)]

        def passed():
            return [copy(a, 4 + j, (*chip, c), sibling) for j, chip in enumerate(chips) for a in range(n)]

        def last():
            return [cp for a in range(n) for cp in
                    [copy(a, 0, sibling, me)] + [copy(a, 4 + j, (*chip, 1 - c), me) for j, chip in enumerate(chips)]]

        return mine, first, landed, passed, last

    def start(ins, outs, sems):
        mine, first, _, _, _ = copies(ins, outs, sems)
        for cp in mine() + first():
            cp.start()

    def mid(ins, outs, sems):
        _, _, landed, passed, _ = copies(ins, outs, sems)
        for got, fwd in zip(landed(), passed()):
            got.wait_recv()
            fwd.start()

    def finish(ins, outs, sems):
        mine, first, _, passed, last = copies(ins, outs, sems)
        for cp in last():
            cp.wait_recv()
        for cp in first() + passed():
            cp.wait_send()
        for cp in mine():
            cp.wait()

    return _Plan(
        shards,
        [jax.ShapeDtypeStruct(full, s.dtype) for s, (full, _, _) in zip(shards, specs)],
        [pltpu.SemaphoreType.DMA((n * n_copy,)), pltpu.SemaphoreType.DMA((n * n_copy,)), pltpu.SemaphoreType.DMA((n,))],
        start, finish, mid,
    )


def _sibling_plan(grads, specs):
    n = len(grads)

    def shard_shape(a):
        axis, size = specs[a]
        shp = list(grads[a].shape)
        shp[axis] = size
        return tuple(shp)

    def copies(ins, outs, sems):
        send_sems, recv_sems = sems
        x, y, c = _place()
        return [
            pltpu.make_async_remote_copy(
                src_ref=_shard_view(ins[a], specs[a][0], specs[a][1], 2 * q + (1 - c)),
                dst_ref=outs[a].at[q],
                send_sem=send_sems.at[a * N_CHIP + q],
                recv_sem=recv_sems.at[a * N_CHIP + q],
                device_id=(x, y, 1 - c),
                device_id_type=MESH,
            )
            for a in range(n) for q in range(N_CHIP)
        ]

    def start(ins, outs, sems):
        for cp in copies(ins, outs, sems):
            cp.start()

    def finish(ins, outs, sems):
        for cp in copies(ins, outs, sems):
            cp.wait()

    return _Plan(
        grads,
        [jax.ShapeDtypeStruct((N_CHIP, *shard_shape(a)), grads[a].dtype) for a in range(n)],
        [pltpu.SemaphoreType.DMA((n * N_CHIP,)), pltpu.SemaphoreType.DMA((n * N_CHIP,))],
        start, finish, peers="sibling",
    )


def _chip_plan(parts, full_rows=None, row0=None, into=None):
    n = len(parts)
    full_rows = [p.shape[1] for p in parts] if full_rows is None else full_rows
    row0 = [0] * n if row0 is None else row0

    def copies(ins, outs, sems):
        send_sems, recv_sems, local_sems = sems
        x, y, c = _place()
        my_chip = 2 * x + y
        chips = _other_chips(x, y)

        def land(a, q):
            return outs[a].at[q, pl.ds(row0[a], parts[a].shape[1]), :]

        def mine():
            return [pltpu.make_async_copy(ins[a].at[my_chip], land(a, my_chip), local_sems.at[a]) for a in range(n)]

        def remote(sending):
            return [
                pltpu.make_async_remote_copy(
                    src_ref=ins[a].at[2 * chip[0] + chip[1]],
                    dst_ref=land(a, my_chip if sending else 2 * chip[0] + chip[1]),
                    send_sem=send_sems.at[a * 3 + j], recv_sem=recv_sems.at[a * 3 + j],
                    device_id=(*chip, c), device_id_type=MESH)
                for a in range(n) for j, chip in enumerate(chips)
            ]

        return mine, remote

    def start(ins, outs, sems):
        mine, remote = copies(ins, outs, sems)
        for cp in mine() + remote(True):
            cp.start()

    def finish(ins, outs, sems):
        mine, remote = copies(ins, outs, sems)
        for cp in remote(False):
            cp.wait_recv()
        for cp in remote(True):
            cp.wait_send()
        for cp in mine():
            cp.wait()

    return _Plan(
        list(parts) + (list(into) if into is not None else []),
        [jax.ShapeDtypeStruct((N_CHIP, full_rows[a], parts[a].shape[2]), parts[a].dtype) for a in range(n)],
        [pltpu.SemaphoreType.DMA((n * 3,)), pltpu.SemaphoreType.DMA((n * 3,)), pltpu.SemaphoreType.DMA((n,))],
        start, finish,
        aliases={n + a: a for a in range(n)} if into is not None else None, peers="chips",
    )


def _call(body, *, name, operands, out_shape, grid=(), in_specs=None, out_specs=None, scratch_shapes=(),
          plans=(), aliases=None, prefetch=(), start_after_body=False, pass_on_at=0.6, body_peers=None):
    operands, out_shape, scratch_shapes = list(operands), list(out_shape), list(scratch_shapes)
    n_pf, n_in, n_out, n_scr = len(prefetch), len(operands), len(out_shape), len(scratch_shapes)
    in_vmem = pl.BlockSpec(memory_space=pltpu.VMEM)
    in_hbm = pl.BlockSpec(memory_space=pl.ANY)
    in_specs = [in_vmem] * n_in if in_specs is None else list(in_specs)
    out_specs = [in_vmem] * n_out if out_specs is None else list(out_specs)
    io_alias = {n_pf + i: o for i, o in (aliases or {}).items()}
    pos_in, pos_out = n_pf + n_in, n_out
    for p in plans:
        for i, o in p.aliases.items():
            io_alias[pos_in + i] = pos_out + o
        pos_in += len(p.ins)
        pos_out += len(p.out_shape)
    steps = math.prod(grid) if grid else 1
    mid_step = min(steps - 1, int(steps * pass_on_at))
    patterns = {p.peers for p in plans} | ({body_peers} if body_peers else set())
    pattern = None if not patterns else patterns.pop() if len(patterns) == 1 else "all"

    def shake_hands():
        x, y, c = _place()
        peers = [(x, y, 1 - c)] if pattern in ("sibling", "all") else []
        peers += [(*chip, c) for chip in _other_chips(x, y)] if pattern in ("chips", "all") else []
        barrier = pltpu.get_barrier_semaphore()
        for peer in peers:
            pl.semaphore_signal(barrier, inc=1, device_id=peer, device_id_type=MESH)
        pl.semaphore_wait(barrier, len(peers))

    def wrapped(*refs):
        refs = list(refs)
        take = lambda k: [refs.pop(0) for _ in range(k)]
        pf = take(n_pf)
        ins, p_ins = take(n_in), [take(len(p.ins)) for p in plans]
        outs, p_outs = take(n_out), [take(len(p.out_shape)) for p in plans]
        scr, p_sems = take(n_scr), [take(len(p.sems)) for p in plans]
        step = 0
        for ax, g in enumerate(grid):
            step = step * g + pl.program_id(ax)

        def hook(kind, at):
            todo = [(getattr(p, kind), a, b, c) for p, a, b, c in zip(plans, p_ins, p_outs, p_sems) if getattr(p, kind)]

            def run():
                for fn, a, b, c in todo:
                    fn(a, b, c)

            if todo and grid:
                pl.when(step == at)(run)
            elif todo:
                run()

        if pattern and grid:
            pl.when(step == 0)(shake_hands)
        elif pattern:
            shake_hands()
        if not start_after_body:
            hook("start", 0)
        if body is not None:
            body(*pf, *ins, *outs, *scr)
        if start_after_body:
            hook("start", 0)
        hook("mid", mid_step)
        hook("finish", steps - 1)

    layout = dict(
        grid=grid,
        in_specs=in_specs + [in_hbm] * sum(len(p.ins) for p in plans),
        out_specs=out_specs + [in_hbm] * sum(len(p.out_shape) for p in plans),
        scratch_shapes=scratch_shapes + [s for p in plans for s in p.sems],
    )
    if n_pf:
        layout = dict(grid_spec=pltpu.PrefetchScalarGridSpec(num_scalar_prefetch=n_pf, **layout))
    res = pl.pallas_call(
        wrapped,
        name=name,
        out_shape=out_shape + [o for p in plans for o in p.out_shape],
        input_output_aliases=io_alias,
        compiler_params=pltpu.CompilerParams(
            dimension_semantics=("arbitrary",) * len(grid) if grid else None,
            collective_id=BARRIER_ID[pattern] if pattern else None),
        **layout,
    )(*prefetch, *operands, *[a for p in plans for a in p.ins])
    res = list(res)
    outs, plan_outs = res[:n_out], []
    pos = n_out
    for p in plans:
        plan_outs.append(res[pos : pos + len(p.out_shape)])
        pos += len(p.out_shape)
    return outs, plan_outs


def _inproj(x, g, w_shard, order, plans=()):
    s = x.shape[0]
    tm = min(TM_INPROJ, s)
    nt = s // tm
    wide = 2 * SHARD_IN
    fwd_at = nt // 2

    def body(order_ref, x_ref, g_ref, w_hbm, proj_ref, h1_ref, h1t_ref, wfull_hbm,
             w_scr, h1_scr, send_sems, recv_sems, local_sems):
        k, i = pl.program_id(0), pl.program_id(1)
        px, py, c = _place()
        me, sibling = (px, py, c), (px, py, 1 - c)
        chips = _other_chips(px, py)

        def half(slot, core):
            return w_scr.at[slot, :, pl.ds(pl.multiple_of(core * SHARD_IN, 128), SHARD_IN)]

        def copy(n, slot, core, to, src=None):
            return pltpu.make_async_remote_copy(
                src_ref=half(slot, core) if src is None else src, dst_ref=half(slot, core),
                send_sem=send_sems.at[n], recv_sem=recv_sems.at[n], device_id=to, device_id_type=MESH)

        def mine():
            return pltpu.make_async_copy(w_hbm, half(0, c), local_sems.at[0])

        def first():
            return [copy(0, 0, c, sibling, src=w_hbm)] + [copy(1 + j, 1 + j, c, (*chips[j], c), src=w_hbm) for j in range(2)]

        def relay():
            to = (c * px + (1 - c) * (1 - px), c * (1 - py) + (1 - c) * py, c)
            return pltpu.make_async_remote_copy(
                src_ref=half(2 - c, c), dst_ref=half(3, c), send_sem=send_sems.at[3], recv_sem=recv_sems.at[3],
                device_id=to, device_id_type=MESH)

        def save(slot):
            cols = pl.ds(pl.multiple_of(order_ref[slot] * wide, 128), wide)
            return pltpu.make_async_copy(w_scr.at[slot], wfull_hbm.at[:, cols], local_sems.at[1 + slot])

        @pl.when((k == 0) & (i == 0))
        def _():
            mine().start()
            for cp in first():
                cp.start()
            mine().wait()
            copy(0, 0, 1 - c, me).wait_recv()

        for j in range(3):
            @pl.when((k == j + 1) & (i == 0))
            def _():
                copy(4 + j, 1 + j, 1 - c, me).wait_recv()

        @pl.when((k == 3) & (i == 0))
        def _():
            for slot in range(N_CHIP):
                save(slot).start()

        @pl.when(k == 0)
        def _():
            xh, _ = _rms(x_ref[...])
            h = xh * g_ref[...]
            h1_ref[...] = h.astype(BF16)
            h1t_ref[...] = h.T.astype(BF16)
            h1_scr[pl.ds(pl.multiple_of(i * tm, tm), tm), :] = h.astype(BF16)

        proj_ref[...] = _dot(h1_scr[pl.ds(pl.multiple_of(i * tm, tm), tm), :], w_scr[k]).astype(BF16)

        for j in range(3):
            @pl.when((k == 0) & (i == nt - 1) if j < 2 else (k == 2) & (i == fwd_at))
            def _():
                copy(1 + j, 1 + j, c, me).wait_recv()
                copy(4 + j, 1 + j, c, sibling).start()
                if j < 2:
                    pl.when(c == 1 - j)(lambda: relay().start())

        @pl.when((k == 3) & (i == nt - 1))
        def _():
            for cp in first() + [relay()] + [copy(4 + j, 1 + j, c, sibling) for j in range(3)]:
                cp.wait_send()
            for slot in range(N_CHIP):
                save(slot).wait()

    once = lambda k, i, order_ref: (jnp.where(k == 0, i, nt - 1), 0)
    return _call(
        body,
        name="inproj",
        prefetch=[order],
        operands=[x, g, w_shard],
        grid=(N_CHIP, nt),
        in_specs=[
            pl.BlockSpec((tm, D_MODEL), once),
            pl.BlockSpec((1, D_MODEL), lambda k, i, order_ref: (0, 0)),
            pl.BlockSpec(memory_space=pl.ANY),
        ],
        out_specs=[
            pl.BlockSpec((tm, wide), lambda k, i, order_ref: (i, order_ref[k])),
            pl.BlockSpec((tm, D_MODEL), once),
            pl.BlockSpec((D_MODEL, tm), lambda k, i, order_ref: (0, jnp.where(k == 0, i, nt - 1))),
            pl.BlockSpec(memory_space=pl.ANY),
        ],
        out_shape=[
            jax.ShapeDtypeStruct((s, IN_DIM), BF16),
            jax.ShapeDtypeStruct((s, D_MODEL), BF16),
            jax.ShapeDtypeStruct((D_MODEL, s), BF16),
            jax.ShapeDtypeStruct((D_MODEL, IN_DIM), BF16),
        ],
        scratch_shapes=[
            pltpu.VMEM((N_CHIP, D_MODEL, wide), BF16),
            pltpu.VMEM((s, D_MODEL), BF16),
            pltpu.SemaphoreType.DMA((7,)),
            pltpu.SemaphoreType.DMA((7,)),
            pltpu.SemaphoreType.DMA((1 + N_CHIP,)),
        ],
        plans=plans,
        start_after_body=True,
        pass_on_at=0.8,
        body_peers="all",
    )


def _rows_from_above(v, above):
    row = lax.broadcasted_iota(jnp.int32, (8, v.shape[1]), 0)
    r1, r2 = pltpu.roll(v, 1, 0), pltpu.roll(v, 2, 0)
    top1 = jnp.where(row == 0, above[7:8], r1[0:8])
    top2 = jnp.where(row == 0, above[6:7], jnp.where(row == 1, above[7:8], r2[0:8]))
    return jnp.concatenate([top1, r1[8:]], axis=0), jnp.concatenate([top2, r2[8:]], axis=0)


def _rows_from_below(v, below):
    n = v.shape[0]
    row = lax.broadcasted_iota(jnp.int32, (8, v.shape[1]), 0)
    r1, r2 = pltpu.roll(v, n - 1, 0), pltpu.roll(v, n - 2, 0)
    end1 = jnp.where(row == 7, below[0:1], r1[n - 8 :])
    end2 = jnp.where(row == 6, below[0:1], jnp.where(row == 7, below[1:2], r2[n - 8 :]))
    return jnp.concatenate([r1[: n - 8], end1], axis=0), jnp.concatenate([r2[: n - 8], end2], axis=0)


def _causal_mask():
    row = lax.broadcasted_iota(jnp.int32, (CHUNK, CHUNK), 0)
    col = lax.broadcasted_iota(jnp.int32, (CHUNK, CHUNK), 1)
    return row, row >= col


def _mixer_fwd(proj, conv_w, ln_g, ln_b, ws, bs_b, plans=()):
    s = proj.shape[0]
    nt = s // CHUNK

    def body(proj_ref, cw_ref, lng_ref, lnb_ref, ws_ref, bsb_ref, cat_ref, pcar_ref):
        @pl.when(pl.program_id(0) == 0)
        def _():
            pcar_ref[...] = jnp.zeros_like(pcar_ref)

        row, tril = _causal_mask()
        for j in range(8):
            cs = slice(CHUNK * j, CHUNK * (j + 1))

            def grp(k):
                return proj_ref[:, k * D_MODEL + CHUNK * j : k * D_MODEL + CHUNK * (j + 1)].astype(F32)

            w = cw_ref[:, cs]
            p = grp(1) * grp(2)
            p1, p2 = _rows_from_above(p, pcar_ref[:, cs])
            pcar_ref[:, cs] = p[CHUNK - 8 :]
            cv = w[0:1] * p2 + w[1:2] * p1 + w[2:3] * p
            sa = _silu(grp(3), with_grad=False)
            cat_ref[:, cs] = ((grp(0) * cv) * sa).astype(BF16)
            gu = _gelu(grp(4), with_grad=False)
            gv = _gelu(grp(5), with_grad=False)
            dv = gv - jnp.mean(gv, axis=-1, keepdims=True)
            vn = dv * lax.rsqrt(jnp.mean(dv * dv, axis=-1, keepdims=True) + EPS)
            vn = vn * lng_ref[:, cs] + lnb_ref[:, cs]
            wc = jnp.where(tril, ws_ref[j], 0.0).astype(BF16)
            sp = _dot(wc, vn.astype(BF16)) + bsb_ref[j]
            sb = _silu(grp(6), with_grad=False)
            cat_ref[:, D_MODEL + CHUNK * j : D_MODEL + CHUNK * (j + 1)] = ((gu * sp) * sb).astype(BF16)

    whole = lambda shape: pl.BlockSpec(shape, lambda i: (0,) * len(shape))
    (cat,), plan_outs = _call(
        body,
        name="mixer_fwd",
        operands=[proj, conv_w, ln_g, ln_b, ws, bs_b],
        grid=(nt,),
        in_specs=[
            pl.BlockSpec((CHUNK, IN_DIM), lambda i: (i, 0)),
            whole((8, D_MODEL)),
            whole((1, D_MODEL)),
            whole((1, D_MODEL)),
            whole((8, CHUNK, CHUNK)),
            whole((8, CHUNK, CHUNK)),
        ],
        out_specs=[pl.BlockSpec((CHUNK, MIX_DIM), lambda i: (i, 0))],
        out_shape=[jax.ShapeDtypeStruct((s, MIX_DIM), BF16)],
        scratch_shapes=[pltpu.VMEM((8, D_MODEL), F32)],
        plans=plans,
        pass_on_at=0.85,
    )
    return cat, plan_outs


def _mixer_bwd(proj, dcat, conv_w, ln_g, ln_b, ws, bs_b, plans=()):
    s = proj.shape[0]
    nt = s // CHUNK

    def body(proj_ref, halo_ref, dcat_ref, cw_ref, lng_ref, lnb_ref, ws_ref, bsb_ref,
             dproj_ref, dcw_ref, dlng_ref, dlnb_ref, dws_ref, dbs_ref,
             car_ref, acc_cw, acc_lng, acc_lnb, acc_bs, wc_scr, wct_scr):
        i = pl.program_id(0)
        tile = nt - 1 - i
        row, tril = _causal_mask()

        @pl.when(i == 0)
        def _():
            for j in range(8):
                wc = jnp.where(tril, ws_ref[j], 0.0)
                wc_scr[j] = wc.astype(BF16)
                wct_scr[j] = wc.T.astype(BF16)
            car_ref[...] = jnp.zeros_like(car_ref)
            acc_cw[...] = jnp.zeros_like(acc_cw)
            acc_lng[...] = jnp.zeros_like(acc_lng)
            acc_lnb[...] = jnp.zeros_like(acc_lnb)
            acc_bs[...] = jnp.zeros_like(acc_bs)
            dws_ref[...] = jnp.zeros_like(dws_ref)

        has_prev = jnp.where(tile > 0, 1.0, 0.0).astype(F32)
        for j in range(8):
            cs = slice(CHUNK * j, CHUNK * (j + 1))

            def col(k):
                return slice(k * D_MODEL + CHUNK * j, k * D_MODEL + CHUNK * (j + 1))

            def grp(k):
                return proj_ref[:, col(k)].astype(F32)

            w = cw_ref[:, cs]
            gb, gc, xa = grp(0), grp(1), grp(2)
            p = gc * xa
            pprev = halo_ref[:, col(1)].astype(F32) * halo_ref[:, col(2)].astype(F32) * has_prev
            p1, p2 = _rows_from_above(p, pprev[8:16])
            cv = w[0:1] * p2 + w[1:2] * p1 + w[2:3] * p
            za = grp(3)
            sa, dsa = _silu(za)
            da = dcat_ref[:, cs].astype(F32)
            da_gb = da * gb
            dproj_ref[:, col(0)] = ((da * sa) * cv).astype(BF16)
            dproj_ref[:, col(3)] = ((da_gb * cv) * dsa).astype(BF16)
            dcv = da_gb * sa
            d1, d2 = _rows_from_below(dcv, car_ref[:, cs])
            car_ref[:, cs] = dcv[0:8]
            dp = w[2:3] * dcv + w[1:2] * d1 + w[0:1] * d2
            dproj_ref[:, col(1)] = (dp * xa).astype(BF16)
            dproj_ref[:, col(2)] = (dp * gc).astype(BF16)
            acc_cw[0, :, cs] += _rows8(dcv * p2)
            acc_cw[1, :, cs] += _rows8(dcv * p1)
            acc_cw[2, :, cs] += _rows8(dcv * p)
            gu, dgu = _gelu(grp(4))
            gv, dgv = _gelu(grp(5))
            dv = gv - jnp.mean(gv, axis=-1, keepdims=True)
            rstd = lax.rsqrt(jnp.mean(dv * dv, axis=-1, keepdims=True) + EPS)
            vnh = dv * rstd
            lng = lng_ref[:, cs]
            vnb = (vnh * lng + lnb_ref[:, cs]).astype(BF16)
            sp = _dot(wc_scr[j], vnb) + bsb_ref[j]
            zb = grp(6)
            sb, dsb = _silu(zb)
            db = dcat_ref[:, D_MODEL + CHUNK * j : D_MODEL + CHUNK * (j + 1)].astype(F32)
            db_sp, db_gu = db * sp, db * gu
            dproj_ref[:, col(4)] = ((db_sp * sb) * dgu).astype(BF16)
            dproj_ref[:, col(6)] = ((db_sp * gu) * dsb).astype(BF16)
            dsp = db_gu * sb
            acc_bs[j] += dsp
            dspb = dsp.astype(BF16)
            dws_ref[j] += _dot_nt(dspb, vnb)
            dvn = _dot(wct_scr[j], dspb)
            acc_lnb[:, cs] += _rows8(dvn)
            acc_lng[:, cs] += _rows8(dvn * vnh)
            dvh = dvn * lng
            dgvv = rstd * (dvh - jnp.mean(dvh, axis=-1, keepdims=True) - vnh * jnp.mean(dvh * vnh, axis=-1, keepdims=True))
            dproj_ref[:, col(5)] = (dgvv * dgv).astype(BF16)

        @pl.when(i == nt - 1)
        def _():
            dlng_ref[...] = jnp.sum(acc_lng[...], axis=0, keepdims=True)
            dlnb_ref[...] = jnp.sum(acc_lnb[...], axis=0, keepdims=True)
            dcw_ref[...] = jnp.zeros_like(dcw_ref)
            for k in range(3):
                dcw_ref[k : k + 1, :] = jnp.sum(acc_cw[k], axis=0, keepdims=True)
            for j in range(8):
                dws_ref[j] = jnp.where(tril, dws_ref[j], 0.0)
                dbs_ref[:, CHUNK * j : CHUNK * (j + 1)] = jnp.sum(acc_bs[j].T, axis=0, keepdims=True)

    whole = lambda shape: pl.BlockSpec(shape, lambda i: (0,) * len(shape))
    halo_rows = 16
    per = CHUNK // halo_rows
    return _call(
        body,
        name="mixer_bwd",
        operands=[proj, proj, dcat, conv_w, ln_g, ln_b, ws, bs_b],
        plans=plans,
        grid=(nt,),
        in_specs=[
            pl.BlockSpec((CHUNK, IN_DIM), lambda i: (nt - 1 - i, 0)),
            pl.BlockSpec((halo_rows, IN_DIM), lambda i: (jnp.maximum((nt - 1 - i) * per - 1, 0), 0)),
            pl.BlockSpec((CHUNK, MIX_DIM), lambda i: (nt - 1 - i, 0)),
            whole((8, D_MODEL)),
            whole((1, D_MODEL)),
            whole((1, D_MODEL)),
            whole((8, CHUNK, CHUNK)),
            whole((8, CHUNK, CHUNK)),
        ],
        out_specs=[
            pl.BlockSpec((CHUNK, IN_DIM), lambda i: (nt - 1 - i, 0)),
            whole((8, D_MODEL)),
            whole((1, D_MODEL)),
            whole((1, D_MODEL)),
            whole((8, CHUNK, CHUNK)),
            whole((1, D_MODEL)),
        ],
        out_shape=[
            jax.ShapeDtypeStruct((s, IN_DIM), BF16),
            jax.ShapeDtypeStruct((8, D_MODEL), F32),
            jax.ShapeDtypeStruct((1, D_MODEL), F32),
            jax.ShapeDtypeStruct((1, D_MODEL), F32),
            jax.ShapeDtypeStruct((8, CHUNK, CHUNK), F32),
            jax.ShapeDtypeStruct((1, D_MODEL), F32),
        ],
        scratch_shapes=[
            pltpu.VMEM((8, D_MODEL), F32),
            pltpu.VMEM((3, 8, D_MODEL), F32),
            pltpu.VMEM((8, D_MODEL), F32),
            pltpu.VMEM((8, D_MODEL), F32),
            pltpu.VMEM((8, CHUNK, CHUNK), F32),
            pltpu.VMEM((8, CHUNK, CHUNK), BF16),
            pltpu.VMEM((8, CHUNK, CHUNK), BF16),
        ],
    )


def _kv_fwd(mem, g_mem, w_kv):
    def body(mem_ref, g_ref, w_ref, k_ref, vv_ref, mt_ref):
        mh, _ = _rms(mem_ref[...])
        m = mh * g_ref[...]
        kv = _dot(m.astype(BF16), w_ref[...])
        k_ref[...] = kv[:, :D_MODEL].astype(BF16)
        vv_ref[...] = kv[:, D_MODEL:].astype(BF16)
        mt_ref[...] = m.T.astype(BF16)

    return pl.pallas_call(
        body,
        name="kv_fwd",
        out_shape=[
            jax.ShapeDtypeStruct((MEM_LEN, D_MODEL), BF16),
            jax.ShapeDtypeStruct((MEM_LEN, D_MODEL), BF16),
            jax.ShapeDtypeStruct((D_MODEL, MEM_LEN), BF16),
        ],
    )(mem, g_mem, w_kv)


def _kv_bwd(mem, w_kv, mt, dk, dvv, plans=()):
    def body(mem_ref, w_ref, mt_ref, dk_ref, dvv_ref, dw_ref, dwb_ref, dg_ref):
        dkv = jnp.concatenate([dk_ref[...], dvv_ref[...]], axis=1).astype(BF16)
        dw = _dot(mt_ref[...], dkv)
        dw_ref[...] = dw
        dwb_ref[...] = dw.astype(BF16)
        dm = _dot_nt(dkv, w_ref[...])
        mh, _ = _rms(mem_ref[...])
        dg_ref[...] = jnp.sum(dm * mh, axis=0, keepdims=True)

    return _call(
        body,
        name="kv_bwd",
        operands=[mem, w_kv, mt, dk, dvv],
        out_shape=[
            jax.ShapeDtypeStruct((D_MODEL, 2 * D_MODEL), F32),
            jax.ShapeDtypeStruct((D_MODEL, 2 * D_MODEL), BF16),
            jax.ShapeDtypeStruct((1, D_MODEL), F32),
        ],
        plans=plans,
    )


def _attn(x, cat, target, w_out, w_q, w_xo, k, vv, g_x, g_f):
    s = x.shape[0]
    tm = min(TM_ATTN, s)
    sub = min(SUB_ATTN, tm)
    nt = s // tm
    scale = 1.0 / math.sqrt(X_HEAD_DIM)

    def body(x_ref, cat_ref, t_ref, wout_ref, wq_ref, wxo_ref, k_ref, vv_ref, gx_ref, gf_ref,
             dcat_ref, actt_ref, cot_ref,
             dk_ref, dvv_ref, dgf_ref, dgx_ref, loss_ref,
             p_scr, q_scr, o_scr, dq_scr, acc_gf, acc_gx, acc_loss):
        i = pl.program_id(0)

        @pl.when(i == 0)
        def _():
            dk_ref[...] = jnp.zeros_like(dk_ref)
            dvv_ref[...] = jnp.zeros_like(dvv_ref)
            acc_gf[...] = jnp.zeros_like(acc_gf)
            acc_gx[...] = jnp.zeros_like(acc_gx)
            acc_loss[...] = jnp.zeros_like(acc_loss)

        subs = [slice(u * sub, (u + 1) * sub) for u in range(tm // sub)]
        heads = [slice(X_HEAD_DIM * h, X_HEAD_DIM * (h + 1)) for h in range(X_HEADS)]
        gx, gf = gx_ref[...], gf_ref[...]
        x1, x1h, r2 = [], [], []
        for rs in subs:
            cat = cat_ref[rs, :]
            actt_ref[0:MIX_DIM, rs] = cat.astype(F32).T.astype(BF16)
            v = x_ref[rs, :] + _dot(cat, wout_ref[...])
            vh, r = _rms(v)
            h2 = vh * gx
            actt_ref[MIX_DIM : MIX_DIM + D_MODEL, rs] = h2.T.astype(BF16)
            q_scr[rs, :] = _dot(h2.astype(BF16), wq_ref[...]).astype(BF16)
            x1.append(v), x1h.append(vh), r2.append(r)
        for rs in subs:
            for hs in heads:
                sc = _dot_nt(q_scr[rs, hs], k_ref[:, hs]) * scale
                e = jnp.exp(sc - jnp.max(sc, axis=-1, keepdims=True))
                p = e / jnp.sum(e, axis=-1, keepdims=True)
                p_scr[rs, hs] = p
                o_scr[rs, hs] = _dot(p.astype(BF16), vv_ref[:, hs])
        dx2, sq, d_gf = [], None, None
        for u, rs in enumerate(subs):
            o = o_scr[rs, :]
            actt_ref[MIX_DIM + D_MODEL :, rs] = o.T.astype(BF16)
            x2 = x1[u] + _dot(o.astype(BF16), wxo_ref[...])
            x2h, r3 = _rms(x2)
            err = x2h * gf - t_ref[rs, :]
            dy = err * (1.0 / D_MODEL)
            sq = _rows8(err * err) if sq is None else sq + _rows8(err * err)
            d_gf = _rows8(dy * x2h) if d_gf is None else d_gf + _rows8(dy * x2h)
            dx2.append(_rms_bwd(dy * gf, x2h, r3))
        acc_loss[...] += sq
        acc_gf[...] += d_gf
        d_vv, d_k = [None] * X_HEADS, [None] * X_HEADS
        for u, rs in enumerate(subs):
            dx2b = dx2[u].astype(BF16)
            cot_ref[2, rs, :] = dx2b
            dob = _dot_nt(dx2b, wxo_ref[...]).astype(BF16)
            for h, hs in enumerate(heads):
                p = p_scr[rs, hs]
                dp = _dot_nt(dob[:, hs], vv_ref[:, hs])
                ds = p * (dp - jnp.sum(dp * p, axis=-1, keepdims=True)) * scale
                dq_scr[rs, hs] = _dot(ds.astype(BF16), k_ref[:, hs]).astype(BF16)
                dvv_h = _dot(p.T.astype(BF16), dob[:, hs])
                dk_h = _dot(ds.T.astype(BF16), q_scr[rs, hs])
                d_vv[h] = dvv_h if d_vv[h] is None else d_vv[h] + dvv_h
                d_k[h] = dk_h if d_k[h] is None else d_k[h] + dk_h
        for h, hs in enumerate(heads):
            dvv_ref[:, hs] += d_vv[h]
            dk_ref[:, hs] += d_k[h]
        d_gx = None
        for u, rs in enumerate(subs):
            dqb = dq_scr[rs, :]
            cot_ref[1, rs, :] = dqb
            dh2 = _dot_nt(dqb, wq_ref[...])
            d_gx = _rows8(dh2 * x1h[u]) if d_gx is None else d_gx + _rows8(dh2 * x1h[u])
            dx1 = dx2[u] + _rms_bwd(dh2 * gx, x1h[u], r2[u])
            dx1b = dx1.astype(BF16)
            cot_ref[0, rs, :] = dx1b
            dcat_ref[rs, :] = _dot_nt(dx1b, wout_ref[...]).astype(BF16)
        acc_gx[...] += d_gx

        @pl.when(i == nt - 1)
        def _():
            dgf_ref[...] = jnp.sum(acc_gf[...], axis=0, keepdims=True)
            dgx_ref[...] = jnp.sum(acc_gx[...], axis=0, keepdims=True)
            loss_ref[...] = jnp.sum(acc_loss[...], axis=0, keepdims=True)

    whole = lambda shape: pl.BlockSpec(shape, lambda i: (0,) * len(shape))
    once = lambda shape: pl.BlockSpec(shape, lambda i: (0,) * len(shape), pipeline_mode=pl.Buffered(1))
    rows = lambda width: pl.BlockSpec((tm, width), lambda i: (i, 0))
    cols = lambda height: pl.BlockSpec((height, tm), lambda i: (0, i))
    vec = jax.ShapeDtypeStruct((1, D_MODEL), F32)
    return pl.pallas_call(
        body,
        name="attn",
        grid=(nt,),
        in_specs=[
            rows(D_MODEL), rows(MIX_DIM), rows(D_MODEL),
            once((MIX_DIM, D_MODEL)), once((D_MODEL, D_MODEL)), once((D_MODEL, D_MODEL)),
            once((MEM_LEN, D_MODEL)), once((MEM_LEN, D_MODEL)),
            whole((1, D_MODEL)), whole((1, D_MODEL)),
        ],
        out_specs=[
            rows(MIX_DIM), cols(2 * MIX_DIM),
            pl.BlockSpec((3, tm, D_MODEL), lambda i: (0, i, 0)),
            whole((MEM_LEN, D_MODEL)), whole((MEM_LEN, D_MODEL)),
            whole((1, D_MODEL)), whole((1, D_MODEL)), whole((1, D_MODEL)),
        ],
        out_shape=[
            jax.ShapeDtypeStruct((s, MIX_DIM), BF16),
            jax.ShapeDtypeStruct((2 * MIX_DIM, s), BF16),
            jax.ShapeDtypeStruct((3, s, D_MODEL), BF16),
            jax.ShapeDtypeStruct((MEM_LEN, D_MODEL), F32),
            jax.ShapeDtypeStruct((MEM_LEN, D_MODEL), F32),
            vec, vec, vec,
        ],
        scratch_shapes=[
            pltpu.VMEM((tm, D_MODEL), F32),
            pltpu.VMEM((tm, D_MODEL), BF16),
            pltpu.VMEM((tm, D_MODEL), F32),
            pltpu.VMEM((tm, D_MODEL), BF16),
            pltpu.VMEM((8, D_MODEL), F32),
            pltpu.VMEM((8, D_MODEL), F32),
            pltpu.VMEM((8, D_MODEL), F32),
        ],
        compiler_params=pltpu.CompilerParams(dimension_semantics=("arbitrary",)),
    )(x, cat, target, w_out, w_q, w_xo, k, vv, g_x, g_f)


def _matmul_f32(name, a, b, rows=None, plans=()):
    first, m = (0, a.shape[0]) if rows is None else rows
    kk = a.shape[1]
    n = b.shape[1]
    tm, tn, tk = min(m, TM_DW), min(n, D_MODEL), min(kk, TK_DW)
    nk = kk // tk
    i0 = first // tm

    def body(a_ref, b_ref, o_ref, ob_ref):
        prod = _dot(a_ref[...], b_ref[...])
        if nk == 1:
            o_ref[...] = prod
            ob_ref[...] = prod.astype(BF16)
            return
        kid = pl.program_id(2)

        @pl.when(kid == 0)
        def _():
            o_ref[...] = prod

        @pl.when(kid > 0)
        def _():
            o_ref[...] += prod

        @pl.when(kid == nk - 1)
        def _():
            ob_ref[...] = o_ref[...].astype(BF16)

    out_spec = pl.BlockSpec((tm, tn), lambda i, j, k: (i, j))
    outs, plan_outs = _call(
        body,
        name=name,
        operands=[a, b],
        grid=(m // tm, n // tn, nk),
        in_specs=[pl.BlockSpec((tm, tk), lambda i, j, k: (i + i0, k)), pl.BlockSpec((tk, tn), lambda i, j, k: (k, j))],
        out_specs=[out_spec, out_spec],
        out_shape=[jax.ShapeDtypeStruct((m, n), F32), jax.ShapeDtypeStruct((m, n), BF16)],
        plans=plans,
    )
    return outs, plan_outs


def _dw_attn(actt, cot, plans=()):
    s = actt.shape[1]
    tm = TM_DW
    counts = [MIX_DIM // tm, D_MODEL // tm, D_MODEL // tm]
    starts = [0, counts[0], counts[0] + counts[1]]
    which = lambda i: jnp.where(i < starts[1], 0, jnp.where(i < starts[2], 1, 2))

    def body(a_ref, b_ref, *outs):
        i = pl.program_id(0)
        prod = _dot(a_ref[...], b_ref[0])
        for m in range(3):
            @pl.when(which(i) == m)
            def _():
                outs[2 * m][...] = prod
                outs[2 * m + 1][...] = prod.astype(BF16)

    out_specs, out_shape = [], []
    for m in range(3):
        spec = pl.BlockSpec((tm, D_MODEL), lambda i, m=m: (jnp.clip(i - starts[m], 0, counts[m] - 1), 0))
        out_specs += [spec, spec]
        out_shape += [jax.ShapeDtypeStruct((counts[m] * tm, D_MODEL), F32), jax.ShapeDtypeStruct((counts[m] * tm, D_MODEL), BF16)]
    outs, plan_outs = _call(
        body,
        name="dw_attn",
        operands=[actt, cot],
        grid=(sum(counts),),
        in_specs=[pl.BlockSpec((tm, s), lambda i: (i, 0)), pl.BlockSpec((1, s, D_MODEL), lambda i: (which(i), 0, 0))],
        out_specs=out_specs,
        out_shape=out_shape,
        plans=plans,
    )
    return [outs[2 * m : 2 * m + 2] for m in range(3)], plan_outs


def _dh1(name, dproj, w_in, x, dx1, g, tiles, into=None, after=(), plans=()):
    s = x.shape[0]
    tm = min(TM_DH1, s)
    t0, nt = tiles[0], tiles[1] - tiles[0]

    def body(dp_ref, w_ref, x_ref, dx1_ref, g_ref, *rest):
        gx_ref, dg_ref, acc_g = rest[-3:]
        i = pl.program_id(0)

        @pl.when(i == 0)
        def _():
            acc_g[...] = jnp.zeros_like(acc_g)

        dh1 = _dot_nt(dp_ref[...], w_ref[...])
        xh, r = _rms(x_ref[...])
        acc_g[...] += _rows8(dh1 * xh)
        gx_ref[...] = dx1_ref[0].astype(F32) + _rms_bwd(dh1 * g_ref[...], xh, r)

        @pl.when(i == nt - 1)
        def _():
            dg_ref[...] = jnp.sum(acc_g[...], axis=0, keepdims=True)

    rows = pl.BlockSpec((tm, D_MODEL), lambda i: (i + t0, 0))
    return _call(
        body,
        name=name,
        operands=[dproj, w_in, x, dx1, g] + ([into] if into is not None else []) + list(after),
        grid=(nt,),
        in_specs=[
            pl.BlockSpec((tm, IN_DIM), lambda i: (i + t0, 0)),
            pl.BlockSpec((D_MODEL, IN_DIM), lambda i: (0, 0), pipeline_mode=pl.Buffered(1)),
            rows,
            pl.BlockSpec((1, tm, D_MODEL), lambda i: (0, i + t0, 0)),
            pl.BlockSpec((1, D_MODEL), lambda i: (0, 0)),
        ] + [pl.BlockSpec(memory_space=pl.ANY)] * ((into is not None) + len(after)),
        out_specs=[rows, pl.BlockSpec((1, D_MODEL), lambda i: (0, 0))],
        out_shape=[jax.ShapeDtypeStruct((s, D_MODEL), F32), jax.ShapeDtypeStruct((1, D_MODEL), F32)],
        scratch_shapes=[pltpu.VMEM((8, D_MODEL), F32)],
        plans=plans,
        aliases={5: 0} if into is not None else None,
    )


def _pair_sum(name, grad, got, axis, size, core, plans=()):
    shard = list(grad.shape)
    shard[axis] = size
    r, cdim = shard
    tr = min(r, TR_PAIR_SUM)
    nr = r // tr

    def body(core_ref, g_ref, got_ref, o_ref):
        o_ref[0] = (g_ref[...] + got_ref[0].astype(F32)).astype(BF16)

    if axis == 1:
        g_map = lambda q, t, core_ref: (t, 2 * q + core_ref[0])
    else:
        g_map = lambda q, t, core_ref: ((2 * q + core_ref[0]) * nr + t, 0)
    (out,), plan_outs = _call(
        body,
        name=name,
        prefetch=[core],
        operands=[grad, got],
        grid=(N_CHIP, nr),
        in_specs=[
            pl.BlockSpec((tr, cdim), g_map),
            pl.BlockSpec((1, tr, cdim), lambda q, t, core_ref: (q, t, 0)),
        ],
        out_specs=[pl.BlockSpec((1, tr, cdim), lambda q, t, core_ref: (q, t, 0))],
        out_shape=[jax.ShapeDtypeStruct((N_CHIP, r, cdim), BF16)],
        plans=plans,
    )
    return (out, plan_outs) if plans else out


def _adam_step(w, m, v, g):
    c1 = 1.0 / (1.0 - ADAM_B1**ADAM_STEP)
    c2 = 1.0 / (1.0 - ADAM_B2**ADAM_STEP)
    m_new = ADAM_B1 * m + (1.0 - ADAM_B1) * g
    v_new = ADAM_B2 * v + (1.0 - ADAM_B2) * (g * g)
    return -ADAM_LR * ((m_new * c1) / (jnp.sqrt(v_new * c2) + ADAM_EPS) + ADAM_WD * w), m_new, v_new


def _update_small(dev, vec_w, vec_m, vec_v, conv_w, conv_m, conv_v, small, landed):
    n_vec = len(vec_w)

    def body(dev_ref, *refs):
        refs = list(refs)
        take = lambda k: [refs.pop(0) for _ in range(k)]
        w_refs, m_refs, v_refs = take(n_vec), take(n_vec), take(n_vec)
        cw_ref, cm_ref, cv_ref, small_ref, land_ref, small_conv_ref, land_conv_ref = take(7)
        vec_outs, conv_outs, (loss_ref,) = take(4 * n_vec), take(4), take(1)

        def total(own_ref, others_ref):
            acc = None
            for d in range(N_DEV):
                part = jnp.where(dev_ref[0] == d, own_ref[0], others_ref[d])
                acc = part if acc is None else acc + part
            return acc

        g_all = total(small_ref, land_ref)
        for i in range(n_vec):
            g = g_all[i : i + 1]
            delta, m_new, v_new = _adam_step(w_refs[i][...], m_refs[i][...], v_refs[i][...], g)
            for ref, val in zip(vec_outs[4 * i : 4 * i + 4], (g, delta, m_new, v_new)):
                ref[...] = val
        g = total(small_conv_ref, land_conv_ref)[0:3]
        delta, m_new, v_new = _adam_step(cw_ref[0], cm_ref[0], cv_ref[0], g)
        for ref, val in zip(conv_outs, (g, delta, m_new, v_new)):
            ref[0] = val
        loss_ref[...] = jnp.broadcast_to(0.5 * jnp.sum(g_all[7:8], axis=-1, keepdims=True) / D_MODEL, loss_ref.shape)

    vec_spec = pl.BlockSpec((1, D_MODEL), lambda i, dev_ref: (0, 0))
    conv_spec = pl.BlockSpec((1, 3, CHUNK), lambda i, dev_ref: (0, 0, 0))
    vec_shape = jax.ShapeDtypeStruct((1, D_MODEL), F32)
    conv_shape = jax.ShapeDtypeStruct((1, 3, CHUNK), F32)
    outs, _ = _call(
        body,
        name="update_small",
        prefetch=[dev],
        operands=list(vec_w) + list(vec_m) + list(vec_v) + [conv_w, conv_m, conv_v, small, landed, small, landed],
        grid=(1,),
        in_specs=[vec_spec] * (3 * n_vec) + [conv_spec] * 3 + [
            pl.BlockSpec((1, 16, D_MODEL), lambda i, dev_ref: (0, 0, 0)),
            pl.BlockSpec((N_DEV, 16, D_MODEL), lambda i, dev_ref: (0, 0, 0)),
            pl.BlockSpec((1, 8, CHUNK), lambda i, dev_ref: (0, 1, dev_ref[0])),
            pl.BlockSpec((N_DEV, 8, CHUNK), lambda i, dev_ref: (0, 1, dev_ref[0])),
        ],
        out_specs=[vec_spec] * (4 * n_vec) + [conv_spec] * 4 + [pl.BlockSpec((1, CHUNK), lambda i, dev_ref: (0, 0))],
        out_shape=[vec_shape] * (4 * n_vec) + [conv_shape] * 4 + [jax.ShapeDtypeStruct((1, CHUNK), F32)],
    )
    return [outs[4 * i : 4 * i + 4] for i in range(n_vec)], outs[4 * n_vec : 4 * n_vec + 4], outs[-1]


def _adamw(name, groups, steps, own_slot=None, plans=()):
    lists = lambda p: list(p) if isinstance(p, (list, tuple)) else [p]
    groups = [(g[0], g[1], g[2], lists(g[3]), lists(g[4]) if len(g) > 4 else []) for g in groups]
    counts = [3 + len(g[3]) + len(g[4]) for g in groups]
    use_slot = own_slot is not None

    def first_tiles(w, parts):
        firsts, first = [], 0
        for p in parts:
            firsts.append(first)
            first += p.shape[1] // (w.shape[0] // steps)
        return firsts

    def update(t, slot_ref, firsts, w_ref, m_ref, v_ref, p_refs, o_refs, g_ref, d_ref, nm_ref, nv_ref):
        g = None
        for k, p_ref in enumerate(p_refs):
            gk = None
            for q in range(p_ref.shape[0]):
                part = p_ref[q].astype(F32)
                if o_refs:
                    part = jnp.where(slot_ref[0] == q, o_refs[k][0].astype(F32), part)
                gk = part if gk is None else gk + part
            g = gk if g is None else jnp.where(t >= firsts[k], gk, g)
        g_ref[...] = g
        d_ref[...], nm_ref[...], nv_ref[...] = _adam_step(w_ref[...], m_ref[...], v_ref[...], g)

    def body(*refs):
        refs = list(refs)
        slot_ref = refs.pop(0) if use_slot else None
        t = pl.program_id(0)
        ins, outs = refs[: sum(counts)], refs[sum(counts) :]
        for i, (w, _, _, parts, own) in enumerate(groups):
            mine = ins[sum(counts[:i]) : sum(counts[: i + 1])]
            update(t, slot_ref, first_tiles(w, parts), *mine[:3], mine[3 : 3 + len(parts)], mine[3 + len(parts) :],
                   *outs[4 * i : 4 * i + 4])

    operands, in_specs, out_specs, out_shape = [], [], [], []
    for w, m, v, parts, own in groups:
        r, cdim = w.shape
        tr = r // steps
        firsts = first_tiles(w, parts)

        def tile_of(k, t, firsts=firsts, parts=parts, tr=tr):
            return jnp.clip(t - firsts[k], 0, parts[k].shape[1] // tr - 1)

        blk = pl.BlockSpec((tr, cdim), lambda t, *_: (t, 0))
        operands += [w, m, v] + parts + own
        in_specs += [blk, blk, blk]
        in_specs += [pl.BlockSpec((p.shape[0], tr, cdim), lambda t, *_, k=k, tile_of=tile_of: (0, tile_of(k, t), 0)) for k, p in enumerate(parts)]
        in_specs += [pl.BlockSpec((1, tr, cdim), lambda t, slot_ref, k=k, tile_of=tile_of: (slot_ref[0], tile_of(k, t), 0)) for k in range(len(own))]
        out_specs += [blk] * 4
        out_shape += [jax.ShapeDtypeStruct((r, cdim), F32)] * 4
    outs, plan_outs = _call(
        body,
        name=name,
        prefetch=[own_slot] if use_slot else [],
        operands=operands,
        grid=(steps,),
        in_specs=in_specs,
        out_specs=out_specs,
        out_shape=out_shape,
        plans=plans,
    )
    return [outs[4 * i : 4 * i + 4] for i in range(len(groups))], plan_outs


def _chip_routes(n):
    x, y, c = _place()
    my_chip = 2 * x + y
    return [(a, (*chip, c), 2 * chip[0] + chip[1], my_chip, 2 * chip[0] + chip[1])
            for a in range(n) for chip in _other_chips(x, y)]


def _gather_routes(n):
    x, y, c = _place()
    flip = lambda v, f: 1 - v if f else v
    peers = [(flip(x, fx), flip(y, fy), flip(c, fc)) for fx in (0, 1) for fy in (0, 1) for fc in (0, 1) if fx + fy + fc]
    return [(0, p, 0, 4 * x + 2 * y + c, 4 * p[0] + 2 * p[1] + p[2]) for p in peers]


_chip_routes.copies_per_array = N_CHIP - 1
_gather_routes.copies_per_array = N_DEV - 1


def _split_start(name, routes, srcs, land_shapes):
    n = len(srcs)
    hbm = pl.BlockSpec(memory_space=pltpu.HBM)
    sem = pl.BlockSpec(memory_space=pltpu.SEMAPHORE)

    def body(*refs):
        src_refs, land_refs = refs[:n], refs[n : 2 * n]
        send_sems, recv_sems = refs[2 * n], refs[2 * n + 1]
        token = refs[-1]
        for k, (a, peer, src_slot, there, _) in enumerate(routes(n)):
            pltpu.make_async_remote_copy(
                src_ref=src_refs[a].at[src_slot], dst_ref=land_refs[a].at[there],
                send_sem=send_sems.at[k], recv_sem=recv_sems.at[k], device_id=peer, device_id_type=MESH).start()
        token[...] = jnp.zeros_like(token)

    in_hbm = [pltpu.with_memory_space_constraint(p, pltpu.HBM) for p in srcs]
    in_hbm += [pltpu.with_memory_space_constraint(lax.empty(shape, p.dtype), pltpu.HBM) for p, shape in zip(srcs, land_shapes)]
    n_copies = routes.copies_per_array * n
    res = pl.pallas_call(
        body,
        name=name,
        out_shape=[pltpu.SemaphoreType.DMA((n_copies,)), pltpu.SemaphoreType.DMA((n_copies,))]
        + [pltpu.HBM(p.shape, p.dtype) for p in srcs] + [pltpu.HBM(shape, p.dtype) for p, shape in zip(srcs, land_shapes)]
        + [jax.ShapeDtypeStruct((8, CHUNK), F32)],
        in_specs=[hbm] * (2 * n),
        out_specs=[sem, sem] + [hbm] * (2 * n) + [pl.BlockSpec(memory_space=pltpu.VMEM)],
        input_output_aliases={i: 2 + i for i in range(2 * n)},
        compiler_params=pltpu.CompilerParams(has_side_effects=pltpu.SideEffectType.DATAFLOW_SIDE_EFFECTING),
    )(*in_hbm)
    return res[:-1], res[-1]


def _split_wait(name, routes, in_flight, after):
    n = (len(in_flight) - 2) // 2
    hbm = pl.BlockSpec(memory_space=pltpu.HBM)
    sem = pl.BlockSpec(memory_space=pltpu.SEMAPHORE)
    send_sems, recv_sems, *bufs = in_flight

    def body(*refs):
        src_refs, land_refs = refs[:n], refs[n : 2 * n]
        send_ref, recv_ref = refs[2 * n], refs[2 * n + 1]
        for k, (a, peer, src_slot, _, here) in enumerate(routes(n)):
            cp = pltpu.make_async_remote_copy(
                src_ref=src_refs[a].at[src_slot], dst_ref=land_refs[a].at[here],
                send_sem=send_ref.at[k], recv_sem=recv_ref.at[k], device_id=peer, device_id_type=MESH)
            cp.wait_send()
            cp.wait_recv()

    res = pl.pallas_call(
        body,
        name=name,
        out_shape=[pltpu.HBM(b.shape, b.dtype) for b in bufs],
        in_specs=[hbm] * (2 * n) + [sem, sem] + [pl.BlockSpec(memory_space=pl.ANY)] * len(after),
        out_specs=[hbm] * (2 * n),
        input_output_aliases={i: i for i in range(2 * n)},
        compiler_params=pltpu.CompilerParams(has_side_effects=pltpu.SideEffectType.DATAFLOW_SIDE_EFFECTING),
    )(*bufs, send_sems, recv_sems, *after)
    return res[:n], res[n:]


def kernel(x, mem, norm_mix_g, w_in, conv_w, gm_ln_g, gm_ln_b, gm_ws, gm_bs, w_out, norm_x_g, norm_mem_g, w_q, w_kv, w_xo, norm_final_g, loss_target, m_norm_mix_g, m_w_in, m_conv_w, m_gm_ln_g, m_gm_ln_b, m_gm_ws, m_gm_bs, m_w_out, m_norm_x_g, m_norm_mem_g, m_w_q, m_w_kv, m_w_xo, m_norm_final_g, v_norm_mix_g, v_w_in, v_conv_w, v_gm_ln_g, v_gm_ln_b, v_gm_ws, v_gm_bs, v_w_out, v_norm_x_g, v_norm_mem_g, v_w_q, v_w_kv, v_w_xo, v_norm_final_g):
    s = x.shape[1]
    dev = 4 * lax.axis_index("x") + 2 * lax.axis_index("y") + lax.axis_index("c")
    core = lax.axis_index("c").astype(jnp.int32).reshape(1)
    x2 = x[0]
    target = loss_target[0]
    pad_taps = lambda t: jnp.pad(t[0], ((0, 5), (0, 0)))

    big_names = ["w_in", "w_out", "w_q", "w_kv", "w_xo"]
    big_w = dict(w_in=w_in[0], w_out=w_out[0], w_q=w_q[0], w_kv=w_kv[0], w_xo=w_xo[0])
    big_m = dict(w_in=m_w_in[0], w_out=m_w_out[0], w_q=m_w_q[0], w_kv=m_w_kv[0], w_xo=m_w_xo[0])
    big_v = dict(w_in=v_w_in[0], w_out=v_w_out[0], w_q=v_w_q[0], w_kv=v_w_kv[0], w_xo=v_w_xo[0])
    shard_axis = dict(w_in=1, w_out=0, w_q=0, w_kv=1, w_xo=0)
    full_shape = dict(w_in=(D_MODEL, IN_DIM), w_out=(MIX_DIM, D_MODEL), w_q=(D_MODEL, D_MODEL),
                      w_kv=(D_MODEL, 2 * D_MODEL), w_xo=(D_MODEL, D_MODEL))
    shard_size = {k: big_w[k].shape[shard_axis[k]] for k in big_names}
    others = big_names[1:]
    gather_spec = lambda k: (full_shape[k], shard_axis[k], shard_size[k])
    reduce_spec = lambda k: (shard_axis[k], shard_size[k])
    ws = gm_ws[0]
    bs_b = jnp.broadcast_to(gm_bs[0][:, :, None], (8, CHUNK, CHUNK))
    g_f = norm_final_g.reshape(1, D_MODEL)

    px, py = lax.axis_index("x"), lax.axis_index("y")
    chip_order = jnp.stack([2 * px + py, 2 * (1 - px) + py, 2 * px + 1 - py, 2 * (1 - px) + 1 - py]).astype(jnp.int32)
    (proj, h1, h1t, w_in_full), ((w_out_full, conv_g),) = _inproj(
        x2, norm_mix_g, big_w["w_in"].astype(BF16), chip_order,
        plans=[_gather_plan([big_w["w_out"].astype(BF16), pad_taps(conv_w)[None]], [gather_spec("w_out"), ((N_DEV, 8, CHUNK), 0, 1)])])
    conv_full = conv_g.transpose(1, 0, 2).reshape(8, D_MODEL)
    attn_w = ["w_q", "w_kv", "w_xo"]
    cat, (gathered,) = _mixer_fwd(
        proj, conv_full, gm_ln_g, gm_ln_b, ws, bs_b,
        plans=[_gather_plan([big_w[k].astype(BF16) for k in attn_w], [gather_spec(k) for k in attn_w])])
    wf = dict(zip(attn_w, gathered), w_out=w_out_full)
    k, vv, mt = _kv_fwd(mem[0], norm_mem_g, wf["w_kv"])
    (dcat, actt, cot, dk, dvv, d_gf, d_gx, sq_err) = _attn(
        x2, cat, target, wf["w_out"], wf["w_q"], wf["w_xo"], k, vv, norm_x_g, g_f)

    def pair_sums(names, grads, got):
        return [_pair_sum("pair_sum_" + k, grads[k], got[i], *reduce_spec(k), core) for i, k in enumerate(names)]

    sibling_plan = lambda names, grads: _sibling_plan([grads[k] for k in names], [reduce_spec(k) for k in names])
    early = ["w_kv", "w_out", "w_q", "w_xo"]
    big_g, big_gb = {}, {}
    dws, _ = _dw_attn(actt, cot)
    for name, (f32, bf16) in zip(early[1:], dws):
        big_g[name], big_gb[name] = f32, bf16
    (big_g["w_kv"], big_gb["w_kv"], d_gmem), (got_rest,) = _kv_bwd(
        mem[0], wf["w_kv"], mt, dk, dvv, plans=[sibling_plan(early[1:], big_gb)])
    pair_out, (got_kv,) = _pair_sum("pair_sum_w_out", big_g["w_out"], got_rest[0], *reduce_spec("w_out"), core,
                                    plans=[sibling_plan(early[:1], big_gb)])
    pair_kv = _pair_sum("pair_sum_w_kv", big_g["w_kv"], got_kv[0], *reduce_spec("w_kv"), core)
    (dproj, d_cw, d_lng, d_lnb, d_ws, d_bs), (parts_early,) = _mixer_bwd(
        proj, dcat, conv_full, gm_ln_g, gm_ln_b, ws, bs_b,
        plans=[_chip_plan([pair_kv, pair_out] + pair_sums(early[2:], big_g, got_rest[1:]))])
    half = D_MODEL // 2
    in_spec = reduce_spec("w_in")
    (dw_top, dwb_top), _ = _matmul_f32("dw_in_top", h1t, dproj, rows=(0, half))
    (dw_bot, dwb_bot), (got_top, (ws_all,)) = _matmul_f32(
        "dw_in_bot", h1t, dproj, rows=(half, half),
        plans=[_sibling_plan([dwb_top], [in_spec]),
               _gather_plan([d_ws.reshape(1, 8 * CHUNK, CHUNK).astype(BF16)], [((N_DEV, 8 * CHUNK, CHUNK), 0, 1)])])
    pair_top, ((got_bot,),) = _pair_sum("pair_sum_w_in_top", dw_top, got_top[0], *in_spec, core,
                                        plans=[_sibling_plan([dwb_bot], [in_spec])])
    pair_bot = _pair_sum("pair_sum_w_in_bot", dw_bot, got_bot, *in_spec, core)
    chip_shapes = [p.shape for p in (pair_top, pair_bot)]
    chips_in_flight, token = _split_start("grad_chip_exchange_start", _chip_routes, [pair_top, pair_bot], chip_shapes)
    n_tiles = s // min(TM_DH1, s)
    (grad_x, d_gmix), _ = _dh1("dh1", dproj, w_in_full, x2, cot, norm_mix_g, (0, n_tiles), after=[token])
    vec_names = ["norm_mix_g", "gm_ln_g", "gm_ln_b", "gm_bs", "norm_x_g", "norm_mem_g", "norm_final_g"]
    vec_g = [d_gmix, d_lng, d_lnb, d_bs, d_gx, d_gmem, d_gf]
    small = jnp.concatenate(vec_g + [sq_err, d_cw], axis=0)[None]
    small_in_flight, _ = _split_start("small_all_gather_start", _gather_routes, [small], [(N_DEV, 16, D_MODEL)])
    flat_ws = lambda t: t.reshape(8 * CHUNK, CHUNK)
    updated, _ = _adamw(
        "adamw_early",
        [(big_w[k], big_m[k], big_v[k], got_k) for k, got_k in zip(early, parts_early)]
        + [(flat_ws(ws), flat_ws(m_gm_ws), flat_ws(v_gm_ws), ws_all)], steps=STEPS_ADAMW // 2)
    big_out = dict(zip(early, updated[:4]))
    ws_out = updated[4]
    (pair_top, pair_bot), (land_top, land_bot) = _split_wait(
        "grad_chip_exchange_wait", _chip_routes, chips_in_flight, [d_gmix] + [u[0] for u in updated])
    my_chip = (2 * px + py).astype(jnp.int32).reshape(1)
    (big_out["w_in"],), _ = _adamw(
        "adamw_w_in", [(big_w["w_in"], big_m["w_in"], big_v["w_in"], [land_top, land_bot], [pair_top, pair_bot])],
        steps=STEPS_ADAMW, own_slot=my_chip)
    (small,), (small_all,) = _split_wait("small_all_gather_wait", _gather_routes, small_in_flight, [big_out["w_in"][0]])
    row = lambda t: t.reshape(1, D_MODEL)
    vec_out, conv_out, loss_row = _update_small(
        dev.astype(jnp.int32).reshape(1),
        [norm_mix_g, gm_ln_g, gm_ln_b, row(gm_bs), norm_x_g, norm_mem_g, row(norm_final_g)],
        [m_norm_mix_g, m_gm_ln_g, m_gm_ln_b, row(m_gm_bs), m_norm_x_g, m_norm_mem_g, row(m_norm_final_g)],
        [v_norm_mix_g, v_gm_ln_g, v_gm_ln_b, row(v_gm_bs), v_norm_x_g, v_norm_mem_g, row(v_norm_final_g)],
        conv_w, m_conv_w, v_conv_w, small, small_all)
    loss = loss_row[0, 0]

    def result(name, which):
        if name in big_out:
            return big_out[name][which][None]
        if name == "conv_w":
            return conv_out[which]
        if name == "gm_ws":
            return ws_out[which].reshape(1, 8, CHUNK, CHUNK)
        r = vec_out[vec_names.index(name)][which]
        if name == "gm_bs":
            return r.reshape(1, 8, CHUNK)
        if name == "norm_final_g":
            return r.reshape(D_MODEL)
        return r

    weights = ["norm_mix_g", "w_in", "conv_w", "gm_ln_g", "gm_ln_b", "gm_ws", "gm_bs", "w_out", "norm_x_g", "norm_mem_g", "w_q", "w_kv", "w_xo", "norm_final_g"]
    outs = [loss, grad_x[None]]
    for which in range(4):
        outs += [result(name, which) for name in weights]
    return tuple(outs)
```

```python
import math

import jax
import jax.numpy as jnp
from jax import lax
from jax.experimental import pallas as pl
from jax.experimental.pallas import tpu as pltpu

F32 = jnp.float32
BF16 = jnp.bfloat16
MESH = pl.DeviceIdType.MESH

D_MODEL = 1024
N_DEV = 8
N_CHIP = 4
CHUNK = 128
GROUPS = 7
IN_DIM = GROUPS * D_MODEL
SHARD_IN = IN_DIM // N_DEV
MIX_DIM = 2 * D_MODEL
X_HEADS = 4
X_HEAD_DIM = D_MODEL // X_HEADS
MEM_LEN = 256
EPS = 1e-6
GELU_K0 = math.sqrt(2.0 / math.pi)
GELU_K1 = 0.044715

ADAM_LR = 0.001
ADAM_B1 = 0.9
ADAM_B2 = 0.999
ADAM_EPS = 1e-08
ADAM_WD = 0.01
ADAM_STEP = 10

TM_INPROJ = 512
TM_ATTN = 512
SUB_ATTN = 256
TM_DH1 = 512
TM_DW = 512
TK_DW = 4096
TR_PAIR_SUM = 1024
BARRIER_ID = {"sibling": 1, "chips": 2, "all": 3}
STEPS_ADAMW = 4


def _dot(a, b):
    return lax.dot_general(a, b, (((1,), (0,)), ((), ())), preferred_element_type=F32)


def _dot_nt(a, b):
    return lax.dot_general(a, b, (((1,), (1,)), ((), ())), preferred_element_type=F32)


def _rows8(v):
    return jnp.sum(v.reshape(v.shape[0] // 8, 8, v.shape[1]), axis=0)


def _gelu(x, with_grad=True):
    x2 = x * x
    t = jnp.tanh(x * (GELU_K0 + (GELU_K0 * GELU_K1) * x2))
    half = 0.5 + 0.5 * t
    g = x * half
    if not with_grad:
        return g
    return g, half + ((0.5 * x) * (1.0 - t * t)) * (GELU_K0 + (3.0 * GELU_K0 * GELU_K1) * x2)


def _silu(z, with_grad=True):
    s = jax.nn.sigmoid(z)
    zs = z * s
    return (zs, s + zs * (1.0 - s)) if with_grad else zs


def _rms(v):
    r = lax.rsqrt(jnp.mean(v * v, axis=-1, keepdims=True) + EPS)
    return v * r, r


def _rms_bwd(dy_g, vh, r):
    return r * (dy_g - vh * jnp.mean(dy_g * vh, axis=-1, keepdims=True))


def _place():
    return lax.axis_index("x"), lax.axis_index("y"), lax.axis_index("c")


def _other_chips(x, y):
    return [(1 - x, y), (x, 1 - y), (1 - x, 1 - y)]


def _pow2_divisor(n):
    return n & (-n)


def _shard_view(ref, axis, size, d):
    start = d * size
    align = _pow2_divisor(size)
    if align > 1:
        start = pl.multiple_of(start, align)
    idx = [slice(None)] * len(ref.shape)
    idx[axis] = pl.ds(start, size)
    return ref.at[tuple(idx)]


class _Plan:
    def __init__(self, ins, out_shape, sems, start, finish, mid=None, aliases=None, peers="all"):
        self.ins, self.out_shape, self.sems = list(ins), list(out_shape), list(sems)
        self.start, self.mid, self.finish = start, mid, finish
        self.aliases = dict(aliases or {})
        self.peers = peers


def _gather_plan(shards, specs):
    n = len(shards)
    n_copy = 7

    def copies(ins, outs, sems):
        send_sems, recv_sems, local_sems = sems
        x, y, c = _place()
        me, sibling = (x, y, c), (x, y, 1 - c)
        chips = _other_chips(x, y)

        def block(a, p):
            _, axis, size = specs[a]
            return _shard_view(outs[a], axis, size, 4 * p[0] + 2 * p[1] + p[2])

        def copy(a, k, p, to, src=None):
            return pltpu.make_async_remote_copy(
                src_ref=block(a, p) if src is None else src,
                dst_ref=block(a, p),
                send_sem=send_sems.at[a * n_copy + k],
                recv_sem=recv_sems.at[a * n_copy + k],
                device_id=to,
                device_id_type=MESH,
            )

        def mine():
            return [pltpu.make_async_copy(ins[a], block(a, me), local_sems.at[a]) for a in range(n)]

        def first():
            return [cp for a in range(n) for cp in
                    [copy(a, 0, me, sibling, src=ins[a])] + [copy(a, 1 + j, me, (*chip, c), src=ins[a]) for j, chip in enumerate(chips)]]

        def landed():
            return [copy(a, 1 + j, (*chip, c), me) for j, chip in enumerate(chips) for a in range(n)]

        def passed():
            return [copy(a, 4 + j, (*chip, c), sibling) for j, chip in enumerate(chips) for a in range(n)]

        def last():
            return [cp for a in range(n) for cp in
                    [copy(a, 0, sibling, me)] + [copy(a, 4 + j, (*chip, 1 - c), me) for j, chip in enumerate(chips)]]

        return mine, first, landed, passed, last

    def start(ins, outs, sems):
        mine, first, _, _, _ = copies(ins, outs, sems)
        for cp in mine() + first():
            cp.start()

    def mid(ins, outs, sems):
        _, _, landed, passed, _ = copies(ins, outs, sems)
        for got, fwd in zip(landed(), passed()):
            got.wait_recv()
            fwd.start()

    def finish(ins, outs, sems):
        mine, first, _, passed, last = copies(ins, outs, sems)
        for cp in last():
            cp.wait_recv()
        for cp in first() + passed():
            cp.wait_send()
        for cp in mine():
            cp.wait()

    return _Plan(
        shards,
        [jax.ShapeDtypeStruct(full, s.dtype) for s, (full, _, _) in zip(shards, specs)],
        [pltpu.SemaphoreType.DMA((n * n_copy,)), pltpu.SemaphoreType.DMA((n * n_copy,)), pltpu.SemaphoreType.DMA((n,))],
        start, finish, mid,
    )


def _sibling_plan(grads, specs):
    n = len(grads)

    def shard_shape(a):
        axis, size = specs[a]
        shp = list(grads[a].shape)
        shp[axis] = size
        return tuple(shp)

    def copies(ins, outs, sems):
        send_sems, recv_sems = sems
        x, y, c = _place()
        return [
            pltpu.make_async_remote_copy(
                src_ref=_shard_view(ins[a], specs[a][0], specs[a][1], 2 * q + (1 - c)),
                dst_ref=outs[a].at[q],
                send_sem=send_sems.at[a * N_CHIP + q],
                recv_sem=recv_sems.at[a * N_CHIP + q],
                device_id=(x, y, 1 - c),
                device_id_type=MESH,
            )
            for a in range(n) for q in range(N_CHIP)
        ]

    def start(ins, outs, sems):
        for cp in copies(ins, outs, sems):
            cp.start()

    def finish(ins, outs, sems):
        for cp in copies(ins, outs, sems):
            cp.wait()

    return _Plan(
        grads,
        [jax.ShapeDtypeStruct((N_CHIP, *shard_shape(a)), grads[a].dtype) for a in range(n)],
        [pltpu.SemaphoreType.DMA((n * N_CHIP,)), pltpu.SemaphoreType.DMA((n * N_CHIP,))],
        start, finish, peers="sibling",
    )


def _chip_plan(parts, full_rows=None, row0=None, into=None):
    n = len(parts)
    full_rows = [p.shape[1] for p in parts] if full_rows is None else full_rows
    row0 = [0] * n if row0 is None else row0

    def copies(ins, outs, sems):
        send_sems, recv_sems, local_sems = sems
        x, y, c = _place()
        my_chip = 2 * x + y
        chips = _other_chips(x, y)

        def land(a, q):
            return outs[a].at[q, pl.ds(row0[a], parts[a].shape[1]), :]

        def mine():
            return [pltpu.make_async_copy(ins[a].at[my_chip], land(a, my_chip), local_sems.at[a]) for a in range(n)]

        def remote(sending):
            return [
                pltpu.make_async_remote_copy(
                    src_ref=ins[a].at[2 * chip[0] + chip[1]],
                    dst_ref=land(a, my_chip if sending else 2 * chip[0] + chip[1]),
                    send_sem=send_sems.at[a * 3 + j], recv_sem=recv_sems.at[a * 3 + j],
                    device_id=(*chip, c), device_id_type=MESH)
                for a in range(n) for j, chip in enumerate(chips)
            ]

        return mine, remote

    def start(ins, outs, sems):
        mine, remote = copies(ins, outs, sems)
        for cp in mine() + remote(True):
            cp.start()

    def finish(ins, outs, sems):
        mine, remote = copies(ins, outs, sems)
        for cp in remote(False):
            cp.wait_recv()
        for cp in remote(True):
            cp.wait_send()
        for cp in mine():
            cp.wait()

    return _Plan(
        list(parts) + (list(into) if into is not None else []),
        [jax.ShapeDtypeStruct((N_CHIP, full_rows[a], parts[a].shape[2]), parts[a].dtype) for a in range(n)],
        [pltpu.SemaphoreType.DMA((n * 3,)), pltpu.SemaphoreType.DMA((n * 3,)), pltpu.SemaphoreType.DMA((n,))],
        start, finish,
        aliases={n + a: a for a in range(n)} if into is not None else None, peers="chips",
    )


def _call(body, *, name, operands, out_shape, grid=(), in_specs=None, out_specs=None, scratch_shapes=(),
          plans=(), aliases=None, prefetch=(), start_after_body=False, pass_on_at=0.6, body_peers=None):
    operands, out_shape, scratch_shapes = list(operands), list(out_shape), list(scratch_shapes)
    n_pf, n_in, n_out, n_scr = len(prefetch), len(operands), len(out_shape), len(scratch_shapes)
    in_vmem = pl.BlockSpec(memory_space=pltpu.VMEM)
    in_hbm = pl.BlockSpec(memory_space=pl.ANY)
    in_specs = [in_vmem] * n_in if in_specs is None else list(in_specs)
    out_specs = [in_vmem] * n_out if out_specs is None else list(out_specs)
    io_alias = {n_pf + i: o for i, o in (aliases or {}).items()}
    pos_in, pos_out = n_pf + n_in, n_out
    for p in plans:
        for i, o in p.aliases.items():
            io_alias[pos_in + i] = pos_out + o
        pos_in += len(p.ins)
        pos_out += len(p.out_shape)
    steps = math.prod(grid) if grid else 1
    mid_step = min(steps - 1, int(steps * pass_on_at))
    patterns = {p.peers for p in plans} | ({body_peers} if body_peers else set())
    pattern = None if not patterns else patterns.pop() if len(patterns) == 1 else "all"

    def shake_hands():
        x, y, c = _place()
        peers = [(x, y, 1 - c)] if pattern in ("sibling", "all") else []
        peers += [(*chip, c) for chip in _other_chips(x, y)] if pattern in ("chips", "all") else []
        barrier = pltpu.get_barrier_semaphore()
        for peer in peers:
            pl.semaphore_signal(barrier, inc=1, device_id=peer, device_id_type=MESH)
        pl.semaphore_wait(barrier, len(peers))

    def wrapped(*refs):
        refs = list(refs)
        take = lambda k: [refs.pop(0) for _ in range(k)]
        pf = take(n_pf)
        ins, p_ins = take(n_in), [take(len(p.ins)) for p in plans]
        outs, p_outs = take(n_out), [take(len(p.out_shape)) for p in plans]
        scr, p_sems = take(n_scr), [take(len(p.sems)) for p in plans]
        step = 0
        for ax, g in enumerate(grid):
            step = step * g + pl.program_id(ax)

        def hook(kind, at):
            todo = [(getattr(p, kind), a, b, c) for p, a, b, c in zip(plans, p_ins, p_outs, p_sems) if getattr(p, kind)]

            def run():
                for fn, a, b, c in todo:
                    fn(a, b, c)

            if todo and grid:
                pl.when(step == at)(run)
            elif todo:
                run()

        if pattern and grid:
            pl.when(step == 0)(shake_hands)
        elif pattern:
            shake_hands()
        if not start_after_body:
            hook("start", 0)
        if body is not None:
            body(*pf, *ins, *outs, *scr)
        if start_after_body:
            hook("start", 0)
        hook("mid", mid_step)
        hook("finish", steps - 1)

    layout = dict(
        grid=grid,
        in_specs=in_specs + [in_hbm] * sum(len(p.ins) for p in plans),
        out_specs=out_specs + [in_hbm] * sum(len(p.out_shape) for p in plans),
        scratch_shapes=scratch_shapes + [s for p in plans for s in p.sems],
    )
    if n_pf:
        layout = dict(grid_spec=pltpu.PrefetchScalarGridSpec(num_scalar_prefetch=n_pf, **layout))
    res = pl.pallas_call(
        wrapped,
        name=name,
        out_shape=out_shape + [o for p in plans for o in p.out_shape],
        input_output_aliases=io_alias,
        compiler_params=pltpu.CompilerParams(
            dimension_semantics=("arbitrary",) * len(grid) if grid else None,
            collective_id=BARRIER_ID[pattern] if pattern else None),
        **layout,
    )(*prefetch, *operands, *[a for p in plans for a in p.ins])
    res = list(res)
    outs, plan_outs = res[:n_out], []
    pos = n_out
    for p in plans:
        plan_outs.append(res[pos : pos + len(p.out_shape)])
        pos += len(p.out_shape)
    return outs, plan_outs


def _inproj(x, g, w_shard, order, plans=()):
    s = x.shape[0]
    tm = min(TM_INPROJ, s)
    nt = s // tm
    wide = 2 * SHARD_IN
    fwd_at = nt // 2

    def body(order_ref, x_ref, g_ref, w_hbm, proj_ref, h1_ref, h1t_ref, wfull_hbm,
             w_scr, h1_scr, send_sems, recv_sems, local_sems):
        k, i = pl.program_id(0), pl.program_id(1)
        px, py, c = _place()
        me, sibling = (px, py, c), (px, py, 1 - c)
        chips = _other_chips(px, py)

        def half(slot, core):
            return w_scr.at[slot, :, pl.ds(pl.multiple_of(core * SHARD_IN, 128), SHARD_IN)]

        def copy(n, slot, core, to, src=None):
            return pltpu.make_async_remote_copy(
                src_ref=half(slot, core) if src is None else src, dst_ref=half(slot, core),
                send_sem=send_sems.at[n], recv_sem=recv_sems.at[n], device_id=to, device_id_type=MESH)

        def mine():
            return pltpu.make_async_copy(w_hbm, half(0, c), local_sems.at[0])

        def first():
            return [copy(0, 0, c, sibling, src=w_hbm)] + [copy(1 + j, 1 + j, c, (*chips[j], c), src=w_hbm) for j in range(2)]

        def relay():
            to = (c * px + (1 - c) * (1 - px), c * (1 - py) + (1 - c) * py, c)
            return pltpu.make_async_remote_copy(
                src_ref=half(2 - c, c), dst_ref=half(3, c), send_sem=send_sems.at[3], recv_sem=recv_sems.at[3],
                device_id=to, device_id_type=MESH)

        def save(slot):
            cols = pl.ds(pl.multiple_of(order_ref[slot] * wide, 128), wide)
            return pltpu.make_async_copy(w_scr.at[slot], wfull_hbm.at[:, cols], local_sems.at[1 + slot])

        @pl.when((k == 0) & (i == 0))
        def _():
            mine().start()
            for cp in first():
                cp.start()
            mine().wait()
            copy(0, 0, 1 - c, me).wait_recv()

        for j in range(3):
            @pl.when((k == j + 1) & (i == 0))
            def _():
                copy(4 + j, 1 + j, 1 - c, me).wait_recv()

        @pl.when((k == 3) & (i == 0))
        def _():
            for slot in range(N_CHIP):
                save(slot).start()

        @pl.when(k == 0)
        def _():
            xh, _ = _rms(x_ref[...])
            h = xh * g_ref[...]
            h1_ref[...] = h.astype(BF16)
            h1t_ref[...] = h.T.astype(BF16)
            h1_scr[pl.ds(pl.multiple_of(i * tm, tm), tm), :] = h.astype(BF16)

        proj_ref[...] = _dot(h1_scr[pl.ds(pl.multiple_of(i * tm, tm), tm), :], w_scr[k]).astype(BF16)

        for j in range(3):
            @pl.when((k == 0) & (i == nt - 1) if j < 2 else (k == 2) & (i == fwd_at))
            def _():
                copy(1 + j, 1 + j, c, me).wait_recv()
                copy(4 + j, 1 + j, c, sibling).start()
                if j < 2:
                    pl.when(c == 1 - j)(lambda: relay().start())

        @pl.when((k == 3) & (i == nt - 1))
        def _():
            for cp in first() + [relay()] + [copy(4 + j, 1 + j, c, sibling) for j in range(3)]:
                cp.wait_send()
            for slot in range(N_CHIP):
                save(slot).wait()

    once = lambda k, i, order_ref: (jnp.where(k == 0, i, nt - 1), 0)
    return _call(
        body,
        name="inproj",
        prefetch=[order],
        operands=[x, g, w_shard],
        grid=(N_CHIP, nt),
        in_specs=[
            pl.BlockSpec((tm, D_MODEL), once),
            pl.BlockSpec((1, D_MODEL), lambda k, i, order_ref: (0, 0)),
            pl.BlockSpec(memory_space=pl.ANY),
        ],
        out_specs=[
            pl.BlockSpec((tm, wide), lambda k, i, order_ref: (i, order_ref[k])),
            pl.BlockSpec((tm, D_MODEL), once),
            pl.BlockSpec((D_MODEL, tm), lambda k, i, order_ref: (0, jnp.where(k == 0, i, nt - 1))),
            pl.BlockSpec(memory_space=pl.ANY),
        ],
        out_shape=[
            jax.ShapeDtypeStruct((s, IN_DIM), BF16),
            jax.ShapeDtypeStruct((s, D_MODEL), BF16),
            jax.ShapeDtypeStruct((D_MODEL, s), BF16),
            jax.ShapeDtypeStruct((D_MODEL, IN_DIM), BF16),
        ],
        scratch_shapes=[
            pltpu.VMEM((N_CHIP, D_MODEL, wide), BF16),
            pltpu.VMEM((s, D_MODEL), BF16),
            pltpu.SemaphoreType.DMA((7,)),
            pltpu.SemaphoreType.DMA((7,)),
            pltpu.SemaphoreType.DMA((1 + N_CHIP,)),
        ],
        plans=plans,
        start_after_body=True,
        pass_on_at=0.8,
        body_peers="all",
    )


def _rows_from_above(v, above):
    row = lax.broadcasted_iota(jnp.int32, (8, v.shape[1]), 0)
    r1, r2 = pltpu.roll(v, 1, 0), pltpu.roll(v, 2, 0)
    top1 = jnp.where(row == 0, above[7:8], r1[0:8])
    top2 = jnp.where(row == 0, above[6:7], jnp.where(row == 1, above[7:8], r2[0:8]))
    return jnp.concatenate([top1, r1[8:]], axis=0), jnp.concatenate([top2, r2[8:]], axis=0)


def _rows_from_below(v, below):
    n = v.shape[0]
    row = lax.broadcasted_iota(jnp.int32, (8, v.shape[1]), 0)
    r1, r2 = pltpu.roll(v, n - 1, 0), pltpu.roll(v, n - 2, 0)
    end1 = jnp.where(row == 7, below[0:1], r1[n - 8 :])
    end2 = jnp.where(row == 6, below[0:1], jnp.where(row == 7, below[1:2], r2[n - 8 :]))
    return jnp.concatenate([r1[: n - 8], end1], axis=0), jnp.concatenate([r2[: n - 8], end2], axis=0)


def _causal_mask():
    row = lax.broadcasted_iota(jnp.int32, (CHUNK, CHUNK), 0)
    col = lax.broadcasted_iota(jnp.int32, (CHUNK, CHUNK), 1)
    return row, row >= col


def _mixer_fwd(proj, conv_w, ln_g, ln_b, ws, bs_b, plans=()):
    s = proj.shape[0]
    nt = s // CHUNK

    def body(proj_ref, cw_ref, lng_ref, lnb_ref, ws_ref, bsb_ref, cat_ref, pcar_ref):
        @pl.when(pl.program_id(0) == 0)
        def _():
            pcar_ref[...] = jnp.zeros_like(pcar_ref)

        row, tril = _causal_mask()
        for j in range(8):
            cs = slice(CHUNK * j, CHUNK * (j + 1))

            def grp(k):
                return proj_ref[:, k * D_MODEL + CHUNK * j : k * D_MODEL + CHUNK * (j + 1)].astype(F32)

            w = cw_ref[:, cs]
            p = grp(1) * grp(2)
            p1, p2 = _rows_from_above(p, pcar_ref[:, cs])
            pcar_ref[:, cs] = p[CHUNK - 8 :]
            cv = w[0:1] * p2 + w[1:2] * p1 + w[2:3] * p
            sa = _silu(grp(3), with_grad=False)
            cat_ref[:, cs] = ((grp(0) * cv) * sa).astype(BF16)
            gu = _gelu(grp(4), with_grad=False)
            gv = _gelu(grp(5), with_grad=False)
            dv = gv - jnp.mean(gv, axis=-1, keepdims=True)
            vn = dv * lax.rsqrt(jnp.mean(dv * dv, axis=-1, keepdims=True) + EPS)
            vn = vn * lng_ref[:, cs] + lnb_ref[:, cs]
            wc = jnp.where(tril, ws_ref[j], 0.0).astype(BF16)
            sp = _dot(wc, vn.astype(BF16)) + bsb_ref[j]
            sb = _silu(grp(6), with_grad=False)
            cat_ref[:, D_MODEL + CHUNK * j : D_MODEL + CHUNK * (j + 1)] = ((gu * sp) * sb).astype(BF16)

    whole = lambda shape: pl.BlockSpec(shape, lambda i: (0,) * len(shape))
    (cat,), plan_outs = _call(
        body,
        name="mixer_fwd",
        operands=[proj, conv_w, ln_g, ln_b, ws, bs_b],
        grid=(nt,),
        in_specs=[
            pl.BlockSpec((CHUNK, IN_DIM), lambda i: (i, 0)),
            whole((8, D_MODEL)),
            whole((1, D_MODEL)),
            whole((1, D_MODEL)),
            whole((8, CHUNK, CHUNK)),
            whole((8, CHUNK, CHUNK)),
        ],
        out_specs=[pl.BlockSpec((CHUNK, MIX_DIM), lambda i: (i, 0))],
        out_shape=[jax.ShapeDtypeStruct((s, MIX_DIM), BF16)],
        scratch_shapes=[pltpu.VMEM((8, D_MODEL), F32)],
        plans=plans,
        pass_on_at=0.85,
    )
    return cat, plan_outs


def _mixer_bwd(proj, dcat, conv_w, ln_g, ln_b, ws, bs_b, plans=()):
    s = proj.shape[0]
    nt = s // CHUNK

    def body(proj_ref, halo_ref, dcat_ref, cw_ref, lng_ref, lnb_ref, ws_ref, bsb_ref,
             dproj_ref, dcw_ref, dlng_ref, dlnb_ref, dws_ref, dbs_ref,
             car_ref, acc_cw, acc_lng, acc_lnb, acc_bs, wc_scr, wct_scr):
        i = pl.program_id(0)
        tile = nt - 1 - i
        row, tril = _causal_mask()

        @pl.when(i == 0)
        def _():
            for j in range(8):
                wc = jnp.where(tril, ws_ref[j], 0.0)
                wc_scr[j] = wc.astype(BF16)
                wct_scr[j] = wc.T.astype(BF16)
            car_ref[...] = jnp.zeros_like(car_ref)
            acc_cw[...] = jnp.zeros_like(acc_cw)
            acc_lng[...] = jnp.zeros_like(acc_lng)
            acc_lnb[...] = jnp.zeros_like(acc_lnb)
            acc_bs[...] = jnp.zeros_like(acc_bs)
            dws_ref[...] = jnp.zeros_like(dws_ref)

        has_prev = jnp.where(tile > 0, 1.0, 0.0).astype(F32)
        for j in range(8):
            cs = slice(CHUNK * j, CHUNK * (j + 1))

            def col(k):
                return slice(k * D_MODEL + CHUNK * j, k * D_MODEL + CHUNK * (j + 1))

            def grp(k):
                return proj_ref[:, col(k)].astype(F32)

            w = cw_ref[:, cs]
            gb, gc, xa = grp(0), grp(1), grp(2)
            p = gc * xa
            pprev = halo_ref[:, col(1)].astype(F32) * halo_ref[:, col(2)].astype(F32) * has_prev
            p1, p2 = _rows_from_above(p, pprev[8:16])
            cv = w[0:1] * p2 + w[1:2] * p1 + w[2:3] * p
            za = grp(3)
            sa, dsa = _silu(za)
            da = dcat_ref[:, cs].astype(F32)
            da_gb = da * gb
            dproj_ref[:, col(0)] = ((da * sa) * cv).astype(BF16)
            dproj_ref[:, col(3)] = ((da_gb * cv) * dsa).astype(BF16)
            dcv = da_gb * sa
            d1, d2 = _rows_from_below(dcv, car_ref[:, cs])
            car_ref[:, cs] = dcv[0:8]
            dp = w[2:3] * dcv + w[1:2] * d1 + w[0:1] * d2
            dproj_ref[:, col(1)] = (dp * xa).astype(BF16)
            dproj_ref[:, col(2)] = (dp * gc).astype(BF16)
            acc_cw[0, :, cs] += _rows8(dcv * p2)
            acc_cw[1, :, cs] += _rows8(dcv * p1)
            acc_cw[2, :, cs] += _rows8(dcv * p)
            gu, dgu = _gelu(grp(4))
            gv, dgv = _gelu(grp(5))
            dv = gv - jnp.mean(gv, axis=-1, keepdims=True)
            rstd = lax.rsqrt(jnp.mean(dv * dv, axis=-1, keepdims=True) + EPS)
            vnh = dv * rstd
            lng = lng_ref[:, cs]
            vnb = (vnh * lng + lnb_ref[:, cs]).astype(BF16)
            sp = _dot(wc_scr[j], vnb) + bsb_ref[j]
            zb = grp(6)
            sb, dsb = _silu(zb)
            db = dcat_ref[:, D_MODEL + CHUNK * j : D_MODEL + CHUNK * (j + 1)].astype(F32)
            db_sp, db_gu = db * sp, db * gu
            dproj_ref[:, col(4)] = ((db_sp * sb) * dgu).astype(BF16)
            dproj_ref[:, col(6)] = ((db_sp * gu) * dsb).astype(BF16)
            dsp = db_gu * sb
            acc_bs[j] += dsp
            dspb = dsp.astype(BF16)
            dws_ref[j] += _dot_nt(dspb, vnb)
            dvn = _dot(wct_scr[j], dspb)
            acc_lnb[:, cs] += _rows8(dvn)
            acc_lng[:, cs] += _rows8(dvn * vnh)
            dvh = dvn * lng
            dgvv = rstd * (dvh - jnp.mean(dvh, axis=-1, keepdims=True) - vnh * jnp.mean(dvh * vnh, axis=-1, keepdims=True))
            dproj_ref[:, col(5)] = (dgvv * dgv).astype(BF16)

        @pl.when(i == nt - 1)
        def _():
            dlng_ref[...] = jnp.sum(acc_lng[...], axis=0, keepdims=True)
            dlnb_ref[...] = jnp.sum(acc_lnb[...], axis=0, keepdims=True)
            dcw_ref[...] = jnp.zeros_like(dcw_ref)
            for k in range(3):
                dcw_ref[k : k + 1, :] = jnp.sum(acc_cw[k], axis=0, keepdims=True)
            for j in range(8):
                dws_ref[j] = jnp.where(tril, dws_ref[j], 0.0)
                dbs_ref[:, CHUNK * j : CHUNK * (j + 1)] = jnp.sum(acc_bs[j].T, axis=0, keepdims=True)

    whole = lambda shape: pl.BlockSpec(shape, lambda i: (0,) * len(shape))
    halo_rows = 16
    per = CHUNK // halo_rows
    return _call(
        body,
        name="mixer_bwd",
        operands=[proj, proj, dcat, conv_w, ln_g, ln_b, ws, bs_b],
        plans=plans,
        grid=(nt,),
        in_specs=[
            pl.BlockSpec((CHUNK, IN_DIM), lambda i: (nt - 1 - i, 0)),
            pl.BlockSpec((halo_rows, IN_DIM), lambda i: (jnp.maximum((nt - 1 - i) * per - 1, 0), 0)),
            pl.BlockSpec((CHUNK, MIX_DIM), lambda i: (nt - 1 - i, 0)),
            whole((8, D_MODEL)),
            whole((1, D_MODEL)),
            whole((1, D_MODEL)),
            whole((8, CHUNK, CHUNK)),
            whole((8, CHUNK, CHUNK)),
        ],
        out_specs=[
            pl.BlockSpec((CHUNK, IN_DIM), lambda i: (nt - 1 - i, 0)),
            whole((8, D_MODEL)),
            whole((1, D_MODEL)),
            whole((1, D_MODEL)),
            whole((8, CHUNK, CHUNK)),
            whole((1, D_MODEL)),
        ],
        out_shape=[
            jax.ShapeDtypeStruct((s, IN_DIM), BF16),
            jax.ShapeDtypeStruct((8, D_MODEL), F32),
            jax.ShapeDtypeStruct((1, D_MODEL), F32),
            jax.ShapeDtypeStruct((1, D_MODEL), F32),
            jax.ShapeDtypeStruct((8, CHUNK, CHUNK), F32),
            jax.ShapeDtypeStruct((1, D_MODEL), F32),
        ],
        scratch_shapes=[
            pltpu.VMEM((8, D_MODEL), F32),
            pltpu.VMEM((3, 8, D_MODEL), F32),
            pltpu.VMEM((8, D_MODEL), F32),
            pltpu.VMEM((8, D_MODEL), F32),
            pltpu.VMEM((8, CHUNK, CHUNK), F32),
            pltpu.VMEM((8, CHUNK, CHUNK), BF16),
            pltpu.VMEM((8, CHUNK, CHUNK), BF16),
        ],
    )


def _kv_fwd(mem, g_mem, w_kv):
    half = N_DEV // 2

    def body(mem_ref, g_ref, w_ref, k_ref, vv_ref, mt_ref, m_scr):
        j = pl.program_id(0)

        @pl.when(j == 0)
        def _():
            mh, _ = _rms(mem_ref[...])
            m = mh * g_ref[...]
            m_scr[...] = m.astype(BF16)
            mt_ref[...] = m.T.astype(BF16)

        kv = _dot(m_scr[...], w_ref[...]).astype(BF16)

        @pl.when(j < half)
        def _():
            k_ref[...] = kv

        @pl.when(j >= half)
        def _():
            vv_ref[...] = kv

    whole = lambda shape: pl.BlockSpec(shape, lambda j: (0,) * len(shape))
    return pl.pallas_call(
        body,
        name="kv_fwd",
        grid=(N_DEV,),
        in_specs=[whole((MEM_LEN, D_MODEL)), whole((1, D_MODEL)), pl.BlockSpec((D_MODEL, X_HEAD_DIM), lambda j: (0, j))],
        out_specs=[
            pl.BlockSpec((MEM_LEN, X_HEAD_DIM), lambda j: (0, jnp.minimum(j, half - 1))),
            pl.BlockSpec((MEM_LEN, X_HEAD_DIM), lambda j: (0, jnp.maximum(j - half, 0))),
            whole((D_MODEL, MEM_LEN)),
        ],
        out_shape=[
            jax.ShapeDtypeStruct((MEM_LEN, D_MODEL), BF16),
            jax.ShapeDtypeStruct((MEM_LEN, D_MODEL), BF16),
            jax.ShapeDtypeStruct((D_MODEL, MEM_LEN), BF16),
        ],
        scratch_shapes=[pltpu.VMEM((MEM_LEN, D_MODEL), BF16)],
        compiler_params=pltpu.CompilerParams(dimension_semantics=("arbitrary",)),
    )(mem, g_mem, w_kv)


def _kv_bwd(mem, w_kv, mt, dk, dvv):
    half = N_DEV // 2

    def body(mem_ref, w_ref, mt_ref, dk_ref, dvv_ref, dw_ref, dwb_ref, dg_ref, dm_scr):
        j = pl.program_id(0)
        dkv = jnp.where(j < half, dk_ref[...], dvv_ref[...]).astype(BF16)
        dw = _dot(mt_ref[...], dkv)
        dw_ref[...] = dw
        dwb_ref[...] = dw.astype(BF16)
        dm = _dot_nt(dkv, w_ref[...])

        @pl.when(j == 0)
        def _():
            dm_scr[...] = dm

        @pl.when(j > 0)
        def _():
            dm_scr[...] += dm

        @pl.when(j == N_DEV - 1)
        def _():
            mh, _ = _rms(mem_ref[...])
            dg_ref[...] = jnp.sum(dm_scr[...] * mh, axis=0, keepdims=True)

    whole = lambda shape: pl.BlockSpec(shape, lambda j: (0,) * len(shape))
    cols = pl.BlockSpec((D_MODEL, X_HEAD_DIM), lambda j: (0, j))
    return pl.pallas_call(
        body,
        name="kv_bwd",
        grid=(N_DEV,),
        in_specs=[
            whole((MEM_LEN, D_MODEL)), cols, whole((D_MODEL, MEM_LEN)),
            pl.BlockSpec((MEM_LEN, X_HEAD_DIM), lambda j: (0, jnp.minimum(j, half - 1))),
            pl.BlockSpec((MEM_LEN, X_HEAD_DIM), lambda j: (0, jnp.maximum(j - half, 0))),
        ],
        out_specs=[cols, cols, whole((1, D_MODEL))],
        out_shape=[
            jax.ShapeDtypeStruct((D_MODEL, 2 * D_MODEL), F32),
            jax.ShapeDtypeStruct((D_MODEL, 2 * D_MODEL), BF16),
            jax.ShapeDtypeStruct((1, D_MODEL), F32),
        ],
        scratch_shapes=[pltpu.VMEM((MEM_LEN, D_MODEL), F32)],
        compiler_params=pltpu.CompilerParams(dimension_semantics=("arbitrary",)),
    )(mem, w_kv, mt, dk, dvv)


def _attn(x, cat, target, w_out, w_q, w_xo, k, vv, g_x, g_f):
    s = x.shape[0]
    tm = min(TM_ATTN, s)
    sub = min(SUB_ATTN, tm)
    nt = s // tm
    scale = 1.0 / math.sqrt(X_HEAD_DIM)

    def body(x_ref, cat_ref, t_ref, wout_ref, wq_ref, wxo_ref, k_ref, vv_ref, gx_ref, gf_ref,
             dcat_ref, actt_ref, cot_ref,
             dk_ref, dvv_ref, dgf_ref, dgx_ref, loss_ref,
             p_scr, q_scr, o_scr, dq_scr, acc_gf, acc_gx, acc_loss):
        i = pl.program_id(0)

        @pl.when(i == 0)
        def _():
            dk_ref[...] = jnp.zeros_like(dk_ref)
            dvv_ref[...] = jnp.zeros_like(dvv_ref)
            acc_gf[...] = jnp.zeros_like(acc_gf)
            acc_gx[...] = jnp.zeros_like(acc_gx)
            acc_loss[...] = jnp.zeros_like(acc_loss)

        subs = [slice(u * sub, (u + 1) * sub) for u in range(tm // sub)]
        heads = [slice(X_HEAD_DIM * h, X_HEAD_DIM * (h + 1)) for h in range(X_HEADS)]
        gx, gf = gx_ref[...], gf_ref[...]
        x1, x1h, r2 = [], [], []
        for rs in subs:
            cat = cat_ref[rs, :]
            actt_ref[0:MIX_DIM, rs] = cat.astype(F32).T.astype(BF16)
            v = x_ref[rs, :] + _dot(cat, wout_ref[...])
            vh, r = _rms(v)
            h2 = vh * gx
            actt_ref[MIX_DIM : MIX_DIM + D_MODEL, rs] = h2.T.astype(BF16)
            q_scr[rs, :] = _dot(h2.astype(BF16), wq_ref[...]).astype(BF16)
            x1.append(v), x1h.append(vh), r2.append(r)
        for rs in subs:
            for hs in heads:
                sc = _dot_nt(q_scr[rs, hs], k_ref[:, hs]) * scale
                e = jnp.exp(sc - jnp.max(sc, axis=-1, keepdims=True))
                p = e / jnp.sum(e, axis=-1, keepdims=True)
                p_scr[rs, hs] = p
                o_scr[rs, hs] = _dot(p.astype(BF16), vv_ref[:, hs])
        dx2, sq, d_gf = [], None, None
        for u, rs in enumerate(subs):
            o = o_scr[rs, :]
            actt_ref[MIX_DIM + D_MODEL :, rs] = o.T.astype(BF16)
            x2 = x1[u] + _dot(o.astype(BF16), wxo_ref[...])
            x2h, r3 = _rms(x2)
            err = x2h * gf - t_ref[rs, :]
            dy = err * (1.0 / D_MODEL)
            sq = _rows8(err * err) if sq is None else sq + _rows8(err * err)
            d_gf = _rows8(dy * x2h) if d_gf is None else d_gf + _rows8(dy * x2h)
            dx2.append(_rms_bwd(dy * gf, x2h, r3))
        acc_loss[...] += sq
        acc_gf[...] += d_gf
        d_vv, d_k = [None] * X_HEADS, [None] * X_HEADS
        for u, rs in enumerate(subs):
            dx2b = dx2[u].astype(BF16)
            cot_ref[2, rs, :] = dx2b
            dob = _dot_nt(dx2b, wxo_ref[...]).astype(BF16)
            for h, hs in enumerate(heads):
                p = p_scr[rs, hs]
                dp = _dot_nt(dob[:, hs], vv_ref[:, hs])
                ds = p * (dp - jnp.sum(dp * p, axis=-1, keepdims=True)) * scale
                dq_scr[rs, hs] = _dot(ds.astype(BF16), k_ref[:, hs]).astype(BF16)
                dvv_h = _dot(p.T.astype(BF16), dob[:, hs])
                dk_h = _dot(ds.T.astype(BF16), q_scr[rs, hs])
                d_vv[h] = dvv_h if d_vv[h] is None else d_vv[h] + dvv_h
                d_k[h] = dk_h if d_k[h] is None else d_k[h] + dk_h
        for h, hs in enumerate(heads):
            dvv_ref[:, hs] += d_vv[h]
            dk_ref[:, hs] += d_k[h]
        d_gx = None
        for u, rs in enumerate(subs):
            dqb = dq_scr[rs, :]
            cot_ref[1, rs, :] = dqb
            dh2 = _dot_nt(dqb, wq_ref[...])
            d_gx = _rows8(dh2 * x1h[u]) if d_gx is None else d_gx + _rows8(dh2 * x1h[u])
            dx1 = dx2[u] + _rms_bwd(dh2 * gx, x1h[u], r2[u])
            dx1b = dx1.astype(BF16)
            cot_ref[0, rs, :] = dx1b
            dcat_ref[rs, :] = _dot_nt(dx1b, wout_ref[...]).astype(BF16)
        acc_gx[...] += d_gx

        @pl.when(i == nt - 1)
        def _():
            dgf_ref[...] = jnp.sum(acc_gf[...], axis=0, keepdims=True)
            dgx_ref[...] = jnp.sum(acc_gx[...], axis=0, keepdims=True)
            loss_ref[...] = jnp.sum(acc_loss[...], axis=0, keepdims=True)

    whole = lambda shape: pl.BlockSpec(shape, lambda i: (0,) * len(shape))
    once = lambda shape: pl.BlockSpec(shape, lambda i: (0,) * len(shape), pipeline_mode=pl.Buffered(1))
    rows = lambda width: pl.BlockSpec((tm, width), lambda i: (i, 0))
    cols = lambda height: pl.BlockSpec((height, tm), lambda i: (0, i))
    vec = jax.ShapeDtypeStruct((1, D_MODEL), F32)
    return pl.pallas_call(
        body,
        name="attn",
        grid=(nt,),
        in_specs=[
            rows(D_MODEL), rows(MIX_DIM), rows(D_MODEL),
            once((MIX_DIM, D_MODEL)), once((D_MODEL, D_MODEL)), once((D_MODEL, D_MODEL)),
            once((MEM_LEN, D_MODEL)), once((MEM_LEN, D_MODEL)),
            whole((1, D_MODEL)), whole((1, D_MODEL)),
        ],
        out_specs=[
            rows(MIX_DIM), cols(2 * MIX_DIM),
            pl.BlockSpec((3, tm, D_MODEL), lambda i: (0, i, 0)),
            whole((MEM_LEN, D_MODEL)), whole((MEM_LEN, D_MODEL)),
            whole((1, D_MODEL)), whole((1, D_MODEL)), whole((1, D_MODEL)),
        ],
        out_shape=[
            jax.ShapeDtypeStruct((s, MIX_DIM), BF16),
            jax.ShapeDtypeStruct((2 * MIX_DIM, s), BF16),
            jax.ShapeDtypeStruct((3, s, D_MODEL), BF16),
            jax.ShapeDtypeStruct((MEM_LEN, D_MODEL), F32),
            jax.ShapeDtypeStruct((MEM_LEN, D_MODEL), F32),
            vec, vec, vec,
        ],
        scratch_shapes=[
            pltpu.VMEM((tm, D_MODEL), F32),
            pltpu.VMEM((tm, D_MODEL), BF16),
            pltpu.VMEM((tm, D_MODEL), F32),
            pltpu.VMEM((tm, D_MODEL), BF16),
            pltpu.VMEM((8, D_MODEL), F32),
            pltpu.VMEM((8, D_MODEL), F32),
            pltpu.VMEM((8, D_MODEL), F32),
        ],
        compiler_params=pltpu.CompilerParams(dimension_semantics=("arbitrary",)),
    )(x, cat, target, w_out, w_q, w_xo, k, vv, g_x, g_f)


def _matmul_f32(name, a, b, rows=None, plans=()):
    first, m = (0, a.shape[0]) if rows is None else rows
    kk = a.shape[1]
    n = b.shape[1]
    tm, tn, tk = min(m, TM_DW), min(n, D_MODEL), min(kk, TK_DW)
    nk = kk // tk
    i0 = first // tm

    def body(a_ref, b_ref, o_ref, ob_ref):
        prod = _dot(a_ref[...], b_ref[...])
        if nk == 1:
            o_ref[...] = prod
            ob_ref[...] = prod.astype(BF16)
            return
        kid = pl.program_id(2)

        @pl.when(kid == 0)
        def _():
            o_ref[...] = prod

        @pl.when(kid > 0)
        def _():
            o_ref[...] += prod

        @pl.when(kid == nk - 1)
        def _():
            ob_ref[...] = o_ref[...].astype(BF16)

    out_spec = pl.BlockSpec((tm, tn), lambda i, j, k: (i, j))
    outs, plan_outs = _call(
        body,
        name=name,
        operands=[a, b],
        grid=(m // tm, n // tn, nk),
        in_specs=[pl.BlockSpec((tm, tk), lambda i, j, k: (i + i0, k)), pl.BlockSpec((tk, tn), lambda i, j, k: (k, j))],
        out_specs=[out_spec, out_spec],
        out_shape=[jax.ShapeDtypeStruct((m, n), F32), jax.ShapeDtypeStruct((m, n), BF16)],
        plans=plans,
    )
    return outs, plan_outs


def _dw_attn(actt, cot, plans=()):
    s = actt.shape[1]
    tm = TM_DW
    counts = [MIX_DIM // tm, D_MODEL // tm, D_MODEL // tm]
    starts = [0, counts[0], counts[0] + counts[1]]
    which = lambda i: jnp.where(i < starts[1], 0, jnp.where(i < starts[2], 1, 2))

    def body(a_ref, b_ref, *outs):
        i = pl.program_id(0)
        prod = _dot(a_ref[...], b_ref[0])
        for m in range(3):
            @pl.when(which(i) == m)
            def _():
                outs[2 * m][...] = prod
                outs[2 * m + 1][...] = prod.astype(BF16)

    out_specs, out_shape = [], []
    for m in range(3):
        spec = pl.BlockSpec((tm, D_MODEL), lambda i, m=m: (jnp.clip(i - starts[m], 0, counts[m] - 1), 0))
        out_specs += [spec, spec]
        out_shape += [jax.ShapeDtypeStruct((counts[m] * tm, D_MODEL), F32), jax.ShapeDtypeStruct((counts[m] * tm, D_MODEL), BF16)]
    outs, plan_outs = _call(
        body,
        name="dw_attn",
        operands=[actt, cot],
        grid=(sum(counts),),
        in_specs=[pl.BlockSpec((tm, s), lambda i: (i, 0)), pl.BlockSpec((1, s, D_MODEL), lambda i: (which(i), 0, 0))],
        out_specs=out_specs,
        out_shape=out_shape,
        plans=plans,
    )
    return [outs[2 * m : 2 * m + 2] for m in range(3)], plan_outs


def _dh1(name, dproj, w_in, x, dx1, g, tiles, into=None, after=(), plans=()):
    s = x.shape[0]
    tm = min(TM_DH1, s)
    t0, nt = tiles[0], tiles[1] - tiles[0]

    def body(dp_ref, w_ref, x_ref, dx1_ref, g_ref, *rest):
        gx_ref, dg_ref, acc_g = rest[-3:]
        i = pl.program_id(0)

        @pl.when(i == 0)
        def _():
            acc_g[...] = jnp.zeros_like(acc_g)

        dh1 = _dot_nt(dp_ref[...], w_ref[...])
        xh, r = _rms(x_ref[...])
        acc_g[...] += _rows8(dh1 * xh)
        gx_ref[...] = dx1_ref[0].astype(F32) + _rms_bwd(dh1 * g_ref[...], xh, r)

        @pl.when(i == nt - 1)
        def _():
            dg_ref[...] = jnp.sum(acc_g[...], axis=0, keepdims=True)

    rows = pl.BlockSpec((tm, D_MODEL), lambda i: (i + t0, 0))
    return _call(
        body,
        name=name,
        operands=[dproj, w_in, x, dx1, g] + ([into] if into is not None else []) + list(after),
        grid=(nt,),
        in_specs=[
            pl.BlockSpec((tm, IN_DIM), lambda i: (i + t0, 0)),
            pl.BlockSpec((D_MODEL, IN_DIM), lambda i: (0, 0), pipeline_mode=pl.Buffered(1)),
            rows,
            pl.BlockSpec((1, tm, D_MODEL), lambda i: (0, i + t0, 0)),
            pl.BlockSpec((1, D_MODEL), lambda i: (0, 0)),
        ] + [pl.BlockSpec(memory_space=pl.ANY)] * ((into is not None) + len(after)),
        out_specs=[rows, pl.BlockSpec((1, D_MODEL), lambda i: (0, 0))],
        out_shape=[jax.ShapeDtypeStruct((s, D_MODEL), F32), jax.ShapeDtypeStruct((1, D_MODEL), F32)],
        scratch_shapes=[pltpu.VMEM((8, D_MODEL), F32)],
        plans=plans,
        aliases={5: 0} if into is not None else None,
    )


def _pair_sum(name, grad, got, axis, size, core, plans=()):
    shard = list(grad.shape)
    shard[axis] = size
    r, cdim = shard
    tr = min(r, TR_PAIR_SUM)
    nr = r // tr

    def body(core_ref, g_ref, got_ref, o_ref):
        o_ref[0] = (g_ref[...] + got_ref[0].astype(F32)).astype(BF16)

    if axis == 1:
        g_map = lambda q, t, core_ref: (t, 2 * q + core_ref[0])
    else:
        g_map = lambda q, t, core_ref: ((2 * q + core_ref[0]) * nr + t, 0)
    (out,), plan_outs = _call(
        body,
        name=name,
        prefetch=[core],
        operands=[grad, got],
        grid=(N_CHIP, nr),
        in_specs=[
            pl.BlockSpec((tr, cdim), g_map),
            pl.BlockSpec((1, tr, cdim), lambda q, t, core_ref: (q, t, 0)),
        ],
        out_specs=[pl.BlockSpec((1, tr, cdim), lambda q, t, core_ref: (q, t, 0))],
        out_shape=[jax.ShapeDtypeStruct((N_CHIP, r, cdim), BF16)],
        plans=plans,
    )
    return (out, plan_outs) if plans else out


def _adam_step(w, m, v, g):
    c1 = 1.0 / (1.0 - ADAM_B1**ADAM_STEP)
    c2 = 1.0 / (1.0 - ADAM_B2**ADAM_STEP)
    m_new = ADAM_B1 * m + (1.0 - ADAM_B1) * g
    v_new = ADAM_B2 * v + (1.0 - ADAM_B2) * (g * g)
    return -ADAM_LR * ((m_new * c1) / (jnp.sqrt(v_new * c2) + ADAM_EPS) + ADAM_WD * w), m_new, v_new


def _update_small(dev, vec_w, vec_m, vec_v, conv_w, conv_m, conv_v, small, landed):
    n_vec = len(vec_w)

    def body(dev_ref, *refs):
        refs = list(refs)
        take = lambda k: [refs.pop(0) for _ in range(k)]
        w_refs, m_refs, v_refs = take(n_vec), take(n_vec), take(n_vec)
        cw_ref, cm_ref, cv_ref, small_ref, land_ref, small_conv_ref, land_conv_ref = take(7)
        vec_outs, conv_outs, (loss_ref,) = take(4 * n_vec), take(4), take(1)

        def total(own_ref, others_ref):
            acc = None
            for d in range(N_DEV):
                part = jnp.where(dev_ref[0] == d, own_ref[0], others_ref[d])
                acc = part if acc is None else acc + part
            return acc

        g_all = total(small_ref, land_ref)
        for i in range(n_vec):
            g = g_all[i : i + 1]
            delta, m_new, v_new = _adam_step(w_refs[i][...], m_refs[i][...], v_refs[i][...], g)
            for ref, val in zip(vec_outs[4 * i : 4 * i + 4], (g, delta, m_new, v_new)):
                ref[...] = val
        g = total(small_conv_ref, land_conv_ref)[0:3]
        delta, m_new, v_new = _adam_step(cw_ref[0], cm_ref[0], cv_ref[0], g)
        for ref, val in zip(conv_outs, (g, delta, m_new, v_new)):
            ref[0] = val
        loss_ref[...] = jnp.broadcast_to(0.5 * jnp.sum(g_all[7:8], axis=-1, keepdims=True) / D_MODEL, loss_ref.shape)

    vec_spec = pl.BlockSpec((1, D_MODEL), lambda i, dev_ref: (0, 0))
    conv_spec = pl.BlockSpec((1, 3, CHUNK), lambda i, dev_ref: (0, 0, 0))
    vec_shape = jax.ShapeDtypeStruct((1, D_MODEL), F32)
    conv_shape = jax.ShapeDtypeStruct((1, 3, CHUNK), F32)
    outs, _ = _call(
        body,
        name="update_small",
        prefetch=[dev],
        operands=list(vec_w) + list(vec_m) + list(vec_v) + [conv_w, conv_m, conv_v, small, landed, small, landed],
        grid=(1,),
        in_specs=[vec_spec] * (3 * n_vec) + [conv_spec] * 3 + [
            pl.BlockSpec((1, 16, D_MODEL), lambda i, dev_ref: (0, 0, 0)),
            pl.BlockSpec((N_DEV, 16, D_MODEL), lambda i, dev_ref: (0, 0, 0)),
            pl.BlockSpec((1, 8, CHUNK), lambda i, dev_ref: (0, 1, dev_ref[0])),
            pl.BlockSpec((N_DEV, 8, CHUNK), lambda i, dev_ref: (0, 1, dev_ref[0])),
        ],
        out_specs=[vec_spec] * (4 * n_vec) + [conv_spec] * 4 + [pl.BlockSpec((1, CHUNK), lambda i, dev_ref: (0, 0))],
        out_shape=[vec_shape] * (4 * n_vec) + [conv_shape] * 4 + [jax.ShapeDtypeStruct((1, CHUNK), F32)],
    )
    return [outs[4 * i : 4 * i + 4] for i in range(n_vec)], outs[4 * n_vec : 4 * n_vec + 4], outs[-1]


def _adamw(name, groups, steps, own_slot=None, plans=()):
    lists = lambda p: list(p) if isinstance(p, (list, tuple)) else [p]
    groups = [(g[0], g[1], g[2], lists(g[3]), lists(g[4]) if len(g) > 4 else []) for g in groups]
    counts = [3 + len(g[3]) + len(g[4]) for g in groups]
    use_slot = own_slot is not None

    def first_tiles(w, parts):
        firsts, first = [], 0
        for p in parts:
            firsts.append(first)
            first += p.shape[1] // (w.shape[0] // steps)
        return firsts

    def update(t, slot_ref, firsts, w_ref, m_ref, v_ref, p_refs, o_refs, g_ref, d_ref, nm_ref, nv_ref):
        g = None
        for k, p_ref in enumerate(p_refs):
            gk = None
            for q in range(p_ref.shape[0]):
                part = p_ref[q].astype(F32)
                if o_refs:
                    part = jnp.where(slot_ref[0] == q, o_refs[k][0].astype(F32), part)
                gk = part if gk is None else gk + part
            g = gk if g is None else jnp.where(t >= firsts[k], gk, g)
        g_ref[...] = g
        d_ref[...], nm_ref[...], nv_ref[...] = _adam_step(w_ref[...], m_ref[...], v_ref[...], g)

    def body(*refs):
        refs = list(refs)
        slot_ref = refs.pop(0) if use_slot else None
        t = pl.program_id(0)
        ins, outs = refs[: sum(counts)], refs[sum(counts) :]
        for i, (w, _, _, parts, own) in enumerate(groups):
            mine = ins[sum(counts[:i]) : sum(counts[: i + 1])]
            update(t, slot_ref, first_tiles(w, parts), *mine[:3], mine[3 : 3 + len(parts)], mine[3 + len(parts) :],
                   *outs[4 * i : 4 * i + 4])

    operands, in_specs, out_specs, out_shape = [], [], [], []
    for w, m, v, parts, own in groups:
        r, cdim = w.shape
        tr = r // steps
        firsts = first_tiles(w, parts)

        def tile_of(k, t, firsts=firsts, parts=parts, tr=tr):
            return jnp.clip(t - firsts[k], 0, parts[k].shape[1] // tr - 1)

        blk = pl.BlockSpec((tr, cdim), lambda t, *_: (t, 0))
        operands += [w, m, v] + parts + own
        in_specs += [blk, blk, blk]
        in_specs += [pl.BlockSpec((p.shape[0], tr, cdim), lambda t, *_, k=k, tile_of=tile_of: (0, tile_of(k, t), 0)) for k, p in enumerate(parts)]
        in_specs += [pl.BlockSpec((1, tr, cdim), lambda t, slot_ref, k=k, tile_of=tile_of: (slot_ref[0], tile_of(k, t), 0)) for k in range(len(own))]
        out_specs += [blk] * 4
        out_shape += [jax.ShapeDtypeStruct((r, cdim), F32)] * 4
    outs, plan_outs = _call(
        body,
        name=name,
        prefetch=[own_slot] if use_slot else [],
        operands=operands,
        grid=(steps,),
        in_specs=in_specs,
        out_specs=out_specs,
        out_shape=out_shape,
        plans=plans,
    )
    return [outs[4 * i : 4 * i + 4] for i in range(len(groups))], plan_outs


def _chip_routes(n):
    x, y, c = _place()
    my_chip = 2 * x + y
    return [(a, (*chip, c), 2 * chip[0] + chip[1], my_chip, 2 * chip[0] + chip[1])
            for a in range(n) for chip in _other_chips(x, y)]


def _gather_routes(n):
    x, y, c = _place()
    flip = lambda v, f: 1 - v if f else v
    peers = [(flip(x, fx), flip(y, fy), flip(c, fc)) for fx in (0, 1) for fy in (0, 1) for fc in (0, 1) if fx + fy + fc]
    return [(0, p, 0, 4 * x + 2 * y + c, 4 * p[0] + 2 * p[1] + p[2]) for p in peers]


_chip_routes.copies_per_array = N_CHIP - 1
_gather_routes.copies_per_array = N_DEV - 1


def _split_start(name, routes, srcs, land_shapes):
    n = len(srcs)
    hbm = pl.BlockSpec(memory_space=pltpu.HBM)
    sem = pl.BlockSpec(memory_space=pltpu.SEMAPHORE)

    def body(*refs):
        src_refs, land_refs = refs[:n], refs[n : 2 * n]
        send_sems, recv_sems = refs[2 * n], refs[2 * n + 1]
        token = refs[-1]
        for k, (a, peer, src_slot, there, _) in enumerate(routes(n)):
            pltpu.make_async_remote_copy(
                src_ref=src_refs[a].at[src_slot], dst_ref=land_refs[a].at[there],
                send_sem=send_sems.at[k], recv_sem=recv_sems.at[k], device_id=peer, device_id_type=MESH).start()
        token[...] = jnp.zeros_like(token)

    in_hbm = [pltpu.with_memory_space_constraint(p, pltpu.HBM) for p in srcs]
    in_hbm += [pltpu.with_memory_space_constraint(lax.empty(shape, p.dtype), pltpu.HBM) for p, shape in zip(srcs, land_shapes)]
    n_copies = routes.copies_per_array * n
    res = pl.pallas_call(
        body,
        name=name,
        out_shape=[pltpu.SemaphoreType.DMA((n_copies,)), pltpu.SemaphoreType.DMA((n_copies,))]
        + [pltpu.HBM(p.shape, p.dtype) for p in srcs] + [pltpu.HBM(shape, p.dtype) for p, shape in zip(srcs, land_shapes)]
        + [jax.ShapeDtypeStruct((8, CHUNK), F32)],
        in_specs=[hbm] * (2 * n),
        out_specs=[sem, sem] + [hbm] * (2 * n) + [pl.BlockSpec(memory_space=pltpu.VMEM)],
        input_output_aliases={i: 2 + i for i in range(2 * n)},
        compiler_params=pltpu.CompilerParams(has_side_effects=pltpu.SideEffectType.DATAFLOW_SIDE_EFFECTING),
    )(*in_hbm)
    return res[:-1], res[-1]


def _split_wait(name, routes, in_flight, after):
    n = (len(in_flight) - 2) // 2
    hbm = pl.BlockSpec(memory_space=pltpu.HBM)
    sem = pl.BlockSpec(memory_space=pltpu.SEMAPHORE)
    send_sems, recv_sems, *bufs = in_flight

    def body(*refs):
        src_refs, land_refs = refs[:n], refs[n : 2 * n]
        send_ref, recv_ref = refs[2 * n], refs[2 * n + 1]
        for k, (a, peer, src_slot, _, here) in enumerate(routes(n)):
            cp = pltpu.make_async_remote_copy(
                src_ref=src_refs[a].at[src_slot], dst_ref=land_refs[a].at[here],
                send_sem=send_ref.at[k], recv_sem=recv_ref.at[k], device_id=peer, device_id_type=MESH)
            cp.wait_send()
            cp.wait_recv()

    res = pl.pallas_call(
        body,
        name=name,
        out_shape=[pltpu.HBM(b.shape, b.dtype) for b in bufs],
        in_specs=[hbm] * (2 * n) + [sem, sem] + [pl.BlockSpec(memory_space=pl.ANY)] * len(after),
        out_specs=[hbm] * (2 * n),
        input_output_aliases={i: i for i in range(2 * n)},
        compiler_params=pltpu.CompilerParams(has_side_effects=pltpu.SideEffectType.DATAFLOW_SIDE_EFFECTING),
    )(*bufs, send_sems, recv_sems, *after)
    return res[:n], res[n:]


def kernel(x, mem, norm_mix_g, w_in, conv_w, gm_ln_g, gm_ln_b, gm_ws, gm_bs, w_out, norm_x_g, norm_mem_g, w_q, w_kv, w_xo, norm_final_g, loss_target, m_norm_mix_g, m_w_in, m_conv_w, m_gm_ln_g, m_gm_ln_b, m_gm_ws, m_gm_bs, m_w_out, m_norm_x_g, m_norm_mem_g, m_w_q, m_w_kv, m_w_xo, m_norm_final_g, v_norm_mix_g, v_w_in, v_conv_w, v_gm_ln_g, v_gm_ln_b, v_gm_ws, v_gm_bs, v_w_out, v_norm_x_g, v_norm_mem_g, v_w_q, v_w_kv, v_w_xo, v_norm_final_g):
    s = x.shape[1]
    dev = 4 * lax.axis_index("x") + 2 * lax.axis_index("y") + lax.axis_index("c")
    core = lax.axis_index("c").astype(jnp.int32).reshape(1)
    x2 = x[0]
    target = loss_target[0]
    pad_taps = lambda t: jnp.pad(t[0], ((0, 5), (0, 0)))

    big_names = ["w_in", "w_out", "w_q", "w_kv", "w_xo"]
    big_w = dict(w_in=w_in[0], w_out=w_out[0], w_q=w_q[0], w_kv=w_kv[0], w_xo=w_xo[0])
    big_m = dict(w_in=m_w_in[0], w_out=m_w_out[0], w_q=m_w_q[0], w_kv=m_w_kv[0], w_xo=m_w_xo[0])
    big_v = dict(w_in=v_w_in[0], w_out=v_w_out[0], w_q=v_w_q[0], w_kv=v_w_kv[0], w_xo=v_w_xo[0])
    shard_axis = dict(w_in=1, w_out=0, w_q=0, w_kv=1, w_xo=0)
    full_shape = dict(w_in=(D_MODEL, IN_DIM), w_out=(MIX_DIM, D_MODEL), w_q=(D_MODEL, D_MODEL),
                      w_kv=(D_MODEL, 2 * D_MODEL), w_xo=(D_MODEL, D_MODEL))
    shard_size = {k: big_w[k].shape[shard_axis[k]] for k in big_names}
    others = big_names[1:]
    gather_spec = lambda k: (full_shape[k], shard_axis[k], shard_size[k])
    reduce_spec = lambda k: (shard_axis[k], shard_size[k])
    ws = gm_ws[0]
    bs_b = jnp.broadcast_to(gm_bs[0][:, :, None], (8, CHUNK, CHUNK))
    g_f = norm_final_g.reshape(1, D_MODEL)

    px, py = lax.axis_index("x"), lax.axis_index("y")
    chip_order = jnp.stack([2 * px + py, 2 * (1 - px) + py, 2 * px + 1 - py, 2 * (1 - px) + 1 - py]).astype(jnp.int32)
    (proj, h1, h1t, w_in_full), ((w_out_full, conv_g),) = _inproj(
        x2, norm_mix_g, big_w["w_in"].astype(BF16), chip_order,
        plans=[_gather_plan([big_w["w_out"].astype(BF16), pad_taps(conv_w)[None]], [gather_spec("w_out"), ((N_DEV, 8, CHUNK), 0, 1)])])
    conv_full = conv_g.transpose(1, 0, 2).reshape(8, D_MODEL)
    attn_w = ["w_q", "w_kv", "w_xo"]
    cat, (gathered,) = _mixer_fwd(
        proj, conv_full, gm_ln_g, gm_ln_b, ws, bs_b,
        plans=[_gather_plan([big_w[k].astype(BF16) for k in attn_w], [gather_spec(k) for k in attn_w])])
    wf = dict(zip(attn_w, gathered), w_out=w_out_full)
    k, vv, mt = _kv_fwd(mem[0], norm_mem_g, wf["w_kv"])
    (dcat, actt, cot, dk, dvv, d_gf, d_gx, sq_err) = _attn(
        x2, cat, target, wf["w_out"], wf["w_q"], wf["w_xo"], k, vv, norm_x_g, g_f)

    def pair_sums(names, grads, got):
        return [_pair_sum("pair_sum_" + k, grads[k], got[i], *reduce_spec(k), core) for i, k in enumerate(names)]

    sibling_plan = lambda names, grads: _sibling_plan([grads[k] for k in names], [reduce_spec(k) for k in names])
    early = ["w_kv", "w_out", "w_q", "w_xo"]
    big_g, big_gb = {}, {}
    big_g["w_kv"], big_gb["w_kv"], d_gmem = _kv_bwd(mem[0], wf["w_kv"], mt, dk, dvv)
    dws, (got_kv,) = _dw_attn(actt, cot, plans=[sibling_plan(early[:1], big_gb)])
    for name, (f32, bf16) in zip(early[1:], dws):
        big_g[name], big_gb[name] = f32, bf16
    pair_kv, (got_rest,) = _pair_sum("pair_sum_w_kv", big_g["w_kv"], got_kv[0], *reduce_spec("w_kv"), core,
                                     plans=[sibling_plan(early[1:], big_gb)])
    (dproj, d_cw, d_lng, d_lnb, d_ws, d_bs), (parts_early,) = _mixer_bwd(
        proj, dcat, conv_full, gm_ln_g, gm_ln_b, ws, bs_b,
        plans=[_chip_plan([pair_kv] + pair_sums(early[1:], big_g, got_rest))])
    half = D_MODEL // 2
    in_spec = reduce_spec("w_in")
    (dw_top, dwb_top), _ = _matmul_f32("dw_in_top", h1t, dproj, rows=(0, half))
    (dw_bot, dwb_bot), (got_top, (ws_all,)) = _matmul_f32(
        "dw_in_bot", h1t, dproj, rows=(half, half),
        plans=[_sibling_plan([dwb_top], [in_spec]),
               _gather_plan([d_ws.reshape(1, 8 * CHUNK, CHUNK).astype(BF16)], [((N_DEV, 8 * CHUNK, CHUNK), 0, 1)])])
    pair_top, ((got_bot,),) = _pair_sum("pair_sum_w_in_top", dw_top, got_top[0], *in_spec, core,
                                        plans=[_sibling_plan([dwb_bot], [in_spec])])
    pair_bot = _pair_sum("pair_sum_w_in_bot", dw_bot, got_bot, *in_spec, core)
    chip_shapes = [p.shape for p in (pair_top, pair_bot)]
    chips_in_flight, token = _split_start("grad_chip_exchange_start", _chip_routes, [pair_top, pair_bot], chip_shapes)
    n_tiles = s // min(TM_DH1, s)
    (grad_x, d_gmix), _ = _dh1("dh1", dproj, w_in_full, x2, cot, norm_mix_g, (0, n_tiles), after=[token])
    vec_names = ["norm_mix_g", "gm_ln_g", "gm_ln_b", "gm_bs", "norm_x_g", "norm_mem_g", "norm_final_g"]
    vec_g = [d_gmix, d_lng, d_lnb, d_bs, d_gx, d_gmem, d_gf]
    small = jnp.concatenate(vec_g + [sq_err, d_cw], axis=0)[None]
    small_in_flight, _ = _split_start("small_all_gather_start", _gather_routes, [small], [(N_DEV, 16, D_MODEL)])
    flat_ws = lambda t: t.reshape(8 * CHUNK, CHUNK)
    updated, _ = _adamw(
        "adamw_early",
        [(big_w[k], big_m[k], big_v[k], got_k) for k, got_k in zip(early, parts_early)]
        + [(flat_ws(ws), flat_ws(m_gm_ws), flat_ws(v_gm_ws), ws_all)], steps=STEPS_ADAMW // 2)
    big_out = dict(zip(early, updated[:4]))
    ws_out = updated[4]
    (pair_top, pair_bot), (land_top, land_bot) = _split_wait(
        "grad_chip_exchange_wait", _chip_routes, chips_in_flight, [d_gmix] + [u[0] for u in updated])
    my_chip = (2 * px + py).astype(jnp.int32).reshape(1)
    (big_out["w_in"],), _ = _adamw(
        "adamw_w_in", [(big_w["w_in"], big_m["w_in"], big_v["w_in"], [land_top, land_bot], [pair_top, pair_bot])],
        steps=STEPS_ADAMW, own_slot=my_chip)
    (small,), (small_all,) = _split_wait("small_all_gather_wait", _gather_routes, small_in_flight, [big_out["w_in"][0]])
    row = lambda t: t.reshape(1, D_MODEL)
    vec_out, conv_out, loss_row = _update_small(
        dev.astype(jnp.int32).reshape(1),
        [norm_mix_g, gm_ln_g, gm_ln_b, row(gm_bs), norm_x_g, norm_mem_g, row(norm_final_g)],
        [m_norm_mix_g, m_gm_ln_g, m_gm_ln_b, row(m_gm_bs), m_norm_x_g, m_norm_mem_g, row(m_norm_final_g)],
        [v_norm_mix_g, v_gm_ln_g, v_gm_ln_b, row(v_gm_bs), v_norm_x_g, v_norm_mem_g, row(v_norm_final_g)],
        conv_w, m_conv_w, v_conv_w, small, small_all)
    loss = loss_row[0, 0]

    def result(name, which):
        if name in big_out:
            return big_out[name][which][None]
        if name == "conv_w":
            return conv_out[which]
        if name == "gm_ws":
            return ws_out[which].reshape(1, 8, CHUNK, CHUNK)
        r = vec_out[vec_names.index(name)][which]
        if name == "gm_bs":
            return r.reshape(1, 8, CHUNK)
        if name == "norm_final_g":
            return r.reshape(D_MODEL)
        return r

    weights = ["norm_mix_g", "w_in", "conv_w", "gm_ln_g", "gm_ln_b", "gm_ws", "gm_bs", "w_out", "norm_x_g", "norm_mem_g", "w_q", "w_kv", "w_xo", "norm_final_g"]
    outs = [loss, grad_x[None]]
    for which in range(4):
        outs += [result(name, which) for name in weights]
    return tuple(outs)
```

```python
import math

import jax
import jax.numpy as jnp
from jax import lax
from jax.experimental import pallas as pl
from jax.experimental.pallas import tpu as pltpu

F32 = jnp.float32
BF16 = jnp.bfloat16
MESH = pl.DeviceIdType.MESH

D_MODEL = 1024
N_DEV = 8
N_CHIP = 4
CHUNK = 128
GROUPS = 7
IN_DIM = GROUPS * D_MODEL
SHARD_IN = IN_DIM // N_DEV
MIX_DIM = 2 * D_MODEL
X_HEADS = 4
X_HEAD_DIM = D_MODEL // X_HEADS
MEM_LEN = 256
EPS = 1e-6
GELU_K0 = math.sqrt(2.0 / math.pi)
GELU_K1 = 0.044715

ADAM_LR = 0.001
ADAM_B1 = 0.9
ADAM_B2 = 0.999
ADAM_EPS = 1e-08
ADAM_WD = 0.01
ADAM_STEP = 10

TM_INPROJ = 512
TM_MIXER_BWD = 256
TM_ATTN = 512
SUB_ATTN = 256
TM_DH1 = 512
TM_DW = 512
TK_DW = 4096
TR_PAIR_SUM = 1024
BARRIER_ID = {"sibling": 1, "chips": 2, "all": 3}
STEPS_ADAMW = 4


def _dot(a, b):
    return lax.dot_general(a, b, (((1,), (0,)), ((), ())), preferred_element_type=F32)


def _dot_nt(a, b):
    return lax.dot_general(a, b, (((1,), (1,)), ((), ())), preferred_element_type=F32)


def _rows8(v):
    return jnp.sum(v.reshape(v.shape[0] // 8, 8, v.shape[1]), axis=0)


def _gelu(x, with_grad=True):
    x2 = x * x
    t = jnp.tanh(x * (GELU_K0 + (GELU_K0 * GELU_K1) * x2))
    half = 0.5 + 0.5 * t
    g = x * half
    if not with_grad:
        return g
    return g, half + ((0.5 * x) * (1.0 - t * t)) * (GELU_K0 + (3.0 * GELU_K0 * GELU_K1) * x2)


def _silu(z, with_grad=True):
    s = jax.nn.sigmoid(z)
    zs = z * s
    return (zs, s + zs * (1.0 - s)) if with_grad else zs


def _rms(v):
    r = lax.rsqrt(jnp.mean(v * v, axis=-1, keepdims=True) + EPS)
    return v * r, r


def _rms_bwd(dy_g, vh, r):
    return r * (dy_g - vh * jnp.mean(dy_g * vh, axis=-1, keepdims=True))


def _place():
    return lax.axis_index("x"), lax.axis_index("y"), lax.axis_index("c")


def _other_chips(x, y):
    return [(1 - x, y), (x, 1 - y), (1 - x, 1 - y)]


def _pow2_divisor(n):
    return n & (-n)


def _shard_view(ref, axis, size, d):
    start = d * size
    align = _pow2_divisor(size)
    if align > 1:
        start = pl.multiple_of(start, align)
    idx = [slice(None)] * len(ref.shape)
    idx[axis] = pl.ds(start, size)
    return ref.at[tuple(idx)]


class _Plan:
    def __init__(self, ins, out_shape, sems, start, finish, mid=None, aliases=None, peers="all"):
        self.ins, self.out_shape, self.sems = list(ins), list(out_shape), list(sems)
        self.start, self.mid, self.finish = start, mid, finish
        self.aliases = dict(aliases or {})
        self.peers = peers


def _gather_plan(shards, specs):
    n = len(shards)
    n_copy = 7

    def copies(ins, outs, sems):
        send_sems, recv_sems, local_sems = sems
        x, y, c = _place()
        me, sibling = (x, y, c), (x, y, 1 - c)
        chips = _other_chips(x, y)

        def block(a, p):
            _, axis, size = specs[a]
            return _shard_view(outs[a], axis, size, 4 * p[0] + 2 * p[1] + p[2])

        def copy(a, k, p, to, src=None):
            return pltpu.make_async_remote_copy(
                src_ref=block(a, p) if src is None else src,
                dst_ref=block(a, p),
                send_sem=send_sems.at[a * n_copy + k],
                recv_sem=recv_sems.at[a * n_copy + k],
                device_id=to,
                device_id_type=MESH,
            )

        def mine():
            return [pltpu.make_async_copy(ins[a], block(a, me), local_sems.at[a]) for a in range(n)]

        def first():
            return [cp for a in range(n) for cp in
                    [copy(a, 0, me, sibling, src=ins[a])] + [copy(a, 1 + j, me, (*chip, c), src=ins[a]) for j, chip in enumerate(chips)]]

        def landed():
            return [copy(a, 1 + j, (*chip, c), me) for j, chip in enumerate(chips) for a in range(n)]

        def passed():
            return [copy(a, 4 + j, (*chip, c), sibling) for j, chip in enumerate(chips) for a in range(n)]

        def last():
            return [cp for a in range(n) for cp in
                    [copy(a, 0, sibling, me)] + [copy(a, 4 + j, (*chip, 1 - c), me) for j, chip in enumerate(chips)]]

        return mine, first, landed, passed, last

    def start(ins, outs, sems):
        mine, first, _, _, _ = copies(ins, outs, sems)
        for cp in mine() + first():
            cp.start()

    def mid(ins, outs, sems):
        _, _, landed, passed, _ = copies(ins, outs, sems)
        for got, fwd in zip(landed(), passed()):
            got.wait_recv()
            fwd.start()

    def finish(ins, outs, sems):
        mine, first, _, passed, last = copies(ins, outs, sems)
        for cp in last():
            cp.wait_recv()
        for cp in first() + passed():
            cp.wait_send()
        for cp in mine():
            cp.wait()

    return _Plan(
        shards,
        [jax.ShapeDtypeStruct(full, s.dtype) for s, (full, _, _) in zip(shards, specs)],
        [pltpu.SemaphoreType.DMA((n * n_copy,)), pltpu.SemaphoreType.DMA((n * n_copy,)), pltpu.SemaphoreType.DMA((n,))],
        start, finish, mid,
    )


def _sibling_plan(grads, specs):
    n = len(grads)

    def shard_shape(a):
        axis, size = specs[a]
        shp = list(grads[a].shape)
        shp[axis] = size
        return tuple(shp)

    def copies(ins, outs, sems):
        send_sems, recv_sems = sems
        x, y, c = _place()
        return [
            pltpu.make_async_remote_copy(
                src_ref=_shard_view(ins[a], specs[a][0], specs[a][1], 2 * q + (1 - c)),
                dst_ref=outs[a].at[q],
                send_sem=send_sems.at[a * N_CHIP + q],
                recv_sem=recv_sems.at[a * N_CHIP + q],
                device_id=(x, y, 1 - c),
                device_id_type=MESH,
            )
            for a in range(n) for q in range(N_CHIP)
        ]

    def start(ins, outs, sems):
        for cp in copies(ins, outs, sems):
            cp.start()

    def finish(ins, outs, sems):
        for cp in copies(ins, outs, sems):
            cp.wait()

    return _Plan(
        grads,
        [jax.ShapeDtypeStruct((N_CHIP, *shard_shape(a)), grads[a].dtype) for a in range(n)],
        [pltpu.SemaphoreType.DMA((n * N_CHIP,)), pltpu.SemaphoreType.DMA((n * N_CHIP,))],
        start, finish, peers="sibling",
    )


def _chip_plan(parts, full_rows=None, row0=None, into=None):
    n = len(parts)
    full_rows = [p.shape[1] for p in parts] if full_rows is None else full_rows
    row0 = [0] * n if row0 is None else row0

    def copies(ins, outs, sems):
        send_sems, recv_sems, local_sems = sems
        x, y, c = _place()
        my_chip = 2 * x + y
        chips = _other_chips(x, y)

        def land(a, q):
            return outs[a].at[q, pl.ds(row0[a], parts[a].shape[1]), :]

        def mine():
            return [pltpu.make_async_copy(ins[a].at[my_chip], land(a, my_chip), local_sems.at[a]) for a in range(n)]

        def remote(sending):
            return [
                pltpu.make_async_remote_copy(
                    src_ref=ins[a].at[2 * chip[0] + chip[1]],
                    dst_ref=land(a, my_chip if sending else 2 * chip[0] + chip[1]),
                    send_sem=send_sems.at[a * 3 + j], recv_sem=recv_sems.at[a * 3 + j],
                    device_id=(*chip, c), device_id_type=MESH)
                for a in range(n) for j, chip in enumerate(chips)
            ]

        return mine, remote

    def start(ins, outs, sems):
        mine, remote = copies(ins, outs, sems)
        for cp in mine() + remote(True):
            cp.start()

    def finish(ins, outs, sems):
        mine, remote = copies(ins, outs, sems)
        for cp in remote(False):
            cp.wait_recv()
        for cp in remote(True):
            cp.wait_send()
        for cp in mine():
            cp.wait()

    return _Plan(
        list(parts) + (list(into) if into is not None else []),
        [jax.ShapeDtypeStruct((N_CHIP, full_rows[a], parts[a].shape[2]), parts[a].dtype) for a in range(n)],
        [pltpu.SemaphoreType.DMA((n * 3,)), pltpu.SemaphoreType.DMA((n * 3,)), pltpu.SemaphoreType.DMA((n,))],
        start, finish,
        aliases={n + a: a for a in range(n)} if into is not None else None, peers="chips",
    )


def _call(body, *, name, operands, out_shape, grid=(), in_specs=None, out_specs=None, scratch_shapes=(),
          plans=(), aliases=None, prefetch=(), start_after_body=False, pass_on_at=0.6, body_peers=None):
    operands, out_shape, scratch_shapes = list(operands), list(out_shape), list(scratch_shapes)
    n_pf, n_in, n_out, n_scr = len(prefetch), len(operands), len(out_shape), len(scratch_shapes)
    in_vmem = pl.BlockSpec(memory_space=pltpu.VMEM)
    in_hbm = pl.BlockSpec(memory_space=pl.ANY)
    in_specs = [in_vmem] * n_in if in_specs is None else list(in_specs)
    out_specs = [in_vmem] * n_out if out_specs is None else list(out_specs)
    io_alias = {n_pf + i: o for i, o in (aliases or {}).items()}
    pos_in, pos_out = n_pf + n_in, n_out
    for p in plans:
        for i, o in p.aliases.items():
            io_alias[pos_in + i] = pos_out + o
        pos_in += len(p.ins)
        pos_out += len(p.out_shape)
    steps = math.prod(grid) if grid else 1
    mid_step = min(steps - 1, int(steps * pass_on_at))
    patterns = {p.peers for p in plans} | ({body_peers} if body_peers else set())
    pattern = None if not patterns else patterns.pop() if len(patterns) == 1 else "all"

    def shake_hands():
        x, y, c = _place()
        peers = [(x, y, 1 - c)] if pattern in ("sibling", "all") else []
        peers += [(*chip, c) for chip in _other_chips(x, y)] if pattern in ("chips", "all") else []
        barrier = pltpu.get_barrier_semaphore()
        for peer in peers:
            pl.semaphore_signal(barrier, inc=1, device_id=peer, device_id_type=MESH)
        pl.semaphore_wait(barrier, len(peers))

    def wrapped(*refs):
        refs = list(refs)
        take = lambda k: [refs.pop(0) for _ in range(k)]
        pf = take(n_pf)
        ins, p_ins = take(n_in), [take(len(p.ins)) for p in plans]
        outs, p_outs = take(n_out), [take(len(p.out_shape)) for p in plans]
        scr, p_sems = take(n_scr), [take(len(p.sems)) for p in plans]
        step = 0
        for ax, g in enumerate(grid):
            step = step * g + pl.program_id(ax)

        def hook(kind, at):
            todo = [(getattr(p, kind), a, b, c) for p, a, b, c in zip(plans, p_ins, p_outs, p_sems) if getattr(p, kind)]

            def run():
                for fn, a, b, c in todo:
                    fn(a, b, c)

            if todo and grid:
                pl.when(step == at)(run)
            elif todo:
                run()

        if pattern and grid:
            pl.when(step == 0)(shake_hands)
        elif pattern:
            shake_hands()
        if not start_after_body:
            hook("start", 0)
        if body is not None:
            body(*pf, *ins, *outs, *scr)
        if start_after_body:
            hook("start", 0)
        hook("mid", mid_step)
        hook("finish", steps - 1)

    layout = dict(
        grid=grid,
        in_specs=in_specs + [in_hbm] * sum(len(p.ins) for p in plans),
        out_specs=out_specs + [in_hbm] * sum(len(p.out_shape) for p in plans),
        scratch_shapes=scratch_shapes + [s for p in plans for s in p.sems],
    )
    if n_pf:
        layout = dict(grid_spec=pltpu.PrefetchScalarGridSpec(num_scalar_prefetch=n_pf, **layout))
    res = pl.pallas_call(
        wrapped,
        name=name,
        out_shape=out_shape + [o for p in plans for o in p.out_shape],
        input_output_aliases=io_alias,
        compiler_params=pltpu.CompilerParams(
            dimension_semantics=("arbitrary",) * len(grid) if grid else None,
            collective_id=BARRIER_ID[pattern] if pattern else None),
        **layout,
    )(*prefetch, *operands, *[a for p in plans for a in p.ins])
    res = list(res)
    outs, plan_outs = res[:n_out], []
    pos = n_out
    for p in plans:
        plan_outs.append(res[pos : pos + len(p.out_shape)])
        pos += len(p.out_shape)
    return outs, plan_outs


def _inproj(x, g, w_shard, order, plans=()):
    s = x.shape[0]
    tm = min(TM_INPROJ, s)
    nt = s // tm
    wide = 2 * SHARD_IN
    fwd_at = nt // 2

    def body(order_ref, x_ref, g_ref, w_hbm, proj_ref, h1_ref, h1t_ref, wfull_hbm,
             w_scr, h1_scr, send_sems, recv_sems, local_sems):
        k, i = pl.program_id(0), pl.program_id(1)
        px, py, c = _place()
        me, sibling = (px, py, c), (px, py, 1 - c)
        chips = _other_chips(px, py)

        def half(slot, core):
            return w_scr.at[slot, :, pl.ds(pl.multiple_of(core * SHARD_IN, 128), SHARD_IN)]

        def copy(n, slot, core, to, src=None):
            return pltpu.make_async_remote_copy(
                src_ref=half(slot, core) if src is None else src, dst_ref=half(slot, core),
                send_sem=send_sems.at[n], recv_sem=recv_sems.at[n], device_id=to, device_id_type=MESH)

        def mine():
            return pltpu.make_async_copy(w_hbm, half(0, c), local_sems.at[0])

        def first():
            return [copy(0, 0, c, sibling, src=w_hbm)] + [copy(1 + j, 1 + j, c, (*chips[j], c), src=w_hbm) for j in range(2)]

        def relay():
            to = (c * px + (1 - c) * (1 - px), c * (1 - py) + (1 - c) * py, c)
            return pltpu.make_async_remote_copy(
                src_ref=half(2 - c, c), dst_ref=half(3, c), send_sem=send_sems.at[3], recv_sem=recv_sems.at[3],
                device_id=to, device_id_type=MESH)

        def save(slot):
            cols = pl.ds(pl.multiple_of(order_ref[slot] * wide, 128), wide)
            return pltpu.make_async_copy(w_scr.at[slot], wfull_hbm.at[:, cols], local_sems.at[1 + slot])

        @pl.when((k == 0) & (i == 0))
        def _():
            mine().start()
            for cp in first():
                cp.start()
            mine().wait()
            copy(0, 0, 1 - c, me).wait_recv()

        for j in range(3):
            @pl.when((k == j + 1) & (i == 0))
            def _():
                copy(4 + j, 1 + j, 1 - c, me).wait_recv()

        @pl.when((k == 3) & (i == 0))
        def _():
            for slot in range(N_CHIP):
                save(slot).start()

        @pl.when(k == 0)
        def _():
            xh, _ = _rms(x_ref[...])
            h = xh * g_ref[...]
            h1_ref[...] = h.astype(BF16)
            h1t_ref[...] = h.T.astype(BF16)
            h1_scr[pl.ds(pl.multiple_of(i * tm, tm), tm), :] = h.astype(BF16)

        proj_ref[...] = _dot(h1_scr[pl.ds(pl.multiple_of(i * tm, tm), tm), :], w_scr[k]).astype(BF16)

        for j in range(3):
            @pl.when((k == 0) & (i == nt - 1) if j < 2 else (k == 2) & (i == fwd_at))
            def _():
                copy(1 + j, 1 + j, c, me).wait_recv()
                copy(4 + j, 1 + j, c, sibling).start()
                if j < 2:
                    pl.when(c == 1 - j)(lambda: relay().start())

        @pl.when((k == 3) & (i == nt - 1))
        def _():
            for cp in first() + [relay()] + [copy(4 + j, 1 + j, c, sibling) for j in range(3)]:
                cp.wait_send()
            for slot in range(N_CHIP):
                save(slot).wait()

    once = lambda k, i, order_ref: (jnp.where(k == 0, i, nt - 1), 0)
    return _call(
        body,
        name="inproj",
        prefetch=[order],
        operands=[x, g, w_shard],
        grid=(N_CHIP, nt),
        in_specs=[
            pl.BlockSpec((tm, D_MODEL), once),
            pl.BlockSpec((1, D_MODEL), lambda k, i, order_ref: (0, 0)),
            pl.BlockSpec(memory_space=pl.ANY),
        ],
        out_specs=[
            pl.BlockSpec((tm, wide), lambda k, i, order_ref: (i, order_ref[k])),
            pl.BlockSpec((tm, D_MODEL), once),
            pl.BlockSpec((D_MODEL, tm), lambda k, i, order_ref: (0, jnp.where(k == 0, i, nt - 1))),
            pl.BlockSpec(memory_space=pl.ANY),
        ],
        out_shape=[
            jax.ShapeDtypeStruct((s, IN_DIM), BF16),
            jax.ShapeDtypeStruct((s, D_MODEL), BF16),
            jax.ShapeDtypeStruct((D_MODEL, s), BF16),
            jax.ShapeDtypeStruct((D_MODEL, IN_DIM), BF16),
        ],
        scratch_shapes=[
            pltpu.VMEM((N_CHIP, D_MODEL, wide), BF16),
            pltpu.VMEM((s, D_MODEL), BF16),
            pltpu.SemaphoreType.DMA((7,)),
            pltpu.SemaphoreType.DMA((7,)),
            pltpu.SemaphoreType.DMA((1 + N_CHIP,)),
        ],
        plans=plans,
        start_after_body=True,
        pass_on_at=0.8,
        body_peers="all",
    )


def _rows_from_above(v, above):
    row = lax.broadcasted_iota(jnp.int32, (8, v.shape[1]), 0)
    r1, r2 = pltpu.roll(v, 1, 0), pltpu.roll(v, 2, 0)
    top1 = jnp.where(row == 0, above[7:8], r1[0:8])
    top2 = jnp.where(row == 0, above[6:7], jnp.where(row == 1, above[7:8], r2[0:8]))
    return jnp.concatenate([top1, r1[8:]], axis=0), jnp.concatenate([top2, r2[8:]], axis=0)


def _rows_from_below(v, below):
    n = v.shape[0]
    row = lax.broadcasted_iota(jnp.int32, (8, v.shape[1]), 0)
    r1, r2 = pltpu.roll(v, n - 1, 0), pltpu.roll(v, n - 2, 0)
    end1 = jnp.where(row == 7, below[0:1], r1[n - 8 :])
    end2 = jnp.where(row == 6, below[0:1], jnp.where(row == 7, below[1:2], r2[n - 8 :]))
    return jnp.concatenate([r1[: n - 8], end1], axis=0), jnp.concatenate([r2[: n - 8], end2], axis=0)


def _causal_mask():
    row = lax.broadcasted_iota(jnp.int32, (CHUNK, CHUNK), 0)
    col = lax.broadcasted_iota(jnp.int32, (CHUNK, CHUNK), 1)
    return row, row >= col


def _mixer_fwd(proj, conv_w, ln_g, ln_b, ws, bs_b, plans=()):
    s = proj.shape[0]
    nt = s // CHUNK

    def body(proj_ref, cw_ref, lng_ref, lnb_ref, ws_ref, bsb_ref, cat_ref, pcar_ref):
        @pl.when(pl.program_id(0) == 0)
        def _():
            pcar_ref[...] = jnp.zeros_like(pcar_ref)

        row, tril = _causal_mask()
        for j in range(8):
            cs = slice(CHUNK * j, CHUNK * (j + 1))

            def grp(k):
                return proj_ref[:, k * D_MODEL + CHUNK * j : k * D_MODEL + CHUNK * (j + 1)].astype(F32)

            w = cw_ref[:, cs]
            p = grp(1) * grp(2)
            p1, p2 = _rows_from_above(p, pcar_ref[:, cs])
            pcar_ref[:, cs] = p[CHUNK - 8 :]
            cv = w[0:1] * p2 + w[1:2] * p1 + w[2:3] * p
            sa = _silu(grp(3), with_grad=False)
            cat_ref[:, cs] = ((grp(0) * cv) * sa).astype(BF16)
            gu = _gelu(grp(4), with_grad=False)
            gv = _gelu(grp(5), with_grad=False)
            dv = gv - jnp.mean(gv, axis=-1, keepdims=True)
            vn = dv * lax.rsqrt(jnp.mean(dv * dv, axis=-1, keepdims=True) + EPS)
            vn = vn * lng_ref[:, cs] + lnb_ref[:, cs]
            wc = jnp.where(tril, ws_ref[j], 0.0).astype(BF16)
            sp = _dot(wc, vn.astype(BF16)) + bsb_ref[j]
            sb = _silu(grp(6), with_grad=False)
            cat_ref[:, D_MODEL + CHUNK * j : D_MODEL + CHUNK * (j + 1)] = ((gu * sp) * sb).astype(BF16)

    whole = lambda shape: pl.BlockSpec(shape, lambda i: (0,) * len(shape))
    (cat,), plan_outs = _call(
        body,
        name="mixer_fwd",
        operands=[proj, conv_w, ln_g, ln_b, ws, bs_b],
        grid=(nt,),
        in_specs=[
            pl.BlockSpec((CHUNK, IN_DIM), lambda i: (i, 0)),
            whole((8, D_MODEL)),
            whole((1, D_MODEL)),
            whole((1, D_MODEL)),
            whole((8, CHUNK, CHUNK)),
            whole((8, CHUNK, CHUNK)),
        ],
        out_specs=[pl.BlockSpec((CHUNK, MIX_DIM), lambda i: (i, 0))],
        out_shape=[jax.ShapeDtypeStruct((s, MIX_DIM), BF16)],
        scratch_shapes=[pltpu.VMEM((8, D_MODEL), F32)],
        plans=plans,
        pass_on_at=0.85,
    )
    return cat, plan_outs


def _mixer_bwd(proj, dcat, conv_w, ln_g, ln_b, ws, bs_b, plans=()):
    s = proj.shape[0]
    rows = min(TM_MIXER_BWD, s)
    nt = s // rows
    chunks = [slice(ch * CHUNK, (ch + 1) * CHUNK) for ch in range(rows // CHUNK)]

    def body(proj_ref, halo_ref, dcat_ref, cw_ref, lng_ref, lnb_ref, ws_ref, bsb_ref,
             dproj_ref, dcw_ref, dlng_ref, dlnb_ref, dws_ref, dbs_ref,
             car_ref, acc_cw, acc_lng, acc_lnb, acc_bs, wc_scr, wct_scr):
        i = pl.program_id(0)
        tile = nt - 1 - i
        row, tril = _causal_mask()

        @pl.when(i == 0)
        def _():
            for j in range(8):
                wc = jnp.where(tril, ws_ref[j], 0.0)
                wc_scr[j] = wc.astype(BF16)
                wct_scr[j] = wc.T.astype(BF16)
            car_ref[...] = jnp.zeros_like(car_ref)
            acc_cw[...] = jnp.zeros_like(acc_cw)
            acc_lng[...] = jnp.zeros_like(acc_lng)
            acc_lnb[...] = jnp.zeros_like(acc_lnb)
            acc_bs[...] = jnp.zeros_like(acc_bs)
            dws_ref[...] = jnp.zeros_like(dws_ref)

        has_prev = jnp.where(tile > 0, 1.0, 0.0).astype(F32)
        for j in range(8):
            cs = slice(CHUNK * j, CHUNK * (j + 1))

            def col(k):
                return slice(k * D_MODEL + CHUNK * j, k * D_MODEL + CHUNK * (j + 1))

            def grp(k):
                return proj_ref[:, col(k)].astype(F32)

            w = cw_ref[:, cs]
            gb, gc, xa = grp(0), grp(1), grp(2)
            p = gc * xa
            pprev = halo_ref[:, col(1)].astype(F32) * halo_ref[:, col(2)].astype(F32) * has_prev
            p1, p2 = _rows_from_above(p, pprev[8:16])
            cv = w[0:1] * p2 + w[1:2] * p1 + w[2:3] * p
            za = grp(3)
            sa, dsa = _silu(za)
            da = dcat_ref[:, cs].astype(F32)
            da_gb = da * gb
            dproj_ref[:, col(0)] = ((da * sa) * cv).astype(BF16)
            dproj_ref[:, col(3)] = ((da_gb * cv) * dsa).astype(BF16)
            dcv = da_gb * sa
            d1, d2 = _rows_from_below(dcv, car_ref[:, cs])
            car_ref[:, cs] = dcv[0:8]
            dp = w[2:3] * dcv + w[1:2] * d1 + w[0:1] * d2
            dproj_ref[:, col(1)] = (dp * xa).astype(BF16)
            dproj_ref[:, col(2)] = (dp * gc).astype(BF16)
            acc_cw[0, :, cs] += _rows8(dcv * p2)
            acc_cw[1, :, cs] += _rows8(dcv * p1)
            acc_cw[2, :, cs] += _rows8(dcv * p)
            gu, dgu = _gelu(grp(4))
            gv, dgv = _gelu(grp(5))
            dv = gv - jnp.mean(gv, axis=-1, keepdims=True)
            rstd = lax.rsqrt(jnp.mean(dv * dv, axis=-1, keepdims=True) + EPS)
            vnh = dv * rstd
            lng = lng_ref[:, cs]
            vnb = (vnh * lng + lnb_ref[:, cs]).astype(BF16)
            sp = jnp.concatenate([_dot(wc_scr[j], vnb[ch]) + bsb_ref[j] for ch in chunks], axis=0)
            zb = grp(6)
            sb, dsb = _silu(zb)
            db = dcat_ref[:, D_MODEL + CHUNK * j : D_MODEL + CHUNK * (j + 1)].astype(F32)
            db_sp, db_gu = db * sp, db * gu
            dproj_ref[:, col(4)] = ((db_sp * sb) * dgu).astype(BF16)
            dproj_ref[:, col(6)] = ((db_sp * gu) * dsb).astype(BF16)
            dsp = db_gu * sb
            dspb = dsp.astype(BF16)
            acc_bs[j] += sum(dsp[ch] for ch in chunks)
            dws_ref[j] += sum(_dot_nt(dspb[ch], vnb[ch]) for ch in chunks)
            dvn = jnp.concatenate([_dot(wct_scr[j], dspb[ch]) for ch in chunks], axis=0)
            acc_lnb[:, cs] += _rows8(dvn)
            acc_lng[:, cs] += _rows8(dvn * vnh)
            dvh = dvn * lng
            dgvv = rstd * (dvh - jnp.mean(dvh, axis=-1, keepdims=True) - vnh * jnp.mean(dvh * vnh, axis=-1, keepdims=True))
            dproj_ref[:, col(5)] = (dgvv * dgv).astype(BF16)

        @pl.when(i == nt - 1)
        def _():
            dlng_ref[...] = jnp.sum(acc_lng[...], axis=0, keepdims=True)
            dlnb_ref[...] = jnp.sum(acc_lnb[...], axis=0, keepdims=True)
            dcw_ref[...] = jnp.zeros_like(dcw_ref)
            for k in range(3):
                dcw_ref[k : k + 1, :] = jnp.sum(acc_cw[k], axis=0, keepdims=True)
            for j in range(8):
                dws_ref[j] = jnp.where(tril, dws_ref[j], 0.0)
                dbs_ref[:, CHUNK * j : CHUNK * (j + 1)] = jnp.sum(acc_bs[j].T, axis=0, keepdims=True)

    whole = lambda shape: pl.BlockSpec(shape, lambda i: (0,) * len(shape))
    halo_rows = 16
    per = rows // halo_rows
    return _call(
        body,
        name="mixer_bwd",
        operands=[proj, proj, dcat, conv_w, ln_g, ln_b, ws, bs_b],
        plans=plans,
        grid=(nt,),
        in_specs=[
            pl.BlockSpec((rows, IN_DIM), lambda i: (nt - 1 - i, 0)),
            pl.BlockSpec((halo_rows, IN_DIM), lambda i: (jnp.maximum((nt - 1 - i) * per - 1, 0), 0)),
            pl.BlockSpec((rows, MIX_DIM), lambda i: (nt - 1 - i, 0)),
            whole((8, D_MODEL)),
            whole((1, D_MODEL)),
            whole((1, D_MODEL)),
            whole((8, CHUNK, CHUNK)),
            whole((8, CHUNK, CHUNK)),
        ],
        out_specs=[
            pl.BlockSpec((rows, IN_DIM), lambda i: (nt - 1 - i, 0)),
            whole((8, D_MODEL)),
            whole((1, D_MODEL)),
            whole((1, D_MODEL)),
            whole((8, CHUNK, CHUNK)),
            whole((1, D_MODEL)),
        ],
        out_shape=[
            jax.ShapeDtypeStruct((s, IN_DIM), BF16),
            jax.ShapeDtypeStruct((8, D_MODEL), F32),
            jax.ShapeDtypeStruct((1, D_MODEL), F32),
            jax.ShapeDtypeStruct((1, D_MODEL), F32),
            jax.ShapeDtypeStruct((8, CHUNK, CHUNK), F32),
            jax.ShapeDtypeStruct((1, D_MODEL), F32),
        ],
        scratch_shapes=[
            pltpu.VMEM((8, D_MODEL), F32),
            pltpu.VMEM((3, 8, D_MODEL), F32),
            pltpu.VMEM((8, D_MODEL), F32),
            pltpu.VMEM((8, D_MODEL), F32),
            pltpu.VMEM((8, CHUNK, CHUNK), F32),
            pltpu.VMEM((8, CHUNK, CHUNK), BF16),
            pltpu.VMEM((8, CHUNK, CHUNK), BF16),
        ],
    )


def _kv_fwd(mem, g_mem, w_kv):
    def body(mem_ref, g_ref, w_ref, k_ref, vv_ref, mt_ref):
        mh, _ = _rms(mem_ref[...])
        m = mh * g_ref[...]
        kv = _dot(m.astype(BF16), w_ref[...])
        k_ref[...] = kv[:, :D_MODEL].astype(BF16)
        vv_ref[...] = kv[:, D_MODEL:].astype(BF16)
        mt_ref[...] = m.T.astype(BF16)

    return pl.pallas_call(
        body,
        name="kv_fwd",
        out_shape=[
            jax.ShapeDtypeStruct((MEM_LEN, D_MODEL), BF16),
            jax.ShapeDtypeStruct((MEM_LEN, D_MODEL), BF16),
            jax.ShapeDtypeStruct((D_MODEL, MEM_LEN), BF16),
        ],
    )(mem, g_mem, w_kv)


def _kv_bwd(mem, w_kv, mt, dk, dvv):
    def body(mem_ref, w_ref, mt_ref, dk_ref, dvv_ref, dw_ref, dwb_ref, dg_ref):
        dkv = jnp.concatenate([dk_ref[...], dvv_ref[...]], axis=1).astype(BF16)
        dw = _dot(mt_ref[...], dkv)
        dw_ref[...] = dw
        dwb_ref[...] = dw.astype(BF16)
        dm = _dot_nt(dkv, w_ref[...])
        mh, _ = _rms(mem_ref[...])
        dg_ref[...] = jnp.sum(dm * mh, axis=0, keepdims=True)

    return pl.pallas_call(
        body,
        name="kv_bwd",
        out_shape=[
            jax.ShapeDtypeStruct((D_MODEL, 2 * D_MODEL), F32),
            jax.ShapeDtypeStruct((D_MODEL, 2 * D_MODEL), BF16),
            jax.ShapeDtypeStruct((1, D_MODEL), F32),
        ],
    )(mem, w_kv, mt, dk, dvv)


def _attn(x, cat, target, w_out, w_q, w_xo, k, vv, g_x, g_f):
    s = x.shape[0]
    tm = min(TM_ATTN, s)
    sub = min(SUB_ATTN, tm)
    nt = s // tm
    scale = 1.0 / math.sqrt(X_HEAD_DIM)

    def body(x_ref, cat_ref, t_ref, wout_ref, wq_ref, wxo_ref, k_ref, vv_ref, gx_ref, gf_ref,
             dcat_ref, actt_ref, cot_ref,
             dk_ref, dvv_ref, dgf_ref, dgx_ref, loss_ref,
             p_scr, q_scr, o_scr, dq_scr, acc_gf, acc_gx, acc_loss):
        i = pl.program_id(0)

        @pl.when(i == 0)
        def _():
            dk_ref[...] = jnp.zeros_like(dk_ref)
            dvv_ref[...] = jnp.zeros_like(dvv_ref)
            acc_gf[...] = jnp.zeros_like(acc_gf)
            acc_gx[...] = jnp.zeros_like(acc_gx)
            acc_loss[...] = jnp.zeros_like(acc_loss)

        subs = [slice(u * sub, (u + 1) * sub) for u in range(tm // sub)]
        heads = [slice(X_HEAD_DIM * h, X_HEAD_DIM * (h + 1)) for h in range(X_HEADS)]
        gx, gf = gx_ref[...], gf_ref[...]
        x1, x1h, r2 = [], [], []
        for rs in subs:
            cat = cat_ref[rs, :]
            actt_ref[0:MIX_DIM, rs] = cat.astype(F32).T.astype(BF16)
            v = x_ref[rs, :] + _dot(cat, wout_ref[...])
            vh, r = _rms(v)
            h2 = vh * gx
            actt_ref[MIX_DIM : MIX_DIM + D_MODEL, rs] = h2.T.astype(BF16)
            q_scr[rs, :] = _dot(h2.astype(BF16), wq_ref[...]).astype(BF16)
            x1.append(v), x1h.append(vh), r2.append(r)
        for rs in subs:
            for hs in heads:
                sc = _dot_nt(q_scr[rs, hs], k_ref[:, hs]) * scale
                e = jnp.exp(sc - jnp.max(sc, axis=-1, keepdims=True))
                p = e / jnp.sum(e, axis=-1, keepdims=True)
                p_scr[rs, hs] = p
                o_scr[rs, hs] = _dot(p.astype(BF16), vv_ref[:, hs])
        dx2, sq, d_gf = [], None, None
        for u, rs in enumerate(subs):
            o = o_scr[rs, :]
            actt_ref[MIX_DIM + D_MODEL :, rs] = o.T.astype(BF16)
            x2 = x1[u] + _dot(o.astype(BF16), wxo_ref[...])
            x2h, r3 = _rms(x2)
            err = x2h * gf - t_ref[rs, :]
            dy = err * (1.0 / D_MODEL)
            sq = _rows8(err * err) if sq is None else sq + _rows8(err * err)
            d_gf = _rows8(dy * x2h) if d_gf is None else d_gf + _rows8(dy * x2h)
            dx2.append(_rms_bwd(dy * gf, x2h, r3))
        acc_loss[...] += sq
        acc_gf[...] += d_gf
        d_vv, d_k = [None] * X_HEADS, [None] * X_HEADS
        for u, rs in enumerate(subs):
            dx2b = dx2[u].astype(BF16)
            cot_ref[2, rs, :] = dx2b
            dob = _dot_nt(dx2b, wxo_ref[...]).astype(BF16)
            for h, hs in enumerate(heads):
                p = p_scr[rs, hs]
                dp = _dot_nt(dob[:, hs], vv_ref[:, hs])
                ds = p * (dp - jnp.sum(dp * p, axis=-1, keepdims=True)) * scale
                dq_scr[rs, hs] = _dot(ds.astype(BF16), k_ref[:, hs]).astype(BF16)
                dvv_h = _dot(p.T.astype(BF16), dob[:, hs])
                dk_h = _dot(ds.T.astype(BF16), q_scr[rs, hs])
                d_vv[h] = dvv_h if d_vv[h] is None else d_vv[h] + dvv_h
                d_k[h] = dk_h if d_k[h] is None else d_k[h] + dk_h
        for h, hs in enumerate(heads):
            dvv_ref[:, hs] += d_vv[h]
            dk_ref[:, hs] += d_k[h]
        d_gx = None
        for u, rs in enumerate(subs):
            dqb = dq_scr[rs, :]
            cot_ref[1, rs, :] = dqb
            dh2 = _dot_nt(dqb, wq_ref[...])
            d_gx = _rows8(dh2 * x1h[u]) if d_gx is None else d_gx + _rows8(dh2 * x1h[u])
            dx1 = dx2[u] + _rms_bwd(dh2 * gx, x1h[u], r2[u])
            dx1b = dx1.astype(BF16)
            cot_ref[0, rs, :] = dx1b
            dcat_ref[rs, :] = _dot_nt(dx1b, wout_ref[...]).astype(BF16)
        acc_gx[...] += d_gx

        @pl.when(i == nt - 1)
        def _():
            dgf_ref[...] = jnp.sum(acc_gf[...], axis=0, keepdims=True)
            dgx_ref[...] = jnp.sum(acc_gx[...], axis=0, keepdims=True)
            loss_ref[...] = jnp.sum(acc_loss[...], axis=0, keepdims=True)

    whole = lambda shape: pl.BlockSpec(shape, lambda i: (0,) * len(shape))
    once = lambda shape: pl.BlockSpec(shape, lambda i: (0,) * len(shape), pipeline_mode=pl.Buffered(1))
    rows = lambda width: pl.BlockSpec((tm, width), lambda i: (i, 0))
    cols = lambda height: pl.BlockSpec((height, tm), lambda i: (0, i))
    vec = jax.ShapeDtypeStruct((1, D_MODEL), F32)
    return pl.pallas_call(
        body,
        name="attn",
        grid=(nt,),
        in_specs=[
            rows(D_MODEL), rows(MIX_DIM), rows(D_MODEL),
            once((MIX_DIM, D_MODEL)), once((D_MODEL, D_MODEL)), once((D_MODEL, D_MODEL)),
            once((MEM_LEN, D_MODEL)), once((MEM_LEN, D_MODEL)),
            whole((1, D_MODEL)), whole((1, D_MODEL)),
        ],
        out_specs=[
            rows(MIX_DIM), cols(2 * MIX_DIM),
            pl.BlockSpec((3, tm, D_MODEL), lambda i: (0, i, 0)),
            whole((MEM_LEN, D_MODEL)), whole((MEM_LEN, D_MODEL)),
            whole((1, D_MODEL)), whole((1, D_MODEL)), whole((1, D_MODEL)),
        ],
        out_shape=[
            jax.ShapeDtypeStruct((s, MIX_DIM), BF16),
            jax.ShapeDtypeStruct((2 * MIX_DIM, s), BF16),
            jax.ShapeDtypeStruct((3, s, D_MODEL), BF16),
            jax.ShapeDtypeStruct((MEM_LEN, D_MODEL), F32),
            jax.ShapeDtypeStruct((MEM_LEN, D_MODEL), F32),
            vec, vec, vec,
        ],
        scratch_shapes=[
            pltpu.VMEM((tm, D_MODEL), F32),
            pltpu.VMEM((tm, D_MODEL), BF16),
            pltpu.VMEM((tm, D_MODEL), F32),
            pltpu.VMEM((tm, D_MODEL), BF16),
            pltpu.VMEM((8, D_MODEL), F32),
            pltpu.VMEM((8, D_MODEL), F32),
            pltpu.VMEM((8, D_MODEL), F32),
        ],
        compiler_params=pltpu.CompilerParams(dimension_semantics=("arbitrary",)),
    )(x, cat, target, w_out, w_q, w_xo, k, vv, g_x, g_f)


def _matmul_f32(name, a, b, rows=None, plans=()):
    first, m = (0, a.shape[0]) if rows is None else rows
    kk = a.shape[1]
    n = b.shape[1]
    tm, tn, tk = min(m, TM_DW), min(n, D_MODEL), min(kk, TK_DW)
    nk = kk // tk
    i0 = first // tm

    def body(a_ref, b_ref, o_ref, ob_ref):
        prod = _dot(a_ref[...], b_ref[...])
        if nk == 1:
            o_ref[...] = prod
            ob_ref[...] = prod.astype(BF16)
            return
        kid = pl.program_id(2)

        @pl.when(kid == 0)
        def _():
            o_ref[...] = prod

        @pl.when(kid > 0)
        def _():
            o_ref[...] += prod

        @pl.when(kid == nk - 1)
        def _():
            ob_ref[...] = o_ref[...].astype(BF16)

    out_spec = pl.BlockSpec((tm, tn), lambda i, j, k: (i, j))
    outs, plan_outs = _call(
        body,
        name=name,
        operands=[a, b],
        grid=(m // tm, n // tn, nk),
        in_specs=[pl.BlockSpec((tm, tk), lambda i, j, k: (i + i0, k)), pl.BlockSpec((tk, tn), lambda i, j, k: (k, j))],
        out_specs=[out_spec, out_spec],
        out_shape=[jax.ShapeDtypeStruct((m, n), F32), jax.ShapeDtypeStruct((m, n), BF16)],
        plans=plans,
    )
    return outs, plan_outs


def _dw_attn(actt, cot, plans=()):
    s = actt.shape[1]
    tm = TM_DW
    counts = [MIX_DIM // tm, D_MODEL // tm, D_MODEL // tm]
    starts = [0, counts[0], counts[0] + counts[1]]
    which = lambda i: jnp.where(i < starts[1], 0, jnp.where(i < starts[2], 1, 2))

    def body(a_ref, b_ref, *outs):
        i = pl.program_id(0)
        prod = _dot(a_ref[...], b_ref[0])
        for m in range(3):
            @pl.when(which(i) == m)
            def _():
                outs[2 * m][...] = prod
                outs[2 * m + 1][...] = prod.astype(BF16)

    out_specs, out_shape = [], []
    for m in range(3):
        spec = pl.BlockSpec((tm, D_MODEL), lambda i, m=m: (jnp.clip(i - starts[m], 0, counts[m] - 1), 0))
        out_specs += [spec, spec]
        out_shape += [jax.ShapeDtypeStruct((counts[m] * tm, D_MODEL), F32), jax.ShapeDtypeStruct((counts[m] * tm, D_MODEL), BF16)]
    outs, plan_outs = _call(
        body,
        name="dw_attn",
        operands=[actt, cot],
        grid=(sum(counts),),
        in_specs=[pl.BlockSpec((tm, s), lambda i: (i, 0)), pl.BlockSpec((1, s, D_MODEL), lambda i: (which(i), 0, 0))],
        out_specs=out_specs,
        out_shape=out_shape,
        plans=plans,
    )
    return [outs[2 * m : 2 * m + 2] for m in range(3)], plan_outs


def _dh1(name, dproj, w_in, x, dx1, g, tiles, into=None, after=(), plans=()):
    s = x.shape[0]
    tm = min(TM_DH1, s)
    t0, nt = tiles[0], tiles[1] - tiles[0]

    def body(dp_ref, w_ref, x_ref, dx1_ref, g_ref, *rest):
        gx_ref, dg_ref, acc_g = rest[-3:]
        i = pl.program_id(0)

        @pl.when(i == 0)
        def _():
            acc_g[...] = jnp.zeros_like(acc_g)

        dh1 = _dot_nt(dp_ref[...], w_ref[...])
        xh, r = _rms(x_ref[...])
        acc_g[...] += _rows8(dh1 * xh)
        gx_ref[...] = dx1_ref[0].astype(F32) + _rms_bwd(dh1 * g_ref[...], xh, r)

        @pl.when(i == nt - 1)
        def _():
            dg_ref[...] = jnp.sum(acc_g[...], axis=0, keepdims=True)

    rows = pl.BlockSpec((tm, D_MODEL), lambda i: (i + t0, 0))
    return _call(
        body,
        name=name,
        operands=[dproj, w_in, x, dx1, g] + ([into] if into is not None else []) + list(after),
        grid=(nt,),
        in_specs=[
            pl.BlockSpec((tm, IN_DIM), lambda i: (i + t0, 0)),
            pl.BlockSpec((D_MODEL, IN_DIM), lambda i: (0, 0), pipeline_mode=pl.Buffered(1)),
            rows,
            pl.BlockSpec((1, tm, D_MODEL), lambda i: (0, i + t0, 0)),
            pl.BlockSpec((1, D_MODEL), lambda i: (0, 0)),
        ] + [pl.BlockSpec(memory_space=pl.ANY)] * ((into is not None) + len(after)),
        out_specs=[rows, pl.BlockSpec((1, D_MODEL), lambda i: (0, 0))],
        out_shape=[jax.ShapeDtypeStruct((s, D_MODEL), F32), jax.ShapeDtypeStruct((1, D_MODEL), F32)],
        scratch_shapes=[pltpu.VMEM((8, D_MODEL), F32)],
        plans=plans,
        aliases={5: 0} if into is not None else None,
    )


def _pair_sum(name, grad, got, axis, size, core, plans=()):
    shard = list(grad.shape)
    shard[axis] = size
    r, cdim = shard
    tr = min(r, TR_PAIR_SUM)
    nr = r // tr

    def body(core_ref, g_ref, got_ref, o_ref):
        o_ref[0] = (g_ref[...] + got_ref[0].astype(F32)).astype(BF16)

    if axis == 1:
        g_map = lambda q, t, core_ref: (t, 2 * q + core_ref[0])
    else:
        g_map = lambda q, t, core_ref: ((2 * q + core_ref[0]) * nr + t, 0)
    (out,), plan_outs = _call(
        body,
        name=name,
        prefetch=[core],
        operands=[grad, got],
        grid=(N_CHIP, nr),
        in_specs=[
            pl.BlockSpec((tr, cdim), g_map),
            pl.BlockSpec((1, tr, cdim), lambda q, t, core_ref: (q, t, 0)),
        ],
        out_specs=[pl.BlockSpec((1, tr, cdim), lambda q, t, core_ref: (q, t, 0))],
        out_shape=[jax.ShapeDtypeStruct((N_CHIP, r, cdim), BF16)],
        plans=plans,
    )
    return (out, plan_outs) if plans else out


def _adam_step(w, m, v, g):
    c1 = 1.0 / (1.0 - ADAM_B1**ADAM_STEP)
    c2 = 1.0 / (1.0 - ADAM_B2**ADAM_STEP)
    m_new = ADAM_B1 * m + (1.0 - ADAM_B1) * g
    v_new = ADAM_B2 * v + (1.0 - ADAM_B2) * (g * g)
    return -ADAM_LR * ((m_new * c1) / (jnp.sqrt(v_new * c2) + ADAM_EPS) + ADAM_WD * w), m_new, v_new


def _update_small(dev, vec_w, vec_m, vec_v, conv_w, conv_m, conv_v, small, landed):
    n_vec = len(vec_w)

    def body(dev_ref, *refs):
        refs = list(refs)
        take = lambda k: [refs.pop(0) for _ in range(k)]
        w_refs, m_refs, v_refs = take(n_vec), take(n_vec), take(n_vec)
        cw_ref, cm_ref, cv_ref, small_ref, land_ref, small_conv_ref, land_conv_ref = take(7)
        vec_outs, conv_outs, (loss_ref,) = take(4 * n_vec), take(4), take(1)

        def total(own_ref, others_ref):
            acc = None
            for d in range(N_DEV):
                part = jnp.where(dev_ref[0] == d, own_ref[0], others_ref[d])
                acc = part if acc is None else acc + part
            return acc

        g_all = total(small_ref, land_ref)
        for i in range(n_vec):
            g = g_all[i : i + 1]
            delta, m_new, v_new = _adam_step(w_refs[i][...], m_refs[i][...], v_refs[i][...], g)
            for ref, val in zip(vec_outs[4 * i : 4 * i + 4], (g, delta, m_new, v_new)):
                ref[...] = val
        g = total(small_conv_ref, land_conv_ref)[0:3]
        delta, m_new, v_new = _adam_step(cw_ref[0], cm_ref[0], cv_ref[0], g)
        for ref, val in zip(conv_outs, (g, delta, m_new, v_new)):
            ref[0] = val
        loss_ref[...] = jnp.broadcast_to(0.5 * jnp.sum(g_all[7:8], axis=-1, keepdims=True) / D_MODEL, loss_ref.shape)

    vec_spec = pl.BlockSpec((1, D_MODEL), lambda i, dev_ref: (0, 0))
    conv_spec = pl.BlockSpec((1, 3, CHUNK), lambda i, dev_ref: (0, 0, 0))
    vec_shape = jax.ShapeDtypeStruct((1, D_MODEL), F32)
    conv_shape = jax.ShapeDtypeStruct((1, 3, CHUNK), F32)
    outs, _ = _call(
        body,
        name="update_small",
        prefetch=[dev],
        operands=list(vec_w) + list(vec_m) + list(vec_v) + [conv_w, conv_m, conv_v, small, landed, small, landed],
        grid=(1,),
        in_specs=[vec_spec] * (3 * n_vec) + [conv_spec] * 3 + [
            pl.BlockSpec((1, 16, D_MODEL), lambda i, dev_ref: (0, 0, 0)),
            pl.BlockSpec((N_DEV, 16, D_MODEL), lambda i, dev_ref: (0, 0, 0)),
            pl.BlockSpec((1, 8, CHUNK), lambda i, dev_ref: (0, 1, dev_ref[0])),
            pl.BlockSpec((N_DEV, 8, CHUNK), lambda i, dev_ref: (0, 1, dev_ref[0])),
        ],
        out_specs=[vec_spec] * (4 * n_vec) + [conv_spec] * 4 + [pl.BlockSpec((1, CHUNK), lambda i, dev_ref: (0, 0))],
        out_shape=[vec_shape] * (4 * n_vec) + [conv_shape] * 4 + [jax.ShapeDtypeStruct((1, CHUNK), F32)],
    )
    return [outs[4 * i : 4 * i + 4] for i in range(n_vec)], outs[4 * n_vec : 4 * n_vec + 4], outs[-1]


def _adamw(name, groups, steps, own_slot=None, plans=()):
    lists = lambda p: list(p) if isinstance(p, (list, tuple)) else [p]
    groups = [(g[0], g[1], g[2], lists(g[3]), lists(g[4]) if len(g) > 4 else []) for g in groups]
    counts = [3 + len(g[3]) + len(g[4]) for g in groups]
    use_slot = own_slot is not None

    def first_tiles(w, parts):
        firsts, first = [], 0
        for p in parts:
            firsts.append(first)
            first += p.shape[1] // (w.shape[0] // steps)
        return firsts

    def update(t, slot_ref, firsts, w_ref, m_ref, v_ref, p_refs, o_refs, g_ref, d_ref, nm_ref, nv_ref):
        g = None
        for k, p_ref in enumerate(p_refs):
            gk = None
            for q in range(p_ref.shape[0]):
                part = p_ref[q].astype(F32)
                if o_refs:
                    part = jnp.where(slot_ref[0] == q, o_refs[k][0].astype(F32), part)
                gk = part if gk is None else gk + part
            g = gk if g is None else jnp.where(t >= firsts[k], gk, g)
        g_ref[...] = g
        d_ref[...], nm_ref[...], nv_ref[...] = _adam_step(w_ref[...], m_ref[...], v_ref[...], g)

    def body(*refs):
        refs = list(refs)
        slot_ref = refs.pop(0) if use_slot else None
        t = pl.program_id(0)
        ins, outs = refs[: sum(counts)], refs[sum(counts) :]
        for i, (w, _, _, parts, own) in enumerate(groups):
            mine = ins[sum(counts[:i]) : sum(counts[: i + 1])]
            update(t, slot_ref, first_tiles(w, parts), *mine[:3], mine[3 : 3 + len(parts)], mine[3 + len(parts) :],
                   *outs[4 * i : 4 * i + 4])

    operands, in_specs, out_specs, out_shape = [], [], [], []
    for w, m, v, parts, own in groups:
        r, cdim = w.shape
        tr = r // steps
        firsts = first_tiles(w, parts)

        def tile_of(k, t, firsts=firsts, parts=parts, tr=tr):
            return jnp.clip(t - firsts[k], 0, parts[k].shape[1] // tr - 1)

        blk = pl.BlockSpec((tr, cdim), lambda t, *_: (t, 0))
        operands += [w, m, v] + parts + own
        in_specs += [blk, blk, blk]
        in_specs += [pl.BlockSpec((p.shape[0], tr, cdim), lambda t, *_, k=k, tile_of=tile_of: (0, tile_of(k, t), 0)) for k, p in enumerate(parts)]
        in_specs += [pl.BlockSpec((1, tr, cdim), lambda t, slot_ref, k=k, tile_of=tile_of: (slot_ref[0], tile_of(k, t), 0)) for k in range(len(own))]
        out_specs += [blk] * 4
        out_shape += [jax.ShapeDtypeStruct((r, cdim), F32)] * 4
    outs, plan_outs = _call(
        body,
        name=name,
        prefetch=[own_slot] if use_slot else [],
        operands=operands,
        grid=(steps,),
        in_specs=in_specs,
        out_specs=out_specs,
        out_shape=out_shape,
        plans=plans,
    )
    return [outs[4 * i : 4 * i + 4] for i in range(len(groups))], plan_outs


def _chip_routes(n):
    x, y, c = _place()
    my_chip = 2 * x + y
    return [(a, (*chip, c), 2 * chip[0] + chip[1], my_chip, 2 * chip[0] + chip[1])
            for a in range(n) for chip in _other_chips(x, y)]


def _gather_routes(n):
    x, y, c = _place()
    flip = lambda v, f: 1 - v if f else v
    peers = [(flip(x, fx), flip(y, fy), flip(c, fc)) for fx in (0, 1) for fy in (0, 1) for fc in (0, 1) if fx + fy + fc]
    return [(0, p, 0, 4 * x + 2 * y + c, 4 * p[0] + 2 * p[1] + p[2]) for p in peers]


_chip_routes.copies_per_array = N_CHIP - 1
_gather_routes.copies_per_array = N_DEV - 1


def _split_start(name, routes, srcs, land_shapes):
    n = len(srcs)
    hbm = pl.BlockSpec(memory_space=pltpu.HBM)
    sem = pl.BlockSpec(memory_space=pltpu.SEMAPHORE)

    def body(*refs):
        src_refs, land_refs = refs[:n], refs[n : 2 * n]
        send_sems, recv_sems = refs[2 * n], refs[2 * n + 1]
        token = refs[-1]
        for k, (a, peer, src_slot, there, _) in enumerate(routes(n)):
            pltpu.make_async_remote_copy(
                src_ref=src_refs[a].at[src_slot], dst_ref=land_refs[a].at[there],
                send_sem=send_sems.at[k], recv_sem=recv_sems.at[k], device_id=peer, device_id_type=MESH).start()
        token[...] = jnp.zeros_like(token)

    in_hbm = [pltpu.with_memory_space_constraint(p, pltpu.HBM) for p in srcs]
    in_hbm += [pltpu.with_memory_space_constraint(lax.empty(shape, p.dtype), pltpu.HBM) for p, shape in zip(srcs, land_shapes)]
    n_copies = routes.copies_per_array * n
    res = pl.pallas_call(
        body,
        name=name,
        out_shape=[pltpu.SemaphoreType.DMA((n_copies,)), pltpu.SemaphoreType.DMA((n_copies,))]
        + [pltpu.HBM(p.shape, p.dtype) for p in srcs] + [pltpu.HBM(shape, p.dtype) for p, shape in zip(srcs, land_shapes)]
        + [jax.ShapeDtypeStruct((8, CHUNK), F32)],
        in_specs=[hbm] * (2 * n),
        out_specs=[sem, sem] + [hbm] * (2 * n) + [pl.BlockSpec(memory_space=pltpu.VMEM)],
        input_output_aliases={i: 2 + i for i in range(2 * n)},
        compiler_params=pltpu.CompilerParams(has_side_effects=pltpu.SideEffectType.DATAFLOW_SIDE_EFFECTING),
    )(*in_hbm)
    return res[:-1], res[-1]


def _split_wait(name, routes, in_flight, after):
    n = (len(in_flight) - 2) // 2
    hbm = pl.BlockSpec(memory_space=pltpu.HBM)
    sem = pl.BlockSpec(memory_space=pltpu.SEMAPHORE)
    send_sems, recv_sems, *bufs = in_flight

    def body(*refs):
        src_refs, land_refs = refs[:n], refs[n : 2 * n]
        send_ref, recv_ref = refs[2 * n], refs[2 * n + 1]
        for k, (a, peer, src_slot, _, here) in enumerate(routes(n)):
            cp = pltpu.make_async_remote_copy(
                src_ref=src_refs[a].at[src_slot], dst_ref=land_refs[a].at[here],
                send_sem=send_ref.at[k], recv_sem=recv_ref.at[k], device_id=peer, device_id_type=MESH)
            cp.wait_send()
            cp.wait_recv()

    res = pl.pallas_call(
        body,
        name=name,
        out_shape=[pltpu.HBM(b.shape, b.dtype) for b in bufs],
        in_specs=[hbm] * (2 * n) + [sem, sem] + [pl.BlockSpec(memory_space=pl.ANY)] * len(after),
        out_specs=[hbm] * (2 * n),
        input_output_aliases={i: i for i in range(2 * n)},
        compiler_params=pltpu.CompilerParams(has_side_effects=pltpu.SideEffectType.DATAFLOW_SIDE_EFFECTING),
    )(*bufs, send_sems, recv_sems, *after)
    return res[:n], res[n:]


def kernel(x, mem, norm_mix_g, w_in, conv_w, gm_ln_g, gm_ln_b, gm_ws, gm_bs, w_out, norm_x_g, norm_mem_g, w_q, w_kv, w_xo, norm_final_g, loss_target, m_norm_mix_g, m_w_in, m_conv_w, m_gm_ln_g, m_gm_ln_b, m_gm_ws, m_gm_bs, m_w_out, m_norm_x_g, m_norm_mem_g, m_w_q, m_w_kv, m_w_xo, m_norm_final_g, v_norm_mix_g, v_w_in, v_conv_w, v_gm_ln_g, v_gm_ln_b, v_gm_ws, v_gm_bs, v_w_out, v_norm_x_g, v_norm_mem_g, v_w_q, v_w_kv, v_w_xo, v_norm_final_g):
    s = x.shape[1]
    dev = 4 * lax.axis_index("x") + 2 * lax.axis_index("y") + lax.axis_index("c")
    core = lax.axis_index("c").astype(jnp.int32).reshape(1)
    x2 = x[0]
    target = loss_target[0]
    pad_taps = lambda t: jnp.pad(t[0], ((0, 5), (0, 0)))

    big_names = ["w_in", "w_out", "w_q", "w_kv", "w_xo"]
    big_w = dict(w_in=w_in[0], w_out=w_out[0], w_q=w_q[0], w_kv=w_kv[0], w_xo=w_xo[0])
    big_m = dict(w_in=m_w_in[0], w_out=m_w_out[0], w_q=m_w_q[0], w_kv=m_w_kv[0], w_xo=m_w_xo[0])
    big_v = dict(w_in=v_w_in[0], w_out=v_w_out[0], w_q=v_w_q[0], w_kv=v_w_kv[0], w_xo=v_w_xo[0])
    shard_axis = dict(w_in=1, w_out=0, w_q=0, w_kv=1, w_xo=0)
    full_shape = dict(w_in=(D_MODEL, IN_DIM), w_out=(MIX_DIM, D_MODEL), w_q=(D_MODEL, D_MODEL),
                      w_kv=(D_MODEL, 2 * D_MODEL), w_xo=(D_MODEL, D_MODEL))
    shard_size = {k: big_w[k].shape[shard_axis[k]] for k in big_names}
    others = big_names[1:]
    gather_spec = lambda k: (full_shape[k], shard_axis[k], shard_size[k])
    reduce_spec = lambda k: (shard_axis[k], shard_size[k])
    ws = gm_ws[0]
    bs_b = jnp.broadcast_to(gm_bs[0][:, :, None], (8, CHUNK, CHUNK))
    g_f = norm_final_g.reshape(1, D_MODEL)

    px, py = lax.axis_index("x"), lax.axis_index("y")
    chip_order = jnp.stack([2 * px + py, 2 * (1 - px) + py, 2 * px + 1 - py, 2 * (1 - px) + 1 - py]).astype(jnp.int32)
    (proj, h1, h1t, w_in_full), ((w_out_full, conv_g),) = _inproj(
        x2, norm_mix_g, big_w["w_in"].astype(BF16), chip_order,
        plans=[_gather_plan([big_w["w_out"].astype(BF16), pad_taps(conv_w)[None]], [gather_spec("w_out"), ((N_DEV, 8, CHUNK), 0, 1)])])
    conv_full = conv_g.transpose(1, 0, 2).reshape(8, D_MODEL)
    attn_w = ["w_q", "w_kv", "w_xo"]
    cat, (gathered,) = _mixer_fwd(
        proj, conv_full, gm_ln_g, gm_ln_b, ws, bs_b,
        plans=[_gather_plan([big_w[k].astype(BF16) for k in attn_w], [gather_spec(k) for k in attn_w])])
    wf = dict(zip(attn_w, gathered), w_out=w_out_full)
    k, vv, mt = _kv_fwd(mem[0], norm_mem_g, wf["w_kv"])
    (dcat, actt, cot, dk, dvv, d_gf, d_gx, sq_err) = _attn(
        x2, cat, target, wf["w_out"], wf["w_q"], wf["w_xo"], k, vv, norm_x_g, g_f)

    def pair_sums(names, grads, got):
        return [_pair_sum("pair_sum_" + k, grads[k], got[i], *reduce_spec(k), core) for i, k in enumerate(names)]

    sibling_plan = lambda names, grads: _sibling_plan([grads[k] for k in names], [reduce_spec(k) for k in names])
    early = ["w_kv", "w_out", "w_q", "w_xo"]
    big_g, big_gb = {}, {}
    big_g["w_kv"], big_gb["w_kv"], d_gmem = _kv_bwd(mem[0], wf["w_kv"], mt, dk, dvv)
    dws, (got_kv,) = _dw_attn(actt, cot, plans=[sibling_plan(early[:1], big_gb)])
    for name, (f32, bf16) in zip(early[1:], dws):
        big_g[name], big_gb[name] = f32, bf16
    pair_kv, (got_rest,) = _pair_sum("pair_sum_w_kv", big_g["w_kv"], got_kv[0], *reduce_spec("w_kv"), core,
                                     plans=[sibling_plan(early[1:], big_gb)])
    (dproj, d_cw, d_lng, d_lnb, d_ws, d_bs), (parts_early,) = _mixer_bwd(
        proj, dcat, conv_full, gm_ln_g, gm_ln_b, ws, bs_b,
        plans=[_chip_plan([pair_kv] + pair_sums(early[1:], big_g, got_rest))])
    half = D_MODEL // 2
    in_spec = reduce_spec("w_in")
    (dw_top, dwb_top), _ = _matmul_f32("dw_in_top", h1t, dproj, rows=(0, half))
    (dw_bot, dwb_bot), (got_top, (ws_all,)) = _matmul_f32(
        "dw_in_bot", h1t, dproj, rows=(half, half),
        plans=[_sibling_plan([dwb_top], [in_spec]),
               _gather_plan([d_ws.reshape(1, 8 * CHUNK, CHUNK).astype(BF16)], [((N_DEV, 8 * CHUNK, CHUNK), 0, 1)])])
    pair_top, ((got_bot,),) = _pair_sum("pair_sum_w_in_top", dw_top, got_top[0], *in_spec, core,
                                        plans=[_sibling_plan([dwb_bot], [in_spec])])
    pair_bot = _pair_sum("pair_sum_w_in_bot", dw_bot, got_bot, *in_spec, core)
    chip_shapes = [p.shape for p in (pair_top, pair_bot)]
    chips_in_flight, token = _split_start("grad_chip_exchange_start", _chip_routes, [pair_top, pair_bot], chip_shapes)
    n_tiles = s // min(TM_DH1, s)
    (grad_x, d_gmix), _ = _dh1("dh1", dproj, w_in_full, x2, cot, norm_mix_g, (0, n_tiles), after=[token])
    vec_names = ["norm_mix_g", "gm_ln_g", "gm_ln_b", "gm_bs", "norm_x_g", "norm_mem_g", "norm_final_g"]
    vec_g = [d_gmix, d_lng, d_lnb, d_bs, d_gx, d_gmem, d_gf]
    small = jnp.concatenate(vec_g + [sq_err, d_cw], axis=0)[None]
    small_in_flight, _ = _split_start("small_all_gather_start", _gather_routes, [small], [(N_DEV, 16, D_MODEL)])
    flat_ws = lambda t: t.reshape(8 * CHUNK, CHUNK)
    updated, _ = _adamw(
        "adamw_early",
        [(big_w[k], big_m[k], big_v[k], got_k) for k, got_k in zip(early, parts_early)]
        + [(flat_ws(ws), flat_ws(m_gm_ws), flat_ws(v_gm_ws), ws_all)], steps=STEPS_ADAMW // 2)
    big_out = dict(zip(early, updated[:4]))
    ws_out = updated[4]
    (pair_top, pair_bot), (land_top, land_bot) = _split_wait(
        "grad_chip_exchange_wait", _chip_routes, chips_in_flight, [d_gmix] + [u[0] for u in updated])
    my_chip = (2 * px + py).astype(jnp.int32).reshape(1)
    (big_out["w_in"],), _ = _adamw(
        "adamw_w_in", [(big_w["w_in"], big_m["w_in"], big_v["w_in"], [land_top, land_bot], [pair_top, pair_bot])],
        steps=STEPS_ADAMW, own_slot=my_chip)
    (small,), (small_all,) = _split_wait("small_all_gather_wait", _gather_routes, small_in_flight, [big_out["w_in"][0]])
    row = lambda t: t.reshape(1, D_MODEL)
    vec_out, conv_out, loss_row = _update_small(
        dev.astype(jnp.int32).reshape(1),
        [norm_mix_g, gm_ln_g, gm_ln_b, row(gm_bs), norm_x_g, norm_mem_g, row(norm_final_g)],
        [m_norm_mix_g, m_gm_ln_g, m_gm_ln_b, row(m_gm_bs), m_norm_x_g, m_norm_mem_g, row(m_norm_final_g)],
        [v_norm_mix_g, v_gm_ln_g, v_gm_ln_b, row(v_gm_bs), v_norm_x_g, v_norm_mem_g, row(v_norm_final_g)],
        conv_w, m_conv_w, v_conv_w, small, small_all)
    loss = loss_row[0, 0]

    def result(name, which):
        if name in big_out:
            return big_out[name][which][None]
        if name == "conv_w":
            return conv_out[which]
        if name == "gm_ws":
            return ws_out[which].reshape(1, 8, CHUNK, CHUNK)
        r = vec_out[vec_names.index(name)][which]
        if name == "gm_bs":
            return r.reshape(1, 8, CHUNK)
        if name == "norm_final_g":
            return r.reshape(D_MODEL)
        return r

    weights = ["norm_mix_g", "w_in", "conv_w", "gm_ln_g", "gm_ln_b", "gm_ws", "gm_bs", "w_out", "norm_x_g", "norm_mem_g", "w_q", "w_kv", "w_xo", "norm_final_g"]
    outs = [loss, grad_x[None]]
    for which in range(4):
        outs += [result(name, which) for name in weights]
    return tuple(outs)
```

```python
import math

import jax
import jax.numpy as jnp
from jax import lax
from jax.experimental import pallas as pl
from jax.experimental.pallas import tpu as pltpu

F32 = jnp.float32
BF16 = jnp.bfloat16
MESH = pl.DeviceIdType.MESH

D_MODEL = 1024
N_DEV = 8
N_CHIP = 4
CHUNK = 128
GROUPS = 7
IN_DIM = GROUPS * D_MODEL
SHARD_IN = IN_DIM // N_DEV
MIX_DIM = 2 * D_MODEL
X_HEADS = 4
X_HEAD_DIM = D_MODEL // X_HEADS
MEM_LEN = 256
EPS = 1e-6
GELU_K0 = math.sqrt(2.0 / math.pi)
GELU_K1 = 0.044715

ADAM_LR = 0.001
ADAM_B1 = 0.9
ADAM_B2 = 0.999
ADAM_EPS = 1e-08
ADAM_WD = 0.01
ADAM_STEP = 10

TM_INPROJ = 512
TM_ATTN = 512
SUB_ATTN = 256
TM_DH1 = 512
TM_DW = 512
TK_DW = 4096
TR_PAIR_SUM = 1024
BARRIER_ID = {"sibling": 1, "chips": 2, "all": 3}
STEPS_ADAMW = 4


def _dot(a, b):
    return lax.dot_general(a, b, (((1,), (0,)), ((), ())), preferred_element_type=F32)


def _dot_nt(a, b):
    return lax.dot_general(a, b, (((1,), (1,)), ((), ())), preferred_element_type=F32)


def _rows8(v):
    return jnp.sum(v.reshape(v.shape[0] // 8, 8, v.shape[1]), axis=0)


def _gelu(x, with_grad=True):
    x2 = x * x
    t = jnp.tanh(x * (GELU_K0 + (GELU_K0 * GELU_K1) * x2))
    half = 0.5 + 0.5 * t
    g = x * half
    if not with_grad:
        return g
    return g, half + ((0.5 * x) * (1.0 - t * t)) * (GELU_K0 + (3.0 * GELU_K0 * GELU_K1) * x2)


def _silu(z, with_grad=True):
    s = jax.nn.sigmoid(z)
    zs = z * s
    return (zs, s + zs * (1.0 - s)) if with_grad else zs


def _rms(v):
    r = lax.rsqrt(jnp.mean(v * v, axis=-1, keepdims=True) + EPS)
    return v * r, r


def _rms_bwd(dy_g, vh, r):
    return r * (dy_g - vh * jnp.mean(dy_g * vh, axis=-1, keepdims=True))


def _place():
    return lax.axis_index("x"), lax.axis_index("y"), lax.axis_index("c")


def _other_chips(x, y):
    return [(1 - x, y), (x, 1 - y), (1 - x, 1 - y)]


def _pow2_divisor(n):
    return n & (-n)


def _shard_view(ref, axis, size, d):
    start = d * size
    align = _pow2_divisor(size)
    if align > 1:
        start = pl.multiple_of(start, align)
    idx = [slice(None)] * len(ref.shape)
    idx[axis] = pl.ds(start, size)
    return ref.at[tuple(idx)]


class _Plan:
    def __init__(self, ins, out_shape, sems, start, finish, mid=None, aliases=None, peers="all"):
        self.ins, self.out_shape, self.sems = list(ins), list(out_shape), list(sems)
        self.start, self.mid, self.finish = start, mid, finish
        self.aliases = dict(aliases or {})
        self.peers = peers


def _gather_plan(shards, specs):
    n = len(shards)
    n_copy = 7

    def copies(ins, outs, sems):
        send_sems, recv_sems, local_sems = sems
        x, y, c = _place()
        me, sibling = (x, y, c), (x, y, 1 - c)
        chips = _other_chips(x, y)

        def block(a, p):
            _, axis, size = specs[a]
            return _shard_view(outs[a], axis, size, 4 * p[0] + 2 * p[1] + p[2])

        def copy(a, k, p, to, src=None):
            return pltpu.make_async_remote_copy(
                src_ref=block(a, p) if src is None else src,
                dst_ref=block(a, p),
                send_sem=send_sems.at[a * n_copy + k],
                recv_sem=recv_sems.at[a * n_copy + k],
                device_id=to,
                device_id_type=MESH,
            )

        def mine():
            return [pltpu.make_async_copy(ins[a], block(a, me), local_sems.at[a]) for a in range(n)]

        def first():
            return [cp for a in range(n) for cp in
                    [copy(a, 0, me, sibling, src=ins[a])] + [copy(a, 1 + j, me, (*chip, c), src=ins[a]) for j, chip in enumerate(chips)]]

        def landed():
            return [copy(a, 1 + j, (*chip, c), me) for j, chip in enumerate(chips) for a in range(n)]

        def passed():
            return [copy(a, 4 + j, (*chip, c), sibling) for j, chip in enumerate(chips) for a in range(n)]

        def last():
            return [cp for a in range(n) for cp in
                    [copy(a, 0, sibling, me)] + [copy(a, 4 + j, (*chip, 1 - c), me) for j, chip in enumerate(chips)]]

        return mine, first, landed, passed, last

    def start(ins, outs, sems):
        mine, first, _, _, _ = copies(ins, outs, sems)
        for cp in mine() + first():
            cp.start()

    def mid(ins, outs, sems):
        _, _, landed, passed, _ = copies(ins, outs, sems)
        for got, fwd in zip(landed(), passed()):
            got.wait_recv()
            fwd.start()

    def finish(ins, outs, sems):
        mine, first, _, passed, last = copies(ins, outs, sems)
        for cp in last():
            cp.wait_recv()
        for cp in first() + passed():
            cp.wait_send()
        for cp in mine():
            cp.wait()

    return _Plan(
        shards,
        [jax.ShapeDtypeStruct(full, s.dtype) for s, (full, _, _) in zip(shards, specs)],
        [pltpu.SemaphoreType.DMA((n * n_copy,)), pltpu.SemaphoreType.DMA((n * n_copy,)), pltpu.SemaphoreType.DMA((n,))],
        start, finish, mid,
    )


def _sibling_plan(grads, specs):
    n = len(grads)

    def shard_shape(a):
        axis, size = specs[a]
        shp = list(grads[a].shape)
        shp[axis] = size
        return tuple(shp)

    def copies(ins, outs, sems):
        send_sems, recv_sems = sems
        x, y, c = _place()
        return [
            pltpu.make_async_remote_copy(
                src_ref=_shard_view(ins[a], specs[a][0], specs[a][1], 2 * q + (1 - c)),
                dst_ref=outs[a].at[q],
                send_sem=send_sems.at[a * N_CHIP + q],
                recv_sem=recv_sems.at[a * N_CHIP + q],
                device_id=(x, y, 1 - c),
                device_id_type=MESH,
            )
            for a in range(n) for q in range(N_CHIP)
        ]

    def start(ins, outs, sems):
        for cp in copies(ins, outs, sems):
            cp.start()

    def finish(ins, outs, sems):
        for cp in copies(ins, outs, sems):
            cp.wait()

    return _Plan(
        grads,
        [jax.ShapeDtypeStruct((N_CHIP, *shard_shape(a)), grads[a].dtype) for a in range(n)],
        [pltpu.SemaphoreType.DMA((n * N_CHIP,)), pltpu.SemaphoreType.DMA((n * N_CHIP,))],
        start, finish, peers="sibling",
    )


def _chip_plan(parts, full_rows=None, row0=None, into=None):
    n = len(parts)
    full_rows = [p.shape[1] for p in parts] if full_rows is None else full_rows
    row0 = [0] * n if row0 is None else row0

    def copies(ins, outs, sems):
        send_sems, recv_sems, local_sems = sems
        x, y, c = _place()
        my_chip = 2 * x + y
        chips = _other_chips(x, y)

        def land(a, q):
            return outs[a].at[q, pl.ds(row0[a], parts[a].shape[1]), :]

        def mine():
            return [pltpu.make_async_copy(ins[a].at[my_chip], land(a, my_chip), local_sems.at[a]) for a in range(n)]

        def remote(sending):
            return [
                pltpu.make_async_remote_copy(
                    src_ref=ins[a].at[2 * chip[0] + chip[1]],
                    dst_ref=land(a, my_chip if sending else 2 * chip[0] + chip[1]),
                    send_sem=send_sems.at[a * 3 + j], recv_sem=recv_sems.at[a * 3 + j],
                    device_id=(*chip, c), device_id_type=MESH)
                for a in range(n) for j, chip in enumerate(chips)
            ]

        return mine, remote

    def start(ins, outs, sems):
        mine, remote = copies(ins, outs, sems)
        for cp in mine() + remote(True):
            cp.start()

    def finish(ins, outs, sems):
        mine, remote = copies(ins, outs, sems)
        for cp in remote(False):
            cp.wait_recv()
        for cp in remote(True):
            cp.wait_send()
        for cp in mine():
            cp.wait()

    return _Plan(
        list(parts) + (list(into) if into is not None else []),
        [jax.ShapeDtypeStruct((N_CHIP, full_rows[a], parts[a].shape[2]), parts[a].dtype) for a in range(n)],
        [pltpu.SemaphoreType.DMA((n * 3,)), pltpu.SemaphoreType.DMA((n * 3,)), pltpu.SemaphoreType.DMA((n,))],
        start, finish,
        aliases={n + a: a for a in range(n)} if into is not None else None, peers="chips",
    )


def _call(body, *, name, operands, out_shape, grid=(), in_specs=None, out_specs=None, scratch_shapes=(),
          plans=(), aliases=None, prefetch=(), start_after_body=False, pass_on_at=0.6, body_peers=None):
    operands, out_shape, scratch_shapes = list(operands), list(out_shape), list(scratch_shapes)
    n_pf, n_in, n_out, n_scr = len(prefetch), len(operands), len(out_shape), len(scratch_shapes)
    in_vmem = pl.BlockSpec(memory_space=pltpu.VMEM)
    in_hbm = pl.BlockSpec(memory_space=pl.ANY)
    in_specs = [in_vmem] * n_in if in_specs is None else list(in_specs)
    out_specs = [in_vmem] * n_out if out_specs is None else list(out_specs)
    io_alias = {n_pf + i: o for i, o in (aliases or {}).items()}
    pos_in, pos_out = n_pf + n_in, n_out
    for p in plans:
        for i, o in p.aliases.items():
            io_alias[pos_in + i] = pos_out + o
        pos_in += len(p.ins)
        pos_out += len(p.out_shape)
    steps = math.prod(grid) if grid else 1
    mid_step = min(steps - 1, int(steps * pass_on_at))
    patterns = {p.peers for p in plans} | ({body_peers} if body_peers else set())
    pattern = None if not patterns else patterns.pop() if len(patterns) == 1 else "all"

    def shake_hands():
        x, y, c = _place()
        peers = [(x, y, 1 - c)] if pattern in ("sibling", "all") else []
        peers += [(*chip, c) for chip in _other_chips(x, y)] if pattern in ("chips", "all") else []
        barrier = pltpu.get_barrier_semaphore()
        for peer in peers:
            pl.semaphore_signal(barrier, inc=1, device_id=peer, device_id_type=MESH)
        pl.semaphore_wait(barrier, len(peers))

    def wrapped(*refs):
        refs = list(refs)
        take = lambda k: [refs.pop(0) for _ in range(k)]
        pf = take(n_pf)
        ins, p_ins = take(n_in), [take(len(p.ins)) for p in plans]
        outs, p_outs = take(n_out), [take(len(p.out_shape)) for p in plans]
        scr, p_sems = take(n_scr), [take(len(p.sems)) for p in plans]
        step = 0
        for ax, g in enumerate(grid):
            step = step * g + pl.program_id(ax)

        def hook(kind, at):
            todo = [(getattr(p, kind), a, b, c) for p, a, b, c in zip(plans, p_ins, p_outs, p_sems) if getattr(p, kind)]

            def run():
                for fn, a, b, c in todo:
                    fn(a, b, c)

            if todo and grid:
                pl.when(step == at)(run)
            elif todo:
                run()

        if pattern and grid:
            pl.when(step == 0)(shake_hands)
        elif pattern:
            shake_hands()
        if not start_after_body:
            hook("start", 0)
        if body is not None:
            body(*pf, *ins, *outs, *scr)
        if start_after_body:
            hook("start", 0)
        hook("mid", mid_step)
        hook("finish", steps - 1)

    layout = dict(
        grid=grid,
        in_specs=in_specs + [in_hbm] * sum(len(p.ins) for p in plans),
        out_specs=out_specs + [in_hbm] * sum(len(p.out_shape) for p in plans),
        scratch_shapes=scratch_shapes + [s for p in plans for s in p.sems],
    )
    if n_pf:
        layout = dict(grid_spec=pltpu.PrefetchScalarGridSpec(num_scalar_prefetch=n_pf, **layout))
    res = pl.pallas_call(
        wrapped,
        name=name,
        out_shape=out_shape + [o for p in plans for o in p.out_shape],
        input_output_aliases=io_alias,
        compiler_params=pltpu.CompilerParams(
            dimension_semantics=("arbitrary",) * len(grid) if grid else None,
            collective_id=BARRIER_ID[pattern] if pattern else None),
        **layout,
    )(*prefetch, *operands, *[a for p in plans for a in p.ins])
    res = list(res)
    outs, plan_outs = res[:n_out], []
    pos = n_out
    for p in plans:
        plan_outs.append(res[pos : pos + len(p.out_shape)])
        pos += len(p.out_shape)
    return outs, plan_outs


def _inproj(x, g, w_shard, order, plans=()):
    s = x.shape[0]
    tm = min(TM_INPROJ, s)
    nt = s // tm
    wide = 2 * SHARD_IN
    fwd_at = nt // 2

    def body(order_ref, x_ref, g_ref, w_hbm, proj_ref, h1_ref, h1t_ref, wfull_hbm,
             w_scr, h1_scr, send_sems, recv_sems, local_sems):
        k, i = pl.program_id(0), pl.program_id(1)
        px, py, c = _place()
        me, sibling = (px, py, c), (px, py, 1 - c)
        chips = _other_chips(px, py)

        def half(slot, core):
            return w_scr.at[slot, :, pl.ds(pl.multiple_of(core * SHARD_IN, 128), SHARD_IN)]

        def copy(n, slot, core, to, src=None):
            return pltpu.make_async_remote_copy(
                src_ref=half(slot, core) if src is None else src, dst_ref=half(slot, core),
                send_sem=send_sems.at[n], recv_sem=recv_sems.at[n], device_id=to, device_id_type=MESH)

        def mine():
            return pltpu.make_async_copy(w_hbm, half(0, c), local_sems.at[0])

        def first():
            return [copy(0, 0, c, sibling, src=w_hbm)] + [copy(1 + j, 1 + j, c, (*chips[j], c), src=w_hbm) for j in range(2)]

        def relay():
            to = (c * px + (1 - c) * (1 - px), c * (1 - py) + (1 - c) * py, c)
            return pltpu.make_async_remote_copy(
                src_ref=half(2 - c, c), dst_ref=half(3, c), send_sem=send_sems.at[3], recv_sem=recv_sems.at[3],
                device_id=to, device_id_type=MESH)

        def save(slot):
            cols = pl.ds(pl.multiple_of(order_ref[slot] * wide, 128), wide)
            return pltpu.make_async_copy(w_scr.at[slot], wfull_hbm.at[:, cols], local_sems.at[1 + slot])

        @pl.when((k == 0) & (i == 0))
        def _():
            mine().start()
            for cp in first():
                cp.start()
            mine().wait()
            copy(0, 0, 1 - c, me).wait_recv()

        for j in range(3):
            @pl.when((k == j + 1) & (i == 0))
            def _():
                copy(4 + j, 1 + j, 1 - c, me).wait_recv()

        @pl.when((k == 3) & (i == 0))
        def _():
            for slot in range(N_CHIP):
                save(slot).start()

        @pl.when(k == 0)
        def _():
            xh, _ = _rms(x_ref[...])
            h = xh * g_ref[...]
            h1_ref[...] = h.astype(BF16)
            h1t_ref[...] = h.T.astype(BF16)
            h1_scr[pl.ds(pl.multiple_of(i * tm, tm), tm), :] = h.astype(BF16)

        proj_ref[...] = _dot(h1_scr[pl.ds(pl.multiple_of(i * tm, tm), tm), :], w_scr[k]).astype(BF16)

        for j in range(3):
            @pl.when((k == 0) & (i == nt - 1) if j < 2 else (k == 2) & (i == fwd_at))
            def _():
                copy(1 + j, 1 + j, c, me).wait_recv()
                copy(4 + j, 1 + j, c, sibling).start()
                if j < 2:
                    pl.when(c == 1 - j)(lambda: relay().start())

        @pl.when((k == 3) & (i == nt - 1))
        def _():
            for cp in first() + [relay()] + [copy(4 + j, 1 + j, c, sibling) for j in range(3)]:
                cp.wait_send()
            for slot in range(N_CHIP):
                save(slot).wait()

    once = lambda k, i, order_ref: (jnp.where(k == 0, i, nt - 1), 0)
    return _call(
        body,
        name="inproj",
        prefetch=[order],
        operands=[x, g, w_shard],
        grid=(N_CHIP, nt),
        in_specs=[
            pl.BlockSpec((tm, D_MODEL), once),
            pl.BlockSpec((1, D_MODEL), lambda k, i, order_ref: (0, 0)),
            pl.BlockSpec(memory_space=pl.ANY),
        ],
        out_specs=[
            pl.BlockSpec((tm, wide), lambda k, i, order_ref: (i, order_ref[k])),
            pl.BlockSpec((tm, D_MODEL), once),
            pl.BlockSpec((D_MODEL, tm), lambda k, i, order_ref: (0, jnp.where(k == 0, i, nt - 1))),
            pl.BlockSpec(memory_space=pl.ANY),
        ],
        out_shape=[
            jax.ShapeDtypeStruct((s, IN_DIM), BF16),
            jax.ShapeDtypeStruct((s, D_MODEL), BF16),
            jax.ShapeDtypeStruct((D_MODEL, s), BF16),
            jax.ShapeDtypeStruct((D_MODEL, IN_DIM), BF16),
        ],
        scratch_shapes=[
            pltpu.VMEM((N_CHIP, D_MODEL, wide), BF16),
            pltpu.VMEM((s, D_MODEL), BF16),
            pltpu.SemaphoreType.DMA((7,)),
            pltpu.SemaphoreType.DMA((7,)),
            pltpu.SemaphoreType.DMA((1 + N_CHIP,)),
        ],
        plans=plans,
        start_after_body=True,
        pass_on_at=0.8,
        body_peers="all",
    )


def _rows_from_above(v, above):
    row = lax.broadcasted_iota(jnp.int32, (8, v.shape[1]), 0)
    r1, r2 = pltpu.roll(v, 1, 0), pltpu.roll(v, 2, 0)
    top1 = jnp.where(row == 0, above[7:8], r1[0:8])
    top2 = jnp.where(row == 0, above[6:7], jnp.where(row == 1, above[7:8], r2[0:8]))
    return jnp.concatenate([top1, r1[8:]], axis=0), jnp.concatenate([top2, r2[8:]], axis=0)


def _rows_from_below(v, below):
    n = v.shape[0]
    row = lax.broadcasted_iota(jnp.int32, (8, v.shape[1]), 0)
    r1, r2 = pltpu.roll(v, n - 1, 0), pltpu.roll(v, n - 2, 0)
    end1 = jnp.where(row == 7, below[0:1], r1[n - 8 :])
    end2 = jnp.where(row == 6, below[0:1], jnp.where(row == 7, below[1:2], r2[n - 8 :]))
    return jnp.concatenate([r1[: n - 8], end1], axis=0), jnp.concatenate([r2[: n - 8], end2], axis=0)


def _causal_mask():
    row = lax.broadcasted_iota(jnp.int32, (CHUNK, CHUNK), 0)
    col = lax.broadcasted_iota(jnp.int32, (CHUNK, CHUNK), 1)
    return row, row >= col


def _mixer_fwd(proj, conv_w, ln_g, ln_b, ws, bs_b, plans=()):
    s = proj.shape[0]
    nt = s // CHUNK

    def body(proj_ref, cw_ref, lng_ref, lnb_ref, ws_ref, bsb_ref, cat_ref, pcar_ref):
        @pl.when(pl.program_id(0) == 0)
        def _():
            pcar_ref[...] = jnp.zeros_like(pcar_ref)

        row, tril = _causal_mask()
        for j in range(8):
            cs = slice(CHUNK * j, CHUNK * (j + 1))

            def grp(k):
                return proj_ref[:, k * D_MODEL + CHUNK * j : k * D_MODEL + CHUNK * (j + 1)].astype(F32)

            w = cw_ref[:, cs]
            p = grp(1) * grp(2)
            p1, p2 = _rows_from_above(p, pcar_ref[:, cs])
            pcar_ref[:, cs] = p[CHUNK - 8 :]
            cv = w[0:1] * p2 + w[1:2] * p1 + w[2:3] * p
            sa = _silu(grp(3), with_grad=False)
            cat_ref[:, cs] = ((grp(0) * cv) * sa).astype(BF16)
            gu = _gelu(grp(4), with_grad=False)
            gv = _gelu(grp(5), with_grad=False)
            dv = gv - jnp.mean(gv, axis=-1, keepdims=True)
            vn = dv * lax.rsqrt(jnp.mean(dv * dv, axis=-1, keepdims=True) + EPS)
            vn = vn * lng_ref[:, cs] + lnb_ref[:, cs]
            wc = jnp.where(tril, ws_ref[j], 0.0).astype(BF16)
            sp = _dot(wc, vn.astype(BF16)) + bsb_ref[j]
            sb = _silu(grp(6), with_grad=False)
            cat_ref[:, D_MODEL + CHUNK * j : D_MODEL + CHUNK * (j + 1)] = ((gu * sp) * sb).astype(BF16)

    whole = lambda shape: pl.BlockSpec(shape, lambda i: (0,) * len(shape))
    (cat,), plan_outs = _call(
        body,
        name="mixer_fwd",
        operands=[proj, conv_w, ln_g, ln_b, ws, bs_b],
        grid=(nt,),
        in_specs=[
            pl.BlockSpec((CHUNK, IN_DIM), lambda i: (i, 0)),
            whole((8, D_MODEL)),
            whole((1, D_MODEL)),
            whole((1, D_MODEL)),
            whole((8, CHUNK, CHUNK)),
            whole((8, CHUNK, CHUNK)),
        ],
        out_specs=[pl.BlockSpec((CHUNK, MIX_DIM), lambda i: (i, 0))],
        out_shape=[jax.ShapeDtypeStruct((s, MIX_DIM), BF16)],
        scratch_shapes=[pltpu.VMEM((8, D_MODEL), F32)],
        plans=plans,
        pass_on_at=0.85,
    )
    return cat, plan_outs


def _mixer_bwd(proj, dcat, conv_w, ln_g, ln_b, ws, bs_b, plans=()):
    s = proj.shape[0]
    nt = s // CHUNK

    def body(proj_ref, halo_ref, dcat_ref, cw_ref, lng_ref, lnb_ref, ws_ref, bsb_ref,
             dproj_ref, dcw_ref, dlng_ref, dlnb_ref, dws_ref, dbs_ref,
             car_ref, acc_cw, acc_lng, acc_lnb, acc_bs, wc_scr, wct_scr):
        i = pl.program_id(0)
        tile = nt - 1 - i
        row, tril = _causal_mask()

        @pl.when(i == 0)
        def _():
            for j in range(8):
                wc = jnp.where(tril, ws_ref[j], 0.0)
                wc_scr[j] = wc.astype(BF16)
                wct_scr[j] = wc.T.astype(BF16)
            car_ref[...] = jnp.zeros_like(car_ref)
            acc_cw[...] = jnp.zeros_like(acc_cw)
            acc_lng[...] = jnp.zeros_like(acc_lng)
            acc_lnb[...] = jnp.zeros_like(acc_lnb)
            acc_bs[...] = jnp.zeros_like(acc_bs)
            dws_ref[...] = jnp.zeros_like(dws_ref)

        has_prev = jnp.where(tile > 0, 1.0, 0.0).astype(F32)
        for j in range(8):
            cs = slice(CHUNK * j, CHUNK * (j + 1))

            def col(k):
                return slice(k * D_MODEL + CHUNK * j, k * D_MODEL + CHUNK * (j + 1))

            def grp(k):
                return proj_ref[:, col(k)].astype(F32)

            w = cw_ref[:, cs]
            gb, gc, xa = grp(0), grp(1), grp(2)
            p = gc * xa
            pprev = halo_ref[:, col(1)].astype(F32) * halo_ref[:, col(2)].astype(F32) * has_prev
            p1, p2 = _rows_from_above(p, pprev[8:16])
            cv = w[0:1] * p2 + w[1:2] * p1 + w[2:3] * p
            za = grp(3)
            sa, dsa = _silu(za)
            da = dcat_ref[:, cs].astype(F32)
            da_gb = da * gb
            dproj_ref[:, col(0)] = ((da * sa) * cv).astype(BF16)
            dproj_ref[:, col(3)] = ((da_gb * cv) * dsa).astype(BF16)
            dcv = da_gb * sa
            d1, d2 = _rows_from_below(dcv, car_ref[:, cs])
            car_ref[:, cs] = dcv[0:8]
            dp = w[2:3] * dcv + w[1:2] * d1 + w[0:1] * d2
            dproj_ref[:, col(1)] = (dp * xa).astype(BF16)
            dproj_ref[:, col(2)] = (dp * gc).astype(BF16)
            acc_cw[0, :, cs] += _rows8(dcv * p2)
            acc_cw[1, :, cs] += _rows8(dcv * p1)
            acc_cw[2, :, cs] += _rows8(dcv * p)
            gu, dgu = _gelu(grp(4))
            gv, dgv = _gelu(grp(5))
            dv = gv - jnp.mean(gv, axis=-1, keepdims=True)
            rstd = lax.rsqrt(jnp.mean(dv * dv, axis=-1, keepdims=True) + EPS)
            vnh = dv * rstd
            lng = lng_ref[:, cs]
            vnb = (vnh * lng + lnb_ref[:, cs]).astype(BF16)
            sp = _dot(wc_scr[j], vnb) + bsb_ref[j]
            zb = grp(6)
            sb, dsb = _silu(zb)
            db = dcat_ref[:, D_MODEL + CHUNK * j : D_MODEL + CHUNK * (j + 1)].astype(F32)
            db_sp, db_gu = db * sp, db * gu
            dproj_ref[:, col(4)] = ((db_sp * sb) * dgu).astype(BF16)
            dproj_ref[:, col(6)] = ((db_sp * gu) * dsb).astype(BF16)
            dsp = db_gu * sb
            acc_bs[j] += dsp
            dspb = dsp.astype(BF16)
            dws_ref[j] += _dot_nt(dspb, vnb)
            dvn = _dot(wct_scr[j], dspb)
            acc_lnb[:, cs] += _rows8(dvn)
            acc_lng[:, cs] += _rows8(dvn * vnh)
            dvh = dvn * lng
            dgvv = rstd * (dvh - jnp.mean(dvh, axis=-1, keepdims=True) - vnh * jnp.mean(dvh * vnh, axis=-1, keepdims=True))
            dproj_ref[:, col(5)] = (dgvv * dgv).astype(BF16)

        @pl.when(i == nt - 1)
        def _():
            dlng_ref[...] = jnp.sum(acc_lng[...], axis=0, keepdims=True)
            dlnb_ref[...] = jnp.sum(acc_lnb[...], axis=0, keepdims=True)
            dcw_ref[...] = jnp.zeros_like(dcw_ref)
            for k in range(3):
                dcw_ref[k : k + 1, :] = jnp.sum(acc_cw[k], axis=0, keepdims=True)
            for j in range(8):
                dws_ref[j] = jnp.where(tril, dws_ref[j], 0.0)
                dbs_ref[:, CHUNK * j : CHUNK * (j + 1)] = jnp.sum(acc_bs[j].T, axis=0, keepdims=True)

    whole = lambda shape: pl.BlockSpec(shape, lambda i: (0,) * len(shape))
    halo_rows = 16
    per = CHUNK // halo_rows
    return _call(
        body,
        name="mixer_bwd",
        operands=[proj, proj, dcat, conv_w, ln_g, ln_b, ws, bs_b],
        plans=plans,
        grid=(nt,),
        in_specs=[
            pl.BlockSpec((CHUNK, IN_DIM), lambda i: (nt - 1 - i, 0)),
            pl.BlockSpec((halo_rows, IN_DIM), lambda i: (jnp.maximum((nt - 1 - i) * per - 1, 0), 0)),
            pl.BlockSpec((CHUNK, MIX_DIM), lambda i: (nt - 1 - i, 0)),
            whole((8, D_MODEL)),
            whole((1, D_MODEL)),
            whole((1, D_MODEL)),
            whole((8, CHUNK, CHUNK)),
            whole((8, CHUNK, CHUNK)),
        ],
        out_specs=[
            pl.BlockSpec((CHUNK, IN_DIM), lambda i: (nt - 1 - i, 0)),
            whole((8, D_MODEL)),
            whole((1, D_MODEL)),
            whole((1, D_MODEL)),
            whole((8, CHUNK, CHUNK)),
            whole((1, D_MODEL)),
        ],
        out_shape=[
            jax.ShapeDtypeStruct((s, IN_DIM), BF16),
            jax.ShapeDtypeStruct((8, D_MODEL), F32),
            jax.ShapeDtypeStruct((1, D_MODEL), F32),
            jax.ShapeDtypeStruct((1, D_MODEL), F32),
            jax.ShapeDtypeStruct((8, CHUNK, CHUNK), F32),
            jax.ShapeDtypeStruct((1, D_MODEL), F32),
        ],
        scratch_shapes=[
            pltpu.VMEM((8, D_MODEL), F32),
            pltpu.VMEM((3, 8, D_MODEL), F32),
            pltpu.VMEM((8, D_MODEL), F32),
            pltpu.VMEM((8, D_MODEL), F32),
            pltpu.VMEM((8, CHUNK, CHUNK), F32),
            pltpu.VMEM((8, CHUNK, CHUNK), BF16),
            pltpu.VMEM((8, CHUNK, CHUNK), BF16),
        ],
    )


def _kv_fwd(mem, g_mem, w_kv):
    def body(mem_ref, g_ref, w_ref, k_ref, vv_ref, mt_ref):
        mh, _ = _rms(mem_ref[...])
        m = mh * g_ref[...]
        kv = _dot(m.astype(BF16), w_ref[...])
        k_ref[...] = kv[:, :D_MODEL].astype(BF16)
        vv_ref[...] = kv[:, D_MODEL:].astype(BF16)
        mt_ref[...] = m.T.astype(BF16)

    return pl.pallas_call(
        body,
        name="kv_fwd",
        out_shape=[
            jax.ShapeDtypeStruct((MEM_LEN, D_MODEL), BF16),
            jax.ShapeDtypeStruct((MEM_LEN, D_MODEL), BF16),
            jax.ShapeDtypeStruct((D_MODEL, MEM_LEN), BF16),
        ],
    )(mem, g_mem, w_kv)


def _kv_bwd(mem, w_kv, mt, dk, dvv):
    def body(mem_ref, w_ref, mt_ref, dk_ref, dvv_ref, dw_ref, dwb_ref, dg_ref):
        dkv = jnp.concatenate([dk_ref[...], dvv_ref[...]], axis=1).astype(BF16)
        dw = _dot(mt_ref[...], dkv)
        dw_ref[...] = dw
        dwb_ref[...] = dw.astype(BF16)
        dm = _dot_nt(dkv, w_ref[...])
        mh, _ = _rms(mem_ref[...])
        dg_ref[...] = jnp.sum(dm * mh, axis=0, keepdims=True)

    return pl.pallas_call(
        body,
        name="kv_bwd",
        out_shape=[
            jax.ShapeDtypeStruct((D_MODEL, 2 * D_MODEL), F32),
            jax.ShapeDtypeStruct((D_MODEL, 2 * D_MODEL), BF16),
            jax.ShapeDtypeStruct((1, D_MODEL), F32),
        ],
    )(mem, w_kv, mt, dk, dvv)


def _attn(x, cat, target, w_out, w_q, w_xo, k, vv, g_x, g_f):
    s = x.shape[0]
    tm = min(TM_ATTN, s)
    sub = min(SUB_ATTN, tm)
    nt = s // tm
    scale = 1.0 / math.sqrt(X_HEAD_DIM)

    def body(x_ref, cat_ref, t_ref, wout_ref, wq_ref, wxo_ref, k_ref, vv_ref, gx_ref, gf_ref,
             dcat_ref, actt_ref, cot_ref,
             dk_ref, dvv_ref, dgf_ref, dgx_ref, loss_ref,
             p_scr, q_scr, o_scr, dq_scr, acc_gf, acc_gx, acc_loss):
        i = pl.program_id(0)

        @pl.when(i == 0)
        def _():
            dk_ref[...] = jnp.zeros_like(dk_ref)
            dvv_ref[...] = jnp.zeros_like(dvv_ref)
            acc_gf[...] = jnp.zeros_like(acc_gf)
            acc_gx[...] = jnp.zeros_like(acc_gx)
            acc_loss[...] = jnp.zeros_like(acc_loss)

        subs = [slice(u * sub, (u + 1) * sub) for u in range(tm // sub)]
        heads = [slice(X_HEAD_DIM * h, X_HEAD_DIM * (h + 1)) for h in range(X_HEADS)]
        gx, gf = gx_ref[...], gf_ref[...]
        x1, x1h, r2 = [], [], []
        for rs in subs:
            cat = cat_ref[rs, :]
            actt_ref[0:MIX_DIM, rs] = cat.astype(F32).T.astype(BF16)
            v = x_ref[rs, :] + _dot(cat, wout_ref[...])
            vh, r = _rms(v)
            h2 = vh * gx
            actt_ref[MIX_DIM : MIX_DIM + D_MODEL, rs] = h2.T.astype(BF16)
            q_scr[rs, :] = _dot(h2.astype(BF16), wq_ref[...]).astype(BF16)
            x1.append(v), x1h.append(vh), r2.append(r)
        for rs in subs:
            for hs in heads:
                sc = _dot_nt(q_scr[rs, hs], k_ref[:, hs]) * scale
                e = jnp.exp(sc - jnp.max(sc, axis=-1, keepdims=True))
                p = e / jnp.sum(e, axis=-1, keepdims=True)
                p_scr[rs, hs] = p
                o_scr[rs, hs] = _dot(p.astype(BF16), vv_ref[:, hs])
        dx2, sq, d_gf = [], None, None
        for u, rs in enumerate(subs):
            o = o_scr[rs, :]
            actt_ref[MIX_DIM + D_MODEL :, rs] = o.T.astype(BF16)
            x2 = x1[u] + _dot(o.astype(BF16), wxo_ref[...])
            x2h, r3 = _rms(x2)
            err = x2h * gf - t_ref[rs, :]
            dy = err * (1.0 / D_MODEL)
            sq = _rows8(err * err) if sq is None else sq + _rows8(err * err)
            d_gf = _rows8(dy * x2h) if d_gf is None else d_gf + _rows8(dy * x2h)
            dx2.append(_rms_bwd(dy * gf, x2h, r3))
        acc_loss[...] += sq
        acc_gf[...] += d_gf
        d_vv, d_k = [None] * X_HEADS, [None] * X_HEADS
        for u, rs in enumerate(subs):
            dx2b = dx2[u].astype(BF16)
            cot_ref[2, rs, :] = dx2b
            dob = _dot_nt(dx2b, wxo_ref[...]).astype(BF16)
            for h, hs in enumerate(heads):
                p = p_scr[rs, hs]
                dp = _dot_nt(dob[:, hs], vv_ref[:, hs])
                ds = p * (dp - jnp.sum(dp * p, axis=-1, keepdims=True)) * scale
                dq_scr[rs, hs] = _dot(ds.astype(BF16), k_ref[:, hs]).astype(BF16)
                dvv_h = _dot(p.T.astype(BF16), dob[:, hs])
                dk_h = _dot(ds.T.astype(BF16), q_scr[rs, hs])
                d_vv[h] = dvv_h if d_vv[h] is None else d_vv[h] + dvv_h
                d_k[h] = dk_h if d_k[h] is None else d_k[h] + dk_h
        for h, hs in enumerate(heads):
            dvv_ref[:, hs] += d_vv[h]
            dk_ref[:, hs] += d_k[h]
        d_gx = None
        for u, rs in enumerate(subs):
            dqb = dq_scr[rs, :]
            cot_ref[1, rs, :] = dqb
            dh2 = _dot_nt(dqb, wq_ref[...])
            d_gx = _rows8(dh2 * x1h[u]) if d_gx is None else d_gx + _rows8(dh2 * x1h[u])
            dx1 = dx2[u] + _rms_bwd(dh2 * gx, x1h[u], r2[u])
            dx1b = dx1.astype(BF16)
            cot_ref[0, rs, :] = dx1b
            dcat_ref[rs, :] = _dot_nt(dx1b, wout_ref[...]).astype(BF16)
        acc_gx[...] += d_gx

        @pl.when(i == nt - 1)
        def _():
            dgf_ref[...] = jnp.sum(acc_gf[...], axis=0, keepdims=True)
            dgx_ref[...] = jnp.sum(acc_gx[...], axis=0, keepdims=True)
            loss_ref[...] = jnp.sum(acc_loss[...], axis=0, keepdims=True)

    whole = lambda shape: pl.BlockSpec(shape, lambda i: (0,) * len(shape))
    once = lambda shape: pl.BlockSpec(shape, lambda i: (0,) * len(shape), pipeline_mode=pl.Buffered(1))
    rows = lambda width: pl.BlockSpec((tm, width), lambda i: (i, 0))
    cols = lambda height: pl.BlockSpec((height, tm), lambda i: (0, i))
    vec = jax.ShapeDtypeStruct((1, D_MODEL), F32)
    return pl.pallas_call(
        body,
        name="attn",
        grid=(nt,),
        in_specs=[
            rows(D_MODEL), rows(MIX_DIM), rows(D_MODEL),
            once((MIX_DIM, D_MODEL)), once((D_MODEL, D_MODEL)), once((D_MODEL, D_MODEL)),
            once((MEM_LEN, D_MODEL)), once((MEM_LEN, D_MODEL)),
            whole((1, D_MODEL)), whole((1, D_MODEL)),
        ],
        out_specs=[
            rows(MIX_DIM), cols(2 * MIX_DIM),
            pl.BlockSpec((3, tm, D_MODEL), lambda i: (0, i, 0)),
            whole((MEM_LEN, D_MODEL)), whole((MEM_LEN, D_MODEL)),
            whole((1, D_MODEL)), whole((1, D_MODEL)), whole((1, D_MODEL)),
        ],
        out_shape=[
            jax.ShapeDtypeStruct((s, MIX_DIM), BF16),
            jax.ShapeDtypeStruct((2 * MIX_DIM, s), BF16),
            jax.ShapeDtypeStruct((3, s, D_MODEL), BF16),
            jax.ShapeDtypeStruct((MEM_LEN, D_MODEL), F32),
            jax.ShapeDtypeStruct((MEM_LEN, D_MODEL), F32),
            vec, vec, vec,
        ],
        scratch_shapes=[
            pltpu.VMEM((tm, D_MODEL), F32),
            pltpu.VMEM((tm, D_MODEL), BF16),
            pltpu.VMEM((tm, D_MODEL), F32),
            pltpu.VMEM((tm, D_MODEL), BF16),
            pltpu.VMEM((8, D_MODEL), F32),
            pltpu.VMEM((8, D_MODEL), F32),
            pltpu.VMEM((8, D_MODEL), F32),
        ],
        compiler_params=pltpu.CompilerParams(dimension_semantics=("arbitrary",)),
    )(x, cat, target, w_out, w_q, w_xo, k, vv, g_x, g_f)


def _matmul_f32(name, a, b, rows=None, plans=()):
    first, m = (0, a.shape[0]) if rows is None else rows
    kk = a.shape[1]
    n = b.shape[1]
    tm, tn, tk = min(m, TM_DW), min(n, D_MODEL), min(kk, TK_DW)
    nk = kk // tk
    i0 = first // tm

    def body(a_ref, b_ref, o_ref, ob_ref):
        prod = _dot(a_ref[...], b_ref[...])
        if nk == 1:
            o_ref[...] = prod
            ob_ref[...] = prod.astype(BF16)
            return
        kid = pl.program_id(2)

        @pl.when(kid == 0)
        def _():
            o_ref[...] = prod

        @pl.when(kid > 0)
        def _():
            o_ref[...] += prod

        @pl.when(kid == nk - 1)
        def _():
            ob_ref[...] = o_ref[...].astype(BF16)

    out_spec = pl.BlockSpec((tm, tn), lambda i, j, k: (i, j))
    outs, plan_outs = _call(
        body,
        name=name,
        operands=[a, b],
        grid=(m // tm, n // tn, nk),
        in_specs=[pl.BlockSpec((tm, tk), lambda i, j, k: (i + i0, k)), pl.BlockSpec((tk, tn), lambda i, j, k: (k, j))],
        out_specs=[out_spec, out_spec],
        out_shape=[jax.ShapeDtypeStruct((m, n), F32), jax.ShapeDtypeStruct((m, n), BF16)],
        plans=plans,
    )
    return outs, plan_outs


def _dw_attn(actt, cot, plans=()):
    s = actt.shape[1]
    tm = TM_DW
    counts = [MIX_DIM // tm, D_MODEL // tm, D_MODEL // tm]
    starts = [0, counts[0], counts[0] + counts[1]]
    which = lambda i: jnp.where(i < starts[1], 0, jnp.where(i < starts[2], 1, 2))

    def body(a_ref, b_ref, *outs):
        i = pl.program_id(0)
        prod = _dot(a_ref[...], b_ref[0])
        for m in range(3):
            @pl.when(which(i) == m)
            def _():
                outs[2 * m][...] = prod
                outs[2 * m + 1][...] = prod.astype(BF16)

    out_specs, out_shape = [], []
    for m in range(3):
        spec = pl.BlockSpec((tm, D_MODEL), lambda i, m=m: (jnp.clip(i - starts[m], 0, counts[m] - 1), 0))
        out_specs += [spec, spec]
        out_shape += [jax.ShapeDtypeStruct((counts[m] * tm, D_MODEL), F32), jax.ShapeDtypeStruct((counts[m] * tm, D_MODEL), BF16)]
    outs, plan_outs = _call(
        body,
        name="dw_attn",
        operands=[actt, cot],
        grid=(sum(counts),),
        in_specs=[pl.BlockSpec((tm, s), lambda i: (i, 0)), pl.BlockSpec((1, s, D_MODEL), lambda i: (which(i), 0, 0))],
        out_specs=out_specs,
        out_shape=out_shape,
        plans=plans,
    )
    return [outs[2 * m : 2 * m + 2] for m in range(3)], plan_outs


def _dh1(name, dproj, w_in, x, dx1, g, tiles, into=None, after=(), plans=()):
    s = x.shape[0]
    tm = min(TM_DH1, s)
    t0, nt = tiles[0], tiles[1] - tiles[0]

    def body(dp_ref, w_ref, x_ref, dx1_ref, g_ref, *rest):
        gx_ref, dg_ref, acc_g = rest[-3:]
        i = pl.program_id(0)

        @pl.when(i == 0)
        def _():
            acc_g[...] = jnp.zeros_like(acc_g)

        dh1 = _dot_nt(dp_ref[...], w_ref[...])
        xh, r = _rms(x_ref[...])
        acc_g[...] += _rows8(dh1 * xh)
        gx_ref[...] = dx1_ref[0].astype(F32) + _rms_bwd(dh1 * g_ref[...], xh, r)

        @pl.when(i == nt - 1)
        def _():
            dg_ref[...] = jnp.sum(acc_g[...], axis=0, keepdims=True)

    rows = pl.BlockSpec((tm, D_MODEL), lambda i: (i + t0, 0))
    return _call(
        body,
        name=name,
        operands=[dproj, w_in, x, dx1, g] + ([into] if into is not None else []) + list(after),
        grid=(nt,),
        in_specs=[
            pl.BlockSpec((tm, IN_DIM), lambda i: (i + t0, 0)),
            pl.BlockSpec((D_MODEL, IN_DIM), lambda i: (0, 0), pipeline_mode=pl.Buffered(1)),
            rows,
            pl.BlockSpec((1, tm, D_MODEL), lambda i: (0, i + t0, 0)),
            pl.BlockSpec((1, D_MODEL), lambda i: (0, 0)),
        ] + [pl.BlockSpec(memory_space=pl.ANY)] * ((into is not None) + len(after)),
        out_specs=[rows, pl.BlockSpec((1, D_MODEL), lambda i: (0, 0))],
        out_shape=[jax.ShapeDtypeStruct((s, D_MODEL), F32), jax.ShapeDtypeStruct((1, D_MODEL), F32)],
        scratch_shapes=[pltpu.VMEM((8, D_MODEL), F32)],
        plans=plans,
        aliases={5: 0} if into is not None else None,
    )


def _pair_sum(name, grad, got, axis, size, core, plans=()):
    shard = list(grad.shape)
    shard[axis] = size
    r, cdim = shard
    tr = min(r, TR_PAIR_SUM)
    nr = r // tr

    def body(core_ref, g_ref, got_ref, o_ref):
        o_ref[0] = (g_ref[...] + got_ref[0].astype(F32)).astype(BF16)

    if axis == 1:
        g_map = lambda q, t, core_ref: (t, 2 * q + core_ref[0])
    else:
        g_map = lambda q, t, core_ref: ((2 * q + core_ref[0]) * nr + t, 0)
    (out,), plan_outs = _call(
        body,
        name=name,
        prefetch=[core],
        operands=[grad, got],
        grid=(N_CHIP, nr),
        in_specs=[
            pl.BlockSpec((tr, cdim), g_map),
            pl.BlockSpec((1, tr, cdim), lambda q, t, core_ref: (q, t, 0)),
        ],
        out_specs=[pl.BlockSpec((1, tr, cdim), lambda q, t, core_ref: (q, t, 0))],
        out_shape=[jax.ShapeDtypeStruct((N_CHIP, r, cdim), BF16)],
        plans=plans,
    )
    return (out, plan_outs) if plans else out


def _adam_step(w, m, v, g):
    c1 = 1.0 / (1.0 - ADAM_B1**ADAM_STEP)
    c2 = 1.0 / (1.0 - ADAM_B2**ADAM_STEP)
    m_new = ADAM_B1 * m + (1.0 - ADAM_B1) * g
    v_new = ADAM_B2 * v + (1.0 - ADAM_B2) * (g * g)
    return -ADAM_LR * ((m_new * c1) / (jnp.sqrt(v_new * c2) + ADAM_EPS) + ADAM_WD * w), m_new, v_new


def _update_small(dev, vec_w, vec_m, vec_v, conv_w, conv_m, conv_v, small, landed):
    n_vec = len(vec_w)

    def body(dev_ref, *refs):
        refs = list(refs)
        take = lambda k: [refs.pop(0) for _ in range(k)]
        w_refs, m_refs, v_refs = take(n_vec), take(n_vec), take(n_vec)
        cw_ref, cm_ref, cv_ref, small_ref, land_ref, small_conv_ref, land_conv_ref = take(7)
        vec_outs, conv_outs, (loss_ref,) = take(4 * n_vec), take(4), take(1)

        def total(own_ref, others_ref):
            acc = None
            for d in range(N_DEV):
                part = jnp.where(dev_ref[0] == d, own_ref[0], others_ref[d])
                acc = part if acc is None else acc + part
            return acc

        g_all = total(small_ref, land_ref)
        for i in range(n_vec):
            g = g_all[i : i + 1]
            delta, m_new, v_new = _adam_step(w_refs[i][...], m_refs[i][...], v_refs[i][...], g)
            for ref, val in zip(vec_outs[4 * i : 4 * i + 4], (g, delta, m_new, v_new)):
                ref[...] = val
        g_conv = total(small_conv_ref, land_conv_ref)
        for tap in range(3):
            g = g_conv[tap : tap + 1]
            delta, m_new, v_new = _adam_step(cw_ref[tap], cm_ref[tap], cv_ref[tap], g)
            for ref, val in zip(conv_outs, (g, delta, m_new, v_new)):
                ref[tap] = val
        loss_ref[...] = jnp.broadcast_to(0.5 * jnp.sum(g_all[7:8], axis=-1, keepdims=True) / D_MODEL, loss_ref.shape)

    vec_spec = pl.BlockSpec((1, D_MODEL), lambda i, dev_ref: (0, 0))
    conv_spec = pl.BlockSpec((3, 1, CHUNK), lambda i, dev_ref: (0, 0, 0))
    vec_shape = jax.ShapeDtypeStruct((1, D_MODEL), F32)
    conv_shape = jax.ShapeDtypeStruct((3, 1, CHUNK), F32)
    outs, _ = _call(
        body,
        name="update_small",
        prefetch=[dev],
        operands=list(vec_w) + list(vec_m) + list(vec_v) + [conv_w, conv_m, conv_v, small, landed, small, landed],
        grid=(1,),
        in_specs=[vec_spec] * (3 * n_vec) + [conv_spec] * 3 + [
            pl.BlockSpec((1, 16, D_MODEL), lambda i, dev_ref: (0, 0, 0)),
            pl.BlockSpec((N_DEV, 16, D_MODEL), lambda i, dev_ref: (0, 0, 0)),
            pl.BlockSpec((1, 8, CHUNK), lambda i, dev_ref: (0, 1, dev_ref[0])),
            pl.BlockSpec((N_DEV, 8, CHUNK), lambda i, dev_ref: (0, 1, dev_ref[0])),
        ],
        out_specs=[vec_spec] * (4 * n_vec) + [conv_spec] * 4 + [pl.BlockSpec((1, CHUNK), lambda i, dev_ref: (0, 0))],
        out_shape=[vec_shape] * (4 * n_vec) + [conv_shape] * 4 + [jax.ShapeDtypeStruct((1, CHUNK), F32)],
    )
    return [outs[4 * i : 4 * i + 4] for i in range(n_vec)], outs[4 * n_vec : 4 * n_vec + 4], outs[-1]


def _adamw(name, groups, steps, own_slot=None, plans=()):
    lists = lambda p: list(p) if isinstance(p, (list, tuple)) else [p]
    groups = [(g[0], g[1], g[2], lists(g[3]), lists(g[4]) if len(g) > 4 else []) for g in groups]
    counts = [3 + len(g[3]) + len(g[4]) for g in groups]
    use_slot = own_slot is not None

    def first_tiles(w, parts):
        firsts, first = [], 0
        for p in parts:
            firsts.append(first)
            first += p.shape[1] // (w.shape[0] // steps)
        return firsts

    def update(t, slot_ref, firsts, w_ref, m_ref, v_ref, p_refs, o_refs, g_ref, d_ref, nm_ref, nv_ref):
        g = None
        for k, p_ref in enumerate(p_refs):
            gk = None
            for q in range(p_ref.shape[0]):
                part = p_ref[q].astype(F32)
                if o_refs:
                    part = jnp.where(slot_ref[0] == q, o_refs[k][0].astype(F32), part)
                gk = part if gk is None else gk + part
            g = gk if g is None else jnp.where(t >= firsts[k], gk, g)
        g_ref[...] = g
        d_ref[...], nm_ref[...], nv_ref[...] = _adam_step(w_ref[...], m_ref[...], v_ref[...], g)

    def body(*refs):
        refs = list(refs)
        slot_ref = refs.pop(0) if use_slot else None
        t = pl.program_id(0)
        ins, outs = refs[: sum(counts)], refs[sum(counts) :]
        for i, (w, _, _, parts, own) in enumerate(groups):
            mine = ins[sum(counts[:i]) : sum(counts[: i + 1])]
            update(t, slot_ref, first_tiles(w, parts), *mine[:3], mine[3 : 3 + len(parts)], mine[3 + len(parts) :],
                   *outs[4 * i : 4 * i + 4])

    operands, in_specs, out_specs, out_shape = [], [], [], []
    for w, m, v, parts, own in groups:
        r, cdim = w.shape
        tr = r // steps
        firsts = first_tiles(w, parts)

        def tile_of(k, t, firsts=firsts, parts=parts, tr=tr):
            return jnp.clip(t - firsts[k], 0, parts[k].shape[1] // tr - 1)

        blk = pl.BlockSpec((tr, cdim), lambda t, *_: (t, 0))
        operands += [w, m, v] + parts + own
        in_specs += [blk, blk, blk]
        in_specs += [pl.BlockSpec((p.shape[0], tr, cdim), lambda t, *_, k=k, tile_of=tile_of: (0, tile_of(k, t), 0)) for k, p in enumerate(parts)]
        in_specs += [pl.BlockSpec((1, tr, cdim), lambda t, slot_ref, k=k, tile_of=tile_of: (slot_ref[0], tile_of(k, t), 0)) for k in range(len(own))]
        out_specs += [blk] * 4
        out_shape += [jax.ShapeDtypeStruct((r, cdim), F32)] * 4
    outs, plan_outs = _call(
        body,
        name=name,
        prefetch=[own_slot] if use_slot else [],
        operands=operands,
        grid=(steps,),
        in_specs=in_specs,
        out_specs=out_specs,
        out_shape=out_shape,
        plans=plans,
    )
    return [outs[4 * i : 4 * i + 4] for i in range(len(groups))], plan_outs


def _chip_routes(n):
    x, y, c = _place()
    my_chip = 2 * x + y
    return [(a, (*chip, c), 2 * chip[0] + chip[1], my_chip, 2 * chip[0] + chip[1])
            for a in range(n) for chip in _other_chips(x, y)]


def _gather_routes(n):
    x, y, c = _place()
    flip = lambda v, f: 1 - v if f else v
    peers = [(flip(x, fx), flip(y, fy), flip(c, fc)) for fx in (0, 1) for fy in (0, 1) for fc in (0, 1) if fx + fy + fc]
    return [(0, p, 0, 4 * x + 2 * y + c, 4 * p[0] + 2 * p[1] + p[2]) for p in peers]


_chip_routes.copies_per_array = N_CHIP - 1
_gather_routes.copies_per_array = N_DEV - 1


def _split_start(name, routes, srcs, land_shapes):
    n = len(srcs)
    hbm = pl.BlockSpec(memory_space=pltpu.HBM)
    sem = pl.BlockSpec(memory_space=pltpu.SEMAPHORE)

    def body(*refs):
        src_refs, land_refs = refs[:n], refs[n : 2 * n]
        send_sems, recv_sems = refs[2 * n], refs[2 * n + 1]
        token = refs[-1]
        for k, (a, peer, src_slot, there, _) in enumerate(routes(n)):
            pltpu.make_async_remote_copy(
                src_ref=src_refs[a].at[src_slot], dst_ref=land_refs[a].at[there],
                send_sem=send_sems.at[k], recv_sem=recv_sems.at[k], device_id=peer, device_id_type=MESH).start()
        token[...] = jnp.zeros_like(token)

    in_hbm = [pltpu.with_memory_space_constraint(p, pltpu.HBM) for p in srcs]
    in_hbm += [pltpu.with_memory_space_constraint(lax.empty(shape, p.dtype), pltpu.HBM) for p, shape in zip(srcs, land_shapes)]
    n_copies = routes.copies_per_array * n
    res = pl.pallas_call(
        body,
        name=name,
        out_shape=[pltpu.SemaphoreType.DMA((n_copies,)), pltpu.SemaphoreType.DMA((n_copies,))]
        + [pltpu.HBM(p.shape, p.dtype) for p in srcs] + [pltpu.HBM(shape, p.dtype) for p, shape in zip(srcs, land_shapes)]
        + [jax.ShapeDtypeStruct((8, CHUNK), F32)],
        in_specs=[hbm] * (2 * n),
        out_specs=[sem, sem] + [hbm] * (2 * n) + [pl.BlockSpec(memory_space=pltpu.VMEM)],
        input_output_aliases={i: 2 + i for i in range(2 * n)},
        compiler_params=pltpu.CompilerParams(has_side_effects=pltpu.SideEffectType.DATAFLOW_SIDE_EFFECTING),
    )(*in_hbm)
    return res[:-1], res[-1]


def _split_wait(name, routes, in_flight, after):
    n = (len(in_flight) - 2) // 2
    hbm = pl.BlockSpec(memory_space=pltpu.HBM)
    sem = pl.BlockSpec(memory_space=pltpu.SEMAPHORE)
    send_sems, recv_sems, *bufs = in_flight

    def body(*refs):
        src_refs, land_refs = refs[:n], refs[n : 2 * n]
        send_ref, recv_ref = refs[2 * n], refs[2 * n + 1]
        for k, (a, peer, src_slot, _, here) in enumerate(routes(n)):
            cp = pltpu.make_async_remote_copy(
                src_ref=src_refs[a].at[src_slot], dst_ref=land_refs[a].at[here],
                send_sem=send_ref.at[k], recv_sem=recv_ref.at[k], device_id=peer, device_id_type=MESH)
            cp.wait_send()
            cp.wait_recv()

    res = pl.pallas_call(
        body,
        name=name,
        out_shape=[pltpu.HBM(b.shape, b.dtype) for b in bufs],
        in_specs=[hbm] * (2 * n) + [sem, sem] + [pl.BlockSpec(memory_space=pl.ANY)] * len(after),
        out_specs=[hbm] * (2 * n),
        input_output_aliases={i: i for i in range(2 * n)},
        compiler_params=pltpu.CompilerParams(has_side_effects=pltpu.SideEffectType.DATAFLOW_SIDE_EFFECTING),
    )(*bufs, send_sems, recv_sems, *after)
    return res[:n], res[n:]


def kernel(x, mem, norm_mix_g, w_in, conv_w, gm_ln_g, gm_ln_b, gm_ws, gm_bs, w_out, norm_x_g, norm_mem_g, w_q, w_kv, w_xo, norm_final_g, loss_target, m_norm_mix_g, m_w_in, m_conv_w, m_gm_ln_g, m_gm_ln_b, m_gm_ws, m_gm_bs, m_w_out, m_norm_x_g, m_norm_mem_g, m_w_q, m_w_kv, m_w_xo, m_norm_final_g, v_norm_mix_g, v_w_in, v_conv_w, v_gm_ln_g, v_gm_ln_b, v_gm_ws, v_gm_bs, v_w_out, v_norm_x_g, v_norm_mem_g, v_w_q, v_w_kv, v_w_xo, v_norm_final_g):
    s = x.shape[1]
    dev = 4 * lax.axis_index("x") + 2 * lax.axis_index("y") + lax.axis_index("c")
    core = lax.axis_index("c").astype(jnp.int32).reshape(1)
    x2 = x[0]
    target = loss_target[0]
    pad_taps = lambda t: jnp.pad(t[0], ((0, 5), (0, 0)))

    big_names = ["w_in", "w_out", "w_q", "w_kv", "w_xo"]
    big_w = dict(w_in=w_in[0], w_out=w_out[0], w_q=w_q[0], w_kv=w_kv[0], w_xo=w_xo[0])
    big_m = dict(w_in=m_w_in[0], w_out=m_w_out[0], w_q=m_w_q[0], w_kv=m_w_kv[0], w_xo=m_w_xo[0])
    big_v = dict(w_in=v_w_in[0], w_out=v_w_out[0], w_q=v_w_q[0], w_kv=v_w_kv[0], w_xo=v_w_xo[0])
    shard_axis = dict(w_in=1, w_out=0, w_q=0, w_kv=1, w_xo=0)
    full_shape = dict(w_in=(D_MODEL, IN_DIM), w_out=(MIX_DIM, D_MODEL), w_q=(D_MODEL, D_MODEL),
                      w_kv=(D_MODEL, 2 * D_MODEL), w_xo=(D_MODEL, D_MODEL))
    shard_size = {k: big_w[k].shape[shard_axis[k]] for k in big_names}
    others = big_names[1:]
    gather_spec = lambda k: (full_shape[k], shard_axis[k], shard_size[k])
    reduce_spec = lambda k: (shard_axis[k], shard_size[k])
    ws = gm_ws[0]
    bs_b = jnp.broadcast_to(gm_bs[0][:, :, None], (8, CHUNK, CHUNK))
    g_f = norm_final_g.reshape(1, D_MODEL)

    px, py = lax.axis_index("x"), lax.axis_index("y")
    chip_order = jnp.stack([2 * px + py, 2 * (1 - px) + py, 2 * px + 1 - py, 2 * (1 - px) + 1 - py]).astype(jnp.int32)
    (proj, h1, h1t, w_in_full), ((w_out_full, conv_g),) = _inproj(
        x2, norm_mix_g, big_w["w_in"].astype(BF16), chip_order,
        plans=[_gather_plan([big_w["w_out"].astype(BF16), pad_taps(conv_w)[None]], [gather_spec("w_out"), ((N_DEV, 8, CHUNK), 0, 1)])])
    conv_full = conv_g.transpose(1, 0, 2).reshape(8, D_MODEL)
    attn_w = ["w_q", "w_kv", "w_xo"]
    cat, (gathered,) = _mixer_fwd(
        proj, conv_full, gm_ln_g, gm_ln_b, ws, bs_b,
        plans=[_gather_plan([big_w[k].astype(BF16) for k in attn_w], [gather_spec(k) for k in attn_w])])
    wf = dict(zip(attn_w, gathered), w_out=w_out_full)
    k, vv, mt = _kv_fwd(mem[0], norm_mem_g, wf["w_kv"])
    (dcat, actt, cot, dk, dvv, d_gf, d_gx, sq_err) = _attn(
        x2, cat, target, wf["w_out"], wf["w_q"], wf["w_xo"], k, vv, norm_x_g, g_f)

    def pair_sums(names, grads, got):
        return [_pair_sum("pair_sum_" + k, grads[k], got[i], *reduce_spec(k), core) for i, k in enumerate(names)]

    sibling_plan = lambda names, grads: _sibling_plan([grads[k] for k in names], [reduce_spec(k) for k in names])
    early = ["w_kv", "w_out", "w_q", "w_xo"]
    big_g, big_gb = {}, {}
    big_g["w_kv"], big_gb["w_kv"], d_gmem = _kv_bwd(mem[0], wf["w_kv"], mt, dk, dvv)
    dws, (got_kv,) = _dw_attn(actt, cot, plans=[sibling_plan(early[:1], big_gb)])
    for name, (f32, bf16) in zip(early[1:], dws):
        big_g[name], big_gb[name] = f32, bf16
    pair_kv, (got_rest,) = _pair_sum("pair_sum_w_kv", big_g["w_kv"], got_kv[0], *reduce_spec("w_kv"), core,
                                     plans=[sibling_plan(early[1:], big_gb)])
    (dproj, d_cw, d_lng, d_lnb, d_ws, d_bs), (parts_early,) = _mixer_bwd(
        proj, dcat, conv_full, gm_ln_g, gm_ln_b, ws, bs_b,
        plans=[_chip_plan([pair_kv] + pair_sums(early[1:], big_g, got_rest))])
    half = D_MODEL // 2
    in_spec = reduce_spec("w_in")
    (dw_top, dwb_top), _ = _matmul_f32("dw_in_top", h1t, dproj, rows=(0, half))
    (dw_bot, dwb_bot), (got_top, (ws_all,)) = _matmul_f32(
        "dw_in_bot", h1t, dproj, rows=(half, half),
        plans=[_sibling_plan([dwb_top], [in_spec]),
               _gather_plan([d_ws.reshape(1, 8 * CHUNK, CHUNK).astype(BF16)], [((N_DEV, 8 * CHUNK, CHUNK), 0, 1)])])
    pair_top, ((got_bot,),) = _pair_sum("pair_sum_w_in_top", dw_top, got_top[0], *in_spec, core,
                                        plans=[_sibling_plan([dwb_bot], [in_spec])])
    pair_bot = _pair_sum("pair_sum_w_in_bot", dw_bot, got_bot, *in_spec, core)
    chip_shapes = [p.shape for p in (pair_top, pair_bot)]
    chips_in_flight, token = _split_start("grad_chip_exchange_start", _chip_routes, [pair_top, pair_bot], chip_shapes)
    n_tiles = s // min(TM_DH1, s)
    (grad_x, d_gmix), _ = _dh1("dh1", dproj, w_in_full, x2, cot, norm_mix_g, (0, n_tiles), after=[token])
    vec_names = ["norm_mix_g", "gm_ln_g", "gm_ln_b", "gm_bs", "norm_x_g", "norm_mem_g", "norm_final_g"]
    vec_g = [d_gmix, d_lng, d_lnb, d_bs, d_gx, d_gmem, d_gf]
    small = jnp.concatenate(vec_g + [sq_err, d_cw], axis=0)[None]
    small_in_flight, small_token = _split_start("small_all_gather_start", _gather_routes, [small], [(N_DEV, 16, D_MODEL)])
    flat_ws = lambda t: t.reshape(8 * CHUNK, CHUNK)
    updated, _ = _adamw(
        "adamw_early",
        [(big_w[k], big_m[k], big_v[k], got_k) for k, got_k in zip(early, parts_early)]
        + [(flat_ws(ws), flat_ws(m_gm_ws), flat_ws(v_gm_ws), ws_all)], steps=STEPS_ADAMW // 2)
    big_out = dict(zip(early, updated[:4]))
    ws_out = updated[4]
    (pair_top, pair_bot), (land_top, land_bot) = _split_wait(
        "grad_chip_exchange_wait", _chip_routes, chips_in_flight, [small_token] + [u[0] for u in updated])
    my_chip = (2 * px + py).astype(jnp.int32).reshape(1)
    (big_out["w_in"],), _ = _adamw(
        "adamw_w_in", [(big_w["w_in"], big_m["w_in"], big_v["w_in"], [land_top, land_bot], [pair_top, pair_bot])],
        steps=STEPS_ADAMW, own_slot=my_chip)
    (small,), (small_all,) = _split_wait("small_all_gather_wait", _gather_routes, small_in_flight, [big_out["w_in"][0]])
    row = lambda t: t.reshape(1, D_MODEL)
    vec_out, conv_out, loss_row = _update_small(
        dev.astype(jnp.int32).reshape(1),
        [norm_mix_g, gm_ln_g, gm_ln_b, row(gm_bs), norm_x_g, norm_mem_g, row(norm_final_g)],
        [m_norm_mix_g, m_gm_ln_g, m_gm_ln_b, row(m_gm_bs), m_norm_x_g, m_norm_mem_g, row(m_norm_final_g)],
        [v_norm_mix_g, v_gm_ln_g, v_gm_ln_b, row(v_gm_bs), v_norm_x_g, v_norm_mem_g, row(v_norm_final_g)],
        *(t.transpose(1, 0, 2) for t in (conv_w, m_conv_w, v_conv_w)), small, small_all)
    loss = loss_row[0, 0]

    def result(name, which):
        if name in big_out:
            return big_out[name][which][None]
        if name == "conv_w":
            return conv_out[which].transpose(1, 0, 2)
        if name == "gm_ws":
            return ws_out[which].reshape(1, 8, CHUNK, CHUNK)
        r = vec_out[vec_names.index(name)][which]
        if name == "gm_bs":
            return r.reshape(1, 8, CHUNK)
        if name == "norm_final_g":
            return r.reshape(D_MODEL)
        return r

    weights = ["norm_mix_g", "w_in", "conv_w", "gm_ln_g", "gm_ln_b", "gm_ws", "gm_bs", "w_out", "norm_x_g", "norm_mem_g", "w_q", "w_kv", "w_xo", "norm_final_g"]
    outs = [loss, grad_x[None]]
    for which in range(4):
        outs += [result(name, which) for name in weights]
    return tuple(outs)
```

```python
import math

import jax
import jax.numpy as jnp
from jax import lax
from jax.experimental import pallas as pl
from jax.experimental.pallas import tpu as pltpu

F32 = jnp.float32
BF16 = jnp.bfloat16
MESH = pl.DeviceIdType.MESH

D_MODEL = 1024
N_DEV = 8
N_CHIP = 4
CHUNK = 128
GROUPS = 7
IN_DIM = GROUPS * D_MODEL
SHARD_IN = IN_DIM // N_DEV
MIX_DIM = 2 * D_MODEL
X_HEADS = 4
X_HEAD_DIM = D_MODEL // X_HEADS
MEM_LEN = 256
EPS = 1e-6
GELU_K0 = math.sqrt(2.0 / math.pi)
GELU_K1 = 0.044715

ADAM_LR = 0.001
ADAM_B1 = 0.9
ADAM_B2 = 0.999
ADAM_EPS = 1e-08
ADAM_WD = 0.01
ADAM_STEP = 10

TM_INPROJ = 512
TM_ATTN = 512
SUB_ATTN = 256
TM_DH1 = 512
TM_DW = 512
TK_DW = 4096
TR_PAIR_SUM = 1024
BARRIER_ID = {"sibling": 1, "chips": 2, "all": 3}
STEPS_ADAMW = 4


def _dot(a, b):
    return lax.dot_general(a, b, (((1,), (0,)), ((), ())), preferred_element_type=F32)


def _dot_nt(a, b):
    return lax.dot_general(a, b, (((1,), (1,)), ((), ())), preferred_element_type=F32)


def _rows8(v):
    return jnp.sum(v.reshape(v.shape[0] // 8, 8, v.shape[1]), axis=0)


def _gelu(x, with_grad=True):
    x2 = x * x
    t = jnp.tanh(x * (GELU_K0 + (GELU_K0 * GELU_K1) * x2))
    half = 0.5 + 0.5 * t
    g = x * half
    if not with_grad:
        return g
    return g, half + ((0.5 * x) * (1.0 - t * t)) * (GELU_K0 + (3.0 * GELU_K0 * GELU_K1) * x2)


def _silu(z, with_grad=True):
    s = jax.nn.sigmoid(z)
    zs = z * s
    return (zs, s + zs * (1.0 - s)) if with_grad else zs


def _rms(v):
    r = lax.rsqrt(jnp.mean(v * v, axis=-1, keepdims=True) + EPS)
    return v * r, r


def _rms_bwd(dy_g, vh, r):
    return r * (dy_g - vh * jnp.mean(dy_g * vh, axis=-1, keepdims=True))


def _place():
    return lax.axis_index("x"), lax.axis_index("y"), lax.axis_index("c")


def _other_chips(x, y):
    return [(1 - x, y), (x, 1 - y), (1 - x, 1 - y)]


def _pow2_divisor(n):
    return n & (-n)


def _shard_view(ref, axis, size, d):
    start = d * size
    align = _pow2_divisor(size)
    if align > 1:
        start = pl.multiple_of(start, align)
    idx = [slice(None)] * len(ref.shape)
    idx[axis] = pl.ds(start, size)
    return ref.at[tuple(idx)]


class _Plan:
    def __init__(self, ins, out_shape, sems, start, finish, mid=None, aliases=None, peers="all"):
        self.ins, self.out_shape, self.sems = list(ins), list(out_shape), list(sems)
        self.start, self.mid, self.finish = start, mid, finish
        self.aliases = dict(aliases or {})
        self.peers = peers


def _gather_plan(shards, specs):
    n = len(shards)
    n_copy = 7

    def copies(ins, outs, sems):
        send_sems, recv_sems, local_sems = sems
        x, y, c = _place()
        me, sibling = (x, y, c), (x, y, 1 - c)
        chips = _other_chips(x, y)

        def block(a, p):
            _, axis, size = specs[a]
            return _shard_view(outs[a], axis, size, 4 * p[0] + 2 * p[1] + p[2])

        def copy(a, k, p, to, src=None):
            return pltpu.make_async_remote_copy(
                src_ref=block(a, p) if src is None else src,
                dst_ref=block(a, p),
                send_sem=send_sems.at[a * n_copy + k],
                recv_sem=recv_sems.at[a * n_copy + k],
                device_id=to,
                device_id_type=MESH,
            )

        def mine():
            return [pltpu.make_async_copy(ins[a], block(a, me), local_sems.at[a]) for a in range(n)]

        def first():
            return [cp for a in range(n) for cp in
                    [copy(a, 0, me, sibling, src=ins[a])] + [copy(a, 1 + j, me, (*chip, c), src=ins[a]) for j, chip in enumerate(chips)]]

        def landed():
            return [copy(a, 1 + j, (*chip, c), me) for j, chip in enumerate(chips) for a in range(n)]

        def passed():
            return [copy(a, 4 + j, (*chip, c), sibling) for j, chip in enumerate(chips) for a in range(n)]

        def last():
            return [cp for a in range(n) for cp in
                    [copy(a, 0, sibling, me)] + [copy(a, 4 + j, (*chip, 1 - c), me) for j, chip in enumerate(chips)]]

        return mine, first, landed, passed, last

    def start(ins, outs, sems):
        mine, first, _, _, _ = copies(ins, outs, sems)
        for cp in mine() + first():
            cp.start()

    def mid(ins, outs, sems):
        _, _, landed, passed, _ = copies(ins, outs, sems)
        for got, fwd in zip(landed(), passed()):
            got.wait_recv()
            fwd.start()

    def finish(ins, outs, sems):
        mine, first, _, passed, last = copies(ins, outs, sems)
        for cp in last():
            cp.wait_recv()
        for cp in first() + passed():
            cp.wait_send()
        for cp in mine():
            cp.wait()

    return _Plan(
        shards,
        [jax.ShapeDtypeStruct(full, s.dtype) for s, (full, _, _) in zip(shards, specs)],
        [pltpu.SemaphoreType.DMA((n * n_copy,)), pltpu.SemaphoreType.DMA((n * n_copy,)), pltpu.SemaphoreType.DMA((n,))],
        start, finish, mid,
    )


def _sibling_plan(grads, specs):
    n = len(grads)

    def shard_shape(a):
        axis, size = specs[a]
        shp = list(grads[a].shape)
        shp[axis] = size
        return tuple(shp)

    def copies(ins, outs, sems):
        send_sems, recv_sems = sems
        x, y, c = _place()
        return [
            pltpu.make_async_remote_copy(
                src_ref=_shard_view(ins[a], specs[a][0], specs[a][1], 2 * q + (1 - c)),
                dst_ref=outs[a].at[q],
                send_sem=send_sems.at[a * N_CHIP + q],
                recv_sem=recv_sems.at[a * N_CHIP + q],
                device_id=(x, y, 1 - c),
                device_id_type=MESH,
            )
            for a in range(n) for q in range(N_CHIP)
        ]

    def start(ins, outs, sems):
        for cp in copies(ins, outs, sems):
            cp.start()

    def finish(ins, outs, sems):
        for cp in copies(ins, outs, sems):
            cp.wait()

    return _Plan(
        grads,
        [jax.ShapeDtypeStruct((N_CHIP, *shard_shape(a)), grads[a].dtype) for a in range(n)],
        [pltpu.SemaphoreType.DMA((n * N_CHIP,)), pltpu.SemaphoreType.DMA((n * N_CHIP,))],
        start, finish, peers="sibling",
    )


def _chip_plan(parts, full_rows=None, row0=None, into=None):
    n = len(parts)
    full_rows = [p.shape[1] for p in parts] if full_rows is None else full_rows
    row0 = [0] * n if row0 is None else row0

    def copies(ins, outs, sems):
        send_sems, recv_sems, local_sems = sems
        x, y, c = _place()
        my_chip = 2 * x + y
        chips = _other_chips(x, y)

        def land(a, q):
            return outs[a].at[q, pl.ds(row0[a], parts[a].shape[1]), :]

        def mine():
            return [pltpu.make_async_copy(ins[a].at[my_chip], land(a, my_chip), local_sems.at[a]) for a in range(n)]

        def remote(sending):
            return [
                pltpu.make_async_remote_copy(
                    src_ref=ins[a].at[2 * chip[0] + chip[1]],
                    dst_ref=land(a, my_chip if sending else 2 * chip[0] + chip[1]),
                    send_sem=send_sems.at[a * 3 + j], recv_sem=recv_sems.at[a * 3 + j],
                    device_id=(*chip, c), device_id_type=MESH)
                for a in range(n) for j, chip in enumerate(chips)
            ]

        return mine, remote

    def start(ins, outs, sems):
        mine, remote = copies(ins, outs, sems)
        for cp in mine() + remote(True):
            cp.start()

    def finish(ins, outs, sems):
        mine, remote = copies(ins, outs, sems)
        for cp in remote(False):
            cp.wait_recv()
        for cp in remote(True):
            cp.wait_send()
        for cp in mine():
            cp.wait()

    return _Plan(
        list(parts) + (list(into) if into is not None else []),
        [jax.ShapeDtypeStruct((N_CHIP, full_rows[a], parts[a].shape[2]), parts[a].dtype) for a in range(n)],
        [pltpu.SemaphoreType.DMA((n * 3,)), pltpu.SemaphoreType.DMA((n * 3,)), pltpu.SemaphoreType.DMA((n,))],
        start, finish,
        aliases={n + a: a for a in range(n)} if into is not None else None, peers="chips",
    )


def _call(body, *, name, operands, out_shape, grid=(), in_specs=None, out_specs=None, scratch_shapes=(),
          plans=(), aliases=None, prefetch=(), start_after_body=False, pass_on_at=0.6, body_peers=None):
    operands, out_shape, scratch_shapes = list(operands), list(out_shape), list(scratch_shapes)
    n_pf, n_in, n_out, n_scr = len(prefetch), len(operands), len(out_shape), len(scratch_shapes)
    in_vmem = pl.BlockSpec(memory_space=pltpu.VMEM)
    in_hbm = pl.BlockSpec(memory_space=pl.ANY)
    in_specs = [in_vmem] * n_in if in_specs is None else list(in_specs)
    out_specs = [in_vmem] * n_out if out_specs is None else list(out_specs)
    io_alias = {n_pf + i: o for i, o in (aliases or {}).items()}
    pos_in, pos_out = n_pf + n_in, n_out
    for p in plans:
        for i, o in p.aliases.items():
            io_alias[pos_in + i] = pos_out + o
        pos_in += len(p.ins)
        pos_out += len(p.out_shape)
    steps = math.prod(grid) if grid else 1
    mid_step = min(steps - 1, int(steps * pass_on_at))
    patterns = {p.peers for p in plans} | ({body_peers} if body_peers else set())
    pattern = None if not patterns else patterns.pop() if len(patterns) == 1 else "all"

    def shake_hands():
        x, y, c = _place()
        peers = [(x, y, 1 - c)] if pattern in ("sibling", "all") else []
        peers += [(*chip, c) for chip in _other_chips(x, y)] if pattern in ("chips", "all") else []
        barrier = pltpu.get_barrier_semaphore()
        for peer in peers:
            pl.semaphore_signal(barrier, inc=1, device_id=peer, device_id_type=MESH)
        pl.semaphore_wait(barrier, len(peers))

    def wrapped(*refs):
        refs = list(refs)
        take = lambda k: [refs.pop(0) for _ in range(k)]
        pf = take(n_pf)
        ins, p_ins = take(n_in), [take(len(p.ins)) for p in plans]
        outs, p_outs = take(n_out), [take(len(p.out_shape)) for p in plans]
        scr, p_sems = take(n_scr), [take(len(p.sems)) for p in plans]
        step = 0
        for ax, g in enumerate(grid):
            step = step * g + pl.program_id(ax)

        def hook(kind, at):
            todo = [(getattr(p, kind), a, b, c) for p, a, b, c in zip(plans, p_ins, p_outs, p_sems) if getattr(p, kind)]

            def run():
                for fn, a, b, c in todo:
                    fn(a, b, c)

            if todo and grid:
                pl.when(step == at)(run)
            elif todo:
                run()

        if pattern and grid:
            pl.when(step == 0)(shake_hands)
        elif pattern:
            shake_hands()
        if not start_after_body:
            hook("start", 0)
        if body is not None:
            body(*pf, *ins, *outs, *scr)
        if start_after_body:
            hook("start", 0)
        hook("mid", mid_step)
        hook("finish", steps - 1)

    layout = dict(
        grid=grid,
        in_specs=in_specs + [in_hbm] * sum(len(p.ins) for p in plans),
        out_specs=out_specs + [in_hbm] * sum(len(p.out_shape) for p in plans),
        scratch_shapes=scratch_shapes + [s for p in plans for s in p.sems],
    )
    if n_pf:
        layout = dict(grid_spec=pltpu.PrefetchScalarGridSpec(num_scalar_prefetch=n_pf, **layout))
    res = pl.pallas_call(
        wrapped,
        name=name,
        out_shape=out_shape + [o for p in plans for o in p.out_shape],
        input_output_aliases=io_alias,
        compiler_params=pltpu.CompilerParams(
            dimension_semantics=("arbitrary",) * len(grid) if grid else None,
            collective_id=BARRIER_ID[pattern] if pattern else None),
        **layout,
    )(*prefetch, *operands, *[a for p in plans for a in p.ins])
    res = list(res)
    outs, plan_outs = res[:n_out], []
    pos = n_out
    for p in plans:
        plan_outs.append(res[pos : pos + len(p.out_shape)])
        pos += len(p.out_shape)
    return outs, plan_outs


def _inproj(x, g, w_shard, order, plans=()):
    s = x.shape[0]
    tm = min(TM_INPROJ, s)
    nt = s // tm
    wide = 2 * SHARD_IN
    fwd_at = nt // 2

    def body(order_ref, x_ref, g_ref, w_hbm, proj_ref, h1_ref, h1t_ref, wfull_hbm,
             w_scr, h1_scr, send_sems, recv_sems, local_sems):
        k, i = pl.program_id(0), pl.program_id(1)
        px, py, c = _place()
        me, sibling = (px, py, c), (px, py, 1 - c)
        chips = _other_chips(px, py)

        def half(slot, core):
            return w_scr.at[slot, :, pl.ds(pl.multiple_of(core * SHARD_IN, 128), SHARD_IN)]

        def copy(n, slot, core, to, src=None):
            return pltpu.make_async_remote_copy(
                src_ref=half(slot, core) if src is None else src, dst_ref=half(slot, core),
                send_sem=send_sems.at[n], recv_sem=recv_sems.at[n], device_id=to, device_id_type=MESH)

        def mine():
            return pltpu.make_async_copy(w_hbm, half(0, c), local_sems.at[0])

        def first():
            return [copy(0, 0, c, sibling, src=w_hbm)] + [copy(1 + j, 1 + j, c, (*chips[j], c), src=w_hbm) for j in range(2)]

        def relay():
            to = (c * px + (1 - c) * (1 - px), c * (1 - py) + (1 - c) * py, c)
            return pltpu.make_async_remote_copy(
                src_ref=half(2 - c, c), dst_ref=half(3, c), send_sem=send_sems.at[3], recv_sem=recv_sems.at[3],
                device_id=to, device_id_type=MESH)

        def save(slot):
            cols = pl.ds(pl.multiple_of(order_ref[slot] * wide, 128), wide)
            return pltpu.make_async_copy(w_scr.at[slot], wfull_hbm.at[:, cols], local_sems.at[1 + slot])

        @pl.when((k == 0) & (i == 0))
        def _():
            mine().start()
            for cp in first():
                cp.start()
            mine().wait()
            copy(0, 0, 1 - c, me).wait_recv()

        for j in range(3):
            @pl.when((k == j + 1) & (i == 0))
            def _():
                copy(4 + j, 1 + j, 1 - c, me).wait_recv()

        @pl.when((k == 3) & (i == 0))
        def _():
            for slot in range(N_CHIP):
                save(slot).start()

        @pl.when(k == 0)
        def _():
            xh, _ = _rms(x_ref[...])
            h = xh * g_ref[...]
            h1_ref[...] = h.astype(BF16)
            h1t_ref[...] = h.T.astype(BF16)
            h1_scr[pl.ds(pl.multiple_of(i * tm, tm), tm), :] = h.astype(BF16)

        proj_ref[...] = _dot(h1_scr[pl.ds(pl.multiple_of(i * tm, tm), tm), :], w_scr[k]).astype(BF16)

        for j in range(3):
            @pl.when((k == 0) & (i == nt - 1) if j < 2 else (k == 2) & (i == fwd_at))
            def _():
                copy(1 + j, 1 + j, c, me).wait_recv()
                copy(4 + j, 1 + j, c, sibling).start()
                if j < 2:
                    pl.when(c == 1 - j)(lambda: relay().start())

        @pl.when((k == 3) & (i == nt - 1))
        def _():
            for cp in first() + [relay()] + [copy(4 + j, 1 + j, c, sibling) for j in range(3)]:
                cp.wait_send()
            for slot in range(N_CHIP):
                save(slot).wait()

    once = lambda k, i, order_ref: (jnp.where(k == 0, i, nt - 1), 0)
    return _call(
        body,
        name="inproj",
        prefetch=[order],
        operands=[x, g, w_shard],
        grid=(N_CHIP, nt),
        in_specs=[
            pl.BlockSpec((tm, D_MODEL), once),
            pl.BlockSpec((1, D_MODEL), lambda k, i, order_ref: (0, 0)),
            pl.BlockSpec(memory_space=pl.ANY),
        ],
        out_specs=[
            pl.BlockSpec((tm, wide), lambda k, i, order_ref: (i, order_ref[k])),
            pl.BlockSpec((tm, D_MODEL), once),
            pl.BlockSpec((D_MODEL, tm), lambda k, i, order_ref: (0, jnp.where(k == 0, i, nt - 1))),
            pl.BlockSpec(memory_space=pl.ANY),
        ],
        out_shape=[
            jax.ShapeDtypeStruct((s, IN_DIM), BF16),
            jax.ShapeDtypeStruct((s, D_MODEL), BF16),
            jax.ShapeDtypeStruct((D_MODEL, s), BF16),
            jax.ShapeDtypeStruct((D_MODEL, IN_DIM), BF16),
        ],
        scratch_shapes=[
            pltpu.VMEM((N_CHIP, D_MODEL, wide), BF16),
            pltpu.VMEM((s, D_MODEL), BF16),
            pltpu.SemaphoreType.DMA((7,)),
            pltpu.SemaphoreType.DMA((7,)),
            pltpu.SemaphoreType.DMA((1 + N_CHIP,)),
        ],
        plans=plans,
        start_after_body=True,
        pass_on_at=0.8,
        body_peers="all",
    )


def _rows_from_above(v, above):
    row = lax.broadcasted_iota(jnp.int32, (8, v.shape[1]), 0)
    r1, r2 = pltpu.roll(v, 1, 0), pltpu.roll(v, 2, 0)
    top1 = jnp.where(row == 0, above[7:8], r1[0:8])
    top2 = jnp.where(row == 0, above[6:7], jnp.where(row == 1, above[7:8], r2[0:8]))
    return jnp.concatenate([top1, r1[8:]], axis=0), jnp.concatenate([top2, r2[8:]], axis=0)


def _rows_from_below(v, below):
    n = v.shape[0]
    row = lax.broadcasted_iota(jnp.int32, (8, v.shape[1]), 0)
    r1, r2 = pltpu.roll(v, n - 1, 0), pltpu.roll(v, n - 2, 0)
    end1 = jnp.where(row == 7, below[0:1], r1[n - 8 :])
    end2 = jnp.where(row == 6, below[0:1], jnp.where(row == 7, below[1:2], r2[n - 8 :]))
    return jnp.concatenate([r1[: n - 8], end1], axis=0), jnp.concatenate([r2[: n - 8], end2], axis=0)


def _causal_mask():
    row = lax.broadcasted_iota(jnp.int32, (CHUNK, CHUNK), 0)
    col = lax.broadcasted_iota(jnp.int32, (CHUNK, CHUNK), 1)
    return row, row >= col


def _mixer_fwd(proj, conv_w, ln_g, ln_b, ws, bs_b, plans=()):
    s = proj.shape[0]
    nt = s // CHUNK

    def body(proj_ref, cw_ref, lng_ref, lnb_ref, ws_ref, bsb_ref, cat_ref, pcar_ref):
        @pl.when(pl.program_id(0) == 0)
        def _():
            pcar_ref[...] = jnp.zeros_like(pcar_ref)

        row, tril = _causal_mask()
        for j in range(8):
            cs = slice(CHUNK * j, CHUNK * (j + 1))

            def grp(k):
                return proj_ref[:, k * D_MODEL + CHUNK * j : k * D_MODEL + CHUNK * (j + 1)].astype(F32)

            w = cw_ref[:, cs]
            p = grp(1) * grp(2)
            p1, p2 = _rows_from_above(p, pcar_ref[:, cs])
            pcar_ref[:, cs] = p[CHUNK - 8 :]
            cv = w[0:1] * p2 + w[1:2] * p1 + w[2:3] * p
            sa = _silu(grp(3), with_grad=False)
            cat_ref[:, cs] = ((grp(0) * cv) * sa).astype(BF16)
            gu = _gelu(grp(4), with_grad=False)
            gv = _gelu(grp(5), with_grad=False)
            dv = gv - jnp.mean(gv, axis=-1, keepdims=True)
            vn = dv * lax.rsqrt(jnp.mean(dv * dv, axis=-1, keepdims=True) + EPS)
            vn = vn * lng_ref[:, cs] + lnb_ref[:, cs]
            wc = jnp.where(tril, ws_ref[j], 0.0).astype(BF16)
            sp = _dot(wc, vn.astype(BF16)) + bsb_ref[j]
            sb = _silu(grp(6), with_grad=False)
            cat_ref[:, D_MODEL + CHUNK * j : D_MODEL + CHUNK * (j + 1)] = ((gu * sp) * sb).astype(BF16)

    whole = lambda shape: pl.BlockSpec(shape, lambda i: (0,) * len(shape))
    (cat,), plan_outs = _call(
        body,
        name="mixer_fwd",
        operands=[proj, conv_w, ln_g, ln_b, ws, bs_b],
        grid=(nt,),
        in_specs=[
            pl.BlockSpec((CHUNK, IN_DIM), lambda i: (i, 0)),
            whole((8, D_MODEL)),
            whole((1, D_MODEL)),
            whole((1, D_MODEL)),
            whole((8, CHUNK, CHUNK)),
            whole((8, CHUNK, CHUNK)),
        ],
        out_specs=[pl.BlockSpec((CHUNK, MIX_DIM), lambda i: (i, 0))],
        out_shape=[jax.ShapeDtypeStruct((s, MIX_DIM), BF16)],
        scratch_shapes=[pltpu.VMEM((8, D_MODEL), F32)],
        plans=plans,
        pass_on_at=0.85,
    )
    return cat, plan_outs


def _mixer_bwd(proj, dcat, conv_w, ln_g, ln_b, ws, bs_b, plans=()):
    s = proj.shape[0]
    nt = s // CHUNK

    def body(proj_ref, halo_ref, dcat_ref, cw_ref, lng_ref, lnb_ref, ws_ref, bsb_ref,
             dproj_ref, dcw_ref, dlng_ref, dlnb_ref, dws_ref, dbs_ref,
             car_ref, acc_cw, acc_lng, acc_lnb, acc_bs, wc_scr, wct_scr):
        i = pl.program_id(0)
        tile = nt - 1 - i
        row, tril = _causal_mask()

        @pl.when(i == 0)
        def _():
            for j in range(8):
                wc = jnp.where(tril, ws_ref[j], 0.0)
                wc_scr[j] = wc.astype(BF16)
                wct_scr[j] = wc.T.astype(BF16)
            car_ref[...] = jnp.zeros_like(car_ref)
            acc_cw[...] = jnp.zeros_like(acc_cw)
            acc_lng[...] = jnp.zeros_like(acc_lng)
            acc_lnb[...] = jnp.zeros_like(acc_lnb)
            acc_bs[...] = jnp.zeros_like(acc_bs)
            dws_ref[...] = jnp.zeros_like(dws_ref)

        has_prev = jnp.where(tile > 0, 1.0, 0.0).astype(F32)
        for j in range(8):
            cs = slice(CHUNK * j, CHUNK * (j + 1))

            def col(k):
                return slice(k * D_MODEL + CHUNK * j, k * D_MODEL + CHUNK * (j + 1))

            def grp(k):
                return proj_ref[:, col(k)].astype(F32)

            w = cw_ref[:, cs]
            gb, gc, xa = grp(0), grp(1), grp(2)
            p = gc * xa
            pprev = halo_ref[:, col(1)].astype(F32) * halo_ref[:, col(2)].astype(F32) * has_prev
            p1, p2 = _rows_from_above(p, pprev[8:16])
            cv = w[0:1] * p2 + w[1:2] * p1 + w[2:3] * p
            za = grp(3)
            sa, dsa = _silu(za)
            da = dcat_ref[:, cs].astype(F32)
            da_gb = da * gb
            dproj_ref[:, col(0)] = ((da * sa) * cv).astype(BF16)
            dproj_ref[:, col(3)] = ((da_gb * cv) * dsa).astype(BF16)
            dcv = da_gb * sa
            d1, d2 = _rows_from_below(dcv, car_ref[:, cs])
            car_ref[:, cs] = dcv[0:8]
            dp = w[2:3] * dcv + w[1:2] * d1 + w[0:1] * d2
            dproj_ref[:, col(1)] = (dp * xa).astype(BF16)
            dproj_ref[:, col(2)] = (dp * gc).astype(BF16)
            acc_cw[0, :, cs] += _rows8(dcv * p2)
            acc_cw[1, :, cs] += _rows8(dcv * p1)
            acc_cw[2, :, cs] += _rows8(dcv * p)
            gu, dgu = _gelu(grp(4))
            gv, dgv = _gelu(grp(5))
            dv = gv - jnp.mean(gv, axis=-1, keepdims=True)
            rstd = lax.rsqrt(jnp.mean(dv * dv, axis=-1, keepdims=True) + EPS)
            vnh = dv * rstd
            lng = lng_ref[:, cs]
            vnb = (vnh * lng + lnb_ref[:, cs]).astype(BF16)
            sp = _dot(wc_scr[j], vnb) + bsb_ref[j]
            zb = grp(6)
            sb, dsb = _silu(zb)
            db = dcat_ref[:, D_MODEL + CHUNK * j : D_MODEL + CHUNK * (j + 1)].astype(F32)
            db_sp, db_gu = db * sp, db * gu
            dproj_ref[:, col(4)] = ((db_sp * sb) * dgu).astype(BF16)
            dproj_ref[:, col(6)] = ((db_sp * gu) * dsb).astype(BF16)
            dsp = db_gu * sb
            acc_bs[j] += dsp
            dspb = dsp.astype(BF16)
            dws_ref[j] += _dot_nt(dspb, vnb)
            dvn = _dot(wct_scr[j], dspb)
            acc_lnb[:, cs] += _rows8(dvn)
            acc_lng[:, cs] += _rows8(dvn * vnh)
            dvh = dvn * lng
            dgvv = rstd * (dvh - jnp.mean(dvh, axis=-1, keepdims=True) - vnh * jnp.mean(dvh * vnh, axis=-1, keepdims=True))
            dproj_ref[:, col(5)] = (dgvv * dgv).astype(BF16)

        @pl.when(i == nt - 1)
        def _():
            dlng_ref[...] = jnp.sum(acc_lng[...], axis=0, keepdims=True)
            dlnb_ref[...] = jnp.sum(acc_lnb[...], axis=0, keepdims=True)
            dcw_ref[...] = jnp.zeros_like(dcw_ref)
            for k in range(3):
                dcw_ref[k : k + 1, :] = jnp.sum(acc_cw[k], axis=0, keepdims=True)
            for j in range(8):
                dws_ref[j] = jnp.where(tril, dws_ref[j], 0.0)
                dbs_ref[:, CHUNK * j : CHUNK * (j + 1)] = jnp.sum(acc_bs[j].T, axis=0, keepdims=True)

    whole = lambda shape: pl.BlockSpec(shape, lambda i: (0,) * len(shape))
    halo_rows = 16
    per = CHUNK // halo_rows
    return _call(
        body,
        name="mixer_bwd",
        operands=[proj, proj, dcat, conv_w, ln_g, ln_b, ws, bs_b],
        plans=plans,
        grid=(nt,),
        in_specs=[
            pl.BlockSpec((CHUNK, IN_DIM), lambda i: (nt - 1 - i, 0)),
            pl.BlockSpec((halo_rows, IN_DIM), lambda i: (jnp.maximum((nt - 1 - i) * per - 1, 0), 0)),
            pl.BlockSpec((CHUNK, MIX_DIM), lambda i: (nt - 1 - i, 0)),
            whole((8, D_MODEL)),
            whole((1, D_MODEL)),
            whole((1, D_MODEL)),
            whole((8, CHUNK, CHUNK)),
            whole((8, CHUNK, CHUNK)),
        ],
        out_specs=[
            pl.BlockSpec((CHUNK, IN_DIM), lambda i: (nt - 1 - i, 0)),
            whole((8, D_MODEL)),
            whole((1, D_MODEL)),
            whole((1, D_MODEL)),
            whole((8, CHUNK, CHUNK)),
            whole((1, D_MODEL)),
        ],
        out_shape=[
            jax.ShapeDtypeStruct((s, IN_DIM), BF16),
            jax.ShapeDtypeStruct((8, D_MODEL), F32),
            jax.ShapeDtypeStruct((1, D_MODEL), F32),
            jax.ShapeDtypeStruct((1, D_MODEL), F32),
            jax.ShapeDtypeStruct((8, CHUNK, CHUNK), F32),
            jax.ShapeDtypeStruct((1, D_MODEL), F32),
        ],
        scratch_shapes=[
            pltpu.VMEM((8, D_MODEL), F32),
            pltpu.VMEM((3, 8, D_MODEL), F32),
            pltpu.VMEM((8, D_MODEL), F32),
            pltpu.VMEM((8, D_MODEL), F32),
            pltpu.VMEM((8, CHUNK, CHUNK), F32),
            pltpu.VMEM((8, CHUNK, CHUNK), BF16),
            pltpu.VMEM((8, CHUNK, CHUNK), BF16),
        ],
    )


def _kv_fwd(mem, g_mem, w_kv):
    def body(mem_ref, g_ref, w_ref, k_ref, vv_ref, mt_ref):
        mh, _ = _rms(mem_ref[...])
        m = mh * g_ref[...]
        kv = _dot(m.astype(BF16), w_ref[...])
        k_ref[...] = kv[:, :D_MODEL].astype(BF16)
        vv_ref[...] = kv[:, D_MODEL:].astype(BF16)
        mt_ref[...] = m.T.astype(BF16)

    return pl.pallas_call(
        body,
        name="kv_fwd",
        out_shape=[
            jax.ShapeDtypeStruct((MEM_LEN, D_MODEL), BF16),
            jax.ShapeDtypeStruct((MEM_LEN, D_MODEL), BF16),
            jax.ShapeDtypeStruct((D_MODEL, MEM_LEN), BF16),
        ],
    )(mem, g_mem, w_kv)


def _kv_bwd(mem, w_kv, mt, dk, dvv):
    def body(mem_ref, w_ref, mt_ref, dk_ref, dvv_ref, dw_ref, dwb_ref, dg_ref):
        dkv = jnp.concatenate([dk_ref[...], dvv_ref[...]], axis=1).astype(BF16)
        dw = _dot(mt_ref[...], dkv)
        dw_ref[...] = dw
        dwb_ref[...] = dw.astype(BF16)
        dm = _dot_nt(dkv, w_ref[...])
        mh, _ = _rms(mem_ref[...])
        dg_ref[...] = jnp.sum(dm * mh, axis=0, keepdims=True)

    return pl.pallas_call(
        body,
        name="kv_bwd",
        out_shape=[
            jax.ShapeDtypeStruct((D_MODEL, 2 * D_MODEL), F32),
            jax.ShapeDtypeStruct((D_MODEL, 2 * D_MODEL), BF16),
            jax.ShapeDtypeStruct((1, D_MODEL), F32),
        ],
    )(mem, w_kv, mt, dk, dvv)


def _attn(x, cat, target, w_out, w_q, w_xo, k, vv, g_x, g_f):
    s = x.shape[0]
    tm = min(TM_ATTN, s)
    sub = min(SUB_ATTN, tm)
    nt = s // tm
    scale = 1.0 / math.sqrt(X_HEAD_DIM)

    def body(x_ref, cat_ref, t_ref, wout_ref, wq_ref, wxo_ref, k_ref, vv_ref, gx_ref, gf_ref,
             dcat_ref, actt_ref, cot_ref,
             dk_ref, dvv_ref, dgf_ref, dgx_ref, loss_ref,
             p_scr, q_scr, o_scr, dq_scr, acc_gf, acc_gx, acc_loss):
        i = pl.program_id(0)

        @pl.when(i == 0)
        def _():
            dk_ref[...] = jnp.zeros_like(dk_ref)
            dvv_ref[...] = jnp.zeros_like(dvv_ref)
            acc_gf[...] = jnp.zeros_like(acc_gf)
            acc_gx[...] = jnp.zeros_like(acc_gx)
            acc_loss[...] = jnp.zeros_like(acc_loss)

        subs = [slice(u * sub, (u + 1) * sub) for u in range(tm // sub)]
        heads = [slice(X_HEAD_DIM * h, X_HEAD_DIM * (h + 1)) for h in range(X_HEADS)]
        gx, gf = gx_ref[...], gf_ref[...]
        x1, x1h, r2 = [], [], []
        for rs in subs:
            cat = cat_ref[rs, :]
            actt_ref[0:MIX_DIM, rs] = cat.astype(F32).T.astype(BF16)
            v = x_ref[rs, :] + _dot(cat, wout_ref[...])
            vh, r = _rms(v)
            h2 = vh * gx
            actt_ref[MIX_DIM : MIX_DIM + D_MODEL, rs] = h2.T.astype(BF16)
            q_scr[rs, :] = _dot(h2.astype(BF16), wq_ref[...]).astype(BF16)
            x1.append(v), x1h.append(vh), r2.append(r)
        for rs in subs:
            for hs in heads:
                sc = _dot_nt(q_scr[rs, hs], k_ref[:, hs]) * scale
                e = jnp.exp(sc - jnp.max(sc, axis=-1, keepdims=True))
                p = e / jnp.sum(e, axis=-1, keepdims=True)
                p_scr[rs, hs] = p
                o_scr[rs, hs] = _dot(p.astype(BF16), vv_ref[:, hs])
        dx2, sq, d_gf = [], None, None
        for u, rs in enumerate(subs):
            o = o_scr[rs, :]
            actt_ref[MIX_DIM + D_MODEL :, rs] = o.T.astype(BF16)
            x2 = x1[u] + _dot(o.astype(BF16), wxo_ref[...])
            x2h, r3 = _rms(x2)
            err = x2h * gf - t_ref[rs, :]
            dy = err * (1.0 / D_MODEL)
            sq = _rows8(err * err) if sq is None else sq + _rows8(err * err)
            d_gf = _rows8(dy * x2h) if d_gf is None else d_gf + _rows8(dy * x2h)
            dx2.append(_rms_bwd(dy * gf, x2h, r3))
        acc_loss[...] += sq
        acc_gf[...] += d_gf
        d_vv, d_k = [None] * X_HEADS, [None] * X_HEADS
        for u, rs in enumerate(subs):
            dx2b = dx2[u].astype(BF16)
            cot_ref[2, rs, :] = dx2b
            dob = _dot_nt(dx2b, wxo_ref[...]).astype(BF16)
            for h, hs in enumerate(heads):
                p = p_scr[rs, hs]
                dp = _dot_nt(dob[:, hs], vv_ref[:, hs])
                ds = p * (dp - jnp.sum(dp * p, axis=-1, keepdims=True)) * scale
                dq_scr[rs, hs] = _dot(ds.astype(BF16), k_ref[:, hs]).astype(BF16)
                dvv_h = _dot(p.T.astype(BF16), dob[:, hs])
                dk_h = _dot(ds.T.astype(BF16), q_scr[rs, hs])
                d_vv[h] = dvv_h if d_vv[h] is None else d_vv[h] + dvv_h
                d_k[h] = dk_h if d_k[h] is None else d_k[h] + dk_h
        for h, hs in enumerate(heads):
            dvv_ref[:, hs] += d_vv[h]
            dk_ref[:, hs] += d_k[h]
        d_gx = None
        for u, rs in enumerate(subs):
            dqb = dq_scr[rs, :]
            cot_ref[1, rs, :] = dqb
            dh2 = _dot_nt(dqb, wq_ref[...])
            d_gx = _rows8(dh2 * x1h[u]) if d_gx is None else d_gx + _rows8(dh2 * x1h[u])
            dx1 = dx2[u] + _rms_bwd(dh2 * gx, x1h[u], r2[u])
            dx1b = dx1.astype(BF16)
            cot_ref[0, rs, :] = dx1b
            dcat_ref[rs, :] = _dot_nt(dx1b, wout_ref[...]).astype(BF16)
        acc_gx[...] += d_gx

        @pl.when(i == nt - 1)
        def _():
            dgf_ref[...] = jnp.sum(acc_gf[...], axis=0, keepdims=True)
            dgx_ref[...] = jnp.sum(acc_gx[...], axis=0, keepdims=True)
            loss_ref[...] = jnp.sum(acc_loss[...], axis=0, keepdims=True)

    whole = lambda shape: pl.BlockSpec(shape, lambda i: (0,) * len(shape))
    once = lambda shape: pl.BlockSpec(shape, lambda i: (0,) * len(shape), pipeline_mode=pl.Buffered(1))
    rows = lambda width: pl.BlockSpec((tm, width), lambda i: (i, 0))
    cols = lambda height: pl.BlockSpec((height, tm), lambda i: (0, i))
    vec = jax.ShapeDtypeStruct((1, D_MODEL), F32)
    return pl.pallas_call(
        body,
        name="attn",
        grid=(nt,),
        in_specs=[
            rows(D_MODEL), rows(MIX_DIM), rows(D_MODEL),
            once((MIX_DIM, D_MODEL)), once((D_MODEL, D_MODEL)), once((D_MODEL, D_MODEL)),
            once((MEM_LEN, D_MODEL)), once((MEM_LEN, D_MODEL)),
            whole((1, D_MODEL)), whole((1, D_MODEL)),
        ],
        out_specs=[
            rows(MIX_DIM), cols(2 * MIX_DIM),
            pl.BlockSpec((3, tm, D_MODEL), lambda i: (0, i, 0)),
            whole((MEM_LEN, D_MODEL)), whole((MEM_LEN, D_MODEL)),
            whole((1, D_MODEL)), whole((1, D_MODEL)), whole((1, D_MODEL)),
        ],
        out_shape=[
            jax.ShapeDtypeStruct((s, MIX_DIM), BF16),
            jax.ShapeDtypeStruct((2 * MIX_DIM, s), BF16),
            jax.ShapeDtypeStruct((3, s, D_MODEL), BF16),
            jax.ShapeDtypeStruct((MEM_LEN, D_MODEL), F32),
            jax.ShapeDtypeStruct((MEM_LEN, D_MODEL), F32),
            vec, vec, vec,
        ],
        scratch_shapes=[
            pltpu.VMEM((tm, D_MODEL), F32),
            pltpu.VMEM((tm, D_MODEL), BF16),
            pltpu.VMEM((tm, D_MODEL), F32),
            pltpu.VMEM((tm, D_MODEL), BF16),
            pltpu.VMEM((8, D_MODEL), F32),
            pltpu.VMEM((8, D_MODEL), F32),
            pltpu.VMEM((8, D_MODEL), F32),
        ],
        compiler_params=pltpu.CompilerParams(dimension_semantics=("arbitrary",)),
    )(x, cat, target, w_out, w_q, w_xo, k, vv, g_x, g_f)


def _matmul_f32(name, a, b, rows=None, plans=()):
    first, m = (0, a.shape[0]) if rows is None else rows
    kk = a.shape[1]
    n = b.shape[1]
    tm, tn, tk = min(m, TM_DW), min(n, D_MODEL), min(kk, TK_DW)
    nk = kk // tk
    i0 = first // tm

    def body(a_ref, b_ref, o_ref, ob_ref):
        prod = _dot(a_ref[...], b_ref[...])
        if nk == 1:
            o_ref[...] = prod
            ob_ref[...] = prod.astype(BF16)
            return
        kid = pl.program_id(2)

        @pl.when(kid == 0)
        def _():
            o_ref[...] = prod

        @pl.when(kid > 0)
        def _():
            o_ref[...] += prod

        @pl.when(kid == nk - 1)
        def _():
            ob_ref[...] = o_ref[...].astype(BF16)

    out_spec = pl.BlockSpec((tm, tn), lambda i, j, k: (i, j))
    outs, plan_outs = _call(
        body,
        name=name,
        operands=[a, b],
        grid=(m // tm, n // tn, nk),
        in_specs=[pl.BlockSpec((tm, tk), lambda i, j, k: (i + i0, k)), pl.BlockSpec((tk, tn), lambda i, j, k: (k, j))],
        out_specs=[out_spec, out_spec],
        out_shape=[jax.ShapeDtypeStruct((m, n), F32), jax.ShapeDtypeStruct((m, n), BF16)],
        plans=plans,
    )
    return outs, plan_outs


def _dw_attn(actt, cot, plans=()):
    s = actt.shape[1]
    tm = TM_DW
    counts = [MIX_DIM // tm, D_MODEL // tm, D_MODEL // tm]
    starts = [0, counts[0], counts[0] + counts[1]]
    which = lambda i: jnp.where(i < starts[1], 0, jnp.where(i < starts[2], 1, 2))

    def body(a_ref, b_ref, *outs):
        i = pl.program_id(0)
        prod = _dot(a_ref[...], b_ref[0])
        for m in range(3):
            @pl.when(which(i) == m)
            def _():
                outs[2 * m][...] = prod
                outs[2 * m + 1][...] = prod.astype(BF16)

    out_specs, out_shape = [], []
    for m in range(3):
        spec = pl.BlockSpec((tm, D_MODEL), lambda i, m=m: (jnp.clip(i - starts[m], 0, counts[m] - 1), 0))
        out_specs += [spec, spec]
        out_shape += [jax.ShapeDtypeStruct((counts[m] * tm, D_MODEL), F32), jax.ShapeDtypeStruct((counts[m] * tm, D_MODEL), BF16)]
    outs, plan_outs = _call(
        body,
        name="dw_attn",
        operands=[actt, cot],
        grid=(sum(counts),),
        in_specs=[pl.BlockSpec((tm, s), lambda i: (i, 0)), pl.BlockSpec((1, s, D_MODEL), lambda i: (which(i), 0, 0))],
        out_specs=out_specs,
        out_shape=out_shape,
        plans=plans,
    )
    return [outs[2 * m : 2 * m + 2] for m in range(3)], plan_outs


def _dh1(name, dproj, w_in, x, dx1, g, tiles, into=None, after=(), plans=()):
    s = x.shape[0]
    tm = min(TM_DH1, s)
    t0, nt = tiles[0], tiles[1] - tiles[0]

    def body(dp_ref, w_ref, x_ref, dx1_ref, g_ref, *rest):
        gx_ref, dg_ref, acc_g = rest[-3:]
        i = pl.program_id(0)

        @pl.when(i == 0)
        def _():
            acc_g[...] = jnp.zeros_like(acc_g)

        dh1 = _dot_nt(dp_ref[...], w_ref[...])
        xh, r = _rms(x_ref[...])
        acc_g[...] += _rows8(dh1 * xh)
        gx_ref[...] = dx1_ref[0].astype(F32) + _rms_bwd(dh1 * g_ref[...], xh, r)

        @pl.when(i == nt - 1)
        def _():
            dg_ref[...] = jnp.sum(acc_g[...], axis=0, keepdims=True)

    rows = pl.BlockSpec((tm, D_MODEL), lambda i: (i + t0, 0))
    return _call(
        body,
        name=name,
        operands=[dproj, w_in, x, dx1, g] + ([into] if into is not None else []) + list(after),
        grid=(nt,),
        in_specs=[
            pl.BlockSpec((tm, IN_DIM), lambda i: (i + t0, 0)),
            pl.BlockSpec((D_MODEL, IN_DIM), lambda i: (0, 0), pipeline_mode=pl.Buffered(1)),
            rows,
            pl.BlockSpec((1, tm, D_MODEL), lambda i: (0, i + t0, 0)),
            pl.BlockSpec((1, D_MODEL), lambda i: (0, 0)),
        ] + [pl.BlockSpec(memory_space=pl.ANY)] * ((into is not None) + len(after)),
        out_specs=[rows, pl.BlockSpec((1, D_MODEL), lambda i: (0, 0))],
        out_shape=[jax.ShapeDtypeStruct((s, D_MODEL), F32), jax.ShapeDtypeStruct((1, D_MODEL), F32)],
        scratch_shapes=[pltpu.VMEM((8, D_MODEL), F32)],
        plans=plans,
        aliases={5: 0} if into is not None else None,
    )


def _pair_sum(name, grad, got, axis, size, core, plans=()):
    shard = list(grad.shape)
    shard[axis] = size
    r, cdim = shard
    tr = min(r, TR_PAIR_SUM)
    nr = r // tr

    def body(core_ref, g_ref, got_ref, o_ref):
        o_ref[0] = (g_ref[...] + got_ref[0].astype(F32)).astype(BF16)

    if axis == 1:
        g_map = lambda q, t, core_ref: (t, 2 * q + core_ref[0])
    else:
        g_map = lambda q, t, core_ref: ((2 * q + core_ref[0]) * nr + t, 0)
    (out,), plan_outs = _call(
        body,
        name=name,
        prefetch=[core],
        operands=[grad, got],
        grid=(N_CHIP, nr),
        in_specs=[
            pl.BlockSpec((tr, cdim), g_map),
            pl.BlockSpec((1, tr, cdim), lambda q, t, core_ref: (q, t, 0)),
        ],
        out_specs=[pl.BlockSpec((1, tr, cdim), lambda q, t, core_ref: (q, t, 0))],
        out_shape=[jax.ShapeDtypeStruct((N_CHIP, r, cdim), BF16)],
        plans=plans,
    )
    return (out, plan_outs) if plans else out


def _adam_step(w, m, v, g):
    c1 = 1.0 / (1.0 - ADAM_B1**ADAM_STEP)
    c2 = 1.0 / (1.0 - ADAM_B2**ADAM_STEP)
    m_new = ADAM_B1 * m + (1.0 - ADAM_B1) * g
    v_new = ADAM_B2 * v + (1.0 - ADAM_B2) * (g * g)
    return -ADAM_LR * ((m_new * c1) / (jnp.sqrt(v_new * c2) + ADAM_EPS) + ADAM_WD * w), m_new, v_new


def _update_small(dev, vec_w, vec_m, vec_v, conv_w, conv_m, conv_v, small, landed):
    n_vec = len(vec_w)

    def body(dev_ref, *refs):
        refs = list(refs)
        take = lambda k: [refs.pop(0) for _ in range(k)]
        w_refs, m_refs, v_refs = take(n_vec), take(n_vec), take(n_vec)
        cw_ref, cm_ref, cv_ref, small_ref, land_ref, small_conv_ref, land_conv_ref = take(7)
        vec_outs, conv_outs, (loss_ref,) = take(4 * n_vec), take(4), take(1)

        def total(own_ref, others_ref):
            acc = None
            for d in range(N_DEV):
                part = jnp.where(dev_ref[0] == d, own_ref[0], others_ref[d])
                acc = part if acc is None else acc + part
            return acc

        g_all = total(small_ref, land_ref)
        for i in range(n_vec):
            g = g_all[i : i + 1]
            delta, m_new, v_new = _adam_step(w_refs[i][...], m_refs[i][...], v_refs[i][...], g)
            for ref, val in zip(vec_outs[4 * i : 4 * i + 4], (g, delta, m_new, v_new)):
                ref[...] = val
        g_conv = total(small_conv_ref, land_conv_ref)
        for tap in range(3):
            g = g_conv[tap : tap + 1]
            delta, m_new, v_new = _adam_step(cw_ref[tap], cm_ref[tap], cv_ref[tap], g)
            for ref, val in zip(conv_outs, (g, delta, m_new, v_new)):
                ref[tap] = val
        loss_ref[...] = jnp.broadcast_to(0.5 * jnp.sum(g_all[7:8], axis=-1, keepdims=True) / D_MODEL, loss_ref.shape)

    vec_spec = pl.BlockSpec((1, D_MODEL), lambda i, dev_ref: (0, 0))
    conv_spec = pl.BlockSpec((3, 1, CHUNK), lambda i, dev_ref: (0, 0, 0))
    vec_shape = jax.ShapeDtypeStruct((1, D_MODEL), F32)
    conv_shape = jax.ShapeDtypeStruct((3, 1, CHUNK), F32)
    outs, _ = _call(
        body,
        name="update_small",
        prefetch=[dev],
        operands=list(vec_w) + list(vec_m) + list(vec_v) + [conv_w, conv_m, conv_v, small, landed, small, landed],
        grid=(1,),
        in_specs=[vec_spec] * (3 * n_vec) + [conv_spec] * 3 + [
            pl.BlockSpec((1, 16, D_MODEL), lambda i, dev_ref: (0, 0, 0)),
            pl.BlockSpec((N_DEV, 16, D_MODEL), lambda i, dev_ref: (0, 0, 0)),
            pl.BlockSpec((1, 8, CHUNK), lambda i, dev_ref: (0, 1, dev_ref[0])),
            pl.BlockSpec((N_DEV, 8, CHUNK), lambda i, dev_ref: (0, 1, dev_ref[0])),
        ],
        out_specs=[vec_spec] * (4 * n_vec) + [conv_spec] * 4 + [pl.BlockSpec((1, CHUNK), lambda i, dev_ref: (0, 0))],
        out_shape=[vec_shape] * (4 * n_vec) + [conv_shape] * 4 + [jax.ShapeDtypeStruct((1, CHUNK), F32)],
    )
    return [outs[4 * i : 4 * i + 4] for i in range(n_vec)], outs[4 * n_vec : 4 * n_vec + 4], outs[-1]


def _adamw(name, groups, steps, own_slot=None, plans=()):
    lists = lambda p: list(p) if isinstance(p, (list, tuple)) else [p]
    groups = [(g[0], g[1], g[2], lists(g[3]), lists(g[4]) if len(g) > 4 else []) for g in groups]
    counts = [3 + len(g[3]) + len(g[4]) for g in groups]
    use_slot = own_slot is not None

    def first_tiles(w, parts):
        firsts, first = [], 0
        for p in parts:
            firsts.append(first)
            first += p.shape[1] // (w.shape[0] // steps)
        return firsts

    def update(t, slot_ref, firsts, w_ref, m_ref, v_ref, p_refs, o_refs, g_ref, d_ref, nm_ref, nv_ref):
        g = None
        for k, p_ref in enumerate(p_refs):
            gk = None
            for q in range(p_ref.shape[0]):
                part = p_ref[q].astype(F32)
                if o_refs:
                    part = jnp.where(slot_ref[0] == q, o_refs[k][0].astype(F32), part)
                gk = part if gk is None else gk + part
            g = gk if g is None else jnp.where(t >= firsts[k], gk, g)
        g_ref[...] = g
        d_ref[...], nm_ref[...], nv_ref[...] = _adam_step(w_ref[...], m_ref[...], v_ref[...], g)

    def body(*refs):
        refs = list(refs)
        slot_ref = refs.pop(0) if use_slot else None
        t = pl.program_id(0)
        ins, outs = refs[: sum(counts)], refs[sum(counts) :]
        for i, (w, _, _, parts, own) in enumerate(groups):
            mine = ins[sum(counts[:i]) : sum(counts[: i + 1])]
            update(t, slot_ref, first_tiles(w, parts), *mine[:3], mine[3 : 3 + len(parts)], mine[3 + len(parts) :],
                   *outs[4 * i : 4 * i + 4])

    operands, in_specs, out_specs, out_shape = [], [], [], []
    for w, m, v, parts, own in groups:
        r, cdim = w.shape
        tr = r // steps
        firsts = first_tiles(w, parts)

        def tile_of(k, t, firsts=firsts, parts=parts, tr=tr):
            return jnp.clip(t - firsts[k], 0, parts[k].shape[1] // tr - 1)

        blk = pl.BlockSpec((tr, cdim), lambda t, *_: (t, 0))
        operands += [w, m, v] + parts + own
        in_specs += [blk, blk, blk]
        in_specs += [pl.BlockSpec((p.shape[0], tr, cdim), lambda t, *_, k=k, tile_of=tile_of: (0, tile_of(k, t), 0)) for k, p in enumerate(parts)]
        in_specs += [pl.BlockSpec((1, tr, cdim), lambda t, slot_ref, k=k, tile_of=tile_of: (slot_ref[0], tile_of(k, t), 0)) for k in range(len(own))]
        out_specs += [blk] * 4
        out_shape += [jax.ShapeDtypeStruct((r, cdim), F32)] * 4
    outs, plan_outs = _call(
        body,
        name=name,
        prefetch=[own_slot] if use_slot else [],
        operands=operands,
        grid=(steps,),
        in_specs=in_specs,
        out_specs=out_specs,
        out_shape=out_shape,
        plans=plans,
    )
    return [outs[4 * i : 4 * i + 4] for i in range(len(groups))], plan_outs


def _chip_routes(n):
    x, y, c = _place()
    my_chip = 2 * x + y
    return [(a, (*chip, c), 2 * chip[0] + chip[1], my_chip, 2 * chip[0] + chip[1])
            for a in range(n) for chip in _other_chips(x, y)]


def _gather_routes(n):
    x, y, c = _place()
    flip = lambda v, f: 1 - v if f else v
    peers = [(flip(x, fx), flip(y, fy), flip(c, fc)) for fx in (0, 1) for fy in (0, 1) for fc in (0, 1) if fx + fy + fc]
    return [(0, p, 0, 4 * x + 2 * y + c, 4 * p[0] + 2 * p[1] + p[2]) for p in peers]


_chip_routes.copies_per_array = N_CHIP - 1
_gather_routes.copies_per_array = N_DEV - 1


def _split_start(name, routes, srcs, land_shapes):
    n = len(srcs)
    hbm = pl.BlockSpec(memory_space=pltpu.HBM)
    sem = pl.BlockSpec(memory_space=pltpu.SEMAPHORE)

    def body(*refs):
        src_refs, land_refs = refs[:n], refs[n : 2 * n]
        send_sems, recv_sems = refs[2 * n], refs[2 * n + 1]
        token = refs[-1]
        for k, (a, peer, src_slot, there, _) in enumerate(routes(n)):
            pltpu.make_async_remote_copy(
                src_ref=src_refs[a].at[src_slot], dst_ref=land_refs[a].at[there],
                send_sem=send_sems.at[k], recv_sem=recv_sems.at[k], device_id=peer, device_id_type=MESH).start()
        token[...] = jnp.zeros_like(token)

    in_hbm = [pltpu.with_memory_space_constraint(p, pltpu.HBM) for p in srcs]
    in_hbm += [pltpu.with_memory_space_constraint(lax.empty(shape, p.dtype), pltpu.HBM) for p, shape in zip(srcs, land_shapes)]
    n_copies = routes.copies_per_array * n
    res = pl.pallas_call(
        body,
        name=name,
        out_shape=[pltpu.SemaphoreType.DMA((n_copies,)), pltpu.SemaphoreType.DMA((n_copies,))]
        + [pltpu.HBM(p.shape, p.dtype) for p in srcs] + [pltpu.HBM(shape, p.dtype) for p, shape in zip(srcs, land_shapes)]
        + [jax.ShapeDtypeStruct((8, CHUNK), F32)],
        in_specs=[hbm] * (2 * n),
        out_specs=[sem, sem] + [hbm] * (2 * n) + [pl.BlockSpec(memory_space=pltpu.VMEM)],
        input_output_aliases={i: 2 + i for i in range(2 * n)},
        compiler_params=pltpu.CompilerParams(has_side_effects=pltpu.SideEffectType.DATAFLOW_SIDE_EFFECTING),
    )(*in_hbm)
    return res[:-1], res[-1]


def _split_wait(name, routes, in_flight, after):
    n = (len(in_flight) - 2) // 2
    hbm = pl.BlockSpec(memory_space=pltpu.HBM)
    sem = pl.BlockSpec(memory_space=pltpu.SEMAPHORE)
    send_sems, recv_sems, *bufs = in_flight

    def body(*refs):
        src_refs, land_refs = refs[:n], refs[n : 2 * n]
        send_ref, recv_ref = refs[2 * n], refs[2 * n + 1]
        for k, (a, peer, src_slot, _, here) in enumerate(routes(n)):
            cp = pltpu.make_async_remote_copy(
                src_ref=src_refs[a].at[src_slot], dst_ref=land_refs[a].at[here],
                send_sem=send_ref.at[k], recv_sem=recv_ref.at[k], device_id=peer, device_id_type=MESH)
            cp.wait_send()
            cp.wait_recv()

    res = pl.pallas_call(
        body,
        name=name,
        out_shape=[pltpu.HBM(b.shape, b.dtype) for b in bufs],
        in_specs=[hbm] * (2 * n) + [sem, sem] + [pl.BlockSpec(memory_space=pl.ANY)] * len(after),
        out_specs=[hbm] * (2 * n),
        input_output_aliases={i: i for i in range(2 * n)},
        compiler_params=pltpu.CompilerParams(has_side_effects=pltpu.SideEffectType.DATAFLOW_SIDE_EFFECTING),
    )(*bufs, send_sems, recv_sems, *after)
    return res[:n], res[n:]


def kernel(x, mem, norm_mix_g, w_in, conv_w, gm_ln_g, gm_ln_b, gm_ws, gm_bs, w_out, norm_x_g, norm_mem_g, w_q, w_kv, w_xo, norm_final_g, loss_target, m_norm_mix_g, m_w_in, m_conv_w, m_gm_ln_g, m_gm_ln_b, m_gm_ws, m_gm_bs, m_w_out, m_norm_x_g, m_norm_mem_g, m_w_q, m_w_kv, m_w_xo, m_norm_final_g, v_norm_mix_g, v_w_in, v_conv_w, v_gm_ln_g, v_gm_ln_b, v_gm_ws, v_gm_bs, v_w_out, v_norm_x_g, v_norm_mem_g, v_w_q, v_w_kv, v_w_xo, v_norm_final_g):
    s = x.shape[1]
    dev = 4 * lax.axis_index("x") + 2 * lax.axis_index("y") + lax.axis_index("c")
    core = lax.axis_index("c").astype(jnp.int32).reshape(1)
    x2 = x[0]
    target = loss_target[0]
    pad_taps = lambda t: jnp.pad(t[0], ((0, 5), (0, 0)))

    big_names = ["w_in", "w_out", "w_q", "w_kv", "w_xo"]
    big_w = dict(w_in=w_in[0], w_out=w_out[0], w_q=w_q[0], w_kv=w_kv[0], w_xo=w_xo[0])
    big_m = dict(w_in=m_w_in[0], w_out=m_w_out[0], w_q=m_w_q[0], w_kv=m_w_kv[0], w_xo=m_w_xo[0])
    big_v = dict(w_in=v_w_in[0], w_out=v_w_out[0], w_q=v_w_q[0], w_kv=v_w_kv[0], w_xo=v_w_xo[0])
    shard_axis = dict(w_in=1, w_out=0, w_q=0, w_kv=1, w_xo=0)
    full_shape = dict(w_in=(D_MODEL, IN_DIM), w_out=(MIX_DIM, D_MODEL), w_q=(D_MODEL, D_MODEL),
                      w_kv=(D_MODEL, 2 * D_MODEL), w_xo=(D_MODEL, D_MODEL))
    shard_size = {k: big_w[k].shape[shard_axis[k]] for k in big_names}
    others = big_names[1:]
    gather_spec = lambda k: (full_shape[k], shard_axis[k], shard_size[k])
    reduce_spec = lambda k: (shard_axis[k], shard_size[k])
    ws = gm_ws[0]
    bs_b = jnp.broadcast_to(gm_bs[0][:, :, None], (8, CHUNK, CHUNK))
    g_f = norm_final_g.reshape(1, D_MODEL)

    px, py = lax.axis_index("x"), lax.axis_index("y")
    chip_order = jnp.stack([2 * px + py, 2 * (1 - px) + py, 2 * px + 1 - py, 2 * (1 - px) + 1 - py]).astype(jnp.int32)
    with_inproj, with_mixer = ["w_out", "w_q"], ["w_kv", "w_xo"]
    (proj, h1, h1t, w_in_full), ((*gathered_early, conv_g),) = _inproj(
        x2, norm_mix_g, big_w["w_in"].astype(BF16), chip_order,
        plans=[_gather_plan([big_w[k].astype(BF16) for k in with_inproj] + [pad_taps(conv_w)[None]],
                            [gather_spec(k) for k in with_inproj] + [((N_DEV, 8, CHUNK), 0, 1)])])
    conv_full = conv_g.transpose(1, 0, 2).reshape(8, D_MODEL)
    cat, (gathered,) = _mixer_fwd(
        proj, conv_full, gm_ln_g, gm_ln_b, ws, bs_b,
        plans=[_gather_plan([big_w[k].astype(BF16) for k in with_mixer], [gather_spec(k) for k in with_mixer])])
    wf = dict(zip(with_inproj + with_mixer, list(gathered_early) + list(gathered)))
    k, vv, mt = _kv_fwd(mem[0], norm_mem_g, wf["w_kv"])
    (dcat, actt, cot, dk, dvv, d_gf, d_gx, sq_err) = _attn(
        x2, cat, target, wf["w_out"], wf["w_q"], wf["w_xo"], k, vv, norm_x_g, g_f)

    def pair_sums(names, grads, got):
        return [_pair_sum("pair_sum_" + k, grads[k], got[i], *reduce_spec(k), core) for i, k in enumerate(names)]

    sibling_plan = lambda names, grads: _sibling_plan([grads[k] for k in names], [reduce_spec(k) for k in names])
    early = ["w_kv", "w_out", "w_q", "w_xo"]
    big_g, big_gb = {}, {}
    big_g["w_kv"], big_gb["w_kv"], d_gmem = _kv_bwd(mem[0], wf["w_kv"], mt, dk, dvv)
    dws, (got_kv,) = _dw_attn(actt, cot, plans=[sibling_plan(early[:1], big_gb)])
    for name, (f32, bf16) in zip(early[1:], dws):
        big_g[name], big_gb[name] = f32, bf16
    pair_kv, (got_rest,) = _pair_sum("pair_sum_w_kv", big_g["w_kv"], got_kv[0], *reduce_spec("w_kv"), core,
                                     plans=[sibling_plan(early[1:], big_gb)])
    (dproj, d_cw, d_lng, d_lnb, d_ws, d_bs), (parts_early,) = _mixer_bwd(
        proj, dcat, conv_full, gm_ln_g, gm_ln_b, ws, bs_b,
        plans=[_chip_plan([pair_kv] + pair_sums(early[1:], big_g, got_rest))])
    half = D_MODEL // 2
    in_spec = reduce_spec("w_in")
    (dw_top, dwb_top), _ = _matmul_f32("dw_in_top", h1t, dproj, rows=(0, half))
    (dw_bot, dwb_bot), (got_top, (ws_all,)) = _matmul_f32(
        "dw_in_bot", h1t, dproj, rows=(half, half),
        plans=[_sibling_plan([dwb_top], [in_spec]),
               _gather_plan([d_ws.reshape(1, 8 * CHUNK, CHUNK).astype(BF16)], [((N_DEV, 8 * CHUNK, CHUNK), 0, 1)])])
    pair_top, ((got_bot,),) = _pair_sum("pair_sum_w_in_top", dw_top, got_top[0], *in_spec, core,
                                        plans=[_sibling_plan([dwb_bot], [in_spec])])
    pair_bot = _pair_sum("pair_sum_w_in_bot", dw_bot, got_bot, *in_spec, core)
    chip_shapes = [p.shape for p in (pair_top, pair_bot)]
    chips_in_flight, token = _split_start("grad_chip_exchange_start", _chip_routes, [pair_top, pair_bot], chip_shapes)
    n_tiles = s // min(TM_DH1, s)
    (grad_x, d_gmix), _ = _dh1("dh1", dproj, w_in_full, x2, cot, norm_mix_g, (0, n_tiles), after=[token])
    vec_names = ["norm_mix_g", "gm_ln_g", "gm_ln_b", "gm_bs", "norm_x_g", "norm_mem_g", "norm_final_g"]
    vec_g = [d_gmix, d_lng, d_lnb, d_bs, d_gx, d_gmem, d_gf]
    small = jnp.concatenate(vec_g + [sq_err, d_cw], axis=0)[None]
    small_in_flight, small_token = _split_start("small_all_gather_start", _gather_routes, [small], [(N_DEV, 16, D_MODEL)])
    flat_ws = lambda t: t.reshape(8 * CHUNK, CHUNK)
    updated, _ = _adamw(
        "adamw_early",
        [(big_w[k], big_m[k], big_v[k], got_k) for k, got_k in zip(early, parts_early)]
        + [(flat_ws(ws), flat_ws(m_gm_ws), flat_ws(v_gm_ws), ws_all)], steps=STEPS_ADAMW // 2)
    big_out = dict(zip(early, updated[:4]))
    ws_out = updated[4]
    (pair_top, pair_bot), (land_top, land_bot) = _split_wait(
        "grad_chip_exchange_wait", _chip_routes, chips_in_flight, [small_token] + [u[0] for u in updated])
    my_chip = (2 * px + py).astype(jnp.int32).reshape(1)
    (big_out["w_in"],), _ = _adamw(
        "adamw_w_in", [(big_w["w_in"], big_m["w_in"], big_v["w_in"], [land_top, land_bot], [pair_top, pair_bot])],
        steps=STEPS_ADAMW, own_slot=my_chip)
    (small,), (small_all,) = _split_wait("small_all_gather_wait", _gather_routes, small_in_flight, [big_out["w_in"][0]])
    row = lambda t: t.reshape(1, D_MODEL)
    vec_out, conv_out, loss_row = _update_small(
        dev.astype(jnp.int32).reshape(1),
        [norm_mix_g, gm_ln_g, gm_ln_b, row(gm_bs), norm_x_g, norm_mem_g, row(norm_final_g)],
        [m_norm_mix_g, m_gm_ln_g, m_gm_ln_b, row(m_gm_bs), m_norm_x_g, m_norm_mem_g, row(m_norm_final_g)],
        [v_norm_mix_g, v_gm_ln_g, v_gm_ln_b, row(v_gm_bs), v_norm_x_g, v_norm_mem_g, row(v_norm_final_g)],
        *(t.transpose(1, 0, 2) for t in (conv_w, m_conv_w, v_conv_w)), small, small_all)
    loss = loss_row[0, 0]

    def result(name, which):
        if name in big_out:
            return big_out[name][which][None]
        if name == "conv_w":
            return conv_out[which].transpose(1, 0, 2)
        if name == "gm_ws":
            return ws_out[which].reshape(1, 8, CHUNK, CHUNK)
        r = vec_out[vec_names.index(name)][which]
        if name == "gm_bs":
            return r.reshape(1, 8, CHUNK)
        if name == "norm_final_g":
            return r.reshape(D_MODEL)
        return r

    weights = ["norm_mix_g", "w_in", "conv_w", "gm_ln_g", "gm_ln_b", "gm_ws", "gm_bs", "w_out", "norm_x_g", "norm_mem_g", "w_q", "w_kv", "w_xo", "norm_final_g"]
    outs = [loss, grad_x[None]]
    for which in range(4):
        outs += [result(name, which) for name in weights]
    return tuple(outs)
```

```python
import math

import jax
import jax.numpy as jnp
from jax import lax
from jax.experimental import pallas as pl
from jax.experimental.pallas import tpu as pltpu

F32 = jnp.float32
BF16 = jnp.bfloat16
MESH = pl.DeviceIdType.MESH

D_MODEL = 1024
N_DEV = 8
N_CHIP = 4
CHUNK = 128
GROUPS = 7
IN_DIM = GROUPS * D_MODEL
SHARD_IN = IN_DIM // N_DEV
MIX_DIM = 2 * D_MODEL
X_HEADS = 4
X_HEAD_DIM = D_MODEL // X_HEADS
MEM_LEN = 256
EPS = 1e-6
GELU_K0 = math.sqrt(2.0 / math.pi)
GELU_K1 = 0.044715

ADAM_LR = 0.001
ADAM_B1 = 0.9
ADAM_B2 = 0.999
ADAM_EPS = 1e-08
ADAM_WD = 0.01
ADAM_STEP = 10

TM_INPROJ = 512
TM_ATTN = 512
SUB_ATTN = 256
TM_DH1 = 512
TM_DW = 512
TK_DW = 4096
TR_PAIR_SUM = 1024
BARRIER_ID = {"sibling": 1, "chips": 2, "all": 3}
STEPS_ADAMW = 4


def _dot(a, b):
    return lax.dot_general(a, b, (((1,), (0,)), ((), ())), preferred_element_type=F32)


def _dot_nt(a, b):
    return lax.dot_general(a, b, (((1,), (1,)), ((), ())), preferred_element_type=F32)


def _rows8(v):
    return jnp.sum(v.reshape(v.shape[0] // 8, 8, v.shape[1]), axis=0)


def _gelu(x, with_grad=True):
    x2 = x * x
    t = jnp.tanh(x * (GELU_K0 + (GELU_K0 * GELU_K1) * x2))
    half = 0.5 + 0.5 * t
    g = x * half
    if not with_grad:
        return g
    return g, half + ((0.5 * x) * (1.0 - t * t)) * (GELU_K0 + (3.0 * GELU_K0 * GELU_K1) * x2)


def _silu(z, with_grad=True):
    s = jax.nn.sigmoid(z)
    zs = z * s
    return (zs, s + zs * (1.0 - s)) if with_grad else zs


def _rms(v):
    r = lax.rsqrt(jnp.mean(v * v, axis=-1, keepdims=True) + EPS)
    return v * r, r


def _rms_bwd(dy_g, vh, r):
    return r * (dy_g - vh * jnp.mean(dy_g * vh, axis=-1, keepdims=True))


def _place():
    return lax.axis_index("x"), lax.axis_index("y"), lax.axis_index("c")


def _other_chips(x, y):
    return [(1 - x, y), (x, 1 - y), (1 - x, 1 - y)]


def _pow2_divisor(n):
    return n & (-n)


def _shard_view(ref, axis, size, d):
    start = d * size
    align = _pow2_divisor(size)
    if align > 1:
        start = pl.multiple_of(start, align)
    idx = [slice(None)] * len(ref.shape)
    idx[axis] = pl.ds(start, size)
    return ref.at[tuple(idx)]


class _Plan:
    def __init__(self, ins, out_shape, sems, start, finish, mid=None, aliases=None, peers="all"):
        self.ins, self.out_shape, self.sems = list(ins), list(out_shape), list(sems)
        self.start, self.mid, self.finish = start, mid, finish
        self.aliases = dict(aliases or {})
        self.peers = peers


def _gather_plan(shards, specs):
    n = len(shards)
    n_copy = 7

    def copies(ins, outs, sems):
        send_sems, recv_sems, local_sems = sems
        x, y, c = _place()
        me, sibling = (x, y, c), (x, y, 1 - c)
        chips = _other_chips(x, y)

        def block(a, p):
            _, axis, size = specs[a]
            return _shard_view(outs[a], axis, size, 4 * p[0] + 2 * p[1] + p[2])

        def copy(a, k, p, to, src=None):
            return pltpu.make_async_remote_copy(
                src_ref=block(a, p) if src is None else src,
                dst_ref=block(a, p),
                send_sem=send_sems.at[a * n_copy + k],
                recv_sem=recv_sems.at[a * n_copy + k],
                device_id=to,
                device_id_type=MESH,
            )

        def mine():
            return [pltpu.make_async_copy(ins[a], block(a, me), local_sems.at[a]) for a in range(n)]

        def first():
            return [cp for a in range(n) for cp in
                    [copy(a, 0, me, sibling, src=ins[a])] + [copy(a, 1 + j, me, (*chip, c), src=ins[a]) for j, chip in enumerate(chips)]]

        def landed():
            return [copy(a, 1 + j, (*chip, c), me) for j, chip in enumerate(chips) for a in range(n)]

        def passed():
            return [copy(a, 4 + j, (*chip, c), sibling) for j, chip in enumerate(chips) for a in range(n)]

        def last():
            return [cp for a in range(n) for cp in
                    [copy(a, 0, sibling, me)] + [copy(a, 4 + j, (*chip, 1 - c), me) for j, chip in enumerate(chips)]]

        return mine, first, landed, passed, last

    def start(ins, outs, sems):
        mine, first, _, _, _ = copies(ins, outs, sems)
        for cp in mine() + first():
            cp.start()

    def mid(ins, outs, sems):
        _, _, landed, passed, _ = copies(ins, outs, sems)
        for got, fwd in zip(landed(), passed()):
            got.wait_recv()
            fwd.start()

    def finish(ins, outs, sems):
        mine, first, _, passed, last = copies(ins, outs, sems)
        for cp in last():
            cp.wait_recv()
        for cp in first() + passed():
            cp.wait_send()
        for cp in mine():
            cp.wait()

    return _Plan(
        shards,
        [jax.ShapeDtypeStruct(full, s.dtype) for s, (full, _, _) in zip(shards, specs)],
        [pltpu.SemaphoreType.DMA((n * n_copy,)), pltpu.SemaphoreType.DMA((n * n_copy,)), pltpu.SemaphoreType.DMA((n,))],
        start, finish, mid,
    )


def _sibling_plan(grads, specs):
    n = len(grads)

    def shard_shape(a):
        axis, size = specs[a]
        shp = list(grads[a].shape)
        shp[axis] = size
        return tuple(shp)

    def copies(ins, outs, sems):
        send_sems, recv_sems = sems
        x, y, c = _place()
        return [
            pltpu.make_async_remote_copy(
                src_ref=_shard_view(ins[a], specs[a][0], specs[a][1], 2 * q + (1 - c)),
                dst_ref=outs[a].at[q],
                send_sem=send_sems.at[a * N_CHIP + q],
                recv_sem=recv_sems.at[a * N_CHIP + q],
                device_id=(x, y, 1 - c),
                device_id_type=MESH,
            )
            for a in range(n) for q in range(N_CHIP)
        ]

    def start(ins, outs, sems):
        for cp in copies(ins, outs, sems):
            cp.start()

    def finish(ins, outs, sems):
        for cp in copies(ins, outs, sems):
            cp.wait()

    return _Plan(
        grads,
        [jax.ShapeDtypeStruct((N_CHIP, *shard_shape(a)), grads[a].dtype) for a in range(n)],
        [pltpu.SemaphoreType.DMA((n * N_CHIP,)), pltpu.SemaphoreType.DMA((n * N_CHIP,))],
        start, finish, peers="sibling",
    )


def _chip_plan(parts, full_rows=None, row0=None, into=None):
    n = len(parts)
    full_rows = [p.shape[1] for p in parts] if full_rows is None else full_rows
    row0 = [0] * n if row0 is None else row0

    def copies(ins, outs, sems):
        send_sems, recv_sems, local_sems = sems
        x, y, c = _place()
        my_chip = 2 * x + y
        chips = _other_chips(x, y)

        def land(a, q):
            return outs[a].at[q, pl.ds(row0[a], parts[a].shape[1]), :]

        def mine():
            return [pltpu.make_async_copy(ins[a].at[my_chip], land(a, my_chip), local_sems.at[a]) for a in range(n)]

        def remote(sending):
            return [
                pltpu.make_async_remote_copy(
                    src_ref=ins[a].at[2 * chip[0] + chip[1]],
                    dst_ref=land(a, my_chip if sending else 2 * chip[0] + chip[1]),
                    send_sem=send_sems.at[a * 3 + j], recv_sem=recv_sems.at[a * 3 + j],
                    device_id=(*chip, c), device_id_type=MESH)
                for a in range(n) for j, chip in enumerate(chips)
            ]

        return mine, remote

    def start(ins, outs, sems):
        mine, remote = copies(ins, outs, sems)
        for cp in mine() + remote(True):
            cp.start()

    def finish(ins, outs, sems):
        mine, remote = copies(ins, outs, sems)
        for cp in remote(False):
            cp.wait_recv()
        for cp in remote(True):
            cp.wait_send()
        for cp in mine():
            cp.wait()

    return _Plan(
        list(parts) + (list(into) if into is not None else []),
        [jax.ShapeDtypeStruct((N_CHIP, full_rows[a], parts[a].shape[2]), parts[a].dtype) for a in range(n)],
        [pltpu.SemaphoreType.DMA((n * 3,)), pltpu.SemaphoreType.DMA((n * 3,)), pltpu.SemaphoreType.DMA((n,))],
        start, finish,
        aliases={n + a: a for a in range(n)} if into is not None else None, peers="chips",
    )


def _call(body, *, name, operands, out_shape, grid=(), in_specs=None, out_specs=None, scratch_shapes=(),
          plans=(), aliases=None, prefetch=(), start_after_body=False, pass_on_at=0.6, body_peers=None):
    operands, out_shape, scratch_shapes = list(operands), list(out_shape), list(scratch_shapes)
    n_pf, n_in, n_out, n_scr = len(prefetch), len(operands), len(out_shape), len(scratch_shapes)
    in_vmem = pl.BlockSpec(memory_space=pltpu.VMEM)
    in_hbm = pl.BlockSpec(memory_space=pl.ANY)
    in_specs = [in_vmem] * n_in if in_specs is None else list(in_specs)
    out_specs = [in_vmem] * n_out if out_specs is None else list(out_specs)
    io_alias = {n_pf + i: o for i, o in (aliases or {}).items()}
    pos_in, pos_out = n_pf + n_in, n_out
    for p in plans:
        for i, o in p.aliases.items():
            io_alias[pos_in + i] = pos_out + o
        pos_in += len(p.ins)
        pos_out += len(p.out_shape)
    steps = math.prod(grid) if grid else 1
    mid_step = min(steps - 1, int(steps * pass_on_at))
    patterns = {p.peers for p in plans} | ({body_peers} if body_peers else set())
    pattern = None if not patterns else patterns.pop() if len(patterns) == 1 else "all"

    def shake_hands():
        x, y, c = _place()
        peers = [(x, y, 1 - c)] if pattern in ("sibling", "all") else []
        peers += [(*chip, c) for chip in _other_chips(x, y)] if pattern in ("chips", "all") else []
        barrier = pltpu.get_barrier_semaphore()
        for peer in peers:
            pl.semaphore_signal(barrier, inc=1, device_id=peer, device_id_type=MESH)
        pl.semaphore_wait(barrier, len(peers))

    def wrapped(*refs):
        refs = list(refs)
        take = lambda k: [refs.pop(0) for _ in range(k)]
        pf = take(n_pf)
        ins, p_ins = take(n_in), [take(len(p.ins)) for p in plans]
        outs, p_outs = take(n_out), [take(len(p.out_shape)) for p in plans]
        scr, p_sems = take(n_scr), [take(len(p.sems)) for p in plans]
        step = 0
        for ax, g in enumerate(grid):
            step = step * g + pl.program_id(ax)

        def hook(kind, at):
            todo = [(getattr(p, kind), a, b, c) for p, a, b, c in zip(plans, p_ins, p_outs, p_sems) if getattr(p, kind)]

            def run():
                for fn, a, b, c in todo:
                    fn(a, b, c)

            if todo and grid:
                pl.when(step == at)(run)
            elif todo:
                run()

        if pattern and grid:
            pl.when(step == 0)(shake_hands)
        elif pattern:
            shake_hands()
        if not start_after_body:
            hook("start", 0)
        if body is not None:
            body(*pf, *ins, *outs, *scr)
        if start_after_body:
            hook("start", 0)
        hook("mid", mid_step)
        hook("finish", steps - 1)

    layout = dict(
        grid=grid,
        in_specs=in_specs + [in_hbm] * sum(len(p.ins) for p in plans),
        out_specs=out_specs + [in_hbm] * sum(len(p.out_shape) for p in plans),
        scratch_shapes=scratch_shapes + [s for p in plans for s in p.sems],
    )
    if n_pf:
        layout = dict(grid_spec=pltpu.PrefetchScalarGridSpec(num_scalar_prefetch=n_pf, **layout))
    res = pl.pallas_call(
        wrapped,
        name=name,
        out_shape=out_shape + [o for p in plans for o in p.out_shape],
        input_output_aliases=io_alias,
        compiler_params=pltpu.CompilerParams(
            dimension_semantics=("arbitrary",) * len(grid) if grid else None,
            collective_id=BARRIER_ID[pattern] if pattern else None),
        **layout,
    )(*prefetch, *operands, *[a for p in plans for a in p.ins])
    res = list(res)
    outs, plan_outs = res[:n_out], []
    pos = n_out
    for p in plans:
        plan_outs.append(res[pos : pos + len(p.out_shape)])
        pos += len(p.out_shape)
    return outs, plan_outs


def _inproj(x, g, w_shard, order, plans=()):
    s = x.shape[0]
    tm = min(TM_INPROJ, s)
    nt = s // tm
    wide = 2 * SHARD_IN
    fwd_at = nt // 2

    def body(order_ref, x_ref, g_ref, w_hbm, proj_ref, h1_ref, h1t_ref, wfull_hbm,
             w_scr, h1_scr, send_sems, recv_sems, local_sems):
        k, i = pl.program_id(0), pl.program_id(1)
        px, py, c = _place()
        me, sibling = (px, py, c), (px, py, 1 - c)
        chips = _other_chips(px, py)

        def half(slot, core):
            return w_scr.at[slot, :, pl.ds(pl.multiple_of(core * SHARD_IN, 128), SHARD_IN)]

        def copy(n, slot, core, to, src=None):
            return pltpu.make_async_remote_copy(
                src_ref=half(slot, core) if src is None else src, dst_ref=half(slot, core),
                send_sem=send_sems.at[n], recv_sem=recv_sems.at[n], device_id=to, device_id_type=MESH)

        def mine():
            return pltpu.make_async_copy(w_hbm, half(0, c), local_sems.at[0])

        def first():
            return [copy(0, 0, c, sibling, src=w_hbm)] + [copy(1 + j, 1 + j, c, (*chips[j], c), src=w_hbm) for j in range(2)]

        def relay():
            to = (c * px + (1 - c) * (1 - px), c * (1 - py) + (1 - c) * py, c)
            return pltpu.make_async_remote_copy(
                src_ref=half(2 - c, c), dst_ref=half(3, c), send_sem=send_sems.at[3], recv_sem=recv_sems.at[3],
                device_id=to, device_id_type=MESH)

        def save(slot):
            cols = pl.ds(pl.multiple_of(order_ref[slot] * wide, 128), wide)
            return pltpu.make_async_copy(w_scr.at[slot], wfull_hbm.at[:, cols], local_sems.at[1 + slot])

        @pl.when((k == 0) & (i == 0))
        def _():
            mine().start()
            for cp in first():
                cp.start()
            mine().wait()
            copy(0, 0, 1 - c, me).wait_recv()

        for j in range(3):
            @pl.when((k == j + 1) & (i == 0))
            def _():
                copy(4 + j, 1 + j, 1 - c, me).wait_recv()

        @pl.when((k == 3) & (i == 0))
        def _():
            for slot in range(N_CHIP):
                save(slot).start()

        @pl.when(k == 0)
        def _():
            xh, _ = _rms(x_ref[...])
            h = xh * g_ref[...]
            h1_ref[...] = h.astype(BF16)
            h1t_ref[...] = h.T.astype(BF16)
            h1_scr[pl.ds(pl.multiple_of(i * tm, tm), tm), :] = h.astype(BF16)

        proj_ref[...] = _dot(h1_scr[pl.ds(pl.multiple_of(i * tm, tm), tm), :], w_scr[k]).astype(BF16)

        for j in range(3):
            @pl.when((k == 0) & (i == nt - 1) if j < 2 else (k == 2) & (i == fwd_at))
            def _():
                copy(1 + j, 1 + j, c, me).wait_recv()
                copy(4 + j, 1 + j, c, sibling).start()
                if j < 2:
                    pl.when(c == 1 - j)(lambda: relay().start())

        @pl.when((k == 3) & (i == nt - 1))
        def _():
            for cp in first() + [relay()] + [copy(4 + j, 1 + j, c, sibling) for j in range(3)]:
                cp.wait_send()
            for slot in range(N_CHIP):
                save(slot).wait()

    once = lambda k, i, order_ref: (jnp.where(k == 0, i, nt - 1), 0)
    return _call(
        body,
        name="inproj",
        prefetch=[order],
        operands=[x, g, w_shard],
        grid=(N_CHIP, nt),
        in_specs=[
            pl.BlockSpec((tm, D_MODEL), once),
            pl.BlockSpec((1, D_MODEL), lambda k, i, order_ref: (0, 0)),
            pl.BlockSpec(memory_space=pl.ANY),
        ],
        out_specs=[
            pl.BlockSpec((tm, wide), lambda k, i, order_ref: (i, order_ref[k])),
            pl.BlockSpec((tm, D_MODEL), once),
            pl.BlockSpec((D_MODEL, tm), lambda k, i, order_ref: (0, jnp.where(k == 0, i, nt - 1))),
            pl.BlockSpec(memory_space=pl.ANY),
        ],
        out_shape=[
            jax.ShapeDtypeStruct((s, IN_DIM), BF16),
            jax.ShapeDtypeStruct((s, D_MODEL), BF16),
            jax.ShapeDtypeStruct((D_MODEL, s), BF16),
            jax.ShapeDtypeStruct((D_MODEL, IN_DIM), BF16),
        ],
        scratch_shapes=[
            pltpu.VMEM((N_CHIP, D_MODEL, wide), BF16),
            pltpu.VMEM((s, D_MODEL), BF16),
            pltpu.SemaphoreType.DMA((7,)),
            pltpu.SemaphoreType.DMA((7,)),
            pltpu.SemaphoreType.DMA((1 + N_CHIP,)),
        ],
        plans=plans,
        start_after_body=True,
        pass_on_at=0.8,
        body_peers="all",
    )


def _rows_from_above(v, above):
    row = lax.broadcasted_iota(jnp.int32, (8, v.shape[1]), 0)
    r1, r2 = pltpu.roll(v, 1, 0), pltpu.roll(v, 2, 0)
    top1 = jnp.where(row == 0, above[7:8], r1[0:8])
    top2 = jnp.where(row == 0, above[6:7], jnp.where(row == 1, above[7:8], r2[0:8]))
    return jnp.concatenate([top1, r1[8:]], axis=0), jnp.concatenate([top2, r2[8:]], axis=0)


def _rows_from_below(v, below):
    n = v.shape[0]
    row = lax.broadcasted_iota(jnp.int32, (8, v.shape[1]), 0)
    r1, r2 = pltpu.roll(v, n - 1, 0), pltpu.roll(v, n - 2, 0)
    end1 = jnp.where(row == 7, below[0:1], r1[n - 8 :])
    end2 = jnp.where(row == 6, below[0:1], jnp.where(row == 7, below[1:2], r2[n - 8 :]))
    return jnp.concatenate([r1[: n - 8], end1], axis=0), jnp.concatenate([r2[: n - 8], end2], axis=0)


def _causal_mask():
    row = lax.broadcasted_iota(jnp.int32, (CHUNK, CHUNK), 0)
    col = lax.broadcasted_iota(jnp.int32, (CHUNK, CHUNK), 1)
    return row, row >= col


def _mixer_fwd(proj, conv_w, ln_g, ln_b, ws, bs_b, plans=()):
    s = proj.shape[0]
    nt = s // CHUNK

    def body(proj_ref, cw_ref, lng_ref, lnb_ref, ws_ref, bsb_ref, cat_ref, pcar_ref):
        @pl.when(pl.program_id(0) == 0)
        def _():
            pcar_ref[...] = jnp.zeros_like(pcar_ref)

        row, tril = _causal_mask()
        for j in range(8):
            cs = slice(CHUNK * j, CHUNK * (j + 1))

            def grp(k):
                return proj_ref[:, k * D_MODEL + CHUNK * j : k * D_MODEL + CHUNK * (j + 1)].astype(F32)

            w = cw_ref[:, cs]
            p = grp(1) * grp(2)
            p1, p2 = _rows_from_above(p, pcar_ref[:, cs])
            pcar_ref[:, cs] = p[CHUNK - 8 :]
            cv = w[0:1] * p2 + w[1:2] * p1 + w[2:3] * p
            sa = _silu(grp(3), with_grad=False)
            cat_ref[:, cs] = ((grp(0) * cv) * sa).astype(BF16)
            gu = _gelu(grp(4), with_grad=False)
            gv = _gelu(grp(5), with_grad=False)
            dv = gv - jnp.mean(gv, axis=-1, keepdims=True)
            vn = dv * lax.rsqrt(jnp.mean(dv * dv, axis=-1, keepdims=True) + EPS)
            vn = vn * lng_ref[:, cs] + lnb_ref[:, cs]
            wc = jnp.where(tril, ws_ref[j], 0.0).astype(BF16)
            sp = _dot(wc, vn.astype(BF16)) + bsb_ref[j]
            sb = _silu(grp(6), with_grad=False)
            cat_ref[:, D_MODEL + CHUNK * j : D_MODEL + CHUNK * (j + 1)] = ((gu * sp) * sb).astype(BF16)

    whole = lambda shape: pl.BlockSpec(shape, lambda i: (0,) * len(shape))
    (cat,), plan_outs = _call(
        body,
        name="mixer_fwd",
        operands=[proj, conv_w, ln_g, ln_b, ws, bs_b],
        grid=(nt,),
        in_specs=[
            pl.BlockSpec((CHUNK, IN_DIM), lambda i: (i, 0)),
            whole((8, D_MODEL)),
            whole((1, D_MODEL)),
            whole((1, D_MODEL)),
            whole((8, CHUNK, CHUNK)),
            whole((8, CHUNK, CHUNK)),
        ],
        out_specs=[pl.BlockSpec((CHUNK, MIX_DIM), lambda i: (i, 0))],
        out_shape=[jax.ShapeDtypeStruct((s, MIX_DIM), BF16)],
        scratch_shapes=[pltpu.VMEM((8, D_MODEL), F32)],
        plans=plans,
        pass_on_at=0.85,
    )
    return cat, plan_outs


def _mixer_bwd(proj, dcat, conv_w, ln_g, ln_b, ws, bs_b, plans=()):
    s = proj.shape[0]
    nt = s // CHUNK

    def body(proj_ref, halo_ref, dcat_ref, cw_ref, lng_ref, lnb_ref, ws_ref, bsb_ref,
             dproj_ref, dcw_ref, dlng_ref, dlnb_ref, dws_ref, dbs_ref,
             car_ref, acc_cw, acc_lng, acc_lnb, acc_bs, wc_scr, wct_scr):
        i = pl.program_id(0)
        tile = nt - 1 - i
        row, tril = _causal_mask()

        @pl.when(i == 0)
        def _():
            for j in range(8):
                wc = jnp.where(tril, ws_ref[j], 0.0)
                wc_scr[j] = wc.astype(BF16)
                wct_scr[j] = wc.T.astype(BF16)
            car_ref[...] = jnp.zeros_like(car_ref)
            acc_cw[...] = jnp.zeros_like(acc_cw)
            acc_lng[...] = jnp.zeros_like(acc_lng)
            acc_lnb[...] = jnp.zeros_like(acc_lnb)
            acc_bs[...] = jnp.zeros_like(acc_bs)
            dws_ref[...] = jnp.zeros_like(dws_ref)

        has_prev = jnp.where(tile > 0, 1.0, 0.0).astype(F32)
        for j in range(8):
            cs = slice(CHUNK * j, CHUNK * (j + 1))

            def col(k):
                return slice(k * D_MODEL + CHUNK * j, k * D_MODEL + CHUNK * (j + 1))

            def grp(k):
                return proj_ref[:, col(k)].astype(F32)

            w = cw_ref[:, cs]
            gb, gc, xa = grp(0), grp(1), grp(2)
            p = gc * xa
            pprev = halo_ref[:, col(1)].astype(F32) * halo_ref[:, col(2)].astype(F32) * has_prev
            p1, p2 = _rows_from_above(p, pprev[8:16])
            cv = w[0:1] * p2 + w[1:2] * p1 + w[2:3] * p
            za = grp(3)
            sa, dsa = _silu(za)
            da = dcat_ref[:, cs].astype(F32)
            da_gb = da * gb
            dproj_ref[:, col(0)] = ((da * sa) * cv).astype(BF16)
            dproj_ref[:, col(3)] = ((da_gb * cv) * dsa).astype(BF16)
            dcv = da_gb * sa
            d1, d2 = _rows_from_below(dcv, car_ref[:, cs])
            car_ref[:, cs] = dcv[0:8]
            dp = w[2:3] * dcv + w[1:2] * d1 + w[0:1] * d2
            dproj_ref[:, col(1)] = (dp * xa).astype(BF16)
            dproj_ref[:, col(2)] = (dp * gc).astype(BF16)
            acc_cw[0, :, cs] += _rows8(dcv * p2)
            acc_cw[1, :, cs] += _rows8(dcv * p1)
            acc_cw[2, :, cs] += _rows8(dcv * p)
            gu, dgu = _gelu(grp(4))
            gv, dgv = _gelu(grp(5))
            dv = gv - jnp.mean(gv, axis=-1, keepdims=True)
            rstd = lax.rsqrt(jnp.mean(dv * dv, axis=-1, keepdims=True) + EPS)
            vnh = dv * rstd
            lng = lng_ref[:, cs]
            vnb = (vnh * lng + lnb_ref[:, cs]).astype(BF16)
            sp = _dot(wc_scr[j], vnb) + bsb_ref[j]
            zb = grp(6)
            sb, dsb = _silu(zb)
            db = dcat_ref[:, D_MODEL + CHUNK * j : D_MODEL + CHUNK * (j + 1)].astype(F32)
            db_sp, db_gu = db * sp, db * gu
            dproj_ref[:, col(4)] = ((db_sp * sb) * dgu).astype(BF16)
            dproj_ref[:, col(6)] = ((db_sp * gu) * dsb).astype(BF16)
            dsp = db_gu * sb
            acc_bs[j] += dsp
            dspb = dsp.astype(BF16)
            dws_ref[j] += _dot_nt(dspb, vnb)
            dvn = _dot(wct_scr[j], dspb)
            acc_lnb[:, cs] += _rows8(dvn)
            acc_lng[:, cs] += _rows8(dvn * vnh)
            dvh = dvn * lng
            dgvv = rstd * (dvh - jnp.mean(dvh, axis=-1, keepdims=True) - vnh * jnp.mean(dvh * vnh, axis=-1, keepdims=True))
            dproj_ref[:, col(5)] = (dgvv * dgv).astype(BF16)

        @pl.when(i == nt - 1)
        def _():
            dlng_ref[...] = jnp.sum(acc_lng[...], axis=0, keepdims=True)
            dlnb_ref[...] = jnp.sum(acc_lnb[...], axis=0, keepdims=True)
            dcw_ref[...] = jnp.zeros_like(dcw_ref)
            for k in range(3):
                dcw_ref[k : k + 1, :] = jnp.sum(acc_cw[k], axis=0, keepdims=True)
            for j in range(8):
                dws_ref[j] = jnp.where(tril, dws_ref[j], 0.0)
                dbs_ref[:, CHUNK * j : CHUNK * (j + 1)] = jnp.sum(acc_bs[j].T, axis=0, keepdims=True)

    whole = lambda shape: pl.BlockSpec(shape, lambda i: (0,) * len(shape))
    halo_rows = 16
    per = CHUNK // halo_rows
    return _call(
        body,
        name="mixer_bwd",
        operands=[proj, proj, dcat, conv_w, ln_g, ln_b, ws, bs_b],
        plans=plans,
        grid=(nt,),
        in_specs=[
            pl.BlockSpec((CHUNK, IN_DIM), lambda i: (nt - 1 - i, 0)),
            pl.BlockSpec((halo_rows, IN_DIM), lambda i: (jnp.maximum((nt - 1 - i) * per - 1, 0), 0)),
            pl.BlockSpec((CHUNK, MIX_DIM), lambda i: (nt - 1 - i, 0)),
            whole((8, D_MODEL)),
            whole((1, D_MODEL)),
            whole((1, D_MODEL)),
            whole((8, CHUNK, CHUNK)),
            whole((8, CHUNK, CHUNK)),
        ],
        out_specs=[
            pl.BlockSpec((CHUNK, IN_DIM), lambda i: (nt - 1 - i, 0)),
            whole((8, D_MODEL)),
            whole((1, D_MODEL)),
            whole((1, D_MODEL)),
            whole((8, CHUNK, CHUNK)),
            whole((1, D_MODEL)),
        ],
        out_shape=[
            jax.ShapeDtypeStruct((s, IN_DIM), BF16),
            jax.ShapeDtypeStruct((8, D_MODEL), F32),
            jax.ShapeDtypeStruct((1, D_MODEL), F32),
            jax.ShapeDtypeStruct((1, D_MODEL), F32),
            jax.ShapeDtypeStruct((8, CHUNK, CHUNK), F32),
            jax.ShapeDtypeStruct((1, D_MODEL), F32),
        ],
        scratch_shapes=[
            pltpu.VMEM((8, D_MODEL), F32),
            pltpu.VMEM((3, 8, D_MODEL), F32),
            pltpu.VMEM((8, D_MODEL), F32),
            pltpu.VMEM((8, D_MODEL), F32),
            pltpu.VMEM((8, CHUNK, CHUNK), F32),
            pltpu.VMEM((8, CHUNK, CHUNK), BF16),
            pltpu.VMEM((8, CHUNK, CHUNK), BF16),
        ],
    )


def _kv_fwd(mem, g_mem, w_kv):
    def body(mem_ref, g_ref, w_ref, k_ref, vv_ref, mt_ref):
        mh, _ = _rms(mem_ref[...])
        m = mh * g_ref[...]
        kv = _dot(m.astype(BF16), w_ref[...])
        k_ref[...] = kv[:, :D_MODEL].astype(BF16)
        vv_ref[...] = kv[:, D_MODEL:].astype(BF16)
        mt_ref[...] = m.T.astype(BF16)

    return pl.pallas_call(
        body,
        name="kv_fwd",
        out_shape=[
            jax.ShapeDtypeStruct((MEM_LEN, D_MODEL), BF16),
            jax.ShapeDtypeStruct((MEM_LEN, D_MODEL), BF16),
            jax.ShapeDtypeStruct((D_MODEL, MEM_LEN), BF16),
        ],
    )(mem, g_mem, w_kv)


def _kv_bwd(mem, w_kv, mt, dk, dvv):
    def body(mem_ref, w_ref, mt_ref, dk_ref, dvv_ref, dw_ref, dwb_ref, dg_ref):
        dkv = jnp.concatenate([dk_ref[...], dvv_ref[...]], axis=1).astype(BF16)
        dw = _dot(mt_ref[...], dkv)
        dw_ref[...] = dw
        dwb_ref[...] = dw.astype(BF16)
        dm = _dot_nt(dkv, w_ref[...])
        mh, _ = _rms(mem_ref[...])
        dg_ref[...] = jnp.sum(dm * mh, axis=0, keepdims=True)

    return pl.pallas_call(
        body,
        name="kv_bwd",
        out_shape=[
            jax.ShapeDtypeStruct((D_MODEL, 2 * D_MODEL), F32),
            jax.ShapeDtypeStruct((D_MODEL, 2 * D_MODEL), BF16),
            jax.ShapeDtypeStruct((1, D_MODEL), F32),
        ],
    )(mem, w_kv, mt, dk, dvv)


def _attn(x, cat, target, w_out, w_q, w_xo, k, vv, g_x, g_f):
    s = x.shape[0]
    tm = min(TM_ATTN, s)
    sub = min(SUB_ATTN, tm)
    nt = s // tm
    scale = 1.0 / math.sqrt(X_HEAD_DIM)

    def body(x_ref, cat_ref, t_ref, wout_ref, wq_ref, wxo_ref, k_ref, vv_ref, gx_ref, gf_ref,
             dcat_ref, actt_ref, cot_ref,
             dk_ref, dvv_ref, dgf_ref, dgx_ref, loss_ref,
             p_scr, q_scr, o_scr, dq_scr, acc_gf, acc_gx, acc_loss):
        i = pl.program_id(0)

        @pl.when(i == 0)
        def _():
            dk_ref[...] = jnp.zeros_like(dk_ref)
            dvv_ref[...] = jnp.zeros_like(dvv_ref)
            acc_gf[...] = jnp.zeros_like(acc_gf)
            acc_gx[...] = jnp.zeros_like(acc_gx)
            acc_loss[...] = jnp.zeros_like(acc_loss)

        subs = [slice(u * sub, (u + 1) * sub) for u in range(tm // sub)]
        heads = [slice(X_HEAD_DIM * h, X_HEAD_DIM * (h + 1)) for h in range(X_HEADS)]
        gx, gf = gx_ref[...], gf_ref[...]
        x1, x1h, r2 = [], [], []
        for rs in subs:
            cat = cat_ref[rs, :]
            actt_ref[0:MIX_DIM, rs] = cat.astype(F32).T.astype(BF16)
            v = x_ref[rs, :] + _dot(cat, wout_ref[...])
            vh, r = _rms(v)
            h2 = vh * gx
            actt_ref[MIX_DIM : MIX_DIM + D_MODEL, rs] = h2.T.astype(BF16)
            q_scr[rs, :] = _dot(h2.astype(BF16), wq_ref[...]).astype(BF16)
            x1.append(v), x1h.append(vh), r2.append(r)
        for rs in subs:
            for hs in heads:
                sc = _dot_nt(q_scr[rs, hs], k_ref[:, hs]) * scale
                e = jnp.exp(sc - jnp.max(sc, axis=-1, keepdims=True))
                p = e / jnp.sum(e, axis=-1, keepdims=True)
                p_scr[rs, hs] = p
                o_scr[rs, hs] = _dot(p.astype(BF16), vv_ref[:, hs])
        dx2, sq, d_gf = [], None, None
        for u, rs in enumerate(subs):
            o = o_scr[rs, :]
            actt_ref[MIX_DIM + D_MODEL :, rs] = o.T.astype(BF16)
            x2 = x1[u] + _dot(o.astype(BF16), wxo_ref[...])
            x2h, r3 = _rms(x2)
            err = x2h * gf - t_ref[rs, :]
            dy = err * (1.0 / D_MODEL)
            sq = _rows8(err * err) if sq is None else sq + _rows8(err * err)
            d_gf = _rows8(dy * x2h) if d_gf is None else d_gf + _rows8(dy * x2h)
            dx2.append(_rms_bwd(dy * gf, x2h, r3))
        acc_loss[...] += sq
        acc_gf[...] += d_gf
        d_vv, d_k = [None] * X_HEADS, [None] * X_HEADS
        for u, rs in enumerate(subs):
            dx2b = dx2[u].astype(BF16)
            cot_ref[2, rs, :] = dx2b
            dob = _dot_nt(dx2b, wxo_ref[...]).astype(BF16)
            for h, hs in enumerate(heads):
                p = p_scr[rs, hs]
                dp = _dot_nt(dob[:, hs], vv_ref[:, hs])
                ds = p * (dp - jnp.sum(dp * p, axis=-1, keepdims=True)) * scale
                dq_scr[rs, hs] = _dot(ds.astype(BF16), k_ref[:, hs]).astype(BF16)
                dvv_h = _dot(p.T.astype(BF16), dob[:, hs])
                dk_h = _dot(ds.T.astype(BF16), q_scr[rs, hs])
                d_vv[h] = dvv_h if d_vv[h] is None else d_vv[h] + dvv_h
                d_k[h] = dk_h if d_k[h] is None else d_k[h] + dk_h
        for h, hs in enumerate(heads):
            dvv_ref[:, hs] += d_vv[h]
            dk_ref[:, hs] += d_k[h]
        d_gx = None
        for u, rs in enumerate(subs):
            dqb = dq_scr[rs, :]
            cot_ref[1, rs, :] = dqb
            dh2 = _dot_nt(dqb, wq_ref[...])
            d_gx = _rows8(dh2 * x1h[u]) if d_gx is None else d_gx + _rows8(dh2 * x1h[u])
            dx1 = dx2[u] + _rms_bwd(dh2 * gx, x1h[u], r2[u])
            dx1b = dx1.astype(BF16)
            cot_ref[0, rs, :] = dx1b
            dcat_ref[rs, :] = _dot_nt(dx1b, wout_ref[...]).astype(BF16)
        acc_gx[...] += d_gx

        @pl.when(i == nt - 1)
        def _():
            dgf_ref[...] = jnp.sum(acc_gf[...], axis=0, keepdims=True)
            dgx_ref[...] = jnp.sum(acc_gx[...], axis=0, keepdims=True)
            loss_ref[...] = jnp.sum(acc_loss[...], axis=0, keepdims=True)

    whole = lambda shape: pl.BlockSpec(shape, lambda i: (0,) * len(shape))
    once = lambda shape: pl.BlockSpec(shape, lambda i: (0,) * len(shape), pipeline_mode=pl.Buffered(1))
    rows = lambda width: pl.BlockSpec((tm, width), lambda i: (i, 0))
    cols = lambda height: pl.BlockSpec((height, tm), lambda i: (0, i))
    vec = jax.ShapeDtypeStruct((1, D_MODEL), F32)
    return pl.pallas_call(
        body,
        name="attn",
        grid=(nt,),
        in_specs=[
            rows(D_MODEL), rows(MIX_DIM), rows(D_MODEL),
            once((MIX_DIM, D_MODEL)), once((D_MODEL, D_MODEL)), once((D_MODEL, D_MODEL)),
            once((MEM_LEN, D_MODEL)), once((MEM_LEN, D_MODEL)),
            whole((1, D_MODEL)), whole((1, D_MODEL)),
        ],
        out_specs=[
            rows(MIX_DIM), cols(2 * MIX_DIM),
            pl.BlockSpec((3, tm, D_MODEL), lambda i: (0, i, 0)),
            whole((MEM_LEN, D_MODEL)), whole((MEM_LEN, D_MODEL)),
            whole((1, D_MODEL)), whole((1, D_MODEL)), whole((1, D_MODEL)),
        ],
        out_shape=[
            jax.ShapeDtypeStruct((s, MIX_DIM), BF16),
            jax.ShapeDtypeStruct((2 * MIX_DIM, s), BF16),
            jax.ShapeDtypeStruct((3, s, D_MODEL), BF16),
            jax.ShapeDtypeStruct((MEM_LEN, D_MODEL), F32),
            jax.ShapeDtypeStruct((MEM_LEN, D_MODEL), F32),
            vec, vec, vec,
        ],
        scratch_shapes=[
            pltpu.VMEM((tm, D_MODEL), F32),
            pltpu.VMEM((tm, D_MODEL), BF16),
            pltpu.VMEM((tm, D_MODEL), F32),
            pltpu.VMEM((tm, D_MODEL), BF16),
            pltpu.VMEM((8, D_MODEL), F32),
            pltpu.VMEM((8, D_MODEL), F32),
            pltpu.VMEM((8, D_MODEL), F32),
        ],
        compiler_params=pltpu.CompilerParams(dimension_semantics=("arbitrary",)),
    )(x, cat, target, w_out, w_q, w_xo, k, vv, g_x, g_f)


def _matmul_f32(name, a, b, rows=None, plans=()):
    first, m = (0, a.shape[0]) if rows is None else rows
    kk = a.shape[1]
    n = b.shape[1]
    tm, tn, tk = min(m, TM_DW), min(n, D_MODEL), min(kk, TK_DW)
    nk = kk // tk
    i0 = first // tm

    def body(a_ref, b_ref, o_ref, ob_ref):
        prod = _dot(a_ref[...], b_ref[...])
        if nk == 1:
            o_ref[...] = prod
            ob_ref[...] = prod.astype(BF16)
            return
        kid = pl.program_id(2)

        @pl.when(kid == 0)
        def _():
            o_ref[...] = prod

        @pl.when(kid > 0)
        def _():
            o_ref[...] += prod

        @pl.when(kid == nk - 1)
        def _():
            ob_ref[...] = o_ref[...].astype(BF16)

    out_spec = pl.BlockSpec((tm, tn), lambda i, j, k: (i, j))
    outs, plan_outs = _call(
        body,
        name=name,
        operands=[a, b],
        grid=(m // tm, n // tn, nk),
        in_specs=[pl.BlockSpec((tm, tk), lambda i, j, k: (i + i0, k)), pl.BlockSpec((tk, tn), lambda i, j, k: (k, j))],
        out_specs=[out_spec, out_spec],
        out_shape=[jax.ShapeDtypeStruct((m, n), F32), jax.ShapeDtypeStruct((m, n), BF16)],
        plans=plans,
    )
    return outs, plan_outs


def _dw_attn(actt, cot, plans=()):
    s = actt.shape[1]
    tm = TM_DW
    counts = [MIX_DIM // tm, D_MODEL // tm, D_MODEL // tm]
    starts = [0, counts[0], counts[0] + counts[1]]
    which = lambda i: jnp.where(i < starts[1], 0, jnp.where(i < starts[2], 1, 2))

    def body(a_ref, b_ref, *outs):
        i = pl.program_id(0)
        prod = _dot(a_ref[...], b_ref[0])
        for m in range(3):
            @pl.when(which(i) == m)
            def _():
                outs[2 * m][...] = prod
                outs[2 * m + 1][...] = prod.astype(BF16)

    out_specs, out_shape = [], []
    for m in range(3):
        spec = pl.BlockSpec((tm, D_MODEL), lambda i, m=m: (jnp.clip(i - starts[m], 0, counts[m] - 1), 0))
        out_specs += [spec, spec]
        out_shape += [jax.ShapeDtypeStruct((counts[m] * tm, D_MODEL), F32), jax.ShapeDtypeStruct((counts[m] * tm, D_MODEL), BF16)]
    outs, plan_outs = _call(
        body,
        name="dw_attn",
        operands=[actt, cot],
        grid=(sum(counts),),
        in_specs=[pl.BlockSpec((tm, s), lambda i: (i, 0)), pl.BlockSpec((1, s, D_MODEL), lambda i: (which(i), 0, 0))],
        out_specs=out_specs,
        out_shape=out_shape,
        plans=plans,
    )
    return [outs[2 * m : 2 * m + 2] for m in range(3)], plan_outs


def _dh1(name, dproj, w_in, x, dx1, g, tiles, into=None, after=(), plans=()):
    s = x.shape[0]
    tm = min(TM_DH1, s)
    t0, nt = tiles[0], tiles[1] - tiles[0]

    def body(dp_ref, w_ref, x_ref, dx1_ref, g_ref, *rest):
        gx_ref, dg_ref, acc_g = rest[-3:]
        i = pl.program_id(0)

        @pl.when(i == 0)
        def _():
            acc_g[...] = jnp.zeros_like(acc_g)

        dh1 = _dot_nt(dp_ref[...], w_ref[...])
        xh, r = _rms(x_ref[...])
        acc_g[...] += _rows8(dh1 * xh)
        gx_ref[...] = dx1_ref[0].astype(F32) + _rms_bwd(dh1 * g_ref[...], xh, r)

        @pl.when(i == nt - 1)
        def _():
            dg_ref[...] = jnp.sum(acc_g[...], axis=0, keepdims=True)

    rows = pl.BlockSpec((tm, D_MODEL), lambda i: (i + t0, 0))
    return _call(
        body,
        name=name,
        operands=[dproj, w_in, x, dx1, g] + ([into] if into is not None else []) + list(after),
        grid=(nt,),
        in_specs=[
            pl.BlockSpec((tm, IN_DIM), lambda i: (i + t0, 0)),
            pl.BlockSpec((D_MODEL, IN_DIM), lambda i: (0, 0), pipeline_mode=pl.Buffered(1)),
            rows,
            pl.BlockSpec((1, tm, D_MODEL), lambda i: (0, i + t0, 0)),
            pl.BlockSpec((1, D_MODEL), lambda i: (0, 0)),
        ] + [pl.BlockSpec(memory_space=pl.ANY)] * ((into is not None) + len(after)),
        out_specs=[rows, pl.BlockSpec((1, D_MODEL), lambda i: (0, 0))],
        out_shape=[jax.ShapeDtypeStruct((s, D_MODEL), F32), jax.ShapeDtypeStruct((1, D_MODEL), F32)],
        scratch_shapes=[pltpu.VMEM((8, D_MODEL), F32)],
        plans=plans,
        aliases={5: 0} if into is not None else None,
    )


def _pair_sum(name, grad, got, axis, size, core, plans=()):
    shard = list(grad.shape)
    shard[axis] = size
    r, cdim = shard
    tr = min(r, TR_PAIR_SUM)
    nr = r // tr

    def body(core_ref, g_ref, got_ref, o_ref):
        o_ref[0] = (g_ref[...] + got_ref[0].astype(F32)).astype(BF16)

    if axis == 1:
        g_map = lambda q, t, core_ref: (t, 2 * q + core_ref[0])
    else:
        g_map = lambda q, t, core_ref: ((2 * q + core_ref[0]) * nr + t, 0)
    (out,), plan_outs = _call(
        body,
        name=name,
        prefetch=[core],
        operands=[grad, got],
        grid=(N_CHIP, nr),
        in_specs=[
            pl.BlockSpec((tr, cdim), g_map),
            pl.BlockSpec((1, tr, cdim), lambda q, t, core_ref: (q, t, 0)),
        ],
        out_specs=[pl.BlockSpec((1, tr, cdim), lambda q, t, core_ref: (q, t, 0))],
        out_shape=[jax.ShapeDtypeStruct((N_CHIP, r, cdim), BF16)],
        plans=plans,
    )
    return (out, plan_outs) if plans else out


def _adam_step(w, m, v, g):
    c1 = 1.0 / (1.0 - ADAM_B1**ADAM_STEP)
    c2 = 1.0 / (1.0 - ADAM_B2**ADAM_STEP)
    m_new = ADAM_B1 * m + (1.0 - ADAM_B1) * g
    v_new = ADAM_B2 * v + (1.0 - ADAM_B2) * (g * g)
    return -ADAM_LR * ((m_new * c1) / (jnp.sqrt(v_new * c2) + ADAM_EPS) + ADAM_WD * w), m_new, v_new


def _update_small(dev, vec_w, vec_m, vec_v, conv_w, conv_m, conv_v, small, landed):
    n_vec = len(vec_w)

    def body(dev_ref, *refs):
        refs = list(refs)
        take = lambda k: [refs.pop(0) for _ in range(k)]
        w_refs, m_refs, v_refs = take(n_vec), take(n_vec), take(n_vec)
        cw_ref, cm_ref, cv_ref, small_ref, land_ref, small_conv_ref, land_conv_ref = take(7)
        vec_outs, conv_outs, (loss_ref,) = take(4 * n_vec), take(4), take(1)

        def total(own_ref, others_ref):
            acc = None
            for d in range(N_DEV):
                part = jnp.where(dev_ref[0] == d, own_ref[0], others_ref[d])
                acc = part if acc is None else acc + part
            return acc

        g_all = total(small_ref, land_ref)
        for i in range(n_vec):
            g = g_all[i : i + 1]
            delta, m_new, v_new = _adam_step(w_refs[i][...], m_refs[i][...], v_refs[i][...], g)
            for ref, val in zip(vec_outs[4 * i : 4 * i + 4], (g, delta, m_new, v_new)):
                ref[...] = val
        g_conv = total(small_conv_ref, land_conv_ref)
        for tap in range(3):
            g = g_conv[tap : tap + 1]
            delta, m_new, v_new = _adam_step(cw_ref[tap], cm_ref[tap], cv_ref[tap], g)
            for ref, val in zip(conv_outs, (g, delta, m_new, v_new)):
                ref[tap] = val
        loss_ref[...] = jnp.broadcast_to(0.5 * jnp.sum(g_all[7:8], axis=-1, keepdims=True) / D_MODEL, loss_ref.shape)

    vec_spec = pl.BlockSpec((1, D_MODEL), lambda i, dev_ref: (0, 0))
    conv_spec = pl.BlockSpec((3, 1, CHUNK), lambda i, dev_ref: (0, 0, 0))
    vec_shape = jax.ShapeDtypeStruct((1, D_MODEL), F32)
    conv_shape = jax.ShapeDtypeStruct((3, 1, CHUNK), F32)
    outs, _ = _call(
        body,
        name="update_small",
        prefetch=[dev],
        operands=list(vec_w) + list(vec_m) + list(vec_v) + [conv_w, conv_m, conv_v, small, landed, small, landed],
        grid=(1,),
        in_specs=[vec_spec] * (3 * n_vec) + [conv_spec] * 3 + [
            pl.BlockSpec((1, 16, D_MODEL), lambda i, dev_ref: (0, 0, 0)),
            pl.BlockSpec((N_DEV, 16, D_MODEL), lambda i, dev_ref: (0, 0, 0)),
            pl.BlockSpec((1, 8, CHUNK), lambda i, dev_ref: (0, 1, dev_ref[0])),
            pl.BlockSpec((N_DEV, 8, CHUNK), lambda i, dev_ref: (0, 1, dev_ref[0])),
        ],
        out_specs=[vec_spec] * (4 * n_vec) + [conv_spec] * 4 + [pl.BlockSpec((1, CHUNK), lambda i, dev_ref: (0, 0))],
        out_shape=[vec_shape] * (4 * n_vec) + [conv_shape] * 4 + [jax.ShapeDtypeStruct((1, CHUNK), F32)],
    )
    return [outs[4 * i : 4 * i + 4] for i in range(n_vec)], outs[4 * n_vec : 4 * n_vec + 4], outs[-1]


def _adamw(name, groups, steps, own_slot=None, plans=()):
    lists = lambda p: list(p) if isinstance(p, (list, tuple)) else [p]
    groups = [(g[0], g[1], g[2], lists(g[3]), lists(g[4]) if len(g) > 4 else []) for g in groups]
    counts = [3 + len(g[3]) + len(g[4]) for g in groups]
    use_slot = own_slot is not None

    def first_tiles(w, parts):
        firsts, first = [], 0
        for p in parts:
            firsts.append(first)
            first += p.shape[1] // (w.shape[0] // steps)
        return firsts

    def update(t, slot_ref, firsts, w_ref, m_ref, v_ref, p_refs, o_refs, g_ref, d_ref, nm_ref, nv_ref):
        g = None
        for k, p_ref in enumerate(p_refs):
            gk = None
            for q in range(p_ref.shape[0]):
                part = p_ref[q].astype(F32)
                if o_refs:
                    part = jnp.where(slot_ref[0] == q, o_refs[k][0].astype(F32), part)
                gk = part if gk is None else gk + part
            g = gk if g is None else jnp.where(t >= firsts[k], gk, g)
        g_ref[...] = g
        d_ref[...], nm_ref[...], nv_ref[...] = _adam_step(w_ref[...], m_ref[...], v_ref[...], g)

    def body(*refs):
        refs = list(refs)
        slot_ref = refs.pop(0) if use_slot else None
        t = pl.program_id(0)
        ins, outs = refs[: sum(counts)], refs[sum(counts) :]
        for i, (w, _, _, parts, own) in enumerate(groups):
            mine = ins[sum(counts[:i]) : sum(counts[: i + 1])]
            update(t, slot_ref, first_tiles(w, parts), *mine[:3], mine[3 : 3 + len(parts)], mine[3 + len(parts) :],
                   *outs[4 * i : 4 * i + 4])

    operands, in_specs, out_specs, out_shape = [], [], [], []
    for w, m, v, parts, own in groups:
        r, cdim = w.shape
        tr = r // steps
        firsts = first_tiles(w, parts)

        def tile_of(k, t, firsts=firsts, parts=parts, tr=tr):
            return jnp.clip(t - firsts[k], 0, parts[k].shape[1] // tr - 1)

        blk = pl.BlockSpec((tr, cdim), lambda t, *_: (t, 0))
        operands += [w, m, v] + parts + own
        in_specs += [blk, blk, blk]
        in_specs += [pl.BlockSpec((p.shape[0], tr, cdim), lambda t, *_, k=k, tile_of=tile_of: (0, tile_of(k, t), 0)) for k, p in enumerate(parts)]
        in_specs += [pl.BlockSpec((1, tr, cdim), lambda t, slot_ref, k=k, tile_of=tile_of: (slot_ref[0], tile_of(k, t), 0)) for k in range(len(own))]
        out_specs += [blk] * 4
        out_shape += [jax.ShapeDtypeStruct((r, cdim), F32)] * 4
    outs, plan_outs = _call(
        body,
        name=name,
        prefetch=[own_slot] if use_slot else [],
        operands=operands,
        grid=(steps,),
        in_specs=in_specs,
        out_specs=out_specs,
        out_shape=out_shape,
        plans=plans,
    )
    return [outs[4 * i : 4 * i + 4] for i in range(len(groups))], plan_outs


def _chip_routes(n):
    x, y, c = _place()
    my_chip = 2 * x + y
    return [(a, (*chip, c), 2 * chip[0] + chip[1], my_chip, 2 * chip[0] + chip[1])
            for a in range(n) for chip in _other_chips(x, y)]


def _gather_routes(n):
    x, y, c = _place()
    flip = lambda v, f: 1 - v if f else v
    peers = [(flip(x, fx), flip(y, fy), flip(c, fc)) for fx in (0, 1) for fy in (0, 1) for fc in (0, 1) if fx + fy + fc]
    return [(0, p, 0, 4 * x + 2 * y + c, 4 * p[0] + 2 * p[1] + p[2]) for p in peers]


_chip_routes.copies_per_array = N_CHIP - 1
_gather_routes.copies_per_array = N_DEV - 1


def _split_start(name, routes, srcs, land_shapes):
    n = len(srcs)
    hbm = pl.BlockSpec(memory_space=pltpu.HBM)
    sem = pl.BlockSpec(memory_space=pltpu.SEMAPHORE)

    def body(*refs):
        src_refs, land_refs = refs[:n], refs[n : 2 * n]
        send_sems, recv_sems = refs[2 * n], refs[2 * n + 1]
        token = refs[-1]
        for k, (a, peer, src_slot, there, _) in enumerate(routes(n)):
            pltpu.make_async_remote_copy(
                src_ref=src_refs[a].at[src_slot], dst_ref=land_refs[a].at[there],
                send_sem=send_sems.at[k], recv_sem=recv_sems.at[k], device_id=peer, device_id_type=MESH).start()
        token[...] = jnp.zeros_like(token)

    in_hbm = [pltpu.with_memory_space_constraint(p, pltpu.HBM) for p in srcs]
    in_hbm += [pltpu.with_memory_space_constraint(lax.empty(shape, p.dtype), pltpu.HBM) for p, shape in zip(srcs, land_shapes)]
    n_copies = routes.copies_per_array * n
    res = pl.pallas_call(
        body,
        name=name,
        out_shape=[pltpu.SemaphoreType.DMA((n_copies,)), pltpu.SemaphoreType.DMA((n_copies,))]
        + [pltpu.HBM(p.shape, p.dtype) for p in srcs] + [pltpu.HBM(shape, p.dtype) for p, shape in zip(srcs, land_shapes)]
        + [jax.ShapeDtypeStruct((8, CHUNK), F32)],
        in_specs=[hbm] * (2 * n),
        out_specs=[sem, sem] + [hbm] * (2 * n) + [pl.BlockSpec(memory_space=pltpu.VMEM)],
        input_output_aliases={i: 2 + i for i in range(2 * n)},
        compiler_params=pltpu.CompilerParams(has_side_effects=pltpu.SideEffectType.DATAFLOW_SIDE_EFFECTING),
    )(*in_hbm)
    return res[:-1], res[-1]


def _split_wait(name, routes, in_flight, after):
    n = (len(in_flight) - 2) // 2
    hbm = pl.BlockSpec(memory_space=pltpu.HBM)
    sem = pl.BlockSpec(memory_space=pltpu.SEMAPHORE)
    send_sems, recv_sems, *bufs = in_flight

    def body(*refs):
        src_refs, land_refs = refs[:n], refs[n : 2 * n]
        send_ref, recv_ref = refs[2 * n], refs[2 * n + 1]
        for k, (a, peer, src_slot, _, here) in enumerate(routes(n)):
            cp = pltpu.make_async_remote_copy(
                src_ref=src_refs[a].at[src_slot], dst_ref=land_refs[a].at[here],
                send_sem=send_ref.at[k], recv_sem=recv_ref.at[k], device_id=peer, device_id_type=MESH)
            cp.wait_send()
            cp.wait_recv()

    res = pl.pallas_call(
        body,
        name=name,
        out_shape=[pltpu.HBM(b.shape, b.dtype) for b in bufs],
        in_specs=[hbm] * (2 * n) + [sem, sem] + [pl.BlockSpec(memory_space=pl.ANY)] * len(after),
        out_specs=[hbm] * (2 * n),
        input_output_aliases={i: i for i in range(2 * n)},
        compiler_params=pltpu.CompilerParams(has_side_effects=pltpu.SideEffectType.DATAFLOW_SIDE_EFFECTING),
    )(*bufs, send_sems, recv_sems, *after)
    return res[:n], res[n:]


def kernel(x, mem, norm_mix_g, w_in, conv_w, gm_ln_g, gm_ln_b, gm_ws, gm_bs, w_out, norm_x_g, norm_mem_g, w_q, w_kv, w_xo, norm_final_g, loss_target, m_norm_mix_g, m_w_in, m_conv_w, m_gm_ln_g, m_gm_ln_b, m_gm_ws, m_gm_bs, m_w_out, m_norm_x_g, m_norm_mem_g, m_w_q, m_w_kv, m_w_xo, m_norm_final_g, v_norm_mix_g, v_w_in, v_conv_w, v_gm_ln_g, v_gm_ln_b, v_gm_ws, v_gm_bs, v_w_out, v_norm_x_g, v_norm_mem_g, v_w_q, v_w_kv, v_w_xo, v_norm_final_g):
    s = x.shape[1]
    dev = 4 * lax.axis_index("x") + 2 * lax.axis_index("y") + lax.axis_index("c")
    core = lax.axis_index("c").astype(jnp.int32).reshape(1)
    x2 = x[0]
    target = loss_target[0]
    pad_taps = lambda t: jnp.pad(t[0], ((0, 5), (0, 0)))

    big_names = ["w_in", "w_out", "w_q", "w_kv", "w_xo"]
    big_w = dict(w_in=w_in[0], w_out=w_out[0], w_q=w_q[0], w_kv=w_kv[0], w_xo=w_xo[0])
    big_m = dict(w_in=m_w_in[0], w_out=m_w_out[0], w_q=m_w_q[0], w_kv=m_w_kv[0], w_xo=m_w_xo[0])
    big_v = dict(w_in=v_w_in[0], w_out=v_w_out[0], w_q=v_w_q[0], w_kv=v_w_kv[0], w_xo=v_w_xo[0])
    shard_axis = dict(w_in=1, w_out=0, w_q=0, w_kv=1, w_xo=0)
    full_shape = dict(w_in=(D_MODEL, IN_DIM), w_out=(MIX_DIM, D_MODEL), w_q=(D_MODEL, D_MODEL),
                      w_kv=(D_MODEL, 2 * D_MODEL), w_xo=(D_MODEL, D_MODEL))
    shard_size = {k: big_w[k].shape[shard_axis[k]] for k in big_names}
    others = big_names[1:]
    gather_spec = lambda k: (full_shape[k], shard_axis[k], shard_size[k])
    reduce_spec = lambda k: (shard_axis[k], shard_size[k])
    ws = gm_ws[0]
    bs_b = jnp.broadcast_to(gm_bs[0][:, :, None], (8, CHUNK, CHUNK))
    g_f = norm_final_g.reshape(1, D_MODEL)

    px, py = lax.axis_index("x"), lax.axis_index("y")
    chip_order = jnp.stack([2 * px + py, 2 * (1 - px) + py, 2 * px + 1 - py, 2 * (1 - px) + 1 - py]).astype(jnp.int32)
    with_inproj, with_mixer = [], ["w_out", "w_q", "w_kv", "w_xo"]
    (proj, h1, h1t, w_in_full), ((*gathered_early, conv_g),) = _inproj(
        x2, norm_mix_g, big_w["w_in"].astype(BF16), chip_order,
        plans=[_gather_plan([big_w[k].astype(BF16) for k in with_inproj] + [pad_taps(conv_w)[None]],
                            [gather_spec(k) for k in with_inproj] + [((N_DEV, 8, CHUNK), 0, 1)])])
    conv_full = conv_g.transpose(1, 0, 2).reshape(8, D_MODEL)
    cat, (gathered,) = _mixer_fwd(
        proj, conv_full, gm_ln_g, gm_ln_b, ws, bs_b,
        plans=[_gather_plan([big_w[k].astype(BF16) for k in with_mixer], [gather_spec(k) for k in with_mixer])])
    wf = dict(zip(with_inproj + with_mixer, list(gathered_early) + list(gathered)))
    k, vv, mt = _kv_fwd(mem[0], norm_mem_g, wf["w_kv"])
    (dcat, actt, cot, dk, dvv, d_gf, d_gx, sq_err) = _attn(
        x2, cat, target, wf["w_out"], wf["w_q"], wf["w_xo"], k, vv, norm_x_g, g_f)

    def pair_sums(names, grads, got):
        return [_pair_sum("pair_sum_" + k, grads[k], got[i], *reduce_spec(k), core) for i, k in enumerate(names)]

    sibling_plan = lambda names, grads: _sibling_plan([grads[k] for k in names], [reduce_spec(k) for k in names])
    early = ["w_kv", "w_out", "w_q", "w_xo"]
    big_g, big_gb = {}, {}
    big_g["w_kv"], big_gb["w_kv"], d_gmem = _kv_bwd(mem[0], wf["w_kv"], mt, dk, dvv)
    dws, (got_kv,) = _dw_attn(actt, cot, plans=[sibling_plan(early[:1], big_gb)])
    for name, (f32, bf16) in zip(early[1:], dws):
        big_g[name], big_gb[name] = f32, bf16
    pair_kv, (got_rest,) = _pair_sum("pair_sum_w_kv", big_g["w_kv"], got_kv[0], *reduce_spec("w_kv"), core,
                                     plans=[sibling_plan(early[1:], big_gb)])
    (dproj, d_cw, d_lng, d_lnb, d_ws, d_bs), (parts_early,) = _mixer_bwd(
        proj, dcat, conv_full, gm_ln_g, gm_ln_b, ws, bs_b,
        plans=[_chip_plan([pair_kv] + pair_sums(early[1:], big_g, got_rest))])
    half = D_MODEL // 2
    in_spec = reduce_spec("w_in")
    (dw_top, dwb_top), _ = _matmul_f32("dw_in_top", h1t, dproj, rows=(0, half))
    (dw_bot, dwb_bot), (got_top, (ws_all,)) = _matmul_f32(
        "dw_in_bot", h1t, dproj, rows=(half, half),
        plans=[_sibling_plan([dwb_top], [in_spec]),
               _gather_plan([d_ws.reshape(1, 8 * CHUNK, CHUNK).astype(BF16)], [((N_DEV, 8 * CHUNK, CHUNK), 0, 1)])])
    pair_top, ((got_bot,),) = _pair_sum("pair_sum_w_in_top", dw_top, got_top[0], *in_spec, core,
                                        plans=[_sibling_plan([dwb_bot], [in_spec])])
    pair_bot = _pair_sum("pair_sum_w_in_bot", dw_bot, got_bot, *in_spec, core)
    chip_shapes = [p.shape for p in (pair_top, pair_bot)]
    chips_in_flight, token = _split_start("grad_chip_exchange_start", _chip_routes, [pair_top, pair_bot], chip_shapes)
    n_tiles = s // min(TM_DH1, s)
    (grad_x, d_gmix), _ = _dh1("dh1", dproj, w_in_full, x2, cot, norm_mix_g, (0, n_tiles), after=[token])
    vec_names = ["norm_mix_g", "gm_ln_g", "gm_ln_b", "gm_bs", "norm_x_g", "norm_mem_g", "norm_final_g"]
    vec_g = [d_gmix, d_lng, d_lnb, d_bs, d_gx, d_gmem, d_gf]
    small = jnp.concatenate(vec_g + [sq_err, d_cw], axis=0)[None]
    small_in_flight, small_token = _split_start("small_all_gather_start", _gather_routes, [small], [(N_DEV, 16, D_MODEL)])
    flat_ws = lambda t: t.reshape(8 * CHUNK, CHUNK)
    updated, _ = _adamw(
        "adamw_early",
        [(big_w[k], big_m[k], big_v[k], got_k) for k, got_k in zip(early, parts_early)]
        + [(flat_ws(ws), flat_ws(m_gm_ws), flat_ws(v_gm_ws), ws_all)], steps=STEPS_ADAMW // 2)
    big_out = dict(zip(early, updated[:4]))
    ws_out = updated[4]
    (pair_top, pair_bot), (land_top, land_bot) = _split_wait(
        "grad_chip_exchange_wait", _chip_routes, chips_in_flight, [small_token] + [u[0] for u in updated])
    my_chip = (2 * px + py).astype(jnp.int32).reshape(1)
    (big_out["w_in"],), _ = _adamw(
        "adamw_w_in", [(big_w["w_in"], big_m["w_in"], big_v["w_in"], [land_top, land_bot], [pair_top, pair_bot])],
        steps=STEPS_ADAMW, own_slot=my_chip)
    (small,), (small_all,) = _split_wait("small_all_gather_wait", _gather_routes, small_in_flight, [big_out["w_in"][0]])
    row = lambda t: t.reshape(1, D_MODEL)
    vec_out, conv_out, loss_row = _update_small(
        dev.astype(jnp.int32).reshape(1),
        [norm_mix_g, gm_ln_g, gm_ln_b, row(gm_bs), norm_x_g, norm_mem_g, row(norm_final_g)],
        [m_norm_mix_g, m_gm_ln_g, m_gm_ln_b, row(m_gm_bs), m_norm_x_g, m_norm_mem_g, row(m_norm_final_g)],
        [v_norm_mix_g, v_gm_ln_g, v_gm_ln_b, row(v_gm_bs), v_norm_x_g, v_norm_mem_g, row(v_norm_final_g)],
        *(t.transpose(1, 0, 2) for t in (conv_w, m_conv_w, v_conv_w)), small, small_all)
    loss = loss_row[0, 0]

    def result(name, which):
        if name in big_out:
            return big_out[name][which][None]
        if name == "conv_w":
            return conv_out[which].transpose(1, 0, 2)
        if name == "gm_ws":
            return ws_out[which].reshape(1, 8, CHUNK, CHUNK)
        r = vec_out[vec_names.index(name)][which]
        if name == "gm_bs":
            return r.reshape(1, 8, CHUNK)
        if name == "norm_final_g":
            return r.reshape(D_MODEL)
        return r

    weights = ["norm_mix_g", "w_in", "conv_w", "gm_ln_g", "gm_ln_b", "gm_ws", "gm_bs", "w_out", "norm_x_g", "norm_mem_g", "w_q", "w_kv", "w_xo", "norm_final_g"]
    outs = [loss, grad_x[None]]
    for which in range(4):
        outs += [result(name, which) for name in weights]
    return tuple(outs)
```

```python
import math

import jax
import jax.numpy as jnp
from jax import lax
from jax.experimental import pallas as pl
from jax.experimental.pallas import tpu as pltpu

F32 = jnp.float32
BF16 = jnp.bfloat16
MESH = pl.DeviceIdType.MESH

D_MODEL = 1024
N_DEV = 8
N_CHIP = 4
CHUNK = 128
GROUPS = 7
IN_DIM = GROUPS * D_MODEL
SHARD_IN = IN_DIM // N_DEV
MIX_DIM = 2 * D_MODEL
X_HEADS = 4
X_HEAD_DIM = D_MODEL // X_HEADS
MEM_LEN = 256
EPS = 1e-6
GELU_K0 = math.sqrt(2.0 / math.pi)
GELU_K1 = 0.044715

ADAM_LR = 0.001
ADAM_B1 = 0.9
ADAM_B2 = 0.999
ADAM_EPS = 1e-08
ADAM_WD = 0.01
ADAM_STEP = 10

TM_INPROJ = 512
TM_ATTN = 512
SUB_ATTN = 256
TM_DH1 = 512
TM_DW = 512
TK_DW = 4096
TR_PAIR_SUM = 1024
BARRIER_ID = {"sibling": 1, "chips": 2, "all": 3}
STEPS_ADAMW = 4


def _dot(a, b):
    return lax.dot_general(a, b, (((1,), (0,)), ((), ())), preferred_element_type=F32)


def _dot_nt(a, b):
    return lax.dot_general(a, b, (((1,), (1,)), ((), ())), preferred_element_type=F32)


def _rows8(v):
    return jnp.sum(v.reshape(v.shape[0] // 8, 8, v.shape[1]), axis=0)


def _gelu(x, with_grad=True):
    x2 = x * x
    t = jnp.tanh(x * (GELU_K0 + (GELU_K0 * GELU_K1) * x2))
    half = 0.5 + 0.5 * t
    g = x * half
    if not with_grad:
        return g
    return g, half + ((0.5 * x) * (1.0 - t * t)) * (GELU_K0 + (3.0 * GELU_K0 * GELU_K1) * x2)


def _silu(z, with_grad=True):
    s = jax.nn.sigmoid(z)
    zs = z * s
    return (zs, s + zs * (1.0 - s)) if with_grad else zs


def _rms(v):
    r = lax.rsqrt(jnp.mean(v * v, axis=-1, keepdims=True) + EPS)
    return v * r, r


def _rms_bwd(dy_g, vh, r):
    return r * (dy_g - vh * jnp.mean(dy_g * vh, axis=-1, keepdims=True))


def _place():
    return lax.axis_index("x"), lax.axis_index("y"), lax.axis_index("c")


def _other_chips(x, y):
    return [(1 - x, y), (x, 1 - y), (1 - x, 1 - y)]


def _pow2_divisor(n):
    return n & (-n)


def _shard_view(ref, axis, size, d):
    start = d * size
    align = _pow2_divisor(size)
    if align > 1:
        start = pl.multiple_of(start, align)
    idx = [slice(None)] * len(ref.shape)
    idx[axis] = pl.ds(start, size)
    return ref.at[tuple(idx)]


class _Plan:
    def __init__(self, ins, out_shape, sems, start, finish, mid=None, aliases=None, peers="all"):
        self.ins, self.out_shape, self.sems = list(ins), list(out_shape), list(sems)
        self.start, self.mid, self.finish = start, mid, finish
        self.aliases = dict(aliases or {})
        self.peers = peers


def _gather_plan(shards, specs):
    n = len(shards)
    n_copy = 7

    def copies(ins, outs, sems):
        send_sems, recv_sems, local_sems = sems
        x, y, c = _place()
        me, sibling = (x, y, c), (x, y, 1 - c)
        chips = _other_chips(x, y)

        def block(a, p):
            _, axis, size = specs[a]
            return _shard_view(outs[a], axis, size, 4 * p[0] + 2 * p[1] + p[2])

        def copy(a, k, p, to, src=None):
            return pltpu.make_async_remote_copy(
                src_ref=block(a, p) if src is None else src,
                dst_ref=block(a, p),
                send_sem=send_sems.at[a * n_copy + k],
                recv_sem=recv_sems.at[a * n_copy + k],
                device_id=to,
                device_id_type=MESH,
            )

        def mine():
            return [pltpu.make_async_copy(ins[a], block(a, me), local_sems.at[a]) for a in range(n)]

        def first():
            return [cp for a in range(n) for cp in
                    [copy(a, 0, me, sibling, src=ins[a])] + [copy(a, 1 + j, me, (*chip, c), src=ins[a]) for j, chip in enumerate(chips)]]

        def landed():
            return [copy(a, 1 + j, (*chip, c), me) for j, chip in enumerate(chips) for a in range(n)]

        def passed():
            return [copy(a, 4 + j, (*chip, c), sibling) for j, chip in enumerate(chips) for a in range(n)]

        def last():
            return [cp for a in range(n) for cp in
                    [copy(a, 0, sibling, me)] + [copy(a, 4 + j, (*chip, 1 - c), me) for j, chip in enumerate(chips)]]

        return mine, first, landed, passed, last

    def start(ins, outs, sems):
        mine, first, _, _, _ = copies(ins, outs, sems)
        for cp in mine() + first():
            cp.start()

    def mid(ins, outs, sems):
        _, _, landed, passed, _ = copies(ins, outs, sems)
        for got, fwd in zip(landed(), passed()):
            got.wait_recv()
            fwd.start()

    def finish(ins, outs, sems):
        mine, first, _, passed, last = copies(ins, outs, sems)
        for cp in last():
            cp.wait_recv()
        for cp in first() + passed():
            cp.wait_send()
        for cp in mine():
            cp.wait()

    return _Plan(
        shards,
        [jax.ShapeDtypeStruct(full, s.dtype) for s, (full, _, _) in zip(shards, specs)],
        [pltpu.SemaphoreType.DMA((n * n_copy,)), pltpu.SemaphoreType.DMA((n * n_copy,)), pltpu.SemaphoreType.DMA((n,))],
        start, finish, mid,
    )


def _sibling_plan(grads, specs):
    n = len(grads)

    def shard_shape(a):
        axis, size = specs[a]
        shp = list(grads[a].shape)
        shp[axis] = size
        return tuple(shp)

    def copies(ins, outs, sems):
        send_sems, recv_sems = sems
        x, y, c = _place()
        return [
            pltpu.make_async_remote_copy(
                src_ref=_shard_view(ins[a], specs[a][0], specs[a][1], 2 * q + (1 - c)),
                dst_ref=outs[a].at[q],
                send_sem=send_sems.at[a * N_CHIP + q],
                recv_sem=recv_sems.at[a * N_CHIP + q],
                device_id=(x, y, 1 - c),
                device_id_type=MESH,
            )
            for a in range(n) for q in range(N_CHIP)
        ]

    def start(ins, outs, sems):
        for cp in copies(ins, outs, sems):
            cp.start()

    def finish(ins, outs, sems):
        for cp in copies(ins, outs, sems):
            cp.wait()

    return _Plan(
        grads,
        [jax.ShapeDtypeStruct((N_CHIP, *shard_shape(a)), grads[a].dtype) for a in range(n)],
        [pltpu.SemaphoreType.DMA((n * N_CHIP,)), pltpu.SemaphoreType.DMA((n * N_CHIP,))],
        start, finish, peers="sibling",
    )


def _chip_plan(parts, full_rows=None, row0=None, into=None):
    n = len(parts)
    full_rows = [p.shape[1] for p in parts] if full_rows is None else full_rows
    row0 = [0] * n if row0 is None else row0

    def copies(ins, outs, sems):
        send_sems, recv_sems, local_sems = sems
        x, y, c = _place()
        my_chip = 2 * x + y
        chips = _other_chips(x, y)

        def land(a, q):
            return outs[a].at[q, pl.ds(row0[a], parts[a].shape[1]), :]

        def mine():
            return [pltpu.make_async_copy(ins[a].at[my_chip], land(a, my_chip), local_sems.at[a]) for a in range(n)]

        def remote(sending):
            return [
                pltpu.make_async_remote_copy(
                    src_ref=ins[a].at[2 * chip[0] + chip[1]],
                    dst_ref=land(a, my_chip if sending else 2 * chip[0] + chip[1]),
                    send_sem=send_sems.at[a * 3 + j], recv_sem=recv_sems.at[a * 3 + j],
                    device_id=(*chip, c), device_id_type=MESH)
                for a in range(n) for j, chip in enumerate(chips)
            ]

        return mine, remote

    def start(ins, outs, sems):
        mine, remote = copies(ins, outs, sems)
        for cp in mine() + remote(True):
            cp.start()

    def finish(ins, outs, sems):
        mine, remote = copies(ins, outs, sems)
        for cp in remote(False):
            cp.wait_recv()
        for cp in remote(True):
            cp.wait_send()
        for cp in mine():
            cp.wait()

    return _Plan(
        list(parts) + (list(into) if into is not None else []),
        [jax.ShapeDtypeStruct((N_CHIP, full_rows[a], parts[a].shape[2]), parts[a].dtype) for a in range(n)],
        [pltpu.SemaphoreType.DMA((n * 3,)), pltpu.SemaphoreType.DMA((n * 3,)), pltpu.SemaphoreType.DMA((n,))],
        start, finish,
        aliases={n + a: a for a in range(n)} if into is not None else None, peers="chips",
    )


def _call(body, *, name, operands, out_shape, grid=(), in_specs=None, out_specs=None, scratch_shapes=(),
          plans=(), aliases=None, prefetch=(), start_after_body=False, pass_on_at=0.6, body_peers=None):
    operands, out_shape, scratch_shapes = list(operands), list(out_shape), list(scratch_shapes)
    n_pf, n_in, n_out, n_scr = len(prefetch), len(operands), len(out_shape), len(scratch_shapes)
    in_vmem = pl.BlockSpec(memory_space=pltpu.VMEM)
    in_hbm = pl.BlockSpec(memory_space=pl.ANY)
    in_specs = [in_vmem] * n_in if in_specs is None else list(in_specs)
    out_specs = [in_vmem] * n_out if out_specs is None else list(out_specs)
    io_alias = {n_pf + i: o for i, o in (aliases or {}).items()}
    pos_in, pos_out = n_pf + n_in, n_out
    for p in plans:
        for i, o in p.aliases.items():
            io_alias[pos_in + i] = pos_out + o
        pos_in += len(p.ins)
        pos_out += len(p.out_shape)
    steps = math.prod(grid) if grid else 1
    mid_step = min(steps - 1, int(steps * pass_on_at))
    patterns = {p.peers for p in plans} | ({body_peers} if body_peers else set())
    pattern = None if not patterns else patterns.pop() if len(patterns) == 1 else "all"

    def shake_hands():
        x, y, c = _place()
        peers = [(x, y, 1 - c)] if pattern in ("sibling", "all") else []
        peers += [(*chip, c) for chip in _other_chips(x, y)] if pattern in ("chips", "all") else []
        barrier = pltpu.get_barrier_semaphore()
        for peer in peers:
            pl.semaphore_signal(barrier, inc=1, device_id=peer, device_id_type=MESH)
        pl.semaphore_wait(barrier, len(peers))

    def wrapped(*refs):
        refs = list(refs)
        take = lambda k: [refs.pop(0) for _ in range(k)]
        pf = take(n_pf)
        ins, p_ins = take(n_in), [take(len(p.ins)) for p in plans]
        outs, p_outs = take(n_out), [take(len(p.out_shape)) for p in plans]
        scr, p_sems = take(n_scr), [take(len(p.sems)) for p in plans]
        step = 0
        for ax, g in enumerate(grid):
            step = step * g + pl.program_id(ax)

        def hook(kind, at):
            todo = [(getattr(p, kind), a, b, c) for p, a, b, c in zip(plans, p_ins, p_outs, p_sems) if getattr(p, kind)]

            def run():
                for fn, a, b, c in todo:
                    fn(a, b, c)

            if todo and grid:
                pl.when(step == at)(run)
            elif todo:
                run()

        if pattern and grid:
            pl.when(step == 0)(shake_hands)
        elif pattern:
            shake_hands()
        if not start_after_body:
            hook("start", 0)
        if body is not None:
            body(*pf, *ins, *outs, *scr)
        if start_after_body:
            hook("start", 0)
        hook("mid", mid_step)
        hook("finish", steps - 1)

    layout = dict(
        grid=grid,
        in_specs=in_specs + [in_hbm] * sum(len(p.ins) for p in plans),
        out_specs=out_specs + [in_hbm] * sum(len(p.out_shape) for p in plans),
        scratch_shapes=scratch_shapes + [s for p in plans for s in p.sems],
    )
    if n_pf:
        layout = dict(grid_spec=pltpu.PrefetchScalarGridSpec(num_scalar_prefetch=n_pf, **layout))
    res = pl.pallas_call(
        wrapped,
        name=name,
        out_shape=out_shape + [o for p in plans for o in p.out_shape],
        input_output_aliases=io_alias,
        compiler_params=pltpu.CompilerParams(
            dimension_semantics=("arbitrary",) * len(grid) if grid else None,
            collective_id=BARRIER_ID[pattern] if pattern else None),
        **layout,
    )(*prefetch, *operands, *[a for p in plans for a in p.ins])
    res = list(res)
    outs, plan_outs = res[:n_out], []
    pos = n_out
    for p in plans:
        plan_outs.append(res[pos : pos + len(p.out_shape)])
        pos += len(p.out_shape)
    return outs, plan_outs


def _inproj(x, g, w_shard, order, plans=()):
    s = x.shape[0]
    tm = min(TM_INPROJ, s)
    nt = s // tm
    wide = 2 * SHARD_IN
    fwd_at = nt // 2

    def body(order_ref, x_ref, g_ref, w_hbm, proj_ref, h1_ref, h1t_ref, wfull_hbm,
             w_scr, h1_scr, send_sems, recv_sems, local_sems):
        k, i = pl.program_id(0), pl.program_id(1)
        px, py, c = _place()
        me, sibling = (px, py, c), (px, py, 1 - c)
        chips = _other_chips(px, py)

        def half(slot, core):
            return w_scr.at[slot, :, pl.ds(pl.multiple_of(core * SHARD_IN, 128), SHARD_IN)]

        def copy(n, slot, core, to, src=None):
            return pltpu.make_async_remote_copy(
                src_ref=half(slot, core) if src is None else src, dst_ref=half(slot, core),
                send_sem=send_sems.at[n], recv_sem=recv_sems.at[n], device_id=to, device_id_type=MESH)

        def mine():
            return pltpu.make_async_copy(w_hbm, half(0, c), local_sems.at[0])

        def first():
            return [copy(0, 0, c, sibling, src=w_hbm)] + [copy(1 + j, 1 + j, c, (*chips[j], c), src=w_hbm) for j in range(2)]

        def relay():
            to = (c * px + (1 - c) * (1 - px), c * (1 - py) + (1 - c) * py, c)
            return pltpu.make_async_remote_copy(
                src_ref=half(2 - c, c), dst_ref=half(3, c), send_sem=send_sems.at[3], recv_sem=recv_sems.at[3],
                device_id=to, device_id_type=MESH)

        def save(slot):
            cols = pl.ds(pl.multiple_of(order_ref[slot] * wide, 128), wide)
            return pltpu.make_async_copy(w_scr.at[slot], wfull_hbm.at[:, cols], local_sems.at[1 + slot])

        @pl.when((k == 0) & (i == 0))
        def _():
            mine().start()
            for cp in first():
                cp.start()
            mine().wait()
            copy(0, 0, 1 - c, me).wait_recv()

        for j in range(3):
            @pl.when((k == j + 1) & (i == 0))
            def _():
                copy(4 + j, 1 + j, 1 - c, me).wait_recv()

        @pl.when((k == 3) & (i == 0))
        def _():
            for slot in range(N_CHIP):
                save(slot).start()

        @pl.when(k == 0)
        def _():
            xh, _ = _rms(x_ref[...])
            h = xh * g_ref[...]
            h1_ref[...] = h.astype(BF16)
            h1t_ref[...] = h.T.astype(BF16)
            h1_scr[pl.ds(pl.multiple_of(i * tm, tm), tm), :] = h.astype(BF16)

        proj_ref[...] = _dot(h1_scr[pl.ds(pl.multiple_of(i * tm, tm), tm), :], w_scr[k]).astype(BF16)

        for j in range(3):
            @pl.when((k == 0) & (i == nt - 1) if j < 2 else (k == 2) & (i == fwd_at))
            def _():
                copy(1 + j, 1 + j, c, me).wait_recv()
                copy(4 + j, 1 + j, c, sibling).start()
                if j < 2:
                    pl.when(c == 1 - j)(lambda: relay().start())

        @pl.when((k == 3) & (i == nt - 1))
        def _():
            for cp in first() + [relay()] + [copy(4 + j, 1 + j, c, sibling) for j in range(3)]:
                cp.wait_send()
            for slot in range(N_CHIP):
                save(slot).wait()

    once = lambda k, i, order_ref: (jnp.where(k == 0, i, nt - 1), 0)
    return _call(
        body,
        name="inproj",
        prefetch=[order],
        operands=[x, g, w_shard],
        grid=(N_CHIP, nt),
        in_specs=[
            pl.BlockSpec((tm, D_MODEL), once),
            pl.BlockSpec((1, D_MODEL), lambda k, i, order_ref: (0, 0)),
            pl.BlockSpec(memory_space=pl.ANY),
        ],
        out_specs=[
            pl.BlockSpec((tm, wide), lambda k, i, order_ref: (i, order_ref[k])),
            pl.BlockSpec((tm, D_MODEL), once),
            pl.BlockSpec((D_MODEL, tm), lambda k, i, order_ref: (0, jnp.where(k == 0, i, nt - 1))),
            pl.BlockSpec(memory_space=pl.ANY),
        ],
        out_shape=[
            jax.ShapeDtypeStruct((s, IN_DIM), BF16),
            jax.ShapeDtypeStruct((s, D_MODEL), BF16),
            jax.ShapeDtypeStruct((D_MODEL, s), BF16),
            jax.ShapeDtypeStruct((D_MODEL, IN_DIM), BF16),
        ],
        scratch_shapes=[
            pltpu.VMEM((N_CHIP, D_MODEL, wide), BF16),
            pltpu.VMEM((s, D_MODEL), BF16),
            pltpu.SemaphoreType.DMA((7,)),
            pltpu.SemaphoreType.DMA((7,)),
            pltpu.SemaphoreType.DMA((1 + N_CHIP,)),
        ],
        plans=plans,
        start_after_body=True,
        pass_on_at=0.8,
        body_peers="all",
    )


def _rows_from_above(v, above):
    row = lax.broadcasted_iota(jnp.int32, (8, v.shape[1]), 0)
    r1, r2 = pltpu.roll(v, 1, 0), pltpu.roll(v, 2, 0)
    top1 = jnp.where(row == 0, above[7:8], r1[0:8])
    top2 = jnp.where(row == 0, above[6:7], jnp.where(row == 1, above[7:8], r2[0:8]))
    return jnp.concatenate([top1, r1[8:]], axis=0), jnp.concatenate([top2, r2[8:]], axis=0)


def _rows_from_below(v, below):
    n = v.shape[0]
    row = lax.broadcasted_iota(jnp.int32, (8, v.shape[1]), 0)
    r1, r2 = pltpu.roll(v, n - 1, 0), pltpu.roll(v, n - 2, 0)
    end1 = jnp.where(row == 7, below[0:1], r1[n - 8 :])
    end2 = jnp.where(row == 6, below[0:1], jnp.where(row == 7, below[1:2], r2[n - 8 :]))
    return jnp.concatenate([r1[: n - 8], end1], axis=0), jnp.concatenate([r2[: n - 8], end2], axis=0)


def _causal_mask():
    row = lax.broadcasted_iota(jnp.int32, (CHUNK, CHUNK), 0)
    col = lax.broadcasted_iota(jnp.int32, (CHUNK, CHUNK), 1)
    return row, row >= col


def _mixer_fwd(proj, conv_w, ln_g, ln_b, ws, bs_b, plans=()):
    s = proj.shape[0]
    nt = s // CHUNK

    def body(proj_ref, cw_ref, lng_ref, lnb_ref, ws_ref, bsb_ref, cat_ref, pcar_ref):
        @pl.when(pl.program_id(0) == 0)
        def _():
            pcar_ref[...] = jnp.zeros_like(pcar_ref)

        row, tril = _causal_mask()
        for j in range(8):
            cs = slice(CHUNK * j, CHUNK * (j + 1))

            def grp(k):
                return proj_ref[:, k * D_MODEL + CHUNK * j : k * D_MODEL + CHUNK * (j + 1)].astype(F32)

            w = cw_ref[:, cs]
            p = grp(1) * grp(2)
            p1, p2 = _rows_from_above(p, pcar_ref[:, cs])
            pcar_ref[:, cs] = p[CHUNK - 8 :]
            cv = w[0:1] * p2 + w[1:2] * p1 + w[2:3] * p
            sa = _silu(grp(3), with_grad=False)
            cat_ref[:, cs] = ((grp(0) * cv) * sa).astype(BF16)
            gu = _gelu(grp(4), with_grad=False)
            gv = _gelu(grp(5), with_grad=False)
            dv = gv - jnp.mean(gv, axis=-1, keepdims=True)
            vn = dv * lax.rsqrt(jnp.mean(dv * dv, axis=-1, keepdims=True) + EPS)
            vn = vn * lng_ref[:, cs] + lnb_ref[:, cs]
            wc = jnp.where(tril, ws_ref[j], 0.0).astype(BF16)
            sp = _dot(wc, vn.astype(BF16)) + bsb_ref[j]
            sb = _silu(grp(6), with_grad=False)
            cat_ref[:, D_MODEL + CHUNK * j : D_MODEL + CHUNK * (j + 1)] = ((gu * sp) * sb).astype(BF16)

    whole = lambda shape: pl.BlockSpec(shape, lambda i: (0,) * len(shape))
    (cat,), plan_outs = _call(
        body,
        name="mixer_fwd",
        operands=[proj, conv_w, ln_g, ln_b, ws, bs_b],
        grid=(nt,),
        in_specs=[
            pl.BlockSpec((CHUNK, IN_DIM), lambda i: (i, 0)),
            whole((8, D_MODEL)),
            whole((1, D_MODEL)),
            whole((1, D_MODEL)),
            whole((8, CHUNK, CHUNK)),
            whole((8, CHUNK, CHUNK)),
        ],
        out_specs=[pl.BlockSpec((CHUNK, MIX_DIM), lambda i: (i, 0))],
        out_shape=[jax.ShapeDtypeStruct((s, MIX_DIM), BF16)],
        scratch_shapes=[pltpu.VMEM((8, D_MODEL), F32)],
        plans=plans,
        pass_on_at=0.85,
    )
    return cat, plan_outs


def _mixer_bwd(proj, dcat, conv_w, ln_g, ln_b, ws, bs_b, plans=()):
    s = proj.shape[0]
    nt = s // CHUNK

    def body(proj_ref, halo_ref, dcat_ref, cw_ref, lng_ref, lnb_ref, ws_ref, bsb_ref,
             dproj_ref, dcw_ref, dlng_ref, dlnb_ref, dws_ref, dbs_ref,
             car_ref, acc_cw, acc_lng, acc_lnb, acc_bs, wc_scr, wct_scr):
        i = pl.program_id(0)
        tile = nt - 1 - i
        row, tril = _causal_mask()

        @pl.when(i == 0)
        def _():
            for j in range(8):
                wc = jnp.where(tril, ws_ref[j], 0.0)
                wc_scr[j] = wc.astype(BF16)
                wct_scr[j] = wc.T.astype(BF16)
            car_ref[...] = jnp.zeros_like(car_ref)
            acc_cw[...] = jnp.zeros_like(acc_cw)
            acc_lng[...] = jnp.zeros_like(acc_lng)
            acc_lnb[...] = jnp.zeros_like(acc_lnb)
            acc_bs[...] = jnp.zeros_like(acc_bs)
            dws_ref[...] = jnp.zeros_like(dws_ref)

        has_prev = jnp.where(tile > 0, 1.0, 0.0).astype(F32)
        for j in range(8):
            cs = slice(CHUNK * j, CHUNK * (j + 1))

            def col(k):
                return slice(k * D_MODEL + CHUNK * j, k * D_MODEL + CHUNK * (j + 1))

            def grp(k):
                return proj_ref[:, col(k)].astype(F32)

            w = cw_ref[:, cs]
            gb, gc, xa = grp(0), grp(1), grp(2)
            p = gc * xa
            pprev = halo_ref[:, col(1)].astype(F32) * halo_ref[:, col(2)].astype(F32) * has_prev
            p1, p2 = _rows_from_above(p, pprev[8:16])
            cv = w[0:1] * p2 + w[1:2] * p1 + w[2:3] * p
            za = grp(3)
            sa, dsa = _silu(za)
            da = dcat_ref[:, cs].astype(F32)
            da_gb = da * gb
            dproj_ref[:, col(0)] = ((da * sa) * cv).astype(BF16)
            dproj_ref[:, col(3)] = ((da_gb * cv) * dsa).astype(BF16)
            dcv = da_gb * sa
            d1, d2 = _rows_from_below(dcv, car_ref[:, cs])
            car_ref[:, cs] = dcv[0:8]
            dp = w[2:3] * dcv + w[1:2] * d1 + w[0:1] * d2
            dproj_ref[:, col(1)] = (dp * xa).astype(BF16)
            dproj_ref[:, col(2)] = (dp * gc).astype(BF16)
            acc_cw[0, :, cs] += _rows8(dcv * p2)
            acc_cw[1, :, cs] += _rows8(dcv * p1)
            acc_cw[2, :, cs] += _rows8(dcv * p)
            gu, dgu = _gelu(grp(4))
            gv, dgv = _gelu(grp(5))
            dv = gv - jnp.mean(gv, axis=-1, keepdims=True)
            rstd = lax.rsqrt(jnp.mean(dv * dv, axis=-1, keepdims=True) + EPS)
            vnh = dv * rstd
            lng = lng_ref[:, cs]
            vnb = (vnh * lng + lnb_ref[:, cs]).astype(BF16)
            sp = _dot(wc_scr[j], vnb) + bsb_ref[j]
            zb = grp(6)
            sb, dsb = _silu(zb)
            db = dcat_ref[:, D_MODEL + CHUNK * j : D_MODEL + CHUNK * (j + 1)].astype(F32)
            db_sp, db_gu = db * sp, db * gu
            dproj_ref[:, col(4)] = ((db_sp * sb) * dgu).astype(BF16)
            dproj_ref[:, col(6)] = ((db_sp * gu) * dsb).astype(BF16)
            dsp = db_gu * sb
            acc_bs[j] += dsp
            dspb = dsp.astype(BF16)
            dws_ref[j] += _dot_nt(dspb, vnb)
            dvn = _dot(wct_scr[j], dspb)
            acc_lnb[:, cs] += _rows8(dvn)
            acc_lng[:, cs] += _rows8(dvn * vnh)
            dvh = dvn * lng
            dgvv = rstd * (dvh - jnp.mean(dvh, axis=-1, keepdims=True) - vnh * jnp.mean(dvh * vnh, axis=-1, keepdims=True))
            dproj_ref[:, col(5)] = (dgvv * dgv).astype(BF16)

        @pl.when(i == nt - 1)
        def _():
            dlng_ref[...] = jnp.sum(acc_lng[...], axis=0, keepdims=True)
            dlnb_ref[...] = jnp.sum(acc_lnb[...], axis=0, keepdims=True)
            dcw_ref[...] = jnp.zeros_like(dcw_ref)
            for k in range(3):
                dcw_ref[k : k + 1, :] = jnp.sum(acc_cw[k], axis=0, keepdims=True)
            for j in range(8):
                dws_ref[j] = jnp.where(tril, dws_ref[j], 0.0)
                dbs_ref[:, CHUNK * j : CHUNK * (j + 1)] = jnp.sum(acc_bs[j].T, axis=0, keepdims=True)

    whole = lambda shape: pl.BlockSpec(shape, lambda i: (0,) * len(shape))
    halo_rows = 16
    per = CHUNK // halo_rows
    return _call(
        body,
        name="mixer_bwd",
        operands=[proj, proj, dcat, conv_w, ln_g, ln_b, ws, bs_b],
        plans=plans,
        grid=(nt,),
        in_specs=[
            pl.BlockSpec((CHUNK, IN_DIM), lambda i: (nt - 1 - i, 0)),
            pl.BlockSpec((halo_rows, IN_DIM), lambda i: (jnp.maximum((nt - 1 - i) * per - 1, 0), 0)),
            pl.BlockSpec((CHUNK, MIX_DIM), lambda i: (nt - 1 - i, 0)),
            whole((8, D_MODEL)),
            whole((1, D_MODEL)),
            whole((1, D_MODEL)),
            whole((8, CHUNK, CHUNK)),
            whole((8, CHUNK, CHUNK)),
        ],
        out_specs=[
            pl.BlockSpec((CHUNK, IN_DIM), lambda i: (nt - 1 - i, 0)),
            whole((8, D_MODEL)),
            whole((1, D_MODEL)),
            whole((1, D_MODEL)),
            whole((8, CHUNK, CHUNK)),
            whole((1, D_MODEL)),
        ],
        out_shape=[
            jax.ShapeDtypeStruct((s, IN_DIM), BF16),
            jax.ShapeDtypeStruct((8, D_MODEL), F32),
            jax.ShapeDtypeStruct((1, D_MODEL), F32),
            jax.ShapeDtypeStruct((1, D_MODEL), F32),
            jax.ShapeDtypeStruct((8, CHUNK, CHUNK), F32),
            jax.ShapeDtypeStruct((1, D_MODEL), F32),
        ],
        scratch_shapes=[
            pltpu.VMEM((8, D_MODEL), F32),
            pltpu.VMEM((3, 8, D_MODEL), F32),
            pltpu.VMEM((8, D_MODEL), F32),
            pltpu.VMEM((8, D_MODEL), F32),
            pltpu.VMEM((8, CHUNK, CHUNK), F32),
            pltpu.VMEM((8, CHUNK, CHUNK), BF16),
            pltpu.VMEM((8, CHUNK, CHUNK), BF16),
        ],
    )


def _kv_fwd(mem, g_mem, w_kv):
    def body(mem_ref, g_ref, w_ref, k_ref, vv_ref, mt_ref):
        mh, _ = _rms(mem_ref[...])
        m = mh * g_ref[...]
        kv = _dot(m.astype(BF16), w_ref[...])
        k_ref[...] = kv[:, :D_MODEL].astype(BF16)
        vv_ref[...] = kv[:, D_MODEL:].astype(BF16)
        mt_ref[...] = m.T.astype(BF16)

    return pl.pallas_call(
        body,
        name="kv_fwd",
        out_shape=[
            jax.ShapeDtypeStruct((MEM_LEN, D_MODEL), BF16),
            jax.ShapeDtypeStruct((MEM_LEN, D_MODEL), BF16),
            jax.ShapeDtypeStruct((D_MODEL, MEM_LEN), BF16),
        ],
    )(mem, g_mem, w_kv)


def _kv_bwd(mem, w_kv, mt, dk, dvv):
    def body(mem_ref, w_ref, mt_ref, dk_ref, dvv_ref, dw_ref, dwb_ref, dg_ref):
        dkv = jnp.concatenate([dk_ref[...], dvv_ref[...]], axis=1).astype(BF16)
        dw = _dot(mt_ref[...], dkv)
        dw_ref[...] = dw
        dwb_ref[...] = dw.astype(BF16)
        dm = _dot_nt(dkv, w_ref[...])
        mh, _ = _rms(mem_ref[...])
        dg_ref[...] = jnp.sum(dm * mh, axis=0, keepdims=True)

    return pl.pallas_call(
        body,
        name="kv_bwd",
        out_shape=[
            jax.ShapeDtypeStruct((D_MODEL, 2 * D_MODEL), F32),
            jax.ShapeDtypeStruct((D_MODEL, 2 * D_MODEL), BF16),
            jax.ShapeDtypeStruct((1, D_MODEL), F32),
        ],
    )(mem, w_kv, mt, dk, dvv)


def _attn(x, cat, target, w_out, w_q, w_xo, k, vv, g_x, g_f):
    s = x.shape[0]
    tm = min(TM_ATTN, s)
    sub = min(SUB_ATTN, tm)
    nt = s // tm
    scale = 1.0 / math.sqrt(X_HEAD_DIM)

    def body(x_ref, cat_ref, t_ref, wout_ref, wq_ref, wxo_ref, k_ref, vv_ref, gx_ref, gf_ref,
             dcat_ref, actt_ref, cot_ref,
             dk_ref, dvv_ref, dgf_ref, dgx_ref, loss_ref,
             p_scr, q_scr, o_scr, dq_scr, acc_gf, acc_gx, acc_loss):
        i = pl.program_id(0)

        @pl.when(i == 0)
        def _():
            dk_ref[...] = jnp.zeros_like(dk_ref)
            dvv_ref[...] = jnp.zeros_like(dvv_ref)
            acc_gf[...] = jnp.zeros_like(acc_gf)
            acc_gx[...] = jnp.zeros_like(acc_gx)
            acc_loss[...] = jnp.zeros_like(acc_loss)

        subs = [slice(u * sub, (u + 1) * sub) for u in range(tm // sub)]
        heads = [slice(X_HEAD_DIM * h, X_HEAD_DIM * (h + 1)) for h in range(X_HEADS)]
        gx, gf = gx_ref[...], gf_ref[...]
        x1, x1h, r2 = [], [], []
        for rs in subs:
            cat = cat_ref[rs, :]
            actt_ref[0:MIX_DIM, rs] = cat.astype(F32).T.astype(BF16)
            v = x_ref[rs, :] + _dot(cat, wout_ref[...])
            vh, r = _rms(v)
            h2 = vh * gx
            actt_ref[MIX_DIM : MIX_DIM + D_MODEL, rs] = h2.T.astype(BF16)
            q_scr[rs, :] = _dot(h2.astype(BF16), wq_ref[...]).astype(BF16)
            x1.append(v), x1h.append(vh), r2.append(r)
        for rs in subs:
            for hs in heads:
                sc = _dot_nt(q_scr[rs, hs], k_ref[:, hs]) * scale
                e = jnp.exp(sc - jnp.max(sc, axis=-1, keepdims=True))
                p = e / jnp.sum(e, axis=-1, keepdims=True)
                p_scr[rs, hs] = p
                o_scr[rs, hs] = _dot(p.astype(BF16), vv_ref[:, hs])
        dx2, sq, d_gf = [], None, None
        for u, rs in enumerate(subs):
            o = o_scr[rs, :]
            actt_ref[MIX_DIM + D_MODEL :, rs] = o.T.astype(BF16)
            x2 = x1[u] + _dot(o.astype(BF16), wxo_ref[...])
            x2h, r3 = _rms(x2)
            err = x2h * gf - t_ref[rs, :]
            dy = err * (1.0 / D_MODEL)
            sq = _rows8(err * err) if sq is None else sq + _rows8(err * err)
            d_gf = _rows8(dy * x2h) if d_gf is None else d_gf + _rows8(dy * x2h)
            dx2.append(_rms_bwd(dy * gf, x2h, r3))
        acc_loss[...] += sq
        acc_gf[...] += d_gf
        d_vv, d_k = [None] * X_HEADS, [None] * X_HEADS
        for u, rs in enumerate(subs):
            dx2b = dx2[u].astype(BF16)
            cot_ref[2, rs, :] = dx2b
            dob = _dot_nt(dx2b, wxo_ref[...]).astype(BF16)
            for h, hs in enumerate(heads):
                p = p_scr[rs, hs]
                dp = _dot_nt(dob[:, hs], vv_ref[:, hs])
                ds = p * (dp - jnp.sum(dp * p, axis=-1, keepdims=True)) * scale
                dq_scr[rs, hs] = _dot(ds.astype(BF16), k_ref[:, hs]).astype(BF16)
                dvv_h = _dot(p.T.astype(BF16), dob[:, hs])
                dk_h = _dot(ds.T.astype(BF16), q_scr[rs, hs])
                d_vv[h] = dvv_h if d_vv[h] is None else d_vv[h] + dvv_h
                d_k[h] = dk_h if d_k[h] is None else d_k[h] + dk_h
        for h, hs in enumerate(heads):
            dvv_ref[:, hs] += d_vv[h]
            dk_ref[:, hs] += d_k[h]
        d_gx = None
        for u, rs in enumerate(subs):
            dqb = dq_scr[rs, :]
            cot_ref[1, rs, :] = dqb
            dh2 = _dot_nt(dqb, wq_ref[...])
            d_gx = _rows8(dh2 * x1h[u]) if d_gx is None else d_gx + _rows8(dh2 * x1h[u])
            dx1 = dx2[u] + _rms_bwd(dh2 * gx, x1h[u], r2[u])
            dx1b = dx1.astype(BF16)
            cot_ref[0, rs, :] = dx1b
            dcat_ref[rs, :] = _dot_nt(dx1b, wout_ref[...]).astype(BF16)
        acc_gx[...] += d_gx

        @pl.when(i == nt - 1)
        def _():
            dgf_ref[...] = jnp.sum(acc_gf[...], axis=0, keepdims=True)
            dgx_ref[...] = jnp.sum(acc_gx[...], axis=0, keepdims=True)
            loss_ref[...] = jnp.sum(acc_loss[...], axis=0, keepdims=True)

    whole = lambda shape: pl.BlockSpec(shape, lambda i: (0,) * len(shape))
    once = lambda shape: pl.BlockSpec(shape, lambda i: (0,) * len(shape), pipeline_mode=pl.Buffered(1))
    rows = lambda width: pl.BlockSpec((tm, width), lambda i: (i, 0))
    cols = lambda height: pl.BlockSpec((height, tm), lambda i: (0, i))
    vec = jax.ShapeDtypeStruct((1, D_MODEL), F32)
    return pl.pallas_call(
        body,
        name="attn",
        grid=(nt,),
        in_specs=[
            rows(D_MODEL), rows(MIX_DIM), rows(D_MODEL),
            once((MIX_DIM, D_MODEL)), once((D_MODEL, D_MODEL)), once((D_MODEL, D_MODEL)),
            once((MEM_LEN, D_MODEL)), once((MEM_LEN, D_MODEL)),
            whole((1, D_MODEL)), whole((1, D_MODEL)),
        ],
        out_specs=[
            rows(MIX_DIM), cols(2 * MIX_DIM),
            pl.BlockSpec((3, tm, D_MODEL), lambda i: (0, i, 0)),
            whole((MEM_LEN, D_MODEL)), whole((MEM_LEN, D_MODEL)),
            whole((1, D_MODEL)), whole((1, D_MODEL)), whole((1, D_MODEL)),
        ],
        out_shape=[
            jax.ShapeDtypeStruct((s, MIX_DIM), BF16),
            jax.ShapeDtypeStruct((2 * MIX_DIM, s), BF16),
            jax.ShapeDtypeStruct((3, s, D_MODEL), BF16),
            jax.ShapeDtypeStruct((MEM_LEN, D_MODEL), F32),
            jax.ShapeDtypeStruct((MEM_LEN, D_MODEL), F32),
            vec, vec, vec,
        ],
        scratch_shapes=[
            pltpu.VMEM((tm, D_MODEL), F32),
            pltpu.VMEM((tm, D_MODEL), BF16),
            pltpu.VMEM((tm, D_MODEL), F32),
            pltpu.VMEM((tm, D_MODEL), BF16),
            pltpu.VMEM((8, D_MODEL), F32),
            pltpu.VMEM((8, D_MODEL), F32),
            pltpu.VMEM((8, D_MODEL), F32),
        ],
        compiler_params=pltpu.CompilerParams(dimension_semantics=("arbitrary",)),
    )(x, cat, target, w_out, w_q, w_xo, k, vv, g_x, g_f)


def _matmul_f32(name, a, b, rows=None, plans=()):
    first, m = (0, a.shape[0]) if rows is None else rows
    kk = a.shape[1]
    n = b.shape[1]
    tm, tn, tk = min(m, TM_DW), min(n, D_MODEL), min(kk, TK_DW)
    nk = kk // tk
    i0 = first // tm

    def body(a_ref, b_ref, o_ref, ob_ref):
        prod = _dot(a_ref[...], b_ref[...])
        if nk == 1:
            o_ref[...] = prod
            ob_ref[...] = prod.astype(BF16)
            return
        kid = pl.program_id(2)

        @pl.when(kid == 0)
        def _():
            o_ref[...] = prod

        @pl.when(kid > 0)
        def _():
            o_ref[...] += prod

        @pl.when(kid == nk - 1)
        def _():
            ob_ref[...] = o_ref[...].astype(BF16)

    out_spec = pl.BlockSpec((tm, tn), lambda i, j, k: (i, j))
    outs, plan_outs = _call(
        body,
        name=name,
        operands=[a, b],
        grid=(m // tm, n // tn, nk),
        in_specs=[pl.BlockSpec((tm, tk), lambda i, j, k: (i + i0, k)), pl.BlockSpec((tk, tn), lambda i, j, k: (k, j))],
        out_specs=[out_spec, out_spec],
        out_shape=[jax.ShapeDtypeStruct((m, n), F32), jax.ShapeDtypeStruct((m, n), BF16)],
        plans=plans,
    )
    return outs, plan_outs


def _dw_attn(actt, cot, plans=()):
    s = actt.shape[1]
    tm = TM_DW
    counts = [MIX_DIM // tm, D_MODEL // tm, D_MODEL // tm]
    starts = [0, counts[0], counts[0] + counts[1]]
    which = lambda i: jnp.where(i < starts[1], 0, jnp.where(i < starts[2], 1, 2))

    def body(a_ref, b_ref, *outs):
        i = pl.program_id(0)
        prod = _dot(a_ref[...], b_ref[0])
        for m in range(3):
            @pl.when(which(i) == m)
            def _():
                outs[2 * m][...] = prod
                outs[2 * m + 1][...] = prod.astype(BF16)

    out_specs, out_shape = [], []
    for m in range(3):
        spec = pl.BlockSpec((tm, D_MODEL), lambda i, m=m: (jnp.clip(i - starts[m], 0, counts[m] - 1), 0))
        out_specs += [spec, spec]
        out_shape += [jax.ShapeDtypeStruct((counts[m] * tm, D_MODEL), F32), jax.ShapeDtypeStruct((counts[m] * tm, D_MODEL), BF16)]
    outs, plan_outs = _call(
        body,
        name="dw_attn",
        operands=[actt, cot],
        grid=(sum(counts),),
        in_specs=[pl.BlockSpec((tm, s), lambda i: (i, 0)), pl.BlockSpec((1, s, D_MODEL), lambda i: (which(i), 0, 0))],
        out_specs=out_specs,
        out_shape=out_shape,
        plans=plans,
    )
    return [outs[2 * m : 2 * m + 2] for m in range(3)], plan_outs


def _dh1(name, dproj, w_in, x, dx1, g, tiles, into=None, after=(), plans=()):
    s = x.shape[0]
    tm = min(TM_DH1, s)
    t0, nt = tiles[0], tiles[1] - tiles[0]

    n_chunks = IN_DIM // D_MODEL

    def body(dp_ref, w_ref, x_ref, dx1_ref, g_ref, *rest):
        gx_ref, dg_ref, acc_g, w_scr, dh_scr, w_sems = rest[-6:]
        i = pl.program_id(0)
        cols = lambda c: pl.ds(c * D_MODEL, D_MODEL)
        fetch = lambda c: pltpu.make_async_copy(w_ref.at[:, cols(c)], w_scr.at[:, cols(c)], w_sems.at[c])

        def finish(dh1):
            xh, r = _rms(x_ref[...])
            acc_g[...] += _rows8(dh1 * xh)
            gx_ref[...] = dx1_ref[0].astype(F32) + _rms_bwd(dh1 * g_ref[...], xh, r)

        @pl.when(i == 0)
        def _():
            acc_g[...] = jnp.zeros_like(acc_g)
            for c in range(n_chunks):
                fetch(c).start()
            for c in range(n_chunks):
                fetch(c).wait()
                part = _dot_nt(dp_ref[:, cols(c)], w_scr[:, cols(c)])
                dh_scr[...] = part if c == 0 else dh_scr[...] + part
            finish(dh_scr[...])

        @pl.when(i > 0)
        def _():
            finish(_dot_nt(dp_ref[...], w_scr[...]))

        @pl.when(i == nt - 1)
        def _():
            dg_ref[...] = jnp.sum(acc_g[...], axis=0, keepdims=True)

    rows = pl.BlockSpec((tm, D_MODEL), lambda i: (i + t0, 0))
    return _call(
        body,
        name=name,
        operands=[dproj, w_in, x, dx1, g] + ([into] if into is not None else []) + list(after),
        grid=(nt,),
        in_specs=[
            pl.BlockSpec((tm, IN_DIM), lambda i: (i + t0, 0)),
            pl.BlockSpec(memory_space=pl.ANY),
            rows,
            pl.BlockSpec((1, tm, D_MODEL), lambda i: (0, i + t0, 0)),
            pl.BlockSpec((1, D_MODEL), lambda i: (0, 0)),
        ] + [pl.BlockSpec(memory_space=pl.ANY)] * ((into is not None) + len(after)),
        out_specs=[rows, pl.BlockSpec((1, D_MODEL), lambda i: (0, 0))],
        out_shape=[jax.ShapeDtypeStruct((s, D_MODEL), F32), jax.ShapeDtypeStruct((1, D_MODEL), F32)],
        scratch_shapes=[pltpu.VMEM((8, D_MODEL), F32), pltpu.VMEM((D_MODEL, IN_DIM), BF16), pltpu.VMEM((tm, D_MODEL), F32),
                        pltpu.SemaphoreType.DMA((n_chunks,))],
        plans=plans,
        aliases={5: 0} if into is not None else None,
    )


def _pair_sum(name, grad, got, axis, size, core, plans=()):
    shard = list(grad.shape)
    shard[axis] = size
    r, cdim = shard
    tr = min(r, TR_PAIR_SUM)
    nr = r // tr

    def body(core_ref, g_ref, got_ref, o_ref):
        o_ref[0] = (g_ref[...] + got_ref[0].astype(F32)).astype(BF16)

    if axis == 1:
        g_map = lambda q, t, core_ref: (t, 2 * q + core_ref[0])
    else:
        g_map = lambda q, t, core_ref: ((2 * q + core_ref[0]) * nr + t, 0)
    (out,), plan_outs = _call(
        body,
        name=name,
        prefetch=[core],
        operands=[grad, got],
        grid=(N_CHIP, nr),
        in_specs=[
            pl.BlockSpec((tr, cdim), g_map),
            pl.BlockSpec((1, tr, cdim), lambda q, t, core_ref: (q, t, 0)),
        ],
        out_specs=[pl.BlockSpec((1, tr, cdim), lambda q, t, core_ref: (q, t, 0))],
        out_shape=[jax.ShapeDtypeStruct((N_CHIP, r, cdim), BF16)],
        plans=plans,
    )
    return (out, plan_outs) if plans else out


def _adam_step(w, m, v, g):
    c1 = 1.0 / (1.0 - ADAM_B1**ADAM_STEP)
    c2 = 1.0 / (1.0 - ADAM_B2**ADAM_STEP)
    m_new = ADAM_B1 * m + (1.0 - ADAM_B1) * g
    v_new = ADAM_B2 * v + (1.0 - ADAM_B2) * (g * g)
    return -ADAM_LR * ((m_new * c1) / (jnp.sqrt(v_new * c2) + ADAM_EPS) + ADAM_WD * w), m_new, v_new


def _update_small(dev, vec_w, vec_m, vec_v, conv_w, conv_m, conv_v, small, landed):
    n_vec = len(vec_w)

    def body(dev_ref, *refs):
        refs = list(refs)
        take = lambda k: [refs.pop(0) for _ in range(k)]
        w_refs, m_refs, v_refs = take(n_vec), take(n_vec), take(n_vec)
        cw_ref, cm_ref, cv_ref, small_ref, land_ref, small_conv_ref, land_conv_ref = take(7)
        vec_outs, conv_outs, (loss_ref,) = take(4 * n_vec), take(4), take(1)

        def total(own_ref, others_ref):
            acc = None
            for d in range(N_DEV):
                part = jnp.where(dev_ref[0] == d, own_ref[0], others_ref[d])
                acc = part if acc is None else acc + part
            return acc

        g_all = total(small_ref, land_ref)
        for i in range(n_vec):
            g = g_all[i : i + 1]
            delta, m_new, v_new = _adam_step(w_refs[i][...], m_refs[i][...], v_refs[i][...], g)
            for ref, val in zip(vec_outs[4 * i : 4 * i + 4], (g, delta, m_new, v_new)):
                ref[...] = val
        g_conv = total(small_conv_ref, land_conv_ref)
        for tap in range(3):
            g = g_conv[tap : tap + 1]
            delta, m_new, v_new = _adam_step(cw_ref[tap], cm_ref[tap], cv_ref[tap], g)
            for ref, val in zip(conv_outs, (g, delta, m_new, v_new)):
                ref[tap] = val
        loss_ref[...] = jnp.broadcast_to(0.5 * jnp.sum(g_all[7:8], axis=-1, keepdims=True) / D_MODEL, loss_ref.shape)

    vec_spec = pl.BlockSpec((1, D_MODEL), lambda i, dev_ref: (0, 0))
    conv_spec = pl.BlockSpec((3, 1, CHUNK), lambda i, dev_ref: (0, 0, 0))
    vec_shape = jax.ShapeDtypeStruct((1, D_MODEL), F32)
    conv_shape = jax.ShapeDtypeStruct((3, 1, CHUNK), F32)
    outs, _ = _call(
        body,
        name="update_small",
        prefetch=[dev],
        operands=list(vec_w) + list(vec_m) + list(vec_v) + [conv_w, conv_m, conv_v, small, landed, small, landed],
        grid=(1,),
        in_specs=[vec_spec] * (3 * n_vec) + [conv_spec] * 3 + [
            pl.BlockSpec((1, 16, D_MODEL), lambda i, dev_ref: (0, 0, 0)),
            pl.BlockSpec((N_DEV, 16, D_MODEL), lambda i, dev_ref: (0, 0, 0)),
            pl.BlockSpec((1, 8, CHUNK), lambda i, dev_ref: (0, 1, dev_ref[0])),
            pl.BlockSpec((N_DEV, 8, CHUNK), lambda i, dev_ref: (0, 1, dev_ref[0])),
        ],
        out_specs=[vec_spec] * (4 * n_vec) + [conv_spec] * 4 + [pl.BlockSpec((1, CHUNK), lambda i, dev_ref: (0, 0))],
        out_shape=[vec_shape] * (4 * n_vec) + [conv_shape] * 4 + [jax.ShapeDtypeStruct((1, CHUNK), F32)],
    )
    return [outs[4 * i : 4 * i + 4] for i in range(n_vec)], outs[4 * n_vec : 4 * n_vec + 4], outs[-1]


def _adamw(name, groups, steps, own_slot=None, plans=()):
    lists = lambda p: list(p) if isinstance(p, (list, tuple)) else [p]
    groups = [(g[0], g[1], g[2], lists(g[3]), lists(g[4]) if len(g) > 4 else []) for g in groups]
    counts = [3 + len(g[3]) + len(g[4]) for g in groups]
    use_slot = own_slot is not None

    def first_tiles(w, parts):
        firsts, first = [], 0
        for p in parts:
            firsts.append(first)
            first += p.shape[1] // (w.shape[0] // steps)
        return firsts

    def update(t, slot_ref, firsts, w_ref, m_ref, v_ref, p_refs, o_refs, g_ref, d_ref, nm_ref, nv_ref):
        g = None
        for k, p_ref in enumerate(p_refs):
            gk = None
            for q in range(p_ref.shape[0]):
                part = p_ref[q].astype(F32)
                if o_refs:
                    part = jnp.where(slot_ref[0] == q, o_refs[k][0].astype(F32), part)
                gk = part if gk is None else gk + part
            g = gk if g is None else jnp.where(t >= firsts[k], gk, g)
        g_ref[...] = g
        d_ref[...], nm_ref[...], nv_ref[...] = _adam_step(w_ref[...], m_ref[...], v_ref[...], g)

    def body(*refs):
        refs = list(refs)
        slot_ref = refs.pop(0) if use_slot else None
        t = pl.program_id(0)
        ins, outs = refs[: sum(counts)], refs[sum(counts) :]
        for i, (w, _, _, parts, own) in enumerate(groups):
            mine = ins[sum(counts[:i]) : sum(counts[: i + 1])]
            update(t, slot_ref, first_tiles(w, parts), *mine[:3], mine[3 : 3 + len(parts)], mine[3 + len(parts) :],
                   *outs[4 * i : 4 * i + 4])

    operands, in_specs, out_specs, out_shape = [], [], [], []
    for w, m, v, parts, own in groups:
        r, cdim = w.shape
        tr = r // steps
        firsts = first_tiles(w, parts)

        def tile_of(k, t, firsts=firsts, parts=parts, tr=tr):
            return jnp.clip(t - firsts[k], 0, parts[k].shape[1] // tr - 1)

        blk = pl.BlockSpec((tr, cdim), lambda t, *_: (t, 0))
        operands += [w, m, v] + parts + own
        in_specs += [blk, blk, blk]
        in_specs += [pl.BlockSpec((p.shape[0], tr, cdim), lambda t, *_, k=k, tile_of=tile_of: (0, tile_of(k, t), 0)) for k, p in enumerate(parts)]
        in_specs += [pl.BlockSpec((1, tr, cdim), lambda t, slot_ref, k=k, tile_of=tile_of: (slot_ref[0], tile_of(k, t), 0)) for k in range(len(own))]
        out_specs += [blk] * 4
        out_shape += [jax.ShapeDtypeStruct((r, cdim), F32)] * 4
    outs, plan_outs = _call(
        body,
        name=name,
        prefetch=[own_slot] if use_slot else [],
        operands=operands,
        grid=(steps,),
        in_specs=in_specs,
        out_specs=out_specs,
        out_shape=out_shape,
        plans=plans,
    )
    return [outs[4 * i : 4 * i + 4] for i in range(len(groups))], plan_outs


def _chip_routes(n):
    x, y, c = _place()
    my_chip = 2 * x + y
    return [(a, (*chip, c), 2 * chip[0] + chip[1], my_chip, 2 * chip[0] + chip[1])
            for a in range(n) for chip in _other_chips(x, y)]


def _gather_routes(n):
    x, y, c = _place()
    flip = lambda v, f: 1 - v if f else v
    peers = [(flip(x, fx), flip(y, fy), flip(c, fc)) for fx in (0, 1) for fy in (0, 1) for fc in (0, 1) if fx + fy + fc]
    return [(0, p, 0, 4 * x + 2 * y + c, 4 * p[0] + 2 * p[1] + p[2]) for p in peers]


_chip_routes.copies_per_array = N_CHIP - 1
_gather_routes.copies_per_array = N_DEV - 1


def _split_start(name, routes, srcs, land_shapes):
    n = len(srcs)
    hbm = pl.BlockSpec(memory_space=pltpu.HBM)
    sem = pl.BlockSpec(memory_space=pltpu.SEMAPHORE)

    def body(*refs):
        src_refs, land_refs = refs[:n], refs[n : 2 * n]
        send_sems, recv_sems = refs[2 * n], refs[2 * n + 1]
        token = refs[-1]
        for k, (a, peer, src_slot, there, _) in enumerate(routes(n)):
            pltpu.make_async_remote_copy(
                src_ref=src_refs[a].at[src_slot], dst_ref=land_refs[a].at[there],
                send_sem=send_sems.at[k], recv_sem=recv_sems.at[k], device_id=peer, device_id_type=MESH).start()
        token[...] = jnp.zeros_like(token)

    in_hbm = [pltpu.with_memory_space_constraint(p, pltpu.HBM) for p in srcs]
    in_hbm += [pltpu.with_memory_space_constraint(lax.empty(shape, p.dtype), pltpu.HBM) for p, shape in zip(srcs, land_shapes)]
    n_copies = routes.copies_per_array * n
    res = pl.pallas_call(
        body,
        name=name,
        out_shape=[pltpu.SemaphoreType.DMA((n_copies,)), pltpu.SemaphoreType.DMA((n_copies,))]
        + [pltpu.HBM(p.shape, p.dtype) for p in srcs] + [pltpu.HBM(shape, p.dtype) for p, shape in zip(srcs, land_shapes)]
        + [jax.ShapeDtypeStruct((8, CHUNK), F32)],
        in_specs=[hbm] * (2 * n),
        out_specs=[sem, sem] + [hbm] * (2 * n) + [pl.BlockSpec(memory_space=pltpu.VMEM)],
        input_output_aliases={i: 2 + i for i in range(2 * n)},
        compiler_params=pltpu.CompilerParams(has_side_effects=pltpu.SideEffectType.DATAFLOW_SIDE_EFFECTING),
    )(*in_hbm)
    return res[:-1], res[-1]


def _split_wait(name, routes, in_flight, after):
    n = (len(in_flight) - 2) // 2
    hbm = pl.BlockSpec(memory_space=pltpu.HBM)
    sem = pl.BlockSpec(memory_space=pltpu.SEMAPHORE)
    send_sems, recv_sems, *bufs = in_flight

    def body(*refs):
        src_refs, land_refs = refs[:n], refs[n : 2 * n]
        send_ref, recv_ref = refs[2 * n], refs[2 * n + 1]
        for k, (a, peer, src_slot, _, here) in enumerate(routes(n)):
            cp = pltpu.make_async_remote_copy(
                src_ref=src_refs[a].at[src_slot], dst_ref=land_refs[a].at[here],
                send_sem=send_ref.at[k], recv_sem=recv_ref.at[k], device_id=peer, device_id_type=MESH)
            cp.wait_send()
            cp.wait_recv()

    res = pl.pallas_call(
        body,
        name=name,
        out_shape=[pltpu.HBM(b.shape, b.dtype) for b in bufs],
        in_specs=[hbm] * (2 * n) + [sem, sem] + [pl.BlockSpec(memory_space=pl.ANY)] * len(after),
        out_specs=[hbm] * (2 * n),
        input_output_aliases={i: i for i in range(2 * n)},
        compiler_params=pltpu.CompilerParams(has_side_effects=pltpu.SideEffectType.DATAFLOW_SIDE_EFFECTING),
    )(*bufs, send_sems, recv_sems, *after)
    return res[:n], res[n:]


def kernel(x, mem, norm_mix_g, w_in, conv_w, gm_ln_g, gm_ln_b, gm_ws, gm_bs, w_out, norm_x_g, norm_mem_g, w_q, w_kv, w_xo, norm_final_g, loss_target, m_norm_mix_g, m_w_in, m_conv_w, m_gm_ln_g, m_gm_ln_b, m_gm_ws, m_gm_bs, m_w_out, m_norm_x_g, m_norm_mem_g, m_w_q, m_w_kv, m_w_xo, m_norm_final_g, v_norm_mix_g, v_w_in, v_conv_w, v_gm_ln_g, v_gm_ln_b, v_gm_ws, v_gm_bs, v_w_out, v_norm_x_g, v_norm_mem_g, v_w_q, v_w_kv, v_w_xo, v_norm_final_g):
    s = x.shape[1]
    dev = 4 * lax.axis_index("x") + 2 * lax.axis_index("y") + lax.axis_index("c")
    core = lax.axis_index("c").astype(jnp.int32).reshape(1)
    x2 = x[0]
    target = loss_target[0]
    pad_taps = lambda t: jnp.pad(t[0], ((0, 5), (0, 0)))

    big_names = ["w_in", "w_out", "w_q", "w_kv", "w_xo"]
    big_w = dict(w_in=w_in[0], w_out=w_out[0], w_q=w_q[0], w_kv=w_kv[0], w_xo=w_xo[0])
    big_m = dict(w_in=m_w_in[0], w_out=m_w_out[0], w_q=m_w_q[0], w_kv=m_w_kv[0], w_xo=m_w_xo[0])
    big_v = dict(w_in=v_w_in[0], w_out=v_w_out[0], w_q=v_w_q[0], w_kv=v_w_kv[0], w_xo=v_w_xo[0])
    shard_axis = dict(w_in=1, w_out=0, w_q=0, w_kv=1, w_xo=0)
    full_shape = dict(w_in=(D_MODEL, IN_DIM), w_out=(MIX_DIM, D_MODEL), w_q=(D_MODEL, D_MODEL),
                      w_kv=(D_MODEL, 2 * D_MODEL), w_xo=(D_MODEL, D_MODEL))
    shard_size = {k: big_w[k].shape[shard_axis[k]] for k in big_names}
    others = big_names[1:]
    gather_spec = lambda k: (full_shape[k], shard_axis[k], shard_size[k])
    reduce_spec = lambda k: (shard_axis[k], shard_size[k])
    ws = gm_ws[0]
    bs_b = jnp.broadcast_to(gm_bs[0][:, :, None], (8, CHUNK, CHUNK))
    g_f = norm_final_g.reshape(1, D_MODEL)

    px, py = lax.axis_index("x"), lax.axis_index("y")
    chip_order = jnp.stack([2 * px + py, 2 * (1 - px) + py, 2 * px + 1 - py, 2 * (1 - px) + 1 - py]).astype(jnp.int32)
    (proj, h1, h1t, w_in_full), ((w_out_full, conv_g),) = _inproj(
        x2, norm_mix_g, big_w["w_in"].astype(BF16), chip_order,
        plans=[_gather_plan([big_w["w_out"].astype(BF16), pad_taps(conv_w)[None]], [gather_spec("w_out"), ((N_DEV, 8, CHUNK), 0, 1)])])
    conv_full = conv_g.transpose(1, 0, 2).reshape(8, D_MODEL)
    attn_w = ["w_q", "w_kv", "w_xo"]
    cat, (gathered,) = _mixer_fwd(
        proj, conv_full, gm_ln_g, gm_ln_b, ws, bs_b,
        plans=[_gather_plan([big_w[k].astype(BF16) for k in attn_w], [gather_spec(k) for k in attn_w])])
    wf = dict(zip(attn_w, gathered), w_out=w_out_full)
    k, vv, mt = _kv_fwd(mem[0], norm_mem_g, wf["w_kv"])
    (dcat, actt, cot, dk, dvv, d_gf, d_gx, sq_err) = _attn(
        x2, cat, target, wf["w_out"], wf["w_q"], wf["w_xo"], k, vv, norm_x_g, g_f)

    def pair_sums(names, grads, got):
        return [_pair_sum("pair_sum_" + k, grads[k], got[i], *reduce_spec(k), core) for i, k in enumerate(names)]

    sibling_plan = lambda names, grads: _sibling_plan([grads[k] for k in names], [reduce_spec(k) for k in names])
    early = ["w_kv", "w_out", "w_q", "w_xo"]
    big_g, big_gb = {}, {}
    big_g["w_kv"], big_gb["w_kv"], d_gmem = _kv_bwd(mem[0], wf["w_kv"], mt, dk, dvv)
    dws, (got_kv,) = _dw_attn(actt, cot, plans=[sibling_plan(early[:1], big_gb)])
    for name, (f32, bf16) in zip(early[1:], dws):
        big_g[name], big_gb[name] = f32, bf16
    pair_kv, (got_rest,) = _pair_sum("pair_sum_w_kv", big_g["w_kv"], got_kv[0], *reduce_spec("w_kv"), core,
                                     plans=[sibling_plan(early[1:], big_gb)])
    (dproj, d_cw, d_lng, d_lnb, d_ws, d_bs), (parts_early,) = _mixer_bwd(
        proj, dcat, conv_full, gm_ln_g, gm_ln_b, ws, bs_b,
        plans=[_chip_plan([pair_kv] + pair_sums(early[1:], big_g, got_rest))])
    half = D_MODEL // 2
    in_spec = reduce_spec("w_in")
    (dw_top, dwb_top), _ = _matmul_f32("dw_in_top", h1t, dproj, rows=(0, half))
    (dw_bot, dwb_bot), (got_top, (ws_all,)) = _matmul_f32(
        "dw_in_bot", h1t, dproj, rows=(half, half),
        plans=[_sibling_plan([dwb_top], [in_spec]),
               _gather_plan([d_ws.reshape(1, 8 * CHUNK, CHUNK).astype(BF16)], [((N_DEV, 8 * CHUNK, CHUNK), 0, 1)])])
    pair_top, ((got_bot,),) = _pair_sum("pair_sum_w_in_top", dw_top, got_top[0], *in_spec, core,
                                        plans=[_sibling_plan([dwb_bot], [in_spec])])
    pair_bot = _pair_sum("pair_sum_w_in_bot", dw_bot, got_bot, *in_spec, core)
    chip_shapes = [p.shape for p in (pair_top, pair_bot)]
    chips_in_flight, token = _split_start("grad_chip_exchange_start", _chip_routes, [pair_top, pair_bot], chip_shapes)
    n_tiles = s // min(TM_DH1, s)
    (grad_x, d_gmix), _ = _dh1("dh1", dproj, w_in_full, x2, cot, norm_mix_g, (0, n_tiles), after=[token])
    vec_names = ["norm_mix_g", "gm_ln_g", "gm_ln_b", "gm_bs", "norm_x_g", "norm_mem_g", "norm_final_g"]
    vec_g = [d_gmix, d_lng, d_lnb, d_bs, d_gx, d_gmem, d_gf]
    small = jnp.concatenate(vec_g + [sq_err, d_cw], axis=0)[None]
    small_in_flight, small_token = _split_start("small_all_gather_start", _gather_routes, [small], [(N_DEV, 16, D_MODEL)])
    flat_ws = lambda t: t.reshape(8 * CHUNK, CHUNK)
    updated, _ = _adamw(
        "adamw_early",
        [(big_w[k], big_m[k], big_v[k], got_k) for k, got_k in zip(early, parts_early)]
        + [(flat_ws(ws), flat_ws(m_gm_ws), flat_ws(v_gm_ws), ws_all)], steps=STEPS_ADAMW // 2)
    big_out = dict(zip(early, updated[:4]))
    ws_out = updated[4]
    (pair_top, pair_bot), (land_top, land_bot) = _split_wait(
        "grad_chip_exchange_wait", _chip_routes, chips_in_flight, [small_token] + [u[0] for u in updated])
    my_chip = (2 * px + py).astype(jnp.int32).reshape(1)
    (big_out["w_in"],), _ = _adamw(
        "adamw_w_in", [(big_w["w_in"], big_m["w_in"], big_v["w_in"], [land_top, land_bot], [pair_top, pair_bot])],
        steps=STEPS_ADAMW, own_slot=my_chip)
    (small,), (small_all,) = _split_wait("small_all_gather_wait", _gather_routes, small_in_flight, [big_out["w_in"][0]])
    row = lambda t: t.reshape(1, D_MODEL)
    vec_out, conv_out, loss_row = _update_small(
        dev.astype(jnp.int32).reshape(1),
        [norm_mix_g, gm_ln_g, gm_ln_b, row(gm_bs), norm_x_g, norm_mem_g, row(norm_final_g)],
        [m_norm_mix_g, m_gm_ln_g, m_gm_ln_b, row(m_gm_bs), m_norm_x_g, m_norm_mem_g, row(m_norm_final_g)],
        [v_norm_mix_g, v_gm_ln_g, v_gm_ln_b, row(v_gm_bs), v_norm_x_g, v_norm_mem_g, row(v_norm_final_g)],
        *(t.transpose(1, 0, 2) for t in (conv_w, m_conv_w, v_conv_w)), small, small_all)
    loss = loss_row[0, 0]

    def result(name, which):
        if name in big_out:
            return big_out[name][which][None]
        if name == "conv_w":
            return conv_out[which].transpose(1, 0, 2)
        if name == "gm_ws":
            return ws_out[which].reshape(1, 8, CHUNK, CHUNK)
        r = vec_out[vec_names.index(name)][which]
        if name == "gm_bs":
            return r.reshape(1, 8, CHUNK)
        if name == "norm_final_g":
            return r.reshape(D_MODEL)
        return r

    weights = ["norm_mix_g", "w_in", "conv_w", "gm_ln_g", "gm_ln_b", "gm_ws", "gm_bs", "w_out", "norm_x_g", "norm_mem_g", "w_q", "w_kv", "w_xo", "norm_final_g"]
    outs = [loss, grad_x[None]]
    for which in range(4):
        outs += [result(name, which) for name in weights]
    return tuple(outs)
```

```python
import math

import jax
import jax.numpy as jnp
from jax import lax
from jax.experimental import pallas as pl
from jax.experimental.pallas import tpu as pltpu

F32 = jnp.float32
BF16 = jnp.bfloat16
MESH = pl.DeviceIdType.MESH

D_MODEL = 1024
N_DEV = 8
N_CHIP = 4
CHUNK = 128
GROUPS = 7
IN_DIM = GROUPS * D_MODEL
SHARD_IN = IN_DIM // N_DEV
MIX_DIM = 2 * D_MODEL
X_HEADS = 4
X_HEAD_DIM = D_MODEL // X_HEADS
MEM_LEN = 256
EPS = 1e-6
GELU_K0 = math.sqrt(2.0 / math.pi)
GELU_K1 = 0.044715

ADAM_LR = 0.001
ADAM_B1 = 0.9
ADAM_B2 = 0.999
ADAM_EPS = 1e-08
ADAM_WD = 0.01
ADAM_STEP = 10

TM_INPROJ = 512
TM_ATTN = 512
SUB_ATTN = 256
TM_DH1 = 512
TM_DW = 512
TK_DW = 4096
TR_PAIR_SUM = 1024
BARRIER_ID = {"sibling": 1, "chips": 2, "all": 3}
STEPS_ADAMW = 4


def _dot(a, b):
    return lax.dot_general(a, b, (((1,), (0,)), ((), ())), preferred_element_type=F32)


def _dot_nt(a, b):
    return lax.dot_general(a, b, (((1,), (1,)), ((), ())), preferred_element_type=F32)


def _rows8(v):
    return jnp.sum(v.reshape(v.shape[0] // 8, 8, v.shape[1]), axis=0)


def _gelu(x, with_grad=True):
    x2 = x * x
    t = jnp.tanh(x * (GELU_K0 + (GELU_K0 * GELU_K1) * x2))
    half = 0.5 + 0.5 * t
    g = x * half
    if not with_grad:
        return g
    return g, half + ((0.5 * x) * (1.0 - t * t)) * (GELU_K0 + (3.0 * GELU_K0 * GELU_K1) * x2)


def _silu(z, with_grad=True):
    s = jax.nn.sigmoid(z)
    zs = z * s
    return (zs, s + zs * (1.0 - s)) if with_grad else zs


def _rms(v):
    r = lax.rsqrt(jnp.mean(v * v, axis=-1, keepdims=True) + EPS)
    return v * r, r


def _rms_bwd(dy_g, vh, r):
    return r * (dy_g - vh * jnp.mean(dy_g * vh, axis=-1, keepdims=True))


def _place():
    return lax.axis_index("x"), lax.axis_index("y"), lax.axis_index("c")


def _other_chips(x, y):
    return [(1 - x, y), (x, 1 - y), (1 - x, 1 - y)]


def _pow2_divisor(n):
    return n & (-n)


def _shard_view(ref, axis, size, d):
    start = d * size
    align = _pow2_divisor(size)
    if align > 1:
        start = pl.multiple_of(start, align)
    idx = [slice(None)] * len(ref.shape)
    idx[axis] = pl.ds(start, size)
    return ref.at[tuple(idx)]


class _Plan:
    def __init__(self, ins, out_shape, sems, start, finish, mid=None, aliases=None, peers="all"):
        self.ins, self.out_shape, self.sems = list(ins), list(out_shape), list(sems)
        self.start, self.mid, self.finish = start, mid, finish
        self.aliases = dict(aliases or {})
        self.peers = peers


def _gather_plan(shards, specs):
    n = len(shards)
    n_copy = 7

    def copies(ins, outs, sems):
        send_sems, recv_sems, local_sems = sems
        x, y, c = _place()
        me, sibling = (x, y, c), (x, y, 1 - c)
        chips = _other_chips(x, y)

        def block(a, p):
            _, axis, size = specs[a]
            return _shard_view(outs[a], axis, size, 4 * p[0] + 2 * p[1] + p[2])

        def copy(a, k, p, to, src=None):
            return pltpu.make_async_remote_copy(
                src_ref=block(a, p) if src is None else src,
                dst_ref=block(a, p),
                send_sem=send_sems.at[a * n_copy + k],
                recv_sem=recv_sems.at[a * n_copy + k],
                device_id=to,
                device_id_type=MESH,
            )

        def mine():
            return [pltpu.make_async_copy(ins[a], block(a, me), local_sems.at[a]) for a in range(n)]

        def first():
            return [cp for a in range(n) for cp in
                    [copy(a, 0, me, sibling, src=ins[a])] + [copy(a, 1 + j, me, (*chip, c), src=ins[a]) for j, chip in enumerate(chips)]]

        def landed():
            return [copy(a, 1 + j, (*chip, c), me) for j, chip in enumerate(chips) for a in range(n)]

        def passed():
            return [copy(a, 4 + j, (*chip, c), sibling) for j, chip in enumerate(chips) for a in range(n)]

        def last():
            return [cp for a in range(n) for cp in
                    [copy(a, 0, sibling, me)] + [copy(a, 4 + j, (*chip, 1 - c), me) for j, chip in enumerate(chips)]]

        return mine, first, landed, passed, last

    def start(ins, outs, sems):
        mine, first, _, _, _ = copies(ins, outs, sems)
        for cp in mine() + first():
            cp.start()

    def mid(ins, outs, sems):
        _, _, landed, passed, _ = copies(ins, outs, sems)
        for got, fwd in zip(landed(), passed()):
            got.wait_recv()
            fwd.start()

    def finish(ins, outs, sems):
        mine, first, _, passed, last = copies(ins, outs, sems)
        for cp in last():
            cp.wait_recv()
        for cp in first() + passed():
            cp.wait_send()
        for cp in mine():
            cp.wait()

    return _Plan(
        shards,
        [jax.ShapeDtypeStruct(full, s.dtype) for s, (full, _, _) in zip(shards, specs)],
        [pltpu.SemaphoreType.DMA((n * n_copy,)), pltpu.SemaphoreType.DMA((n * n_copy,)), pltpu.SemaphoreType.DMA((n,))],
        start, finish, mid,
    )


def _sibling_plan(grads, specs):
    n = len(grads)

    def shard_shape(a):
        axis, size = specs[a]
        shp = list(grads[a].shape)
        shp[axis] = size
        return tuple(shp)

    def copies(ins, outs, sems):
        send_sems, recv_sems = sems
        x, y, c = _place()
        return [
            pltpu.make_async_remote_copy(
                src_ref=_shard_view(ins[a], specs[a][0], specs[a][1], 2 * q + (1 - c)),
                dst_ref=outs[a].at[q],
                send_sem=send_sems.at[a * N_CHIP + q],
                recv_sem=recv_sems.at[a * N_CHIP + q],
                device_id=(x, y, 1 - c),
                device_id_type=MESH,
            )
            for a in range(n) for q in range(N_CHIP)
        ]

    def start(ins, outs, sems):
        for cp in copies(ins, outs, sems):
            cp.start()

    def finish(ins, outs, sems):
        for cp in copies(ins, outs, sems):
            cp.wait()

    return _Plan(
        grads,
        [jax.ShapeDtypeStruct((N_CHIP, *shard_shape(a)), grads[a].dtype) for a in range(n)],
        [pltpu.SemaphoreType.DMA((n * N_CHIP,)), pltpu.SemaphoreType.DMA((n * N_CHIP,))],
        start, finish, peers="sibling",
    )


def _chip_plan(parts, full_rows=None, row0=None, into=None):
    n = len(parts)
    full_rows = [p.shape[1] for p in parts] if full_rows is None else full_rows
    row0 = [0] * n if row0 is None else row0

    def copies(ins, outs, sems):
        send_sems, recv_sems, local_sems = sems
        x, y, c = _place()
        my_chip = 2 * x + y
        chips = _other_chips(x, y)

        def land(a, q):
            return outs[a].at[q, pl.ds(row0[a], parts[a].shape[1]), :]

        def mine():
            return [pltpu.make_async_copy(ins[a].at[my_chip], land(a, my_chip), local_sems.at[a]) for a in range(n)]

        def remote(sending):
            return [
                pltpu.make_async_remote_copy(
                    src_ref=ins[a].at[2 * chip[0] + chip[1]],
                    dst_ref=land(a, my_chip if sending else 2 * chip[0] + chip[1]),
                    send_sem=send_sems.at[a * 3 + j], recv_sem=recv_sems.at[a * 3 + j],
                    device_id=(*chip, c), device_id_type=MESH)
                for a in range(n) for j, chip in enumerate(chips)
            ]

        return mine, remote

    def start(ins, outs, sems):
        mine, remote = copies(ins, outs, sems)
        for cp in mine() + remote(True):
            cp.start()

    def finish(ins, outs, sems):
        mine, remote = copies(ins, outs, sems)
        for cp in remote(False):
            cp.wait_recv()
        for cp in remote(True):
            cp.wait_send()
        for cp in mine():
            cp.wait()

    return _Plan(
        list(parts) + (list(into) if into is not None else []),
        [jax.ShapeDtypeStruct((N_CHIP, full_rows[a], parts[a].shape[2]), parts[a].dtype) for a in range(n)],
        [pltpu.SemaphoreType.DMA((n * 3,)), pltpu.SemaphoreType.DMA((n * 3,)), pltpu.SemaphoreType.DMA((n,))],
        start, finish,
        aliases={n + a: a for a in range(n)} if into is not None else None, peers="chips",
    )


def _call(body, *, name, operands, out_shape, grid=(), in_specs=None, out_specs=None, scratch_shapes=(),
          plans=(), aliases=None, prefetch=(), start_after_body=False, pass_on_at=0.6, body_peers=None):
    operands, out_shape, scratch_shapes = list(operands), list(out_shape), list(scratch_shapes)
    n_pf, n_in, n_out, n_scr = len(prefetch), len(operands), len(out_shape), len(scratch_shapes)
    in_vmem = pl.BlockSpec(memory_space=pltpu.VMEM)
    in_hbm = pl.BlockSpec(memory_space=pl.ANY)
    in_specs = [in_vmem] * n_in if in_specs is None else list(in_specs)
    out_specs = [in_vmem] * n_out if out_specs is None else list(out_specs)
    io_alias = {n_pf + i: o for i, o in (aliases or {}).items()}
    pos_in, pos_out = n_pf + n_in, n_out
    for p in plans:
        for i, o in p.aliases.items():
            io_alias[pos_in + i] = pos_out + o
        pos_in += len(p.ins)
        pos_out += len(p.out_shape)
    steps = math.prod(grid) if grid else 1
    mid_step = min(steps - 1, int(steps * pass_on_at))
    patterns = {p.peers for p in plans} | ({body_peers} if body_peers else set())
    pattern = None if not patterns else patterns.pop() if len(patterns) == 1 else "all"

    def shake_hands():
        x, y, c = _place()
        peers = [(x, y, 1 - c)] if pattern in ("sibling", "all") else []
        peers += [(*chip, c) for chip in _other_chips(x, y)] if pattern in ("chips", "all") else []
        barrier = pltpu.get_barrier_semaphore()
        for peer in peers:
            pl.semaphore_signal(barrier, inc=1, device_id=peer, device_id_type=MESH)
        pl.semaphore_wait(barrier, len(peers))

    def wrapped(*refs):
        refs = list(refs)
        take = lambda k: [refs.pop(0) for _ in range(k)]
        pf = take(n_pf)
        ins, p_ins = take(n_in), [take(len(p.ins)) for p in plans]
        outs, p_outs = take(n_out), [take(len(p.out_shape)) for p in plans]
        scr, p_sems = take(n_scr), [take(len(p.sems)) for p in plans]
        step = 0
        for ax, g in enumerate(grid):
            step = step * g + pl.program_id(ax)

        def hook(kind, at):
            todo = [(getattr(p, kind), a, b, c) for p, a, b, c in zip(plans, p_ins, p_outs, p_sems) if getattr(p, kind)]

            def run():
                for fn, a, b, c in todo:
                    fn(a, b, c)

            if todo and grid:
                pl.when(step == at)(run)
            elif todo:
                run()

        if pattern and grid:
            pl.when(step == 0)(shake_hands)
        elif pattern:
            shake_hands()
        if not start_after_body:
            hook("start", 0)
        if body is not None:
            body(*pf, *ins, *outs, *scr)
        if start_after_body:
            hook("start", 0)
        hook("mid", mid_step)
        hook("finish", steps - 1)

    layout = dict(
        grid=grid,
        in_specs=in_specs + [in_hbm] * sum(len(p.ins) for p in plans),
        out_specs=out_specs + [in_hbm] * sum(len(p.out_shape) for p in plans),
        scratch_shapes=scratch_shapes + [s for p in plans for s in p.sems],
    )
    if n_pf:
        layout = dict(grid_spec=pltpu.PrefetchScalarGridSpec(num_scalar_prefetch=n_pf, **layout))
    res = pl.pallas_call(
        wrapped,
        name=name,
        out_shape=out_shape + [o for p in plans for o in p.out_shape],
        input_output_aliases=io_alias,
        compiler_params=pltpu.CompilerParams(
            dimension_semantics=("arbitrary",) * len(grid) if grid else None,
            collective_id=BARRIER_ID[pattern] if pattern else None),
        **layout,
    )(*prefetch, *operands, *[a for p in plans for a in p.ins])
    res = list(res)
    outs, plan_outs = res[:n_out], []
    pos = n_out
    for p in plans:
        plan_outs.append(res[pos : pos + len(p.out_shape)])
        pos += len(p.out_shape)
    return outs, plan_outs


def _inproj(x, g, w_shard, order, plans=()):
    s = x.shape[0]
    tm = min(TM_INPROJ, s)
    nt = s // tm
    wide = 2 * SHARD_IN
    fwd_at = nt // 2

    def body(order_ref, x_ref, g_ref, w_hbm, proj_ref, h1_ref, h1t_ref, wfull_hbm,
             w_scr, h1_scr, send_sems, recv_sems, local_sems):
        k, i = pl.program_id(0), pl.program_id(1)
        px, py, c = _place()
        me, sibling = (px, py, c), (px, py, 1 - c)
        chips = _other_chips(px, py)

        def half(slot, core):
            return w_scr.at[slot, :, pl.ds(pl.multiple_of(core * SHARD_IN, 128), SHARD_IN)]

        def copy(n, slot, core, to, src=None):
            return pltpu.make_async_remote_copy(
                src_ref=half(slot, core) if src is None else src, dst_ref=half(slot, core),
                send_sem=send_sems.at[n], recv_sem=recv_sems.at[n], device_id=to, device_id_type=MESH)

        def mine():
            return pltpu.make_async_copy(w_hbm, half(0, c), local_sems.at[0])

        def first():
            return [copy(0, 0, c, sibling, src=w_hbm)] + [copy(1 + j, 1 + j, c, (*chips[j], c), src=w_hbm) for j in range(2)]

        def relay():
            to = (c * px + (1 - c) * (1 - px), c * (1 - py) + (1 - c) * py, c)
            return pltpu.make_async_remote_copy(
                src_ref=half(2 - c, c), dst_ref=half(3, c), send_sem=send_sems.at[3], recv_sem=recv_sems.at[3],
                device_id=to, device_id_type=MESH)

        def save(slot):
            cols = pl.ds(pl.multiple_of(order_ref[slot] * wide, 128), wide)
            return pltpu.make_async_copy(w_scr.at[slot], wfull_hbm.at[:, cols], local_sems.at[1 + slot])

        @pl.when((k == 0) & (i == 0))
        def _():
            mine().start()
            for cp in first():
                cp.start()
            mine().wait()
            copy(0, 0, 1 - c, me).wait_recv()

        for j in range(3):
            @pl.when((k == j + 1) & (i == 0))
            def _():
                copy(4 + j, 1 + j, 1 - c, me).wait_recv()

        @pl.when((k == 3) & (i == 0))
        def _():
            for slot in range(N_CHIP):
                save(slot).start()

        @pl.when(k == 0)
        def _():
            xh, _ = _rms(x_ref[...])
            h = xh * g_ref[...]
            h1_ref[...] = h.astype(BF16)
            h1t_ref[...] = h.T.astype(BF16)
            h1_scr[pl.ds(pl.multiple_of(i * tm, tm), tm), :] = h.astype(BF16)

        proj_ref[...] = _dot(h1_scr[pl.ds(pl.multiple_of(i * tm, tm), tm), :], w_scr[k]).astype(BF16)

        for j in range(3):
            @pl.when((k == 0) & (i == nt - 1) if j < 2 else (k == 2) & (i == fwd_at))
            def _():
                copy(1 + j, 1 + j, c, me).wait_recv()
                copy(4 + j, 1 + j, c, sibling).start()
                if j < 2:
                    pl.when(c == 1 - j)(lambda: relay().start())

        @pl.when((k == 3) & (i == nt - 1))
        def _():
            for cp in first() + [relay()] + [copy(4 + j, 1 + j, c, sibling) for j in range(3)]:
                cp.wait_send()
            for slot in range(N_CHIP):
                save(slot).wait()

    once = lambda k, i, order_ref: (jnp.where(k == 0, i, nt - 1), 0)
    return _call(
        body,
        name="inproj",
        prefetch=[order],
        operands=[x, g, w_shard],
        grid=(N_CHIP, nt),
        in_specs=[
            pl.BlockSpec((tm, D_MODEL), once),
            pl.BlockSpec((1, D_MODEL), lambda k, i, order_ref: (0, 0)),
            pl.BlockSpec(memory_space=pl.ANY),
        ],
        out_specs=[
            pl.BlockSpec((tm, wide), lambda k, i, order_ref: (i, order_ref[k])),
            pl.BlockSpec((tm, D_MODEL), once),
            pl.BlockSpec((D_MODEL, tm), lambda k, i, order_ref: (0, jnp.where(k == 0, i, nt - 1))),
            pl.BlockSpec(memory_space=pl.ANY),
        ],
        out_shape=[
            jax.ShapeDtypeStruct((s, IN_DIM), BF16),
            jax.ShapeDtypeStruct((s, D_MODEL), BF16),
            jax.ShapeDtypeStruct((D_MODEL, s), BF16),
            jax.ShapeDtypeStruct((D_MODEL, IN_DIM), BF16),
        ],
        scratch_shapes=[
            pltpu.VMEM((N_CHIP, D_MODEL, wide), BF16),
            pltpu.VMEM((s, D_MODEL), BF16),
            pltpu.SemaphoreType.DMA((7,)),
            pltpu.SemaphoreType.DMA((7,)),
            pltpu.SemaphoreType.DMA((1 + N_CHIP,)),
        ],
        plans=plans,
        start_after_body=True,
        pass_on_at=0.8,
        body_peers="all",
    )


def _rows_from_above(v, above):
    row = lax.broadcasted_iota(jnp.int32, (8, v.shape[1]), 0)
    r1, r2 = pltpu.roll(v, 1, 0), pltpu.roll(v, 2, 0)
    top1 = jnp.where(row == 0, above[7:8], r1[0:8])
    top2 = jnp.where(row == 0, above[6:7], jnp.where(row == 1, above[7:8], r2[0:8]))
    return jnp.concatenate([top1, r1[8:]], axis=0), jnp.concatenate([top2, r2[8:]], axis=0)


def _rows_from_below(v, below):
    n = v.shape[0]
    row = lax.broadcasted_iota(jnp.int32, (8, v.shape[1]), 0)
    r1, r2 = pltpu.roll(v, n - 1, 0), pltpu.roll(v, n - 2, 0)
    end1 = jnp.where(row == 7, below[0:1], r1[n - 8 :])
    end2 = jnp.where(row == 6, below[0:1], jnp.where(row == 7, below[1:2], r2[n - 8 :]))
    return jnp.concatenate([r1[: n - 8], end1], axis=0), jnp.concatenate([r2[: n - 8], end2], axis=0)


def _causal_mask():
    row = lax.broadcasted_iota(jnp.int32, (CHUNK, CHUNK), 0)
    col = lax.broadcasted_iota(jnp.int32, (CHUNK, CHUNK), 1)
    return row, row >= col


def _mixer_fwd(proj, conv_w, ln_g, ln_b, ws, bs_b, plans=()):
    s = proj.shape[0]
    nt = s // CHUNK

    def body(proj_ref, cw_ref, lng_ref, lnb_ref, ws_ref, bsb_ref, cat_ref, pcar_ref):
        @pl.when(pl.program_id(0) == 0)
        def _():
            pcar_ref[...] = jnp.zeros_like(pcar_ref)

        row, tril = _causal_mask()
        for j in range(8):
            cs = slice(CHUNK * j, CHUNK * (j + 1))

            def grp(k):
                return proj_ref[:, k * D_MODEL + CHUNK * j : k * D_MODEL + CHUNK * (j + 1)].astype(F32)

            w = cw_ref[:, cs]
            p = grp(1) * grp(2)
            p1, p2 = _rows_from_above(p, pcar_ref[:, cs])
            pcar_ref[:, cs] = p[CHUNK - 8 :]
            cv = w[0:1] * p2 + w[1:2] * p1 + w[2:3] * p
            sa = _silu(grp(3), with_grad=False)
            cat_ref[:, cs] = ((grp(0) * cv) * sa).astype(BF16)
            gu = _gelu(grp(4), with_grad=False)
            gv = _gelu(grp(5), with_grad=False)
            dv = gv - jnp.mean(gv, axis=-1, keepdims=True)
            vn = dv * lax.rsqrt(jnp.mean(dv * dv, axis=-1, keepdims=True) + EPS)
            vn = vn * lng_ref[:, cs] + lnb_ref[:, cs]
            wc = jnp.where(tril, ws_ref[j], 0.0).astype(BF16)
            sp = _dot(wc, vn.astype(BF16)) + bsb_ref[j]
            sb = _silu(grp(6), with_grad=False)
            cat_ref[:, D_MODEL + CHUNK * j : D_MODEL + CHUNK * (j + 1)] = ((gu * sp) * sb).astype(BF16)

    whole = lambda shape: pl.BlockSpec(shape, lambda i: (0,) * len(shape))
    (cat,), plan_outs = _call(
        body,
        name="mixer_fwd",
        operands=[proj, conv_w, ln_g, ln_b, ws, bs_b],
        grid=(nt,),
        in_specs=[
            pl.BlockSpec((CHUNK, IN_DIM), lambda i: (i, 0)),
            whole((8, D_MODEL)),
            whole((1, D_MODEL)),
            whole((1, D_MODEL)),
            whole((8, CHUNK, CHUNK)),
            whole((8, CHUNK, CHUNK)),
        ],
        out_specs=[pl.BlockSpec((CHUNK, MIX_DIM), lambda i: (i, 0))],
        out_shape=[jax.ShapeDtypeStruct((s, MIX_DIM), BF16)],
        scratch_shapes=[pltpu.VMEM((8, D_MODEL), F32)],
        plans=plans,
        pass_on_at=0.85,
    )
    return cat, plan_outs


def _mixer_bwd(proj, dcat, conv_w, ln_g, ln_b, ws, bs_b, plans=()):
    s = proj.shape[0]
    nt = s // CHUNK

    def body(proj_ref, halo_ref, dcat_ref, cw_ref, lng_ref, lnb_ref, ws_ref, bsb_ref,
             dproj_ref, dcw_ref, dlng_ref, dlnb_ref, dws_ref, dbs_ref,
             car_ref, acc_cw, acc_lng, acc_lnb, acc_bs, wc_scr, wct_scr):
        i = pl.program_id(0)
        tile = nt - 1 - i
        row, tril = _causal_mask()

        @pl.when(i == 0)
        def _():
            for j in range(8):
                wc = jnp.where(tril, ws_ref[j], 0.0)
                wc_scr[j] = wc.astype(BF16)
                wct_scr[j] = wc.T.astype(BF16)
            car_ref[...] = jnp.zeros_like(car_ref)
            acc_cw[...] = jnp.zeros_like(acc_cw)
            acc_lng[...] = jnp.zeros_like(acc_lng)
            acc_lnb[...] = jnp.zeros_like(acc_lnb)
            acc_bs[...] = jnp.zeros_like(acc_bs)
            dws_ref[...] = jnp.zeros_like(dws_ref)

        has_prev = jnp.where(tile > 0, 1.0, 0.0).astype(F32)
        for j in range(8):
            cs = slice(CHUNK * j, CHUNK * (j + 1))

            def col(k):
                return slice(k * D_MODEL + CHUNK * j, k * D_MODEL + CHUNK * (j + 1))

            def grp(k):
                return proj_ref[:, col(k)].astype(F32)

            w = cw_ref[:, cs]
            gb, gc, xa = grp(0), grp(1), grp(2)
            p = gc * xa
            pprev = halo_ref[:, col(1)].astype(F32) * halo_ref[:, col(2)].astype(F32) * has_prev
            p1, p2 = _rows_from_above(p, pprev[8:16])
            cv = w[0:1] * p2 + w[1:2] * p1 + w[2:3] * p
            za = grp(3)
            sa, dsa = _silu(za)
            da = dcat_ref[:, cs].astype(F32)
            da_gb = da * gb
            dproj_ref[:, col(0)] = ((da * sa) * cv).astype(BF16)
            dproj_ref[:, col(3)] = ((da_gb * cv) * dsa).astype(BF16)
            dcv = da_gb * sa
            d1, d2 = _rows_from_below(dcv, car_ref[:, cs])
            car_ref[:, cs] = dcv[0:8]
            dp = w[2:3] * dcv + w[1:2] * d1 + w[0:1] * d2
            dproj_ref[:, col(1)] = (dp * xa).astype(BF16)
            dproj_ref[:, col(2)] = (dp * gc).astype(BF16)
            acc_cw[0, :, cs] += _rows8(dcv * p2)
            acc_cw[1, :, cs] += _rows8(dcv * p1)
            acc_cw[2, :, cs] += _rows8(dcv * p)
            gu, dgu = _gelu(grp(4))
            gv, dgv = _gelu(grp(5))
            dv = gv - jnp.mean(gv, axis=-1, keepdims=True)
            rstd = lax.rsqrt(jnp.mean(dv * dv, axis=-1, keepdims=True) + EPS)
            vnh = dv * rstd
            lng = lng_ref[:, cs]
            vnb = (vnh * lng + lnb_ref[:, cs]).astype(BF16)
            sp = _dot(wc_scr[j], vnb) + bsb_ref[j]
            zb = grp(6)
            sb, dsb = _silu(zb)
            db = dcat_ref[:, D_MODEL + CHUNK * j : D_MODEL + CHUNK * (j + 1)].astype(F32)
            db_sp, db_gu = db * sp, db * gu
            dproj_ref[:, col(4)] = ((db_sp * sb) * dgu).astype(BF16)
            dproj_ref[:, col(6)] = ((db_sp * gu) * dsb).astype(BF16)
            dsp = db_gu * sb
            acc_bs[j] += dsp
            dspb = dsp.astype(BF16)
            dws_ref[j] += _dot_nt(dspb, vnb)
            dvn = _dot(wct_scr[j], dspb)
            acc_lnb[:, cs] += _rows8(dvn)
            acc_lng[:, cs] += _rows8(dvn * vnh)
            dvh = dvn * lng
            dgvv = rstd * (dvh - jnp.mean(dvh, axis=-1, keepdims=True) - vnh * jnp.mean(dvh * vnh, axis=-1, keepdims=True))
            dproj_ref[:, col(5)] = (dgvv * dgv).astype(BF16)

        @pl.when(i == nt - 1)
        def _():
            dlng_ref[...] = jnp.sum(acc_lng[...], axis=0, keepdims=True)
            dlnb_ref[...] = jnp.sum(acc_lnb[...], axis=0, keepdims=True)
            dcw_ref[...] = jnp.zeros_like(dcw_ref)
            for k in range(3):
                dcw_ref[k : k + 1, :] = jnp.sum(acc_cw[k], axis=0, keepdims=True)
            for j in range(8):
                dws_ref[j] = jnp.where(tril, dws_ref[j], 0.0)
                dbs_ref[:, CHUNK * j : CHUNK * (j + 1)] = jnp.sum(acc_bs[j].T, axis=0, keepdims=True)

    whole = lambda shape: pl.BlockSpec(shape, lambda i: (0,) * len(shape))
    halo_rows = 16
    per = CHUNK // halo_rows
    return _call(
        body,
        name="mixer_bwd",
        operands=[proj, proj, dcat, conv_w, ln_g, ln_b, ws, bs_b],
        plans=plans,
        grid=(nt,),
        in_specs=[
            pl.BlockSpec((CHUNK, IN_DIM), lambda i: (nt - 1 - i, 0)),
            pl.BlockSpec((halo_rows, IN_DIM), lambda i: (jnp.maximum((nt - 1 - i) * per - 1, 0), 0)),
            pl.BlockSpec((CHUNK, MIX_DIM), lambda i: (nt - 1 - i, 0)),
            whole((8, D_MODEL)),
            whole((1, D_MODEL)),
            whole((1, D_MODEL)),
            whole((8, CHUNK, CHUNK)),
            whole((8, CHUNK, CHUNK)),
        ],
        out_specs=[
            pl.BlockSpec((CHUNK, IN_DIM), lambda i: (nt - 1 - i, 0)),
            whole((8, D_MODEL)),
            whole((1, D_MODEL)),
            whole((1, D_MODEL)),
            whole((8, CHUNK, CHUNK)),
            whole((1, D_MODEL)),
        ],
        out_shape=[
            jax.ShapeDtypeStruct((s, IN_DIM), BF16),
            jax.ShapeDtypeStruct((8, D_MODEL), F32),
            jax.ShapeDtypeStruct((1, D_MODEL), F32),
            jax.ShapeDtypeStruct((1, D_MODEL), F32),
            jax.ShapeDtypeStruct((8, CHUNK, CHUNK), F32),
            jax.ShapeDtypeStruct((1, D_MODEL), F32),
        ],
        scratch_shapes=[
            pltpu.VMEM((8, D_MODEL), F32),
            pltpu.VMEM((3, 8, D_MODEL), F32),
            pltpu.VMEM((8, D_MODEL), F32),
            pltpu.VMEM((8, D_MODEL), F32),
            pltpu.VMEM((8, CHUNK, CHUNK), F32),
            pltpu.VMEM((8, CHUNK, CHUNK), BF16),
            pltpu.VMEM((8, CHUNK, CHUNK), BF16),
        ],
    )


def _kv_fwd(mem, g_mem, w_kv):
    def body(mem_ref, g_ref, w_ref, k_ref, vv_ref, mt_ref):
        mh, _ = _rms(mem_ref[...])
        m = mh * g_ref[...]
        kv = _dot(m.astype(BF16), w_ref[...])
        k_ref[...] = kv[:, :D_MODEL].astype(BF16)
        vv_ref[...] = kv[:, D_MODEL:].astype(BF16)
        mt_ref[...] = m.T.astype(BF16)

    return pl.pallas_call(
        body,
        name="kv_fwd",
        out_shape=[
            jax.ShapeDtypeStruct((MEM_LEN, D_MODEL), BF16),
            jax.ShapeDtypeStruct((MEM_LEN, D_MODEL), BF16),
            jax.ShapeDtypeStruct((D_MODEL, MEM_LEN), BF16),
        ],
    )(mem, g_mem, w_kv)


def _kv_bwd(mem, w_kv, mt, dk, dvv):
    def body(mem_ref, w_ref, mt_ref, dk_ref, dvv_ref, dw_ref, dg_ref):
        dkv = jnp.concatenate([dk_ref[...], dvv_ref[...]], axis=1).astype(BF16)
        dw_ref[...] = _dot(mt_ref[...], dkv).astype(BF16)
        dm = _dot_nt(dkv, w_ref[...])
        mh, _ = _rms(mem_ref[...])
        dg_ref[...] = jnp.sum(dm * mh, axis=0, keepdims=True)

    return pl.pallas_call(
        body,
        name="kv_bwd",
        out_shape=[jax.ShapeDtypeStruct((D_MODEL, 2 * D_MODEL), BF16), jax.ShapeDtypeStruct((1, D_MODEL), F32)],
    )(mem, w_kv, mt, dk, dvv)


def _attn(x, cat, target, w_out, w_q, w_xo, k, vv, g_x, g_f):
    s = x.shape[0]
    tm = min(TM_ATTN, s)
    sub = min(SUB_ATTN, tm)
    nt = s // tm
    scale = 1.0 / math.sqrt(X_HEAD_DIM)

    def body(x_ref, cat_ref, t_ref, wout_ref, wq_ref, wxo_ref, k_ref, vv_ref, gx_ref, gf_ref,
             dcat_ref, actt_ref, cot_ref,
             dk_ref, dvv_ref, dgf_ref, dgx_ref, loss_ref,
             p_scr, q_scr, o_scr, dq_scr, acc_gf, acc_gx, acc_loss):
        i = pl.program_id(0)

        @pl.when(i == 0)
        def _():
            dk_ref[...] = jnp.zeros_like(dk_ref)
            dvv_ref[...] = jnp.zeros_like(dvv_ref)
            acc_gf[...] = jnp.zeros_like(acc_gf)
            acc_gx[...] = jnp.zeros_like(acc_gx)
            acc_loss[...] = jnp.zeros_like(acc_loss)

        subs = [slice(u * sub, (u + 1) * sub) for u in range(tm // sub)]
        heads = [slice(X_HEAD_DIM * h, X_HEAD_DIM * (h + 1)) for h in range(X_HEADS)]
        gx, gf = gx_ref[...], gf_ref[...]
        x1, x1h, r2 = [], [], []
        for rs in subs:
            cat = cat_ref[rs, :]
            actt_ref[0:MIX_DIM, rs] = cat.astype(F32).T.astype(BF16)
            v = x_ref[rs, :] + _dot(cat, wout_ref[...])
            vh, r = _rms(v)
            h2 = vh * gx
            actt_ref[MIX_DIM : MIX_DIM + D_MODEL, rs] = h2.T.astype(BF16)
            q_scr[rs, :] = _dot(h2.astype(BF16), wq_ref[...]).astype(BF16)
            x1.append(v), x1h.append(vh), r2.append(r)
        for rs in subs:
            for hs in heads:
                sc = _dot_nt(q_scr[rs, hs], k_ref[:, hs]) * scale
                e = jnp.exp(sc - jnp.max(sc, axis=-1, keepdims=True))
                p = e / jnp.sum(e, axis=-1, keepdims=True)
                p_scr[rs, hs] = p
                o_scr[rs, hs] = _dot(p.astype(BF16), vv_ref[:, hs])
        dx2, sq, d_gf = [], None, None
        for u, rs in enumerate(subs):
            o = o_scr[rs, :]
            actt_ref[MIX_DIM + D_MODEL :, rs] = o.T.astype(BF16)
            x2 = x1[u] + _dot(o.astype(BF16), wxo_ref[...])
            x2h, r3 = _rms(x2)
            err = x2h * gf - t_ref[rs, :]
            dy = err * (1.0 / D_MODEL)
            sq = _rows8(err * err) if sq is None else sq + _rows8(err * err)
            d_gf = _rows8(dy * x2h) if d_gf is None else d_gf + _rows8(dy * x2h)
            dx2.append(_rms_bwd(dy * gf, x2h, r3))
        acc_loss[...] += sq
        acc_gf[...] += d_gf
        d_vv, d_k = [None] * X_HEADS, [None] * X_HEADS
        for u, rs in enumerate(subs):
            dx2b = dx2[u].astype(BF16)
            cot_ref[2, rs, :] = dx2b
            dob = _dot_nt(dx2b, wxo_ref[...]).astype(BF16)
            for h, hs in enumerate(heads):
                p = p_scr[rs, hs]
                dp = _dot_nt(dob[:, hs], vv_ref[:, hs])
                ds = p * (dp - jnp.sum(dp * p, axis=-1, keepdims=True)) * scale
                dq_scr[rs, hs] = _dot(ds.astype(BF16), k_ref[:, hs]).astype(BF16)
                dvv_h = _dot(p.T.astype(BF16), dob[:, hs])
                dk_h = _dot(ds.T.astype(BF16), q_scr[rs, hs])
                d_vv[h] = dvv_h if d_vv[h] is None else d_vv[h] + dvv_h
                d_k[h] = dk_h if d_k[h] is None else d_k[h] + dk_h
        for h, hs in enumerate(heads):
            dvv_ref[:, hs] += d_vv[h]
            dk_ref[:, hs] += d_k[h]
        d_gx = None
        for u, rs in enumerate(subs):
            dqb = dq_scr[rs, :]
            cot_ref[1, rs, :] = dqb
            dh2 = _dot_nt(dqb, wq_ref[...])
            d_gx = _rows8(dh2 * x1h[u]) if d_gx is None else d_gx + _rows8(dh2 * x1h[u])
            dx1 = dx2[u] + _rms_bwd(dh2 * gx, x1h[u], r2[u])
            dx1b = dx1.astype(BF16)
            cot_ref[0, rs, :] = dx1b
            dcat_ref[rs, :] = _dot_nt(dx1b, wout_ref[...]).astype(BF16)
        acc_gx[...] += d_gx

        @pl.when(i == nt - 1)
        def _():
            dgf_ref[...] = jnp.sum(acc_gf[...], axis=0, keepdims=True)
            dgx_ref[...] = jnp.sum(acc_gx[...], axis=0, keepdims=True)
            loss_ref[...] = jnp.sum(acc_loss[...], axis=0, keepdims=True)

    whole = lambda shape: pl.BlockSpec(shape, lambda i: (0,) * len(shape))
    once = lambda shape: pl.BlockSpec(shape, lambda i: (0,) * len(shape), pipeline_mode=pl.Buffered(1))
    rows = lambda width: pl.BlockSpec((tm, width), lambda i: (i, 0))
    cols = lambda height: pl.BlockSpec((height, tm), lambda i: (0, i))
    vec = jax.ShapeDtypeStruct((1, D_MODEL), F32)
    return pl.pallas_call(
        body,
        name="attn",
        grid=(nt,),
        in_specs=[
            rows(D_MODEL), rows(MIX_DIM), rows(D_MODEL),
            once((MIX_DIM, D_MODEL)), once((D_MODEL, D_MODEL)), once((D_MODEL, D_MODEL)),
            once((MEM_LEN, D_MODEL)), once((MEM_LEN, D_MODEL)),
            whole((1, D_MODEL)), whole((1, D_MODEL)),
        ],
        out_specs=[
            rows(MIX_DIM), cols(2 * MIX_DIM),
            pl.BlockSpec((3, tm, D_MODEL), lambda i: (0, i, 0)),
            whole((MEM_LEN, D_MODEL)), whole((MEM_LEN, D_MODEL)),
            whole((1, D_MODEL)), whole((1, D_MODEL)), whole((1, D_MODEL)),
        ],
        out_shape=[
            jax.ShapeDtypeStruct((s, MIX_DIM), BF16),
            jax.ShapeDtypeStruct((2 * MIX_DIM, s), BF16),
            jax.ShapeDtypeStruct((3, s, D_MODEL), BF16),
            jax.ShapeDtypeStruct((MEM_LEN, D_MODEL), F32),
            jax.ShapeDtypeStruct((MEM_LEN, D_MODEL), F32),
            vec, vec, vec,
        ],
        scratch_shapes=[
            pltpu.VMEM((tm, D_MODEL), F32),
            pltpu.VMEM((tm, D_MODEL), BF16),
            pltpu.VMEM((tm, D_MODEL), F32),
            pltpu.VMEM((tm, D_MODEL), BF16),
            pltpu.VMEM((8, D_MODEL), F32),
            pltpu.VMEM((8, D_MODEL), F32),
            pltpu.VMEM((8, D_MODEL), F32),
        ],
        compiler_params=pltpu.CompilerParams(dimension_semantics=("arbitrary",)),
    )(x, cat, target, w_out, w_q, w_xo, k, vv, g_x, g_f)


def _matmul_bf16(name, a, b, rows=None, plans=()):
    first, m = (0, a.shape[0]) if rows is None else rows
    kk = a.shape[1]
    n = b.shape[1]
    tm, tn, tk = min(m, TM_DW), min(n, D_MODEL), min(kk, TK_DW)
    nk = kk // tk
    i0 = first // tm

    def body(a_ref, b_ref, o_ref, *acc):
        prod = _dot(a_ref[...], b_ref[...])
        if nk == 1:
            o_ref[...] = prod.astype(BF16)
            return
        (acc_ref,) = acc
        kid = pl.program_id(2)

        @pl.when(kid == 0)
        def _():
            acc_ref[...] = prod

        @pl.when(kid > 0)
        def _():
            acc_ref[...] += prod

        @pl.when(kid == nk - 1)
        def _():
            o_ref[...] = acc_ref[...].astype(BF16)

    (out,), plan_outs = _call(
        body,
        name=name,
        operands=[a, b],
        grid=(m // tm, n // tn, nk),
        in_specs=[pl.BlockSpec((tm, tk), lambda i, j, k: (i + i0, k)), pl.BlockSpec((tk, tn), lambda i, j, k: (k, j))],
        out_specs=[pl.BlockSpec((tm, tn), lambda i, j, k: (i, j))],
        out_shape=[jax.ShapeDtypeStruct((m, n), BF16)],
        scratch_shapes=[pltpu.VMEM((tm, tn), F32)] if nk > 1 else [],
        plans=plans,
    )
    return out, plan_outs


def _dw_attn(actt, cot, plans=()):
    s = actt.shape[1]
    tm = TM_DW
    counts = [MIX_DIM // tm, D_MODEL // tm, D_MODEL // tm]
    starts = [0, counts[0], counts[0] + counts[1]]
    which = lambda i: jnp.where(i < starts[1], 0, jnp.where(i < starts[2], 1, 2))

    def body(a_ref, b_ref, *outs):
        i = pl.program_id(0)
        prod = _dot(a_ref[...], b_ref[0]).astype(BF16)
        for m in range(3):
            @pl.when(which(i) == m)
            def _():
                outs[m][...] = prod

    out_specs = [pl.BlockSpec((tm, D_MODEL), lambda i, m=m: (jnp.clip(i - starts[m], 0, counts[m] - 1), 0)) for m in range(3)]
    out_shape = [jax.ShapeDtypeStruct((counts[m] * tm, D_MODEL), BF16) for m in range(3)]
    outs, plan_outs = _call(
        body,
        name="dw_attn",
        operands=[actt, cot],
        grid=(sum(counts),),
        in_specs=[pl.BlockSpec((tm, s), lambda i: (i, 0)), pl.BlockSpec((1, s, D_MODEL), lambda i: (which(i), 0, 0))],
        out_specs=out_specs,
        out_shape=out_shape,
        plans=plans,
    )
    return outs, plan_outs


def _dh1(name, dproj, w_in, x, dx1, g, tiles, into=None, after=(), plans=()):
    s = x.shape[0]
    tm = min(TM_DH1, s)
    t0, nt = tiles[0], tiles[1] - tiles[0]

    def body(dp_ref, w_ref, x_ref, dx1_ref, g_ref, *rest):
        gx_ref, dg_ref, acc_g = rest[-3:]
        i = pl.program_id(0)

        @pl.when(i == 0)
        def _():
            acc_g[...] = jnp.zeros_like(acc_g)

        dh1 = _dot_nt(dp_ref[...], w_ref[...])
        xh, r = _rms(x_ref[...])
        acc_g[...] += _rows8(dh1 * xh)
        gx_ref[...] = dx1_ref[0].astype(F32) + _rms_bwd(dh1 * g_ref[...], xh, r)

        @pl.when(i == nt - 1)
        def _():
            dg_ref[...] = jnp.sum(acc_g[...], axis=0, keepdims=True)

    rows = pl.BlockSpec((tm, D_MODEL), lambda i: (i + t0, 0))
    return _call(
        body,
        name=name,
        operands=[dproj, w_in, x, dx1, g] + ([into] if into is not None else []) + list(after),
        grid=(nt,),
        in_specs=[
            pl.BlockSpec((tm, IN_DIM), lambda i: (i + t0, 0)),
            pl.BlockSpec((D_MODEL, IN_DIM), lambda i: (0, 0), pipeline_mode=pl.Buffered(1)),
            rows,
            pl.BlockSpec((1, tm, D_MODEL), lambda i: (0, i + t0, 0)),
            pl.BlockSpec((1, D_MODEL), lambda i: (0, 0)),
        ] + [pl.BlockSpec(memory_space=pl.ANY)] * ((into is not None) + len(after)),
        out_specs=[rows, pl.BlockSpec((1, D_MODEL), lambda i: (0, 0))],
        out_shape=[jax.ShapeDtypeStruct((s, D_MODEL), F32), jax.ShapeDtypeStruct((1, D_MODEL), F32)],
        scratch_shapes=[pltpu.VMEM((8, D_MODEL), F32)],
        plans=plans,
        aliases={5: 0} if into is not None else None,
    )


def _pair_sum(name, grad, got, axis, size, core, plans=()):
    shard = list(grad.shape)
    shard[axis] = size
    r, cdim = shard
    tr = min(r, TR_PAIR_SUM)
    nr = r // tr

    def body(core_ref, g_ref, got_ref, o_ref):
        o_ref[0] = (g_ref[...].astype(F32) + got_ref[0].astype(F32)).astype(BF16)

    if axis == 1:
        g_map = lambda q, t, core_ref: (t, 2 * q + core_ref[0])
    else:
        g_map = lambda q, t, core_ref: ((2 * q + core_ref[0]) * nr + t, 0)
    (out,), plan_outs = _call(
        body,
        name=name,
        prefetch=[core],
        operands=[grad, got],
        grid=(N_CHIP, nr),
        in_specs=[
            pl.BlockSpec((tr, cdim), g_map),
            pl.BlockSpec((1, tr, cdim), lambda q, t, core_ref: (q, t, 0)),
        ],
        out_specs=[pl.BlockSpec((1, tr, cdim), lambda q, t, core_ref: (q, t, 0))],
        out_shape=[jax.ShapeDtypeStruct((N_CHIP, r, cdim), BF16)],
        plans=plans,
    )
    return (out, plan_outs) if plans else out


def _adam_step(w, m, v, g):
    c1 = 1.0 / (1.0 - ADAM_B1**ADAM_STEP)
    c2 = 1.0 / (1.0 - ADAM_B2**ADAM_STEP)
    m_new = ADAM_B1 * m + (1.0 - ADAM_B1) * g
    v_new = ADAM_B2 * v + (1.0 - ADAM_B2) * (g * g)
    return -ADAM_LR * ((m_new * c1) / (jnp.sqrt(v_new * c2) + ADAM_EPS) + ADAM_WD * w), m_new, v_new


def _update_small(dev, vec_w, vec_m, vec_v, conv_w, conv_m, conv_v, small, landed):
    n_vec = len(vec_w)

    def body(dev_ref, *refs):
        refs = list(refs)
        take = lambda k: [refs.pop(0) for _ in range(k)]
        w_refs, m_refs, v_refs = take(n_vec), take(n_vec), take(n_vec)
        cw_ref, cm_ref, cv_ref, small_ref, land_ref, small_conv_ref, land_conv_ref = take(7)
        vec_outs, conv_outs, (loss_ref,) = take(4 * n_vec), take(4), take(1)

        def total(own_ref, others_ref):
            acc = None
            for d in range(N_DEV):
                part = jnp.where(dev_ref[0] == d, own_ref[0], others_ref[d])
                acc = part if acc is None else acc + part
            return acc

        g_all = total(small_ref, land_ref)
        for i in range(n_vec):
            g = g_all[i : i + 1]
            delta, m_new, v_new = _adam_step(w_refs[i][...], m_refs[i][...], v_refs[i][...], g)
            for ref, val in zip(vec_outs[4 * i : 4 * i + 4], (g, delta, m_new, v_new)):
                ref[...] = val
        g_conv = total(small_conv_ref, land_conv_ref)
        for tap in range(3):
            g = g_conv[tap : tap + 1]
            delta, m_new, v_new = _adam_step(cw_ref[tap], cm_ref[tap], cv_ref[tap], g)
            for ref, val in zip(conv_outs, (g, delta, m_new, v_new)):
                ref[tap] = val
        loss_ref[...] = jnp.broadcast_to(0.5 * jnp.sum(g_all[7:8], axis=-1, keepdims=True) / D_MODEL, loss_ref.shape)

    vec_spec = pl.BlockSpec((1, D_MODEL), lambda i, dev_ref: (0, 0))
    conv_spec = pl.BlockSpec((3, 1, CHUNK), lambda i, dev_ref: (0, 0, 0))
    vec_shape = jax.ShapeDtypeStruct((1, D_MODEL), F32)
    conv_shape = jax.ShapeDtypeStruct((3, 1, CHUNK), F32)
    outs, _ = _call(
        body,
        name="update_small",
        prefetch=[dev],
        operands=list(vec_w) + list(vec_m) + list(vec_v) + [conv_w, conv_m, conv_v, small, landed, small, landed],
        grid=(1,),
        in_specs=[vec_spec] * (3 * n_vec) + [conv_spec] * 3 + [
            pl.BlockSpec((1, 16, D_MODEL), lambda i, dev_ref: (0, 0, 0)),
            pl.BlockSpec((N_DEV, 16, D_MODEL), lambda i, dev_ref: (0, 0, 0)),
            pl.BlockSpec((1, 8, CHUNK), lambda i, dev_ref: (0, 1, dev_ref[0])),
            pl.BlockSpec((N_DEV, 8, CHUNK), lambda i, dev_ref: (0, 1, dev_ref[0])),
        ],
        out_specs=[vec_spec] * (4 * n_vec) + [conv_spec] * 4 + [pl.BlockSpec((1, CHUNK), lambda i, dev_ref: (0, 0))],
        out_shape=[vec_shape] * (4 * n_vec) + [conv_shape] * 4 + [jax.ShapeDtypeStruct((1, CHUNK), F32)],
    )
    return [outs[4 * i : 4 * i + 4] for i in range(n_vec)], outs[4 * n_vec : 4 * n_vec + 4], outs[-1]


def _adamw(name, groups, steps, own_slot=None, plans=()):
    lists = lambda p: list(p) if isinstance(p, (list, tuple)) else [p]
    groups = [(g[0], g[1], g[2], lists(g[3]), lists(g[4]) if len(g) > 4 else []) for g in groups]
    counts = [3 + len(g[3]) + len(g[4]) for g in groups]
    use_slot = own_slot is not None

    def first_tiles(w, parts):
        firsts, first = [], 0
        for p in parts:
            firsts.append(first)
            first += p.shape[1] // (w.shape[0] // steps)
        return firsts

    def update(t, slot_ref, firsts, w_ref, m_ref, v_ref, p_refs, o_refs, g_ref, d_ref, nm_ref, nv_ref):
        g = None
        for k, p_ref in enumerate(p_refs):
            gk = None
            for q in range(p_ref.shape[0]):
                part = p_ref[q].astype(F32)
                if o_refs:
                    part = jnp.where(slot_ref[0] == q, o_refs[k][0].astype(F32), part)
                gk = part if gk is None else gk + part
            g = gk if g is None else jnp.where(t >= firsts[k], gk, g)
        g_ref[...] = g
        d_ref[...], nm_ref[...], nv_ref[...] = _adam_step(w_ref[...], m_ref[...], v_ref[...], g)

    def body(*refs):
        refs = list(refs)
        slot_ref = refs.pop(0) if use_slot else None
        t = pl.program_id(0)
        ins, outs = refs[: sum(counts)], refs[sum(counts) :]
        for i, (w, _, _, parts, own) in enumerate(groups):
            mine = ins[sum(counts[:i]) : sum(counts[: i + 1])]
            update(t, slot_ref, first_tiles(w, parts), *mine[:3], mine[3 : 3 + len(parts)], mine[3 + len(parts) :],
                   *outs[4 * i : 4 * i + 4])

    operands, in_specs, out_specs, out_shape = [], [], [], []
    for w, m, v, parts, own in groups:
        r, cdim = w.shape
        tr = r // steps
        firsts = first_tiles(w, parts)

        def tile_of(k, t, firsts=firsts, parts=parts, tr=tr):
            return jnp.clip(t - firsts[k], 0, parts[k].shape[1] // tr - 1)

        blk = pl.BlockSpec((tr, cdim), lambda t, *_: (t, 0))
        operands += [w, m, v] + parts + own
        in_specs += [blk, blk, blk]
        in_specs += [pl.BlockSpec((p.shape[0], tr, cdim), lambda t, *_, k=k, tile_of=tile_of: (0, tile_of(k, t), 0)) for k, p in enumerate(parts)]
        in_specs += [pl.BlockSpec((1, tr, cdim), lambda t, slot_ref, k=k, tile_of=tile_of: (slot_ref[0], tile_of(k, t), 0)) for k in range(len(own))]
        out_specs += [blk] * 4
        out_shape += [jax.ShapeDtypeStruct((r, cdim), F32)] * 4
    outs, plan_outs = _call(
        body,
        name=name,
        prefetch=[own_slot] if use_slot else [],
        operands=operands,
        grid=(steps,),
        in_specs=in_specs,
        out_specs=out_specs,
        out_shape=out_shape,
        plans=plans,
    )
    return [outs[4 * i : 4 * i + 4] for i in range(len(groups))], plan_outs


def _chip_routes(n):
    x, y, c = _place()
    my_chip = 2 * x + y
    return [(a, (*chip, c), 2 * chip[0] + chip[1], my_chip, 2 * chip[0] + chip[1])
            for a in range(n) for chip in _other_chips(x, y)]


def _gather_routes(n):
    x, y, c = _place()
    flip = lambda v, f: 1 - v if f else v
    peers = [(flip(x, fx), flip(y, fy), flip(c, fc)) for fx in (0, 1) for fy in (0, 1) for fc in (0, 1) if fx + fy + fc]
    return [(0, p, 0, 4 * x + 2 * y + c, 4 * p[0] + 2 * p[1] + p[2]) for p in peers]


_chip_routes.copies_per_array = N_CHIP - 1
_gather_routes.copies_per_array = N_DEV - 1


def _split_start(name, routes, srcs, land_shapes):
    n = len(srcs)
    hbm = pl.BlockSpec(memory_space=pltpu.HBM)
    sem = pl.BlockSpec(memory_space=pltpu.SEMAPHORE)

    def body(*refs):
        src_refs, land_refs = refs[:n], refs[n : 2 * n]
        send_sems, recv_sems = refs[2 * n], refs[2 * n + 1]
        token = refs[-1]
        for k, (a, peer, src_slot, there, _) in enumerate(routes(n)):
            pltpu.make_async_remote_copy(
                src_ref=src_refs[a].at[src_slot], dst_ref=land_refs[a].at[there],
                send_sem=send_sems.at[k], recv_sem=recv_sems.at[k], device_id=peer, device_id_type=MESH).start()
        token[...] = jnp.zeros_like(token)

    in_hbm = [pltpu.with_memory_space_constraint(p, pltpu.HBM) for p in srcs]
    in_hbm += [pltpu.with_memory_space_constraint(lax.empty(shape, p.dtype), pltpu.HBM) for p, shape in zip(srcs, land_shapes)]
    n_copies = routes.copies_per_array * n
    res = pl.pallas_call(
        body,
        name=name,
        out_shape=[pltpu.SemaphoreType.DMA((n_copies,)), pltpu.SemaphoreType.DMA((n_copies,))]
        + [pltpu.HBM(p.shape, p.dtype) for p in srcs] + [pltpu.HBM(shape, p.dtype) for p, shape in zip(srcs, land_shapes)]
        + [jax.ShapeDtypeStruct((8, CHUNK), F32)],
        in_specs=[hbm] * (2 * n),
        out_specs=[sem, sem] + [hbm] * (2 * n) + [pl.BlockSpec(memory_space=pltpu.VMEM)],
        input_output_aliases={i: 2 + i for i in range(2 * n)},
        compiler_params=pltpu.CompilerParams(has_side_effects=pltpu.SideEffectType.DATAFLOW_SIDE_EFFECTING),
    )(*in_hbm)
    return res[:-1], res[-1]


def _split_wait(name, routes, in_flight, after):
    n = (len(in_flight) - 2) // 2
    hbm = pl.BlockSpec(memory_space=pltpu.HBM)
    sem = pl.BlockSpec(memory_space=pltpu.SEMAPHORE)
    send_sems, recv_sems, *bufs = in_flight

    def body(*refs):
        src_refs, land_refs = refs[:n], refs[n : 2 * n]
        send_ref, recv_ref = refs[2 * n], refs[2 * n + 1]
        for k, (a, peer, src_slot, _, here) in enumerate(routes(n)):
            cp = pltpu.make_async_remote_copy(
                src_ref=src_refs[a].at[src_slot], dst_ref=land_refs[a].at[here],
                send_sem=send_ref.at[k], recv_sem=recv_ref.at[k], device_id=peer, device_id_type=MESH)
            cp.wait_send()
            cp.wait_recv()

    res = pl.pallas_call(
        body,
        name=name,
        out_shape=[pltpu.HBM(b.shape, b.dtype) for b in bufs],
        in_specs=[hbm] * (2 * n) + [sem, sem] + [pl.BlockSpec(memory_space=pl.ANY)] * len(after),
        out_specs=[hbm] * (2 * n),
        input_output_aliases={i: i for i in range(2 * n)},
        compiler_params=pltpu.CompilerParams(has_side_effects=pltpu.SideEffectType.DATAFLOW_SIDE_EFFECTING),
    )(*bufs, send_sems, recv_sems, *after)
    return res[:n], res[n:]


def kernel(x, mem, norm_mix_g, w_in, conv_w, gm_ln_g, gm_ln_b, gm_ws, gm_bs, w_out, norm_x_g, norm_mem_g, w_q, w_kv, w_xo, norm_final_g, loss_target, m_norm_mix_g, m_w_in, m_conv_w, m_gm_ln_g, m_gm_ln_b, m_gm_ws, m_gm_bs, m_w_out, m_norm_x_g, m_norm_mem_g, m_w_q, m_w_kv, m_w_xo, m_norm_final_g, v_norm_mix_g, v_w_in, v_conv_w, v_gm_ln_g, v_gm_ln_b, v_gm_ws, v_gm_bs, v_w_out, v_norm_x_g, v_norm_mem_g, v_w_q, v_w_kv, v_w_xo, v_norm_final_g):
    s = x.shape[1]
    dev = 4 * lax.axis_index("x") + 2 * lax.axis_index("y") + lax.axis_index("c")
    core = lax.axis_index("c").astype(jnp.int32).reshape(1)
    x2 = x[0]
    target = loss_target[0]
    pad_taps = lambda t: jnp.pad(t[0], ((0, 5), (0, 0)))

    big_names = ["w_in", "w_out", "w_q", "w_kv", "w_xo"]
    big_w = dict(w_in=w_in[0], w_out=w_out[0], w_q=w_q[0], w_kv=w_kv[0], w_xo=w_xo[0])
    big_m = dict(w_in=m_w_in[0], w_out=m_w_out[0], w_q=m_w_q[0], w_kv=m_w_kv[0], w_xo=m_w_xo[0])
    big_v = dict(w_in=v_w_in[0], w_out=v_w_out[0], w_q=v_w_q[0], w_kv=v_w_kv[0], w_xo=v_w_xo[0])
    shard_axis = dict(w_in=1, w_out=0, w_q=0, w_kv=1, w_xo=0)
    full_shape = dict(w_in=(D_MODEL, IN_DIM), w_out=(MIX_DIM, D_MODEL), w_q=(D_MODEL, D_MODEL),
                      w_kv=(D_MODEL, 2 * D_MODEL), w_xo=(D_MODEL, D_MODEL))
    shard_size = {k: big_w[k].shape[shard_axis[k]] for k in big_names}
    others = big_names[1:]
    gather_spec = lambda k: (full_shape[k], shard_axis[k], shard_size[k])
    reduce_spec = lambda k: (shard_axis[k], shard_size[k])
    ws = gm_ws[0]
    bs_b = jnp.broadcast_to(gm_bs[0][:, :, None], (8, CHUNK, CHUNK))
    g_f = norm_final_g.reshape(1, D_MODEL)

    px, py = lax.axis_index("x"), lax.axis_index("y")
    chip_order = jnp.stack([2 * px + py, 2 * (1 - px) + py, 2 * px + 1 - py, 2 * (1 - px) + 1 - py]).astype(jnp.int32)
    (proj, h1, h1t, w_in_full), ((w_out_full, conv_g),) = _inproj(
        x2, norm_mix_g, big_w["w_in"].astype(BF16), chip_order,
        plans=[_gather_plan([big_w["w_out"].astype(BF16), pad_taps(conv_w)[None]], [gather_spec("w_out"), ((N_DEV, 8, CHUNK), 0, 1)])])
    conv_full = conv_g.transpose(1, 0, 2).reshape(8, D_MODEL)
    attn_w = ["w_q", "w_kv", "w_xo"]
    cat, (gathered,) = _mixer_fwd(
        proj, conv_full, gm_ln_g, gm_ln_b, ws, bs_b,
        plans=[_gather_plan([big_w[k].astype(BF16) for k in attn_w], [gather_spec(k) for k in attn_w])])
    wf = dict(zip(attn_w, gathered), w_out=w_out_full)
    k, vv, mt = _kv_fwd(mem[0], norm_mem_g, wf["w_kv"])
    (dcat, actt, cot, dk, dvv, d_gf, d_gx, sq_err) = _attn(
        x2, cat, target, wf["w_out"], wf["w_q"], wf["w_xo"], k, vv, norm_x_g, g_f)

    def pair_sums(names, grads, got):
        return [_pair_sum("pair_sum_" + k, grads[k], got[i], *reduce_spec(k), core) for i, k in enumerate(names)]

    sibling_plan = lambda names, grads: _sibling_plan([grads[k] for k in names], [reduce_spec(k) for k in names])
    early = ["w_kv", "w_out", "w_q", "w_xo"]
    big_g = {}
    big_g["w_kv"], d_gmem = _kv_bwd(mem[0], wf["w_kv"], mt, dk, dvv)
    dws, (got_kv,) = _dw_attn(actt, cot, plans=[sibling_plan(early[:1], big_g)])
    big_g.update(zip(early[1:], dws))
    pair_kv, (got_rest,) = _pair_sum("pair_sum_w_kv", big_g["w_kv"], got_kv[0], *reduce_spec("w_kv"), core,
                                     plans=[sibling_plan(early[1:], big_g)])
    (dproj, d_cw, d_lng, d_lnb, d_ws, d_bs), (parts_early,) = _mixer_bwd(
        proj, dcat, conv_full, gm_ln_g, gm_ln_b, ws, bs_b,
        plans=[_chip_plan([pair_kv] + pair_sums(early[1:], big_g, got_rest))])
    half = D_MODEL // 2
    in_spec = reduce_spec("w_in")
    dw_top, _ = _matmul_bf16("dw_in_top", h1t, dproj, rows=(0, half))
    dw_bot, (got_top, (ws_all,)) = _matmul_bf16(
        "dw_in_bot", h1t, dproj, rows=(half, half),
        plans=[_sibling_plan([dw_top], [in_spec]),
               _gather_plan([d_ws.reshape(1, 8 * CHUNK, CHUNK).astype(BF16)], [((N_DEV, 8 * CHUNK, CHUNK), 0, 1)])])
    pair_top, ((got_bot,),) = _pair_sum("pair_sum_w_in_top", dw_top, got_top[0], *in_spec, core,
                                        plans=[_sibling_plan([dw_bot], [in_spec])])
    pair_bot = _pair_sum("pair_sum_w_in_bot", dw_bot, got_bot, *in_spec, core)
    chip_shapes = [p.shape for p in (pair_top, pair_bot)]
    chips_in_flight, token = _split_start("grad_chip_exchange_start", _chip_routes, [pair_top, pair_bot], chip_shapes)
    n_tiles = s // min(TM_DH1, s)
    (grad_x, d_gmix), _ = _dh1("dh1", dproj, w_in_full, x2, cot, norm_mix_g, (0, n_tiles), after=[token])
    vec_names = ["norm_mix_g", "gm_ln_g", "gm_ln_b", "gm_bs", "norm_x_g", "norm_mem_g", "norm_final_g"]
    vec_g = [d_gmix, d_lng, d_lnb, d_bs, d_gx, d_gmem, d_gf]
    small = jnp.concatenate(vec_g + [sq_err, d_cw], axis=0)[None]
    small_in_flight, small_token = _split_start("small_all_gather_start", _gather_routes, [small], [(N_DEV, 16, D_MODEL)])
    flat_ws = lambda t: t.reshape(8 * CHUNK, CHUNK)
    updated, _ = _adamw(
        "adamw_early",
        [(big_w[k], big_m[k], big_v[k], got_k) for k, got_k in zip(early, parts_early)]
        + [(flat_ws(ws), flat_ws(m_gm_ws), flat_ws(v_gm_ws), ws_all)], steps=STEPS_ADAMW // 2)
    big_out = dict(zip(early, updated[:4]))
    ws_out = updated[4]
    (pair_top, pair_bot), (land_top, land_bot) = _split_wait(
        "grad_chip_exchange_wait", _chip_routes, chips_in_flight, [small_token] + [u[0] for u in updated])
    my_chip = (2 * px + py).astype(jnp.int32).reshape(1)
    (big_out["w_in"],), _ = _adamw(
        "adamw_w_in", [(big_w["w_in"], big_m["w_in"], big_v["w_in"], [land_top, land_bot], [pair_top, pair_bot])],
        steps=STEPS_ADAMW, own_slot=my_chip)
    (small,), (small_all,) = _split_wait("small_all_gather_wait", _gather_routes, small_in_flight, [big_out["w_in"][0]])
    row = lambda t: t.reshape(1, D_MODEL)
    vec_out, conv_out, loss_row = _update_small(
        dev.astype(jnp.int32).reshape(1),
        [norm_mix_g, gm_ln_g, gm_ln_b, row(gm_bs), norm_x_g, norm_mem_g, row(norm_final_g)],
        [m_norm_mix_g, m_gm_ln_g, m_gm_ln_b, row(m_gm_bs), m_norm_x_g, m_norm_mem_g, row(m_norm_final_g)],
        [v_norm_mix_g, v_gm_ln_g, v_gm_ln_b, row(v_gm_bs), v_norm_x_g, v_norm_mem_g, row(v_norm_final_g)],
        *(t.transpose(1, 0, 2) for t in (conv_w, m_conv_w, v_conv_w)), small, small_all)
    loss = loss_row[0, 0]

    def result(name, which):
        if name in big_out:
            return big_out[name][which][None]
        if name == "conv_w":
            return conv_out[which].transpose(1, 0, 2)
        if name == "gm_ws":
            return ws_out[which].reshape(1, 8, CHUNK, CHUNK)
        r = vec_out[vec_names.index(name)][which]
        if name == "gm_bs":
            return r.reshape(1, 8, CHUNK)
        if name == "norm_final_g":
            return r.reshape(D_MODEL)
        return r

    weights = ["norm_mix_g", "w_in", "conv_w", "gm_ln_g", "gm_ln_b", "gm_ws", "gm_bs", "w_out", "norm_x_g", "norm_mem_g", "w_q", "w_kv", "w_xo", "norm_final_g"]
    outs = [loss, grad_x[None]]
    for which in range(4):
        outs += [result(name, which) for name in weights]
    return tuple(outs)
```

```python
import math

import jax
import jax.numpy as jnp
from jax import lax
from jax.experimental import pallas as pl
from jax.experimental.pallas import tpu as pltpu

F32 = jnp.float32
BF16 = jnp.bfloat16
MESH = pl.DeviceIdType.MESH

D_MODEL = 1024
N_DEV = 8
N_CHIP = 4
CHUNK = 128
GROUPS = 7
IN_DIM = GROUPS * D_MODEL
SHARD_IN = IN_DIM // N_DEV
MIX_DIM = 2 * D_MODEL
X_HEADS = 4
X_HEAD_DIM = D_MODEL // X_HEADS
MEM_LEN = 256
EPS = 1e-6
GELU_K0 = math.sqrt(2.0 / math.pi)
GELU_K1 = 0.044715

ADAM_LR = 0.001
ADAM_B1 = 0.9
ADAM_B2 = 0.999
ADAM_EPS = 1e-08
ADAM_WD = 0.01
ADAM_STEP = 10

TM_INPROJ = 512
TM_ATTN = 512
SUB_ATTN = 256
TM_DH1 = 512
TM_DW = 512
TK_DW = 4096
BARRIER_ID = {"sibling": 1, "chips": 2, "all": 3}
STEPS_ADAMW = 4


def _dot(a, b):
    return lax.dot_general(a, b, (((1,), (0,)), ((), ())), preferred_element_type=F32)


def _dot_nt(a, b):
    return lax.dot_general(a, b, (((1,), (1,)), ((), ())), preferred_element_type=F32)


def _rows8(v):
    return jnp.sum(v.reshape(v.shape[0] // 8, 8, v.shape[1]), axis=0)


def _gelu(x, with_grad=True):
    x2 = x * x
    t = jnp.tanh(x * (GELU_K0 + (GELU_K0 * GELU_K1) * x2))
    half = 0.5 + 0.5 * t
    g = x * half
    if not with_grad:
        return g
    return g, half + ((0.5 * x) * (1.0 - t * t)) * (GELU_K0 + (3.0 * GELU_K0 * GELU_K1) * x2)


def _silu(z, with_grad=True):
    s = jax.nn.sigmoid(z)
    zs = z * s
    return (zs, s + zs * (1.0 - s)) if with_grad else zs


def _rms(v):
    r = lax.rsqrt(jnp.mean(v * v, axis=-1, keepdims=True) + EPS)
    return v * r, r


def _rms_bwd(dy_g, vh, r):
    return r * (dy_g - vh * jnp.mean(dy_g * vh, axis=-1, keepdims=True))


def _place():
    return lax.axis_index("x"), lax.axis_index("y"), lax.axis_index("c")


def _other_chips(x, y):
    return [(1 - x, y), (x, 1 - y), (1 - x, 1 - y)]


def _pow2_divisor(n):
    return n & (-n)


def _shard_view(ref, axis, size, d):
    start = d * size
    align = _pow2_divisor(size)
    if align > 1:
        start = pl.multiple_of(start, align)
    idx = [slice(None)] * len(ref.shape)
    idx[axis] = pl.ds(start, size)
    return ref.at[tuple(idx)]


class _Plan:
    def __init__(self, ins, out_shape, sems, start, finish, mid=None, aliases=None, peers="all"):
        self.ins, self.out_shape, self.sems = list(ins), list(out_shape), list(sems)
        self.start, self.mid, self.finish = start, mid, finish
        self.aliases = dict(aliases or {})
        self.peers = peers


def _gather_plan(shards, specs):
    n = len(shards)
    n_copy = 7

    def copies(ins, outs, sems):
        send_sems, recv_sems, local_sems = sems
        x, y, c = _place()
        me, sibling = (x, y, c), (x, y, 1 - c)
        chips = _other_chips(x, y)

        def block(a, p):
            _, axis, size = specs[a]
            return _shard_view(outs[a], axis, size, 4 * p[0] + 2 * p[1] + p[2])

        def copy(a, k, p, to, src=None):
            return pltpu.make_async_remote_copy(
                src_ref=block(a, p) if src is None else src,
                dst_ref=block(a, p),
                send_sem=send_sems.at[a * n_copy + k],
                recv_sem=recv_sems.at[a * n_copy + k],
                device_id=to,
                device_id_type=MESH,
            )

        def mine():
            return [pltpu.make_async_copy(ins[a], block(a, me), local_sems.at[a]) for a in range(n)]

        def first():
            return [cp for a in range(n) for cp in
                    [copy(a, 0, me, sibling, src=ins[a])] + [copy(a, 1 + j, me, (*chip, c), src=ins[a]) for j, chip in enumerate(chips)]]

        def landed():
            return [copy(a, 1 + j, (*chip, c), me) for j, chip in enumerate(chips) for a in range(n)]

        def passed():
            return [copy(a, 4 + j, (*chip, c), sibling) for j, chip in enumerate(chips) for a in range(n)]

        def last():
            return [cp for a in range(n) for cp in
                    [copy(a, 0, sibling, me)] + [copy(a, 4 + j, (*chip, 1 - c), me) for j, chip in enumerate(chips)]]

        return mine, first, landed, passed, last

    def start(ins, outs, sems):
        mine, first, _, _, _ = copies(ins, outs, sems)
        for cp in mine() + first():
            cp.start()

    def mid(ins, outs, sems):
        _, _, landed, passed, _ = copies(ins, outs, sems)
        for got, fwd in zip(landed(), passed()):
            got.wait_recv()
            fwd.start()

    def finish(ins, outs, sems):
        mine, first, _, passed, last = copies(ins, outs, sems)
        for cp in last():
            cp.wait_recv()
        for cp in first() + passed():
            cp.wait_send()
        for cp in mine():
            cp.wait()

    return _Plan(
        shards,
        [jax.ShapeDtypeStruct(full, s.dtype) for s, (full, _, _) in zip(shards, specs)],
        [pltpu.SemaphoreType.DMA((n * n_copy,)), pltpu.SemaphoreType.DMA((n * n_copy,)), pltpu.SemaphoreType.DMA((n,))],
        start, finish, mid,
    )


def _sibling_plan(grads, specs):
    n = len(grads)

    def shard_shape(a):
        axis, size = specs[a]
        shp = list(grads[a].shape)
        shp[axis] = size
        return tuple(shp)

    def copies(ins, outs, sems):
        send_sems, recv_sems = sems
        x, y, c = _place()
        return [
            pltpu.make_async_remote_copy(
                src_ref=_shard_view(ins[a], specs[a][0], specs[a][1], 2 * q + (1 - c)),
                dst_ref=outs[a].at[q],
                send_sem=send_sems.at[a * N_CHIP + q],
                recv_sem=recv_sems.at[a * N_CHIP + q],
                device_id=(x, y, 1 - c),
                device_id_type=MESH,
            )
            for a in range(n) for q in range(N_CHIP)
        ]

    def start(ins, outs, sems):
        for cp in copies(ins, outs, sems):
            cp.start()

    def finish(ins, outs, sems):
        for cp in copies(ins, outs, sems):
            cp.wait()

    return _Plan(
        grads,
        [jax.ShapeDtypeStruct((N_CHIP, *shard_shape(a)), grads[a].dtype) for a in range(n)],
        [pltpu.SemaphoreType.DMA((n * N_CHIP,)), pltpu.SemaphoreType.DMA((n * N_CHIP,))],
        start, finish, peers="sibling",
    )


def _chip_plan(parts, full_rows=None, row0=None, into=None):
    n = len(parts)
    full_rows = [p.shape[1] for p in parts] if full_rows is None else full_rows
    row0 = [0] * n if row0 is None else row0

    def copies(ins, outs, sems):
        send_sems, recv_sems, local_sems = sems
        x, y, c = _place()
        my_chip = 2 * x + y
        chips = _other_chips(x, y)

        def land(a, q):
            return outs[a].at[q, pl.ds(row0[a], parts[a].shape[1]), :]

        def mine():
            return [pltpu.make_async_copy(ins[a].at[my_chip], land(a, my_chip), local_sems.at[a]) for a in range(n)]

        def remote(sending):
            return [
                pltpu.make_async_remote_copy(
                    src_ref=ins[a].at[2 * chip[0] + chip[1]],
                    dst_ref=land(a, my_chip if sending else 2 * chip[0] + chip[1]),
                    send_sem=send_sems.at[a * 3 + j], recv_sem=recv_sems.at[a * 3 + j],
                    device_id=(*chip, c), device_id_type=MESH)
                for a in range(n) for j, chip in enumerate(chips)
            ]

        return mine, remote

    def start(ins, outs, sems):
        mine, remote = copies(ins, outs, sems)
        for cp in mine() + remote(True):
            cp.start()

    def finish(ins, outs, sems):
        mine, remote = copies(ins, outs, sems)
        for cp in remote(False):
            cp.wait_recv()
        for cp in remote(True):
            cp.wait_send()
        for cp in mine():
            cp.wait()

    return _Plan(
        list(parts) + (list(into) if into is not None else []),
        [jax.ShapeDtypeStruct((N_CHIP, full_rows[a], parts[a].shape[2]), parts[a].dtype) for a in range(n)],
        [pltpu.SemaphoreType.DMA((n * 3,)), pltpu.SemaphoreType.DMA((n * 3,)), pltpu.SemaphoreType.DMA((n,))],
        start, finish,
        aliases={n + a: a for a in range(n)} if into is not None else None, peers="chips",
    )


def _call(body, *, name, operands, out_shape, grid=(), in_specs=None, out_specs=None, scratch_shapes=(),
          plans=(), aliases=None, prefetch=(), start_after_body=False, pass_on_at=0.6, body_peers=None):
    operands, out_shape, scratch_shapes = list(operands), list(out_shape), list(scratch_shapes)
    n_pf, n_in, n_out, n_scr = len(prefetch), len(operands), len(out_shape), len(scratch_shapes)
    in_vmem = pl.BlockSpec(memory_space=pltpu.VMEM)
    in_hbm = pl.BlockSpec(memory_space=pl.ANY)
    in_specs = [in_vmem] * n_in if in_specs is None else list(in_specs)
    out_specs = [in_vmem] * n_out if out_specs is None else list(out_specs)
    io_alias = {n_pf + i: o for i, o in (aliases or {}).items()}
    pos_in, pos_out = n_pf + n_in, n_out
    for p in plans:
        for i, o in p.aliases.items():
            io_alias[pos_in + i] = pos_out + o
        pos_in += len(p.ins)
        pos_out += len(p.out_shape)
    steps = math.prod(grid) if grid else 1
    mid_step = min(steps - 1, int(steps * pass_on_at))
    patterns = {p.peers for p in plans} | ({body_peers} if body_peers else set())
    pattern = None if not patterns else patterns.pop() if len(patterns) == 1 else "all"

    def shake_hands():
        x, y, c = _place()
        peers = [(x, y, 1 - c)] if pattern in ("sibling", "all") else []
        peers += [(*chip, c) for chip in _other_chips(x, y)] if pattern in ("chips", "all") else []
        barrier = pltpu.get_barrier_semaphore()
        for peer in peers:
            pl.semaphore_signal(barrier, inc=1, device_id=peer, device_id_type=MESH)
        pl.semaphore_wait(barrier, len(peers))

    def wrapped(*refs):
        refs = list(refs)
        take = lambda k: [refs.pop(0) for _ in range(k)]
        pf = take(n_pf)
        ins, p_ins = take(n_in), [take(len(p.ins)) for p in plans]
        outs, p_outs = take(n_out), [take(len(p.out_shape)) for p in plans]
        scr, p_sems = take(n_scr), [take(len(p.sems)) for p in plans]
        step = 0
        for ax, g in enumerate(grid):
            step = step * g + pl.program_id(ax)

        def hook(kind, at):
            todo = [(getattr(p, kind), a, b, c) for p, a, b, c in zip(plans, p_ins, p_outs, p_sems) if getattr(p, kind)]

            def run():
                for fn, a, b, c in todo:
                    fn(a, b, c)

            if todo and grid:
                pl.when(step == at)(run)
            elif todo:
                run()

        if pattern and grid:
            pl.when(step == 0)(shake_hands)
        elif pattern:
            shake_hands()
        if not start_after_body:
            hook("start", 0)
        if body is not None:
            body(*pf, *ins, *outs, *scr)
        if start_after_body:
            hook("start", 0)
        hook("mid", mid_step)
        hook("finish", steps - 1)

    layout = dict(
        grid=grid,
        in_specs=in_specs + [in_hbm] * sum(len(p.ins) for p in plans),
        out_specs=out_specs + [in_hbm] * sum(len(p.out_shape) for p in plans),
        scratch_shapes=scratch_shapes + [s for p in plans for s in p.sems],
    )
    if n_pf:
        layout = dict(grid_spec=pltpu.PrefetchScalarGridSpec(num_scalar_prefetch=n_pf, **layout))
    res = pl.pallas_call(
        wrapped,
        name=name,
        out_shape=out_shape + [o for p in plans for o in p.out_shape],
        input_output_aliases=io_alias,
        compiler_params=pltpu.CompilerParams(
            dimension_semantics=("arbitrary",) * len(grid) if grid else None,
            collective_id=BARRIER_ID[pattern] if pattern else None),
        **layout,
    )(*prefetch, *operands, *[a for p in plans for a in p.ins])
    res = list(res)
    outs, plan_outs = res[:n_out], []
    pos = n_out
    for p in plans:
        plan_outs.append(res[pos : pos + len(p.out_shape)])
        pos += len(p.out_shape)
    return outs, plan_outs


def _inproj(x, g, w_shard, order, plans=()):
    s = x.shape[0]
    tm = min(TM_INPROJ, s)
    nt = s // tm
    wide = 2 * SHARD_IN
    fwd_at = nt // 2

    def body(order_ref, x_ref, g_ref, w_hbm, proj_ref, h1_ref, h1t_ref, wfull_hbm,
             w_scr, h1_scr, send_sems, recv_sems, local_sems):
        k, i = pl.program_id(0), pl.program_id(1)
        px, py, c = _place()
        me, sibling = (px, py, c), (px, py, 1 - c)
        chips = _other_chips(px, py)

        def half(slot, core):
            return w_scr.at[slot, :, pl.ds(pl.multiple_of(core * SHARD_IN, 128), SHARD_IN)]

        def copy(n, slot, core, to, src=None):
            return pltpu.make_async_remote_copy(
                src_ref=half(slot, core) if src is None else src, dst_ref=half(slot, core),
                send_sem=send_sems.at[n], recv_sem=recv_sems.at[n], device_id=to, device_id_type=MESH)

        def mine():
            return pltpu.make_async_copy(w_hbm, half(0, c), local_sems.at[0])

        def first():
            return [copy(0, 0, c, sibling, src=w_hbm)] + [copy(1 + j, 1 + j, c, (*chips[j], c), src=w_hbm) for j in range(2)]

        def relay():
            to = (c * px + (1 - c) * (1 - px), c * (1 - py) + (1 - c) * py, c)
            return pltpu.make_async_remote_copy(
                src_ref=half(2 - c, c), dst_ref=half(3, c), send_sem=send_sems.at[3], recv_sem=recv_sems.at[3],
                device_id=to, device_id_type=MESH)

        def save(slot):
            cols = pl.ds(pl.multiple_of(order_ref[slot] * wide, 128), wide)
            return pltpu.make_async_copy(w_scr.at[slot], wfull_hbm.at[:, cols], local_sems.at[1 + slot])

        @pl.when((k == 0) & (i == 0))
        def _():
            mine().start()
            for cp in first():
                cp.start()
            mine().wait()
            copy(0, 0, 1 - c, me).wait_recv()

        for j in range(3):
            @pl.when((k == j + 1) & (i == 0))
            def _():
                copy(4 + j, 1 + j, 1 - c, me).wait_recv()

        @pl.when((k == 3) & (i == 0))
        def _():
            for slot in range(N_CHIP):
                save(slot).start()

        @pl.when(k == 0)
        def _():
            xh, _ = _rms(x_ref[...])
            h = xh * g_ref[...]
            h1_ref[...] = h.astype(BF16)
            h1t_ref[...] = h.T.astype(BF16)
            h1_scr[pl.ds(pl.multiple_of(i * tm, tm), tm), :] = h.astype(BF16)

        proj_ref[...] = _dot(h1_scr[pl.ds(pl.multiple_of(i * tm, tm), tm), :], w_scr[k]).astype(BF16)

        for j in range(3):
            @pl.when((k == 0) & (i == nt - 1) if j < 2 else (k == 2) & (i == fwd_at))
            def _():
                copy(1 + j, 1 + j, c, me).wait_recv()
                copy(4 + j, 1 + j, c, sibling).start()
                if j < 2:
                    pl.when(c == 1 - j)(lambda: relay().start())

        @pl.when((k == 3) & (i == nt - 1))
        def _():
            for cp in first() + [relay()] + [copy(4 + j, 1 + j, c, sibling) for j in range(3)]:
                cp.wait_send()
            for slot in range(N_CHIP):
                save(slot).wait()

    once = lambda k, i, order_ref: (jnp.where(k == 0, i, nt - 1), 0)
    return _call(
        body,
        name="inproj",
        prefetch=[order],
        operands=[x, g, w_shard],
        grid=(N_CHIP, nt),
        in_specs=[
            pl.BlockSpec((tm, D_MODEL), once),
            pl.BlockSpec((1, D_MODEL), lambda k, i, order_ref: (0, 0)),
            pl.BlockSpec(memory_space=pl.ANY),
        ],
        out_specs=[
            pl.BlockSpec((tm, wide), lambda k, i, order_ref: (i, order_ref[k])),
            pl.BlockSpec((tm, D_MODEL), once),
            pl.BlockSpec((D_MODEL, tm), lambda k, i, order_ref: (0, jnp.where(k == 0, i, nt - 1))),
            pl.BlockSpec(memory_space=pl.ANY),
        ],
        out_shape=[
            jax.ShapeDtypeStruct((s, IN_DIM), BF16),
            jax.ShapeDtypeStruct((s, D_MODEL), BF16),
            jax.ShapeDtypeStruct((D_MODEL, s), BF16),
            jax.ShapeDtypeStruct((D_MODEL, IN_DIM), BF16),
        ],
        scratch_shapes=[
            pltpu.VMEM((N_CHIP, D_MODEL, wide), BF16),
            pltpu.VMEM((s, D_MODEL), BF16),
            pltpu.SemaphoreType.DMA((7,)),
            pltpu.SemaphoreType.DMA((7,)),
            pltpu.SemaphoreType.DMA((1 + N_CHIP,)),
        ],
        plans=plans,
        start_after_body=True,
        pass_on_at=0.8,
        body_peers="all",
    )


def _rows_from_above(v, above):
    row = lax.broadcasted_iota(jnp.int32, (8, v.shape[1]), 0)
    r1, r2 = pltpu.roll(v, 1, 0), pltpu.roll(v, 2, 0)
    top1 = jnp.where(row == 0, above[7:8], r1[0:8])
    top2 = jnp.where(row == 0, above[6:7], jnp.where(row == 1, above[7:8], r2[0:8]))
    return jnp.concatenate([top1, r1[8:]], axis=0), jnp.concatenate([top2, r2[8:]], axis=0)


def _rows_from_below(v, below):
    n = v.shape[0]
    row = lax.broadcasted_iota(jnp.int32, (8, v.shape[1]), 0)
    r1, r2 = pltpu.roll(v, n - 1, 0), pltpu.roll(v, n - 2, 0)
    end1 = jnp.where(row == 7, below[0:1], r1[n - 8 :])
    end2 = jnp.where(row == 6, below[0:1], jnp.where(row == 7, below[1:2], r2[n - 8 :]))
    return jnp.concatenate([r1[: n - 8], end1], axis=0), jnp.concatenate([r2[: n - 8], end2], axis=0)


def _causal_mask():
    row = lax.broadcasted_iota(jnp.int32, (CHUNK, CHUNK), 0)
    col = lax.broadcasted_iota(jnp.int32, (CHUNK, CHUNK), 1)
    return row, row >= col


def _mixer_fwd(proj, conv_w, ln_g, ln_b, ws, bs_b, plans=()):
    s = proj.shape[0]
    nt = s // CHUNK

    def body(proj_ref, cw_ref, lng_ref, lnb_ref, ws_ref, bsb_ref, cat_ref, pcar_ref):
        @pl.when(pl.program_id(0) == 0)
        def _():
            pcar_ref[...] = jnp.zeros_like(pcar_ref)

        row, tril = _causal_mask()
        for j in range(8):
            cs = slice(CHUNK * j, CHUNK * (j + 1))

            def grp(k):
                return proj_ref[:, k * D_MODEL + CHUNK * j : k * D_MODEL + CHUNK * (j + 1)].astype(F32)

            w = cw_ref[:, cs]
            p = grp(1) * grp(2)
            p1, p2 = _rows_from_above(p, pcar_ref[:, cs])
            pcar_ref[:, cs] = p[CHUNK - 8 :]
            cv = w[0:1] * p2 + w[1:2] * p1 + w[2:3] * p
            sa = _silu(grp(3), with_grad=False)
            cat_ref[:, cs] = ((grp(0) * cv) * sa).astype(BF16)
            gu = _gelu(grp(4), with_grad=False)
            gv = _gelu(grp(5), with_grad=False)
            dv = gv - jnp.mean(gv, axis=-1, keepdims=True)
            vn = dv * lax.rsqrt(jnp.mean(dv * dv, axis=-1, keepdims=True) + EPS)
            vn = vn * lng_ref[:, cs] + lnb_ref[:, cs]
            wc = jnp.where(tril, ws_ref[j], 0.0).astype(BF16)
            sp = _dot(wc, vn.astype(BF16)) + bsb_ref[j]
            sb = _silu(grp(6), with_grad=False)
            cat_ref[:, D_MODEL + CHUNK * j : D_MODEL + CHUNK * (j + 1)] = ((gu * sp) * sb).astype(BF16)

    whole = lambda shape: pl.BlockSpec(shape, lambda i: (0,) * len(shape))
    (cat,), plan_outs = _call(
        body,
        name="mixer_fwd",
        operands=[proj, conv_w, ln_g, ln_b, ws, bs_b],
        grid=(nt,),
        in_specs=[
            pl.BlockSpec((CHUNK, IN_DIM), lambda i: (i, 0)),
            whole((8, D_MODEL)),
            whole((1, D_MODEL)),
            whole((1, D_MODEL)),
            whole((8, CHUNK, CHUNK)),
            whole((8, CHUNK, CHUNK)),
        ],
        out_specs=[pl.BlockSpec((CHUNK, MIX_DIM), lambda i: (i, 0))],
        out_shape=[jax.ShapeDtypeStruct((s, MIX_DIM), BF16)],
        scratch_shapes=[pltpu.VMEM((8, D_MODEL), F32)],
        plans=plans,
        pass_on_at=0.85,
    )
    return cat, plan_outs


def _mixer_bwd(proj, dcat, conv_w, ln_g, ln_b, ws, bs_b, plans=()):
    s = proj.shape[0]
    nt = s // CHUNK

    def body(proj_ref, halo_ref, dcat_ref, cw_ref, lng_ref, lnb_ref, ws_ref, bsb_ref,
             dproj_ref, dcw_ref, dlng_ref, dlnb_ref, dws_ref, dbs_ref,
             car_ref, acc_cw, acc_lng, acc_lnb, acc_bs, wc_scr, wct_scr):
        i = pl.program_id(0)
        tile = nt - 1 - i
        row, tril = _causal_mask()

        @pl.when(i == 0)
        def _():
            for j in range(8):
                wc = jnp.where(tril, ws_ref[j], 0.0)
                wc_scr[j] = wc.astype(BF16)
                wct_scr[j] = wc.T.astype(BF16)
            car_ref[...] = jnp.zeros_like(car_ref)
            acc_cw[...] = jnp.zeros_like(acc_cw)
            acc_lng[...] = jnp.zeros_like(acc_lng)
            acc_lnb[...] = jnp.zeros_like(acc_lnb)
            acc_bs[...] = jnp.zeros_like(acc_bs)
            dws_ref[...] = jnp.zeros_like(dws_ref)

        has_prev = jnp.where(tile > 0, 1.0, 0.0).astype(F32)
        for j in range(8):
            cs = slice(CHUNK * j, CHUNK * (j + 1))

            def col(k):
                return slice(k * D_MODEL + CHUNK * j, k * D_MODEL + CHUNK * (j + 1))

            def grp(k):
                return proj_ref[:, col(k)].astype(F32)

            w = cw_ref[:, cs]
            gb, gc, xa = grp(0), grp(1), grp(2)
            p = gc * xa
            pprev = halo_ref[:, col(1)].astype(F32) * halo_ref[:, col(2)].astype(F32) * has_prev
            p1, p2 = _rows_from_above(p, pprev[8:16])
            cv = w[0:1] * p2 + w[1:2] * p1 + w[2:3] * p
            za = grp(3)
            sa, dsa = _silu(za)
            da = dcat_ref[:, cs].astype(F32)
            da_gb = da * gb
            dproj_ref[:, col(0)] = ((da * sa) * cv).astype(BF16)
            dproj_ref[:, col(3)] = ((da_gb * cv) * dsa).astype(BF16)
            dcv = da_gb * sa
            d1, d2 = _rows_from_below(dcv, car_ref[:, cs])
            car_ref[:, cs] = dcv[0:8]
            dp = w[2:3] * dcv + w[1:2] * d1 + w[0:1] * d2
            dproj_ref[:, col(1)] = (dp * xa).astype(BF16)
            dproj_ref[:, col(2)] = (dp * gc).astype(BF16)
            acc_cw[0, :, cs] += _rows8(dcv * p2)
            acc_cw[1, :, cs] += _rows8(dcv * p1)
            acc_cw[2, :, cs] += _rows8(dcv * p)
            gu, dgu = _gelu(grp(4))
            gv, dgv = _gelu(grp(5))
            dv = gv - jnp.mean(gv, axis=-1, keepdims=True)
            rstd = lax.rsqrt(jnp.mean(dv * dv, axis=-1, keepdims=True) + EPS)
            vnh = dv * rstd
            lng = lng_ref[:, cs]
            vnb = (vnh * lng + lnb_ref[:, cs]).astype(BF16)
            sp = _dot(wc_scr[j], vnb) + bsb_ref[j]
            zb = grp(6)
            sb, dsb = _silu(zb)
            db = dcat_ref[:, D_MODEL + CHUNK * j : D_MODEL + CHUNK * (j + 1)].astype(F32)
            db_sp, db_gu = db * sp, db * gu
            dproj_ref[:, col(4)] = ((db_sp * sb) * dgu).astype(BF16)
            dproj_ref[:, col(6)] = ((db_sp * gu) * dsb).astype(BF16)
            dsp = db_gu * sb
            acc_bs[j] += dsp
            dspb = dsp.astype(BF16)
            dws_ref[j] += _dot_nt(dspb, vnb)
            dvn = _dot(wct_scr[j], dspb)
            acc_lnb[:, cs] += _rows8(dvn)
            acc_lng[:, cs] += _rows8(dvn * vnh)
            dvh = dvn * lng
            dgvv = rstd * (dvh - jnp.mean(dvh, axis=-1, keepdims=True) - vnh * jnp.mean(dvh * vnh, axis=-1, keepdims=True))
            dproj_ref[:, col(5)] = (dgvv * dgv).astype(BF16)

        @pl.when(i == nt - 1)
        def _():
            dlng_ref[...] = jnp.sum(acc_lng[...], axis=0, keepdims=True)
            dlnb_ref[...] = jnp.sum(acc_lnb[...], axis=0, keepdims=True)
            dcw_ref[...] = jnp.zeros_like(dcw_ref)
            for k in range(3):
                dcw_ref[k : k + 1, :] = jnp.sum(acc_cw[k], axis=0, keepdims=True)
            for j in range(8):
                dws_ref[j] = jnp.where(tril, dws_ref[j], 0.0)
                dbs_ref[:, CHUNK * j : CHUNK * (j + 1)] = jnp.sum(acc_bs[j].T, axis=0, keepdims=True)

    whole = lambda shape: pl.BlockSpec(shape, lambda i: (0,) * len(shape))
    halo_rows = 16
    per = CHUNK // halo_rows
    return _call(
        body,
        name="mixer_bwd",
        operands=[proj, proj, dcat, conv_w, ln_g, ln_b, ws, bs_b],
        plans=plans,
        grid=(nt,),
        in_specs=[
            pl.BlockSpec((CHUNK, IN_DIM), lambda i: (nt - 1 - i, 0)),
            pl.BlockSpec((halo_rows, IN_DIM), lambda i: (jnp.maximum((nt - 1 - i) * per - 1, 0), 0)),
            pl.BlockSpec((CHUNK, MIX_DIM), lambda i: (nt - 1 - i, 0)),
            whole((8, D_MODEL)),
            whole((1, D_MODEL)),
            whole((1, D_MODEL)),
            whole((8, CHUNK, CHUNK)),
            whole((8, CHUNK, CHUNK)),
        ],
        out_specs=[
            pl.BlockSpec((CHUNK, IN_DIM), lambda i: (nt - 1 - i, 0)),
            whole((8, D_MODEL)),
            whole((1, D_MODEL)),
            whole((1, D_MODEL)),
            whole((8, CHUNK, CHUNK)),
            whole((1, D_MODEL)),
        ],
        out_shape=[
            jax.ShapeDtypeStruct((s, IN_DIM), BF16),
            jax.ShapeDtypeStruct((8, D_MODEL), F32),
            jax.ShapeDtypeStruct((1, D_MODEL), F32),
            jax.ShapeDtypeStruct((1, D_MODEL), F32),
            jax.ShapeDtypeStruct((8, CHUNK, CHUNK), F32),
            jax.ShapeDtypeStruct((1, D_MODEL), F32),
        ],
        scratch_shapes=[
            pltpu.VMEM((8, D_MODEL), F32),
            pltpu.VMEM((3, 8, D_MODEL), F32),
            pltpu.VMEM((8, D_MODEL), F32),
            pltpu.VMEM((8, D_MODEL), F32),
            pltpu.VMEM((8, CHUNK, CHUNK), F32),
            pltpu.VMEM((8, CHUNK, CHUNK), BF16),
            pltpu.VMEM((8, CHUNK, CHUNK), BF16),
        ],
    )


def _kv_fwd(mem, g_mem, w_kv):
    def body(mem_ref, g_ref, w_ref, k_ref, vv_ref, mt_ref):
        mh, _ = _rms(mem_ref[...])
        m = mh * g_ref[...]
        kv = _dot(m.astype(BF16), w_ref[...])
        k_ref[...] = kv[:, :D_MODEL].astype(BF16)
        vv_ref[...] = kv[:, D_MODEL:].astype(BF16)
        mt_ref[...] = m.T.astype(BF16)

    return pl.pallas_call(
        body,
        name="kv_fwd",
        out_shape=[
            jax.ShapeDtypeStruct((MEM_LEN, D_MODEL), BF16),
            jax.ShapeDtypeStruct((MEM_LEN, D_MODEL), BF16),
            jax.ShapeDtypeStruct((D_MODEL, MEM_LEN), BF16),
        ],
    )(mem, g_mem, w_kv)


def _kv_bwd(mem, w_kv, mt, dk, dvv):
    def body(mem_ref, w_ref, mt_ref, dk_ref, dvv_ref, dw_ref, dg_ref):
        dkv = jnp.concatenate([dk_ref[...], dvv_ref[...]], axis=1).astype(BF16)
        dw_ref[...] = _dot(mt_ref[...], dkv).astype(BF16)
        dm = _dot_nt(dkv, w_ref[...])
        mh, _ = _rms(mem_ref[...])
        dg_ref[...] = jnp.sum(dm * mh, axis=0, keepdims=True)

    return pl.pallas_call(
        body,
        name="kv_bwd",
        out_shape=[jax.ShapeDtypeStruct((D_MODEL, 2 * D_MODEL), BF16), jax.ShapeDtypeStruct((1, D_MODEL), F32)],
    )(mem, w_kv, mt, dk, dvv)


def _attn(x, cat, target, w_out, w_q, w_xo, k, vv, g_x, g_f):
    s = x.shape[0]
    tm = min(TM_ATTN, s)
    sub = min(SUB_ATTN, tm)
    nt = s // tm
    scale = 1.0 / math.sqrt(X_HEAD_DIM)

    def body(x_ref, cat_ref, t_ref, wout_ref, wq_ref, wxo_ref, k_ref, vv_ref, gx_ref, gf_ref,
             dcat_ref, actt_ref, cot_ref,
             dk_ref, dvv_ref, dgf_ref, dgx_ref, loss_ref,
             p_scr, q_scr, o_scr, dq_scr, acc_gf, acc_gx, acc_loss):
        i = pl.program_id(0)

        @pl.when(i == 0)
        def _():
            dk_ref[...] = jnp.zeros_like(dk_ref)
            dvv_ref[...] = jnp.zeros_like(dvv_ref)
            acc_gf[...] = jnp.zeros_like(acc_gf)
            acc_gx[...] = jnp.zeros_like(acc_gx)
            acc_loss[...] = jnp.zeros_like(acc_loss)

        subs = [slice(u * sub, (u + 1) * sub) for u in range(tm // sub)]
        heads = [slice(X_HEAD_DIM * h, X_HEAD_DIM * (h + 1)) for h in range(X_HEADS)]
        gx, gf = gx_ref[...], gf_ref[...]
        x1, x1h, r2 = [], [], []
        for rs in subs:
            cat = cat_ref[rs, :]
            actt_ref[0:MIX_DIM, rs] = cat.astype(F32).T.astype(BF16)
            v = x_ref[rs, :] + _dot(cat, wout_ref[...])
            vh, r = _rms(v)
            h2 = vh * gx
            actt_ref[MIX_DIM : MIX_DIM + D_MODEL, rs] = h2.T.astype(BF16)
            q_scr[rs, :] = _dot(h2.astype(BF16), wq_ref[...]).astype(BF16)
            x1.append(v), x1h.append(vh), r2.append(r)
        for rs in subs:
            for hs in heads:
                sc = _dot_nt(q_scr[rs, hs], k_ref[:, hs]) * scale
                e = jnp.exp(sc - jnp.max(sc, axis=-1, keepdims=True))
                p = e / jnp.sum(e, axis=-1, keepdims=True)
                p_scr[rs, hs] = p
                o_scr[rs, hs] = _dot(p.astype(BF16), vv_ref[:, hs])
        dx2, sq, d_gf = [], None, None
        for u, rs in enumerate(subs):
            o = o_scr[rs, :]
            actt_ref[MIX_DIM + D_MODEL :, rs] = o.T.astype(BF16)
            x2 = x1[u] + _dot(o.astype(BF16), wxo_ref[...])
            x2h, r3 = _rms(x2)
            err = x2h * gf - t_ref[rs, :]
            dy = err * (1.0 / D_MODEL)
            sq = _rows8(err * err) if sq is None else sq + _rows8(err * err)
            d_gf = _rows8(dy * x2h) if d_gf is None else d_gf + _rows8(dy * x2h)
            dx2.append(_rms_bwd(dy * gf, x2h, r3))
        acc_loss[...] += sq
        acc_gf[...] += d_gf
        d_vv, d_k = [None] * X_HEADS, [None] * X_HEADS
        for u, rs in enumerate(subs):
            dx2b = dx2[u].astype(BF16)
            cot_ref[2, rs, :] = dx2b
            dob = _dot_nt(dx2b, wxo_ref[...]).astype(BF16)
            for h, hs in enumerate(heads):
                p = p_scr[rs, hs]
                dp = _dot_nt(dob[:, hs], vv_ref[:, hs])
                ds = p * (dp - jnp.sum(dp * p, axis=-1, keepdims=True)) * scale
                dq_scr[rs, hs] = _dot(ds.astype(BF16), k_ref[:, hs]).astype(BF16)
                dvv_h = _dot(p.T.astype(BF16), dob[:, hs])
                dk_h = _dot(ds.T.astype(BF16), q_scr[rs, hs])
                d_vv[h] = dvv_h if d_vv[h] is None else d_vv[h] + dvv_h
                d_k[h] = dk_h if d_k[h] is None else d_k[h] + dk_h
        for h, hs in enumerate(heads):
            dvv_ref[:, hs] += d_vv[h]
            dk_ref[:, hs] += d_k[h]
        d_gx = None
        for u, rs in enumerate(subs):
            dqb = dq_scr[rs, :]
            cot_ref[1, rs, :] = dqb
            dh2 = _dot_nt(dqb, wq_ref[...])
            d_gx = _rows8(dh2 * x1h[u]) if d_gx is None else d_gx + _rows8(dh2 * x1h[u])
            dx1 = dx2[u] + _rms_bwd(dh2 * gx, x1h[u], r2[u])
            dx1b = dx1.astype(BF16)
            cot_ref[0, rs, :] = dx1b
            dcat_ref[rs, :] = _dot_nt(dx1b, wout_ref[...]).astype(BF16)
        acc_gx[...] += d_gx

        @pl.when(i == nt - 1)
        def _():
            dgf_ref[...] = jnp.sum(acc_gf[...], axis=0, keepdims=True)
            dgx_ref[...] = jnp.sum(acc_gx[...], axis=0, keepdims=True)
            loss_ref[...] = jnp.sum(acc_loss[...], axis=0, keepdims=True)

    whole = lambda shape: pl.BlockSpec(shape, lambda i: (0,) * len(shape))
    once = lambda shape: pl.BlockSpec(shape, lambda i: (0,) * len(shape), pipeline_mode=pl.Buffered(1))
    rows = lambda width: pl.BlockSpec((tm, width), lambda i: (i, 0))
    cols = lambda height: pl.BlockSpec((height, tm), lambda i: (0, i))
    vec = jax.ShapeDtypeStruct((1, D_MODEL), F32)
    return pl.pallas_call(
        body,
        name="attn",
        grid=(nt,),
        in_specs=[
            rows(D_MODEL), rows(MIX_DIM), rows(D_MODEL),
            once((MIX_DIM, D_MODEL)), once((D_MODEL, D_MODEL)), once((D_MODEL, D_MODEL)),
            once((MEM_LEN, D_MODEL)), once((MEM_LEN, D_MODEL)),
            whole((1, D_MODEL)), whole((1, D_MODEL)),
        ],
        out_specs=[
            rows(MIX_DIM), cols(2 * MIX_DIM),
            pl.BlockSpec((3, tm, D_MODEL), lambda i: (0, i, 0)),
            whole((MEM_LEN, D_MODEL)), whole((MEM_LEN, D_MODEL)),
            whole((1, D_MODEL)), whole((1, D_MODEL)), whole((1, D_MODEL)),
        ],
        out_shape=[
            jax.ShapeDtypeStruct((s, MIX_DIM), BF16),
            jax.ShapeDtypeStruct((2 * MIX_DIM, s), BF16),
            jax.ShapeDtypeStruct((3, s, D_MODEL), BF16),
            jax.ShapeDtypeStruct((MEM_LEN, D_MODEL), F32),
            jax.ShapeDtypeStruct((MEM_LEN, D_MODEL), F32),
            vec, vec, vec,
        ],
        scratch_shapes=[
            pltpu.VMEM((tm, D_MODEL), F32),
            pltpu.VMEM((tm, D_MODEL), BF16),
            pltpu.VMEM((tm, D_MODEL), F32),
            pltpu.VMEM((tm, D_MODEL), BF16),
            pltpu.VMEM((8, D_MODEL), F32),
            pltpu.VMEM((8, D_MODEL), F32),
            pltpu.VMEM((8, D_MODEL), F32),
        ],
        compiler_params=pltpu.CompilerParams(dimension_semantics=("arbitrary",)),
    )(x, cat, target, w_out, w_q, w_xo, k, vv, g_x, g_f)


def _matmul_bf16(name, a, b, rows=None, plans=()):
    first, m = (0, a.shape[0]) if rows is None else rows
    kk = a.shape[1]
    n = b.shape[1]
    tm, tn, tk = min(m, TM_DW), min(n, D_MODEL), min(kk, TK_DW)
    nk = kk // tk
    i0 = first // tm

    def body(a_ref, b_ref, o_ref, *acc):
        prod = _dot(a_ref[...], b_ref[...])
        if nk == 1:
            o_ref[...] = prod.astype(BF16)
            return
        (acc_ref,) = acc
        kid = pl.program_id(2)

        @pl.when(kid == 0)
        def _():
            acc_ref[...] = prod

        @pl.when(kid > 0)
        def _():
            acc_ref[...] += prod

        @pl.when(kid == nk - 1)
        def _():
            o_ref[...] = acc_ref[...].astype(BF16)

    (out,), plan_outs = _call(
        body,
        name=name,
        operands=[a, b],
        grid=(m // tm, n // tn, nk),
        in_specs=[pl.BlockSpec((tm, tk), lambda i, j, k: (i + i0, k)), pl.BlockSpec((tk, tn), lambda i, j, k: (k, j))],
        out_specs=[pl.BlockSpec((tm, tn), lambda i, j, k: (i, j))],
        out_shape=[jax.ShapeDtypeStruct((m, n), BF16)],
        scratch_shapes=[pltpu.VMEM((tm, tn), F32)] if nk > 1 else [],
        plans=plans,
    )
    return out, plan_outs


def _dw_attn(actt, cot, plans=()):
    s = actt.shape[1]
    tm = TM_DW
    counts = [MIX_DIM // tm, D_MODEL // tm, D_MODEL // tm]
    starts = [0, counts[0], counts[0] + counts[1]]
    which = lambda i: jnp.where(i < starts[1], 0, jnp.where(i < starts[2], 1, 2))

    def body(a_ref, b_ref, *outs):
        i = pl.program_id(0)
        prod = _dot(a_ref[...], b_ref[0]).astype(BF16)
        for m in range(3):
            @pl.when(which(i) == m)
            def _():
                outs[m][...] = prod

    out_specs = [pl.BlockSpec((tm, D_MODEL), lambda i, m=m: (jnp.clip(i - starts[m], 0, counts[m] - 1), 0)) for m in range(3)]
    out_shape = [jax.ShapeDtypeStruct((counts[m] * tm, D_MODEL), BF16) for m in range(3)]
    outs, plan_outs = _call(
        body,
        name="dw_attn",
        operands=[actt, cot],
        grid=(sum(counts),),
        in_specs=[pl.BlockSpec((tm, s), lambda i: (i, 0)), pl.BlockSpec((1, s, D_MODEL), lambda i: (which(i), 0, 0))],
        out_specs=out_specs,
        out_shape=out_shape,
        plans=plans,
    )
    return outs, plan_outs


def _dh1(name, dproj, w_in, x, dx1, g, tiles, into=None, after=(), plans=()):
    s = x.shape[0]
    tm = min(TM_DH1, s)
    t0, nt = tiles[0], tiles[1] - tiles[0]

    def body(dp_ref, w_ref, x_ref, dx1_ref, g_ref, *rest):
        gx_ref, dg_ref, acc_g = rest[-3:]
        i = pl.program_id(0)

        @pl.when(i == 0)
        def _():
            acc_g[...] = jnp.zeros_like(acc_g)

        dh1 = _dot_nt(dp_ref[...], w_ref[...])
        xh, r = _rms(x_ref[...])
        acc_g[...] += _rows8(dh1 * xh)
        gx_ref[...] = dx1_ref[0].astype(F32) + _rms_bwd(dh1 * g_ref[...], xh, r)

        @pl.when(i == nt - 1)
        def _():
            dg_ref[...] = jnp.sum(acc_g[...], axis=0, keepdims=True)

    rows = pl.BlockSpec((tm, D_MODEL), lambda i: (i + t0, 0))
    return _call(
        body,
        name=name,
        operands=[dproj, w_in, x, dx1, g] + ([into] if into is not None else []) + list(after),
        grid=(nt,),
        in_specs=[
            pl.BlockSpec((tm, IN_DIM), lambda i: (i + t0, 0)),
            pl.BlockSpec((D_MODEL, IN_DIM), lambda i: (0, 0), pipeline_mode=pl.Buffered(1)),
            rows,
            pl.BlockSpec((1, tm, D_MODEL), lambda i: (0, i + t0, 0)),
            pl.BlockSpec((1, D_MODEL), lambda i: (0, 0)),
        ] + [pl.BlockSpec(memory_space=pl.ANY)] * ((into is not None) + len(after)),
        out_specs=[rows, pl.BlockSpec((1, D_MODEL), lambda i: (0, 0))],
        out_shape=[jax.ShapeDtypeStruct((s, D_MODEL), F32), jax.ShapeDtypeStruct((1, D_MODEL), F32)],
        scratch_shapes=[pltpu.VMEM((8, D_MODEL), F32)],
        plans=plans,
        aliases={5: 0} if into is not None else None,
    )


def _pair_sum(name, grads, gots, specs, core, plans=()):
    n = len(grads)
    shards = []
    for grad, (axis, size) in zip(grads, specs):
        shard = list(grad.shape)
        shard[axis] = size
        shards.append(tuple(shard))

    def body(core_ref, *refs):
        for g_ref, got_ref, o_ref in zip(refs[:n], refs[n : 2 * n], refs[2 * n :]):
            o_ref[0] = (g_ref[...].astype(F32) + got_ref[0].astype(F32)).astype(BF16)

    def own(shard, axis):
        if axis == 1:
            return pl.BlockSpec(shard, lambda q, core_ref: (0, 2 * q + core_ref[0]))
        return pl.BlockSpec(shard, lambda q, core_ref: (2 * q + core_ref[0], 0))

    slots = [pl.BlockSpec((1, *shard), lambda q, core_ref: (q, 0, 0)) for shard in shards]
    outs, plan_outs = _call(
        body,
        name=name,
        prefetch=[core],
        operands=list(grads) + list(gots),
        grid=(N_CHIP,),
        in_specs=[own(shard, axis) for shard, (axis, _) in zip(shards, specs)] + slots,
        out_specs=slots,
        out_shape=[jax.ShapeDtypeStruct((N_CHIP, *shard), BF16) for shard in shards],
        plans=plans,
    )
    return (outs, plan_outs) if plans else outs


def _adam_step(w, m, v, g):
    c1 = 1.0 / (1.0 - ADAM_B1**ADAM_STEP)
    c2 = 1.0 / (1.0 - ADAM_B2**ADAM_STEP)
    m_new = ADAM_B1 * m + (1.0 - ADAM_B1) * g
    v_new = ADAM_B2 * v + (1.0 - ADAM_B2) * (g * g)
    return -ADAM_LR * ((m_new * c1) / (jnp.sqrt(v_new * c2) + ADAM_EPS) + ADAM_WD * w), m_new, v_new


def _update_small(dev, vec_w, vec_m, vec_v, conv_w, conv_m, conv_v, small, landed):
    n_vec = len(vec_w)

    def body(dev_ref, *refs):
        refs = list(refs)
        take = lambda k: [refs.pop(0) for _ in range(k)]
        w_refs, m_refs, v_refs = take(n_vec), take(n_vec), take(n_vec)
        cw_ref, cm_ref, cv_ref, small_ref, land_ref, small_conv_ref, land_conv_ref = take(7)
        vec_outs, conv_outs, (loss_ref,) = take(4 * n_vec), take(4), take(1)

        def total(own_ref, others_ref):
            acc = None
            for d in range(N_DEV):
                part = jnp.where(dev_ref[0] == d, own_ref[0], others_ref[d])
                acc = part if acc is None else acc + part
            return acc

        g_all = total(small_ref, land_ref)
        for i in range(n_vec):
            g = g_all[i : i + 1]
            delta, m_new, v_new = _adam_step(w_refs[i][...], m_refs[i][...], v_refs[i][...], g)
            for ref, val in zip(vec_outs[4 * i : 4 * i + 4], (g, delta, m_new, v_new)):
                ref[...] = val
        g_conv = total(small_conv_ref, land_conv_ref)
        for tap in range(3):
            g = g_conv[tap : tap + 1]
            delta, m_new, v_new = _adam_step(cw_ref[tap], cm_ref[tap], cv_ref[tap], g)
            for ref, val in zip(conv_outs, (g, delta, m_new, v_new)):
                ref[tap] = val
        loss_ref[...] = jnp.broadcast_to(0.5 * jnp.sum(g_all[7:8], axis=-1, keepdims=True) / D_MODEL, loss_ref.shape)

    vec_spec = pl.BlockSpec((1, D_MODEL), lambda i, dev_ref: (0, 0))
    conv_spec = pl.BlockSpec((3, 1, CHUNK), lambda i, dev_ref: (0, 0, 0))
    vec_shape = jax.ShapeDtypeStruct((1, D_MODEL), F32)
    conv_shape = jax.ShapeDtypeStruct((3, 1, CHUNK), F32)
    outs, _ = _call(
        body,
        name="update_small",
        prefetch=[dev],
        operands=list(vec_w) + list(vec_m) + list(vec_v) + [conv_w, conv_m, conv_v, small, landed, small, landed],
        grid=(1,),
        in_specs=[vec_spec] * (3 * n_vec) + [conv_spec] * 3 + [
            pl.BlockSpec((1, 16, D_MODEL), lambda i, dev_ref: (0, 0, 0)),
            pl.BlockSpec((N_DEV, 16, D_MODEL), lambda i, dev_ref: (0, 0, 0)),
            pl.BlockSpec((1, 8, CHUNK), lambda i, dev_ref: (0, 1, dev_ref[0])),
            pl.BlockSpec((N_DEV, 8, CHUNK), lambda i, dev_ref: (0, 1, dev_ref[0])),
        ],
        out_specs=[vec_spec] * (4 * n_vec) + [conv_spec] * 4 + [pl.BlockSpec((1, CHUNK), lambda i, dev_ref: (0, 0))],
        out_shape=[vec_shape] * (4 * n_vec) + [conv_shape] * 4 + [jax.ShapeDtypeStruct((1, CHUNK), F32)],
    )
    return [outs[4 * i : 4 * i + 4] for i in range(n_vec)], outs[4 * n_vec : 4 * n_vec + 4], outs[-1]


def _adamw(name, groups, steps, own_slot=None, plans=()):
    lists = lambda p: list(p) if isinstance(p, (list, tuple)) else [p]
    groups = [(g[0], g[1], g[2], lists(g[3]), lists(g[4]) if len(g) > 4 else []) for g in groups]
    counts = [3 + len(g[3]) + len(g[4]) for g in groups]
    use_slot = own_slot is not None

    def first_tiles(w, parts):
        firsts, first = [], 0
        for p in parts:
            firsts.append(first)
            first += p.shape[1] // (w.shape[0] // steps)
        return firsts

    def update(t, slot_ref, firsts, w_ref, m_ref, v_ref, p_refs, o_refs, g_ref, d_ref, nm_ref, nv_ref):
        g = None
        for k, p_ref in enumerate(p_refs):
            gk = None
            for q in range(p_ref.shape[0]):
                part = p_ref[q].astype(F32)
                if o_refs:
                    part = jnp.where(slot_ref[0] == q, o_refs[k][0].astype(F32), part)
                gk = part if gk is None else gk + part
            g = gk if g is None else jnp.where(t >= firsts[k], gk, g)
        g_ref[...] = g
        d_ref[...], nm_ref[...], nv_ref[...] = _adam_step(w_ref[...], m_ref[...], v_ref[...], g)

    def body(*refs):
        refs = list(refs)
        slot_ref = refs.pop(0) if use_slot else None
        t = pl.program_id(0)
        ins, outs = refs[: sum(counts)], refs[sum(counts) :]
        for i, (w, _, _, parts, own) in enumerate(groups):
            mine = ins[sum(counts[:i]) : sum(counts[: i + 1])]
            update(t, slot_ref, first_tiles(w, parts), *mine[:3], mine[3 : 3 + len(parts)], mine[3 + len(parts) :],
                   *outs[4 * i : 4 * i + 4])

    operands, in_specs, out_specs, out_shape = [], [], [], []
    for w, m, v, parts, own in groups:
        r, cdim = w.shape
        tr = r // steps
        firsts = first_tiles(w, parts)

        def tile_of(k, t, firsts=firsts, parts=parts, tr=tr):
            return jnp.clip(t - firsts[k], 0, parts[k].shape[1] // tr - 1)

        blk = pl.BlockSpec((tr, cdim), lambda t, *_: (t, 0))
        operands += [w, m, v] + parts + own
        in_specs += [blk, blk, blk]
        in_specs += [pl.BlockSpec((p.shape[0], tr, cdim), lambda t, *_, k=k, tile_of=tile_of: (0, tile_of(k, t), 0)) for k, p in enumerate(parts)]
        in_specs += [pl.BlockSpec((1, tr, cdim), lambda t, slot_ref, k=k, tile_of=tile_of: (slot_ref[0], tile_of(k, t), 0)) for k in range(len(own))]
        out_specs += [blk] * 4
        out_shape += [jax.ShapeDtypeStruct((r, cdim), F32)] * 4
    outs, plan_outs = _call(
        body,
        name=name,
        prefetch=[own_slot] if use_slot else [],
        operands=operands,
        grid=(steps,),
        in_specs=in_specs,
        out_specs=out_specs,
        out_shape=out_shape,
        plans=plans,
    )
    return [outs[4 * i : 4 * i + 4] for i in range(len(groups))], plan_outs


def _chip_routes(n):
    x, y, c = _place()
    my_chip = 2 * x + y
    return [(a, (*chip, c), 2 * chip[0] + chip[1], my_chip, 2 * chip[0] + chip[1])
            for a in range(n) for chip in _other_chips(x, y)]


def _gather_routes(n):
    x, y, c = _place()
    flip = lambda v, f: 1 - v if f else v
    peers = [(flip(x, fx), flip(y, fy), flip(c, fc)) for fx in (0, 1) for fy in (0, 1) for fc in (0, 1) if fx + fy + fc]
    return [(0, p, 0, 4 * x + 2 * y + c, 4 * p[0] + 2 * p[1] + p[2]) for p in peers]


_chip_routes.copies_per_array = N_CHIP - 1
_gather_routes.copies_per_array = N_DEV - 1


def _split_start(name, routes, srcs, land_shapes):
    n = len(srcs)
    hbm = pl.BlockSpec(memory_space=pltpu.HBM)
    sem = pl.BlockSpec(memory_space=pltpu.SEMAPHORE)

    def body(*refs):
        src_refs, land_refs = refs[:n], refs[n : 2 * n]
        send_sems, recv_sems = refs[2 * n], refs[2 * n + 1]
        token = refs[-1]
        for k, (a, peer, src_slot, there, _) in enumerate(routes(n)):
            pltpu.make_async_remote_copy(
                src_ref=src_refs[a].at[src_slot], dst_ref=land_refs[a].at[there],
                send_sem=send_sems.at[k], recv_sem=recv_sems.at[k], device_id=peer, device_id_type=MESH).start()
        token[...] = jnp.zeros_like(token)

    in_hbm = [pltpu.with_memory_space_constraint(p, pltpu.HBM) for p in srcs]
    in_hbm += [pltpu.with_memory_space_constraint(lax.empty(shape, p.dtype), pltpu.HBM) for p, shape in zip(srcs, land_shapes)]
    n_copies = routes.copies_per_array * n
    res = pl.pallas_call(
        body,
        name=name,
        out_shape=[pltpu.SemaphoreType.DMA((n_copies,)), pltpu.SemaphoreType.DMA((n_copies,))]
        + [pltpu.HBM(p.shape, p.dtype) for p in srcs] + [pltpu.HBM(shape, p.dtype) for p, shape in zip(srcs, land_shapes)]
        + [jax.ShapeDtypeStruct((8, CHUNK), F32)],
        in_specs=[hbm] * (2 * n),
        out_specs=[sem, sem] + [hbm] * (2 * n) + [pl.BlockSpec(memory_space=pltpu.VMEM)],
        input_output_aliases={i: 2 + i for i in range(2 * n)},
        compiler_params=pltpu.CompilerParams(has_side_effects=pltpu.SideEffectType.DATAFLOW_SIDE_EFFECTING),
    )(*in_hbm)
    return res[:-1], res[-1]


def _split_wait(name, routes, in_flight, after):
    n = (len(in_flight) - 2) // 2
    hbm = pl.BlockSpec(memory_space=pltpu.HBM)
    sem = pl.BlockSpec(memory_space=pltpu.SEMAPHORE)
    send_sems, recv_sems, *bufs = in_flight

    def body(*refs):
        src_refs, land_refs = refs[:n], refs[n : 2 * n]
        send_ref, recv_ref = refs[2 * n], refs[2 * n + 1]
        for k, (a, peer, src_slot, _, here) in enumerate(routes(n)):
            cp = pltpu.make_async_remote_copy(
                src_ref=src_refs[a].at[src_slot], dst_ref=land_refs[a].at[here],
                send_sem=send_ref.at[k], recv_sem=recv_ref.at[k], device_id=peer, device_id_type=MESH)
            cp.wait_send()
            cp.wait_recv()

    res = pl.pallas_call(
        body,
        name=name,
        out_shape=[pltpu.HBM(b.shape, b.dtype) for b in bufs],
        in_specs=[hbm] * (2 * n) + [sem, sem] + [pl.BlockSpec(memory_space=pl.ANY)] * len(after),
        out_specs=[hbm] * (2 * n),
        input_output_aliases={i: i for i in range(2 * n)},
        compiler_params=pltpu.CompilerParams(has_side_effects=pltpu.SideEffectType.DATAFLOW_SIDE_EFFECTING),
    )(*bufs, send_sems, recv_sems, *after)
    return res[:n], res[n:]


def kernel(x, mem, norm_mix_g, w_in, conv_w, gm_ln_g, gm_ln_b, gm_ws, gm_bs, w_out, norm_x_g, norm_mem_g, w_q, w_kv, w_xo, norm_final_g, loss_target, m_norm_mix_g, m_w_in, m_conv_w, m_gm_ln_g, m_gm_ln_b, m_gm_ws, m_gm_bs, m_w_out, m_norm_x_g, m_norm_mem_g, m_w_q, m_w_kv, m_w_xo, m_norm_final_g, v_norm_mix_g, v_w_in, v_conv_w, v_gm_ln_g, v_gm_ln_b, v_gm_ws, v_gm_bs, v_w_out, v_norm_x_g, v_norm_mem_g, v_w_q, v_w_kv, v_w_xo, v_norm_final_g):
    s = x.shape[1]
    dev = 4 * lax.axis_index("x") + 2 * lax.axis_index("y") + lax.axis_index("c")
    core = lax.axis_index("c").astype(jnp.int32).reshape(1)
    x2 = x[0]
    target = loss_target[0]
    pad_taps = lambda t: jnp.pad(t[0], ((0, 5), (0, 0)))

    big_names = ["w_in", "w_out", "w_q", "w_kv", "w_xo"]
    big_w = dict(w_in=w_in[0], w_out=w_out[0], w_q=w_q[0], w_kv=w_kv[0], w_xo=w_xo[0])
    big_m = dict(w_in=m_w_in[0], w_out=m_w_out[0], w_q=m_w_q[0], w_kv=m_w_kv[0], w_xo=m_w_xo[0])
    big_v = dict(w_in=v_w_in[0], w_out=v_w_out[0], w_q=v_w_q[0], w_kv=v_w_kv[0], w_xo=v_w_xo[0])
    shard_axis = dict(w_in=1, w_out=0, w_q=0, w_kv=1, w_xo=0)
    full_shape = dict(w_in=(D_MODEL, IN_DIM), w_out=(MIX_DIM, D_MODEL), w_q=(D_MODEL, D_MODEL),
                      w_kv=(D_MODEL, 2 * D_MODEL), w_xo=(D_MODEL, D_MODEL))
    shard_size = {k: big_w[k].shape[shard_axis[k]] for k in big_names}
    others = big_names[1:]
    gather_spec = lambda k: (full_shape[k], shard_axis[k], shard_size[k])
    reduce_spec = lambda k: (shard_axis[k], shard_size[k])
    ws = gm_ws[0]
    bs_b = jnp.broadcast_to(gm_bs[0][:, :, None], (8, CHUNK, CHUNK))
    g_f = norm_final_g.reshape(1, D_MODEL)

    px, py = lax.axis_index("x"), lax.axis_index("y")
    chip_order = jnp.stack([2 * px + py, 2 * (1 - px) + py, 2 * px + 1 - py, 2 * (1 - px) + 1 - py]).astype(jnp.int32)
    (proj, h1, h1t, w_in_full), ((w_out_full, conv_g),) = _inproj(
        x2, norm_mix_g, big_w["w_in"].astype(BF16), chip_order,
        plans=[_gather_plan([big_w["w_out"].astype(BF16), pad_taps(conv_w)[None]], [gather_spec("w_out"), ((N_DEV, 8, CHUNK), 0, 1)])])
    conv_full = conv_g.transpose(1, 0, 2).reshape(8, D_MODEL)
    attn_w = ["w_q", "w_kv", "w_xo"]
    cat, (gathered,) = _mixer_fwd(
        proj, conv_full, gm_ln_g, gm_ln_b, ws, bs_b,
        plans=[_gather_plan([big_w[k].astype(BF16) for k in attn_w], [gather_spec(k) for k in attn_w])])
    wf = dict(zip(attn_w, gathered), w_out=w_out_full)
    k, vv, mt = _kv_fwd(mem[0], norm_mem_g, wf["w_kv"])
    (dcat, actt, cot, dk, dvv, d_gf, d_gx, sq_err) = _attn(
        x2, cat, target, wf["w_out"], wf["w_q"], wf["w_xo"], k, vv, norm_x_g, g_f)

    sibling_plan = lambda names, grads: _sibling_plan([grads[k] for k in names], [reduce_spec(k) for k in names])
    early = ["w_kv", "w_out", "w_q", "w_xo"]
    big_g = {}
    big_g["w_kv"], d_gmem = _kv_bwd(mem[0], wf["w_kv"], mt, dk, dvv)
    dws, (got_kv,) = _dw_attn(actt, cot, plans=[sibling_plan(early[:1], big_g)])
    big_g.update(zip(early[1:], dws))
    pair_kv, (got_rest,) = _pair_sum("pair_sum_w_kv", [big_g["w_kv"]], got_kv, [reduce_spec("w_kv")], core,
                                     plans=[sibling_plan(early[1:], big_g)])
    pair_rest = _pair_sum("pair_sum_attn", [big_g[k] for k in early[1:]], got_rest, [reduce_spec(k) for k in early[1:]], core)
    (dproj, d_cw, d_lng, d_lnb, d_ws, d_bs), (parts_early,) = _mixer_bwd(
        proj, dcat, conv_full, gm_ln_g, gm_ln_b, ws, bs_b, plans=[_chip_plan(list(pair_kv) + list(pair_rest))])
    half = D_MODEL // 2
    in_spec = reduce_spec("w_in")
    dw_top, _ = _matmul_bf16("dw_in_top", h1t, dproj, rows=(0, half))
    dw_bot, (got_top, (ws_all,)) = _matmul_bf16(
        "dw_in_bot", h1t, dproj, rows=(half, half),
        plans=[_sibling_plan([dw_top], [in_spec]),
               _gather_plan([d_ws.reshape(1, 8 * CHUNK, CHUNK).astype(BF16)], [((N_DEV, 8 * CHUNK, CHUNK), 0, 1)])])
    (pair_top,), (got_bot,) = _pair_sum("pair_sum_w_in_top", [dw_top], got_top, [in_spec], core,
                                        plans=[_sibling_plan([dw_bot], [in_spec])])
    (pair_bot,) = _pair_sum("pair_sum_w_in_bot", [dw_bot], got_bot, [in_spec], core)
    chip_shapes = [p.shape for p in (pair_top, pair_bot)]
    chips_in_flight, token = _split_start("grad_chip_exchange_start", _chip_routes, [pair_top, pair_bot], chip_shapes)
    n_tiles = s // min(TM_DH1, s)
    (grad_x, d_gmix), _ = _dh1("dh1", dproj, w_in_full, x2, cot, norm_mix_g, (0, n_tiles), after=[token])
    vec_names = ["norm_mix_g", "gm_ln_g", "gm_ln_b", "gm_bs", "norm_x_g", "norm_mem_g", "norm_final_g"]
    vec_g = [d_gmix, d_lng, d_lnb, d_bs, d_gx, d_gmem, d_gf]
    small = jnp.concatenate(vec_g + [sq_err, d_cw], axis=0)[None]
    small_in_flight, small_token = _split_start("small_all_gather_start", _gather_routes, [small], [(N_DEV, 16, D_MODEL)])
    flat_ws = lambda t: t.reshape(8 * CHUNK, CHUNK)
    updated, _ = _adamw(
        "adamw_early",
        [(big_w[k], big_m[k], big_v[k], got_k) for k, got_k in zip(early, parts_early)]
        + [(flat_ws(ws), flat_ws(m_gm_ws), flat_ws(v_gm_ws), ws_all)], steps=STEPS_ADAMW // 2)
    big_out = dict(zip(early, updated[:4]))
    ws_out = updated[4]
    (pair_top, pair_bot), (land_top, land_bot) = _split_wait(
        "grad_chip_exchange_wait", _chip_routes, chips_in_flight, [small_token] + [u[0] for u in updated])
    my_chip = (2 * px + py).astype(jnp.int32).reshape(1)
    (big_out["w_in"],), _ = _adamw(
        "adamw_w_in", [(big_w["w_in"], big_m["w_in"], big_v["w_in"], [land_top, land_bot], [pair_top, pair_bot])],
        steps=STEPS_ADAMW, own_slot=my_chip)
    (small,), (small_all,) = _split_wait("small_all_gather_wait", _gather_routes, small_in_flight, [big_out["w_in"][0]])
    row = lambda t: t.reshape(1, D_MODEL)
    vec_out, conv_out, loss_row = _update_small(
        dev.astype(jnp.int32).reshape(1),
        [norm_mix_g, gm_ln_g, gm_ln_b, row(gm_bs), norm_x_g, norm_mem_g, row(norm_final_g)],
        [m_norm_mix_g, m_gm_ln_g, m_gm_ln_b, row(m_gm_bs), m_norm_x_g, m_norm_mem_g, row(m_norm_final_g)],
        [v_norm_mix_g, v_gm_ln_g, v_gm_ln_b, row(v_gm_bs), v_norm_x_g, v_norm_mem_g, row(v_norm_final_g)],
        *(t.transpose(1, 0, 2) for t in (conv_w, m_conv_w, v_conv_w)), small, small_all)
    loss = loss_row[0, 0]

    def result(name, which):
        if name in big_out:
            return big_out[name][which][None]
        if name == "conv_w":
            return conv_out[which].transpose(1, 0, 2)
        if name == "gm_ws":
            return ws_out[which].reshape(1, 8, CHUNK, CHUNK)
        r = vec_out[vec_names.index(name)][which]
        if name == "gm_bs":
            return r.reshape(1, 8, CHUNK)
        if name == "norm_final_g":
            return r.reshape(D_MODEL)
        return r

    weights = ["norm_mix_g", "w_in", "conv_w", "gm_ln_g", "gm_ln_b", "gm_ws", "gm_bs", "w_out", "norm_x_g", "norm_mem_g", "w_q", "w_kv", "w_xo", "norm_final_g"]
    outs = [loss, grad_x[None]]
    for which in range(4):
        outs += [result(name, which) for name in weights]
    return tuple(outs)
```

```python
import math

import jax
import jax.numpy as jnp
from jax import lax
from jax.experimental import pallas as pl
from jax.experimental.pallas import tpu as pltpu

F32 = jnp.float32
BF16 = jnp.bfloat16
MESH = pl.DeviceIdType.MESH

D_MODEL = 1024
N_DEV = 8
N_CHIP = 4
CHUNK = 128
GROUPS = 7
IN_DIM = GROUPS * D_MODEL
SHARD_IN = IN_DIM // N_DEV
MIX_DIM = 2 * D_MODEL
X_HEADS = 4
X_HEAD_DIM = D_MODEL // X_HEADS
MEM_LEN = 256
EPS = 1e-6
GELU_K0 = math.sqrt(2.0 / math.pi)
GELU_K1 = 0.044715

ADAM_LR = 0.001
ADAM_B1 = 0.9
ADAM_B2 = 0.999
ADAM_EPS = 1e-08
ADAM_WD = 0.01
ADAM_STEP = 10

TM_INPROJ = 512
TM_ATTN = 512
SUB_ATTN = 256
TM_DH1 = 512
TM_DW = 512
TK_DW = 4096
BARRIER_ID = {"sibling": 1, "chips": 2, "all": 3}
STEPS_ADAMW = 4


def _dot(a, b):
    return lax.dot_general(a, b, (((1,), (0,)), ((), ())), preferred_element_type=F32)


def _dot_nt(a, b):
    return lax.dot_general(a, b, (((1,), (1,)), ((), ())), preferred_element_type=F32)


def _rows8(v):
    return jnp.sum(v.reshape(v.shape[0] // 8, 8, v.shape[1]), axis=0)


def _gelu(x, with_grad=True):
    x2 = x * x
    t = jnp.tanh(x * (GELU_K0 + (GELU_K0 * GELU_K1) * x2))
    half = 0.5 + 0.5 * t
    g = x * half
    if not with_grad:
        return g
    return g, half + ((0.5 * x) * (1.0 - t * t)) * (GELU_K0 + (3.0 * GELU_K0 * GELU_K1) * x2)


def _silu(z, with_grad=True):
    s = jax.nn.sigmoid(z)
    zs = z * s
    return (zs, s + zs * (1.0 - s)) if with_grad else zs


def _rms(v):
    r = lax.rsqrt(jnp.mean(v * v, axis=-1, keepdims=True) + EPS)
    return v * r, r


def _rms_bwd(dy_g, vh, r):
    return r * (dy_g - vh * jnp.mean(dy_g * vh, axis=-1, keepdims=True))


def _place():
    return lax.axis_index("x"), lax.axis_index("y"), lax.axis_index("c")


def _other_chips(x, y):
    return [(1 - x, y), (x, 1 - y), (1 - x, 1 - y)]


def _pow2_divisor(n):
    return n & (-n)


def _shard_view(ref, axis, size, d):
    start = d * size
    align = _pow2_divisor(size)
    if align > 1:
        start = pl.multiple_of(start, align)
    idx = [slice(None)] * len(ref.shape)
    idx[axis] = pl.ds(start, size)
    return ref.at[tuple(idx)]


class _Plan:
    def __init__(self, ins, out_shape, sems, start, finish, mid=None, aliases=None, peers="all"):
        self.ins, self.out_shape, self.sems = list(ins), list(out_shape), list(sems)
        self.start, self.mid, self.finish = start, mid, finish
        self.aliases = dict(aliases or {})
        self.peers = peers


def _gather_plan(shards, specs):
    n = len(shards)
    n_copy = 7

    def copies(ins, outs, sems):
        send_sems, recv_sems, local_sems = sems
        x, y, c = _place()
        me, sibling = (x, y, c), (x, y, 1 - c)
        chips = _other_chips(x, y)

        def block(a, p):
            _, axis, size = specs[a]
            return _shard_view(outs[a], axis, size, 4 * p[0] + 2 * p[1] + p[2])

        def copy(a, k, p, to, src=None):
            return pltpu.make_async_remote_copy(
                src_ref=block(a, p) if src is None else src,
                dst_ref=block(a, p),
                send_sem=send_sems.at[a * n_copy + k],
                recv_sem=recv_sems.at[a * n_copy + k],
                device_id=to,
                device_id_type=MESH,
            )

        def mine():
            return [pltpu.make_async_copy(ins[a], block(a, me), local_sems.at[a]) for a in range(n)]

        def first():
            return [cp for a in range(n) for cp in
                    [copy(a, 0, me, sibling, src=ins[a])] + [copy(a, 1 + j, me, (*chip, c), src=ins[a]) for j, chip in enumerate(chips)]]

        def landed():
            return [copy(a, 1 + j, (*chip, c), me) for j, chip in enumerate(chips) for a in range(n)]

        def passed():
            return [copy(a, 4 + j, (*chip, c), sibling) for j, chip in enumerate(chips) for a in range(n)]

        def last():
            return [cp for a in range(n) for cp in
                    [copy(a, 0, sibling, me)] + [copy(a, 4 + j, (*chip, 1 - c), me) for j, chip in enumerate(chips)]]

        return mine, first, landed, passed, last

    def start(ins, outs, sems):
        mine, first, _, _, _ = copies(ins, outs, sems)
        for cp in mine() + first():
            cp.start()

    def mid(ins, outs, sems):
        _, _, landed, passed, _ = copies(ins, outs, sems)
        for got, fwd in zip(landed(), passed()):
            got.wait_recv()
            fwd.start()

    def finish(ins, outs, sems):
        mine, first, _, passed, last = copies(ins, outs, sems)
        for cp in last():
            cp.wait_recv()
        for cp in first() + passed():
            cp.wait_send()
        for cp in mine():
            cp.wait()

    return _Plan(
        shards,
        [jax.ShapeDtypeStruct(full, s.dtype) for s, (full, _, _) in zip(shards, specs)],
        [pltpu.SemaphoreType.DMA((n * n_copy,)), pltpu.SemaphoreType.DMA((n * n_copy,)), pltpu.SemaphoreType.DMA((n,))],
        start, finish, mid,
    )


def _sibling_plan(grads, specs):
    n = len(grads)

    def shard_shape(a):
        axis, size = specs[a]
        shp = list(grads[a].shape)
        shp[axis] = size
        return tuple(shp)

    def copies(ins, outs, sems):
        send_sems, recv_sems = sems
        x, y, c = _place()
        return [
            pltpu.make_async_remote_copy(
                src_ref=_shard_view(ins[a], specs[a][0], specs[a][1], 2 * q + (1 - c)),
                dst_ref=outs[a].at[q],
                send_sem=send_sems.at[a * N_CHIP + q],
                recv_sem=recv_sems.at[a * N_CHIP + q],
                device_id=(x, y, 1 - c),
                device_id_type=MESH,
            )
            for a in range(n) for q in range(N_CHIP)
        ]

    def start(ins, outs, sems):
        for cp in copies(ins, outs, sems):
            cp.start()

    def finish(ins, outs, sems):
        for cp in copies(ins, outs, sems):
            cp.wait()

    return _Plan(
        grads,
        [jax.ShapeDtypeStruct((N_CHIP, *shard_shape(a)), grads[a].dtype) for a in range(n)],
        [pltpu.SemaphoreType.DMA((n * N_CHIP,)), pltpu.SemaphoreType.DMA((n * N_CHIP,))],
        start, finish, peers="sibling",
    )


def _chip_plan(parts, full_rows=None, row0=None, into=None):
    n = len(parts)
    full_rows = [p.shape[1] for p in parts] if full_rows is None else full_rows
    row0 = [0] * n if row0 is None else row0

    def copies(ins, outs, sems):
        send_sems, recv_sems, local_sems = sems
        x, y, c = _place()
        my_chip = 2 * x + y
        chips = _other_chips(x, y)

        def land(a, q):
            return outs[a].at[q, pl.ds(row0[a], parts[a].shape[1]), :]

        def mine():
            return [pltpu.make_async_copy(ins[a].at[my_chip], land(a, my_chip), local_sems.at[a]) for a in range(n)]

        def remote(sending):
            return [
                pltpu.make_async_remote_copy(
                    src_ref=ins[a].at[2 * chip[0] + chip[1]],
                    dst_ref=land(a, my_chip if sending else 2 * chip[0] + chip[1]),
                    send_sem=send_sems.at[a * 3 + j], recv_sem=recv_sems.at[a * 3 + j],
                    device_id=(*chip, c), device_id_type=MESH)
                for a in range(n) for j, chip in enumerate(chips)
            ]

        return mine, remote

    def start(ins, outs, sems):
        mine, remote = copies(ins, outs, sems)
        for cp in mine() + remote(True):
            cp.start()

    def finish(ins, outs, sems):
        mine, remote = copies(ins, outs, sems)
        for cp in remote(False):
            cp.wait_recv()
        for cp in remote(True):
            cp.wait_send()
        for cp in mine():
            cp.wait()

    return _Plan(
        list(parts) + (list(into) if into is not None else []),
        [jax.ShapeDtypeStruct((N_CHIP, full_rows[a], parts[a].shape[2]), parts[a].dtype) for a in range(n)],
        [pltpu.SemaphoreType.DMA((n * 3,)), pltpu.SemaphoreType.DMA((n * 3,)), pltpu.SemaphoreType.DMA((n,))],
        start, finish,
        aliases={n + a: a for a in range(n)} if into is not None else None, peers="chips",
    )


def _call(body, *, name, operands, out_shape, grid=(), in_specs=None, out_specs=None, scratch_shapes=(),
          plans=(), aliases=None, prefetch=(), start_after_body=False, pass_on_at=0.6, body_peers=None):
    operands, out_shape, scratch_shapes = list(operands), list(out_shape), list(scratch_shapes)
    n_pf, n_in, n_out, n_scr = len(prefetch), len(operands), len(out_shape), len(scratch_shapes)
    in_vmem = pl.BlockSpec(memory_space=pltpu.VMEM)
    in_hbm = pl.BlockSpec(memory_space=pl.ANY)
    in_specs = [in_vmem] * n_in if in_specs is None else list(in_specs)
    out_specs = [in_vmem] * n_out if out_specs is None else list(out_specs)
    io_alias = {n_pf + i: o for i, o in (aliases or {}).items()}
    pos_in, pos_out = n_pf + n_in, n_out
    for p in plans:
        for i, o in p.aliases.items():
            io_alias[pos_in + i] = pos_out + o
        pos_in += len(p.ins)
        pos_out += len(p.out_shape)
    steps = math.prod(grid) if grid else 1
    mid_step = min(steps - 1, int(steps * pass_on_at))
    patterns = {p.peers for p in plans} | ({body_peers} if body_peers else set())
    pattern = None if not patterns else patterns.pop() if len(patterns) == 1 else "all"

    def shake_hands():
        x, y, c = _place()
        peers = [(x, y, 1 - c)] if pattern in ("sibling", "all") else []
        peers += [(*chip, c) for chip in _other_chips(x, y)] if pattern in ("chips", "all") else []
        barrier = pltpu.get_barrier_semaphore()
        for peer in peers:
            pl.semaphore_signal(barrier, inc=1, device_id=peer, device_id_type=MESH)
        pl.semaphore_wait(barrier, len(peers))

    def wrapped(*refs):
        refs = list(refs)
        take = lambda k: [refs.pop(0) for _ in range(k)]
        pf = take(n_pf)
        ins, p_ins = take(n_in), [take(len(p.ins)) for p in plans]
        outs, p_outs = take(n_out), [take(len(p.out_shape)) for p in plans]
        scr, p_sems = take(n_scr), [take(len(p.sems)) for p in plans]
        step = 0
        for ax, g in enumerate(grid):
            step = step * g + pl.program_id(ax)

        def hook(kind, at):
            todo = [(getattr(p, kind), a, b, c) for p, a, b, c in zip(plans, p_ins, p_outs, p_sems) if getattr(p, kind)]

            def run():
                for fn, a, b, c in todo:
                    fn(a, b, c)

            if todo and grid:
                pl.when(step == at)(run)
            elif todo:
                run()

        if pattern and grid:
            pl.when(step == 0)(shake_hands)
        elif pattern:
            shake_hands()
        if not start_after_body:
            hook("start", 0)
        if body is not None:
            body(*pf, *ins, *outs, *scr)
        if start_after_body:
            hook("start", 0)
        hook("mid", mid_step)
        hook("finish", steps - 1)

    layout = dict(
        grid=grid,
        in_specs=in_specs + [in_hbm] * sum(len(p.ins) for p in plans),
        out_specs=out_specs + [in_hbm] * sum(len(p.out_shape) for p in plans),
        scratch_shapes=scratch_shapes + [s for p in plans for s in p.sems],
    )
    if n_pf:
        layout = dict(grid_spec=pltpu.PrefetchScalarGridSpec(num_scalar_prefetch=n_pf, **layout))
    res = pl.pallas_call(
        wrapped,
        name=name,
        out_shape=out_shape + [o for p in plans for o in p.out_shape],
        input_output_aliases=io_alias,
        compiler_params=pltpu.CompilerParams(
            dimension_semantics=("arbitrary",) * len(grid) if grid else None,
            collective_id=BARRIER_ID[pattern] if pattern else None),
        **layout,
    )(*prefetch, *operands, *[a for p in plans for a in p.ins])
    res = list(res)
    outs, plan_outs = res[:n_out], []
    pos = n_out
    for p in plans:
        plan_outs.append(res[pos : pos + len(p.out_shape)])
        pos += len(p.out_shape)
    return outs, plan_outs


def _inproj(x, g, w_shard, order, plans=()):
    s = x.shape[0]
    tm = min(TM_INPROJ, s)
    nt = s // tm
    wide = 2 * SHARD_IN
    fwd_at = nt // 2

    def body(order_ref, x_ref, g_ref, w_hbm, proj_ref, h1t_ref, wfull_hbm,
             w_scr, h1_scr, send_sems, recv_sems, local_sems):
        k, i = pl.program_id(0), pl.program_id(1)
        px, py, c = _place()
        me, sibling = (px, py, c), (px, py, 1 - c)
        chips = _other_chips(px, py)

        def half(slot, core):
            return w_scr.at[slot, :, pl.ds(pl.multiple_of(core * SHARD_IN, 128), SHARD_IN)]

        def copy(n, slot, core, to, src=None):
            return pltpu.make_async_remote_copy(
                src_ref=half(slot, core) if src is None else src, dst_ref=half(slot, core),
                send_sem=send_sems.at[n], recv_sem=recv_sems.at[n], device_id=to, device_id_type=MESH)

        def mine():
            return pltpu.make_async_copy(w_hbm, half(0, c), local_sems.at[0])

        def first():
            return [copy(0, 0, c, sibling, src=w_hbm)] + [copy(1 + j, 1 + j, c, (*chips[j], c), src=w_hbm) for j in range(2)]

        def relay():
            to = (c * px + (1 - c) * (1 - px), c * (1 - py) + (1 - c) * py, c)
            return pltpu.make_async_remote_copy(
                src_ref=half(2 - c, c), dst_ref=half(3, c), send_sem=send_sems.at[3], recv_sem=recv_sems.at[3],
                device_id=to, device_id_type=MESH)

        def save(slot):
            cols = pl.ds(pl.multiple_of(order_ref[slot] * wide, 128), wide)
            return pltpu.make_async_copy(w_scr.at[slot], wfull_hbm.at[:, cols], local_sems.at[1 + slot])

        @pl.when((k == 0) & (i == 0))
        def _():
            mine().start()
            for cp in first():
                cp.start()
            mine().wait()
            copy(0, 0, 1 - c, me).wait_recv()

        for j in range(3):
            @pl.when((k == j + 1) & (i == 0))
            def _():
                copy(4 + j, 1 + j, 1 - c, me).wait_recv()

        @pl.when((k == 3) & (i == 0))
        def _():
            for slot in range(N_CHIP):
                save(slot).start()

        @pl.when(k == 0)
        def _():
            xh, _ = _rms(x_ref[...])
            h = xh * g_ref[...]
            h1t_ref[...] = h.T.astype(BF16)
            h1_scr[pl.ds(pl.multiple_of(i * tm, tm), tm), :] = h.astype(BF16)

        proj_ref[...] = _dot(h1_scr[pl.ds(pl.multiple_of(i * tm, tm), tm), :], w_scr[k]).astype(BF16)

        for j in range(3):
            @pl.when((k == 0) & (i == nt - 1) if j < 2 else (k == 2) & (i == fwd_at))
            def _():
                copy(1 + j, 1 + j, c, me).wait_recv()
                copy(4 + j, 1 + j, c, sibling).start()
                if j < 2:
                    pl.when(c == 1 - j)(lambda: relay().start())

        @pl.when((k == 3) & (i == nt - 1))
        def _():
            for cp in first() + [relay()] + [copy(4 + j, 1 + j, c, sibling) for j in range(3)]:
                cp.wait_send()
            for slot in range(N_CHIP):
                save(slot).wait()

    once = lambda k, i, order_ref: (jnp.where(k == 0, i, nt - 1), 0)
    return _call(
        body,
        name="inproj",
        prefetch=[order],
        operands=[x, g, w_shard],
        grid=(N_CHIP, nt),
        in_specs=[
            pl.BlockSpec((tm, D_MODEL), once),
            pl.BlockSpec((1, D_MODEL), lambda k, i, order_ref: (0, 0)),
            pl.BlockSpec(memory_space=pl.ANY),
        ],
        out_specs=[
            pl.BlockSpec((tm, wide), lambda k, i, order_ref: (i, order_ref[k])),
            pl.BlockSpec((D_MODEL, tm), lambda k, i, order_ref: (0, jnp.where(k == 0, i, nt - 1))),
            pl.BlockSpec(memory_space=pl.ANY),
        ],
        out_shape=[
            jax.ShapeDtypeStruct((s, IN_DIM), BF16),
            jax.ShapeDtypeStruct((D_MODEL, s), BF16),
            jax.ShapeDtypeStruct((D_MODEL, IN_DIM), BF16),
        ],
        scratch_shapes=[
            pltpu.VMEM((N_CHIP, D_MODEL, wide), BF16),
            pltpu.VMEM((s, D_MODEL), BF16),
            pltpu.SemaphoreType.DMA((7,)),
            pltpu.SemaphoreType.DMA((7,)),
            pltpu.SemaphoreType.DMA((1 + N_CHIP,)),
        ],
        plans=plans,
        start_after_body=True,
        pass_on_at=0.8,
        body_peers="all",
    )


def _rows_from_above(v, above):
    row = lax.broadcasted_iota(jnp.int32, (8, v.shape[1]), 0)
    r1, r2 = pltpu.roll(v, 1, 0), pltpu.roll(v, 2, 0)
    top1 = jnp.where(row == 0, above[7:8], r1[0:8])
    top2 = jnp.where(row == 0, above[6:7], jnp.where(row == 1, above[7:8], r2[0:8]))
    return jnp.concatenate([top1, r1[8:]], axis=0), jnp.concatenate([top2, r2[8:]], axis=0)


def _rows_from_below(v, below):
    n = v.shape[0]
    row = lax.broadcasted_iota(jnp.int32, (8, v.shape[1]), 0)
    r1, r2 = pltpu.roll(v, n - 1, 0), pltpu.roll(v, n - 2, 0)
    end1 = jnp.where(row == 7, below[0:1], r1[n - 8 :])
    end2 = jnp.where(row == 6, below[0:1], jnp.where(row == 7, below[1:2], r2[n - 8 :]))
    return jnp.concatenate([r1[: n - 8], end1], axis=0), jnp.concatenate([r2[: n - 8], end2], axis=0)


def _causal_mask():
    row = lax.broadcasted_iota(jnp.int32, (CHUNK, CHUNK), 0)
    col = lax.broadcasted_iota(jnp.int32, (CHUNK, CHUNK), 1)
    return row, row >= col


def _mixer_fwd(proj, conv_w, ln_g, ln_b, ws, bs_b, plans=()):
    s = proj.shape[0]
    nt = s // CHUNK

    def body(proj_ref, cw_ref, lng_ref, lnb_ref, ws_ref, bsb_ref, cat_ref, pcar_ref):
        @pl.when(pl.program_id(0) == 0)
        def _():
            pcar_ref[...] = jnp.zeros_like(pcar_ref)

        row, tril = _causal_mask()
        for j in range(8):
            cs = slice(CHUNK * j, CHUNK * (j + 1))

            def grp(k):
                return proj_ref[:, k * D_MODEL + CHUNK * j : k * D_MODEL + CHUNK * (j + 1)].astype(F32)

            w = cw_ref[:, cs]
            p = grp(1) * grp(2)
            p1, p2 = _rows_from_above(p, pcar_ref[:, cs])
            pcar_ref[:, cs] = p[CHUNK - 8 :]
            cv = w[0:1] * p2 + w[1:2] * p1 + w[2:3] * p
            sa = _silu(grp(3), with_grad=False)
            cat_ref[:, cs] = ((grp(0) * cv) * sa).astype(BF16)
            gu = _gelu(grp(4), with_grad=False)
            gv = _gelu(grp(5), with_grad=False)
            dv = gv - jnp.mean(gv, axis=-1, keepdims=True)
            vn = dv * lax.rsqrt(jnp.mean(dv * dv, axis=-1, keepdims=True) + EPS)
            vn = vn * lng_ref[:, cs] + lnb_ref[:, cs]
            wc = jnp.where(tril, ws_ref[j], 0.0).astype(BF16)
            sp = _dot(wc, vn.astype(BF16)) + bsb_ref[j]
            sb = _silu(grp(6), with_grad=False)
            cat_ref[:, D_MODEL + CHUNK * j : D_MODEL + CHUNK * (j + 1)] = ((gu * sp) * sb).astype(BF16)

    whole = lambda shape: pl.BlockSpec(shape, lambda i: (0,) * len(shape))
    (cat,), plan_outs = _call(
        body,
        name="mixer_fwd",
        operands=[proj, conv_w, ln_g, ln_b, ws, bs_b],
        grid=(nt,),
        in_specs=[
            pl.BlockSpec((CHUNK, IN_DIM), lambda i: (i, 0)),
            whole((8, D_MODEL)),
            whole((1, D_MODEL)),
            whole((1, D_MODEL)),
            whole((8, CHUNK, CHUNK)),
            whole((8, CHUNK, CHUNK)),
        ],
        out_specs=[pl.BlockSpec((CHUNK, MIX_DIM), lambda i: (i, 0))],
        out_shape=[jax.ShapeDtypeStruct((s, MIX_DIM), BF16)],
        scratch_shapes=[pltpu.VMEM((8, D_MODEL), F32)],
        plans=plans,
        pass_on_at=0.85,
    )
    return cat, plan_outs


def _mixer_bwd(proj, dcat, conv_w, ln_g, ln_b, ws, bs_b, plans=()):
    s = proj.shape[0]
    nt = s // CHUNK

    def body(proj_ref, halo_ref, dcat_ref, cw_ref, lng_ref, lnb_ref, ws_ref, bsb_ref,
             dproj_ref, dcw_ref, dlng_ref, dlnb_ref, dws_ref, dbs_ref,
             car_ref, acc_cw, acc_lng, acc_lnb, acc_bs, wc_scr, wct_scr):
        i = pl.program_id(0)
        tile = nt - 1 - i
        row, tril = _causal_mask()

        @pl.when(i == 0)
        def _():
            for j in range(8):
                wc = jnp.where(tril, ws_ref[j], 0.0)
                wc_scr[j] = wc.astype(BF16)
                wct_scr[j] = wc.T.astype(BF16)
            car_ref[...] = jnp.zeros_like(car_ref)
            acc_cw[...] = jnp.zeros_like(acc_cw)
            acc_lng[...] = jnp.zeros_like(acc_lng)
            acc_lnb[...] = jnp.zeros_like(acc_lnb)
            acc_bs[...] = jnp.zeros_like(acc_bs)
            dws_ref[...] = jnp.zeros_like(dws_ref)

        has_prev = jnp.where(tile > 0, 1.0, 0.0).astype(F32)
        for j in range(8):
            cs = slice(CHUNK * j, CHUNK * (j + 1))

            def col(k):
                return slice(k * D_MODEL + CHUNK * j, k * D_MODEL + CHUNK * (j + 1))

            def grp(k):
                return proj_ref[:, col(k)].astype(F32)

            w = cw_ref[:, cs]
            gb, gc, xa = grp(0), grp(1), grp(2)
            p = gc * xa
            pprev = halo_ref[:, col(1)].astype(F32) * halo_ref[:, col(2)].astype(F32) * has_prev
            p1, p2 = _rows_from_above(p, pprev[8:16])
            cv = w[0:1] * p2 + w[1:2] * p1 + w[2:3] * p
            za = grp(3)
            sa, dsa = _silu(za)
            da = dcat_ref[:, cs].astype(F32)
            da_gb = da * gb
            dproj_ref[:, col(0)] = ((da * sa) * cv).astype(BF16)
            dproj_ref[:, col(3)] = ((da_gb * cv) * dsa).astype(BF16)
            dcv = da_gb * sa
            d1, d2 = _rows_from_below(dcv, car_ref[:, cs])
            car_ref[:, cs] = dcv[0:8]
            dp = w[2:3] * dcv + w[1:2] * d1 + w[0:1] * d2
            dproj_ref[:, col(1)] = (dp * xa).astype(BF16)
            dproj_ref[:, col(2)] = (dp * gc).astype(BF16)
            acc_cw[0, :, cs] += _rows8(dcv * p2)
            acc_cw[1, :, cs] += _rows8(dcv * p1)
            acc_cw[2, :, cs] += _rows8(dcv * p)
            gu, dgu = _gelu(grp(4))
            gv, dgv = _gelu(grp(5))
            dv = gv - jnp.mean(gv, axis=-1, keepdims=True)
            rstd = lax.rsqrt(jnp.mean(dv * dv, axis=-1, keepdims=True) + EPS)
            vnh = dv * rstd
            lng = lng_ref[:, cs]
            vnb = (vnh * lng + lnb_ref[:, cs]).astype(BF16)
            sp = _dot(wc_scr[j], vnb) + bsb_ref[j]
            zb = grp(6)
            sb, dsb = _silu(zb)
            db = dcat_ref[:, D_MODEL + CHUNK * j : D_MODEL + CHUNK * (j + 1)].astype(F32)
            db_sp, db_gu = db * sp, db * gu
            dproj_ref[:, col(4)] = ((db_sp * sb) * dgu).astype(BF16)
            dproj_ref[:, col(6)] = ((db_sp * gu) * dsb).astype(BF16)
            dsp = db_gu * sb
            acc_bs[j] += dsp
            dspb = dsp.astype(BF16)
            dws_ref[j] += _dot_nt(dspb, vnb)
            dvn = _dot(wct_scr[j], dspb)
            acc_lnb[:, cs] += _rows8(dvn)
            acc_lng[:, cs] += _rows8(dvn * vnh)
            dvh = dvn * lng
            dgvv = rstd * (dvh - jnp.mean(dvh, axis=-1, keepdims=True) - vnh * jnp.mean(dvh * vnh, axis=-1, keepdims=True))
            dproj_ref[:, col(5)] = (dgvv * dgv).astype(BF16)

        @pl.when(i == nt - 1)
        def _():
            dlng_ref[...] = jnp.sum(acc_lng[...], axis=0, keepdims=True)
            dlnb_ref[...] = jnp.sum(acc_lnb[...], axis=0, keepdims=True)
            dcw_ref[...] = jnp.zeros_like(dcw_ref)
            for k in range(3):
                dcw_ref[k : k + 1, :] = jnp.sum(acc_cw[k], axis=0, keepdims=True)
            for j in range(8):
                dws_ref[j] = jnp.where(tril, dws_ref[j], 0.0)
                dbs_ref[:, CHUNK * j : CHUNK * (j + 1)] = jnp.sum(acc_bs[j].T, axis=0, keepdims=True)

    whole = lambda shape: pl.BlockSpec(shape, lambda i: (0,) * len(shape))
    halo_rows = 16
    per = CHUNK // halo_rows
    return _call(
        body,
        name="mixer_bwd",
        operands=[proj, proj, dcat, conv_w, ln_g, ln_b, ws, bs_b],
        plans=plans,
        grid=(nt,),
        in_specs=[
            pl.BlockSpec((CHUNK, IN_DIM), lambda i: (nt - 1 - i, 0)),
            pl.BlockSpec((halo_rows, IN_DIM), lambda i: (jnp.maximum((nt - 1 - i) * per - 1, 0), 0)),
            pl.BlockSpec((CHUNK, MIX_DIM), lambda i: (nt - 1 - i, 0)),
            whole((8, D_MODEL)),
            whole((1, D_MODEL)),
            whole((1, D_MODEL)),
            whole((8, CHUNK, CHUNK)),
            whole((8, CHUNK, CHUNK)),
        ],
        out_specs=[
            pl.BlockSpec((CHUNK, IN_DIM), lambda i: (nt - 1 - i, 0)),
            whole((8, D_MODEL)),
            whole((1, D_MODEL)),
            whole((1, D_MODEL)),
            whole((8, CHUNK, CHUNK)),
            whole((1, D_MODEL)),
        ],
        out_shape=[
            jax.ShapeDtypeStruct((s, IN_DIM), BF16),
            jax.ShapeDtypeStruct((8, D_MODEL), F32),
            jax.ShapeDtypeStruct((1, D_MODEL), F32),
            jax.ShapeDtypeStruct((1, D_MODEL), F32),
            jax.ShapeDtypeStruct((8, CHUNK, CHUNK), F32),
            jax.ShapeDtypeStruct((1, D_MODEL), F32),
        ],
        scratch_shapes=[
            pltpu.VMEM((8, D_MODEL), F32),
            pltpu.VMEM((3, 8, D_MODEL), F32),
            pltpu.VMEM((8, D_MODEL), F32),
            pltpu.VMEM((8, D_MODEL), F32),
            pltpu.VMEM((8, CHUNK, CHUNK), F32),
            pltpu.VMEM((8, CHUNK, CHUNK), BF16),
            pltpu.VMEM((8, CHUNK, CHUNK), BF16),
        ],
    )


def _kv_fwd(mem, g_mem, w_kv):
    def body(mem_ref, g_ref, w_ref, k_ref, vv_ref, mt_ref):
        mh, _ = _rms(mem_ref[...])
        m = mh * g_ref[...]
        kv = _dot(m.astype(BF16), w_ref[...])
        k_ref[...] = kv[:, :D_MODEL].astype(BF16)
        vv_ref[...] = kv[:, D_MODEL:].astype(BF16)
        mt_ref[...] = m.T.astype(BF16)

    return pl.pallas_call(
        body,
        name="kv_fwd",
        out_shape=[
            jax.ShapeDtypeStruct((MEM_LEN, D_MODEL), BF16),
            jax.ShapeDtypeStruct((MEM_LEN, D_MODEL), BF16),
            jax.ShapeDtypeStruct((D_MODEL, MEM_LEN), BF16),
        ],
    )(mem, g_mem, w_kv)


def _kv_bwd(mem, w_kv, mt, dk, dvv):
    def body(mem_ref, w_ref, mt_ref, dk_ref, dvv_ref, dw_ref, dg_ref):
        dkv = jnp.concatenate([dk_ref[...], dvv_ref[...]], axis=1).astype(BF16)
        dw_ref[...] = _dot(mt_ref[...], dkv).astype(BF16)
        dm = _dot_nt(dkv, w_ref[...])
        mh, _ = _rms(mem_ref[...])
        dg_ref[...] = jnp.sum(dm * mh, axis=0, keepdims=True)

    return pl.pallas_call(
        body,
        name="kv_bwd",
        out_shape=[jax.ShapeDtypeStruct((D_MODEL, 2 * D_MODEL), BF16), jax.ShapeDtypeStruct((1, D_MODEL), F32)],
    )(mem, w_kv, mt, dk, dvv)


def _attn(x, cat, target, w_out, w_q, w_xo, k, vv, g_x, g_f):
    s = x.shape[0]
    tm = min(TM_ATTN, s)
    sub = min(SUB_ATTN, tm)
    nt = s // tm
    scale = 1.0 / math.sqrt(X_HEAD_DIM)

    def body(x_ref, cat_ref, t_ref, wout_ref, wq_ref, wxo_ref, k_ref, vv_ref, gx_ref, gf_ref,
             dcat_ref, actt_ref, cot_ref,
             dk_ref, dvv_ref, dgf_ref, dgx_ref, loss_ref,
             p_scr, q_scr, o_scr, dq_scr, acc_gf, acc_gx, acc_loss):
        i = pl.program_id(0)

        @pl.when(i == 0)
        def _():
            dk_ref[...] = jnp.zeros_like(dk_ref)
            dvv_ref[...] = jnp.zeros_like(dvv_ref)
            acc_gf[...] = jnp.zeros_like(acc_gf)
            acc_gx[...] = jnp.zeros_like(acc_gx)
            acc_loss[...] = jnp.zeros_like(acc_loss)

        subs = [slice(u * sub, (u + 1) * sub) for u in range(tm // sub)]
        heads = [slice(X_HEAD_DIM * h, X_HEAD_DIM * (h + 1)) for h in range(X_HEADS)]
        gx, gf = gx_ref[...], gf_ref[...]
        x1, x1h, r2 = [], [], []
        for rs in subs:
            cat = cat_ref[rs, :]
            actt_ref[0:MIX_DIM, rs] = cat.astype(F32).T.astype(BF16)
            v = x_ref[rs, :] + _dot(cat, wout_ref[...])
            vh, r = _rms(v)
            h2 = vh * gx
            actt_ref[MIX_DIM : MIX_DIM + D_MODEL, rs] = h2.T.astype(BF16)
            q_scr[rs, :] = _dot(h2.astype(BF16), wq_ref[...]).astype(BF16)
            x1.append(v), x1h.append(vh), r2.append(r)
        for rs in subs:
            for hs in heads:
                sc = _dot_nt(q_scr[rs, hs], k_ref[:, hs]) * scale
                e = jnp.exp(sc - jnp.max(sc, axis=-1, keepdims=True))
                p = e / jnp.sum(e, axis=-1, keepdims=True)
                p_scr[rs, hs] = p
                o_scr[rs, hs] = _dot(p.astype(BF16), vv_ref[:, hs])
        dx2, sq, d_gf = [], None, None
        for u, rs in enumerate(subs):
            o = o_scr[rs, :]
            actt_ref[MIX_DIM + D_MODEL :, rs] = o.T.astype(BF16)
            x2 = x1[u] + _dot(o.astype(BF16), wxo_ref[...])
            x2h, r3 = _rms(x2)
            err = x2h * gf - t_ref[rs, :]
            dy = err * (1.0 / D_MODEL)
            sq = _rows8(err * err) if sq is None else sq + _rows8(err * err)
            d_gf = _rows8(dy * x2h) if d_gf is None else d_gf + _rows8(dy * x2h)
            dx2.append(_rms_bwd(dy * gf, x2h, r3))
        acc_loss[...] += sq
        acc_gf[...] += d_gf
        d_vv, d_k = [None] * X_HEADS, [None] * X_HEADS
        for u, rs in enumerate(subs):
            dx2b = dx2[u].astype(BF16)
            cot_ref[2, rs, :] = dx2b
            dob = _dot_nt(dx2b, wxo_ref[...]).astype(BF16)
            for h, hs in enumerate(heads):
                p = p_scr[rs, hs]
                dp = _dot_nt(dob[:, hs], vv_ref[:, hs])
                ds = p * (dp - jnp.sum(dp * p, axis=-1, keepdims=True)) * scale
                dq_scr[rs, hs] = _dot(ds.astype(BF16), k_ref[:, hs]).astype(BF16)
                dvv_h = _dot(p.T.astype(BF16), dob[:, hs])
                dk_h = _dot(ds.T.astype(BF16), q_scr[rs, hs])
                d_vv[h] = dvv_h if d_vv[h] is None else d_vv[h] + dvv_h
                d_k[h] = dk_h if d_k[h] is None else d_k[h] + dk_h
        for h, hs in enumerate(heads):
            dvv_ref[:, hs] += d_vv[h]
            dk_ref[:, hs] += d_k[h]
        d_gx = None
        for u, rs in enumerate(subs):
            dqb = dq_scr[rs, :]
            cot_ref[1, rs, :] = dqb
            dh2 = _dot_nt(dqb, wq_ref[...])
            d_gx = _rows8(dh2 * x1h[u]) if d_gx is None else d_gx + _rows8(dh2 * x1h[u])
            dx1 = dx2[u] + _rms_bwd(dh2 * gx, x1h[u], r2[u])
            dx1b = dx1.astype(BF16)
            cot_ref[0, rs, :] = dx1b
            dcat_ref[rs, :] = _dot_nt(dx1b, wout_ref[...]).astype(BF16)
        acc_gx[...] += d_gx

        @pl.when(i == nt - 1)
        def _():
            dgf_ref[...] = jnp.sum(acc_gf[...], axis=0, keepdims=True)
            dgx_ref[...] = jnp.sum(acc_gx[...], axis=0, keepdims=True)
            loss_ref[...] = jnp.sum(acc_loss[...], axis=0, keepdims=True)

    whole = lambda shape: pl.BlockSpec(shape, lambda i: (0,) * len(shape))
    once = lambda shape: pl.BlockSpec(shape, lambda i: (0,) * len(shape), pipeline_mode=pl.Buffered(1))
    rows = lambda width: pl.BlockSpec((tm, width), lambda i: (i, 0))
    cols = lambda height: pl.BlockSpec((height, tm), lambda i: (0, i))
    vec = jax.ShapeDtypeStruct((1, D_MODEL), F32)
    return pl.pallas_call(
        body,
        name="attn",
        grid=(nt,),
        in_specs=[
            rows(D_MODEL), rows(MIX_DIM), rows(D_MODEL),
            once((MIX_DIM, D_MODEL)), once((D_MODEL, D_MODEL)), once((D_MODEL, D_MODEL)),
            once((MEM_LEN, D_MODEL)), once((MEM_LEN, D_MODEL)),
            whole((1, D_MODEL)), whole((1, D_MODEL)),
        ],
        out_specs=[
            rows(MIX_DIM), cols(2 * MIX_DIM),
            pl.BlockSpec((3, tm, D_MODEL), lambda i: (0, i, 0)),
            whole((MEM_LEN, D_MODEL)), whole((MEM_LEN, D_MODEL)),
            whole((1, D_MODEL)), whole((1, D_MODEL)), whole((1, D_MODEL)),
        ],
        out_shape=[
            jax.ShapeDtypeStruct((s, MIX_DIM), BF16),
            jax.ShapeDtypeStruct((2 * MIX_DIM, s), BF16),
            jax.ShapeDtypeStruct((3, s, D_MODEL), BF16),
            jax.ShapeDtypeStruct((MEM_LEN, D_MODEL), F32),
            jax.ShapeDtypeStruct((MEM_LEN, D_MODEL), F32),
            vec, vec, vec,
        ],
        scratch_shapes=[
            pltpu.VMEM((tm, D_MODEL), F32),
            pltpu.VMEM((tm, D_MODEL), BF16),
            pltpu.VMEM((tm, D_MODEL), F32),
            pltpu.VMEM((tm, D_MODEL), BF16),
            pltpu.VMEM((8, D_MODEL), F32),
            pltpu.VMEM((8, D_MODEL), F32),
            pltpu.VMEM((8, D_MODEL), F32),
        ],
        compiler_params=pltpu.CompilerParams(dimension_semantics=("arbitrary",)),
    )(x, cat, target, w_out, w_q, w_xo, k, vv, g_x, g_f)


def _matmul_bf16(name, a, b, rows=None, plans=()):
    first, m = (0, a.shape[0]) if rows is None else rows
    kk = a.shape[1]
    n = b.shape[1]
    tm, tn, tk = min(m, TM_DW), min(n, D_MODEL), min(kk, TK_DW)
    nk = kk // tk
    i0 = first // tm

    def body(a_ref, b_ref, o_ref, *acc):
        prod = _dot(a_ref[...], b_ref[...])
        if nk == 1:
            o_ref[...] = prod.astype(BF16)
            return
        (acc_ref,) = acc
        kid = pl.program_id(2)

        @pl.when(kid == 0)
        def _():
            acc_ref[...] = prod

        @pl.when(kid > 0)
        def _():
            acc_ref[...] += prod

        @pl.when(kid == nk - 1)
        def _():
            o_ref[...] = acc_ref[...].astype(BF16)

    (out,), plan_outs = _call(
        body,
        name=name,
        operands=[a, b],
        grid=(m // tm, n // tn, nk),
        in_specs=[pl.BlockSpec((tm, tk), lambda i, j, k: (i + i0, k)), pl.BlockSpec((tk, tn), lambda i, j, k: (k, j))],
        out_specs=[pl.BlockSpec((tm, tn), lambda i, j, k: (i, j))],
        out_shape=[jax.ShapeDtypeStruct((m, n), BF16)],
        scratch_shapes=[pltpu.VMEM((tm, tn), F32)] if nk > 1 else [],
        plans=plans,
    )
    return out, plan_outs


def _dw_attn(actt, cot, plans=()):
    s = actt.shape[1]
    tm = TM_DW
    counts = [MIX_DIM // tm, D_MODEL // tm, D_MODEL // tm]
    starts = [0, counts[0], counts[0] + counts[1]]
    which = lambda i: jnp.where(i < starts[1], 0, jnp.where(i < starts[2], 1, 2))

    def body(a_ref, b_ref, *outs):
        i = pl.program_id(0)
        prod = _dot(a_ref[...], b_ref[0]).astype(BF16)
        for m in range(3):
            @pl.when(which(i) == m)
            def _():
                outs[m][...] = prod

    out_specs = [pl.BlockSpec((tm, D_MODEL), lambda i, m=m: (jnp.clip(i - starts[m], 0, counts[m] - 1), 0)) for m in range(3)]
    out_shape = [jax.ShapeDtypeStruct((counts[m] * tm, D_MODEL), BF16) for m in range(3)]
    outs, plan_outs = _call(
        body,
        name="dw_attn",
        operands=[actt, cot],
        grid=(sum(counts),),
        in_specs=[pl.BlockSpec((tm, s), lambda i: (i, 0)), pl.BlockSpec((1, s, D_MODEL), lambda i: (which(i), 0, 0))],
        out_specs=out_specs,
        out_shape=out_shape,
        plans=plans,
    )
    return outs, plan_outs


def _dh1(dproj, w_in, x, dx1, g, after=()):
    s = x.shape[0]
    tm = min(TM_DH1, s)
    nt = s // tm

    def body(dp_ref, w_ref, x_ref, dx1_ref, g_ref, *rest):
        gx_ref, dg_ref, acc_g = rest[-3:]
        i = pl.program_id(0)

        @pl.when(i == 0)
        def _():
            acc_g[...] = jnp.zeros_like(acc_g)

        dh1 = _dot_nt(dp_ref[...], w_ref[...])
        xh, r = _rms(x_ref[...])
        acc_g[...] += _rows8(dh1 * xh)
        gx_ref[...] = dx1_ref[0].astype(F32) + _rms_bwd(dh1 * g_ref[...], xh, r)

        @pl.when(i == nt - 1)
        def _():
            dg_ref[...] = jnp.sum(acc_g[...], axis=0, keepdims=True)

    rows = pl.BlockSpec((tm, D_MODEL), lambda i: (i, 0))
    outs, _ = _call(
        body,
        name="dh1",
        operands=[dproj, w_in, x, dx1, g] + list(after),
        grid=(nt,),
        in_specs=[
            pl.BlockSpec((tm, IN_DIM), lambda i: (i, 0)),
            pl.BlockSpec((D_MODEL, IN_DIM), lambda i: (0, 0), pipeline_mode=pl.Buffered(1)),
            rows,
            pl.BlockSpec((1, tm, D_MODEL), lambda i: (0, i, 0)),
            pl.BlockSpec((1, D_MODEL), lambda i: (0, 0)),
        ] + [pl.BlockSpec(memory_space=pl.ANY)] * len(after),
        out_specs=[rows, pl.BlockSpec((1, D_MODEL), lambda i: (0, 0))],
        out_shape=[jax.ShapeDtypeStruct((s, D_MODEL), F32), jax.ShapeDtypeStruct((1, D_MODEL), F32)],
        scratch_shapes=[pltpu.VMEM((8, D_MODEL), F32)],
    )
    return outs


def _pair_sum(name, grads, gots, specs, core, plans=()):
    n = len(grads)
    shards = []
    for grad, (axis, size) in zip(grads, specs):
        shard = list(grad.shape)
        shard[axis] = size
        shards.append(tuple(shard))

    def body(core_ref, *refs):
        for g_ref, got_ref, o_ref in zip(refs[:n], refs[n : 2 * n], refs[2 * n :]):
            o_ref[0] = (g_ref[...].astype(F32) + got_ref[0].astype(F32)).astype(BF16)

    def own(shard, axis):
        if axis == 1:
            return pl.BlockSpec(shard, lambda q, core_ref: (0, 2 * q + core_ref[0]))
        return pl.BlockSpec(shard, lambda q, core_ref: (2 * q + core_ref[0], 0))

    slots = [pl.BlockSpec((1, *shard), lambda q, core_ref: (q, 0, 0)) for shard in shards]
    outs, plan_outs = _call(
        body,
        name=name,
        prefetch=[core],
        operands=list(grads) + list(gots),
        grid=(N_CHIP,),
        in_specs=[own(shard, axis) for shard, (axis, _) in zip(shards, specs)] + slots,
        out_specs=slots,
        out_shape=[jax.ShapeDtypeStruct((N_CHIP, *shard), BF16) for shard in shards],
        plans=plans,
    )
    return (outs, plan_outs) if plans else outs


def _adam_step(w, m, v, g):
    c1 = 1.0 / (1.0 - ADAM_B1**ADAM_STEP)
    c2 = 1.0 / (1.0 - ADAM_B2**ADAM_STEP)
    m_new = ADAM_B1 * m + (1.0 - ADAM_B1) * g
    v_new = ADAM_B2 * v + (1.0 - ADAM_B2) * (g * g)
    return -ADAM_LR * ((m_new * c1) / (jnp.sqrt(v_new * c2) + ADAM_EPS) + ADAM_WD * w), m_new, v_new


def _update_small(dev, vec_w, vec_m, vec_v, conv_w, conv_m, conv_v, small, landed):
    n_vec = len(vec_w)

    def body(dev_ref, *refs):
        refs = list(refs)
        take = lambda k: [refs.pop(0) for _ in range(k)]
        w_refs, m_refs, v_refs = take(n_vec), take(n_vec), take(n_vec)
        cw_ref, cm_ref, cv_ref, small_ref, land_ref, small_conv_ref, land_conv_ref = take(7)
        vec_outs, conv_outs, (loss_ref,) = take(4 * n_vec), take(4), take(1)

        def total(own_ref, others_ref):
            acc = None
            for d in range(N_DEV):
                part = jnp.where(dev_ref[0] == d, own_ref[0], others_ref[d])
                acc = part if acc is None else acc + part
            return acc

        g_all = total(small_ref, land_ref)
        for i in range(n_vec):
            g = g_all[i : i + 1]
            delta, m_new, v_new = _adam_step(w_refs[i][...], m_refs[i][...], v_refs[i][...], g)
            for ref, val in zip(vec_outs[4 * i : 4 * i + 4], (g, delta, m_new, v_new)):
                ref[...] = val
        g_conv = total(small_conv_ref, land_conv_ref)
        for tap in range(3):
            g = g_conv[tap : tap + 1]
            delta, m_new, v_new = _adam_step(cw_ref[tap], cm_ref[tap], cv_ref[tap], g)
            for ref, val in zip(conv_outs, (g, delta, m_new, v_new)):
                ref[tap] = val
        loss_ref[...] = jnp.broadcast_to(0.5 * jnp.sum(g_all[7:8], axis=-1, keepdims=True) / D_MODEL, loss_ref.shape)

    vec_spec = pl.BlockSpec((1, D_MODEL), lambda i, dev_ref: (0, 0))
    conv_spec = pl.BlockSpec((3, 1, CHUNK), lambda i, dev_ref: (0, 0, 0))
    vec_shape = jax.ShapeDtypeStruct((1, D_MODEL), F32)
    conv_shape = jax.ShapeDtypeStruct((3, 1, CHUNK), F32)
    outs, _ = _call(
        body,
        name="update_small",
        prefetch=[dev],
        operands=list(vec_w) + list(vec_m) + list(vec_v) + [conv_w, conv_m, conv_v, small, landed, small, landed],
        grid=(1,),
        in_specs=[vec_spec] * (3 * n_vec) + [conv_spec] * 3 + [
            pl.BlockSpec((1, 16, D_MODEL), lambda i, dev_ref: (0, 0, 0)),
            pl.BlockSpec((N_DEV, 16, D_MODEL), lambda i, dev_ref: (0, 0, 0)),
            pl.BlockSpec((1, 8, CHUNK), lambda i, dev_ref: (0, 1, dev_ref[0])),
            pl.BlockSpec((N_DEV, 8, CHUNK), lambda i, dev_ref: (0, 1, dev_ref[0])),
        ],
        out_specs=[vec_spec] * (4 * n_vec) + [conv_spec] * 4 + [pl.BlockSpec((1, CHUNK), lambda i, dev_ref: (0, 0))],
        out_shape=[vec_shape] * (4 * n_vec) + [conv_shape] * 4 + [jax.ShapeDtypeStruct((1, CHUNK), F32)],
    )
    return [outs[4 * i : 4 * i + 4] for i in range(n_vec)], outs[4 * n_vec : 4 * n_vec + 4], outs[-1]


def _adamw(name, groups, steps, own_slot=None, plans=()):
    lists = lambda p: list(p) if isinstance(p, (list, tuple)) else [p]
    groups = [(g[0], g[1], g[2], lists(g[3]), lists(g[4]) if len(g) > 4 else []) for g in groups]
    counts = [3 + len(g[3]) + len(g[4]) for g in groups]
    use_slot = own_slot is not None

    def first_tiles(w, parts):
        firsts, first = [], 0
        for p in parts:
            firsts.append(first)
            first += p.shape[1] // (w.shape[0] // steps)
        return firsts

    def update(t, slot_ref, firsts, w_ref, m_ref, v_ref, p_refs, o_refs, g_ref, d_ref, nm_ref, nv_ref):
        g = None
        for k, p_ref in enumerate(p_refs):
            gk = None
            for q in range(p_ref.shape[0]):
                part = p_ref[q].astype(F32)
                if o_refs:
                    part = jnp.where(slot_ref[0] == q, o_refs[k][0].astype(F32), part)
                gk = part if gk is None else gk + part
            g = gk if g is None else jnp.where(t >= firsts[k], gk, g)
        g_ref[...] = g
        d_ref[...], nm_ref[...], nv_ref[...] = _adam_step(w_ref[...], m_ref[...], v_ref[...], g)

    def body(*refs):
        refs = list(refs)
        slot_ref = refs.pop(0) if use_slot else None
        t = pl.program_id(0)
        ins, outs = refs[: sum(counts)], refs[sum(counts) :]
        for i, (w, _, _, parts, own) in enumerate(groups):
            mine = ins[sum(counts[:i]) : sum(counts[: i + 1])]
            update(t, slot_ref, first_tiles(w, parts), *mine[:3], mine[3 : 3 + len(parts)], mine[3 + len(parts) :],
                   *outs[4 * i : 4 * i + 4])

    operands, in_specs, out_specs, out_shape = [], [], [], []
    for w, m, v, parts, own in groups:
        r, cdim = w.shape
        tr = r // steps
        firsts = first_tiles(w, parts)

        def tile_of(k, t, firsts=firsts, parts=parts, tr=tr):
            return jnp.clip(t - firsts[k], 0, parts[k].shape[1] // tr - 1)

        blk = pl.BlockSpec((tr, cdim), lambda t, *_: (t, 0))
        operands += [w, m, v] + parts + own
        in_specs += [blk, blk, blk]
        in_specs += [pl.BlockSpec((p.shape[0], tr, cdim), lambda t, *_, k=k, tile_of=tile_of: (0, tile_of(k, t), 0)) for k, p in enumerate(parts)]
        in_specs += [pl.BlockSpec((1, tr, cdim), lambda t, slot_ref, k=k, tile_of=tile_of: (slot_ref[0], tile_of(k, t), 0)) for k in range(len(own))]
        out_specs += [blk] * 4
        out_shape += [jax.ShapeDtypeStruct((r, cdim), F32)] * 4
    outs, plan_outs = _call(
        body,
        name=name,
        prefetch=[own_slot] if use_slot else [],
        operands=operands,
        grid=(steps,),
        in_specs=in_specs,
        out_specs=out_specs,
        out_shape=out_shape,
        plans=plans,
    )
    return [outs[4 * i : 4 * i + 4] for i in range(len(groups))], plan_outs


def _chip_routes(n):
    x, y, c = _place()
    my_chip = 2 * x + y
    return [(a, (*chip, c), 2 * chip[0] + chip[1], my_chip, 2 * chip[0] + chip[1])
            for a in range(n) for chip in _other_chips(x, y)]


def _gather_routes(n):
    x, y, c = _place()
    flip = lambda v, f: 1 - v if f else v
    peers = [(flip(x, fx), flip(y, fy), flip(c, fc)) for fx in (0, 1) for fy in (0, 1) for fc in (0, 1) if fx + fy + fc]
    return [(0, p, 0, 4 * x + 2 * y + c, 4 * p[0] + 2 * p[1] + p[2]) for p in peers]


_chip_routes.copies_per_array = N_CHIP - 1
_gather_routes.copies_per_array = N_DEV - 1


def _split_start(name, routes, srcs, land_shapes):
    n = len(srcs)
    hbm = pl.BlockSpec(memory_space=pltpu.HBM)
    sem = pl.BlockSpec(memory_space=pltpu.SEMAPHORE)

    def body(*refs):
        src_refs, land_refs = refs[:n], refs[n : 2 * n]
        send_sems, recv_sems = refs[2 * n], refs[2 * n + 1]
        token = refs[-1]
        for k, (a, peer, src_slot, there, _) in enumerate(routes(n)):
            pltpu.make_async_remote_copy(
                src_ref=src_refs[a].at[src_slot], dst_ref=land_refs[a].at[there],
                send_sem=send_sems.at[k], recv_sem=recv_sems.at[k], device_id=peer, device_id_type=MESH).start()
        token[...] = jnp.zeros_like(token)

    in_hbm = [pltpu.with_memory_space_constraint(p, pltpu.HBM) for p in srcs]
    in_hbm += [pltpu.with_memory_space_constraint(lax.empty(shape, p.dtype), pltpu.HBM) for p, shape in zip(srcs, land_shapes)]
    n_copies = routes.copies_per_array * n
    res = pl.pallas_call(
        body,
        name=name,
        out_shape=[pltpu.SemaphoreType.DMA((n_copies,)), pltpu.SemaphoreType.DMA((n_copies,))]
        + [pltpu.HBM(p.shape, p.dtype) for p in srcs] + [pltpu.HBM(shape, p.dtype) for p, shape in zip(srcs, land_shapes)]
        + [jax.ShapeDtypeStruct((8, CHUNK), F32)],
        in_specs=[hbm] * (2 * n),
        out_specs=[sem, sem] + [hbm] * (2 * n) + [pl.BlockSpec(memory_space=pltpu.VMEM)],
        input_output_aliases={i: 2 + i for i in range(2 * n)},
        compiler_params=pltpu.CompilerParams(has_side_effects=pltpu.SideEffectType.DATAFLOW_SIDE_EFFECTING),
    )(*in_hbm)
    return res[:-1], res[-1]


def _split_wait(name, routes, in_flight, after):
    n = (len(in_flight) - 2) // 2
    hbm = pl.BlockSpec(memory_space=pltpu.HBM)
    sem = pl.BlockSpec(memory_space=pltpu.SEMAPHORE)
    send_sems, recv_sems, *bufs = in_flight

    def body(*refs):
        src_refs, land_refs = refs[:n], refs[n : 2 * n]
        send_ref, recv_ref = refs[2 * n], refs[2 * n + 1]
        for k, (a, peer, src_slot, _, here) in enumerate(routes(n)):
            cp = pltpu.make_async_remote_copy(
                src_ref=src_refs[a].at[src_slot], dst_ref=land_refs[a].at[here],
                send_sem=send_ref.at[k], recv_sem=recv_ref.at[k], device_id=peer, device_id_type=MESH)
            cp.wait_send()
            cp.wait_recv()

    res = pl.pallas_call(
        body,
        name=name,
        out_shape=[pltpu.HBM(b.shape, b.dtype) for b in bufs],
        in_specs=[hbm] * (2 * n) + [sem, sem] + [pl.BlockSpec(memory_space=pl.ANY)] * len(after),
        out_specs=[hbm] * (2 * n),
        input_output_aliases={i: i for i in range(2 * n)},
        compiler_params=pltpu.CompilerParams(has_side_effects=pltpu.SideEffectType.DATAFLOW_SIDE_EFFECTING),
    )(*bufs, send_sems, recv_sems, *after)
    return res[:n], res[n:]


def kernel(x, mem, norm_mix_g, w_in, conv_w, gm_ln_g, gm_ln_b, gm_ws, gm_bs, w_out, norm_x_g, norm_mem_g, w_q, w_kv, w_xo, norm_final_g, loss_target, m_norm_mix_g, m_w_in, m_conv_w, m_gm_ln_g, m_gm_ln_b, m_gm_ws, m_gm_bs, m_w_out, m_norm_x_g, m_norm_mem_g, m_w_q, m_w_kv, m_w_xo, m_norm_final_g, v_norm_mix_g, v_w_in, v_conv_w, v_gm_ln_g, v_gm_ln_b, v_gm_ws, v_gm_bs, v_w_out, v_norm_x_g, v_norm_mem_g, v_w_q, v_w_kv, v_w_xo, v_norm_final_g):
    dev = 4 * lax.axis_index("x") + 2 * lax.axis_index("y") + lax.axis_index("c")
    core = lax.axis_index("c").astype(jnp.int32).reshape(1)
    x2 = x[0]
    target = loss_target[0]
    pad_taps = lambda t: jnp.pad(t[0], ((0, 5), (0, 0)))

    big_names = ["w_in", "w_out", "w_q", "w_kv", "w_xo"]
    big_w = dict(w_in=w_in[0], w_out=w_out[0], w_q=w_q[0], w_kv=w_kv[0], w_xo=w_xo[0])
    big_m = dict(w_in=m_w_in[0], w_out=m_w_out[0], w_q=m_w_q[0], w_kv=m_w_kv[0], w_xo=m_w_xo[0])
    big_v = dict(w_in=v_w_in[0], w_out=v_w_out[0], w_q=v_w_q[0], w_kv=v_w_kv[0], w_xo=v_w_xo[0])
    shard_axis = dict(w_in=1, w_out=0, w_q=0, w_kv=1, w_xo=0)
    full_shape = dict(w_in=(D_MODEL, IN_DIM), w_out=(MIX_DIM, D_MODEL), w_q=(D_MODEL, D_MODEL),
                      w_kv=(D_MODEL, 2 * D_MODEL), w_xo=(D_MODEL, D_MODEL))
    shard_size = {k: big_w[k].shape[shard_axis[k]] for k in big_names}
    others = big_names[1:]
    gather_spec = lambda k: (full_shape[k], shard_axis[k], shard_size[k])
    reduce_spec = lambda k: (shard_axis[k], shard_size[k])
    ws = gm_ws[0]
    bs_b = jnp.broadcast_to(gm_bs[0][:, :, None], (8, CHUNK, CHUNK))
    g_f = norm_final_g.reshape(1, D_MODEL)

    px, py = lax.axis_index("x"), lax.axis_index("y")
    chip_order = jnp.stack([2 * px + py, 2 * (1 - px) + py, 2 * px + 1 - py, 2 * (1 - px) + 1 - py]).astype(jnp.int32)
    (proj, h1t, w_in_full), ((w_out_full, conv_g),) = _inproj(
        x2, norm_mix_g, big_w["w_in"].astype(BF16), chip_order,
        plans=[_gather_plan([big_w["w_out"].astype(BF16), pad_taps(conv_w)[None]], [gather_spec("w_out"), ((N_DEV, 8, CHUNK), 0, 1)])])
    conv_full = conv_g.transpose(1, 0, 2).reshape(8, D_MODEL)
    attn_w = ["w_q", "w_kv", "w_xo"]
    cat, (gathered,) = _mixer_fwd(
        proj, conv_full, gm_ln_g, gm_ln_b, ws, bs_b,
        plans=[_gather_plan([big_w[k].astype(BF16) for k in attn_w], [gather_spec(k) for k in attn_w])])
    wf = dict(zip(attn_w, gathered), w_out=w_out_full)
    k, vv, mt = _kv_fwd(mem[0], norm_mem_g, wf["w_kv"])
    (dcat, actt, cot, dk, dvv, d_gf, d_gx, sq_err) = _attn(
        x2, cat, target, wf["w_out"], wf["w_q"], wf["w_xo"], k, vv, norm_x_g, g_f)

    sibling_plan = lambda names, grads: _sibling_plan([grads[k] for k in names], [reduce_spec(k) for k in names])
    early = ["w_kv", "w_out", "w_q", "w_xo"]
    big_g = {}
    big_g["w_kv"], d_gmem = _kv_bwd(mem[0], wf["w_kv"], mt, dk, dvv)
    dws, (got_kv,) = _dw_attn(actt, cot, plans=[sibling_plan(early[:1], big_g)])
    big_g.update(zip(early[1:], dws))
    pair_kv, (got_rest,) = _pair_sum("pair_sum_w_kv", [big_g["w_kv"]], got_kv, [reduce_spec("w_kv")], core,
                                     plans=[sibling_plan(early[1:], big_g)])
    pair_rest = _pair_sum("pair_sum_attn", [big_g[k] for k in early[1:]], got_rest, [reduce_spec(k) for k in early[1:]], core)
    (dproj, d_cw, d_lng, d_lnb, d_ws, d_bs), (parts_early,) = _mixer_bwd(
        proj, dcat, conv_full, gm_ln_g, gm_ln_b, ws, bs_b, plans=[_chip_plan(list(pair_kv) + list(pair_rest))])
    half = D_MODEL // 2
    in_spec = reduce_spec("w_in")
    dw_top, _ = _matmul_bf16("dw_in_top", h1t, dproj, rows=(0, half))
    dw_bot, (got_top, (ws_all,)) = _matmul_bf16(
        "dw_in_bot", h1t, dproj, rows=(half, half),
        plans=[_sibling_plan([dw_top], [in_spec]),
               _gather_plan([d_ws.reshape(1, 8 * CHUNK, CHUNK).astype(BF16)], [((N_DEV, 8 * CHUNK, CHUNK), 0, 1)])])
    (pair_top,), (got_bot,) = _pair_sum("pair_sum_w_in_top", [dw_top], got_top, [in_spec], core,
                                        plans=[_sibling_plan([dw_bot], [in_spec])])
    (pair_bot,) = _pair_sum("pair_sum_w_in_bot", [dw_bot], got_bot, [in_spec], core)
    chip_shapes = [p.shape for p in (pair_top, pair_bot)]
    chips_in_flight, token = _split_start("grad_chip_exchange_start", _chip_routes, [pair_top, pair_bot], chip_shapes)
    grad_x, d_gmix = _dh1(dproj, w_in_full, x2, cot, norm_mix_g, after=[token])
    vec_names = ["norm_mix_g", "gm_ln_g", "gm_ln_b", "gm_bs", "norm_x_g", "norm_mem_g", "norm_final_g"]
    vec_g = [d_gmix, d_lng, d_lnb, d_bs, d_gx, d_gmem, d_gf]
    small = jnp.concatenate(vec_g + [sq_err, d_cw], axis=0)[None]
    small_in_flight, small_token = _split_start("small_all_gather_start", _gather_routes, [small], [(N_DEV, 16, D_MODEL)])
    flat_ws = lambda t: t.reshape(8 * CHUNK, CHUNK)
    updated, _ = _adamw(
        "adamw_early",
        [(big_w[k], big_m[k], big_v[k], got_k) for k, got_k in zip(early, parts_early)]
        + [(flat_ws(ws), flat_ws(m_gm_ws), flat_ws(v_gm_ws), ws_all)], steps=STEPS_ADAMW // 2)
    big_out = dict(zip(early, updated[:4]))
    ws_out = updated[4]
    (pair_top, pair_bot), (land_top, land_bot) = _split_wait(
        "grad_chip_exchange_wait", _chip_routes, chips_in_flight, [small_token] + [u[0] for u in updated])
    my_chip = (2 * px + py).astype(jnp.int32).reshape(1)
    (big_out["w_in"],), _ = _adamw(
        "adamw_w_in", [(big_w["w_in"], big_m["w_in"], big_v["w_in"], [land_top, land_bot], [pair_top, pair_bot])],
        steps=STEPS_ADAMW, own_slot=my_chip)
    (small,), (small_all,) = _split_wait("small_all_gather_wait", _gather_routes, small_in_flight, [big_out["w_in"][0]])
    row = lambda t: t.reshape(1, D_MODEL)
    vec_out, conv_out, loss_row = _update_small(
        dev.astype(jnp.int32).reshape(1),
        [norm_mix_g, gm_ln_g, gm_ln_b, row(gm_bs), norm_x_g, norm_mem_g, row(norm_final_g)],
        [m_norm_mix_g, m_gm_ln_g, m_gm_ln_b, row(m_gm_bs), m_norm_x_g, m_norm_mem_g, row(m_norm_final_g)],
        [v_norm_mix_g, v_gm_ln_g, v_gm_ln_b, row(v_gm_bs), v_norm_x_g, v_norm_mem_g, row(v_norm_final_g)],
        *(t.transpose(1, 0, 2) for t in (conv_w, m_conv_w, v_conv_w)), small, small_all)
    loss = loss_row[0, 0]

    def result(name, which):
        if name in big_out:
            return big_out[name][which][None]
        if name == "conv_w":
            return conv_out[which].transpose(1, 0, 2)
        if name == "gm_ws":
            return ws_out[which].reshape(1, 8, CHUNK, CHUNK)
        r = vec_out[vec_names.index(name)][which]
        if name == "gm_bs":
            return r.reshape(1, 8, CHUNK)
        if name == "norm_final_g":
            return r.reshape(D_MODEL)
        return r

    weights = ["norm_mix_g", "w_in", "conv_w", "gm_ln_g", "gm_ln_b", "gm_ws", "gm_bs", "w_out", "norm_x_g", "norm_mem_g", "w_q", "w_kv", "w_xo", "norm_final_g"]
    outs = [loss, grad_x[None]]
    for which in range(4):
        outs += [result(name, which) for name in weights]
    return tuple(outs)
```

```python
import math

import jax
import jax.numpy as jnp
from jax import lax
from jax.experimental import pallas as pl
from jax.experimental.pallas import tpu as pltpu

F32 = jnp.float32
BF16 = jnp.bfloat16
MESH = pl.DeviceIdType.MESH

D_MODEL = 1024
N_DEV = 8
N_CHIP = 4
CHUNK = 128
GROUPS = 7
IN_DIM = GROUPS * D_MODEL
SHARD_IN = IN_DIM // N_DEV
MIX_DIM = 2 * D_MODEL
X_HEADS = 4
X_HEAD_DIM = D_MODEL // X_HEADS
MEM_LEN = 256
EPS = 1e-6
GELU_K0 = math.sqrt(2.0 / math.pi)
GELU_K1 = 0.044715

ADAM_LR = 0.001
ADAM_B1 = 0.9
ADAM_B2 = 0.999
ADAM_EPS = 1e-08
ADAM_WD = 0.01
ADAM_STEP = 10

TM_INPROJ = 512
TM_ATTN = 512
SUB_ATTN = 256
TM_DH1 = 512
TM_DW = 512
TK_DW = 4096
BARRIER_ID = {"sibling": 1, "chips": 2, "all": 3}
STEPS_ADAMW = 4


def _dot(a, b):
    return lax.dot_general(a, b, (((1,), (0,)), ((), ())), preferred_element_type=F32)


def _dot_nt(a, b):
    return lax.dot_general(a, b, (((1,), (1,)), ((), ())), preferred_element_type=F32)


def _rows8(v):
    return jnp.sum(v.reshape(v.shape[0] // 8, 8, v.shape[1]), axis=0)


def _gelu(x, with_grad=True):
    x2 = x * x
    t = jnp.tanh(x * (GELU_K0 + (GELU_K0 * GELU_K1) * x2))
    half = 0.5 + 0.5 * t
    g = x * half
    if not with_grad:
        return g
    return g, half + ((0.5 * x) * (1.0 - t * t)) * (GELU_K0 + (3.0 * GELU_K0 * GELU_K1) * x2)


def _silu(z, with_grad=True):
    s = jax.nn.sigmoid(z)
    zs = z * s
    return (zs, s + zs * (1.0 - s)) if with_grad else zs


def _rms(v):
    r = lax.rsqrt(jnp.mean(v * v, axis=-1, keepdims=True) + EPS)
    return v * r, r


def _rms_bwd(dy_g, vh, r):
    return r * (dy_g - vh * jnp.mean(dy_g * vh, axis=-1, keepdims=True))


def _place():
    return lax.axis_index("x"), lax.axis_index("y"), lax.axis_index("c")


def _other_chips(x, y):
    return [(1 - x, y), (x, 1 - y), (1 - x, 1 - y)]


def _pow2_divisor(n):
    return n & (-n)


def _shard_view(ref, axis, size, d):
    start = d * size
    align = _pow2_divisor(size)
    if align > 1:
        start = pl.multiple_of(start, align)
    idx = [slice(None)] * len(ref.shape)
    idx[axis] = pl.ds(start, size)
    return ref.at[tuple(idx)]


class _Plan:
    def __init__(self, ins, out_shape, sems, start, finish, mid=None, aliases=None, peers="all"):
        self.ins, self.out_shape, self.sems = list(ins), list(out_shape), list(sems)
        self.start, self.mid, self.finish = start, mid, finish
        self.aliases = dict(aliases or {})
        self.peers = peers


def _gather_plan(shards, specs):
    n = len(shards)
    n_copy = 7

    def copies(ins, outs, sems):
        send_sems, recv_sems, local_sems = sems
        x, y, c = _place()
        me, sibling = (x, y, c), (x, y, 1 - c)
        chips = _other_chips(x, y)

        def block(a, p):
            _, axis, size = specs[a]
            return _shard_view(outs[a], axis, size, 4 * p[0] + 2 * p[1] + p[2])

        def copy(a, k, p, to, src=None):
            return pltpu.make_async_remote_copy(
                src_ref=block(a, p) if src is None else src,
                dst_ref=block(a, p),
                send_sem=send_sems.at[a * n_copy + k],
                recv_sem=recv_sems.at[a * n_copy + k],
                device_id=to,
                device_id_type=MESH,
            )

        def mine():
            return [pltpu.make_async_copy(ins[a], block(a, me), local_sems.at[a]) for a in range(n)]

        def first():
            return [cp for a in range(n) for cp in
                    [copy(a, 0, me, sibling, src=ins[a])] + [copy(a, 1 + j, me, (*chip, c), src=ins[a]) for j, chip in enumerate(chips)]]

        def landed():
            return [copy(a, 1 + j, (*chip, c), me) for j, chip in enumerate(chips) for a in range(n)]

        def passed():
            return [copy(a, 4 + j, (*chip, c), sibling) for j, chip in enumerate(chips) for a in range(n)]

        def last():
            return [cp for a in range(n) for cp in
                    [copy(a, 0, sibling, me)] + [copy(a, 4 + j, (*chip, 1 - c), me) for j, chip in enumerate(chips)]]

        return mine, first, landed, passed, last

    def start(ins, outs, sems):
        mine, first, _, _, _ = copies(ins, outs, sems)
        for cp in mine() + first():
            cp.start()

    def mid(ins, outs, sems):
        _, _, landed, passed, _ = copies(ins, outs, sems)
        for got, fwd in zip(landed(), passed()):
            got.wait_recv()
            fwd.start()

    def finish(ins, outs, sems):
        mine, first, _, passed, last = copies(ins, outs, sems)
        for cp in last():
            cp.wait_recv()
        for cp in first() + passed():
            cp.wait_send()
        for cp in mine():
            cp.wait()

    return _Plan(
        shards,
        [jax.ShapeDtypeStruct(full, s.dtype) for s, (full, _, _) in zip(shards, specs)],
        [pltpu.SemaphoreType.DMA((n * n_copy,)), pltpu.SemaphoreType.DMA((n * n_copy,)), pltpu.SemaphoreType.DMA((n,))],
        start, finish, mid,
    )


def _sibling_plan(grads, specs):
    n = len(grads)

    def shard_shape(a):
        axis, size = specs[a]
        shp = list(grads[a].shape)
        shp[axis] = size
        return tuple(shp)

    def copies(ins, outs, sems):
        send_sems, recv_sems = sems
        x, y, c = _place()
        return [
            pltpu.make_async_remote_copy(
                src_ref=_shard_view(ins[a], specs[a][0], specs[a][1], 2 * q + (1 - c)),
                dst_ref=outs[a].at[q],
                send_sem=send_sems.at[a * N_CHIP + q],
                recv_sem=recv_sems.at[a * N_CHIP + q],
                device_id=(x, y, 1 - c),
                device_id_type=MESH,
            )
            for a in range(n) for q in range(N_CHIP)
        ]

    def start(ins, outs, sems):
        for cp in copies(ins, outs, sems):
            cp.start()

    def finish(ins, outs, sems):
        for cp in copies(ins, outs, sems):
            cp.wait()

    return _Plan(
        grads,
        [jax.ShapeDtypeStruct((N_CHIP, *shard_shape(a)), grads[a].dtype) for a in range(n)],
        [pltpu.SemaphoreType.DMA((n * N_CHIP,)), pltpu.SemaphoreType.DMA((n * N_CHIP,))],
        start, finish, peers="sibling",
    )


def _chip_plan(parts, full_rows=None, row0=None, into=None):
    n = len(parts)
    full_rows = [p.shape[1] for p in parts] if full_rows is None else full_rows
    row0 = [0] * n if row0 is None else row0

    def copies(ins, outs, sems):
        send_sems, recv_sems, local_sems = sems
        x, y, c = _place()
        my_chip = 2 * x + y
        chips = _other_chips(x, y)

        def land(a, q):
            return outs[a].at[q, pl.ds(row0[a], parts[a].shape[1]), :]

        def mine():
            return [pltpu.make_async_copy(ins[a].at[my_chip], land(a, my_chip), local_sems.at[a]) for a in range(n)]

        def remote(sending):
            return [
                pltpu.make_async_remote_copy(
                    src_ref=ins[a].at[2 * chip[0] + chip[1]],
                    dst_ref=land(a, my_chip if sending else 2 * chip[0] + chip[1]),
                    send_sem=send_sems.at[a * 3 + j], recv_sem=recv_sems.at[a * 3 + j],
                    device_id=(*chip, c), device_id_type=MESH)
                for a in range(n) for j, chip in enumerate(chips)
            ]

        return mine, remote

    def start(ins, outs, sems):
        mine, remote = copies(ins, outs, sems)
        for cp in mine() + remote(True):
            cp.start()

    def finish(ins, outs, sems):
        mine, remote = copies(ins, outs, sems)
        for cp in remote(False):
            cp.wait_recv()
        for cp in remote(True):
            cp.wait_send()
        for cp in mine():
            cp.wait()

    return _Plan(
        list(parts) + (list(into) if into is not None else []),
        [jax.ShapeDtypeStruct((N_CHIP, full_rows[a], parts[a].shape[2]), parts[a].dtype) for a in range(n)],
        [pltpu.SemaphoreType.DMA((n * 3,)), pltpu.SemaphoreType.DMA((n * 3,)), pltpu.SemaphoreType.DMA((n,))],
        start, finish,
        aliases={n + a: a for a in range(n)} if into is not None else None, peers="chips",
    )


def _call(body, *, name, operands, out_shape, grid=(), in_specs=None, out_specs=None, scratch_shapes=(),
          plans=(), aliases=None, prefetch=(), start_after_body=False, pass_on_at=0.6, body_peers=None):
    operands, out_shape, scratch_shapes = list(operands), list(out_shape), list(scratch_shapes)
    n_pf, n_in, n_out, n_scr = len(prefetch), len(operands), len(out_shape), len(scratch_shapes)
    in_vmem = pl.BlockSpec(memory_space=pltpu.VMEM)
    in_hbm = pl.BlockSpec(memory_space=pl.ANY)
    in_specs = [in_vmem] * n_in if in_specs is None else list(in_specs)
    out_specs = [in_vmem] * n_out if out_specs is None else list(out_specs)
    io_alias = {n_pf + i: o for i, o in (aliases or {}).items()}
    pos_in, pos_out = n_pf + n_in, n_out
    for p in plans:
        for i, o in p.aliases.items():
            io_alias[pos_in + i] = pos_out + o
        pos_in += len(p.ins)
        pos_out += len(p.out_shape)
    steps = math.prod(grid) if grid else 1
    mid_step = min(steps - 1, int(steps * pass_on_at))
    patterns = {p.peers for p in plans} | ({body_peers} if body_peers else set())
    pattern = None if not patterns else patterns.pop() if len(patterns) == 1 else "all"

    def shake_hands():
        x, y, c = _place()
        peers = [(x, y, 1 - c)] if pattern in ("sibling", "all") else []
        peers += [(*chip, c) for chip in _other_chips(x, y)] if pattern in ("chips", "all") else []
        barrier = pltpu.get_barrier_semaphore()
        for peer in peers:
            pl.semaphore_signal(barrier, inc=1, device_id=peer, device_id_type=MESH)
        pl.semaphore_wait(barrier, len(peers))

    def wrapped(*refs):
        refs = list(refs)
        take = lambda k: [refs.pop(0) for _ in range(k)]
        pf = take(n_pf)
        ins, p_ins = take(n_in), [take(len(p.ins)) for p in plans]
        outs, p_outs = take(n_out), [take(len(p.out_shape)) for p in plans]
        scr, p_sems = take(n_scr), [take(len(p.sems)) for p in plans]
        step = 0
        for ax, g in enumerate(grid):
            step = step * g + pl.program_id(ax)

        def hook(kind, at):
            todo = [(getattr(p, kind), a, b, c) for p, a, b, c in zip(plans, p_ins, p_outs, p_sems) if getattr(p, kind)]

            def run():
                for fn, a, b, c in todo:
                    fn(a, b, c)

            if todo and grid:
                pl.when(step == at)(run)
            elif todo:
                run()

        if pattern and grid:
            pl.when(step == 0)(shake_hands)
        elif pattern:
            shake_hands()
        if not start_after_body:
            hook("start", 0)
        if body is not None:
            body(*pf, *ins, *outs, *scr)
        if start_after_body:
            hook("start", 0)
        hook("mid", mid_step)
        hook("finish", steps - 1)

    layout = dict(
        grid=grid,
        in_specs=in_specs + [in_hbm] * sum(len(p.ins) for p in plans),
        out_specs=out_specs + [in_hbm] * sum(len(p.out_shape) for p in plans),
        scratch_shapes=scratch_shapes + [s for p in plans for s in p.sems],
    )
    if n_pf:
        layout = dict(grid_spec=pltpu.PrefetchScalarGridSpec(num_scalar_prefetch=n_pf, **layout))
    res = pl.pallas_call(
        wrapped,
        name=name,
        out_shape=out_shape + [o for p in plans for o in p.out_shape],
        input_output_aliases=io_alias,
        compiler_params=pltpu.CompilerParams(
            dimension_semantics=("arbitrary",) * len(grid) if grid else None,
            collective_id=BARRIER_ID[pattern] if pattern else None),
        **layout,
    )(*prefetch, *operands, *[a for p in plans for a in p.ins])
    res = list(res)
    outs, plan_outs = res[:n_out], []
    pos = n_out
    for p in plans:
        plan_outs.append(res[pos : pos + len(p.out_shape)])
        pos += len(p.out_shape)
    return outs, plan_outs


def _inproj(x, g, w_shard, order, plans=()):
    s = x.shape[0]
    tm = min(TM_INPROJ, s)
    nt = s // tm
    wide = 2 * SHARD_IN
    fwd_at = nt // 2

    def body(order_ref, x_ref, g_ref, w_hbm, proj_ref, h1t_ref, wfull_hbm,
             w_scr, h1_scr, send_sems, recv_sems, local_sems):
        k, i = pl.program_id(0), pl.program_id(1)
        px, py, c = _place()
        me, sibling = (px, py, c), (px, py, 1 - c)
        chips = _other_chips(px, py)

        def half(slot, core):
            return w_scr.at[slot, :, pl.ds(pl.multiple_of(core * SHARD_IN, 128), SHARD_IN)]

        def copy(n, slot, core, to, src=None):
            return pltpu.make_async_remote_copy(
                src_ref=half(slot, core) if src is None else src, dst_ref=half(slot, core),
                send_sem=send_sems.at[n], recv_sem=recv_sems.at[n], device_id=to, device_id_type=MESH)

        def mine():
            return pltpu.make_async_copy(w_hbm, half(0, c), local_sems.at[0])

        def first():
            return [copy(0, 0, c, sibling, src=w_hbm)] + [copy(1 + j, 1 + j, c, (*chips[j], c), src=w_hbm) for j in range(2)]

        def relay():
            to = (c * px + (1 - c) * (1 - px), c * (1 - py) + (1 - c) * py, c)
            return pltpu.make_async_remote_copy(
                src_ref=half(2 - c, c), dst_ref=half(3, c), send_sem=send_sems.at[3], recv_sem=recv_sems.at[3],
                device_id=to, device_id_type=MESH)

        def save(slot):
            cols = pl.ds(pl.multiple_of(order_ref[slot] * wide, 128), wide)
            return pltpu.make_async_copy(w_scr.at[slot], wfull_hbm.at[:, cols], local_sems.at[1 + slot])

        @pl.when((k == 0) & (i == 0))
        def _():
            mine().start()
            for cp in first():
                cp.start()
            mine().wait()
            copy(0, 0, 1 - c, me).wait_recv()

        for j in range(3):
            @pl.when((k == j + 1) & (i == 0))
            def _():
                copy(4 + j, 1 + j, 1 - c, me).wait_recv()

        @pl.when((k == 3) & (i == 0))
        def _():
            for slot in range(N_CHIP):
                save(slot).start()

        @pl.when(k == 0)
        def _():
            xh, _ = _rms(x_ref[...])
            h = xh * g_ref[...]
            h1t_ref[...] = h.T.astype(BF16)
            h1_scr[pl.ds(pl.multiple_of(i * tm, tm), tm), :] = h.astype(BF16)

        proj_ref[...] = _dot(h1_scr[pl.ds(pl.multiple_of(i * tm, tm), tm), :], w_scr[k]).astype(BF16)

        for j in range(3):
            @pl.when((k == 0) & (i == nt - 1) if j < 2 else (k == 2) & (i == fwd_at))
            def _():
                copy(1 + j, 1 + j, c, me).wait_recv()
                copy(4 + j, 1 + j, c, sibling).start()
                if j < 2:
                    pl.when(c == 1 - j)(lambda: relay().start())

        @pl.when((k == 3) & (i == nt - 1))
        def _():
            for cp in first() + [relay()] + [copy(4 + j, 1 + j, c, sibling) for j in range(3)]:
                cp.wait_send()
            for slot in range(N_CHIP):
                save(slot).wait()

    once = lambda k, i, order_ref: (jnp.where(k == 0, i, nt - 1), 0)
    return _call(
        body,
        name="inproj",
        prefetch=[order],
        operands=[x, g, w_shard],
        grid=(N_CHIP, nt),
        in_specs=[
            pl.BlockSpec((tm, D_MODEL), once),
            pl.BlockSpec((1, D_MODEL), lambda k, i, order_ref: (0, 0)),
            pl.BlockSpec(memory_space=pl.ANY),
        ],
        out_specs=[
            pl.BlockSpec((tm, wide), lambda k, i, order_ref: (i, order_ref[k])),
            pl.BlockSpec((D_MODEL, tm), lambda k, i, order_ref: (0, jnp.where(k == 0, i, nt - 1))),
            pl.BlockSpec(memory_space=pl.ANY),
        ],
        out_shape=[
            jax.ShapeDtypeStruct((s, IN_DIM), BF16),
            jax.ShapeDtypeStruct((D_MODEL, s), BF16),
            jax.ShapeDtypeStruct((D_MODEL, IN_DIM), BF16),
        ],
        scratch_shapes=[
            pltpu.VMEM((N_CHIP, D_MODEL, wide), BF16),
            pltpu.VMEM((s, D_MODEL), BF16),
            pltpu.SemaphoreType.DMA((7,)),
            pltpu.SemaphoreType.DMA((7,)),
            pltpu.SemaphoreType.DMA((1 + N_CHIP,)),
        ],
        plans=plans,
        start_after_body=True,
        pass_on_at=0.8,
        body_peers="all",
    )


def _rows_from_above(v, above):
    row = lax.broadcasted_iota(jnp.int32, (8, v.shape[1]), 0)
    r1, r2 = pltpu.roll(v, 1, 0), pltpu.roll(v, 2, 0)
    top1 = jnp.where(row == 0, above[7:8], r1[0:8])
    top2 = jnp.where(row == 0, above[6:7], jnp.where(row == 1, above[7:8], r2[0:8]))
    return jnp.concatenate([top1, r1[8:]], axis=0), jnp.concatenate([top2, r2[8:]], axis=0)


def _rows_from_below(v, below):
    n = v.shape[0]
    row = lax.broadcasted_iota(jnp.int32, (8, v.shape[1]), 0)
    r1, r2 = pltpu.roll(v, n - 1, 0), pltpu.roll(v, n - 2, 0)
    end1 = jnp.where(row == 7, below[0:1], r1[n - 8 :])
    end2 = jnp.where(row == 6, below[0:1], jnp.where(row == 7, below[1:2], r2[n - 8 :]))
    return jnp.concatenate([r1[: n - 8], end1], axis=0), jnp.concatenate([r2[: n - 8], end2], axis=0)


def _causal_mask():
    row = lax.broadcasted_iota(jnp.int32, (CHUNK, CHUNK), 0)
    col = lax.broadcasted_iota(jnp.int32, (CHUNK, CHUNK), 1)
    return row, row >= col


def _mixer_fwd(proj, conv_w, ln_g, ln_b, ws, bs_b, plans=()):
    s = proj.shape[0]
    nt = s // CHUNK

    def body(proj_ref, cw_ref, lng_ref, lnb_ref, ws_ref, bsb_ref, cat_ref, pcar_ref):
        @pl.when(pl.program_id(0) == 0)
        def _():
            pcar_ref[...] = jnp.zeros_like(pcar_ref)

        row, tril = _causal_mask()
        for j in range(8):
            cs = slice(CHUNK * j, CHUNK * (j + 1))

            def grp(k):
                return proj_ref[:, k * D_MODEL + CHUNK * j : k * D_MODEL + CHUNK * (j + 1)].astype(F32)

            w = cw_ref[:, cs]
            p = grp(1) * grp(2)
            p1, p2 = _rows_from_above(p, pcar_ref[:, cs])
            pcar_ref[:, cs] = p[CHUNK - 8 :]
            cv = w[0:1] * p2 + w[1:2] * p1 + w[2:3] * p
            sa = _silu(grp(3), with_grad=False)
            cat_ref[:, cs] = ((grp(0) * cv) * sa).astype(BF16)
            gu = _gelu(grp(4), with_grad=False)
            gv = _gelu(grp(5), with_grad=False)
            dv = gv - jnp.mean(gv, axis=-1, keepdims=True)
            vn = dv * lax.rsqrt(jnp.mean(dv * dv, axis=-1, keepdims=True) + EPS)
            vn = vn * lng_ref[:, cs] + lnb_ref[:, cs]
            wc = jnp.where(tril, ws_ref[j], 0.0).astype(BF16)
            sp = _dot(wc, vn.astype(BF16)) + bsb_ref[j]
            sb = _silu(grp(6), with_grad=False)
            cat_ref[:, D_MODEL + CHUNK * j : D_MODEL + CHUNK * (j + 1)] = ((gu * sp) * sb).astype(BF16)

    whole = lambda shape: pl.BlockSpec(shape, lambda i: (0,) * len(shape))
    (cat,), plan_outs = _call(
        body,
        name="mixer_fwd",
        operands=[proj, conv_w, ln_g, ln_b, ws, bs_b],
        grid=(nt,),
        in_specs=[
            pl.BlockSpec((CHUNK, IN_DIM), lambda i: (i, 0)),
            whole((8, D_MODEL)),
            whole((1, D_MODEL)),
            whole((1, D_MODEL)),
            whole((8, CHUNK, CHUNK)),
            whole((8, CHUNK, CHUNK)),
        ],
        out_specs=[pl.BlockSpec((CHUNK, MIX_DIM), lambda i: (i, 0))],
        out_shape=[jax.ShapeDtypeStruct((s, MIX_DIM), BF16)],
        scratch_shapes=[pltpu.VMEM((8, D_MODEL), F32)],
        plans=plans,
        pass_on_at=0.75,
    )
    return cat, plan_outs


def _mixer_bwd(proj, dcat, conv_w, ln_g, ln_b, ws, bs_b, plans=()):
    s = proj.shape[0]
    nt = s // CHUNK

    def body(proj_ref, halo_ref, dcat_ref, cw_ref, lng_ref, lnb_ref, ws_ref, bsb_ref,
             dproj_ref, dcw_ref, dlng_ref, dlnb_ref, dws_ref, dbs_ref,
             car_ref, acc_cw, acc_lng, acc_lnb, acc_bs, wc_scr, wct_scr):
        i = pl.program_id(0)
        tile = nt - 1 - i
        row, tril = _causal_mask()

        @pl.when(i == 0)
        def _():
            for j in range(8):
                wc = jnp.where(tril, ws_ref[j], 0.0)
                wc_scr[j] = wc.astype(BF16)
                wct_scr[j] = wc.T.astype(BF16)
            car_ref[...] = jnp.zeros_like(car_ref)
            acc_cw[...] = jnp.zeros_like(acc_cw)
            acc_lng[...] = jnp.zeros_like(acc_lng)
            acc_lnb[...] = jnp.zeros_like(acc_lnb)
            acc_bs[...] = jnp.zeros_like(acc_bs)
            dws_ref[...] = jnp.zeros_like(dws_ref)

        has_prev = jnp.where(tile > 0, 1.0, 0.0).astype(F32)
        for j in range(8):
            cs = slice(CHUNK * j, CHUNK * (j + 1))

            def col(k):
                return slice(k * D_MODEL + CHUNK * j, k * D_MODEL + CHUNK * (j + 1))

            def grp(k):
                return proj_ref[:, col(k)].astype(F32)

            w = cw_ref[:, cs]
            gb, gc, xa = grp(0), grp(1), grp(2)
            p = gc * xa
            pprev = halo_ref[:, col(1)].astype(F32) * halo_ref[:, col(2)].astype(F32) * has_prev
            p1, p2 = _rows_from_above(p, pprev[8:16])
            cv = w[0:1] * p2 + w[1:2] * p1 + w[2:3] * p
            za = grp(3)
            sa, dsa = _silu(za)
            da = dcat_ref[:, cs].astype(F32)
            da_gb = da * gb
            dproj_ref[:, col(0)] = ((da * sa) * cv).astype(BF16)
            dproj_ref[:, col(3)] = ((da_gb * cv) * dsa).astype(BF16)
            dcv = da_gb * sa
            d1, d2 = _rows_from_below(dcv, car_ref[:, cs])
            car_ref[:, cs] = dcv[0:8]
            dp = w[2:3] * dcv + w[1:2] * d1 + w[0:1] * d2
            dproj_ref[:, col(1)] = (dp * xa).astype(BF16)
            dproj_ref[:, col(2)] = (dp * gc).astype(BF16)
            acc_cw[0, :, cs] += _rows8(dcv * p2)
            acc_cw[1, :, cs] += _rows8(dcv * p1)
            acc_cw[2, :, cs] += _rows8(dcv * p)
            gu, dgu = _gelu(grp(4))
            gv, dgv = _gelu(grp(5))
            dv = gv - jnp.mean(gv, axis=-1, keepdims=True)
            rstd = lax.rsqrt(jnp.mean(dv * dv, axis=-1, keepdims=True) + EPS)
            vnh = dv * rstd
            lng = lng_ref[:, cs]
            vnb = (vnh * lng + lnb_ref[:, cs]).astype(BF16)
            sp = _dot(wc_scr[j], vnb) + bsb_ref[j]
            zb = grp(6)
            sb, dsb = _silu(zb)
            db = dcat_ref[:, D_MODEL + CHUNK * j : D_MODEL + CHUNK * (j + 1)].astype(F32)
            db_sp, db_gu = db * sp, db * gu
            dproj_ref[:, col(4)] = ((db_sp * sb) * dgu).astype(BF16)
            dproj_ref[:, col(6)] = ((db_sp * gu) * dsb).astype(BF16)
            dsp = db_gu * sb
            acc_bs[j] += dsp
            dspb = dsp.astype(BF16)
            dws_ref[j] += _dot_nt(dspb, vnb)
            dvn = _dot(wct_scr[j], dspb)
            acc_lnb[:, cs] += _rows8(dvn)
            acc_lng[:, cs] += _rows8(dvn * vnh)
            dvh = dvn * lng
            dgvv = rstd * (dvh - jnp.mean(dvh, axis=-1, keepdims=True) - vnh * jnp.mean(dvh * vnh, axis=-1, keepdims=True))
            dproj_ref[:, col(5)] = (dgvv * dgv).astype(BF16)

        @pl.when(i == nt - 1)
        def _():
            dlng_ref[...] = jnp.sum(acc_lng[...], axis=0, keepdims=True)
            dlnb_ref[...] = jnp.sum(acc_lnb[...], axis=0, keepdims=True)
            dcw_ref[...] = jnp.zeros_like(dcw_ref)
            for k in range(3):
                dcw_ref[k : k + 1, :] = jnp.sum(acc_cw[k], axis=0, keepdims=True)
            for j in range(8):
                dws_ref[j] = jnp.where(tril, dws_ref[j], 0.0)
                dbs_ref[:, CHUNK * j : CHUNK * (j + 1)] = jnp.sum(acc_bs[j].T, axis=0, keepdims=True)

    whole = lambda shape: pl.BlockSpec(shape, lambda i: (0,) * len(shape))
    halo_rows = 16
    per = CHUNK // halo_rows
    return _call(
        body,
        name="mixer_bwd",
        operands=[proj, proj, dcat, conv_w, ln_g, ln_b, ws, bs_b],
        plans=plans,
        grid=(nt,),
        in_specs=[
            pl.BlockSpec((CHUNK, IN_DIM), lambda i: (nt - 1 - i, 0)),
            pl.BlockSpec((halo_rows, IN_DIM), lambda i: (jnp.maximum((nt - 1 - i) * per - 1, 0), 0)),
            pl.BlockSpec((CHUNK, MIX_DIM), lambda i: (nt - 1 - i, 0)),
            whole((8, D_MODEL)),
            whole((1, D_MODEL)),
            whole((1, D_MODEL)),
            whole((8, CHUNK, CHUNK)),
            whole((8, CHUNK, CHUNK)),
        ],
        out_specs=[
            pl.BlockSpec((CHUNK, IN_DIM), lambda i: (nt - 1 - i, 0)),
            whole((8, D_MODEL)),
            whole((1, D_MODEL)),
            whole((1, D_MODEL)),
            whole((8, CHUNK, CHUNK)),
            whole((1, D_MODEL)),
        ],
        out_shape=[
            jax.ShapeDtypeStruct((s, IN_DIM), BF16),
            jax.ShapeDtypeStruct((8, D_MODEL), F32),
            jax.ShapeDtypeStruct((1, D_MODEL), F32),
            jax.ShapeDtypeStruct((1, D_MODEL), F32),
            jax.ShapeDtypeStruct((8, CHUNK, CHUNK), F32),
            jax.ShapeDtypeStruct((1, D_MODEL), F32),
        ],
        scratch_shapes=[
            pltpu.VMEM((8, D_MODEL), F32),
            pltpu.VMEM((3, 8, D_MODEL), F32),
            pltpu.VMEM((8, D_MODEL), F32),
            pltpu.VMEM((8, D_MODEL), F32),
            pltpu.VMEM((8, CHUNK, CHUNK), F32),
            pltpu.VMEM((8, CHUNK, CHUNK), BF16),
            pltpu.VMEM((8, CHUNK, CHUNK), BF16),
        ],
    )


def _kv_fwd(mem, g_mem, w_kv):
    def body(mem_ref, g_ref, w_ref, k_ref, vv_ref, mt_ref):
        mh, _ = _rms(mem_ref[...])
        m = mh * g_ref[...]
        kv = _dot(m.astype(BF16), w_ref[...])
        k_ref[...] = kv[:, :D_MODEL].astype(BF16)
        vv_ref[...] = kv[:, D_MODEL:].astype(BF16)
        mt_ref[...] = m.T.astype(BF16)

    return pl.pallas_call(
        body,
        name="kv_fwd",
        out_shape=[
            jax.ShapeDtypeStruct((MEM_LEN, D_MODEL), BF16),
            jax.ShapeDtypeStruct((MEM_LEN, D_MODEL), BF16),
            jax.ShapeDtypeStruct((D_MODEL, MEM_LEN), BF16),
        ],
    )(mem, g_mem, w_kv)


def _kv_bwd(mem, w_kv, mt, dk, dvv):
    def body(mem_ref, w_ref, mt_ref, dk_ref, dvv_ref, dw_ref, dg_ref):
        dkv = jnp.concatenate([dk_ref[...], dvv_ref[...]], axis=1).astype(BF16)
        dw_ref[...] = _dot(mt_ref[...], dkv).astype(BF16)
        dm = _dot_nt(dkv, w_ref[...])
        mh, _ = _rms(mem_ref[...])
        dg_ref[...] = jnp.sum(dm * mh, axis=0, keepdims=True)

    return pl.pallas_call(
        body,
        name="kv_bwd",
        out_shape=[jax.ShapeDtypeStruct((D_MODEL, 2 * D_MODEL), BF16), jax.ShapeDtypeStruct((1, D_MODEL), F32)],
    )(mem, w_kv, mt, dk, dvv)


def _attn(x, cat, target, w_out, w_q, w_xo, k, vv, g_x, g_f):
    s = x.shape[0]
    tm = min(TM_ATTN, s)
    sub = min(SUB_ATTN, tm)
    nt = s // tm
    scale = 1.0 / math.sqrt(X_HEAD_DIM)

    def body(x_ref, cat_ref, t_ref, wout_ref, wq_ref, wxo_ref, k_ref, vv_ref, gx_ref, gf_ref,
             dcat_ref, actt_ref, cot_ref,
             dk_ref, dvv_ref, dgf_ref, dgx_ref, loss_ref,
             p_scr, q_scr, o_scr, dq_scr, acc_gf, acc_gx, acc_loss):
        i = pl.program_id(0)

        @pl.when(i == 0)
        def _():
            dk_ref[...] = jnp.zeros_like(dk_ref)
            dvv_ref[...] = jnp.zeros_like(dvv_ref)
            acc_gf[...] = jnp.zeros_like(acc_gf)
            acc_gx[...] = jnp.zeros_like(acc_gx)
            acc_loss[...] = jnp.zeros_like(acc_loss)

        subs = [slice(u * sub, (u + 1) * sub) for u in range(tm // sub)]
        heads = [slice(X_HEAD_DIM * h, X_HEAD_DIM * (h + 1)) for h in range(X_HEADS)]
        gx, gf = gx_ref[...], gf_ref[...]
        x1, x1h, r2 = [], [], []
        for rs in subs:
            cat = cat_ref[rs, :]
            actt_ref[0:MIX_DIM, rs] = cat.astype(F32).T.astype(BF16)
            v = x_ref[rs, :] + _dot(cat, wout_ref[...])
            vh, r = _rms(v)
            h2 = vh * gx
            actt_ref[MIX_DIM : MIX_DIM + D_MODEL, rs] = h2.T.astype(BF16)
            q_scr[rs, :] = _dot(h2.astype(BF16), wq_ref[...]).astype(BF16)
            x1.append(v), x1h.append(vh), r2.append(r)
        for rs in subs:
            for hs in heads:
                sc = _dot_nt(q_scr[rs, hs], k_ref[:, hs]) * scale
                e = jnp.exp(sc - jnp.max(sc, axis=-1, keepdims=True))
                p = e / jnp.sum(e, axis=-1, keepdims=True)
                p_scr[rs, hs] = p
                o_scr[rs, hs] = _dot(p.astype(BF16), vv_ref[:, hs])
        dx2, sq, d_gf = [], None, None
        for u, rs in enumerate(subs):
            o = o_scr[rs, :]
            actt_ref[MIX_DIM + D_MODEL :, rs] = o.T.astype(BF16)
            x2 = x1[u] + _dot(o.astype(BF16), wxo_ref[...])
            x2h, r3 = _rms(x2)
            err = x2h * gf - t_ref[rs, :]
            dy = err * (1.0 / D_MODEL)
            sq = _rows8(err * err) if sq is None else sq + _rows8(err * err)
            d_gf = _rows8(dy * x2h) if d_gf is None else d_gf + _rows8(dy * x2h)
            dx2.append(_rms_bwd(dy * gf, x2h, r3))
        acc_loss[...] += sq
        acc_gf[...] += d_gf
        d_vv, d_k = [None] * X_HEADS, [None] * X_HEADS
        for u, rs in enumerate(subs):
            dx2b = dx2[u].astype(BF16)
            cot_ref[2, rs, :] = dx2b
            dob = _dot_nt(dx2b, wxo_ref[...]).astype(BF16)
            for h, hs in enumerate(heads):
                p = p_scr[rs, hs]
                dp = _dot_nt(dob[:, hs], vv_ref[:, hs])
                ds = p * (dp - jnp.sum(dp * p, axis=-1, keepdims=True)) * scale
                dq_scr[rs, hs] = _dot(ds.astype(BF16), k_ref[:, hs]).astype(BF16)
                dvv_h = _dot(p.T.astype(BF16), dob[:, hs])
                dk_h = _dot(ds.T.astype(BF16), q_scr[rs, hs])
                d_vv[h] = dvv_h if d_vv[h] is None else d_vv[h] + dvv_h
                d_k[h] = dk_h if d_k[h] is None else d_k[h] + dk_h
        for h, hs in enumerate(heads):
            dvv_ref[:, hs] += d_vv[h]
            dk_ref[:, hs] += d_k[h]
        d_gx = None
        for u, rs in enumerate(subs):
            dqb = dq_scr[rs, :]
            cot_ref[1, rs, :] = dqb
            dh2 = _dot_nt(dqb, wq_ref[...])
            d_gx = _rows8(dh2 * x1h[u]) if d_gx is None else d_gx + _rows8(dh2 * x1h[u])
            dx1 = dx2[u] + _rms_bwd(dh2 * gx, x1h[u], r2[u])
            dx1b = dx1.astype(BF16)
            cot_ref[0, rs, :] = dx1b
            dcat_ref[rs, :] = _dot_nt(dx1b, wout_ref[...]).astype(BF16)
        acc_gx[...] += d_gx

        @pl.when(i == nt - 1)
        def _():
            dgf_ref[...] = jnp.sum(acc_gf[...], axis=0, keepdims=True)
            dgx_ref[...] = jnp.sum(acc_gx[...], axis=0, keepdims=True)
            loss_ref[...] = jnp.sum(acc_loss[...], axis=0, keepdims=True)

    whole = lambda shape: pl.BlockSpec(shape, lambda i: (0,) * len(shape))
    once = lambda shape: pl.BlockSpec(shape, lambda i: (0,) * len(shape), pipeline_mode=pl.Buffered(1))
    rows = lambda width: pl.BlockSpec((tm, width), lambda i: (i, 0))
    cols = lambda height: pl.BlockSpec((height, tm), lambda i: (0, i))
    vec = jax.ShapeDtypeStruct((1, D_MODEL), F32)
    return pl.pallas_call(
        body,
        name="attn",
        grid=(nt,),
        in_specs=[
            rows(D_MODEL), rows(MIX_DIM), rows(D_MODEL),
            once((MIX_DIM, D_MODEL)), once((D_MODEL, D_MODEL)), once((D_MODEL, D_MODEL)),
            once((MEM_LEN, D_MODEL)), once((MEM_LEN, D_MODEL)),
            whole((1, D_MODEL)), whole((1, D_MODEL)),
        ],
        out_specs=[
            rows(MIX_DIM), cols(2 * MIX_DIM),
            pl.BlockSpec((3, tm, D_MODEL), lambda i: (0, i, 0)),
            whole((MEM_LEN, D_MODEL)), whole((MEM_LEN, D_MODEL)),
            whole((1, D_MODEL)), whole((1, D_MODEL)), whole((1, D_MODEL)),
        ],
        out_shape=[
            jax.ShapeDtypeStruct((s, MIX_DIM), BF16),
            jax.ShapeDtypeStruct((2 * MIX_DIM, s), BF16),
            jax.ShapeDtypeStruct((3, s, D_MODEL), BF16),
            jax.ShapeDtypeStruct((MEM_LEN, D_MODEL), F32),
            jax.ShapeDtypeStruct((MEM_LEN, D_MODEL), F32),
            vec, vec, vec,
        ],
        scratch_shapes=[
            pltpu.VMEM((tm, D_MODEL), F32),
            pltpu.VMEM((tm, D_MODEL), BF16),
            pltpu.VMEM((tm, D_MODEL), F32),
            pltpu.VMEM((tm, D_MODEL), BF16),
            pltpu.VMEM((8, D_MODEL), F32),
            pltpu.VMEM((8, D_MODEL), F32),
            pltpu.VMEM((8, D_MODEL), F32),
        ],
        compiler_params=pltpu.CompilerParams(dimension_semantics=("arbitrary",)),
    )(x, cat, target, w_out, w_q, w_xo, k, vv, g_x, g_f)


def _matmul_bf16(name, a, b, rows=None, plans=()):
    first, m = (0, a.shape[0]) if rows is None else rows
    kk = a.shape[1]
    n = b.shape[1]
    tm, tn, tk = min(m, TM_DW), min(n, D_MODEL), min(kk, TK_DW)
    nk = kk // tk
    i0 = first // tm

    def body(a_ref, b_ref, o_ref, *acc):
        prod = _dot(a_ref[...], b_ref[...])
        if nk == 1:
            o_ref[...] = prod.astype(BF16)
            return
        (acc_ref,) = acc
        kid = pl.program_id(2)

        @pl.when(kid == 0)
        def _():
            acc_ref[...] = prod

        @pl.when(kid > 0)
        def _():
            acc_ref[...] += prod

        @pl.when(kid == nk - 1)
        def _():
            o_ref[...] = acc_ref[...].astype(BF16)

    (out,), plan_outs = _call(
        body,
        name=name,
        operands=[a, b],
        grid=(m // tm, n // tn, nk),
        in_specs=[pl.BlockSpec((tm, tk), lambda i, j, k: (i + i0, k)), pl.BlockSpec((tk, tn), lambda i, j, k: (k, j))],
        out_specs=[pl.BlockSpec((tm, tn), lambda i, j, k: (i, j))],
        out_shape=[jax.ShapeDtypeStruct((m, n), BF16)],
        scratch_shapes=[pltpu.VMEM((tm, tn), F32)] if nk > 1 else [],
        plans=plans,
    )
    return out, plan_outs


def _dw_attn(actt, cot, plans=()):
    s = actt.shape[1]
    tm = TM_DW
    counts = [MIX_DIM // tm, D_MODEL // tm, D_MODEL // tm]
    starts = [0, counts[0], counts[0] + counts[1]]
    which = lambda i: jnp.where(i < starts[1], 0, jnp.where(i < starts[2], 1, 2))

    def body(a_ref, b_ref, *outs):
        i = pl.program_id(0)
        prod = _dot(a_ref[...], b_ref[0]).astype(BF16)
        for m in range(3):
            @pl.when(which(i) == m)
            def _():
                outs[m][...] = prod

    out_specs = [pl.BlockSpec((tm, D_MODEL), lambda i, m=m: (jnp.clip(i - starts[m], 0, counts[m] - 1), 0)) for m in range(3)]
    out_shape = [jax.ShapeDtypeStruct((counts[m] * tm, D_MODEL), BF16) for m in range(3)]
    outs, plan_outs = _call(
        body,
        name="dw_attn",
        operands=[actt, cot],
        grid=(sum(counts),),
        in_specs=[pl.BlockSpec((tm, s), lambda i: (i, 0)), pl.BlockSpec((1, s, D_MODEL), lambda i: (which(i), 0, 0))],
        out_specs=out_specs,
        out_shape=out_shape,
        plans=plans,
    )
    return outs, plan_outs


def _dh1(dproj, w_in, x, dx1, g, after=()):
    s = x.shape[0]
    tm = min(TM_DH1, s)
    nt = s // tm

    def body(dp_ref, w_ref, x_ref, dx1_ref, g_ref, *rest):
        gx_ref, dg_ref, acc_g = rest[-3:]
        i = pl.program_id(0)

        @pl.when(i == 0)
        def _():
            acc_g[...] = jnp.zeros_like(acc_g)

        dh1 = _dot_nt(dp_ref[...], w_ref[...])
        xh, r = _rms(x_ref[...])
        acc_g[...] += _rows8(dh1 * xh)
        gx_ref[...] = dx1_ref[0].astype(F32) + _rms_bwd(dh1 * g_ref[...], xh, r)

        @pl.when(i == nt - 1)
        def _():
            dg_ref[...] = jnp.sum(acc_g[...], axis=0, keepdims=True)

    rows = pl.BlockSpec((tm, D_MODEL), lambda i: (i, 0))
    outs, _ = _call(
        body,
        name="dh1",
        operands=[dproj, w_in, x, dx1, g] + list(after),
        grid=(nt,),
        in_specs=[
            pl.BlockSpec((tm, IN_DIM), lambda i: (i, 0)),
            pl.BlockSpec((D_MODEL, IN_DIM), lambda i: (0, 0), pipeline_mode=pl.Buffered(1)),
            rows,
            pl.BlockSpec((1, tm, D_MODEL), lambda i: (0, i, 0)),
            pl.BlockSpec((1, D_MODEL), lambda i: (0, 0)),
        ] + [pl.BlockSpec(memory_space=pl.ANY)] * len(after),
        out_specs=[rows, pl.BlockSpec((1, D_MODEL), lambda i: (0, 0))],
        out_shape=[jax.ShapeDtypeStruct((s, D_MODEL), F32), jax.ShapeDtypeStruct((1, D_MODEL), F32)],
        scratch_shapes=[pltpu.VMEM((8, D_MODEL), F32)],
    )
    return outs


def _pair_sum(name, grads, gots, specs, core, plans=()):
    n = len(grads)
    shards = []
    for grad, (axis, size) in zip(grads, specs):
        shard = list(grad.shape)
        shard[axis] = size
        shards.append(tuple(shard))

    def body(core_ref, *refs):
        for g_ref, got_ref, o_ref in zip(refs[:n], refs[n : 2 * n], refs[2 * n :]):
            o_ref[0] = (g_ref[...].astype(F32) + got_ref[0].astype(F32)).astype(BF16)

    def own(shard, axis):
        if axis == 1:
            return pl.BlockSpec(shard, lambda q, core_ref: (0, 2 * q + core_ref[0]))
        return pl.BlockSpec(shard, lambda q, core_ref: (2 * q + core_ref[0], 0))

    slots = [pl.BlockSpec((1, *shard), lambda q, core_ref: (q, 0, 0)) for shard in shards]
    outs, plan_outs = _call(
        body,
        name=name,
        prefetch=[core],
        operands=list(grads) + list(gots),
        grid=(N_CHIP,),
        in_specs=[own(shard, axis) for shard, (axis, _) in zip(shards, specs)] + slots,
        out_specs=slots,
        out_shape=[jax.ShapeDtypeStruct((N_CHIP, *shard), BF16) for shard in shards],
        plans=plans,
    )
    return (outs, plan_outs) if plans else outs


def _adam_step(w, m, v, g):
    c1 = 1.0 / (1.0 - ADAM_B1**ADAM_STEP)
    c2 = 1.0 / (1.0 - ADAM_B2**ADAM_STEP)
    m_new = ADAM_B1 * m + (1.0 - ADAM_B1) * g
    v_new = ADAM_B2 * v + (1.0 - ADAM_B2) * (g * g)
    return -ADAM_LR * ((m_new * c1) / (jnp.sqrt(v_new * c2) + ADAM_EPS) + ADAM_WD * w), m_new, v_new


def _update_small(dev, vec_w, vec_m, vec_v, conv_w, conv_m, conv_v, small, landed):
    n_vec = len(vec_w)

    def body(dev_ref, *refs):
        refs = list(refs)
        take = lambda k: [refs.pop(0) for _ in range(k)]
        w_refs, m_refs, v_refs = take(n_vec), take(n_vec), take(n_vec)
        cw_ref, cm_ref, cv_ref, small_ref, land_ref, small_conv_ref, land_conv_ref = take(7)
        vec_outs, conv_outs, (loss_ref,) = take(4 * n_vec), take(4), take(1)

        def total(own_ref, others_ref):
            acc = None
            for d in range(N_DEV):
                part = jnp.where(dev_ref[0] == d, own_ref[0], others_ref[d])
                acc = part if acc is None else acc + part
            return acc

        g_all = total(small_ref, land_ref)
        for i in range(n_vec):
            g = g_all[i : i + 1]
            delta, m_new, v_new = _adam_step(w_refs[i][...], m_refs[i][...], v_refs[i][...], g)
            for ref, val in zip(vec_outs[4 * i : 4 * i + 4], (g, delta, m_new, v_new)):
                ref[...] = val
        g_conv = total(small_conv_ref, land_conv_ref)
        for tap in range(3):
            g = g_conv[tap : tap + 1]
            delta, m_new, v_new = _adam_step(cw_ref[tap], cm_ref[tap], cv_ref[tap], g)
            for ref, val in zip(conv_outs, (g, delta, m_new, v_new)):
                ref[tap] = val
        loss_ref[...] = jnp.broadcast_to(0.5 * jnp.sum(g_all[7:8], axis=-1, keepdims=True) / D_MODEL, loss_ref.shape)

    vec_spec = pl.BlockSpec((1, D_MODEL), lambda i, dev_ref: (0, 0))
    conv_spec = pl.BlockSpec((3, 1, CHUNK), lambda i, dev_ref: (0, 0, 0))
    vec_shape = jax.ShapeDtypeStruct((1, D_MODEL), F32)
    conv_shape = jax.ShapeDtypeStruct((3, 1, CHUNK), F32)
    outs, _ = _call(
        body,
        name="update_small",
        prefetch=[dev],
        operands=list(vec_w) + list(vec_m) + list(vec_v) + [conv_w, conv_m, conv_v, small, landed, small, landed],
        grid=(1,),
        in_specs=[vec_spec] * (3 * n_vec) + [conv_spec] * 3 + [
            pl.BlockSpec((1, 16, D_MODEL), lambda i, dev_ref: (0, 0, 0)),
            pl.BlockSpec((N_DEV, 16, D_MODEL), lambda i, dev_ref: (0, 0, 0)),
            pl.BlockSpec((1, 8, CHUNK), lambda i, dev_ref: (0, 1, dev_ref[0])),
            pl.BlockSpec((N_DEV, 8, CHUNK), lambda i, dev_ref: (0, 1, dev_ref[0])),
        ],
        out_specs=[vec_spec] * (4 * n_vec) + [conv_spec] * 4 + [pl.BlockSpec((1, CHUNK), lambda i, dev_ref: (0, 0))],
        out_shape=[vec_shape] * (4 * n_vec) + [conv_shape] * 4 + [jax.ShapeDtypeStruct((1, CHUNK), F32)],
    )
    return [outs[4 * i : 4 * i + 4] for i in range(n_vec)], outs[4 * n_vec : 4 * n_vec + 4], outs[-1]


def _adamw(name, groups, steps, own_slot=None, plans=()):
    lists = lambda p: list(p) if isinstance(p, (list, tuple)) else [p]
    groups = [(g[0], g[1], g[2], lists(g[3]), lists(g[4]) if len(g) > 4 else []) for g in groups]
    counts = [3 + len(g[3]) + len(g[4]) for g in groups]
    use_slot = own_slot is not None

    def first_tiles(w, parts):
        firsts, first = [], 0
        for p in parts:
            firsts.append(first)
            first += p.shape[1] // (w.shape[0] // steps)
        return firsts

    def update(t, slot_ref, firsts, w_ref, m_ref, v_ref, p_refs, o_refs, g_ref, d_ref, nm_ref, nv_ref):
        g = None
        for k, p_ref in enumerate(p_refs):
            gk = None
            for q in range(p_ref.shape[0]):
                part = p_ref[q].astype(F32)
                if o_refs:
                    part = jnp.where(slot_ref[0] == q, o_refs[k][0].astype(F32), part)
                gk = part if gk is None else gk + part
            g = gk if g is None else jnp.where(t >= firsts[k], gk, g)
        g_ref[...] = g
        d_ref[...], nm_ref[...], nv_ref[...] = _adam_step(w_ref[...], m_ref[...], v_ref[...], g)

    def body(*refs):
        refs = list(refs)
        slot_ref = refs.pop(0) if use_slot else None
        t = pl.program_id(0)
        ins, outs = refs[: sum(counts)], refs[sum(counts) :]
        for i, (w, _, _, parts, own) in enumerate(groups):
            mine = ins[sum(counts[:i]) : sum(counts[: i + 1])]
            update(t, slot_ref, first_tiles(w, parts), *mine[:3], mine[3 : 3 + len(parts)], mine[3 + len(parts) :],
                   *outs[4 * i : 4 * i + 4])

    operands, in_specs, out_specs, out_shape = [], [], [], []
    for w, m, v, parts, own in groups:
        r, cdim = w.shape
        tr = r // steps
        firsts = first_tiles(w, parts)

        def tile_of(k, t, firsts=firsts, parts=parts, tr=tr):
            return jnp.clip(t - firsts[k], 0, parts[k].shape[1] // tr - 1)

        blk = pl.BlockSpec((tr, cdim), lambda t, *_: (t, 0))
        operands += [w, m, v] + parts + own
        in_specs += [blk, blk, blk]
        in_specs += [pl.BlockSpec((p.shape[0], tr, cdim), lambda t, *_, k=k, tile_of=tile_of: (0, tile_of(k, t), 0)) for k, p in enumerate(parts)]
        in_specs += [pl.BlockSpec((1, tr, cdim), lambda t, slot_ref, k=k, tile_of=tile_of: (slot_ref[0], tile_of(k, t), 0)) for k in range(len(own))]
        out_specs += [blk] * 4
        out_shape += [jax.ShapeDtypeStruct((r, cdim), F32)] * 4
    outs, plan_outs = _call(
        body,
        name=name,
        prefetch=[own_slot] if use_slot else [],
        operands=operands,
        grid=(steps,),
        in_specs=in_specs,
        out_specs=out_specs,
        out_shape=out_shape,
        plans=plans,
    )
    return [outs[4 * i : 4 * i + 4] for i in range(len(groups))], plan_outs


def _chip_routes(n):
    x, y, c = _place()
    my_chip = 2 * x + y
    return [(a, (*chip, c), 2 * chip[0] + chip[1], my_chip, 2 * chip[0] + chip[1])
            for a in range(n) for chip in _other_chips(x, y)]


def _gather_routes(n):
    x, y, c = _place()
    flip = lambda v, f: 1 - v if f else v
    peers = [(flip(x, fx), flip(y, fy), flip(c, fc)) for fx in (0, 1) for fy in (0, 1) for fc in (0, 1) if fx + fy + fc]
    return [(0, p, 0, 4 * x + 2 * y + c, 4 * p[0] + 2 * p[1] + p[2]) for p in peers]


_chip_routes.copies_per_array = N_CHIP - 1
_gather_routes.copies_per_array = N_DEV - 1


def _split_start(name, routes, srcs, land_shapes):
    n = len(srcs)
    hbm = pl.BlockSpec(memory_space=pltpu.HBM)
    sem = pl.BlockSpec(memory_space=pltpu.SEMAPHORE)

    def body(*refs):
        src_refs, land_refs = refs[:n], refs[n : 2 * n]
        send_sems, recv_sems = refs[2 * n], refs[2 * n + 1]
        token = refs[-1]
        for k, (a, peer, src_slot, there, _) in enumerate(routes(n)):
            pltpu.make_async_remote_copy(
                src_ref=src_refs[a].at[src_slot], dst_ref=land_refs[a].at[there],
                send_sem=send_sems.at[k], recv_sem=recv_sems.at[k], device_id=peer, device_id_type=MESH).start()
        token[...] = jnp.zeros_like(token)

    in_hbm = [pltpu.with_memory_space_constraint(p, pltpu.HBM) for p in srcs]
    in_hbm += [pltpu.with_memory_space_constraint(lax.empty(shape, p.dtype), pltpu.HBM) for p, shape in zip(srcs, land_shapes)]
    n_copies = routes.copies_per_array * n
    res = pl.pallas_call(
        body,
        name=name,
        out_shape=[pltpu.SemaphoreType.DMA((n_copies,)), pltpu.SemaphoreType.DMA((n_copies,))]
        + [pltpu.HBM(p.shape, p.dtype) for p in srcs] + [pltpu.HBM(shape, p.dtype) for p, shape in zip(srcs, land_shapes)]
        + [jax.ShapeDtypeStruct((8, CHUNK), F32)],
        in_specs=[hbm] * (2 * n),
        out_specs=[sem, sem] + [hbm] * (2 * n) + [pl.BlockSpec(memory_space=pltpu.VMEM)],
        input_output_aliases={i: 2 + i for i in range(2 * n)},
        compiler_params=pltpu.CompilerParams(has_side_effects=pltpu.SideEffectType.DATAFLOW_SIDE_EFFECTING),
    )(*in_hbm)
    return res[:-1], res[-1]


def _split_wait(name, routes, in_flight, after):
    n = (len(in_flight) - 2) // 2
    hbm = pl.BlockSpec(memory_space=pltpu.HBM)
    sem = pl.BlockSpec(memory_space=pltpu.SEMAPHORE)
    send_sems, recv_sems, *bufs = in_flight

    def body(*refs):
        src_refs, land_refs = refs[:n], refs[n : 2 * n]
        send_ref, recv_ref = refs[2 * n], refs[2 * n + 1]
        for k, (a, peer, src_slot, _, here) in enumerate(routes(n)):
            cp = pltpu.make_async_remote_copy(
                src_ref=src_refs[a].at[src_slot], dst_ref=land_refs[a].at[here],
                send_sem=send_ref.at[k], recv_sem=recv_ref.at[k], device_id=peer, device_id_type=MESH)
            cp.wait_send()
            cp.wait_recv()

    res = pl.pallas_call(
        body,
        name=name,
        out_shape=[pltpu.HBM(b.shape, b.dtype) for b in bufs],
        in_specs=[hbm] * (2 * n) + [sem, sem] + [pl.BlockSpec(memory_space=pl.ANY)] * len(after),
        out_specs=[hbm] * (2 * n),
        input_output_aliases={i: i for i in range(2 * n)},
        compiler_params=pltpu.CompilerParams(has_side_effects=pltpu.SideEffectType.DATAFLOW_SIDE_EFFECTING),
    )(*bufs, send_sems, recv_sems, *after)
    return res[:n], res[n:]


def kernel(x, mem, norm_mix_g, w_in, conv_w, gm_ln_g, gm_ln_b, gm_ws, gm_bs, w_out, norm_x_g, norm_mem_g, w_q, w_kv, w_xo, norm_final_g, loss_target, m_norm_mix_g, m_w_in, m_conv_w, m_gm_ln_g, m_gm_ln_b, m_gm_ws, m_gm_bs, m_w_out, m_norm_x_g, m_norm_mem_g, m_w_q, m_w_kv, m_w_xo, m_norm_final_g, v_norm_mix_g, v_w_in, v_conv_w, v_gm_ln_g, v_gm_ln_b, v_gm_ws, v_gm_bs, v_w_out, v_norm_x_g, v_norm_mem_g, v_w_q, v_w_kv, v_w_xo, v_norm_final_g):
    dev = 4 * lax.axis_index("x") + 2 * lax.axis_index("y") + lax.axis_index("c")
    core = lax.axis_index("c").astype(jnp.int32).reshape(1)
    x2 = x[0]
    target = loss_target[0]
    pad_taps = lambda t: jnp.pad(t[0], ((0, 5), (0, 0)))

    big_names = ["w_in", "w_out", "w_q", "w_kv", "w_xo"]
    big_w = dict(w_in=w_in[0], w_out=w_out[0], w_q=w_q[0], w_kv=w_kv[0], w_xo=w_xo[0])
    big_m = dict(w_in=m_w_in[0], w_out=m_w_out[0], w_q=m_w_q[0], w_kv=m_w_kv[0], w_xo=m_w_xo[0])
    big_v = dict(w_in=v_w_in[0], w_out=v_w_out[0], w_q=v_w_q[0], w_kv=v_w_kv[0], w_xo=v_w_xo[0])
    shard_axis = dict(w_in=1, w_out=0, w_q=0, w_kv=1, w_xo=0)
    full_shape = dict(w_in=(D_MODEL, IN_DIM), w_out=(MIX_DIM, D_MODEL), w_q=(D_MODEL, D_MODEL),
                      w_kv=(D_MODEL, 2 * D_MODEL), w_xo=(D_MODEL, D_MODEL))
    shard_size = {k: big_w[k].shape[shard_axis[k]] for k in big_names}
    others = big_names[1:]
    gather_spec = lambda k: (full_shape[k], shard_axis[k], shard_size[k])
    reduce_spec = lambda k: (shard_axis[k], shard_size[k])
    ws = gm_ws[0]
    bs_b = jnp.broadcast_to(gm_bs[0][:, :, None], (8, CHUNK, CHUNK))
    g_f = norm_final_g.reshape(1, D_MODEL)

    px, py = lax.axis_index("x"), lax.axis_index("y")
    chip_order = jnp.stack([2 * px + py, 2 * (1 - px) + py, 2 * px + 1 - py, 2 * (1 - px) + 1 - py]).astype(jnp.int32)
    (proj, h1t, w_in_full), ((w_out_full, conv_g),) = _inproj(
        x2, norm_mix_g, big_w["w_in"].astype(BF16), chip_order,
        plans=[_gather_plan([big_w["w_out"].astype(BF16), pad_taps(conv_w)[None]], [gather_spec("w_out"), ((N_DEV, 8, CHUNK), 0, 1)])])
    conv_full = conv_g.transpose(1, 0, 2).reshape(8, D_MODEL)
    attn_w = ["w_q", "w_kv", "w_xo"]
    cat, (gathered,) = _mixer_fwd(
        proj, conv_full, gm_ln_g, gm_ln_b, ws, bs_b,
        plans=[_gather_plan([big_w[k].astype(BF16) for k in attn_w], [gather_spec(k) for k in attn_w])])
    wf = dict(zip(attn_w, gathered), w_out=w_out_full)
    k, vv, mt = _kv_fwd(mem[0], norm_mem_g, wf["w_kv"])
    (dcat, actt, cot, dk, dvv, d_gf, d_gx, sq_err) = _attn(
        x2, cat, target, wf["w_out"], wf["w_q"], wf["w_xo"], k, vv, norm_x_g, g_f)

    sibling_plan = lambda names, grads: _sibling_plan([grads[k] for k in names], [reduce_spec(k) for k in names])
    early = ["w_kv", "w_out", "w_q", "w_xo"]
    big_g = {}
    big_g["w_kv"], d_gmem = _kv_bwd(mem[0], wf["w_kv"], mt, dk, dvv)
    dws, (got_kv,) = _dw_attn(actt, cot, plans=[sibling_plan(early[:1], big_g)])
    big_g.update(zip(early[1:], dws))
    pair_kv, (got_rest,) = _pair_sum("pair_sum_w_kv", [big_g["w_kv"]], got_kv, [reduce_spec("w_kv")], core,
                                     plans=[sibling_plan(early[1:], big_g)])
    pair_rest = _pair_sum("pair_sum_attn", [big_g[k] for k in early[1:]], got_rest, [reduce_spec(k) for k in early[1:]], core)
    (dproj, d_cw, d_lng, d_lnb, d_ws, d_bs), (parts_early,) = _mixer_bwd(
        proj, dcat, conv_full, gm_ln_g, gm_ln_b, ws, bs_b, plans=[_chip_plan(list(pair_kv) + list(pair_rest))])
    half = D_MODEL // 2
    in_spec = reduce_spec("w_in")
    dw_top, _ = _matmul_bf16("dw_in_top", h1t, dproj, rows=(0, half))
    dw_bot, (got_top, (ws_all,)) = _matmul_bf16(
        "dw_in_bot", h1t, dproj, rows=(half, half),
        plans=[_sibling_plan([dw_top], [in_spec]),
               _gather_plan([d_ws.reshape(1, 8 * CHUNK, CHUNK).astype(BF16)], [((N_DEV, 8 * CHUNK, CHUNK), 0, 1)])])
    (pair_top,), (got_bot,) = _pair_sum("pair_sum_w_in_top", [dw_top], got_top, [in_spec], core,
                                        plans=[_sibling_plan([dw_bot], [in_spec])])
    (pair_bot,) = _pair_sum("pair_sum_w_in_bot", [dw_bot], got_bot, [in_spec], core)
    chip_shapes = [p.shape for p in (pair_top, pair_bot)]
    chips_in_flight, token = _split_start("grad_chip_exchange_start", _chip_routes, [pair_top, pair_bot], chip_shapes)
    grad_x, d_gmix = _dh1(dproj, w_in_full, x2, cot, norm_mix_g, after=[token])
    vec_names = ["norm_mix_g", "gm_ln_g", "gm_ln_b", "gm_bs", "norm_x_g", "norm_mem_g", "norm_final_g"]
    vec_g = [d_gmix, d_lng, d_lnb, d_bs, d_gx, d_gmem, d_gf]
    small = jnp.concatenate(vec_g + [sq_err, d_cw], axis=0)[None]
    small_in_flight, small_token = _split_start("small_all_gather_start", _gather_routes, [small], [(N_DEV, 16, D_MODEL)])
    flat_ws = lambda t: t.reshape(8 * CHUNK, CHUNK)
    updated, _ = _adamw(
        "adamw_early",
        [(big_w[k], big_m[k], big_v[k], got_k) for k, got_k in zip(early, parts_early)]
        + [(flat_ws(ws), flat_ws(m_gm_ws), flat_ws(v_gm_ws), ws_all)], steps=STEPS_ADAMW // 2)
    big_out = dict(zip(early, updated[:4]))
    ws_out = updated[4]
    (pair_top, pair_bot), (land_top, land_bot) = _split_wait(
        "grad_chip_exchange_wait", _chip_routes, chips_in_flight, [small_token] + [u[0] for u in updated])
    my_chip = (2 * px + py).astype(jnp.int32).reshape(1)
    (big_out["w_in"],), _ = _adamw(
        "adamw_w_in", [(big_w["w_in"], big_m["w_in"], big_v["w_in"], [land_top, land_bot], [pair_top, pair_bot])],
        steps=STEPS_ADAMW, own_slot=my_chip)
    (small,), (small_all,) = _split_wait("small_all_gather_wait", _gather_routes, small_in_flight, [big_out["w_in"][0]])
    row = lambda t: t.reshape(1, D_MODEL)
    vec_out, conv_out, loss_row = _update_small(
        dev.astype(jnp.int32).reshape(1),
        [norm_mix_g, gm_ln_g, gm_ln_b, row(gm_bs), norm_x_g, norm_mem_g, row(norm_final_g)],
        [m_norm_mix_g, m_gm_ln_g, m_gm_ln_b, row(m_gm_bs), m_norm_x_g, m_norm_mem_g, row(m_norm_final_g)],
        [v_norm_mix_g, v_gm_ln_g, v_gm_ln_b, row(v_gm_bs), v_norm_x_g, v_norm_mem_g, row(v_norm_final_g)],
        *(t.transpose(1, 0, 2) for t in (conv_w, m_conv_w, v_conv_w)), small, small_all)
    loss = loss_row[0, 0]

    def result(name, which):
        if name in big_out:
            return big_out[name][which][None]
        if name == "conv_w":
            return conv_out[which].transpose(1, 0, 2)
        if name == "gm_ws":
            return ws_out[which].reshape(1, 8, CHUNK, CHUNK)
        r = vec_out[vec_names.index(name)][which]
        if name == "gm_bs":
            return r.reshape(1, 8, CHUNK)
        if name == "norm_final_g":
            return r.reshape(D_MODEL)
        return r

    weights = ["norm_mix_g", "w_in", "conv_w", "gm_ln_g", "gm_ln_b", "gm_ws", "gm_bs", "w_out", "norm_x_g", "norm_mem_g", "w_q", "w_kv", "w_xo", "norm_final_g"]
    outs = [loss, grad_x[None]]
    for which in range(4):
        outs += [result(name, which) for name in weights]
    return tuple(outs)
```

```python
import math

import jax
import jax.numpy as jnp
from jax import lax
from jax.experimental import pallas as pl
from jax.experimental.pallas import tpu as pltpu

F32 = jnp.float32
BF16 = jnp.bfloat16
MESH = pl.DeviceIdType.MESH

D_MODEL = 1024
N_DEV = 8
N_CHIP = 4
CHUNK = 128
GROUPS = 7
IN_DIM = GROUPS * D_MODEL
SHARD_IN = IN_DIM // N_DEV
MIX_DIM = 2 * D_MODEL
X_HEADS = 4
X_HEAD_DIM = D_MODEL // X_HEADS
MEM_LEN = 256
EPS = 1e-6
GELU_K0 = math.sqrt(2.0 / math.pi)
GELU_K1 = 0.044715

ADAM_LR = 0.001
ADAM_B1 = 0.9
ADAM_B2 = 0.999
ADAM_EPS = 1e-08
ADAM_WD = 0.01
ADAM_STEP = 10

TM_INPROJ = 512
TM_ATTN = 512
SUB_ATTN = 256
TM_DH1 = 512
TM_DW = 512
TK_DW = 4096
BARRIER_ID = {"sibling": 1, "chips": 2, "all": 3}
STEPS_ADAMW = 4


def _dot(a, b):
    return lax.dot_general(a, b, (((1,), (0,)), ((), ())), preferred_element_type=F32)


def _dot_nt(a, b):
    return lax.dot_general(a, b, (((1,), (1,)), ((), ())), preferred_element_type=F32)


def _rows8(v):
    return jnp.sum(v.reshape(v.shape[0] // 8, 8, v.shape[1]), axis=0)


def _gelu(x, with_grad=True):
    x2 = x * x
    t = jnp.tanh(x * (GELU_K0 + (GELU_K0 * GELU_K1) * x2))
    half = 0.5 + 0.5 * t
    g = x * half
    if not with_grad:
        return g
    return g, half + ((0.5 * x) * (1.0 - t * t)) * (GELU_K0 + (3.0 * GELU_K0 * GELU_K1) * x2)


def _silu(z, with_grad=True):
    s = jax.nn.sigmoid(z)
    zs = z * s
    return (zs, s + zs * (1.0 - s)) if with_grad else zs


def _rms(v):
    r = lax.rsqrt(jnp.mean(v * v, axis=-1, keepdims=True) + EPS)
    return v * r, r


def _rms_bwd(dy_g, vh, r):
    return r * (dy_g - vh * jnp.mean(dy_g * vh, axis=-1, keepdims=True))


def _place():
    return lax.axis_index("x"), lax.axis_index("y"), lax.axis_index("c")


def _other_chips(x, y):
    return [(1 - x, y), (x, 1 - y), (1 - x, 1 - y)]


def _pow2_divisor(n):
    return n & (-n)


def _shard_view(ref, axis, size, d):
    start = d * size
    align = _pow2_divisor(size)
    if align > 1:
        start = pl.multiple_of(start, align)
    idx = [slice(None)] * len(ref.shape)
    idx[axis] = pl.ds(start, size)
    return ref.at[tuple(idx)]


class _Plan:
    def __init__(self, ins, out_shape, sems, start, finish, mid=None, aliases=None, peers="all"):
        self.ins, self.out_shape, self.sems = list(ins), list(out_shape), list(sems)
        self.start, self.mid, self.finish = start, mid, finish
        self.aliases = dict(aliases or {})
        self.peers = peers


def _gather_plan(shards, specs):
    n = len(shards)
    n_copy = 7

    def copies(ins, outs, sems):
        send_sems, recv_sems, local_sems = sems
        x, y, c = _place()
        me, sibling = (x, y, c), (x, y, 1 - c)
        chips = _other_chips(x, y)

        def block(a, p):
            _, axis, size = specs[a]
            return _shard_view(outs[a], axis, size, 4 * p[0] + 2 * p[1] + p[2])

        def copy(a, k, p, to, src=None):
            return pltpu.make_async_remote_copy(
                src_ref=block(a, p) if src is None else src,
                dst_ref=block(a, p),
                send_sem=send_sems.at[a * n_copy + k],
                recv_sem=recv_sems.at[a * n_copy + k],
                device_id=to,
                device_id_type=MESH,
            )

        def mine():
            return [pltpu.make_async_copy(ins[a], block(a, me), local_sems.at[a]) for a in range(n)]

        def first():
            return [cp for a in range(n) for cp in
                    [copy(a, 0, me, sibling, src=ins[a])] + [copy(a, 1 + j, me, (*chip, c), src=ins[a]) for j, chip in enumerate(chips)]]

        def landed():
            return [copy(a, 1 + j, (*chip, c), me) for j, chip in enumerate(chips) for a in range(n)]

        def passed():
            return [copy(a, 4 + j, (*chip, c), sibling) for j, chip in enumerate(chips) for a in range(n)]

        def last():
            return [cp for a in range(n) for cp in
                    [copy(a, 0, sibling, me)] + [copy(a, 4 + j, (*chip, 1 - c), me) for j, chip in enumerate(chips)]]

        return mine, first, landed, passed, last

    def start(ins, outs, sems):
        mine, first, _, _, _ = copies(ins, outs, sems)
        for cp in mine() + first():
            cp.start()

    def mid(ins, outs, sems):
        _, _, landed, passed, _ = copies(ins, outs, sems)
        for got, fwd in zip(landed(), passed()):
            got.wait_recv()
            fwd.start()

    def finish(ins, outs, sems):
        mine, first, _, passed, last = copies(ins, outs, sems)
        for cp in last():
            cp.wait_recv()
        for cp in first() + passed():
            cp.wait_send()
        for cp in mine():
            cp.wait()

    return _Plan(
        shards,
        [jax.ShapeDtypeStruct(full, s.dtype) for s, (full, _, _) in zip(shards, specs)],
        [pltpu.SemaphoreType.DMA((n * n_copy,)), pltpu.SemaphoreType.DMA((n * n_copy,)), pltpu.SemaphoreType.DMA((n,))],
        start, finish, mid,
    )


def _sibling_plan(grads, specs):
    n = len(grads)

    def shard_shape(a):
        axis, size = specs[a]
        shp = list(grads[a].shape)
        shp[axis] = size
        return tuple(shp)

    def copies(ins, outs, sems):
        send_sems, recv_sems = sems
        x, y, c = _place()
        return [
            pltpu.make_async_remote_copy(
                src_ref=_shard_view(ins[a], specs[a][0], specs[a][1], 2 * q + (1 - c)),
                dst_ref=outs[a].at[q],
                send_sem=send_sems.at[a * N_CHIP + q],
                recv_sem=recv_sems.at[a * N_CHIP + q],
                device_id=(x, y, 1 - c),
                device_id_type=MESH,
            )
            for a in range(n) for q in range(N_CHIP)
        ]

    def start(ins, outs, sems):
        for cp in copies(ins, outs, sems):
            cp.start()

    def finish(ins, outs, sems):
        for cp in copies(ins, outs, sems):
            cp.wait()

    return _Plan(
        grads,
        [jax.ShapeDtypeStruct((N_CHIP, *shard_shape(a)), grads[a].dtype) for a in range(n)],
        [pltpu.SemaphoreType.DMA((n * N_CHIP,)), pltpu.SemaphoreType.DMA((n * N_CHIP,))],
        start, finish, peers="sibling",
    )


def _chip_plan(parts, full_rows=None, row0=None, into=None):
    n = len(parts)
    full_rows = [p.shape[1] for p in parts] if full_rows is None else full_rows
    row0 = [0] * n if row0 is None else row0

    def copies(ins, outs, sems):
        send_sems, recv_sems, local_sems = sems
        x, y, c = _place()
        my_chip = 2 * x + y
        chips = _other_chips(x, y)

        def land(a, q):
            return outs[a].at[q, pl.ds(row0[a], parts[a].shape[1]), :]

        def mine():
            return [pltpu.make_async_copy(ins[a].at[my_chip], land(a, my_chip), local_sems.at[a]) for a in range(n)]

        def remote(sending):
            return [
                pltpu.make_async_remote_copy(
                    src_ref=ins[a].at[2 * chip[0] + chip[1]],
                    dst_ref=land(a, my_chip if sending else 2 * chip[0] + chip[1]),
                    send_sem=send_sems.at[a * 3 + j], recv_sem=recv_sems.at[a * 3 + j],
                    device_id=(*chip, c), device_id_type=MESH)
                for a in range(n) for j, chip in enumerate(chips)
            ]

        return mine, remote

    def start(ins, outs, sems):
        mine, remote = copies(ins, outs, sems)
        for cp in mine() + remote(True):
            cp.start()

    def finish(ins, outs, sems):
        mine, remote = copies(ins, outs, sems)
        for cp in remote(False):
            cp.wait_recv()
        for cp in remote(True):
            cp.wait_send()
        for cp in mine():
            cp.wait()

    return _Plan(
        list(parts) + (list(into) if into is not None else []),
        [jax.ShapeDtypeStruct((N_CHIP, full_rows[a], parts[a].shape[2]), parts[a].dtype) for a in range(n)],
        [pltpu.SemaphoreType.DMA((n * 3,)), pltpu.SemaphoreType.DMA((n * 3,)), pltpu.SemaphoreType.DMA((n,))],
        start, finish,
        aliases={n + a: a for a in range(n)} if into is not None else None, peers="chips",
    )


def _call(body, *, name, operands, out_shape, grid=(), in_specs=None, out_specs=None, scratch_shapes=(),
          plans=(), aliases=None, prefetch=(), start_after_body=False, pass_on_at=0.6, body_peers=None):
    operands, out_shape, scratch_shapes = list(operands), list(out_shape), list(scratch_shapes)
    n_pf, n_in, n_out, n_scr = len(prefetch), len(operands), len(out_shape), len(scratch_shapes)
    in_vmem = pl.BlockSpec(memory_space=pltpu.VMEM)
    in_hbm = pl.BlockSpec(memory_space=pl.ANY)
    in_specs = [in_vmem] * n_in if in_specs is None else list(in_specs)
    out_specs = [in_vmem] * n_out if out_specs is None else list(out_specs)
    io_alias = {n_pf + i: o for i, o in (aliases or {}).items()}
    pos_in, pos_out = n_pf + n_in, n_out
    for p in plans:
        for i, o in p.aliases.items():
            io_alias[pos_in + i] = pos_out + o
        pos_in += len(p.ins)
        pos_out += len(p.out_shape)
    steps = math.prod(grid) if grid else 1
    mid_step = min(steps - 1, int(steps * pass_on_at))
    patterns = {p.peers for p in plans} | ({body_peers} if body_peers else set())
    pattern = None if not patterns else patterns.pop() if len(patterns) == 1 else "all"

    def shake_hands():
        x, y, c = _place()
        peers = [(x, y, 1 - c)] if pattern in ("sibling", "all") else []
        peers += [(*chip, c) for chip in _other_chips(x, y)] if pattern in ("chips", "all") else []
        barrier = pltpu.get_barrier_semaphore()
        for peer in peers:
            pl.semaphore_signal(barrier, inc=1, device_id=peer, device_id_type=MESH)
        pl.semaphore_wait(barrier, len(peers))

    def wrapped(*refs):
        refs = list(refs)
        take = lambda k: [refs.pop(0) for _ in range(k)]
        pf = take(n_pf)
        ins, p_ins = take(n_in), [take(len(p.ins)) for p in plans]
        outs, p_outs = take(n_out), [take(len(p.out_shape)) for p in plans]
        scr, p_sems = take(n_scr), [take(len(p.sems)) for p in plans]
        step = 0
        for ax, g in enumerate(grid):
            step = step * g + pl.program_id(ax)

        def hook(kind, at):
            todo = [(getattr(p, kind), a, b, c) for p, a, b, c in zip(plans, p_ins, p_outs, p_sems) if getattr(p, kind)]

            def run():
                for fn, a, b, c in todo:
                    fn(a, b, c)

            if todo and grid:
                pl.when(step == at)(run)
            elif todo:
                run()

        if pattern and grid:
            pl.when(step == 0)(shake_hands)
        elif pattern:
            shake_hands()
        if not start_after_body:
            hook("start", 0)
        if body is not None:
            body(*pf, *ins, *outs, *scr)
        if start_after_body:
            hook("start", 0)
        hook("mid", mid_step)
        hook("finish", steps - 1)

    layout = dict(
        grid=grid,
        in_specs=in_specs + [in_hbm] * sum(len(p.ins) for p in plans),
        out_specs=out_specs + [in_hbm] * sum(len(p.out_shape) for p in plans),
        scratch_shapes=scratch_shapes + [s for p in plans for s in p.sems],
    )
    if n_pf:
        layout = dict(grid_spec=pltpu.PrefetchScalarGridSpec(num_scalar_prefetch=n_pf, **layout))
    res = pl.pallas_call(
        wrapped,
        name=name,
        out_shape=out_shape + [o for p in plans for o in p.out_shape],
        input_output_aliases=io_alias,
        compiler_params=pltpu.CompilerParams(
            dimension_semantics=("arbitrary",) * len(grid) if grid else None,
            collective_id=BARRIER_ID[pattern] if pattern else None),
        **layout,
    )(*prefetch, *operands, *[a for p in plans for a in p.ins])
    res = list(res)
    outs, plan_outs = res[:n_out], []
    pos = n_out
    for p in plans:
        plan_outs.append(res[pos : pos + len(p.out_shape)])
        pos += len(p.out_shape)
    return outs, plan_outs


def _inproj(x, g, w_shard, order, plans=()):
    s = x.shape[0]
    tm = min(TM_INPROJ, s)
    nt = s // tm
    wide = 2 * SHARD_IN
    fwd_at = nt // 2

    def body(order_ref, x_ref, g_ref, w_hbm, proj_ref, h1t_ref, wfull_hbm,
             w_scr, h1_scr, send_sems, recv_sems, local_sems):
        k, i = pl.program_id(0), pl.program_id(1)
        px, py, c = _place()
        me, sibling = (px, py, c), (px, py, 1 - c)
        chips = _other_chips(px, py)

        def half(slot, core):
            return w_scr.at[slot, :, pl.ds(pl.multiple_of(core * SHARD_IN, 128), SHARD_IN)]

        def copy(n, slot, core, to, src=None):
            return pltpu.make_async_remote_copy(
                src_ref=half(slot, core) if src is None else src, dst_ref=half(slot, core),
                send_sem=send_sems.at[n], recv_sem=recv_sems.at[n], device_id=to, device_id_type=MESH)

        def mine():
            return pltpu.make_async_copy(w_hbm, half(0, c), local_sems.at[0])

        def first():
            return [copy(0, 0, c, sibling, src=w_hbm)] + [copy(1 + j, 1 + j, c, (*chips[j], c), src=w_hbm) for j in range(2)]

        def relay():
            to = (c * px + (1 - c) * (1 - px), c * (1 - py) + (1 - c) * py, c)
            return pltpu.make_async_remote_copy(
                src_ref=half(2 - c, c), dst_ref=half(3, c), send_sem=send_sems.at[3], recv_sem=recv_sems.at[3],
                device_id=to, device_id_type=MESH)

        def save(slot):
            cols = pl.ds(pl.multiple_of(order_ref[slot] * wide, 128), wide)
            return pltpu.make_async_copy(w_scr.at[slot], wfull_hbm.at[:, cols], local_sems.at[1 + slot])

        @pl.when((k == 0) & (i == 0))
        def _():
            mine().start()
            for cp in first():
                cp.start()
            mine().wait()
            copy(0, 0, 1 - c, me).wait_recv()

        for j in range(3):
            @pl.when((k == j + 1) & (i == 0))
            def _():
                copy(4 + j, 1 + j, 1 - c, me).wait_recv()

        @pl.when((k == 3) & (i == 0))
        def _():
            for slot in range(N_CHIP):
                save(slot).start()

        @pl.when(k == 0)
        def _():
            xh, _ = _rms(x_ref[...])
            h = xh * g_ref[...]
            h1t_ref[...] = h.T.astype(BF16)
            h1_scr[pl.ds(pl.multiple_of(i * tm, tm), tm), :] = h.astype(BF16)

        proj_ref[...] = _dot(h1_scr[pl.ds(pl.multiple_of(i * tm, tm), tm), :], w_scr[k]).astype(BF16)

        for j in range(3):
            @pl.when((k == 0) & (i == nt - 1) if j < 2 else (k == 2) & (i == fwd_at))
            def _():
                copy(1 + j, 1 + j, c, me).wait_recv()
                copy(4 + j, 1 + j, c, sibling).start()
                if j < 2:
                    pl.when(c == 1 - j)(lambda: relay().start())

        @pl.when((k == 3) & (i == nt - 1))
        def _():
            for cp in first() + [relay()] + [copy(4 + j, 1 + j, c, sibling) for j in range(3)]:
                cp.wait_send()
            for slot in range(N_CHIP):
                save(slot).wait()

    once = lambda k, i, order_ref: (jnp.where(k == 0, i, nt - 1), 0)
    return _call(
        body,
        name="inproj",
        prefetch=[order],
        operands=[x, g, w_shard],
        grid=(N_CHIP, nt),
        in_specs=[
            pl.BlockSpec((tm, D_MODEL), once),
            pl.BlockSpec((1, D_MODEL), lambda k, i, order_ref: (0, 0)),
            pl.BlockSpec(memory_space=pl.ANY),
        ],
        out_specs=[
            pl.BlockSpec((tm, wide), lambda k, i, order_ref: (i, order_ref[k])),
            pl.BlockSpec((D_MODEL, tm), lambda k, i, order_ref: (0, jnp.where(k == 0, i, nt - 1))),
            pl.BlockSpec(memory_space=pl.ANY),
        ],
        out_shape=[
            jax.ShapeDtypeStruct((s, IN_DIM), BF16),
            jax.ShapeDtypeStruct((D_MODEL, s), BF16),
            jax.ShapeDtypeStruct((D_MODEL, IN_DIM), BF16),
        ],
        scratch_shapes=[
            pltpu.VMEM((N_CHIP, D_MODEL, wide), BF16),
            pltpu.VMEM((s, D_MODEL), BF16),
            pltpu.SemaphoreType.DMA((7,)),
            pltpu.SemaphoreType.DMA((7,)),
            pltpu.SemaphoreType.DMA((1 + N_CHIP,)),
        ],
        plans=plans,
        start_after_body=True,
        pass_on_at=0.9,
        body_peers="all",
    )


def _rows_from_above(v, above):
    row = lax.broadcasted_iota(jnp.int32, (8, v.shape[1]), 0)
    r1, r2 = pltpu.roll(v, 1, 0), pltpu.roll(v, 2, 0)
    top1 = jnp.where(row == 0, above[7:8], r1[0:8])
    top2 = jnp.where(row == 0, above[6:7], jnp.where(row == 1, above[7:8], r2[0:8]))
    return jnp.concatenate([top1, r1[8:]], axis=0), jnp.concatenate([top2, r2[8:]], axis=0)


def _rows_from_below(v, below):
    n = v.shape[0]
    row = lax.broadcasted_iota(jnp.int32, (8, v.shape[1]), 0)
    r1, r2 = pltpu.roll(v, n - 1, 0), pltpu.roll(v, n - 2, 0)
    end1 = jnp.where(row == 7, below[0:1], r1[n - 8 :])
    end2 = jnp.where(row == 6, below[0:1], jnp.where(row == 7, below[1:2], r2[n - 8 :]))
    return jnp.concatenate([r1[: n - 8], end1], axis=0), jnp.concatenate([r2[: n - 8], end2], axis=0)


def _causal_mask():
    row = lax.broadcasted_iota(jnp.int32, (CHUNK, CHUNK), 0)
    col = lax.broadcasted_iota(jnp.int32, (CHUNK, CHUNK), 1)
    return row, row >= col


def _mixer_fwd(proj, conv_w, ln_g, ln_b, ws, bs_b, plans=()):
    s = proj.shape[0]
    nt = s // CHUNK

    def body(proj_ref, cw_ref, lng_ref, lnb_ref, ws_ref, bsb_ref, cat_ref, pcar_ref):
        @pl.when(pl.program_id(0) == 0)
        def _():
            pcar_ref[...] = jnp.zeros_like(pcar_ref)

        row, tril = _causal_mask()
        for j in range(8):
            cs = slice(CHUNK * j, CHUNK * (j + 1))

            def grp(k):
                return proj_ref[:, k * D_MODEL + CHUNK * j : k * D_MODEL + CHUNK * (j + 1)].astype(F32)

            w = cw_ref[:, cs]
            p = grp(1) * grp(2)
            p1, p2 = _rows_from_above(p, pcar_ref[:, cs])
            pcar_ref[:, cs] = p[CHUNK - 8 :]
            cv = w[0:1] * p2 + w[1:2] * p1 + w[2:3] * p
            sa = _silu(grp(3), with_grad=False)
            cat_ref[:, cs] = ((grp(0) * cv) * sa).astype(BF16)
            gu = _gelu(grp(4), with_grad=False)
            gv = _gelu(grp(5), with_grad=False)
            dv = gv - jnp.mean(gv, axis=-1, keepdims=True)
            vn = dv * lax.rsqrt(jnp.mean(dv * dv, axis=-1, keepdims=True) + EPS)
            vn = vn * lng_ref[:, cs] + lnb_ref[:, cs]
            wc = jnp.where(tril, ws_ref[j], 0.0).astype(BF16)
            sp = _dot(wc, vn.astype(BF16)) + bsb_ref[j]
            sb = _silu(grp(6), with_grad=False)
            cat_ref[:, D_MODEL + CHUNK * j : D_MODEL + CHUNK * (j + 1)] = ((gu * sp) * sb).astype(BF16)

    whole = lambda shape: pl.BlockSpec(shape, lambda i: (0,) * len(shape))
    (cat,), plan_outs = _call(
        body,
        name="mixer_fwd",
        operands=[proj, conv_w, ln_g, ln_b, ws, bs_b],
        grid=(nt,),
        in_specs=[
            pl.BlockSpec((CHUNK, IN_DIM), lambda i: (i, 0)),
            whole((8, D_MODEL)),
            whole((1, D_MODEL)),
            whole((1, D_MODEL)),
            whole((8, CHUNK, CHUNK)),
            whole((8, CHUNK, CHUNK)),
        ],
        out_specs=[pl.BlockSpec((CHUNK, MIX_DIM), lambda i: (i, 0))],
        out_shape=[jax.ShapeDtypeStruct((s, MIX_DIM), BF16)],
        scratch_shapes=[pltpu.VMEM((8, D_MODEL), F32)],
        plans=plans,
        pass_on_at=0.85,
    )
    return cat, plan_outs


def _mixer_bwd(proj, dcat, conv_w, ln_g, ln_b, ws, bs_b, plans=()):
    s = proj.shape[0]
    nt = s // CHUNK

    def body(proj_ref, halo_ref, dcat_ref, cw_ref, lng_ref, lnb_ref, ws_ref, bsb_ref,
             dproj_ref, dcw_ref, dlng_ref, dlnb_ref, dws_ref, dbs_ref,
             car_ref, acc_cw, acc_lng, acc_lnb, acc_bs, wc_scr, wct_scr):
        i = pl.program_id(0)
        tile = nt - 1 - i
        row, tril = _causal_mask()

        @pl.when(i == 0)
        def _():
            for j in range(8):
                wc = jnp.where(tril, ws_ref[j], 0.0)
                wc_scr[j] = wc.astype(BF16)
                wct_scr[j] = wc.T.astype(BF16)
            car_ref[...] = jnp.zeros_like(car_ref)
            acc_cw[...] = jnp.zeros_like(acc_cw)
            acc_lng[...] = jnp.zeros_like(acc_lng)
            acc_lnb[...] = jnp.zeros_like(acc_lnb)
            acc_bs[...] = jnp.zeros_like(acc_bs)
            dws_ref[...] = jnp.zeros_like(dws_ref)

        has_prev = jnp.where(tile > 0, 1.0, 0.0).astype(F32)
        for j in range(8):
            cs = slice(CHUNK * j, CHUNK * (j + 1))

            def col(k):
                return slice(k * D_MODEL + CHUNK * j, k * D_MODEL + CHUNK * (j + 1))

            def grp(k):
                return proj_ref[:, col(k)].astype(F32)

            w = cw_ref[:, cs]
            gb, gc, xa = grp(0), grp(1), grp(2)
            p = gc * xa
            pprev = halo_ref[:, col(1)].astype(F32) * halo_ref[:, col(2)].astype(F32) * has_prev
            p1, p2 = _rows_from_above(p, pprev[8:16])
            cv = w[0:1] * p2 + w[1:2] * p1 + w[2:3] * p
            za = grp(3)
            sa, dsa = _silu(za)
            da = dcat_ref[:, cs].astype(F32)
            da_gb = da * gb
            dproj_ref[:, col(0)] = ((da * sa) * cv).astype(BF16)
            dproj_ref[:, col(3)] = ((da_gb * cv) * dsa).astype(BF16)
            dcv = da_gb * sa
            d1, d2 = _rows_from_below(dcv, car_ref[:, cs])
            car_ref[:, cs] = dcv[0:8]
            dp = w[2:3] * dcv + w[1:2] * d1 + w[0:1] * d2
            dproj_ref[:, col(1)] = (dp * xa).astype(BF16)
            dproj_ref[:, col(2)] = (dp * gc).astype(BF16)
            acc_cw[0, :, cs] += _rows8(dcv * p2)
            acc_cw[1, :, cs] += _rows8(dcv * p1)
            acc_cw[2, :, cs] += _rows8(dcv * p)
            gu, dgu = _gelu(grp(4))
            gv, dgv = _gelu(grp(5))
            dv = gv - jnp.mean(gv, axis=-1, keepdims=True)
            rstd = lax.rsqrt(jnp.mean(dv * dv, axis=-1, keepdims=True) + EPS)
            vnh = dv * rstd
            lng = lng_ref[:, cs]
            vnb = (vnh * lng + lnb_ref[:, cs]).astype(BF16)
            sp = _dot(wc_scr[j], vnb) + bsb_ref[j]
            zb = grp(6)
            sb, dsb = _silu(zb)
            db = dcat_ref[:, D_MODEL + CHUNK * j : D_MODEL + CHUNK * (j + 1)].astype(F32)
            db_sp, db_gu = db * sp, db * gu
            dproj_ref[:, col(4)] = ((db_sp * sb) * dgu).astype(BF16)
            dproj_ref[:, col(6)] = ((db_sp * gu) * dsb).astype(BF16)
            dsp = db_gu * sb
            acc_bs[j] += dsp
            dspb = dsp.astype(BF16)
            dws_ref[j] += _dot_nt(dspb, vnb)
            dvn = _dot(wct_scr[j], dspb)
            acc_lnb[:, cs] += _rows8(dvn)
            acc_lng[:, cs] += _rows8(dvn * vnh)
            dvh = dvn * lng
            dgvv = rstd * (dvh - jnp.mean(dvh, axis=-1, keepdims=True) - vnh * jnp.mean(dvh * vnh, axis=-1, keepdims=True))
            dproj_ref[:, col(5)] = (dgvv * dgv).astype(BF16)

        @pl.when(i == nt - 1)
        def _():
            dlng_ref[...] = jnp.sum(acc_lng[...], axis=0, keepdims=True)
            dlnb_ref[...] = jnp.sum(acc_lnb[...], axis=0, keepdims=True)
            dcw_ref[...] = jnp.zeros_like(dcw_ref)
            for k in range(3):
                dcw_ref[k : k + 1, :] = jnp.sum(acc_cw[k], axis=0, keepdims=True)
            for j in range(8):
                dws_ref[j] = jnp.where(tril, dws_ref[j], 0.0)
                dbs_ref[:, CHUNK * j : CHUNK * (j + 1)] = jnp.sum(acc_bs[j].T, axis=0, keepdims=True)

    whole = lambda shape: pl.BlockSpec(shape, lambda i: (0,) * len(shape))
    halo_rows = 16
    per = CHUNK // halo_rows
    return _call(
        body,
        name="mixer_bwd",
        operands=[proj, proj, dcat, conv_w, ln_g, ln_b, ws, bs_b],
        plans=plans,
        grid=(nt,),
        in_specs=[
            pl.BlockSpec((CHUNK, IN_DIM), lambda i: (nt - 1 - i, 0)),
            pl.BlockSpec((halo_rows, IN_DIM), lambda i: (jnp.maximum((nt - 1 - i) * per - 1, 0), 0)),
            pl.BlockSpec((CHUNK, MIX_DIM), lambda i: (nt - 1 - i, 0)),
            whole((8, D_MODEL)),
            whole((1, D_MODEL)),
            whole((1, D_MODEL)),
            whole((8, CHUNK, CHUNK)),
            whole((8, CHUNK, CHUNK)),
        ],
        out_specs=[
            pl.BlockSpec((CHUNK, IN_DIM), lambda i: (nt - 1 - i, 0)),
            whole((8, D_MODEL)),
            whole((1, D_MODEL)),
            whole((1, D_MODEL)),
            whole((8, CHUNK, CHUNK)),
            whole((1, D_MODEL)),
        ],
        out_shape=[
            jax.ShapeDtypeStruct((s, IN_DIM), BF16),
            jax.ShapeDtypeStruct((8, D_MODEL), F32),
            jax.ShapeDtypeStruct((1, D_MODEL), F32),
            jax.ShapeDtypeStruct((1, D_MODEL), F32),
            jax.ShapeDtypeStruct((8, CHUNK, CHUNK), F32),
            jax.ShapeDtypeStruct((1, D_MODEL), F32),
        ],
        scratch_shapes=[
            pltpu.VMEM((8, D_MODEL), F32),
            pltpu.VMEM((3, 8, D_MODEL), F32),
            pltpu.VMEM((8, D_MODEL), F32),
            pltpu.VMEM((8, D_MODEL), F32),
            pltpu.VMEM((8, CHUNK, CHUNK), F32),
            pltpu.VMEM((8, CHUNK, CHUNK), BF16),
            pltpu.VMEM((8, CHUNK, CHUNK), BF16),
        ],
    )


def _kv_fwd(mem, g_mem, w_kv):
    def body(mem_ref, g_ref, w_ref, k_ref, vv_ref, mt_ref):
        mh, _ = _rms(mem_ref[...])
        m = mh * g_ref[...]
        kv = _dot(m.astype(BF16), w_ref[...])
        k_ref[...] = kv[:, :D_MODEL].astype(BF16)
        vv_ref[...] = kv[:, D_MODEL:].astype(BF16)
        mt_ref[...] = m.T.astype(BF16)

    return pl.pallas_call(
        body,
        name="kv_fwd",
        out_shape=[
            jax.ShapeDtypeStruct((MEM_LEN, D_MODEL), BF16),
            jax.ShapeDtypeStruct((MEM_LEN, D_MODEL), BF16),
            jax.ShapeDtypeStruct((D_MODEL, MEM_LEN), BF16),
        ],
    )(mem, g_mem, w_kv)


def _kv_bwd(mem, w_kv, mt, dk, dvv):
    def body(mem_ref, w_ref, mt_ref, dk_ref, dvv_ref, dw_ref, dg_ref):
        dkv = jnp.concatenate([dk_ref[...], dvv_ref[...]], axis=1).astype(BF16)
        dw_ref[...] = _dot(mt_ref[...], dkv).astype(BF16)
        dm = _dot_nt(dkv, w_ref[...])
        mh, _ = _rms(mem_ref[...])
        dg_ref[...] = jnp.sum(dm * mh, axis=0, keepdims=True)

    return pl.pallas_call(
        body,
        name="kv_bwd",
        out_shape=[jax.ShapeDtypeStruct((D_MODEL, 2 * D_MODEL), BF16), jax.ShapeDtypeStruct((1, D_MODEL), F32)],
    )(mem, w_kv, mt, dk, dvv)


def _attn(x, cat, target, w_out, w_q, w_xo, k, vv, g_x, g_f):
    s = x.shape[0]
    tm = min(TM_ATTN, s)
    sub = min(SUB_ATTN, tm)
    nt = s // tm
    scale = 1.0 / math.sqrt(X_HEAD_DIM)

    def body(x_ref, cat_ref, t_ref, wout_ref, wq_ref, wxo_ref, k_ref, vv_ref, gx_ref, gf_ref,
             dcat_ref, actt_ref, cot_ref,
             dk_ref, dvv_ref, dgf_ref, dgx_ref, loss_ref,
             p_scr, q_scr, o_scr, dq_scr, acc_gf, acc_gx, acc_loss):
        i = pl.program_id(0)

        @pl.when(i == 0)
        def _():
            dk_ref[...] = jnp.zeros_like(dk_ref)
            dvv_ref[...] = jnp.zeros_like(dvv_ref)
            acc_gf[...] = jnp.zeros_like(acc_gf)
            acc_gx[...] = jnp.zeros_like(acc_gx)
            acc_loss[...] = jnp.zeros_like(acc_loss)

        subs = [slice(u * sub, (u + 1) * sub) for u in range(tm // sub)]
        heads = [slice(X_HEAD_DIM * h, X_HEAD_DIM * (h + 1)) for h in range(X_HEADS)]
        gx, gf = gx_ref[...], gf_ref[...]
        x1, x1h, r2 = [], [], []
        for rs in subs:
            cat = cat_ref[rs, :]
            actt_ref[0:MIX_DIM, rs] = cat.astype(F32).T.astype(BF16)
            v = x_ref[rs, :] + _dot(cat, wout_ref[...])
            vh, r = _rms(v)
            h2 = vh * gx
            actt_ref[MIX_DIM : MIX_DIM + D_MODEL, rs] = h2.T.astype(BF16)
            q_scr[rs, :] = _dot(h2.astype(BF16), wq_ref[...]).astype(BF16)
            x1.append(v), x1h.append(vh), r2.append(r)
        for rs in subs:
            for hs in heads:
                sc = _dot_nt(q_scr[rs, hs], k_ref[:, hs]) * scale
                e = jnp.exp(sc - jnp.max(sc, axis=-1, keepdims=True))
                p = e / jnp.sum(e, axis=-1, keepdims=True)
                p_scr[rs, hs] = p
                o_scr[rs, hs] = _dot(p.astype(BF16), vv_ref[:, hs])
        dx2, sq, d_gf = [], None, None
        for u, rs in enumerate(subs):
            o = o_scr[rs, :]
            actt_ref[MIX_DIM + D_MODEL :, rs] = o.T.astype(BF16)
            x2 = x1[u] + _dot(o.astype(BF16), wxo_ref[...])
            x2h, r3 = _rms(x2)
            err = x2h * gf - t_ref[rs, :]
            dy = err * (1.0 / D_MODEL)
            sq = _rows8(err * err) if sq is None else sq + _rows8(err * err)
            d_gf = _rows8(dy * x2h) if d_gf is None else d_gf + _rows8(dy * x2h)
            dx2.append(_rms_bwd(dy * gf, x2h, r3))
        acc_loss[...] += sq
        acc_gf[...] += d_gf
        d_vv, d_k = [None] * X_HEADS, [None] * X_HEADS
        for u, rs in enumerate(subs):
            dx2b = dx2[u].astype(BF16)
            cot_ref[2, rs, :] = dx2b
            dob = _dot_nt(dx2b, wxo_ref[...]).astype(BF16)
            for h, hs in enumerate(heads):
                p = p_scr[rs, hs]
                dp = _dot_nt(dob[:, hs], vv_ref[:, hs])
                ds = p * (dp - jnp.sum(dp * p, axis=-1, keepdims=True)) * scale
                dq_scr[rs, hs] = _dot(ds.astype(BF16), k_ref[:, hs]).astype(BF16)
                dvv_h = _dot(p.T.astype(BF16), dob[:, hs])
                dk_h = _dot(ds.T.astype(BF16), q_scr[rs, hs])
                d_vv[h] = dvv_h if d_vv[h] is None else d_vv[h] + dvv_h
                d_k[h] = dk_h if d_k[h] is None else d_k[h] + dk_h
        for h, hs in enumerate(heads):
            dvv_ref[:, hs] += d_vv[h]
            dk_ref[:, hs] += d_k[h]
        d_gx = None
        for u, rs in enumerate(subs):
            dqb = dq_scr[rs, :]
            cot_ref[1, rs, :] = dqb
            dh2 = _dot_nt(dqb, wq_ref[...])
            d_gx = _rows8(dh2 * x1h[u]) if d_gx is None else d_gx + _rows8(dh2 * x1h[u])
            dx1 = dx2[u] + _rms_bwd(dh2 * gx, x1h[u], r2[u])
            dx1b = dx1.astype(BF16)
            cot_ref[0, rs, :] = dx1b
            dcat_ref[rs, :] = _dot_nt(dx1b, wout_ref[...]).astype(BF16)
        acc_gx[...] += d_gx

        @pl.when(i == nt - 1)
        def _():
            dgf_ref[...] = jnp.sum(acc_gf[...], axis=0, keepdims=True)
            dgx_ref[...] = jnp.sum(acc_gx[...], axis=0, keepdims=True)
            loss_ref[...] = jnp.sum(acc_loss[...], axis=0, keepdims=True)

    whole = lambda shape: pl.BlockSpec(shape, lambda i: (0,) * len(shape))
    once = lambda shape: pl.BlockSpec(shape, lambda i: (0,) * len(shape), pipeline_mode=pl.Buffered(1))
    rows = lambda width: pl.BlockSpec((tm, width), lambda i: (i, 0))
    cols = lambda height: pl.BlockSpec((height, tm), lambda i: (0, i))
    vec = jax.ShapeDtypeStruct((1, D_MODEL), F32)
    return pl.pallas_call(
        body,
        name="attn",
        grid=(nt,),
        in_specs=[
            rows(D_MODEL), rows(MIX_DIM), rows(D_MODEL),
            once((MIX_DIM, D_MODEL)), once((D_MODEL, D_MODEL)), once((D_MODEL, D_MODEL)),
            once((MEM_LEN, D_MODEL)), once((MEM_LEN, D_MODEL)),
            whole((1, D_MODEL)), whole((1, D_MODEL)),
        ],
        out_specs=[
            rows(MIX_DIM), cols(2 * MIX_DIM),
            pl.BlockSpec((3, tm, D_MODEL), lambda i: (0, i, 0)),
            whole((MEM_LEN, D_MODEL)), whole((MEM_LEN, D_MODEL)),
            whole((1, D_MODEL)), whole((1, D_MODEL)), whole((1, D_MODEL)),
        ],
        out_shape=[
            jax.ShapeDtypeStruct((s, MIX_DIM), BF16),
            jax.ShapeDtypeStruct((2 * MIX_DIM, s), BF16),
            jax.ShapeDtypeStruct((3, s, D_MODEL), BF16),
            jax.ShapeDtypeStruct((MEM_LEN, D_MODEL), F32),
            jax.ShapeDtypeStruct((MEM_LEN, D_MODEL), F32),
            vec, vec, vec,
        ],
        scratch_shapes=[
            pltpu.VMEM((tm, D_MODEL), F32),
            pltpu.VMEM((tm, D_MODEL), BF16),
            pltpu.VMEM((tm, D_MODEL), F32),
            pltpu.VMEM((tm, D_MODEL), BF16),
            pltpu.VMEM((8, D_MODEL), F32),
            pltpu.VMEM((8, D_MODEL), F32),
            pltpu.VMEM((8, D_MODEL), F32),
        ],
        compiler_params=pltpu.CompilerParams(dimension_semantics=("arbitrary",)),
    )(x, cat, target, w_out, w_q, w_xo, k, vv, g_x, g_f)


def _matmul_bf16(name, a, b, rows=None, plans=()):
    first, m = (0, a.shape[0]) if rows is None else rows
    kk = a.shape[1]
    n = b.shape[1]
    tm, tn, tk = min(m, TM_DW), min(n, D_MODEL), min(kk, TK_DW)
    nk = kk // tk
    i0 = first // tm

    def body(a_ref, b_ref, o_ref, *acc):
        prod = _dot(a_ref[...], b_ref[...])
        if nk == 1:
            o_ref[...] = prod.astype(BF16)
            return
        (acc_ref,) = acc
        kid = pl.program_id(2)

        @pl.when(kid == 0)
        def _():
            acc_ref[...] = prod

        @pl.when(kid > 0)
        def _():
            acc_ref[...] += prod

        @pl.when(kid == nk - 1)
        def _():
            o_ref[...] = acc_ref[...].astype(BF16)

    (out,), plan_outs = _call(
        body,
        name=name,
        operands=[a, b],
        grid=(m // tm, n // tn, nk),
        in_specs=[pl.BlockSpec((tm, tk), lambda i, j, k: (i + i0, k)), pl.BlockSpec((tk, tn), lambda i, j, k: (k, j))],
        out_specs=[pl.BlockSpec((tm, tn), lambda i, j, k: (i, j))],
        out_shape=[jax.ShapeDtypeStruct((m, n), BF16)],
        scratch_shapes=[pltpu.VMEM((tm, tn), F32)] if nk > 1 else [],
        plans=plans,
    )
    return out, plan_outs


def _dw_attn(actt, cot, plans=()):
    s = actt.shape[1]
    tm = TM_DW
    counts = [MIX_DIM // tm, D_MODEL // tm, D_MODEL // tm]
    starts = [0, counts[0], counts[0] + counts[1]]
    which = lambda i: jnp.where(i < starts[1], 0, jnp.where(i < starts[2], 1, 2))

    def body(a_ref, b_ref, *outs):
        i = pl.program_id(0)
        prod = _dot(a_ref[...], b_ref[0]).astype(BF16)
        for m in range(3):
            @pl.when(which(i) == m)
            def _():
                outs[m][...] = prod

    out_specs = [pl.BlockSpec((tm, D_MODEL), lambda i, m=m: (jnp.clip(i - starts[m], 0, counts[m] - 1), 0)) for m in range(3)]
    out_shape = [jax.ShapeDtypeStruct((counts[m] * tm, D_MODEL), BF16) for m in range(3)]
    outs, plan_outs = _call(
        body,
        name="dw_attn",
        operands=[actt, cot],
        grid=(sum(counts),),
        in_specs=[pl.BlockSpec((tm, s), lambda i: (i, 0)), pl.BlockSpec((1, s, D_MODEL), lambda i: (which(i), 0, 0))],
        out_specs=out_specs,
        out_shape=out_shape,
        plans=plans,
    )
    return outs, plan_outs


def _dh1(dproj, w_in, x, dx1, g, after=()):
    s = x.shape[0]
    tm = min(TM_DH1, s)
    nt = s // tm

    def body(dp_ref, w_ref, x_ref, dx1_ref, g_ref, *rest):
        gx_ref, dg_ref, acc_g = rest[-3:]
        i = pl.program_id(0)

        @pl.when(i == 0)
        def _():
            acc_g[...] = jnp.zeros_like(acc_g)

        dh1 = _dot_nt(dp_ref[...], w_ref[...])
        xh, r = _rms(x_ref[...])
        acc_g[...] += _rows8(dh1 * xh)
        gx_ref[...] = dx1_ref[0].astype(F32) + _rms_bwd(dh1 * g_ref[...], xh, r)

        @pl.when(i == nt - 1)
        def _():
            dg_ref[...] = jnp.sum(acc_g[...], axis=0, keepdims=True)

    rows = pl.BlockSpec((tm, D_MODEL), lambda i: (i, 0))
    outs, _ = _call(
        body,
        name="dh1",
        operands=[dproj, w_in, x, dx1, g] + list(after),
        grid=(nt,),
        in_specs=[
            pl.BlockSpec((tm, IN_DIM), lambda i: (i, 0)),
            pl.BlockSpec((D_MODEL, IN_DIM), lambda i: (0, 0), pipeline_mode=pl.Buffered(1)),
            rows,
            pl.BlockSpec((1, tm, D_MODEL), lambda i: (0, i, 0)),
            pl.BlockSpec((1, D_MODEL), lambda i: (0, 0)),
        ] + [pl.BlockSpec(memory_space=pl.ANY)] * len(after),
        out_specs=[rows, pl.BlockSpec((1, D_MODEL), lambda i: (0, 0))],
        out_shape=[jax.ShapeDtypeStruct((s, D_MODEL), F32), jax.ShapeDtypeStruct((1, D_MODEL), F32)],
        scratch_shapes=[pltpu.VMEM((8, D_MODEL), F32)],
    )
    return outs


def _pair_sum(name, grads, gots, specs, core, plans=()):
    n = len(grads)
    shards = []
    for grad, (axis, size) in zip(grads, specs):
        shard = list(grad.shape)
        shard[axis] = size
        shards.append(tuple(shard))

    def body(core_ref, *refs):
        for g_ref, got_ref, o_ref in zip(refs[:n], refs[n : 2 * n], refs[2 * n :]):
            o_ref[0] = (g_ref[...].astype(F32) + got_ref[0].astype(F32)).astype(BF16)

    def own(shard, axis):
        if axis == 1:
            return pl.BlockSpec(shard, lambda q, core_ref: (0, 2 * q + core_ref[0]))
        return pl.BlockSpec(shard, lambda q, core_ref: (2 * q + core_ref[0], 0))

    slots = [pl.BlockSpec((1, *shard), lambda q, core_ref: (q, 0, 0)) for shard in shards]
    outs, plan_outs = _call(
        body,
        name=name,
        prefetch=[core],
        operands=list(grads) + list(gots),
        grid=(N_CHIP,),
        in_specs=[own(shard, axis) for shard, (axis, _) in zip(shards, specs)] + slots,
        out_specs=slots,
        out_shape=[jax.ShapeDtypeStruct((N_CHIP, *shard), BF16) for shard in shards],
        plans=plans,
    )
    return (outs, plan_outs) if plans else outs


def _adam_step(w, m, v, g):
    c1 = 1.0 / (1.0 - ADAM_B1**ADAM_STEP)
    c2 = 1.0 / (1.0 - ADAM_B2**ADAM_STEP)
    m_new = ADAM_B1 * m + (1.0 - ADAM_B1) * g
    v_new = ADAM_B2 * v + (1.0 - ADAM_B2) * (g * g)
    return -ADAM_LR * ((m_new * c1) / (jnp.sqrt(v_new * c2) + ADAM_EPS) + ADAM_WD * w), m_new, v_new


def _update_small(dev, vec_w, vec_m, vec_v, conv_w, conv_m, conv_v, small, landed):
    n_vec = len(vec_w)

    def body(dev_ref, *refs):
        refs = list(refs)
        take = lambda k: [refs.pop(0) for _ in range(k)]
        w_refs, m_refs, v_refs = take(n_vec), take(n_vec), take(n_vec)
        cw_ref, cm_ref, cv_ref, small_ref, land_ref, small_conv_ref, land_conv_ref = take(7)
        vec_outs, conv_outs, (loss_ref,) = take(4 * n_vec), take(4), take(1)

        def total(own_ref, others_ref):
            acc = None
            for d in range(N_DEV):
                part = jnp.where(dev_ref[0] == d, own_ref[0], others_ref[d])
                acc = part if acc is None else acc + part
            return acc

        g_all = total(small_ref, land_ref)
        for i in range(n_vec):
            g = g_all[i : i + 1]
            delta, m_new, v_new = _adam_step(w_refs[i][...], m_refs[i][...], v_refs[i][...], g)
            for ref, val in zip(vec_outs[4 * i : 4 * i + 4], (g, delta, m_new, v_new)):
                ref[...] = val
        g_conv = total(small_conv_ref, land_conv_ref)
        for tap in range(3):
            g = g_conv[tap : tap + 1]
            delta, m_new, v_new = _adam_step(cw_ref[tap], cm_ref[tap], cv_ref[tap], g)
            for ref, val in zip(conv_outs, (g, delta, m_new, v_new)):
                ref[tap] = val
        loss_ref[...] = jnp.broadcast_to(0.5 * jnp.sum(g_all[7:8], axis=-1, keepdims=True) / D_MODEL, loss_ref.shape)

    vec_spec = pl.BlockSpec((1, D_MODEL), lambda i, dev_ref: (0, 0))
    conv_spec = pl.BlockSpec((3, 1, CHUNK), lambda i, dev_ref: (0, 0, 0))
    vec_shape = jax.ShapeDtypeStruct((1, D_MODEL), F32)
    conv_shape = jax.ShapeDtypeStruct((3, 1, CHUNK), F32)
    outs, _ = _call(
        body,
        name="update_small",
        prefetch=[dev],
        operands=list(vec_w) + list(vec_m) + list(vec_v) + [conv_w, conv_m, conv_v, small, landed, small, landed],
        grid=(1,),
        in_specs=[vec_spec] * (3 * n_vec) + [conv_spec] * 3 + [
            pl.BlockSpec((1, 16, D_MODEL), lambda i, dev_ref: (0, 0, 0)),
            pl.BlockSpec((N_DEV, 16, D_MODEL), lambda i, dev_ref: (0, 0, 0)),
            pl.BlockSpec((1, 8, CHUNK), lambda i, dev_ref: (0, 1, dev_ref[0])),
            pl.BlockSpec((N_DEV, 8, CHUNK), lambda i, dev_ref: (0, 1, dev_ref[0])),
        ],
        out_specs=[vec_spec] * (4 * n_vec) + [conv_spec] * 4 + [pl.BlockSpec((1, CHUNK), lambda i, dev_ref: (0, 0))],
        out_shape=[vec_shape] * (4 * n_vec) + [conv_shape] * 4 + [jax.ShapeDtypeStruct((1, CHUNK), F32)],
    )
    return [outs[4 * i : 4 * i + 4] for i in range(n_vec)], outs[4 * n_vec : 4 * n_vec + 4], outs[-1]


def _adamw(name, groups, steps, own_slot=None, plans=()):
    lists = lambda p: list(p) if isinstance(p, (list, tuple)) else [p]
    groups = [(g[0], g[1], g[2], lists(g[3]), lists(g[4]) if len(g) > 4 else []) for g in groups]
    counts = [3 + len(g[3]) + len(g[4]) for g in groups]
    use_slot = own_slot is not None

    def first_tiles(w, parts):
        firsts, first = [], 0
        for p in parts:
            firsts.append(first)
            first += p.shape[1] // (w.shape[0] // steps)
        return firsts

    def update(t, slot_ref, firsts, w_ref, m_ref, v_ref, p_refs, o_refs, g_ref, d_ref, nm_ref, nv_ref):
        g = None
        for k, p_ref in enumerate(p_refs):
            gk = None
            for q in range(p_ref.shape[0]):
                part = p_ref[q].astype(F32)
                if o_refs:
                    part = jnp.where(slot_ref[0] == q, o_refs[k][0].astype(F32), part)
                gk = part if gk is None else gk + part
            g = gk if g is None else jnp.where(t >= firsts[k], gk, g)
        g_ref[...] = g
        d_ref[...], nm_ref[...], nv_ref[...] = _adam_step(w_ref[...], m_ref[...], v_ref[...], g)

    def body(*refs):
        refs = list(refs)
        slot_ref = refs.pop(0) if use_slot else None
        t = pl.program_id(0)
        ins, outs = refs[: sum(counts)], refs[sum(counts) :]
        for i, (w, _, _, parts, own) in enumerate(groups):
            mine = ins[sum(counts[:i]) : sum(counts[: i + 1])]
            update(t, slot_ref, first_tiles(w, parts), *mine[:3], mine[3 : 3 + len(parts)], mine[3 + len(parts) :],
                   *outs[4 * i : 4 * i + 4])

    operands, in_specs, out_specs, out_shape = [], [], [], []
    for w, m, v, parts, own in groups:
        r, cdim = w.shape
        tr = r // steps
        firsts = first_tiles(w, parts)

        def tile_of(k, t, firsts=firsts, parts=parts, tr=tr):
            return jnp.clip(t - firsts[k], 0, parts[k].shape[1] // tr - 1)

        blk = pl.BlockSpec((tr, cdim), lambda t, *_: (t, 0))
        operands += [w, m, v] + parts + own
        in_specs += [blk, blk, blk]
        in_specs += [pl.BlockSpec((p.shape[0], tr, cdim), lambda t, *_, k=k, tile_of=tile_of: (0, tile_of(k, t), 0)) for k, p in enumerate(parts)]
        in_specs += [pl.BlockSpec((1, tr, cdim), lambda t, slot_ref, k=k, tile_of=tile_of: (slot_ref[0], tile_of(k, t), 0)) for k in range(len(own))]
        out_specs += [blk] * 4
        out_shape += [jax.ShapeDtypeStruct((r, cdim), F32)] * 4
    outs, plan_outs = _call(
        body,
        name=name,
        prefetch=[own_slot] if use_slot else [],
        operands=operands,
        grid=(steps,),
        in_specs=in_specs,
        out_specs=out_specs,
        out_shape=out_shape,
        plans=plans,
    )
    return [outs[4 * i : 4 * i + 4] for i in range(len(groups))], plan_outs


def _chip_routes(n):
    x, y, c = _place()
    my_chip = 2 * x + y
    return [(a, (*chip, c), 2 * chip[0] + chip[1], my_chip, 2 * chip[0] + chip[1])
            for a in range(n) for chip in _other_chips(x, y)]


def _gather_routes(n):
    x, y, c = _place()
    flip = lambda v, f: 1 - v if f else v
    peers = [(flip(x, fx), flip(y, fy), flip(c, fc)) for fx in (0, 1) for fy in (0, 1) for fc in (0, 1) if fx + fy + fc]
    return [(0, p, 0, 4 * x + 2 * y + c, 4 * p[0] + 2 * p[1] + p[2]) for p in peers]


_chip_routes.copies_per_array = N_CHIP - 1
_gather_routes.copies_per_array = N_DEV - 1


def _split_start(name, routes, srcs, land_shapes):
    n = len(srcs)
    hbm = pl.BlockSpec(memory_space=pltpu.HBM)
    sem = pl.BlockSpec(memory_space=pltpu.SEMAPHORE)

    def body(*refs):
        src_refs, land_refs = refs[:n], refs[n : 2 * n]
        send_sems, recv_sems = refs[2 * n], refs[2 * n + 1]
        token = refs[-1]
        for k, (a, peer, src_slot, there, _) in enumerate(routes(n)):
            pltpu.make_async_remote_copy(
                src_ref=src_refs[a].at[src_slot], dst_ref=land_refs[a].at[there],
                send_sem=send_sems.at[k], recv_sem=recv_sems.at[k], device_id=peer, device_id_type=MESH).start()
        token[...] = jnp.zeros_like(token)

    in_hbm = [pltpu.with_memory_space_constraint(p, pltpu.HBM) for p in srcs]
    in_hbm += [pltpu.with_memory_space_constraint(lax.empty(shape, p.dtype), pltpu.HBM) for p, shape in zip(srcs, land_shapes)]
    n_copies = routes.copies_per_array * n
    res = pl.pallas_call(
        body,
        name=name,
        out_shape=[pltpu.SemaphoreType.DMA((n_copies,)), pltpu.SemaphoreType.DMA((n_copies,))]
        + [pltpu.HBM(p.shape, p.dtype) for p in srcs] + [pltpu.HBM(shape, p.dtype) for p, shape in zip(srcs, land_shapes)]
        + [jax.ShapeDtypeStruct((8, CHUNK), F32)],
        in_specs=[hbm] * (2 * n),
        out_specs=[sem, sem] + [hbm] * (2 * n) + [pl.BlockSpec(memory_space=pltpu.VMEM)],
        input_output_aliases={i: 2 + i for i in range(2 * n)},
        compiler_params=pltpu.CompilerParams(has_side_effects=pltpu.SideEffectType.DATAFLOW_SIDE_EFFECTING),
    )(*in_hbm)
    return res[:-1], res[-1]


def _split_wait(name, routes, in_flight, after):
    n = (len(in_flight) - 2) // 2
    hbm = pl.BlockSpec(memory_space=pltpu.HBM)
    sem = pl.BlockSpec(memory_space=pltpu.SEMAPHORE)
    send_sems, recv_sems, *bufs = in_flight

    def body(*refs):
        src_refs, land_refs = refs[:n], refs[n : 2 * n]
        send_ref, recv_ref = refs[2 * n], refs[2 * n + 1]
        for k, (a, peer, src_slot, _, here) in enumerate(routes(n)):
            cp = pltpu.make_async_remote_copy(
                src_ref=src_refs[a].at[src_slot], dst_ref=land_refs[a].at[here],
                send_sem=send_ref.at[k], recv_sem=recv_ref.at[k], device_id=peer, device_id_type=MESH)
            cp.wait_send()
            cp.wait_recv()

    res = pl.pallas_call(
        body,
        name=name,
        out_shape=[pltpu.HBM(b.shape, b.dtype) for b in bufs],
        in_specs=[hbm] * (2 * n) + [sem, sem] + [pl.BlockSpec(memory_space=pl.ANY)] * len(after),
        out_specs=[hbm] * (2 * n),
        input_output_aliases={i: i for i in range(2 * n)},
        compiler_params=pltpu.CompilerParams(has_side_effects=pltpu.SideEffectType.DATAFLOW_SIDE_EFFECTING),
    )(*bufs, send_sems, recv_sems, *after)
    return res[:n], res[n:]


def kernel(x, mem, norm_mix_g, w_in, conv_w, gm_ln_g, gm_ln_b, gm_ws, gm_bs, w_out, norm_x_g, norm_mem_g, w_q, w_kv, w_xo, norm_final_g, loss_target, m_norm_mix_g, m_w_in, m_conv_w, m_gm_ln_g, m_gm_ln_b, m_gm_ws, m_gm_bs, m_w_out, m_norm_x_g, m_norm_mem_g, m_w_q, m_w_kv, m_w_xo, m_norm_final_g, v_norm_mix_g, v_w_in, v_conv_w, v_gm_ln_g, v_gm_ln_b, v_gm_ws, v_gm_bs, v_w_out, v_norm_x_g, v_norm_mem_g, v_w_q, v_w_kv, v_w_xo, v_norm_final_g):
    dev = 4 * lax.axis_index("x") + 2 * lax.axis_index("y") + lax.axis_index("c")
    core = lax.axis_index("c").astype(jnp.int32).reshape(1)
    x2 = x[0]
    target = loss_target[0]
    pad_taps = lambda t: jnp.pad(t[0], ((0, 5), (0, 0)))

    big_names = ["w_in", "w_out", "w_q", "w_kv", "w_xo"]
    big_w = dict(w_in=w_in[0], w_out=w_out[0], w_q=w_q[0], w_kv=w_kv[0], w_xo=w_xo[0])
    big_m = dict(w_in=m_w_in[0], w_out=m_w_out[0], w_q=m_w_q[0], w_kv=m_w_kv[0], w_xo=m_w_xo[0])
    big_v = dict(w_in=v_w_in[0], w_out=v_w_out[0], w_q=v_w_q[0], w_kv=v_w_kv[0], w_xo=v_w_xo[0])
    shard_axis = dict(w_in=1, w_out=0, w_q=0, w_kv=1, w_xo=0)
    full_shape = dict(w_in=(D_MODEL, IN_DIM), w_out=(MIX_DIM, D_MODEL), w_q=(D_MODEL, D_MODEL),
                      w_kv=(D_MODEL, 2 * D_MODEL), w_xo=(D_MODEL, D_MODEL))
    shard_size = {k: big_w[k].shape[shard_axis[k]] for k in big_names}
    others = big_names[1:]
    gather_spec = lambda k: (full_shape[k], shard_axis[k], shard_size[k])
    reduce_spec = lambda k: (shard_axis[k], shard_size[k])
    ws = gm_ws[0]
    bs_b = jnp.broadcast_to(gm_bs[0][:, :, None], (8, CHUNK, CHUNK))
    g_f = norm_final_g.reshape(1, D_MODEL)

    px, py = lax.axis_index("x"), lax.axis_index("y")
    chip_order = jnp.stack([2 * px + py, 2 * (1 - px) + py, 2 * px + 1 - py, 2 * (1 - px) + 1 - py]).astype(jnp.int32)
    (proj, h1t, w_in_full), ((w_out_full, conv_g),) = _inproj(
        x2, norm_mix_g, big_w["w_in"].astype(BF16), chip_order,
        plans=[_gather_plan([big_w["w_out"].astype(BF16), pad_taps(conv_w)[None]], [gather_spec("w_out"), ((N_DEV, 8, CHUNK), 0, 1)])])
    conv_full = conv_g.transpose(1, 0, 2).reshape(8, D_MODEL)
    attn_w = ["w_q", "w_kv", "w_xo"]
    cat, (gathered,) = _mixer_fwd(
        proj, conv_full, gm_ln_g, gm_ln_b, ws, bs_b,
        plans=[_gather_plan([big_w[k].astype(BF16) for k in attn_w], [gather_spec(k) for k in attn_w])])
    wf = dict(zip(attn_w, gathered), w_out=w_out_full)
    k, vv, mt = _kv_fwd(mem[0], norm_mem_g, wf["w_kv"])
    (dcat, actt, cot, dk, dvv, d_gf, d_gx, sq_err) = _attn(
        x2, cat, target, wf["w_out"], wf["w_q"], wf["w_xo"], k, vv, norm_x_g, g_f)

    sibling_plan = lambda names, grads: _sibling_plan([grads[k] for k in names], [reduce_spec(k) for k in names])
    early = ["w_kv", "w_out", "w_q", "w_xo"]
    big_g = {}
    big_g["w_kv"], d_gmem = _kv_bwd(mem[0], wf["w_kv"], mt, dk, dvv)
    dws, (got_kv,) = _dw_attn(actt, cot, plans=[sibling_plan(early[:1], big_g)])
    big_g.update(zip(early[1:], dws))
    pair_kv, (got_rest,) = _pair_sum("pair_sum_w_kv", [big_g["w_kv"]], got_kv, [reduce_spec("w_kv")], core,
                                     plans=[sibling_plan(early[1:], big_g)])
    pair_rest = _pair_sum("pair_sum_attn", [big_g[k] for k in early[1:]], got_rest, [reduce_spec(k) for k in early[1:]], core)
    (dproj, d_cw, d_lng, d_lnb, d_ws, d_bs), (parts_early,) = _mixer_bwd(
        proj, dcat, conv_full, gm_ln_g, gm_ln_b, ws, bs_b, plans=[_chip_plan(list(pair_kv) + list(pair_rest))])
    half = D_MODEL // 2
    in_spec = reduce_spec("w_in")
    dw_top, _ = _matmul_bf16("dw_in_top", h1t, dproj, rows=(0, half))
    dw_bot, (got_top, (ws_all,)) = _matmul_bf16(
        "dw_in_bot", h1t, dproj, rows=(half, half),
        plans=[_sibling_plan([dw_top], [in_spec]),
               _gather_plan([d_ws.reshape(1, 8 * CHUNK, CHUNK).astype(BF16)], [((N_DEV, 8 * CHUNK, CHUNK), 0, 1)])])
    (pair_top,), (got_bot,) = _pair_sum("pair_sum_w_in_top", [dw_top], got_top, [in_spec], core,
                                        plans=[_sibling_plan([dw_bot], [in_spec])])
    (pair_bot,) = _pair_sum("pair_sum_w_in_bot", [dw_bot], got_bot, [in_spec], core)
    chip_shapes = [p.shape for p in (pair_top, pair_bot)]
    chips_in_flight, token = _split_start("grad_chip_exchange_start", _chip_routes, [pair_top, pair_bot], chip_shapes)
    grad_x, d_gmix = _dh1(dproj, w_in_full, x2, cot, norm_mix_g, after=[token])
    vec_names = ["norm_mix_g", "gm_ln_g", "gm_ln_b", "gm_bs", "norm_x_g", "norm_mem_g", "norm_final_g"]
    vec_g = [d_gmix, d_lng, d_lnb, d_bs, d_gx, d_gmem, d_gf]
    small = jnp.concatenate(vec_g + [sq_err, d_cw], axis=0)[None]
    small_in_flight, small_token = _split_start("small_all_gather_start", _gather_routes, [small], [(N_DEV, 16, D_MODEL)])
    flat_ws = lambda t: t.reshape(8 * CHUNK, CHUNK)
    updated, _ = _adamw(
        "adamw_early",
        [(big_w[k], big_m[k], big_v[k], got_k) for k, got_k in zip(early, parts_early)]
        + [(flat_ws(ws), flat_ws(m_gm_ws), flat_ws(v_gm_ws), ws_all)], steps=STEPS_ADAMW // 2)
    big_out = dict(zip(early, updated[:4]))
    ws_out = updated[4]
    (pair_top, pair_bot), (land_top, land_bot) = _split_wait(
        "grad_chip_exchange_wait", _chip_routes, chips_in_flight, [small_token] + [u[0] for u in updated])
    my_chip = (2 * px + py).astype(jnp.int32).reshape(1)
    (big_out["w_in"],), _ = _adamw(
        "adamw_w_in", [(big_w["w_in"], big_m["w_in"], big_v["w_in"], [land_top, land_bot], [pair_top, pair_bot])],
        steps=STEPS_ADAMW, own_slot=my_chip)
    (small,), (small_all,) = _split_wait("small_all_gather_wait", _gather_routes, small_in_flight, [big_out["w_in"][0]])
    row = lambda t: t.reshape(1, D_MODEL)
    vec_out, conv_out, loss_row = _update_small(
        dev.astype(jnp.int32).reshape(1),
        [norm_mix_g, gm_ln_g, gm_ln_b, row(gm_bs), norm_x_g, norm_mem_g, row(norm_final_g)],
        [m_norm_mix_g, m_gm_ln_g, m_gm_ln_b, row(m_gm_bs), m_norm_x_g, m_norm_mem_g, row(m_norm_final_g)],
        [v_norm_mix_g, v_gm_ln_g, v_gm_ln_b, row(v_gm_bs), v_norm_x_g, v_norm_mem_g, row(v_norm_final_g)],
        *(t.transpose(1, 0, 2) for t in (conv_w, m_conv_w, v_conv_w)), small, small_all)
    loss = loss_row[0, 0]

    def result(name, which):
        if name in big_out:
            return big_out[name][which][None]
        if name == "conv_w":
            return conv_out[which].transpose(1, 0, 2)
        if name == "gm_ws":
            return ws_out[which].reshape(1, 8, CHUNK, CHUNK)
        r = vec_out[vec_names.index(name)][which]
        if name == "gm_bs":
            return r.reshape(1, 8, CHUNK)
        if name == "norm_final_g":
            return r.reshape(D_MODEL)
        return r

    weights = ["norm_mix_g", "w_in", "conv_w", "gm_ln_g", "gm_ln_b", "gm_ws", "gm_bs", "w_out", "norm_x_g", "norm_mem_g", "w_q", "w_kv", "w_xo", "norm_final_g"]
    outs = [loss, grad_x[None]]
    for which in range(4):
        outs += [result(name, which) for name in weights]
    return tuple(outs)
```

```python
import functools
import math

import jax
import jax.numpy as jnp
from jax import lax
from jax.experimental import pallas as pl
from jax.experimental.pallas import tpu as pltpu

F32 = jnp.float32
BF16 = jnp.bfloat16
MESH = pl.DeviceIdType.MESH

D_MODEL = 1024
N_DEV = 8
N_CHIP = 4
CHUNK = 128
GROUPS = 7
IN_DIM = GROUPS * D_MODEL
SHARD_IN = IN_DIM // N_DEV
MIX_DIM = 2 * D_MODEL
X_HEADS = 4
X_HEAD_DIM = D_MODEL // X_HEADS
MEM_LEN = 256
EPS = 1e-6
GELU_K0 = math.sqrt(2.0 / math.pi)
GELU_K1 = 0.044715

ADAM_LR = 0.001
ADAM_B1 = 0.9
ADAM_B2 = 0.999
ADAM_EPS = 1e-08
ADAM_WD = 0.01
ADAM_STEP = 10

TM_INPROJ = 512
TM_ATTN = 512
SUB_ATTN = 256
TM_DH1 = 512
TM_DW = 512
TK_DW = 4096
BARRIER_ID = {"sibling": 1, "chips": 2, "all": 3}
STEPS_ADAMW = 4


def _dot(a, b):
    return lax.dot_general(a, b, (((1,), (0,)), ((), ())), preferred_element_type=F32)


def _dot_nt(a, b):
    return lax.dot_general(a, b, (((1,), (1,)), ((), ())), preferred_element_type=F32)


def _rows8(v):
    return jnp.sum(v.reshape(v.shape[0] // 8, 8, v.shape[1]), axis=0)


def _gelu(x, with_grad=True):
    x2 = x * x
    t = jnp.tanh(x * (GELU_K0 + (GELU_K0 * GELU_K1) * x2))
    half = 0.5 + 0.5 * t
    g = x * half
    if not with_grad:
        return g
    return g, half + ((0.5 * x) * (1.0 - t * t)) * (GELU_K0 + (3.0 * GELU_K0 * GELU_K1) * x2)


def _silu(z, with_grad=True):
    s = jax.nn.sigmoid(z)
    zs = z * s
    return (zs, s + zs * (1.0 - s)) if with_grad else zs


def _rms(v):
    r = lax.rsqrt(jnp.mean(v * v, axis=-1, keepdims=True) + EPS)
    return v * r, r


def _rms_bwd(dy_g, vh, r):
    return r * (dy_g - vh * jnp.mean(dy_g * vh, axis=-1, keepdims=True))


def _place():
    return lax.axis_index("x"), lax.axis_index("y"), lax.axis_index("c")


def _other_chips(x, y):
    return [(1 - x, y), (x, 1 - y), (1 - x, 1 - y)]


def _pow2_divisor(n):
    return n & (-n)


def _shard_view(ref, axis, size, d):
    start = d * size
    align = _pow2_divisor(size)
    if align > 1:
        start = pl.multiple_of(start, align)
    idx = [slice(None)] * len(ref.shape)
    idx[axis] = pl.ds(start, size)
    return ref.at[tuple(idx)]


class _Plan:
    def __init__(self, ins, out_shape, sems, start, finish, mid=None, aliases=None, peers="all"):
        self.ins, self.out_shape, self.sems = list(ins), list(out_shape), list(sems)
        self.start, self.mid, self.finish = start, mid, finish
        self.aliases = dict(aliases or {})
        self.peers = peers


def _gather_plan(shards, specs):
    n = len(shards)
    n_copy = 7

    def copies(ins, outs, sems):
        send_sems, recv_sems, local_sems = sems
        x, y, c = _place()
        me, sibling = (x, y, c), (x, y, 1 - c)
        chips = _other_chips(x, y)

        def block(a, p):
            _, axis, size = specs[a]
            return _shard_view(outs[a], axis, size, 4 * p[0] + 2 * p[1] + p[2])

        def copy(a, k, p, to, src=None):
            return pltpu.make_async_remote_copy(
                src_ref=block(a, p) if src is None else src,
                dst_ref=block(a, p),
                send_sem=send_sems.at[a * n_copy + k],
                recv_sem=recv_sems.at[a * n_copy + k],
                device_id=to,
                device_id_type=MESH,
            )

        def mine():
            return [pltpu.make_async_copy(ins[a], block(a, me), local_sems.at[a]) for a in range(n)]

        def first():
            return [cp for a in range(n) for cp in
                    [copy(a, 0, me, sibling, src=ins[a])] + [copy(a, 1 + j, me, (*chip, c), src=ins[a]) for j, chip in enumerate(chips)]]

        def landed():
            return [copy(a, 1 + j, (*chip, c), me) for j, chip in enumerate(chips) for a in range(n)]

        def passed():
            return [copy(a, 4 + j, (*chip, c), sibling) for j, chip in enumerate(chips) for a in range(n)]

        def last():
            return [cp for a in range(n) for cp in
                    [copy(a, 0, sibling, me)] + [copy(a, 4 + j, (*chip, 1 - c), me) for j, chip in enumerate(chips)]]

        return mine, first, landed, passed, last

    def start(ins, outs, sems):
        mine, first, _, _, _ = copies(ins, outs, sems)
        for cp in mine() + first():
            cp.start()

    def mid(ins, outs, sems):
        _, _, landed, passed, _ = copies(ins, outs, sems)
        for got, fwd in zip(landed(), passed()):
            got.wait_recv()
            fwd.start()

    def finish(ins, outs, sems):
        mine, first, _, passed, last = copies(ins, outs, sems)
        for cp in last():
            cp.wait_recv()
        for cp in first() + passed():
            cp.wait_send()
        for cp in mine():
            cp.wait()

    return _Plan(
        shards,
        [jax.ShapeDtypeStruct(full, s.dtype) for s, (full, _, _) in zip(shards, specs)],
        [pltpu.SemaphoreType.DMA((n * n_copy,)), pltpu.SemaphoreType.DMA((n * n_copy,)), pltpu.SemaphoreType.DMA((n,))],
        start, finish, mid,
    )


def _sibling_plan(grads, specs):
    n = len(grads)

    def shard_shape(a):
        axis, size = specs[a]
        shp = list(grads[a].shape)
        shp[axis] = size
        return tuple(shp)

    def copies(ins, outs, sems):
        send_sems, recv_sems = sems
        x, y, c = _place()
        return [
            pltpu.make_async_remote_copy(
                src_ref=_shard_view(ins[a], specs[a][0], specs[a][1], 2 * q + (1 - c)),
                dst_ref=outs[a].at[q],
                send_sem=send_sems.at[a * N_CHIP + q],
                recv_sem=recv_sems.at[a * N_CHIP + q],
                device_id=(x, y, 1 - c),
                device_id_type=MESH,
            )
            for a in range(n) for q in range(N_CHIP)
        ]

    def start(ins, outs, sems):
        for cp in copies(ins, outs, sems):
            cp.start()

    def finish(ins, outs, sems):
        for cp in copies(ins, outs, sems):
            cp.wait()

    return _Plan(
        grads,
        [jax.ShapeDtypeStruct((N_CHIP, *shard_shape(a)), grads[a].dtype) for a in range(n)],
        [pltpu.SemaphoreType.DMA((n * N_CHIP,)), pltpu.SemaphoreType.DMA((n * N_CHIP,))],
        start, finish, peers="sibling",
    )


def _chip_plan(parts, full_rows=None, row0=None, into=None):
    n = len(parts)
    full_rows = [p.shape[1] for p in parts] if full_rows is None else full_rows
    row0 = [0] * n if row0 is None else row0

    def copies(ins, outs, sems):
        send_sems, recv_sems, local_sems = sems
        x, y, c = _place()
        my_chip = 2 * x + y
        chips = _other_chips(x, y)

        def land(a, q):
            return outs[a].at[q, pl.ds(row0[a], parts[a].shape[1]), :]

        def mine():
            return [pltpu.make_async_copy(ins[a].at[my_chip], land(a, my_chip), local_sems.at[a]) for a in range(n)]

        def remote(sending):
            return [
                pltpu.make_async_remote_copy(
                    src_ref=ins[a].at[2 * chip[0] + chip[1]],
                    dst_ref=land(a, my_chip if sending else 2 * chip[0] + chip[1]),
                    send_sem=send_sems.at[a * 3 + j], recv_sem=recv_sems.at[a * 3 + j],
                    device_id=(*chip, c), device_id_type=MESH)
                for a in range(n) for j, chip in enumerate(chips)
            ]

        return mine, remote

    def start(ins, outs, sems):
        mine, remote = copies(ins, outs, sems)
        for cp in mine() + remote(True):
            cp.start()

    def finish(ins, outs, sems):
        mine, remote = copies(ins, outs, sems)
        for cp in remote(False):
            cp.wait_recv()
        for cp in remote(True):
            cp.wait_send()
        for cp in mine():
            cp.wait()

    return _Plan(
        list(parts) + (list(into) if into is not None else []),
        [jax.ShapeDtypeStruct((N_CHIP, full_rows[a], parts[a].shape[2]), parts[a].dtype) for a in range(n)],
        [pltpu.SemaphoreType.DMA((n * 3,)), pltpu.SemaphoreType.DMA((n * 3,)), pltpu.SemaphoreType.DMA((n,))],
        start, finish,
        aliases={n + a: a for a in range(n)} if into is not None else None, peers="chips",
    )


def _call(body, *, name, operands, out_shape, grid=(), in_specs=None, out_specs=None, scratch_shapes=(),
          plans=(), aliases=None, prefetch=(), start_after_body=False, pass_on_at=0.6, body_peers=None):
    operands, out_shape, scratch_shapes = list(operands), list(out_shape), list(scratch_shapes)
    n_pf, n_in, n_out, n_scr = len(prefetch), len(operands), len(out_shape), len(scratch_shapes)
    in_vmem = pl.BlockSpec(memory_space=pltpu.VMEM)
    in_hbm = pl.BlockSpec(memory_space=pl.ANY)
    in_specs = [in_vmem] * n_in if in_specs is None else list(in_specs)
    out_specs = [in_vmem] * n_out if out_specs is None else list(out_specs)
    io_alias = {n_pf + i: o for i, o in (aliases or {}).items()}
    pos_in, pos_out = n_pf + n_in, n_out
    for p in plans:
        for i, o in p.aliases.items():
            io_alias[pos_in + i] = pos_out + o
        pos_in += len(p.ins)
        pos_out += len(p.out_shape)
    steps = math.prod(grid) if grid else 1
    fractions = list(pass_on_at) if isinstance(pass_on_at, (list, tuple)) else [pass_on_at] * len(plans)
    mid_steps = [min(steps - 1, int(steps * f)) for f in fractions]
    patterns = {p.peers for p in plans} | ({body_peers} if body_peers else set())
    pattern = None if not patterns else patterns.pop() if len(patterns) == 1 else "all"

    def shake_hands():
        x, y, c = _place()
        peers = [(x, y, 1 - c)] if pattern in ("sibling", "all") else []
        peers += [(*chip, c) for chip in _other_chips(x, y)] if pattern in ("chips", "all") else []
        barrier = pltpu.get_barrier_semaphore()
        for peer in peers:
            pl.semaphore_signal(barrier, inc=1, device_id=peer, device_id_type=MESH)
        pl.semaphore_wait(barrier, len(peers))

    def wrapped(*refs):
        refs = list(refs)
        take = lambda k: [refs.pop(0) for _ in range(k)]
        pf = take(n_pf)
        ins, p_ins = take(n_in), [take(len(p.ins)) for p in plans]
        outs, p_outs = take(n_out), [take(len(p.out_shape)) for p in plans]
        scr, p_sems = take(n_scr), [take(len(p.sems)) for p in plans]
        step = 0
        for ax, g in enumerate(grid):
            step = step * g + pl.program_id(ax)

        def hook(kind, at):
            for p, a, b, c, when in zip(plans, p_ins, p_outs, p_sems, at):
                if getattr(p, kind):
                    run = functools.partial(getattr(p, kind), a, b, c)
                    if grid:
                        pl.when(step == when)(run)
                    else:
                        run()

        if pattern and grid:
            pl.when(step == 0)(shake_hands)
        elif pattern:
            shake_hands()
        if not start_after_body:
            hook("start", [0] * len(plans))
        if body is not None:
            body(*pf, *ins, *outs, *scr)
        if start_after_body:
            hook("start", [0] * len(plans))
        hook("mid", mid_steps)
        hook("finish", [steps - 1] * len(plans))

    layout = dict(
        grid=grid,
        in_specs=in_specs + [in_hbm] * sum(len(p.ins) for p in plans),
        out_specs=out_specs + [in_hbm] * sum(len(p.out_shape) for p in plans),
        scratch_shapes=scratch_shapes + [s for p in plans for s in p.sems],
    )
    if n_pf:
        layout = dict(grid_spec=pltpu.PrefetchScalarGridSpec(num_scalar_prefetch=n_pf, **layout))
    res = pl.pallas_call(
        wrapped,
        name=name,
        out_shape=out_shape + [o for p in plans for o in p.out_shape],
        input_output_aliases=io_alias,
        compiler_params=pltpu.CompilerParams(
            dimension_semantics=("arbitrary",) * len(grid) if grid else None,
            collective_id=BARRIER_ID[pattern] if pattern else None),
        **layout,
    )(*prefetch, *operands, *[a for p in plans for a in p.ins])
    res = list(res)
    outs, plan_outs = res[:n_out], []
    pos = n_out
    for p in plans:
        plan_outs.append(res[pos : pos + len(p.out_shape)])
        pos += len(p.out_shape)
    return outs, plan_outs


def _inproj(x, g, w_shard, order, plans=()):
    s = x.shape[0]
    tm = min(TM_INPROJ, s)
    nt = s // tm
    wide = 2 * SHARD_IN
    fwd_at = nt // 2

    def body(order_ref, x_ref, g_ref, w_hbm, proj_ref, h1t_ref, wfull_hbm,
             w_scr, h1_scr, send_sems, recv_sems, local_sems):
        k, i = pl.program_id(0), pl.program_id(1)
        px, py, c = _place()
        me, sibling = (px, py, c), (px, py, 1 - c)
        chips = _other_chips(px, py)

        def half(slot, core):
            return w_scr.at[slot, :, pl.ds(pl.multiple_of(core * SHARD_IN, 128), SHARD_IN)]

        def copy(n, slot, core, to, src=None):
            return pltpu.make_async_remote_copy(
                src_ref=half(slot, core) if src is None else src, dst_ref=half(slot, core),
                send_sem=send_sems.at[n], recv_sem=recv_sems.at[n], device_id=to, device_id_type=MESH)

        def mine():
            return pltpu.make_async_copy(w_hbm, half(0, c), local_sems.at[0])

        def first():
            return [copy(0, 0, c, sibling, src=w_hbm)] + [copy(1 + j, 1 + j, c, (*chips[j], c), src=w_hbm) for j in range(2)]

        def relay():
            to = (c * px + (1 - c) * (1 - px), c * (1 - py) + (1 - c) * py, c)
            return pltpu.make_async_remote_copy(
                src_ref=half(2 - c, c), dst_ref=half(3, c), send_sem=send_sems.at[3], recv_sem=recv_sems.at[3],
                device_id=to, device_id_type=MESH)

        def save(slot):
            cols = pl.ds(pl.multiple_of(order_ref[slot] * wide, 128), wide)
            return pltpu.make_async_copy(w_scr.at[slot], wfull_hbm.at[:, cols], local_sems.at[1 + slot])

        @pl.when((k == 0) & (i == 0))
        def _():
            mine().start()
            for cp in first():
                cp.start()
            mine().wait()
            copy(0, 0, 1 - c, me).wait_recv()

        for j in range(3):
            @pl.when((k == j + 1) & (i == 0))
            def _():
                copy(4 + j, 1 + j, 1 - c, me).wait_recv()

        @pl.when((k == 3) & (i == 0))
        def _():
            for slot in range(N_CHIP):
                save(slot).start()

        @pl.when(k == 0)
        def _():
            xh, _ = _rms(x_ref[...])
            h = xh * g_ref[...]
            h1t_ref[...] = h.T.astype(BF16)
            h1_scr[pl.ds(pl.multiple_of(i * tm, tm), tm), :] = h.astype(BF16)

        proj_ref[...] = _dot(h1_scr[pl.ds(pl.multiple_of(i * tm, tm), tm), :], w_scr[k]).astype(BF16)

        for j in range(3):
            @pl.when((k == 0) & (i == nt - 1) if j < 2 else (k == 2) & (i == fwd_at))
            def _():
                copy(1 + j, 1 + j, c, me).wait_recv()
                copy(4 + j, 1 + j, c, sibling).start()
                if j < 2:
                    pl.when(c == 1 - j)(lambda: relay().start())

        @pl.when((k == 3) & (i == nt - 1))
        def _():
            for cp in first() + [relay()] + [copy(4 + j, 1 + j, c, sibling) for j in range(3)]:
                cp.wait_send()
            for slot in range(N_CHIP):
                save(slot).wait()

    once = lambda k, i, order_ref: (jnp.where(k == 0, i, nt - 1), 0)
    return _call(
        body,
        name="inproj",
        prefetch=[order],
        operands=[x, g, w_shard],
        grid=(N_CHIP, nt),
        in_specs=[
            pl.BlockSpec((tm, D_MODEL), once),
            pl.BlockSpec((1, D_MODEL), lambda k, i, order_ref: (0, 0)),
            pl.BlockSpec(memory_space=pl.ANY),
        ],
        out_specs=[
            pl.BlockSpec((tm, wide), lambda k, i, order_ref: (i, order_ref[k])),
            pl.BlockSpec((D_MODEL, tm), lambda k, i, order_ref: (0, jnp.where(k == 0, i, nt - 1))),
            pl.BlockSpec(memory_space=pl.ANY),
        ],
        out_shape=[
            jax.ShapeDtypeStruct((s, IN_DIM), BF16),
            jax.ShapeDtypeStruct((D_MODEL, s), BF16),
            jax.ShapeDtypeStruct((D_MODEL, IN_DIM), BF16),
        ],
        scratch_shapes=[
            pltpu.VMEM((N_CHIP, D_MODEL, wide), BF16),
            pltpu.VMEM((s, D_MODEL), BF16),
            pltpu.SemaphoreType.DMA((7,)),
            pltpu.SemaphoreType.DMA((7,)),
            pltpu.SemaphoreType.DMA((1 + N_CHIP,)),
        ],
        plans=plans,
        start_after_body=True,
        pass_on_at=0.8,
        body_peers="all",
    )


def _rows_from_above(v, above):
    row = lax.broadcasted_iota(jnp.int32, (8, v.shape[1]), 0)
    r1, r2 = pltpu.roll(v, 1, 0), pltpu.roll(v, 2, 0)
    top1 = jnp.where(row == 0, above[7:8], r1[0:8])
    top2 = jnp.where(row == 0, above[6:7], jnp.where(row == 1, above[7:8], r2[0:8]))
    return jnp.concatenate([top1, r1[8:]], axis=0), jnp.concatenate([top2, r2[8:]], axis=0)


def _rows_from_below(v, below):
    n = v.shape[0]
    row = lax.broadcasted_iota(jnp.int32, (8, v.shape[1]), 0)
    r1, r2 = pltpu.roll(v, n - 1, 0), pltpu.roll(v, n - 2, 0)
    end1 = jnp.where(row == 7, below[0:1], r1[n - 8 :])
    end2 = jnp.where(row == 6, below[0:1], jnp.where(row == 7, below[1:2], r2[n - 8 :]))
    return jnp.concatenate([r1[: n - 8], end1], axis=0), jnp.concatenate([r2[: n - 8], end2], axis=0)


def _causal_mask():
    row = lax.broadcasted_iota(jnp.int32, (CHUNK, CHUNK), 0)
    col = lax.broadcasted_iota(jnp.int32, (CHUNK, CHUNK), 1)
    return row, row >= col


def _mixer_fwd(proj, conv_w, ln_g, ln_b, ws, bs_b, plans=(), pass_on_at=0.85):
    s = proj.shape[0]
    nt = s // CHUNK

    def body(proj_ref, cw_ref, lng_ref, lnb_ref, ws_ref, bsb_ref, cat_ref, pcar_ref):
        @pl.when(pl.program_id(0) == 0)
        def _():
            pcar_ref[...] = jnp.zeros_like(pcar_ref)

        row, tril = _causal_mask()
        for j in range(8):
            cs = slice(CHUNK * j, CHUNK * (j + 1))

            def grp(k):
                return proj_ref[:, k * D_MODEL + CHUNK * j : k * D_MODEL + CHUNK * (j + 1)].astype(F32)

            w = cw_ref[:, cs]
            p = grp(1) * grp(2)
            p1, p2 = _rows_from_above(p, pcar_ref[:, cs])
            pcar_ref[:, cs] = p[CHUNK - 8 :]
            cv = w[0:1] * p2 + w[1:2] * p1 + w[2:3] * p
            sa = _silu(grp(3), with_grad=False)
            cat_ref[:, cs] = ((grp(0) * cv) * sa).astype(BF16)
            gu = _gelu(grp(4), with_grad=False)
            gv = _gelu(grp(5), with_grad=False)
            dv = gv - jnp.mean(gv, axis=-1, keepdims=True)
            vn = dv * lax.rsqrt(jnp.mean(dv * dv, axis=-1, keepdims=True) + EPS)
            vn = vn * lng_ref[:, cs] + lnb_ref[:, cs]
            wc = jnp.where(tril, ws_ref[j], 0.0).astype(BF16)
            sp = _dot(wc, vn.astype(BF16)) + bsb_ref[j]
            sb = _silu(grp(6), with_grad=False)
            cat_ref[:, D_MODEL + CHUNK * j : D_MODEL + CHUNK * (j + 1)] = ((gu * sp) * sb).astype(BF16)

    whole = lambda shape: pl.BlockSpec(shape, lambda i: (0,) * len(shape))
    (cat,), plan_outs = _call(
        body,
        name="mixer_fwd",
        operands=[proj, conv_w, ln_g, ln_b, ws, bs_b],
        grid=(nt,),
        in_specs=[
            pl.BlockSpec((CHUNK, IN_DIM), lambda i: (i, 0)),
            whole((8, D_MODEL)),
            whole((1, D_MODEL)),
            whole((1, D_MODEL)),
            whole((8, CHUNK, CHUNK)),
            whole((8, CHUNK, CHUNK)),
        ],
        out_specs=[pl.BlockSpec((CHUNK, MIX_DIM), lambda i: (i, 0))],
        out_shape=[jax.ShapeDtypeStruct((s, MIX_DIM), BF16)],
        scratch_shapes=[pltpu.VMEM((8, D_MODEL), F32)],
        plans=plans,
        pass_on_at=pass_on_at,
    )
    return cat, plan_outs


def _mixer_bwd(proj, dcat, conv_w, ln_g, ln_b, ws, bs_b, plans=()):
    s = proj.shape[0]
    nt = s // CHUNK

    def body(proj_ref, halo_ref, dcat_ref, cw_ref, lng_ref, lnb_ref, ws_ref, bsb_ref,
             dproj_ref, dcw_ref, dlng_ref, dlnb_ref, dws_ref, dbs_ref,
             car_ref, acc_cw, acc_lng, acc_lnb, acc_bs, wc_scr, wct_scr):
        i = pl.program_id(0)
        tile = nt - 1 - i
        row, tril = _causal_mask()

        @pl.when(i == 0)
        def _():
            for j in range(8):
                wc = jnp.where(tril, ws_ref[j], 0.0)
                wc_scr[j] = wc.astype(BF16)
                wct_scr[j] = wc.T.astype(BF16)
            car_ref[...] = jnp.zeros_like(car_ref)
            acc_cw[...] = jnp.zeros_like(acc_cw)
            acc_lng[...] = jnp.zeros_like(acc_lng)
            acc_lnb[...] = jnp.zeros_like(acc_lnb)
            acc_bs[...] = jnp.zeros_like(acc_bs)
            dws_ref[...] = jnp.zeros_like(dws_ref)

        has_prev = jnp.where(tile > 0, 1.0, 0.0).astype(F32)
        for j in range(8):
            cs = slice(CHUNK * j, CHUNK * (j + 1))

            def col(k):
                return slice(k * D_MODEL + CHUNK * j, k * D_MODEL + CHUNK * (j + 1))

            def grp(k):
                return proj_ref[:, col(k)].astype(F32)

            w = cw_ref[:, cs]
            gb, gc, xa = grp(0), grp(1), grp(2)
            p = gc * xa
            pprev = halo_ref[:, col(1)].astype(F32) * halo_ref[:, col(2)].astype(F32) * has_prev
            p1, p2 = _rows_from_above(p, pprev[8:16])
            cv = w[0:1] * p2 + w[1:2] * p1 + w[2:3] * p
            za = grp(3)
            sa, dsa = _silu(za)
            da = dcat_ref[:, cs].astype(F32)
            da_gb = da * gb
            dproj_ref[:, col(0)] = ((da * sa) * cv).astype(BF16)
            dproj_ref[:, col(3)] = ((da_gb * cv) * dsa).astype(BF16)
            dcv = da_gb * sa
            d1, d2 = _rows_from_below(dcv, car_ref[:, cs])
            car_ref[:, cs] = dcv[0:8]
            dp = w[2:3] * dcv + w[1:2] * d1 + w[0:1] * d2
            dproj_ref[:, col(1)] = (dp * xa).astype(BF16)
            dproj_ref[:, col(2)] = (dp * gc).astype(BF16)
            acc_cw[0, :, cs] += _rows8(dcv * p2)
            acc_cw[1, :, cs] += _rows8(dcv * p1)
            acc_cw[2, :, cs] += _rows8(dcv * p)
            gu, dgu = _gelu(grp(4))
            gv, dgv = _gelu(grp(5))
            dv = gv - jnp.mean(gv, axis=-1, keepdims=True)
            rstd = lax.rsqrt(jnp.mean(dv * dv, axis=-1, keepdims=True) + EPS)
            vnh = dv * rstd
            lng = lng_ref[:, cs]
            vnb = (vnh * lng + lnb_ref[:, cs]).astype(BF16)
            sp = _dot(wc_scr[j], vnb) + bsb_ref[j]
            zb = grp(6)
            sb, dsb = _silu(zb)
            db = dcat_ref[:, D_MODEL + CHUNK * j : D_MODEL + CHUNK * (j + 1)].astype(F32)
            db_sp, db_gu = db * sp, db * gu
            dproj_ref[:, col(4)] = ((db_sp * sb) * dgu).astype(BF16)
            dproj_ref[:, col(6)] = ((db_sp * gu) * dsb).astype(BF16)
            dsp = db_gu * sb
            acc_bs[j] += dsp
            dspb = dsp.astype(BF16)
            dws_ref[j] += _dot_nt(dspb, vnb)
            dvn = _dot(wct_scr[j], dspb)
            acc_lnb[:, cs] += _rows8(dvn)
            acc_lng[:, cs] += _rows8(dvn * vnh)
            dvh = dvn * lng
            dgvv = rstd * (dvh - jnp.mean(dvh, axis=-1, keepdims=True) - vnh * jnp.mean(dvh * vnh, axis=-1, keepdims=True))
            dproj_ref[:, col(5)] = (dgvv * dgv).astype(BF16)

        @pl.when(i == nt - 1)
        def _():
            dlng_ref[...] = jnp.sum(acc_lng[...], axis=0, keepdims=True)
            dlnb_ref[...] = jnp.sum(acc_lnb[...], axis=0, keepdims=True)
            dcw_ref[...] = jnp.zeros_like(dcw_ref)
            for k in range(3):
                dcw_ref[k : k + 1, :] = jnp.sum(acc_cw[k], axis=0, keepdims=True)
            for j in range(8):
                dws_ref[j] = jnp.where(tril, dws_ref[j], 0.0)
                dbs_ref[:, CHUNK * j : CHUNK * (j + 1)] = jnp.sum(acc_bs[j].T, axis=0, keepdims=True)

    whole = lambda shape: pl.BlockSpec(shape, lambda i: (0,) * len(shape))
    halo_rows = 16
    per = CHUNK // halo_rows
    return _call(
        body,
        name="mixer_bwd",
        operands=[proj, proj, dcat, conv_w, ln_g, ln_b, ws, bs_b],
        plans=plans,
        grid=(nt,),
        in_specs=[
            pl.BlockSpec((CHUNK, IN_DIM), lambda i: (nt - 1 - i, 0)),
            pl.BlockSpec((halo_rows, IN_DIM), lambda i: (jnp.maximum((nt - 1 - i) * per - 1, 0), 0)),
            pl.BlockSpec((CHUNK, MIX_DIM), lambda i: (nt - 1 - i, 0)),
            whole((8, D_MODEL)),
            whole((1, D_MODEL)),
            whole((1, D_MODEL)),
            whole((8, CHUNK, CHUNK)),
            whole((8, CHUNK, CHUNK)),
        ],
        out_specs=[
            pl.BlockSpec((CHUNK, IN_DIM), lambda i: (nt - 1 - i, 0)),
            whole((8, D_MODEL)),
            whole((1, D_MODEL)),
            whole((1, D_MODEL)),
            whole((8, CHUNK, CHUNK)),
            whole((1, D_MODEL)),
        ],
        out_shape=[
            jax.ShapeDtypeStruct((s, IN_DIM), BF16),
            jax.ShapeDtypeStruct((8, D_MODEL), F32),
            jax.ShapeDtypeStruct((1, D_MODEL), F32),
            jax.ShapeDtypeStruct((1, D_MODEL), F32),
            jax.ShapeDtypeStruct((8, CHUNK, CHUNK), F32),
            jax.ShapeDtypeStruct((1, D_MODEL), F32),
        ],
        scratch_shapes=[
            pltpu.VMEM((8, D_MODEL), F32),
            pltpu.VMEM((3, 8, D_MODEL), F32),
            pltpu.VMEM((8, D_MODEL), F32),
            pltpu.VMEM((8, D_MODEL), F32),
            pltpu.VMEM((8, CHUNK, CHUNK), F32),
            pltpu.VMEM((8, CHUNK, CHUNK), BF16),
            pltpu.VMEM((8, CHUNK, CHUNK), BF16),
        ],
    )


def _kv_fwd(mem, g_mem, w_kv):
    def body(mem_ref, g_ref, w_ref, k_ref, vv_ref, mt_ref):
        mh, _ = _rms(mem_ref[...])
        m = mh * g_ref[...]
        kv = _dot(m.astype(BF16), w_ref[...])
        k_ref[...] = kv[:, :D_MODEL].astype(BF16)
        vv_ref[...] = kv[:, D_MODEL:].astype(BF16)
        mt_ref[...] = m.T.astype(BF16)

    return pl.pallas_call(
        body,
        name="kv_fwd",
        out_shape=[
            jax.ShapeDtypeStruct((MEM_LEN, D_MODEL), BF16),
            jax.ShapeDtypeStruct((MEM_LEN, D_MODEL), BF16),
            jax.ShapeDtypeStruct((D_MODEL, MEM_LEN), BF16),
        ],
    )(mem, g_mem, w_kv)


def _kv_bwd(mem, w_kv, mt, dk, dvv):
    def body(mem_ref, w_ref, mt_ref, dk_ref, dvv_ref, dw_ref, dg_ref):
        dkv = jnp.concatenate([dk_ref[...], dvv_ref[...]], axis=1).astype(BF16)
        dw_ref[...] = _dot(mt_ref[...], dkv).astype(BF16)
        dm = _dot_nt(dkv, w_ref[...])
        mh, _ = _rms(mem_ref[...])
        dg_ref[...] = jnp.sum(dm * mh, axis=0, keepdims=True)

    return pl.pallas_call(
        body,
        name="kv_bwd",
        out_shape=[jax.ShapeDtypeStruct((D_MODEL, 2 * D_MODEL), BF16), jax.ShapeDtypeStruct((1, D_MODEL), F32)],
    )(mem, w_kv, mt, dk, dvv)


def _attn(x, cat, target, w_out, w_q, w_xo, k, vv, g_x, g_f):
    s = x.shape[0]
    tm = min(TM_ATTN, s)
    sub = min(SUB_ATTN, tm)
    nt = s // tm
    scale = 1.0 / math.sqrt(X_HEAD_DIM)

    def body(x_ref, cat_ref, t_ref, wout_ref, wq_ref, wxo_ref, k_ref, vv_ref, gx_ref, gf_ref,
             dcat_ref, actt_ref, cot_ref,
             dk_ref, dvv_ref, dgf_ref, dgx_ref, loss_ref,
             p_scr, q_scr, o_scr, dq_scr, acc_gf, acc_gx, acc_loss):
        i = pl.program_id(0)

        @pl.when(i == 0)
        def _():
            dk_ref[...] = jnp.zeros_like(dk_ref)
            dvv_ref[...] = jnp.zeros_like(dvv_ref)
            acc_gf[...] = jnp.zeros_like(acc_gf)
            acc_gx[...] = jnp.zeros_like(acc_gx)
            acc_loss[...] = jnp.zeros_like(acc_loss)

        subs = [slice(u * sub, (u + 1) * sub) for u in range(tm // sub)]
        heads = [slice(X_HEAD_DIM * h, X_HEAD_DIM * (h + 1)) for h in range(X_HEADS)]
        gx, gf = gx_ref[...], gf_ref[...]
        x1, x1h, r2 = [], [], []
        for rs in subs:
            cat = cat_ref[rs, :]
            actt_ref[0:MIX_DIM, rs] = cat.astype(F32).T.astype(BF16)
            v = x_ref[rs, :] + _dot(cat, wout_ref[...])
            vh, r = _rms(v)
            h2 = vh * gx
            actt_ref[MIX_DIM : MIX_DIM + D_MODEL, rs] = h2.T.astype(BF16)
            q_scr[rs, :] = _dot(h2.astype(BF16), wq_ref[...]).astype(BF16)
            x1.append(v), x1h.append(vh), r2.append(r)
        for rs in subs:
            for hs in heads:
                sc = _dot_nt(q_scr[rs, hs], k_ref[:, hs]) * scale
                e = jnp.exp(sc - jnp.max(sc, axis=-1, keepdims=True))
                p = e / jnp.sum(e, axis=-1, keepdims=True)
                p_scr[rs, hs] = p
                o_scr[rs, hs] = _dot(p.astype(BF16), vv_ref[:, hs])
        dx2, sq, d_gf = [], None, None
        for u, rs in enumerate(subs):
            o = o_scr[rs, :]
            actt_ref[MIX_DIM + D_MODEL :, rs] = o.T.astype(BF16)
            x2 = x1[u] + _dot(o.astype(BF16), wxo_ref[...])
            x2h, r3 = _rms(x2)
            err = x2h * gf - t_ref[rs, :]
            dy = err * (1.0 / D_MODEL)
            sq = _rows8(err * err) if sq is None else sq + _rows8(err * err)
            d_gf = _rows8(dy * x2h) if d_gf is None else d_gf + _rows8(dy * x2h)
            dx2.append(_rms_bwd(dy * gf, x2h, r3))
        acc_loss[...] += sq
        acc_gf[...] += d_gf
        d_vv, d_k = [None] * X_HEADS, [None] * X_HEADS
        for u, rs in enumerate(subs):
            dx2b = dx2[u].astype(BF16)
            cot_ref[2, rs, :] = dx2b
            dob = _dot_nt(dx2b, wxo_ref[...]).astype(BF16)
            for h, hs in enumerate(heads):
                p = p_scr[rs, hs]
                dp = _dot_nt(dob[:, hs], vv_ref[:, hs])
                ds = p * (dp - jnp.sum(dp * p, axis=-1, keepdims=True)) * scale
                dq_scr[rs, hs] = _dot(ds.astype(BF16), k_ref[:, hs]).astype(BF16)
                dvv_h = _dot(p.T.astype(BF16), dob[:, hs])
                dk_h = _dot(ds.T.astype(BF16), q_scr[rs, hs])
                d_vv[h] = dvv_h if d_vv[h] is None else d_vv[h] + dvv_h
                d_k[h] = dk_h if d_k[h] is None else d_k[h] + dk_h
        for h, hs in enumerate(heads):
            dvv_ref[:, hs] += d_vv[h]
            dk_ref[:, hs] += d_k[h]
        d_gx = None
        for u, rs in enumerate(subs):
            dqb = dq_scr[rs, :]
            cot_ref[1, rs, :] = dqb
            dh2 = _dot_nt(dqb, wq_ref[...])
            d_gx = _rows8(dh2 * x1h[u]) if d_gx is None else d_gx + _rows8(dh2 * x1h[u])
            dx1 = dx2[u] + _rms_bwd(dh2 * gx, x1h[u], r2[u])
            dx1b = dx1.astype(BF16)
            cot_ref[0, rs, :] = dx1b
            dcat_ref[rs, :] = _dot_nt(dx1b, wout_ref[...]).astype(BF16)
        acc_gx[...] += d_gx

        @pl.when(i == nt - 1)
        def _():
            dgf_ref[...] = jnp.sum(acc_gf[...], axis=0, keepdims=True)
            dgx_ref[...] = jnp.sum(acc_gx[...], axis=0, keepdims=True)
            loss_ref[...] = jnp.sum(acc_loss[...], axis=0, keepdims=True)

    whole = lambda shape: pl.BlockSpec(shape, lambda i: (0,) * len(shape))
    once = lambda shape: pl.BlockSpec(shape, lambda i: (0,) * len(shape), pipeline_mode=pl.Buffered(1))
    rows = lambda width: pl.BlockSpec((tm, width), lambda i: (i, 0))
    cols = lambda height: pl.BlockSpec((height, tm), lambda i: (0, i))
    vec = jax.ShapeDtypeStruct((1, D_MODEL), F32)
    return pl.pallas_call(
        body,
        name="attn",
        grid=(nt,),
        in_specs=[
            rows(D_MODEL), rows(MIX_DIM), rows(D_MODEL),
            once((MIX_DIM, D_MODEL)), once((D_MODEL, D_MODEL)), once((D_MODEL, D_MODEL)),
            once((MEM_LEN, D_MODEL)), once((MEM_LEN, D_MODEL)),
            whole((1, D_MODEL)), whole((1, D_MODEL)),
        ],
        out_specs=[
            rows(MIX_DIM), cols(2 * MIX_DIM),
            pl.BlockSpec((3, tm, D_MODEL), lambda i: (0, i, 0)),
            whole((MEM_LEN, D_MODEL)), whole((MEM_LEN, D_MODEL)),
            whole((1, D_MODEL)), whole((1, D_MODEL)), whole((1, D_MODEL)),
        ],
        out_shape=[
            jax.ShapeDtypeStruct((s, MIX_DIM), BF16),
            jax.ShapeDtypeStruct((2 * MIX_DIM, s), BF16),
            jax.ShapeDtypeStruct((3, s, D_MODEL), BF16),
            jax.ShapeDtypeStruct((MEM_LEN, D_MODEL), F32),
            jax.ShapeDtypeStruct((MEM_LEN, D_MODEL), F32),
            vec, vec, vec,
        ],
        scratch_shapes=[
            pltpu.VMEM((tm, D_MODEL), F32),
            pltpu.VMEM((tm, D_MODEL), BF16),
            pltpu.VMEM((tm, D_MODEL), F32),
            pltpu.VMEM((tm, D_MODEL), BF16),
            pltpu.VMEM((8, D_MODEL), F32),
            pltpu.VMEM((8, D_MODEL), F32),
            pltpu.VMEM((8, D_MODEL), F32),
        ],
        compiler_params=pltpu.CompilerParams(dimension_semantics=("arbitrary",)),
    )(x, cat, target, w_out, w_q, w_xo, k, vv, g_x, g_f)


def _matmul_bf16(name, a, b, rows=None, plans=()):
    first, m = (0, a.shape[0]) if rows is None else rows
    kk = a.shape[1]
    n = b.shape[1]
    tm, tn, tk = min(m, TM_DW), min(n, D_MODEL), min(kk, TK_DW)
    nk = kk // tk
    i0 = first // tm

    def body(a_ref, b_ref, o_ref, *acc):
        prod = _dot(a_ref[...], b_ref[...])
        if nk == 1:
            o_ref[...] = prod.astype(BF16)
            return
        (acc_ref,) = acc
        kid = pl.program_id(2)

        @pl.when(kid == 0)
        def _():
            acc_ref[...] = prod

        @pl.when(kid > 0)
        def _():
            acc_ref[...] += prod

        @pl.when(kid == nk - 1)
        def _():
            o_ref[...] = acc_ref[...].astype(BF16)

    (out,), plan_outs = _call(
        body,
        name=name,
        operands=[a, b],
        grid=(m // tm, n // tn, nk),
        in_specs=[pl.BlockSpec((tm, tk), lambda i, j, k: (i + i0, k)), pl.BlockSpec((tk, tn), lambda i, j, k: (k, j))],
        out_specs=[pl.BlockSpec((tm, tn), lambda i, j, k: (i, j))],
        out_shape=[jax.ShapeDtypeStruct((m, n), BF16)],
        scratch_shapes=[pltpu.VMEM((tm, tn), F32)] if nk > 1 else [],
        plans=plans,
    )
    return out, plan_outs


def _dw_attn(actt, cot, plans=()):
    s = actt.shape[1]
    tm = TM_DW
    counts = [MIX_DIM // tm, D_MODEL // tm, D_MODEL // tm]
    starts = [0, counts[0], counts[0] + counts[1]]
    which = lambda i: jnp.where(i < starts[1], 0, jnp.where(i < starts[2], 1, 2))

    def body(a_ref, b_ref, *outs):
        i = pl.program_id(0)
        prod = _dot(a_ref[...], b_ref[0]).astype(BF16)
        for m in range(3):
            @pl.when(which(i) == m)
            def _():
                outs[m][...] = prod

    out_specs = [pl.BlockSpec((tm, D_MODEL), lambda i, m=m: (jnp.clip(i - starts[m], 0, counts[m] - 1), 0)) for m in range(3)]
    out_shape = [jax.ShapeDtypeStruct((counts[m] * tm, D_MODEL), BF16) for m in range(3)]
    outs, plan_outs = _call(
        body,
        name="dw_attn",
        operands=[actt, cot],
        grid=(sum(counts),),
        in_specs=[pl.BlockSpec((tm, s), lambda i: (i, 0)), pl.BlockSpec((1, s, D_MODEL), lambda i: (which(i), 0, 0))],
        out_specs=out_specs,
        out_shape=out_shape,
        plans=plans,
    )
    return outs, plan_outs


def _dh1(dproj, w_in, x, dx1, g, after=()):
    s = x.shape[0]
    tm = min(TM_DH1, s)
    nt = s // tm

    def body(dp_ref, w_ref, x_ref, dx1_ref, g_ref, *rest):
        gx_ref, dg_ref, acc_g = rest[-3:]
        i = pl.program_id(0)

        @pl.when(i == 0)
        def _():
            acc_g[...] = jnp.zeros_like(acc_g)

        dh1 = _dot_nt(dp_ref[...], w_ref[...])
        xh, r = _rms(x_ref[...])
        acc_g[...] += _rows8(dh1 * xh)
        gx_ref[...] = dx1_ref[0].astype(F32) + _rms_bwd(dh1 * g_ref[...], xh, r)

        @pl.when(i == nt - 1)
        def _():
            dg_ref[...] = jnp.sum(acc_g[...], axis=0, keepdims=True)

    rows = pl.BlockSpec((tm, D_MODEL), lambda i: (i, 0))
    outs, _ = _call(
        body,
        name="dh1",
        operands=[dproj, w_in, x, dx1, g] + list(after),
        grid=(nt,),
        in_specs=[
            pl.BlockSpec((tm, IN_DIM), lambda i: (i, 0)),
            pl.BlockSpec((D_MODEL, IN_DIM), lambda i: (0, 0), pipeline_mode=pl.Buffered(1)),
            rows,
            pl.BlockSpec((1, tm, D_MODEL), lambda i: (0, i, 0)),
            pl.BlockSpec((1, D_MODEL), lambda i: (0, 0)),
        ] + [pl.BlockSpec(memory_space=pl.ANY)] * len(after),
        out_specs=[rows, pl.BlockSpec((1, D_MODEL), lambda i: (0, 0))],
        out_shape=[jax.ShapeDtypeStruct((s, D_MODEL), F32), jax.ShapeDtypeStruct((1, D_MODEL), F32)],
        scratch_shapes=[pltpu.VMEM((8, D_MODEL), F32)],
    )
    return outs


def _pair_sum(name, grads, gots, specs, core, plans=()):
    n = len(grads)
    shards = []
    for grad, (axis, size) in zip(grads, specs):
        shard = list(grad.shape)
        shard[axis] = size
        shards.append(tuple(shard))

    def body(core_ref, *refs):
        for g_ref, got_ref, o_ref in zip(refs[:n], refs[n : 2 * n], refs[2 * n :]):
            o_ref[0] = (g_ref[...].astype(F32) + got_ref[0].astype(F32)).astype(BF16)

    def own(shard, axis):
        if axis == 1:
            return pl.BlockSpec(shard, lambda q, core_ref: (0, 2 * q + core_ref[0]))
        return pl.BlockSpec(shard, lambda q, core_ref: (2 * q + core_ref[0], 0))

    slots = [pl.BlockSpec((1, *shard), lambda q, core_ref: (q, 0, 0)) for shard in shards]
    outs, plan_outs = _call(
        body,
        name=name,
        prefetch=[core],
        operands=list(grads) + list(gots),
        grid=(N_CHIP,),
        in_specs=[own(shard, axis) for shard, (axis, _) in zip(shards, specs)] + slots,
        out_specs=slots,
        out_shape=[jax.ShapeDtypeStruct((N_CHIP, *shard), BF16) for shard in shards],
        plans=plans,
    )
    return (outs, plan_outs) if plans else outs


def _adam_step(w, m, v, g):
    c1 = 1.0 / (1.0 - ADAM_B1**ADAM_STEP)
    c2 = 1.0 / (1.0 - ADAM_B2**ADAM_STEP)
    m_new = ADAM_B1 * m + (1.0 - ADAM_B1) * g
    v_new = ADAM_B2 * v + (1.0 - ADAM_B2) * (g * g)
    return -ADAM_LR * ((m_new * c1) / (jnp.sqrt(v_new * c2) + ADAM_EPS) + ADAM_WD * w), m_new, v_new


def _update_small(dev, vec_w, vec_m, vec_v, conv_w, conv_m, conv_v, small, landed):
    n_vec = len(vec_w)

    def body(dev_ref, *refs):
        refs = list(refs)
        take = lambda k: [refs.pop(0) for _ in range(k)]
        w_refs, m_refs, v_refs = take(n_vec), take(n_vec), take(n_vec)
        cw_ref, cm_ref, cv_ref, small_ref, land_ref, small_conv_ref, land_conv_ref = take(7)
        vec_outs, conv_outs, (loss_ref,) = take(4 * n_vec), take(4), take(1)

        def total(own_ref, others_ref):
            acc = None
            for d in range(N_DEV):
                part = jnp.where(dev_ref[0] == d, own_ref[0], others_ref[d])
                acc = part if acc is None else acc + part
            return acc

        g_all = total(small_ref, land_ref)
        for i in range(n_vec):
            g = g_all[i : i + 1]
            delta, m_new, v_new = _adam_step(w_refs[i][...], m_refs[i][...], v_refs[i][...], g)
            for ref, val in zip(vec_outs[4 * i : 4 * i + 4], (g, delta, m_new, v_new)):
                ref[...] = val
        g_conv = total(small_conv_ref, land_conv_ref)
        for tap in range(3):
            g = g_conv[tap : tap + 1]
            delta, m_new, v_new = _adam_step(cw_ref[tap], cm_ref[tap], cv_ref[tap], g)
            for ref, val in zip(conv_outs, (g, delta, m_new, v_new)):
                ref[tap] = val
        loss_ref[...] = jnp.broadcast_to(0.5 * jnp.sum(g_all[7:8], axis=-1, keepdims=True) / D_MODEL, loss_ref.shape)

    vec_spec = pl.BlockSpec((1, D_MODEL), lambda i, dev_ref: (0, 0))
    conv_spec = pl.BlockSpec((3, 1, CHUNK), lambda i, dev_ref: (0, 0, 0))
    vec_shape = jax.ShapeDtypeStruct((1, D_MODEL), F32)
    conv_shape = jax.ShapeDtypeStruct((3, 1, CHUNK), F32)
    outs, _ = _call(
        body,
        name="update_small",
        prefetch=[dev],
        operands=list(vec_w) + list(vec_m) + list(vec_v) + [conv_w, conv_m, conv_v, small, landed, small, landed],
        grid=(1,),
        in_specs=[vec_spec] * (3 * n_vec) + [conv_spec] * 3 + [
            pl.BlockSpec((1, 16, D_MODEL), lambda i, dev_ref: (0, 0, 0)),
            pl.BlockSpec((N_DEV, 16, D_MODEL), lambda i, dev_ref: (0, 0, 0)),
            pl.BlockSpec((1, 8, CHUNK), lambda i, dev_ref: (0, 1, dev_ref[0])),
            pl.BlockSpec((N_DEV, 8, CHUNK), lambda i, dev_ref: (0, 1, dev_ref[0])),
        ],
        out_specs=[vec_spec] * (4 * n_vec) + [conv_spec] * 4 + [pl.BlockSpec((1, CHUNK), lambda i, dev_ref: (0, 0))],
        out_shape=[vec_shape] * (4 * n_vec) + [conv_shape] * 4 + [jax.ShapeDtypeStruct((1, CHUNK), F32)],
    )
    return [outs[4 * i : 4 * i + 4] for i in range(n_vec)], outs[4 * n_vec : 4 * n_vec + 4], outs[-1]


def _adamw(name, groups, steps, own_slot=None, plans=()):
    lists = lambda p: list(p) if isinstance(p, (list, tuple)) else [p]
    groups = [(g[0], g[1], g[2], lists(g[3]), lists(g[4]) if len(g) > 4 else []) for g in groups]
    counts = [3 + len(g[3]) + len(g[4]) for g in groups]
    use_slot = own_slot is not None

    def first_tiles(w, parts):
        firsts, first = [], 0
        for p in parts:
            firsts.append(first)
            first += p.shape[1] // (w.shape[0] // steps)
        return firsts

    def update(t, slot_ref, firsts, w_ref, m_ref, v_ref, p_refs, o_refs, g_ref, d_ref, nm_ref, nv_ref):
        g = None
        for k, p_ref in enumerate(p_refs):
            gk = None
            for q in range(p_ref.shape[0]):
                part = p_ref[q].astype(F32)
                if o_refs:
                    part = jnp.where(slot_ref[0] == q, o_refs[k][0].astype(F32), part)
                gk = part if gk is None else gk + part
            g = gk if g is None else jnp.where(t >= firsts[k], gk, g)
        g_ref[...] = g
        d_ref[...], nm_ref[...], nv_ref[...] = _adam_step(w_ref[...], m_ref[...], v_ref[...], g)

    def body(*refs):
        refs = list(refs)
        slot_ref = refs.pop(0) if use_slot else None
        t = pl.program_id(0)
        ins, outs = refs[: sum(counts)], refs[sum(counts) :]
        for i, (w, _, _, parts, own) in enumerate(groups):
            mine = ins[sum(counts[:i]) : sum(counts[: i + 1])]
            update(t, slot_ref, first_tiles(w, parts), *mine[:3], mine[3 : 3 + len(parts)], mine[3 + len(parts) :],
                   *outs[4 * i : 4 * i + 4])

    operands, in_specs, out_specs, out_shape = [], [], [], []
    for w, m, v, parts, own in groups:
        r, cdim = w.shape
        tr = r // steps
        firsts = first_tiles(w, parts)

        def tile_of(k, t, firsts=firsts, parts=parts, tr=tr):
            return jnp.clip(t - firsts[k], 0, parts[k].shape[1] // tr - 1)

        blk = pl.BlockSpec((tr, cdim), lambda t, *_: (t, 0))
        operands += [w, m, v] + parts + own
        in_specs += [blk, blk, blk]
        in_specs += [pl.BlockSpec((p.shape[0], tr, cdim), lambda t, *_, k=k, tile_of=tile_of: (0, tile_of(k, t), 0)) for k, p in enumerate(parts)]
        in_specs += [pl.BlockSpec((1, tr, cdim), lambda t, slot_ref, k=k, tile_of=tile_of: (slot_ref[0], tile_of(k, t), 0)) for k in range(len(own))]
        out_specs += [blk] * 4
        out_shape += [jax.ShapeDtypeStruct((r, cdim), F32)] * 4
    outs, plan_outs = _call(
        body,
        name=name,
        prefetch=[own_slot] if use_slot else [],
        operands=operands,
        grid=(steps,),
        in_specs=in_specs,
        out_specs=out_specs,
        out_shape=out_shape,
        plans=plans,
    )
    return [outs[4 * i : 4 * i + 4] for i in range(len(groups))], plan_outs


def _chip_routes(n):
    x, y, c = _place()
    my_chip = 2 * x + y
    return [(a, (*chip, c), 2 * chip[0] + chip[1], my_chip, 2 * chip[0] + chip[1])
            for a in range(n) for chip in _other_chips(x, y)]


def _gather_routes(n):
    x, y, c = _place()
    flip = lambda v, f: 1 - v if f else v
    peers = [(flip(x, fx), flip(y, fy), flip(c, fc)) for fx in (0, 1) for fy in (0, 1) for fc in (0, 1) if fx + fy + fc]
    return [(0, p, 0, 4 * x + 2 * y + c, 4 * p[0] + 2 * p[1] + p[2]) for p in peers]


_chip_routes.copies_per_array = N_CHIP - 1
_gather_routes.copies_per_array = N_DEV - 1


def _split_start(name, routes, srcs, land_shapes):
    n = len(srcs)
    hbm = pl.BlockSpec(memory_space=pltpu.HBM)
    sem = pl.BlockSpec(memory_space=pltpu.SEMAPHORE)

    def body(*refs):
        src_refs, land_refs = refs[:n], refs[n : 2 * n]
        send_sems, recv_sems = refs[2 * n], refs[2 * n + 1]
        token = refs[-1]
        for k, (a, peer, src_slot, there, _) in enumerate(routes(n)):
            pltpu.make_async_remote_copy(
                src_ref=src_refs[a].at[src_slot], dst_ref=land_refs[a].at[there],
                send_sem=send_sems.at[k], recv_sem=recv_sems.at[k], device_id=peer, device_id_type=MESH).start()
        token[...] = jnp.zeros_like(token)

    in_hbm = [pltpu.with_memory_space_constraint(p, pltpu.HBM) for p in srcs]
    in_hbm += [pltpu.with_memory_space_constraint(lax.empty(shape, p.dtype), pltpu.HBM) for p, shape in zip(srcs, land_shapes)]
    n_copies = routes.copies_per_array * n
    res = pl.pallas_call(
        body,
        name=name,
        out_shape=[pltpu.SemaphoreType.DMA((n_copies,)), pltpu.SemaphoreType.DMA((n_copies,))]
        + [pltpu.HBM(p.shape, p.dtype) for p in srcs] + [pltpu.HBM(shape, p.dtype) for p, shape in zip(srcs, land_shapes)]
        + [jax.ShapeDtypeStruct((8, CHUNK), F32)],
        in_specs=[hbm] * (2 * n),
        out_specs=[sem, sem] + [hbm] * (2 * n) + [pl.BlockSpec(memory_space=pltpu.VMEM)],
        input_output_aliases={i: 2 + i for i in range(2 * n)},
        compiler_params=pltpu.CompilerParams(has_side_effects=pltpu.SideEffectType.DATAFLOW_SIDE_EFFECTING),
    )(*in_hbm)
    return res[:-1], res[-1]


def _split_wait(name, routes, in_flight, after):
    n = (len(in_flight) - 2) // 2
    hbm = pl.BlockSpec(memory_space=pltpu.HBM)
    sem = pl.BlockSpec(memory_space=pltpu.SEMAPHORE)
    send_sems, recv_sems, *bufs = in_flight

    def body(*refs):
        src_refs, land_refs = refs[:n], refs[n : 2 * n]
        send_ref, recv_ref = refs[2 * n], refs[2 * n + 1]
        for k, (a, peer, src_slot, _, here) in enumerate(routes(n)):
            cp = pltpu.make_async_remote_copy(
                src_ref=src_refs[a].at[src_slot], dst_ref=land_refs[a].at[here],
                send_sem=send_ref.at[k], recv_sem=recv_ref.at[k], device_id=peer, device_id_type=MESH)
            cp.wait_send()
            cp.wait_recv()

    res = pl.pallas_call(
        body,
        name=name,
        out_shape=[pltpu.HBM(b.shape, b.dtype) for b in bufs],
        in_specs=[hbm] * (2 * n) + [sem, sem] + [pl.BlockSpec(memory_space=pl.ANY)] * len(after),
        out_specs=[hbm] * (2 * n),
        input_output_aliases={i: i for i in range(2 * n)},
        compiler_params=pltpu.CompilerParams(has_side_effects=pltpu.SideEffectType.DATAFLOW_SIDE_EFFECTING),
    )(*bufs, send_sems, recv_sems, *after)
    return res[:n], res[n:]


def kernel(x, mem, norm_mix_g, w_in, conv_w, gm_ln_g, gm_ln_b, gm_ws, gm_bs, w_out, norm_x_g, norm_mem_g, w_q, w_kv, w_xo, norm_final_g, loss_target, m_norm_mix_g, m_w_in, m_conv_w, m_gm_ln_g, m_gm_ln_b, m_gm_ws, m_gm_bs, m_w_out, m_norm_x_g, m_norm_mem_g, m_w_q, m_w_kv, m_w_xo, m_norm_final_g, v_norm_mix_g, v_w_in, v_conv_w, v_gm_ln_g, v_gm_ln_b, v_gm_ws, v_gm_bs, v_w_out, v_norm_x_g, v_norm_mem_g, v_w_q, v_w_kv, v_w_xo, v_norm_final_g):
    dev = 4 * lax.axis_index("x") + 2 * lax.axis_index("y") + lax.axis_index("c")
    core = lax.axis_index("c").astype(jnp.int32).reshape(1)
    x2 = x[0]
    target = loss_target[0]
    pad_taps = lambda t: jnp.pad(t[0], ((0, 5), (0, 0)))

    big_names = ["w_in", "w_out", "w_q", "w_kv", "w_xo"]
    big_w = dict(w_in=w_in[0], w_out=w_out[0], w_q=w_q[0], w_kv=w_kv[0], w_xo=w_xo[0])
    big_m = dict(w_in=m_w_in[0], w_out=m_w_out[0], w_q=m_w_q[0], w_kv=m_w_kv[0], w_xo=m_w_xo[0])
    big_v = dict(w_in=v_w_in[0], w_out=v_w_out[0], w_q=v_w_q[0], w_kv=v_w_kv[0], w_xo=v_w_xo[0])
    shard_axis = dict(w_in=1, w_out=0, w_q=0, w_kv=1, w_xo=0)
    full_shape = dict(w_in=(D_MODEL, IN_DIM), w_out=(MIX_DIM, D_MODEL), w_q=(D_MODEL, D_MODEL),
                      w_kv=(D_MODEL, 2 * D_MODEL), w_xo=(D_MODEL, D_MODEL))
    shard_size = {k: big_w[k].shape[shard_axis[k]] for k in big_names}
    others = big_names[1:]
    gather_spec = lambda k: (full_shape[k], shard_axis[k], shard_size[k])
    reduce_spec = lambda k: (shard_axis[k], shard_size[k])
    ws = gm_ws[0]
    bs_b = jnp.broadcast_to(gm_bs[0][:, :, None], (8, CHUNK, CHUNK))
    g_f = norm_final_g.reshape(1, D_MODEL)

    px, py = lax.axis_index("x"), lax.axis_index("y")
    chip_order = jnp.stack([2 * px + py, 2 * (1 - px) + py, 2 * px + 1 - py, 2 * (1 - px) + 1 - py]).astype(jnp.int32)
    (proj, h1t, w_in_full), ((w_out_full, conv_g),) = _inproj(
        x2, norm_mix_g, big_w["w_in"].astype(BF16), chip_order,
        plans=[_gather_plan([big_w["w_out"].astype(BF16), pad_taps(conv_w)[None]], [gather_spec("w_out"), ((N_DEV, 8, CHUNK), 0, 1)])])
    conv_full = conv_g.transpose(1, 0, 2).reshape(8, D_MODEL)
    attn_w = ["w_q", "w_kv", "w_xo"]
    cat, gathered = _mixer_fwd(
        proj, conv_full, gm_ln_g, gm_ln_b, ws, bs_b,
        plans=[_gather_plan([big_w[k].astype(BF16)], [gather_spec(k)]) for k in attn_w], pass_on_at=[0.4, 0.7, 0.88])
    wf = dict(zip(attn_w, [g[0] for g in gathered]), w_out=w_out_full)
    k, vv, mt = _kv_fwd(mem[0], norm_mem_g, wf["w_kv"])
    (dcat, actt, cot, dk, dvv, d_gf, d_gx, sq_err) = _attn(
        x2, cat, target, wf["w_out"], wf["w_q"], wf["w_xo"], k, vv, norm_x_g, g_f)

    sibling_plan = lambda names, grads: _sibling_plan([grads[k] for k in names], [reduce_spec(k) for k in names])
    early = ["w_kv", "w_out", "w_q", "w_xo"]
    big_g = {}
    big_g["w_kv"], d_gmem = _kv_bwd(mem[0], wf["w_kv"], mt, dk, dvv)
    dws, (got_kv,) = _dw_attn(actt, cot, plans=[sibling_plan(early[:1], big_g)])
    big_g.update(zip(early[1:], dws))
    pair_kv, (got_rest,) = _pair_sum("pair_sum_w_kv", [big_g["w_kv"]], got_kv, [reduce_spec("w_kv")], core,
                                     plans=[sibling_plan(early[1:], big_g)])
    pair_rest = _pair_sum("pair_sum_attn", [big_g[k] for k in early[1:]], got_rest, [reduce_spec(k) for k in early[1:]], core)
    (dproj, d_cw, d_lng, d_lnb, d_ws, d_bs), (parts_early,) = _mixer_bwd(
        proj, dcat, conv_full, gm_ln_g, gm_ln_b, ws, bs_b, plans=[_chip_plan(list(pair_kv) + list(pair_rest))])
    half = D_MODEL // 2
    in_spec = reduce_spec("w_in")
    dw_top, _ = _matmul_bf16("dw_in_top", h1t, dproj, rows=(0, half))
    dw_bot, (got_top, (ws_all,)) = _matmul_bf16(
        "dw_in_bot", h1t, dproj, rows=(half, half),
        plans=[_sibling_plan([dw_top], [in_spec]),
               _gather_plan([d_ws.reshape(1, 8 * CHUNK, CHUNK).astype(BF16)], [((N_DEV, 8 * CHUNK, CHUNK), 0, 1)])])
    (pair_top,), (got_bot,) = _pair_sum("pair_sum_w_in_top", [dw_top], got_top, [in_spec], core,
                                        plans=[_sibling_plan([dw_bot], [in_spec])])
    (pair_bot,) = _pair_sum("pair_sum_w_in_bot", [dw_bot], got_bot, [in_spec], core)
    chip_shapes = [p.shape for p in (pair_top, pair_bot)]
    chips_in_flight, token = _split_start("grad_chip_exchange_start", _chip_routes, [pair_top, pair_bot], chip_shapes)
    grad_x, d_gmix = _dh1(dproj, w_in_full, x2, cot, norm_mix_g, after=[token])
    vec_names = ["norm_mix_g", "gm_ln_g", "gm_ln_b", "gm_bs", "norm_x_g", "norm_mem_g", "norm_final_g"]
    vec_g = [d_gmix, d_lng, d_lnb, d_bs, d_gx, d_gmem, d_gf]
    small = jnp.concatenate(vec_g + [sq_err, d_cw], axis=0)[None]
    small_in_flight, small_token = _split_start("small_all_gather_start", _gather_routes, [small], [(N_DEV, 16, D_MODEL)])
    flat_ws = lambda t: t.reshape(8 * CHUNK, CHUNK)
    updated, _ = _adamw(
        "adamw_early",
        [(big_w[k], big_m[k], big_v[k], got_k) for k, got_k in zip(early, parts_early)]
        + [(flat_ws(ws), flat_ws(m_gm_ws), flat_ws(v_gm_ws), ws_all)], steps=STEPS_ADAMW // 2)
    big_out = dict(zip(early, updated[:4]))
    ws_out = updated[4]
    (pair_top, pair_bot), (land_top, land_bot) = _split_wait(
        "grad_chip_exchange_wait", _chip_routes, chips_in_flight, [small_token] + [u[0] for u in updated])
    my_chip = (2 * px + py).astype(jnp.int32).reshape(1)
    (big_out["w_in"],), _ = _adamw(
        "adamw_w_in", [(big_w["w_in"], big_m["w_in"], big_v["w_in"], [land_top, land_bot], [pair_top, pair_bot])],
        steps=STEPS_ADAMW, own_slot=my_chip)
    (small,), (small_all,) = _split_wait("small_all_gather_wait", _gather_routes, small_in_flight, [big_out["w_in"][0]])
    row = lambda t: t.reshape(1, D_MODEL)
    vec_out, conv_out, loss_row = _update_small(
        dev.astype(jnp.int32).reshape(1),
        [norm_mix_g, gm_ln_g, gm_ln_b, row(gm_bs), norm_x_g, norm_mem_g, row(norm_final_g)],
        [m_norm_mix_g, m_gm_ln_g, m_gm_ln_b, row(m_gm_bs), m_norm_x_g, m_norm_mem_g, row(m_norm_final_g)],
        [v_norm_mix_g, v_gm_ln_g, v_gm_ln_b, row(v_gm_bs), v_norm_x_g, v_norm_mem_g, row(v_norm_final_g)],
        *(t.transpose(1, 0, 2) for t in (conv_w, m_conv_w, v_conv_w)), small, small_all)
    loss = loss_row[0, 0]

    def result(name, which):
        if name in big_out:
            return big_out[name][which][None]
        if name == "conv_w":
            return conv_out[which].transpose(1, 0, 2)
        if name == "gm_ws":
            return ws_out[which].reshape(1, 8, CHUNK, CHUNK)
        r = vec_out[vec_names.index(name)][which]
        if name == "gm_bs":
            return r.reshape(1, 8, CHUNK)
        if name == "norm_final_g":
            return r.reshape(D_MODEL)
        return r

    weights = ["norm_mix_g", "w_in", "conv_w", "gm_ln_g", "gm_ln_b", "gm_ws", "gm_bs", "w_out", "norm_x_g", "norm_mem_g", "w_q", "w_kv", "w_xo", "norm_final_g"]
    outs = [loss, grad_x[None]]
    for which in range(4):
        outs += [result(name, which) for name in weights]
    return tuple(outs)
```

```python
import math

import jax
import jax.numpy as jnp
from jax import lax
from jax.experimental import pallas as pl
from jax.experimental.pallas import tpu as pltpu

F32 = jnp.float32
BF16 = jnp.bfloat16
MESH = pl.DeviceIdType.MESH

D_MODEL = 1024
N_DEV = 8
N_CHIP = 4
CHUNK = 128
GROUPS = 7
IN_DIM = GROUPS * D_MODEL
SHARD_IN = IN_DIM // N_DEV
MIX_DIM = 2 * D_MODEL
X_HEADS = 4
X_HEAD_DIM = D_MODEL // X_HEADS
MEM_LEN = 256
EPS = 1e-6
GELU_K0 = math.sqrt(2.0 / math.pi)
GELU_K1 = 0.044715

ADAM_LR = 0.001
ADAM_B1 = 0.9
ADAM_B2 = 0.999
ADAM_EPS = 1e-08
ADAM_WD = 0.01
ADAM_STEP = 10

TM_INPROJ = 512
TM_ATTN = 512
SUB_ATTN = 256
TM_DH1 = 512
TM_DW = 512
TK_DW = 4096
BARRIER_ID = {"sibling": 1, "chips": 2, "all": 3}
STEPS_ADAMW = 4


def _dot(a, b):
    return lax.dot_general(a, b, (((1,), (0,)), ((), ())), preferred_element_type=F32)


def _dot_nt(a, b):
    return lax.dot_general(a, b, (((1,), (1,)), ((), ())), preferred_element_type=F32)


def _rows8(v):
    return jnp.sum(v.reshape(v.shape[0] // 8, 8, v.shape[1]), axis=0)


def _gelu(x, with_grad=True):
    x2 = x * x
    t = jnp.tanh(x * (GELU_K0 + (GELU_K0 * GELU_K1) * x2))
    half = 0.5 + 0.5 * t
    g = x * half
    if not with_grad:
        return g
    return g, half + ((0.5 * x) * (1.0 - t * t)) * (GELU_K0 + (3.0 * GELU_K0 * GELU_K1) * x2)


def _silu(z, with_grad=True):
    s = jax.nn.sigmoid(z)
    zs = z * s
    return (zs, s + zs * (1.0 - s)) if with_grad else zs


def _rms(v):
    r = lax.rsqrt(jnp.mean(v * v, axis=-1, keepdims=True) + EPS)
    return v * r, r


def _rms_bwd(dy_g, vh, r):
    return r * (dy_g - vh * jnp.mean(dy_g * vh, axis=-1, keepdims=True))


def _place():
    return lax.axis_index("x"), lax.axis_index("y"), lax.axis_index("c")


def _other_chips(x, y):
    return [(1 - x, y), (x, 1 - y), (1 - x, 1 - y)]


def _pow2_divisor(n):
    return n & (-n)


def _shard_view(ref, axis, size, d):
    start = d * size
    align = _pow2_divisor(size)
    if align > 1:
        start = pl.multiple_of(start, align)
    idx = [slice(None)] * len(ref.shape)
    idx[axis] = pl.ds(start, size)
    return ref.at[tuple(idx)]


class _Plan:
    def __init__(self, ins, out_shape, sems, start, finish, mid=None, aliases=None, peers="all"):
        self.ins, self.out_shape, self.sems = list(ins), list(out_shape), list(sems)
        self.start, self.mid, self.finish = start, mid, finish
        self.aliases = dict(aliases or {})
        self.peers = peers


def _gather_plan(shards, specs):
    n = len(shards)
    n_copy = 7

    def copies(ins, outs, sems):
        send_sems, recv_sems, local_sems = sems
        x, y, c = _place()
        me, sibling = (x, y, c), (x, y, 1 - c)
        chips = _other_chips(x, y)

        def block(a, p):
            _, axis, size = specs[a]
            return _shard_view(outs[a], axis, size, 4 * p[0] + 2 * p[1] + p[2])

        def copy(a, k, p, to, src=None):
            return pltpu.make_async_remote_copy(
                src_ref=block(a, p) if src is None else src,
                dst_ref=block(a, p),
                send_sem=send_sems.at[a * n_copy + k],
                recv_sem=recv_sems.at[a * n_copy + k],
                device_id=to,
                device_id_type=MESH,
            )

        def mine():
            return [pltpu.make_async_copy(ins[a], block(a, me), local_sems.at[a]) for a in range(n)]

        def first():
            return [cp for a in range(n) for cp in
                    [copy(a, 0, me, sibling, src=ins[a])] + [copy(a, 1 + j, me, (*chip, c), src=ins[a]) for j, chip in enumerate(chips)]]

        def landed():
            return [copy(a, 1 + j, (*chip, c), me) for j, chip in enumerate(chips) for a in range(n)]

        def passed():
            return [copy(a, 4 + j, (*chip, c), sibling) for j, chip in enumerate(chips) for a in range(n)]

        def last():
            return [cp for a in range(n) for cp in
                    [copy(a, 0, sibling, me)] + [copy(a, 4 + j, (*chip, 1 - c), me) for j, chip in enumerate(chips)]]

        return mine, first, landed, passed, last

    def start(ins, outs, sems):
        mine, first, _, _, _ = copies(ins, outs, sems)
        for cp in mine() + first():
            cp.start()

    def mid(ins, outs, sems):
        _, _, landed, passed, _ = copies(ins, outs, sems)
        for got, fwd in zip(landed(), passed()):
            got.wait_recv()
            fwd.start()

    def finish(ins, outs, sems):
        mine, first, _, passed, last = copies(ins, outs, sems)
        for cp in last():
            cp.wait_recv()
        for cp in first() + passed():
            cp.wait_send()
        for cp in mine():
            cp.wait()

    return _Plan(
        shards,
        [jax.ShapeDtypeStruct(full, s.dtype) for s, (full, _, _) in zip(shards, specs)],
        [pltpu.SemaphoreType.DMA((n * n_copy,)), pltpu.SemaphoreType.DMA((n * n_copy,)), pltpu.SemaphoreType.DMA((n,))],
        start, finish, mid,
    )


def _sibling_plan(grads, specs):
    n = len(grads)

    def shard_shape(a):
        axis, size = specs[a]
        shp = list(grads[a].shape)
        shp[axis] = size
        return tuple(shp)

    def copies(ins, outs, sems):
        send_sems, recv_sems = sems
        x, y, c = _place()
        return [
            pltpu.make_async_remote_copy(
                src_ref=_shard_view(ins[a], specs[a][0], specs[a][1], 2 * q + (1 - c)),
                dst_ref=outs[a].at[q],
                send_sem=send_sems.at[a * N_CHIP + q],
                recv_sem=recv_sems.at[a * N_CHIP + q],
                device_id=(x, y, 1 - c),
                device_id_type=MESH,
            )
            for a in range(n) for q in range(N_CHIP)
        ]

    def start(ins, outs, sems):
        for cp in copies(ins, outs, sems):
            cp.start()

    def finish(ins, outs, sems):
        for cp in copies(ins, outs, sems):
            cp.wait()

    return _Plan(
        grads,
        [jax.ShapeDtypeStruct((N_CHIP, *shard_shape(a)), grads[a].dtype) for a in range(n)],
        [pltpu.SemaphoreType.DMA((n * N_CHIP,)), pltpu.SemaphoreType.DMA((n * N_CHIP,))],
        start, finish, peers="sibling",
    )


def _chip_plan(parts, full_rows=None, row0=None, into=None):
    n = len(parts)
    full_rows = [p.shape[1] for p in parts] if full_rows is None else full_rows
    row0 = [0] * n if row0 is None else row0

    def copies(ins, outs, sems):
        send_sems, recv_sems, local_sems = sems
        x, y, c = _place()
        my_chip = 2 * x + y
        chips = _other_chips(x, y)

        def land(a, q):
            return outs[a].at[q, pl.ds(row0[a], parts[a].shape[1]), :]

        def mine():
            return [pltpu.make_async_copy(ins[a].at[my_chip], land(a, my_chip), local_sems.at[a]) for a in range(n)]

        def remote(sending):
            return [
                pltpu.make_async_remote_copy(
                    src_ref=ins[a].at[2 * chip[0] + chip[1]],
                    dst_ref=land(a, my_chip if sending else 2 * chip[0] + chip[1]),
                    send_sem=send_sems.at[a * 3 + j], recv_sem=recv_sems.at[a * 3 + j],
                    device_id=(*chip, c), device_id_type=MESH)
                for a in range(n) for j, chip in enumerate(chips)
            ]

        return mine, remote

    def start(ins, outs, sems):
        mine, remote = copies(ins, outs, sems)
        for cp in mine() + remote(True):
            cp.start()

    def finish(ins, outs, sems):
        mine, remote = copies(ins, outs, sems)
        for cp in remote(False):
            cp.wait_recv()
        for cp in remote(True):
            cp.wait_send()
        for cp in mine():
            cp.wait()

    return _Plan(
        list(parts) + (list(into) if into is not None else []),
        [jax.ShapeDtypeStruct((N_CHIP, full_rows[a], parts[a].shape[2]), parts[a].dtype) for a in range(n)],
        [pltpu.SemaphoreType.DMA((n * 3,)), pltpu.SemaphoreType.DMA((n * 3,)), pltpu.SemaphoreType.DMA((n,))],
        start, finish,
        aliases={n + a: a for a in range(n)} if into is not None else None, peers="chips",
    )


def _call(body, *, name, operands, out_shape, grid=(), in_specs=None, out_specs=None, scratch_shapes=(),
          plans=(), aliases=None, prefetch=(), start_after_body=False, pass_on_at=0.6, body_peers=None):
    operands, out_shape, scratch_shapes = list(operands), list(out_shape), list(scratch_shapes)
    n_pf, n_in, n_out, n_scr = len(prefetch), len(operands), len(out_shape), len(scratch_shapes)
    in_vmem = pl.BlockSpec(memory_space=pltpu.VMEM)
    in_hbm = pl.BlockSpec(memory_space=pl.ANY)
    in_specs = [in_vmem] * n_in if in_specs is None else list(in_specs)
    out_specs = [in_vmem] * n_out if out_specs is None else list(out_specs)
    io_alias = {n_pf + i: o for i, o in (aliases or {}).items()}
    pos_in, pos_out = n_pf + n_in, n_out
    for p in plans:
        for i, o in p.aliases.items():
            io_alias[pos_in + i] = pos_out + o
        pos_in += len(p.ins)
        pos_out += len(p.out_shape)
    steps = math.prod(grid) if grid else 1
    mid_step = min(steps - 1, int(steps * pass_on_at))
    patterns = {p.peers for p in plans} | ({body_peers} if body_peers else set())
    pattern = None if not patterns else patterns.pop() if len(patterns) == 1 else "all"

    def shake_hands():
        x, y, c = _place()
        peers = [(x, y, 1 - c)] if pattern in ("sibling", "all") else []
        peers += [(*chip, c) for chip in _other_chips(x, y)] if pattern in ("chips", "all") else []
        barrier = pltpu.get_barrier_semaphore()
        for peer in peers:
            pl.semaphore_signal(barrier, inc=1, device_id=peer, device_id_type=MESH)
        pl.semaphore_wait(barrier, len(peers))

    def wrapped(*refs):
        refs = list(refs)
        take = lambda k: [refs.pop(0) for _ in range(k)]
        pf = take(n_pf)
        ins, p_ins = take(n_in), [take(len(p.ins)) for p in plans]
        outs, p_outs = take(n_out), [take(len(p.out_shape)) for p in plans]
        scr, p_sems = take(n_scr), [take(len(p.sems)) for p in plans]
        step = 0
        for ax, g in enumerate(grid):
            step = step * g + pl.program_id(ax)

        def hook(kind, at):
            todo = [(getattr(p, kind), a, b, c) for p, a, b, c in zip(plans, p_ins, p_outs, p_sems) if getattr(p, kind)]

            def run():
                for fn, a, b, c in todo:
                    fn(a, b, c)

            if todo and grid:
                pl.when(step == at)(run)
            elif todo:
                run()

        if pattern and grid:
            pl.when(step == 0)(shake_hands)
        elif pattern:
            shake_hands()
        if not start_after_body:
            hook("start", 0)
        if body is not None:
            body(*pf, *ins, *outs, *scr)
        if start_after_body:
            hook("start", 0)
        hook("mid", mid_step)
        hook("finish", steps - 1)

    layout = dict(
        grid=grid,
        in_specs=in_specs + [in_hbm] * sum(len(p.ins) for p in plans),
        out_specs=out_specs + [in_hbm] * sum(len(p.out_shape) for p in plans),
        scratch_shapes=scratch_shapes + [s for p in plans for s in p.sems],
    )
    if n_pf:
        layout = dict(grid_spec=pltpu.PrefetchScalarGridSpec(num_scalar_prefetch=n_pf, **layout))
    res = pl.pallas_call(
        wrapped,
        name=name,
        out_shape=out_shape + [o for p in plans for o in p.out_shape],
        input_output_aliases=io_alias,
        compiler_params=pltpu.CompilerParams(
            dimension_semantics=("arbitrary",) * len(grid) if grid else None,
            collective_id=BARRIER_ID[pattern] if pattern else None),
        **layout,
    )(*prefetch, *operands, *[a for p in plans for a in p.ins])
    res = list(res)
    outs, plan_outs = res[:n_out], []
    pos = n_out
    for p in plans:
        plan_outs.append(res[pos : pos + len(p.out_shape)])
        pos += len(p.out_shape)
    return outs, plan_outs


def _inproj(x, g, w_shard, order, plans=()):
    s = x.shape[0]
    tm = min(TM_INPROJ, s)
    nt = s // tm
    wide = 2 * SHARD_IN
    fwd_at = nt // 2

    def body(order_ref, x_ref, g_ref, w_hbm, proj_ref, h1t_ref, wfull_hbm,
             w_scr, h1_scr, send_sems, recv_sems, local_sems):
        k, i = pl.program_id(0), pl.program_id(1)
        px, py, c = _place()
        me, sibling = (px, py, c), (px, py, 1 - c)
        chips = _other_chips(px, py)

        def half(slot, core):
            return w_scr.at[slot, :, pl.ds(pl.multiple_of(core * SHARD_IN, 128), SHARD_IN)]

        def copy(n, slot, core, to, src=None):
            return pltpu.make_async_remote_copy(
                src_ref=half(slot, core) if src is None else src, dst_ref=half(slot, core),
                send_sem=send_sems.at[n], recv_sem=recv_sems.at[n], device_id=to, device_id_type=MESH)

        def mine():
            return pltpu.make_async_copy(w_hbm, half(0, c), local_sems.at[0])

        def first():
            return [copy(0, 0, c, sibling, src=w_hbm)] + [copy(1 + j, 1 + j, c, (*chips[j], c), src=w_hbm) for j in range(2)]

        def relay():
            to = (c * px + (1 - c) * (1 - px), c * (1 - py) + (1 - c) * py, c)
            return pltpu.make_async_remote_copy(
                src_ref=half(2 - c, c), dst_ref=half(3, c), send_sem=send_sems.at[3], recv_sem=recv_sems.at[3],
                device_id=to, device_id_type=MESH)

        def save(slot):
            cols = pl.ds(pl.multiple_of(order_ref[slot] * wide, 128), wide)
            return pltpu.make_async_copy(w_scr.at[slot], wfull_hbm.at[:, cols], local_sems.at[1 + slot])

        @pl.when((k == 0) & (i == 0))
        def _():
            mine().start()
            for cp in first():
                cp.start()
            mine().wait()
            copy(0, 0, 1 - c, me).wait_recv()

        for j in range(3):
            @pl.when((k == j + 1) & (i == 0))
            def _():
                copy(4 + j, 1 + j, 1 - c, me).wait_recv()

        @pl.when((k == 3) & (i == 0))
        def _():
            for slot in range(N_CHIP):
                save(slot).start()

        @pl.when(k == 0)
        def _():
            xh, _ = _rms(x_ref[...])
            h = xh * g_ref[...]
            h1t_ref[...] = h.T.astype(BF16)
            h1_scr[pl.ds(pl.multiple_of(i * tm, tm), tm), :] = h.astype(BF16)

        proj_ref[...] = _dot(h1_scr[pl.ds(pl.multiple_of(i * tm, tm), tm), :], w_scr[k]).astype(BF16)

        for j in range(3):
            @pl.when((k == 0) & (i == nt - 1) if j < 2 else (k == 2) & (i == fwd_at))
            def _():
                copy(1 + j, 1 + j, c, me).wait_recv()
                copy(4 + j, 1 + j, c, sibling).start()
                if j < 2:
                    pl.when(c == 1 - j)(lambda: relay().start())

        @pl.when((k == 3) & (i == nt - 1))
        def _():
            for cp in first() + [relay()] + [copy(4 + j, 1 + j, c, sibling) for j in range(3)]:
                cp.wait_send()
            for slot in range(N_CHIP):
                save(slot).wait()

    once = lambda k, i, order_ref: (jnp.where(k == 0, i, nt - 1), 0)
    return _call(
        body,
        name="inproj",
        prefetch=[order],
        operands=[x, g, w_shard],
        grid=(N_CHIP, nt),
        in_specs=[
            pl.BlockSpec((tm, D_MODEL), once),
            pl.BlockSpec((1, D_MODEL), lambda k, i, order_ref: (0, 0)),
            pl.BlockSpec(memory_space=pl.ANY),
        ],
        out_specs=[
            pl.BlockSpec((tm, wide), lambda k, i, order_ref: (i, order_ref[k])),
            pl.BlockSpec((D_MODEL, tm), lambda k, i, order_ref: (0, jnp.where(k == 0, i, nt - 1))),
            pl.BlockSpec(memory_space=pl.ANY),
        ],
        out_shape=[
            jax.ShapeDtypeStruct((s, IN_DIM), BF16),
            jax.ShapeDtypeStruct((D_MODEL, s), BF16),
            jax.ShapeDtypeStruct((D_MODEL, IN_DIM), BF16),
        ],
        scratch_shapes=[
            pltpu.VMEM((N_CHIP, D_MODEL, wide), BF16),
            pltpu.VMEM((s, D_MODEL), BF16),
            pltpu.SemaphoreType.DMA((7,)),
            pltpu.SemaphoreType.DMA((7,)),
            pltpu.SemaphoreType.DMA((1 + N_CHIP,)),
        ],
        plans=plans,
        start_after_body=True,
        pass_on_at=0.8,
        body_peers="all",
    )


def _rows_from_above(v, above):
    row = lax.broadcasted_iota(jnp.int32, (8, v.shape[1]), 0)
    r1, r2 = pltpu.roll(v, 1, 0), pltpu.roll(v, 2, 0)
    top1 = jnp.where(row == 0, above[7:8], r1[0:8])
    top2 = jnp.where(row == 0, above[6:7], jnp.where(row == 1, above[7:8], r2[0:8]))
    return jnp.concatenate([top1, r1[8:]], axis=0), jnp.concatenate([top2, r2[8:]], axis=0)


def _rows_from_below(v, below):
    n = v.shape[0]
    row = lax.broadcasted_iota(jnp.int32, (8, v.shape[1]), 0)
    r1, r2 = pltpu.roll(v, n - 1, 0), pltpu.roll(v, n - 2, 0)
    end1 = jnp.where(row == 7, below[0:1], r1[n - 8 :])
    end2 = jnp.where(row == 6, below[0:1], jnp.where(row == 7, below[1:2], r2[n - 8 :]))
    return jnp.concatenate([r1[: n - 8], end1], axis=0), jnp.concatenate([r2[: n - 8], end2], axis=0)


def _causal_mask():
    row = lax.broadcasted_iota(jnp.int32, (CHUNK, CHUNK), 0)
    col = lax.broadcasted_iota(jnp.int32, (CHUNK, CHUNK), 1)
    return row, row >= col


def _mixer_fwd(proj, conv_w, ln_g, ln_b, ws, bs_b, plans=()):
    s = proj.shape[0]
    nt = s // CHUNK

    def body(proj_ref, cw_ref, lng_ref, lnb_ref, ws_ref, bsb_ref, cat_ref, pcar_ref):
        @pl.when(pl.program_id(0) == 0)
        def _():
            pcar_ref[...] = jnp.zeros_like(pcar_ref)

        row, tril = _causal_mask()
        for j in range(8):
            cs = slice(CHUNK * j, CHUNK * (j + 1))

            def grp(k):
                return proj_ref[:, k * D_MODEL + CHUNK * j : k * D_MODEL + CHUNK * (j + 1)].astype(F32)

            w = cw_ref[:, cs]
            p = grp(1) * grp(2)
            p1, p2 = _rows_from_above(p, pcar_ref[:, cs])
            pcar_ref[:, cs] = p[CHUNK - 8 :]
            cv = w[0:1] * p2 + w[1:2] * p1 + w[2:3] * p
            sa = _silu(grp(3), with_grad=False)
            cat_ref[:, cs] = ((grp(0) * cv) * sa).astype(BF16)
            gu = _gelu(grp(4), with_grad=False)
            gv = _gelu(grp(5), with_grad=False)
            dv = gv - jnp.mean(gv, axis=-1, keepdims=True)
            vn = dv * lax.rsqrt(jnp.mean(dv * dv, axis=-1, keepdims=True) + EPS)
            vn = vn * lng_ref[:, cs] + lnb_ref[:, cs]
            wc = jnp.where(tril, ws_ref[j], 0.0).astype(BF16)
            sp = _dot(wc, vn.astype(BF16)) + bsb_ref[j]
            sb = _silu(grp(6), with_grad=False)
            cat_ref[:, D_MODEL + CHUNK * j : D_MODEL + CHUNK * (j + 1)] = ((gu * sp) * sb).astype(BF16)

    whole = lambda shape: pl.BlockSpec(shape, lambda i: (0,) * len(shape))
    (cat,), plan_outs = _call(
        body,
        name="mixer_fwd",
        operands=[proj, conv_w, ln_g, ln_b, ws, bs_b],
        grid=(nt,),
        in_specs=[
            pl.BlockSpec((CHUNK, IN_DIM), lambda i: (i, 0)),
            whole((8, D_MODEL)),
            whole((1, D_MODEL)),
            whole((1, D_MODEL)),
            whole((8, CHUNK, CHUNK)),
            whole((8, CHUNK, CHUNK)),
        ],
        out_specs=[pl.BlockSpec((CHUNK, MIX_DIM), lambda i: (i, 0))],
        out_shape=[jax.ShapeDtypeStruct((s, MIX_DIM), BF16)],
        scratch_shapes=[pltpu.VMEM((8, D_MODEL), F32)],
        plans=plans,
        pass_on_at=0.85,
    )
    return cat, plan_outs


def _mixer_bwd(proj, dcat, conv_w, ln_g, ln_b, ws, bs_b, plans=()):
    s = proj.shape[0]
    nt = s // CHUNK

    def body(proj_ref, halo_ref, dcat_ref, cw_ref, lng_ref, lnb_ref, ws_ref, bsb_ref,
             dproj_ref, dcw_ref, dlng_ref, dlnb_ref, dws_ref, dbs_ref,
             car_ref, acc_cw, acc_lng, acc_lnb, acc_bs, wc_scr, wct_scr):
        i = pl.program_id(0)
        tile = nt - 1 - i
        row, tril = _causal_mask()

        @pl.when(i == 0)
        def _():
            for j in range(8):
                wc = jnp.where(tril, ws_ref[j], 0.0)
                wc_scr[j] = wc.astype(BF16)
                wct_scr[j] = wc.T.astype(BF16)
            car_ref[...] = jnp.zeros_like(car_ref)
            acc_cw[...] = jnp.zeros_like(acc_cw)
            acc_lng[...] = jnp.zeros_like(acc_lng)
            acc_lnb[...] = jnp.zeros_like(acc_lnb)
            acc_bs[...] = jnp.zeros_like(acc_bs)
            dws_ref[...] = jnp.zeros_like(dws_ref)

        has_prev = jnp.where(tile > 0, 1.0, 0.0).astype(F32)
        for j in range(8):
            cs = slice(CHUNK * j, CHUNK * (j + 1))

            def col(k):
                return slice(k * D_MODEL + CHUNK * j, k * D_MODEL + CHUNK * (j + 1))

            def grp(k):
                return proj_ref[:, col(k)].astype(F32)

            w = cw_ref[:, cs]
            gb, gc, xa = grp(0), grp(1), grp(2)
            p = gc * xa
            pprev = halo_ref[:, col(1)].astype(F32) * halo_ref[:, col(2)].astype(F32) * has_prev
            p1, p2 = _rows_from_above(p, pprev[8:16])
            cv = w[0:1] * p2 + w[1:2] * p1 + w[2:3] * p
            za = grp(3)
            sa, dsa = _silu(za)
            da = dcat_ref[:, cs].astype(F32)
            da_gb = da * gb
            dproj_ref[:, col(0)] = ((da * sa) * cv).astype(BF16)
            dproj_ref[:, col(3)] = ((da_gb * cv) * dsa).astype(BF16)
            dcv = da_gb * sa
            d1, d2 = _rows_from_below(dcv, car_ref[:, cs])
            car_ref[:, cs] = dcv[0:8]
            dp = w[2:3] * dcv + w[1:2] * d1 + w[0:1] * d2
            dproj_ref[:, col(1)] = (dp * xa).astype(BF16)
            dproj_ref[:, col(2)] = (dp * gc).astype(BF16)
            acc_cw[0, :, cs] += _rows8(dcv * p2)
            acc_cw[1, :, cs] += _rows8(dcv * p1)
            acc_cw[2, :, cs] += _rows8(dcv * p)
            gu, dgu = _gelu(grp(4))
            gv, dgv = _gelu(grp(5))
            dv = gv - jnp.mean(gv, axis=-1, keepdims=True)
            rstd = lax.rsqrt(jnp.mean(dv * dv, axis=-1, keepdims=True) + EPS)
            vnh = dv * rstd
            lng = lng_ref[:, cs]
            vnb = (vnh * lng + lnb_ref[:, cs]).astype(BF16)
            sp = _dot(wc_scr[j], vnb) + bsb_ref[j]
            zb = grp(6)
            sb, dsb = _silu(zb)
            db = dcat_ref[:, D_MODEL + CHUNK * j : D_MODEL + CHUNK * (j + 1)].astype(F32)
            db_sp, db_gu = db * sp, db * gu
            dproj_ref[:, col(4)] = ((db_sp * sb) * dgu).astype(BF16)
            dproj_ref[:, col(6)] = ((db_sp * gu) * dsb).astype(BF16)
            dsp = db_gu * sb
            acc_bs[j] += dsp
            dspb = dsp.astype(BF16)
            dws_ref[j] += _dot_nt(dspb, vnb)
            dvn = _dot(wct_scr[j], dspb)
            acc_lnb[:, cs] += _rows8(dvn)
            acc_lng[:, cs] += _rows8(dvn * vnh)
            dvh = dvn * lng
            dgvv = rstd * (dvh - jnp.mean(dvh, axis=-1, keepdims=True) - vnh * jnp.mean(dvh * vnh, axis=-1, keepdims=True))
            dproj_ref[:, col(5)] = (dgvv * dgv).astype(BF16)

        @pl.when(i == nt - 1)
        def _():
            dlng_ref[...] = jnp.sum(acc_lng[...], axis=0, keepdims=True)
            dlnb_ref[...] = jnp.sum(acc_lnb[...], axis=0, keepdims=True)
            dcw_ref[...] = jnp.zeros_like(dcw_ref)
            for k in range(3):
                dcw_ref[k : k + 1, :] = jnp.sum(acc_cw[k], axis=0, keepdims=True)
            for j in range(8):
                dws_ref[j] = jnp.where(tril, dws_ref[j], 0.0)
                dbs_ref[:, CHUNK * j : CHUNK * (j + 1)] = jnp.sum(acc_bs[j].T, axis=0, keepdims=True)

    whole = lambda shape: pl.BlockSpec(shape, lambda i: (0,) * len(shape))
    halo_rows = 16
    per = CHUNK // halo_rows
    return _call(
        body,
        name="mixer_bwd",
        operands=[proj, proj, dcat, conv_w, ln_g, ln_b, ws, bs_b],
        plans=plans,
        grid=(nt,),
        in_specs=[
            pl.BlockSpec((CHUNK, IN_DIM), lambda i: (nt - 1 - i, 0)),
            pl.BlockSpec((halo_rows, IN_DIM), lambda i: (jnp.maximum((nt - 1 - i) * per - 1, 0), 0)),
            pl.BlockSpec((CHUNK, MIX_DIM), lambda i: (nt - 1 - i, 0)),
            whole((8, D_MODEL)),
            whole((1, D_MODEL)),
            whole((1, D_MODEL)),
            whole((8, CHUNK, CHUNK)),
            whole((8, CHUNK, CHUNK)),
        ],
        out_specs=[
            pl.BlockSpec((CHUNK, IN_DIM), lambda i: (nt - 1 - i, 0)),
            whole((8, D_MODEL)),
            whole((1, D_MODEL)),
            whole((1, D_MODEL)),
            whole((8, CHUNK, CHUNK)),
            whole((1, D_MODEL)),
        ],
        out_shape=[
            jax.ShapeDtypeStruct((s, IN_DIM), BF16),
            jax.ShapeDtypeStruct((8, D_MODEL), F32),
            jax.ShapeDtypeStruct((1, D_MODEL), F32),
            jax.ShapeDtypeStruct((1, D_MODEL), F32),
            jax.ShapeDtypeStruct((8, CHUNK, CHUNK), F32),
            jax.ShapeDtypeStruct((1, D_MODEL), F32),
        ],
        scratch_shapes=[
            pltpu.VMEM((8, D_MODEL), F32),
            pltpu.VMEM((3, 8, D_MODEL), F32),
            pltpu.VMEM((8, D_MODEL), F32),
            pltpu.VMEM((8, D_MODEL), F32),
            pltpu.VMEM((8, CHUNK, CHUNK), F32),
            pltpu.VMEM((8, CHUNK, CHUNK), BF16),
            pltpu.VMEM((8, CHUNK, CHUNK), BF16),
        ],
    )


def _kv_fwd(mem, g_mem, w_kv):
    def body(mem_ref, g_ref, w_ref, k_ref, vv_ref, mt_ref):
        mh, _ = _rms(mem_ref[...])
        m = mh * g_ref[...]
        kv = _dot(m.astype(BF16), w_ref[...])
        k_ref[...] = kv[:, :D_MODEL].astype(BF16)
        vv_ref[...] = kv[:, D_MODEL:].astype(BF16)
        mt_ref[...] = m.T.astype(BF16)

    return pl.pallas_call(
        body,
        name="kv_fwd",
        out_shape=[
            jax.ShapeDtypeStruct((MEM_LEN, D_MODEL), BF16),
            jax.ShapeDtypeStruct((MEM_LEN, D_MODEL), BF16),
            jax.ShapeDtypeStruct((D_MODEL, MEM_LEN), BF16),
        ],
    )(mem, g_mem, w_kv)


def _kv_bwd(mem, w_kv, mt, dk, dvv):
    def body(mem_ref, w_ref, mt_ref, dk_ref, dvv_ref, dw_ref, dg_ref):
        dkv = jnp.concatenate([dk_ref[...], dvv_ref[...]], axis=1).astype(BF16)
        dw_ref[...] = _dot(mt_ref[...], dkv).astype(BF16)
        dm = _dot_nt(dkv, w_ref[...])
        mh, _ = _rms(mem_ref[...])
        dg_ref[...] = jnp.sum(dm * mh, axis=0, keepdims=True)

    return pl.pallas_call(
        body,
        name="kv_bwd",
        out_shape=[jax.ShapeDtypeStruct((D_MODEL, 2 * D_MODEL), BF16), jax.ShapeDtypeStruct((1, D_MODEL), F32)],
    )(mem, w_kv, mt, dk, dvv)


def _attn(x, cat, target, w_out, w_q, w_xo, k, vv, g_x, g_f):
    s = x.shape[0]
    tm = min(TM_ATTN, s)
    sub = min(SUB_ATTN, tm)
    nt = s // tm
    scale = 1.0 / math.sqrt(X_HEAD_DIM)

    def body(x_ref, cat_ref, t_ref, wout_ref, wq_ref, wxo_ref, k_ref, vv_ref, gx_ref, gf_ref,
             dcat_ref, actt_ref, cot_ref,
             dk_ref, dvv_ref, dgf_ref, dgx_ref, loss_ref,
             p_scr, q_scr, o_scr, dq_scr, acc_gf, acc_gx, acc_loss):
        i = pl.program_id(0)

        @pl.when(i == 0)
        def _():
            dk_ref[...] = jnp.zeros_like(dk_ref)
            dvv_ref[...] = jnp.zeros_like(dvv_ref)
            acc_gf[...] = jnp.zeros_like(acc_gf)
            acc_gx[...] = jnp.zeros_like(acc_gx)
            acc_loss[...] = jnp.zeros_like(acc_loss)

        subs = [slice(u * sub, (u + 1) * sub) for u in range(tm // sub)]
        heads = [slice(X_HEAD_DIM * h, X_HEAD_DIM * (h + 1)) for h in range(X_HEADS)]
        gx, gf = gx_ref[...], gf_ref[...]
        x1, x1h, r2 = [], [], []
        for rs in subs:
            cat = cat_ref[rs, :]
            actt_ref[0:MIX_DIM, rs] = cat.astype(F32).T.astype(BF16)
            v = x_ref[rs, :] + _dot(cat, wout_ref[...])
            vh, r = _rms(v)
            h2 = vh * gx
            actt_ref[MIX_DIM : MIX_DIM + D_MODEL, rs] = h2.T.astype(BF16)
            q_scr[rs, :] = _dot(h2.astype(BF16), wq_ref[...]).astype(BF16)
            x1.append(v), x1h.append(vh), r2.append(r)
        for rs in subs:
            for hs in heads:
                sc = _dot_nt(q_scr[rs, hs], k_ref[:, hs]) * scale
                e = jnp.exp(sc - jnp.max(sc, axis=-1, keepdims=True))
                p = e / jnp.sum(e, axis=-1, keepdims=True)
                p_scr[rs, hs] = p
                o_scr[rs, hs] = _dot(p.astype(BF16), vv_ref[:, hs])
        dx2, sq, d_gf = [], None, None
        for u, rs in enumerate(subs):
            o = o_scr[rs, :]
            actt_ref[MIX_DIM + D_MODEL :, rs] = o.T.astype(BF16)
            x2 = x1[u] + _dot(o.astype(BF16), wxo_ref[...])
            x2h, r3 = _rms(x2)
            err = x2h * gf - t_ref[rs, :]
            dy = err * (1.0 / D_MODEL)
            sq = _rows8(err * err) if sq is None else sq + _rows8(err * err)
            d_gf = _rows8(dy * x2h) if d_gf is None else d_gf + _rows8(dy * x2h)
            dx2.append(_rms_bwd(dy * gf, x2h, r3))
        acc_loss[...] += sq
        acc_gf[...] += d_gf
        d_vv, d_k = [None] * X_HEADS, [None] * X_HEADS
        for u, rs in enumerate(subs):
            dx2b = dx2[u].astype(BF16)
            cot_ref[2, rs, :] = dx2b
            dob = _dot_nt(dx2b, wxo_ref[...]).astype(BF16)
            for h, hs in enumerate(heads):
                p = p_scr[rs, hs]
                dp = _dot_nt(dob[:, hs], vv_ref[:, hs])
                ds = p * (dp - jnp.sum(dp * p, axis=-1, keepdims=True)) * scale
                dq_scr[rs, hs] = _dot(ds.astype(BF16), k_ref[:, hs]).astype(BF16)
                dvv_h = _dot(p.T.astype(BF16), dob[:, hs])
                dk_h = _dot(ds.T.astype(BF16), q_scr[rs, hs])
                d_vv[h] = dvv_h if d_vv[h] is None else d_vv[h] + dvv_h
                d_k[h] = dk_h if d_k[h] is None else d_k[h] + dk_h
        for h, hs in enumerate(heads):
            dvv_ref[:, hs] += d_vv[h]
            dk_ref[:, hs] += d_k[h]
        d_gx = None
        for u, rs in enumerate(subs):
            dqb = dq_scr[rs, :]
            cot_ref[1, rs, :] = dqb
            dh2 = _dot_nt(dqb, wq_ref[...])
            d_gx = _rows8(dh2 * x1h[u]) if d_gx is None else d_gx + _rows8(dh2 * x1h[u])
            dx1 = dx2[u] + _rms_bwd(dh2 * gx, x1h[u], r2[u])
            dx1b = dx1.astype(BF16)
            cot_ref[0, rs, :] = dx1b
            dcat_ref[rs, :] = _dot_nt(dx1b, wout_ref[...]).astype(BF16)
        acc_gx[...] += d_gx

        @pl.when(i == nt - 1)
        def _():
            dgf_ref[...] = jnp.sum(acc_gf[...], axis=0, keepdims=True)
            dgx_ref[...] = jnp.sum(acc_gx[...], axis=0, keepdims=True)
            loss_ref[...] = jnp.sum(acc_loss[...], axis=0, keepdims=True)

    whole = lambda shape: pl.BlockSpec(shape, lambda i: (0,) * len(shape))
    once = lambda shape: pl.BlockSpec(shape, lambda i: (0,) * len(shape), pipeline_mode=pl.Buffered(1))
    rows = lambda width: pl.BlockSpec((tm, width), lambda i: (i, 0))
    cols = lambda height: pl.BlockSpec((height, tm), lambda i: (0, i))
    vec = jax.ShapeDtypeStruct((1, D_MODEL), F32)
    return pl.pallas_call(
        body,
        name="attn",
        grid=(nt,),
        in_specs=[
            rows(D_MODEL), rows(MIX_DIM), rows(D_MODEL),
            once((MIX_DIM, D_MODEL)), once((D_MODEL, D_MODEL)), once((D_MODEL, D_MODEL)),
            once((MEM_LEN, D_MODEL)), once((MEM_LEN, D_MODEL)),
            whole((1, D_MODEL)), whole((1, D_MODEL)),
        ],
        out_specs=[
            rows(MIX_DIM), cols(2 * MIX_DIM),
            pl.BlockSpec((3, tm, D_MODEL), lambda i: (0, i, 0)),
            whole((MEM_LEN, D_MODEL)), whole((MEM_LEN, D_MODEL)),
            whole((1, D_MODEL)), whole((1, D_MODEL)), whole((1, D_MODEL)),
        ],
        out_shape=[
            jax.ShapeDtypeStruct((s, MIX_DIM), BF16),
            jax.ShapeDtypeStruct((2 * MIX_DIM, s), BF16),
            jax.ShapeDtypeStruct((3, s, D_MODEL), BF16),
            jax.ShapeDtypeStruct((MEM_LEN, D_MODEL), F32),
            jax.ShapeDtypeStruct((MEM_LEN, D_MODEL), F32),
            vec, vec, vec,
        ],
        scratch_shapes=[
            pltpu.VMEM((tm, D_MODEL), F32),
            pltpu.VMEM((tm, D_MODEL), BF16),
            pltpu.VMEM((tm, D_MODEL), F32),
            pltpu.VMEM((tm, D_MODEL), BF16),
            pltpu.VMEM((8, D_MODEL), F32),
            pltpu.VMEM((8, D_MODEL), F32),
            pltpu.VMEM((8, D_MODEL), F32),
        ],
        compiler_params=pltpu.CompilerParams(dimension_semantics=("arbitrary",)),
    )(x, cat, target, w_out, w_q, w_xo, k, vv, g_x, g_f)


def _matmul_bf16(name, a, b, rows=None, plans=()):
    first, m = (0, a.shape[0]) if rows is None else rows
    kk = a.shape[1]
    n = b.shape[1]
    tm, tn, tk = min(m, TM_DW), min(n, D_MODEL), min(kk, TK_DW)
    nk = kk // tk
    i0 = first // tm

    def body(a_ref, b_ref, o_ref, *acc):
        prod = _dot(a_ref[...], b_ref[...])
        if nk == 1:
            o_ref[...] = prod.astype(BF16)
            return
        (acc_ref,) = acc
        kid = pl.program_id(2)

        @pl.when(kid == 0)
        def _():
            acc_ref[...] = prod

        @pl.when(kid > 0)
        def _():
            acc_ref[...] += prod

        @pl.when(kid == nk - 1)
        def _():
            o_ref[...] = acc_ref[...].astype(BF16)

    (out,), plan_outs = _call(
        body,
        name=name,
        operands=[a, b],
        grid=(m // tm, n // tn, nk),
        in_specs=[pl.BlockSpec((tm, tk), lambda i, j, k: (i + i0, k)), pl.BlockSpec((tk, tn), lambda i, j, k: (k, j))],
        out_specs=[pl.BlockSpec((tm, tn), lambda i, j, k: (i, j))],
        out_shape=[jax.ShapeDtypeStruct((m, n), BF16)],
        scratch_shapes=[pltpu.VMEM((tm, tn), F32)] if nk > 1 else [],
        plans=plans,
    )
    return out, plan_outs


def _dw_attn(actt, cot, plans=()):
    s = actt.shape[1]
    tm = TM_DW
    counts = [MIX_DIM // tm, D_MODEL // tm, D_MODEL // tm]
    starts = [0, counts[0], counts[0] + counts[1]]
    which = lambda i: jnp.where(i < starts[1], 0, jnp.where(i < starts[2], 1, 2))

    def body(a_ref, b_ref, *outs):
        i = pl.program_id(0)
        prod = _dot(a_ref[...], b_ref[0]).astype(BF16)
        for m in range(3):
            @pl.when(which(i) == m)
            def _():
                outs[m][...] = prod

    out_specs = [pl.BlockSpec((tm, D_MODEL), lambda i, m=m: (jnp.clip(i - starts[m], 0, counts[m] - 1), 0)) for m in range(3)]
    out_shape = [jax.ShapeDtypeStruct((counts[m] * tm, D_MODEL), BF16) for m in range(3)]
    outs, plan_outs = _call(
        body,
        name="dw_attn",
        operands=[actt, cot],
        grid=(sum(counts),),
        in_specs=[pl.BlockSpec((tm, s), lambda i: (i, 0)), pl.BlockSpec((1, s, D_MODEL), lambda i: (which(i), 0, 0))],
        out_specs=out_specs,
        out_shape=out_shape,
        plans=plans,
    )
    return outs, plan_outs


def _dh1(dproj, w_in, x, dx1, g, after=()):
    s = x.shape[0]
    tm = min(TM_DH1, s)
    nt = s // tm

    def body(dp_ref, w_ref, x_ref, dx1_ref, g_ref, *rest):
        gx_ref, dg_ref, acc_g = rest[-3:]
        i = pl.program_id(0)

        @pl.when(i == 0)
        def _():
            acc_g[...] = jnp.zeros_like(acc_g)

        dh1 = _dot_nt(dp_ref[...], w_ref[...])
        xh, r = _rms(x_ref[...])
        acc_g[...] += _rows8(dh1 * xh)
        gx_ref[...] = dx1_ref[0].astype(F32) + _rms_bwd(dh1 * g_ref[...], xh, r)

        @pl.when(i == nt - 1)
        def _():
            dg_ref[...] = jnp.sum(acc_g[...], axis=0, keepdims=True)

    rows = pl.BlockSpec((tm, D_MODEL), lambda i: (i, 0))
    outs, _ = _call(
        body,
        name="dh1",
        operands=[dproj, w_in, x, dx1, g] + list(after),
        grid=(nt,),
        in_specs=[
            pl.BlockSpec((tm, IN_DIM), lambda i: (i, 0)),
            pl.BlockSpec((D_MODEL, IN_DIM), lambda i: (0, 0), pipeline_mode=pl.Buffered(1)),
            rows,
            pl.BlockSpec((1, tm, D_MODEL), lambda i: (0, i, 0)),
            pl.BlockSpec((1, D_MODEL), lambda i: (0, 0)),
        ] + [pl.BlockSpec(memory_space=pl.ANY)] * len(after),
        out_specs=[rows, pl.BlockSpec((1, D_MODEL), lambda i: (0, 0))],
        out_shape=[jax.ShapeDtypeStruct((s, D_MODEL), F32), jax.ShapeDtypeStruct((1, D_MODEL), F32)],
        scratch_shapes=[pltpu.VMEM((8, D_MODEL), F32)],
    )
    return outs


def _pair_sum(name, grads, gots, specs, core, plans=()):
    n = len(grads)
    shards = []
    for grad, (axis, size) in zip(grads, specs):
        shard = list(grad.shape)
        shard[axis] = size
        shards.append(tuple(shard))

    def body(core_ref, *refs):
        for g_ref, got_ref, o_ref in zip(refs[:n], refs[n : 2 * n], refs[2 * n :]):
            o_ref[0] = (g_ref[...].astype(F32) + got_ref[0].astype(F32)).astype(BF16)

    def own(shard, axis):
        if axis == 1:
            return pl.BlockSpec(shard, lambda q, core_ref: (0, 2 * q + core_ref[0]))
        return pl.BlockSpec(shard, lambda q, core_ref: (2 * q + core_ref[0], 0))

    slots = [pl.BlockSpec((1, *shard), lambda q, core_ref: (q, 0, 0)) for shard in shards]
    outs, plan_outs = _call(
        body,
        name=name,
        prefetch=[core],
        operands=list(grads) + list(gots),
        grid=(N_CHIP,),
        in_specs=[own(shard, axis) for shard, (axis, _) in zip(shards, specs)] + slots,
        out_specs=slots,
        out_shape=[jax.ShapeDtypeStruct((N_CHIP, *shard), BF16) for shard in shards],
        plans=plans,
    )
    return (outs, plan_outs) if plans else outs


def _adam_step(w, m, v, g):
    c1 = 1.0 / (1.0 - ADAM_B1**ADAM_STEP)
    c2 = 1.0 / (1.0 - ADAM_B2**ADAM_STEP)
    m_new = ADAM_B1 * m + (1.0 - ADAM_B1) * g
    v_new = ADAM_B2 * v + (1.0 - ADAM_B2) * (g * g)
    return -ADAM_LR * ((m_new * c1) / (jnp.sqrt(v_new * c2) + ADAM_EPS) + ADAM_WD * w), m_new, v_new


def _update_small(dev, vec_w, vec_m, vec_v, conv_w, conv_m, conv_v, small, landed):
    n_vec = len(vec_w)

    def body(dev_ref, *refs):
        refs = list(refs)
        take = lambda k: [refs.pop(0) for _ in range(k)]
        w_refs, m_refs, v_refs = take(n_vec), take(n_vec), take(n_vec)
        cw_ref, cm_ref, cv_ref, small_ref, land_ref, small_conv_ref, land_conv_ref = take(7)
        vec_outs, conv_outs, (loss_ref,) = take(4 * n_vec), take(4), take(1)

        def total(own_ref, others_ref):
            acc = None
            for d in range(N_DEV):
                part = jnp.where(dev_ref[0] == d, own_ref[0], others_ref[d])
                acc = part if acc is None else acc + part
            return acc

        g_all = total(small_ref, land_ref)
        for i in range(n_vec):
            g = g_all[i : i + 1]
            delta, m_new, v_new = _adam_step(w_refs[i][...], m_refs[i][...], v_refs[i][...], g)
            for ref, val in zip(vec_outs[4 * i : 4 * i + 4], (g, delta, m_new, v_new)):
                ref[...] = val
        g_conv = total(small_conv_ref, land_conv_ref)
        for tap in range(3):
            g = g_conv[tap : tap + 1]
            delta, m_new, v_new = _adam_step(cw_ref[tap], cm_ref[tap], cv_ref[tap], g)
            for ref, val in zip(conv_outs, (g, delta, m_new, v_new)):
                ref[tap] = val
        loss_ref[...] = jnp.broadcast_to(0.5 * jnp.sum(g_all[7:8], axis=-1, keepdims=True) / D_MODEL, loss_ref.shape)

    vec_spec = pl.BlockSpec((1, D_MODEL), lambda i, dev_ref: (0, 0))
    conv_spec = pl.BlockSpec((3, 1, CHUNK), lambda i, dev_ref: (0, 0, 0))
    vec_shape = jax.ShapeDtypeStruct((1, D_MODEL), F32)
    conv_shape = jax.ShapeDtypeStruct((3, 1, CHUNK), F32)
    outs, _ = _call(
        body,
        name="update_small",
        prefetch=[dev],
        operands=list(vec_w) + list(vec_m) + list(vec_v) + [conv_w, conv_m, conv_v, small, landed, small, landed],
        grid=(1,),
        in_specs=[vec_spec] * (3 * n_vec) + [conv_spec] * 3 + [
            pl.BlockSpec((1, 16, D_MODEL), lambda i, dev_ref: (0, 0, 0)),
            pl.BlockSpec((N_DEV, 16, D_MODEL), lambda i, dev_ref: (0, 0, 0)),
            pl.BlockSpec((1, 8, CHUNK), lambda i, dev_ref: (0, 1, dev_ref[0])),
            pl.BlockSpec((N_DEV, 8, CHUNK), lambda i, dev_ref: (0, 1, dev_ref[0])),
        ],
        out_specs=[vec_spec] * (4 * n_vec) + [conv_spec] * 4 + [pl.BlockSpec((1, CHUNK), lambda i, dev_ref: (0, 0))],
        out_shape=[vec_shape] * (4 * n_vec) + [conv_shape] * 4 + [jax.ShapeDtypeStruct((1, CHUNK), F32)],
    )
    return [outs[4 * i : 4 * i + 4] for i in range(n_vec)], outs[4 * n_vec : 4 * n_vec + 4], outs[-1]


def _adamw(name, groups, steps, own_slot=None, plans=()):
    lists = lambda p: list(p) if isinstance(p, (list, tuple)) else [p]
    groups = [(g[0], g[1], g[2], lists(g[3]), lists(g[4]) if len(g) > 4 else []) for g in groups]
    counts = [3 + len(g[3]) + len(g[4]) for g in groups]
    use_slot = own_slot is not None

    def first_tiles(w, parts):
        firsts, first = [], 0
        for p in parts:
            firsts.append(first)
            first += p.shape[1] // (w.shape[0] // steps)
        return firsts

    def update(t, slot_ref, firsts, w_ref, m_ref, v_ref, p_refs, o_refs, g_ref, d_ref, nm_ref, nv_ref):
        g = None
        for k, p_ref in enumerate(p_refs):
            gk = None
            for q in range(p_ref.shape[0]):
                part = p_ref[q].astype(F32)
                if o_refs:
                    part = jnp.where(slot_ref[0] == q, o_refs[k][0].astype(F32), part)
                gk = part if gk is None else gk + part
            g = gk if g is None else jnp.where(t >= firsts[k], gk, g)
        g_ref[...] = g
        d_ref[...], nm_ref[...], nv_ref[...] = _adam_step(w_ref[...], m_ref[...], v_ref[...], g)

    def body(*refs):
        refs = list(refs)
        slot_ref = refs.pop(0) if use_slot else None
        t = pl.program_id(0)
        ins, outs = refs[: sum(counts)], refs[sum(counts) :]
        for i, (w, _, _, parts, own) in enumerate(groups):
            mine = ins[sum(counts[:i]) : sum(counts[: i + 1])]
            update(t, slot_ref, first_tiles(w, parts), *mine[:3], mine[3 : 3 + len(parts)], mine[3 + len(parts) :],
                   *outs[4 * i : 4 * i + 4])

    operands, in_specs, out_specs, out_shape = [], [], [], []
    for w, m, v, parts, own in groups:
        r, cdim = w.shape
        tr = r // steps
        firsts = first_tiles(w, parts)

        def tile_of(k, t, firsts=firsts, parts=parts, tr=tr):
            return jnp.clip(t - firsts[k], 0, parts[k].shape[1] // tr - 1)

        blk = pl.BlockSpec((tr, cdim), lambda t, *_: (t, 0))
        operands += [w, m, v] + parts + own
        in_specs += [blk, blk, blk]
        in_specs += [pl.BlockSpec((p.shape[0], tr, cdim), lambda t, *_, k=k, tile_of=tile_of: (0, tile_of(k, t), 0)) for k, p in enumerate(parts)]
        in_specs += [pl.BlockSpec((1, tr, cdim), lambda t, slot_ref, k=k, tile_of=tile_of: (slot_ref[0], tile_of(k, t), 0)) for k in range(len(own))]
        out_specs += [blk] * 4
        out_shape += [jax.ShapeDtypeStruct((r, cdim), F32)] * 4
    outs, plan_outs = _call(
        body,
        name=name,
        prefetch=[own_slot] if use_slot else [],
        operands=operands,
        grid=(steps,),
        in_specs=in_specs,
        out_specs=out_specs,
        out_shape=out_shape,
        plans=plans,
    )
    return [outs[4 * i : 4 * i + 4] for i in range(len(groups))], plan_outs


def _chip_routes(n):
    x, y, c = _place()
    my_chip = 2 * x + y
    return [(a, (*chip, c), 2 * chip[0] + chip[1], my_chip, 2 * chip[0] + chip[1])
            for a in range(n) for chip in _other_chips(x, y)]


def _gather_routes(n):
    x, y, c = _place()
    flip = lambda v, f: 1 - v if f else v
    peers = [(flip(x, fx), flip(y, fy), flip(c, fc)) for fx in (0, 1) for fy in (0, 1) for fc in (0, 1) if fx + fy + fc]
    return [(0, p, 0, 4 * x + 2 * y + c, 4 * p[0] + 2 * p[1] + p[2]) for p in peers]


_chip_routes.copies_per_array = N_CHIP - 1
_gather_routes.copies_per_array = N_DEV - 1


def _split_start(name, routes, srcs, land_shapes):
    n = len(srcs)
    hbm = pl.BlockSpec(memory_space=pltpu.HBM)
    sem = pl.BlockSpec(memory_space=pltpu.SEMAPHORE)

    def body(*refs):
        src_refs, land_refs = refs[:n], refs[n : 2 * n]
        send_sems, recv_sems = refs[2 * n], refs[2 * n + 1]
        token = refs[-1]
        for k, (a, peer, src_slot, there, _) in enumerate(routes(n)):
            pltpu.make_async_remote_copy(
                src_ref=src_refs[a].at[src_slot], dst_ref=land_refs[a].at[there],
                send_sem=send_sems.at[k], recv_sem=recv_sems.at[k], device_id=peer, device_id_type=MESH).start()
        token[...] = jnp.zeros_like(token)

    in_hbm = [pltpu.with_memory_space_constraint(p, pltpu.HBM) for p in srcs]
    in_hbm += [pltpu.with_memory_space_constraint(lax.empty(shape, p.dtype), pltpu.HBM) for p, shape in zip(srcs, land_shapes)]
    n_copies = routes.copies_per_array * n
    res = pl.pallas_call(
        body,
        name=name,
        out_shape=[pltpu.SemaphoreType.DMA((n_copies,)), pltpu.SemaphoreType.DMA((n_copies,))]
        + [pltpu.HBM(p.shape, p.dtype) for p in srcs] + [pltpu.HBM(shape, p.dtype) for p, shape in zip(srcs, land_shapes)]
        + [jax.ShapeDtypeStruct((8, CHUNK), F32)],
        in_specs=[hbm] * (2 * n),
        out_specs=[sem, sem] + [hbm] * (2 * n) + [pl.BlockSpec(memory_space=pltpu.VMEM)],
        input_output_aliases={i: 2 + i for i in range(2 * n)},
        compiler_params=pltpu.CompilerParams(has_side_effects=pltpu.SideEffectType.DATAFLOW_SIDE_EFFECTING),
    )(*in_hbm)
    return res[:-1], res[-1]


def _split_wait(name, routes, in_flight, after):
    n = (len(in_flight) - 2) // 2
    hbm = pl.BlockSpec(memory_space=pltpu.HBM)
    sem = pl.BlockSpec(memory_space=pltpu.SEMAPHORE)
    send_sems, recv_sems, *bufs = in_flight

    def body(*refs):
        src_refs, land_refs = refs[:n], refs[n : 2 * n]
        send_ref, recv_ref = refs[2 * n], refs[2 * n + 1]
        for k, (a, peer, src_slot, _, here) in enumerate(routes(n)):
            cp = pltpu.make_async_remote_copy(
                src_ref=src_refs[a].at[src_slot], dst_ref=land_refs[a].at[here],
                send_sem=send_ref.at[k], recv_sem=recv_ref.at[k], device_id=peer, device_id_type=MESH)
            cp.wait_send()
            cp.wait_recv()

    res = pl.pallas_call(
        body,
        name=name,
        out_shape=[pltpu.HBM(b.shape, b.dtype) for b in bufs],
        in_specs=[hbm] * (2 * n) + [sem, sem] + [pl.BlockSpec(memory_space=pl.ANY)] * len(after),
        out_specs=[hbm] * (2 * n),
        input_output_aliases={i: i for i in range(2 * n)},
        compiler_params=pltpu.CompilerParams(has_side_effects=pltpu.SideEffectType.DATAFLOW_SIDE_EFFECTING),
    )(*bufs, send_sems, recv_sems, *after)
    return res[:n], res[n:]


def kernel(x, mem, norm_mix_g, w_in, conv_w, gm_ln_g, gm_ln_b, gm_ws, gm_bs, w_out, norm_x_g, norm_mem_g, w_q, w_kv, w_xo, norm_final_g, loss_target, m_norm_mix_g, m_w_in, m_conv_w, m_gm_ln_g, m_gm_ln_b, m_gm_ws, m_gm_bs, m_w_out, m_norm_x_g, m_norm_mem_g, m_w_q, m_w_kv, m_w_xo, m_norm_final_g, v_norm_mix_g, v_w_in, v_conv_w, v_gm_ln_g, v_gm_ln_b, v_gm_ws, v_gm_bs, v_w_out, v_norm_x_g, v_norm_mem_g, v_w_q, v_w_kv, v_w_xo, v_norm_final_g):
    dev = 4 * lax.axis_index("x") + 2 * lax.axis_index("y") + lax.axis_index("c")
    core = lax.axis_index("c").astype(jnp.int32).reshape(1)
    x2 = x[0]
    target = loss_target[0]
    pad_taps = lambda t: jnp.pad(t[0], ((0, 5), (0, 0)))

    big_names = ["w_in", "w_out", "w_q", "w_kv", "w_xo"]
    big_w = dict(w_in=w_in[0], w_out=w_out[0], w_q=w_q[0], w_kv=w_kv[0], w_xo=w_xo[0])
    big_m = dict(w_in=m_w_in[0], w_out=m_w_out[0], w_q=m_w_q[0], w_kv=m_w_kv[0], w_xo=m_w_xo[0])
    big_v = dict(w_in=v_w_in[0], w_out=v_w_out[0], w_q=v_w_q[0], w_kv=v_w_kv[0], w_xo=v_w_xo[0])
    shard_axis = dict(w_in=1, w_out=0, w_q=0, w_kv=1, w_xo=0)
    full_shape = dict(w_in=(D_MODEL, IN_DIM), w_out=(MIX_DIM, D_MODEL), w_q=(D_MODEL, D_MODEL),
                      w_kv=(D_MODEL, 2 * D_MODEL), w_xo=(D_MODEL, D_MODEL))
    shard_size = {k: big_w[k].shape[shard_axis[k]] for k in big_names}
    others = big_names[1:]
    gather_spec = lambda k: (full_shape[k], shard_axis[k], shard_size[k])
    reduce_spec = lambda k: (shard_axis[k], shard_size[k])
    ws = gm_ws[0]
    bs_b = jnp.broadcast_to(gm_bs[0][:, :, None], (8, CHUNK, CHUNK))
    g_f = norm_final_g.reshape(1, D_MODEL)

    px, py = lax.axis_index("x"), lax.axis_index("y")
    chip_order = jnp.stack([2 * px + py, 2 * (1 - px) + py, 2 * px + 1 - py, 2 * (1 - px) + 1 - py]).astype(jnp.int32)
    (proj, h1t, w_in_full), ((w_out_full, conv_g),) = _inproj(
        x2, norm_mix_g, big_w["w_in"].astype(BF16), chip_order,
        plans=[_gather_plan([big_w["w_out"].astype(BF16), pad_taps(conv_w)[None]], [gather_spec("w_out"), ((N_DEV, 8, CHUNK), 0, 1)])])
    conv_full = conv_g.transpose(1, 0, 2).reshape(8, D_MODEL)
    attn_w = ["w_q", "w_kv", "w_xo"]
    cat, (gathered,) = _mixer_fwd(
        proj, conv_full, gm_ln_g, gm_ln_b, ws, bs_b,
        plans=[_gather_plan([big_w[k].astype(BF16) for k in attn_w], [gather_spec(k) for k in attn_w])])
    wf = dict(zip(attn_w, gathered), w_out=w_out_full)
    k, vv, mt = _kv_fwd(mem[0], norm_mem_g, wf["w_kv"])
    (dcat, actt, cot, dk, dvv, d_gf, d_gx, sq_err) = _attn(
        x2, cat, target, wf["w_out"], wf["w_q"], wf["w_xo"], k, vv, norm_x_g, g_f)

    sibling_plan = lambda names, grads: _sibling_plan([grads[k] for k in names], [reduce_spec(k) for k in names])
    early = ["w_kv", "w_out", "w_q", "w_xo"]
    big_g = {}
    big_g["w_kv"], d_gmem = _kv_bwd(mem[0], wf["w_kv"], mt, dk, dvv)
    dws, (got_kv,) = _dw_attn(actt, cot, plans=[sibling_plan(early[:1], big_g)])
    big_g.update(zip(early[1:], dws))
    pair_kv, (got_rest,) = _pair_sum("pair_sum_w_kv", [big_g["w_kv"]], got_kv, [reduce_spec("w_kv")], core,
                                     plans=[sibling_plan(early[1:], big_g)])
    pair_rest = _pair_sum("pair_sum_attn", [big_g[k] for k in early[1:]], got_rest, [reduce_spec(k) for k in early[1:]], core)
    (dproj, d_cw, d_lng, d_lnb, d_ws, d_bs), (parts_early,) = _mixer_bwd(
        proj, dcat, conv_full, gm_ln_g, gm_ln_b, ws, bs_b, plans=[_chip_plan(list(pair_kv) + list(pair_rest))])
    half = D_MODEL // 2
    in_spec = reduce_spec("w_in")
    dw_top, _ = _matmul_bf16("dw_in_top", h1t, dproj, rows=(0, half))
    dw_bot, (got_top, (ws_all,)) = _matmul_bf16(
        "dw_in_bot", h1t, dproj, rows=(half, half),
        plans=[_sibling_plan([dw_top], [in_spec]),
               _gather_plan([d_ws.reshape(1, 8 * CHUNK, CHUNK).astype(BF16)], [((N_DEV, 8 * CHUNK, CHUNK), 0, 1)])])
    flat_ws = lambda t: t.reshape(8 * CHUNK, CHUNK)
    updated, (got_bot,) = _adamw(
        "adamw_early",
        [(big_w[k], big_m[k], big_v[k], got_k) for k, got_k in zip(early, parts_early)]
        + [(flat_ws(ws), flat_ws(m_gm_ws), flat_ws(v_gm_ws), ws_all)], steps=STEPS_ADAMW // 2,
        plans=[_sibling_plan([dw_bot], [in_spec])])
    big_out = dict(zip(early, updated[:4]))
    ws_out = updated[4]
    pair_top, pair_bot = _pair_sum("pair_sum_w_in", [dw_top, dw_bot], list(got_top) + list(got_bot), [in_spec, in_spec], core)
    chip_shapes = [p.shape for p in (pair_top, pair_bot)]
    chips_in_flight, token = _split_start("grad_chip_exchange_start", _chip_routes, [pair_top, pair_bot], chip_shapes)
    grad_x, d_gmix = _dh1(dproj, w_in_full, x2, cot, norm_mix_g, after=[token])
    vec_names = ["norm_mix_g", "gm_ln_g", "gm_ln_b", "gm_bs", "norm_x_g", "norm_mem_g", "norm_final_g"]
    vec_g = [d_gmix, d_lng, d_lnb, d_bs, d_gx, d_gmem, d_gf]
    small = jnp.concatenate(vec_g + [sq_err, d_cw], axis=0)[None]
    small_in_flight, small_token = _split_start("small_all_gather_start", _gather_routes, [small], [(N_DEV, 16, D_MODEL)])
    (pair_top, pair_bot), (land_top, land_bot) = _split_wait(
        "grad_chip_exchange_wait", _chip_routes, chips_in_flight, [small_token] + [u[0] for u in updated])
    my_chip = (2 * px + py).astype(jnp.int32).reshape(1)
    (big_out["w_in"],), _ = _adamw(
        "adamw_w_in", [(big_w["w_in"], big_m["w_in"], big_v["w_in"], [land_top, land_bot], [pair_top, pair_bot])],
        steps=STEPS_ADAMW, own_slot=my_chip)
    (small,), (small_all,) = _split_wait("small_all_gather_wait", _gather_routes, small_in_flight, [big_out["w_in"][0]])
    row = lambda t: t.reshape(1, D_MODEL)
    vec_out, conv_out, loss_row = _update_small(
        dev.astype(jnp.int32).reshape(1),
        [norm_mix_g, gm_ln_g, gm_ln_b, row(gm_bs), norm_x_g, norm_mem_g, row(norm_final_g)],
        [m_norm_mix_g, m_gm_ln_g, m_gm_ln_b, row(m_gm_bs), m_norm_x_g, m_norm_mem_g, row(m_norm_final_g)],
        [v_norm_mix_g, v_gm_ln_g, v_gm_ln_b, row(v_gm_bs), v_norm_x_g, v_norm_mem_g, row(v_norm_final_g)],
        *(t.transpose(1, 0, 2) for t in (conv_w, m_conv_w, v_conv_w)), small, small_all)
    loss = loss_row[0, 0]

    def result(name, which):
        if name in big_out:
            return big_out[name][which][None]
        if name == "conv_w":
            return conv_out[which].transpose(1, 0, 2)
        if name == "gm_ws":
            return ws_out[which].reshape(1, 8, CHUNK, CHUNK)
        r = vec_out[vec_names.index(name)][which]
        if name == "gm_bs":
            return r.reshape(1, 8, CHUNK)
        if name == "norm_final_g":
            return r.reshape(D_MODEL)
        return r

    weights = ["norm_mix_g", "w_in", "conv_w", "gm_ln_g", "gm_ln_b", "gm_ws", "gm_bs", "w_out", "norm_x_g", "norm_mem_g", "w_q", "w_kv", "w_xo", "norm_final_g"]
    outs = [loss, grad_x[None]]
    for which in range(4):
        outs += [result(name, which) for name in weights]
    return tuple(outs)
```

```python
import math

import jax
import jax.numpy as jnp
from jax import lax
from jax.experimental import pallas as pl
from jax.experimental.pallas import tpu as pltpu

F32 = jnp.float32
BF16 = jnp.bfloat16
MESH = pl.DeviceIdType.MESH

D_MODEL = 1024
N_DEV = 8
N_CHIP = 4
CHUNK = 128
GROUPS = 7
IN_DIM = GROUPS * D_MODEL
SHARD_IN = IN_DIM // N_DEV
MIX_DIM = 2 * D_MODEL
X_HEADS = 4
X_HEAD_DIM = D_MODEL // X_HEADS
MEM_LEN = 256
EPS = 1e-6
GELU_K0 = math.sqrt(2.0 / math.pi)
GELU_K1 = 0.044715

ADAM_LR = 0.001
ADAM_B1 = 0.9
ADAM_B2 = 0.999
ADAM_EPS = 1e-08
ADAM_WD = 0.01
ADAM_STEP = 10

TM_INPROJ = 512
TM_ATTN = 512
SUB_ATTN = 256
TM_DH1 = 512
TM_DW = 512
TK_DW = 4096
BARRIER_ID = {"sibling": 1, "chips": 2, "all": 3}
STEPS_ADAMW = 4


def _dot(a, b):
    return lax.dot_general(a, b, (((1,), (0,)), ((), ())), preferred_element_type=F32)


def _dot_nt(a, b):
    return lax.dot_general(a, b, (((1,), (1,)), ((), ())), preferred_element_type=F32)


def _rows8(v):
    return jnp.sum(v.reshape(v.shape[0] // 8, 8, v.shape[1]), axis=0)


def _gelu(x, with_grad=True):
    x2 = x * x
    t = jnp.tanh(x * (GELU_K0 + (GELU_K0 * GELU_K1) * x2))
    half = 0.5 + 0.5 * t
    g = x * half
    if not with_grad:
        return g
    return g, half + ((0.5 * x) * (1.0 - t * t)) * (GELU_K0 + (3.0 * GELU_K0 * GELU_K1) * x2)


def _silu(z, with_grad=True):
    s = jax.nn.sigmoid(z)
    zs = z * s
    return (zs, s + zs * (1.0 - s)) if with_grad else zs


def _rms(v):
    r = lax.rsqrt(jnp.mean(v * v, axis=-1, keepdims=True) + EPS)
    return v * r, r


def _rms_bwd(dy_g, vh, r):
    return r * (dy_g - vh * jnp.mean(dy_g * vh, axis=-1, keepdims=True))


def _place():
    return lax.axis_index("x"), lax.axis_index("y"), lax.axis_index("c")


def _other_chips(x, y):
    return [(1 - x, y), (x, 1 - y), (1 - x, 1 - y)]


def _pow2_divisor(n):
    return n & (-n)


def _shard_view(ref, axis, size, d):
    start = d * size
    align = _pow2_divisor(size)
    if align > 1:
        start = pl.multiple_of(start, align)
    idx = [slice(None)] * len(ref.shape)
    idx[axis] = pl.ds(start, size)
    return ref.at[tuple(idx)]


class _Plan:
    def __init__(self, ins, out_shape, sems, start, finish, mid=None, aliases=None, peers="all"):
        self.ins, self.out_shape, self.sems = list(ins), list(out_shape), list(sems)
        self.start, self.mid, self.finish = start, mid, finish
        self.aliases = dict(aliases or {})
        self.peers = peers


def _gather_plan(shards, specs):
    n = len(shards)
    n_copy = 7

    def copies(ins, outs, sems):
        send_sems, recv_sems, local_sems = sems
        x, y, c = _place()
        me, sibling = (x, y, c), (x, y, 1 - c)
        chips = _other_chips(x, y)

        def block(a, p):
            _, axis, size = specs[a]
            return _shard_view(outs[a], axis, size, 4 * p[0] + 2 * p[1] + p[2])

        def copy(a, k, p, to, src=None):
            return pltpu.make_async_remote_copy(
                src_ref=block(a, p) if src is None else src,
                dst_ref=block(a, p),
                send_sem=send_sems.at[a * n_copy + k],
                recv_sem=recv_sems.at[a * n_copy + k],
                device_id=to,
                device_id_type=MESH,
            )

        def mine():
            return [pltpu.make_async_copy(ins[a], block(a, me), local_sems.at[a]) for a in range(n)]

        def first():
            return [cp for a in range(n) for cp in
                    [copy(a, 0, me, sibling, src=ins[a])] + [copy(a, 1 + j, me, (*chip, c), src=ins[a]) for j, chip in enumerate(chips)]]

        def landed():
            return [copy(a, 1 + j, (*chip, c), me) for j, chip in enumerate(chips) for a in range(n)]

        def passed():
            return [copy(a, 4 + j, (*chip, c), sibling) for j, chip in enumerate(chips) for a in range(n)]

        def last():
            return [cp for a in range(n) for cp in
                    [copy(a, 0, sibling, me)] + [copy(a, 4 + j, (*chip, 1 - c), me) for j, chip in enumerate(chips)]]

        return mine, first, landed, passed, last

    def start(ins, outs, sems):
        mine, first, _, _, _ = copies(ins, outs, sems)
        for cp in mine() + first():
            cp.start()

    def mid(ins, outs, sems):
        _, _, landed, passed, _ = copies(ins, outs, sems)
        for got, fwd in zip(landed(), passed()):
            got.wait_recv()
            fwd.start()

    def finish(ins, outs, sems):
        mine, first, _, passed, last = copies(ins, outs, sems)
        for cp in last():
            cp.wait_recv()
        for cp in first() + passed():
            cp.wait_send()
        for cp in mine():
            cp.wait()

    return _Plan(
        shards,
        [jax.ShapeDtypeStruct(full, s.dtype) for s, (full, _, _) in zip(shards, specs)],
        [pltpu.SemaphoreType.DMA((n * n_copy,)), pltpu.SemaphoreType.DMA((n * n_copy,)), pltpu.SemaphoreType.DMA((n,))],
        start, finish, mid,
    )


def _sibling_plan(grads, specs):
    n = len(grads)

    def shard_shape(a):
        axis, size = specs[a]
        shp = list(grads[a].shape)
        shp[axis] = size
        return tuple(shp)

    def copies(ins, outs, sems):
        send_sems, recv_sems = sems
        x, y, c = _place()
        return [
            pltpu.make_async_remote_copy(
                src_ref=_shard_view(ins[a], specs[a][0], specs[a][1], 2 * q + (1 - c)),
                dst_ref=outs[a].at[q],
                send_sem=send_sems.at[a * N_CHIP + q],
                recv_sem=recv_sems.at[a * N_CHIP + q],
                device_id=(x, y, 1 - c),
                device_id_type=MESH,
            )
            for a in range(n) for q in range(N_CHIP)
        ]

    def start(ins, outs, sems):
        for cp in copies(ins, outs, sems):
            cp.start()

    def finish(ins, outs, sems):
        for cp in copies(ins, outs, sems):
            cp.wait()

    return _Plan(
        grads,
        [jax.ShapeDtypeStruct((N_CHIP, *shard_shape(a)), grads[a].dtype) for a in range(n)],
        [pltpu.SemaphoreType.DMA((n * N_CHIP,)), pltpu.SemaphoreType.DMA((n * N_CHIP,))],
        start, finish, peers="sibling",
    )


def _chip_plan(parts, full_rows=None, row0=None, into=None):
    n = len(parts)
    full_rows = [p.shape[1] for p in parts] if full_rows is None else full_rows
    row0 = [0] * n if row0 is None else row0

    def copies(ins, outs, sems):
        send_sems, recv_sems, local_sems = sems
        x, y, c = _place()
        my_chip = 2 * x + y
        chips = _other_chips(x, y)

        def land(a, q):
            return outs[a].at[q, pl.ds(row0[a], parts[a].shape[1]), :]

        def mine():
            return [pltpu.make_async_copy(ins[a].at[my_chip], land(a, my_chip), local_sems.at[a]) for a in range(n)]

        def remote(sending):
            return [
                pltpu.make_async_remote_copy(
                    src_ref=ins[a].at[2 * chip[0] + chip[1]],
                    dst_ref=land(a, my_chip if sending else 2 * chip[0] + chip[1]),
                    send_sem=send_sems.at[a * 3 + j], recv_sem=recv_sems.at[a * 3 + j],
                    device_id=(*chip, c), device_id_type=MESH)
                for a in range(n) for j, chip in enumerate(chips)
            ]

        return mine, remote

    def start(ins, outs, sems):
        mine, remote = copies(ins, outs, sems)
        for cp in mine() + remote(True):
            cp.start()

    def finish(ins, outs, sems):
        mine, remote = copies(ins, outs, sems)
        for cp in remote(False):
            cp.wait_recv()
        for cp in remote(True):
            cp.wait_send()
        for cp in mine():
            cp.wait()

    return _Plan(
        list(parts) + (list(into) if into is not None else []),
        [jax.ShapeDtypeStruct((N_CHIP, full_rows[a], parts[a].shape[2]), parts[a].dtype) for a in range(n)],
        [pltpu.SemaphoreType.DMA((n * 3,)), pltpu.SemaphoreType.DMA((n * 3,)), pltpu.SemaphoreType.DMA((n,))],
        start, finish,
        aliases={n + a: a for a in range(n)} if into is not None else None, peers="chips",
    )


def _call(body, *, name, operands, out_shape, grid=(), in_specs=None, out_specs=None, scratch_shapes=(),
          plans=(), aliases=None, prefetch=(), start_after_body=False, pass_on_at=0.6, body_peers=None):
    operands, out_shape, scratch_shapes = list(operands), list(out_shape), list(scratch_shapes)
    n_pf, n_in, n_out, n_scr = len(prefetch), len(operands), len(out_shape), len(scratch_shapes)
    in_vmem = pl.BlockSpec(memory_space=pltpu.VMEM)
    in_hbm = pl.BlockSpec(memory_space=pl.ANY)
    in_specs = [in_vmem] * n_in if in_specs is None else list(in_specs)
    out_specs = [in_vmem] * n_out if out_specs is None else list(out_specs)
    io_alias = {n_pf + i: o for i, o in (aliases or {}).items()}
    pos_in, pos_out = n_pf + n_in, n_out
    for p in plans:
        for i, o in p.aliases.items():
            io_alias[pos_in + i] = pos_out + o
        pos_in += len(p.ins)
        pos_out += len(p.out_shape)
    steps = math.prod(grid) if grid else 1
    mid_step = min(steps - 1, int(steps * pass_on_at))
    patterns = {p.peers for p in plans} | ({body_peers} if body_peers else set())
    pattern = None if not patterns else patterns.pop() if len(patterns) == 1 else "all"

    def shake_hands():
        x, y, c = _place()
        peers = [(x, y, 1 - c)] if pattern in ("sibling", "all") else []
        peers += [(*chip, c) for chip in _other_chips(x, y)] if pattern in ("chips", "all") else []
        barrier = pltpu.get_barrier_semaphore()
        for peer in peers:
            pl.semaphore_signal(barrier, inc=1, device_id=peer, device_id_type=MESH)
        pl.semaphore_wait(barrier, len(peers))

    def wrapped(*refs):
        refs = list(refs)
        take = lambda k: [refs.pop(0) for _ in range(k)]
        pf = take(n_pf)
        ins, p_ins = take(n_in), [take(len(p.ins)) for p in plans]
        outs, p_outs = take(n_out), [take(len(p.out_shape)) for p in plans]
        scr, p_sems = take(n_scr), [take(len(p.sems)) for p in plans]
        step = 0
        for ax, g in enumerate(grid):
            step = step * g + pl.program_id(ax)

        def hook(kind, at):
            todo = [(getattr(p, kind), a, b, c) for p, a, b, c in zip(plans, p_ins, p_outs, p_sems) if getattr(p, kind)]

            def run():
                for fn, a, b, c in todo:
                    fn(a, b, c)

            if todo and grid:
                pl.when(step == at)(run)
            elif todo:
                run()

        if pattern and grid:
            pl.when(step == 0)(shake_hands)
        elif pattern:
            shake_hands()
        if not start_after_body:
            hook("start", 0)
        if body is not None:
            body(*pf, *ins, *outs, *scr)
        if start_after_body:
            hook("start", 0)
        hook("mid", mid_step)
        hook("finish", steps - 1)

    layout = dict(
        grid=grid,
        in_specs=in_specs + [in_hbm] * sum(len(p.ins) for p in plans),
        out_specs=out_specs + [in_hbm] * sum(len(p.out_shape) for p in plans),
        scratch_shapes=scratch_shapes + [s for p in plans for s in p.sems],
    )
    if n_pf:
        layout = dict(grid_spec=pltpu.PrefetchScalarGridSpec(num_scalar_prefetch=n_pf, **layout))
    res = pl.pallas_call(
        wrapped,
        name=name,
        out_shape=out_shape + [o for p in plans for o in p.out_shape],
        input_output_aliases=io_alias,
        compiler_params=pltpu.CompilerParams(
            dimension_semantics=("arbitrary",) * len(grid) if grid else None,
            collective_id=BARRIER_ID[pattern] if pattern else None),
        **layout,
    )(*prefetch, *operands, *[a for p in plans for a in p.ins])
    res = list(res)
    outs, plan_outs = res[:n_out], []
    pos = n_out
    for p in plans:
        plan_outs.append(res[pos : pos + len(p.out_shape)])
        pos += len(p.out_shape)
    return outs, plan_outs


def _inproj(x, g, w_shard, order, plans=()):
    s = x.shape[0]
    tm = min(TM_INPROJ, s)
    nt = s // tm
    wide = 2 * SHARD_IN
    fwd_at = nt // 2

    def body(order_ref, x_ref, g_ref, w_hbm, proj_ref, h1t_ref, wfull_hbm,
             w_scr, h1_scr, send_sems, recv_sems, local_sems):
        k, i = pl.program_id(0), pl.program_id(1)
        px, py, c = _place()
        me, sibling = (px, py, c), (px, py, 1 - c)
        chips = _other_chips(px, py)

        def half(slot, core):
            return w_scr.at[slot, :, pl.ds(pl.multiple_of(core * SHARD_IN, 128), SHARD_IN)]

        def copy(n, slot, core, to, src=None):
            return pltpu.make_async_remote_copy(
                src_ref=half(slot, core) if src is None else src, dst_ref=half(slot, core),
                send_sem=send_sems.at[n], recv_sem=recv_sems.at[n], device_id=to, device_id_type=MESH)

        def mine():
            return pltpu.make_async_copy(w_hbm, half(0, c), local_sems.at[0])

        def first():
            return [copy(0, 0, c, sibling, src=w_hbm)] + [copy(1 + j, 1 + j, c, (*chips[j], c), src=w_hbm) for j in range(2)]

        def relay():
            to = (c * px + (1 - c) * (1 - px), c * (1 - py) + (1 - c) * py, c)
            return pltpu.make_async_remote_copy(
                src_ref=half(2 - c, c), dst_ref=half(3, c), send_sem=send_sems.at[3], recv_sem=recv_sems.at[3],
                device_id=to, device_id_type=MESH)

        def save(slot):
            cols = pl.ds(pl.multiple_of(order_ref[slot] * wide, 128), wide)
            return pltpu.make_async_copy(w_scr.at[slot], wfull_hbm.at[:, cols], local_sems.at[1 + slot])

        @pl.when((k == 0) & (i == 0))
        def _():
            mine().start()
            for cp in first():
                cp.start()
            mine().wait()
            copy(0, 0, 1 - c, me).wait_recv()

        for j in range(3):
            @pl.when((k == j + 1) & (i == 0))
            def _():
                copy(4 + j, 1 + j, 1 - c, me).wait_recv()

        @pl.when((k == 3) & (i == 0))
        def _():
            for slot in range(N_CHIP):
                save(slot).start()

        @pl.when(k == 0)
        def _():
            xh, _ = _rms(x_ref[...])
            h = xh * g_ref[...]
            h1t_ref[...] = h.T.astype(BF16)
            h1_scr[pl.ds(pl.multiple_of(i * tm, tm), tm), :] = h.astype(BF16)

        proj_ref[...] = _dot(h1_scr[pl.ds(pl.multiple_of(i * tm, tm), tm), :], w_scr[k]).astype(BF16)

        for j in range(3):
            @pl.when((k == 0) & (i == nt - 1) if j < 2 else (k == 2) & (i == fwd_at))
            def _():
                copy(1 + j, 1 + j, c, me).wait_recv()
                copy(4 + j, 1 + j, c, sibling).start()
                if j < 2:
                    pl.when(c == 1 - j)(lambda: relay().start())

        @pl.when((k == 3) & (i == nt - 1))
        def _():
            for cp in first() + [relay()] + [copy(4 + j, 1 + j, c, sibling) for j in range(3)]:
                cp.wait_send()
            for slot in range(N_CHIP):
                save(slot).wait()

    once = lambda k, i, order_ref: (jnp.where(k == 0, i, nt - 1), 0)
    return _call(
        body,
        name="inproj",
        prefetch=[order],
        operands=[x, g, w_shard],
        grid=(N_CHIP, nt),
        in_specs=[
            pl.BlockSpec((tm, D_MODEL), once),
            pl.BlockSpec((1, D_MODEL), lambda k, i, order_ref: (0, 0)),
            pl.BlockSpec(memory_space=pl.ANY),
        ],
        out_specs=[
            pl.BlockSpec((tm, wide), lambda k, i, order_ref: (i, order_ref[k])),
            pl.BlockSpec((D_MODEL, tm), lambda k, i, order_ref: (0, jnp.where(k == 0, i, nt - 1))),
            pl.BlockSpec(memory_space=pl.ANY),
        ],
        out_shape=[
            jax.ShapeDtypeStruct((s, IN_DIM), BF16),
            jax.ShapeDtypeStruct((D_MODEL, s), BF16),
            jax.ShapeDtypeStruct((D_MODEL, IN_DIM), BF16),
        ],
        scratch_shapes=[
            pltpu.VMEM((N_CHIP, D_MODEL, wide), BF16),
            pltpu.VMEM((s, D_MODEL), BF16),
            pltpu.SemaphoreType.DMA((7,)),
            pltpu.SemaphoreType.DMA((7,)),
            pltpu.SemaphoreType.DMA((1 + N_CHIP,)),
        ],
        plans=plans,
        start_after_body=True,
        pass_on_at=0.8,
        body_peers="all",
    )


def _rows_from_above(v, above):
    row = lax.broadcasted_iota(jnp.int32, (8, v.shape[1]), 0)
    r1, r2 = pltpu.roll(v, 1, 0), pltpu.roll(v, 2, 0)
    top1 = jnp.where(row == 0, above[7:8], r1[0:8])
    top2 = jnp.where(row == 0, above[6:7], jnp.where(row == 1, above[7:8], r2[0:8]))
    return jnp.concatenate([top1, r1[8:]], axis=0), jnp.concatenate([top2, r2[8:]], axis=0)


def _rows_from_below(v, below):
    n = v.shape[0]
    row = lax.broadcasted_iota(jnp.int32, (8, v.shape[1]), 0)
    r1, r2 = pltpu.roll(v, n - 1, 0), pltpu.roll(v, n - 2, 0)
    end1 = jnp.where(row == 7, below[0:1], r1[n - 8 :])
    end2 = jnp.where(row == 6, below[0:1], jnp.where(row == 7, below[1:2], r2[n - 8 :]))
    return jnp.concatenate([r1[: n - 8], end1], axis=0), jnp.concatenate([r2[: n - 8], end2], axis=0)


def _causal_mask():
    row = lax.broadcasted_iota(jnp.int32, (CHUNK, CHUNK), 0)
    col = lax.broadcasted_iota(jnp.int32, (CHUNK, CHUNK), 1)
    return row, row >= col


def _mixer_fwd(proj, conv_w, ln_g, ln_b, ws, bs_b, plans=()):
    s = proj.shape[0]
    nt = s // CHUNK

    def body(proj_ref, cw_ref, lng_ref, lnb_ref, ws_ref, bsb_ref, cat_ref, pcar_ref):
        @pl.when(pl.program_id(0) == 0)
        def _():
            pcar_ref[...] = jnp.zeros_like(pcar_ref)

        row, tril = _causal_mask()
        for j in range(8):
            cs = slice(CHUNK * j, CHUNK * (j + 1))

            def grp(k):
                return proj_ref[:, k * D_MODEL + CHUNK * j : k * D_MODEL + CHUNK * (j + 1)].astype(F32)

            w = cw_ref[:, cs]
            p = grp(1) * grp(2)
            p1, p2 = _rows_from_above(p, pcar_ref[:, cs])
            pcar_ref[:, cs] = p[CHUNK - 8 :]
            cv = w[0:1] * p2 + w[1:2] * p1 + w[2:3] * p
            sa = _silu(grp(3), with_grad=False)
            cat_ref[:, cs] = ((grp(0) * cv) * sa).astype(BF16)
            gu = _gelu(grp(4), with_grad=False)
            gv = _gelu(grp(5), with_grad=False)
            dv = gv - jnp.mean(gv, axis=-1, keepdims=True)
            vn = dv * lax.rsqrt(jnp.mean(dv * dv, axis=-1, keepdims=True) + EPS)
            vn = vn * lng_ref[:, cs] + lnb_ref[:, cs]
            wc = jnp.where(tril, ws_ref[j], 0.0).astype(BF16)
            sp = _dot(wc, vn.astype(BF16)) + bsb_ref[j]
            sb = _silu(grp(6), with_grad=False)
            cat_ref[:, D_MODEL + CHUNK * j : D_MODEL + CHUNK * (j + 1)] = ((gu * sp) * sb).astype(BF16)

    whole = lambda shape: pl.BlockSpec(shape, lambda i: (0,) * len(shape))
    (cat,), plan_outs = _call(
        body,
        name="mixer_fwd",
        operands=[proj, conv_w, ln_g, ln_b, ws, bs_b],
        grid=(nt,),
        in_specs=[
            pl.BlockSpec((CHUNK, IN_DIM), lambda i: (i, 0)),
            whole((8, D_MODEL)),
            whole((1, D_MODEL)),
            whole((1, D_MODEL)),
            whole((8, CHUNK, CHUNK)),
            whole((8, CHUNK, CHUNK)),
        ],
        out_specs=[pl.BlockSpec((CHUNK, MIX_DIM), lambda i: (i, 0))],
        out_shape=[jax.ShapeDtypeStruct((s, MIX_DIM), BF16)],
        scratch_shapes=[pltpu.VMEM((8, D_MODEL), F32)],
        plans=plans,
        pass_on_at=0.85,
    )
    return cat, plan_outs


def _mixer_bwd(proj, dcat, conv_w, ln_g, ln_b, ws, bs_b, plans=()):
    s = proj.shape[0]
    nt = s // CHUNK

    def body(proj_ref, halo_ref, dcat_ref, cw_ref, lng_ref, lnb_ref, ws_ref, bsb_ref,
             dproj_ref, dcw_ref, dlng_ref, dlnb_ref, dws_ref, dbs_ref,
             car_ref, acc_cw, acc_lng, acc_lnb, acc_bs, wc_scr, wct_scr):
        i = pl.program_id(0)
        tile = nt - 1 - i
        row, tril = _causal_mask()

        @pl.when(i == 0)
        def _():
            for j in range(8):
                wc = jnp.where(tril, ws_ref[j], 0.0)
                wc_scr[j] = wc.astype(BF16)
                wct_scr[j] = wc.T.astype(BF16)
            car_ref[...] = jnp.zeros_like(car_ref)
            acc_cw[...] = jnp.zeros_like(acc_cw)
            acc_lng[...] = jnp.zeros_like(acc_lng)
            acc_lnb[...] = jnp.zeros_like(acc_lnb)
            acc_bs[...] = jnp.zeros_like(acc_bs)
            dws_ref[...] = jnp.zeros_like(dws_ref)

        has_prev = jnp.where(tile > 0, 1.0, 0.0).astype(F32)
        for j in range(8):
            cs = slice(CHUNK * j, CHUNK * (j + 1))

            def col(k):
                return slice(k * D_MODEL + CHUNK * j, k * D_MODEL + CHUNK * (j + 1))

            def grp(k):
                return proj_ref[:, col(k)].astype(F32)

            w = cw_ref[:, cs]
            gb, gc, xa = grp(0), grp(1), grp(2)
            p = gc * xa
            pprev = halo_ref[:, col(1)].astype(F32) * halo_ref[:, col(2)].astype(F32) * has_prev
            p1, p2 = _rows_from_above(p, pprev[8:16])
            cv = w[0:1] * p2 + w[1:2] * p1 + w[2:3] * p
            za = grp(3)
            sa, dsa = _silu(za)
            da = dcat_ref[:, cs].astype(F32)
            da_gb = da * gb
            dproj_ref[:, col(0)] = ((da * sa) * cv).astype(BF16)
            dproj_ref[:, col(3)] = ((da_gb * cv) * dsa).astype(BF16)
            dcv = da_gb * sa
            d1, d2 = _rows_from_below(dcv, car_ref[:, cs])
            car_ref[:, cs] = dcv[0:8]
            dp = w[2:3] * dcv + w[1:2] * d1 + w[0:1] * d2
            dproj_ref[:, col(1)] = (dp * xa).astype(BF16)
            dproj_ref[:, col(2)] = (dp * gc).astype(BF16)
            acc_cw[0, :, cs] += _rows8(dcv * p2)
            acc_cw[1, :, cs] += _rows8(dcv * p1)
            acc_cw[2, :, cs] += _rows8(dcv * p)
            gu, dgu = _gelu(grp(4))
            gv, dgv = _gelu(grp(5))
            dv = gv - jnp.mean(gv, axis=-1, keepdims=True)
            rstd = lax.rsqrt(jnp.mean(dv * dv, axis=-1, keepdims=True) + EPS)
            vnh = dv * rstd
            lng = lng_ref[:, cs]
            vnb = (vnh * lng + lnb_ref[:, cs]).astype(BF16)
            sp = _dot(wc_scr[j], vnb) + bsb_ref[j]
            zb = grp(6)
            sb, dsb = _silu(zb)
            db = dcat_ref[:, D_MODEL + CHUNK * j : D_MODEL + CHUNK * (j + 1)].astype(F32)
            db_sp, db_gu = db * sp, db * gu
            dproj_ref[:, col(4)] = ((db_sp * sb) * dgu).astype(BF16)
            dproj_ref[:, col(6)] = ((db_sp * gu) * dsb).astype(BF16)
            dsp = db_gu * sb
            acc_bs[j] += dsp
            dspb = dsp.astype(BF16)
            dws_ref[j] += _dot_nt(dspb, vnb)
            dvn = _dot(wct_scr[j], dspb)
            acc_lnb[:, cs] += _rows8(dvn)
            acc_lng[:, cs] += _rows8(dvn * vnh)
            dvh = dvn * lng
            dgvv = rstd * (dvh - jnp.mean(dvh, axis=-1, keepdims=True) - vnh * jnp.mean(dvh * vnh, axis=-1, keepdims=True))
            dproj_ref[:, col(5)] = (dgvv * dgv).astype(BF16)

        @pl.when(i == nt - 1)
        def _():
            dlng_ref[...] = jnp.sum(acc_lng[...], axis=0, keepdims=True)
            dlnb_ref[...] = jnp.sum(acc_lnb[...], axis=0, keepdims=True)
            dcw_ref[...] = jnp.zeros_like(dcw_ref)
            for k in range(3):
                dcw_ref[k : k + 1, :] = jnp.sum(acc_cw[k], axis=0, keepdims=True)
            for j in range(8):
                dws_ref[j] = jnp.where(tril, dws_ref[j], 0.0)
                dbs_ref[:, CHUNK * j : CHUNK * (j + 1)] = jnp.sum(acc_bs[j].T, axis=0, keepdims=True)

    whole = lambda shape: pl.BlockSpec(shape, lambda i: (0,) * len(shape))
    halo_rows = 16
    per = CHUNK // halo_rows
    return _call(
        body,
        name="mixer_bwd",
        operands=[proj, proj, dcat, conv_w, ln_g, ln_b, ws, bs_b],
        plans=plans,
        grid=(nt,),
        in_specs=[
            pl.BlockSpec((CHUNK, IN_DIM), lambda i: (nt - 1 - i, 0)),
            pl.BlockSpec((halo_rows, IN_DIM), lambda i: (jnp.maximum((nt - 1 - i) * per - 1, 0), 0)),
            pl.BlockSpec((CHUNK, MIX_DIM), lambda i: (nt - 1 - i, 0)),
            whole((8, D_MODEL)),
            whole((1, D_MODEL)),
            whole((1, D_MODEL)),
            whole((8, CHUNK, CHUNK)),
            whole((8, CHUNK, CHUNK)),
        ],
        out_specs=[
            pl.BlockSpec((CHUNK, IN_DIM), lambda i: (nt - 1 - i, 0)),
            whole((8, D_MODEL)),
            whole((1, D_MODEL)),
            whole((1, D_MODEL)),
            whole((8, CHUNK, CHUNK)),
            whole((1, D_MODEL)),
        ],
        out_shape=[
            jax.ShapeDtypeStruct((s, IN_DIM), BF16),
            jax.ShapeDtypeStruct((8, D_MODEL), F32),
            jax.ShapeDtypeStruct((1, D_MODEL), F32),
            jax.ShapeDtypeStruct((1, D_MODEL), F32),
            jax.ShapeDtypeStruct((8, CHUNK, CHUNK), F32),
            jax.ShapeDtypeStruct((1, D_MODEL), F32),
        ],
        scratch_shapes=[
            pltpu.VMEM((8, D_MODEL), F32),
            pltpu.VMEM((3, 8, D_MODEL), F32),
            pltpu.VMEM((8, D_MODEL), F32),
            pltpu.VMEM((8, D_MODEL), F32),
            pltpu.VMEM((8, CHUNK, CHUNK), F32),
            pltpu.VMEM((8, CHUNK, CHUNK), BF16),
            pltpu.VMEM((8, CHUNK, CHUNK), BF16),
        ],
    )


def _kv_fwd(mem, g_mem, w_kv):
    def body(mem_ref, g_ref, w_ref, k_ref, vv_ref, mt_ref):
        mh, _ = _rms(mem_ref[...])
        m = mh * g_ref[...]
        kv = _dot(m.astype(BF16), w_ref[...])
        k_ref[...] = kv[:, :D_MODEL].astype(BF16)
        vv_ref[...] = kv[:, D_MODEL:].astype(BF16)
        mt_ref[...] = m.T.astype(BF16)

    return pl.pallas_call(
        body,
        name="kv_fwd",
        out_shape=[
            jax.ShapeDtypeStruct((MEM_LEN, D_MODEL), BF16),
            jax.ShapeDtypeStruct((MEM_LEN, D_MODEL), BF16),
            jax.ShapeDtypeStruct((D_MODEL, MEM_LEN), BF16),
        ],
    )(mem, g_mem, w_kv)


def _kv_bwd(mem, w_kv, mt, dk, dvv):
    def body(mem_ref, w_ref, mt_ref, dk_ref, dvv_ref, dw_ref, dg_ref):
        dkv = jnp.concatenate([dk_ref[...], dvv_ref[...]], axis=1).astype(BF16)
        dw_ref[...] = _dot(mt_ref[...], dkv).astype(BF16)
        dm = _dot_nt(dkv, w_ref[...])
        mh, _ = _rms(mem_ref[...])
        dg_ref[...] = jnp.sum(dm * mh, axis=0, keepdims=True)

    return pl.pallas_call(
        body,
        name="kv_bwd",
        out_shape=[jax.ShapeDtypeStruct((D_MODEL, 2 * D_MODEL), BF16), jax.ShapeDtypeStruct((1, D_MODEL), F32)],
    )(mem, w_kv, mt, dk, dvv)


def _attn(x, cat, target, w_out, w_q, w_xo, k, vv, g_x, g_f):
    s = x.shape[0]
    tm = min(TM_ATTN, s)
    sub = min(SUB_ATTN, tm)
    nt = s // tm
    scale = 1.0 / math.sqrt(X_HEAD_DIM)

    def body(x_ref, cat_ref, t_ref, wout_ref, wq_ref, wxo_ref, k_ref, vv_ref, gx_ref, gf_ref,
             dcat_ref, actt_ref, cot_ref,
             dk_ref, dvv_ref, dgf_ref, dgx_ref, loss_ref,
             p_scr, q_scr, o_scr, dq_scr, acc_gf, acc_gx, acc_loss):
        i = pl.program_id(0)

        @pl.when(i == 0)
        def _():
            dk_ref[...] = jnp.zeros_like(dk_ref)
            dvv_ref[...] = jnp.zeros_like(dvv_ref)
            acc_gf[...] = jnp.zeros_like(acc_gf)
            acc_gx[...] = jnp.zeros_like(acc_gx)
            acc_loss[...] = jnp.zeros_like(acc_loss)

        subs = [slice(u * sub, (u + 1) * sub) for u in range(tm // sub)]
        heads = [slice(X_HEAD_DIM * h, X_HEAD_DIM * (h + 1)) for h in range(X_HEADS)]
        gx, gf = gx_ref[...], gf_ref[...]
        x1, x1h, r2 = [], [], []
        for rs in subs:
            cat = cat_ref[rs, :]
            actt_ref[0:MIX_DIM, rs] = cat.astype(F32).T.astype(BF16)
            v = x_ref[rs, :] + _dot(cat, wout_ref[...])
            vh, r = _rms(v)
            h2 = vh * gx
            actt_ref[MIX_DIM : MIX_DIM + D_MODEL, rs] = h2.T.astype(BF16)
            q_scr[rs, :] = _dot(h2.astype(BF16), wq_ref[...]).astype(BF16)
            x1.append(v), x1h.append(vh), r2.append(r)
        for rs in subs:
            for hs in heads:
                sc = _dot_nt(q_scr[rs, hs], k_ref[:, hs]) * scale
                e = jnp.exp(sc - jnp.max(sc, axis=-1, keepdims=True))
                p = e / jnp.sum(e, axis=-1, keepdims=True)
                p_scr[rs, hs] = p
                o_scr[rs, hs] = _dot(p.astype(BF16), vv_ref[:, hs])
        dx2, sq, d_gf = [], None, None
        for u, rs in enumerate(subs):
            o = o_scr[rs, :]
            actt_ref[MIX_DIM + D_MODEL :, rs] = o.T.astype(BF16)
            x2 = x1[u] + _dot(o.astype(BF16), wxo_ref[...])
            x2h, r3 = _rms(x2)
            err = x2h * gf - t_ref[rs, :]
            dy = err * (1.0 / D_MODEL)
            sq = _rows8(err * err) if sq is None else sq + _rows8(err * err)
            d_gf = _rows8(dy * x2h) if d_gf is None else d_gf + _rows8(dy * x2h)
            dx2.append(_rms_bwd(dy * gf, x2h, r3))
        acc_loss[...] += sq
        acc_gf[...] += d_gf
        d_vv, d_k = [None] * X_HEADS, [None] * X_HEADS
        for u, rs in enumerate(subs):
            dx2b = dx2[u].astype(BF16)
            cot_ref[2, rs, :] = dx2b
            dob = _dot_nt(dx2b, wxo_ref[...]).astype(BF16)
            for h, hs in enumerate(heads):
                p = p_scr[rs, hs]
                dp = _dot_nt(dob[:, hs], vv_ref[:, hs])
                ds = p * (dp - jnp.sum(dp * p, axis=-1, keepdims=True)) * scale
                dq_scr[rs, hs] = _dot(ds.astype(BF16), k_ref[:, hs]).astype(BF16)
                dvv_h = _dot(p.T.astype(BF16), dob[:, hs])
                dk_h = _dot(ds.T.astype(BF16), q_scr[rs, hs])
                d_vv[h] = dvv_h if d_vv[h] is None else d_vv[h] + dvv_h
                d_k[h] = dk_h if d_k[h] is None else d_k[h] + dk_h
        for h, hs in enumerate(heads):
            dvv_ref[:, hs] += d_vv[h]
            dk_ref[:, hs] += d_k[h]
        d_gx = None
        for u, rs in enumerate(subs):
            dqb = dq_scr[rs, :]
            cot_ref[1, rs, :] = dqb
            dh2 = _dot_nt(dqb, wq_ref[...])
            d_gx = _rows8(dh2 * x1h[u]) if d_gx is None else d_gx + _rows8(dh2 * x1h[u])
            dx1 = dx2[u] + _rms_bwd(dh2 * gx, x1h[u], r2[u])
            dx1b = dx1.astype(BF16)
            cot_ref[0, rs, :] = dx1b
            dcat_ref[rs, :] = _dot_nt(dx1b, wout_ref[...]).astype(BF16)
        acc_gx[...] += d_gx

        @pl.when(i == nt - 1)
        def _():
            dgf_ref[...] = jnp.sum(acc_gf[...], axis=0, keepdims=True)
            dgx_ref[...] = jnp.sum(acc_gx[...], axis=0, keepdims=True)
            loss_ref[...] = jnp.sum(acc_loss[...], axis=0, keepdims=True)

    whole = lambda shape: pl.BlockSpec(shape, lambda i: (0,) * len(shape))
    once = lambda shape: pl.BlockSpec(shape, lambda i: (0,) * len(shape), pipeline_mode=pl.Buffered(1))
    rows = lambda width: pl.BlockSpec((tm, width), lambda i: (i, 0))
    cols = lambda height: pl.BlockSpec((height, tm), lambda i: (0, i))
    vec = jax.ShapeDtypeStruct((1, D_MODEL), F32)
    return pl.pallas_call(
        body,
        name="attn",
        grid=(nt,),
        in_specs=[
            rows(D_MODEL), rows(MIX_DIM), rows(D_MODEL),
            once((MIX_DIM, D_MODEL)), once((D_MODEL, D_MODEL)), once((D_MODEL, D_MODEL)),
            once((MEM_LEN, D_MODEL)), once((MEM_LEN, D_MODEL)),
            whole((1, D_MODEL)), whole((1, D_MODEL)),
        ],
        out_specs=[
            rows(MIX_DIM), cols(2 * MIX_DIM),
            pl.BlockSpec((3, tm, D_MODEL), lambda i: (0, i, 0)),
            whole((MEM_LEN, D_MODEL)), whole((MEM_LEN, D_MODEL)),
            whole((1, D_MODEL)), whole((1, D_MODEL)), whole((1, D_MODEL)),
        ],
        out_shape=[
            jax.ShapeDtypeStruct((s, MIX_DIM), BF16),
            jax.ShapeDtypeStruct((2 * MIX_DIM, s), BF16),
            jax.ShapeDtypeStruct((3, s, D_MODEL), BF16),
            jax.ShapeDtypeStruct((MEM_LEN, D_MODEL), F32),
            jax.ShapeDtypeStruct((MEM_LEN, D_MODEL), F32),
            vec, vec, vec,
        ],
        scratch_shapes=[
            pltpu.VMEM((tm, D_MODEL), F32),
            pltpu.VMEM((tm, D_MODEL), BF16),
            pltpu.VMEM((tm, D_MODEL), F32),
            pltpu.VMEM((tm, D_MODEL), BF16),
            pltpu.VMEM((8, D_MODEL), F32),
            pltpu.VMEM((8, D_MODEL), F32),
            pltpu.VMEM((8, D_MODEL), F32),
        ],
        compiler_params=pltpu.CompilerParams(dimension_semantics=("arbitrary",)),
    )(x, cat, target, w_out, w_q, w_xo, k, vv, g_x, g_f)


def _matmul_bf16(name, a, b, rows=None, plans=()):
    first, m = (0, a.shape[0]) if rows is None else rows
    kk = a.shape[1]
    n = b.shape[1]
    tm, tn, tk = min(m, TM_DW), min(n, D_MODEL), min(kk, TK_DW)
    nk = kk // tk
    nj = n // tn
    i0 = first // tm
    total = (m // tm) * nj * nk
    ring_slots = 3

    def body(a_ref, b_hbm, o_ref, *scratch):
        ring, sems = scratch[-2:]
        step = (pl.program_id(0) * nj + pl.program_id(1)) * nk + pl.program_id(2)

        def fetch(t):
            j, k, slot = (t // nk) % nj, t % nk, t % ring_slots
            return pltpu.make_async_copy(b_hbm.at[pl.ds(k * tk, tk), pl.ds(j * tn, tn)], ring.at[slot], sems.at[slot])

        @pl.when(step == 0)
        def _():
            for t in range(min(ring_slots - 1, total)):
                fetch(t).start()

        @pl.when(step + ring_slots - 1 < total)
        def _():
            fetch(step + ring_slots - 1).start()

        fetch(step).wait()
        prod = _dot(a_ref[...], ring[step % ring_slots])
        if nk == 1:
            o_ref[...] = prod.astype(BF16)
            return
        acc_ref = scratch[0]
        kid = pl.program_id(2)

        @pl.when(kid == 0)
        def _():
            acc_ref[...] = prod

        @pl.when(kid > 0)
        def _():
            acc_ref[...] += prod

        @pl.when(kid == nk - 1)
        def _():
            o_ref[...] = acc_ref[...].astype(BF16)

    (out,), plan_outs = _call(
        body,
        name=name,
        operands=[a, b],
        grid=(m // tm, n // tn, nk),
        in_specs=[pl.BlockSpec((tm, tk), lambda i, j, k: (i + i0, k)), pl.BlockSpec(memory_space=pl.ANY)],
        out_specs=[pl.BlockSpec((tm, tn), lambda i, j, k: (i, j))],
        out_shape=[jax.ShapeDtypeStruct((m, n), BF16)],
        scratch_shapes=([pltpu.VMEM((tm, tn), F32)] if nk > 1 else [])
        + [pltpu.VMEM((ring_slots, tk, tn), BF16), pltpu.SemaphoreType.DMA((ring_slots,))],
        plans=plans,
    )
    return out, plan_outs


def _dw_attn(actt, cot, plans=()):
    s = actt.shape[1]
    tm = TM_DW
    counts = [MIX_DIM // tm, D_MODEL // tm, D_MODEL // tm]
    starts = [0, counts[0], counts[0] + counts[1]]
    which = lambda i: jnp.where(i < starts[1], 0, jnp.where(i < starts[2], 1, 2))

    def body(a_ref, b_ref, *outs):
        i = pl.program_id(0)
        prod = _dot(a_ref[...], b_ref[0]).astype(BF16)
        for m in range(3):
            @pl.when(which(i) == m)
            def _():
                outs[m][...] = prod

    out_specs = [pl.BlockSpec((tm, D_MODEL), lambda i, m=m: (jnp.clip(i - starts[m], 0, counts[m] - 1), 0)) for m in range(3)]
    out_shape = [jax.ShapeDtypeStruct((counts[m] * tm, D_MODEL), BF16) for m in range(3)]
    outs, plan_outs = _call(
        body,
        name="dw_attn",
        operands=[actt, cot],
        grid=(sum(counts),),
        in_specs=[pl.BlockSpec((tm, s), lambda i: (i, 0)), pl.BlockSpec((1, s, D_MODEL), lambda i: (which(i), 0, 0))],
        out_specs=out_specs,
        out_shape=out_shape,
        plans=plans,
    )
    return outs, plan_outs


def _dh1(dproj, w_in, x, dx1, g, after=()):
    s = x.shape[0]
    tm = min(TM_DH1, s)
    nt = s // tm

    def body(dp_ref, w_ref, x_ref, dx1_ref, g_ref, *rest):
        gx_ref, dg_ref, acc_g = rest[-3:]
        i = pl.program_id(0)

        @pl.when(i == 0)
        def _():
            acc_g[...] = jnp.zeros_like(acc_g)

        dh1 = _dot_nt(dp_ref[...], w_ref[...])
        xh, r = _rms(x_ref[...])
        acc_g[...] += _rows8(dh1 * xh)
        gx_ref[...] = dx1_ref[0].astype(F32) + _rms_bwd(dh1 * g_ref[...], xh, r)

        @pl.when(i == nt - 1)
        def _():
            dg_ref[...] = jnp.sum(acc_g[...], axis=0, keepdims=True)

    rows = pl.BlockSpec((tm, D_MODEL), lambda i: (i, 0))
    outs, _ = _call(
        body,
        name="dh1",
        operands=[dproj, w_in, x, dx1, g] + list(after),
        grid=(nt,),
        in_specs=[
            pl.BlockSpec((tm, IN_DIM), lambda i: (i, 0)),
            pl.BlockSpec((D_MODEL, IN_DIM), lambda i: (0, 0), pipeline_mode=pl.Buffered(1)),
            rows,
            pl.BlockSpec((1, tm, D_MODEL), lambda i: (0, i, 0)),
            pl.BlockSpec((1, D_MODEL), lambda i: (0, 0)),
        ] + [pl.BlockSpec(memory_space=pl.ANY)] * len(after),
        out_specs=[rows, pl.BlockSpec((1, D_MODEL), lambda i: (0, 0))],
        out_shape=[jax.ShapeDtypeStruct((s, D_MODEL), F32), jax.ShapeDtypeStruct((1, D_MODEL), F32)],
        scratch_shapes=[pltpu.VMEM((8, D_MODEL), F32)],
    )
    return outs


def _pair_sum(name, grads, gots, specs, core, plans=()):
    n = len(grads)
    shards = []
    for grad, (axis, size) in zip(grads, specs):
        shard = list(grad.shape)
        shard[axis] = size
        shards.append(tuple(shard))

    def body(core_ref, *refs):
        for g_ref, got_ref, o_ref in zip(refs[:n], refs[n : 2 * n], refs[2 * n :]):
            o_ref[0] = (g_ref[...].astype(F32) + got_ref[0].astype(F32)).astype(BF16)

    def own(shard, axis):
        if axis == 1:
            return pl.BlockSpec(shard, lambda q, core_ref: (0, 2 * q + core_ref[0]))
        return pl.BlockSpec(shard, lambda q, core_ref: (2 * q + core_ref[0], 0))

    slots = [pl.BlockSpec((1, *shard), lambda q, core_ref: (q, 0, 0)) for shard in shards]
    outs, plan_outs = _call(
        body,
        name=name,
        prefetch=[core],
        operands=list(grads) + list(gots),
        grid=(N_CHIP,),
        in_specs=[own(shard, axis) for shard, (axis, _) in zip(shards, specs)] + slots,
        out_specs=slots,
        out_shape=[jax.ShapeDtypeStruct((N_CHIP, *shard), BF16) for shard in shards],
        plans=plans,
    )
    return (outs, plan_outs) if plans else outs


def _adam_step(w, m, v, g):
    c1 = 1.0 / (1.0 - ADAM_B1**ADAM_STEP)
    c2 = 1.0 / (1.0 - ADAM_B2**ADAM_STEP)
    m_new = ADAM_B1 * m + (1.0 - ADAM_B1) * g
    v_new = ADAM_B2 * v + (1.0 - ADAM_B2) * (g * g)
    return -ADAM_LR * ((m_new * c1) / (jnp.sqrt(v_new * c2) + ADAM_EPS) + ADAM_WD * w), m_new, v_new


def _update_small(dev, vec_w, vec_m, vec_v, conv_w, conv_m, conv_v, small, landed):
    n_vec = len(vec_w)

    def body(dev_ref, *refs):
        refs = list(refs)
        take = lambda k: [refs.pop(0) for _ in range(k)]
        w_refs, m_refs, v_refs = take(n_vec), take(n_vec), take(n_vec)
        cw_ref, cm_ref, cv_ref, small_ref, land_ref, small_conv_ref, land_conv_ref = take(7)
        vec_outs, conv_outs, (loss_ref,) = take(4 * n_vec), take(4), take(1)

        def total(own_ref, others_ref):
            acc = None
            for d in range(N_DEV):
                part = jnp.where(dev_ref[0] == d, own_ref[0], others_ref[d])
                acc = part if acc is None else acc + part
            return acc

        g_all = total(small_ref, land_ref)
        for i in range(n_vec):
            g = g_all[i : i + 1]
            delta, m_new, v_new = _adam_step(w_refs[i][...], m_refs[i][...], v_refs[i][...], g)
            for ref, val in zip(vec_outs[4 * i : 4 * i + 4], (g, delta, m_new, v_new)):
                ref[...] = val
        g_conv = total(small_conv_ref, land_conv_ref)
        for tap in range(3):
            g = g_conv[tap : tap + 1]
            delta, m_new, v_new = _adam_step(cw_ref[tap], cm_ref[tap], cv_ref[tap], g)
            for ref, val in zip(conv_outs, (g, delta, m_new, v_new)):
                ref[tap] = val
        loss_ref[...] = jnp.broadcast_to(0.5 * jnp.sum(g_all[7:8], axis=-1, keepdims=True) / D_MODEL, loss_ref.shape)

    vec_spec = pl.BlockSpec((1, D_MODEL), lambda i, dev_ref: (0, 0))
    conv_spec = pl.BlockSpec((3, 1, CHUNK), lambda i, dev_ref: (0, 0, 0))
    vec_shape = jax.ShapeDtypeStruct((1, D_MODEL), F32)
    conv_shape = jax.ShapeDtypeStruct((3, 1, CHUNK), F32)
    outs, _ = _call(
        body,
        name="update_small",
        prefetch=[dev],
        operands=list(vec_w) + list(vec_m) + list(vec_v) + [conv_w, conv_m, conv_v, small, landed, small, landed],
        grid=(1,),
        in_specs=[vec_spec] * (3 * n_vec) + [conv_spec] * 3 + [
            pl.BlockSpec((1, 16, D_MODEL), lambda i, dev_ref: (0, 0, 0)),
            pl.BlockSpec((N_DEV, 16, D_MODEL), lambda i, dev_ref: (0, 0, 0)),
            pl.BlockSpec((1, 8, CHUNK), lambda i, dev_ref: (0, 1, dev_ref[0])),
            pl.BlockSpec((N_DEV, 8, CHUNK), lambda i, dev_ref: (0, 1, dev_ref[0])),
        ],
        out_specs=[vec_spec] * (4 * n_vec) + [conv_spec] * 4 + [pl.BlockSpec((1, CHUNK), lambda i, dev_ref: (0, 0))],
        out_shape=[vec_shape] * (4 * n_vec) + [conv_shape] * 4 + [jax.ShapeDtypeStruct((1, CHUNK), F32)],
    )
    return [outs[4 * i : 4 * i + 4] for i in range(n_vec)], outs[4 * n_vec : 4 * n_vec + 4], outs[-1]


def _adamw(name, groups, steps, own_slot=None, plans=()):
    lists = lambda p: list(p) if isinstance(p, (list, tuple)) else [p]
    groups = [(g[0], g[1], g[2], lists(g[3]), lists(g[4]) if len(g) > 4 else []) for g in groups]
    counts = [3 + len(g[3]) + len(g[4]) for g in groups]
    use_slot = own_slot is not None

    def first_tiles(w, parts):
        firsts, first = [], 0
        for p in parts:
            firsts.append(first)
            first += p.shape[1] // (w.shape[0] // steps)
        return firsts

    def update(t, slot_ref, firsts, w_ref, m_ref, v_ref, p_refs, o_refs, g_ref, d_ref, nm_ref, nv_ref):
        g = None
        for k, p_ref in enumerate(p_refs):
            gk = None
            for q in range(p_ref.shape[0]):
                part = p_ref[q].astype(F32)
                if o_refs:
                    part = jnp.where(slot_ref[0] == q, o_refs[k][0].astype(F32), part)
                gk = part if gk is None else gk + part
            g = gk if g is None else jnp.where(t >= firsts[k], gk, g)
        g_ref[...] = g
        d_ref[...], nm_ref[...], nv_ref[...] = _adam_step(w_ref[...], m_ref[...], v_ref[...], g)

    def body(*refs):
        refs = list(refs)
        slot_ref = refs.pop(0) if use_slot else None
        t = pl.program_id(0)
        ins, outs = refs[: sum(counts)], refs[sum(counts) :]
        for i, (w, _, _, parts, own) in enumerate(groups):
            mine = ins[sum(counts[:i]) : sum(counts[: i + 1])]
            update(t, slot_ref, first_tiles(w, parts), *mine[:3], mine[3 : 3 + len(parts)], mine[3 + len(parts) :],
                   *outs[4 * i : 4 * i + 4])

    operands, in_specs, out_specs, out_shape = [], [], [], []
    for w, m, v, parts, own in groups:
        r, cdim = w.shape
        tr = r // steps
        firsts = first_tiles(w, parts)

        def tile_of(k, t, firsts=firsts, parts=parts, tr=tr):
            return jnp.clip(t - firsts[k], 0, parts[k].shape[1] // tr - 1)

        blk = pl.BlockSpec((tr, cdim), lambda t, *_: (t, 0))
        operands += [w, m, v] + parts + own
        in_specs += [blk, blk, blk]
        in_specs += [pl.BlockSpec((p.shape[0], tr, cdim), lambda t, *_, k=k, tile_of=tile_of: (0, tile_of(k, t), 0)) for k, p in enumerate(parts)]
        in_specs += [pl.BlockSpec((1, tr, cdim), lambda t, slot_ref, k=k, tile_of=tile_of: (slot_ref[0], tile_of(k, t), 0)) for k in range(len(own))]
        out_specs += [blk] * 4
        out_shape += [jax.ShapeDtypeStruct((r, cdim), F32)] * 4
    outs, plan_outs = _call(
        body,
        name=name,
        prefetch=[own_slot] if use_slot else [],
        operands=operands,
        grid=(steps,),
        in_specs=in_specs,
        out_specs=out_specs,
        out_shape=out_shape,
        plans=plans,
    )
    return [outs[4 * i : 4 * i + 4] for i in range(len(groups))], plan_outs


def _chip_routes(n):
    x, y, c = _place()
    my_chip = 2 * x + y
    return [(a, (*chip, c), 2 * chip[0] + chip[1], my_chip, 2 * chip[0] + chip[1])
            for a in range(n) for chip in _other_chips(x, y)]


def _gather_routes(n):
    x, y, c = _place()
    flip = lambda v, f: 1 - v if f else v
    peers = [(flip(x, fx), flip(y, fy), flip(c, fc)) for fx in (0, 1) for fy in (0, 1) for fc in (0, 1) if fx + fy + fc]
    return [(0, p, 0, 4 * x + 2 * y + c, 4 * p[0] + 2 * p[1] + p[2]) for p in peers]


_chip_routes.copies_per_array = N_CHIP - 1
_gather_routes.copies_per_array = N_DEV - 1


def _split_start(name, routes, srcs, land_shapes):
    n = len(srcs)
    hbm = pl.BlockSpec(memory_space=pltpu.HBM)
    sem = pl.BlockSpec(memory_space=pltpu.SEMAPHORE)

    def body(*refs):
        src_refs, land_refs = refs[:n], refs[n : 2 * n]
        send_sems, recv_sems = refs[2 * n], refs[2 * n + 1]
        token = refs[-1]
        for k, (a, peer, src_slot, there, _) in enumerate(routes(n)):
            pltpu.make_async_remote_copy(
                src_ref=src_refs[a].at[src_slot], dst_ref=land_refs[a].at[there],
                send_sem=send_sems.at[k], recv_sem=recv_sems.at[k], device_id=peer, device_id_type=MESH).start()
        token[...] = jnp.zeros_like(token)

    in_hbm = [pltpu.with_memory_space_constraint(p, pltpu.HBM) for p in srcs]
    in_hbm += [pltpu.with_memory_space_constraint(lax.empty(shape, p.dtype), pltpu.HBM) for p, shape in zip(srcs, land_shapes)]
    n_copies = routes.copies_per_array * n
    res = pl.pallas_call(
        body,
        name=name,
        out_shape=[pltpu.SemaphoreType.DMA((n_copies,)), pltpu.SemaphoreType.DMA((n_copies,))]
        + [pltpu.HBM(p.shape, p.dtype) for p in srcs] + [pltpu.HBM(shape, p.dtype) for p, shape in zip(srcs, land_shapes)]
        + [jax.ShapeDtypeStruct((8, CHUNK), F32)],
        in_specs=[hbm] * (2 * n),
        out_specs=[sem, sem] + [hbm] * (2 * n) + [pl.BlockSpec(memory_space=pltpu.VMEM)],
        input_output_aliases={i: 2 + i for i in range(2 * n)},
        compiler_params=pltpu.CompilerParams(has_side_effects=pltpu.SideEffectType.DATAFLOW_SIDE_EFFECTING),
    )(*in_hbm)
    return res[:-1], res[-1]


def _split_wait(name, routes, in_flight, after):
    n = (len(in_flight) - 2) // 2
    hbm = pl.BlockSpec(memory_space=pltpu.HBM)
    sem = pl.BlockSpec(memory_space=pltpu.SEMAPHORE)
    send_sems, recv_sems, *bufs = in_flight

    def body(*refs):
        src_refs, land_refs = refs[:n], refs[n : 2 * n]
        send_ref, recv_ref = refs[2 * n], refs[2 * n + 1]
        for k, (a, peer, src_slot, _, here) in enumerate(routes(n)):
            cp = pltpu.make_async_remote_copy(
                src_ref=src_refs[a].at[src_slot], dst_ref=land_refs[a].at[here],
                send_sem=send_ref.at[k], recv_sem=recv_ref.at[k], device_id=peer, device_id_type=MESH)
            cp.wait_send()
            cp.wait_recv()

    res = pl.pallas_call(
        body,
        name=name,
        out_shape=[pltpu.HBM(b.shape, b.dtype) for b in bufs],
        in_specs=[hbm] * (2 * n) + [sem, sem] + [pl.BlockSpec(memory_space=pl.ANY)] * len(after),
        out_specs=[hbm] * (2 * n),
        input_output_aliases={i: i for i in range(2 * n)},
        compiler_params=pltpu.CompilerParams(has_side_effects=pltpu.SideEffectType.DATAFLOW_SIDE_EFFECTING),
    )(*bufs, send_sems, recv_sems, *after)
    return res[:n], res[n:]


def kernel(x, mem, norm_mix_g, w_in, conv_w, gm_ln_g, gm_ln_b, gm_ws, gm_bs, w_out, norm_x_g, norm_mem_g, w_q, w_kv, w_xo, norm_final_g, loss_target, m_norm_mix_g, m_w_in, m_conv_w, m_gm_ln_g, m_gm_ln_b, m_gm_ws, m_gm_bs, m_w_out, m_norm_x_g, m_norm_mem_g, m_w_q, m_w_kv, m_w_xo, m_norm_final_g, v_norm_mix_g, v_w_in, v_conv_w, v_gm_ln_g, v_gm_ln_b, v_gm_ws, v_gm_bs, v_w_out, v_norm_x_g, v_norm_mem_g, v_w_q, v_w_kv, v_w_xo, v_norm_final_g):
    dev = 4 * lax.axis_index("x") + 2 * lax.axis_index("y") + lax.axis_index("c")
    core = lax.axis_index("c").astype(jnp.int32).reshape(1)
    x2 = x[0]
    target = loss_target[0]
    pad_taps = lambda t: jnp.pad(t[0], ((0, 5), (0, 0)))

    big_names = ["w_in", "w_out", "w_q", "w_kv", "w_xo"]
    big_w = dict(w_in=w_in[0], w_out=w_out[0], w_q=w_q[0], w_kv=w_kv[0], w_xo=w_xo[0])
    big_m = dict(w_in=m_w_in[0], w_out=m_w_out[0], w_q=m_w_q[0], w_kv=m_w_kv[0], w_xo=m_w_xo[0])
    big_v = dict(w_in=v_w_in[0], w_out=v_w_out[0], w_q=v_w_q[0], w_kv=v_w_kv[0], w_xo=v_w_xo[0])
    shard_axis = dict(w_in=1, w_out=0, w_q=0, w_kv=1, w_xo=0)
    full_shape = dict(w_in=(D_MODEL, IN_DIM), w_out=(MIX_DIM, D_MODEL), w_q=(D_MODEL, D_MODEL),
                      w_kv=(D_MODEL, 2 * D_MODEL), w_xo=(D_MODEL, D_MODEL))
    shard_size = {k: big_w[k].shape[shard_axis[k]] for k in big_names}
    others = big_names[1:]
    gather_spec = lambda k: (full_shape[k], shard_axis[k], shard_size[k])
    reduce_spec = lambda k: (shard_axis[k], shard_size[k])
    ws = gm_ws[0]
    bs_b = jnp.broadcast_to(gm_bs[0][:, :, None], (8, CHUNK, CHUNK))
    g_f = norm_final_g.reshape(1, D_MODEL)

    px, py = lax.axis_index("x"), lax.axis_index("y")
    chip_order = jnp.stack([2 * px + py, 2 * (1 - px) + py, 2 * px + 1 - py, 2 * (1 - px) + 1 - py]).astype(jnp.int32)
    (proj, h1t, w_in_full), ((w_out_full, conv_g),) = _inproj(
        x2, norm_mix_g, big_w["w_in"].astype(BF16), chip_order,
        plans=[_gather_plan([big_w["w_out"].astype(BF16), pad_taps(conv_w)[None]], [gather_spec("w_out"), ((N_DEV, 8, CHUNK), 0, 1)])])
    conv_full = conv_g.transpose(1, 0, 2).reshape(8, D_MODEL)
    attn_w = ["w_q", "w_kv", "w_xo"]
    cat, (gathered,) = _mixer_fwd(
        proj, conv_full, gm_ln_g, gm_ln_b, ws, bs_b,
        plans=[_gather_plan([big_w[k].astype(BF16) for k in attn_w], [gather_spec(k) for k in attn_w])])
    wf = dict(zip(attn_w, gathered), w_out=w_out_full)
    k, vv, mt = _kv_fwd(mem[0], norm_mem_g, wf["w_kv"])
    (dcat, actt, cot, dk, dvv, d_gf, d_gx, sq_err) = _attn(
        x2, cat, target, wf["w_out"], wf["w_q"], wf["w_xo"], k, vv, norm_x_g, g_f)

    sibling_plan = lambda names, grads: _sibling_plan([grads[k] for k in names], [reduce_spec(k) for k in names])
    early = ["w_kv", "w_out", "w_q", "w_xo"]
    big_g = {}
    big_g["w_kv"], d_gmem = _kv_bwd(mem[0], wf["w_kv"], mt, dk, dvv)
    dws, (got_kv,) = _dw_attn(actt, cot, plans=[sibling_plan(early[:1], big_g)])
    big_g.update(zip(early[1:], dws))
    pair_kv, (got_rest,) = _pair_sum("pair_sum_w_kv", [big_g["w_kv"]], got_kv, [reduce_spec("w_kv")], core,
                                     plans=[sibling_plan(early[1:], big_g)])
    pair_rest = _pair_sum("pair_sum_attn", [big_g[k] for k in early[1:]], got_rest, [reduce_spec(k) for k in early[1:]], core)
    (dproj, d_cw, d_lng, d_lnb, d_ws, d_bs), (parts_early,) = _mixer_bwd(
        proj, dcat, conv_full, gm_ln_g, gm_ln_b, ws, bs_b, plans=[_chip_plan(list(pair_kv) + list(pair_rest))])
    half = D_MODEL // 2
    in_spec = reduce_spec("w_in")
    dw_top, _ = _matmul_bf16("dw_in_top", h1t, dproj, rows=(0, half))
    dw_bot, (got_top, (ws_all,)) = _matmul_bf16(
        "dw_in_bot", h1t, dproj, rows=(half, half),
        plans=[_sibling_plan([dw_top], [in_spec]),
               _gather_plan([d_ws.reshape(1, 8 * CHUNK, CHUNK).astype(BF16)], [((N_DEV, 8 * CHUNK, CHUNK), 0, 1)])])
    (pair_top,), (got_bot,) = _pair_sum("pair_sum_w_in_top", [dw_top], got_top, [in_spec], core,
                                        plans=[_sibling_plan([dw_bot], [in_spec])])
    (pair_bot,) = _pair_sum("pair_sum_w_in_bot", [dw_bot], got_bot, [in_spec], core)
    chip_shapes = [p.shape for p in (pair_top, pair_bot)]
    chips_in_flight, token = _split_start("grad_chip_exchange_start", _chip_routes, [pair_top, pair_bot], chip_shapes)
    grad_x, d_gmix = _dh1(dproj, w_in_full, x2, cot, norm_mix_g, after=[token])
    vec_names = ["norm_mix_g", "gm_ln_g", "gm_ln_b", "gm_bs", "norm_x_g", "norm_mem_g", "norm_final_g"]
    vec_g = [d_gmix, d_lng, d_lnb, d_bs, d_gx, d_gmem, d_gf]
    small = jnp.concatenate(vec_g + [sq_err, d_cw], axis=0)[None]
    small_in_flight, small_token = _split_start("small_all_gather_start", _gather_routes, [small], [(N_DEV, 16, D_MODEL)])
    flat_ws = lambda t: t.reshape(8 * CHUNK, CHUNK)
    updated, _ = _adamw(
        "adamw_early",
        [(big_w[k], big_m[k], big_v[k], got_k) for k, got_k in zip(early, parts_early)]
        + [(flat_ws(ws), flat_ws(m_gm_ws), flat_ws(v_gm_ws), ws_all)], steps=STEPS_ADAMW // 2)
    big_out = dict(zip(early, updated[:4]))
    ws_out = updated[4]
    (pair_top, pair_bot), (land_top, land_bot) = _split_wait(
        "grad_chip_exchange_wait", _chip_routes, chips_in_flight, [small_token] + [u[0] for u in updated])
    my_chip = (2 * px + py).astype(jnp.int32).reshape(1)
    (big_out["w_in"],), _ = _adamw(
        "adamw_w_in", [(big_w["w_in"], big_m["w_in"], big_v["w_in"], [land_top, land_bot], [pair_top, pair_bot])],
        steps=STEPS_ADAMW, own_slot=my_chip)
    (small,), (small_all,) = _split_wait("small_all_gather_wait", _gather_routes, small_in_flight, [big_out["w_in"][0]])
    row = lambda t: t.reshape(1, D_MODEL)
    vec_out, conv_out, loss_row = _update_small(
        dev.astype(jnp.int32).reshape(1),
        [norm_mix_g, gm_ln_g, gm_ln_b, row(gm_bs), norm_x_g, norm_mem_g, row(norm_final_g)],
        [m_norm_mix_g, m_gm_ln_g, m_gm_ln_b, row(m_gm_bs), m_norm_x_g, m_norm_mem_g, row(m_norm_final_g)],
        [v_norm_mix_g, v_gm_ln_g, v_gm_ln_b, row(v_gm_bs), v_norm_x_g, v_norm_mem_g, row(v_norm_final_g)],
        *(t.transpose(1, 0, 2) for t in (conv_w, m_conv_w, v_conv_w)), small, small_all)
    loss = loss_row[0, 0]

    def result(name, which):
        if name in big_out:
            return big_out[name][which][None]
        if name == "conv_w":
            return conv_out[which].transpose(1, 0, 2)
        if name == "gm_ws":
            return ws_out[which].reshape(1, 8, CHUNK, CHUNK)
        r = vec_out[vec_names.index(name)][which]
        if name == "gm_bs":
            return r.reshape(1, 8, CHUNK)
        if name == "norm_final_g":
            return r.reshape(D_MODEL)
        return r

    weights = ["norm_mix_g", "w_in", "conv_w", "gm_ln_g", "gm_ln_b", "gm_ws", "gm_bs", "w_out", "norm_x_g", "norm_mem_g", "w_q", "w_kv", "w_xo", "norm_final_g"]
    outs = [loss, grad_x[None]]
    for which in range(4):
        outs += [result(name, which) for name in weights]
    return tuple(outs)
```
